```python
import math
import jax, jax.numpy as jnp
from jax import lax
import numpy as np

D_MODEL = 1024
BATCH = 8
SEQ = 8192
DEPTH = 1

HEAD_DIM = 64
N_ATTN_HEADS = 8
N_DELTA_HEADS = 8
ATTN_WIDTH = N_ATTN_HEADS * HEAD_DIM
DELTA_WIDTH = N_DELTA_HEADS * HEAD_DIM
MIX_WIDTH = ATTN_WIDTH + DELTA_WIDTH
DILATED_BRANCHES = ((128, 1), (512, 4), (2048, 16))
PAD_UNIT = 2048
N_BUCKETS = 32
MAX_DISTANCE = 2048
CONV_WIDTH = 4
CHUNK = 64
D_FF = (8 * D_MODEL + 3 * 256 - 1) // (3 * 256) * 256
IN_WIDTH = 3 * ATTN_WIDTH + 4 * DELTA_WIDTH + 2 * N_DELTA_HEADS
EPS = 1e-6
NEG_INF = -1e30

kernel_name = 'hybrid_dilated_attn_gated_deltanet_block'


def _rmsnorm(x, g):
    x32 = x.astype(jnp.float32)
    y = x32 * lax.rsqrt(jnp.mean(x32 * x32, axis=-1, keepdims=True) + EPS)
    return (y * g.astype(jnp.float32)).astype(x.dtype)


def _l2norm(x):
    return x * lax.rsqrt(jnp.sum(x * x, axis=-1, keepdims=True) + EPS)


def _t5_bucket(distance):
    max_exact = N_BUCKETS // 2
    dist_f = jnp.maximum(distance, 1).astype(jnp.float32)
    large = max_exact + (jnp.log(dist_f / max_exact) / math.log(MAX_DISTANCE / max_exact)
                         * (N_BUCKETS - max_exact)).astype(jnp.int32)
    return jnp.where(distance < max_exact, distance, jnp.minimum(large, N_BUCKETS - 1))


def _dilated_branch(q, k, v, rel_bias, window, dilation):
    b, h, p, dh = q.shape
    band = window // dilation
    length = p // dilation
    n_blocks = length // band

    def to_blocks(t):
        t = t.reshape(b, h, length, dilation, dh).transpose(0, 1, 3, 2, 4)
        return t.reshape(b, h, dilation, n_blocks, band, dh)

    qb, kb, vb = to_blocks(q), to_blocks(k), to_blocks(v)

    def with_prev(t):
        prev = jnp.pad(t, ((0, 0), (0, 0), (0, 0), (1, 0), (0, 0), (0, 0)))[:, :, :, :-1]
        return jnp.concatenate([prev, t], axis=-2)

    kw, vw = with_prev(kb), with_prev(vb)
    qi = jnp.arange(band)[:, None]
    kj = jnp.arange(2 * band)[None, :]
    steps = qi + band - kj
    in_window = (steps >= 0) & (steps <= band)
    not_before_start = (jnp.arange(n_blocks)[:, None, None] > 0) | (kj >= band)[None]
    valid = in_window[None] & not_before_start
    bias = rel_bias.astype(jnp.float32)[_t5_bucket(jnp.maximum(steps, 0) * dilation)]
    bias = bias.transpose(2, 0, 1)
    s = jnp.einsum('bhrnqd,bhrnkd->bhrnqk', qb * dh ** -0.5, kw) + bias[None, :, None, None]
    s = jnp.where(valid, s, NEG_INF)
    m = jnp.max(s, axis=-1, keepdims=True)
    e = jnp.exp(s - m)
    denom = jnp.sum(e, axis=-1, keepdims=True)
    o = jnp.einsum('bhrnqk,bhrnkd->bhrnqd', e, vw) / denom
    lse = (m + jnp.log(denom))[..., 0]
    o = o.reshape(b, h, dilation, length, dh).transpose(0, 1, 3, 2, 4).reshape(b, h, p, dh)
    lse = lse.reshape(b, h, dilation, length).transpose(0, 1, 3, 2).reshape(b, h, p)
    return o, lse


def _dilated_attention(q, k, v, rel_bias):
    b, s, h, dh = q.shape
    p = (s + PAD_UNIT - 1) // PAD_UNIT * PAD_UNIT

    def to_bhpd(t):
        t = jnp.pad(t.astype(jnp.float32), ((0, 0), (0, p - s), (0, 0), (0, 0)))
        return t.transpose(0, 2, 1, 3)

    q, k, v = to_bhpd(q), to_bhpd(k), to_bhpd(v)
    outs, lses = [], []
    for window, dilation in DILATED_BRANCHES:
        o_i, lse_i = _dilated_branch(q, k, v, rel_bias, window, dilation)
        outs.append(o_i)
        lses.append(lse_i)
    w = jax.nn.softmax(jnp.stack(lses), axis=0)
    o = jnp.sum(w[..., None] * jnp.stack(outs), axis=0)
    return o[:, :, :s].transpose(0, 2, 1, 3).reshape(b, s, h * dh)


def _causal_conv(x, w):
    return lax.conv_general_dilated(x, w[:, None, :], window_strides=(1,),
                                    padding=((CONV_WIDTH - 1, 0),),
                                    dimension_numbers=('NWC', 'WIO', 'NWC'),
                                    feature_group_count=x.shape[-1])


def _chunk_gated_delta_rule(q, k, v, g, beta):
    b, s, h, dk = q.shape
    dv = v.shape[-1]
    nc = s // CHUNK

    def chunks(t):
        return t.reshape(b, nc, CHUNK, h, t.shape[-1]).transpose(1, 0, 3, 2, 4)

    qc, kc, vc = chunks(q), chunks(k), chunks(v)
    gcum = jnp.cumsum(g.reshape(b, nc, CHUNK, h).transpose(1, 0, 3, 2), axis=-1)
    bc = beta.reshape(b, nc, CHUNK, h).transpose(1, 0, 3, 2)
    causal = jnp.tril(jnp.ones((CHUNK, CHUNK), dtype=bool))
    strict = jnp.tril(jnp.ones((CHUNK, CHUNK), dtype=bool), k=-1)
    diff = gcum[..., :, None] - gcum[..., None, :]
    decay = jnp.where(causal, jnp.exp(jnp.where(causal, diff, 0.0)), 0.0)
    k_beta = kc * bc[..., None]
    a_mat = jnp.where(strict, jnp.einsum('nbhcd,nbhed->nbhce', k_beta, kc) * decay, 0.0)
    rhs = jnp.concatenate([vc * bc[..., None], k_beta * jnp.exp(gcum)[..., None]], axis=-1)
    sol = lax.linalg.triangular_solve(a_mat + jnp.eye(CHUNK, dtype=a_mat.dtype), rhs,
                                      left_side=True, lower=True, unit_diagonal=True)
    u, w = sol[..., :dv], sol[..., dv:]
    qk = jnp.where(causal, jnp.einsum('nbhcd,nbhed->nbhce', qc, kc) * decay, 0.0)

    def step(state, xs):
        q_i, k_i, u_i, w_i, g_i, qk_i = xs
        v_new = u_i - jnp.einsum('bhck,bhkv->bhcv', w_i, state)
        o_i = (jnp.einsum('bhck,bhkv->bhcv', q_i * jnp.exp(g_i)[..., None], state)
               + jnp.einsum('bhce,bhev->bhcv', qk_i, v_new))
        g_last = g_i[..., -1]
        k_dec = k_i * jnp.exp(g_last[..., None] - g_i)[..., None]
        state = state * jnp.exp(g_last)[..., None, None] + jnp.einsum('bhck,bhcv->bhkv', k_dec, v_new)
        return state, o_i

    state0 = jnp.zeros((b, h, dk, dv), jnp.float32)
    _, o = lax.scan(step, state0, (qc, kc, u, w, gcum, qk))
    return o.transpose(1, 0, 3, 2, 4).reshape(b, s, h, dv)


def _gated_deltanet(q, k, v, z, b_logit, a_logit, conv_w, a_log, dt_bias, norm_g):
    bsz, s, _ = q.shape
    qkv = jax.nn.silu(_causal_conv(jnp.concatenate([q, k, v], axis=-1), conv_w))
    q, k, v = jnp.split(qkv.astype(jnp.float32), 3, axis=-1)
    shp = (bsz, s, N_DELTA_HEADS, HEAD_DIM)
    q = _l2norm(q.reshape(shp)) * HEAD_DIM ** -0.5
    k = _l2norm(k.reshape(shp))
    v = v.reshape(shp)
    beta = jax.nn.sigmoid(b_logit.astype(jnp.float32))
    g = -jnp.exp(a_log.astype(jnp.float32)) * jax.nn.softplus(
        a_logit.astype(jnp.float32) + dt_bias.astype(jnp.float32))
    o = _chunk_gated_delta_rule(q, k, v, g, beta)
    o = _rmsnorm(o, norm_g) * jax.nn.silu(z.astype(jnp.float32).reshape(shp))
    return o.reshape(bsz, s, DELTA_WIDTH).astype(z.dtype)


def _fwd_setup_inputs(seed: int = 0) -> dict:
    key = jax.random.key(seed)
    ks = jax.random.split(key, 20)
    f32 = jnp.float32

    def nrm(k, shape, scale):
        return jax.random.normal(k, shape, f32) * scale

    dt = jnp.exp(jax.random.uniform(ks[9], (DEPTH, N_DELTA_HEADS), f32,
                                    minval=math.log(1e-3), maxval=math.log(1e-1)))
    return {
        'x': nrm(ks[0], (BATCH, SEQ, D_MODEL), 1.0),
        'c': nrm(ks[1], (BATCH, D_MODEL), 1.0),
        'w_ada': nrm(ks[2], (DEPTH, D_MODEL, 6 * D_MODEL), 0.5 * D_MODEL ** -0.5),
        'b_ada': nrm(ks[3], (DEPTH, 6 * D_MODEL), 0.02),
        'norm_attn_g': 1.0 + nrm(ks[4], (DEPTH, D_MODEL), 0.05),
        'w_in': nrm(ks[5], (DEPTH, D_MODEL, IN_WIDTH), D_MODEL ** -0.5),
        'rel_bias': nrm(ks[6], (N_BUCKETS, N_ATTN_HEADS), 0.5),
        'conv_w': nrm(ks[7], (DEPTH, CONV_WIDTH, 3 * DELTA_WIDTH), CONV_WIDTH ** -0.5),
        'a_log': jnp.log(jax.random.uniform(ks[8], (DEPTH, N_DELTA_HEADS), f32, minval=1.0, maxval=16.0)),
        'dt_bias': dt + jnp.log(-jnp.expm1(-dt)),
        'delta_norm_g': 1.0 + nrm(ks[10], (DEPTH, HEAD_DIM), 0.05),
        'w_out': nrm(ks[11], (DEPTH, MIX_WIDTH, D_MODEL), MIX_WIDTH ** -0.5),
        'norm_ffn_g': 1.0 + nrm(ks[12], (DEPTH, D_MODEL), 0.05),
        'w_gate': nrm(ks[13], (DEPTH, D_MODEL, D_FF), D_MODEL ** -0.5),
        'w_up': nrm(ks[14], (DEPTH, D_MODEL, D_FF), D_MODEL ** -0.5),
        'w_down': nrm(ks[15], (DEPTH, D_FF, D_MODEL), D_FF ** -0.5),
        'final_norm_g': 1.0 + nrm(ks[16], (D_MODEL,), 0.05),
    }


def _fwd_reference(x, c, w_ada, b_ada, norm_attn_g, w_in, rel_bias, conv_w, a_log, dt_bias,
              delta_norm_g, w_out, norm_ffn_g, w_gate, w_up, w_down, final_norm_g):
    bsz, s, _ = x.shape
    split_points = np.cumsum([ATTN_WIDTH] * 3 + [DELTA_WIDTH] * 4 + [N_DELTA_HEADS])
    c_act = jax.nn.silu(c)
    for l in range(DEPTH):
        mod = c_act @ w_ada[l] + b_ada[l]
        sh1, sc1, g1, sh2, sc2, g2 = [m[:, None, :] for m in jnp.split(mod, 6, axis=-1)]
        h = _rmsnorm(x, norm_attn_g[l]) * (1.0 + sc1) + sh1
        proj = h @ w_in[l]
        q_a, k_a, v_a, q_d, k_d, v_d, z_d, b_d, a_d = jnp.split(proj, split_points, axis=-1)
        hs = (bsz, s, N_ATTN_HEADS, HEAD_DIM)
        y_attn = _dilated_attention(q_a.reshape(hs), k_a.reshape(hs), v_a.reshape(hs), rel_bias).astype(x.dtype)
        y_delta = _gated_deltanet(q_d, k_d, v_d, z_d, b_d, a_d, conv_w[l], a_log[l], dt_bias[l], delta_norm_g[l])
        y = jnp.concatenate([y_attn, y_delta], axis=-1) @ w_out[l]
        x = x + g1 * y
        h = _rmsnorm(x, norm_ffn_g[l]) * (1.0 + sc2) + sh2
        y = (jax.nn.silu(h @ w_gate[l]) * (h @ w_up[l])) @ w_down[l]
        x = x + g2 * y
    return _rmsnorm(x, final_norm_g)


import jax as _jax
import jax.numpy as _jnp

TWIN_FORMAT = 'train_step'
FWD_PARAMS = ['x', 'c', 'w_ada', 'b_ada', 'norm_attn_g', 'w_in', 'rel_bias', 'conv_w', 'a_log', 'dt_bias', 'delta_norm_g', 'w_out', 'norm_ffn_g', 'w_gate', 'w_up', 'w_down', 'final_norm_g']
TWIN_WEIGHTS = ['w_ada', 'b_ada', 'norm_attn_g', 'w_in', 'rel_bias', 'conv_w', 'a_log', 'dt_bias', 'delta_norm_g', 'w_out', 'norm_ffn_g', 'w_gate', 'w_up', 'w_down', 'final_norm_g']
TWIN_DIFF_INPUT = 'x'
TWIN_INPUTS = ['x', 'c', 'w_ada', 'b_ada', 'norm_attn_g', 'w_in', 'rel_bias', 'conv_w', 'a_log', 'dt_bias', 'delta_norm_g', 'w_out', 'norm_ffn_g', 'w_gate', 'w_up', 'w_down', 'final_norm_g', 'loss_target', 'm_w_ada', 'm_b_ada', 'm_norm_attn_g', 'm_w_in', 'm_rel_bias', 'm_conv_w', 'm_a_log', 'm_dt_bias', 'm_delta_norm_g', 'm_w_out', 'm_norm_ffn_g', 'm_w_gate', 'm_w_up', 'm_w_down', 'm_final_norm_g', 'v_w_ada', 'v_b_ada', 'v_norm_attn_g', 'v_w_in', 'v_rel_bias', 'v_conv_w', 'v_a_log', 'v_dt_bias', 'v_delta_norm_g', 'v_w_out', 'v_norm_ffn_g', 'v_w_gate', 'v_w_up', 'v_w_down', 'v_final_norm_g']
TWIN_OUTPUTS = ['loss', 'grad_x', 'grad_w_ada', 'grad_b_ada', 'grad_norm_attn_g', 'grad_w_in', 'grad_rel_bias', 'grad_conv_w', 'grad_a_log', 'grad_dt_bias', 'grad_delta_norm_g', 'grad_w_out', 'grad_norm_ffn_g', 'grad_w_gate', 'grad_w_up', 'grad_w_down', 'grad_final_norm_g', 'delta_w_ada', 'delta_b_ada', 'delta_norm_attn_g', 'delta_w_in', 'delta_rel_bias', 'delta_conv_w', 'delta_a_log', 'delta_dt_bias', 'delta_delta_norm_g', 'delta_w_out', 'delta_norm_ffn_g', 'delta_w_gate', 'delta_w_up', 'delta_w_down', 'delta_final_norm_g', 'new_m_w_ada', 'new_m_b_ada', 'new_m_norm_attn_g', 'new_m_w_in', 'new_m_rel_bias', 'new_m_conv_w', 'new_m_a_log', 'new_m_dt_bias', 'new_m_delta_norm_g', 'new_m_w_out', 'new_m_norm_ffn_g', 'new_m_w_gate', 'new_m_w_up', 'new_m_w_down', 'new_m_final_norm_g', 'new_v_w_ada', 'new_v_b_ada', 'new_v_norm_attn_g', 'new_v_w_in', 'new_v_rel_bias', 'new_v_conv_w', 'new_v_a_log', 'new_v_dt_bias', 'new_v_delta_norm_g', 'new_v_w_out', 'new_v_norm_ffn_g', 'new_v_w_gate', 'new_v_w_up', 'new_v_w_down', 'new_v_final_norm_g']
TWIN_LEAF_KINDS = {'loss': 'loss', 'grad_x': 'grad_x', 'grad_w_ada': 'grad_w', 'grad_b_ada': 'grad_w', 'grad_norm_attn_g': 'grad_w', 'grad_w_in': 'grad_w', 'grad_rel_bias': 'grad_w', 'grad_conv_w': 'grad_w', 'grad_a_log': 'grad_w', 'grad_dt_bias': 'grad_w', 'grad_delta_norm_g': 'grad_w', 'grad_w_out': 'grad_w', 'grad_norm_ffn_g': 'grad_w', 'grad_w_gate': 'grad_w', 'grad_w_up': 'grad_w', 'grad_w_down': 'grad_w', 'grad_final_norm_g': 'grad_w', 'delta_w_ada': 'delta_w', 'delta_b_ada': 'delta_w', 'delta_norm_attn_g': 'delta_w', 'delta_w_in': 'delta_w', 'delta_rel_bias': 'delta_w', 'delta_conv_w': 'delta_w', 'delta_a_log': 'delta_w', 'delta_dt_bias': 'delta_w', 'delta_delta_norm_g': 'delta_w', 'delta_w_out': 'delta_w', 'delta_norm_ffn_g': 'delta_w', 'delta_w_gate': 'delta_w', 'delta_w_up': 'delta_w', 'delta_w_down': 'delta_w', 'delta_final_norm_g': 'delta_w', 'new_m_w_ada': 'new_m', 'new_m_b_ada': 'new_m', 'new_m_norm_attn_g': 'new_m', 'new_m_w_in': 'new_m', 'new_m_rel_bias': 'new_m', 'new_m_conv_w': 'new_m', 'new_m_a_log': 'new_m', 'new_m_dt_bias': 'new_m', 'new_m_delta_norm_g': 'new_m', 'new_m_w_out': 'new_m', 'new_m_norm_ffn_g': 'new_m', 'new_m_w_gate': 'new_m', 'new_m_w_up': 'new_m', 'new_m_w_down': 'new_m', 'new_m_final_norm_g': 'new_m', 'new_v_w_ada': 'new_v', 'new_v_b_ada': 'new_v', 'new_v_norm_attn_g': 'new_v', 'new_v_w_in': 'new_v', 'new_v_rel_bias': 'new_v', 'new_v_conv_w': 'new_v', 'new_v_a_log': 'new_v', 'new_v_dt_bias': 'new_v', 'new_v_delta_norm_g': 'new_v', 'new_v_w_out': 'new_v', 'new_v_norm_ffn_g': 'new_v', 'new_v_w_gate': 'new_v', 'new_v_w_up': 'new_v', 'new_v_w_down': 'new_v', 'new_v_final_norm_g': 'new_v'}


def _forward(args):
    return _fwd_reference(*[args[k] for k in FWD_PARAMS])


def _output_shape():
    out = _jax.eval_shape(lambda: _forward(_fwd_setup_inputs(0)))
    return out.shape, out.dtype

N_MICROBATCH = 1
ADAM_LR = 0.001
ADAM_B1 = 0.9
ADAM_B2 = 0.999
ADAM_EPS = 1e-08
ADAM_WD = 0.01
ADAM_STEP = 10
PER_EXAMPLE_BATCH_AXIS = {'x': 0, 'c': 0, 'loss_target': 0}
SHARED_INPUTS = []
_WEIGHT_DTYPES = {'w_ada': _jnp.float32, 'b_ada': _jnp.float32, 'norm_attn_g': _jnp.float32, 'w_in': _jnp.float32, 'rel_bias': _jnp.float32, 'conv_w': _jnp.float32, 'a_log': _jnp.float32, 'dt_bias': _jnp.float32, 'delta_norm_g': _jnp.float32, 'w_out': _jnp.float32, 'norm_ffn_g': _jnp.float32, 'w_gate': _jnp.float32, 'w_up': _jnp.float32, 'w_down': _jnp.float32, 'final_norm_g': _jnp.float32}
MOMENT_SCALE = {'w_ada': 8.706285e-02, 'b_ada': 1.539242e-01, 'norm_attn_g': 7.766429e-02, 'w_in': 4.065255e-02, 'rel_bias': 2.354831e-02, 'conv_w': 4.814686e-02, 'a_log': 9.361540e-01, 'dt_bias': 8.533934e-01, 'delta_norm_g': 1.966711e-01, 'w_out': 4.328599e-02, 'norm_ffn_g': 7.578762e-02, 'w_gate': 3.404707e-02, 'w_up': 3.302567e-02, 'w_down': 5.457125e-02, 'final_norm_g': 6.413584e+01}


def _to_microbatches(a, axis):
    t = _jnp.moveaxis(a, axis, 0)
    t = t.reshape((N_MICROBATCH, t.shape[0] // N_MICROBATCH) + t.shape[1:])
    return _jnp.moveaxis(t, 1, axis + 1)


def setup_inputs(seed: int = 0) -> dict:
    inp = _fwd_setup_inputs(seed)
    key = _jax.random.fold_in(_jax.random.key(seed), 7919)
    shape, _ = _output_shape()
    out = dict(inp)
    out["loss_target"] = _jax.random.normal(_jax.random.fold_in(key, 0), shape, _jnp.float32)
    for i, name in enumerate(TWIN_WEIGHTS):
        w = inp[name].astype(_jnp.float32)
        if MOMENT_SCALE is None:
            s = _jnp.sqrt(_jnp.mean(_jnp.square(w)) + 1e-30)
        else:
            s = MOMENT_SCALE[name]
        km, kv = _jax.random.split(_jax.random.fold_in(key, i + 1))
        out[name] = w
        out["m_" + name] = s * _jax.random.normal(km, w.shape, _jnp.float32)
        out["v_" + name] = (s * s) * _jax.random.uniform(kv, w.shape, _jnp.float32, 0.5, 1.5)
    if N_MICROBATCH > 1:
        for name, axis in PER_EXAMPLE_BATCH_AXIS.items():
            out[name] = _to_microbatches(out[name], axis)
    return {'x': out['x'], 'c': out['c'], 'w_ada': out['w_ada'], 'b_ada': out['b_ada'], 'norm_attn_g': out['norm_attn_g'], 'w_in': out['w_in'], 'rel_bias': out['rel_bias'], 'conv_w': out['conv_w'], 'a_log': out['a_log'], 'dt_bias': out['dt_bias'], 'delta_norm_g': out['delta_norm_g'], 'w_out': out['w_out'], 'norm_ffn_g': out['norm_ffn_g'], 'w_gate': out['w_gate'], 'w_up': out['w_up'], 'w_down': out['w_down'], 'final_norm_g': out['final_norm_g'], 'loss_target': out['loss_target'], 'm_w_ada': out['m_w_ada'], 'm_b_ada': out['m_b_ada'], 'm_norm_attn_g': out['m_norm_attn_g'], 'm_w_in': out['m_w_in'], 'm_rel_bias': out['m_rel_bias'], 'm_conv_w': out['m_conv_w'], 'm_a_log': out['m_a_log'], 'm_dt_bias': out['m_dt_bias'], 'm_delta_norm_g': out['m_delta_norm_g'], 'm_w_out': out['m_w_out'], 'm_norm_ffn_g': out['m_norm_ffn_g'], 'm_w_gate': out['m_w_gate'], 'm_w_up': out['m_w_up'], 'm_w_down': out['m_w_down'], 'm_final_norm_g': out['m_final_norm_g'], 'v_w_ada': out['v_w_ada'], 'v_b_ada': out['v_b_ada'], 'v_norm_attn_g': out['v_norm_attn_g'], 'v_w_in': out['v_w_in'], 'v_rel_bias': out['v_rel_bias'], 'v_conv_w': out['v_conv_w'], 'v_a_log': out['v_a_log'], 'v_dt_bias': out['v_dt_bias'], 'v_delta_norm_g': out['v_delta_norm_g'], 'v_w_out': out['v_w_out'], 'v_norm_ffn_g': out['v_norm_ffn_g'], 'v_w_gate': out['v_w_gate'], 'v_w_up': out['v_w_up'], 'v_w_down': out['v_w_down'], 'v_final_norm_g': out['v_final_norm_g']}


def _loss(weights, diff, rest, loss_target):
    with _jax.named_scope("forward"):
        args = {**rest, TWIN_DIFF_INPUT: diff, **{k: w.astype(_WEIGHT_DTYPES[k]) for k, w in weights.items()}}
        y = _forward(args)
    with _jax.named_scope("loss_head"):
        err = _jnp.square(y.astype(_jnp.float32) - loss_target)
        return 0.5 * _jnp.sum(_jnp.mean(err, axis=-1)) if err.ndim else 0.5 * err


def _adamw(w, g, m, v):
    m = ADAM_B1 * m + (1.0 - ADAM_B1) * g
    v = ADAM_B2 * v + (1.0 - ADAM_B2) * _jnp.square(g)
    m_hat = m / (1.0 - ADAM_B1 ** ADAM_STEP)
    v_hat = v / (1.0 - ADAM_B2 ** ADAM_STEP)
    delta = -ADAM_LR * (m_hat / (_jnp.sqrt(v_hat) + ADAM_EPS) + ADAM_WD * w)
    return delta, m, v


def reference(x, c, w_ada, b_ada, norm_attn_g, w_in, rel_bias, conv_w, a_log, dt_bias, delta_norm_g, w_out, norm_ffn_g, w_gate, w_up, w_down, final_norm_g, loss_target, m_w_ada, m_b_ada, m_norm_attn_g, m_w_in, m_rel_bias, m_conv_w, m_a_log, m_dt_bias, m_delta_norm_g, m_w_out, m_norm_ffn_g, m_w_gate, m_w_up, m_w_down, m_final_norm_g, v_w_ada, v_b_ada, v_norm_attn_g, v_w_in, v_rel_bias, v_conv_w, v_a_log, v_dt_bias, v_delta_norm_g, v_w_out, v_norm_ffn_g, v_w_gate, v_w_up, v_w_down, v_final_norm_g):
    given = dict(x=x, c=c, w_ada=w_ada, b_ada=b_ada, norm_attn_g=norm_attn_g, w_in=w_in, rel_bias=rel_bias, conv_w=conv_w, a_log=a_log, dt_bias=dt_bias, delta_norm_g=delta_norm_g, w_out=w_out, norm_ffn_g=norm_ffn_g, w_gate=w_gate, w_up=w_up, w_down=w_down, final_norm_g=final_norm_g, loss_target=loss_target, m_w_ada=m_w_ada, m_b_ada=m_b_ada, m_norm_attn_g=m_norm_attn_g, m_w_in=m_w_in, m_rel_bias=m_rel_bias, m_conv_w=m_conv_w, m_a_log=m_a_log, m_dt_bias=m_dt_bias, m_delta_norm_g=m_delta_norm_g, m_w_out=m_w_out, m_norm_ffn_g=m_norm_ffn_g, m_w_gate=m_w_gate, m_w_up=m_w_up, m_w_down=m_w_down, m_final_norm_g=m_final_norm_g, v_w_ada=v_w_ada, v_b_ada=v_b_ada, v_norm_attn_g=v_norm_attn_g, v_w_in=v_w_in, v_rel_bias=v_rel_bias, v_conv_w=v_conv_w, v_a_log=v_a_log, v_dt_bias=v_dt_bias, v_delta_norm_g=v_delta_norm_g, v_w_out=v_w_out, v_norm_ffn_g=v_norm_ffn_g, v_w_gate=v_w_gate, v_w_up=v_w_up, v_w_down=v_w_down, v_final_norm_g=v_final_norm_g)
    weights = {n: given[n] for n in TWIN_WEIGHTS}
    shared = {n: given[n] for n in SHARED_INPUTS}
    per_example = {n: given[n] for n in ['x', 'c']}
    grad_fn = _jax.value_and_grad(_loss, argnums=(0, 1))

    def one_microbatch(ex, loss_target):
        ex = dict(ex)
        diff = ex.pop(TWIN_DIFF_INPUT)
        return grad_fn(weights, diff, {**shared, **ex}, loss_target)

    if N_MICROBATCH == 1:
        loss, (grad_w, grad_x) = one_microbatch(per_example, given["loss_target"])
    else:
        def body(carry, xs):
            loss_sum, grad_sum = carry
            l_k, (gw_k, gx_k) = one_microbatch(xs[0], xs[1])
            with _jax.named_scope("update"):
                return (loss_sum + l_k, _jax.tree.map(_jnp.add, grad_sum, gw_k)), gx_k

        init = (_jnp.zeros((), _jnp.float32), _jax.tree.map(_jnp.zeros_like, weights))
        (loss, grad_w), grad_x = _jax.lax.scan(body, init, (per_example, given["loss_target"]))
    with _jax.named_scope("update"):
        delta_w, new_m, new_v = {}, {}, {}
        for n in TWIN_WEIGHTS:
            delta_w[n], new_m[n], new_v[n] = _adamw(weights[n], grad_w[n], given["m_" + n], given["v_" + n])
    return (loss, grad_x, *[grad_w[n] for n in TWIN_WEIGHTS], *[delta_w[n] for n in TWIN_WEIGHTS],
            *[new_m[n] for n in TWIN_WEIGHTS], *[new_v[n] for n in TWIN_WEIGHTS])
```

```python
import functools
import math

import numpy as np
import jax
import jax.numpy as jnp
from jax import lax
from jax.experimental import pallas as pl
from jax.experimental.pallas import tpu as pltpu

F32 = jnp.float32
BF16 = jnp.bfloat16
HIGHEST = lax.Precision.HIGHEST

D_MODEL = 1024
HEAD_DIM = 64
N_HEADS = 8
HEAD_W = 512
BRANCHES = ((128, 1), (512, 4), (2048, 16))
BAND = 128
ATT_TILE = 2048
N_BUCKETS = 32
MAX_DISTANCE = 2048
CHUNK = 64
D_FF = 2816
EPS = 1e-6
NEG_INF = -1e30
LANES = 128
VMEM_LIMIT = 56 * 1024 * 1024

ADAM_LR = 0.001
ADAM_B1 = 0.9
ADAM_B2 = 0.999
ADAM_EPS = 1e-08
ADAM_WD = 0.01
ADAM_STEP = 10


def _nn(a, b, precision=None):
    return jnp.dot(a, b, preferred_element_type=F32, precision=precision)


def _nt(a, b, precision=None):
    return lax.dot_general(a, b, (((1,), (1,)), ((), ())), preferred_element_type=F32, precision=precision)


def _tn(a, b, precision=None):
    return lax.dot_general(a, b, (((0,), (0,)), ((), ())), preferred_element_type=F32, precision=precision)


def _params(sem, vmem=VMEM_LIMIT):
    return pltpu.CompilerParams(dimension_semantics=sem, vmem_limit_bytes=vmem)


def _sigmoid(x):
    return 1.0 / (1.0 + jnp.exp(-x))


def _silu(x):
    return x * _sigmoid(x)


def _dsilu(x):
    s = _sigmoid(x)
    return s * (1.0 + x * (1.0 - s))


def _attn_tables():
    qi = np.arange(BAND)[:, None]
    kj = np.arange(2 * BAND)[None, :]
    steps = qi + BAND - kj
    in_window = (steps >= 0) & (steps <= BAND)
    max_exact = N_BUCKETS // 2
    out = np.zeros((3, 2, BAND, 2 * BAND), np.int32)
    for b, (_, dil) in enumerate(BRANCHES):
        dist = np.maximum(steps, 0) * dil
        dist_f = np.maximum(dist, 1).astype(np.float32)
        large = max_exact + (np.log(dist_f / np.float32(max_exact)) / np.float32(math.log(MAX_DISTANCE / max_exact))
                             * np.float32(N_BUCKETS - max_exact)).astype(np.int32)
        bucket = np.where(dist < max_exact, dist, np.minimum(large, N_BUCKETS - 1)).astype(np.int32)
        out[b, 0] = np.where(in_window, bucket, -1)
        out[b, 1] = np.where(in_window & (kj >= BAND), bucket, -1)
    return out


def _attn_bias_tables(rel_ref, tab_ref, bias_s, pair):
    for b in range(3):
        for first in range(2):
            tab = tab_ref[b, first]
            for hh in range(2):
                head = 2 * pair + hh

                def pick(kk, acc, tab=tab, head=head):
                    return jnp.where(tab == kk, rel_ref[kk, head], acc)

                acc = lax.fori_loop(0, N_BUCKETS, pick, jnp.zeros((BAND, 2 * BAND), F32))
                bias_s[b, hh, first] = jnp.where(tab < 0, NEG_INF, acc)


def _attn_block_index(idx, t, r):
    nb = ATT_TILE // (BAND * r)
    rho = idx // nb
    n = idx % nb
    qs = rho + r * BAND * n
    gs = t * ATT_TILE + qs
    first = (t * nb + n) == 0
    ps = jnp.where(first, gs, gs - r * BAND)
    return qs, gs, ps, first.astype(jnp.int32)


def _rows(start, r):
    return pl.ds(start, BAND) if r == 1 else pl.ds(start, BAND, stride=r)


def _attention_fwd(qkv, rel_bias, tables):
    seq = qkv.shape[0]
    n_tiles = seq // ATT_TILE

    def body(rel_ref, tab_ref, q_ref, k_ref, v_ref, y_ref, lse_ref, bias_s, o_s, l_s):
        pair = pl.program_id(0)
        t = pl.program_id(1)
        lane = lax.broadcasted_iota(jnp.int32, (1, LANES), 1)
        head0 = lane < HEAD_DIM

        @pl.when(t == 0)
        def _():
            _attn_bias_tables(rel_ref, tab_ref, bias_s, pair)

        for b, (_, r) in enumerate(BRANCHES):
            def block(idx, carry, b=b, r=r):
                qs, gs, ps, first = _attn_block_index(idx, t, r)
                qb = q_ref[_rows(qs, r), :] * (HEAD_DIM ** -0.5)
                kcat = jnp.concatenate([k_ref[_rows(ps, r), :], k_ref[_rows(gs, r), :]], axis=0).astype(BF16)
                vcat = jnp.concatenate([v_ref[_rows(ps, r), :], v_ref[_rows(gs, r), :]], axis=0).astype(BF16)
                outs, lses = [], []
                for hh in range(2):
                    mine = head0 if hh == 0 else jnp.logical_not(head0)
                    qh = jnp.where(mine, qb, 0.0).astype(BF16)
                    s = _nt(qh, kcat) + bias_s[b, hh, first]
                    m = jnp.max(s, axis=-1, keepdims=True)
                    e = jnp.exp(s - m)
                    den = jnp.sum(e, axis=-1, keepdims=True)
                    outs.append(_nn(e.astype(BF16), vcat) / den)
                    lses.append(m + jnp.log(den))
                o_s[b, _rows(qs, r), :] = jnp.where(head0, outs[0], outs[1])
                l_s[b, _rows(qs, r), :] = jnp.where(head0, lses[0], lses[1])
                return carry

            lax.fori_loop(0, ATT_TILE // BAND, block, 0)

        def merge(i, carry):
            rows = pl.ds(pl.multiple_of(i * BAND, BAND), BAND)
            l0, l1, l2 = l_s[0, rows, :], l_s[1, rows, :], l_s[2, rows, :]
            m = jnp.maximum(jnp.maximum(l0, l1), l2)
            w0, w1, w2 = jnp.exp(l0 - m), jnp.exp(l1 - m), jnp.exp(l2 - m)
            tot = w0 + w1 + w2
            y_ref[rows, :] = (w0 * o_s[0, rows, :] + w1 * o_s[1, rows, :] + w2 * o_s[2, rows, :]) / tot
            lse_ref[rows, :] = m + jnp.log(tot)
            return carry

        lax.fori_loop(0, ATT_TILE // BAND, merge, 0)

    tile = pl.BlockSpec((ATT_TILE, LANES), lambda p, t: (t, p))
    return pl.pallas_call(
        body,
        grid=(N_HEADS // 2, n_tiles),
        in_specs=[
            pl.BlockSpec(memory_space=pltpu.SMEM),
            pl.BlockSpec((3, 2, BAND, 2 * BAND), lambda p, t: (0, 0, 0, 0)),
            pl.BlockSpec((ATT_TILE, LANES), lambda p, t: (t, p)),
            pl.BlockSpec((seq, LANES), lambda p, t: (0, 4 + p)),
            pl.BlockSpec((seq, LANES), lambda p, t: (0, 8 + p)),
        ],
        out_specs=[tile, tile],
        out_shape=[jax.ShapeDtypeStruct((seq, HEAD_W), F32), jax.ShapeDtypeStruct((seq, HEAD_W), F32)],
        scratch_shapes=[
            pltpu.VMEM((3, 2, 2, BAND, 2 * BAND), F32),
            pltpu.VMEM((3, ATT_TILE, LANES), F32),
            pltpu.VMEM((3, ATT_TILE, LANES), F32),
        ],
        compiler_params=_params(("arbitrary", "arbitrary")),
        name="attn_fwd",
    )(rel_bias, tables, qkv, qkv, qkv)


def _attention_bwd(qkv, dy, y, lse, rel_bias, tables):
    seq = qkv.shape[0]
    n_tiles = seq // ATT_TILE

    def body(rel_ref, tab_ref, q_ref, k_ref, v_ref, dy_ref, y_ref, lse_ref,
             dq_ref, dk_ref, dv_ref, dbias_ref, bias_s):
        pair = pl.program_id(0)
        t = pl.program_id(1)
        lane = lax.broadcasted_iota(jnp.int32, (1, LANES), 1)
        head0 = lane < HEAD_DIM

        @pl.when(t == 0)
        def _():
            _attn_bias_tables(rel_ref, tab_ref, bias_s, pair)
            dk_ref[...] = jnp.zeros_like(dk_ref)
            dv_ref[...] = jnp.zeros_like(dv_ref)
            dbias_ref[...] = jnp.zeros_like(dbias_ref)

        dq_ref[...] = jnp.zeros_like(dq_ref)

        for b, (_, r) in enumerate(BRANCHES):
            def block(idx, carry, b=b, r=r):
                qs, gs, ps, first = _attn_block_index(idx, t, r)
                scale = HEAD_DIM ** -0.5
                qb = q_ref[_rows(qs, r), :] * scale
                kcat = jnp.concatenate([k_ref[_rows(ps, r), :], k_ref[_rows(gs, r), :]], axis=0).astype(BF16)
                vcat = jnp.concatenate([v_ref[_rows(ps, r), :], v_ref[_rows(gs, r), :]], axis=0).astype(BF16)
                dob = dy_ref[_rows(qs, r), :]
                ob = y_ref[_rows(qs, r), :]
                lb = lse_ref[_rows(qs, r), :]
                dqs = []
                dkcat = jnp.zeros((2 * BAND, LANES), F32)
                dvcat = jnp.zeros((2 * BAND, LANES), F32)
                for hh in range(2):
                    mine = head0 if hh == 0 else jnp.logical_not(head0)
                    qh = jnp.where(mine, qb, 0.0).astype(BF16)
                    s = _nt(qh, kcat) + bias_s[b, hh, first]
                    lcol = jnp.max(jnp.where(mine, lb, -jnp.inf), axis=-1, keepdims=True)
                    prob = jnp.exp(s - lcol)
                    doh = jnp.where(mine, dob, 0.0)
                    delta = jnp.sum(doh * ob, axis=-1, keepdims=True)
                    dp = _nt(doh.astype(BF16), vcat)
                    ds = prob * (dp - delta)
                    dbias_ref[0, b, hh] += ds
                    dsb = ds.astype(BF16)
                    dqs.append(_nn(dsb, kcat))
                    dkcat = dkcat + _tn(dsb, qh)
                    dvcat = dvcat + _tn(prob.astype(BF16), doh.astype(BF16))
                dq_ref[_rows(qs, r), :] += jnp.where(head0, dqs[0], dqs[1]) * scale
                dk_ref[_rows(ps, r), :] += dkcat[:BAND]
                dk_ref[_rows(gs, r), :] += dkcat[BAND:]
                dv_ref[_rows(ps, r), :] += dvcat[:BAND]
                dv_ref[_rows(gs, r), :] += dvcat[BAND:]
                return carry

            lax.fori_loop(0, ATT_TILE // BAND, block, 0)

    tile = pl.BlockSpec((ATT_TILE, LANES), lambda p, t: (t, p))
    full = pl.BlockSpec((seq, LANES), lambda p, t: (0, p))
    return pl.pallas_call(
        body,
        grid=(N_HEADS // 2, n_tiles),
        in_specs=[
            pl.BlockSpec(memory_space=pltpu.SMEM),
            pl.BlockSpec((3, 2, BAND, 2 * BAND), lambda p, t: (0, 0, 0, 0)),
            pl.BlockSpec((ATT_TILE, LANES), lambda p, t: (t, p)),
            pl.BlockSpec((seq, LANES), lambda p, t: (0, 4 + p)),
            pl.BlockSpec((seq, LANES), lambda p, t: (0, 8 + p)),
            tile, tile, tile,
        ],
        out_specs=[tile, full, full,
                   pl.BlockSpec((1, 3, 2, BAND, 2 * BAND), lambda p, t: (p, 0, 0, 0, 0))],
        out_shape=[jax.ShapeDtypeStruct((seq, HEAD_W), F32)] * 3
        + [jax.ShapeDtypeStruct((N_HEADS // 2, 3, 2, BAND, 2 * BAND), F32)],
        scratch_shapes=[pltpu.VMEM((3, 2, 2, BAND, 2 * BAND), F32)],
        compiler_params=_params(("arbitrary", "arbitrary")),
        name="attn_bwd",
    )(rel_bias, tables, qkv, qkv, qkv, dy, y, lse)


def _rel_bias_grad(dbias, tables):
    def body(tab_ref, db_ref, out_ref):
        lane = lax.broadcasted_iota(jnp.int32, (1, LANES), 1)
        for pair in range(N_HEADS // 2):
            for hh in range(2):
                row = jnp.zeros((1, LANES), F32)
                for b in range(3):
                    tab = tab_ref[b, 0]
                    d = db_ref[pair, b, hh]

                    def bucket_sum(kk, row, tab=tab, d=d):
                        s = jnp.sum(jnp.where(tab == kk, d, 0.0), keepdims=True)
                        return row + jnp.where(lane == kk, s, 0.0)

                    row = lax.fori_loop(0, N_BUCKETS, bucket_sum, row)
                out_ref[pl.ds(2 * pair + hh, 1), :] = row

    return pl.pallas_call(
        body,
        out_shape=jax.ShapeDtypeStruct((N_HEADS, LANES), F32),
        compiler_params=pltpu.CompilerParams(vmem_limit_bytes=VMEM_LIMIT),
        name="rel_bias_grad",
    )(tables, dbias)


ROW_TILE = 512


def _head_sum_matrix():
    return (lax.broadcasted_iota(jnp.int32, (HEAD_W, LANES), 0) // HEAD_DIM
            == lax.broadcasted_iota(jnp.int32, (HEAD_W, LANES), 1)).astype(F32)


def _head_spread_matrix(offset=0):
    return (lax.broadcasted_iota(jnp.int32, (LANES, HEAD_W), 0)
            == lax.broadcasted_iota(jnp.int32, (LANES, HEAD_W), 1) // HEAD_DIM + offset).astype(F32)


def _head_gather_matrix(offset=0):
    return (lax.broadcasted_iota(jnp.int32, (HEAD_W, LANES), 0) // HEAD_DIM + offset
            == lax.broadcasted_iota(jnp.int32, (HEAD_W, LANES), 1)).astype(F32)


def _tri(lower, strict=False):
    r = lax.broadcasted_iota(jnp.int32, (CHUNK, CHUNK), 0)
    c = lax.broadcasted_iota(jnp.int32, (CHUNK, CHUNK), 1)
    if lower:
        return (c < r) if strict else (c <= r)
    return c >= r


def _softplus(z):
    return jnp.maximum(z, 0.0) + jnp.log(1.0 + jnp.exp(-jnp.abs(z)))


def _conv_taps(stage, w_ref, rows):
    return (w_ref[3:4, :] * stage[8:8 + rows, :] + w_ref[2:3, :] * stage[7:7 + rows, :]
            + w_ref[1:2, :] * stage[6:6 + rows, :] + w_ref[0:1, :] * stage[5:5 + rows, :])


def _l2_scale(xc, hsum, hspread):
    ssq = _nn(xc * xc, hsum, HIGHEST)
    return _nn(lax.rsqrt(ssq + EPS), hspread, HIGHEST)


def _stage_rows(stage, x_ref, xp_ref, i):
    stage[0:8, :] = jnp.where(i == 0, 0.0, xp_ref[...])
    stage[8:8 + ROW_TILE, :] = x_ref[...]


def _delta_prep_fwd(qkvz, ba, conv_w, alog_row, dt_row):
    seq = qkvz.shape[0]
    qkv_w = 3 * HEAD_W

    def body(x_ref, xp_ref, ba_ref, w_ref, al_ref, dt_ref, out_ref, stage):
        i = pl.program_id(0)
        _stage_rows(stage, x_ref, xp_ref, i)
        act = _silu(_conv_taps(stage, w_ref, ROW_TILE))
        hsum, hspread = _head_sum_matrix(), _head_spread_matrix()
        qc, kc = act[:, :HEAD_W], act[:, HEAD_W:2 * HEAD_W]
        out_ref[0] = qc * _l2_scale(qc, hsum, hspread) * (HEAD_DIM ** -0.5)
        out_ref[1] = kc * _l2_scale(kc, hsum, hspread)
        out_ref[2] = act[:, 2 * HEAD_W:]
        bav = ba_ref[...]
        out_ref[3] = _nn(_sigmoid(bav), hspread, HIGHEST)
        g8 = -jnp.exp(al_ref[...]) * _softplus(bav + dt_ref[...])
        gb = _nn(g8, _head_spread_matrix(N_HEADS), HIGHEST)
        cum = _tri(True).astype(F32)
        for ch in range(ROW_TILE // CHUNK):
            rows = slice(ch * CHUNK, (ch + 1) * CHUNK)
            out_ref[4, rows, :] = _nn(cum, gb[rows], HIGHEST)

    return pl.pallas_call(
        body,
        grid=(seq // ROW_TILE,),
        in_specs=[
            pl.BlockSpec((ROW_TILE, qkv_w), lambda i: (i, 0)),
            pl.BlockSpec((8, qkv_w), lambda i: (jnp.maximum(i * (ROW_TILE // 8) - 1, 0), 0)),
            pl.BlockSpec((ROW_TILE, LANES), lambda i: (i, 0)),
            pl.BlockSpec((4, qkv_w), lambda i: (0, 0)),
            pl.BlockSpec((1, LANES), lambda i: (0, 0)),
            pl.BlockSpec((1, LANES), lambda i: (0, 0)),
        ],
        out_specs=pl.BlockSpec((5, ROW_TILE, HEAD_W), lambda i: (0, i, 0)),
        out_shape=jax.ShapeDtypeStruct((5, seq, HEAD_W), F32),
        scratch_shapes=[pltpu.VMEM((ROW_TILE + 8, qkv_w), F32)],
        compiler_params=_params(("arbitrary",)),
        name="delta_prep_fwd",
    )(qkvz, qkvz, ba, conv_w, alog_row, dt_row)


def _unit_lower_inverse(a):
    eye = (lax.broadcasted_iota(jnp.int32, (CHUNK, CHUNK), 0)
           == lax.broadcasted_iota(jnp.int32, (CHUNK, CHUNK), 1)).astype(F32)
    inv = eye - a
    power = a
    for _ in range(5):
        power = _nn(power, power, HIGHEST)
        inv = inv + _nn(inv, power, HIGHEST)
    return inv


def _chunk_terms(q, k, v, beta, gc):
    causal, strict = _tri(True), _tri(True, strict=True)
    e = jnp.exp(gc)
    g_last = jnp.broadcast_to(gc[CHUNK - 1:CHUNK, :], (CHUNK, CHUNK))
    f = jnp.exp(g_last - gc)
    e_last = jnp.exp(g_last)
    decay = jnp.where(causal, jnp.exp(jnp.where(causal, gc - gc.T, 0.0)), 0.0)
    kb = k * beta
    a_mat = jnp.where(strict, _nt(kb.astype(BF16), k.astype(BF16)) * decay, 0.0)
    qk = jnp.where(causal, _nt(q.astype(BF16), k.astype(BF16)) * decay, 0.0)
    return e, f, e_last, decay, kb, a_mat, qk


GROUP = 8


def _delta_scan_fwd(xh):
    seq = xh.shape[2]
    rows_per_step = GROUP * CHUNK

    def body(x_ref, o_ref, inv_ref, st_ref, state):
        @pl.when(pl.program_id(1) == 0)
        def _():
            state[...] = jnp.zeros_like(state)

        def chunk(ci, carry):
            rows = pl.ds(pl.multiple_of(ci * CHUNK, CHUNK), CHUNK)
            q, k, v, beta, gc = (x_ref[j, 0, rows, :] for j in range(5))
            e, f, e_last, _, kb, a_mat, qk = _chunk_terms(q, k, v, beta, gc)
            inv = _unit_lower_inverse(a_mat)
            u = _nn(inv, v * beta, HIGHEST)
            w = _nn(inv, kb * e, HIGHEST)
            s = state[...]
            sb = s.astype(BF16)
            v_new = u - _nn(w.astype(BF16), sb)
            o_ref[0, rows, :] = _nn((q * e).astype(BF16), sb) + _nn(qk.astype(BF16), v_new.astype(BF16))
            inv_ref[0, rows, :] = inv
            st_ref[0, rows, :] = s
            state[...] = s * e_last + _tn((k * f).astype(BF16), v_new.astype(BF16))
            return carry

        lax.fori_loop(0, GROUP, chunk, 0)

    out = pl.BlockSpec((1, rows_per_step, HEAD_DIM), lambda h, g: (h, g, 0))
    return pl.pallas_call(
        body,
        grid=(N_HEADS, seq // rows_per_step),
        in_specs=[pl.BlockSpec((5, 1, rows_per_step, HEAD_DIM), lambda h, g: (0, h, g, 0))],
        out_specs=[out, out, out],
        out_shape=[jax.ShapeDtypeStruct((N_HEADS, seq, HEAD_DIM), F32)] * 3,
        scratch_shapes=[pltpu.VMEM((CHUNK, CHUNK), F32)],
        compiler_params=_params(("arbitrary", "arbitrary")),
        name="delta_scan_fwd",
    )(xh)


def _delta_scan_bwd(xh, inv_h, st_h, do_h):
    seq = xh.shape[2]
    rows_per_step = GROUP * CHUNK
    n_steps = seq // rows_per_step

    def body(x_ref, inv_ref, st_ref, do_ref, dx_ref, dstate):
        @pl.when(pl.program_id(1) == 0)
        def _():
            dstate[...] = jnp.zeros_like(dstate)

        causal, strict = _tri(True), _tri(True, strict=True)
        last_row = lax.broadcasted_iota(jnp.int32, (CHUNK, CHUNK), 0) == CHUNK - 1

        def chunk(step, carry):
            ci = GROUP - 1 - step
            rows = pl.ds(pl.multiple_of(ci * CHUNK, CHUNK), CHUNK)
            q, k, v, beta, gc = (x_ref[j, 0, rows, :] for j in range(5))
            e, f, e_last, decay, kb, a_mat, qk = _chunk_terms(q, k, v, beta, gc)
            inv = inv_ref[0, rows, :]
            s = st_ref[0, rows, :]
            do = do_ref[0, rows, :]
            ds_next = dstate[...]
            u = _nn(inv, v * beta, HIGHEST)
            w = _nn(inv, kb * e, HIGHEST)
            sb, dob, dsb = s.astype(BF16), do.astype(BF16), ds_next.astype(BF16)
            v_new = u - _nn(w.astype(BF16), sb)
            vnb = v_new.astype(BF16)
            qe, kf = q * e, k * f

            dv_new = _tn(qk.astype(BF16), dob) + _nn(kf.astype(BF16), dsb)
            dvb = dv_new.astype(BF16)
            dqk = jnp.where(causal, _nt(dob, vnb), 0.0)
            dqe = _nt(dob, sb)
            dstate[...] = _tn(qe.astype(BF16), dob) + e_last * ds_next - _tn(w.astype(BF16), dvb)
            d_elast = s * ds_next * e_last
            dkf = _nt(vnb, dsb)
            dw = -_nt(dvb, sb)
            drhs_u = _tn(inv, dv_new, HIGHEST)
            drhs_w = _tn(inv, dw, HIGHEST)
            da = -jnp.where(strict, _nt(drhs_u.astype(BF16), u.astype(BF16)) + _nt(drhs_w.astype(BF16), w.astype(BF16)), 0.0)
            dkb = e * drhs_w
            de_full = kb * drhs_w + q * dqe
            dad = (da * decay).astype(BF16)
            dqd = (dqk * decay).astype(BF16)
            kbf, qbf, kbb = k.astype(BF16), q.astype(BF16), kb.astype(BF16)
            dkb = dkb + _nn(dad, kbf)
            dk = _tn(dad, kbb) + _tn(dqd, qbf) + f * dkf + beta * dkb
            dq = _nn(dqd, kbf) + e * dqe
            df_full = k * dkf
            m = da * a_mat + dqk * qk
            dgc = de_full * e - df_full * f + m - m.T
            tail = jnp.sum(df_full * f + d_elast, axis=0, keepdims=True)
            dgc = dgc + jnp.where(last_row, jnp.broadcast_to(tail, (CHUNK, CHUNK)), 0.0)
            dx_ref[0, 0, rows, :] = dq
            dx_ref[1, 0, rows, :] = dk
            dx_ref[2, 0, rows, :] = beta * drhs_u
            dx_ref[3, 0, rows, :] = v * drhs_u + k * dkb
            dx_ref[4, 0, rows, :] = dgc
            return carry

        lax.fori_loop(0, GROUP, chunk, 0)

    blk = pl.BlockSpec((1, rows_per_step, HEAD_DIM), lambda h, g: (h, n_steps - 1 - g, 0))
    blk5 = pl.BlockSpec((5, 1, rows_per_step, HEAD_DIM), lambda h, g: (0, h, n_steps - 1 - g, 0))
    return pl.pallas_call(
        body,
        grid=(N_HEADS, n_steps),
        in_specs=[blk5, blk, blk, blk],
        out_specs=blk5,
        out_shape=jax.ShapeDtypeStruct((5, N_HEADS, seq, HEAD_DIM), F32),
        scratch_shapes=[pltpu.VMEM((CHUNK, CHUNK), F32)],
        compiler_params=_params(("arbitrary", "arbitrary")),
        name="delta_scan_bwd",
    )(xh, inv_h, st_h, do_h)


def _delta_post_fwd(o, qkvz, gain_row):
    seq = o.shape[0]

    def body(o_ref, z_ref, g_ref, y_ref):
        ov = o_ref[...]
        ms = _nn(ov * ov, _head_sum_matrix(), HIGHEST) * (1.0 / HEAD_DIM)
        rb = _nn(lax.rsqrt(ms + EPS), _head_spread_matrix(), HIGHEST)
        y_ref[...] = (ov * rb * g_ref[...] * _silu(z_ref[...])).astype(y_ref.dtype)

    tile = pl.BlockSpec((ROW_TILE, HEAD_W), lambda i: (i, 0))
    return pl.pallas_call(
        body,
        grid=(seq // ROW_TILE,),
        in_specs=[tile, pl.BlockSpec((ROW_TILE, HEAD_W), lambda i: (i, 3)), pl.BlockSpec((1, HEAD_W), lambda i: (0, 0))],
        out_specs=tile,
        out_shape=jax.ShapeDtypeStruct((seq, HEAD_W), BF16),
        compiler_params=_params(("arbitrary",)),
        name="delta_post_fwd",
    )(o, qkvz, gain_row)


def _delta_post_bwd(dy, o, qkvz, gain_row):
    seq = o.shape[0]

    def body(dy_ref, o_ref, z_ref, g_ref, do_ref, dz_ref, dg_ref):
        @pl.when(pl.program_id(0) == 0)
        def _():
            dg_ref[...] = jnp.zeros_like(dg_ref)

        ov, zv, dyv, gain = o_ref[...], z_ref[...], dy_ref[...], g_ref[...]
        hsum, hspread = _head_sum_matrix(), _head_spread_matrix()
        ms = _nn(ov * ov, hsum, HIGHEST) * (1.0 / HEAD_DIM)
        rb = _nn(lax.rsqrt(ms + EPS), hspread, HIGHEST)
        ohat = ov * rb
        dz_ref[...] = dyv * ohat * gain * _dsilu(zv)
        dn = dyv * _silu(zv)
        dg_ref[0:1, :] += jnp.sum(dn * ohat, axis=0, keepdims=True)
        dohat = dn * gain

        @pl.when(pl.program_id(0) == pl.num_programs(0) - 1)
        def _():
            fold = (lax.broadcasted_iota(jnp.int32, (HEAD_W, HEAD_W), 0) % HEAD_DIM
                    == lax.broadcasted_iota(jnp.int32, (HEAD_W, HEAD_W), 1)).astype(F32)
            dg_ref[1:2, :] = _nn(dg_ref[0:1, :], fold, HIGHEST)

        proj = _nn(_nn(dohat * ohat, hsum, HIGHEST) * (1.0 / HEAD_DIM), hspread, HIGHEST)
        do_ref[...] = rb * (dohat - ohat * proj)

    tile = pl.BlockSpec((ROW_TILE, HEAD_W), lambda i: (i, 0))
    return pl.pallas_call(
        body,
        grid=(seq // ROW_TILE,),
        in_specs=[pl.BlockSpec((ROW_TILE, HEAD_W), lambda i: (i, 1)), tile,
                  pl.BlockSpec((ROW_TILE, HEAD_W), lambda i: (i, 3)), pl.BlockSpec((1, HEAD_W), lambda i: (0, 0))],
        out_specs=[tile, tile, pl.BlockSpec((2, HEAD_W), lambda i: (0, 0))],
        out_shape=[jax.ShapeDtypeStruct((seq, HEAD_W), F32), jax.ShapeDtypeStruct((seq, HEAD_W), F32),
                   jax.ShapeDtypeStruct((2, HEAD_W), F32)],
        compiler_params=_params(("arbitrary",)),
        name="delta_post_bwd",
    )(dy, o, qkvz, gain_row)


def _delta_prep_bwd(qkvz, ba, conv_w, alog_row, dt_row, dxs):
    seq = qkvz.shape[0]
    qkv_w = 3 * HEAD_W

    def body(x_ref, xp_ref, ba_ref, w_ref, al_ref, dt_ref, dx_ref, dconv_ref, dba_ref, dvec_ref, stage):
        i = pl.program_id(0)

        @pl.when(i == 0)
        def _():
            dvec_ref[...] = jnp.zeros_like(dvec_ref)

        _stage_rows(stage, x_ref, xp_ref, i)
        pre = _conv_taps(stage, w_ref, ROW_TILE)
        act = _silu(pre)
        slope = _dsilu(pre)
        hsum, hspread = _head_sum_matrix(), _head_spread_matrix()
        for j, scale in ((0, HEAD_DIM ** -0.5), (1, 1.0)):
            cols = slice(j * HEAD_W, (j + 1) * HEAD_W)
            xc = act[:, cols]
            rb = _l2_scale(xc, hsum, hspread)
            xhat = xc * rb
            dhat = dx_ref[j] * scale
            proj = _nn(_nn(dhat * xhat, hsum, HIGHEST), hspread, HIGHEST)
            dconv_ref[:, cols] = rb * (dhat - xhat * proj) * slope[:, cols]
        dconv_ref[:, 2 * HEAD_W:] = dx_ref[2] * slope[:, 2 * HEAD_W:]

        bav = ba_ref[...]
        beta8 = _sigmoid(bav)
        dbeta8 = _nn(dx_ref[3], _head_gather_matrix(), HIGHEST)
        dgc8 = _nn(dx_ref[4], _head_gather_matrix(N_HEADS), HIGHEST)
        rev = _tri(False).astype(F32)
        z = bav + dt_ref[...]
        ea = jnp.exp(al_ref[...])
        g8 = -ea * _softplus(z)
        sig = _sigmoid(z)
        d_alog = jnp.zeros((1, LANES), F32)
        d_dt = jnp.zeros((1, LANES), F32)
        for ch in range(ROW_TILE // CHUNK):
            rows = slice(ch * CHUNK, (ch + 1) * CHUNK)
            dg8 = _nn(rev, dgc8[rows], HIGHEST)
            da = -dg8 * ea * sig[rows]
            dba_ref[rows, :] = dbeta8[rows] * beta8[rows] * (1.0 - beta8[rows]) + da
            d_alog = d_alog + jnp.sum(dg8 * g8[rows], axis=0, keepdims=True)
            d_dt = d_dt + jnp.sum(da, axis=0, keepdims=True)
        dvec_ref[0:1, :] += d_alog
        dvec_ref[1:2, :] += d_dt

    return pl.pallas_call(
        body,
        grid=(seq // ROW_TILE,),
        in_specs=[
            pl.BlockSpec((ROW_TILE, qkv_w), lambda i: (i, 0)),
            pl.BlockSpec((8, qkv_w), lambda i: (jnp.maximum(i * (ROW_TILE // 8) - 1, 0), 0)),
            pl.BlockSpec((ROW_TILE, LANES), lambda i: (i, 0)),
            pl.BlockSpec((4, qkv_w), lambda i: (0, 0)),
            pl.BlockSpec((1, LANES), lambda i: (0, 0)),
            pl.BlockSpec((1, LANES), lambda i: (0, 0)),
            pl.BlockSpec((5, ROW_TILE, HEAD_W), lambda i: (0, i, 0)),
        ],
        out_specs=[pl.BlockSpec((ROW_TILE, qkv_w), lambda i: (i, 0)),
                   pl.BlockSpec((ROW_TILE, LANES), lambda i: (i, 0)),
                   pl.BlockSpec((2, LANES), lambda i: (0, 0))],
        out_shape=[jax.ShapeDtypeStruct((seq, qkv_w), F32), jax.ShapeDtypeStruct((seq, LANES), F32),
                   jax.ShapeDtypeStruct((2, LANES), F32)],
        scratch_shapes=[pltpu.VMEM((ROW_TILE + 8, qkv_w), F32)],
        compiler_params=_params(("arbitrary",)),
        name="delta_prep_bwd",
    )(qkvz, qkvz, ba, conv_w, alog_row, dt_row, dxs)


def _conv_bwd(dconv, qkvz, conv_w):
    seq = dconv.shape[0]
    qkv_w = 3 * HEAD_W
    n_tiles = seq // ROW_TILE

    def body(dy_ref, dyn_ref, x_ref, xp_ref, w_ref, dx_ref, dw_ref, stage, dstage):
        i = pl.program_id(0)

        @pl.when(i == 0)
        def _():
            dw_ref[...] = jnp.zeros_like(dw_ref)

        _stage_rows(stage, x_ref, xp_ref, i)
        dstage[0:ROW_TILE, :] = dy_ref[...]
        dstage[ROW_TILE:ROW_TILE + 8, :] = jnp.where(i == n_tiles - 1, 0.0, dyn_ref[...])
        dy = dy_ref[...]
        dx_ref[...] = (w_ref[3:4, :] * dy + w_ref[2:3, :] * dstage[1:1 + ROW_TILE, :]
                       + w_ref[1:2, :] * dstage[2:2 + ROW_TILE, :] + w_ref[0:1, :] * dstage[3:3 + ROW_TILE, :])
        for j in range(4):
            dw_ref[j:j + 1, :] += jnp.sum(dy * stage[5 + j:5 + j + ROW_TILE, :], axis=0, keepdims=True)

    tile = pl.BlockSpec((ROW_TILE, qkv_w), lambda i: (i, 0))
    return pl.pallas_call(
        body,
        grid=(n_tiles,),
        in_specs=[
            tile,
            pl.BlockSpec((8, qkv_w), lambda i: (jnp.minimum((i + 1) * (ROW_TILE // 8), seq // 8 - 1), 0)),
            tile,
            pl.BlockSpec((8, qkv_w), lambda i: (jnp.maximum(i * (ROW_TILE // 8) - 1, 0), 0)),
            pl.BlockSpec((4, qkv_w), lambda i: (0, 0)),
        ],
        out_specs=[tile, pl.BlockSpec((4, qkv_w), lambda i: (0, 0))],
        out_shape=[jax.ShapeDtypeStruct((seq, qkv_w), F32), jax.ShapeDtypeStruct((4, qkv_w), F32)],
        scratch_shapes=[pltpu.VMEM((ROW_TILE + 8, qkv_w), F32), pltpu.VMEM((ROW_TILE + 8, qkv_w), F32)],
        compiler_params=_params(("arbitrary",)),
        name="conv_bwd",
    )(dconv, dconv, qkvz, qkvz, conv_w)


def _to_heads(a):
    lead = a.shape[:-2]
    seq = a.shape[-2]
    a = a.reshape(*lead, seq, N_HEADS, HEAD_DIM)
    return jnp.swapaxes(a, -2, -3)


def _from_heads(a):
    a = jnp.swapaxes(a, -2, -3)
    return a.reshape(*a.shape[:-2], HEAD_W)


FF_TILE = 1408
FF_TILE_BWD = 256


def _row(a):
    return pl.BlockSpec((1, a), lambda *_: (0, 0))


def _rms_fwd(xv, gain):
    rstd = lax.rsqrt(jnp.mean(xv * xv, axis=-1, keepdims=True) + EPS)
    xhat = xv * rstd
    return xhat, rstd, xhat * gain


def _rms_bwd(dnorm, xhat, rstd, gain):
    dxhat = dnorm * gain
    dx = rstd * (dxhat - xhat * jnp.mean(dxhat * xhat, axis=-1, keepdims=True))
    return dx, jnp.sum(dnorm * xhat, axis=0, keepdims=True)


def _inproj_fwd(x, gain, scale, shift, w_a, w_d, w_ba):
    seq = x.shape[0]

    def body(x_ref, g_ref, sc_ref, sh_ref, wa_ref, wd_ref, wb_ref, h_ref, a_ref, d_ref, b_ref):
        _, _, norm = _rms_fwd(x_ref[...], g_ref[...])
        h = (norm * (1.0 + sc_ref[...]) + sh_ref[...]).astype(BF16)
        h_ref[...] = h
        a_ref[...] = _nn(h, wa_ref[...])
        d_ref[...] = _nn(h, wd_ref[...])
        b_ref[...] = _nn(h, wb_ref[...])

    def rows(width):
        return pl.BlockSpec((ROW_TILE, width), lambda i: (i, 0))

    def whole(a):
        return pl.BlockSpec(a.shape, lambda i: (0, 0))

    return pl.pallas_call(
        body,
        grid=(seq // ROW_TILE,),
        in_specs=[rows(D_MODEL), _row(D_MODEL), _row(D_MODEL), _row(D_MODEL), whole(w_a), whole(w_d), whole(w_ba)],
        out_specs=[rows(D_MODEL), rows(3 * HEAD_W), rows(4 * HEAD_W), rows(LANES)],
        out_shape=[jax.ShapeDtypeStruct((seq, D_MODEL), BF16), jax.ShapeDtypeStruct((seq, 3 * HEAD_W), F32),
                   jax.ShapeDtypeStruct((seq, 4 * HEAD_W), F32), jax.ShapeDtypeStruct((seq, LANES), F32)],
        compiler_params=_params(("arbitrary",)),
        name="inproj_fwd",
    )(x, gain, scale, shift, w_a, w_d, w_ba)


def _outproj_fwd(y_attn, y_delta, w_out, x, gate1, gain, scale, shift):
    seq = x.shape[0]

    def body(ya_ref, yd_ref, wa_ref, wd_ref, x_ref, g1_ref, g_ref, sc_ref, sh_ref, x1_ref, h_ref, y_ref):
        y = _nn(ya_ref[...].astype(BF16), wa_ref[...]) + _nn(yd_ref[...], wd_ref[...])
        x1 = x_ref[...] + g1_ref[...] * y
        _, _, norm = _rms_fwd(x1, g_ref[...])
        x1_ref[...] = x1
        h_ref[...] = (norm * (1.0 + sc_ref[...]) + sh_ref[...]).astype(BF16)
        y_ref[...] = y.astype(BF16)

    def rows(width):
        return pl.BlockSpec((ROW_TILE, width), lambda i: (i, 0))

    return pl.pallas_call(
        body,
        grid=(seq // ROW_TILE,),
        in_specs=[rows(HEAD_W), rows(HEAD_W),
                  pl.BlockSpec((HEAD_W, D_MODEL), lambda i: (0, 0)), pl.BlockSpec((HEAD_W, D_MODEL), lambda i: (1, 0)),
                  rows(D_MODEL), _row(D_MODEL), _row(D_MODEL), _row(D_MODEL), _row(D_MODEL)],
        out_specs=[rows(D_MODEL), rows(D_MODEL), rows(D_MODEL)],
        out_shape=[jax.ShapeDtypeStruct((seq, D_MODEL), F32), jax.ShapeDtypeStruct((seq, D_MODEL), BF16),
                   jax.ShapeDtypeStruct((seq, D_MODEL), BF16)],
        compiler_params=_params(("arbitrary",)),
        name="outproj_fwd",
    )(y_attn, y_delta, w_out, w_out, x, gate1, gain, scale, shift)


def _ffn_fwd(h2, w_gate, w_up, w_down, x1, gate2, final_gain, target):
    seq = h2.shape[0]
    n_rows, n_ff = seq // ROW_TILE, D_FF // FF_TILE

    def body(h_ref, wg_ref, wu_ref, wd_ref, x1_ref, g2_ref, gf_ref, t_ref, gate_ref, up_ref, dx2_ref, st_ref, acc):
        i, j = pl.program_id(0), pl.program_id(1)

        @pl.when((i == 0) & (j == 0))
        def _():
            st_ref[...] = jnp.zeros_like(st_ref)

        h = h_ref[...]
        gate = _nn(h, wg_ref[...])
        up = _nn(h, wu_ref[...])
        gate_ref[...] = gate.astype(BF16)
        up_ref[...] = up.astype(BF16)
        part = _nn((_silu(gate) * up).astype(BF16), wd_ref[...])

        @pl.when(j == 0)
        def _():
            acc[...] = part

        @pl.when(j > 0)
        def _():
            acc[...] += part

        @pl.when(j == n_ff - 1)
        def _():
            y2 = acc[...]
            x2 = x1_ref[...] + g2_ref[...] * y2
            xhat, rstd, out = _rms_fwd(x2, gf_ref[...])
            diff = out - t_ref[...]
            dx2, dgain = _rms_bwd(diff * (1.0 / D_MODEL), xhat, rstd, gf_ref[...])
            dx2_ref[...] = dx2
            st_ref[0:1, :] += dgain
            st_ref[1:2, :] += jnp.sum(dx2 * y2, axis=0, keepdims=True)
            st_ref[2:3, :] += jnp.sum(diff * diff, axis=0, keepdims=True) * (0.5 / D_MODEL)

        @pl.when((i == n_rows - 1) & (j == n_ff - 1))
        def _():
            st_ref[3:4, :] = jnp.broadcast_to(jnp.sum(st_ref[2:3, :], keepdims=True), (1, D_MODEL))

    def rows(width):
        return pl.BlockSpec((ROW_TILE, width), lambda i, j: (i, 0))

    ff = pl.BlockSpec((ROW_TILE, FF_TILE), lambda i, j: (i, j))
    return pl.pallas_call(
        body,
        grid=(n_rows, n_ff),
        in_specs=[rows(D_MODEL),
                  pl.BlockSpec((D_MODEL, FF_TILE), lambda i, j: (0, j)), pl.BlockSpec((D_MODEL, FF_TILE), lambda i, j: (0, j)),
                  pl.BlockSpec((FF_TILE, D_MODEL), lambda i, j: (j, 0)),
                  rows(D_MODEL), _row(D_MODEL), _row(D_MODEL), rows(D_MODEL)],
        out_specs=[ff, ff, rows(D_MODEL), pl.BlockSpec((8, D_MODEL), lambda i, j: (0, 0))],
        out_shape=[jax.ShapeDtypeStruct((seq, D_FF), BF16), jax.ShapeDtypeStruct((seq, D_FF), BF16),
                   jax.ShapeDtypeStruct((seq, D_MODEL), F32), jax.ShapeDtypeStruct((8, D_MODEL), F32)],
        scratch_shapes=[pltpu.VMEM((ROW_TILE, D_MODEL), F32)],
        compiler_params=_params(("arbitrary", "arbitrary")),
        name="ffn_fwd",
    )(h2, w_gate, w_up, w_down, x1, gate2, final_gain, target)


def _ffn_bwd(dx2, gate, up, w_gate, w_up, w_down, x1, y, gate2, gate1, gain, scale):
    seq = dx2.shape[0]
    n_rows, n_ff = seq // ROW_TILE, D_FF // FF_TILE_BWD

    def body(dx2_ref, gate_ref, up_ref, wg_ref, wu_ref, wd_ref, x1_ref, y_ref, g2_ref, g1_ref, g_ref, sc_ref,
             dgate_ref, dup_ref, act_ref, dy2_ref, dx1_ref, dy_ref, st_ref, acc):
        i, j = pl.program_id(0), pl.program_id(1)

        @pl.when((i == 0) & (j == 0))
        def _():
            st_ref[...] = jnp.zeros_like(st_ref)

        dy2 = (g2_ref[...] * dx2_ref[...]).astype(BF16)
        dy2_ref[...] = dy2
        gate = gate_ref[...].astype(F32)
        up = up_ref[...].astype(F32)
        dact = _nt(dy2, wd_ref[...])
        act_ref[...] = (_silu(gate) * up).astype(BF16)
        dgate = (dact * up * _dsilu(gate)).astype(BF16)
        dup = (dact * _silu(gate)).astype(BF16)
        dgate_ref[...] = dgate
        dup_ref[...] = dup
        part = _nt(dgate, wg_ref[...]) + _nt(dup, wu_ref[...])

        @pl.when(j == 0)
        def _():
            acc[...] = part

        @pl.when(j > 0)
        def _():
            acc[...] += part

        @pl.when(j == n_ff - 1)
        def _():
            dh = acc[...]
            xhat, rstd, norm = _rms_fwd(x1_ref[...], g_ref[...])
            dnorm = dh * (1.0 + sc_ref[...])
            dxn, dgain = _rms_bwd(dnorm, xhat, rstd, g_ref[...])
            dx1 = dx2_ref[...] + dxn
            dx1_ref[...] = dx1
            dy_ref[...] = (g1_ref[...] * dx1).astype(BF16)
            st_ref[0:1, :] += jnp.sum(dh, axis=0, keepdims=True)
            st_ref[1:2, :] += jnp.sum(dh * norm, axis=0, keepdims=True)
            st_ref[2:3, :] += dgain
            st_ref[3:4, :] += jnp.sum(dx1 * y_ref[...].astype(F32), axis=0, keepdims=True)

    def rows(width):
        return pl.BlockSpec((ROW_TILE, width), lambda i, j: (i, 0))

    ff = pl.BlockSpec((ROW_TILE, FF_TILE_BWD), lambda i, j: (i, j))
    w_cols = pl.BlockSpec((D_MODEL, FF_TILE_BWD), lambda i, j: (0, j))
    return pl.pallas_call(
        body,
        grid=(n_rows, n_ff),
        in_specs=[rows(D_MODEL), ff, ff, w_cols, w_cols,
                  pl.BlockSpec((FF_TILE_BWD, D_MODEL), lambda i, j: (j, 0)),
                  rows(D_MODEL), rows(D_MODEL), _row(D_MODEL), _row(D_MODEL), _row(D_MODEL), _row(D_MODEL)],
        out_specs=[ff, ff, ff, rows(D_MODEL), rows(D_MODEL), rows(D_MODEL), pl.BlockSpec((8, D_MODEL), lambda i, j: (0, 0))],
        out_shape=[jax.ShapeDtypeStruct((seq, D_FF), BF16)] * 3
        + [jax.ShapeDtypeStruct((seq, D_MODEL), BF16), jax.ShapeDtypeStruct((seq, D_MODEL), F32),
           jax.ShapeDtypeStruct((seq, D_MODEL), BF16), jax.ShapeDtypeStruct((8, D_MODEL), F32)],
        scratch_shapes=[pltpu.VMEM((ROW_TILE, D_MODEL), F32)],
        compiler_params=_params(("arbitrary", "arbitrary")),
        name="ffn_bwd",
    )(dx2, gate, up, w_gate, w_up, w_down, x1, y, gate2, gate1, gain, scale)


def _outproj_bwd(dy, w_out):
    seq = dy.shape[0]

    def body(dy_ref, w_ref, out_ref):
        out_ref[...] = _nt(dy_ref[...], w_ref[...])

    rows = pl.BlockSpec((ROW_TILE, D_MODEL), lambda i: (i, 0))
    return pl.pallas_call(
        body,
        grid=(seq // ROW_TILE,),
        in_specs=[rows, pl.BlockSpec((D_MODEL, D_MODEL), lambda i: (0, 0))],
        out_specs=rows,
        out_shape=jax.ShapeDtypeStruct((seq, D_MODEL), F32),
        compiler_params=_params(("arbitrary",)),
        name="outproj_bwd",
    )(dy, w_out)


def _inproj_bwd(dq, dk, dv, dxd, dz, dba, w_a, w_d, w_ba, x, dx1, gain, scale):
    seq = x.shape[0]

    def body(dq_ref, dk_ref, dv_ref, dxd_ref, dz_ref, dba_ref, wa_ref, wd_ref, wb_ref, x_ref, dx1_ref, g_ref, sc_ref,
             gx_ref, st_ref):
        @pl.when(pl.program_id(0) == 0)
        def _():
            st_ref[...] = jnp.zeros_like(st_ref)

        dh = (_nt(dq_ref[...].astype(BF16), wa_ref[:, 0:HEAD_W])
              + _nt(dk_ref[...].astype(BF16), wa_ref[:, HEAD_W:2 * HEAD_W])
              + _nt(dv_ref[...].astype(BF16), wa_ref[:, 2 * HEAD_W:])
              + _nt(dxd_ref[...].astype(BF16), wd_ref[:, 0:3 * HEAD_W])
              + _nt(dz_ref[...].astype(BF16), wd_ref[:, 3 * HEAD_W:])
              + _nt(dba_ref[...].astype(BF16), wb_ref[...]))
        xhat, rstd, norm = _rms_fwd(x_ref[...], g_ref[...])
        dxn, dgain = _rms_bwd(dh * (1.0 + sc_ref[...]), xhat, rstd, g_ref[...])
        gx_ref[...] = dx1_ref[...] + dxn
        st_ref[0:1, :] += jnp.sum(dh, axis=0, keepdims=True)
        st_ref[1:2, :] += jnp.sum(dh * norm, axis=0, keepdims=True)
        st_ref[2:3, :] += dgain

    def rows(width):
        return pl.BlockSpec((ROW_TILE, width), lambda i: (i, 0))

    def whole(a):
        return pl.BlockSpec(a.shape, lambda i: (0, 0))

    return pl.pallas_call(
        body,
        grid=(seq // ROW_TILE,),
        in_specs=[rows(HEAD_W), rows(HEAD_W), rows(HEAD_W), rows(3 * HEAD_W), rows(HEAD_W), rows(LANES),
                  whole(w_a), whole(w_d), whole(w_ba), rows(D_MODEL), rows(D_MODEL), _row(D_MODEL), _row(D_MODEL)],
        out_specs=[rows(D_MODEL), pl.BlockSpec((8, D_MODEL), lambda i: (0, 0))],
        out_shape=[jax.ShapeDtypeStruct((seq, D_MODEL), F32), jax.ShapeDtypeStruct((8, D_MODEL), F32)],
        compiler_params=_params(("arbitrary",)),
        name="inproj_bwd",
    )(dq, dk, dv, dxd, dz, dba, w_a, w_d, w_ba, x, dx1, gain, scale)


def _weight_grad(a, b, name):
    seq, m = a.shape
    n = b.shape[1]
    tm = m if m <= 1536 else m // 2
    tn = n if n <= 1536 else n // 2
    n_k = seq // ROW_TILE

    def body(a_ref, b_ref, out_ref):
        part = _tn(a_ref[...].astype(BF16), b_ref[...].astype(BF16))

        @pl.when(pl.program_id(2) == 0)
        def _():
            out_ref[...] = part

        @pl.when(pl.program_id(2) > 0)
        def _():
            out_ref[...] += part

    return pl.pallas_call(
        body,
        grid=(m // tm, n // tn, n_k),
        in_specs=[pl.BlockSpec((ROW_TILE, tm), lambda i, j, k: (k, i)),
                  pl.BlockSpec((ROW_TILE, tn), lambda i, j, k: (k, j))],
        out_specs=pl.BlockSpec((tm, tn), lambda i, j, k: (i, j)),
        out_shape=jax.ShapeDtypeStruct((m, n), F32),
        compiler_params=_params(("arbitrary", "arbitrary", "arbitrary")),
        name=name,
    )(a, b)


def _adamw(w, g, m, v, name):
    n_rows, n_cols = w.shape
    tr = 256 if n_rows % 256 == 0 else n_rows

    def body(w_ref, g_ref, m_ref, v_ref, d_ref, nm_ref, nv_ref):
        gv = g_ref[...]
        nm = ADAM_B1 * m_ref[...] + (1.0 - ADAM_B1) * gv
        nv = ADAM_B2 * v_ref[...] + (1.0 - ADAM_B2) * (gv * gv)
        m_hat = nm / (1.0 - ADAM_B1 ** ADAM_STEP)
        v_hat = nv / (1.0 - ADAM_B2 ** ADAM_STEP)
        d_ref[...] = -ADAM_LR * (m_hat / (jnp.sqrt(v_hat) + ADAM_EPS) + ADAM_WD * w_ref[...])
        nm_ref[...] = nm
        nv_ref[...] = nv

    blk = pl.BlockSpec((tr, n_cols), lambda i: (i, 0))
    shape = jax.ShapeDtypeStruct((n_rows, n_cols), F32)
    return pl.pallas_call(
        body,
        grid=(n_rows // tr,),
        in_specs=[blk] * 4,
        out_specs=[blk] * 3,
        out_shape=[shape] * 3,
        compiler_params=_params(("arbitrary",)),
        name=name,
    )(w, g, m, v)


IN_WIDTH = 3600
BA_COL = 7 * HEAD_W


def _local_step(x, target, mod, norm_attn_g, w_in, rel_bias, conv_w, a_log, dt_bias, delta_norm_g, w_out,
                norm_ffn_g, w_gate, w_up, w_down, final_norm_g):
    sh1, sc1, g1, sh2, sc2, g2 = [mod[:, i * D_MODEL:(i + 1) * D_MODEL] for i in range(6)]
    w_a = w_in[:, :3 * HEAD_W]
    w_d = w_in[:, 3 * HEAD_W:BA_COL]
    w_ba = jnp.pad(w_in[:, BA_COL:], ((0, 0), (0, LANES - 2 * N_HEADS)))
    tables = jnp.asarray(_attn_tables())
    alog_row = jnp.pad(a_log, ((0, 0), (N_HEADS, LANES - 2 * N_HEADS)))
    dt_row = jnp.pad(dt_bias, ((0, 0), (N_HEADS, LANES - 2 * N_HEADS)))
    gain_row = jnp.tile(delta_norm_g, (1, N_HEADS))

    h1, qkv_a, qkvz, ba = _inproj_fwd(x, norm_attn_g, sc1, sh1, w_a, w_d, w_ba)
    y_attn, lse = _attention_fwd(qkv_a, rel_bias, tables)
    xh = _to_heads(_delta_prep_fwd(qkvz, ba, conv_w, alog_row, dt_row))
    o_h, inv_h, st_h = _delta_scan_fwd(xh)
    o = _from_heads(o_h)
    y_delta = _delta_post_fwd(o, qkvz, gain_row)
    x1, h2, y = _outproj_fwd(y_attn, y_delta, w_out, x, g1, norm_ffn_g, sc2, sh2)
    gate, up, dx2, st_f = _ffn_fwd(h2, w_gate, w_up, w_down, x1, g2, final_norm_g, target)

    dgate, dup, act, dy2, dx1, dy, st_b = _ffn_bwd(dx2, gate, up, w_gate, w_up, w_down, x1, y, g2, g1, norm_ffn_g, sc2)
    grads = {
        "w_gate": _weight_grad(h2, dgate, "wgrad_gate"),
        "w_up": _weight_grad(h2, dup, "wgrad_up"),
        "w_down": _weight_grad(act, dy2, "wgrad_down"),
        "w_out": jnp.concatenate([_weight_grad(y_attn, dy, "wgrad_out_attn"),
                                  _weight_grad(y_delta, dy, "wgrad_out_delta")], axis=0),
    }
    dycat = _outproj_bwd(dy, w_out)
    do, dz, dgain = _delta_post_bwd(dycat, o, qkvz, gain_row)
    dxs = _from_heads(_delta_scan_bwd(xh, inv_h, st_h, _to_heads(do)))
    dconv, dba, dvec = _delta_prep_bwd(qkvz, ba, conv_w, alog_row, dt_row, dxs)
    dxd, grads["conv_w"] = _conv_bwd(dconv, qkvz, conv_w)
    dq, dk, dv, dbias = _attention_bwd(qkv_a, dycat, y_attn, lse, rel_bias, tables)
    grad_x, st_i = _inproj_bwd(dq, dk, dv, dxd, dz, dba, w_a, w_d, w_ba, x, dx1, norm_attn_g, sc1)
    grads["w_in"] = jnp.concatenate(
        [_weight_grad(h1, dq, "wgrad_in_q"), _weight_grad(h1, dk, "wgrad_in_k"), _weight_grad(h1, dv, "wgrad_in_v"),
         _weight_grad(h1, dxd, "wgrad_in_delta"), _weight_grad(h1, dz, "wgrad_in_z"),
         _weight_grad(h1, dba, "wgrad_in_gates")[:, :2 * N_HEADS]], axis=1)
    grads["rel_bias"] = _rel_bias_grad(dbias, tables)[:, :N_BUCKETS].T
    grads["a_log"] = dvec[0:1, N_HEADS:2 * N_HEADS]
    grads["dt_bias"] = dvec[1:2, N_HEADS:2 * N_HEADS]
    grads["delta_norm_g"] = dgain[1:2, :HEAD_DIM]
    grads["norm_attn_g"] = st_i[2:3]
    grads["norm_ffn_g"] = st_b[2:3]
    grads["final_norm_g"] = st_f[0:1]
    dmod = jnp.concatenate([st_i[0:1], st_i[1:2], st_b[3:4], st_b[0:1], st_b[1:2], st_f[1:2]], axis=1)
    return st_f[3, 0], grad_x, grads, dmod


MESH = pl.DeviceIdType.MESH
OTHER_CHIPS = ((1, 0), (0, 1), (1, 1))
ALL_PEERS = tuple((m >> 2 & 1, m >> 1 & 1, m & 1) for m in range(1, 8))
ANY = pl.BlockSpec(memory_space=pl.ANY)
VMEM_SPEC = pl.BlockSpec(memory_space=pltpu.VMEM)
N_BIG = 5


def _me():
    return lax.axis_index("x"), lax.axis_index("y"), lax.axis_index("c")


def _flip(pos, mask):
    return tuple(1 - p if m else p for p, m in zip(pos, mask))


def _remote(src, dst, send_sems, recv_sems, k, to):
    return pltpu.make_async_remote_copy(src_ref=src, dst_ref=dst, send_sem=send_sems.at[k], recv_sem=recv_sems.at[k],
                                        device_id=to, device_id_type=MESH)


def _ada_exchange(c8, w_ada, b_ada, conv8):
    def body(c_ref, w_ref, b_ref, cv_ref, mod_ref, cact_ref, conv_ref, c_all, part_all, send_sems, recv_sems):
        x, y, c = me = _me()
        dev = 4 * x + 2 * y + c
        chip = 2 * x + y
        c_all[dev] = c_ref[...]
        conv_ref[chip] = cv_ref[...]
        first = [_remote(c_ref, c_all.at[dev], send_sems, recv_sems, k, _flip(me, mask))
                 for k, mask in enumerate(ALL_PEERS)]
        first += [_remote(cv_ref, conv_ref.at[chip], send_sems, recv_sems, 7 + j, _flip(me, (*mask, 0)))
                  for j, mask in enumerate(OTHER_CHIPS)]
        for cp in first:
            cp.start()
        for cp in first:
            cp.wait()
        row = lax.broadcasted_iota(jnp.int32, (8, D_MODEL), 0)
        c_rows = jnp.zeros((8, D_MODEL), F32)
        for d in range(8):
            c_rows = jnp.where(row == d, c_all[d], c_rows)
        c_act = _silu(c_rows)
        cact_ref[...] = c_act
        part_all[chip] = _nn(c_act, w_ref[...], HIGHEST)
        second = [_remote(part_all.at[chip], part_all.at[chip], send_sems, recv_sems, 10 + j, _flip(me, (*mask, 0)))
                  for j, mask in enumerate(OTHER_CHIPS)]
        for cp in second:
            cp.start()
        for cp in second:
            cp.wait()
        cols = w_ref.shape[1]
        for k in range(4):
            mod_ref[:, k * cols:(k + 1) * cols] = part_all[k] + b_ref[:, k * cols:(k + 1) * cols]

    cols = w_ada.shape[1]
    return pl.pallas_call(
        body,
        in_specs=[VMEM_SPEC] * 4,
        out_specs=[VMEM_SPEC] * 3,
        out_shape=[jax.ShapeDtypeStruct((8, 4 * cols), F32), jax.ShapeDtypeStruct((8, D_MODEL), F32),
                   jax.ShapeDtypeStruct((4, 8, conv8.shape[1]), F32)],
        scratch_shapes=[pltpu.VMEM((8, 8, D_MODEL), F32), pltpu.VMEM((4, 8, cols), F32),
                        pltpu.SemaphoreType.DMA((13,)), pltpu.SemaphoreType.DMA((13,))],
        compiler_params=pltpu.CompilerParams(vmem_limit_bytes=VMEM_LIMIT),
        name="ada_exchange",
    )(c8, w_ada, b_ada, conv8)


def _gather_weights(shards):
    def body(*refs):
        srcs, dsts = refs[:N_BIG], refs[N_BIG:2 * N_BIG]
        send_sems, recv_sems, local_sems = refs[2 * N_BIG:]
        x, y, c = me = _me()
        chip = 2 * x + y
        copies = []
        for a in range(N_BIG):
            copies.append(pltpu.make_async_copy(srcs[a], dsts[a].at[chip], local_sems.at[a]))
            copies += [_remote(srcs[a], dsts[a].at[chip], send_sems, recv_sems, 3 * a + j, _flip(me, (*mask, 0)))
                       for j, mask in enumerate(OTHER_CHIPS)]
        for cp in copies:
            cp.start()
        for cp in copies:
            cp.wait()

    return pl.pallas_call(
        body,
        in_specs=[ANY] * N_BIG,
        out_specs=[ANY] * N_BIG,
        out_shape=[jax.ShapeDtypeStruct((4, *s.shape), s.dtype) for s in shards],
        scratch_shapes=[pltpu.SemaphoreType.DMA((3 * N_BIG,)), pltpu.SemaphoreType.DMA((3 * N_BIG,)),
                        pltpu.SemaphoreType.DMA((N_BIG,))],
        name="gather_weights",
    )(*shards)


def _swap_halves(grads):
    def body(*refs):
        srcs, own, got = refs[:N_BIG], refs[N_BIG:2 * N_BIG], refs[2 * N_BIG:3 * N_BIG]
        send_sems, recv_sems, local_sems = refs[3 * N_BIG:]
        x, y, c = me = _me()
        copies = []
        for a in range(N_BIG):
            copies.append(pltpu.make_async_copy(srcs[a].at[:, c], own[a], local_sems.at[a]))
            copies.append(_remote(srcs[a].at[:, 1 - c], got[a], send_sems, recv_sems, a, _flip(me, (0, 0, 1))))
        for cp in copies:
            cp.start()
        for cp in copies:
            cp.wait()

    halves = [jax.ShapeDtypeStruct((4, g.shape[2], g.shape[3]), g.dtype) for g in grads]
    return pl.pallas_call(
        body,
        in_specs=[ANY] * N_BIG,
        out_specs=[ANY] * (2 * N_BIG),
        out_shape=halves + halves,
        scratch_shapes=[pltpu.SemaphoreType.DMA((N_BIG,)), pltpu.SemaphoreType.DMA((N_BIG,)),
                        pltpu.SemaphoreType.DMA((N_BIG,))],
        name="swap_halves",
    )(*grads)


def _scatter_partials(partials):
    def body(*refs):
        srcs, dsts = refs[:N_BIG], refs[N_BIG:2 * N_BIG]
        send_sems, recv_sems, local_sems = refs[2 * N_BIG:]
        x, y, c = me = _me()
        chip = 2 * x + y
        copies = []
        for a in range(N_BIG):
            copies.append(pltpu.make_async_copy(srcs[a].at[chip], dsts[a].at[chip], local_sems.at[a]))
            for j, mask in enumerate(OTHER_CHIPS):
                to = _flip(me, (*mask, 0))
                copies.append(_remote(srcs[a].at[2 * to[0] + to[1]], dsts[a].at[chip], send_sems, recv_sems, 3 * a + j, to))
        for cp in copies:
            cp.start()
        for cp in copies:
            cp.wait()

    return pl.pallas_call(
        body,
        in_specs=[ANY] * N_BIG,
        out_specs=[ANY] * N_BIG,
        out_shape=[jax.ShapeDtypeStruct(p.shape, p.dtype) for p in partials],
        scratch_shapes=[pltpu.SemaphoreType.DMA((3 * N_BIG,)), pltpu.SemaphoreType.DMA((3 * N_BIG,)),
                        pltpu.SemaphoreType.DMA((N_BIG,))],
        name="scatter_partials",
    )(*partials)


def _join_halves(halves):
    def body(*refs):
        srcs, dsts = refs[:N_BIG], refs[N_BIG:2 * N_BIG]
        send_sems, recv_sems, local_sems = refs[2 * N_BIG:]
        x, y, c = me = _me()
        copies = []
        for a in range(N_BIG):
            copies.append(pltpu.make_async_copy(srcs[a], dsts[a].at[c], local_sems.at[a]))
            copies.append(_remote(srcs[a], dsts[a].at[c], send_sems, recv_sems, a, _flip(me, (0, 0, 1))))
        for cp in copies:
            cp.start()
        for cp in copies:
            cp.wait()

    return pl.pallas_call(
        body,
        in_specs=[ANY] * N_BIG,
        out_specs=[ANY] * N_BIG,
        out_shape=[jax.ShapeDtypeStruct((2, *h.shape), h.dtype) for h in halves],
        scratch_shapes=[pltpu.SemaphoreType.DMA((N_BIG,)), pltpu.SemaphoreType.DMA((N_BIG,)),
                        pltpu.SemaphoreType.DMA((N_BIG,))],
        name="join_halves",
    )(*halves)


def _gather_small(packed):
    n_rows = packed.shape[0]

    def body(p_ref, all_ref, sum_ref, send_sems, recv_sems):
        x, y, c = me = _me()
        dev = 4 * x + 2 * y + c
        all_ref[dev] = p_ref[...]
        copies = [_remote(p_ref, all_ref.at[dev], send_sems, recv_sems, k, _flip(me, mask))
                  for k, mask in enumerate(ALL_PEERS)]
        for cp in copies:
            cp.start()
        for cp in copies:
            cp.wait()
        total = all_ref[0]
        for d in range(1, 8):
            total = total + all_ref[d]
        sum_ref[...] = total

    return pl.pallas_call(
        body,
        in_specs=[VMEM_SPEC],
        out_specs=[VMEM_SPEC, VMEM_SPEC],
        out_shape=[jax.ShapeDtypeStruct((8, n_rows, LANES), F32), jax.ShapeDtypeStruct((n_rows, LANES), F32)],
        scratch_shapes=[pltpu.SemaphoreType.DMA((7,)), pltpu.SemaphoreType.DMA((7,))],
        name="gather_small",
    )(packed)


def _add_pair(a, b, out_dtype, name):
    n_rows, n_cols = a.shape
    tr = 128

    def body(a_ref, b_ref, o_ref):
        o_ref[...] = (a_ref[...] + b_ref[...]).astype(o_ref.dtype)

    blk = pl.BlockSpec((tr, n_cols), lambda i: (i, 0))
    return pl.pallas_call(
        body, grid=(n_rows // tr,), in_specs=[blk, blk], out_specs=blk,
        out_shape=jax.ShapeDtypeStruct((n_rows, n_cols), out_dtype),
        compiler_params=_params(("arbitrary",)), name=name,
    )(a, b)


def _add_slots(a, name):
    _, n_rows, n_cols = a.shape
    tr = 64 if n_rows % 64 == 0 else 32

    def body(a_ref, o_ref):
        total = a_ref[0].astype(F32)
        for k in range(1, 4):
            total = total + a_ref[k].astype(F32)
        o_ref[...] = total

    return pl.pallas_call(
        body, grid=(n_rows // tr,),
        in_specs=[pl.BlockSpec((4, tr, n_cols), lambda i: (0, i, 0))],
        out_specs=pl.BlockSpec((tr, n_cols), lambda i: (i, 0)),
        out_shape=jax.ShapeDtypeStruct((n_rows, n_cols), F32),
        compiler_params=_params(("arbitrary",)), name=name,
    )(a)


def _ada_weight_grad(c_act, dmod_cols):
    def body(c_ref, d_ref, o_ref):
        o_ref[...] = _tn(c_ref[...], d_ref[...], HIGHEST)

    return pl.pallas_call(
        body, in_specs=[VMEM_SPEC, VMEM_SPEC], out_specs=VMEM_SPEC,
        out_shape=jax.ShapeDtypeStruct((c_act.shape[1], dmod_cols.shape[1]), F32),
        compiler_params=pltpu.CompilerParams(vmem_limit_bytes=VMEM_LIMIT), name="ada_weight_grad",
    )(c_act, dmod_cols)


def kernel(x, c, w_ada, b_ada, norm_attn_g, w_in, rel_bias, conv_w, a_log, dt_bias, delta_norm_g, w_out, norm_ffn_g, w_gate, w_up, w_down, final_norm_g, loss_target, m_w_ada, m_b_ada, m_norm_attn_g, m_w_in, m_rel_bias, m_conv_w, m_a_log, m_dt_bias, m_delta_norm_g, m_w_out, m_norm_ffn_g, m_w_gate, m_w_up, m_w_down, m_final_norm_g, v_w_ada, v_b_ada, v_norm_attn_g, v_w_in, v_rel_bias, v_conv_w, v_a_log, v_dt_bias, v_delta_norm_g, v_w_out, v_norm_ffn_g, v_w_gate, v_w_up, v_w_down, v_final_norm_g):
    xi, yi, ci = _me()
    dev = 4 * xi + 2 * yi + ci
    chip = 2 * xi + yi

    conv_cols = conv_w.shape[2]
    mod_all, c_act, conv_all = _ada_exchange(jnp.broadcast_to(c, (8, D_MODEL)), w_ada[0], b_ada,
                                             jnp.pad(conv_w[0], ((0, 4), (0, 0))))
    mod = lax.dynamic_slice_in_dim(mod_all, dev, 1, axis=0)
    conv_full = jnp.swapaxes(conv_all[:, :4, :], 0, 1).reshape(4, 4 * conv_cols)

    big = [w_in[0], w_out[0], w_gate[0], w_up[0], w_down[0]]
    by_cols = [True, False, True, True, False]
    gathered = _gather_weights([w.astype(BF16) for w in big])
    whole = [jnp.swapaxes(g, 0, 1).reshape(g.shape[1], 4 * g.shape[2]) if cols else g.reshape(4 * g.shape[1], g.shape[2])
             for g, cols in zip(gathered, by_cols)]

    loss, grad_x, grads, dmod = _local_step(
        x[0], loss_target[0], mod, norm_attn_g, whole[0], rel_bias, conv_full, a_log, dt_bias, delta_norm_g,
        whole[1], norm_ffn_g, whole[2], whole[3], whole[4], final_norm_g[None])

    slots = []
    for name, w, cols in zip(("w_in", "w_out", "w_gate", "w_up", "w_down"), big, by_cols):
        rows, ncol = w.shape
        g = grads[name]
        g = jnp.swapaxes(g.reshape(rows, 4, ncol), 0, 1) if cols else g.reshape(4, rows, ncol)
        slots.append(g.reshape(4, 2, rows // 2, ncol))
    swapped = _swap_halves(slots)
    partials = []
    for a, (own, got) in enumerate(zip(swapped[:N_BIG], swapped[N_BIG:])):
        _, half, ncol = own.shape
        partials.append(_add_pair(own.reshape(4 * half, ncol), got.reshape(4 * half, ncol), BF16,
                                  f"add_pair_{a}").reshape(4, half, ncol))
    by_source = _scatter_partials(partials)
    joined = _join_halves([_add_slots(p, f"add_slots_{a}") for a, p in enumerate(by_source)])
    big_grads = [j.reshape(w.shape) for j, w in zip(joined, big)]

    pieces = [dmod, grads["conv_w"], grads["norm_attn_g"], grads["norm_ffn_g"], grads["final_norm_g"],
              grads["rel_bias"], grads["a_log"], grads["dt_bias"], grads["delta_norm_g"]]
    flat = [jnp.pad(p.reshape(-1), (0, -p.size % LANES)) for p in pieces]
    n_rows = [f.size // LANES for f in flat]
    packed = jnp.concatenate(flat).reshape(-1, LANES)
    packed = jnp.pad(packed, ((0, -packed.shape[0] % 8), (0, 0)))
    all_small, total = _gather_small(packed)
    sums, start = [], 0
    for p, n in zip(pieces, n_rows):
        sums.append(total[start:start + n].reshape(-1)[:p.size].reshape(p.shape))
        start += n
    g_b_ada, g_conv, g_norm_attn, g_norm_ffn, g_final, g_rel, g_alog, g_dt, g_dnorm = sums
    dmod_all = all_small[:, :n_rows[0], :].reshape(8, -1)
    ada_cols = w_ada.shape[2]
    g_w_ada = _ada_weight_grad(c_act, lax.dynamic_slice_in_dim(dmod_all, chip * ada_cols, ada_cols, axis=1))
    g_conv = lax.dynamic_slice_in_dim(g_conv, chip * conv_cols, conv_cols, axis=1)

    grad = {"w_ada": g_w_ada[None], "b_ada": g_b_ada, "norm_attn_g": g_norm_attn, "w_in": big_grads[0][None],
            "rel_bias": g_rel, "conv_w": g_conv[None], "a_log": g_alog, "dt_bias": g_dt, "delta_norm_g": g_dnorm,
            "w_out": big_grads[1][None], "norm_ffn_g": g_norm_ffn, "w_gate": big_grads[2][None],
            "w_up": big_grads[3][None], "w_down": big_grads[4][None], "final_norm_g": g_final.reshape(-1)}
    weight = {"w_ada": w_ada, "b_ada": b_ada, "norm_attn_g": norm_attn_g, "w_in": w_in, "rel_bias": rel_bias,
              "conv_w": conv_w, "a_log": a_log, "dt_bias": dt_bias, "delta_norm_g": delta_norm_g, "w_out": w_out,
              "norm_ffn_g": norm_ffn_g, "w_gate": w_gate, "w_up": w_up, "w_down": w_down, "final_norm_g": final_norm_g}
    first = {"w_ada": m_w_ada, "b_ada": m_b_ada, "norm_attn_g": m_norm_attn_g, "w_in": m_w_in, "rel_bias": m_rel_bias,
             "conv_w": m_conv_w, "a_log": m_a_log, "dt_bias": m_dt_bias, "delta_norm_g": m_delta_norm_g,
             "w_out": m_w_out, "norm_ffn_g": m_norm_ffn_g, "w_gate": m_w_gate, "w_up": m_w_up, "w_down": m_w_down,
             "final_norm_g": m_final_norm_g}
    second = {"w_ada": v_w_ada, "b_ada": v_b_ada, "norm_attn_g": v_norm_attn_g, "w_in": v_w_in, "rel_bias": v_rel_bias,
              "conv_w": v_conv_w, "a_log": v_a_log, "dt_bias": v_dt_bias, "delta_norm_g": v_delta_norm_g,
              "w_out": v_w_out, "norm_ffn_g": v_norm_ffn_g, "w_gate": v_w_gate, "w_up": v_w_up, "w_down": v_w_down,
              "final_norm_g": v_final_norm_g}
    delta, new_m, new_v = {}, {}, {}
    for name, w in weight.items():
        two_d = (-1, w.shape[-1])
        d, nm, nv = _adamw(w.reshape(two_d), grad[name].reshape(two_d), first[name].reshape(two_d),
                           second[name].reshape(two_d), f"adamw_{name}")
        delta[name], new_m[name], new_v[name] = d.reshape(w.shape), nm.reshape(w.shape), nv.reshape(w.shape)

    names = list(weight)
    return (lax.psum(loss, ("x", "y", "c")), grad_x[None], *[grad[n] for n in names], *[delta[n] for n in names],
            *[new_m[n] for n in names], *[new_v[n] for n in names])
```

```python
import functools
import math

import numpy as np
import jax
import jax.numpy as jnp
from jax import lax
from jax.experimental import pallas as pl
from jax.experimental.pallas import tpu as pltpu

F32 = jnp.float32
BF16 = jnp.bfloat16
HIGHEST = lax.Precision.HIGHEST

D_MODEL = 1024
HEAD_DIM = 64
N_HEADS = 8
HEAD_W = 512
BRANCHES = ((128, 1), (512, 4), (2048, 16))
BAND = 128
ATT_TILE = 2048
ATT_UNROLL = 2
N_BUCKETS = 32
MAX_DISTANCE = 2048
CHUNK = 64
D_FF = 2816
EPS = 1e-6
NEG_INF = -1e30
LANES = 128
VMEM_LIMIT = 56 * 1024 * 1024

ADAM_LR = 0.001
ADAM_B1 = 0.9
ADAM_B2 = 0.999
ADAM_EPS = 1e-08
ADAM_WD = 0.01
ADAM_STEP = 10


def _nn(a, b, precision=None):
    return jnp.dot(a, b, preferred_element_type=F32, precision=precision)


def _nt(a, b, precision=None):
    return lax.dot_general(a, b, (((1,), (1,)), ((), ())), preferred_element_type=F32, precision=precision)


def _tn(a, b, precision=None):
    return lax.dot_general(a, b, (((0,), (0,)), ((), ())), preferred_element_type=F32, precision=precision)


def _params(sem, vmem=VMEM_LIMIT):
    return pltpu.CompilerParams(dimension_semantics=sem, vmem_limit_bytes=vmem)


def _sigmoid(x):
    return 1.0 / (1.0 + jnp.exp(-x))


def _silu(x):
    return x * _sigmoid(x)


def _dsilu(x):
    s = _sigmoid(x)
    return s * (1.0 + x * (1.0 - s))


def _attn_tables():
    qi = np.arange(BAND)[:, None]
    kj = np.arange(2 * BAND)[None, :]
    steps = qi + BAND - kj
    in_window = (steps >= 0) & (steps <= BAND)
    max_exact = N_BUCKETS // 2
    out = np.zeros((3, 2, BAND, 2 * BAND), np.int32)
    for b, (_, dil) in enumerate(BRANCHES):
        dist = np.maximum(steps, 0) * dil
        dist_f = np.maximum(dist, 1).astype(np.float32)
        large = max_exact + (np.log(dist_f / np.float32(max_exact)) / np.float32(math.log(MAX_DISTANCE / max_exact))
                             * np.float32(N_BUCKETS - max_exact)).astype(np.int32)
        bucket = np.where(dist < max_exact, dist, np.minimum(large, N_BUCKETS - 1)).astype(np.int32)
        out[b, 0] = np.where(in_window, bucket, -1)
        out[b, 1] = np.where(in_window & (kj >= BAND), bucket, -1)
    return out


def _attn_bias_tables(rel_ref, tab_ref, bias_s, pair):
    for b in range(3):
        for first in range(2):
            tab = tab_ref[b, first]
            for hh in range(2):
                head = 2 * pair + hh

                def pick(kk, acc, tab=tab, head=head):
                    return jnp.where(tab == kk, rel_ref[kk, head], acc)

                acc = lax.fori_loop(0, N_BUCKETS, pick, jnp.zeros((BAND, 2 * BAND), F32))
                bias_s[b, hh, first] = jnp.where(tab < 0, NEG_INF, acc)


def _attn_block_index(idx, t, r):
    nb = ATT_TILE // (BAND * r)
    rho = idx // nb
    n = idx % nb
    qs = rho + r * BAND * n
    gs = t * ATT_TILE + qs
    first = (t * nb + n) == 0
    ps = jnp.where(first, gs, gs - r * BAND)
    return qs, gs, ps, first.astype(jnp.int32)


def _rows(start, r):
    return pl.ds(start, BAND) if r == 1 else pl.ds(start, BAND, stride=r)


def _attention_fwd(qkv, rel_bias, tables):
    seq = qkv.shape[0]
    n_tiles = seq // ATT_TILE

    def body(rel_ref, tab_ref, q_ref, k_ref, v_ref, y_ref, lse_ref, bias_s, o_s, l_s):
        pair = pl.program_id(0)
        t = pl.program_id(1)
        lane = lax.broadcasted_iota(jnp.int32, (1, LANES), 1)
        head0 = lane < HEAD_DIM

        @pl.when(t == 0)
        def _():
            _attn_bias_tables(rel_ref, tab_ref, bias_s, pair)

        masks = (head0, jnp.logical_not(head0))
        for b, (_, r) in enumerate(BRANCHES):
            def blocks(it, carry, b=b, r=r):
                idx = [_attn_block_index(it * ATT_UNROLL + j, t, r) for j in range(ATT_UNROLL)]
                qb = [q_ref[_rows(qs, r), :] * (HEAD_DIM ** -0.5) for qs, _, _, _ in idx]
                kcat = [jnp.concatenate([k_ref[_rows(ps, r), :], k_ref[_rows(gs, r), :]], axis=0).astype(BF16)
                        for _, gs, ps, _ in idx]
                vcat = [jnp.concatenate([v_ref[_rows(ps, r), :], v_ref[_rows(gs, r), :]], axis=0).astype(BF16)
                        for _, gs, ps, _ in idx]
                work = [(j, hh) for j in range(ATT_UNROLL) for hh in range(2)]
                s = [_nt(jnp.where(masks[hh], qb[j], 0.0).astype(BF16), kcat[j]) + bias_s[b, hh, idx[j][3]]
                     for j, hh in work]
                m = [jnp.max(sv, axis=-1, keepdims=True) for sv in s]
                e = [jnp.exp(sv - mv) for sv, mv in zip(s, m)]
                den = [jnp.sum(ev, axis=-1, keepdims=True) for ev in e]
                out = [_nn(ev.astype(BF16), vcat[j]) / dv for ev, dv, (j, _) in zip(e, den, work)]
                lse = [mv + jnp.log(dv) for mv, dv in zip(m, den)]
                for j in range(ATT_UNROLL):
                    o_s[b, _rows(idx[j][0], r), :] = jnp.where(head0, out[2 * j], out[2 * j + 1])
                    l_s[b, _rows(idx[j][0], r), :] = jnp.where(head0, lse[2 * j], lse[2 * j + 1])
                return carry

            lax.fori_loop(0, ATT_TILE // BAND // ATT_UNROLL, blocks, 0)

        def merge(i, carry):
            rows = pl.ds(pl.multiple_of(i * BAND, BAND), BAND)
            l0, l1, l2 = l_s[0, rows, :], l_s[1, rows, :], l_s[2, rows, :]
            m = jnp.maximum(jnp.maximum(l0, l1), l2)
            w0, w1, w2 = jnp.exp(l0 - m), jnp.exp(l1 - m), jnp.exp(l2 - m)
            tot = w0 + w1 + w2
            y_ref[rows, :] = (w0 * o_s[0, rows, :] + w1 * o_s[1, rows, :] + w2 * o_s[2, rows, :]) / tot
            lse_ref[rows, :] = m + jnp.log(tot)
            return carry

        lax.fori_loop(0, ATT_TILE // BAND, merge, 0)

    tile = pl.BlockSpec((ATT_TILE, LANES), lambda p, t: (t, p))
    return pl.pallas_call(
        body,
        grid=(N_HEADS // 2, n_tiles),
        in_specs=[
            pl.BlockSpec(memory_space=pltpu.SMEM),
            pl.BlockSpec((3, 2, BAND, 2 * BAND), lambda p, t: (0, 0, 0, 0)),
            pl.BlockSpec((ATT_TILE, LANES), lambda p, t: (t, p)),
            pl.BlockSpec((seq, LANES), lambda p, t: (0, 4 + p)),
            pl.BlockSpec((seq, LANES), lambda p, t: (0, 8 + p)),
        ],
        out_specs=[tile, tile],
        out_shape=[jax.ShapeDtypeStruct((seq, HEAD_W), F32), jax.ShapeDtypeStruct((seq, HEAD_W), F32)],
        scratch_shapes=[
            pltpu.VMEM((3, 2, 2, BAND, 2 * BAND), F32),
            pltpu.VMEM((3, ATT_TILE, LANES), F32),
            pltpu.VMEM((3, ATT_TILE, LANES), F32),
        ],
        compiler_params=_params(("arbitrary", "arbitrary")),
        name="attn_fwd",
    )(rel_bias, tables, qkv, qkv, qkv)


def _attention_bwd(qkv, dy, y, lse, rel_bias, tables):
    seq = qkv.shape[0]
    n_tiles = seq // ATT_TILE

    def body(rel_ref, tab_ref, q_ref, k_ref, v_ref, dy_ref, y_ref, lse_ref,
             dq_ref, dk_ref, dv_ref, dbias_ref, bias_s):
        pair = pl.program_id(0)
        t = pl.program_id(1)
        lane = lax.broadcasted_iota(jnp.int32, (1, LANES), 1)
        head0 = lane < HEAD_DIM

        @pl.when(t == 0)
        def _():
            _attn_bias_tables(rel_ref, tab_ref, bias_s, pair)
            dk_ref[...] = jnp.zeros_like(dk_ref)
            dv_ref[...] = jnp.zeros_like(dv_ref)
            dbias_ref[...] = jnp.zeros_like(dbias_ref)

        dq_ref[...] = jnp.zeros_like(dq_ref)

        masks = (head0, jnp.logical_not(head0))
        scale = HEAD_DIM ** -0.5
        for b, (_, r) in enumerate(BRANCHES):
            def blocks(it, carry, b=b, r=r):
                idx = [_attn_block_index(it * ATT_UNROLL + j, t, r) for j in range(ATT_UNROLL)]
                qb = [q_ref[_rows(qs, r), :] * scale for qs, _, _, _ in idx]
                kcat = [jnp.concatenate([k_ref[_rows(ps, r), :], k_ref[_rows(gs, r), :]], axis=0).astype(BF16)
                        for _, gs, ps, _ in idx]
                vcat = [jnp.concatenate([v_ref[_rows(ps, r), :], v_ref[_rows(gs, r), :]], axis=0).astype(BF16)
                        for _, gs, ps, _ in idx]
                dob = [dy_ref[_rows(qs, r), :] for qs, _, _, _ in idx]
                ob = [y_ref[_rows(qs, r), :] for qs, _, _, _ in idx]
                lb = [lse_ref[_rows(qs, r), :] for qs, _, _, _ in idx]
                work = [(j, hh) for j in range(ATT_UNROLL) for hh in range(2)]
                qh = [jnp.where(masks[hh], qb[j], 0.0).astype(BF16) for j, hh in work]
                doh = [jnp.where(masks[hh], dob[j], 0.0) for j, hh in work]
                dohb = [d.astype(BF16) for d in doh]
                s = [_nt(qh[w], kcat[j]) + bias_s[b, hh, idx[j][3]] for w, (j, hh) in enumerate(work)]
                dp = [_nt(dohb[w], vcat[j]) for w, (j, _) in enumerate(work)]
                lcol = [jnp.max(jnp.where(masks[hh], lb[j], -jnp.inf), axis=-1, keepdims=True) for j, hh in work]
                delta = [jnp.sum(doh[w] * ob[j], axis=-1, keepdims=True) for w, (j, _) in enumerate(work)]
                prob = [jnp.exp(sv - lv) for sv, lv in zip(s, lcol)]
                ds = [pv * (dv - de) for pv, dv, de in zip(prob, dp, delta)]
                dsb = [d.astype(BF16) for d in ds]
                dq = [_nn(dsb[w], kcat[j]) for w, (j, _) in enumerate(work)]
                dkc = [_tn(dsb[w], qh[w]) for w in range(len(work))]
                dvc = [_tn(prob[w].astype(BF16), dohb[w]) for w in range(len(work))]
                for w, (j, hh) in enumerate(work):
                    dbias_ref[0, b, hh] += ds[w]
                for j in range(ATT_UNROLL):
                    qs, gs, ps, _ = idx[j]
                    dkcat = dkc[2 * j] + dkc[2 * j + 1]
                    dvcat = dvc[2 * j] + dvc[2 * j + 1]
                    dq_ref[_rows(qs, r), :] += jnp.where(head0, dq[2 * j], dq[2 * j + 1]) * scale
                    dk_ref[_rows(ps, r), :] += dkcat[:BAND]
                    dk_ref[_rows(gs, r), :] += dkcat[BAND:]
                    dv_ref[_rows(ps, r), :] += dvcat[:BAND]
                    dv_ref[_rows(gs, r), :] += dvcat[BAND:]
                return carry

            lax.fori_loop(0, ATT_TILE // BAND // ATT_UNROLL, blocks, 0)

    tile = pl.BlockSpec((ATT_TILE, LANES), lambda p, t: (t, p))
    full = pl.BlockSpec((seq, LANES), lambda p, t: (0, p))
    return pl.pallas_call(
        body,
        grid=(N_HEADS // 2, n_tiles),
        in_specs=[
            pl.BlockSpec(memory_space=pltpu.SMEM),
            pl.BlockSpec((3, 2, BAND, 2 * BAND), lambda p, t: (0, 0, 0, 0)),
            pl.BlockSpec((ATT_TILE, LANES), lambda p, t: (t, p)),
            pl.BlockSpec((seq, LANES), lambda p, t: (0, 4 + p)),
            pl.BlockSpec((seq, LANES), lambda p, t: (0, 8 + p)),
            tile, tile, tile,
        ],
        out_specs=[tile, full, full,
                   pl.BlockSpec((1, 3, 2, BAND, 2 * BAND), lambda p, t: (p, 0, 0, 0, 0))],
        out_shape=[jax.ShapeDtypeStruct((seq, HEAD_W), F32)] * 3
        + [jax.ShapeDtypeStruct((N_HEADS // 2, 3, 2, BAND, 2 * BAND), F32)],
        scratch_shapes=[pltpu.VMEM((3, 2, 2, BAND, 2 * BAND), F32)],
        compiler_params=_params(("arbitrary", "arbitrary")),
        name="attn_bwd",
    )(rel_bias, tables, qkv, qkv, qkv, dy, y, lse)


def _rel_bias_grad(dbias, tables):
    def body(tab_ref, db_ref, out_ref):
        lane = lax.broadcasted_iota(jnp.int32, (1, LANES), 1)
        for pair in range(N_HEADS // 2):
            for hh in range(2):
                row = jnp.zeros((1, LANES), F32)
                for b in range(3):
                    tab = tab_ref[b, 0]
                    d = db_ref[pair, b, hh]

                    def bucket_sum(kk, row, tab=tab, d=d):
                        s = jnp.sum(jnp.where(tab == kk, d, 0.0), keepdims=True)
                        return row + jnp.where(lane == kk, s, 0.0)

                    row = lax.fori_loop(0, N_BUCKETS, bucket_sum, row)
                out_ref[pl.ds(2 * pair + hh, 1), :] = row

    return pl.pallas_call(
        body,
        out_shape=jax.ShapeDtypeStruct((N_HEADS, LANES), F32),
        compiler_params=pltpu.CompilerParams(vmem_limit_bytes=VMEM_LIMIT),
        name="rel_bias_grad",
    )(tables, dbias)


ROW_TILE = 512


def _head_sum_matrix():
    return (lax.broadcasted_iota(jnp.int32, (HEAD_W, LANES), 0) // HEAD_DIM
            == lax.broadcasted_iota(jnp.int32, (HEAD_W, LANES), 1)).astype(F32)


def _head_spread_matrix(offset=0):
    return (lax.broadcasted_iota(jnp.int32, (LANES, HEAD_W), 0)
            == lax.broadcasted_iota(jnp.int32, (LANES, HEAD_W), 1) // HEAD_DIM + offset).astype(F32)


def _head_gather_matrix(offset=0):
    return (lax.broadcasted_iota(jnp.int32, (HEAD_W, LANES), 0) // HEAD_DIM + offset
            == lax.broadcasted_iota(jnp.int32, (HEAD_W, LANES), 1)).astype(F32)


def _tri(lower, strict=False):
    r = lax.broadcasted_iota(jnp.int32, (CHUNK, CHUNK), 0)
    c = lax.broadcasted_iota(jnp.int32, (CHUNK, CHUNK), 1)
    if lower:
        return (c < r) if strict else (c <= r)
    return c >= r


def _softplus(z):
    return jnp.maximum(z, 0.0) + jnp.log(1.0 + jnp.exp(-jnp.abs(z)))


def _conv_taps(stage, w_ref, rows):
    return (w_ref[3:4, :] * stage[8:8 + rows, :] + w_ref[2:3, :] * stage[7:7 + rows, :]
            + w_ref[1:2, :] * stage[6:6 + rows, :] + w_ref[0:1, :] * stage[5:5 + rows, :])


def _l2_scale(xc, hsum, hspread):
    ssq = _nn(xc * xc, hsum, HIGHEST)
    return _nn(lax.rsqrt(ssq + EPS), hspread, HIGHEST)


def _stage_rows(stage, x_ref, xp_ref, i):
    stage[0:8, :] = jnp.where(i == 0, 0.0, xp_ref[...])
    stage[8:8 + ROW_TILE, :] = x_ref[...]


def _delta_prep_fwd(qkvz, ba, conv_w, alog_row, dt_row):
    seq = qkvz.shape[0]
    qkv_w = 3 * HEAD_W

    def body(x_ref, xp_ref, ba_ref, w_ref, al_ref, dt_ref, out_ref, stage):
        i = pl.program_id(0)
        _stage_rows(stage, x_ref, xp_ref, i)
        act = _silu(_conv_taps(stage, w_ref, ROW_TILE))
        hsum, hspread = _head_sum_matrix(), _head_spread_matrix()
        qc, kc = act[:, :HEAD_W], act[:, HEAD_W:2 * HEAD_W]
        out_ref[0] = qc * _l2_scale(qc, hsum, hspread) * (HEAD_DIM ** -0.5)
        out_ref[1] = kc * _l2_scale(kc, hsum, hspread)
        out_ref[2] = act[:, 2 * HEAD_W:]
        bav = ba_ref[...]
        out_ref[3] = _nn(_sigmoid(bav), hspread, HIGHEST)
        g8 = -jnp.exp(al_ref[...]) * _softplus(bav + dt_ref[...])
        gb = _nn(g8, _head_spread_matrix(N_HEADS), HIGHEST)
        cum = _tri(True).astype(F32)
        for ch in range(ROW_TILE // CHUNK):
            rows = slice(ch * CHUNK, (ch + 1) * CHUNK)
            out_ref[4, rows, :] = _nn(cum, gb[rows], HIGHEST)

    return pl.pallas_call(
        body,
        grid=(seq // ROW_TILE,),
        in_specs=[
            pl.BlockSpec((ROW_TILE, qkv_w), lambda i: (i, 0)),
            pl.BlockSpec((8, qkv_w), lambda i: (jnp.maximum(i * (ROW_TILE // 8) - 1, 0), 0)),
            pl.BlockSpec((ROW_TILE, LANES), lambda i: (i, 0)),
            pl.BlockSpec((4, qkv_w), lambda i: (0, 0)),
            pl.BlockSpec((1, LANES), lambda i: (0, 0)),
            pl.BlockSpec((1, LANES), lambda i: (0, 0)),
        ],
        out_specs=pl.BlockSpec((5, ROW_TILE, HEAD_W), lambda i: (0, i, 0)),
        out_shape=jax.ShapeDtypeStruct((5, seq, HEAD_W), F32),
        scratch_shapes=[pltpu.VMEM((ROW_TILE + 8, qkv_w), F32)],
        compiler_params=_params(("arbitrary",)),
        name="delta_prep_fwd",
    )(qkvz, qkvz, ba, conv_w, alog_row, dt_row)


def _split(x):
    hi = x.astype(BF16)
    return hi, (x - hi.astype(F32)).astype(BF16)


def _dot3(a, b, dot=_nn):
    return dot(a[0], b[0]) + (dot(a[0], b[1]) + dot(a[1], b[0]))


def _unit_lower_inverses(mats):
    eye = (lax.broadcasted_iota(jnp.int32, (CHUNK, CHUNK), 0)
           == lax.broadcasted_iota(jnp.int32, (CHUNK, CHUNK), 1)).astype(F32)
    invs = [eye - a for a in mats]
    powers = [_split(a) for a in mats]
    for step in range(5):
        squares = [_dot3(p, p) for p in powers]
        powers = [_split(s) for s in squares]
        invs = [inv + _dot3(_split(inv), p) for inv, p in zip(invs, powers)]
    return invs


def _chunk_terms(q, k, v, beta, gc):
    causal, strict = _tri(True), _tri(True, strict=True)
    e = jnp.exp(gc)
    g_last = jnp.broadcast_to(gc[CHUNK - 1:CHUNK, :], (CHUNK, CHUNK))
    f = jnp.exp(g_last - gc)
    e_last = jnp.exp(g_last)
    decay = jnp.where(causal, jnp.exp(jnp.where(causal, gc - gc.T, 0.0)), 0.0)
    kb = k * beta
    a_mat = jnp.where(strict, _nt(kb.astype(BF16), k.astype(BF16)) * decay, 0.0)
    qk = jnp.where(causal, _nt(q.astype(BF16), k.astype(BF16)) * decay, 0.0)
    return e, f, e_last, decay, kb, a_mat, qk


GROUP = 8
UNROLL = 8


def _chunk_rows(ci):
    return pl.ds(pl.multiple_of(ci * CHUNK, CHUNK), CHUNK)


def _delta_chunk_fwd(xh):
    seq = xh.shape[2]
    rows_per_step = GROUP * CHUNK

    def body(x_ref, inv_ref, qk_ref, u_ref, w_ref):
        def group(gi, carry):
            rows = [_chunk_rows(gi * UNROLL + step) for step in range(UNROLL)]
            xs = [[x_ref[j, 0, r, :] for j in range(5)] for r in rows]
            terms = [_chunk_terms(*x) for x in xs]
            invs = _unit_lower_inverses([t[5] for t in terms])
            for r, x, t, inv in zip(rows, xs, terms, invs):
                e, kb, qk = t[0], t[4], t[6]
                inv_parts = _split(inv)
                inv_ref[0, r, :] = inv
                qk_ref[0, r, :] = qk
                u_ref[0, r, :] = _dot3(inv_parts, _split(x[2] * x[3]))
                w_ref[0, r, :] = _dot3(inv_parts, _split(kb * e))
            return carry

        lax.fori_loop(0, GROUP // UNROLL, group, 0)

    out = pl.BlockSpec((1, rows_per_step, HEAD_DIM), lambda h, g: (h, g, 0))
    return pl.pallas_call(
        body,
        grid=(N_HEADS, seq // rows_per_step),
        in_specs=[pl.BlockSpec((5, 1, rows_per_step, HEAD_DIM), lambda h, g: (0, h, g, 0))],
        out_specs=[out] * 4,
        out_shape=[jax.ShapeDtypeStruct((N_HEADS, seq, HEAD_DIM), F32)] * 4,
        compiler_params=_params(("parallel", "parallel")),
        name="delta_chunk_fwd",
    )(xh)


def _decays(gc):
    g_last = jnp.broadcast_to(gc[CHUNK - 1:CHUNK, :], (CHUNK, CHUNK))
    return jnp.exp(gc), jnp.exp(g_last - gc), jnp.exp(g_last)


def _head_blocks(index, n_steps=None):
    rows_per_step = GROUP * CHUNK
    if n_steps is None:
        return pl.BlockSpec((1, N_HEADS, rows_per_step, HEAD_DIM), lambda g: (index, 0, g, 0))
    return pl.BlockSpec((1, N_HEADS, rows_per_step, HEAD_DIM), lambda g: (index, 0, n_steps - 1 - g, 0))


def _delta_scan_fwd(xh, qk_h, u_h, w_h):
    seq = xh.shape[2]
    rows_per_step = GROUP * CHUNK

    def body(q_ref, k_ref, gc_ref, qk_ref, u_ref, w_ref, o_ref, st_ref, state):
        @pl.when(pl.program_id(0) == 0)
        def _():
            state[...] = jnp.zeros_like(state)

        def chunk(ci, carry):
            rows = _chunk_rows(ci)
            heads = range(N_HEADS)
            dec = [_decays(gc_ref[0, h, rows, :]) for h in heads]
            s = [state[h] for h in heads]
            sb = [s[h].astype(BF16) for h in heads]
            vnb = [(u_ref[h, rows, :] - _nn(w_ref[h, rows, :].astype(BF16), sb[h])).astype(BF16) for h in heads]
            for h in heads:
                o_ref[h, rows, :] = (_nn((q_ref[0, h, rows, :] * dec[h][0]).astype(BF16), sb[h])
                                     + _nn(qk_ref[h, rows, :].astype(BF16), vnb[h]))
                st_ref[h, rows, :] = s[h]
            for h in heads:
                state[h] = s[h] * dec[h][2] + _tn((k_ref[0, h, rows, :] * dec[h][1]).astype(BF16), vnb[h])
            return carry

        lax.fori_loop(0, GROUP, chunk, 0)

    blk = pl.BlockSpec((N_HEADS, rows_per_step, HEAD_DIM), lambda g: (0, g, 0))
    return pl.pallas_call(
        body,
        grid=(seq // rows_per_step,),
        in_specs=[_head_blocks(0), _head_blocks(1), _head_blocks(4), blk, blk, blk],
        out_specs=[blk, blk],
        out_shape=[jax.ShapeDtypeStruct((N_HEADS, seq, HEAD_DIM), F32)] * 2,
        scratch_shapes=[pltpu.VMEM((N_HEADS, CHUNK, CHUNK), F32)],
        compiler_params=_params(("arbitrary",)),
        name="delta_scan_fwd",
    )(xh, xh, xh, qk_h, u_h, w_h)


def _delta_scan_bwd(xh, qk_h, w_h, do_h):
    seq = xh.shape[2]
    rows_per_step = GROUP * CHUNK
    n_steps = seq // rows_per_step

    def body(q_ref, k_ref, gc_ref, qk_ref, w_ref, do_ref, dsn_ref, dvn_ref, dstate):
        @pl.when(pl.program_id(0) == 0)
        def _():
            dstate[...] = jnp.zeros_like(dstate)

        def chunk(step, carry):
            rows = _chunk_rows(GROUP - 1 - step)
            heads = range(N_HEADS)
            dec = [_decays(gc_ref[0, h, rows, :]) for h in heads]
            ds_next = [dstate[h] for h in heads]
            dob = [do_ref[h, rows, :].astype(BF16) for h in heads]
            dv_new = [_tn(qk_ref[h, rows, :].astype(BF16), dob[h])
                      + _nn((k_ref[0, h, rows, :] * dec[h][1]).astype(BF16), ds_next[h].astype(BF16)) for h in heads]
            for h in heads:
                dsn_ref[h, rows, :] = ds_next[h]
                dvn_ref[h, rows, :] = dv_new[h]
            for h in heads:
                dstate[h] = (_tn((q_ref[0, h, rows, :] * dec[h][0]).astype(BF16), dob[h]) + dec[h][2] * ds_next[h]
                             - _tn(w_ref[h, rows, :].astype(BF16), dv_new[h].astype(BF16)))
            return carry

        lax.fori_loop(0, GROUP, chunk, 0)

    blk = pl.BlockSpec((N_HEADS, rows_per_step, HEAD_DIM), lambda g: (0, n_steps - 1 - g, 0))
    return pl.pallas_call(
        body,
        grid=(n_steps,),
        in_specs=[_head_blocks(0, n_steps), _head_blocks(1, n_steps), _head_blocks(4, n_steps), blk, blk, blk],
        out_specs=[blk, blk],
        out_shape=[jax.ShapeDtypeStruct((N_HEADS, seq, HEAD_DIM), F32)] * 2,
        scratch_shapes=[pltpu.VMEM((N_HEADS, CHUNK, CHUNK), F32)],
        compiler_params=_params(("arbitrary",)),
        name="delta_scan_bwd",
    )(xh, xh, xh, qk_h, w_h, do_h)


def _delta_chunk_bwd(xh, inv_h, u_h, w_h, st_h, dsn_h, dvn_h, do_h):
    seq = xh.shape[2]
    rows_per_step = GROUP * CHUNK

    def body(x_ref, inv_ref, u_ref, w_ref, st_ref, dsn_ref, dvn_ref, do_ref, dx_ref):
        causal, strict = _tri(True), _tri(True, strict=True)
        last_row = lax.broadcasted_iota(jnp.int32, (CHUNK, CHUNK), 0) == CHUNK - 1

        def bf(vals):
            return [val.astype(BF16) for val in vals]

        def group(gi, carry):
            rows = [_chunk_rows(gi * UNROLL + step) for step in range(UNROLL)]
            n = range(UNROLL)
            q, k, v, beta, gc = [[x_ref[j, 0, r, :] for r in rows] for j in range(5)]
            terms = [_chunk_terms(q[i], k[i], v[i], beta[i], gc[i]) for i in n]
            e, f, e_last, decay, kb, a_mat, qk = [[t[j] for t in terms] for j in range(7)]
            inv = [_split(inv_ref[0, r, :]) for r in rows]
            u = [u_ref[0, r, :] for r in rows]
            w = [w_ref[0, r, :] for r in rows]
            s = [st_ref[0, r, :] for r in rows]
            ds_next = [dsn_ref[0, r, :] for r in rows]
            dv_new = [dvn_ref[0, r, :] for r in rows]
            sb, dsb, dvb, wb = bf(s), bf(ds_next), bf(dv_new), bf(w)
            dob = bf([do_ref[0, r, :] for r in rows])
            qbf, kbf, kbb = bf(q), bf(k), bf(kb)
            vnb = bf([u[i] - _nn(wb[i], sb[i]) for i in n])
            dqe = [_nt(dob[i], sb[i]) for i in n]
            dw = [-_nt(dvb[i], sb[i]) for i in n]
            dkf = [_nt(vnb[i], dsb[i]) for i in n]
            dqk = [jnp.where(causal, _nt(dob[i], vnb[i]), 0.0) for i in n]
            drhs_u = [_dot3(inv[i], _split(dv_new[i]), _tn) for i in n]
            drhs_w = [_dot3(inv[i], _split(dw[i]), _tn) for i in n]
            da = [-jnp.where(strict, _nt(drhs_u[i].astype(BF16), u[i].astype(BF16))
                             + _nt(drhs_w[i].astype(BF16), wb[i]), 0.0) for i in n]
            dad = bf([da[i] * decay[i] for i in n])
            dqd = bf([dqk[i] * decay[i] for i in n])
            dkb = [e[i] * drhs_w[i] + _nn(dad[i], kbf[i]) for i in n]
            dk = [_tn(dad[i], kbb[i]) + _tn(dqd[i], qbf[i]) + f[i] * dkf[i] + beta[i] * dkb[i] for i in n]
            dq = [_nn(dqd[i], kbf[i]) + e[i] * dqe[i] for i in n]
            for i in n:
                de_full = kb[i] * drhs_w[i] + q[i] * dqe[i]
                df_full = k[i] * dkf[i]
                m = da[i] * a_mat[i] + dqk[i] * qk[i]
                dgc = de_full * e[i] - df_full * f[i] + m - m.T
                tail = jnp.sum(df_full * f[i] + s[i] * ds_next[i] * e_last[i], axis=0, keepdims=True)
                dgc = dgc + jnp.where(last_row, jnp.broadcast_to(tail, (CHUNK, CHUNK)), 0.0)
                dx_ref[0, 0, rows[i], :] = dq[i]
                dx_ref[1, 0, rows[i], :] = dk[i]
                dx_ref[2, 0, rows[i], :] = beta[i] * drhs_u[i]
                dx_ref[3, 0, rows[i], :] = v[i] * drhs_u[i] + k[i] * dkb[i]
                dx_ref[4, 0, rows[i], :] = dgc
            return carry

        lax.fori_loop(0, GROUP // UNROLL, group, 0)

    blk = pl.BlockSpec((1, rows_per_step, HEAD_DIM), lambda h, g: (h, g, 0))
    blk5 = pl.BlockSpec((5, 1, rows_per_step, HEAD_DIM), lambda h, g: (0, h, g, 0))
    return pl.pallas_call(
        body,
        grid=(N_HEADS, seq // rows_per_step),
        in_specs=[blk5] + [blk] * 7,
        out_specs=blk5,
        out_shape=jax.ShapeDtypeStruct((5, N_HEADS, seq, HEAD_DIM), F32),
        compiler_params=_params(("parallel", "parallel")),
        name="delta_chunk_bwd",
    )(xh, inv_h, u_h, w_h, st_h, dsn_h, dvn_h, do_h)


def _delta_post_fwd(o, qkvz, gain_row):
    seq = o.shape[0]

    def body(o_ref, z_ref, g_ref, y_ref):
        ov = o_ref[...]
        ms = _nn(ov * ov, _head_sum_matrix(), HIGHEST) * (1.0 / HEAD_DIM)
        rb = _nn(lax.rsqrt(ms + EPS), _head_spread_matrix(), HIGHEST)
        y_ref[...] = (ov * rb * g_ref[...] * _silu(z_ref[...])).astype(y_ref.dtype)

    tile = pl.BlockSpec((ROW_TILE, HEAD_W), lambda i: (i, 0))
    return pl.pallas_call(
        body,
        grid=(seq // ROW_TILE,),
        in_specs=[tile, pl.BlockSpec((ROW_TILE, HEAD_W), lambda i: (i, 3)), pl.BlockSpec((1, HEAD_W), lambda i: (0, 0))],
        out_specs=tile,
        out_shape=jax.ShapeDtypeStruct((seq, HEAD_W), BF16),
        compiler_params=_params(("arbitrary",)),
        name="delta_post_fwd",
    )(o, qkvz, gain_row)


def _delta_post_bwd(dy, o, qkvz, gain_row):
    seq = o.shape[0]

    def body(dy_ref, o_ref, z_ref, g_ref, do_ref, dz_ref, dg_ref):
        @pl.when(pl.program_id(0) == 0)
        def _():
            dg_ref[...] = jnp.zeros_like(dg_ref)

        ov, zv, dyv, gain = o_ref[...], z_ref[...], dy_ref[...], g_ref[...]
        hsum, hspread = _head_sum_matrix(), _head_spread_matrix()
        ms = _nn(ov * ov, hsum, HIGHEST) * (1.0 / HEAD_DIM)
        rb = _nn(lax.rsqrt(ms + EPS), hspread, HIGHEST)
        ohat = ov * rb
        dz_ref[...] = dyv * ohat * gain * _dsilu(zv)
        dn = dyv * _silu(zv)
        dg_ref[0:1, :] += jnp.sum(dn * ohat, axis=0, keepdims=True)
        dohat = dn * gain

        @pl.when(pl.program_id(0) == pl.num_programs(0) - 1)
        def _():
            fold = (lax.broadcasted_iota(jnp.int32, (HEAD_W, HEAD_W), 0) % HEAD_DIM
                    == lax.broadcasted_iota(jnp.int32, (HEAD_W, HEAD_W), 1)).astype(F32)
            dg_ref[1:2, :] = _nn(dg_ref[0:1, :], fold, HIGHEST)

        proj = _nn(_nn(dohat * ohat, hsum, HIGHEST) * (1.0 / HEAD_DIM), hspread, HIGHEST)
        do_ref[...] = rb * (dohat - ohat * proj)

    tile = pl.BlockSpec((ROW_TILE, HEAD_W), lambda i: (i, 0))
    return pl.pallas_call(
        body,
        grid=(seq // ROW_TILE,),
        in_specs=[pl.BlockSpec((ROW_TILE, HEAD_W), lambda i: (i, 1)), tile,
                  pl.BlockSpec((ROW_TILE, HEAD_W), lambda i: (i, 3)), pl.BlockSpec((1, HEAD_W), lambda i: (0, 0))],
        out_specs=[tile, tile, pl.BlockSpec((2, HEAD_W), lambda i: (0, 0))],
        out_shape=[jax.ShapeDtypeStruct((seq, HEAD_W), F32), jax.ShapeDtypeStruct((seq, HEAD_W), F32),
                   jax.ShapeDtypeStruct((2, HEAD_W), F32)],
        compiler_params=_params(("arbitrary",)),
        name="delta_post_bwd",
    )(dy, o, qkvz, gain_row)


def _delta_prep_bwd(qkvz, ba, conv_w, alog_row, dt_row, dxs):
    seq = qkvz.shape[0]
    qkv_w = 3 * HEAD_W

    def body(x_ref, xp_ref, ba_ref, w_ref, al_ref, dt_ref, dx_ref, dconv_ref, dba_ref, dvec_ref, stage):
        i = pl.program_id(0)

        @pl.when(i == 0)
        def _():
            dvec_ref[...] = jnp.zeros_like(dvec_ref)

        _stage_rows(stage, x_ref, xp_ref, i)
        pre = _conv_taps(stage, w_ref, ROW_TILE)
        act = _silu(pre)
        slope = _dsilu(pre)
        hsum, hspread = _head_sum_matrix(), _head_spread_matrix()
        for j, scale in ((0, HEAD_DIM ** -0.5), (1, 1.0)):
            cols = slice(j * HEAD_W, (j + 1) * HEAD_W)
            xc = act[:, cols]
            rb = _l2_scale(xc, hsum, hspread)
            xhat = xc * rb
            dhat = dx_ref[j] * scale
            proj = _nn(_nn(dhat * xhat, hsum, HIGHEST), hspread, HIGHEST)
            dconv_ref[:, cols] = rb * (dhat - xhat * proj) * slope[:, cols]
        dconv_ref[:, 2 * HEAD_W:] = dx_ref[2] * slope[:, 2 * HEAD_W:]

        bav = ba_ref[...]
        beta8 = _sigmoid(bav)
        dbeta8 = _nn(dx_ref[3], _head_gather_matrix(), HIGHEST)
        dgc8 = _nn(dx_ref[4], _head_gather_matrix(N_HEADS), HIGHEST)
        rev = _tri(False).astype(F32)
        z = bav + dt_ref[...]
        ea = jnp.exp(al_ref[...])
        g8 = -ea * _softplus(z)
        sig = _sigmoid(z)
        d_alog = jnp.zeros((1, LANES), F32)
        d_dt = jnp.zeros((1, LANES), F32)
        for ch in range(ROW_TILE // CHUNK):
            rows = slice(ch * CHUNK, (ch + 1) * CHUNK)
            dg8 = _nn(rev, dgc8[rows], HIGHEST)
            da = -dg8 * ea * sig[rows]
            dba_ref[rows, :] = dbeta8[rows] * beta8[rows] * (1.0 - beta8[rows]) + da
            d_alog = d_alog + jnp.sum(dg8 * g8[rows], axis=0, keepdims=True)
            d_dt = d_dt + jnp.sum(da, axis=0, keepdims=True)
        dvec_ref[0:1, :] += d_alog
        dvec_ref[1:2, :] += d_dt

    return pl.pallas_call(
        body,
        grid=(seq // ROW_TILE,),
        in_specs=[
            pl.BlockSpec((ROW_TILE, qkv_w), lambda i: (i, 0)),
            pl.BlockSpec((8, qkv_w), lambda i: (jnp.maximum(i * (ROW_TILE // 8) - 1, 0), 0)),
            pl.BlockSpec((ROW_TILE, LANES), lambda i: (i, 0)),
            pl.BlockSpec((4, qkv_w), lambda i: (0, 0)),
            pl.BlockSpec((1, LANES), lambda i: (0, 0)),
            pl.BlockSpec((1, LANES), lambda i: (0, 0)),
            pl.BlockSpec((5, ROW_TILE, HEAD_W), lambda i: (0, i, 0)),
        ],
        out_specs=[pl.BlockSpec((ROW_TILE, qkv_w), lambda i: (i, 0)),
                   pl.BlockSpec((ROW_TILE, LANES), lambda i: (i, 0)),
                   pl.BlockSpec((2, LANES), lambda i: (0, 0))],
        out_shape=[jax.ShapeDtypeStruct((seq, qkv_w), F32), jax.ShapeDtypeStruct((seq, LANES), F32),
                   jax.ShapeDtypeStruct((2, LANES), F32)],
        scratch_shapes=[pltpu.VMEM((ROW_TILE + 8, qkv_w), F32)],
        compiler_params=_params(("arbitrary",)),
        name="delta_prep_bwd",
    )(qkvz, qkvz, ba, conv_w, alog_row, dt_row, dxs)


def _conv_bwd(dconv, qkvz, conv_w):
    seq = dconv.shape[0]
    qkv_w = 3 * HEAD_W
    n_tiles = seq // ROW_TILE

    def body(dy_ref, dyn_ref, x_ref, xp_ref, w_ref, dx_ref, dw_ref, stage, dstage):
        i = pl.program_id(0)

        @pl.when(i == 0)
        def _():
            dw_ref[...] = jnp.zeros_like(dw_ref)

        _stage_rows(stage, x_ref, xp_ref, i)
        dstage[0:ROW_TILE, :] = dy_ref[...]
        dstage[ROW_TILE:ROW_TILE + 8, :] = jnp.where(i == n_tiles - 1, 0.0, dyn_ref[...])
        dy = dy_ref[...]
        dx_ref[...] = (w_ref[3:4, :] * dy + w_ref[2:3, :] * dstage[1:1 + ROW_TILE, :]
                       + w_ref[1:2, :] * dstage[2:2 + ROW_TILE, :] + w_ref[0:1, :] * dstage[3:3 + ROW_TILE, :])
        for j in range(4):
            dw_ref[j:j + 1, :] += jnp.sum(dy * stage[5 + j:5 + j + ROW_TILE, :], axis=0, keepdims=True)

    tile = pl.BlockSpec((ROW_TILE, qkv_w), lambda i: (i, 0))
    return pl.pallas_call(
        body,
        grid=(n_tiles,),
        in_specs=[
            tile,
            pl.BlockSpec((8, qkv_w), lambda i: (jnp.minimum((i + 1) * (ROW_TILE // 8), seq // 8 - 1), 0)),
            tile,
            pl.BlockSpec((8, qkv_w), lambda i: (jnp.maximum(i * (ROW_TILE // 8) - 1, 0), 0)),
            pl.BlockSpec((4, qkv_w), lambda i: (0, 0)),
        ],
        out_specs=[tile, pl.BlockSpec((4, qkv_w), lambda i: (0, 0))],
        out_shape=[jax.ShapeDtypeStruct((seq, qkv_w), F32), jax.ShapeDtypeStruct((4, qkv_w), F32)],
        scratch_shapes=[pltpu.VMEM((ROW_TILE + 8, qkv_w), F32), pltpu.VMEM((ROW_TILE + 8, qkv_w), F32)],
        compiler_params=_params(("arbitrary",)),
        name="conv_bwd",
    )(dconv, dconv, qkvz, qkvz, conv_w)


def _to_heads(a):
    lead = a.shape[:-2]
    seq = a.shape[-2]
    a = a.reshape(*lead, seq, N_HEADS, HEAD_DIM)
    return jnp.swapaxes(a, -2, -3)


def _from_heads(a):
    a = jnp.swapaxes(a, -2, -3)
    return a.reshape(*a.shape[:-2], HEAD_W)


FF_TILE = 1408
FF_TILE_BWD = 256


def _row(a):
    return pl.BlockSpec((1, a), lambda *_: (0, 0))


def _rms_fwd(xv, gain):
    rstd = lax.rsqrt(jnp.mean(xv * xv, axis=-1, keepdims=True) + EPS)
    xhat = xv * rstd
    return xhat, rstd, xhat * gain


def _rms_bwd(dnorm, xhat, rstd, gain):
    dxhat = dnorm * gain
    dx = rstd * (dxhat - xhat * jnp.mean(dxhat * xhat, axis=-1, keepdims=True))
    return dx, jnp.sum(dnorm * xhat, axis=0, keepdims=True)


def _inproj_fwd(x, gain, scale, shift, w_a, w_d, w_ba):
    seq = x.shape[0]

    def body(x_ref, g_ref, sc_ref, sh_ref, wa_ref, wd_ref, wb_ref, h_ref, a_ref, d_ref, b_ref):
        _, _, norm = _rms_fwd(x_ref[...], g_ref[...])
        h = (norm * (1.0 + sc_ref[...]) + sh_ref[...]).astype(BF16)
        h_ref[...] = h
        a_ref[...] = _nn(h, wa_ref[...])
        d_ref[...] = _nn(h, wd_ref[...])
        b_ref[...] = _nn(h, wb_ref[...])

    def rows(width):
        return pl.BlockSpec((ROW_TILE, width), lambda i: (i, 0))

    def whole(a):
        return pl.BlockSpec(a.shape, lambda i: (0, 0))

    return pl.pallas_call(
        body,
        grid=(seq // ROW_TILE,),
        in_specs=[rows(D_MODEL), _row(D_MODEL), _row(D_MODEL), _row(D_MODEL), whole(w_a), whole(w_d), whole(w_ba)],
        out_specs=[rows(D_MODEL), rows(3 * HEAD_W), rows(4 * HEAD_W), rows(LANES)],
        out_shape=[jax.ShapeDtypeStruct((seq, D_MODEL), BF16), jax.ShapeDtypeStruct((seq, 3 * HEAD_W), F32),
                   jax.ShapeDtypeStruct((seq, 4 * HEAD_W), F32), jax.ShapeDtypeStruct((seq, LANES), F32)],
        compiler_params=_params(("arbitrary",)),
        name="inproj_fwd",
    )(x, gain, scale, shift, w_a, w_d, w_ba)


def _outproj_fwd(y_attn, y_delta, w_out, x, gate1, gain, scale, shift):
    seq = x.shape[0]

    def body(ya_ref, yd_ref, wa_ref, wd_ref, x_ref, g1_ref, g_ref, sc_ref, sh_ref, x1_ref, h_ref, y_ref):
        y = _nn(ya_ref[...].astype(BF16), wa_ref[...]) + _nn(yd_ref[...], wd_ref[...])
        x1 = x_ref[...] + g1_ref[...] * y
        _, _, norm = _rms_fwd(x1, g_ref[...])
        x1_ref[...] = x1
        h_ref[...] = (norm * (1.0 + sc_ref[...]) + sh_ref[...]).astype(BF16)
        y_ref[...] = y.astype(BF16)

    def rows(width):
        return pl.BlockSpec((ROW_TILE, width), lambda i: (i, 0))

    return pl.pallas_call(
        body,
        grid=(seq // ROW_TILE,),
        in_specs=[rows(HEAD_W), rows(HEAD_W),
                  pl.BlockSpec((HEAD_W, D_MODEL), lambda i: (0, 0)), pl.BlockSpec((HEAD_W, D_MODEL), lambda i: (1, 0)),
                  rows(D_MODEL), _row(D_MODEL), _row(D_MODEL), _row(D_MODEL), _row(D_MODEL)],
        out_specs=[rows(D_MODEL), rows(D_MODEL), rows(D_MODEL)],
        out_shape=[jax.ShapeDtypeStruct((seq, D_MODEL), F32), jax.ShapeDtypeStruct((seq, D_MODEL), BF16),
                   jax.ShapeDtypeStruct((seq, D_MODEL), BF16)],
        compiler_params=_params(("arbitrary",)),
        name="outproj_fwd",
    )(y_attn, y_delta, w_out, w_out, x, gate1, gain, scale, shift)


def _ffn_fwd(h2, w_gate, w_up, w_down, x1, gate2, final_gain, target):
    seq = h2.shape[0]
    n_rows, n_ff = seq // ROW_TILE, D_FF // FF_TILE

    def body(h_ref, wg_ref, wu_ref, wd_ref, x1_ref, g2_ref, gf_ref, t_ref, gate_ref, up_ref, dx2_ref, st_ref, acc):
        i, j = pl.program_id(0), pl.program_id(1)

        @pl.when((i == 0) & (j == 0))
        def _():
            st_ref[...] = jnp.zeros_like(st_ref)

        h = h_ref[...]
        gate = _nn(h, wg_ref[...])
        up = _nn(h, wu_ref[...])
        gate_ref[...] = gate.astype(BF16)
        up_ref[...] = up.astype(BF16)
        part = _nn((_silu(gate) * up).astype(BF16), wd_ref[...])

        @pl.when(j == 0)
        def _():
            acc[...] = part

        @pl.when(j > 0)
        def _():
            acc[...] += part

        @pl.when(j == n_ff - 1)
        def _():
            y2 = acc[...]
            x2 = x1_ref[...] + g2_ref[...] * y2
            xhat, rstd, out = _rms_fwd(x2, gf_ref[...])
            diff = out - t_ref[...]
            dx2, dgain = _rms_bwd(diff * (1.0 / D_MODEL), xhat, rstd, gf_ref[...])
            dx2_ref[...] = dx2
            st_ref[0:1, :] += dgain
            st_ref[1:2, :] += jnp.sum(dx2 * y2, axis=0, keepdims=True)
            st_ref[2:3, :] += jnp.sum(diff * diff, axis=0, keepdims=True) * (0.5 / D_MODEL)

        @pl.when((i == n_rows - 1) & (j == n_ff - 1))
        def _():
            st_ref[3:4, :] = jnp.broadcast_to(jnp.sum(st_ref[2:3, :], keepdims=True), (1, D_MODEL))

    def rows(width):
        return pl.BlockSpec((ROW_TILE, width), lambda i, j: (i, 0))

    ff = pl.BlockSpec((ROW_TILE, FF_TILE), lambda i, j: (i, j))
    return pl.pallas_call(
        body,
        grid=(n_rows, n_ff),
        in_specs=[rows(D_MODEL),
                  pl.BlockSpec((D_MODEL, FF_TILE), lambda i, j: (0, j)), pl.BlockSpec((D_MODEL, FF_TILE), lambda i, j: (0, j)),
                  pl.BlockSpec((FF_TILE, D_MODEL), lambda i, j: (j, 0)),
                  rows(D_MODEL), _row(D_MODEL), _row(D_MODEL), rows(D_MODEL)],
        out_specs=[ff, ff, rows(D_MODEL), pl.BlockSpec((8, D_MODEL), lambda i, j: (0, 0))],
        out_shape=[jax.ShapeDtypeStruct((seq, D_FF), BF16), jax.ShapeDtypeStruct((seq, D_FF), BF16),
                   jax.ShapeDtypeStruct((seq, D_MODEL), F32), jax.ShapeDtypeStruct((8, D_MODEL), F32)],
        scratch_shapes=[pltpu.VMEM((ROW_TILE, D_MODEL), F32)],
        compiler_params=_params(("arbitrary", "arbitrary")),
        name="ffn_fwd",
    )(h2, w_gate, w_up, w_down, x1, gate2, final_gain, target)


def _ffn_bwd(dx2, gate, up, w_gate, w_up, w_down, x1, y, gate2, gate1, gain, scale):
    seq = dx2.shape[0]
    n_rows, n_ff = seq // ROW_TILE, D_FF // FF_TILE_BWD

    def body(dx2_ref, gate_ref, up_ref, wg_ref, wu_ref, wd_ref, x1_ref, y_ref, g2_ref, g1_ref, g_ref, sc_ref,
             dgate_ref, dup_ref, act_ref, dy2_ref, dx1_ref, dy_ref, st_ref, acc):
        i, j = pl.program_id(0), pl.program_id(1)

        @pl.when((i == 0) & (j == 0))
        def _():
            st_ref[...] = jnp.zeros_like(st_ref)

        dy2 = (g2_ref[...] * dx2_ref[...]).astype(BF16)
        dy2_ref[...] = dy2
        gate = gate_ref[...].astype(F32)
        up = up_ref[...].astype(F32)
        dact = _nt(dy2, wd_ref[...])
        act_ref[...] = (_silu(gate) * up).astype(BF16)
        dgate = (dact * up * _dsilu(gate)).astype(BF16)
        dup = (dact * _silu(gate)).astype(BF16)
        dgate_ref[...] = dgate
        dup_ref[...] = dup
        part = _nt(dgate, wg_ref[...]) + _nt(dup, wu_ref[...])

        @pl.when(j == 0)
        def _():
            acc[...] = part

        @pl.when(j > 0)
        def _():
            acc[...] += part

        @pl.when(j == n_ff - 1)
        def _():
            dh = acc[...]
            xhat, rstd, norm = _rms_fwd(x1_ref[...], g_ref[...])
            dnorm = dh * (1.0 + sc_ref[...])
            dxn, dgain = _rms_bwd(dnorm, xhat, rstd, g_ref[...])
            dx1 = dx2_ref[...] + dxn
            dx1_ref[...] = dx1
            dy_ref[...] = (g1_ref[...] * dx1).astype(BF16)
            st_ref[0:1, :] += jnp.sum(dh, axis=0, keepdims=True)
            st_ref[1:2, :] += jnp.sum(dh * norm, axis=0, keepdims=True)
            st_ref[2:3, :] += dgain
            st_ref[3:4, :] += jnp.sum(dx1 * y_ref[...].astype(F32), axis=0, keepdims=True)

    def rows(width):
        return pl.BlockSpec((ROW_TILE, width), lambda i, j: (i, 0))

    ff = pl.BlockSpec((ROW_TILE, FF_TILE_BWD), lambda i, j: (i, j))
    w_cols = pl.BlockSpec((D_MODEL, FF_TILE_BWD), lambda i, j: (0, j))
    return pl.pallas_call(
        body,
        grid=(n_rows, n_ff),
        in_specs=[rows(D_MODEL), ff, ff, w_cols, w_cols,
                  pl.BlockSpec((FF_TILE_BWD, D_MODEL), lambda i, j: (j, 0)),
                  rows(D_MODEL), rows(D_MODEL), _row(D_MODEL), _row(D_MODEL), _row(D_MODEL), _row(D_MODEL)],
        out_specs=[ff, ff, ff, rows(D_MODEL), rows(D_MODEL), rows(D_MODEL), pl.BlockSpec((8, D_MODEL), lambda i, j: (0, 0))],
        out_shape=[jax.ShapeDtypeStruct((seq, D_FF), BF16)] * 3
        + [jax.ShapeDtypeStruct((seq, D_MODEL), BF16), jax.ShapeDtypeStruct((seq, D_MODEL), F32),
           jax.ShapeDtypeStruct((seq, D_MODEL), BF16), jax.ShapeDtypeStruct((8, D_MODEL), F32)],
        scratch_shapes=[pltpu.VMEM((ROW_TILE, D_MODEL), F32)],
        compiler_params=_params(("arbitrary", "arbitrary")),
        name="ffn_bwd",
    )(dx2, gate, up, w_gate, w_up, w_down, x1, y, gate2, gate1, gain, scale)


def _outproj_bwd(dy, w_out):
    seq = dy.shape[0]

    def body(dy_ref, w_ref, out_ref):
        out_ref[...] = _nt(dy_ref[...], w_ref[...])

    rows = pl.BlockSpec((ROW_TILE, D_MODEL), lambda i: (i, 0))
    return pl.pallas_call(
        body,
        grid=(seq // ROW_TILE,),
        in_specs=[rows, pl.BlockSpec((D_MODEL, D_MODEL), lambda i: (0, 0))],
        out_specs=rows,
        out_shape=jax.ShapeDtypeStruct((seq, D_MODEL), F32),
        compiler_params=_params(("arbitrary",)),
        name="outproj_bwd",
    )(dy, w_out)


def _inproj_bwd(dq, dk, dv, dxd, dz, dba, w_a, w_d, w_ba, x, dx1, gain, scale):
    seq = x.shape[0]

    def body(dq_ref, dk_ref, dv_ref, dxd_ref, dz_ref, dba_ref, wa_ref, wd_ref, wb_ref, x_ref, dx1_ref, g_ref, sc_ref,
             gx_ref, st_ref):
        @pl.when(pl.program_id(0) == 0)
        def _():
            st_ref[...] = jnp.zeros_like(st_ref)

        dh = (_nt(dq_ref[...].astype(BF16), wa_ref[:, 0:HEAD_W])
              + _nt(dk_ref[...].astype(BF16), wa_ref[:, HEAD_W:2 * HEAD_W])
              + _nt(dv_ref[...].astype(BF16), wa_ref[:, 2 * HEAD_W:])
              + _nt(dxd_ref[...].astype(BF16), wd_ref[:, 0:3 * HEAD_W])
              + _nt(dz_ref[...].astype(BF16), wd_ref[:, 3 * HEAD_W:])
              + _nt(dba_ref[...].astype(BF16), wb_ref[...]))
        xhat, rstd, norm = _rms_fwd(x_ref[...], g_ref[...])
        dxn, dgain = _rms_bwd(dh * (1.0 + sc_ref[...]), xhat, rstd, g_ref[...])
        gx_ref[...] = dx1_ref[...] + dxn
        st_ref[0:1, :] += jnp.sum(dh, axis=0, keepdims=True)
        st_ref[1:2, :] += jnp.sum(dh * norm, axis=0, keepdims=True)
        st_ref[2:3, :] += dgain

    def rows(width):
        return pl.BlockSpec((ROW_TILE, width), lambda i: (i, 0))

    def whole(a):
        return pl.BlockSpec(a.shape, lambda i: (0, 0))

    return pl.pallas_call(
        body,
        grid=(seq // ROW_TILE,),
        in_specs=[rows(HEAD_W), rows(HEAD_W), rows(HEAD_W), rows(3 * HEAD_W), rows(HEAD_W), rows(LANES),
                  whole(w_a), whole(w_d), whole(w_ba), rows(D_MODEL), rows(D_MODEL), _row(D_MODEL), _row(D_MODEL)],
        out_specs=[rows(D_MODEL), pl.BlockSpec((8, D_MODEL), lambda i: (0, 0))],
        out_shape=[jax.ShapeDtypeStruct((seq, D_MODEL), F32), jax.ShapeDtypeStruct((8, D_MODEL), F32)],
        compiler_params=_params(("arbitrary",)),
        name="inproj_bwd",
    )(dq, dk, dv, dxd, dz, dba, w_a, w_d, w_ba, x, dx1, gain, scale)


def _weight_grad(a, b, name):
    seq, m = a.shape
    n = b.shape[1]
    tm = m if m <= 1536 else m // 2
    tn = n if n <= 1536 else n // 2
    n_k = seq // ROW_TILE

    def body(a_ref, b_ref, out_ref):
        part = _tn(a_ref[...].astype(BF16), b_ref[...].astype(BF16))

        @pl.when(pl.program_id(2) == 0)
        def _():
            out_ref[...] = part

        @pl.when(pl.program_id(2) > 0)
        def _():
            out_ref[...] += part

    return pl.pallas_call(
        body,
        grid=(m // tm, n // tn, n_k),
        in_specs=[pl.BlockSpec((ROW_TILE, tm), lambda i, j, k: (k, i)),
                  pl.BlockSpec((ROW_TILE, tn), lambda i, j, k: (k, j))],
        out_specs=pl.BlockSpec((tm, tn), lambda i, j, k: (i, j)),
        out_shape=jax.ShapeDtypeStruct((m, n), F32),
        compiler_params=_params(("arbitrary", "arbitrary", "arbitrary")),
        name=name,
    )(a, b)


def _adamw(w, g, m, v, name):
    n_rows, n_cols = w.shape
    tr = 256 if n_rows % 256 == 0 else n_rows

    def body(w_ref, g_ref, m_ref, v_ref, d_ref, nm_ref, nv_ref):
        gv = g_ref[...]
        nm = ADAM_B1 * m_ref[...] + (1.0 - ADAM_B1) * gv
        nv = ADAM_B2 * v_ref[...] + (1.0 - ADAM_B2) * (gv * gv)
        m_hat = nm / (1.0 - ADAM_B1 ** ADAM_STEP)
        v_hat = nv / (1.0 - ADAM_B2 ** ADAM_STEP)
        d_ref[...] = -ADAM_LR * (m_hat / (jnp.sqrt(v_hat) + ADAM_EPS) + ADAM_WD * w_ref[...])
        nm_ref[...] = nm
        nv_ref[...] = nv

    blk = pl.BlockSpec((tr, n_cols), lambda i: (i, 0))
    shape = jax.ShapeDtypeStruct((n_rows, n_cols), F32)
    return pl.pallas_call(
        body,
        grid=(n_rows // tr,),
        in_specs=[blk] * 4,
        out_specs=[blk] * 3,
        out_shape=[shape] * 3,
        compiler_params=_params(("arbitrary",)),
        name=name,
    )(w, g, m, v)


IN_WIDTH = 3600
BA_COL = 7 * HEAD_W


def _local_step(x, target, mod, norm_attn_g, w_in, rel_bias, conv_w, a_log, dt_bias, delta_norm_g, w_out,
                norm_ffn_g, w_gate, w_up, w_down, final_norm_g):
    sh1, sc1, g1, sh2, sc2, g2 = [mod[:, i * D_MODEL:(i + 1) * D_MODEL] for i in range(6)]
    w_a = w_in[:, :3 * HEAD_W]
    w_d = w_in[:, 3 * HEAD_W:BA_COL]
    w_ba = jnp.pad(w_in[:, BA_COL:], ((0, 0), (0, LANES - 2 * N_HEADS)))
    tables = jnp.asarray(_attn_tables())
    alog_row = jnp.pad(a_log, ((0, 0), (N_HEADS, LANES - 2 * N_HEADS)))
    dt_row = jnp.pad(dt_bias, ((0, 0), (N_HEADS, LANES - 2 * N_HEADS)))
    gain_row = jnp.tile(delta_norm_g, (1, N_HEADS))

    h1, qkv_a, qkvz, ba = _inproj_fwd(x, norm_attn_g, sc1, sh1, w_a, w_d, w_ba)
    y_attn, lse = _attention_fwd(qkv_a, rel_bias, tables)
    xh = _to_heads(_delta_prep_fwd(qkvz, ba, conv_w, alog_row, dt_row))
    inv_h, qk_h, u_h, w_h = _delta_chunk_fwd(xh)
    o_h, st_h = _delta_scan_fwd(xh, qk_h, u_h, w_h)
    o = _from_heads(o_h)
    y_delta = _delta_post_fwd(o, qkvz, gain_row)
    x1, h2, y = _outproj_fwd(y_attn, y_delta, w_out, x, g1, norm_ffn_g, sc2, sh2)
    gate, up, dx2, st_f = _ffn_fwd(h2, w_gate, w_up, w_down, x1, g2, final_norm_g, target)

    dgate, dup, act, dy2, dx1, dy, st_b = _ffn_bwd(dx2, gate, up, w_gate, w_up, w_down, x1, y, g2, g1, norm_ffn_g, sc2)
    grads = {
        "w_gate": _weight_grad(h2, dgate, "wgrad_gate"),
        "w_up": _weight_grad(h2, dup, "wgrad_up"),
        "w_down": _weight_grad(act, dy2, "wgrad_down"),
        "w_out": jnp.concatenate([_weight_grad(y_attn, dy, "wgrad_out_attn"),
                                  _weight_grad(y_delta, dy, "wgrad_out_delta")], axis=0),
    }
    dycat = _outproj_bwd(dy, w_out)
    do, dz, dgain = _delta_post_bwd(dycat, o, qkvz, gain_row)
    do_h = _to_heads(do)
    dsn_h, dvn_h = _delta_scan_bwd(xh, qk_h, w_h, do_h)
    dxs = _from_heads(_delta_chunk_bwd(xh, inv_h, u_h, w_h, st_h, dsn_h, dvn_h, do_h))
    dconv, dba, dvec = _delta_prep_bwd(qkvz, ba, conv_w, alog_row, dt_row, dxs)
    dxd, grads["conv_w"] = _conv_bwd(dconv, qkvz, conv_w)
    dq, dk, dv, dbias = _attention_bwd(qkv_a, dycat, y_attn, lse, rel_bias, tables)
    grad_x, st_i = _inproj_bwd(dq, dk, dv, dxd, dz, dba, w_a, w_d, w_ba, x, dx1, norm_attn_g, sc1)
    grads["w_in"] = jnp.concatenate(
        [_weight_grad(h1, dq, "wgrad_in_q"), _weight_grad(h1, dk, "wgrad_in_k"), _weight_grad(h1, dv, "wgrad_in_v"),
         _weight_grad(h1, dxd, "wgrad_in_delta"), _weight_grad(h1, dz, "wgrad_in_z"),
         _weight_grad(h1, dba, "wgrad_in_gates")[:, :2 * N_HEADS]], axis=1)
    grads["rel_bias"] = _rel_bias_grad(dbias, tables)[:, :N_BUCKETS].T
    grads["a_log"] = dvec[0:1, N_HEADS:2 * N_HEADS]
    grads["dt_bias"] = dvec[1:2, N_HEADS:2 * N_HEADS]
    grads["delta_norm_g"] = dgain[1:2, :HEAD_DIM]
    grads["norm_attn_g"] = st_i[2:3]
    grads["norm_ffn_g"] = st_b[2:3]
    grads["final_norm_g"] = st_f[0:1]
    dmod = jnp.concatenate([st_i[0:1], st_i[1:2], st_b[3:4], st_b[0:1], st_b[1:2], st_f[1:2]], axis=1)
    return st_f[3, 0], grad_x, grads, dmod


MESH = pl.DeviceIdType.MESH
OTHER_CHIPS = ((1, 0), (0, 1), (1, 1))
ALL_PEERS = tuple((m >> 2 & 1, m >> 1 & 1, m & 1) for m in range(1, 8))
ANY = pl.BlockSpec(memory_space=pl.ANY)
VMEM_SPEC = pl.BlockSpec(memory_space=pltpu.VMEM)
N_BIG = 5


def _me():
    return lax.axis_index("x"), lax.axis_index("y"), lax.axis_index("c")


def _flip(pos, mask):
    return tuple(1 - p if m else p for p, m in zip(pos, mask))


def _remote(src, dst, send_sems, recv_sems, k, to):
    return pltpu.make_async_remote_copy(src_ref=src, dst_ref=dst, send_sem=send_sems.at[k], recv_sem=recv_sems.at[k],
                                        device_id=to, device_id_type=MESH)


def _ada_exchange(c8, w_ada, b_ada, conv8):
    def body(c_ref, w_ref, b_ref, cv_ref, mod_ref, cact_ref, conv_ref, c_all, part_all, send_sems, recv_sems):
        x, y, c = me = _me()
        dev = 4 * x + 2 * y + c
        chip = 2 * x + y
        c_all[dev] = c_ref[...]
        conv_ref[chip] = cv_ref[...]
        first = [_remote(c_ref, c_all.at[dev], send_sems, recv_sems, k, _flip(me, mask))
                 for k, mask in enumerate(ALL_PEERS)]
        first += [_remote(cv_ref, conv_ref.at[chip], send_sems, recv_sems, 7 + j, _flip(me, (*mask, 0)))
                  for j, mask in enumerate(OTHER_CHIPS)]
        for cp in first:
            cp.start()
        for cp in first:
            cp.wait()
        row = lax.broadcasted_iota(jnp.int32, (8, D_MODEL), 0)
        c_rows = jnp.zeros((8, D_MODEL), F32)
        for d in range(8):
            c_rows = jnp.where(row == d, c_all[d], c_rows)
        c_act = _silu(c_rows)
        cact_ref[...] = c_act
        part_all[chip] = _nn(c_act, w_ref[...], HIGHEST)
        second = [_remote(part_all.at[chip], part_all.at[chip], send_sems, recv_sems, 10 + j, _flip(me, (*mask, 0)))
                  for j, mask in enumerate(OTHER_CHIPS)]
        for cp in second:
            cp.start()
        for cp in second:
            cp.wait()
        cols = w_ref.shape[1]
        for k in range(4):
            mod_ref[:, k * cols:(k + 1) * cols] = part_all[k] + b_ref[:, k * cols:(k + 1) * cols]

    cols = w_ada.shape[1]
    return pl.pallas_call(
        body,
        in_specs=[VMEM_SPEC] * 4,
        out_specs=[VMEM_SPEC] * 3,
        out_shape=[jax.ShapeDtypeStruct((8, 4 * cols), F32), jax.ShapeDtypeStruct((8, D_MODEL), F32),
                   jax.ShapeDtypeStruct((4, 8, conv8.shape[1]), F32)],
        scratch_shapes=[pltpu.VMEM((8, 8, D_MODEL), F32), pltpu.VMEM((4, 8, cols), F32),
                        pltpu.SemaphoreType.DMA((13,)), pltpu.SemaphoreType.DMA((13,))],
        compiler_params=pltpu.CompilerParams(vmem_limit_bytes=VMEM_LIMIT),
        name="ada_exchange",
    )(c8, w_ada, b_ada, conv8)


def _gather_weights(shards):
    def body(*refs):
        srcs, dsts = refs[:N_BIG], refs[N_BIG:2 * N_BIG]
        send_sems, recv_sems, local_sems = refs[2 * N_BIG:]
        x, y, c = me = _me()
        chip = 2 * x + y
        sibling = _flip(me, (0, 0, 1))
        local, first, passed = [], [], []
        for a in range(N_BIG):
            local.append(pltpu.make_async_copy(srcs[a], dsts[a].at[chip], local_sems.at[a]))
            for j, mask in enumerate(OTHER_CHIPS):
                to = _flip(me, (*mask, 0))
                first.append(_remote(srcs[a].at[c], dsts[a].at[chip, c], send_sems, recv_sems, 6 * a + j, to))
                landed = dsts[a].at[2 * to[0] + to[1], c]
                passed.append(_remote(landed, landed, send_sems, recv_sems, 6 * a + 3 + j, sibling))
        for cp in local + first:
            cp.start()
        for cp, fwd in zip(first, passed):
            cp.wait_recv()
            fwd.start()
        for cp in local:
            cp.wait()
        for cp in first:
            cp.wait_send()
        for fwd in passed:
            fwd.wait()

    return pl.pallas_call(
        body,
        in_specs=[ANY] * N_BIG,
        out_specs=[ANY] * N_BIG,
        out_shape=[jax.ShapeDtypeStruct((4, *s.shape), s.dtype) for s in shards],
        scratch_shapes=[pltpu.SemaphoreType.DMA((6 * N_BIG,)), pltpu.SemaphoreType.DMA((6 * N_BIG,)),
                        pltpu.SemaphoreType.DMA((N_BIG,))],
        name="gather_weights",
    )(*shards)


def _swap_halves(grads):
    def body(*refs):
        srcs, own, got = refs[:N_BIG], refs[N_BIG:2 * N_BIG], refs[2 * N_BIG:3 * N_BIG]
        send_sems, recv_sems, local_sems = refs[3 * N_BIG:]
        x, y, c = me = _me()
        copies = []
        for a in range(N_BIG):
            copies.append(pltpu.make_async_copy(srcs[a].at[:, c], own[a], local_sems.at[a]))
            copies.append(_remote(srcs[a].at[:, 1 - c], got[a], send_sems, recv_sems, a, _flip(me, (0, 0, 1))))
        for cp in copies:
            cp.start()
        for cp in copies:
            cp.wait()

    halves = [jax.ShapeDtypeStruct((4, g.shape[2], g.shape[3]), g.dtype) for g in grads]
    return pl.pallas_call(
        body,
        in_specs=[ANY] * N_BIG,
        out_specs=[ANY] * (2 * N_BIG),
        out_shape=halves + halves,
        scratch_shapes=[pltpu.SemaphoreType.DMA((N_BIG,)), pltpu.SemaphoreType.DMA((N_BIG,)),
                        pltpu.SemaphoreType.DMA((N_BIG,))],
        name="swap_halves",
    )(*grads)


def _scatter_partials(partials):
    def body(*refs):
        srcs, dsts = refs[:N_BIG], refs[N_BIG:2 * N_BIG]
        send_sems, recv_sems, local_sems = refs[2 * N_BIG:]
        x, y, c = me = _me()
        chip = 2 * x + y
        copies = []
        for a in range(N_BIG):
            copies.append(pltpu.make_async_copy(srcs[a].at[chip], dsts[a].at[chip], local_sems.at[a]))
            for j, mask in enumerate(OTHER_CHIPS):
                to = _flip(me, (*mask, 0))
                copies.append(_remote(srcs[a].at[2 * to[0] + to[1]], dsts[a].at[chip], send_sems, recv_sems, 3 * a + j, to))
        for cp in copies:
            cp.start()
        for cp in copies:
            cp.wait()

    return pl.pallas_call(
        body,
        in_specs=[ANY] * N_BIG,
        out_specs=[ANY] * N_BIG,
        out_shape=[jax.ShapeDtypeStruct(p.shape, p.dtype) for p in partials],
        scratch_shapes=[pltpu.SemaphoreType.DMA((3 * N_BIG,)), pltpu.SemaphoreType.DMA((3 * N_BIG,)),
                        pltpu.SemaphoreType.DMA((N_BIG,))],
        name="scatter_partials",
    )(*partials)


def _join_halves(halves):
    def body(*refs):
        srcs, dsts = refs[:N_BIG], refs[N_BIG:2 * N_BIG]
        send_sems, recv_sems, local_sems = refs[2 * N_BIG:]
        x, y, c = me = _me()
        copies = []
        for a in range(N_BIG):
            copies.append(pltpu.make_async_copy(srcs[a], dsts[a].at[c], local_sems.at[a]))
            copies.append(_remote(srcs[a], dsts[a].at[c], send_sems, recv_sems, a, _flip(me, (0, 0, 1))))
        for cp in copies:
            cp.start()
        for cp in copies:
            cp.wait()

    return pl.pallas_call(
        body,
        in_specs=[ANY] * N_BIG,
        out_specs=[ANY] * N_BIG,
        out_shape=[jax.ShapeDtypeStruct((2, *h.shape), h.dtype) for h in halves],
        scratch_shapes=[pltpu.SemaphoreType.DMA((N_BIG,)), pltpu.SemaphoreType.DMA((N_BIG,)),
                        pltpu.SemaphoreType.DMA((N_BIG,))],
        name="join_halves",
    )(*halves)


def _gather_small(packed):
    n_rows = packed.shape[0]

    def body(p_ref, all_ref, sum_ref, send_sems, recv_sems):
        x, y, c = me = _me()
        dev = 4 * x + 2 * y + c
        all_ref[dev] = p_ref[...]
        copies = [_remote(p_ref, all_ref.at[dev], send_sems, recv_sems, k, _flip(me, mask))
                  for k, mask in enumerate(ALL_PEERS)]
        for cp in copies:
            cp.start()
        for cp in copies:
            cp.wait()
        total = all_ref[0]
        for d in range(1, 8):
            total = total + all_ref[d]
        sum_ref[...] = total

    return pl.pallas_call(
        body,
        in_specs=[VMEM_SPEC],
        out_specs=[VMEM_SPEC, VMEM_SPEC],
        out_shape=[jax.ShapeDtypeStruct((8, n_rows, LANES), F32), jax.ShapeDtypeStruct((n_rows, LANES), F32)],
        scratch_shapes=[pltpu.SemaphoreType.DMA((7,)), pltpu.SemaphoreType.DMA((7,))],
        name="gather_small",
    )(packed)


def _add_pair(a, b, out_dtype, name):
    def body(a_ref, b_ref, o_ref):
        o_ref[...] = (a_ref[...] + b_ref[...]).astype(o_ref.dtype)

    blk = pl.BlockSpec((1, *a.shape[1:]), lambda i: (i, 0, 0))
    return pl.pallas_call(
        body, grid=(a.shape[0],), in_specs=[blk, blk], out_specs=blk,
        out_shape=jax.ShapeDtypeStruct(a.shape, out_dtype),
        compiler_params=_params(("arbitrary",)), name=name,
    )(a, b)


def _add_slots(a, name):
    def body(a_ref, o_ref):
        total = a_ref[0].astype(F32)
        for k in range(1, 4):
            total = total + a_ref[k].astype(F32)
        o_ref[...] = total

    return pl.pallas_call(
        body, in_specs=[VMEM_SPEC], out_specs=VMEM_SPEC,
        out_shape=jax.ShapeDtypeStruct(a.shape[1:], F32),
        compiler_params=pltpu.CompilerParams(vmem_limit_bytes=VMEM_LIMIT), name=name,
    )(a)


def _ada_weight_grad(c_act, dmod_cols):
    def body(c_ref, d_ref, o_ref):
        o_ref[...] = _tn(c_ref[...], d_ref[...], HIGHEST)

    return pl.pallas_call(
        body, in_specs=[VMEM_SPEC, VMEM_SPEC], out_specs=VMEM_SPEC,
        out_shape=jax.ShapeDtypeStruct((c_act.shape[1], dmod_cols.shape[1]), F32),
        compiler_params=pltpu.CompilerParams(vmem_limit_bytes=VMEM_LIMIT), name="ada_weight_grad",
    )(c_act, dmod_cols)


def kernel(x, c, w_ada, b_ada, norm_attn_g, w_in, rel_bias, conv_w, a_log, dt_bias, delta_norm_g, w_out, norm_ffn_g, w_gate, w_up, w_down, final_norm_g, loss_target, m_w_ada, m_b_ada, m_norm_attn_g, m_w_in, m_rel_bias, m_conv_w, m_a_log, m_dt_bias, m_delta_norm_g, m_w_out, m_norm_ffn_g, m_w_gate, m_w_up, m_w_down, m_final_norm_g, v_w_ada, v_b_ada, v_norm_attn_g, v_w_in, v_rel_bias, v_conv_w, v_a_log, v_dt_bias, v_delta_norm_g, v_w_out, v_norm_ffn_g, v_w_gate, v_w_up, v_w_down, v_final_norm_g):
    xi, yi, ci = _me()
    dev = 4 * xi + 2 * yi + ci
    chip = 2 * xi + yi

    conv_cols = conv_w.shape[2]
    mod_all, c_act, conv_all = _ada_exchange(jnp.broadcast_to(c, (8, D_MODEL)), w_ada[0], b_ada,
                                             jnp.pad(conv_w[0], ((0, 4), (0, 0))))
    mod = lax.dynamic_slice_in_dim(mod_all, dev, 1, axis=0)
    conv_full = jnp.swapaxes(conv_all[:, :4, :], 0, 1).reshape(4, 4 * conv_cols)

    big = [w_in[0], w_out[0], w_gate[0], w_up[0], w_down[0]]
    by_cols = [True, False, True, True, False]
    gathered = _gather_weights([w.astype(BF16).reshape(2, w.size // (2 * LANES), LANES) for w in big])
    gathered = [g.reshape(4, *w.shape) for g, w in zip(gathered, big)]
    whole = [jnp.swapaxes(g, 0, 1).reshape(g.shape[1], 4 * g.shape[2]) if cols else g.reshape(4 * g.shape[1], g.shape[2])
             for g, cols in zip(gathered, by_cols)]

    loss, grad_x, grads, dmod = _local_step(
        x[0], loss_target[0], mod, norm_attn_g, whole[0], rel_bias, conv_full, a_log, dt_bias, delta_norm_g,
        whole[1], norm_ffn_g, whole[2], whole[3], whole[4], final_norm_g[None])

    slots = []
    for name, w, cols in zip(("w_in", "w_out", "w_gate", "w_up", "w_down"), big, by_cols):
        rows, ncol = w.shape
        g = grads[name]
        g = jnp.swapaxes(g.reshape(rows, 4, ncol), 0, 1) if cols else g.reshape(4, rows, ncol)
        slots.append(g.reshape(4, 2, rows * ncol // (2 * LANES), LANES))
    swapped = _swap_halves(slots)
    partials = []
    for a, (own, got) in enumerate(zip(swapped[:N_BIG], swapped[N_BIG:])):
        partials.append(_add_pair(own, got, BF16, f"add_pair_{a}"))
    by_source = _scatter_partials(partials)
    joined = _join_halves([_add_slots(p, f"add_slots_{a}") for a, p in enumerate(by_source)])
    big_grads = [j.reshape(w.shape) for j, w in zip(joined, big)]

    pieces = [dmod, grads["conv_w"], grads["norm_attn_g"], grads["norm_ffn_g"], grads["final_norm_g"],
              grads["rel_bias"], grads["a_log"], grads["dt_bias"], grads["delta_norm_g"]]
    flat = [jnp.pad(p.reshape(-1), (0, -p.size % LANES)) for p in pieces]
    n_rows = [f.size // LANES for f in flat]
    packed = jnp.concatenate(flat).reshape(-1, LANES)
    packed = jnp.pad(packed, ((0, -packed.shape[0] % 8), (0, 0)))
    all_small, total = _gather_small(packed)
    sums, start = [], 0
    for p, n in zip(pieces, n_rows):
        sums.append(total[start:start + n].reshape(-1)[:p.size].reshape(p.shape))
        start += n
    g_b_ada, g_conv, g_norm_attn, g_norm_ffn, g_final, g_rel, g_alog, g_dt, g_dnorm = sums
    dmod_all = all_small[:, :n_rows[0], :].reshape(8, -1)
    ada_cols = w_ada.shape[2]
    g_w_ada = _ada_weight_grad(c_act, lax.dynamic_slice_in_dim(dmod_all, chip * ada_cols, ada_cols, axis=1))
    g_conv = lax.dynamic_slice_in_dim(g_conv, chip * conv_cols, conv_cols, axis=1)

    grad = {"w_ada": g_w_ada[None], "b_ada": g_b_ada, "norm_attn_g": g_norm_attn, "w_in": big_grads[0][None],
            "rel_bias": g_rel, "conv_w": g_conv[None], "a_log": g_alog, "dt_bias": g_dt, "delta_norm_g": g_dnorm,
            "w_out": big_grads[1][None], "norm_ffn_g": g_norm_ffn, "w_gate": big_grads[2][None],
            "w_up": big_grads[3][None], "w_down": big_grads[4][None], "final_norm_g": g_final.reshape(-1)}
    weight = {"w_ada": w_ada, "b_ada": b_ada, "norm_attn_g": norm_attn_g, "w_in": w_in, "rel_bias": rel_bias,
              "conv_w": conv_w, "a_log": a_log, "dt_bias": dt_bias, "delta_norm_g": delta_norm_g, "w_out": w_out,
              "norm_ffn_g": norm_ffn_g, "w_gate": w_gate, "w_up": w_up, "w_down": w_down, "final_norm_g": final_norm_g}
    first = {"w_ada": m_w_ada, "b_ada": m_b_ada, "norm_attn_g": m_norm_attn_g, "w_in": m_w_in, "rel_bias": m_rel_bias,
             "conv_w": m_conv_w, "a_log": m_a_log, "dt_bias": m_dt_bias, "delta_norm_g": m_delta_norm_g,
             "w_out": m_w_out, "norm_ffn_g": m_norm_ffn_g, "w_gate": m_w_gate, "w_up": m_w_up, "w_down": m_w_down,
             "final_norm_g": m_final_norm_g}
    second = {"w_ada": v_w_ada, "b_ada": v_b_ada, "norm_attn_g": v_norm_attn_g, "w_in": v_w_in, "rel_bias": v_rel_bias,
              "conv_w": v_conv_w, "a_log": v_a_log, "dt_bias": v_dt_bias, "delta_norm_g": v_delta_norm_g,
              "w_out": v_w_out, "norm_ffn_g": v_norm_ffn_g, "w_gate": v_w_gate, "w_up": v_w_up, "w_down": v_w_down,
              "final_norm_g": v_final_norm_g}
    delta, new_m, new_v = {}, {}, {}
    for name, w in weight.items():
        two_d = (-1, w.shape[-1])
        d, nm, nv = _adamw(w.reshape(two_d), grad[name].reshape(two_d), first[name].reshape(two_d),
                           second[name].reshape(two_d), f"adamw_{name}")
        delta[name], new_m[name], new_v[name] = d.reshape(w.shape), nm.reshape(w.shape), nv.reshape(w.shape)

    names = list(weight)
    return (lax.psum(loss, ("x", "y", "c")), grad_x[None], *[grad[n] for n in names], *[delta[n] for n in names],
            *[new_m[n] for n in names], *[new_v[n] for n in names])
```

```python
import functools
import math

import numpy as np
import jax
import jax.numpy as jnp
from jax import lax
from jax.experimental import pallas as pl
from jax.experimental.pallas import tpu as pltpu

F32 = jnp.float32
BF16 = jnp.bfloat16
HIGHEST = lax.Precision.HIGHEST

D_MODEL = 1024
HEAD_DIM = 64
N_HEADS = 8
HEAD_W = 512
BRANCHES = ((128, 1), (512, 4), (2048, 16))
BAND = 128
ATT_TILE = 2048
ATT_UNROLL = 2
N_BUCKETS = 32
MAX_DISTANCE = 2048
CHUNK = 64
D_FF = 2816
EPS = 1e-6
NEG_INF = -1e30
LANES = 128
VMEM_LIMIT = 56 * 1024 * 1024

ADAM_LR = 0.001
ADAM_B1 = 0.9
ADAM_B2 = 0.999
ADAM_EPS = 1e-08
ADAM_WD = 0.01
ADAM_STEP = 10


def _nn(a, b, precision=None):
    return jnp.dot(a, b, preferred_element_type=F32, precision=precision)


def _nt(a, b, precision=None):
    return lax.dot_general(a, b, (((1,), (1,)), ((), ())), preferred_element_type=F32, precision=precision)


def _tn(a, b, precision=None):
    return lax.dot_general(a, b, (((0,), (0,)), ((), ())), preferred_element_type=F32, precision=precision)


def _params(sem, vmem=VMEM_LIMIT):
    return pltpu.CompilerParams(dimension_semantics=sem, vmem_limit_bytes=vmem)


def _sigmoid(x):
    return 1.0 / (1.0 + jnp.exp(-x))


def _silu(x):
    return x * _sigmoid(x)


def _dsilu(x):
    s = _sigmoid(x)
    return s * (1.0 + x * (1.0 - s))


def _attn_tables():
    qi = np.arange(BAND)[:, None]
    kj = np.arange(2 * BAND)[None, :]
    steps = qi + BAND - kj
    in_window = (steps >= 0) & (steps <= BAND)
    max_exact = N_BUCKETS // 2
    out = np.zeros((3, 2, BAND, 2 * BAND), np.int32)
    for b, (_, dil) in enumerate(BRANCHES):
        dist = np.maximum(steps, 0) * dil
        dist_f = np.maximum(dist, 1).astype(np.float32)
        large = max_exact + (np.log(dist_f / np.float32(max_exact)) / np.float32(math.log(MAX_DISTANCE / max_exact))
                             * np.float32(N_BUCKETS - max_exact)).astype(np.int32)
        bucket = np.where(dist < max_exact, dist, np.minimum(large, N_BUCKETS - 1)).astype(np.int32)
        out[b, 0] = np.where(in_window, bucket, -1)
        out[b, 1] = np.where(in_window & (kj >= BAND), bucket, -1)
    return out


def _attn_bias_tables(rel_ref, tab_ref, bias_s, pair):
    for b in range(3):
        for first in range(2):
            tab = tab_ref[b, first]
            for hh in range(2):
                head = 2 * pair + hh

                def pick(kk, acc, tab=tab, head=head):
                    return jnp.where(tab == kk, rel_ref[kk, head], acc)

                acc = lax.fori_loop(0, N_BUCKETS, pick, jnp.zeros((BAND, 2 * BAND), F32))
                bias_s[b, hh, first] = jnp.where(tab < 0, NEG_INF, acc)


def _attn_block_index(idx, t, r):
    nb = ATT_TILE // (BAND * r)
    rho = idx // nb
    n = idx % nb
    qs = rho + r * BAND * n
    gs = t * ATT_TILE + qs
    first = (t * nb + n) == 0
    ps = jnp.where(first, gs, gs - r * BAND)
    return qs, gs, ps, first.astype(jnp.int32)


def _rows(start, r):
    return pl.ds(start, BAND) if r == 1 else pl.ds(start, BAND, stride=r)


def _attention_fwd(qkv, rel_bias, tables):
    seq = qkv.shape[0]
    n_tiles = seq // ATT_TILE

    def body(rel_ref, tab_ref, q_ref, k_ref, v_ref, y_ref, lse_ref, bias_s, o_s, l_s):
        pair = pl.program_id(0)
        t = pl.program_id(1)
        lane = lax.broadcasted_iota(jnp.int32, (1, LANES), 1)
        head0 = lane < HEAD_DIM

        @pl.when(t == 0)
        def _():
            _attn_bias_tables(rel_ref, tab_ref, bias_s, pair)

        masks = (head0, jnp.logical_not(head0))
        for b, (_, r) in enumerate(BRANCHES):
            def blocks(it, carry, b=b, r=r):
                idx = [_attn_block_index(it * ATT_UNROLL + j, t, r) for j in range(ATT_UNROLL)]
                qb = [q_ref[_rows(qs, r), :] * (HEAD_DIM ** -0.5) for qs, _, _, _ in idx]
                kcat = [jnp.concatenate([k_ref[_rows(ps, r), :], k_ref[_rows(gs, r), :]], axis=0).astype(BF16)
                        for _, gs, ps, _ in idx]
                vcat = [jnp.concatenate([v_ref[_rows(ps, r), :], v_ref[_rows(gs, r), :]], axis=0).astype(BF16)
                        for _, gs, ps, _ in idx]
                work = [(j, hh) for j in range(ATT_UNROLL) for hh in range(2)]
                s = [_nt(jnp.where(masks[hh], qb[j], 0.0).astype(BF16), kcat[j]) + bias_s[b, hh, idx[j][3]]
                     for j, hh in work]
                m = [jnp.max(sv, axis=-1, keepdims=True) for sv in s]
                e = [jnp.exp(sv - mv) for sv, mv in zip(s, m)]
                den = [jnp.sum(ev, axis=-1, keepdims=True) for ev in e]
                out = [_nn(ev.astype(BF16), vcat[j]) / dv for ev, dv, (j, _) in zip(e, den, work)]
                lse = [mv + jnp.log(dv) for mv, dv in zip(m, den)]
                for j in range(ATT_UNROLL):
                    o_s[b, _rows(idx[j][0], r), :] = jnp.where(head0, out[2 * j], out[2 * j + 1])
                    l_s[b, _rows(idx[j][0], r), :] = jnp.where(head0, lse[2 * j], lse[2 * j + 1])
                return carry

            lax.fori_loop(0, ATT_TILE // BAND // ATT_UNROLL, blocks, 0)

        def merge(i, carry):
            rows = pl.ds(pl.multiple_of(i * BAND, BAND), BAND)
            l0, l1, l2 = l_s[0, rows, :], l_s[1, rows, :], l_s[2, rows, :]
            m = jnp.maximum(jnp.maximum(l0, l1), l2)
            w0, w1, w2 = jnp.exp(l0 - m), jnp.exp(l1 - m), jnp.exp(l2 - m)
            tot = w0 + w1 + w2
            y_ref[rows, :] = (w0 * o_s[0, rows, :] + w1 * o_s[1, rows, :] + w2 * o_s[2, rows, :]) / tot
            lse_ref[rows, :] = m + jnp.log(tot)
            return carry

        lax.fori_loop(0, ATT_TILE // BAND, merge, 0)

    tile = pl.BlockSpec((ATT_TILE, LANES), lambda p, t: (t, p))
    return pl.pallas_call(
        body,
        grid=(N_HEADS // 2, n_tiles),
        in_specs=[
            pl.BlockSpec(memory_space=pltpu.SMEM),
            pl.BlockSpec((3, 2, BAND, 2 * BAND), lambda p, t: (0, 0, 0, 0)),
            pl.BlockSpec((ATT_TILE, LANES), lambda p, t: (t, p)),
            pl.BlockSpec((seq, LANES), lambda p, t: (0, 4 + p)),
            pl.BlockSpec((seq, LANES), lambda p, t: (0, 8 + p)),
        ],
        out_specs=[tile, tile],
        out_shape=[jax.ShapeDtypeStruct((seq, HEAD_W), F32), jax.ShapeDtypeStruct((seq, HEAD_W), F32)],
        scratch_shapes=[
            pltpu.VMEM((3, 2, 2, BAND, 2 * BAND), F32),
            pltpu.VMEM((3, ATT_TILE, LANES), F32),
            pltpu.VMEM((3, ATT_TILE, LANES), F32),
        ],
        compiler_params=_params(("arbitrary", "arbitrary")),
        name="attn_fwd",
    )(rel_bias, tables, qkv, qkv, qkv)


def _attention_bwd(qkv, dy, y, lse, rel_bias, tables):
    seq = qkv.shape[0]
    n_tiles = seq // ATT_TILE

    def body(rel_ref, tab_ref, q_ref, k_ref, v_ref, dy_ref, y_ref, lse_ref,
             dq_ref, dk_ref, dv_ref, dbias_ref, bias_s):
        pair = pl.program_id(0)
        t = pl.program_id(1)
        lane = lax.broadcasted_iota(jnp.int32, (1, LANES), 1)
        head0 = lane < HEAD_DIM

        @pl.when(t == 0)
        def _():
            _attn_bias_tables(rel_ref, tab_ref, bias_s, pair)
            dk_ref[...] = jnp.zeros_like(dk_ref)
            dv_ref[...] = jnp.zeros_like(dv_ref)
            dbias_ref[...] = jnp.zeros_like(dbias_ref)

        dq_ref[...] = jnp.zeros_like(dq_ref)

        masks = (head0, jnp.logical_not(head0))
        scale = HEAD_DIM ** -0.5
        for b, (_, r) in enumerate(BRANCHES):
            def blocks(it, carry, b=b, r=r):
                idx = [_attn_block_index(it * ATT_UNROLL + j, t, r) for j in range(ATT_UNROLL)]
                qb = [q_ref[_rows(qs, r), :] * scale for qs, _, _, _ in idx]
                kcat = [jnp.concatenate([k_ref[_rows(ps, r), :], k_ref[_rows(gs, r), :]], axis=0).astype(BF16)
                        for _, gs, ps, _ in idx]
                vcat = [jnp.concatenate([v_ref[_rows(ps, r), :], v_ref[_rows(gs, r), :]], axis=0).astype(BF16)
                        for _, gs, ps, _ in idx]
                dob = [dy_ref[_rows(qs, r), :] for qs, _, _, _ in idx]
                ob = [y_ref[_rows(qs, r), :] for qs, _, _, _ in idx]
                lb = [lse_ref[_rows(qs, r), :] for qs, _, _, _ in idx]
                work = [(j, hh) for j in range(ATT_UNROLL) for hh in range(2)]
                qh = [jnp.where(masks[hh], qb[j], 0.0).astype(BF16) for j, hh in work]
                doh = [jnp.where(masks[hh], dob[j], 0.0) for j, hh in work]
                dohb = [d.astype(BF16) for d in doh]
                s = [_nt(qh[w], kcat[j]) + bias_s[b, hh, idx[j][3]] for w, (j, hh) in enumerate(work)]
                dp = [_nt(dohb[w], vcat[j]) for w, (j, _) in enumerate(work)]
                lcol = [jnp.max(jnp.where(masks[hh], lb[j], -jnp.inf), axis=-1, keepdims=True) for j, hh in work]
                delta = [jnp.sum(doh[w] * ob[j], axis=-1, keepdims=True) for w, (j, _) in enumerate(work)]
                prob = [jnp.exp(sv - lv) for sv, lv in zip(s, lcol)]
                ds = [pv * (dv - de) for pv, dv, de in zip(prob, dp, delta)]
                dsb = [d.astype(BF16) for d in ds]
                dq = [_nn(dsb[w], kcat[j]) for w, (j, _) in enumerate(work)]
                dkc = [_tn(dsb[w], qh[w]) for w in range(len(work))]
                dvc = [_tn(prob[w].astype(BF16), dohb[w]) for w in range(len(work))]
                for w, (j, hh) in enumerate(work):
                    dbias_ref[0, b, hh] += ds[w]
                for j in range(ATT_UNROLL):
                    qs, gs, ps, _ = idx[j]
                    dkcat = dkc[2 * j] + dkc[2 * j + 1]
                    dvcat = dvc[2 * j] + dvc[2 * j + 1]
                    dq_ref[_rows(qs, r), :] += jnp.where(head0, dq[2 * j], dq[2 * j + 1]) * scale
                    dk_ref[_rows(ps, r), :] += dkcat[:BAND]
                    dk_ref[_rows(gs, r), :] += dkcat[BAND:]
                    dv_ref[_rows(ps, r), :] += dvcat[:BAND]
                    dv_ref[_rows(gs, r), :] += dvcat[BAND:]
                return carry

            lax.fori_loop(0, ATT_TILE // BAND // ATT_UNROLL, blocks, 0)

    tile = pl.BlockSpec((ATT_TILE, LANES), lambda p, t: (t, p))
    full = pl.BlockSpec((seq, LANES), lambda p, t: (0, p))
    return pl.pallas_call(
        body,
        grid=(N_HEADS // 2, n_tiles),
        in_specs=[
            pl.BlockSpec(memory_space=pltpu.SMEM),
            pl.BlockSpec((3, 2, BAND, 2 * BAND), lambda p, t: (0, 0, 0, 0)),
            pl.BlockSpec((ATT_TILE, LANES), lambda p, t: (t, p)),
            pl.BlockSpec((seq, LANES), lambda p, t: (0, 4 + p)),
            pl.BlockSpec((seq, LANES), lambda p, t: (0, 8 + p)),
            tile, tile, tile,
        ],
        out_specs=[tile, full, full,
                   pl.BlockSpec((1, 3, 2, BAND, 2 * BAND), lambda p, t: (p, 0, 0, 0, 0))],
        out_shape=[jax.ShapeDtypeStruct((seq, HEAD_W), F32)] * 3
        + [jax.ShapeDtypeStruct((N_HEADS // 2, 3, 2, BAND, 2 * BAND), F32)],
        scratch_shapes=[pltpu.VMEM((3, 2, 2, BAND, 2 * BAND), F32)],
        compiler_params=_params(("arbitrary", "arbitrary")),
        name="attn_bwd",
    )(rel_bias, tables, qkv, qkv, qkv, dy, y, lse)


def _rel_bias_grad(dbias, tables):
    def body(tab_ref, db_ref, out_ref):
        lane = lax.broadcasted_iota(jnp.int32, (1, LANES), 1)
        for pair in range(N_HEADS // 2):
            for hh in range(2):
                row = jnp.zeros((1, LANES), F32)
                for b in range(3):
                    tab = tab_ref[b, 0]
                    d = db_ref[pair, b, hh]

                    def bucket_sum(kk, row, tab=tab, d=d):
                        s = jnp.sum(jnp.where(tab == kk, d, 0.0), keepdims=True)
                        return row + jnp.where(lane == kk, s, 0.0)

                    row = lax.fori_loop(0, N_BUCKETS, bucket_sum, row)
                out_ref[pl.ds(2 * pair + hh, 1), :] = row

    return pl.pallas_call(
        body,
        out_shape=jax.ShapeDtypeStruct((N_HEADS, LANES), F32),
        compiler_params=pltpu.CompilerParams(vmem_limit_bytes=VMEM_LIMIT),
        name="rel_bias_grad",
    )(tables, dbias)


ROW_TILE = 512


def _head_sum_matrix():
    return (lax.broadcasted_iota(jnp.int32, (HEAD_W, LANES), 0) // HEAD_DIM
            == lax.broadcasted_iota(jnp.int32, (HEAD_W, LANES), 1)).astype(F32)


def _head_spread_matrix(offset=0):
    return (lax.broadcasted_iota(jnp.int32, (LANES, HEAD_W), 0)
            == lax.broadcasted_iota(jnp.int32, (LANES, HEAD_W), 1) // HEAD_DIM + offset).astype(F32)


def _head_gather_matrix(offset=0):
    return (lax.broadcasted_iota(jnp.int32, (HEAD_W, LANES), 0) // HEAD_DIM + offset
            == lax.broadcasted_iota(jnp.int32, (HEAD_W, LANES), 1)).astype(F32)


def _split3(x):
    hi = x.astype(BF16)
    rest = x - hi.astype(F32)
    mid = rest.astype(BF16)
    return hi, mid, (rest - mid.astype(F32)).astype(BF16)


def _pick(x, onehot):
    m = onehot.astype(BF16)
    hi, mid, lo = _split3(x)
    return _nn(hi, m) + (_nn(mid, m) + _nn(lo, m))


def _pick_left(onehot, x):
    m = onehot.astype(BF16)
    hi, mid, lo = _split3(x)
    return _nn(m, hi) + (_nn(m, mid) + _nn(m, lo))


def _tri(lower, strict=False):
    r = lax.broadcasted_iota(jnp.int32, (CHUNK, CHUNK), 0)
    c = lax.broadcasted_iota(jnp.int32, (CHUNK, CHUNK), 1)
    if lower:
        return (c < r) if strict else (c <= r)
    return c >= r


def _softplus(z):
    return jnp.maximum(z, 0.0) + jnp.log(1.0 + jnp.exp(-jnp.abs(z)))


def _conv_taps(stage, w_ref, rows):
    return (w_ref[3:4, :] * stage[8:8 + rows, :] + w_ref[2:3, :] * stage[7:7 + rows, :]
            + w_ref[1:2, :] * stage[6:6 + rows, :] + w_ref[0:1, :] * stage[5:5 + rows, :])


def _l2_scale(xc, hsum, hspread):
    ssq = _pick(xc * xc, hsum)
    return _pick(lax.rsqrt(ssq + EPS), hspread)


def _stage_rows(stage, x_ref, xp_ref, i):
    stage[0:8, :] = jnp.where(i == 0, 0.0, xp_ref[...])
    stage[8:8 + ROW_TILE, :] = x_ref[...]


def _delta_prep_fwd(qkvz, ba, conv_w, alog_row, dt_row):
    seq = qkvz.shape[0]
    qkv_w = 3 * HEAD_W

    def body(x_ref, xp_ref, ba_ref, w_ref, al_ref, dt_ref, out_ref, stage):
        i = pl.program_id(0)
        _stage_rows(stage, x_ref, xp_ref, i)
        act = _silu(_conv_taps(stage, w_ref, ROW_TILE))
        hsum, hspread = _head_sum_matrix(), _head_spread_matrix()
        qc, kc = act[:, :HEAD_W], act[:, HEAD_W:2 * HEAD_W]
        out_ref[0] = qc * _l2_scale(qc, hsum, hspread) * (HEAD_DIM ** -0.5)
        out_ref[1] = kc * _l2_scale(kc, hsum, hspread)
        out_ref[2] = act[:, 2 * HEAD_W:]
        bav = ba_ref[...]
        out_ref[3] = _pick(_sigmoid(bav), hspread)
        g8 = -jnp.exp(al_ref[...]) * _softplus(bav + dt_ref[...])
        gb = _pick(g8, _head_spread_matrix(N_HEADS))
        cum = _tri(True).astype(F32)
        for ch in range(ROW_TILE // CHUNK):
            rows = slice(ch * CHUNK, (ch + 1) * CHUNK)
            out_ref[4, rows, :] = _pick_left(cum, gb[rows])

    return pl.pallas_call(
        body,
        grid=(seq // ROW_TILE,),
        in_specs=[
            pl.BlockSpec((ROW_TILE, qkv_w), lambda i: (i, 0)),
            pl.BlockSpec((8, qkv_w), lambda i: (jnp.maximum(i * (ROW_TILE // 8) - 1, 0), 0)),
            pl.BlockSpec((ROW_TILE, LANES), lambda i: (i, 0)),
            pl.BlockSpec((4, qkv_w), lambda i: (0, 0)),
            pl.BlockSpec((1, LANES), lambda i: (0, 0)),
            pl.BlockSpec((1, LANES), lambda i: (0, 0)),
        ],
        out_specs=pl.BlockSpec((5, ROW_TILE, HEAD_W), lambda i: (0, i, 0)),
        out_shape=jax.ShapeDtypeStruct((5, seq, HEAD_W), F32),
        scratch_shapes=[pltpu.VMEM((ROW_TILE + 8, qkv_w), F32)],
        compiler_params=_params(("arbitrary",)),
        name="delta_prep_fwd",
    )(qkvz, qkvz, ba, conv_w, alog_row, dt_row)


def _split(x):
    hi = x.astype(BF16)
    return hi, (x - hi.astype(F32)).astype(BF16)


def _dot3(a, b, dot=_nn):
    return dot(a[0], b[0]) + (dot(a[0], b[1]) + dot(a[1], b[0]))


def _unit_lower_inverses(mats):
    eye = (lax.broadcasted_iota(jnp.int32, (CHUNK, CHUNK), 0)
           == lax.broadcasted_iota(jnp.int32, (CHUNK, CHUNK), 1)).astype(F32)
    invs = [eye - a for a in mats]
    powers = [_split(a) for a in mats]
    for step in range(5):
        squares = [_dot3(p, p) for p in powers]
        powers = [_split(s) for s in squares]
        invs = [inv + _dot3(_split(inv), p) for inv, p in zip(invs, powers)]
    return invs


def _chunk_terms(q, k, v, beta, gc):
    causal, strict = _tri(True), _tri(True, strict=True)
    e = jnp.exp(gc)
    g_last = jnp.broadcast_to(gc[CHUNK - 1:CHUNK, :], (CHUNK, CHUNK))
    f = jnp.exp(g_last - gc)
    e_last = jnp.exp(g_last)
    decay = jnp.where(causal, jnp.exp(jnp.where(causal, gc - gc.T, 0.0)), 0.0)
    kb = k * beta
    a_mat = jnp.where(strict, _nt(kb.astype(BF16), k.astype(BF16)) * decay, 0.0)
    qk = jnp.where(causal, _nt(q.astype(BF16), k.astype(BF16)) * decay, 0.0)
    return e, f, e_last, decay, kb, a_mat, qk


GROUP = 8
UNROLL = 8


def _chunk_rows(ci):
    return pl.ds(pl.multiple_of(ci * CHUNK, CHUNK), CHUNK)


def _delta_chunk_fwd(xh):
    seq = xh.shape[2]
    rows_per_step = GROUP * CHUNK

    def body(x_ref, inv_ref, qk_ref, u_ref, w_ref):
        def group(gi, carry):
            rows = [_chunk_rows(gi * UNROLL + step) for step in range(UNROLL)]
            xs = [[x_ref[j, 0, r, :] for j in range(5)] for r in rows]
            terms = [_chunk_terms(*x) for x in xs]
            invs = _unit_lower_inverses([t[5] for t in terms])
            for r, x, t, inv in zip(rows, xs, terms, invs):
                e, kb, qk = t[0], t[4], t[6]
                inv_parts = _split(inv)
                inv_ref[0, r, :] = inv
                qk_ref[0, r, :] = qk
                u_ref[0, r, :] = _dot3(inv_parts, _split(x[2] * x[3]))
                w_ref[0, r, :] = _dot3(inv_parts, _split(kb * e))
            return carry

        lax.fori_loop(0, GROUP // UNROLL, group, 0)

    out = pl.BlockSpec((1, rows_per_step, HEAD_DIM), lambda h, g: (h, g, 0))
    return pl.pallas_call(
        body,
        grid=(N_HEADS, seq // rows_per_step),
        in_specs=[pl.BlockSpec((5, 1, rows_per_step, HEAD_DIM), lambda h, g: (0, h, g, 0))],
        out_specs=[out] * 4,
        out_shape=[jax.ShapeDtypeStruct((N_HEADS, seq, HEAD_DIM), F32)] * 4,
        compiler_params=_params(("parallel", "parallel")),
        name="delta_chunk_fwd",
    )(xh)


def _decays(gc):
    g_last = jnp.broadcast_to(gc[CHUNK - 1:CHUNK, :], (CHUNK, CHUNK))
    return jnp.exp(gc), jnp.exp(g_last - gc), jnp.exp(g_last)


def _head_blocks(index, n_steps=None):
    rows_per_step = GROUP * CHUNK
    if n_steps is None:
        return pl.BlockSpec((1, N_HEADS, rows_per_step, HEAD_DIM), lambda g: (index, 0, g, 0))
    return pl.BlockSpec((1, N_HEADS, rows_per_step, HEAD_DIM), lambda g: (index, 0, n_steps - 1 - g, 0))


def _delta_scan_fwd(xh, qk_h, u_h, w_h):
    seq = xh.shape[2]
    rows_per_step = GROUP * CHUNK

    def body(q_ref, k_ref, gc_ref, qk_ref, u_ref, w_ref, o_ref, st_ref, state):
        @pl.when(pl.program_id(0) == 0)
        def _():
            state[...] = jnp.zeros_like(state)

        def chunk(ci, carry):
            rows = _chunk_rows(ci)
            heads = range(N_HEADS)
            dec = [_decays(gc_ref[0, h, rows, :]) for h in heads]
            s = [state[h] for h in heads]
            sb = [s[h].astype(BF16) for h in heads]
            vnb = [(u_ref[h, rows, :] - _nn(w_ref[h, rows, :].astype(BF16), sb[h])).astype(BF16) for h in heads]
            for h in heads:
                o_ref[h, rows, :] = (_nn((q_ref[0, h, rows, :] * dec[h][0]).astype(BF16), sb[h])
                                     + _nn(qk_ref[h, rows, :].astype(BF16), vnb[h]))
                st_ref[h, rows, :] = s[h]
            for h in heads:
                state[h] = s[h] * dec[h][2] + _tn((k_ref[0, h, rows, :] * dec[h][1]).astype(BF16), vnb[h])
            return carry

        lax.fori_loop(0, GROUP, chunk, 0)

    blk = pl.BlockSpec((N_HEADS, rows_per_step, HEAD_DIM), lambda g: (0, g, 0))
    return pl.pallas_call(
        body,
        grid=(seq // rows_per_step,),
        in_specs=[_head_blocks(0), _head_blocks(1), _head_blocks(4), blk, blk, blk],
        out_specs=[blk, blk],
        out_shape=[jax.ShapeDtypeStruct((N_HEADS, seq, HEAD_DIM), F32)] * 2,
        scratch_shapes=[pltpu.VMEM((N_HEADS, CHUNK, CHUNK), F32)],
        compiler_params=_params(("arbitrary",)),
        name="delta_scan_fwd",
    )(xh, xh, xh, qk_h, u_h, w_h)


def _delta_scan_bwd(xh, qk_h, w_h, do_h):
    seq = xh.shape[2]
    rows_per_step = GROUP * CHUNK
    n_steps = seq // rows_per_step

    def body(q_ref, k_ref, gc_ref, qk_ref, w_ref, do_ref, dsn_ref, dvn_ref, dstate):
        @pl.when(pl.program_id(0) == 0)
        def _():
            dstate[...] = jnp.zeros_like(dstate)

        def chunk(step, carry):
            rows = _chunk_rows(GROUP - 1 - step)
            heads = range(N_HEADS)
            dec = [_decays(gc_ref[0, h, rows, :]) for h in heads]
            ds_next = [dstate[h] for h in heads]
            dob = [do_ref[h, rows, :].astype(BF16) for h in heads]
            dv_new = [_tn(qk_ref[h, rows, :].astype(BF16), dob[h])
                      + _nn((k_ref[0, h, rows, :] * dec[h][1]).astype(BF16), ds_next[h].astype(BF16)) for h in heads]
            for h in heads:
                dsn_ref[h, rows, :] = ds_next[h]
                dvn_ref[h, rows, :] = dv_new[h]
            for h in heads:
                dstate[h] = (_tn((q_ref[0, h, rows, :] * dec[h][0]).astype(BF16), dob[h]) + dec[h][2] * ds_next[h]
                             - _tn(w_ref[h, rows, :].astype(BF16), dv_new[h].astype(BF16)))
            return carry

        lax.fori_loop(0, GROUP, chunk, 0)

    blk = pl.BlockSpec((N_HEADS, rows_per_step, HEAD_DIM), lambda g: (0, n_steps - 1 - g, 0))
    return pl.pallas_call(
        body,
        grid=(n_steps,),
        in_specs=[_head_blocks(0, n_steps), _head_blocks(1, n_steps), _head_blocks(4, n_steps), blk, blk, blk],
        out_specs=[blk, blk],
        out_shape=[jax.ShapeDtypeStruct((N_HEADS, seq, HEAD_DIM), F32)] * 2,
        scratch_shapes=[pltpu.VMEM((N_HEADS, CHUNK, CHUNK), F32)],
        compiler_params=_params(("arbitrary",)),
        name="delta_scan_bwd",
    )(xh, xh, xh, qk_h, w_h, do_h)


def _delta_chunk_bwd(xh, inv_h, u_h, w_h, st_h, dsn_h, dvn_h, do_h):
    seq = xh.shape[2]
    rows_per_step = GROUP * CHUNK

    def body(x_ref, inv_ref, u_ref, w_ref, st_ref, dsn_ref, dvn_ref, do_ref, dx_ref):
        causal, strict = _tri(True), _tri(True, strict=True)
        last_row = lax.broadcasted_iota(jnp.int32, (CHUNK, CHUNK), 0) == CHUNK - 1

        def bf(vals):
            return [val.astype(BF16) for val in vals]

        def group(gi, carry):
            rows = [_chunk_rows(gi * UNROLL + step) for step in range(UNROLL)]
            n = range(UNROLL)
            q, k, v, beta, gc = [[x_ref[j, 0, r, :] for r in rows] for j in range(5)]
            terms = [_chunk_terms(q[i], k[i], v[i], beta[i], gc[i]) for i in n]
            e, f, e_last, decay, kb, a_mat, qk = [[t[j] for t in terms] for j in range(7)]
            inv = [_split(inv_ref[0, r, :]) for r in rows]
            u = [u_ref[0, r, :] for r in rows]
            w = [w_ref[0, r, :] for r in rows]
            s = [st_ref[0, r, :] for r in rows]
            ds_next = [dsn_ref[0, r, :] for r in rows]
            dv_new = [dvn_ref[0, r, :] for r in rows]
            sb, dsb, dvb, wb = bf(s), bf(ds_next), bf(dv_new), bf(w)
            dob = bf([do_ref[0, r, :] for r in rows])
            qbf, kbf, kbb = bf(q), bf(k), bf(kb)
            vnb = bf([u[i] - _nn(wb[i], sb[i]) for i in n])
            dqe = [_nt(dob[i], sb[i]) for i in n]
            dw = [-_nt(dvb[i], sb[i]) for i in n]
            dkf = [_nt(vnb[i], dsb[i]) for i in n]
            dqk = [jnp.where(causal, _nt(dob[i], vnb[i]), 0.0) for i in n]
            drhs_u = [_dot3(inv[i], _split(dv_new[i]), _tn) for i in n]
            drhs_w = [_dot3(inv[i], _split(dw[i]), _tn) for i in n]
            da = [-jnp.where(strict, _nt(drhs_u[i].astype(BF16), u[i].astype(BF16))
                             + _nt(drhs_w[i].astype(BF16), wb[i]), 0.0) for i in n]
            dad = bf([da[i] * decay[i] for i in n])
            dqd = bf([dqk[i] * decay[i] for i in n])
            dkb = [e[i] * drhs_w[i] + _nn(dad[i], kbf[i]) for i in n]
            dk = [_tn(dad[i], kbb[i]) + _tn(dqd[i], qbf[i]) + f[i] * dkf[i] + beta[i] * dkb[i] for i in n]
            dq = [_nn(dqd[i], kbf[i]) + e[i] * dqe[i] for i in n]
            for i in n:
                de_full = kb[i] * drhs_w[i] + q[i] * dqe[i]
                df_full = k[i] * dkf[i]
                m = da[i] * a_mat[i] + dqk[i] * qk[i]
                dgc = de_full * e[i] - df_full * f[i] + m - m.T
                tail = jnp.sum(df_full * f[i] + s[i] * ds_next[i] * e_last[i], axis=0, keepdims=True)
                dgc = dgc + jnp.where(last_row, jnp.broadcast_to(tail, (CHUNK, CHUNK)), 0.0)
                dx_ref[0, 0, rows[i], :] = dq[i]
                dx_ref[1, 0, rows[i], :] = dk[i]
                dx_ref[2, 0, rows[i], :] = beta[i] * drhs_u[i]
                dx_ref[3, 0, rows[i], :] = v[i] * drhs_u[i] + k[i] * dkb[i]
                dx_ref[4, 0, rows[i], :] = dgc
            return carry

        lax.fori_loop(0, GROUP // UNROLL, group, 0)

    blk = pl.BlockSpec((1, rows_per_step, HEAD_DIM), lambda h, g: (h, g, 0))
    blk5 = pl.BlockSpec((5, 1, rows_per_step, HEAD_DIM), lambda h, g: (0, h, g, 0))
    return pl.pallas_call(
        body,
        grid=(N_HEADS, seq // rows_per_step),
        in_specs=[blk5] + [blk] * 7,
        out_specs=blk5,
        out_shape=jax.ShapeDtypeStruct((5, N_HEADS, seq, HEAD_DIM), F32),
        compiler_params=_params(("parallel", "parallel")),
        name="delta_chunk_bwd",
    )(xh, inv_h, u_h, w_h, st_h, dsn_h, dvn_h, do_h)


def _delta_post_fwd(o, qkvz, gain_row):
    seq = o.shape[0]

    def body(o_ref, z_ref, g_ref, y_ref):
        ov = o_ref[...]
        ms = _pick(ov * ov, _head_sum_matrix()) * (1.0 / HEAD_DIM)
        rb = _pick(lax.rsqrt(ms + EPS), _head_spread_matrix())
        y_ref[...] = (ov * rb * g_ref[...] * _silu(z_ref[...])).astype(y_ref.dtype)

    tile = pl.BlockSpec((ROW_TILE, HEAD_W), lambda i: (i, 0))
    return pl.pallas_call(
        body,
        grid=(seq // ROW_TILE,),
        in_specs=[tile, pl.BlockSpec((ROW_TILE, HEAD_W), lambda i: (i, 3)), pl.BlockSpec((1, HEAD_W), lambda i: (0, 0))],
        out_specs=tile,
        out_shape=jax.ShapeDtypeStruct((seq, HEAD_W), BF16),
        compiler_params=_params(("arbitrary",)),
        name="delta_post_fwd",
    )(o, qkvz, gain_row)


def _delta_post_bwd(dy, o, qkvz, gain_row):
    seq = o.shape[0]

    def body(dy_ref, o_ref, z_ref, g_ref, do_ref, dz_ref, dg_ref):
        @pl.when(pl.program_id(0) == 0)
        def _():
            dg_ref[...] = jnp.zeros_like(dg_ref)

        ov, zv, dyv, gain = o_ref[...], z_ref[...], dy_ref[...], g_ref[...]
        hsum, hspread = _head_sum_matrix(), _head_spread_matrix()
        ms = _pick(ov * ov, hsum) * (1.0 / HEAD_DIM)
        rb = _pick(lax.rsqrt(ms + EPS), hspread)
        ohat = ov * rb
        dz_ref[...] = dyv * ohat * gain * _dsilu(zv)
        dn = dyv * _silu(zv)
        dg_ref[0:1, :] += jnp.sum(dn * ohat, axis=0, keepdims=True)
        dohat = dn * gain

        @pl.when(pl.program_id(0) == pl.num_programs(0) - 1)
        def _():
            fold = (lax.broadcasted_iota(jnp.int32, (HEAD_W, HEAD_W), 0) % HEAD_DIM
                    == lax.broadcasted_iota(jnp.int32, (HEAD_W, HEAD_W), 1)).astype(F32)
            dg_ref[1:2, :] = _pick(dg_ref[0:1, :], fold)

        proj = _pick(_pick(dohat * ohat, hsum) * (1.0 / HEAD_DIM), hspread)
        do_ref[...] = rb * (dohat - ohat * proj)

    tile = pl.BlockSpec((ROW_TILE, HEAD_W), lambda i: (i, 0))
    return pl.pallas_call(
        body,
        grid=(seq // ROW_TILE,),
        in_specs=[pl.BlockSpec((ROW_TILE, HEAD_W), lambda i: (i, 1)), tile,
                  pl.BlockSpec((ROW_TILE, HEAD_W), lambda i: (i, 3)), pl.BlockSpec((1, HEAD_W), lambda i: (0, 0))],
        out_specs=[tile, tile, pl.BlockSpec((2, HEAD_W), lambda i: (0, 0))],
        out_shape=[jax.ShapeDtypeStruct((seq, HEAD_W), F32), jax.ShapeDtypeStruct((seq, HEAD_W), F32),
                   jax.ShapeDtypeStruct((2, HEAD_W), F32)],
        compiler_params=_params(("arbitrary",)),
        name="delta_post_bwd",
    )(dy, o, qkvz, gain_row)


def _delta_prep_bwd(qkvz, ba, conv_w, alog_row, dt_row, dxs):
    seq = qkvz.shape[0]
    qkv_w = 3 * HEAD_W

    def body(x_ref, xp_ref, ba_ref, w_ref, al_ref, dt_ref, dx_ref, dconv_ref, dba_ref, dvec_ref, stage):
        i = pl.program_id(0)

        @pl.when(i == 0)
        def _():
            dvec_ref[...] = jnp.zeros_like(dvec_ref)

        _stage_rows(stage, x_ref, xp_ref, i)
        pre = _conv_taps(stage, w_ref, ROW_TILE)
        act = _silu(pre)
        slope = _dsilu(pre)
        hsum, hspread = _head_sum_matrix(), _head_spread_matrix()
        for j, scale in ((0, HEAD_DIM ** -0.5), (1, 1.0)):
            cols = slice(j * HEAD_W, (j + 1) * HEAD_W)
            xc = act[:, cols]
            rb = _l2_scale(xc, hsum, hspread)
            xhat = xc * rb
            dhat = dx_ref[j] * scale
            proj = _pick(_pick(dhat * xhat, hsum), hspread)
            dconv_ref[:, cols] = rb * (dhat - xhat * proj) * slope[:, cols]
        dconv_ref[:, 2 * HEAD_W:] = dx_ref[2] * slope[:, 2 * HEAD_W:]

        bav = ba_ref[...]
        beta8 = _sigmoid(bav)
        dbeta8 = _pick(dx_ref[3], _head_gather_matrix())
        dgc8 = _pick(dx_ref[4], _head_gather_matrix(N_HEADS))
        rev = _tri(False).astype(F32)
        z = bav + dt_ref[...]
        ea = jnp.exp(al_ref[...])
        g8 = -ea * _softplus(z)
        sig = _sigmoid(z)
        d_alog = jnp.zeros((1, LANES), F32)
        d_dt = jnp.zeros((1, LANES), F32)
        for ch in range(ROW_TILE // CHUNK):
            rows = slice(ch * CHUNK, (ch + 1) * CHUNK)
            dg8 = _pick_left(rev, dgc8[rows])
            da = -dg8 * ea * sig[rows]
            dba_ref[rows, :] = dbeta8[rows] * beta8[rows] * (1.0 - beta8[rows]) + da
            d_alog = d_alog + jnp.sum(dg8 * g8[rows], axis=0, keepdims=True)
            d_dt = d_dt + jnp.sum(da, axis=0, keepdims=True)
        dvec_ref[0:1, :] += d_alog
        dvec_ref[1:2, :] += d_dt

    return pl.pallas_call(
        body,
        grid=(seq // ROW_TILE,),
        in_specs=[
            pl.BlockSpec((ROW_TILE, qkv_w), lambda i: (i, 0)),
            pl.BlockSpec((8, qkv_w), lambda i: (jnp.maximum(i * (ROW_TILE // 8) - 1, 0), 0)),
            pl.BlockSpec((ROW_TILE, LANES), lambda i: (i, 0)),
            pl.BlockSpec((4, qkv_w), lambda i: (0, 0)),
            pl.BlockSpec((1, LANES), lambda i: (0, 0)),
            pl.BlockSpec((1, LANES), lambda i: (0, 0)),
            pl.BlockSpec((5, ROW_TILE, HEAD_W), lambda i: (0, i, 0)),
        ],
        out_specs=[pl.BlockSpec((ROW_TILE, qkv_w), lambda i: (i, 0)),
                   pl.BlockSpec((ROW_TILE, LANES), lambda i: (i, 0)),
                   pl.BlockSpec((2, LANES), lambda i: (0, 0))],
        out_shape=[jax.ShapeDtypeStruct((seq, qkv_w), F32), jax.ShapeDtypeStruct((seq, LANES), F32),
                   jax.ShapeDtypeStruct((2, LANES), F32)],
        scratch_shapes=[pltpu.VMEM((ROW_TILE + 8, qkv_w), F32)],
        compiler_params=_params(("arbitrary",)),
        name="delta_prep_bwd",
    )(qkvz, qkvz, ba, conv_w, alog_row, dt_row, dxs)


def _conv_bwd(dconv, qkvz, conv_w):
    seq = dconv.shape[0]
    qkv_w = 3 * HEAD_W
    n_tiles = seq // ROW_TILE

    def body(dy_ref, dyn_ref, x_ref, xp_ref, w_ref, dx_ref, dw_ref, stage, dstage):
        i = pl.program_id(0)

        @pl.when(i == 0)
        def _():
            dw_ref[...] = jnp.zeros_like(dw_ref)

        _stage_rows(stage, x_ref, xp_ref, i)
        dstage[0:ROW_TILE, :] = dy_ref[...]
        dstage[ROW_TILE:ROW_TILE + 8, :] = jnp.where(i == n_tiles - 1, 0.0, dyn_ref[...])
        dy = dy_ref[...]
        dx_ref[...] = (w_ref[3:4, :] * dy + w_ref[2:3, :] * dstage[1:1 + ROW_TILE, :]
                       + w_ref[1:2, :] * dstage[2:2 + ROW_TILE, :] + w_ref[0:1, :] * dstage[3:3 + ROW_TILE, :])
        for j in range(4):
            dw_ref[j:j + 1, :] += jnp.sum(dy * stage[5 + j:5 + j + ROW_TILE, :], axis=0, keepdims=True)

    tile = pl.BlockSpec((ROW_TILE, qkv_w), lambda i: (i, 0))
    return pl.pallas_call(
        body,
        grid=(n_tiles,),
        in_specs=[
            tile,
            pl.BlockSpec((8, qkv_w), lambda i: (jnp.minimum((i + 1) * (ROW_TILE // 8), seq // 8 - 1), 0)),
            tile,
            pl.BlockSpec((8, qkv_w), lambda i: (jnp.maximum(i * (ROW_TILE // 8) - 1, 0), 0)),
            pl.BlockSpec((4, qkv_w), lambda i: (0, 0)),
        ],
        out_specs=[tile, pl.BlockSpec((4, qkv_w), lambda i: (0, 0))],
        out_shape=[jax.ShapeDtypeStruct((seq, qkv_w), F32), jax.ShapeDtypeStruct((4, qkv_w), F32)],
        scratch_shapes=[pltpu.VMEM((ROW_TILE + 8, qkv_w), F32), pltpu.VMEM((ROW_TILE + 8, qkv_w), F32)],
        compiler_params=_params(("arbitrary",)),
        name="conv_bwd",
    )(dconv, dconv, qkvz, qkvz, conv_w)


def _to_heads(a):
    lead = a.shape[:-2]
    seq = a.shape[-2]
    a = a.reshape(*lead, seq, N_HEADS, HEAD_DIM)
    return jnp.swapaxes(a, -2, -3)


def _from_heads(a):
    a = jnp.swapaxes(a, -2, -3)
    return a.reshape(*a.shape[:-2], HEAD_W)


FF_TILE = 1408


def _row(a):
    return pl.BlockSpec((1, a), lambda *_: (0, 0))


def _rms_fwd(xv, gain):
    rstd = lax.rsqrt(jnp.mean(xv * xv, axis=-1, keepdims=True) + EPS)
    xhat = xv * rstd
    return xhat, rstd, xhat * gain


def _rms_bwd(dnorm, xhat, rstd, gain):
    dxhat = dnorm * gain
    dx = rstd * (dxhat - xhat * jnp.mean(dxhat * xhat, axis=-1, keepdims=True))
    return dx, jnp.sum(dnorm * xhat, axis=0, keepdims=True)


def _inproj_fwd(x, gain, scale, shift, w_a, w_d, w_ba):
    seq = x.shape[0]

    def body(x_ref, g_ref, sc_ref, sh_ref, wa_ref, wd_ref, wb_ref, h_ref, a_ref, d_ref, b_ref):
        _, _, norm = _rms_fwd(x_ref[...], g_ref[...])
        h = (norm * (1.0 + sc_ref[...]) + sh_ref[...]).astype(BF16)
        h_ref[...] = h
        a_ref[...] = _nn(h, wa_ref[...])
        d_ref[...] = _nn(h, wd_ref[...])
        b_ref[...] = _nn(h, wb_ref[...])

    def rows(width):
        return pl.BlockSpec((ROW_TILE, width), lambda i: (i, 0))

    def whole(a):
        return pl.BlockSpec(a.shape, lambda i: (0, 0))

    return pl.pallas_call(
        body,
        grid=(seq // ROW_TILE,),
        in_specs=[rows(D_MODEL), _row(D_MODEL), _row(D_MODEL), _row(D_MODEL), whole(w_a), whole(w_d), whole(w_ba)],
        out_specs=[rows(D_MODEL), rows(3 * HEAD_W), rows(4 * HEAD_W), rows(LANES)],
        out_shape=[jax.ShapeDtypeStruct((seq, D_MODEL), BF16), jax.ShapeDtypeStruct((seq, 3 * HEAD_W), F32),
                   jax.ShapeDtypeStruct((seq, 4 * HEAD_W), F32), jax.ShapeDtypeStruct((seq, LANES), F32)],
        compiler_params=_params(("arbitrary",)),
        name="inproj_fwd",
    )(x, gain, scale, shift, w_a, w_d, w_ba)


def _outproj_fwd(y_attn, y_delta, w_out, x, gate1, gain, scale, shift):
    seq = x.shape[0]

    def body(ya_ref, yd_ref, wa_ref, wd_ref, x_ref, g1_ref, g_ref, sc_ref, sh_ref, x1_ref, h_ref, y_ref):
        y = _nn(ya_ref[...].astype(BF16), wa_ref[...]) + _nn(yd_ref[...], wd_ref[...])
        x1 = x_ref[...] + g1_ref[...] * y
        _, _, norm = _rms_fwd(x1, g_ref[...])
        x1_ref[...] = x1
        h_ref[...] = (norm * (1.0 + sc_ref[...]) + sh_ref[...]).astype(BF16)
        y_ref[...] = y.astype(BF16)

    def rows(width):
        return pl.BlockSpec((ROW_TILE, width), lambda i: (i, 0))

    return pl.pallas_call(
        body,
        grid=(seq // ROW_TILE,),
        in_specs=[rows(HEAD_W), rows(HEAD_W),
                  pl.BlockSpec((HEAD_W, D_MODEL), lambda i: (0, 0)), pl.BlockSpec((HEAD_W, D_MODEL), lambda i: (1, 0)),
                  rows(D_MODEL), _row(D_MODEL), _row(D_MODEL), _row(D_MODEL), _row(D_MODEL)],
        out_specs=[rows(D_MODEL), rows(D_MODEL), rows(D_MODEL)],
        out_shape=[jax.ShapeDtypeStruct((seq, D_MODEL), F32), jax.ShapeDtypeStruct((seq, D_MODEL), BF16),
                   jax.ShapeDtypeStruct((seq, D_MODEL), BF16)],
        compiler_params=_params(("arbitrary",)),
        name="outproj_fwd",
    )(y_attn, y_delta, w_out, w_out, x, gate1, gain, scale, shift)


def _ffn_fwd(h2, w_gate, w_up, w_down, x1, gate2, final_gain, target):
    seq = h2.shape[0]
    n_rows, n_ff = seq // ROW_TILE, D_FF // FF_TILE

    def body(h_ref, wg_ref, wu_ref, wd_ref, x1_ref, g2_ref, gf_ref, t_ref, gate_ref, up_ref, dx2_ref, st_ref, acc):
        i, j = pl.program_id(0), pl.program_id(1)

        @pl.when((i == 0) & (j == 0))
        def _():
            st_ref[...] = jnp.zeros_like(st_ref)

        h = h_ref[...]
        gate = _nn(h, wg_ref[...])
        up = _nn(h, wu_ref[...])
        gate_ref[...] = gate.astype(BF16)
        up_ref[...] = up.astype(BF16)
        part = _nn((_silu(gate) * up).astype(BF16), wd_ref[...])

        @pl.when(j == 0)
        def _():
            acc[...] = part

        @pl.when(j > 0)
        def _():
            acc[...] += part

        @pl.when(j == n_ff - 1)
        def _():
            y2 = acc[...]
            x2 = x1_ref[...] + g2_ref[...] * y2
            xhat, rstd, out = _rms_fwd(x2, gf_ref[...])
            diff = out - t_ref[...]
            dx2, dgain = _rms_bwd(diff * (1.0 / D_MODEL), xhat, rstd, gf_ref[...])
            dx2_ref[...] = dx2
            st_ref[0:1, :] += dgain
            st_ref[1:2, :] += jnp.sum(dx2 * y2, axis=0, keepdims=True)
            st_ref[2:3, :] += jnp.sum(diff * diff, axis=0, keepdims=True) * (0.5 / D_MODEL)

        @pl.when((i == n_rows - 1) & (j == n_ff - 1))
        def _():
            st_ref[3:4, :] = jnp.broadcast_to(jnp.sum(st_ref[2:3, :], keepdims=True), (1, D_MODEL))

    def rows(width):
        return pl.BlockSpec((ROW_TILE, width), lambda i, j: (i, 0))

    ff = pl.BlockSpec((ROW_TILE, FF_TILE), lambda i, j: (i, j))
    return pl.pallas_call(
        body,
        grid=(n_rows, n_ff),
        in_specs=[rows(D_MODEL),
                  pl.BlockSpec((D_MODEL, FF_TILE), lambda i, j: (0, j)), pl.BlockSpec((D_MODEL, FF_TILE), lambda i, j: (0, j)),
                  pl.BlockSpec((FF_TILE, D_MODEL), lambda i, j: (j, 0)),
                  rows(D_MODEL), _row(D_MODEL), _row(D_MODEL), rows(D_MODEL)],
        out_specs=[ff, ff, rows(D_MODEL), pl.BlockSpec((8, D_MODEL), lambda i, j: (0, 0))],
        out_shape=[jax.ShapeDtypeStruct((seq, D_FF), BF16), jax.ShapeDtypeStruct((seq, D_FF), BF16),
                   jax.ShapeDtypeStruct((seq, D_MODEL), F32), jax.ShapeDtypeStruct((8, D_MODEL), F32)],
        scratch_shapes=[pltpu.VMEM((ROW_TILE, D_MODEL), F32)],
        compiler_params=_params(("arbitrary", "arbitrary")),
        name="ffn_fwd",
    )(h2, w_gate, w_up, w_down, x1, gate2, final_gain, target)


def _ffn_bwd(dx2, gate, up, w_gate, w_up, w_down, x1, y, gate2, gate1, gain, scale):
    seq = dx2.shape[0]

    def act_body(dx2_ref, g2_ref, gate_ref, up_ref, wd_ref, dgate_ref, dup_ref, act_ref, dy2_ref):
        dy2 = (g2_ref[...] * dx2_ref[...]).astype(BF16)
        dy2_ref[...] = dy2
        gate = gate_ref[...].astype(F32)
        up = up_ref[...].astype(F32)
        dact = _nt(dy2, wd_ref[...])
        silu = _silu(gate)
        act_ref[...] = (silu * up).astype(BF16)
        dgate_ref[...] = (dact * up * _dsilu(gate)).astype(BF16)
        dup_ref[...] = (dact * silu).astype(BF16)

    def rows2(width):
        return pl.BlockSpec((ROW_TILE, width), lambda i, j: (i, 0))

    ff = pl.BlockSpec((ROW_TILE, FF_TILE), lambda i, j: (i, j))
    dgate, dup, act, dy2 = pl.pallas_call(
        act_body,
        grid=(seq // ROW_TILE, D_FF // FF_TILE),
        in_specs=[rows2(D_MODEL), _row(D_MODEL), ff, ff, pl.BlockSpec((FF_TILE, D_MODEL), lambda i, j: (j, 0))],
        out_specs=[ff, ff, ff, rows2(D_MODEL)],
        out_shape=[jax.ShapeDtypeStruct((seq, D_FF), BF16)] * 3 + [jax.ShapeDtypeStruct((seq, D_MODEL), BF16)],
        compiler_params=_params(("arbitrary", "arbitrary")),
        name="ffn_bwd_act",
    )(dx2, gate2, gate, up, w_down)

    def in_body(dgate_ref, dup_ref, wg_ref, wu_ref, dx2_ref, x1_ref, y_ref, g1_ref, g_ref, sc_ref,
                dx1_ref, dy_ref, st_ref):
        @pl.when(pl.program_id(0) == 0)
        def _():
            st_ref[...] = jnp.zeros_like(st_ref)

        dh = _nt(dgate_ref[...], wg_ref[...]) + _nt(dup_ref[...], wu_ref[...])
        xhat, rstd, norm = _rms_fwd(x1_ref[...], g_ref[...])
        dxn, dgain = _rms_bwd(dh * (1.0 + sc_ref[...]), xhat, rstd, g_ref[...])
        dx1 = dx2_ref[...] + dxn
        dx1_ref[...] = dx1
        dy_ref[...] = (g1_ref[...] * dx1).astype(BF16)
        st_ref[0:1, :] += jnp.sum(dh, axis=0, keepdims=True)
        st_ref[1:2, :] += jnp.sum(dh * norm, axis=0, keepdims=True)
        st_ref[2:3, :] += dgain
        st_ref[3:4, :] += jnp.sum(dx1 * y_ref[...].astype(F32), axis=0, keepdims=True)

    half_tile = ROW_TILE // 2

    def rows(width):
        return pl.BlockSpec((half_tile, width), lambda i: (i, 0))

    whole = pl.BlockSpec((D_MODEL, D_FF), lambda i: (0, 0))
    dx1, dy, stats = pl.pallas_call(
        in_body,
        grid=(seq // half_tile,),
        in_specs=[rows(D_FF), rows(D_FF), whole, whole, rows(D_MODEL), rows(D_MODEL), rows(D_MODEL),
                  _row(D_MODEL), _row(D_MODEL), _row(D_MODEL)],
        out_specs=[rows(D_MODEL), rows(D_MODEL), pl.BlockSpec((8, D_MODEL), lambda i: (0, 0))],
        out_shape=[jax.ShapeDtypeStruct((seq, D_MODEL), F32), jax.ShapeDtypeStruct((seq, D_MODEL), BF16),
                   jax.ShapeDtypeStruct((8, D_MODEL), F32)],
        compiler_params=_params(("arbitrary",)),
        name="ffn_bwd_in",
    )(dgate, dup, w_gate, w_up, dx2, x1, y, gate1, gain, scale)
    return dgate, dup, act, dy2, dx1, dy, stats


def _outproj_bwd(dy, w_out):
    seq = dy.shape[0]

    def body(dy_ref, w_ref, out_ref):
        out_ref[...] = _nt(dy_ref[...], w_ref[...])

    rows = pl.BlockSpec((ROW_TILE, D_MODEL), lambda i: (i, 0))
    return pl.pallas_call(
        body,
        grid=(seq // ROW_TILE,),
        in_specs=[rows, pl.BlockSpec((D_MODEL, D_MODEL), lambda i: (0, 0))],
        out_specs=rows,
        out_shape=jax.ShapeDtypeStruct((seq, D_MODEL), F32),
        compiler_params=_params(("arbitrary",)),
        name="outproj_bwd",
    )(dy, w_out)


def _inproj_bwd(dq, dk, dv, dxd, dz, dba, w_a, w_d, w_ba, x, dx1, gain, scale):
    seq = x.shape[0]

    def body(dq_ref, dk_ref, dv_ref, dxd_ref, dz_ref, dba_ref, wa_ref, wd_ref, wb_ref, x_ref, dx1_ref, g_ref, sc_ref,
             gx_ref, st_ref):
        @pl.when(pl.program_id(0) == 0)
        def _():
            st_ref[...] = jnp.zeros_like(st_ref)

        dh = (_nt(dq_ref[...].astype(BF16), wa_ref[:, 0:HEAD_W])
              + _nt(dk_ref[...].astype(BF16), wa_ref[:, HEAD_W:2 * HEAD_W])
              + _nt(dv_ref[...].astype(BF16), wa_ref[:, 2 * HEAD_W:])
              + _nt(dxd_ref[...].astype(BF16), wd_ref[:, 0:3 * HEAD_W])
              + _nt(dz_ref[...].astype(BF16), wd_ref[:, 3 * HEAD_W:])
              + _nt(dba_ref[...].astype(BF16), wb_ref[...]))
        xhat, rstd, norm = _rms_fwd(x_ref[...], g_ref[...])
        dxn, dgain = _rms_bwd(dh * (1.0 + sc_ref[...]), xhat, rstd, g_ref[...])
        gx_ref[...] = dx1_ref[...] + dxn
        st_ref[0:1, :] += jnp.sum(dh, axis=0, keepdims=True)
        st_ref[1:2, :] += jnp.sum(dh * norm, axis=0, keepdims=True)
        st_ref[2:3, :] += dgain

    def rows(width):
        return pl.BlockSpec((ROW_TILE, width), lambda i: (i, 0))

    def whole(a):
        return pl.BlockSpec(a.shape, lambda i: (0, 0))

    return pl.pallas_call(
        body,
        grid=(seq // ROW_TILE,),
        in_specs=[rows(HEAD_W), rows(HEAD_W), rows(HEAD_W), rows(3 * HEAD_W), rows(HEAD_W), rows(LANES),
                  whole(w_a), whole(w_d), whole(w_ba), rows(D_MODEL), rows(D_MODEL), _row(D_MODEL), _row(D_MODEL)],
        out_specs=[rows(D_MODEL), pl.BlockSpec((8, D_MODEL), lambda i: (0, 0))],
        out_shape=[jax.ShapeDtypeStruct((seq, D_MODEL), F32), jax.ShapeDtypeStruct((8, D_MODEL), F32)],
        compiler_params=_params(("arbitrary",)),
        name="inproj_bwd",
    )(dq, dk, dv, dxd, dz, dba, w_a, w_d, w_ba, x, dx1, gain, scale)


def _weight_grad(a, b, name):
    seq, m = a.shape
    n = b.shape[1]
    tm = m if m <= 1536 else m // 2
    tn = n if n <= 1536 else n // 2
    n_k = seq // ROW_TILE

    def body(a_ref, b_ref, out_ref):
        part = _tn(a_ref[...].astype(BF16), b_ref[...].astype(BF16))

        @pl.when(pl.program_id(2) == 0)
        def _():
            out_ref[...] = part

        @pl.when(pl.program_id(2) > 0)
        def _():
            out_ref[...] += part

    return pl.pallas_call(
        body,
        grid=(m // tm, n // tn, n_k),
        in_specs=[pl.BlockSpec((ROW_TILE, tm), lambda i, j, k: (k, i)),
                  pl.BlockSpec((ROW_TILE, tn), lambda i, j, k: (k, j))],
        out_specs=pl.BlockSpec((tm, tn), lambda i, j, k: (i, j)),
        out_shape=jax.ShapeDtypeStruct((m, n), F32),
        compiler_params=_params(("arbitrary", "arbitrary", "arbitrary")),
        name=name,
    )(a, b)


def _adamw(w, g, m, v, name):
    n_rows, n_cols = w.shape
    tr = 256 if n_rows % 256 == 0 else n_rows

    def body(w_ref, g_ref, m_ref, v_ref, d_ref, nm_ref, nv_ref):
        gv = g_ref[...]
        nm = ADAM_B1 * m_ref[...] + (1.0 - ADAM_B1) * gv
        nv = ADAM_B2 * v_ref[...] + (1.0 - ADAM_B2) * (gv * gv)
        m_hat = nm / (1.0 - ADAM_B1 ** ADAM_STEP)
        v_hat = nv / (1.0 - ADAM_B2 ** ADAM_STEP)
        d_ref[...] = -ADAM_LR * (m_hat / (jnp.sqrt(v_hat) + ADAM_EPS) + ADAM_WD * w_ref[...])
        nm_ref[...] = nm
        nv_ref[...] = nv

    blk = pl.BlockSpec((tr, n_cols), lambda i: (i, 0))
    shape = jax.ShapeDtypeStruct((n_rows, n_cols), F32)
    return pl.pallas_call(
        body,
        grid=(n_rows // tr,),
        in_specs=[blk] * 4,
        out_specs=[blk] * 3,
        out_shape=[shape] * 3,
        compiler_params=_params(("arbitrary",)),
        name=name,
    )(w, g, m, v)


IN_WIDTH = 3600
BA_COL = 7 * HEAD_W


def _local_step(x, target, mod, norm_attn_g, w_in, rel_bias, conv_w, a_log, dt_bias, delta_norm_g, w_out,
                norm_ffn_g, w_gate, w_up, w_down, final_norm_g):
    sh1, sc1, g1, sh2, sc2, g2 = [mod[:, i * D_MODEL:(i + 1) * D_MODEL] for i in range(6)]
    w_a = w_in[:, :3 * HEAD_W]
    w_d = w_in[:, 3 * HEAD_W:BA_COL]
    w_ba = jnp.pad(w_in[:, BA_COL:], ((0, 0), (0, LANES - 2 * N_HEADS)))
    tables = jnp.asarray(_attn_tables())
    alog_row = jnp.pad(a_log, ((0, 0), (N_HEADS, LANES - 2 * N_HEADS)))
    dt_row = jnp.pad(dt_bias, ((0, 0), (N_HEADS, LANES - 2 * N_HEADS)))
    gain_row = jnp.tile(delta_norm_g, (1, N_HEADS))

    h1, qkv_a, qkvz, ba = _inproj_fwd(x, norm_attn_g, sc1, sh1, w_a, w_d, w_ba)
    y_attn, lse = _attention_fwd(qkv_a, rel_bias, tables)
    xh = _to_heads(_delta_prep_fwd(qkvz, ba, conv_w, alog_row, dt_row))
    inv_h, qk_h, u_h, w_h = _delta_chunk_fwd(xh)
    o_h, st_h = _delta_scan_fwd(xh, qk_h, u_h, w_h)
    o = _from_heads(o_h)
    y_delta = _delta_post_fwd(o, qkvz, gain_row)
    x1, h2, y = _outproj_fwd(y_attn, y_delta, w_out, x, g1, norm_ffn_g, sc2, sh2)
    gate, up, dx2, st_f = _ffn_fwd(h2, w_gate, w_up, w_down, x1, g2, final_norm_g, target)

    dgate, dup, act, dy2, dx1, dy, st_b = _ffn_bwd(dx2, gate, up, w_gate, w_up, w_down, x1, y, g2, g1, norm_ffn_g, sc2)
    grads = {
        "w_gate": _weight_grad(h2, dgate, "wgrad_gate"),
        "w_up": _weight_grad(h2, dup, "wgrad_up"),
        "w_down": _weight_grad(act, dy2, "wgrad_down"),
        "w_out": jnp.concatenate([_weight_grad(y_attn, dy, "wgrad_out_attn"),
                                  _weight_grad(y_delta, dy, "wgrad_out_delta")], axis=0),
    }
    dycat = _outproj_bwd(dy, w_out)
    do, dz, dgain = _delta_post_bwd(dycat, o, qkvz, gain_row)
    do_h = _to_heads(do)
    dsn_h, dvn_h = _delta_scan_bwd(xh, qk_h, w_h, do_h)
    dxs = _from_heads(_delta_chunk_bwd(xh, inv_h, u_h, w_h, st_h, dsn_h, dvn_h, do_h))
    dconv, dba, dvec = _delta_prep_bwd(qkvz, ba, conv_w, alog_row, dt_row, dxs)
    dxd, grads["conv_w"] = _conv_bwd(dconv, qkvz, conv_w)
    dq, dk, dv, dbias = _attention_bwd(qkv_a, dycat, y_attn, lse, rel_bias, tables)
    grad_x, st_i = _inproj_bwd(dq, dk, dv, dxd, dz, dba, w_a, w_d, w_ba, x, dx1, norm_attn_g, sc1)
    grads["w_in"] = jnp.concatenate(
        [_weight_grad(h1, dq, "wgrad_in_q"), _weight_grad(h1, dk, "wgrad_in_k"), _weight_grad(h1, dv, "wgrad_in_v"),
         _weight_grad(h1, dxd, "wgrad_in_delta"), _weight_grad(h1, dz, "wgrad_in_z"),
         _weight_grad(h1, dba, "wgrad_in_gates")[:, :2 * N_HEADS]], axis=1)
    grads["rel_bias"] = _rel_bias_grad(dbias, tables)[:, :N_BUCKETS].T
    grads["a_log"] = dvec[0:1, N_HEADS:2 * N_HEADS]
    grads["dt_bias"] = dvec[1:2, N_HEADS:2 * N_HEADS]
    grads["delta_norm_g"] = dgain[1:2, :HEAD_DIM]
    grads["norm_attn_g"] = st_i[2:3]
    grads["norm_ffn_g"] = st_b[2:3]
    grads["final_norm_g"] = st_f[0:1]
    dmod = jnp.concatenate([st_i[0:1], st_i[1:2], st_b[3:4], st_b[0:1], st_b[1:2], st_f[1:2]], axis=1)
    return st_f[3, 0], grad_x, grads, dmod


MESH = pl.DeviceIdType.MESH
OTHER_CHIPS = ((1, 0), (0, 1), (1, 1))
ALL_PEERS = tuple((m >> 2 & 1, m >> 1 & 1, m & 1) for m in range(1, 8))
ANY = pl.BlockSpec(memory_space=pl.ANY)
VMEM_SPEC = pl.BlockSpec(memory_space=pltpu.VMEM)
N_BIG = 5


def _me():
    return lax.axis_index("x"), lax.axis_index("y"), lax.axis_index("c")


def _flip(pos, mask):
    return tuple(1 - p if m else p for p, m in zip(pos, mask))


def _remote(src, dst, send_sems, recv_sems, k, to):
    return pltpu.make_async_remote_copy(src_ref=src, dst_ref=dst, send_sem=send_sems.at[k], recv_sem=recv_sems.at[k],
                                        device_id=to, device_id_type=MESH)


def _ada_exchange(c8, w_ada, b_ada, conv8):
    def body(c_ref, w_ref, b_ref, cv_ref, mod_ref, cact_ref, conv_ref, c_all, part_all, send_sems, recv_sems):
        x, y, c = me = _me()
        dev = 4 * x + 2 * y + c
        chip = 2 * x + y
        c_all[dev] = c_ref[...]
        conv_ref[chip] = cv_ref[...]
        first = [_remote(c_ref, c_all.at[dev], send_sems, recv_sems, k, _flip(me, mask))
                 for k, mask in enumerate(ALL_PEERS)]
        first += [_remote(cv_ref, conv_ref.at[chip], send_sems, recv_sems, 7 + j, _flip(me, (*mask, 0)))
                  for j, mask in enumerate(OTHER_CHIPS)]
        for cp in first:
            cp.start()
        for cp in first:
            cp.wait()
        row = lax.broadcasted_iota(jnp.int32, (8, D_MODEL), 0)
        c_rows = jnp.zeros((8, D_MODEL), F32)
        for d in range(8):
            c_rows = jnp.where(row == d, c_all[d], c_rows)
        c_act = _silu(c_rows)
        cact_ref[...] = c_act
        part_all[chip] = _nn(c_act, w_ref[...], HIGHEST)
        second = [_remote(part_all.at[chip], part_all.at[chip], send_sems, recv_sems, 10 + j, _flip(me, (*mask, 0)))
                  for j, mask in enumerate(OTHER_CHIPS)]
        for cp in second:
            cp.start()
        for cp in second:
            cp.wait()
        cols = w_ref.shape[1]
        for k in range(4):
            mod_ref[:, k * cols:(k + 1) * cols] = part_all[k] + b_ref[:, k * cols:(k + 1) * cols]

    cols = w_ada.shape[1]
    return pl.pallas_call(
        body,
        in_specs=[VMEM_SPEC] * 4,
        out_specs=[VMEM_SPEC] * 3,
        out_shape=[jax.ShapeDtypeStruct((8, 4 * cols), F32), jax.ShapeDtypeStruct((8, D_MODEL), F32),
                   jax.ShapeDtypeStruct((4, 8, conv8.shape[1]), F32)],
        scratch_shapes=[pltpu.VMEM((8, 8, D_MODEL), F32), pltpu.VMEM((4, 8, cols), F32),
                        pltpu.SemaphoreType.DMA((13,)), pltpu.SemaphoreType.DMA((13,))],
        compiler_params=pltpu.CompilerParams(vmem_limit_bytes=VMEM_LIMIT),
        name="ada_exchange",
    )(c8, w_ada, b_ada, conv8)


def _gather_weights(shards):
    def body(*refs):
        srcs, dsts = refs[:N_BIG], refs[N_BIG:2 * N_BIG]
        send_sems, recv_sems = refs[2 * N_BIG:]
        x, y, c = me = _me()
        chip = 2 * x + y
        sibling = _flip(me, (0, 0, 1))
        first, passed = [], []
        for a in range(N_BIG):
            for j, mask in enumerate(OTHER_CHIPS):
                to = _flip(me, (*mask, 0))
                first.append(_remote(srcs[a].at[c], dsts[a].at[chip, c], send_sems, recv_sems, 6 * a + j, to))
                landed = dsts[a].at[2 * to[0] + to[1], c]
                passed.append(_remote(landed, landed, send_sems, recv_sems, 6 * a + 3 + j, sibling))
        for cp in first:
            cp.start()
        for cp, fwd in zip(first, passed):
            cp.wait_recv()
            fwd.start()
        for cp in first:
            cp.wait_send()
        for fwd in passed:
            fwd.wait()

    return pl.pallas_call(
        body,
        in_specs=[ANY] * N_BIG,
        out_specs=[ANY] * N_BIG,
        out_shape=[jax.ShapeDtypeStruct((4, *s.shape), s.dtype) for s in shards],
        scratch_shapes=[pltpu.SemaphoreType.DMA((6 * N_BIG,)), pltpu.SemaphoreType.DMA((6 * N_BIG,))],
        name="gather_weights",
    )(*shards)


def _start_and_wait(copies):
    for cp in copies:
        cp.start()
    for cp in copies:
        cp.wait()


def _swap_halves(grads):
    def body(*refs):
        srcs, got = refs[:N_BIG], refs[N_BIG:2 * N_BIG]
        send_sems, recv_sems = refs[2 * N_BIG:]
        x, y, c = me = _me()
        _start_and_wait([_remote(srcs[a].at[:, 1 - c], got[a], send_sems, recv_sems, a, _flip(me, (0, 0, 1)))
                         for a in range(N_BIG)])

    return pl.pallas_call(
        body,
        in_specs=[ANY] * N_BIG,
        out_specs=[ANY] * N_BIG,
        out_shape=[jax.ShapeDtypeStruct((4, g.shape[2], g.shape[3]), g.dtype) for g in grads],
        scratch_shapes=[pltpu.SemaphoreType.DMA((N_BIG,)), pltpu.SemaphoreType.DMA((N_BIG,))],
        name="swap_halves",
    )(*grads)


def _scatter_partials(partials):
    def body(*refs):
        srcs, dsts = refs[:N_BIG], refs[N_BIG:2 * N_BIG]
        send_sems, recv_sems = refs[2 * N_BIG:]
        x, y, c = me = _me()
        chip = 2 * x + y
        copies = []
        for a in range(N_BIG):
            for j, mask in enumerate(OTHER_CHIPS):
                to = _flip(me, (*mask, 0))
                copies.append(_remote(srcs[a].at[2 * to[0] + to[1]], dsts[a].at[chip], send_sems, recv_sems, 3 * a + j, to))
        _start_and_wait(copies)

    return pl.pallas_call(
        body,
        in_specs=[ANY] * N_BIG,
        out_specs=[ANY] * N_BIG,
        out_shape=[jax.ShapeDtypeStruct(p.shape, p.dtype) for p in partials],
        scratch_shapes=[pltpu.SemaphoreType.DMA((3 * N_BIG,)), pltpu.SemaphoreType.DMA((3 * N_BIG,))],
        name="scatter_partials",
    )(*partials)


def _join_halves(halves):
    def body(*refs):
        srcs, dsts = refs[:N_BIG], refs[N_BIG:2 * N_BIG]
        send_sems, recv_sems = refs[2 * N_BIG:]
        x, y, c = me = _me()
        _start_and_wait([_remote(srcs[a], dsts[a].at[c], send_sems, recv_sems, a, _flip(me, (0, 0, 1)))
                         for a in range(N_BIG)])

    return pl.pallas_call(
        body,
        in_specs=[ANY] * N_BIG,
        out_specs=[ANY] * N_BIG,
        out_shape=[jax.ShapeDtypeStruct((2, *h.shape), h.dtype) for h in halves],
        scratch_shapes=[pltpu.SemaphoreType.DMA((N_BIG,)), pltpu.SemaphoreType.DMA((N_BIG,))],
        name="join_halves",
    )(*halves)


def _gather_small(packed):
    n_rows = packed.shape[0]

    def body(p_ref, all_ref, sum_ref, send_sems, recv_sems):
        x, y, c = me = _me()
        dev = 4 * x + 2 * y + c
        all_ref[dev] = p_ref[...]
        copies = [_remote(p_ref, all_ref.at[dev], send_sems, recv_sems, k, _flip(me, mask))
                  for k, mask in enumerate(ALL_PEERS)]
        for cp in copies:
            cp.start()
        for cp in copies:
            cp.wait()
        total = all_ref[0]
        for d in range(1, 8):
            total = total + all_ref[d]
        sum_ref[...] = total

    return pl.pallas_call(
        body,
        in_specs=[VMEM_SPEC],
        out_specs=[VMEM_SPEC, VMEM_SPEC],
        out_shape=[jax.ShapeDtypeStruct((8, n_rows, LANES), F32), jax.ShapeDtypeStruct((n_rows, LANES), F32)],
        scratch_shapes=[pltpu.SemaphoreType.DMA((7,)), pltpu.SemaphoreType.DMA((7,))],
        name="gather_small",
    )(packed)


def _add_pair(a, b, out_dtype, name):
    def body(a_ref, b_ref, o_ref):
        o_ref[...] = (a_ref[...] + b_ref[...]).astype(o_ref.dtype)

    blk = pl.BlockSpec((1, *a.shape[1:]), lambda i: (i, 0, 0))
    return pl.pallas_call(
        body, grid=(a.shape[0],), in_specs=[blk, blk], out_specs=blk,
        out_shape=jax.ShapeDtypeStruct(a.shape, out_dtype),
        compiler_params=_params(("arbitrary",)), name=name,
    )(a, b)


def _add_slots(a, name):
    def body(a_ref, o_ref):
        total = a_ref[0].astype(F32)
        for k in range(1, 4):
            total = total + a_ref[k].astype(F32)
        o_ref[...] = total

    return pl.pallas_call(
        body, in_specs=[VMEM_SPEC], out_specs=VMEM_SPEC,
        out_shape=jax.ShapeDtypeStruct(a.shape[1:], F32),
        compiler_params=pltpu.CompilerParams(vmem_limit_bytes=VMEM_LIMIT), name=name,
    )(a)


def _ada_weight_grad(c_act, dmod_cols):
    def body(c_ref, d_ref, o_ref):
        o_ref[...] = _tn(c_ref[...], d_ref[...], HIGHEST)

    return pl.pallas_call(
        body, in_specs=[VMEM_SPEC, VMEM_SPEC], out_specs=VMEM_SPEC,
        out_shape=jax.ShapeDtypeStruct((c_act.shape[1], dmod_cols.shape[1]), F32),
        compiler_params=pltpu.CompilerParams(vmem_limit_bytes=VMEM_LIMIT), name="ada_weight_grad",
    )(c_act, dmod_cols)


def kernel(x, c, w_ada, b_ada, norm_attn_g, w_in, rel_bias, conv_w, a_log, dt_bias, delta_norm_g, w_out, norm_ffn_g, w_gate, w_up, w_down, final_norm_g, loss_target, m_w_ada, m_b_ada, m_norm_attn_g, m_w_in, m_rel_bias, m_conv_w, m_a_log, m_dt_bias, m_delta_norm_g, m_w_out, m_norm_ffn_g, m_w_gate, m_w_up, m_w_down, m_final_norm_g, v_w_ada, v_b_ada, v_norm_attn_g, v_w_in, v_rel_bias, v_conv_w, v_a_log, v_dt_bias, v_delta_norm_g, v_w_out, v_norm_ffn_g, v_w_gate, v_w_up, v_w_down, v_final_norm_g):
    xi, yi, ci = _me()
    dev = 4 * xi + 2 * yi + ci
    chip = 2 * xi + yi

    conv_cols = conv_w.shape[2]
    mod_all, c_act, conv_all = _ada_exchange(jnp.broadcast_to(c, (8, D_MODEL)), w_ada[0], b_ada,
                                             jnp.pad(conv_w[0], ((0, 4), (0, 0))))
    mod = lax.dynamic_slice_in_dim(mod_all, dev, 1, axis=0)
    conv_full = jnp.swapaxes(conv_all[:, :4, :], 0, 1).reshape(4, 4 * conv_cols)

    big = [w_in[0], w_out[0], w_gate[0], w_up[0], w_down[0]]
    by_cols = [True, False, True, True, False]
    shards = [w.astype(BF16).reshape(2, w.size // (2 * LANES), LANES) for w in big]
    gathered = [lax.dynamic_update_index_in_dim(g, s, chip, 0) for g, s in zip(_gather_weights(shards), shards)]
    gathered = [g.reshape(4, *w.shape) for g, w in zip(gathered, big)]
    whole = [jnp.swapaxes(g, 0, 1).reshape(g.shape[1], 4 * g.shape[2]) if cols else g.reshape(4 * g.shape[1], g.shape[2])
             for g, cols in zip(gathered, by_cols)]

    loss, grad_x, grads, dmod = _local_step(
        x[0], loss_target[0], mod, norm_attn_g, whole[0], rel_bias, conv_full, a_log, dt_bias, delta_norm_g,
        whole[1], norm_ffn_g, whole[2], whole[3], whole[4], final_norm_g[None])

    slots = []
    for name, w, cols in zip(("w_in", "w_out", "w_gate", "w_up", "w_down"), big, by_cols):
        rows, ncol = w.shape
        g = grads[name]
        g = jnp.swapaxes(g.reshape(rows, 4, ncol), 0, 1) if cols else g.reshape(4, rows, ncol)
        slots.append(g.reshape(4, 2, rows * ncol // (2 * LANES), LANES))
    partials = [_add_pair(lax.dynamic_index_in_dim(s, ci, 1, keepdims=False), got, BF16, f"add_pair_{a}")
                for a, (s, got) in enumerate(zip(slots, _swap_halves(slots)))]
    by_source = [lax.dynamic_update_index_in_dim(b, lax.dynamic_index_in_dim(p, chip, 0, keepdims=False), chip, 0)
                 for b, p in zip(_scatter_partials(partials), partials)]
    halves = [_add_slots(p, f"add_slots_{a}") for a, p in enumerate(by_source)]
    joined = [lax.dynamic_update_index_in_dim(j, h, ci, 0) for j, h in zip(_join_halves(halves), halves)]
    big_grads = [j.reshape(w.shape) for j, w in zip(joined, big)]

    pieces = [dmod, grads["conv_w"], grads["norm_attn_g"], grads["norm_ffn_g"], grads["final_norm_g"],
              grads["rel_bias"], grads["a_log"], grads["dt_bias"], grads["delta_norm_g"]]
    flat = [jnp.pad(p.reshape(-1), (0, -p.size % LANES)) for p in pieces]
    n_rows = [f.size // LANES for f in flat]
    packed = jnp.concatenate(flat).reshape(-1, LANES)
    packed = jnp.pad(packed, ((0, -packed.shape[0] % 8), (0, 0)))
    all_small, total = _gather_small(packed)
    sums, start = [], 0
    for p, n in zip(pieces, n_rows):
        sums.append(total[start:start + n].reshape(-1)[:p.size].reshape(p.shape))
        start += n
    g_b_ada, g_conv, g_norm_attn, g_norm_ffn, g_final, g_rel, g_alog, g_dt, g_dnorm = sums
    dmod_all = all_small[:, :n_rows[0], :].reshape(8, -1)
    ada_cols = w_ada.shape[2]
    g_w_ada = _ada_weight_grad(c_act, lax.dynamic_slice_in_dim(dmod_all, chip * ada_cols, ada_cols, axis=1))
    g_conv = lax.dynamic_slice_in_dim(g_conv, chip * conv_cols, conv_cols, axis=1)

    grad = {"w_ada": g_w_ada[None], "b_ada": g_b_ada, "norm_attn_g": g_norm_attn, "w_in": big_grads[0][None],
            "rel_bias": g_rel, "conv_w": g_conv[None], "a_log": g_alog, "dt_bias": g_dt, "delta_norm_g": g_dnorm,
            "w_out": big_grads[1][None], "norm_ffn_g": g_norm_ffn, "w_gate": big_grads[2][None],
            "w_up": big_grads[3][None], "w_down": big_grads[4][None], "final_norm_g": g_final.reshape(-1)}
    weight = {"w_ada": w_ada, "b_ada": b_ada, "norm_attn_g": norm_attn_g, "w_in": w_in, "rel_bias": rel_bias,
              "conv_w": conv_w, "a_log": a_log, "dt_bias": dt_bias, "delta_norm_g": delta_norm_g, "w_out": w_out,
              "norm_ffn_g": norm_ffn_g, "w_gate": w_gate, "w_up": w_up, "w_down": w_down, "final_norm_g": final_norm_g}
    first = {"w_ada": m_w_ada, "b_ada": m_b_ada, "norm_attn_g": m_norm_attn_g, "w_in": m_w_in, "rel_bias": m_rel_bias,
             "conv_w": m_conv_w, "a_log": m_a_log, "dt_bias": m_dt_bias, "delta_norm_g": m_delta_norm_g,
             "w_out": m_w_out, "norm_ffn_g": m_norm_ffn_g, "w_gate": m_w_gate, "w_up": m_w_up, "w_down": m_w_down,
             "final_norm_g": m_final_norm_g}
    second = {"w_ada": v_w_ada, "b_ada": v_b_ada, "norm_attn_g": v_norm_attn_g, "w_in": v_w_in, "rel_bias": v_rel_bias,
              "conv_w": v_conv_w, "a_log": v_a_log, "dt_bias": v_dt_bias, "delta_norm_g": v_delta_norm_g,
              "w_out": v_w_out, "norm_ffn_g": v_norm_ffn_g, "w_gate": v_w_gate, "w_up": v_w_up, "w_down": v_w_down,
              "final_norm_g": v_final_norm_g}
    delta, new_m, new_v = {}, {}, {}
    for name, w in weight.items():
        two_d = (-1, w.shape[-1])
        d, nm, nv = _adamw(w.reshape(two_d), grad[name].reshape(two_d), first[name].reshape(two_d),
                           second[name].reshape(two_d), f"adamw_{name}")
        delta[name], new_m[name], new_v[name] = d.reshape(w.shape), nm.reshape(w.shape), nv.reshape(w.shape)

    names = list(weight)
    return (lax.psum(loss, ("x", "y", "c")), grad_x[None], *[grad[n] for n in names], *[delta[n] for n in names],
            *[new_m[n] for n in names], *[new_v[n] for n in names])
```

```python
import functools
import math

import numpy as np
import jax
import jax.numpy as jnp
from jax import lax
from jax.experimental import pallas as pl
from jax.experimental.pallas import tpu as pltpu

F32 = jnp.float32
BF16 = jnp.bfloat16
HIGHEST = lax.Precision.HIGHEST

D_MODEL = 1024
HEAD_DIM = 64
N_HEADS = 8
HEAD_W = 512
BRANCHES = ((128, 1), (512, 4), (2048, 16))
BAND = 128
ATT_TILE = 2048
ATT_UNROLL = 4
N_BUCKETS = 32
MAX_DISTANCE = 2048
CHUNK = 64
D_FF = 2816
EPS = 1e-6
NEG_INF = -1e30
LANES = 128
VMEM_LIMIT = 56 * 1024 * 1024

ADAM_LR = 0.001
ADAM_B1 = 0.9
ADAM_B2 = 0.999
ADAM_EPS = 1e-08
ADAM_WD = 0.01
ADAM_STEP = 10


def _nn(a, b, precision=None):
    return jnp.dot(a, b, preferred_element_type=F32, precision=precision)


def _nt(a, b, precision=None):
    return lax.dot_general(a, b, (((1,), (1,)), ((), ())), preferred_element_type=F32, precision=precision)


def _tn(a, b, precision=None):
    return lax.dot_general(a, b, (((0,), (0,)), ((), ())), preferred_element_type=F32, precision=precision)


def _params(sem, vmem=VMEM_LIMIT):
    return pltpu.CompilerParams(dimension_semantics=sem, vmem_limit_bytes=vmem)


def _sigmoid(x):
    return 1.0 / (1.0 + jnp.exp(-x))


def _silu(x):
    return x * _sigmoid(x)


def _dsilu(x):
    s = _sigmoid(x)
    return s * (1.0 + x * (1.0 - s))


def _attn_tables():
    qi = np.arange(BAND)[:, None]
    kj = np.arange(2 * BAND)[None, :]
    steps = qi + BAND - kj
    in_window = (steps >= 0) & (steps <= BAND)
    max_exact = N_BUCKETS // 2
    out = np.zeros((3, 2, BAND, 2 * BAND), np.int32)
    for b, (_, dil) in enumerate(BRANCHES):
        dist = np.maximum(steps, 0) * dil
        dist_f = np.maximum(dist, 1).astype(np.float32)
        large = max_exact + (np.log(dist_f / np.float32(max_exact)) / np.float32(math.log(MAX_DISTANCE / max_exact))
                             * np.float32(N_BUCKETS - max_exact)).astype(np.int32)
        bucket = np.where(dist < max_exact, dist, np.minimum(large, N_BUCKETS - 1)).astype(np.int32)
        out[b, 0] = np.where(in_window, bucket, -1)
        out[b, 1] = np.where(in_window & (kj >= BAND), bucket, -1)
    return out


def _attn_bias_tables(rel_ref, tab_ref, bias_s, pair):
    for b in range(3):
        for first in range(2):
            tab = tab_ref[b, first]
            for hh in range(2):
                head = 2 * pair + hh

                def pick(kk, acc, tab=tab, head=head):
                    return jnp.where(tab == kk, rel_ref[kk, head], acc)

                acc = lax.fori_loop(0, N_BUCKETS, pick, jnp.zeros((BAND, 2 * BAND), F32))
                bias_s[b, hh, first] = jnp.where(tab < 0, NEG_INF, acc)


def _attn_block_index(idx, t, r):
    nb = ATT_TILE // (BAND * r)
    rho = idx // nb
    n = idx % nb
    qs = rho + r * BAND * n
    gs = t * ATT_TILE + qs
    first = (t * nb + n) == 0
    ps = jnp.where(first, gs, gs - r * BAND)
    return qs, gs, ps, first.astype(jnp.int32)


def _rows(start, r):
    return pl.ds(start, BAND) if r == 1 else pl.ds(start, BAND, stride=r)


def _attention_fwd(qkv, rel_bias, tables):
    seq = qkv.shape[0]
    n_tiles = seq // ATT_TILE

    def body(rel_ref, tab_ref, q_ref, k_ref, v_ref, y_ref, lse_ref, bias_s, o_s, l_s):
        pair = pl.program_id(0)
        t = pl.program_id(1)
        lane = lax.broadcasted_iota(jnp.int32, (1, LANES), 1)
        head0 = lane < HEAD_DIM

        @pl.when(t == 0)
        def _():
            _attn_bias_tables(rel_ref, tab_ref, bias_s, pair)

        masks = (head0, jnp.logical_not(head0))
        for b, (_, r) in enumerate(BRANCHES):
            def blocks(it, carry, b=b, r=r):
                idx = [_attn_block_index(it * ATT_UNROLL + j, t, r) for j in range(ATT_UNROLL)]
                qb = [q_ref[_rows(qs, r), :] * (HEAD_DIM ** -0.5) for qs, _, _, _ in idx]
                kcat = [jnp.concatenate([k_ref[_rows(ps, r), :], k_ref[_rows(gs, r), :]], axis=0).astype(BF16)
                        for _, gs, ps, _ in idx]
                vcat = [jnp.concatenate([v_ref[_rows(ps, r), :], v_ref[_rows(gs, r), :]], axis=0).astype(BF16)
                        for _, gs, ps, _ in idx]
                work = [(j, hh) for j in range(ATT_UNROLL) for hh in range(2)]
                s = [_nt(jnp.where(masks[hh], qb[j], 0.0).astype(BF16), kcat[j]) + bias_s[b, hh, idx[j][3]]
                     for j, hh in work]
                m = [jnp.max(sv, axis=-1, keepdims=True) for sv in s]
                e = [jnp.exp(sv - mv) for sv, mv in zip(s, m)]
                den = [jnp.sum(ev, axis=-1, keepdims=True) for ev in e]
                out = [_nn(ev.astype(BF16), vcat[j]) / dv for ev, dv, (j, _) in zip(e, den, work)]
                lse = [mv + jnp.log(dv) for mv, dv in zip(m, den)]
                for j in range(ATT_UNROLL):
                    o_s[b, _rows(idx[j][0], r), :] = jnp.where(head0, out[2 * j], out[2 * j + 1])
                    l_s[b, _rows(idx[j][0], r), :] = jnp.where(head0, lse[2 * j], lse[2 * j + 1])
                return carry

            lax.fori_loop(0, ATT_TILE // BAND // ATT_UNROLL, blocks, 0)

        def merge(i, carry):
            rows = pl.ds(pl.multiple_of(i * BAND, BAND), BAND)
            l0, l1, l2 = l_s[0, rows, :], l_s[1, rows, :], l_s[2, rows, :]
            m = jnp.maximum(jnp.maximum(l0, l1), l2)
            w0, w1, w2 = jnp.exp(l0 - m), jnp.exp(l1 - m), jnp.exp(l2 - m)
            tot = w0 + w1 + w2
            y_ref[rows, :] = (w0 * o_s[0, rows, :] + w1 * o_s[1, rows, :] + w2 * o_s[2, rows, :]) / tot
            lse_ref[rows, :] = m + jnp.log(tot)
            return carry

        lax.fori_loop(0, ATT_TILE // BAND, merge, 0)

    tile = pl.BlockSpec((ATT_TILE, LANES), lambda p, t: (t, p))
    return pl.pallas_call(
        body,
        grid=(N_HEADS // 2, n_tiles),
        in_specs=[
            pl.BlockSpec(memory_space=pltpu.SMEM),
            pl.BlockSpec((3, 2, BAND, 2 * BAND), lambda p, t: (0, 0, 0, 0)),
            pl.BlockSpec((ATT_TILE, LANES), lambda p, t: (t, p)),
            pl.BlockSpec((seq, LANES), lambda p, t: (0, 4 + p)),
            pl.BlockSpec((seq, LANES), lambda p, t: (0, 8 + p)),
        ],
        out_specs=[tile, tile],
        out_shape=[jax.ShapeDtypeStruct((seq, HEAD_W), F32), jax.ShapeDtypeStruct((seq, HEAD_W), F32)],
        scratch_shapes=[
            pltpu.VMEM((3, 2, 2, BAND, 2 * BAND), F32),
            pltpu.VMEM((3, ATT_TILE, LANES), F32),
            pltpu.VMEM((3, ATT_TILE, LANES), F32),
        ],
        compiler_params=_params(("arbitrary", "arbitrary")),
        name="attn_fwd",
    )(rel_bias, tables, qkv, qkv, qkv)


def _attention_bwd(qkv, dy, y, lse, rel_bias, tables):
    seq = qkv.shape[0]
    n_tiles = seq // ATT_TILE

    def body(rel_ref, tab_ref, q_ref, k_ref, v_ref, dy_ref, y_ref, lse_ref,
             dq_ref, dk_ref, dv_ref, dbias_ref, bias_s):
        pair = pl.program_id(0)
        t = pl.program_id(1)
        lane = lax.broadcasted_iota(jnp.int32, (1, LANES), 1)
        head0 = lane < HEAD_DIM

        @pl.when(t == 0)
        def _():
            _attn_bias_tables(rel_ref, tab_ref, bias_s, pair)
            dk_ref[...] = jnp.zeros_like(dk_ref)
            dv_ref[...] = jnp.zeros_like(dv_ref)
            dbias_ref[...] = jnp.zeros_like(dbias_ref)

        dq_ref[...] = jnp.zeros_like(dq_ref)

        masks = (head0, jnp.logical_not(head0))
        scale = HEAD_DIM ** -0.5
        for b, (_, r) in enumerate(BRANCHES):
            def blocks(it, carry, b=b, r=r):
                idx = [_attn_block_index(it * ATT_UNROLL + j, t, r) for j in range(ATT_UNROLL)]
                qb = [q_ref[_rows(qs, r), :] * scale for qs, _, _, _ in idx]
                kcat = [jnp.concatenate([k_ref[_rows(ps, r), :], k_ref[_rows(gs, r), :]], axis=0).astype(BF16)
                        for _, gs, ps, _ in idx]
                vcat = [jnp.concatenate([v_ref[_rows(ps, r), :], v_ref[_rows(gs, r), :]], axis=0).astype(BF16)
                        for _, gs, ps, _ in idx]
                dob = [dy_ref[_rows(qs, r), :] for qs, _, _, _ in idx]
                ob = [y_ref[_rows(qs, r), :] for qs, _, _, _ in idx]
                lb = [lse_ref[_rows(qs, r), :] for qs, _, _, _ in idx]
                work = [(j, hh) for j in range(ATT_UNROLL) for hh in range(2)]
                qh = [jnp.where(masks[hh], qb[j], 0.0).astype(BF16) for j, hh in work]
                doh = [jnp.where(masks[hh], dob[j], 0.0) for j, hh in work]
                dohb = [d.astype(BF16) for d in doh]
                s = [_nt(qh[w], kcat[j]) + bias_s[b, hh, idx[j][3]] for w, (j, hh) in enumerate(work)]
                dp = [_nt(dohb[w], vcat[j]) for w, (j, _) in enumerate(work)]
                lcol = [jnp.max(jnp.where(masks[hh], lb[j], -jnp.inf), axis=-1, keepdims=True) for j, hh in work]
                delta = [jnp.sum(doh[w] * ob[j], axis=-1, keepdims=True) for w, (j, _) in enumerate(work)]
                prob = [jnp.exp(sv - lv) for sv, lv in zip(s, lcol)]
                ds = [pv * (dv - de) for pv, dv, de in zip(prob, dp, delta)]
                dsb = [d.astype(BF16) for d in ds]
                dq = [_nn(dsb[w], kcat[j]) for w, (j, _) in enumerate(work)]
                dkc = [_tn(dsb[w], qh[w]) for w in range(len(work))]
                dvc = [_tn(prob[w].astype(BF16), dohb[w]) for w in range(len(work))]
                for w, (j, hh) in enumerate(work):
                    dbias_ref[0, b, hh] += ds[w]
                for j in range(ATT_UNROLL):
                    qs, gs, ps, _ = idx[j]
                    dkcat = dkc[2 * j] + dkc[2 * j + 1]
                    dvcat = dvc[2 * j] + dvc[2 * j + 1]
                    dq_ref[_rows(qs, r), :] += jnp.where(head0, dq[2 * j], dq[2 * j + 1]) * scale
                    dk_ref[_rows(ps, r), :] += dkcat[:BAND]
                    dk_ref[_rows(gs, r), :] += dkcat[BAND:]
                    dv_ref[_rows(ps, r), :] += dvcat[:BAND]
                    dv_ref[_rows(gs, r), :] += dvcat[BAND:]
                return carry

            lax.fori_loop(0, ATT_TILE // BAND // ATT_UNROLL, blocks, 0)

    tile = pl.BlockSpec((ATT_TILE, LANES), lambda p, t: (t, p))
    full = pl.BlockSpec((seq, LANES), lambda p, t: (0, p))
    return pl.pallas_call(
        body,
        grid=(N_HEADS // 2, n_tiles),
        in_specs=[
            pl.BlockSpec(memory_space=pltpu.SMEM),
            pl.BlockSpec((3, 2, BAND, 2 * BAND), lambda p, t: (0, 0, 0, 0)),
            pl.BlockSpec((ATT_TILE, LANES), lambda p, t: (t, p)),
            pl.BlockSpec((seq, LANES), lambda p, t: (0, 4 + p)),
            pl.BlockSpec((seq, LANES), lambda p, t: (0, 8 + p)),
            tile, tile, tile,
        ],
        out_specs=[tile, full, full,
                   pl.BlockSpec((1, 3, 2, BAND, 2 * BAND), lambda p, t: (p, 0, 0, 0, 0))],
        out_shape=[jax.ShapeDtypeStruct((seq, HEAD_W), F32)] * 3
        + [jax.ShapeDtypeStruct((N_HEADS // 2, 3, 2, BAND, 2 * BAND), F32)],
        scratch_shapes=[pltpu.VMEM((3, 2, 2, BAND, 2 * BAND), F32)],
        compiler_params=_params(("arbitrary", "arbitrary")),
        name="attn_bwd",
    )(rel_bias, tables, qkv, qkv, qkv, dy, y, lse)


def _rel_bias_grad(dbias, tables):
    def body(tab_ref, db_ref, out_ref):
        lane = lax.broadcasted_iota(jnp.int32, (1, LANES), 1)
        out_ref[...] = jnp.zeros_like(out_ref)
        for b in range(3):
            tab = tab_ref[b, 0]

            def head(h, carry, b=b, tab=tab):
                d = db_ref[h // 2, b, h % 2]
                sums = [jnp.sum(jnp.where(tab == kk, d, 0.0), keepdims=True) for kk in range(N_BUCKETS)]
                row = jnp.zeros((1, LANES), F32)
                for kk, s in enumerate(sums):
                    row = row + jnp.where(lane == kk, s, 0.0)
                out_ref[pl.ds(h, 1), :] += row
                return carry

            lax.fori_loop(0, N_HEADS, head, 0)

    return pl.pallas_call(
        body,
        out_shape=jax.ShapeDtypeStruct((N_HEADS, LANES), F32),
        compiler_params=pltpu.CompilerParams(vmem_limit_bytes=VMEM_LIMIT),
        name="rel_bias_grad",
    )(tables, dbias)


ROW_TILE = 512


def _head_sum_matrix():
    return (lax.broadcasted_iota(jnp.int32, (HEAD_W, LANES), 0) // HEAD_DIM
            == lax.broadcasted_iota(jnp.int32, (HEAD_W, LANES), 1)).astype(F32)


def _head_spread_matrix(offset=0):
    return (lax.broadcasted_iota(jnp.int32, (LANES, HEAD_W), 0)
            == lax.broadcasted_iota(jnp.int32, (LANES, HEAD_W), 1) // HEAD_DIM + offset).astype(F32)


def _head_gather_matrix(offset=0):
    return (lax.broadcasted_iota(jnp.int32, (HEAD_W, LANES), 0) // HEAD_DIM + offset
            == lax.broadcasted_iota(jnp.int32, (HEAD_W, LANES), 1)).astype(F32)


def _split3(x):
    hi = x.astype(BF16)
    rest = x - hi.astype(F32)
    mid = rest.astype(BF16)
    return hi, mid, (rest - mid.astype(F32)).astype(BF16)


def _pick(x, onehot):
    m = onehot.astype(BF16)
    hi, mid, lo = _split3(x)
    return _nn(hi, m) + (_nn(mid, m) + _nn(lo, m))


def _pick_left(onehot, x):
    m = onehot.astype(BF16)
    hi, mid, lo = _split3(x)
    return _nn(m, hi) + (_nn(m, mid) + _nn(m, lo))


def _tri(lower, strict=False):
    r = lax.broadcasted_iota(jnp.int32, (CHUNK, CHUNK), 0)
    c = lax.broadcasted_iota(jnp.int32, (CHUNK, CHUNK), 1)
    if lower:
        return (c < r) if strict else (c <= r)
    return c >= r


def _softplus(z):
    return jnp.maximum(z, 0.0) + jnp.log(1.0 + jnp.exp(-jnp.abs(z)))


def _conv_taps(stage, w_ref, rows):
    return (w_ref[3:4, :] * stage[8:8 + rows, :] + w_ref[2:3, :] * stage[7:7 + rows, :]
            + w_ref[1:2, :] * stage[6:6 + rows, :] + w_ref[0:1, :] * stage[5:5 + rows, :])


def _l2_scale(xc, hsum, hspread):
    ssq = _pick(xc * xc, hsum)
    return _pick(lax.rsqrt(ssq + EPS), hspread)


def _stage_rows(stage, x_ref, xp_ref, i):
    stage[0:8, :] = jnp.where(i == 0, 0.0, xp_ref[...])
    stage[8:8 + ROW_TILE, :] = x_ref[...]


def _delta_prep_fwd(qkvz, ba, conv_w, alog_row, dt_row):
    seq = qkvz.shape[0]
    qkv_w = 3 * HEAD_W

    def body(x_ref, xp_ref, ba_ref, w_ref, al_ref, dt_ref, out_ref, stage):
        i = pl.program_id(0)
        _stage_rows(stage, x_ref, xp_ref, i)
        act = _silu(_conv_taps(stage, w_ref, ROW_TILE))
        hsum, hspread = _head_sum_matrix(), _head_spread_matrix()
        qc, kc = act[:, :HEAD_W], act[:, HEAD_W:2 * HEAD_W]
        out_ref[0] = qc * _l2_scale(qc, hsum, hspread) * (HEAD_DIM ** -0.5)
        out_ref[1] = kc * _l2_scale(kc, hsum, hspread)
        out_ref[2] = act[:, 2 * HEAD_W:]
        bav = ba_ref[...]
        out_ref[3] = _pick(_sigmoid(bav), hspread)
        g8 = -jnp.exp(al_ref[...]) * _softplus(bav + dt_ref[...])
        gb = _pick(g8, _head_spread_matrix(N_HEADS))
        cum = _tri(True).astype(F32)
        for ch in range(ROW_TILE // CHUNK):
            rows = slice(ch * CHUNK, (ch + 1) * CHUNK)
            out_ref[4, rows, :] = _pick_left(cum, gb[rows])

    return pl.pallas_call(
        body,
        grid=(seq // ROW_TILE,),
        in_specs=[
            pl.BlockSpec((ROW_TILE, qkv_w), lambda i: (i, 0)),
            pl.BlockSpec((8, qkv_w), lambda i: (jnp.maximum(i * (ROW_TILE // 8) - 1, 0), 0)),
            pl.BlockSpec((ROW_TILE, LANES), lambda i: (i, 0)),
            pl.BlockSpec((4, qkv_w), lambda i: (0, 0)),
            pl.BlockSpec((1, LANES), lambda i: (0, 0)),
            pl.BlockSpec((1, LANES), lambda i: (0, 0)),
        ],
        out_specs=pl.BlockSpec((5, ROW_TILE, HEAD_W), lambda i: (0, i, 0)),
        out_shape=jax.ShapeDtypeStruct((5, seq, HEAD_W), F32),
        scratch_shapes=[pltpu.VMEM((ROW_TILE + 8, qkv_w), F32)],
        compiler_params=_params(("arbitrary",)),
        name="delta_prep_fwd",
    )(qkvz, qkvz, ba, conv_w, alog_row, dt_row)


def _split(x):
    hi = x.astype(BF16)
    return hi, (x - hi.astype(F32)).astype(BF16)


def _dot3(a, b, dot=_nn):
    return dot(a[0], b[0]) + (dot(a[0], b[1]) + dot(a[1], b[0]))


def _unit_lower_inverses(mats):
    eye = (lax.broadcasted_iota(jnp.int32, (CHUNK, CHUNK), 0)
           == lax.broadcasted_iota(jnp.int32, (CHUNK, CHUNK), 1)).astype(F32)
    invs = [eye - a for a in mats]
    powers = [_split(a) for a in mats]
    for step in range(5):
        squares = [_dot3(p, p) for p in powers]
        powers = [_split(s) for s in squares]
        invs = [inv + _dot3(_split(inv), p) for inv, p in zip(invs, powers)]
    return invs


def _chunk_terms(q, k, v, beta, gc):
    causal, strict = _tri(True), _tri(True, strict=True)
    e = jnp.exp(gc)
    g_last = jnp.broadcast_to(gc[CHUNK - 1:CHUNK, :], (CHUNK, CHUNK))
    f = jnp.exp(g_last - gc)
    e_last = jnp.exp(g_last)
    decay = jnp.where(causal, jnp.exp(jnp.where(causal, gc - gc.T, 0.0)), 0.0)
    kb = k * beta
    a_mat = jnp.where(strict, _nt(kb.astype(BF16), k.astype(BF16)) * decay, 0.0)
    qk = jnp.where(causal, _nt(q.astype(BF16), k.astype(BF16)) * decay, 0.0)
    return e, f, e_last, decay, kb, a_mat, qk


GROUP = 8
UNROLL = 8


def _chunk_rows(ci):
    return pl.ds(pl.multiple_of(ci * CHUNK, CHUNK), CHUNK)


def _delta_chunk_fwd(xh):
    seq = xh.shape[2]
    rows_per_step = GROUP * CHUNK

    def body(x_ref, inv_ref, qk_ref, u_ref, w_ref):
        def group(gi, carry):
            rows = [_chunk_rows(gi * UNROLL + step) for step in range(UNROLL)]
            xs = [[x_ref[j, 0, r, :] for j in range(5)] for r in rows]
            terms = [_chunk_terms(*x) for x in xs]
            invs = _unit_lower_inverses([t[5] for t in terms])
            for r, x, t, inv in zip(rows, xs, terms, invs):
                e, kb, qk = t[0], t[4], t[6]
                inv_parts = _split(inv)
                inv_ref[0, r, :] = inv
                qk_ref[0, r, :] = qk
                u_ref[0, r, :] = _dot3(inv_parts, _split(x[2] * x[3]))
                w_ref[0, r, :] = _dot3(inv_parts, _split(kb * e))
            return carry

        lax.fori_loop(0, GROUP // UNROLL, group, 0)

    out = pl.BlockSpec((1, rows_per_step, HEAD_DIM), lambda h, g: (h, g, 0))
    return pl.pallas_call(
        body,
        grid=(N_HEADS, seq // rows_per_step),
        in_specs=[pl.BlockSpec((5, 1, rows_per_step, HEAD_DIM), lambda h, g: (0, h, g, 0))],
        out_specs=[out] * 4,
        out_shape=[jax.ShapeDtypeStruct((N_HEADS, seq, HEAD_DIM), F32)] * 4,
        compiler_params=_params(("parallel", "parallel")),
        name="delta_chunk_fwd",
    )(xh)


def _decays(gc):
    g_last = jnp.broadcast_to(gc[CHUNK - 1:CHUNK, :], (CHUNK, CHUNK))
    return jnp.exp(gc), jnp.exp(g_last - gc), jnp.exp(g_last)


def _head_blocks(index, n_steps=None):
    rows_per_step = GROUP * CHUNK
    if n_steps is None:
        return pl.BlockSpec((1, N_HEADS, rows_per_step, HEAD_DIM), lambda g: (index, 0, g, 0))
    return pl.BlockSpec((1, N_HEADS, rows_per_step, HEAD_DIM), lambda g: (index, 0, n_steps - 1 - g, 0))


def _delta_scan_fwd(xh, qk_h, u_h, w_h):
    seq = xh.shape[2]
    rows_per_step = GROUP * CHUNK

    def body(q_ref, k_ref, gc_ref, qk_ref, u_ref, w_ref, o_ref, st_ref, state):
        @pl.when(pl.program_id(0) == 0)
        def _():
            state[...] = jnp.zeros_like(state)

        def chunk(ci, carry):
            rows = _chunk_rows(ci)
            heads = range(N_HEADS)
            dec = [_decays(gc_ref[0, h, rows, :]) for h in heads]
            s = [state[h] for h in heads]
            sb = [s[h].astype(BF16) for h in heads]
            vnb = [(u_ref[h, rows, :] - _nn(w_ref[h, rows, :].astype(BF16), sb[h])).astype(BF16) for h in heads]
            for h in heads:
                o_ref[h, rows, :] = (_nn((q_ref[0, h, rows, :] * dec[h][0]).astype(BF16), sb[h])
                                     + _nn(qk_ref[h, rows, :].astype(BF16), vnb[h]))
                st_ref[h, rows, :] = s[h]
            for h in heads:
                state[h] = s[h] * dec[h][2] + _tn((k_ref[0, h, rows, :] * dec[h][1]).astype(BF16), vnb[h])
            return carry

        lax.fori_loop(0, GROUP, chunk, 0)

    blk = pl.BlockSpec((N_HEADS, rows_per_step, HEAD_DIM), lambda g: (0, g, 0))
    return pl.pallas_call(
        body,
        grid=(seq // rows_per_step,),
        in_specs=[_head_blocks(0), _head_blocks(1), _head_blocks(4), blk, blk, blk],
        out_specs=[blk, blk],
        out_shape=[jax.ShapeDtypeStruct((N_HEADS, seq, HEAD_DIM), F32)] * 2,
        scratch_shapes=[pltpu.VMEM((N_HEADS, CHUNK, CHUNK), F32)],
        compiler_params=_params(("arbitrary",)),
        name="delta_scan_fwd",
    )(xh, xh, xh, qk_h, u_h, w_h)


def _delta_scan_bwd(xh, qk_h, w_h, do_h):
    seq = xh.shape[2]
    rows_per_step = GROUP * CHUNK
    n_steps = seq // rows_per_step

    def body(q_ref, k_ref, gc_ref, qk_ref, w_ref, do_ref, dsn_ref, dvn_ref, dstate):
        @pl.when(pl.program_id(0) == 0)
        def _():
            dstate[...] = jnp.zeros_like(dstate)

        def chunk(step, carry):
            rows = _chunk_rows(GROUP - 1 - step)
            heads = range(N_HEADS)
            dec = [_decays(gc_ref[0, h, rows, :]) for h in heads]
            ds_next = [dstate[h] for h in heads]
            dob = [do_ref[h, rows, :].astype(BF16) for h in heads]
            dv_new = [_tn(qk_ref[h, rows, :].astype(BF16), dob[h])
                      + _nn((k_ref[0, h, rows, :] * dec[h][1]).astype(BF16), ds_next[h].astype(BF16)) for h in heads]
            for h in heads:
                dsn_ref[h, rows, :] = ds_next[h]
                dvn_ref[h, rows, :] = dv_new[h]
            for h in heads:
                dstate[h] = (_tn((q_ref[0, h, rows, :] * dec[h][0]).astype(BF16), dob[h]) + dec[h][2] * ds_next[h]
                             - _tn(w_ref[h, rows, :].astype(BF16), dv_new[h].astype(BF16)))
            return carry

        lax.fori_loop(0, GROUP, chunk, 0)

    blk = pl.BlockSpec((N_HEADS, rows_per_step, HEAD_DIM), lambda g: (0, n_steps - 1 - g, 0))
    return pl.pallas_call(
        body,
        grid=(n_steps,),
        in_specs=[_head_blocks(0, n_steps), _head_blocks(1, n_steps), _head_blocks(4, n_steps), blk, blk, blk],
        out_specs=[blk, blk],
        out_shape=[jax.ShapeDtypeStruct((N_HEADS, seq, HEAD_DIM), F32)] * 2,
        scratch_shapes=[pltpu.VMEM((N_HEADS, CHUNK, CHUNK), F32)],
        compiler_params=_params(("arbitrary",)),
        name="delta_scan_bwd",
    )(xh, xh, xh, qk_h, w_h, do_h)


def _delta_chunk_bwd(xh, inv_h, u_h, w_h, st_h, dsn_h, dvn_h, do_h):
    seq = xh.shape[2]
    rows_per_step = GROUP * CHUNK

    def body(x_ref, inv_ref, u_ref, w_ref, st_ref, dsn_ref, dvn_ref, do_ref, dx_ref):
        causal, strict = _tri(True), _tri(True, strict=True)
        last_row = lax.broadcasted_iota(jnp.int32, (CHUNK, CHUNK), 0) == CHUNK - 1

        def bf(vals):
            return [val.astype(BF16) for val in vals]

        def group(gi, carry):
            rows = [_chunk_rows(gi * UNROLL + step) for step in range(UNROLL)]
            n = range(UNROLL)
            q, k, v, beta, gc = [[x_ref[j, 0, r, :] for r in rows] for j in range(5)]
            terms = [_chunk_terms(q[i], k[i], v[i], beta[i], gc[i]) for i in n]
            e, f, e_last, decay, kb, a_mat, qk = [[t[j] for t in terms] for j in range(7)]
            inv = [_split(inv_ref[0, r, :]) for r in rows]
            u = [u_ref[0, r, :] for r in rows]
            w = [w_ref[0, r, :] for r in rows]
            s = [st_ref[0, r, :] for r in rows]
            ds_next = [dsn_ref[0, r, :] for r in rows]
            dv_new = [dvn_ref[0, r, :] for r in rows]
            sb, dsb, dvb, wb = bf(s), bf(ds_next), bf(dv_new), bf(w)
            dob = bf([do_ref[0, r, :] for r in rows])
            qbf, kbf, kbb = bf(q), bf(k), bf(kb)
            vnb = bf([u[i] - _nn(wb[i], sb[i]) for i in n])
            dqe = [_nt(dob[i], sb[i]) for i in n]
            dw = [-_nt(dvb[i], sb[i]) for i in n]
            dkf = [_nt(vnb[i], dsb[i]) for i in n]
            dqk = [jnp.where(causal, _nt(dob[i], vnb[i]), 0.0) for i in n]
            drhs_u = [_dot3(inv[i], _split(dv_new[i]), _tn) for i in n]
            drhs_w = [_dot3(inv[i], _split(dw[i]), _tn) for i in n]
            da = [-jnp.where(strict, _nt(drhs_u[i].astype(BF16), u[i].astype(BF16))
                             + _nt(drhs_w[i].astype(BF16), wb[i]), 0.0) for i in n]
            dad = bf([da[i] * decay[i] for i in n])
            dqd = bf([dqk[i] * decay[i] for i in n])
            dkb = [e[i] * drhs_w[i] + _nn(dad[i], kbf[i]) for i in n]
            dk = [_tn(dad[i], kbb[i]) + _tn(dqd[i], qbf[i]) + f[i] * dkf[i] + beta[i] * dkb[i] for i in n]
            dq = [_nn(dqd[i], kbf[i]) + e[i] * dqe[i] for i in n]
            for i in n:
                de_full = kb[i] * drhs_w[i] + q[i] * dqe[i]
                df_full = k[i] * dkf[i]
                m = da[i] * a_mat[i] + dqk[i] * qk[i]
                dgc = de_full * e[i] - df_full * f[i] + m - m.T
                tail = jnp.sum(df_full * f[i] + s[i] * ds_next[i] * e_last[i], axis=0, keepdims=True)
                dgc = dgc + jnp.where(last_row, jnp.broadcast_to(tail, (CHUNK, CHUNK)), 0.0)
                dx_ref[0, 0, rows[i], :] = dq[i]
                dx_ref[1, 0, rows[i], :] = dk[i]
                dx_ref[2, 0, rows[i], :] = beta[i] * drhs_u[i]
                dx_ref[3, 0, rows[i], :] = v[i] * drhs_u[i] + k[i] * dkb[i]
                dx_ref[4, 0, rows[i], :] = dgc
            return carry

        lax.fori_loop(0, GROUP // UNROLL, group, 0)

    blk = pl.BlockSpec((1, rows_per_step, HEAD_DIM), lambda h, g: (h, g, 0))
    blk5 = pl.BlockSpec((5, 1, rows_per_step, HEAD_DIM), lambda h, g: (0, h, g, 0))
    return pl.pallas_call(
        body,
        grid=(N_HEADS, seq // rows_per_step),
        in_specs=[blk5] + [blk] * 7,
        out_specs=blk5,
        out_shape=jax.ShapeDtypeStruct((5, N_HEADS, seq, HEAD_DIM), F32),
        compiler_params=_params(("parallel", "parallel")),
        name="delta_chunk_bwd",
    )(xh, inv_h, u_h, w_h, st_h, dsn_h, dvn_h, do_h)


def _delta_post_fwd(o, qkvz, gain_row):
    seq = o.shape[0]

    def body(o_ref, z_ref, g_ref, y_ref):
        ov = o_ref[...]
        ms = _pick(ov * ov, _head_sum_matrix()) * (1.0 / HEAD_DIM)
        rb = _pick(lax.rsqrt(ms + EPS), _head_spread_matrix())
        y_ref[...] = (ov * rb * g_ref[...] * _silu(z_ref[...])).astype(y_ref.dtype)

    tile = pl.BlockSpec((ROW_TILE, HEAD_W), lambda i: (i, 0))
    return pl.pallas_call(
        body,
        grid=(seq // ROW_TILE,),
        in_specs=[tile, pl.BlockSpec((ROW_TILE, HEAD_W), lambda i: (i, 3)), pl.BlockSpec((1, HEAD_W), lambda i: (0, 0))],
        out_specs=tile,
        out_shape=jax.ShapeDtypeStruct((seq, HEAD_W), BF16),
        compiler_params=_params(("arbitrary",)),
        name="delta_post_fwd",
    )(o, qkvz, gain_row)


def _delta_post_bwd(dy, o, qkvz, gain_row):
    seq = o.shape[0]

    def body(dy_ref, o_ref, z_ref, g_ref, do_ref, dz_ref, dg_ref):
        @pl.when(pl.program_id(0) == 0)
        def _():
            dg_ref[...] = jnp.zeros_like(dg_ref)

        ov, zv, dyv, gain = o_ref[...], z_ref[...], dy_ref[...], g_ref[...]
        hsum, hspread = _head_sum_matrix(), _head_spread_matrix()
        ms = _pick(ov * ov, hsum) * (1.0 / HEAD_DIM)
        rb = _pick(lax.rsqrt(ms + EPS), hspread)
        ohat = ov * rb
        dz_ref[...] = dyv * ohat * gain * _dsilu(zv)
        dn = dyv * _silu(zv)
        dg_ref[0:1, :] += jnp.sum(dn * ohat, axis=0, keepdims=True)
        dohat = dn * gain

        @pl.when(pl.program_id(0) == pl.num_programs(0) - 1)
        def _():
            fold = (lax.broadcasted_iota(jnp.int32, (HEAD_W, HEAD_W), 0) % HEAD_DIM
                    == lax.broadcasted_iota(jnp.int32, (HEAD_W, HEAD_W), 1)).astype(F32)
            dg_ref[1:2, :] = _pick(dg_ref[0:1, :], fold)

        proj = _pick(_pick(dohat * ohat, hsum) * (1.0 / HEAD_DIM), hspread)
        do_ref[...] = rb * (dohat - ohat * proj)

    tile = pl.BlockSpec((ROW_TILE, HEAD_W), lambda i: (i, 0))
    return pl.pallas_call(
        body,
        grid=(seq // ROW_TILE,),
        in_specs=[pl.BlockSpec((ROW_TILE, HEAD_W), lambda i: (i, 1)), tile,
                  pl.BlockSpec((ROW_TILE, HEAD_W), lambda i: (i, 3)), pl.BlockSpec((1, HEAD_W), lambda i: (0, 0))],
        out_specs=[tile, tile, pl.BlockSpec((2, HEAD_W), lambda i: (0, 0))],
        out_shape=[jax.ShapeDtypeStruct((seq, HEAD_W), F32), jax.ShapeDtypeStruct((seq, HEAD_W), F32),
                   jax.ShapeDtypeStruct((2, HEAD_W), F32)],
        compiler_params=_params(("arbitrary",)),
        name="delta_post_bwd",
    )(dy, o, qkvz, gain_row)


def _delta_prep_bwd(qkvz, ba, conv_w, alog_row, dt_row, dxs):
    seq = qkvz.shape[0]
    qkv_w = 3 * HEAD_W

    def body(x_ref, xp_ref, ba_ref, w_ref, al_ref, dt_ref, dx_ref, dconv_ref, dba_ref, dvec_ref, stage):
        i = pl.program_id(0)

        @pl.when(i == 0)
        def _():
            dvec_ref[...] = jnp.zeros_like(dvec_ref)

        _stage_rows(stage, x_ref, xp_ref, i)
        pre = _conv_taps(stage, w_ref, ROW_TILE)
        act = _silu(pre)
        slope = _dsilu(pre)
        hsum, hspread = _head_sum_matrix(), _head_spread_matrix()
        for j, scale in ((0, HEAD_DIM ** -0.5), (1, 1.0)):
            cols = slice(j * HEAD_W, (j + 1) * HEAD_W)
            xc = act[:, cols]
            rb = _l2_scale(xc, hsum, hspread)
            xhat = xc * rb
            dhat = dx_ref[j] * scale
            proj = _pick(_pick(dhat * xhat, hsum), hspread)
            dconv_ref[:, cols] = rb * (dhat - xhat * proj) * slope[:, cols]
        dconv_ref[:, 2 * HEAD_W:] = dx_ref[2] * slope[:, 2 * HEAD_W:]

        bav = ba_ref[...]
        beta8 = _sigmoid(bav)
        dbeta8 = _pick(dx_ref[3], _head_gather_matrix())
        dgc8 = _pick(dx_ref[4], _head_gather_matrix(N_HEADS))
        rev = _tri(False).astype(F32)
        z = bav + dt_ref[...]
        ea = jnp.exp(al_ref[...])
        g8 = -ea * _softplus(z)
        sig = _sigmoid(z)
        d_alog = jnp.zeros((1, LANES), F32)
        d_dt = jnp.zeros((1, LANES), F32)
        for ch in range(ROW_TILE // CHUNK):
            rows = slice(ch * CHUNK, (ch + 1) * CHUNK)
            dg8 = _pick_left(rev, dgc8[rows])
            da = -dg8 * ea * sig[rows]
            dba_ref[rows, :] = dbeta8[rows] * beta8[rows] * (1.0 - beta8[rows]) + da
            d_alog = d_alog + jnp.sum(dg8 * g8[rows], axis=0, keepdims=True)
            d_dt = d_dt + jnp.sum(da, axis=0, keepdims=True)
        dvec_ref[0:1, :] += d_alog
        dvec_ref[1:2, :] += d_dt

    return pl.pallas_call(
        body,
        grid=(seq // ROW_TILE,),
        in_specs=[
            pl.BlockSpec((ROW_TILE, qkv_w), lambda i: (i, 0)),
            pl.BlockSpec((8, qkv_w), lambda i: (jnp.maximum(i * (ROW_TILE // 8) - 1, 0), 0)),
            pl.BlockSpec((ROW_TILE, LANES), lambda i: (i, 0)),
            pl.BlockSpec((4, qkv_w), lambda i: (0, 0)),
            pl.BlockSpec((1, LANES), lambda i: (0, 0)),
            pl.BlockSpec((1, LANES), lambda i: (0, 0)),
            pl.BlockSpec((5, ROW_TILE, HEAD_W), lambda i: (0, i, 0)),
        ],
        out_specs=[pl.BlockSpec((ROW_TILE, qkv_w), lambda i: (i, 0)),
                   pl.BlockSpec((ROW_TILE, LANES), lambda i: (i, 0)),
                   pl.BlockSpec((2, LANES), lambda i: (0, 0))],
        out_shape=[jax.ShapeDtypeStruct((seq, qkv_w), F32), jax.ShapeDtypeStruct((seq, LANES), F32),
                   jax.ShapeDtypeStruct((2, LANES), F32)],
        scratch_shapes=[pltpu.VMEM((ROW_TILE + 8, qkv_w), F32)],
        compiler_params=_params(("arbitrary",)),
        name="delta_prep_bwd",
    )(qkvz, qkvz, ba, conv_w, alog_row, dt_row, dxs)


def _conv_bwd(dconv, qkvz, conv_w):
    seq = dconv.shape[0]
    qkv_w = 3 * HEAD_W
    n_tiles = seq // ROW_TILE

    def body(dy_ref, dyn_ref, x_ref, xp_ref, w_ref, dx_ref, dw_ref, stage, dstage):
        i = pl.program_id(0)

        @pl.when(i == 0)
        def _():
            dw_ref[...] = jnp.zeros_like(dw_ref)

        _stage_rows(stage, x_ref, xp_ref, i)
        dstage[0:ROW_TILE, :] = dy_ref[...]
        dstage[ROW_TILE:ROW_TILE + 8, :] = jnp.where(i == n_tiles - 1, 0.0, dyn_ref[...])
        dy = dy_ref[...]
        dx_ref[...] = (w_ref[3:4, :] * dy + w_ref[2:3, :] * dstage[1:1 + ROW_TILE, :]
                       + w_ref[1:2, :] * dstage[2:2 + ROW_TILE, :] + w_ref[0:1, :] * dstage[3:3 + ROW_TILE, :])
        for j in range(4):
            dw_ref[j:j + 1, :] += jnp.sum(dy * stage[5 + j:5 + j + ROW_TILE, :], axis=0, keepdims=True)

    tile = pl.BlockSpec((ROW_TILE, qkv_w), lambda i: (i, 0))
    return pl.pallas_call(
        body,
        grid=(n_tiles,),
        in_specs=[
            tile,
            pl.BlockSpec((8, qkv_w), lambda i: (jnp.minimum((i + 1) * (ROW_TILE // 8), seq // 8 - 1), 0)),
            tile,
            pl.BlockSpec((8, qkv_w), lambda i: (jnp.maximum(i * (ROW_TILE // 8) - 1, 0), 0)),
            pl.BlockSpec((4, qkv_w), lambda i: (0, 0)),
        ],
        out_specs=[tile, pl.BlockSpec((4, qkv_w), lambda i: (0, 0))],
        out_shape=[jax.ShapeDtypeStruct((seq, qkv_w), F32), jax.ShapeDtypeStruct((4, qkv_w), F32)],
        scratch_shapes=[pltpu.VMEM((ROW_TILE + 8, qkv_w), F32), pltpu.VMEM((ROW_TILE + 8, qkv_w), F32)],
        compiler_params=_params(("arbitrary",)),
        name="conv_bwd",
    )(dconv, dconv, qkvz, qkvz, conv_w)


def _to_heads(a):
    lead = a.shape[:-2]
    seq = a.shape[-2]
    a = a.reshape(*lead, seq, N_HEADS, HEAD_DIM)
    return jnp.swapaxes(a, -2, -3)


def _from_heads(a):
    a = jnp.swapaxes(a, -2, -3)
    return a.reshape(*a.shape[:-2], HEAD_W)


FF_TILE = 1408


def _row(a):
    return pl.BlockSpec((1, a), lambda *_: (0, 0))


def _rms_fwd(xv, gain):
    rstd = lax.rsqrt(jnp.mean(xv * xv, axis=-1, keepdims=True) + EPS)
    xhat = xv * rstd
    return xhat, rstd, xhat * gain


def _rms_bwd(dnorm, xhat, rstd, gain):
    dxhat = dnorm * gain
    dx = rstd * (dxhat - xhat * jnp.mean(dxhat * xhat, axis=-1, keepdims=True))
    return dx, jnp.sum(dnorm * xhat, axis=0, keepdims=True)


def _inproj_fwd(x, gain, scale, shift, w_a, w_d, w_ba):
    seq = x.shape[0]

    def body(x_ref, g_ref, sc_ref, sh_ref, wa_ref, wd_ref, wb_ref, h_ref, a_ref, d_ref, b_ref):
        _, _, norm = _rms_fwd(x_ref[...], g_ref[...])
        h = (norm * (1.0 + sc_ref[...]) + sh_ref[...]).astype(BF16)
        h_ref[...] = h
        a_ref[...] = _nn(h, wa_ref[...])
        d_ref[...] = _nn(h, wd_ref[...])
        b_ref[...] = _nn(h, wb_ref[...])

    def rows(width):
        return pl.BlockSpec((ROW_TILE, width), lambda i: (i, 0))

    def whole(a):
        return pl.BlockSpec(a.shape, lambda i: (0, 0))

    return pl.pallas_call(
        body,
        grid=(seq // ROW_TILE,),
        in_specs=[rows(D_MODEL), _row(D_MODEL), _row(D_MODEL), _row(D_MODEL), whole(w_a), whole(w_d), whole(w_ba)],
        out_specs=[rows(D_MODEL), rows(3 * HEAD_W), rows(4 * HEAD_W), rows(LANES)],
        out_shape=[jax.ShapeDtypeStruct((seq, D_MODEL), BF16), jax.ShapeDtypeStruct((seq, 3 * HEAD_W), F32),
                   jax.ShapeDtypeStruct((seq, 4 * HEAD_W), F32), jax.ShapeDtypeStruct((seq, LANES), F32)],
        compiler_params=_params(("arbitrary",)),
        name="inproj_fwd",
    )(x, gain, scale, shift, w_a, w_d, w_ba)


def _outproj_fwd(y_attn, y_delta, w_out, x, gate1, gain, scale, shift):
    seq = x.shape[0]

    def body(ya_ref, yd_ref, wa_ref, wd_ref, x_ref, g1_ref, g_ref, sc_ref, sh_ref, x1_ref, h_ref, y_ref):
        y = _nn(ya_ref[...].astype(BF16), wa_ref[...]) + _nn(yd_ref[...], wd_ref[...])
        x1 = x_ref[...] + g1_ref[...] * y
        _, _, norm = _rms_fwd(x1, g_ref[...])
        x1_ref[...] = x1
        h_ref[...] = (norm * (1.0 + sc_ref[...]) + sh_ref[...]).astype(BF16)
        y_ref[...] = y.astype(BF16)

    def rows(width):
        return pl.BlockSpec((ROW_TILE, width), lambda i: (i, 0))

    return pl.pallas_call(
        body,
        grid=(seq // ROW_TILE,),
        in_specs=[rows(HEAD_W), rows(HEAD_W),
                  pl.BlockSpec((HEAD_W, D_MODEL), lambda i: (0, 0)), pl.BlockSpec((HEAD_W, D_MODEL), lambda i: (1, 0)),
                  rows(D_MODEL), _row(D_MODEL), _row(D_MODEL), _row(D_MODEL), _row(D_MODEL)],
        out_specs=[rows(D_MODEL), rows(D_MODEL), rows(D_MODEL)],
        out_shape=[jax.ShapeDtypeStruct((seq, D_MODEL), F32), jax.ShapeDtypeStruct((seq, D_MODEL), BF16),
                   jax.ShapeDtypeStruct((seq, D_MODEL), BF16)],
        compiler_params=_params(("arbitrary",)),
        name="outproj_fwd",
    )(y_attn, y_delta, w_out, w_out, x, gate1, gain, scale, shift)


def _ffn_fwd(h2, w_gate, w_up, w_down, x1, gate2, final_gain, target):
    seq = h2.shape[0]
    n_rows, n_ff = seq // ROW_TILE, D_FF // FF_TILE

    def body(h_ref, wg_ref, wu_ref, wd_ref, x1_ref, g2_ref, gf_ref, t_ref, gate_ref, up_ref, dx2_ref, st_ref, acc):
        i, j = pl.program_id(0), pl.program_id(1)

        @pl.when((i == 0) & (j == 0))
        def _():
            st_ref[...] = jnp.zeros_like(st_ref)

        h = h_ref[...]
        gate = _nn(h, wg_ref[...])
        up = _nn(h, wu_ref[...])
        gate_ref[...] = gate.astype(BF16)
        up_ref[...] = up.astype(BF16)
        part = _nn((_silu(gate) * up).astype(BF16), wd_ref[...])

        @pl.when(j == 0)
        def _():
            acc[...] = part

        @pl.when(j > 0)
        def _():
            acc[...] += part

        @pl.when(j == n_ff - 1)
        def _():
            y2 = acc[...]
            x2 = x1_ref[...] + g2_ref[...] * y2
            xhat, rstd, out = _rms_fwd(x2, gf_ref[...])
            diff = out - t_ref[...]
            dx2, dgain = _rms_bwd(diff * (1.0 / D_MODEL), xhat, rstd, gf_ref[...])
            dx2_ref[...] = dx2
            st_ref[0:1, :] += dgain
            st_ref[1:2, :] += jnp.sum(dx2 * y2, axis=0, keepdims=True)
            st_ref[2:3, :] += jnp.sum(diff * diff, axis=0, keepdims=True) * (0.5 / D_MODEL)

        @pl.when((i == n_rows - 1) & (j == n_ff - 1))
        def _():
            st_ref[3:4, :] = jnp.broadcast_to(jnp.sum(st_ref[2:3, :], keepdims=True), (1, D_MODEL))

    def rows(width):
        return pl.BlockSpec((ROW_TILE, width), lambda i, j: (i, 0))

    ff = pl.BlockSpec((ROW_TILE, FF_TILE), lambda i, j: (i, j))
    return pl.pallas_call(
        body,
        grid=(n_rows, n_ff),
        in_specs=[rows(D_MODEL),
                  pl.BlockSpec((D_MODEL, FF_TILE), lambda i, j: (0, j)), pl.BlockSpec((D_MODEL, FF_TILE), lambda i, j: (0, j)),
                  pl.BlockSpec((FF_TILE, D_MODEL), lambda i, j: (j, 0)),
                  rows(D_MODEL), _row(D_MODEL), _row(D_MODEL), rows(D_MODEL)],
        out_specs=[ff, ff, rows(D_MODEL), pl.BlockSpec((8, D_MODEL), lambda i, j: (0, 0))],
        out_shape=[jax.ShapeDtypeStruct((seq, D_FF), BF16), jax.ShapeDtypeStruct((seq, D_FF), BF16),
                   jax.ShapeDtypeStruct((seq, D_MODEL), F32), jax.ShapeDtypeStruct((8, D_MODEL), F32)],
        scratch_shapes=[pltpu.VMEM((ROW_TILE, D_MODEL), F32)],
        compiler_params=_params(("arbitrary", "arbitrary")),
        name="ffn_fwd",
    )(h2, w_gate, w_up, w_down, x1, gate2, final_gain, target)


def _ffn_bwd(dx2, gate, up, w_gate, w_up, w_down, x1, y, gate2, gate1, gain, scale):
    seq = dx2.shape[0]

    def act_body(dx2_ref, g2_ref, gate_ref, up_ref, wd_ref, dgate_ref, dup_ref, act_ref, dy2_ref):
        dy2 = (g2_ref[...] * dx2_ref[...]).astype(BF16)
        dy2_ref[...] = dy2
        gate = gate_ref[...].astype(F32)
        up = up_ref[...].astype(F32)
        dact = _nt(dy2, wd_ref[...])
        silu = _silu(gate)
        act_ref[...] = (silu * up).astype(BF16)
        dgate_ref[...] = (dact * up * _dsilu(gate)).astype(BF16)
        dup_ref[...] = (dact * silu).astype(BF16)

    def rows2(width):
        return pl.BlockSpec((ROW_TILE, width), lambda i, j: (i, 0))

    ff = pl.BlockSpec((ROW_TILE, FF_TILE), lambda i, j: (i, j))
    dgate, dup, act, dy2 = pl.pallas_call(
        act_body,
        grid=(seq // ROW_TILE, D_FF // FF_TILE),
        in_specs=[rows2(D_MODEL), _row(D_MODEL), ff, ff, pl.BlockSpec((FF_TILE, D_MODEL), lambda i, j: (j, 0))],
        out_specs=[ff, ff, ff, rows2(D_MODEL)],
        out_shape=[jax.ShapeDtypeStruct((seq, D_FF), BF16)] * 3 + [jax.ShapeDtypeStruct((seq, D_MODEL), BF16)],
        compiler_params=_params(("arbitrary", "arbitrary")),
        name="ffn_bwd_act",
    )(dx2, gate2, gate, up, w_down)

    def in_body(dgate_ref, dup_ref, wg_ref, wu_ref, dx2_ref, x1_ref, y_ref, g1_ref, g_ref, sc_ref,
                dx1_ref, dy_ref, st_ref):
        @pl.when(pl.program_id(0) == 0)
        def _():
            st_ref[...] = jnp.zeros_like(st_ref)

        dh = _nt(dgate_ref[...], wg_ref[...]) + _nt(dup_ref[...], wu_ref[...])
        xhat, rstd, norm = _rms_fwd(x1_ref[...], g_ref[...])
        dxn, dgain = _rms_bwd(dh * (1.0 + sc_ref[...]), xhat, rstd, g_ref[...])
        dx1 = dx2_ref[...] + dxn
        dx1_ref[...] = dx1
        dy_ref[...] = (g1_ref[...] * dx1).astype(BF16)
        st_ref[0:1, :] += jnp.sum(dh, axis=0, keepdims=True)
        st_ref[1:2, :] += jnp.sum(dh * norm, axis=0, keepdims=True)
        st_ref[2:3, :] += dgain
        st_ref[3:4, :] += jnp.sum(dx1 * y_ref[...].astype(F32), axis=0, keepdims=True)

    half_tile = ROW_TILE // 2

    def rows(width):
        return pl.BlockSpec((half_tile, width), lambda i: (i, 0))

    whole = pl.BlockSpec((D_MODEL, D_FF), lambda i: (0, 0))
    dx1, dy, stats = pl.pallas_call(
        in_body,
        grid=(seq // half_tile,),
        in_specs=[rows(D_FF), rows(D_FF), whole, whole, rows(D_MODEL), rows(D_MODEL), rows(D_MODEL),
                  _row(D_MODEL), _row(D_MODEL), _row(D_MODEL)],
        out_specs=[rows(D_MODEL), rows(D_MODEL), pl.BlockSpec((8, D_MODEL), lambda i: (0, 0))],
        out_shape=[jax.ShapeDtypeStruct((seq, D_MODEL), F32), jax.ShapeDtypeStruct((seq, D_MODEL), BF16),
                   jax.ShapeDtypeStruct((8, D_MODEL), F32)],
        compiler_params=_params(("arbitrary",)),
        name="ffn_bwd_in",
    )(dgate, dup, w_gate, w_up, dx2, x1, y, gate1, gain, scale)
    return dgate, dup, act, dy2, dx1, dy, stats


def _outproj_bwd(dy, w_out):
    seq = dy.shape[0]

    def body(dy_ref, w_ref, out_ref):
        out_ref[...] = _nt(dy_ref[...], w_ref[...])

    rows = pl.BlockSpec((ROW_TILE, D_MODEL), lambda i: (i, 0))
    return pl.pallas_call(
        body,
        grid=(seq // ROW_TILE,),
        in_specs=[rows, pl.BlockSpec((D_MODEL, D_MODEL), lambda i: (0, 0))],
        out_specs=rows,
        out_shape=jax.ShapeDtypeStruct((seq, D_MODEL), F32),
        compiler_params=_params(("arbitrary",)),
        name="outproj_bwd",
    )(dy, w_out)


def _inproj_bwd(dq, dk, dv, dxd, dz, dba, w_a, w_d, w_ba, x, dx1, gain, scale):
    seq = x.shape[0]

    def body(dq_ref, dk_ref, dv_ref, dxd_ref, dz_ref, dba_ref, wa_ref, wd_ref, wb_ref, x_ref, dx1_ref, g_ref, sc_ref,
             gx_ref, st_ref):
        @pl.when(pl.program_id(0) == 0)
        def _():
            st_ref[...] = jnp.zeros_like(st_ref)

        dh = (_nt(dq_ref[...].astype(BF16), wa_ref[:, 0:HEAD_W])
              + _nt(dk_ref[...].astype(BF16), wa_ref[:, HEAD_W:2 * HEAD_W])
              + _nt(dv_ref[...].astype(BF16), wa_ref[:, 2 * HEAD_W:])
              + _nt(dxd_ref[...].astype(BF16), wd_ref[:, 0:3 * HEAD_W])
              + _nt(dz_ref[...].astype(BF16), wd_ref[:, 3 * HEAD_W:])
              + _nt(dba_ref[...].astype(BF16), wb_ref[...]))
        xhat, rstd, norm = _rms_fwd(x_ref[...], g_ref[...])
        dxn, dgain = _rms_bwd(dh * (1.0 + sc_ref[...]), xhat, rstd, g_ref[...])
        gx_ref[...] = dx1_ref[...] + dxn
        st_ref[0:1, :] += jnp.sum(dh, axis=0, keepdims=True)
        st_ref[1:2, :] += jnp.sum(dh * norm, axis=0, keepdims=True)
        st_ref[2:3, :] += dgain

    def rows(width):
        return pl.BlockSpec((ROW_TILE, width), lambda i: (i, 0))

    def whole(a):
        return pl.BlockSpec(a.shape, lambda i: (0, 0))

    return pl.pallas_call(
        body,
        grid=(seq // ROW_TILE,),
        in_specs=[rows(HEAD_W), rows(HEAD_W), rows(HEAD_W), rows(3 * HEAD_W), rows(HEAD_W), rows(LANES),
                  whole(w_a), whole(w_d), whole(w_ba), rows(D_MODEL), rows(D_MODEL), _row(D_MODEL), _row(D_MODEL)],
        out_specs=[rows(D_MODEL), pl.BlockSpec((8, D_MODEL), lambda i: (0, 0))],
        out_shape=[jax.ShapeDtypeStruct((seq, D_MODEL), F32), jax.ShapeDtypeStruct((8, D_MODEL), F32)],
        compiler_params=_params(("arbitrary",)),
        name="inproj_bwd",
    )(dq, dk, dv, dxd, dz, dba, w_a, w_d, w_ba, x, dx1, gain, scale)


def _weight_grad(a, b, name):
    seq, m = a.shape
    n = b.shape[1]
    tm = m if m <= 1536 else m // 2
    tn = n if n <= 1536 else n // 2
    n_k = seq // ROW_TILE

    def body(a_ref, b_ref, out_ref):
        part = _tn(a_ref[...].astype(BF16), b_ref[...].astype(BF16))

        @pl.when(pl.program_id(2) == 0)
        def _():
            out_ref[...] = part

        @pl.when(pl.program_id(2) > 0)
        def _():
            out_ref[...] += part

    return pl.pallas_call(
        body,
        grid=(m // tm, n // tn, n_k),
        in_specs=[pl.BlockSpec((ROW_TILE, tm), lambda i, j, k: (k, i)),
                  pl.BlockSpec((ROW_TILE, tn), lambda i, j, k: (k, j))],
        out_specs=pl.BlockSpec((tm, tn), lambda i, j, k: (i, j)),
        out_shape=jax.ShapeDtypeStruct((m, n), F32),
        compiler_params=_params(("arbitrary", "arbitrary", "arbitrary")),
        name=name,
    )(a, b)


def _adamw(w, g, m, v, name):
    n_rows, n_cols = w.shape
    if n_rows % 256 == 0:
        block, grid, index = (256, n_cols), (n_rows // 256,), lambda i: (i, 0)
    elif n_cols % 256 == 0:
        block, grid, index = (n_rows, 256), (n_cols // 256,), lambda i: (0, i)
    else:
        block, grid, index = (n_rows, n_cols), (1,), lambda i: (0, 0)

    def body(w_ref, g_ref, m_ref, v_ref, d_ref, nm_ref, nv_ref):
        gv = g_ref[...]
        nm = ADAM_B1 * m_ref[...] + (1.0 - ADAM_B1) * gv
        nv = ADAM_B2 * v_ref[...] + (1.0 - ADAM_B2) * (gv * gv)
        m_hat = nm / (1.0 - ADAM_B1 ** ADAM_STEP)
        v_hat = nv / (1.0 - ADAM_B2 ** ADAM_STEP)
        d_ref[...] = -ADAM_LR * (m_hat / (jnp.sqrt(v_hat) + ADAM_EPS) + ADAM_WD * w_ref[...])
        nm_ref[...] = nm
        nv_ref[...] = nv

    blk = pl.BlockSpec(block, index)
    shape = jax.ShapeDtypeStruct((n_rows, n_cols), F32)
    return pl.pallas_call(
        body,
        grid=grid,
        in_specs=[blk] * 4,
        out_specs=[blk] * 3,
        out_shape=[shape] * 3,
        compiler_params=_params(("arbitrary",)),
        name=name,
    )(w, g, m, v)


IN_WIDTH = 3600
BA_COL = 7 * HEAD_W


def _local_step(x, target, mod, norm_attn_g, w_in, rel_bias, conv_w, a_log, dt_bias, delta_norm_g, w_out,
                norm_ffn_g, w_gate, w_up, w_down, final_norm_g):
    sh1, sc1, g1, sh2, sc2, g2 = [mod[:, i * D_MODEL:(i + 1) * D_MODEL] for i in range(6)]
    w_a = w_in[:, :3 * HEAD_W]
    w_d = w_in[:, 3 * HEAD_W:BA_COL]
    w_ba = jnp.pad(w_in[:, BA_COL:], ((0, 0), (0, LANES - 2 * N_HEADS)))
    tables = jnp.asarray(_attn_tables())
    alog_row = jnp.pad(a_log, ((0, 0), (N_HEADS, LANES - 2 * N_HEADS)))
    dt_row = jnp.pad(dt_bias, ((0, 0), (N_HEADS, LANES - 2 * N_HEADS)))
    gain_row = jnp.tile(delta_norm_g, (1, N_HEADS))

    h1, qkv_a, qkvz, ba = _inproj_fwd(x, norm_attn_g, sc1, sh1, w_a, w_d, w_ba)
    y_attn, lse = _attention_fwd(qkv_a, rel_bias, tables)
    xh = _to_heads(_delta_prep_fwd(qkvz, ba, conv_w, alog_row, dt_row))
    inv_h, qk_h, u_h, w_h = _delta_chunk_fwd(xh)
    o_h, st_h = _delta_scan_fwd(xh, qk_h, u_h, w_h)
    o = _from_heads(o_h)
    y_delta = _delta_post_fwd(o, qkvz, gain_row)
    x1, h2, y = _outproj_fwd(y_attn, y_delta, w_out, x, g1, norm_ffn_g, sc2, sh2)
    gate, up, dx2, st_f = _ffn_fwd(h2, w_gate, w_up, w_down, x1, g2, final_norm_g, target)

    dgate, dup, act, dy2, dx1, dy, st_b = _ffn_bwd(dx2, gate, up, w_gate, w_up, w_down, x1, y, g2, g1, norm_ffn_g, sc2)
    grads = {
        "w_gate": _weight_grad(h2, dgate, "wgrad_gate"),
        "w_up": _weight_grad(h2, dup, "wgrad_up"),
        "w_down": _weight_grad(act, dy2, "wgrad_down"),
        "w_out": jnp.concatenate([_weight_grad(y_attn, dy, "wgrad_out_attn"),
                                  _weight_grad(y_delta, dy, "wgrad_out_delta")], axis=0),
    }
    dycat = _outproj_bwd(dy, w_out)
    do, dz, dgain = _delta_post_bwd(dycat, o, qkvz, gain_row)
    do_h = _to_heads(do)
    dsn_h, dvn_h = _delta_scan_bwd(xh, qk_h, w_h, do_h)
    dxs = _from_heads(_delta_chunk_bwd(xh, inv_h, u_h, w_h, st_h, dsn_h, dvn_h, do_h))
    dconv, dba, dvec = _delta_prep_bwd(qkvz, ba, conv_w, alog_row, dt_row, dxs)
    dxd, grads["conv_w"] = _conv_bwd(dconv, qkvz, conv_w)
    dq, dk, dv, dbias = _attention_bwd(qkv_a, dycat, y_attn, lse, rel_bias, tables)
    grad_x, st_i = _inproj_bwd(dq, dk, dv, dxd, dz, dba, w_a, w_d, w_ba, x, dx1, norm_attn_g, sc1)
    grads["w_in"] = jnp.concatenate(
        [_weight_grad(h1, dq, "wgrad_in_q"), _weight_grad(h1, dk, "wgrad_in_k"), _weight_grad(h1, dv, "wgrad_in_v"),
         _weight_grad(h1, dxd, "wgrad_in_delta"), _weight_grad(h1, dz, "wgrad_in_z"),
         _weight_grad(h1, dba, "wgrad_in_gates")[:, :2 * N_HEADS]], axis=1)
    grads["rel_bias"] = _rel_bias_grad(dbias, tables)[:, :N_BUCKETS].T
    grads["a_log"] = dvec[0:1, N_HEADS:2 * N_HEADS]
    grads["dt_bias"] = dvec[1:2, N_HEADS:2 * N_HEADS]
    grads["delta_norm_g"] = dgain[1:2, :HEAD_DIM]
    grads["norm_attn_g"] = st_i[2:3]
    grads["norm_ffn_g"] = st_b[2:3]
    grads["final_norm_g"] = st_f[0:1]
    dmod = jnp.concatenate([st_i[0:1], st_i[1:2], st_b[3:4], st_b[0:1], st_b[1:2], st_f[1:2]], axis=1)
    return st_f[3, 0], grad_x, grads, dmod


MESH = pl.DeviceIdType.MESH
OTHER_CHIPS = ((1, 0), (0, 1), (1, 1))
ALL_PEERS = tuple((m >> 2 & 1, m >> 1 & 1, m & 1) for m in range(1, 8))
ANY = pl.BlockSpec(memory_space=pl.ANY)
VMEM_SPEC = pl.BlockSpec(memory_space=pltpu.VMEM)
N_BIG = 5


def _me():
    return lax.axis_index("x"), lax.axis_index("y"), lax.axis_index("c")


def _flip(pos, mask):
    return tuple(1 - p if m else p for p, m in zip(pos, mask))


def _remote(src, dst, send_sems, recv_sems, k, to):
    return pltpu.make_async_remote_copy(src_ref=src, dst_ref=dst, send_sem=send_sems.at[k], recv_sem=recv_sems.at[k],
                                        device_id=to, device_id_type=MESH)


def _ada_exchange(c8, w_ada, b_ada, conv8):
    def body(c_ref, w_ref, b_ref, cv_ref, mod_ref, cact_ref, conv_ref, c_all, part_all, send_sems, recv_sems):
        x, y, c = me = _me()
        dev = 4 * x + 2 * y + c
        chip = 2 * x + y
        c_all[dev] = c_ref[...]
        conv_ref[chip] = cv_ref[...]
        first = [_remote(c_ref, c_all.at[dev], send_sems, recv_sems, k, _flip(me, mask))
                 for k, mask in enumerate(ALL_PEERS)]
        first += [_remote(cv_ref, conv_ref.at[chip], send_sems, recv_sems, 7 + j, _flip(me, (*mask, 0)))
                  for j, mask in enumerate(OTHER_CHIPS)]
        for cp in first:
            cp.start()
        for cp in first:
            cp.wait()
        row = lax.broadcasted_iota(jnp.int32, (8, D_MODEL), 0)
        c_rows = jnp.zeros((8, D_MODEL), F32)
        for d in range(8):
            c_rows = jnp.where(row == d, c_all[d], c_rows)
        c_act = _silu(c_rows)
        cact_ref[...] = c_act
        part_all[chip] = _nn(c_act, w_ref[...], HIGHEST)
        second = [_remote(part_all.at[chip], part_all.at[chip], send_sems, recv_sems, 10 + j, _flip(me, (*mask, 0)))
                  for j, mask in enumerate(OTHER_CHIPS)]
        for cp in second:
            cp.start()
        for cp in second:
            cp.wait()
        cols = w_ref.shape[1]
        for k in range(4):
            mod_ref[:, k * cols:(k + 1) * cols] = part_all[k] + b_ref[:, k * cols:(k + 1) * cols]

    cols = w_ada.shape[1]
    return pl.pallas_call(
        body,
        in_specs=[VMEM_SPEC] * 4,
        out_specs=[VMEM_SPEC] * 3,
        out_shape=[jax.ShapeDtypeStruct((8, 4 * cols), F32), jax.ShapeDtypeStruct((8, D_MODEL), F32),
                   jax.ShapeDtypeStruct((4, 8, conv8.shape[1]), F32)],
        scratch_shapes=[pltpu.VMEM((8, 8, D_MODEL), F32), pltpu.VMEM((4, 8, cols), F32),
                        pltpu.SemaphoreType.DMA((13,)), pltpu.SemaphoreType.DMA((13,))],
        compiler_params=pltpu.CompilerParams(vmem_limit_bytes=VMEM_LIMIT),
        name="ada_exchange",
    )(c8, w_ada, b_ada, conv8)


def _gather_weights(shards):
    def body(*refs):
        srcs, dsts = refs[:N_BIG], refs[N_BIG:2 * N_BIG]
        send_sems, recv_sems = refs[2 * N_BIG:]
        x, y, c = me = _me()
        chip = 2 * x + y
        sibling = _flip(me, (0, 0, 1))
        first, passed = [], []
        for a in range(N_BIG):
            for j, mask in enumerate(OTHER_CHIPS):
                to = _flip(me, (*mask, 0))
                first.append(_remote(srcs[a].at[c], dsts[a].at[chip, c], send_sems, recv_sems, 6 * a + j, to))
                landed = dsts[a].at[2 * to[0] + to[1], c]
                passed.append(_remote(landed, landed, send_sems, recv_sems, 6 * a + 3 + j, sibling))
        for cp in first:
            cp.start()
        for cp, fwd in zip(first, passed):
            cp.wait_recv()
            fwd.start()
        for cp in first:
            cp.wait_send()
        for fwd in passed:
            fwd.wait()

    return pl.pallas_call(
        body,
        in_specs=[ANY] * N_BIG,
        out_specs=[ANY] * N_BIG,
        out_shape=[jax.ShapeDtypeStruct((4, *s.shape), s.dtype) for s in shards],
        scratch_shapes=[pltpu.SemaphoreType.DMA((6 * N_BIG,)), pltpu.SemaphoreType.DMA((6 * N_BIG,))],
        name="gather_weights",
    )(*shards)


def _start_and_wait(copies):
    for cp in copies:
        cp.start()
    for cp in copies:
        cp.wait()


def _swap_halves(grads):
    def body(*refs):
        srcs, got = refs[:N_BIG], refs[N_BIG:2 * N_BIG]
        send_sems, recv_sems = refs[2 * N_BIG:]
        x, y, c = me = _me()
        _start_and_wait([_remote(srcs[a].at[:, 1 - c], got[a], send_sems, recv_sems, a, _flip(me, (0, 0, 1)))
                         for a in range(N_BIG)])

    return pl.pallas_call(
        body,
        in_specs=[ANY] * N_BIG,
        out_specs=[ANY] * N_BIG,
        out_shape=[jax.ShapeDtypeStruct((4, g.shape[2], g.shape[3]), g.dtype) for g in grads],
        scratch_shapes=[pltpu.SemaphoreType.DMA((N_BIG,)), pltpu.SemaphoreType.DMA((N_BIG,))],
        name="swap_halves",
    )(*grads)


def _scatter_partials(partials):
    def body(*refs):
        srcs, dsts = refs[:N_BIG], refs[N_BIG:2 * N_BIG]
        send_sems, recv_sems = refs[2 * N_BIG:]
        x, y, c = me = _me()
        chip = 2 * x + y
        copies = []
        for a in range(N_BIG):
            for j, mask in enumerate(OTHER_CHIPS):
                to = _flip(me, (*mask, 0))
                copies.append(_remote(srcs[a].at[2 * to[0] + to[1]], dsts[a].at[chip], send_sems, recv_sems, 3 * a + j, to))
        _start_and_wait(copies)

    return pl.pallas_call(
        body,
        in_specs=[ANY] * N_BIG,
        out_specs=[ANY] * N_BIG,
        out_shape=[jax.ShapeDtypeStruct(p.shape, p.dtype) for p in partials],
        scratch_shapes=[pltpu.SemaphoreType.DMA((3 * N_BIG,)), pltpu.SemaphoreType.DMA((3 * N_BIG,))],
        name="scatter_partials",
    )(*partials)


def _join_halves(halves):
    def body(*refs):
        srcs, dsts = refs[:N_BIG], refs[N_BIG:2 * N_BIG]
        send_sems, recv_sems = refs[2 * N_BIG:]
        x, y, c = me = _me()
        _start_and_wait([_remote(srcs[a], dsts[a].at[c], send_sems, recv_sems, a, _flip(me, (0, 0, 1)))
                         for a in range(N_BIG)])

    return pl.pallas_call(
        body,
        in_specs=[ANY] * N_BIG,
        out_specs=[ANY] * N_BIG,
        out_shape=[jax.ShapeDtypeStruct((2, *h.shape), h.dtype) for h in halves],
        scratch_shapes=[pltpu.SemaphoreType.DMA((N_BIG,)), pltpu.SemaphoreType.DMA((N_BIG,))],
        name="join_halves",
    )(*halves)


def _gather_small(packed):
    n_rows = packed.shape[0]

    def body(p_ref, all_ref, sum_ref, send_sems, recv_sems):
        x, y, c = me = _me()
        dev = 4 * x + 2 * y + c
        all_ref[dev] = p_ref[...]
        copies = [_remote(p_ref, all_ref.at[dev], send_sems, recv_sems, k, _flip(me, mask))
                  for k, mask in enumerate(ALL_PEERS)]
        for cp in copies:
            cp.start()
        for cp in copies:
            cp.wait()
        total = all_ref[0]
        for d in range(1, 8):
            total = total + all_ref[d]
        sum_ref[...] = total

    return pl.pallas_call(
        body,
        in_specs=[VMEM_SPEC],
        out_specs=[VMEM_SPEC, VMEM_SPEC],
        out_shape=[jax.ShapeDtypeStruct((8, n_rows, LANES), F32), jax.ShapeDtypeStruct((n_rows, LANES), F32)],
        scratch_shapes=[pltpu.SemaphoreType.DMA((7,)), pltpu.SemaphoreType.DMA((7,))],
        name="gather_small",
    )(packed)


def _add_pair(a, b, out_dtype, name):
    def body(a_ref, b_ref, o_ref):
        o_ref[...] = (a_ref[...] + b_ref[...]).astype(o_ref.dtype)

    blk = pl.BlockSpec((1, *a.shape[1:]), lambda i: (i, 0, 0))
    return pl.pallas_call(
        body, grid=(a.shape[0],), in_specs=[blk, blk], out_specs=blk,
        out_shape=jax.ShapeDtypeStruct(a.shape, out_dtype),
        compiler_params=_params(("arbitrary",)), name=name,
    )(a, b)


def _add_slots(a, name):
    def body(a_ref, o_ref):
        total = a_ref[0].astype(F32)
        for k in range(1, 4):
            total = total + a_ref[k].astype(F32)
        o_ref[...] = total

    return pl.pallas_call(
        body, in_specs=[VMEM_SPEC], out_specs=VMEM_SPEC,
        out_shape=jax.ShapeDtypeStruct(a.shape[1:], F32),
        compiler_params=pltpu.CompilerParams(vmem_limit_bytes=VMEM_LIMIT), name=name,
    )(a)


def _ada_weight_grad(c_act, dmod_cols):
    def body(c_ref, d_ref, o_ref):
        o_ref[...] = _tn(c_ref[...], d_ref[...], HIGHEST)

    return pl.pallas_call(
        body, in_specs=[VMEM_SPEC, VMEM_SPEC], out_specs=VMEM_SPEC,
        out_shape=jax.ShapeDtypeStruct((c_act.shape[1], dmod_cols.shape[1]), F32),
        compiler_params=pltpu.CompilerParams(vmem_limit_bytes=VMEM_LIMIT), name="ada_weight_grad",
    )(c_act, dmod_cols)


def kernel(x, c, w_ada, b_ada, norm_attn_g, w_in, rel_bias, conv_w, a_log, dt_bias, delta_norm_g, w_out, norm_ffn_g, w_gate, w_up, w_down, final_norm_g, loss_target, m_w_ada, m_b_ada, m_norm_attn_g, m_w_in, m_rel_bias, m_conv_w, m_a_log, m_dt_bias, m_delta_norm_g, m_w_out, m_norm_ffn_g, m_w_gate, m_w_up, m_w_down, m_final_norm_g, v_w_ada, v_b_ada, v_norm_attn_g, v_w_in, v_rel_bias, v_conv_w, v_a_log, v_dt_bias, v_delta_norm_g, v_w_out, v_norm_ffn_g, v_w_gate, v_w_up, v_w_down, v_final_norm_g):
    xi, yi, ci = _me()
    dev = 4 * xi + 2 * yi + ci
    chip = 2 * xi + yi

    conv_cols = conv_w.shape[2]
    mod_all, c_act, conv_all = _ada_exchange(jnp.broadcast_to(c, (8, D_MODEL)), w_ada[0], b_ada,
                                             jnp.pad(conv_w[0], ((0, 4), (0, 0))))
    mod = lax.dynamic_slice_in_dim(mod_all, dev, 1, axis=0)
    conv_full = jnp.swapaxes(conv_all[:, :4, :], 0, 1).reshape(4, 4 * conv_cols)

    big_names = ("w_in", "w_out", "w_gate", "w_up", "w_down")
    by_cols = (True, False, True, True, False)

    def rows_form(a, cols):
        return jnp.swapaxes(a[0], 0, 1) if cols else a[0]

    big = [rows_form(w, cols) for w, cols in zip((w_in, w_out, w_gate, w_up, w_down), by_cols)]
    shards = [w.astype(BF16).reshape(2, w.size // (2 * LANES), LANES) for w in big]
    gathered = [lax.dynamic_update_index_in_dim(g, s, chip, 0) for g, s in zip(_gather_weights(shards), shards)]
    gathered = [g.reshape(4 * w.shape[0], w.shape[1]) for g, w in zip(gathered, big)]
    whole = [g.T if cols else g for g, cols in zip(gathered, by_cols)]

    loss, grad_x, grads, dmod = _local_step(
        x[0], loss_target[0], mod, norm_attn_g, whole[0], rel_bias, conv_full, a_log, dt_bias, delta_norm_g,
        whole[1], norm_ffn_g, whole[2], whole[3], whole[4], final_norm_g[None])

    slots = []
    for name, w, cols in zip(big_names, big, by_cols):
        g = grads[name].T if cols else grads[name]
        slots.append(g.reshape(4, 2, w.size // (2 * LANES), LANES))
    partials = [_add_pair(lax.dynamic_index_in_dim(s, ci, 1, keepdims=False), got, BF16, f"add_pair_{a}")
                for a, (s, got) in enumerate(zip(slots, _swap_halves(slots)))]
    by_source = [lax.dynamic_update_index_in_dim(b, lax.dynamic_index_in_dim(p, chip, 0, keepdims=False), chip, 0)
                 for b, p in zip(_scatter_partials(partials), partials)]
    halves = [_add_slots(p, f"add_slots_{a}") for a, p in enumerate(by_source)]
    joined = [lax.dynamic_update_index_in_dim(j, h, ci, 0) for j, h in zip(_join_halves(halves), halves)]
    big_grads = [j.reshape(w.shape) for j, w in zip(joined, big)]

    pieces = [dmod, grads["conv_w"], grads["norm_attn_g"], grads["norm_ffn_g"], grads["final_norm_g"],
              grads["rel_bias"], grads["a_log"], grads["dt_bias"], grads["delta_norm_g"]]
    flat = [jnp.pad(p.reshape(-1), (0, -p.size % LANES)) for p in pieces]
    n_rows = [f.size // LANES for f in flat]
    packed = jnp.concatenate(flat).reshape(-1, LANES)
    packed = jnp.pad(packed, ((0, -packed.shape[0] % 8), (0, 0)))
    all_small, total = _gather_small(packed)
    sums, start = [], 0
    for p, n in zip(pieces, n_rows):
        sums.append(total[start:start + n].reshape(-1)[:p.size].reshape(p.shape))
        start += n
    g_b_ada, g_conv, g_norm_attn, g_norm_ffn, g_final, g_rel, g_alog, g_dt, g_dnorm = sums
    dmod_all = all_small[:, :n_rows[0], :].reshape(8, -1)
    ada_cols = w_ada.shape[2]
    g_w_ada = _ada_weight_grad(c_act, lax.dynamic_slice_in_dim(dmod_all, chip * ada_cols, ada_cols, axis=1))
    g_conv = lax.dynamic_slice_in_dim(g_conv, chip * conv_cols, conv_cols, axis=1)

    grad = {"w_ada": g_w_ada[None], "b_ada": g_b_ada, "norm_attn_g": g_norm_attn,
            "rel_bias": g_rel, "conv_w": g_conv[None], "a_log": g_alog, "dt_bias": g_dt, "delta_norm_g": g_dnorm,
            "norm_ffn_g": g_norm_ffn, "final_norm_g": g_final.reshape(-1)}
    weight = {"w_ada": w_ada, "b_ada": b_ada, "norm_attn_g": norm_attn_g, "w_in": w_in, "rel_bias": rel_bias,
              "conv_w": conv_w, "a_log": a_log, "dt_bias": dt_bias, "delta_norm_g": delta_norm_g, "w_out": w_out,
              "norm_ffn_g": norm_ffn_g, "w_gate": w_gate, "w_up": w_up, "w_down": w_down, "final_norm_g": final_norm_g}
    first = {"w_ada": m_w_ada, "b_ada": m_b_ada, "norm_attn_g": m_norm_attn_g, "w_in": m_w_in, "rel_bias": m_rel_bias,
             "conv_w": m_conv_w, "a_log": m_a_log, "dt_bias": m_dt_bias, "delta_norm_g": m_delta_norm_g,
             "w_out": m_w_out, "norm_ffn_g": m_norm_ffn_g, "w_gate": m_w_gate, "w_up": m_w_up, "w_down": m_w_down,
             "final_norm_g": m_final_norm_g}
    second = {"w_ada": v_w_ada, "b_ada": v_b_ada, "norm_attn_g": v_norm_attn_g, "w_in": v_w_in, "rel_bias": v_rel_bias,
              "conv_w": v_conv_w, "a_log": v_a_log, "dt_bias": v_dt_bias, "delta_norm_g": v_delta_norm_g,
              "w_out": v_w_out, "norm_ffn_g": v_norm_ffn_g, "w_gate": v_w_gate, "w_up": v_w_up, "w_down": v_w_down,
              "final_norm_g": v_final_norm_g}
    delta, new_m, new_v = {}, {}, {}
    for name, w in weight.items():
        if name in big_names:
            continue
        two_d = (-1, w.shape[-1])
        d, nm, nv = _adamw(w.reshape(two_d), grad[name].reshape(two_d), first[name].reshape(two_d),
                           second[name].reshape(two_d), f"adamw_{name}")
        delta[name], new_m[name], new_v[name] = d.reshape(w.shape), nm.reshape(w.shape), nv.reshape(w.shape)
    for name, w, g, cols in zip(big_names, big, big_grads, by_cols):
        outs = _adamw(w, g, rows_form(first[name], cols), rows_form(second[name], cols), f"adamw_{name}")
        grad[name], delta[name], new_m[name], new_v[name] = [
            (jnp.swapaxes(o, 0, 1) if cols else o)[None] for o in (g, *outs)]

    names = list(weight)
    return (lax.psum(loss, ("x", "y", "c")), grad_x[None], *[grad[n] for n in names], *[delta[n] for n in names],
            *[new_m[n] for n in names], *[new_v[n] for n in names])
```

```python
import functools
import math

import numpy as np
import jax
import jax.numpy as jnp
from jax import lax
from jax.experimental import pallas as pl
from jax.experimental.pallas import tpu as pltpu

F32 = jnp.float32
BF16 = jnp.bfloat16
HIGHEST = lax.Precision.HIGHEST

D_MODEL = 1024
HEAD_DIM = 64
N_HEADS = 8
HEAD_W = 512
BRANCHES = ((128, 1), (512, 4), (2048, 16))
BAND = 128
ATT_TILE = 2048
ATT_UNROLL = 4
N_BUCKETS = 32
MAX_DISTANCE = 2048
CHUNK = 64
D_FF = 2816
EPS = 1e-6
NEG_INF = -1e30
LANES = 128
VMEM_LIMIT = 56 * 1024 * 1024

ADAM_LR = 0.001
ADAM_B1 = 0.9
ADAM_B2 = 0.999
ADAM_EPS = 1e-08
ADAM_WD = 0.01
ADAM_STEP = 10


def _nn(a, b, precision=None):
    return jnp.dot(a, b, preferred_element_type=F32, precision=precision)


def _nt(a, b, precision=None):
    return lax.dot_general(a, b, (((1,), (1,)), ((), ())), preferred_element_type=F32, precision=precision)


def _tn(a, b, precision=None):
    return lax.dot_general(a, b, (((0,), (0,)), ((), ())), preferred_element_type=F32, precision=precision)


def _params(sem, vmem=VMEM_LIMIT):
    return pltpu.CompilerParams(dimension_semantics=sem, vmem_limit_bytes=vmem)


def _sigmoid(x):
    return 1.0 / (1.0 + jnp.exp(-x))


def _silu(x):
    return x * _sigmoid(x)


def _dsilu(x):
    s = _sigmoid(x)
    return s * (1.0 + x * (1.0 - s))


def _attn_tables():
    qi = np.arange(BAND)[:, None]
    kj = np.arange(2 * BAND)[None, :]
    steps = qi + BAND - kj
    in_window = (steps >= 0) & (steps <= BAND)
    max_exact = N_BUCKETS // 2
    out = np.zeros((3, 2, BAND, 2 * BAND), np.int32)
    for b, (_, dil) in enumerate(BRANCHES):
        dist = np.maximum(steps, 0) * dil
        dist_f = np.maximum(dist, 1).astype(np.float32)
        large = max_exact + (np.log(dist_f / np.float32(max_exact)) / np.float32(math.log(MAX_DISTANCE / max_exact))
                             * np.float32(N_BUCKETS - max_exact)).astype(np.int32)
        bucket = np.where(dist < max_exact, dist, np.minimum(large, N_BUCKETS - 1)).astype(np.int32)
        out[b, 0] = np.where(in_window, bucket, -1)
        out[b, 1] = np.where(in_window & (kj >= BAND), bucket, -1)
    return out


def _attn_bias_tables(rel_ref, tab_ref, bias_s, pair):
    for b in range(3):
        for first in range(2):
            tab = tab_ref[b, first]
            for hh in range(2):
                head = 2 * pair + hh

                def pick(kk, acc, tab=tab, head=head):
                    return jnp.where(tab == kk, rel_ref[kk, head], acc)

                acc = lax.fori_loop(0, N_BUCKETS, pick, jnp.zeros((BAND, 2 * BAND), F32))
                bias_s[b, hh, first] = jnp.where(tab < 0, NEG_INF, acc)


def _attn_block_index(idx, t, r):
    nb = ATT_TILE // (BAND * r)
    rho = idx // nb
    n = idx % nb
    qs = rho + r * BAND * n
    gs = t * ATT_TILE + qs
    first = (t * nb + n) == 0
    ps = jnp.where(first, gs, gs - r * BAND)
    return qs, gs, ps, first.astype(jnp.int32)


def _rows(start, r):
    return pl.ds(start, BAND) if r == 1 else pl.ds(start, BAND, stride=r)


def _attention_fwd(qkv, rel_bias, tables):
    seq = qkv.shape[0]
    n_tiles = seq // ATT_TILE

    def body(rel_ref, tab_ref, q_ref, k_ref, v_ref, y_ref, lse_ref, bias_s, o_s, l_s):
        pair = pl.program_id(0)
        t = pl.program_id(1)
        lane = lax.broadcasted_iota(jnp.int32, (1, LANES), 1)
        head0 = lane < HEAD_DIM

        @pl.when(t == 0)
        def _():
            _attn_bias_tables(rel_ref, tab_ref, bias_s, pair)

        masks = (head0, jnp.logical_not(head0))
        for b, (_, r) in enumerate(BRANCHES):
            def blocks(it, carry, b=b, r=r):
                idx = [_attn_block_index(it * ATT_UNROLL + j, t, r) for j in range(ATT_UNROLL)]
                qb = [q_ref[_rows(qs, r), :] * (HEAD_DIM ** -0.5) for qs, _, _, _ in idx]
                kcat = [jnp.concatenate([k_ref[_rows(ps, r), :], k_ref[_rows(gs, r), :]], axis=0).astype(BF16)
                        for _, gs, ps, _ in idx]
                vcat = [jnp.concatenate([v_ref[_rows(ps, r), :], v_ref[_rows(gs, r), :]], axis=0).astype(BF16)
                        for _, gs, ps, _ in idx]
                work = [(j, hh) for j in range(ATT_UNROLL) for hh in range(2)]
                s = [_nt(jnp.where(masks[hh], qb[j], 0.0).astype(BF16), kcat[j]) + bias_s[b, hh, idx[j][3]]
                     for j, hh in work]
                m = [jnp.max(sv, axis=-1, keepdims=True) for sv in s]
                e = [jnp.exp(sv - mv) for sv, mv in zip(s, m)]
                den = [jnp.sum(ev, axis=-1, keepdims=True) for ev in e]
                out = [_nn(ev.astype(BF16), vcat[j]) / dv for ev, dv, (j, _) in zip(e, den, work)]
                lse = [mv + jnp.log(dv) for mv, dv in zip(m, den)]
                for j in range(ATT_UNROLL):
                    o_s[b, _rows(idx[j][0], r), :] = jnp.where(head0, out[2 * j], out[2 * j + 1])
                    l_s[b, _rows(idx[j][0], r), :] = jnp.where(head0, lse[2 * j], lse[2 * j + 1])
                return carry

            lax.fori_loop(0, ATT_TILE // BAND // ATT_UNROLL, blocks, 0)

        def merge(i, carry):
            rows = pl.ds(pl.multiple_of(i * BAND, BAND), BAND)
            l0, l1, l2 = l_s[0, rows, :], l_s[1, rows, :], l_s[2, rows, :]
            m = jnp.maximum(jnp.maximum(l0, l1), l2)
            w0, w1, w2 = jnp.exp(l0 - m), jnp.exp(l1 - m), jnp.exp(l2 - m)
            tot = w0 + w1 + w2
            y_ref[rows, :] = (w0 * o_s[0, rows, :] + w1 * o_s[1, rows, :] + w2 * o_s[2, rows, :]) / tot
            lse_ref[rows, :] = m + jnp.log(tot)
            return carry

        lax.fori_loop(0, ATT_TILE // BAND, merge, 0)

    tile = pl.BlockSpec((ATT_TILE, LANES), lambda p, t: (t, p))
    return pl.pallas_call(
        body,
        grid=(N_HEADS // 2, n_tiles),
        in_specs=[
            pl.BlockSpec(memory_space=pltpu.SMEM),
            pl.BlockSpec((3, 2, BAND, 2 * BAND), lambda p, t: (0, 0, 0, 0)),
            pl.BlockSpec((ATT_TILE, LANES), lambda p, t: (t, p)),
            pl.BlockSpec((seq, LANES), lambda p, t: (0, 4 + p)),
            pl.BlockSpec((seq, LANES), lambda p, t: (0, 8 + p)),
        ],
        out_specs=[tile, tile],
        out_shape=[jax.ShapeDtypeStruct((seq, HEAD_W), F32), jax.ShapeDtypeStruct((seq, HEAD_W), F32)],
        scratch_shapes=[
            pltpu.VMEM((3, 2, 2, BAND, 2 * BAND), F32),
            pltpu.VMEM((3, ATT_TILE, LANES), F32),
            pltpu.VMEM((3, ATT_TILE, LANES), F32),
        ],
        compiler_params=_params(("arbitrary", "arbitrary")),
        name="attn_fwd",
    )(rel_bias, tables, qkv, qkv, qkv)


def _attention_bwd(qkv, dy, y, lse, rel_bias, tables):
    seq = qkv.shape[0]
    n_tiles = seq // ATT_TILE

    def body(rel_ref, tab_ref, q_ref, k_ref, v_ref, dy_ref, y_ref, lse_ref,
             dq_ref, dk_ref, dv_ref, dbias_ref, bias_s):
        pair = pl.program_id(0)
        t = pl.program_id(1)
        lane = lax.broadcasted_iota(jnp.int32, (1, LANES), 1)
        head0 = lane < HEAD_DIM

        @pl.when(t == 0)
        def _():
            _attn_bias_tables(rel_ref, tab_ref, bias_s, pair)
            dk_ref[...] = jnp.zeros_like(dk_ref)
            dv_ref[...] = jnp.zeros_like(dv_ref)
            dbias_ref[...] = jnp.zeros_like(dbias_ref)

        dq_ref[...] = jnp.zeros_like(dq_ref)

        masks = (head0, jnp.logical_not(head0))
        scale = HEAD_DIM ** -0.5
        for b, (_, r) in enumerate(BRANCHES):
            def blocks(it, carry, b=b, r=r):
                idx = [_attn_block_index(it * ATT_UNROLL + j, t, r) for j in range(ATT_UNROLL)]
                qb = [q_ref[_rows(qs, r), :] * scale for qs, _, _, _ in idx]
                kcat = [jnp.concatenate([k_ref[_rows(ps, r), :], k_ref[_rows(gs, r), :]], axis=0).astype(BF16)
                        for _, gs, ps, _ in idx]
                vcat = [jnp.concatenate([v_ref[_rows(ps, r), :], v_ref[_rows(gs, r), :]], axis=0).astype(BF16)
                        for _, gs, ps, _ in idx]
                dob = [dy_ref[_rows(qs, r), :] for qs, _, _, _ in idx]
                ob = [y_ref[_rows(qs, r), :] for qs, _, _, _ in idx]
                lb = [lse_ref[_rows(qs, r), :] for qs, _, _, _ in idx]
                work = [(j, hh) for j in range(ATT_UNROLL) for hh in range(2)]
                qh = [jnp.where(masks[hh], qb[j], 0.0).astype(BF16) for j, hh in work]
                doh = [jnp.where(masks[hh], dob[j], 0.0) for j, hh in work]
                dohb = [d.astype(BF16) for d in doh]
                s = [_nt(qh[w], kcat[j]) + bias_s[b, hh, idx[j][3]] for w, (j, hh) in enumerate(work)]
                dp = [_nt(dohb[w], vcat[j]) for w, (j, _) in enumerate(work)]
                lcol = [jnp.max(jnp.where(masks[hh], lb[j], -jnp.inf), axis=-1, keepdims=True) for j, hh in work]
                delta = [jnp.sum(doh[w] * ob[j], axis=-1, keepdims=True) for w, (j, _) in enumerate(work)]
                prob = [jnp.exp(sv - lv) for sv, lv in zip(s, lcol)]
                ds = [pv * (dv - de) for pv, dv, de in zip(prob, dp, delta)]
                dsb = [d.astype(BF16) for d in ds]
                dq = [_nn(dsb[w], kcat[j]) for w, (j, _) in enumerate(work)]
                dkc = [_tn(dsb[w], qh[w]) for w in range(len(work))]
                dvc = [_tn(prob[w].astype(BF16), dohb[w]) for w in range(len(work))]
                for w, (j, hh) in enumerate(work):
                    dbias_ref[0, b, hh] += ds[w]
                for j in range(ATT_UNROLL):
                    qs, gs, ps, _ = idx[j]
                    dkcat = dkc[2 * j] + dkc[2 * j + 1]
                    dvcat = dvc[2 * j] + dvc[2 * j + 1]
                    dq_ref[_rows(qs, r), :] += jnp.where(head0, dq[2 * j], dq[2 * j + 1]) * scale
                    dk_ref[_rows(ps, r), :] += dkcat[:BAND]
                    dk_ref[_rows(gs, r), :] += dkcat[BAND:]
                    dv_ref[_rows(ps, r), :] += dvcat[:BAND]
                    dv_ref[_rows(gs, r), :] += dvcat[BAND:]
                return carry

            lax.fori_loop(0, ATT_TILE // BAND // ATT_UNROLL, blocks, 0)

    tile = pl.BlockSpec((ATT_TILE, LANES), lambda p, t: (t, p))
    full = pl.BlockSpec((seq, LANES), lambda p, t: (0, p))
    return pl.pallas_call(
        body,
        grid=(N_HEADS // 2, n_tiles),
        in_specs=[
            pl.BlockSpec(memory_space=pltpu.SMEM),
            pl.BlockSpec((3, 2, BAND, 2 * BAND), lambda p, t: (0, 0, 0, 0)),
            pl.BlockSpec((ATT_TILE, LANES), lambda p, t: (t, p)),
            pl.BlockSpec((seq, LANES), lambda p, t: (0, 4 + p)),
            pl.BlockSpec((seq, LANES), lambda p, t: (0, 8 + p)),
            tile, tile, tile,
        ],
        out_specs=[tile, full, full,
                   pl.BlockSpec((1, 3, 2, BAND, 2 * BAND), lambda p, t: (p, 0, 0, 0, 0))],
        out_shape=[jax.ShapeDtypeStruct((seq, HEAD_W), F32)] * 3
        + [jax.ShapeDtypeStruct((N_HEADS // 2, 3, 2, BAND, 2 * BAND), F32)],
        scratch_shapes=[pltpu.VMEM((3, 2, 2, BAND, 2 * BAND), F32)],
        compiler_params=_params(("arbitrary", "arbitrary")),
        name="attn_bwd",
    )(rel_bias, tables, qkv, qkv, qkv, dy, y, lse)


def _rel_bias_grad(dbias, tables):
    def body(tab_ref, db_ref, out_ref):
        lane = lax.broadcasted_iota(jnp.int32, (1, LANES), 1)
        out_ref[...] = jnp.zeros_like(out_ref)
        for b in range(3):
            tab = tab_ref[b, 0]

            def head(h, carry, b=b, tab=tab):
                d = db_ref[h // 2, b, h % 2]
                sums = [jnp.sum(jnp.where(tab == kk, d, 0.0), keepdims=True) for kk in range(N_BUCKETS)]
                row = jnp.zeros((1, LANES), F32)
                for kk, s in enumerate(sums):
                    row = row + jnp.where(lane == kk, s, 0.0)
                out_ref[pl.ds(h, 1), :] += row
                return carry

            lax.fori_loop(0, N_HEADS, head, 0)

    return pl.pallas_call(
        body,
        out_shape=jax.ShapeDtypeStruct((N_HEADS, LANES), F32),
        compiler_params=pltpu.CompilerParams(vmem_limit_bytes=VMEM_LIMIT),
        name="rel_bias_grad",
    )(tables, dbias)


ROW_TILE = 512


def _head_sum_matrix():
    return (lax.broadcasted_iota(jnp.int32, (HEAD_W, LANES), 0) // HEAD_DIM
            == lax.broadcasted_iota(jnp.int32, (HEAD_W, LANES), 1)).astype(F32)


def _head_spread_matrix(offset=0):
    return (lax.broadcasted_iota(jnp.int32, (LANES, HEAD_W), 0)
            == lax.broadcasted_iota(jnp.int32, (LANES, HEAD_W), 1) // HEAD_DIM + offset).astype(F32)


def _head_gather_matrix(offset=0):
    return (lax.broadcasted_iota(jnp.int32, (HEAD_W, LANES), 0) // HEAD_DIM + offset
            == lax.broadcasted_iota(jnp.int32, (HEAD_W, LANES), 1)).astype(F32)


def _split3(x):
    hi = x.astype(BF16)
    rest = x - hi.astype(F32)
    mid = rest.astype(BF16)
    return hi, mid, (rest - mid.astype(F32)).astype(BF16)


def _pick(x, onehot):
    m = onehot.astype(BF16)
    hi, mid, lo = _split3(x)
    return _nn(hi, m) + (_nn(mid, m) + _nn(lo, m))


def _pick_left(onehot, x):
    m = onehot.astype(BF16)
    hi, mid, lo = _split3(x)
    return _nn(m, hi) + (_nn(m, mid) + _nn(m, lo))


def _tri(lower, strict=False):
    r = lax.broadcasted_iota(jnp.int32, (CHUNK, CHUNK), 0)
    c = lax.broadcasted_iota(jnp.int32, (CHUNK, CHUNK), 1)
    if lower:
        return (c < r) if strict else (c <= r)
    return c >= r


def _softplus(z):
    return jnp.maximum(z, 0.0) + jnp.log(1.0 + jnp.exp(-jnp.abs(z)))


def _conv_taps(stage, w_ref, rows):
    return (w_ref[3:4, :] * stage[8:8 + rows, :] + w_ref[2:3, :] * stage[7:7 + rows, :]
            + w_ref[1:2, :] * stage[6:6 + rows, :] + w_ref[0:1, :] * stage[5:5 + rows, :])


def _l2_scale(xc, hsum, hspread):
    ssq = _pick(xc * xc, hsum)
    return _pick(lax.rsqrt(ssq + EPS), hspread)


def _stage_rows(stage, x_ref, xp_ref, i):
    stage[0:8, :] = jnp.where(i == 0, 0.0, xp_ref[...])
    stage[8:8 + ROW_TILE, :] = x_ref[...]


def _delta_prep_fwd(qkvz, ba, conv_w, alog_row, dt_row):
    seq = qkvz.shape[0]
    qkv_w = 3 * HEAD_W

    def body(x_ref, xp_ref, ba_ref, w_ref, al_ref, dt_ref, out_ref, stage):
        i = pl.program_id(0)
        _stage_rows(stage, x_ref, xp_ref, i)
        act = _silu(_conv_taps(stage, w_ref, ROW_TILE))
        hsum, hspread = _head_sum_matrix(), _head_spread_matrix()
        qc, kc = act[:, :HEAD_W], act[:, HEAD_W:2 * HEAD_W]
        out_ref[0] = qc * _l2_scale(qc, hsum, hspread) * (HEAD_DIM ** -0.5)
        out_ref[1] = kc * _l2_scale(kc, hsum, hspread)
        out_ref[2] = act[:, 2 * HEAD_W:]
        bav = ba_ref[...]
        out_ref[3] = _pick(_sigmoid(bav), hspread)
        g8 = -jnp.exp(al_ref[...]) * _softplus(bav + dt_ref[...])
        gb = _pick(g8, _head_spread_matrix(N_HEADS))
        cum = _tri(True).astype(F32)
        for ch in range(ROW_TILE // CHUNK):
            rows = slice(ch * CHUNK, (ch + 1) * CHUNK)
            out_ref[4, rows, :] = _pick_left(cum, gb[rows])

    return pl.pallas_call(
        body,
        grid=(seq // ROW_TILE,),
        in_specs=[
            pl.BlockSpec((ROW_TILE, qkv_w), lambda i: (i, 0)),
            pl.BlockSpec((8, qkv_w), lambda i: (jnp.maximum(i * (ROW_TILE // 8) - 1, 0), 0)),
            pl.BlockSpec((ROW_TILE, LANES), lambda i: (i, 0)),
            pl.BlockSpec((4, qkv_w), lambda i: (0, 0)),
            pl.BlockSpec((1, LANES), lambda i: (0, 0)),
            pl.BlockSpec((1, LANES), lambda i: (0, 0)),
        ],
        out_specs=pl.BlockSpec((5, ROW_TILE, HEAD_W), lambda i: (0, i, 0)),
        out_shape=jax.ShapeDtypeStruct((5, seq, HEAD_W), F32),
        scratch_shapes=[pltpu.VMEM((ROW_TILE + 8, qkv_w), F32)],
        compiler_params=_params(("arbitrary",)),
        name="delta_prep_fwd",
    )(qkvz, qkvz, ba, conv_w, alog_row, dt_row)


def _split(x):
    hi = x.astype(BF16)
    return hi, (x - hi.astype(F32)).astype(BF16)


def _dot3(a, b, dot=_nn):
    return dot(a[0], b[0]) + (dot(a[0], b[1]) + dot(a[1], b[0]))


def _unit_lower_inverses(mats):
    eye = (lax.broadcasted_iota(jnp.int32, (CHUNK, CHUNK), 0)
           == lax.broadcasted_iota(jnp.int32, (CHUNK, CHUNK), 1)).astype(F32)
    invs = [eye - a for a in mats]
    powers = [_split(a) for a in mats]
    for step in range(5):
        squares = [_dot3(p, p) for p in powers]
        powers = [_split(s) for s in squares]
        invs = [inv + _dot3(_split(inv), p) for inv, p in zip(invs, powers)]
    return invs


def _chunk_terms(q, k, v, beta, gc):
    causal, strict = _tri(True), _tri(True, strict=True)
    e = jnp.exp(gc)
    g_last = jnp.broadcast_to(gc[CHUNK - 1:CHUNK, :], (CHUNK, CHUNK))
    f = jnp.exp(g_last - gc)
    e_last = jnp.exp(g_last)
    decay = jnp.where(causal, jnp.exp(jnp.where(causal, gc - gc.T, 0.0)), 0.0)
    kb = k * beta
    a_mat = jnp.where(strict, _nt(kb.astype(BF16), k.astype(BF16)) * decay, 0.0)
    qk = jnp.where(causal, _nt(q.astype(BF16), k.astype(BF16)) * decay, 0.0)
    return e, f, e_last, decay, kb, a_mat, qk


GROUP = 8
UNROLL = 8


def _chunk_rows(ci):
    return pl.ds(pl.multiple_of(ci * CHUNK, CHUNK), CHUNK)


def _pair_specs(n_planes):
    return pl.BlockSpec((n_planes, GROUP * CHUNK, LANES), lambda p, g: (0, g, p))


def _delta_chunk_fwd(xs):
    seq = xs.shape[1]
    rows_per_step = GROUP * CHUNK

    def body(x_ref, inv_ref, qk_ref, u_ref, w_ref):
        for hh in range(2):
            lanes = slice(hh * HEAD_DIM, (hh + 1) * HEAD_DIM)
            rows = [slice(step * CHUNK, (step + 1) * CHUNK) for step in range(GROUP)]
            xh = [[x_ref[j, r, lanes] for j in range(5)] for r in rows]
            terms = [_chunk_terms(*x) for x in xh]
            invs = _unit_lower_inverses([t[5] for t in terms])
            for r, x, t, inv in zip(rows, xh, terms, invs):
                e, kb, qk = t[0], t[4], t[6]
                inv_parts = _split(inv)
                inv_ref[hh, r, :] = inv
                qk_ref[hh, r, :] = qk
                u_ref[hh, r, :] = _dot3(inv_parts, _split(x[2] * x[3]))
                w_ref[hh, r, :] = _dot3(inv_parts, _split(kb * e))

    out = pl.BlockSpec((2, rows_per_step, HEAD_DIM), lambda p, g: (p, g, 0))
    return pl.pallas_call(
        body,
        grid=(N_HEADS // 2, seq // rows_per_step),
        in_specs=[_pair_specs(5)],
        out_specs=[out] * 4,
        out_shape=[jax.ShapeDtypeStruct((N_HEADS, seq, HEAD_DIM), F32)] * 4,
        compiler_params=_params(("parallel", "parallel")),
        name="delta_chunk_fwd",
    )(xs)


def _decays(gc):
    g_last = jnp.broadcast_to(gc[CHUNK - 1:CHUNK, :], (CHUNK, CHUNK))
    return jnp.exp(gc), jnp.exp(g_last - gc), jnp.exp(g_last)


def _token_blocks(index, n_steps=None):
    rows_per_step = GROUP * CHUNK
    if n_steps is None:
        return pl.BlockSpec((1, rows_per_step, HEAD_W), lambda g: (index, g, 0))
    return pl.BlockSpec((1, rows_per_step, HEAD_W), lambda g: (index, n_steps - 1 - g, 0))


def _head_lanes(h):
    return pl.ds(h * HEAD_DIM, HEAD_DIM)


def _delta_scan_fwd(xs, qk_h, u_h, w_h):
    seq = xs.shape[1]
    rows_per_step = GROUP * CHUNK

    def body(q_ref, k_ref, gc_ref, qk_ref, u_ref, w_ref, o_ref, st_ref, state):
        @pl.when(pl.program_id(0) == 0)
        def _():
            state[...] = jnp.zeros_like(state)

        def chunk(ci, carry):
            rows = _chunk_rows(ci)
            heads = range(N_HEADS)
            dec = [_decays(gc_ref[0, rows, _head_lanes(h)]) for h in heads]
            s = [state[h] for h in heads]
            sb = [s[h].astype(BF16) for h in heads]
            vnb = [(u_ref[h, rows, :] - _nn(w_ref[h, rows, :].astype(BF16), sb[h])).astype(BF16) for h in heads]
            for h in heads:
                o_ref[rows, _head_lanes(h)] = (_nn((q_ref[0, rows, _head_lanes(h)] * dec[h][0]).astype(BF16), sb[h])
                                               + _nn(qk_ref[h, rows, :].astype(BF16), vnb[h]))
                st_ref[h, rows, :] = s[h]
            for h in heads:
                state[h] = s[h] * dec[h][2] + _tn((k_ref[0, rows, _head_lanes(h)] * dec[h][1]).astype(BF16), vnb[h])
            return carry

        lax.fori_loop(0, GROUP, chunk, 0)

    blk = pl.BlockSpec((N_HEADS, rows_per_step, HEAD_DIM), lambda g: (0, g, 0))
    return pl.pallas_call(
        body,
        grid=(seq // rows_per_step,),
        in_specs=[_token_blocks(0), _token_blocks(1), _token_blocks(4), blk, blk, blk],
        out_specs=[pl.BlockSpec((rows_per_step, HEAD_W), lambda g: (g, 0)), blk],
        out_shape=[jax.ShapeDtypeStruct((seq, HEAD_W), F32), jax.ShapeDtypeStruct((N_HEADS, seq, HEAD_DIM), F32)],
        scratch_shapes=[pltpu.VMEM((N_HEADS, CHUNK, CHUNK), F32)],
        compiler_params=_params(("arbitrary",)),
        name="delta_scan_fwd",
    )(xs, xs, xs, qk_h, u_h, w_h)


def _delta_scan_bwd(xs, qk_h, w_h, do):
    seq = xs.shape[1]
    rows_per_step = GROUP * CHUNK
    n_steps = seq // rows_per_step

    def body(q_ref, k_ref, gc_ref, qk_ref, w_ref, do_ref, dsn_ref, dvn_ref, dstate):
        @pl.when(pl.program_id(0) == 0)
        def _():
            dstate[...] = jnp.zeros_like(dstate)

        def chunk(step, carry):
            rows = _chunk_rows(GROUP - 1 - step)
            heads = range(N_HEADS)
            dec = [_decays(gc_ref[0, rows, _head_lanes(h)]) for h in heads]
            ds_next = [dstate[h] for h in heads]
            dob = [do_ref[rows, _head_lanes(h)].astype(BF16) for h in heads]
            dv_new = [_tn(qk_ref[h, rows, :].astype(BF16), dob[h])
                      + _nn((k_ref[0, rows, _head_lanes(h)] * dec[h][1]).astype(BF16), ds_next[h].astype(BF16))
                      for h in heads]
            for h in heads:
                dsn_ref[h, rows, :] = ds_next[h]
                dvn_ref[h, rows, :] = dv_new[h]
            for h in heads:
                dstate[h] = (_tn((q_ref[0, rows, _head_lanes(h)] * dec[h][0]).astype(BF16), dob[h])
                             + dec[h][2] * ds_next[h] - _tn(w_ref[h, rows, :].astype(BF16), dv_new[h].astype(BF16)))
            return carry

        lax.fori_loop(0, GROUP, chunk, 0)

    blk = pl.BlockSpec((N_HEADS, rows_per_step, HEAD_DIM), lambda g: (0, n_steps - 1 - g, 0))
    return pl.pallas_call(
        body,
        grid=(n_steps,),
        in_specs=[_token_blocks(0, n_steps), _token_blocks(1, n_steps), _token_blocks(4, n_steps), blk, blk,
                  pl.BlockSpec((rows_per_step, HEAD_W), lambda g: (n_steps - 1 - g, 0))],
        out_specs=[blk, blk],
        out_shape=[jax.ShapeDtypeStruct((N_HEADS, seq, HEAD_DIM), F32)] * 2,
        scratch_shapes=[pltpu.VMEM((N_HEADS, CHUNK, CHUNK), F32)],
        compiler_params=_params(("arbitrary",)),
        name="delta_scan_bwd",
    )(xs, xs, xs, qk_h, w_h, do)


def _delta_chunk_bwd(xs, inv_h, u_h, w_h, st_h, dsn_h, dvn_h, do):
    seq = xs.shape[1]
    rows_per_step = GROUP * CHUNK

    def body(x_ref, inv_ref, u_ref, w_ref, st_ref, dsn_ref, dvn_ref, do_ref, dx_ref):
        causal, strict = _tri(True), _tri(True, strict=True)
        last_row = lax.broadcasted_iota(jnp.int32, (CHUNK, CHUNK), 0) == CHUNK - 1

        def bf(vals):
            return [val.astype(BF16) for val in vals]

        def group(hh):
            lanes = slice(hh * HEAD_DIM, (hh + 1) * HEAD_DIM)
            rows = [slice(step * CHUNK, (step + 1) * CHUNK) for step in range(GROUP)]
            n = range(GROUP)
            q, k, v, beta, gc = [[x_ref[j, r, lanes] for r in rows] for j in range(5)]
            terms = [_chunk_terms(q[i], k[i], v[i], beta[i], gc[i]) for i in n]
            e, f, e_last, decay, kb, a_mat, qk = [[t[j] for t in terms] for j in range(7)]
            inv = [_split(inv_ref[hh, r, :]) for r in rows]
            u = [u_ref[hh, r, :] for r in rows]
            w = [w_ref[hh, r, :] for r in rows]
            s = [st_ref[hh, r, :] for r in rows]
            ds_next = [dsn_ref[hh, r, :] for r in rows]
            dv_new = [dvn_ref[hh, r, :] for r in rows]
            sb, dsb, dvb, wb = bf(s), bf(ds_next), bf(dv_new), bf(w)
            dob = bf([do_ref[r, lanes] for r in rows])
            qbf, kbf, kbb = bf(q), bf(k), bf(kb)
            vnb = bf([u[i] - _nn(wb[i], sb[i]) for i in n])
            dqe = [_nt(dob[i], sb[i]) for i in n]
            dw = [-_nt(dvb[i], sb[i]) for i in n]
            dkf = [_nt(vnb[i], dsb[i]) for i in n]
            dqk = [jnp.where(causal, _nt(dob[i], vnb[i]), 0.0) for i in n]
            drhs_u = [_dot3(inv[i], _split(dv_new[i]), _tn) for i in n]
            drhs_w = [_dot3(inv[i], _split(dw[i]), _tn) for i in n]
            da = [-jnp.where(strict, _nt(drhs_u[i].astype(BF16), u[i].astype(BF16))
                             + _nt(drhs_w[i].astype(BF16), wb[i]), 0.0) for i in n]
            dad = bf([da[i] * decay[i] for i in n])
            dqd = bf([dqk[i] * decay[i] for i in n])
            dkb = [e[i] * drhs_w[i] + _nn(dad[i], kbf[i]) for i in n]
            dk = [_tn(dad[i], kbb[i]) + _tn(dqd[i], qbf[i]) + f[i] * dkf[i] + beta[i] * dkb[i] for i in n]
            dq = [_nn(dqd[i], kbf[i]) + e[i] * dqe[i] for i in n]
            for i in n:
                de_full = kb[i] * drhs_w[i] + q[i] * dqe[i]
                df_full = k[i] * dkf[i]
                m = da[i] * a_mat[i] + dqk[i] * qk[i]
                dgc = de_full * e[i] - df_full * f[i] + m - m.T
                tail = jnp.sum(df_full * f[i] + s[i] * ds_next[i] * e_last[i], axis=0, keepdims=True)
                dgc = dgc + jnp.where(last_row, jnp.broadcast_to(tail, (CHUNK, CHUNK)), 0.0)
                dx_ref[0, rows[i], lanes] = dq[i]
                dx_ref[1, rows[i], lanes] = dk[i]
                dx_ref[2, rows[i], lanes] = beta[i] * drhs_u[i]
                dx_ref[3, rows[i], lanes] = v[i] * drhs_u[i] + k[i] * dkb[i]
                dx_ref[4, rows[i], lanes] = dgc

        for hh in range(2):
            group(hh)

    blk = pl.BlockSpec((2, rows_per_step, HEAD_DIM), lambda p, g: (p, g, 0))
    return pl.pallas_call(
        body,
        grid=(N_HEADS // 2, seq // rows_per_step),
        in_specs=[_pair_specs(5)] + [blk] * 6 + [pl.BlockSpec((rows_per_step, LANES), lambda p, g: (g, p))],
        out_specs=_pair_specs(5),
        out_shape=jax.ShapeDtypeStruct((5, seq, HEAD_W), F32),
        compiler_params=_params(("parallel", "parallel")),
        name="delta_chunk_bwd",
    )(xs, inv_h, u_h, w_h, st_h, dsn_h, dvn_h, do)


def _delta_post_fwd(o, qkvz, gain_row):
    seq = o.shape[0]

    def body(o_ref, z_ref, g_ref, y_ref):
        ov = o_ref[...]
        ms = _pick(ov * ov, _head_sum_matrix()) * (1.0 / HEAD_DIM)
        rb = _pick(lax.rsqrt(ms + EPS), _head_spread_matrix())
        y_ref[...] = (ov * rb * g_ref[...] * _silu(z_ref[...])).astype(y_ref.dtype)

    tile = pl.BlockSpec((ROW_TILE, HEAD_W), lambda i: (i, 0))
    return pl.pallas_call(
        body,
        grid=(seq // ROW_TILE,),
        in_specs=[tile, pl.BlockSpec((ROW_TILE, HEAD_W), lambda i: (i, 3)), pl.BlockSpec((1, HEAD_W), lambda i: (0, 0))],
        out_specs=tile,
        out_shape=jax.ShapeDtypeStruct((seq, HEAD_W), BF16),
        compiler_params=_params(("arbitrary",)),
        name="delta_post_fwd",
    )(o, qkvz, gain_row)


def _delta_post_bwd(dy, o, qkvz, gain_row):
    seq = o.shape[0]

    def body(dy_ref, o_ref, z_ref, g_ref, do_ref, dz_ref, dg_ref):
        @pl.when(pl.program_id(0) == 0)
        def _():
            dg_ref[...] = jnp.zeros_like(dg_ref)

        ov, zv, dyv, gain = o_ref[...], z_ref[...], dy_ref[...], g_ref[...]
        hsum, hspread = _head_sum_matrix(), _head_spread_matrix()
        ms = _pick(ov * ov, hsum) * (1.0 / HEAD_DIM)
        rb = _pick(lax.rsqrt(ms + EPS), hspread)
        ohat = ov * rb
        dz_ref[...] = dyv * ohat * gain * _dsilu(zv)
        dn = dyv * _silu(zv)
        dg_ref[0:1, :] += jnp.sum(dn * ohat, axis=0, keepdims=True)
        dohat = dn * gain

        @pl.when(pl.program_id(0) == pl.num_programs(0) - 1)
        def _():
            fold = (lax.broadcasted_iota(jnp.int32, (HEAD_W, HEAD_W), 0) % HEAD_DIM
                    == lax.broadcasted_iota(jnp.int32, (HEAD_W, HEAD_W), 1)).astype(F32)
            dg_ref[1:2, :] = _pick(dg_ref[0:1, :], fold)

        proj = _pick(_pick(dohat * ohat, hsum) * (1.0 / HEAD_DIM), hspread)
        do_ref[...] = rb * (dohat - ohat * proj)

    tile = pl.BlockSpec((ROW_TILE, HEAD_W), lambda i: (i, 0))
    return pl.pallas_call(
        body,
        grid=(seq // ROW_TILE,),
        in_specs=[pl.BlockSpec((ROW_TILE, HEAD_W), lambda i: (i, 1)), tile,
                  pl.BlockSpec((ROW_TILE, HEAD_W), lambda i: (i, 3)), pl.BlockSpec((1, HEAD_W), lambda i: (0, 0))],
        out_specs=[tile, tile, pl.BlockSpec((2, HEAD_W), lambda i: (0, 0))],
        out_shape=[jax.ShapeDtypeStruct((seq, HEAD_W), F32), jax.ShapeDtypeStruct((seq, HEAD_W), F32),
                   jax.ShapeDtypeStruct((2, HEAD_W), F32)],
        compiler_params=_params(("arbitrary",)),
        name="delta_post_bwd",
    )(dy, o, qkvz, gain_row)


def _delta_prep_bwd(qkvz, ba, conv_w, alog_row, dt_row, dxs):
    seq = qkvz.shape[0]
    qkv_w = 3 * HEAD_W

    def body(x_ref, xp_ref, ba_ref, w_ref, al_ref, dt_ref, dx_ref, dconv_ref, dba_ref, dvec_ref, stage):
        i = pl.program_id(0)

        @pl.when(i == 0)
        def _():
            dvec_ref[...] = jnp.zeros_like(dvec_ref)

        _stage_rows(stage, x_ref, xp_ref, i)
        pre = _conv_taps(stage, w_ref, ROW_TILE)
        act = _silu(pre)
        slope = _dsilu(pre)
        hsum, hspread = _head_sum_matrix(), _head_spread_matrix()
        for j, scale in ((0, HEAD_DIM ** -0.5), (1, 1.0)):
            cols = slice(j * HEAD_W, (j + 1) * HEAD_W)
            xc = act[:, cols]
            rb = _l2_scale(xc, hsum, hspread)
            xhat = xc * rb
            dhat = dx_ref[j] * scale
            proj = _pick(_pick(dhat * xhat, hsum), hspread)
            dconv_ref[:, cols] = rb * (dhat - xhat * proj) * slope[:, cols]
        dconv_ref[:, 2 * HEAD_W:] = dx_ref[2] * slope[:, 2 * HEAD_W:]

        bav = ba_ref[...]
        beta8 = _sigmoid(bav)
        dbeta8 = _pick(dx_ref[3], _head_gather_matrix())
        dgc8 = _pick(dx_ref[4], _head_gather_matrix(N_HEADS))
        rev = _tri(False).astype(F32)
        z = bav + dt_ref[...]
        ea = jnp.exp(al_ref[...])
        g8 = -ea * _softplus(z)
        sig = _sigmoid(z)
        d_alog = jnp.zeros((1, LANES), F32)
        d_dt = jnp.zeros((1, LANES), F32)
        for ch in range(ROW_TILE // CHUNK):
            rows = slice(ch * CHUNK, (ch + 1) * CHUNK)
            dg8 = _pick_left(rev, dgc8[rows])
            da = -dg8 * ea * sig[rows]
            dba_ref[rows, :] = dbeta8[rows] * beta8[rows] * (1.0 - beta8[rows]) + da
            d_alog = d_alog + jnp.sum(dg8 * g8[rows], axis=0, keepdims=True)
            d_dt = d_dt + jnp.sum(da, axis=0, keepdims=True)
        dvec_ref[0:1, :] += d_alog
        dvec_ref[1:2, :] += d_dt

    return pl.pallas_call(
        body,
        grid=(seq // ROW_TILE,),
        in_specs=[
            pl.BlockSpec((ROW_TILE, qkv_w), lambda i: (i, 0)),
            pl.BlockSpec((8, qkv_w), lambda i: (jnp.maximum(i * (ROW_TILE // 8) - 1, 0), 0)),
            pl.BlockSpec((ROW_TILE, LANES), lambda i: (i, 0)),
            pl.BlockSpec((4, qkv_w), lambda i: (0, 0)),
            pl.BlockSpec((1, LANES), lambda i: (0, 0)),
            pl.BlockSpec((1, LANES), lambda i: (0, 0)),
            pl.BlockSpec((5, ROW_TILE, HEAD_W), lambda i: (0, i, 0)),
        ],
        out_specs=[pl.BlockSpec((ROW_TILE, qkv_w), lambda i: (i, 0)),
                   pl.BlockSpec((ROW_TILE, LANES), lambda i: (i, 0)),
                   pl.BlockSpec((2, LANES), lambda i: (0, 0))],
        out_shape=[jax.ShapeDtypeStruct((seq, qkv_w), F32), jax.ShapeDtypeStruct((seq, LANES), F32),
                   jax.ShapeDtypeStruct((2, LANES), F32)],
        scratch_shapes=[pltpu.VMEM((ROW_TILE + 8, qkv_w), F32)],
        compiler_params=_params(("arbitrary",)),
        name="delta_prep_bwd",
    )(qkvz, qkvz, ba, conv_w, alog_row, dt_row, dxs)


def _conv_bwd(dconv, qkvz, conv_w):
    seq = dconv.shape[0]
    qkv_w = 3 * HEAD_W
    n_tiles = seq // ROW_TILE

    def body(dy_ref, dyn_ref, x_ref, xp_ref, w_ref, dx_ref, dw_ref, stage, dstage):
        i = pl.program_id(0)

        @pl.when(i == 0)
        def _():
            dw_ref[...] = jnp.zeros_like(dw_ref)

        _stage_rows(stage, x_ref, xp_ref, i)
        dstage[0:ROW_TILE, :] = dy_ref[...]
        dstage[ROW_TILE:ROW_TILE + 8, :] = jnp.where(i == n_tiles - 1, 0.0, dyn_ref[...])
        dy = dy_ref[...]
        dx_ref[...] = (w_ref[3:4, :] * dy + w_ref[2:3, :] * dstage[1:1 + ROW_TILE, :]
                       + w_ref[1:2, :] * dstage[2:2 + ROW_TILE, :] + w_ref[0:1, :] * dstage[3:3 + ROW_TILE, :])
        for j in range(4):
            dw_ref[j:j + 1, :] += jnp.sum(dy * stage[5 + j:5 + j + ROW_TILE, :], axis=0, keepdims=True)

    tile = pl.BlockSpec((ROW_TILE, qkv_w), lambda i: (i, 0))
    return pl.pallas_call(
        body,
        grid=(n_tiles,),
        in_specs=[
            tile,
            pl.BlockSpec((8, qkv_w), lambda i: (jnp.minimum((i + 1) * (ROW_TILE // 8), seq // 8 - 1), 0)),
            tile,
            pl.BlockSpec((8, qkv_w), lambda i: (jnp.maximum(i * (ROW_TILE // 8) - 1, 0), 0)),
            pl.BlockSpec((4, qkv_w), lambda i: (0, 0)),
        ],
        out_specs=[tile, pl.BlockSpec((4, qkv_w), lambda i: (0, 0))],
        out_shape=[jax.ShapeDtypeStruct((seq, qkv_w), F32), jax.ShapeDtypeStruct((4, qkv_w), F32)],
        scratch_shapes=[pltpu.VMEM((ROW_TILE + 8, qkv_w), F32), pltpu.VMEM((ROW_TILE + 8, qkv_w), F32)],
        compiler_params=_params(("arbitrary",)),
        name="conv_bwd",
    )(dconv, dconv, qkvz, qkvz, conv_w)


FF_TILE = 1408


def _row(a):
    return pl.BlockSpec((1, a), lambda *_: (0, 0))


def _rms_fwd(xv, gain):
    rstd = lax.rsqrt(jnp.mean(xv * xv, axis=-1, keepdims=True) + EPS)
    xhat = xv * rstd
    return xhat, rstd, xhat * gain


def _rms_bwd(dnorm, xhat, rstd, gain):
    dxhat = dnorm * gain
    dx = rstd * (dxhat - xhat * jnp.mean(dxhat * xhat, axis=-1, keepdims=True))
    return dx, jnp.sum(dnorm * xhat, axis=0, keepdims=True)


def _inproj_fwd(x, gain, scale, shift, w_a, w_d, w_ba):
    seq = x.shape[0]

    def body(x_ref, g_ref, sc_ref, sh_ref, wa_ref, wd_ref, wb_ref, h_ref, a_ref, d_ref, b_ref):
        _, _, norm = _rms_fwd(x_ref[...], g_ref[...])
        h = (norm * (1.0 + sc_ref[...]) + sh_ref[...]).astype(BF16)
        h_ref[...] = h
        a_ref[...] = _nn(h, wa_ref[...])
        d_ref[...] = _nn(h, wd_ref[...])
        b_ref[...] = _nn(h, wb_ref[...])

    def rows(width):
        return pl.BlockSpec((ROW_TILE, width), lambda i: (i, 0))

    def whole(a):
        return pl.BlockSpec(a.shape, lambda i: (0, 0))

    return pl.pallas_call(
        body,
        grid=(seq // ROW_TILE,),
        in_specs=[rows(D_MODEL), _row(D_MODEL), _row(D_MODEL), _row(D_MODEL), whole(w_a), whole(w_d), whole(w_ba)],
        out_specs=[rows(D_MODEL), rows(3 * HEAD_W), rows(4 * HEAD_W), rows(LANES)],
        out_shape=[jax.ShapeDtypeStruct((seq, D_MODEL), BF16), jax.ShapeDtypeStruct((seq, 3 * HEAD_W), F32),
                   jax.ShapeDtypeStruct((seq, 4 * HEAD_W), F32), jax.ShapeDtypeStruct((seq, LANES), F32)],
        compiler_params=_params(("arbitrary",)),
        name="inproj_fwd",
    )(x, gain, scale, shift, w_a, w_d, w_ba)


def _outproj_fwd(y_attn, y_delta, w_out, x, gate1, gain, scale, shift):
    seq = x.shape[0]

    def body(ya_ref, yd_ref, wa_ref, wd_ref, x_ref, g1_ref, g_ref, sc_ref, sh_ref, x1_ref, h_ref, y_ref):
        y = _nn(ya_ref[...].astype(BF16), wa_ref[...]) + _nn(yd_ref[...], wd_ref[...])
        x1 = x_ref[...] + g1_ref[...] * y
        _, _, norm = _rms_fwd(x1, g_ref[...])
        x1_ref[...] = x1
        h_ref[...] = (norm * (1.0 + sc_ref[...]) + sh_ref[...]).astype(BF16)
        y_ref[...] = y.astype(BF16)

    def rows(width):
        return pl.BlockSpec((ROW_TILE, width), lambda i: (i, 0))

    return pl.pallas_call(
        body,
        grid=(seq // ROW_TILE,),
        in_specs=[rows(HEAD_W), rows(HEAD_W),
                  pl.BlockSpec((HEAD_W, D_MODEL), lambda i: (0, 0)), pl.BlockSpec((HEAD_W, D_MODEL), lambda i: (1, 0)),
                  rows(D_MODEL), _row(D_MODEL), _row(D_MODEL), _row(D_MODEL), _row(D_MODEL)],
        out_specs=[rows(D_MODEL), rows(D_MODEL), rows(D_MODEL)],
        out_shape=[jax.ShapeDtypeStruct((seq, D_MODEL), F32), jax.ShapeDtypeStruct((seq, D_MODEL), BF16),
                   jax.ShapeDtypeStruct((seq, D_MODEL), BF16)],
        compiler_params=_params(("arbitrary",)),
        name="outproj_fwd",
    )(y_attn, y_delta, w_out, w_out, x, gate1, gain, scale, shift)


def _ffn_fwd(h2, w_gate, w_up, w_down, x1, gate2, final_gain, target):
    seq = h2.shape[0]
    n_rows, n_ff = seq // ROW_TILE, D_FF // FF_TILE

    def body(h_ref, wg_ref, wu_ref, wd_ref, x1_ref, g2_ref, gf_ref, t_ref, gate_ref, up_ref, dx2_ref, st_ref, acc):
        i, j = pl.program_id(0), pl.program_id(1)

        @pl.when((i == 0) & (j == 0))
        def _():
            st_ref[...] = jnp.zeros_like(st_ref)

        h = h_ref[...]
        gate = _nn(h, wg_ref[...])
        up = _nn(h, wu_ref[...])
        gate_ref[...] = gate.astype(BF16)
        up_ref[...] = up.astype(BF16)
        part = _nn((_silu(gate) * up).astype(BF16), wd_ref[...])

        @pl.when(j == 0)
        def _():
            acc[...] = part

        @pl.when(j > 0)
        def _():
            acc[...] += part

        @pl.when(j == n_ff - 1)
        def _():
            y2 = acc[...]
            x2 = x1_ref[...] + g2_ref[...] * y2
            xhat, rstd, out = _rms_fwd(x2, gf_ref[...])
            diff = out - t_ref[...]
            dx2, dgain = _rms_bwd(diff * (1.0 / D_MODEL), xhat, rstd, gf_ref[...])
            dx2_ref[...] = dx2
            st_ref[0:1, :] += dgain
            st_ref[1:2, :] += jnp.sum(dx2 * y2, axis=0, keepdims=True)
            st_ref[2:3, :] += jnp.sum(diff * diff, axis=0, keepdims=True) * (0.5 / D_MODEL)

        @pl.when((i == n_rows - 1) & (j == n_ff - 1))
        def _():
            st_ref[3:4, :] = jnp.broadcast_to(jnp.sum(st_ref[2:3, :], keepdims=True), (1, D_MODEL))

    def rows(width):
        return pl.BlockSpec((ROW_TILE, width), lambda i, j: (i, 0))

    ff = pl.BlockSpec((ROW_TILE, FF_TILE), lambda i, j: (i, j))
    return pl.pallas_call(
        body,
        grid=(n_rows, n_ff),
        in_specs=[rows(D_MODEL),
                  pl.BlockSpec((D_MODEL, FF_TILE), lambda i, j: (0, j)), pl.BlockSpec((D_MODEL, FF_TILE), lambda i, j: (0, j)),
                  pl.BlockSpec((FF_TILE, D_MODEL), lambda i, j: (j, 0)),
                  rows(D_MODEL), _row(D_MODEL), _row(D_MODEL), rows(D_MODEL)],
        out_specs=[ff, ff, rows(D_MODEL), pl.BlockSpec((8, D_MODEL), lambda i, j: (0, 0))],
        out_shape=[jax.ShapeDtypeStruct((seq, D_FF), BF16), jax.ShapeDtypeStruct((seq, D_FF), BF16),
                   jax.ShapeDtypeStruct((seq, D_MODEL), F32), jax.ShapeDtypeStruct((8, D_MODEL), F32)],
        scratch_shapes=[pltpu.VMEM((ROW_TILE, D_MODEL), F32)],
        compiler_params=_params(("arbitrary", "arbitrary")),
        name="ffn_fwd",
    )(h2, w_gate, w_up, w_down, x1, gate2, final_gain, target)


def _ffn_bwd(dx2, gate, up, w_gate, w_up, w_down, x1, y, gate2, gate1, gain, scale):
    seq = dx2.shape[0]

    def act_body(dx2_ref, g2_ref, gate_ref, up_ref, wd_ref, dgate_ref, dup_ref, act_ref, dy2_ref):
        dy2 = (g2_ref[...] * dx2_ref[...]).astype(BF16)
        dy2_ref[...] = dy2
        gate = gate_ref[...].astype(F32)
        up = up_ref[...].astype(F32)
        dact = _nt(dy2, wd_ref[...])
        silu = _silu(gate)
        act_ref[...] = (silu * up).astype(BF16)
        dgate_ref[...] = (dact * up * _dsilu(gate)).astype(BF16)
        dup_ref[...] = (dact * silu).astype(BF16)

    def rows2(width):
        return pl.BlockSpec((ROW_TILE, width), lambda i, j: (i, 0))

    ff = pl.BlockSpec((ROW_TILE, FF_TILE), lambda i, j: (i, j))
    dgate, dup, act, dy2 = pl.pallas_call(
        act_body,
        grid=(seq // ROW_TILE, D_FF // FF_TILE),
        in_specs=[rows2(D_MODEL), _row(D_MODEL), ff, ff, pl.BlockSpec((FF_TILE, D_MODEL), lambda i, j: (j, 0))],
        out_specs=[ff, ff, ff, rows2(D_MODEL)],
        out_shape=[jax.ShapeDtypeStruct((seq, D_FF), BF16)] * 3 + [jax.ShapeDtypeStruct((seq, D_MODEL), BF16)],
        compiler_params=_params(("arbitrary", "arbitrary")),
        name="ffn_bwd_act",
    )(dx2, gate2, gate, up, w_down)

    def in_body(dgate_ref, dup_ref, wg_ref, wu_ref, dx2_ref, x1_ref, y_ref, g1_ref, g_ref, sc_ref,
                dx1_ref, dy_ref, st_ref):
        @pl.when(pl.program_id(0) == 0)
        def _():
            st_ref[...] = jnp.zeros_like(st_ref)

        dh = _nt(dgate_ref[...], wg_ref[...]) + _nt(dup_ref[...], wu_ref[...])
        xhat, rstd, norm = _rms_fwd(x1_ref[...], g_ref[...])
        dxn, dgain = _rms_bwd(dh * (1.0 + sc_ref[...]), xhat, rstd, g_ref[...])
        dx1 = dx2_ref[...] + dxn
        dx1_ref[...] = dx1
        dy_ref[...] = (g1_ref[...] * dx1).astype(BF16)
        st_ref[0:1, :] += jnp.sum(dh, axis=0, keepdims=True)
        st_ref[1:2, :] += jnp.sum(dh * norm, axis=0, keepdims=True)
        st_ref[2:3, :] += dgain
        st_ref[3:4, :] += jnp.sum(dx1 * y_ref[...].astype(F32), axis=0, keepdims=True)

    half_tile = ROW_TILE // 2

    def rows(width):
        return pl.BlockSpec((half_tile, width), lambda i: (i, 0))

    whole = pl.BlockSpec((D_MODEL, D_FF), lambda i: (0, 0))
    dx1, dy, stats = pl.pallas_call(
        in_body,
        grid=(seq // half_tile,),
        in_specs=[rows(D_FF), rows(D_FF), whole, whole, rows(D_MODEL), rows(D_MODEL), rows(D_MODEL),
                  _row(D_MODEL), _row(D_MODEL), _row(D_MODEL)],
        out_specs=[rows(D_MODEL), rows(D_MODEL), pl.BlockSpec((8, D_MODEL), lambda i: (0, 0))],
        out_shape=[jax.ShapeDtypeStruct((seq, D_MODEL), F32), jax.ShapeDtypeStruct((seq, D_MODEL), BF16),
                   jax.ShapeDtypeStruct((8, D_MODEL), F32)],
        compiler_params=_params(("arbitrary",)),
        name="ffn_bwd_in",
    )(dgate, dup, w_gate, w_up, dx2, x1, y, gate1, gain, scale)
    return dgate, dup, act, dy2, dx1, dy, stats


def _outproj_bwd(dy, w_out):
    seq = dy.shape[0]

    def body(dy_ref, w_ref, out_ref):
        out_ref[...] = _nt(dy_ref[...], w_ref[...])

    rows = pl.BlockSpec((ROW_TILE, D_MODEL), lambda i: (i, 0))
    return pl.pallas_call(
        body,
        grid=(seq // ROW_TILE,),
        in_specs=[rows, pl.BlockSpec((D_MODEL, D_MODEL), lambda i: (0, 0))],
        out_specs=rows,
        out_shape=jax.ShapeDtypeStruct((seq, D_MODEL), F32),
        compiler_params=_params(("arbitrary",)),
        name="outproj_bwd",
    )(dy, w_out)


def _inproj_bwd(dq, dk, dv, dxd, dz, dba, w_a, w_d, w_ba, x, dx1, gain, scale):
    seq = x.shape[0]

    def body(dq_ref, dk_ref, dv_ref, dxd_ref, dz_ref, dba_ref, wa_ref, wd_ref, wb_ref, x_ref, dx1_ref, g_ref, sc_ref,
             gx_ref, st_ref):
        @pl.when(pl.program_id(0) == 0)
        def _():
            st_ref[...] = jnp.zeros_like(st_ref)

        dh = (_nt(dq_ref[...].astype(BF16), wa_ref[:, 0:HEAD_W])
              + _nt(dk_ref[...].astype(BF16), wa_ref[:, HEAD_W:2 * HEAD_W])
              + _nt(dv_ref[...].astype(BF16), wa_ref[:, 2 * HEAD_W:])
              + _nt(dxd_ref[...].astype(BF16), wd_ref[:, 0:3 * HEAD_W])
              + _nt(dz_ref[...].astype(BF16), wd_ref[:, 3 * HEAD_W:])
              + _nt(dba_ref[...].astype(BF16), wb_ref[...]))
        xhat, rstd, norm = _rms_fwd(x_ref[...], g_ref[...])
        dxn, dgain = _rms_bwd(dh * (1.0 + sc_ref[...]), xhat, rstd, g_ref[...])
        gx_ref[...] = dx1_ref[...] + dxn
        st_ref[0:1, :] += jnp.sum(dh, axis=0, keepdims=True)
        st_ref[1:2, :] += jnp.sum(dh * norm, axis=0, keepdims=True)
        st_ref[2:3, :] += dgain

    def rows(width):
        return pl.BlockSpec((ROW_TILE, width), lambda i: (i, 0))

    def whole(a):
        return pl.BlockSpec(a.shape, lambda i: (0, 0))

    return pl.pallas_call(
        body,
        grid=(seq // ROW_TILE,),
        in_specs=[rows(HEAD_W), rows(HEAD_W), rows(HEAD_W), rows(3 * HEAD_W), rows(HEAD_W), rows(LANES),
                  whole(w_a), whole(w_d), whole(w_ba), rows(D_MODEL), rows(D_MODEL), _row(D_MODEL), _row(D_MODEL)],
        out_specs=[rows(D_MODEL), pl.BlockSpec((8, D_MODEL), lambda i: (0, 0))],
        out_shape=[jax.ShapeDtypeStruct((seq, D_MODEL), F32), jax.ShapeDtypeStruct((8, D_MODEL), F32)],
        compiler_params=_params(("arbitrary",)),
        name="inproj_bwd",
    )(dq, dk, dv, dxd, dz, dba, w_a, w_d, w_ba, x, dx1, gain, scale)


def _weight_grad(a, b, name):
    seq, m = a.shape
    n = b.shape[1]
    tm = m if m <= 1536 else m // 2
    tn = n if n <= 1536 else n // 2
    n_k = seq // ROW_TILE

    def body(a_ref, b_ref, out_ref):
        part = _tn(a_ref[...].astype(BF16), b_ref[...].astype(BF16))

        @pl.when(pl.program_id(2) == 0)
        def _():
            out_ref[...] = part

        @pl.when(pl.program_id(2) > 0)
        def _():
            out_ref[...] += part

    return pl.pallas_call(
        body,
        grid=(m // tm, n // tn, n_k),
        in_specs=[pl.BlockSpec((ROW_TILE, tm), lambda i, j, k: (k, i)),
                  pl.BlockSpec((ROW_TILE, tn), lambda i, j, k: (k, j))],
        out_specs=pl.BlockSpec((tm, tn), lambda i, j, k: (i, j)),
        out_shape=jax.ShapeDtypeStruct((m, n), F32),
        compiler_params=_params(("arbitrary", "arbitrary", "arbitrary")),
        name=name,
    )(a, b)


def _adamw(w, g, m, v, name):
    n_rows, n_cols = w.shape
    if n_rows % 256 == 0:
        block, grid, index = (256, n_cols), (n_rows // 256,), lambda i: (i, 0)
    elif n_cols % 256 == 0:
        block, grid, index = (n_rows, 256), (n_cols // 256,), lambda i: (0, i)
    else:
        block, grid, index = (n_rows, n_cols), (1,), lambda i: (0, 0)

    def body(w_ref, g_ref, m_ref, v_ref, d_ref, nm_ref, nv_ref):
        gv = g_ref[...]
        nm = ADAM_B1 * m_ref[...] + (1.0 - ADAM_B1) * gv
        nv = ADAM_B2 * v_ref[...] + (1.0 - ADAM_B2) * (gv * gv)
        m_hat = nm / (1.0 - ADAM_B1 ** ADAM_STEP)
        v_hat = nv / (1.0 - ADAM_B2 ** ADAM_STEP)
        d_ref[...] = -ADAM_LR * (m_hat / (jnp.sqrt(v_hat) + ADAM_EPS) + ADAM_WD * w_ref[...])
        nm_ref[...] = nm
        nv_ref[...] = nv

    blk = pl.BlockSpec(block, index)
    shape = jax.ShapeDtypeStruct((n_rows, n_cols), F32)
    return pl.pallas_call(
        body,
        grid=grid,
        in_specs=[blk] * 4,
        out_specs=[blk] * 3,
        out_shape=[shape] * 3,
        compiler_params=_params(("arbitrary",)),
        name=name,
    )(w, g, m, v)


IN_WIDTH = 3600
BA_COL = 7 * HEAD_W


def _local_step(x, target, mod, norm_attn_g, w_in, rel_bias, conv_w, a_log, dt_bias, delta_norm_g, w_out,
                norm_ffn_g, w_gate, w_up, w_down, final_norm_g):
    sh1, sc1, g1, sh2, sc2, g2 = [mod[:, i * D_MODEL:(i + 1) * D_MODEL] for i in range(6)]
    w_a = w_in[:, :3 * HEAD_W]
    w_d = w_in[:, 3 * HEAD_W:BA_COL]
    w_ba = jnp.pad(w_in[:, BA_COL:], ((0, 0), (0, LANES - 2 * N_HEADS)))
    tables = jnp.asarray(_attn_tables())
    alog_row = jnp.pad(a_log, ((0, 0), (N_HEADS, LANES - 2 * N_HEADS)))
    dt_row = jnp.pad(dt_bias, ((0, 0), (N_HEADS, LANES - 2 * N_HEADS)))
    gain_row = jnp.tile(delta_norm_g, (1, N_HEADS))

    h1, qkv_a, qkvz, ba = _inproj_fwd(x, norm_attn_g, sc1, sh1, w_a, w_d, w_ba)
    y_attn, lse = _attention_fwd(qkv_a, rel_bias, tables)
    xs = _delta_prep_fwd(qkvz, ba, conv_w, alog_row, dt_row)
    inv_h, qk_h, u_h, w_h = _delta_chunk_fwd(xs)
    o, st_h = _delta_scan_fwd(xs, qk_h, u_h, w_h)
    y_delta = _delta_post_fwd(o, qkvz, gain_row)
    x1, h2, y = _outproj_fwd(y_attn, y_delta, w_out, x, g1, norm_ffn_g, sc2, sh2)
    gate, up, dx2, st_f = _ffn_fwd(h2, w_gate, w_up, w_down, x1, g2, final_norm_g, target)

    dgate, dup, act, dy2, dx1, dy, st_b = _ffn_bwd(dx2, gate, up, w_gate, w_up, w_down, x1, y, g2, g1, norm_ffn_g, sc2)
    grads = {
        "w_gate": _weight_grad(h2, dgate, "wgrad_gate"),
        "w_up": _weight_grad(h2, dup, "wgrad_up"),
        "w_down": _weight_grad(act, dy2, "wgrad_down"),
        "w_out": jnp.concatenate([_weight_grad(y_attn, dy, "wgrad_out_attn"),
                                  _weight_grad(y_delta, dy, "wgrad_out_delta")], axis=0),
    }
    dycat = _outproj_bwd(dy, w_out)
    do, dz, dgain = _delta_post_bwd(dycat, o, qkvz, gain_row)
    dsn_h, dvn_h = _delta_scan_bwd(xs, qk_h, w_h, do)
    dxs = _delta_chunk_bwd(xs, inv_h, u_h, w_h, st_h, dsn_h, dvn_h, do)
    dconv, dba, dvec = _delta_prep_bwd(qkvz, ba, conv_w, alog_row, dt_row, dxs)
    dxd, grads["conv_w"] = _conv_bwd(dconv, qkvz, conv_w)
    dq, dk, dv, dbias = _attention_bwd(qkv_a, dycat, y_attn, lse, rel_bias, tables)
    grad_x, st_i = _inproj_bwd(dq, dk, dv, dxd, dz, dba, w_a, w_d, w_ba, x, dx1, norm_attn_g, sc1)
    grads["w_in"] = jnp.concatenate(
        [_weight_grad(h1, dq, "wgrad_in_q"), _weight_grad(h1, dk, "wgrad_in_k"), _weight_grad(h1, dv, "wgrad_in_v"),
         _weight_grad(h1, dxd, "wgrad_in_delta"), _weight_grad(h1, dz, "wgrad_in_z"),
         _weight_grad(h1, dba, "wgrad_in_gates")[:, :2 * N_HEADS]], axis=1)
    grads["rel_bias"] = _rel_bias_grad(dbias, tables)[:, :N_BUCKETS].T
    grads["a_log"] = dvec[0:1, N_HEADS:2 * N_HEADS]
    grads["dt_bias"] = dvec[1:2, N_HEADS:2 * N_HEADS]
    grads["delta_norm_g"] = dgain[1:2, :HEAD_DIM]
    grads["norm_attn_g"] = st_i[2:3]
    grads["norm_ffn_g"] = st_b[2:3]
    grads["final_norm_g"] = st_f[0:1]
    dmod = jnp.concatenate([st_i[0:1], st_i[1:2], st_b[3:4], st_b[0:1], st_b[1:2], st_f[1:2]], axis=1)
    return st_f[3, 0], grad_x, grads, dmod


MESH = pl.DeviceIdType.MESH
OTHER_CHIPS = ((1, 0), (0, 1), (1, 1))
ALL_PEERS = tuple((m >> 2 & 1, m >> 1 & 1, m & 1) for m in range(1, 8))
ANY = pl.BlockSpec(memory_space=pl.ANY)
VMEM_SPEC = pl.BlockSpec(memory_space=pltpu.VMEM)
N_BIG = 5


def _me():
    return lax.axis_index("x"), lax.axis_index("y"), lax.axis_index("c")


def _flip(pos, mask):
    return tuple(1 - p if m else p for p, m in zip(pos, mask))


def _remote(src, dst, send_sems, recv_sems, k, to):
    return pltpu.make_async_remote_copy(src_ref=src, dst_ref=dst, send_sem=send_sems.at[k], recv_sem=recv_sems.at[k],
                                        device_id=to, device_id_type=MESH)


def _ada_exchange(c8, w_ada, b_ada, conv8):
    def body(c_ref, w_ref, b_ref, cv_ref, mod_ref, cact_ref, conv_ref, c_all, part_all, send_sems, recv_sems):
        x, y, c = me = _me()
        dev = 4 * x + 2 * y + c
        chip = 2 * x + y
        c_all[dev] = c_ref[...]
        conv_ref[chip] = cv_ref[...]
        first = [_remote(c_ref, c_all.at[dev], send_sems, recv_sems, k, _flip(me, mask))
                 for k, mask in enumerate(ALL_PEERS)]
        first += [_remote(cv_ref, conv_ref.at[chip], send_sems, recv_sems, 7 + j, _flip(me, (*mask, 0)))
                  for j, mask in enumerate(OTHER_CHIPS)]
        for cp in first:
            cp.start()
        for cp in first:
            cp.wait()
        row = lax.broadcasted_iota(jnp.int32, (8, D_MODEL), 0)
        c_rows = jnp.zeros((8, D_MODEL), F32)
        for d in range(8):
            c_rows = jnp.where(row == d, c_all[d], c_rows)
        c_act = _silu(c_rows)
        cact_ref[...] = c_act
        part_all[chip] = _nn(c_act, w_ref[...], HIGHEST)
        second = [_remote(part_all.at[chip], part_all.at[chip], send_sems, recv_sems, 10 + j, _flip(me, (*mask, 0)))
                  for j, mask in enumerate(OTHER_CHIPS)]
        for cp in second:
            cp.start()
        for cp in second:
            cp.wait()
        cols = w_ref.shape[1]
        for k in range(4):
            mod_ref[:, k * cols:(k + 1) * cols] = part_all[k] + b_ref[:, k * cols:(k + 1) * cols]

    cols = w_ada.shape[1]
    return pl.pallas_call(
        body,
        in_specs=[VMEM_SPEC] * 4,
        out_specs=[VMEM_SPEC] * 3,
        out_shape=[jax.ShapeDtypeStruct((8, 4 * cols), F32), jax.ShapeDtypeStruct((8, D_MODEL), F32),
                   jax.ShapeDtypeStruct((4, 8, conv8.shape[1]), F32)],
        scratch_shapes=[pltpu.VMEM((8, 8, D_MODEL), F32), pltpu.VMEM((4, 8, cols), F32),
                        pltpu.SemaphoreType.DMA((13,)), pltpu.SemaphoreType.DMA((13,))],
        compiler_params=pltpu.CompilerParams(vmem_limit_bytes=VMEM_LIMIT),
        name="ada_exchange",
    )(c8, w_ada, b_ada, conv8)


def _gather_weights(shards):
    def body(*refs):
        srcs, dsts = refs[:N_BIG], refs[N_BIG:2 * N_BIG]
        send_sems, recv_sems = refs[2 * N_BIG:]
        x, y, c = me = _me()
        chip = 2 * x + y
        sibling = _flip(me, (0, 0, 1))
        first, passed = [], []
        for a in range(N_BIG):
            for j, mask in enumerate(OTHER_CHIPS):
                to = _flip(me, (*mask, 0))
                first.append(_remote(srcs[a].at[c], dsts[a].at[chip, c], send_sems, recv_sems, 6 * a + j, to))
                landed = dsts[a].at[2 * to[0] + to[1], c]
                passed.append(_remote(landed, landed, send_sems, recv_sems, 6 * a + 3 + j, sibling))
        for cp in first:
            cp.start()
        for cp, fwd in zip(first, passed):
            cp.wait_recv()
            fwd.start()
        for cp in first:
            cp.wait_send()
        for fwd in passed:
            fwd.wait()

    return pl.pallas_call(
        body,
        in_specs=[ANY] * N_BIG,
        out_specs=[ANY] * N_BIG,
        out_shape=[jax.ShapeDtypeStruct((4, *s.shape), s.dtype) for s in shards],
        scratch_shapes=[pltpu.SemaphoreType.DMA((6 * N_BIG,)), pltpu.SemaphoreType.DMA((6 * N_BIG,))],
        name="gather_weights",
    )(*shards)


def _start_and_wait(copies):
    for cp in copies:
        cp.start()
    for cp in copies:
        cp.wait()


def _swap_halves(grads):
    def body(*refs):
        srcs, got = refs[:N_BIG], refs[N_BIG:2 * N_BIG]
        send_sems, recv_sems = refs[2 * N_BIG:]
        x, y, c = me = _me()
        _start_and_wait([_remote(srcs[a].at[:, 1 - c], got[a], send_sems, recv_sems, a, _flip(me, (0, 0, 1)))
                         for a in range(N_BIG)])

    return pl.pallas_call(
        body,
        in_specs=[ANY] * N_BIG,
        out_specs=[ANY] * N_BIG,
        out_shape=[jax.ShapeDtypeStruct((4, g.shape[2], g.shape[3]), g.dtype) for g in grads],
        scratch_shapes=[pltpu.SemaphoreType.DMA((N_BIG,)), pltpu.SemaphoreType.DMA((N_BIG,))],
        name="swap_halves",
    )(*grads)


def _scatter_partials(partials):
    def body(*refs):
        srcs, dsts = refs[:N_BIG], refs[N_BIG:2 * N_BIG]
        send_sems, recv_sems = refs[2 * N_BIG:]
        x, y, c = me = _me()
        chip = 2 * x + y
        copies = []
        for a in range(N_BIG):
            for j, mask in enumerate(OTHER_CHIPS):
                to = _flip(me, (*mask, 0))
                copies.append(_remote(srcs[a].at[2 * to[0] + to[1]], dsts[a].at[chip], send_sems, recv_sems, 3 * a + j, to))
        _start_and_wait(copies)

    return pl.pallas_call(
        body,
        in_specs=[ANY] * N_BIG,
        out_specs=[ANY] * N_BIG,
        out_shape=[jax.ShapeDtypeStruct(p.shape, p.dtype) for p in partials],
        scratch_shapes=[pltpu.SemaphoreType.DMA((3 * N_BIG,)), pltpu.SemaphoreType.DMA((3 * N_BIG,))],
        name="scatter_partials",
    )(*partials)


def _join_halves(halves):
    def body(*refs):
        srcs, dsts = refs[:N_BIG], refs[N_BIG:2 * N_BIG]
        send_sems, recv_sems = refs[2 * N_BIG:]
        x, y, c = me = _me()
        _start_and_wait([_remote(srcs[a], dsts[a].at[c], send_sems, recv_sems, a, _flip(me, (0, 0, 1)))
                         for a in range(N_BIG)])

    return pl.pallas_call(
        body,
        in_specs=[ANY] * N_BIG,
        out_specs=[ANY] * N_BIG,
        out_shape=[jax.ShapeDtypeStruct((2, *h.shape), h.dtype) for h in halves],
        scratch_shapes=[pltpu.SemaphoreType.DMA((N_BIG,)), pltpu.SemaphoreType.DMA((N_BIG,))],
        name="join_halves",
    )(*halves)


def _gather_small(packed):
    n_rows = packed.shape[0]

    def body(p_ref, all_ref, sum_ref, send_sems, recv_sems):
        x, y, c = me = _me()
        dev = 4 * x + 2 * y + c
        all_ref[dev] = p_ref[...]
        copies = [_remote(p_ref, all_ref.at[dev], send_sems, recv_sems, k, _flip(me, mask))
                  for k, mask in enumerate(ALL_PEERS)]
        for cp in copies:
            cp.start()
        for cp in copies:
            cp.wait()
        total = all_ref[0]
        for d in range(1, 8):
            total = total + all_ref[d]
        sum_ref[...] = total

    return pl.pallas_call(
        body,
        in_specs=[VMEM_SPEC],
        out_specs=[VMEM_SPEC, VMEM_SPEC],
        out_shape=[jax.ShapeDtypeStruct((8, n_rows, LANES), F32), jax.ShapeDtypeStruct((n_rows, LANES), F32)],
        scratch_shapes=[pltpu.SemaphoreType.DMA((7,)), pltpu.SemaphoreType.DMA((7,))],
        name="gather_small",
    )(packed)


def _add_pair(a, b, out_dtype, name):
    def body(a_ref, b_ref, o_ref):
        o_ref[...] = (a_ref[...] + b_ref[...]).astype(o_ref.dtype)

    blk = pl.BlockSpec((1, *a.shape[1:]), lambda i: (i, 0, 0))
    return pl.pallas_call(
        body, grid=(a.shape[0],), in_specs=[blk, blk], out_specs=blk,
        out_shape=jax.ShapeDtypeStruct(a.shape, out_dtype),
        compiler_params=_params(("arbitrary",)), name=name,
    )(a, b)


def _add_slots(a, name):
    def body(a_ref, o_ref):
        total = a_ref[0].astype(F32)
        for k in range(1, 4):
            total = total + a_ref[k].astype(F32)
        o_ref[...] = total

    return pl.pallas_call(
        body, in_specs=[VMEM_SPEC], out_specs=VMEM_SPEC,
        out_shape=jax.ShapeDtypeStruct(a.shape[1:], F32),
        compiler_params=pltpu.CompilerParams(vmem_limit_bytes=VMEM_LIMIT), name=name,
    )(a)


def _ada_weight_grad(c_act, dmod_cols):
    def body(c_ref, d_ref, o_ref):
        o_ref[...] = _tn(c_ref[...], d_ref[...], HIGHEST)

    return pl.pallas_call(
        body, in_specs=[VMEM_SPEC, VMEM_SPEC], out_specs=VMEM_SPEC,
        out_shape=jax.ShapeDtypeStruct((c_act.shape[1], dmod_cols.shape[1]), F32),
        compiler_params=pltpu.CompilerParams(vmem_limit_bytes=VMEM_LIMIT), name="ada_weight_grad",
    )(c_act, dmod_cols)


def kernel(x, c, w_ada, b_ada, norm_attn_g, w_in, rel_bias, conv_w, a_log, dt_bias, delta_norm_g, w_out, norm_ffn_g, w_gate, w_up, w_down, final_norm_g, loss_target, m_w_ada, m_b_ada, m_norm_attn_g, m_w_in, m_rel_bias, m_conv_w, m_a_log, m_dt_bias, m_delta_norm_g, m_w_out, m_norm_ffn_g, m_w_gate, m_w_up, m_w_down, m_final_norm_g, v_w_ada, v_b_ada, v_norm_attn_g, v_w_in, v_rel_bias, v_conv_w, v_a_log, v_dt_bias, v_delta_norm_g, v_w_out, v_norm_ffn_g, v_w_gate, v_w_up, v_w_down, v_final_norm_g):
    xi, yi, ci = _me()
    dev = 4 * xi + 2 * yi + ci
    chip = 2 * xi + yi

    conv_cols = conv_w.shape[2]
    mod_all, c_act, conv_all = _ada_exchange(jnp.broadcast_to(c, (8, D_MODEL)), w_ada[0], b_ada,
                                             jnp.pad(conv_w[0], ((0, 4), (0, 0))))
    mod = lax.dynamic_slice_in_dim(mod_all, dev, 1, axis=0)
    conv_full = jnp.swapaxes(conv_all[:, :4, :], 0, 1).reshape(4, 4 * conv_cols)

    big_names = ("w_in", "w_out", "w_gate", "w_up", "w_down")
    by_cols = (True, False, True, True, False)

    def rows_form(a, cols):
        return jnp.swapaxes(a[0], 0, 1) if cols else a[0]

    big = [rows_form(w, cols) for w, cols in zip((w_in, w_out, w_gate, w_up, w_down), by_cols)]
    shards = [w.astype(BF16).reshape(2, w.size // (2 * LANES), LANES) for w in big]
    gathered = [lax.dynamic_update_index_in_dim(g, s, chip, 0) for g, s in zip(_gather_weights(shards), shards)]
    gathered = [g.reshape(4 * w.shape[0], w.shape[1]) for g, w in zip(gathered, big)]
    whole = [g.T if cols else g for g, cols in zip(gathered, by_cols)]

    loss, grad_x, grads, dmod = _local_step(
        x[0], loss_target[0], mod, norm_attn_g, whole[0], rel_bias, conv_full, a_log, dt_bias, delta_norm_g,
        whole[1], norm_ffn_g, whole[2], whole[3], whole[4], final_norm_g[None])

    slots = []
    for name, w, cols in zip(big_names, big, by_cols):
        g = grads[name].T if cols else grads[name]
        slots.append(g.reshape(4, 2, w.size // (2 * LANES), LANES))
    partials = [_add_pair(lax.dynamic_index_in_dim(s, ci, 1, keepdims=False), got, BF16, f"add_pair_{a}")
                for a, (s, got) in enumerate(zip(slots, _swap_halves(slots)))]
    by_source = [lax.dynamic_update_index_in_dim(b, lax.dynamic_index_in_dim(p, chip, 0, keepdims=False), chip, 0)
                 for b, p in zip(_scatter_partials(partials), partials)]
    halves = [_add_slots(p, f"add_slots_{a}") for a, p in enumerate(by_source)]
    joined = [lax.dynamic_update_index_in_dim(j, h, ci, 0) for j, h in zip(_join_halves(halves), halves)]
    big_grads = [j.reshape(w.shape) for j, w in zip(joined, big)]

    pieces = [dmod, grads["conv_w"], grads["norm_attn_g"], grads["norm_ffn_g"], grads["final_norm_g"],
              grads["rel_bias"], grads["a_log"], grads["dt_bias"], grads["delta_norm_g"]]
    flat = [jnp.pad(p.reshape(-1), (0, -p.size % LANES)) for p in pieces]
    n_rows = [f.size // LANES for f in flat]
    packed = jnp.concatenate(flat).reshape(-1, LANES)
    packed = jnp.pad(packed, ((0, -packed.shape[0] % 8), (0, 0)))
    all_small, total = _gather_small(packed)
    sums, start = [], 0
    for p, n in zip(pieces, n_rows):
        sums.append(total[start:start + n].reshape(-1)[:p.size].reshape(p.shape))
        start += n
    g_b_ada, g_conv, g_norm_attn, g_norm_ffn, g_final, g_rel, g_alog, g_dt, g_dnorm = sums
    dmod_all = all_small[:, :n_rows[0], :].reshape(8, -1)
    ada_cols = w_ada.shape[2]
    g_w_ada = _ada_weight_grad(c_act, lax.dynamic_slice_in_dim(dmod_all, chip * ada_cols, ada_cols, axis=1))
    g_conv = lax.dynamic_slice_in_dim(g_conv, chip * conv_cols, conv_cols, axis=1)

    grad = {"w_ada": g_w_ada[None], "b_ada": g_b_ada, "norm_attn_g": g_norm_attn,
            "rel_bias": g_rel, "conv_w": g_conv[None], "a_log": g_alog, "dt_bias": g_dt, "delta_norm_g": g_dnorm,
            "norm_ffn_g": g_norm_ffn, "final_norm_g": g_final.reshape(-1)}
    weight = {"w_ada": w_ada, "b_ada": b_ada, "norm_attn_g": norm_attn_g, "w_in": w_in, "rel_bias": rel_bias,
              "conv_w": conv_w, "a_log": a_log, "dt_bias": dt_bias, "delta_norm_g": delta_norm_g, "w_out": w_out,
              "norm_ffn_g": norm_ffn_g, "w_gate": w_gate, "w_up": w_up, "w_down": w_down, "final_norm_g": final_norm_g}
    first = {"w_ada": m_w_ada, "b_ada": m_b_ada, "norm_attn_g": m_norm_attn_g, "w_in": m_w_in, "rel_bias": m_rel_bias,
             "conv_w": m_conv_w, "a_log": m_a_log, "dt_bias": m_dt_bias, "delta_norm_g": m_delta_norm_g,
             "w_out": m_w_out, "norm_ffn_g": m_norm_ffn_g, "w_gate": m_w_gate, "w_up": m_w_up, "w_down": m_w_down,
             "final_norm_g": m_final_norm_g}
    second = {"w_ada": v_w_ada, "b_ada": v_b_ada, "norm_attn_g": v_norm_attn_g, "w_in": v_w_in, "rel_bias": v_rel_bias,
              "conv_w": v_conv_w, "a_log": v_a_log, "dt_bias": v_dt_bias, "delta_norm_g": v_delta_norm_g,
              "w_out": v_w_out, "norm_ffn_g": v_norm_ffn_g, "w_gate": v_w_gate, "w_up": v_w_up, "w_down": v_w_down,
              "final_norm_g": v_final_norm_g}
    delta, new_m, new_v = {}, {}, {}
    for name, w in weight.items():
        if name in big_names:
            continue
        two_d = (-1, w.shape[-1])
        d, nm, nv = _adamw(w.reshape(two_d), grad[name].reshape(two_d), first[name].reshape(two_d),
                           second[name].reshape(two_d), f"adamw_{name}")
        delta[name], new_m[name], new_v[name] = d.reshape(w.shape), nm.reshape(w.shape), nv.reshape(w.shape)
    for name, w, g, cols in zip(big_names, big, big_grads, by_cols):
        outs = _adamw(w, g, rows_form(first[name], cols), rows_form(second[name], cols), f"adamw_{name}")
        grad[name], delta[name], new_m[name], new_v[name] = [
            (jnp.swapaxes(o, 0, 1) if cols else o)[None] for o in (g, *outs)]

    names = list(weight)
    return (lax.psum(loss, ("x", "y", "c")), grad_x[None], *[grad[n] for n in names], *[delta[n] for n in names],
            *[new_m[n] for n in names], *[new_v[n] for n in names])
```

```python
import functools
import math

import numpy as np
import jax
import jax.numpy as jnp
from jax import lax
from jax.experimental import pallas as pl
from jax.experimental.pallas import tpu as pltpu

F32 = jnp.float32
BF16 = jnp.bfloat16
HIGHEST = lax.Precision.HIGHEST

D_MODEL = 1024
HEAD_DIM = 64
N_HEADS = 8
HEAD_W = 512
BRANCHES = ((128, 1), (512, 4), (2048, 16))
BAND = 128
ATT_TILE = 2048
ATT_UNROLL = 4
N_BUCKETS = 32
MAX_DISTANCE = 2048
CHUNK = 64
D_FF = 2816
EPS = 1e-6
NEG_INF = -1e30
LANES = 128
VMEM_LIMIT = 56 * 1024 * 1024

ADAM_LR = 0.001
ADAM_B1 = 0.9
ADAM_B2 = 0.999
ADAM_EPS = 1e-08
ADAM_WD = 0.01
ADAM_STEP = 10


def _nn(a, b, precision=None):
    return jnp.dot(a, b, preferred_element_type=F32, precision=precision)


def _nt(a, b, precision=None):
    return lax.dot_general(a, b, (((1,), (1,)), ((), ())), preferred_element_type=F32, precision=precision)


def _tn(a, b, precision=None):
    return lax.dot_general(a, b, (((0,), (0,)), ((), ())), preferred_element_type=F32, precision=precision)


def _params(sem, vmem=VMEM_LIMIT):
    return pltpu.CompilerParams(dimension_semantics=sem, vmem_limit_bytes=vmem)


def _sigmoid(x):
    return 1.0 / (1.0 + jnp.exp(-x))


def _silu(x):
    return x * _sigmoid(x)


def _dsilu(x):
    s = _sigmoid(x)
    return s * (1.0 + x * (1.0 - s))


def _attn_tables():
    qi = np.arange(BAND)[:, None]
    kj = np.arange(2 * BAND)[None, :]
    steps = qi + BAND - kj
    in_window = (steps >= 0) & (steps <= BAND)
    max_exact = N_BUCKETS // 2
    out = np.zeros((3, 2, BAND, 2 * BAND), np.int32)
    for b, (_, dil) in enumerate(BRANCHES):
        dist = np.maximum(steps, 0) * dil
        dist_f = np.maximum(dist, 1).astype(np.float32)
        large = max_exact + (np.log(dist_f / np.float32(max_exact)) / np.float32(math.log(MAX_DISTANCE / max_exact))
                             * np.float32(N_BUCKETS - max_exact)).astype(np.int32)
        bucket = np.where(dist < max_exact, dist, np.minimum(large, N_BUCKETS - 1)).astype(np.int32)
        out[b, 0] = np.where(in_window, bucket, -1)
        out[b, 1] = np.where(in_window & (kj >= BAND), bucket, -1)
    return out


def _attn_bias_tables(rel_ref, tab_ref, bias_s, pair):
    for b in range(3):
        for first in range(2):
            tab = tab_ref[b, first]
            for hh in range(2):
                head = 2 * pair + hh

                def pick(kk, acc, tab=tab, head=head):
                    return jnp.where(tab == kk, rel_ref[kk, head], acc)

                acc = lax.fori_loop(0, N_BUCKETS, pick, jnp.zeros((BAND, 2 * BAND), F32))
                bias_s[b, hh, first] = jnp.where(tab < 0, NEG_INF, acc)


def _attn_block_index(idx, t, r):
    nb = ATT_TILE // (BAND * r)
    rho = idx // nb
    n = idx % nb
    qs = rho + r * BAND * n
    gs = t * ATT_TILE + qs
    first = (t * nb + n) == 0
    ps = jnp.where(first, gs, gs - r * BAND)
    return qs, gs, ps, first.astype(jnp.int32)


def _rows(start, r):
    return pl.ds(start, BAND) if r == 1 else pl.ds(start, BAND, stride=r)


def _attention_fwd(qkv, rel_bias, tables):
    seq = qkv.shape[0]
    n_tiles = seq // ATT_TILE

    def body(rel_ref, tab_ref, q_ref, k_ref, v_ref, y_ref, lse_ref, bias_s, o_s, l_s):
        pair = pl.program_id(0)
        t = pl.program_id(1)
        lane = lax.broadcasted_iota(jnp.int32, (1, LANES), 1)
        head0 = lane < HEAD_DIM

        @pl.when(t == 0)
        def _():
            _attn_bias_tables(rel_ref, tab_ref, bias_s, pair)

        masks = (head0, jnp.logical_not(head0))
        ones = jnp.ones((2 * BAND, LANES), BF16)
        for b, (_, r) in enumerate(BRANCHES):
            def blocks(it, carry, b=b, r=r):
                idx = [_attn_block_index(it * ATT_UNROLL + j, t, r) for j in range(ATT_UNROLL)]
                qb = [q_ref[_rows(qs, r), :] * (HEAD_DIM ** -0.5) for qs, _, _, _ in idx]
                kcat = [jnp.concatenate([k_ref[_rows(ps, r), :], k_ref[_rows(gs, r), :]], axis=0).astype(BF16)
                        for _, gs, ps, _ in idx]
                vcat = [jnp.concatenate([v_ref[_rows(ps, r), :], v_ref[_rows(gs, r), :]], axis=0).astype(BF16)
                        for _, gs, ps, _ in idx]
                work = [(j, hh) for j in range(ATT_UNROLL) for hh in range(2)]
                s = [_nt(jnp.where(masks[hh], qb[j], 0.0).astype(BF16), kcat[j]) + bias_s[b, hh, idx[j][3]]
                     for j, hh in work]
                m = [jnp.max(sv, axis=-1, keepdims=True) for sv in s]
                e = [jnp.exp(sv - mv) for sv, mv in zip(s, m)]
                eb = [ev.astype(BF16) for ev in e]
                den = [_nn(ev, ones) for ev in eb]
                out = [_nn(ev, vcat[j]) / dv for ev, dv, (j, _) in zip(eb, den, work)]
                lse = [mv + jnp.log(dv) for mv, dv in zip(m, den)]
                for j in range(ATT_UNROLL):
                    o_s[b, _rows(idx[j][0], r), :] = jnp.where(head0, out[2 * j], out[2 * j + 1])
                    l_s[b, _rows(idx[j][0], r), :] = jnp.where(head0, lse[2 * j], lse[2 * j + 1])
                return carry

            lax.fori_loop(0, ATT_TILE // BAND // ATT_UNROLL, blocks, 0)

        def merge(i, carry):
            rows = pl.ds(pl.multiple_of(i * BAND, BAND), BAND)
            l0, l1, l2 = l_s[0, rows, :], l_s[1, rows, :], l_s[2, rows, :]
            m = jnp.maximum(jnp.maximum(l0, l1), l2)
            w0, w1, w2 = jnp.exp(l0 - m), jnp.exp(l1 - m), jnp.exp(l2 - m)
            tot = w0 + w1 + w2
            y_ref[rows, :] = (w0 * o_s[0, rows, :] + w1 * o_s[1, rows, :] + w2 * o_s[2, rows, :]) / tot
            lse_ref[rows, :] = m + jnp.log(tot)
            return carry

        lax.fori_loop(0, ATT_TILE // BAND, merge, 0)

    tile = pl.BlockSpec((ATT_TILE, LANES), lambda p, t: (t, p))
    return pl.pallas_call(
        body,
        grid=(N_HEADS // 2, n_tiles),
        in_specs=[
            pl.BlockSpec(memory_space=pltpu.SMEM),
            pl.BlockSpec((3, 2, BAND, 2 * BAND), lambda p, t: (0, 0, 0, 0)),
            pl.BlockSpec((ATT_TILE, LANES), lambda p, t: (t, p)),
            pl.BlockSpec((seq, LANES), lambda p, t: (0, 4 + p)),
            pl.BlockSpec((seq, LANES), lambda p, t: (0, 8 + p)),
        ],
        out_specs=[tile, tile],
        out_shape=[jax.ShapeDtypeStruct((seq, HEAD_W), F32), jax.ShapeDtypeStruct((seq, HEAD_W), F32)],
        scratch_shapes=[
            pltpu.VMEM((3, 2, 2, BAND, 2 * BAND), F32),
            pltpu.VMEM((3, ATT_TILE, LANES), F32),
            pltpu.VMEM((3, ATT_TILE, LANES), F32),
        ],
        compiler_params=_params(("arbitrary", "arbitrary")),
        name="attn_fwd",
    )(rel_bias, tables, qkv, qkv, qkv)


def _attention_bwd(qkv, dy, y, lse, rel_bias, tables):
    seq = qkv.shape[0]
    n_tiles = seq // ATT_TILE

    def body(rel_ref, tab_ref, q_ref, k_ref, v_ref, dy_ref, y_ref, lse_ref,
             dq_ref, dk_ref, dv_ref, dbias_ref, bias_s):
        pair = pl.program_id(0)
        t = pl.program_id(1)
        lane = lax.broadcasted_iota(jnp.int32, (1, LANES), 1)
        head0 = lane < HEAD_DIM

        @pl.when(t == 0)
        def _():
            _attn_bias_tables(rel_ref, tab_ref, bias_s, pair)
            dk_ref[...] = jnp.zeros_like(dk_ref)
            dv_ref[...] = jnp.zeros_like(dv_ref)
            dbias_ref[...] = jnp.zeros_like(dbias_ref)

        dq_ref[...] = jnp.zeros_like(dq_ref)

        masks = (head0, jnp.logical_not(head0))
        ones = jnp.ones((LANES, LANES), BF16)
        scale = HEAD_DIM ** -0.5
        for b, (_, r) in enumerate(BRANCHES):
            def blocks(it, carry, b=b, r=r):
                idx = [_attn_block_index(it * ATT_UNROLL + j, t, r) for j in range(ATT_UNROLL)]
                qb = [q_ref[_rows(qs, r), :] * scale for qs, _, _, _ in idx]
                kcat = [jnp.concatenate([k_ref[_rows(ps, r), :], k_ref[_rows(gs, r), :]], axis=0).astype(BF16)
                        for _, gs, ps, _ in idx]
                vcat = [jnp.concatenate([v_ref[_rows(ps, r), :], v_ref[_rows(gs, r), :]], axis=0).astype(BF16)
                        for _, gs, ps, _ in idx]
                dob = [dy_ref[_rows(qs, r), :] for qs, _, _, _ in idx]
                ob = [y_ref[_rows(qs, r), :] for qs, _, _, _ in idx]
                lb = [lse_ref[_rows(qs, r), :] for qs, _, _, _ in idx]
                work = [(j, hh) for j in range(ATT_UNROLL) for hh in range(2)]
                qh = [jnp.where(masks[hh], qb[j], 0.0).astype(BF16) for j, hh in work]
                doh = [jnp.where(masks[hh], dob[j], 0.0) for j, hh in work]
                dohb = [d.astype(BF16) for d in doh]
                s = [_nt(qh[w], kcat[j]) + bias_s[b, hh, idx[j][3]] for w, (j, hh) in enumerate(work)]
                dp = [_nt(dohb[w], vcat[j]) for w, (j, _) in enumerate(work)]
                lrot = [pltpu.roll(lv, HEAD_DIM, 1) for lv in lb]
                lcol = [jnp.where(masks[hh], lb[j], lrot[j]) for j, hh in work]
                parts = [_split(doh[w] * ob[j]) for w, (j, _) in enumerate(work)]
                delta = [_nn(hi, ones) + _nn(lo, ones) for hi, lo in parts]
                prob = [jnp.exp(sv - jnp.concatenate([lv, lv], axis=1)) for sv, lv in zip(s, lcol)]
                ds = [pv * (dv - jnp.concatenate([de, de], axis=1)) for pv, dv, de in zip(prob, dp, delta)]
                dsb = [d.astype(BF16) for d in ds]
                dq = [_nn(dsb[w], kcat[j]) for w, (j, _) in enumerate(work)]
                dkc = [_tn(dsb[w], qh[w]) for w in range(len(work))]
                dvc = [_tn(prob[w].astype(BF16), dohb[w]) for w in range(len(work))]
                for w, (j, hh) in enumerate(work):
                    dbias_ref[0, b, hh] += ds[w]
                for j in range(ATT_UNROLL):
                    qs, gs, ps, _ = idx[j]
                    dkcat = dkc[2 * j] + dkc[2 * j + 1]
                    dvcat = dvc[2 * j] + dvc[2 * j + 1]
                    dq_ref[_rows(qs, r), :] += jnp.where(head0, dq[2 * j], dq[2 * j + 1]) * scale
                    dk_ref[_rows(ps, r), :] += dkcat[:BAND]
                    dk_ref[_rows(gs, r), :] += dkcat[BAND:]
                    dv_ref[_rows(ps, r), :] += dvcat[:BAND]
                    dv_ref[_rows(gs, r), :] += dvcat[BAND:]
                return carry

            lax.fori_loop(0, ATT_TILE // BAND // ATT_UNROLL, blocks, 0)

    tile = pl.BlockSpec((ATT_TILE, LANES), lambda p, t: (t, p))
    full = pl.BlockSpec((seq, LANES), lambda p, t: (0, p))
    return pl.pallas_call(
        body,
        grid=(N_HEADS // 2, n_tiles),
        in_specs=[
            pl.BlockSpec(memory_space=pltpu.SMEM),
            pl.BlockSpec((3, 2, BAND, 2 * BAND), lambda p, t: (0, 0, 0, 0)),
            pl.BlockSpec((ATT_TILE, LANES), lambda p, t: (t, p)),
            pl.BlockSpec((seq, LANES), lambda p, t: (0, 4 + p)),
            pl.BlockSpec((seq, LANES), lambda p, t: (0, 8 + p)),
            tile, tile, tile,
        ],
        out_specs=[tile, full, full,
                   pl.BlockSpec((1, 3, 2, BAND, 2 * BAND), lambda p, t: (p, 0, 0, 0, 0))],
        out_shape=[jax.ShapeDtypeStruct((seq, HEAD_W), F32)] * 3
        + [jax.ShapeDtypeStruct((N_HEADS // 2, 3, 2, BAND, 2 * BAND), F32)],
        scratch_shapes=[pltpu.VMEM((3, 2, 2, BAND, 2 * BAND), F32)],
        compiler_params=_params(("arbitrary", "arbitrary")),
        name="attn_bwd",
    )(rel_bias, tables, qkv, qkv, qkv, dy, y, lse)


def _rel_bias_grad(dbias, tables):
    def body(tab_ref, db_ref, out_ref):
        lane = lax.broadcasted_iota(jnp.int32, (1, LANES), 1)
        out_ref[...] = jnp.zeros_like(out_ref)
        for b in range(3):
            tab = tab_ref[b, 0]

            def head(h, carry, b=b, tab=tab):
                d = db_ref[h // 2, b, h % 2]
                sums = [jnp.sum(jnp.where(tab == kk, d, 0.0), keepdims=True) for kk in range(N_BUCKETS)]
                row = jnp.zeros((1, LANES), F32)
                for kk, s in enumerate(sums):
                    row = row + jnp.where(lane == kk, s, 0.0)
                out_ref[pl.ds(h, 1), :] += row
                return carry

            lax.fori_loop(0, N_HEADS, head, 0)

    return pl.pallas_call(
        body,
        out_shape=jax.ShapeDtypeStruct((N_HEADS, LANES), F32),
        compiler_params=pltpu.CompilerParams(vmem_limit_bytes=VMEM_LIMIT),
        name="rel_bias_grad",
    )(tables, dbias)


ROW_TILE = 512


def _head_sum_matrix():
    return (lax.broadcasted_iota(jnp.int32, (HEAD_W, LANES), 0) // HEAD_DIM
            == lax.broadcasted_iota(jnp.int32, (HEAD_W, LANES), 1)).astype(F32)


def _head_spread_matrix(offset=0):
    return (lax.broadcasted_iota(jnp.int32, (LANES, HEAD_W), 0)
            == lax.broadcasted_iota(jnp.int32, (LANES, HEAD_W), 1) // HEAD_DIM + offset).astype(F32)


def _head_gather_matrix(offset=0):
    return (lax.broadcasted_iota(jnp.int32, (HEAD_W, LANES), 0) // HEAD_DIM + offset
            == lax.broadcasted_iota(jnp.int32, (HEAD_W, LANES), 1)).astype(F32)


def _split3(x):
    hi = x.astype(BF16)
    rest = x - hi.astype(F32)
    mid = rest.astype(BF16)
    return hi, mid, (rest - mid.astype(F32)).astype(BF16)


def _pick(x, onehot):
    m = onehot.astype(BF16)
    hi, mid, lo = _split3(x)
    return _nn(hi, m) + (_nn(mid, m) + _nn(lo, m))


def _pick_left(onehot, x):
    m = onehot.astype(BF16)
    hi, mid, lo = _split3(x)
    return _nn(m, hi) + (_nn(m, mid) + _nn(m, lo))


def _tri(lower, strict=False):
    r = lax.broadcasted_iota(jnp.int32, (CHUNK, CHUNK), 0)
    c = lax.broadcasted_iota(jnp.int32, (CHUNK, CHUNK), 1)
    if lower:
        return (c < r) if strict else (c <= r)
    return c >= r


def _softplus(z):
    return jnp.maximum(z, 0.0) + jnp.log(1.0 + jnp.exp(-jnp.abs(z)))


def _conv_taps(stage, w_ref, rows):
    return (w_ref[3:4, :] * stage[8:8 + rows, :] + w_ref[2:3, :] * stage[7:7 + rows, :]
            + w_ref[1:2, :] * stage[6:6 + rows, :] + w_ref[0:1, :] * stage[5:5 + rows, :])


def _l2_scale(xc, hsum, hspread):
    ssq = _pick(xc * xc, hsum)
    return _pick(lax.rsqrt(ssq + EPS), hspread)


def _stage_rows(stage, x_ref, xp_ref, i):
    stage[0:8, :] = jnp.where(i == 0, 0.0, xp_ref[...])
    stage[8:8 + ROW_TILE, :] = x_ref[...]


def _delta_prep_fwd(qkvz, ba, conv_w, alog_row, dt_row):
    seq = qkvz.shape[0]
    qkv_w = 3 * HEAD_W

    def body(x_ref, xp_ref, ba_ref, w_ref, al_ref, dt_ref, out_ref, stage):
        i = pl.program_id(0)
        _stage_rows(stage, x_ref, xp_ref, i)
        act = _silu(_conv_taps(stage, w_ref, ROW_TILE))
        hsum, hspread = _head_sum_matrix(), _head_spread_matrix()
        qc, kc = act[:, :HEAD_W], act[:, HEAD_W:2 * HEAD_W]
        out_ref[0] = qc * _l2_scale(qc, hsum, hspread) * (HEAD_DIM ** -0.5)
        out_ref[1] = kc * _l2_scale(kc, hsum, hspread)
        out_ref[2] = act[:, 2 * HEAD_W:]
        bav = ba_ref[...]
        out_ref[3] = _pick(_sigmoid(bav), hspread)
        g8 = -jnp.exp(al_ref[...]) * _softplus(bav + dt_ref[...])
        gb = _pick(g8, _head_spread_matrix(N_HEADS))
        cum = _tri(True).astype(F32)
        for ch in range(ROW_TILE // CHUNK):
            rows = slice(ch * CHUNK, (ch + 1) * CHUNK)
            out_ref[4, rows, :] = _pick_left(cum, gb[rows])

    return pl.pallas_call(
        body,
        grid=(seq // ROW_TILE,),
        in_specs=[
            pl.BlockSpec((ROW_TILE, qkv_w), lambda i: (i, 0)),
            pl.BlockSpec((8, qkv_w), lambda i: (jnp.maximum(i * (ROW_TILE // 8) - 1, 0), 0)),
            pl.BlockSpec((ROW_TILE, LANES), lambda i: (i, 0)),
            pl.BlockSpec((4, qkv_w), lambda i: (0, 0)),
            pl.BlockSpec((1, LANES), lambda i: (0, 0)),
            pl.BlockSpec((1, LANES), lambda i: (0, 0)),
        ],
        out_specs=pl.BlockSpec((5, ROW_TILE, HEAD_W), lambda i: (0, i, 0)),
        out_shape=jax.ShapeDtypeStruct((5, seq, HEAD_W), F32),
        scratch_shapes=[pltpu.VMEM((ROW_TILE + 8, qkv_w), F32)],
        compiler_params=_params(("arbitrary",)),
        name="delta_prep_fwd",
    )(qkvz, qkvz, ba, conv_w, alog_row, dt_row)


def _split(x):
    hi = x.astype(BF16)
    return hi, (x - hi.astype(F32)).astype(BF16)


def _dot3(a, b, dot=_nn):
    return dot(a[0], b[0]) + (dot(a[0], b[1]) + dot(a[1], b[0]))


def _unit_lower_inverses(mats):
    eye = (lax.broadcasted_iota(jnp.int32, (CHUNK, CHUNK), 0)
           == lax.broadcasted_iota(jnp.int32, (CHUNK, CHUNK), 1)).astype(F32)
    invs = [eye - a for a in mats]
    powers = [_split(a) for a in mats]
    for step in range(5):
        squares = [_dot3(p, p) for p in powers]
        powers = [_split(s) for s in squares]
        invs = [inv + _dot3(_split(inv), p) for inv, p in zip(invs, powers)]
    return invs


def _chunk_terms(q, k, v, beta, gc):
    causal, strict = _tri(True), _tri(True, strict=True)
    e = jnp.exp(gc)
    g_last = jnp.broadcast_to(gc[CHUNK - 1:CHUNK, :], (CHUNK, CHUNK))
    f = jnp.exp(g_last - gc)
    e_last = jnp.exp(g_last)
    decay = jnp.where(causal, jnp.exp(jnp.where(causal, gc - gc.T, 0.0)), 0.0)
    kb = k * beta
    a_mat = jnp.where(strict, _nt(kb.astype(BF16), k.astype(BF16)) * decay, 0.0)
    qk = jnp.where(causal, _nt(q.astype(BF16), k.astype(BF16)) * decay, 0.0)
    return e, f, e_last, decay, kb, a_mat, qk


GROUP = 8
UNROLL = 8


def _chunk_rows(ci):
    return pl.ds(pl.multiple_of(ci * CHUNK, CHUNK), CHUNK)


def _pair_specs(n_planes):
    return pl.BlockSpec((n_planes, GROUP * CHUNK, LANES), lambda p, g: (0, g, p))


def _delta_chunk_fwd(xs):
    seq = xs.shape[1]
    rows_per_step = GROUP * CHUNK

    def body(x_ref, inv_ref, qk_ref, u_ref, w_ref):
        for hh in range(2):
            lanes = slice(hh * HEAD_DIM, (hh + 1) * HEAD_DIM)
            rows = [slice(step * CHUNK, (step + 1) * CHUNK) for step in range(GROUP)]
            xh = [[x_ref[j, r, lanes] for j in range(5)] for r in rows]
            terms = [_chunk_terms(*x) for x in xh]
            invs = _unit_lower_inverses([t[5] for t in terms])
            for r, x, t, inv in zip(rows, xh, terms, invs):
                e, kb, qk = t[0], t[4], t[6]
                inv_parts = _split(inv)
                inv_ref[hh, r, :] = inv
                qk_ref[hh, r, :] = qk
                u_ref[hh, r, :] = _dot3(inv_parts, _split(x[2] * x[3]))
                w_ref[hh, r, :] = _dot3(inv_parts, _split(kb * e))

    out = pl.BlockSpec((2, rows_per_step, HEAD_DIM), lambda p, g: (p, g, 0))
    return pl.pallas_call(
        body,
        grid=(N_HEADS // 2, seq // rows_per_step),
        in_specs=[_pair_specs(5)],
        out_specs=[out] * 4,
        out_shape=[jax.ShapeDtypeStruct((N_HEADS, seq, HEAD_DIM), F32)] * 4,
        compiler_params=_params(("parallel", "parallel")),
        name="delta_chunk_fwd",
    )(xs)


def _decays(gc):
    g_last = jnp.broadcast_to(gc[CHUNK - 1:CHUNK, :], (CHUNK, CHUNK))
    return jnp.exp(gc), jnp.exp(g_last - gc), jnp.exp(g_last)


def _token_blocks(index, n_steps=None):
    rows_per_step = GROUP * CHUNK
    if n_steps is None:
        return pl.BlockSpec((1, rows_per_step, HEAD_W), lambda g: (index, g, 0))
    return pl.BlockSpec((1, rows_per_step, HEAD_W), lambda g: (index, n_steps - 1 - g, 0))


def _head_lanes(h):
    return pl.ds(h * HEAD_DIM, HEAD_DIM)


def _delta_scan_fwd(xs, qk_h, u_h, w_h):
    seq = xs.shape[1]
    rows_per_step = GROUP * CHUNK

    def body(q_ref, k_ref, gc_ref, qk_ref, u_ref, w_ref, o_ref, st_ref, state):
        @pl.when(pl.program_id(0) == 0)
        def _():
            state[...] = jnp.zeros_like(state)

        def chunk(ci, carry):
            rows = _chunk_rows(ci)
            heads = range(N_HEADS)
            dec = [_decays(gc_ref[0, rows, _head_lanes(h)]) for h in heads]
            s = [state[h] for h in heads]
            sb = [s[h].astype(BF16) for h in heads]
            vnb = [(u_ref[h, rows, :] - _nn(w_ref[h, rows, :].astype(BF16), sb[h])).astype(BF16) for h in heads]
            for h in heads:
                o_ref[rows, _head_lanes(h)] = (_nn((q_ref[0, rows, _head_lanes(h)] * dec[h][0]).astype(BF16), sb[h])
                                               + _nn(qk_ref[h, rows, :].astype(BF16), vnb[h]))
                st_ref[h, rows, :] = s[h]
            for h in heads:
                state[h] = s[h] * dec[h][2] + _tn((k_ref[0, rows, _head_lanes(h)] * dec[h][1]).astype(BF16), vnb[h])
            return carry

        lax.fori_loop(0, GROUP, chunk, 0)

    blk = pl.BlockSpec((N_HEADS, rows_per_step, HEAD_DIM), lambda g: (0, g, 0))
    return pl.pallas_call(
        body,
        grid=(seq // rows_per_step,),
        in_specs=[_token_blocks(0), _token_blocks(1), _token_blocks(4), blk, blk, blk],
        out_specs=[pl.BlockSpec((rows_per_step, HEAD_W), lambda g: (g, 0)), blk],
        out_shape=[jax.ShapeDtypeStruct((seq, HEAD_W), F32), jax.ShapeDtypeStruct((N_HEADS, seq, HEAD_DIM), F32)],
        scratch_shapes=[pltpu.VMEM((N_HEADS, CHUNK, CHUNK), F32)],
        compiler_params=_params(("arbitrary",)),
        name="delta_scan_fwd",
    )(xs, xs, xs, qk_h, u_h, w_h)


def _delta_scan_bwd(xs, qk_h, w_h, do):
    seq = xs.shape[1]
    rows_per_step = GROUP * CHUNK
    n_steps = seq // rows_per_step

    def body(q_ref, k_ref, gc_ref, qk_ref, w_ref, do_ref, dsn_ref, dvn_ref, dstate):
        @pl.when(pl.program_id(0) == 0)
        def _():
            dstate[...] = jnp.zeros_like(dstate)

        def chunk(step, carry):
            rows = _chunk_rows(GROUP - 1 - step)
            heads = range(N_HEADS)
            dec = [_decays(gc_ref[0, rows, _head_lanes(h)]) for h in heads]
            ds_next = [dstate[h] for h in heads]
            dob = [do_ref[rows, _head_lanes(h)].astype(BF16) for h in heads]
            dv_new = [_tn(qk_ref[h, rows, :].astype(BF16), dob[h])
                      + _nn((k_ref[0, rows, _head_lanes(h)] * dec[h][1]).astype(BF16), ds_next[h].astype(BF16))
                      for h in heads]
            for h in heads:
                dsn_ref[h, rows, :] = ds_next[h]
                dvn_ref[h, rows, :] = dv_new[h]
            for h in heads:
                dstate[h] = (_tn((q_ref[0, rows, _head_lanes(h)] * dec[h][0]).astype(BF16), dob[h])
                             + dec[h][2] * ds_next[h] - _tn(w_ref[h, rows, :].astype(BF16), dv_new[h].astype(BF16)))
            return carry

        lax.fori_loop(0, GROUP, chunk, 0)

    blk = pl.BlockSpec((N_HEADS, rows_per_step, HEAD_DIM), lambda g: (0, n_steps - 1 - g, 0))
    return pl.pallas_call(
        body,
        grid=(n_steps,),
        in_specs=[_token_blocks(0, n_steps), _token_blocks(1, n_steps), _token_blocks(4, n_steps), blk, blk,
                  pl.BlockSpec((rows_per_step, HEAD_W), lambda g: (n_steps - 1 - g, 0))],
        out_specs=[blk, blk],
        out_shape=[jax.ShapeDtypeStruct((N_HEADS, seq, HEAD_DIM), F32)] * 2,
        scratch_shapes=[pltpu.VMEM((N_HEADS, CHUNK, CHUNK), F32)],
        compiler_params=_params(("arbitrary",)),
        name="delta_scan_bwd",
    )(xs, xs, xs, qk_h, w_h, do)


def _delta_chunk_bwd(xs, inv_h, u_h, w_h, st_h, dsn_h, dvn_h, do):
    seq = xs.shape[1]
    rows_per_step = GROUP * CHUNK

    def body(x_ref, inv_ref, u_ref, w_ref, st_ref, dsn_ref, dvn_ref, do_ref, dx_ref):
        causal, strict = _tri(True), _tri(True, strict=True)
        last_row = lax.broadcasted_iota(jnp.int32, (CHUNK, CHUNK), 0) == CHUNK - 1

        def bf(vals):
            return [val.astype(BF16) for val in vals]

        def group(hh, first):
            lanes = slice(hh * HEAD_DIM, (hh + 1) * HEAD_DIM)
            rows = [slice(step * CHUNK, (step + 1) * CHUNK) for step in range(first, first + UNROLL)]
            n = range(UNROLL)
            q, k, v, beta, gc = [[x_ref[j, r, lanes] for r in rows] for j in range(5)]
            terms = [_chunk_terms(q[i], k[i], v[i], beta[i], gc[i]) for i in n]
            e, f, e_last, decay, kb, a_mat, qk = [[t[j] for t in terms] for j in range(7)]
            inv = [_split(inv_ref[hh, r, :]) for r in rows]
            u = [u_ref[hh, r, :] for r in rows]
            w = [w_ref[hh, r, :] for r in rows]
            s = [st_ref[hh, r, :] for r in rows]
            ds_next = [dsn_ref[hh, r, :] for r in rows]
            dv_new = [dvn_ref[hh, r, :] for r in rows]
            sb, dsb, dvb, wb = bf(s), bf(ds_next), bf(dv_new), bf(w)
            dob = bf([do_ref[r, lanes] for r in rows])
            qbf, kbf, kbb = bf(q), bf(k), bf(kb)
            vnb = bf([u[i] - _nn(wb[i], sb[i]) for i in n])
            dqe = [_nt(dob[i], sb[i]) for i in n]
            dw = [-_nt(dvb[i], sb[i]) for i in n]
            dkf = [_nt(vnb[i], dsb[i]) for i in n]
            dqk = [jnp.where(causal, _nt(dob[i], vnb[i]), 0.0) for i in n]
            drhs_u = [_dot3(inv[i], _split(dv_new[i]), _tn) for i in n]
            drhs_w = [_dot3(inv[i], _split(dw[i]), _tn) for i in n]
            da = [-jnp.where(strict, _nt(drhs_u[i].astype(BF16), u[i].astype(BF16))
                             + _nt(drhs_w[i].astype(BF16), wb[i]), 0.0) for i in n]
            dad = bf([da[i] * decay[i] for i in n])
            dqd = bf([dqk[i] * decay[i] for i in n])
            dkb = [e[i] * drhs_w[i] + _nn(dad[i], kbf[i]) for i in n]
            dk = [_tn(dad[i], kbb[i]) + _tn(dqd[i], qbf[i]) + f[i] * dkf[i] + beta[i] * dkb[i] for i in n]
            dq = [_nn(dqd[i], kbf[i]) + e[i] * dqe[i] for i in n]
            for i in n:
                de_full = kb[i] * drhs_w[i] + q[i] * dqe[i]
                df_full = k[i] * dkf[i]
                m = da[i] * a_mat[i] + dqk[i] * qk[i]
                dgc = de_full * e[i] - df_full * f[i] + m - m.T
                tail = jnp.sum(df_full * f[i] + s[i] * ds_next[i] * e_last[i], axis=0, keepdims=True)
                dgc = dgc + jnp.where(last_row, jnp.broadcast_to(tail, (CHUNK, CHUNK)), 0.0)
                dx_ref[0, rows[i], lanes] = dq[i]
                dx_ref[1, rows[i], lanes] = dk[i]
                dx_ref[2, rows[i], lanes] = beta[i] * drhs_u[i]
                dx_ref[3, rows[i], lanes] = v[i] * drhs_u[i] + k[i] * dkb[i]
                dx_ref[4, rows[i], lanes] = dgc

        for hh in range(2):
            for first in range(0, GROUP, UNROLL):
                group(hh, first)

    blk = pl.BlockSpec((2, rows_per_step, HEAD_DIM), lambda p, g: (p, g, 0))
    return pl.pallas_call(
        body,
        grid=(N_HEADS // 2, seq // rows_per_step),
        in_specs=[_pair_specs(5)] + [blk] * 6 + [pl.BlockSpec((rows_per_step, LANES), lambda p, g: (g, p))],
        out_specs=_pair_specs(5),
        out_shape=jax.ShapeDtypeStruct((5, seq, HEAD_W), F32),
        compiler_params=_params(("parallel", "parallel")),
        name="delta_chunk_bwd",
    )(xs, inv_h, u_h, w_h, st_h, dsn_h, dvn_h, do)


def _delta_post_fwd(o, qkvz, gain_row):
    seq = o.shape[0]

    def body(o_ref, z_ref, g_ref, y_ref):
        ov = o_ref[...]
        ms = _pick(ov * ov, _head_sum_matrix()) * (1.0 / HEAD_DIM)
        rb = _pick(lax.rsqrt(ms + EPS), _head_spread_matrix())
        y_ref[...] = (ov * rb * g_ref[...] * _silu(z_ref[...])).astype(y_ref.dtype)

    tile = pl.BlockSpec((ROW_TILE, HEAD_W), lambda i: (i, 0))
    return pl.pallas_call(
        body,
        grid=(seq // ROW_TILE,),
        in_specs=[tile, pl.BlockSpec((ROW_TILE, HEAD_W), lambda i: (i, 3)), pl.BlockSpec((1, HEAD_W), lambda i: (0, 0))],
        out_specs=tile,
        out_shape=jax.ShapeDtypeStruct((seq, HEAD_W), BF16),
        compiler_params=_params(("arbitrary",)),
        name="delta_post_fwd",
    )(o, qkvz, gain_row)


def _delta_post_bwd(dy, o, qkvz, gain_row):
    seq = o.shape[0]

    def body(dy_ref, o_ref, z_ref, g_ref, do_ref, dz_ref, dg_ref):
        @pl.when(pl.program_id(0) == 0)
        def _():
            dg_ref[...] = jnp.zeros_like(dg_ref)

        ov, zv, dyv, gain = o_ref[...], z_ref[...], dy_ref[...], g_ref[...]
        hsum, hspread = _head_sum_matrix(), _head_spread_matrix()
        ms = _pick(ov * ov, hsum) * (1.0 / HEAD_DIM)
        rb = _pick(lax.rsqrt(ms + EPS), hspread)
        ohat = ov * rb
        dz_ref[...] = dyv * ohat * gain * _dsilu(zv)
        dn = dyv * _silu(zv)
        dg_ref[0:1, :] += jnp.sum(dn * ohat, axis=0, keepdims=True)
        dohat = dn * gain

        @pl.when(pl.program_id(0) == pl.num_programs(0) - 1)
        def _():
            fold = (lax.broadcasted_iota(jnp.int32, (HEAD_W, HEAD_W), 0) % HEAD_DIM
                    == lax.broadcasted_iota(jnp.int32, (HEAD_W, HEAD_W), 1)).astype(F32)
            dg_ref[1:2, :] = _pick(dg_ref[0:1, :], fold)

        proj = _pick(_pick(dohat * ohat, hsum) * (1.0 / HEAD_DIM), hspread)
        do_ref[...] = rb * (dohat - ohat * proj)

    tile = pl.BlockSpec((ROW_TILE, HEAD_W), lambda i: (i, 0))
    return pl.pallas_call(
        body,
        grid=(seq // ROW_TILE,),
        in_specs=[pl.BlockSpec((ROW_TILE, HEAD_W), lambda i: (i, 1)), tile,
                  pl.BlockSpec((ROW_TILE, HEAD_W), lambda i: (i, 3)), pl.BlockSpec((1, HEAD_W), lambda i: (0, 0))],
        out_specs=[tile, tile, pl.BlockSpec((2, HEAD_W), lambda i: (0, 0))],
        out_shape=[jax.ShapeDtypeStruct((seq, HEAD_W), F32), jax.ShapeDtypeStruct((seq, HEAD_W), F32),
                   jax.ShapeDtypeStruct((2, HEAD_W), F32)],
        compiler_params=_params(("arbitrary",)),
        name="delta_post_bwd",
    )(dy, o, qkvz, gain_row)


def _delta_prep_bwd(qkvz, ba, conv_w, alog_row, dt_row, dxs):
    seq = qkvz.shape[0]
    qkv_w = 3 * HEAD_W

    def body(x_ref, xp_ref, ba_ref, w_ref, al_ref, dt_ref, dx_ref, dconv_ref, dba_ref, dvec_ref, stage):
        i = pl.program_id(0)

        @pl.when(i == 0)
        def _():
            dvec_ref[...] = jnp.zeros_like(dvec_ref)

        _stage_rows(stage, x_ref, xp_ref, i)
        pre = _conv_taps(stage, w_ref, ROW_TILE)
        act = _silu(pre)
        slope = _dsilu(pre)
        hsum, hspread = _head_sum_matrix(), _head_spread_matrix()
        for j, scale in ((0, HEAD_DIM ** -0.5), (1, 1.0)):
            cols = slice(j * HEAD_W, (j + 1) * HEAD_W)
            xc = act[:, cols]
            rb = _l2_scale(xc, hsum, hspread)
            xhat = xc * rb
            dhat = dx_ref[j] * scale
            proj = _pick(_pick(dhat * xhat, hsum), hspread)
            dconv_ref[:, cols] = rb * (dhat - xhat * proj) * slope[:, cols]
        dconv_ref[:, 2 * HEAD_W:] = dx_ref[2] * slope[:, 2 * HEAD_W:]

        bav = ba_ref[...]
        beta8 = _sigmoid(bav)
        dbeta8 = _pick(dx_ref[3], _head_gather_matrix())
        dgc8 = _pick(dx_ref[4], _head_gather_matrix(N_HEADS))
        rev = _tri(False).astype(F32)
        z = bav + dt_ref[...]
        ea = jnp.exp(al_ref[...])
        g8 = -ea * _softplus(z)
        sig = _sigmoid(z)
        d_alog = jnp.zeros((1, LANES), F32)
        d_dt = jnp.zeros((1, LANES), F32)
        for ch in range(ROW_TILE // CHUNK):
            rows = slice(ch * CHUNK, (ch + 1) * CHUNK)
            dg8 = _pick_left(rev, dgc8[rows])
            da = -dg8 * ea * sig[rows]
            dba_ref[rows, :] = dbeta8[rows] * beta8[rows] * (1.0 - beta8[rows]) + da
            d_alog = d_alog + jnp.sum(dg8 * g8[rows], axis=0, keepdims=True)
            d_dt = d_dt + jnp.sum(da, axis=0, keepdims=True)
        dvec_ref[0:1, :] += d_alog
        dvec_ref[1:2, :] += d_dt

    return pl.pallas_call(
        body,
        grid=(seq // ROW_TILE,),
        in_specs=[
            pl.BlockSpec((ROW_TILE, qkv_w), lambda i: (i, 0)),
            pl.BlockSpec((8, qkv_w), lambda i: (jnp.maximum(i * (ROW_TILE // 8) - 1, 0), 0)),
            pl.BlockSpec((ROW_TILE, LANES), lambda i: (i, 0)),
            pl.BlockSpec((4, qkv_w), lambda i: (0, 0)),
            pl.BlockSpec((1, LANES), lambda i: (0, 0)),
            pl.BlockSpec((1, LANES), lambda i: (0, 0)),
            pl.BlockSpec((5, ROW_TILE, HEAD_W), lambda i: (0, i, 0)),
        ],
        out_specs=[pl.BlockSpec((ROW_TILE, qkv_w), lambda i: (i, 0)),
                   pl.BlockSpec((ROW_TILE, LANES), lambda i: (i, 0)),
                   pl.BlockSpec((2, LANES), lambda i: (0, 0))],
        out_shape=[jax.ShapeDtypeStruct((seq, qkv_w), F32), jax.ShapeDtypeStruct((seq, LANES), F32),
                   jax.ShapeDtypeStruct((2, LANES), F32)],
        scratch_shapes=[pltpu.VMEM((ROW_TILE + 8, qkv_w), F32)],
        compiler_params=_params(("arbitrary",)),
        name="delta_prep_bwd",
    )(qkvz, qkvz, ba, conv_w, alog_row, dt_row, dxs)


def _conv_bwd(dconv, qkvz, conv_w):
    seq = dconv.shape[0]
    qkv_w = 3 * HEAD_W
    n_tiles = seq // ROW_TILE

    def body(dy_ref, dyn_ref, x_ref, xp_ref, w_ref, dx_ref, dw_ref, stage, dstage):
        i = pl.program_id(0)

        @pl.when(i == 0)
        def _():
            dw_ref[...] = jnp.zeros_like(dw_ref)

        _stage_rows(stage, x_ref, xp_ref, i)
        dstage[0:ROW_TILE, :] = dy_ref[...]
        dstage[ROW_TILE:ROW_TILE + 8, :] = jnp.where(i == n_tiles - 1, 0.0, dyn_ref[...])
        dy = dy_ref[...]
        dx_ref[...] = (w_ref[3:4, :] * dy + w_ref[2:3, :] * dstage[1:1 + ROW_TILE, :]
                       + w_ref[1:2, :] * dstage[2:2 + ROW_TILE, :] + w_ref[0:1, :] * dstage[3:3 + ROW_TILE, :])
        for j in range(4):
            dw_ref[j:j + 1, :] += jnp.sum(dy * stage[5 + j:5 + j + ROW_TILE, :], axis=0, keepdims=True)

    tile = pl.BlockSpec((ROW_TILE, qkv_w), lambda i: (i, 0))
    return pl.pallas_call(
        body,
        grid=(n_tiles,),
        in_specs=[
            tile,
            pl.BlockSpec((8, qkv_w), lambda i: (jnp.minimum((i + 1) * (ROW_TILE // 8), seq // 8 - 1), 0)),
            tile,
            pl.BlockSpec((8, qkv_w), lambda i: (jnp.maximum(i * (ROW_TILE // 8) - 1, 0), 0)),
            pl.BlockSpec((4, qkv_w), lambda i: (0, 0)),
        ],
        out_specs=[tile, pl.BlockSpec((4, qkv_w), lambda i: (0, 0))],
        out_shape=[jax.ShapeDtypeStruct((seq, qkv_w), F32), jax.ShapeDtypeStruct((4, qkv_w), F32)],
        scratch_shapes=[pltpu.VMEM((ROW_TILE + 8, qkv_w), F32), pltpu.VMEM((ROW_TILE + 8, qkv_w), F32)],
        compiler_params=_params(("arbitrary",)),
        name="conv_bwd",
    )(dconv, dconv, qkvz, qkvz, conv_w)


FF_TILE = 1408


def _row(a):
    return pl.BlockSpec((1, a), lambda *_: (0, 0))


def _rms_fwd(xv, gain):
    rstd = lax.rsqrt(jnp.mean(xv * xv, axis=-1, keepdims=True) + EPS)
    xhat = xv * rstd
    return xhat, rstd, xhat * gain


def _rms_bwd(dnorm, xhat, rstd, gain):
    dxhat = dnorm * gain
    dx = rstd * (dxhat - xhat * jnp.mean(dxhat * xhat, axis=-1, keepdims=True))
    return dx, jnp.sum(dnorm * xhat, axis=0, keepdims=True)


def _inproj_fwd(x, gain, scale, shift, w_a, w_d, w_ba):
    seq = x.shape[0]

    def body(x_ref, g_ref, sc_ref, sh_ref, wa_ref, wd_ref, wb_ref, h_ref, a_ref, d_ref, b_ref):
        _, _, norm = _rms_fwd(x_ref[...], g_ref[...])
        h = (norm * (1.0 + sc_ref[...]) + sh_ref[...]).astype(BF16)
        h_ref[...] = h
        a_ref[...] = _nn(h, wa_ref[...])
        d_ref[...] = _nn(h, wd_ref[...])
        b_ref[...] = _nn(h, wb_ref[...])

    def rows(width):
        return pl.BlockSpec((ROW_TILE, width), lambda i: (i, 0))

    def whole(a):
        return pl.BlockSpec(a.shape, lambda i: (0, 0))

    return pl.pallas_call(
        body,
        grid=(seq // ROW_TILE,),
        in_specs=[rows(D_MODEL), _row(D_MODEL), _row(D_MODEL), _row(D_MODEL), whole(w_a), whole(w_d), whole(w_ba)],
        out_specs=[rows(D_MODEL), rows(3 * HEAD_W), rows(4 * HEAD_W), rows(LANES)],
        out_shape=[jax.ShapeDtypeStruct((seq, D_MODEL), BF16), jax.ShapeDtypeStruct((seq, 3 * HEAD_W), F32),
                   jax.ShapeDtypeStruct((seq, 4 * HEAD_W), F32), jax.ShapeDtypeStruct((seq, LANES), F32)],
        compiler_params=_params(("arbitrary",)),
        name="inproj_fwd",
    )(x, gain, scale, shift, w_a, w_d, w_ba)


def _outproj_fwd(y_attn, y_delta, w_out, x, gate1, gain, scale, shift):
    seq = x.shape[0]

    def body(ya_ref, yd_ref, wa_ref, wd_ref, x_ref, g1_ref, g_ref, sc_ref, sh_ref, x1_ref, h_ref, y_ref):
        y = _nn(ya_ref[...].astype(BF16), wa_ref[...]) + _nn(yd_ref[...], wd_ref[...])
        x1 = x_ref[...] + g1_ref[...] * y
        _, _, norm = _rms_fwd(x1, g_ref[...])
        x1_ref[...] = x1
        h_ref[...] = (norm * (1.0 + sc_ref[...]) + sh_ref[...]).astype(BF16)
        y_ref[...] = y.astype(BF16)

    def rows(width):
        return pl.BlockSpec((ROW_TILE, width), lambda i: (i, 0))

    return pl.pallas_call(
        body,
        grid=(seq // ROW_TILE,),
        in_specs=[rows(HEAD_W), rows(HEAD_W),
                  pl.BlockSpec((HEAD_W, D_MODEL), lambda i: (0, 0)), pl.BlockSpec((HEAD_W, D_MODEL), lambda i: (1, 0)),
                  rows(D_MODEL), _row(D_MODEL), _row(D_MODEL), _row(D_MODEL), _row(D_MODEL)],
        out_specs=[rows(D_MODEL), rows(D_MODEL), rows(D_MODEL)],
        out_shape=[jax.ShapeDtypeStruct((seq, D_MODEL), F32), jax.ShapeDtypeStruct((seq, D_MODEL), BF16),
                   jax.ShapeDtypeStruct((seq, D_MODEL), BF16)],
        compiler_params=_params(("arbitrary",)),
        name="outproj_fwd",
    )(y_attn, y_delta, w_out, w_out, x, gate1, gain, scale, shift)


def _ffn_fwd(h2, w_gate, w_up, w_down, x1, gate2, final_gain, target):
    seq = h2.shape[0]
    n_rows, n_ff = seq // ROW_TILE, D_FF // FF_TILE

    def body(h_ref, wg_ref, wu_ref, wd_ref, x1_ref, g2_ref, gf_ref, t_ref, gate_ref, up_ref, dx2_ref, st_ref, acc):
        i, j = pl.program_id(0), pl.program_id(1)

        @pl.when((i == 0) & (j == 0))
        def _():
            st_ref[...] = jnp.zeros_like(st_ref)

        h = h_ref[...]
        gate = _nn(h, wg_ref[...])
        up = _nn(h, wu_ref[...])
        gate_ref[...] = gate.astype(BF16)
        up_ref[...] = up.astype(BF16)
        part = _nn((_silu(gate) * up).astype(BF16), wd_ref[...])

        @pl.when(j == 0)
        def _():
            acc[...] = part

        @pl.when(j > 0)
        def _():
            acc[...] += part

        @pl.when(j == n_ff - 1)
        def _():
            y2 = acc[...]
            x2 = x1_ref[...] + g2_ref[...] * y2
            xhat, rstd, out = _rms_fwd(x2, gf_ref[...])
            diff = out - t_ref[...]
            dx2, dgain = _rms_bwd(diff * (1.0 / D_MODEL), xhat, rstd, gf_ref[...])
            dx2_ref[...] = dx2
            st_ref[0:1, :] += dgain
            st_ref[1:2, :] += jnp.sum(dx2 * y2, axis=0, keepdims=True)
            st_ref[2:3, :] += jnp.sum(diff * diff, axis=0, keepdims=True) * (0.5 / D_MODEL)

        @pl.when((i == n_rows - 1) & (j == n_ff - 1))
        def _():
            st_ref[3:4, :] = jnp.broadcast_to(jnp.sum(st_ref[2:3, :], keepdims=True), (1, D_MODEL))

    def rows(width):
        return pl.BlockSpec((ROW_TILE, width), lambda i, j: (i, 0))

    ff = pl.BlockSpec((ROW_TILE, FF_TILE), lambda i, j: (i, j))
    return pl.pallas_call(
        body,
        grid=(n_rows, n_ff),
        in_specs=[rows(D_MODEL),
                  pl.BlockSpec((D_MODEL, FF_TILE), lambda i, j: (0, j)), pl.BlockSpec((D_MODEL, FF_TILE), lambda i, j: (0, j)),
                  pl.BlockSpec((FF_TILE, D_MODEL), lambda i, j: (j, 0)),
                  rows(D_MODEL), _row(D_MODEL), _row(D_MODEL), rows(D_MODEL)],
        out_specs=[ff, ff, rows(D_MODEL), pl.BlockSpec((8, D_MODEL), lambda i, j: (0, 0))],
        out_shape=[jax.ShapeDtypeStruct((seq, D_FF), BF16), jax.ShapeDtypeStruct((seq, D_FF), BF16),
                   jax.ShapeDtypeStruct((seq, D_MODEL), F32), jax.ShapeDtypeStruct((8, D_MODEL), F32)],
        scratch_shapes=[pltpu.VMEM((ROW_TILE, D_MODEL), F32)],
        compiler_params=_params(("arbitrary", "arbitrary")),
        name="ffn_fwd",
    )(h2, w_gate, w_up, w_down, x1, gate2, final_gain, target)


def _ffn_bwd(dx2, gate, up, w_gate, w_up, w_down, x1, y, gate2, gate1, gain, scale):
    seq = dx2.shape[0]

    def act_body(dx2_ref, g2_ref, gate_ref, up_ref, wd_ref, dgate_ref, dup_ref, act_ref, dy2_ref):
        dy2 = (g2_ref[...] * dx2_ref[...]).astype(BF16)
        dy2_ref[...] = dy2
        gate = gate_ref[...].astype(F32)
        up = up_ref[...].astype(F32)
        dact = _nt(dy2, wd_ref[...])
        silu = _silu(gate)
        act_ref[...] = (silu * up).astype(BF16)
        dgate_ref[...] = (dact * up * _dsilu(gate)).astype(BF16)
        dup_ref[...] = (dact * silu).astype(BF16)

    def rows2(width):
        return pl.BlockSpec((ROW_TILE, width), lambda i, j: (i, 0))

    ff = pl.BlockSpec((ROW_TILE, FF_TILE), lambda i, j: (i, j))
    dgate, dup, act, dy2 = pl.pallas_call(
        act_body,
        grid=(seq // ROW_TILE, D_FF // FF_TILE),
        in_specs=[rows2(D_MODEL), _row(D_MODEL), ff, ff, pl.BlockSpec((FF_TILE, D_MODEL), lambda i, j: (j, 0))],
        out_specs=[ff, ff, ff, rows2(D_MODEL)],
        out_shape=[jax.ShapeDtypeStruct((seq, D_FF), BF16)] * 3 + [jax.ShapeDtypeStruct((seq, D_MODEL), BF16)],
        compiler_params=_params(("arbitrary", "arbitrary")),
        name="ffn_bwd_act",
    )(dx2, gate2, gate, up, w_down)

    def in_body(dgate_ref, dup_ref, wg_ref, wu_ref, dx2_ref, x1_ref, y_ref, g1_ref, g_ref, sc_ref,
                dx1_ref, dy_ref, st_ref):
        @pl.when(pl.program_id(0) == 0)
        def _():
            st_ref[...] = jnp.zeros_like(st_ref)

        dh = _nt(dgate_ref[...], wg_ref[...]) + _nt(dup_ref[...], wu_ref[...])
        xhat, rstd, norm = _rms_fwd(x1_ref[...], g_ref[...])
        dxn, dgain = _rms_bwd(dh * (1.0 + sc_ref[...]), xhat, rstd, g_ref[...])
        dx1 = dx2_ref[...] + dxn
        dx1_ref[...] = dx1
        dy_ref[...] = (g1_ref[...] * dx1).astype(BF16)
        st_ref[0:1, :] += jnp.sum(dh, axis=0, keepdims=True)
        st_ref[1:2, :] += jnp.sum(dh * norm, axis=0, keepdims=True)
        st_ref[2:3, :] += dgain
        st_ref[3:4, :] += jnp.sum(dx1 * y_ref[...].astype(F32), axis=0, keepdims=True)

    half_tile = ROW_TILE // 2

    def rows(width):
        return pl.BlockSpec((half_tile, width), lambda i: (i, 0))

    whole = pl.BlockSpec((D_MODEL, D_FF), lambda i: (0, 0))
    dx1, dy, stats = pl.pallas_call(
        in_body,
        grid=(seq // half_tile,),
        in_specs=[rows(D_FF), rows(D_FF), whole, whole, rows(D_MODEL), rows(D_MODEL), rows(D_MODEL),
                  _row(D_MODEL), _row(D_MODEL), _row(D_MODEL)],
        out_specs=[rows(D_MODEL), rows(D_MODEL), pl.BlockSpec((8, D_MODEL), lambda i: (0, 0))],
        out_shape=[jax.ShapeDtypeStruct((seq, D_MODEL), F32), jax.ShapeDtypeStruct((seq, D_MODEL), BF16),
                   jax.ShapeDtypeStruct((8, D_MODEL), F32)],
        compiler_params=_params(("arbitrary",)),
        name="ffn_bwd_in",
    )(dgate, dup, w_gate, w_up, dx2, x1, y, gate1, gain, scale)
    return dgate, dup, act, dy2, dx1, dy, stats


def _outproj_bwd(dy, w_out):
    seq = dy.shape[0]

    def body(dy_ref, w_ref, out_ref):
        out_ref[...] = _nt(dy_ref[...], w_ref[...])

    rows = pl.BlockSpec((ROW_TILE, D_MODEL), lambda i: (i, 0))
    return pl.pallas_call(
        body,
        grid=(seq // ROW_TILE,),
        in_specs=[rows, pl.BlockSpec((D_MODEL, D_MODEL), lambda i: (0, 0))],
        out_specs=rows,
        out_shape=jax.ShapeDtypeStruct((seq, D_MODEL), F32),
        compiler_params=_params(("arbitrary",)),
        name="outproj_bwd",
    )(dy, w_out)


def _inproj_bwd(dq, dk, dv, dxd, dz, dba, w_a, w_d, w_ba, x, dx1, gain, scale):
    seq = x.shape[0]

    def body(dq_ref, dk_ref, dv_ref, dxd_ref, dz_ref, dba_ref, wa_ref, wd_ref, wb_ref, x_ref, dx1_ref, g_ref, sc_ref,
             gx_ref, st_ref):
        @pl.when(pl.program_id(0) == 0)
        def _():
            st_ref[...] = jnp.zeros_like(st_ref)

        dh = (_nt(dq_ref[...].astype(BF16), wa_ref[:, 0:HEAD_W])
              + _nt(dk_ref[...].astype(BF16), wa_ref[:, HEAD_W:2 * HEAD_W])
              + _nt(dv_ref[...].astype(BF16), wa_ref[:, 2 * HEAD_W:])
              + _nt(dxd_ref[...].astype(BF16), wd_ref[:, 0:3 * HEAD_W])
              + _nt(dz_ref[...].astype(BF16), wd_ref[:, 3 * HEAD_W:])
              + _nt(dba_ref[...].astype(BF16), wb_ref[...]))
        xhat, rstd, norm = _rms_fwd(x_ref[...], g_ref[...])
        dxn, dgain = _rms_bwd(dh * (1.0 + sc_ref[...]), xhat, rstd, g_ref[...])
        gx_ref[...] = dx1_ref[...] + dxn
        st_ref[0:1, :] += jnp.sum(dh, axis=0, keepdims=True)
        st_ref[1:2, :] += jnp.sum(dh * norm, axis=0, keepdims=True)
        st_ref[2:3, :] += dgain

    def rows(width):
        return pl.BlockSpec((ROW_TILE, width), lambda i: (i, 0))

    def whole(a):
        return pl.BlockSpec(a.shape, lambda i: (0, 0))

    return pl.pallas_call(
        body,
        grid=(seq // ROW_TILE,),
        in_specs=[rows(HEAD_W), rows(HEAD_W), rows(HEAD_W), rows(3 * HEAD_W), rows(HEAD_W), rows(LANES),
                  whole(w_a), whole(w_d), whole(w_ba), rows(D_MODEL), rows(D_MODEL), _row(D_MODEL), _row(D_MODEL)],
        out_specs=[rows(D_MODEL), pl.BlockSpec((8, D_MODEL), lambda i: (0, 0))],
        out_shape=[jax.ShapeDtypeStruct((seq, D_MODEL), F32), jax.ShapeDtypeStruct((8, D_MODEL), F32)],
        compiler_params=_params(("arbitrary",)),
        name="inproj_bwd",
    )(dq, dk, dv, dxd, dz, dba, w_a, w_d, w_ba, x, dx1, gain, scale)


def _weight_grad(a, b, name):
    seq, m = a.shape
    n = b.shape[1]
    tm = m if m <= 1536 else m // 2
    tn = n if n <= 1536 else n // 2
    n_k = seq // ROW_TILE

    def body(a_ref, b_ref, out_ref):
        part = _tn(a_ref[...].astype(BF16), b_ref[...].astype(BF16))

        @pl.when(pl.program_id(2) == 0)
        def _():
            out_ref[...] = part

        @pl.when(pl.program_id(2) > 0)
        def _():
            out_ref[...] += part

    return pl.pallas_call(
        body,
        grid=(m // tm, n // tn, n_k),
        in_specs=[pl.BlockSpec((ROW_TILE, tm), lambda i, j, k: (k, i)),
                  pl.BlockSpec((ROW_TILE, tn), lambda i, j, k: (k, j))],
        out_specs=pl.BlockSpec((tm, tn), lambda i, j, k: (i, j)),
        out_shape=jax.ShapeDtypeStruct((m, n), F32),
        compiler_params=_params(("arbitrary", "arbitrary", "arbitrary")),
        name=name,
    )(a, b)


def _weight_grad_stack(pieces, b, name):
    seq, n = b.shape
    widths = [a.shape[1] for a in pieces]
    starts = [sum(widths[:i]) for i in range(len(pieces))]

    def body(*refs):
        a_refs, b_ref, out_ref = refs[:len(pieces)], refs[len(pieces)], refs[len(pieces) + 1]

        @pl.when(pl.program_id(0) == 0)
        def _():
            out_ref[...] = jnp.zeros_like(out_ref)

        bb = b_ref[...].astype(BF16)
        for a_ref, start, width in zip(a_refs, starts, widths):
            out_ref[start:start + width, :] += _tn(a_ref[...].astype(BF16), bb)

    def rows(width):
        return pl.BlockSpec((ROW_TILE, width), lambda k: (k, 0))

    return pl.pallas_call(
        body,
        grid=(seq // ROW_TILE,),
        in_specs=[rows(w) for w in widths] + [rows(n)],
        out_specs=pl.BlockSpec((sum(widths), n), lambda k: (0, 0)),
        out_shape=jax.ShapeDtypeStruct((sum(widths), n), F32),
        compiler_params=_params(("arbitrary",)),
        name=name,
    )(*pieces, b)


def _adamw(w, g, m, v, name):
    n_rows, n_cols = w.shape
    if n_rows % 256 == 0:
        block, grid, index = (256, n_cols), (n_rows // 256,), lambda i: (i, 0)
    elif n_cols % 256 == 0:
        block, grid, index = (n_rows, 256), (n_cols // 256,), lambda i: (0, i)
    else:
        block, grid, index = (n_rows, n_cols), (1,), lambda i: (0, 0)

    def body(w_ref, g_ref, m_ref, v_ref, d_ref, nm_ref, nv_ref):
        gv = g_ref[...]
        nm = ADAM_B1 * m_ref[...] + (1.0 - ADAM_B1) * gv
        nv = ADAM_B2 * v_ref[...] + (1.0 - ADAM_B2) * (gv * gv)
        m_hat = nm / (1.0 - ADAM_B1 ** ADAM_STEP)
        v_hat = nv / (1.0 - ADAM_B2 ** ADAM_STEP)
        d_ref[...] = -ADAM_LR * (m_hat / (jnp.sqrt(v_hat) + ADAM_EPS) + ADAM_WD * w_ref[...])
        nm_ref[...] = nm
        nv_ref[...] = nv

    blk = pl.BlockSpec(block, index)
    shape = jax.ShapeDtypeStruct((n_rows, n_cols), F32)
    return pl.pallas_call(
        body,
        grid=grid,
        in_specs=[blk] * 4,
        out_specs=[blk] * 3,
        out_shape=[shape] * 3,
        compiler_params=_params(("arbitrary",)),
        name=name,
    )(w, g, m, v)


IN_WIDTH = 3600
BA_COL = 7 * HEAD_W


def _local_step(x, target, mod, norm_attn_g, w_in, rel_bias, conv_w, a_log, dt_bias, delta_norm_g, w_out,
                norm_ffn_g, w_gate, w_up, w_down, final_norm_g):
    sh1, sc1, g1, sh2, sc2, g2 = [mod[:, i * D_MODEL:(i + 1) * D_MODEL] for i in range(6)]
    w_a = w_in[:, :3 * HEAD_W]
    w_d = w_in[:, 3 * HEAD_W:BA_COL]
    w_ba = jnp.pad(w_in[:, BA_COL:], ((0, 0), (0, LANES - 2 * N_HEADS)))
    tables = jnp.asarray(_attn_tables())
    alog_row = jnp.pad(a_log, ((0, 0), (N_HEADS, LANES - 2 * N_HEADS)))
    dt_row = jnp.pad(dt_bias, ((0, 0), (N_HEADS, LANES - 2 * N_HEADS)))
    gain_row = jnp.tile(delta_norm_g, (1, N_HEADS))

    h1, qkv_a, qkvz, ba = _inproj_fwd(x, norm_attn_g, sc1, sh1, w_a, w_d, w_ba)
    y_attn, lse = _attention_fwd(qkv_a, rel_bias, tables)
    xs = _delta_prep_fwd(qkvz, ba, conv_w, alog_row, dt_row)
    inv_h, qk_h, u_h, w_h = _delta_chunk_fwd(xs)
    o, st_h = _delta_scan_fwd(xs, qk_h, u_h, w_h)
    y_delta = _delta_post_fwd(o, qkvz, gain_row)
    x1, h2, y = _outproj_fwd(y_attn, y_delta, w_out, x, g1, norm_ffn_g, sc2, sh2)
    gate, up, dx2, st_f = _ffn_fwd(h2, w_gate, w_up, w_down, x1, g2, final_norm_g, target)

    dgate, dup, act, dy2, dx1, dy, st_b = _ffn_bwd(dx2, gate, up, w_gate, w_up, w_down, x1, y, g2, g1, norm_ffn_g, sc2)
    grads = {
        "w_gate": _weight_grad(dgate, h2, "wgrad_gate"),
        "w_up": _weight_grad(dup, h2, "wgrad_up"),
        "w_down": _weight_grad(act, dy2, "wgrad_down"),
        "w_out": _weight_grad_stack([y_attn, y_delta], dy, "wgrad_out"),
    }
    dycat = _outproj_bwd(dy, w_out)
    do, dz, dgain = _delta_post_bwd(dycat, o, qkvz, gain_row)
    dsn_h, dvn_h = _delta_scan_bwd(xs, qk_h, w_h, do)
    dxs = _delta_chunk_bwd(xs, inv_h, u_h, w_h, st_h, dsn_h, dvn_h, do)
    dconv, dba, dvec = _delta_prep_bwd(qkvz, ba, conv_w, alog_row, dt_row, dxs)
    dxd, grads["conv_w"] = _conv_bwd(dconv, qkvz, conv_w)
    dq, dk, dv, dbias = _attention_bwd(qkv_a, dycat, y_attn, lse, rel_bias, tables)
    grad_x, st_i = _inproj_bwd(dq, dk, dv, dxd, dz, dba, w_a, w_d, w_ba, x, dx1, norm_attn_g, sc1)
    grads["w_in"] = jnp.concatenate(
        [_weight_grad_stack([dq, dk, dv], h1, "wgrad_in_attn"),
         _weight_grad_stack([dxd, dz, dba], h1, "wgrad_in_delta")[:IN_WIDTH - 3 * HEAD_W]], axis=0)
    grads["rel_bias"] = _rel_bias_grad(dbias, tables)[:, :N_BUCKETS].T
    grads["a_log"] = dvec[0:1, N_HEADS:2 * N_HEADS]
    grads["dt_bias"] = dvec[1:2, N_HEADS:2 * N_HEADS]
    grads["delta_norm_g"] = dgain[1:2, :HEAD_DIM]
    grads["norm_attn_g"] = st_i[2:3]
    grads["norm_ffn_g"] = st_b[2:3]
    grads["final_norm_g"] = st_f[0:1]
    dmod = jnp.concatenate([st_i[0:1], st_i[1:2], st_b[3:4], st_b[0:1], st_b[1:2], st_f[1:2]], axis=1)
    return st_f[3, 0], grad_x, grads, dmod


MESH = pl.DeviceIdType.MESH
OTHER_CHIPS = ((1, 0), (0, 1), (1, 1))
ALL_PEERS = tuple((m >> 2 & 1, m >> 1 & 1, m & 1) for m in range(1, 8))
ANY = pl.BlockSpec(memory_space=pl.ANY)
VMEM_SPEC = pl.BlockSpec(memory_space=pltpu.VMEM)
N_BIG = 5


def _me():
    return lax.axis_index("x"), lax.axis_index("y"), lax.axis_index("c")


def _flip(pos, mask):
    return tuple(1 - p if m else p for p, m in zip(pos, mask))


def _remote(src, dst, send_sems, recv_sems, k, to):
    return pltpu.make_async_remote_copy(src_ref=src, dst_ref=dst, send_sem=send_sems.at[k], recv_sem=recv_sems.at[k],
                                        device_id=to, device_id_type=MESH)


def _ada_exchange(c8, w_ada, b_ada, conv8):
    def body(c_ref, w_ref, b_ref, cv_ref, mod_ref, cact_ref, conv_ref, c_all, part_all, send_sems, recv_sems):
        x, y, c = me = _me()
        dev = 4 * x + 2 * y + c
        chip = 2 * x + y
        c_all[dev] = c_ref[...]
        conv_ref[chip] = cv_ref[...]
        first = [_remote(c_ref, c_all.at[dev], send_sems, recv_sems, k, _flip(me, mask))
                 for k, mask in enumerate(ALL_PEERS)]
        first += [_remote(cv_ref, conv_ref.at[chip], send_sems, recv_sems, 7 + j, _flip(me, (*mask, 0)))
                  for j, mask in enumerate(OTHER_CHIPS)]
        for cp in first:
            cp.start()
        for cp in first:
            cp.wait()
        row = lax.broadcasted_iota(jnp.int32, (8, D_MODEL), 0)
        c_rows = jnp.zeros((8, D_MODEL), F32)
        for d in range(8):
            c_rows = jnp.where(row == d, c_all[d], c_rows)
        c_act = _silu(c_rows)
        cact_ref[...] = c_act
        part_all[chip] = _nn(c_act, w_ref[...], HIGHEST)
        second = [_remote(part_all.at[chip], part_all.at[chip], send_sems, recv_sems, 10 + j, _flip(me, (*mask, 0)))
                  for j, mask in enumerate(OTHER_CHIPS)]
        for cp in second:
            cp.start()
        for cp in second:
            cp.wait()
        cols = w_ref.shape[1]
        for k in range(4):
            mod_ref[:, k * cols:(k + 1) * cols] = part_all[k] + b_ref[:, k * cols:(k + 1) * cols]

    cols = w_ada.shape[1]
    return pl.pallas_call(
        body,
        in_specs=[VMEM_SPEC] * 4,
        out_specs=[VMEM_SPEC] * 3,
        out_shape=[jax.ShapeDtypeStruct((8, 4 * cols), F32), jax.ShapeDtypeStruct((8, D_MODEL), F32),
                   jax.ShapeDtypeStruct((4, 8, conv8.shape[1]), F32)],
        scratch_shapes=[pltpu.VMEM((8, 8, D_MODEL), F32), pltpu.VMEM((4, 8, cols), F32),
                        pltpu.SemaphoreType.DMA((13,)), pltpu.SemaphoreType.DMA((13,))],
        compiler_params=pltpu.CompilerParams(vmem_limit_bytes=VMEM_LIMIT),
        name="ada_exchange",
    )(c8, w_ada, b_ada, conv8)


def _gather_weights(shards):
    def body(*refs):
        srcs, dsts = refs[:N_BIG], refs[N_BIG:2 * N_BIG]
        send_sems, recv_sems = refs[2 * N_BIG:]
        x, y, c = me = _me()
        chip = 2 * x + y
        sibling = _flip(me, (0, 0, 1))
        first, passed = [], []
        for a in range(N_BIG):
            for j, mask in enumerate(OTHER_CHIPS):
                to = _flip(me, (*mask, 0))
                first.append(_remote(srcs[a].at[c], dsts[a].at[chip, c], send_sems, recv_sems, 6 * a + j, to))
                landed = dsts[a].at[2 * to[0] + to[1], c]
                passed.append(_remote(landed, landed, send_sems, recv_sems, 6 * a + 3 + j, sibling))
        for cp in first:
            cp.start()
        for cp, fwd in zip(first, passed):
            cp.wait_recv()
            fwd.start()
        for cp in first:
            cp.wait_send()
        for fwd in passed:
            fwd.wait()

    return pl.pallas_call(
        body,
        in_specs=[ANY] * N_BIG,
        out_specs=[ANY] * N_BIG,
        out_shape=[jax.ShapeDtypeStruct((4, *s.shape), s.dtype) for s in shards],
        scratch_shapes=[pltpu.SemaphoreType.DMA((6 * N_BIG,)), pltpu.SemaphoreType.DMA((6 * N_BIG,))],
        name="gather_weights",
    )(*shards)


def _start_and_wait(copies):
    for cp in copies:
        cp.start()
    for cp in copies:
        cp.wait()


def _swap_halves(grads):
    def body(*refs):
        srcs, got = refs[:N_BIG], refs[N_BIG:2 * N_BIG]
        send_sems, recv_sems = refs[2 * N_BIG:]
        x, y, c = me = _me()
        _start_and_wait([_remote(srcs[a].at[:, 1 - c], got[a], send_sems, recv_sems, a, _flip(me, (0, 0, 1)))
                         for a in range(N_BIG)])

    return pl.pallas_call(
        body,
        in_specs=[ANY] * N_BIG,
        out_specs=[ANY] * N_BIG,
        out_shape=[jax.ShapeDtypeStruct((4, g.shape[2], g.shape[3]), g.dtype) for g in grads],
        scratch_shapes=[pltpu.SemaphoreType.DMA((N_BIG,)), pltpu.SemaphoreType.DMA((N_BIG,))],
        name="swap_halves",
    )(*grads)


def _scatter_partials(partials):
    def body(*refs):
        srcs, dsts = refs[:N_BIG], refs[N_BIG:2 * N_BIG]
        send_sems, recv_sems = refs[2 * N_BIG:]
        x, y, c = me = _me()
        chip = 2 * x + y
        copies = []
        for a in range(N_BIG):
            for j, mask in enumerate(OTHER_CHIPS):
                to = _flip(me, (*mask, 0))
                copies.append(_remote(srcs[a].at[2 * to[0] + to[1]], dsts[a].at[chip], send_sems, recv_sems, 3 * a + j, to))
        _start_and_wait(copies)

    return pl.pallas_call(
        body,
        in_specs=[ANY] * N_BIG,
        out_specs=[ANY] * N_BIG,
        out_shape=[jax.ShapeDtypeStruct(p.shape, p.dtype) for p in partials],
        scratch_shapes=[pltpu.SemaphoreType.DMA((3 * N_BIG,)), pltpu.SemaphoreType.DMA((3 * N_BIG,))],
        name="scatter_partials",
    )(*partials)


def _join_halves(halves):
    def body(*refs):
        srcs, dsts = refs[:N_BIG], refs[N_BIG:2 * N_BIG]
        send_sems, recv_sems = refs[2 * N_BIG:]
        x, y, c = me = _me()
        _start_and_wait([_remote(srcs[a], dsts[a].at[c], send_sems, recv_sems, a, _flip(me, (0, 0, 1)))
                         for a in range(N_BIG)])

    return pl.pallas_call(
        body,
        in_specs=[ANY] * N_BIG,
        out_specs=[ANY] * N_BIG,
        out_shape=[jax.ShapeDtypeStruct((2, *h.shape), h.dtype) for h in halves],
        scratch_shapes=[pltpu.SemaphoreType.DMA((N_BIG,)), pltpu.SemaphoreType.DMA((N_BIG,))],
        name="join_halves",
    )(*halves)


def _gather_small(packed):
    n_rows = packed.shape[0]

    def body(p_ref, all_ref, sum_ref, send_sems, recv_sems):
        x, y, c = me = _me()
        dev = 4 * x + 2 * y + c
        all_ref[dev] = p_ref[...]
        copies = [_remote(p_ref, all_ref.at[dev], send_sems, recv_sems, k, _flip(me, mask))
                  for k, mask in enumerate(ALL_PEERS)]
        for cp in copies:
            cp.start()
        for cp in copies:
            cp.wait()
        total = all_ref[0]
        for d in range(1, 8):
            total = total + all_ref[d]
        sum_ref[...] = total

    return pl.pallas_call(
        body,
        in_specs=[VMEM_SPEC],
        out_specs=[VMEM_SPEC, VMEM_SPEC],
        out_shape=[jax.ShapeDtypeStruct((8, n_rows, LANES), F32), jax.ShapeDtypeStruct((n_rows, LANES), F32)],
        scratch_shapes=[pltpu.SemaphoreType.DMA((7,)), pltpu.SemaphoreType.DMA((7,))],
        name="gather_small",
    )(packed)


def _add_pair(a, b, out_dtype, name):
    def body(a_ref, b_ref, o_ref):
        o_ref[...] = (a_ref[...] + b_ref[...]).astype(o_ref.dtype)

    blk = pl.BlockSpec((1, *a.shape[1:]), lambda i: (i, 0, 0))
    return pl.pallas_call(
        body, grid=(a.shape[0],), in_specs=[blk, blk], out_specs=blk,
        out_shape=jax.ShapeDtypeStruct(a.shape, out_dtype),
        compiler_params=_params(("arbitrary",)), name=name,
    )(a, b)


def _add_slots(a, name):
    def body(a_ref, o_ref):
        total = a_ref[0].astype(F32)
        for k in range(1, 4):
            total = total + a_ref[k].astype(F32)
        o_ref[...] = total

    return pl.pallas_call(
        body, in_specs=[VMEM_SPEC], out_specs=VMEM_SPEC,
        out_shape=jax.ShapeDtypeStruct(a.shape[1:], F32),
        compiler_params=pltpu.CompilerParams(vmem_limit_bytes=VMEM_LIMIT), name=name,
    )(a)


def _ada_weight_grad(c_act, dmod_cols):
    def body(c_ref, d_ref, o_ref):
        o_ref[...] = _tn(c_ref[...], d_ref[...], HIGHEST)

    return pl.pallas_call(
        body, in_specs=[VMEM_SPEC, VMEM_SPEC], out_specs=VMEM_SPEC,
        out_shape=jax.ShapeDtypeStruct((c_act.shape[1], dmod_cols.shape[1]), F32),
        compiler_params=pltpu.CompilerParams(vmem_limit_bytes=VMEM_LIMIT), name="ada_weight_grad",
    )(c_act, dmod_cols)


def kernel(x, c, w_ada, b_ada, norm_attn_g, w_in, rel_bias, conv_w, a_log, dt_bias, delta_norm_g, w_out, norm_ffn_g, w_gate, w_up, w_down, final_norm_g, loss_target, m_w_ada, m_b_ada, m_norm_attn_g, m_w_in, m_rel_bias, m_conv_w, m_a_log, m_dt_bias, m_delta_norm_g, m_w_out, m_norm_ffn_g, m_w_gate, m_w_up, m_w_down, m_final_norm_g, v_w_ada, v_b_ada, v_norm_attn_g, v_w_in, v_rel_bias, v_conv_w, v_a_log, v_dt_bias, v_delta_norm_g, v_w_out, v_norm_ffn_g, v_w_gate, v_w_up, v_w_down, v_final_norm_g):
    xi, yi, ci = _me()
    dev = 4 * xi + 2 * yi + ci
    chip = 2 * xi + yi

    conv_cols = conv_w.shape[2]
    mod_all, c_act, conv_all = _ada_exchange(jnp.broadcast_to(c, (8, D_MODEL)), w_ada[0], b_ada,
                                             jnp.pad(conv_w[0], ((0, 4), (0, 0))))
    mod = lax.dynamic_slice_in_dim(mod_all, dev, 1, axis=0)
    conv_full = jnp.swapaxes(conv_all[:, :4, :], 0, 1).reshape(4, 4 * conv_cols)

    big_names = ("w_in", "w_out", "w_gate", "w_up", "w_down")
    by_cols = (True, False, True, True, False)

    def rows_form(a, cols):
        return jnp.swapaxes(a[0], 0, 1) if cols else a[0]

    big = [rows_form(w, cols) for w, cols in zip((w_in, w_out, w_gate, w_up, w_down), by_cols)]
    shards = [w.astype(BF16).reshape(2, w.size // (2 * LANES), LANES) for w in big]
    gathered = [lax.dynamic_update_index_in_dim(g, s, chip, 0) for g, s in zip(_gather_weights(shards), shards)]
    gathered = [g.reshape(4 * w.shape[0], w.shape[1]) for g, w in zip(gathered, big)]
    whole = [g.T if cols else g for g, cols in zip(gathered, by_cols)]

    loss, grad_x, grads, dmod = _local_step(
        x[0], loss_target[0], mod, norm_attn_g, whole[0], rel_bias, conv_full, a_log, dt_bias, delta_norm_g,
        whole[1], norm_ffn_g, whole[2], whole[3], whole[4], final_norm_g[None])

    slots = []
    for name, w in zip(big_names, big):
        slots.append(grads[name].reshape(4, 2, w.size // (2 * LANES), LANES))
    partials = [_add_pair(lax.dynamic_index_in_dim(s, ci, 1, keepdims=False), got, BF16, f"add_pair_{a}")
                for a, (s, got) in enumerate(zip(slots, _swap_halves(slots)))]
    by_source = [lax.dynamic_update_index_in_dim(b, lax.dynamic_index_in_dim(p, chip, 0, keepdims=False), chip, 0)
                 for b, p in zip(_scatter_partials(partials), partials)]
    halves = [_add_slots(p, f"add_slots_{a}") for a, p in enumerate(by_source)]
    joined = [lax.dynamic_update_index_in_dim(j, h, ci, 0) for j, h in zip(_join_halves(halves), halves)]
    big_grads = [j.reshape(w.shape) for j, w in zip(joined, big)]

    pieces = [dmod, grads["conv_w"], grads["norm_attn_g"], grads["norm_ffn_g"], grads["final_norm_g"],
              grads["rel_bias"], grads["a_log"], grads["dt_bias"], grads["delta_norm_g"]]
    flat = [jnp.pad(p.reshape(-1), (0, -p.size % LANES)) for p in pieces]
    n_rows = [f.size // LANES for f in flat]
    packed = jnp.concatenate(flat).reshape(-1, LANES)
    packed = jnp.pad(packed, ((0, -packed.shape[0] % 8), (0, 0)))
    all_small, total = _gather_small(packed)
    sums, start = [], 0
    for p, n in zip(pieces, n_rows):
        sums.append(total[start:start + n].reshape(-1)[:p.size].reshape(p.shape))
        start += n
    g_b_ada, g_conv, g_norm_attn, g_norm_ffn, g_final, g_rel, g_alog, g_dt, g_dnorm = sums
    dmod_all = all_small[:, :n_rows[0], :].reshape(8, -1)
    ada_cols = w_ada.shape[2]
    g_w_ada = _ada_weight_grad(c_act, lax.dynamic_slice_in_dim(dmod_all, chip * ada_cols, ada_cols, axis=1))
    g_conv = lax.dynamic_slice_in_dim(g_conv, chip * conv_cols, conv_cols, axis=1)

    grad = {"w_ada": g_w_ada[None], "b_ada": g_b_ada, "norm_attn_g": g_norm_attn,
            "rel_bias": g_rel, "conv_w": g_conv[None], "a_log": g_alog, "dt_bias": g_dt, "delta_norm_g": g_dnorm,
            "norm_ffn_g": g_norm_ffn, "final_norm_g": g_final.reshape(-1)}
    weight = {"w_ada": w_ada, "b_ada": b_ada, "norm_attn_g": norm_attn_g, "w_in": w_in, "rel_bias": rel_bias,
              "conv_w": conv_w, "a_log": a_log, "dt_bias": dt_bias, "delta_norm_g": delta_norm_g, "w_out": w_out,
              "norm_ffn_g": norm_ffn_g, "w_gate": w_gate, "w_up": w_up, "w_down": w_down, "final_norm_g": final_norm_g}
    first = {"w_ada": m_w_ada, "b_ada": m_b_ada, "norm_attn_g": m_norm_attn_g, "w_in": m_w_in, "rel_bias": m_rel_bias,
             "conv_w": m_conv_w, "a_log": m_a_log, "dt_bias": m_dt_bias, "delta_norm_g": m_delta_norm_g,
             "w_out": m_w_out, "norm_ffn_g": m_norm_ffn_g, "w_gate": m_w_gate, "w_up": m_w_up, "w_down": m_w_down,
             "final_norm_g": m_final_norm_g}
    second = {"w_ada": v_w_ada, "b_ada": v_b_ada, "norm_attn_g": v_norm_attn_g, "w_in": v_w_in, "rel_bias": v_rel_bias,
              "conv_w": v_conv_w, "a_log": v_a_log, "dt_bias": v_dt_bias, "delta_norm_g": v_delta_norm_g,
              "w_out": v_w_out, "norm_ffn_g": v_norm_ffn_g, "w_gate": v_w_gate, "w_up": v_w_up, "w_down": v_w_down,
              "final_norm_g": v_final_norm_g}
    delta, new_m, new_v = {}, {}, {}
    for name, w in weight.items():
        if name in big_names:
            continue
        two_d = (-1, w.shape[-1])
        d, nm, nv = _adamw(w.reshape(two_d), grad[name].reshape(two_d), first[name].reshape(two_d),
                           second[name].reshape(two_d), f"adamw_{name}")
        delta[name], new_m[name], new_v[name] = d.reshape(w.shape), nm.reshape(w.shape), nv.reshape(w.shape)
    for name, w, g, cols in zip(big_names, big, big_grads, by_cols):
        outs = _adamw(w, g, rows_form(first[name], cols), rows_form(second[name], cols), f"adamw_{name}")
        grad[name], delta[name], new_m[name], new_v[name] = [
            (jnp.swapaxes(o, 0, 1) if cols else o)[None] for o in (g, *outs)]

    names = list(weight)
    return (lax.psum(loss, ("x", "y", "c")), grad_x[None], *[grad[n] for n in names], *[delta[n] for n in names],
            *[new_m[n] for n in names], *[new_v[n] for n in names])
```

```python
import functools
import math

import numpy as np
import jax
import jax.numpy as jnp
from jax import lax
from jax.experimental import pallas as pl
from jax.experimental.pallas import tpu as pltpu

F32 = jnp.float32
BF16 = jnp.bfloat16
HIGHEST = lax.Precision.HIGHEST

D_MODEL = 1024
HEAD_DIM = 64
N_HEADS = 8
HEAD_W = 512
BRANCHES = ((128, 1), (512, 4), (2048, 16))
BAND = 128
ATT_TILE = 2048
ATT_UNROLL = 4
N_BUCKETS = 32
MAX_DISTANCE = 2048
CHUNK = 64
D_FF = 2816
EPS = 1e-6
NEG_INF = -1e30
LANES = 128
VMEM_LIMIT = 56 * 1024 * 1024

ADAM_LR = 0.001
ADAM_B1 = 0.9
ADAM_B2 = 0.999
ADAM_EPS = 1e-08
ADAM_WD = 0.01
ADAM_STEP = 10


def _nn(a, b, precision=None):
    return jnp.dot(a, b, preferred_element_type=F32, precision=precision)


def _nt(a, b, precision=None):
    return lax.dot_general(a, b, (((1,), (1,)), ((), ())), preferred_element_type=F32, precision=precision)


def _tn(a, b, precision=None):
    return lax.dot_general(a, b, (((0,), (0,)), ((), ())), preferred_element_type=F32, precision=precision)


def _params(sem, vmem=VMEM_LIMIT):
    return pltpu.CompilerParams(dimension_semantics=sem, vmem_limit_bytes=vmem)


def _sigmoid(x):
    return 1.0 / (1.0 + jnp.exp(-x))


def _silu(x):
    return x * _sigmoid(x)


def _dsilu(x):
    s = _sigmoid(x)
    return s * (1.0 + x * (1.0 - s))


def _attn_tables():
    qi = np.arange(BAND)[:, None]
    kj = np.arange(2 * BAND)[None, :]
    steps = qi + BAND - kj
    in_window = (steps >= 0) & (steps <= BAND)
    max_exact = N_BUCKETS // 2
    out = np.zeros((3, 2, BAND, 2 * BAND), np.int32)
    for b, (_, dil) in enumerate(BRANCHES):
        dist = np.maximum(steps, 0) * dil
        dist_f = np.maximum(dist, 1).astype(np.float32)
        large = max_exact + (np.log(dist_f / np.float32(max_exact)) / np.float32(math.log(MAX_DISTANCE / max_exact))
                             * np.float32(N_BUCKETS - max_exact)).astype(np.int32)
        bucket = np.where(dist < max_exact, dist, np.minimum(large, N_BUCKETS - 1)).astype(np.int32)
        out[b, 0] = np.where(in_window, bucket, -1)
        out[b, 1] = np.where(in_window & (kj >= BAND), bucket, -1)
    return out


def _attention_bias(rel_bias, tables):
    def body(rel_ref, tab_ref, out_ref):
        head = pl.program_id(0)
        for b in range(3):
            tab = tab_ref[b, 0]

            def pick(kk, acc, tab=tab):
                return jnp.where(tab == kk, rel_ref[kk, head], acc)

            acc = lax.fori_loop(0, N_BUCKETS, pick, jnp.zeros((BAND, 2 * BAND), F32))
            for first in range(2):
                out_ref[0, b, first] = jnp.where(tab_ref[b, first] < 0, NEG_INF, acc)

    return pl.pallas_call(
        body,
        grid=(N_HEADS,),
        in_specs=[pl.BlockSpec(memory_space=pltpu.SMEM),
                  pl.BlockSpec((3, 2, BAND, 2 * BAND), lambda h: (0, 0, 0, 0))],
        out_specs=pl.BlockSpec((1, 3, 2, BAND, 2 * BAND), lambda h: (h, 0, 0, 0, 0)),
        out_shape=jax.ShapeDtypeStruct((N_HEADS, 3, 2, BAND, 2 * BAND), F32),
        compiler_params=_params(("arbitrary",)),
        name="attn_bias",
    )(rel_bias, tables)


def _bias_spec():
    return pl.BlockSpec((2, 3, 2, BAND, 2 * BAND), lambda p, t: (p, 0, 0, 0, 0))


def _attn_block_index(idx, t, r):
    nb = ATT_TILE // (BAND * r)
    rho = idx // nb
    n = idx % nb
    qs = rho + r * BAND * n
    gs = t * ATT_TILE + qs
    first = (t * nb + n) == 0
    ps = jnp.where(first, gs, gs - r * BAND)
    return qs, gs, ps, first.astype(jnp.int32)


def _rows(start, r):
    return pl.ds(start, BAND) if r == 1 else pl.ds(start, BAND, stride=r)


def _attention_fwd(qkv, bias, shards):
    seq = qkv.shape[0]
    n_tiles = seq // ATT_TILE
    n = len(shards)

    def body(*refs):
        bias_ref, q_ref, k_ref, v_ref = refs[:4]
        y_ref, lse_ref = refs[4 + n:6 + n]
        o_s, l_s = refs[6 + 2 * n:8 + 2 * n]
        riding = (refs[4:4 + n], refs[6 + n:6 + 2 * n], *refs[8 + 2 * n:])
        pair = pl.program_id(0)
        t = pl.program_id(1)
        if n:
            @pl.when((pair == 0) & (t == 0))
            def _():
                for cp in _gather_copies(*riding)[0]:
                    cp.start()

            @pl.when((pair == 2) & (t == 0))
            def _():
                for cp, fwd in zip(*_gather_copies(*riding)):
                    cp.wait_recv()
                    fwd.start()

        lane = lax.broadcasted_iota(jnp.int32, (1, LANES), 1)
        head0 = lane < HEAD_DIM
        masks = (head0, jnp.logical_not(head0))
        ones = jnp.ones((2 * BAND, LANES), BF16)
        for b, (_, r) in enumerate(BRANCHES):
            def blocks(it, carry, b=b, r=r):
                idx = [_attn_block_index(it * ATT_UNROLL + j, t, r) for j in range(ATT_UNROLL)]
                qb = [q_ref[_rows(qs, r), :] * (HEAD_DIM ** -0.5) for qs, _, _, _ in idx]
                kcat = [jnp.concatenate([k_ref[_rows(ps, r), :], k_ref[_rows(gs, r), :]], axis=0).astype(BF16)
                        for _, gs, ps, _ in idx]
                vcat = [jnp.concatenate([v_ref[_rows(ps, r), :], v_ref[_rows(gs, r), :]], axis=0).astype(BF16)
                        for _, gs, ps, _ in idx]
                work = [(j, hh) for j in range(ATT_UNROLL) for hh in range(2)]
                s = [_nt(jnp.where(masks[hh], qb[j], 0.0).astype(BF16), kcat[j]) + bias_ref[hh, b, idx[j][3]]
                     for j, hh in work]
                m = [jnp.max(sv, axis=-1, keepdims=True) for sv in s]
                e = [jnp.exp(sv - mv) for sv, mv in zip(s, m)]
                eb = [ev.astype(BF16) for ev in e]
                den = [_nn(ev, ones) for ev in eb]
                out = [_nn(ev, vcat[j]) / dv for ev, dv, (j, _) in zip(eb, den, work)]
                lse = [mv + jnp.log(dv) for mv, dv in zip(m, den)]
                for j in range(ATT_UNROLL):
                    o_s[b, _rows(idx[j][0], r), :] = jnp.where(head0, out[2 * j], out[2 * j + 1])
                    l_s[b, _rows(idx[j][0], r), :] = jnp.where(head0, lse[2 * j], lse[2 * j + 1])
                return carry

            lax.fori_loop(0, ATT_TILE // BAND // ATT_UNROLL, blocks, 0)

        def merge(i, carry):
            rows = pl.ds(pl.multiple_of(i * BAND, BAND), BAND)
            l0, l1, l2 = l_s[0, rows, :], l_s[1, rows, :], l_s[2, rows, :]
            m = jnp.maximum(jnp.maximum(l0, l1), l2)
            w0, w1, w2 = jnp.exp(l0 - m), jnp.exp(l1 - m), jnp.exp(l2 - m)
            tot = w0 + w1 + w2
            y_ref[rows, :] = (w0 * o_s[0, rows, :] + w1 * o_s[1, rows, :] + w2 * o_s[2, rows, :]) / tot
            lse_ref[rows, :] = m + jnp.log(tot)
            return carry

        lax.fori_loop(0, ATT_TILE // BAND, merge, 0)

        if n:
            @pl.when((pair == N_HEADS // 2 - 1) & (t == n_tiles - 1))
            def _():
                first, passed = _gather_copies(*riding)
                for cp in first:
                    cp.wait_send()
                for fwd in passed:
                    fwd.wait()

    tile = pl.BlockSpec((ATT_TILE, LANES), lambda p, t: (t, p))
    sems = [pltpu.SemaphoreType.DMA((6 * n,)), pltpu.SemaphoreType.DMA((6 * n,))] if n else []
    return pl.pallas_call(
        body,
        grid=(N_HEADS // 2, n_tiles),
        in_specs=[
            _bias_spec(),
            pl.BlockSpec((ATT_TILE, LANES), lambda p, t: (t, p)),
            pl.BlockSpec((seq, LANES), lambda p, t: (0, 4 + p)),
            pl.BlockSpec((seq, LANES), lambda p, t: (0, 8 + p)),
        ] + [ANY] * n,
        out_specs=[tile, tile] + [ANY] * n,
        out_shape=[jax.ShapeDtypeStruct((seq, HEAD_W), F32), jax.ShapeDtypeStruct((seq, HEAD_W), F32)]
        + _gathered_shapes(shards),
        scratch_shapes=[
            pltpu.VMEM((3, ATT_TILE, LANES), F32),
            pltpu.VMEM((3, ATT_TILE, LANES), F32),
        ] + sems,
        compiler_params=_params(("arbitrary", "arbitrary")),
        name="attn_fwd",
    )(bias, qkv, qkv, qkv, *shards)


def _attention_bwd(qkv, dy, y, lse, bias, partials):
    seq = qkv.shape[0]
    n_tiles = seq // ATT_TILE
    n = len(partials)

    def body(*refs):
        bias_ref, q_ref, k_ref, v_ref, dy_ref, y_ref, lse_ref = refs[:7]
        dq_ref, dk_ref, dv_ref, dbias_ref = refs[7 + n:11 + n]
        riding = (refs[7:7 + n], refs[11 + n:11 + 2 * n], *refs[11 + 2 * n:])
        pair = pl.program_id(0)
        t = pl.program_id(1)
        if n:
            @pl.when((pair == 0) & (t == 0))
            def _():
                for cp in _scatter_copies(*riding):
                    cp.start()

        lane = lax.broadcasted_iota(jnp.int32, (1, LANES), 1)
        head0 = lane < HEAD_DIM

        @pl.when(t == 0)
        def _():
            dk_ref[...] = jnp.zeros_like(dk_ref)
            dv_ref[...] = jnp.zeros_like(dv_ref)
            dbias_ref[...] = jnp.zeros_like(dbias_ref)

        dq_ref[...] = jnp.zeros_like(dq_ref)

        masks = (head0, jnp.logical_not(head0))
        ones = jnp.ones((LANES, LANES), BF16)
        scale = HEAD_DIM ** -0.5
        for b, (_, r) in enumerate(BRANCHES):
            def blocks(it, carry, b=b, r=r):
                idx = [_attn_block_index(it * ATT_UNROLL + j, t, r) for j in range(ATT_UNROLL)]
                qb = [q_ref[_rows(qs, r), :] * scale for qs, _, _, _ in idx]
                kcat = [jnp.concatenate([k_ref[_rows(ps, r), :], k_ref[_rows(gs, r), :]], axis=0).astype(BF16)
                        for _, gs, ps, _ in idx]
                vcat = [jnp.concatenate([v_ref[_rows(ps, r), :], v_ref[_rows(gs, r), :]], axis=0).astype(BF16)
                        for _, gs, ps, _ in idx]
                dob = [dy_ref[_rows(qs, r), :] for qs, _, _, _ in idx]
                ob = [y_ref[_rows(qs, r), :] for qs, _, _, _ in idx]
                lb = [lse_ref[_rows(qs, r), :] for qs, _, _, _ in idx]
                work = [(j, hh) for j in range(ATT_UNROLL) for hh in range(2)]
                qh = [jnp.where(masks[hh], qb[j], 0.0).astype(BF16) for j, hh in work]
                doh = [jnp.where(masks[hh], dob[j], 0.0) for j, hh in work]
                dohb = [d.astype(BF16) for d in doh]
                s = [_nt(qh[w], kcat[j]) + bias_ref[hh, b, idx[j][3]] for w, (j, hh) in enumerate(work)]
                dp = [_nt(dohb[w], vcat[j]) for w, (j, _) in enumerate(work)]
                lrot = [pltpu.roll(lv, HEAD_DIM, 1) for lv in lb]
                lcol = [jnp.where(masks[hh], lb[j], lrot[j]) for j, hh in work]
                parts = [_split(doh[w] * ob[j]) for w, (j, _) in enumerate(work)]
                delta = [_nn(hi, ones) + _nn(lo, ones) for hi, lo in parts]
                prob = [jnp.exp(sv - jnp.concatenate([lv, lv], axis=1)) for sv, lv in zip(s, lcol)]
                ds = [pv * (dv - jnp.concatenate([de, de], axis=1)) for pv, dv, de in zip(prob, dp, delta)]
                dsb = [d.astype(BF16) for d in ds]
                dq = [_nn(dsb[w], kcat[j]) for w, (j, _) in enumerate(work)]
                dkc = [_tn(dsb[w], qh[w]) for w in range(len(work))]
                dvc = [_tn(prob[w].astype(BF16), dohb[w]) for w in range(len(work))]
                for w, (j, hh) in enumerate(work):
                    dbias_ref[0, b, hh] += ds[w]
                for j in range(ATT_UNROLL):
                    qs, gs, ps, _ = idx[j]
                    dkcat = dkc[2 * j] + dkc[2 * j + 1]
                    dvcat = dvc[2 * j] + dvc[2 * j + 1]
                    dq_ref[_rows(qs, r), :] += jnp.where(head0, dq[2 * j], dq[2 * j + 1]) * scale
                    dk_ref[_rows(ps, r), :] += dkcat[:BAND]
                    dk_ref[_rows(gs, r), :] += dkcat[BAND:]
                    dv_ref[_rows(ps, r), :] += dvcat[:BAND]
                    dv_ref[_rows(gs, r), :] += dvcat[BAND:]
                return carry

            lax.fori_loop(0, ATT_TILE // BAND // ATT_UNROLL, blocks, 0)

        if n:
            @pl.when((pair == N_HEADS // 2 - 1) & (t == n_tiles - 1))
            def _():
                for cp in _scatter_copies(*riding):
                    cp.wait()

    tile = pl.BlockSpec((ATT_TILE, LANES), lambda p, t: (t, p))
    full = pl.BlockSpec((seq, LANES), lambda p, t: (0, p))
    sems = [pltpu.SemaphoreType.DMA((3 * n,)), pltpu.SemaphoreType.DMA((3 * n,))] if n else []
    return pl.pallas_call(
        body,
        grid=(N_HEADS // 2, n_tiles),
        in_specs=[
            _bias_spec(),
            pl.BlockSpec((ATT_TILE, LANES), lambda p, t: (t, p)),
            pl.BlockSpec((seq, LANES), lambda p, t: (0, 4 + p)),
            pl.BlockSpec((seq, LANES), lambda p, t: (0, 8 + p)),
            tile, tile, tile,
        ] + [ANY] * n,
        out_specs=[tile, full, full,
                   pl.BlockSpec((1, 3, 2, BAND, 2 * BAND), lambda p, t: (p, 0, 0, 0, 0))] + [ANY] * n,
        out_shape=[jax.ShapeDtypeStruct((seq, HEAD_W), F32)] * 3
        + [jax.ShapeDtypeStruct((N_HEADS // 2, 3, 2, BAND, 2 * BAND), F32)]
        + [jax.ShapeDtypeStruct(p.shape, p.dtype) for p in partials],
        scratch_shapes=sems,
        compiler_params=_params(("arbitrary", "arbitrary")),
        name="attn_bwd",
    )(bias, qkv, qkv, qkv, dy, y, lse, *partials)


def _rel_bias_grad(dbias, tables):
    def body(tab_ref, db_ref, out_ref):
        lane = lax.broadcasted_iota(jnp.int32, (1, LANES), 1)
        out_ref[...] = jnp.zeros_like(out_ref)
        for b in range(3):
            tab = tab_ref[b, 0]

            def head(h, carry, b=b, tab=tab):
                d = db_ref[h // 2, b, h % 2]
                sums = [jnp.sum(jnp.where(tab == kk, d, 0.0), keepdims=True) for kk in range(N_BUCKETS)]
                row = jnp.zeros((1, LANES), F32)
                for kk, s in enumerate(sums):
                    row = row + jnp.where(lane == kk, s, 0.0)
                out_ref[pl.ds(h, 1), :] += row
                return carry

            lax.fori_loop(0, N_HEADS, head, 0)

    return pl.pallas_call(
        body,
        out_shape=jax.ShapeDtypeStruct((N_HEADS, LANES), F32),
        compiler_params=pltpu.CompilerParams(vmem_limit_bytes=VMEM_LIMIT),
        name="rel_bias_grad",
    )(tables, dbias)


ROW_TILE = 512


def _head_sum_matrix():
    return (lax.broadcasted_iota(jnp.int32, (HEAD_W, LANES), 0) // HEAD_DIM
            == lax.broadcasted_iota(jnp.int32, (HEAD_W, LANES), 1)).astype(F32)


def _head_spread_matrix(offset=0):
    return (lax.broadcasted_iota(jnp.int32, (LANES, HEAD_W), 0)
            == lax.broadcasted_iota(jnp.int32, (LANES, HEAD_W), 1) // HEAD_DIM + offset).astype(F32)


def _head_gather_matrix(offset=0):
    return (lax.broadcasted_iota(jnp.int32, (HEAD_W, LANES), 0) // HEAD_DIM + offset
            == lax.broadcasted_iota(jnp.int32, (HEAD_W, LANES), 1)).astype(F32)


def _split3(x):
    hi = x.astype(BF16)
    rest = x - hi.astype(F32)
    mid = rest.astype(BF16)
    return hi, mid, (rest - mid.astype(F32)).astype(BF16)


def _pick(x, onehot):
    m = onehot.astype(BF16)
    hi, mid, lo = _split3(x)
    return _nn(hi, m) + (_nn(mid, m) + _nn(lo, m))


def _pick_left(onehot, x):
    m = onehot.astype(BF16)
    hi, mid, lo = _split3(x)
    return _nn(m, hi) + (_nn(m, mid) + _nn(m, lo))


def _tri(lower, strict=False):
    r = lax.broadcasted_iota(jnp.int32, (CHUNK, CHUNK), 0)
    c = lax.broadcasted_iota(jnp.int32, (CHUNK, CHUNK), 1)
    if lower:
        return (c < r) if strict else (c <= r)
    return c >= r


def _softplus(z):
    return jnp.maximum(z, 0.0) + jnp.log(1.0 + jnp.exp(-jnp.abs(z)))


def _conv_taps(stage, w_ref, rows):
    return (w_ref[3:4, :] * stage[8:8 + rows, :] + w_ref[2:3, :] * stage[7:7 + rows, :]
            + w_ref[1:2, :] * stage[6:6 + rows, :] + w_ref[0:1, :] * stage[5:5 + rows, :])


def _l2_scale(xc, hsum, hspread):
    ssq = _pick(xc * xc, hsum)
    return _pick(lax.rsqrt(ssq + EPS), hspread)


def _stage_rows(stage, x_ref, xp_ref, i):
    stage[0:8, :] = jnp.where(i == 0, 0.0, xp_ref[...])
    stage[8:8 + ROW_TILE, :] = x_ref[...]


def _delta_prep_fwd(qkvz, ba, conv_w, alog_row, dt_row):
    seq = qkvz.shape[0]
    qkv_w = 3 * HEAD_W

    def body(x_ref, xp_ref, ba_ref, w_ref, al_ref, dt_ref, out_ref, stage):
        i = pl.program_id(0)
        _stage_rows(stage, x_ref, xp_ref, i)
        act = _silu(_conv_taps(stage, w_ref, ROW_TILE))
        hsum, hspread = _head_sum_matrix(), _head_spread_matrix()
        qc, kc = act[:, :HEAD_W], act[:, HEAD_W:2 * HEAD_W]
        out_ref[0] = qc * _l2_scale(qc, hsum, hspread) * (HEAD_DIM ** -0.5)
        out_ref[1] = kc * _l2_scale(kc, hsum, hspread)
        out_ref[2] = act[:, 2 * HEAD_W:]
        bav = ba_ref[...]
        out_ref[3] = _pick(_sigmoid(bav), hspread)
        g8 = -jnp.exp(al_ref[...]) * _softplus(bav + dt_ref[...])
        gb = _pick(g8, _head_spread_matrix(N_HEADS))
        cum = _tri(True).astype(F32)
        for ch in range(ROW_TILE // CHUNK):
            rows = slice(ch * CHUNK, (ch + 1) * CHUNK)
            out_ref[4, rows, :] = _pick_left(cum, gb[rows])

    return pl.pallas_call(
        body,
        grid=(seq // ROW_TILE,),
        in_specs=[
            pl.BlockSpec((ROW_TILE, qkv_w), lambda i: (i, 0)),
            pl.BlockSpec((8, qkv_w), lambda i: (jnp.maximum(i * (ROW_TILE // 8) - 1, 0), 0)),
            pl.BlockSpec((ROW_TILE, LANES), lambda i: (i, 0)),
            pl.BlockSpec((4, qkv_w), lambda i: (0, 0)),
            pl.BlockSpec((1, LANES), lambda i: (0, 0)),
            pl.BlockSpec((1, LANES), lambda i: (0, 0)),
        ],
        out_specs=pl.BlockSpec((5, ROW_TILE, HEAD_W), lambda i: (0, i, 0)),
        out_shape=jax.ShapeDtypeStruct((5, seq, HEAD_W), F32),
        scratch_shapes=[pltpu.VMEM((ROW_TILE + 8, qkv_w), F32)],
        compiler_params=_params(("arbitrary",)),
        name="delta_prep_fwd",
    )(qkvz, qkvz, ba, conv_w, alog_row, dt_row)


def _split(x):
    hi = x.astype(BF16)
    return hi, (x - hi.astype(F32)).astype(BF16)


def _dot3(a, b, dot=_nn):
    return dot(a[0], b[0]) + (dot(a[0], b[1]) + dot(a[1], b[0]))


def _unit_lower_inverses(mats):
    eye = (lax.broadcasted_iota(jnp.int32, (CHUNK, CHUNK), 0)
           == lax.broadcasted_iota(jnp.int32, (CHUNK, CHUNK), 1)).astype(F32)
    invs = [eye - a for a in mats]
    powers = [_split(a) for a in mats]
    for step in range(5):
        squares = [_dot3(p, p) for p in powers]
        powers = [_split(s) for s in squares]
        invs = [inv + _dot3(_split(inv), p) for inv, p in zip(invs, powers)]
    return invs


def _chunk_terms(q, k, v, beta, gc):
    causal, strict = _tri(True), _tri(True, strict=True)
    e = jnp.exp(gc)
    g_last = jnp.broadcast_to(gc[CHUNK - 1:CHUNK, :], (CHUNK, CHUNK))
    f = jnp.exp(g_last - gc)
    e_last = jnp.exp(g_last)
    decay = jnp.where(causal, jnp.exp(jnp.where(causal, gc - gc.T, 0.0)), 0.0)
    kb = k * beta
    a_mat = jnp.where(strict, _nt(kb.astype(BF16), k.astype(BF16)) * decay, 0.0)
    qk = jnp.where(causal, _nt(q.astype(BF16), k.astype(BF16)) * decay, 0.0)
    return e, f, e_last, decay, kb, a_mat, qk


GROUP = 8
UNROLL = 8


def _chunk_rows(ci):
    return pl.ds(pl.multiple_of(ci * CHUNK, CHUNK), CHUNK)


def _pair_specs(n_planes):
    return pl.BlockSpec((n_planes, GROUP * CHUNK, LANES), lambda p, g: (0, g, p))


def _delta_chunk_fwd(xs):
    seq = xs.shape[1]
    rows_per_step = GROUP * CHUNK

    def body(x_ref, inv_ref, qk_ref, u_ref, w_ref):
        for hh in range(2):
            lanes = slice(hh * HEAD_DIM, (hh + 1) * HEAD_DIM)
            rows = [slice(step * CHUNK, (step + 1) * CHUNK) for step in range(GROUP)]
            xh = [[x_ref[j, r, lanes] for j in range(5)] for r in rows]
            terms = [_chunk_terms(*x) for x in xh]
            invs = _unit_lower_inverses([t[5] for t in terms])
            for r, x, t, inv in zip(rows, xh, terms, invs):
                e, kb, qk = t[0], t[4], t[6]
                inv_parts = _split(inv)
                inv_ref[hh, r, :] = inv
                qk_ref[hh, r, :] = qk
                u_ref[hh, r, :] = _dot3(inv_parts, _split(x[2] * x[3]))
                w_ref[hh, r, :] = _dot3(inv_parts, _split(kb * e))

    out = pl.BlockSpec((2, rows_per_step, HEAD_DIM), lambda p, g: (p, g, 0))
    return pl.pallas_call(
        body,
        grid=(N_HEADS // 2, seq // rows_per_step),
        in_specs=[_pair_specs(5)],
        out_specs=[out] * 4,
        out_shape=[jax.ShapeDtypeStruct((N_HEADS, seq, HEAD_DIM), F32)] * 4,
        compiler_params=_params(("parallel", "parallel")),
        name="delta_chunk_fwd",
    )(xs)


def _decays(gc):
    g_last = jnp.broadcast_to(gc[CHUNK - 1:CHUNK, :], (CHUNK, CHUNK))
    return jnp.exp(gc), jnp.exp(g_last - gc), jnp.exp(g_last)


def _token_blocks(index, n_steps=None):
    rows_per_step = GROUP * CHUNK
    if n_steps is None:
        return pl.BlockSpec((1, rows_per_step, HEAD_W), lambda g: (index, g, 0))
    return pl.BlockSpec((1, rows_per_step, HEAD_W), lambda g: (index, n_steps - 1 - g, 0))


def _head_lanes(h):
    return pl.ds(h * HEAD_DIM, HEAD_DIM)


def _delta_scan_fwd(xs, qk_h, u_h, w_h):
    seq = xs.shape[1]
    rows_per_step = GROUP * CHUNK

    def body(q_ref, k_ref, gc_ref, qk_ref, u_ref, w_ref, o_ref, st_ref, state):
        @pl.when(pl.program_id(0) == 0)
        def _():
            state[...] = jnp.zeros_like(state)

        def chunk(ci, carry):
            rows = _chunk_rows(ci)
            heads = range(N_HEADS)
            dec = [_decays(gc_ref[0, rows, _head_lanes(h)]) for h in heads]
            s = [state[h] for h in heads]
            sb = [s[h].astype(BF16) for h in heads]
            vnb = [(u_ref[h, rows, :] - _nn(w_ref[h, rows, :].astype(BF16), sb[h])).astype(BF16) for h in heads]
            for h in heads:
                o_ref[rows, _head_lanes(h)] = (_nn((q_ref[0, rows, _head_lanes(h)] * dec[h][0]).astype(BF16), sb[h])
                                               + _nn(qk_ref[h, rows, :].astype(BF16), vnb[h]))
                st_ref[h, rows, :] = s[h]
            for h in heads:
                state[h] = s[h] * dec[h][2] + _tn((k_ref[0, rows, _head_lanes(h)] * dec[h][1]).astype(BF16), vnb[h])
            return carry

        lax.fori_loop(0, GROUP, chunk, 0)

    blk = pl.BlockSpec((N_HEADS, rows_per_step, HEAD_DIM), lambda g: (0, g, 0))
    return pl.pallas_call(
        body,
        grid=(seq // rows_per_step,),
        in_specs=[_token_blocks(0), _token_blocks(1), _token_blocks(4), blk, blk, blk],
        out_specs=[pl.BlockSpec((rows_per_step, HEAD_W), lambda g: (g, 0)), blk],
        out_shape=[jax.ShapeDtypeStruct((seq, HEAD_W), F32), jax.ShapeDtypeStruct((N_HEADS, seq, HEAD_DIM), F32)],
        scratch_shapes=[pltpu.VMEM((N_HEADS, CHUNK, CHUNK), F32)],
        compiler_params=_params(("arbitrary",)),
        name="delta_scan_fwd",
    )(xs, xs, xs, qk_h, u_h, w_h)


def _delta_scan_bwd(xs, qk_h, w_h, do):
    seq = xs.shape[1]
    rows_per_step = GROUP * CHUNK
    n_steps = seq // rows_per_step

    def body(q_ref, k_ref, gc_ref, qk_ref, w_ref, do_ref, dsn_ref, dvn_ref, dstate):
        @pl.when(pl.program_id(0) == 0)
        def _():
            dstate[...] = jnp.zeros_like(dstate)

        def chunk(step, carry):
            rows = _chunk_rows(GROUP - 1 - step)
            heads = range(N_HEADS)
            dec = [_decays(gc_ref[0, rows, _head_lanes(h)]) for h in heads]
            ds_next = [dstate[h] for h in heads]
            dob = [do_ref[rows, _head_lanes(h)].astype(BF16) for h in heads]
            dv_new = [_tn(qk_ref[h, rows, :].astype(BF16), dob[h])
                      + _nn((k_ref[0, rows, _head_lanes(h)] * dec[h][1]).astype(BF16), ds_next[h].astype(BF16))
                      for h in heads]
            for h in heads:
                dsn_ref[h, rows, :] = ds_next[h]
                dvn_ref[h, rows, :] = dv_new[h]
            for h in heads:
                dstate[h] = (_tn((q_ref[0, rows, _head_lanes(h)] * dec[h][0]).astype(BF16), dob[h])
                             + dec[h][2] * ds_next[h] - _tn(w_ref[h, rows, :].astype(BF16), dv_new[h].astype(BF16)))
            return carry

        lax.fori_loop(0, GROUP, chunk, 0)

    blk = pl.BlockSpec((N_HEADS, rows_per_step, HEAD_DIM), lambda g: (0, n_steps - 1 - g, 0))
    return pl.pallas_call(
        body,
        grid=(n_steps,),
        in_specs=[_token_blocks(0, n_steps), _token_blocks(1, n_steps), _token_blocks(4, n_steps), blk, blk,
                  pl.BlockSpec((rows_per_step, HEAD_W), lambda g: (n_steps - 1 - g, 0))],
        out_specs=[blk, blk],
        out_shape=[jax.ShapeDtypeStruct((N_HEADS, seq, HEAD_DIM), F32)] * 2,
        scratch_shapes=[pltpu.VMEM((N_HEADS, CHUNK, CHUNK), F32)],
        compiler_params=_params(("arbitrary",)),
        name="delta_scan_bwd",
    )(xs, xs, xs, qk_h, w_h, do)


def _delta_chunk_bwd(xs, inv_h, u_h, w_h, st_h, dsn_h, dvn_h, do):
    seq = xs.shape[1]
    rows_per_step = GROUP * CHUNK

    def body(x_ref, inv_ref, u_ref, w_ref, st_ref, dsn_ref, dvn_ref, do_ref, dx_ref):
        causal, strict = _tri(True), _tri(True, strict=True)
        last_row = lax.broadcasted_iota(jnp.int32, (CHUNK, CHUNK), 0) == CHUNK - 1

        def bf(vals):
            return [val.astype(BF16) for val in vals]

        def group(hh, first):
            lanes = slice(hh * HEAD_DIM, (hh + 1) * HEAD_DIM)
            rows = [slice(step * CHUNK, (step + 1) * CHUNK) for step in range(first, first + UNROLL)]
            n = range(UNROLL)
            q, k, v, beta, gc = [[x_ref[j, r, lanes] for r in rows] for j in range(5)]
            terms = [_chunk_terms(q[i], k[i], v[i], beta[i], gc[i]) for i in n]
            e, f, e_last, decay, kb, a_mat, qk = [[t[j] for t in terms] for j in range(7)]
            inv = [_split(inv_ref[hh, r, :]) for r in rows]
            u = [u_ref[hh, r, :] for r in rows]
            w = [w_ref[hh, r, :] for r in rows]
            s = [st_ref[hh, r, :] for r in rows]
            ds_next = [dsn_ref[hh, r, :] for r in rows]
            dv_new = [dvn_ref[hh, r, :] for r in rows]
            sb, dsb, dvb, wb = bf(s), bf(ds_next), bf(dv_new), bf(w)
            dob = bf([do_ref[r, lanes] for r in rows])
            qbf, kbf, kbb = bf(q), bf(k), bf(kb)
            vnb = bf([u[i] - _nn(wb[i], sb[i]) for i in n])
            dqe = [_nt(dob[i], sb[i]) for i in n]
            dw = [-_nt(dvb[i], sb[i]) for i in n]
            dkf = [_nt(vnb[i], dsb[i]) for i in n]
            dqk = [jnp.where(causal, _nt(dob[i], vnb[i]), 0.0) for i in n]
            drhs_u = [_dot3(inv[i], _split(dv_new[i]), _tn) for i in n]
            drhs_w = [_dot3(inv[i], _split(dw[i]), _tn) for i in n]
            da = [-jnp.where(strict, _nt(drhs_u[i].astype(BF16), u[i].astype(BF16))
                             + _nt(drhs_w[i].astype(BF16), wb[i]), 0.0) for i in n]
            dad = bf([da[i] * decay[i] for i in n])
            dqd = bf([dqk[i] * decay[i] for i in n])
            dkb = [e[i] * drhs_w[i] + _nn(dad[i], kbf[i]) for i in n]
            dk = [_tn(dad[i], kbb[i]) + _tn(dqd[i], qbf[i]) + f[i] * dkf[i] + beta[i] * dkb[i] for i in n]
            dq = [_nn(dqd[i], kbf[i]) + e[i] * dqe[i] for i in n]
            for i in n:
                de_full = kb[i] * drhs_w[i] + q[i] * dqe[i]
                df_full = k[i] * dkf[i]
                m = da[i] * a_mat[i] + dqk[i] * qk[i]
                dgc = de_full * e[i] - df_full * f[i] + m - m.T
                tail = jnp.sum(df_full * f[i] + s[i] * ds_next[i] * e_last[i], axis=0, keepdims=True)
                dgc = dgc + jnp.where(last_row, jnp.broadcast_to(tail, (CHUNK, CHUNK)), 0.0)
                dx_ref[0, rows[i], lanes] = dq[i]
                dx_ref[1, rows[i], lanes] = dk[i]
                dx_ref[2, rows[i], lanes] = beta[i] * drhs_u[i]
                dx_ref[3, rows[i], lanes] = v[i] * drhs_u[i] + k[i] * dkb[i]
                dx_ref[4, rows[i], lanes] = dgc

        for hh in range(2):
            for first in range(0, GROUP, UNROLL):
                group(hh, first)

    blk = pl.BlockSpec((2, rows_per_step, HEAD_DIM), lambda p, g: (p, g, 0))
    return pl.pallas_call(
        body,
        grid=(N_HEADS // 2, seq // rows_per_step),
        in_specs=[_pair_specs(5)] + [blk] * 6 + [pl.BlockSpec((rows_per_step, LANES), lambda p, g: (g, p))],
        out_specs=_pair_specs(5),
        out_shape=jax.ShapeDtypeStruct((5, seq, HEAD_W), F32),
        compiler_params=_params(("parallel", "parallel")),
        name="delta_chunk_bwd",
    )(xs, inv_h, u_h, w_h, st_h, dsn_h, dvn_h, do)


def _delta_post_fwd(o, qkvz, gain_row):
    seq = o.shape[0]

    def body(o_ref, z_ref, g_ref, y_ref):
        ov = o_ref[...]
        ms = _pick(ov * ov, _head_sum_matrix()) * (1.0 / HEAD_DIM)
        rb = _pick(lax.rsqrt(ms + EPS), _head_spread_matrix())
        y_ref[...] = (ov * rb * g_ref[...] * _silu(z_ref[...])).astype(y_ref.dtype)

    tile = pl.BlockSpec((ROW_TILE, HEAD_W), lambda i: (i, 0))
    return pl.pallas_call(
        body,
        grid=(seq // ROW_TILE,),
        in_specs=[tile, pl.BlockSpec((ROW_TILE, HEAD_W), lambda i: (i, 3)), pl.BlockSpec((1, HEAD_W), lambda i: (0, 0))],
        out_specs=tile,
        out_shape=jax.ShapeDtypeStruct((seq, HEAD_W), BF16),
        compiler_params=_params(("arbitrary",)),
        name="delta_post_fwd",
    )(o, qkvz, gain_row)


def _delta_post_bwd(dy, o, qkvz, gain_row):
    seq = o.shape[0]

    def body(dy_ref, o_ref, z_ref, g_ref, do_ref, dz_ref, dg_ref):
        @pl.when(pl.program_id(0) == 0)
        def _():
            dg_ref[...] = jnp.zeros_like(dg_ref)

        ov, zv, dyv, gain = o_ref[...], z_ref[...], dy_ref[...], g_ref[...]
        hsum, hspread = _head_sum_matrix(), _head_spread_matrix()
        ms = _pick(ov * ov, hsum) * (1.0 / HEAD_DIM)
        rb = _pick(lax.rsqrt(ms + EPS), hspread)
        ohat = ov * rb
        dz_ref[...] = dyv * ohat * gain * _dsilu(zv)
        dn = dyv * _silu(zv)
        dg_ref[0:1, :] += jnp.sum(dn * ohat, axis=0, keepdims=True)
        dohat = dn * gain

        @pl.when(pl.program_id(0) == pl.num_programs(0) - 1)
        def _():
            fold = (lax.broadcasted_iota(jnp.int32, (HEAD_W, HEAD_W), 0) % HEAD_DIM
                    == lax.broadcasted_iota(jnp.int32, (HEAD_W, HEAD_W), 1)).astype(F32)
            dg_ref[1:2, :] = _pick(dg_ref[0:1, :], fold)

        proj = _pick(_pick(dohat * ohat, hsum) * (1.0 / HEAD_DIM), hspread)
        do_ref[...] = rb * (dohat - ohat * proj)

    tile = pl.BlockSpec((ROW_TILE, HEAD_W), lambda i: (i, 0))
    return pl.pallas_call(
        body,
        grid=(seq // ROW_TILE,),
        in_specs=[pl.BlockSpec((ROW_TILE, HEAD_W), lambda i: (i, 1)), tile,
                  pl.BlockSpec((ROW_TILE, HEAD_W), lambda i: (i, 3)), pl.BlockSpec((1, HEAD_W), lambda i: (0, 0))],
        out_specs=[tile, tile, pl.BlockSpec((2, HEAD_W), lambda i: (0, 0))],
        out_shape=[jax.ShapeDtypeStruct((seq, HEAD_W), F32), jax.ShapeDtypeStruct((seq, HEAD_W), F32),
                   jax.ShapeDtypeStruct((2, HEAD_W), F32)],
        compiler_params=_params(("arbitrary",)),
        name="delta_post_bwd",
    )(dy, o, qkvz, gain_row)


def _delta_prep_bwd(qkvz, ba, conv_w, alog_row, dt_row, dxs):
    seq = qkvz.shape[0]
    qkv_w = 3 * HEAD_W

    def body(x_ref, xp_ref, ba_ref, w_ref, al_ref, dt_ref, dx_ref, dconv_ref, dba_ref, dvec_ref, stage):
        i = pl.program_id(0)

        @pl.when(i == 0)
        def _():
            dvec_ref[...] = jnp.zeros_like(dvec_ref)

        _stage_rows(stage, x_ref, xp_ref, i)
        pre = _conv_taps(stage, w_ref, ROW_TILE)
        act = _silu(pre)
        slope = _dsilu(pre)
        hsum, hspread = _head_sum_matrix(), _head_spread_matrix()
        for j, scale in ((0, HEAD_DIM ** -0.5), (1, 1.0)):
            cols = slice(j * HEAD_W, (j + 1) * HEAD_W)
            xc = act[:, cols]
            rb = _l2_scale(xc, hsum, hspread)
            xhat = xc * rb
            dhat = dx_ref[j] * scale
            proj = _pick(_pick(dhat * xhat, hsum), hspread)
            dconv_ref[:, cols] = rb * (dhat - xhat * proj) * slope[:, cols]
        dconv_ref[:, 2 * HEAD_W:] = dx_ref[2] * slope[:, 2 * HEAD_W:]

        bav = ba_ref[...]
        beta8 = _sigmoid(bav)
        dbeta8 = _pick(dx_ref[3], _head_gather_matrix())
        dgc8 = _pick(dx_ref[4], _head_gather_matrix(N_HEADS))
        rev = _tri(False).astype(F32)
        z = bav + dt_ref[...]
        ea = jnp.exp(al_ref[...])
        g8 = -ea * _softplus(z)
        sig = _sigmoid(z)
        d_alog = jnp.zeros((1, LANES), F32)
        d_dt = jnp.zeros((1, LANES), F32)
        for ch in range(ROW_TILE // CHUNK):
            rows = slice(ch * CHUNK, (ch + 1) * CHUNK)
            dg8 = _pick_left(rev, dgc8[rows])
            da = -dg8 * ea * sig[rows]
            dba_ref[rows, :] = dbeta8[rows] * beta8[rows] * (1.0 - beta8[rows]) + da
            d_alog = d_alog + jnp.sum(dg8 * g8[rows], axis=0, keepdims=True)
            d_dt = d_dt + jnp.sum(da, axis=0, keepdims=True)
        dvec_ref[0:1, :] += d_alog
        dvec_ref[1:2, :] += d_dt

    return pl.pallas_call(
        body,
        grid=(seq // ROW_TILE,),
        in_specs=[
            pl.BlockSpec((ROW_TILE, qkv_w), lambda i: (i, 0)),
            pl.BlockSpec((8, qkv_w), lambda i: (jnp.maximum(i * (ROW_TILE // 8) - 1, 0), 0)),
            pl.BlockSpec((ROW_TILE, LANES), lambda i: (i, 0)),
            pl.BlockSpec((4, qkv_w), lambda i: (0, 0)),
            pl.BlockSpec((1, LANES), lambda i: (0, 0)),
            pl.BlockSpec((1, LANES), lambda i: (0, 0)),
            pl.BlockSpec((5, ROW_TILE, HEAD_W), lambda i: (0, i, 0)),
        ],
        out_specs=[pl.BlockSpec((ROW_TILE, qkv_w), lambda i: (i, 0)),
                   pl.BlockSpec((ROW_TILE, LANES), lambda i: (i, 0)),
                   pl.BlockSpec((2, LANES), lambda i: (0, 0))],
        out_shape=[jax.ShapeDtypeStruct((seq, qkv_w), F32), jax.ShapeDtypeStruct((seq, LANES), F32),
                   jax.ShapeDtypeStruct((2, LANES), F32)],
        scratch_shapes=[pltpu.VMEM((ROW_TILE + 8, qkv_w), F32)],
        compiler_params=_params(("arbitrary",)),
        name="delta_prep_bwd",
    )(qkvz, qkvz, ba, conv_w, alog_row, dt_row, dxs)


def _conv_bwd(dconv, qkvz, conv_w):
    seq = dconv.shape[0]
    qkv_w = 3 * HEAD_W
    n_tiles = seq // ROW_TILE

    def body(dy_ref, dyn_ref, x_ref, xp_ref, w_ref, dx_ref, dw_ref, stage, dstage):
        i = pl.program_id(0)

        @pl.when(i == 0)
        def _():
            dw_ref[...] = jnp.zeros_like(dw_ref)

        _stage_rows(stage, x_ref, xp_ref, i)
        dstage[0:ROW_TILE, :] = dy_ref[...]
        dstage[ROW_TILE:ROW_TILE + 8, :] = jnp.where(i == n_tiles - 1, 0.0, dyn_ref[...])
        dy = dy_ref[...]
        dx_ref[...] = (w_ref[3:4, :] * dy + w_ref[2:3, :] * dstage[1:1 + ROW_TILE, :]
                       + w_ref[1:2, :] * dstage[2:2 + ROW_TILE, :] + w_ref[0:1, :] * dstage[3:3 + ROW_TILE, :])
        for j in range(4):
            dw_ref[j:j + 1, :] += jnp.sum(dy * stage[5 + j:5 + j + ROW_TILE, :], axis=0, keepdims=True)

    tile = pl.BlockSpec((ROW_TILE, qkv_w), lambda i: (i, 0))
    return pl.pallas_call(
        body,
        grid=(n_tiles,),
        in_specs=[
            tile,
            pl.BlockSpec((8, qkv_w), lambda i: (jnp.minimum((i + 1) * (ROW_TILE // 8), seq // 8 - 1), 0)),
            tile,
            pl.BlockSpec((8, qkv_w), lambda i: (jnp.maximum(i * (ROW_TILE // 8) - 1, 0), 0)),
            pl.BlockSpec((4, qkv_w), lambda i: (0, 0)),
        ],
        out_specs=[tile, pl.BlockSpec((4, qkv_w), lambda i: (0, 0))],
        out_shape=[jax.ShapeDtypeStruct((seq, qkv_w), F32), jax.ShapeDtypeStruct((4, qkv_w), F32)],
        scratch_shapes=[pltpu.VMEM((ROW_TILE + 8, qkv_w), F32), pltpu.VMEM((ROW_TILE + 8, qkv_w), F32)],
        compiler_params=_params(("arbitrary",)),
        name="conv_bwd",
    )(dconv, dconv, qkvz, qkvz, conv_w)


FF_TILE = 1408


def _row(a):
    return pl.BlockSpec((1, a), lambda *_: (0, 0))


def _rms_fwd(xv, gain):
    rstd = lax.rsqrt(jnp.mean(xv * xv, axis=-1, keepdims=True) + EPS)
    xhat = xv * rstd
    return xhat, rstd, xhat * gain


def _rms_bwd(dnorm, xhat, rstd, gain):
    dxhat = dnorm * gain
    dx = rstd * (dxhat - xhat * jnp.mean(dxhat * xhat, axis=-1, keepdims=True))
    return dx, jnp.sum(dnorm * xhat, axis=0, keepdims=True)


def _inproj_fwd(x, gain, scale, shift, w_a, w_d, w_ba):
    seq = x.shape[0]

    def body(x_ref, g_ref, sc_ref, sh_ref, wa_ref, wd_ref, wb_ref, h_ref, a_ref, d_ref, b_ref):
        _, _, norm = _rms_fwd(x_ref[...], g_ref[...])
        h = (norm * (1.0 + sc_ref[...]) + sh_ref[...]).astype(BF16)
        h_ref[...] = h
        a_ref[...] = _nn(h, wa_ref[...])
        d_ref[...] = _nn(h, wd_ref[...])
        b_ref[...] = _nn(h, wb_ref[...])

    def rows(width):
        return pl.BlockSpec((ROW_TILE, width), lambda i: (i, 0))

    def whole(a):
        return pl.BlockSpec(a.shape, lambda i: (0, 0))

    return pl.pallas_call(
        body,
        grid=(seq // ROW_TILE,),
        in_specs=[rows(D_MODEL), _row(D_MODEL), _row(D_MODEL), _row(D_MODEL), whole(w_a), whole(w_d), whole(w_ba)],
        out_specs=[rows(D_MODEL), rows(3 * HEAD_W), rows(4 * HEAD_W), rows(LANES)],
        out_shape=[jax.ShapeDtypeStruct((seq, D_MODEL), BF16), jax.ShapeDtypeStruct((seq, 3 * HEAD_W), F32),
                   jax.ShapeDtypeStruct((seq, 4 * HEAD_W), F32), jax.ShapeDtypeStruct((seq, LANES), F32)],
        compiler_params=_params(("arbitrary",)),
        name="inproj_fwd",
    )(x, gain, scale, shift, w_a, w_d, w_ba)


def _outproj_fwd(y_attn, y_delta, w_out, x, gate1, gain, scale, shift):
    seq = x.shape[0]

    def body(ya_ref, yd_ref, wa_ref, wd_ref, x_ref, g1_ref, g_ref, sc_ref, sh_ref, x1_ref, h_ref, y_ref):
        y = _nn(ya_ref[...].astype(BF16), wa_ref[...]) + _nn(yd_ref[...], wd_ref[...])
        x1 = x_ref[...] + g1_ref[...] * y
        _, _, norm = _rms_fwd(x1, g_ref[...])
        x1_ref[...] = x1
        h_ref[...] = (norm * (1.0 + sc_ref[...]) + sh_ref[...]).astype(BF16)
        y_ref[...] = y.astype(BF16)

    def rows(width):
        return pl.BlockSpec((ROW_TILE, width), lambda i: (i, 0))

    return pl.pallas_call(
        body,
        grid=(seq // ROW_TILE,),
        in_specs=[rows(HEAD_W), rows(HEAD_W),
                  pl.BlockSpec((HEAD_W, D_MODEL), lambda i: (0, 0)), pl.BlockSpec((HEAD_W, D_MODEL), lambda i: (1, 0)),
                  rows(D_MODEL), _row(D_MODEL), _row(D_MODEL), _row(D_MODEL), _row(D_MODEL)],
        out_specs=[rows(D_MODEL), rows(D_MODEL), rows(D_MODEL)],
        out_shape=[jax.ShapeDtypeStruct((seq, D_MODEL), F32), jax.ShapeDtypeStruct((seq, D_MODEL), BF16),
                   jax.ShapeDtypeStruct((seq, D_MODEL), BF16)],
        compiler_params=_params(("arbitrary",)),
        name="outproj_fwd",
    )(y_attn, y_delta, w_out, w_out, x, gate1, gain, scale, shift)


def _ffn_fwd(h2, w_gate, w_up, w_down, x1, gate2, final_gain, target):
    seq = h2.shape[0]
    n_rows, n_ff = seq // ROW_TILE, D_FF // FF_TILE

    def body(h_ref, wg_ref, wu_ref, wd_ref, x1_ref, g2_ref, gf_ref, t_ref, gate_ref, up_ref, dx2_ref, st_ref, acc):
        i, j = pl.program_id(0), pl.program_id(1)

        @pl.when((i == 0) & (j == 0))
        def _():
            st_ref[...] = jnp.zeros_like(st_ref)

        h = h_ref[...]
        gate = _nn(h, wg_ref[...])
        up = _nn(h, wu_ref[...])
        gate_ref[...] = gate.astype(BF16)
        up_ref[...] = up.astype(BF16)
        part = _nn((_silu(gate) * up).astype(BF16), wd_ref[...])

        @pl.when(j == 0)
        def _():
            acc[...] = part

        @pl.when(j > 0)
        def _():
            acc[...] += part

        @pl.when(j == n_ff - 1)
        def _():
            y2 = acc[...]
            x2 = x1_ref[...] + g2_ref[...] * y2
            xhat, rstd, out = _rms_fwd(x2, gf_ref[...])
            diff = out - t_ref[...]
            dx2, dgain = _rms_bwd(diff * (1.0 / D_MODEL), xhat, rstd, gf_ref[...])
            dx2_ref[...] = dx2
            st_ref[0:1, :] += dgain
            st_ref[1:2, :] += jnp.sum(dx2 * y2, axis=0, keepdims=True)
            st_ref[2:3, :] += jnp.sum(diff * diff, axis=0, keepdims=True) * (0.5 / D_MODEL)

        @pl.when((i == n_rows - 1) & (j == n_ff - 1))
        def _():
            st_ref[3:4, :] = jnp.broadcast_to(jnp.sum(st_ref[2:3, :], keepdims=True), (1, D_MODEL))

    def rows(width):
        return pl.BlockSpec((ROW_TILE, width), lambda i, j: (i, 0))

    ff = pl.BlockSpec((ROW_TILE, FF_TILE), lambda i, j: (i, j))
    return pl.pallas_call(
        body,
        grid=(n_rows, n_ff),
        in_specs=[rows(D_MODEL),
                  pl.BlockSpec((D_MODEL, FF_TILE), lambda i, j: (0, j)), pl.BlockSpec((D_MODEL, FF_TILE), lambda i, j: (0, j)),
                  pl.BlockSpec((FF_TILE, D_MODEL), lambda i, j: (j, 0)),
                  rows(D_MODEL), _row(D_MODEL), _row(D_MODEL), rows(D_MODEL)],
        out_specs=[ff, ff, rows(D_MODEL), pl.BlockSpec((8, D_MODEL), lambda i, j: (0, 0))],
        out_shape=[jax.ShapeDtypeStruct((seq, D_FF), BF16), jax.ShapeDtypeStruct((seq, D_FF), BF16),
                   jax.ShapeDtypeStruct((seq, D_MODEL), F32), jax.ShapeDtypeStruct((8, D_MODEL), F32)],
        scratch_shapes=[pltpu.VMEM((ROW_TILE, D_MODEL), F32)],
        compiler_params=_params(("arbitrary", "arbitrary")),
        name="ffn_fwd",
    )(h2, w_gate, w_up, w_down, x1, gate2, final_gain, target)


def _ffn_bwd(dx2, gate, up, w_gate, w_up, w_down, x1, y, gate2, gate1, gain, scale):
    seq = dx2.shape[0]

    def act_body(dx2_ref, g2_ref, gate_ref, up_ref, wd_ref, dgate_ref, dup_ref, act_ref, dy2_ref):
        dy2 = (g2_ref[...] * dx2_ref[...]).astype(BF16)
        dy2_ref[...] = dy2
        gate = gate_ref[...].astype(F32)
        up = up_ref[...].astype(F32)
        dact = _nt(dy2, wd_ref[...])
        silu = _silu(gate)
        act_ref[...] = (silu * up).astype(BF16)
        dgate_ref[...] = (dact * up * _dsilu(gate)).astype(BF16)
        dup_ref[...] = (dact * silu).astype(BF16)

    def rows2(width):
        return pl.BlockSpec((ROW_TILE, width), lambda i, j: (i, 0))

    ff = pl.BlockSpec((ROW_TILE, FF_TILE), lambda i, j: (i, j))
    dgate, dup, act, dy2 = pl.pallas_call(
        act_body,
        grid=(seq // ROW_TILE, D_FF // FF_TILE),
        in_specs=[rows2(D_MODEL), _row(D_MODEL), ff, ff, pl.BlockSpec((FF_TILE, D_MODEL), lambda i, j: (j, 0))],
        out_specs=[ff, ff, ff, rows2(D_MODEL)],
        out_shape=[jax.ShapeDtypeStruct((seq, D_FF), BF16)] * 3 + [jax.ShapeDtypeStruct((seq, D_MODEL), BF16)],
        compiler_params=_params(("arbitrary", "arbitrary")),
        name="ffn_bwd_act",
    )(dx2, gate2, gate, up, w_down)

    def in_body(dgate_ref, dup_ref, wg_ref, wu_ref, dx2_ref, x1_ref, y_ref, g1_ref, g_ref, sc_ref,
                dx1_ref, dy_ref, st_ref):
        @pl.when(pl.program_id(0) == 0)
        def _():
            st_ref[...] = jnp.zeros_like(st_ref)

        dh = _nt(dgate_ref[...], wg_ref[...]) + _nt(dup_ref[...], wu_ref[...])
        xhat, rstd, norm = _rms_fwd(x1_ref[...], g_ref[...])
        dxn, dgain = _rms_bwd(dh * (1.0 + sc_ref[...]), xhat, rstd, g_ref[...])
        dx1 = dx2_ref[...] + dxn
        dx1_ref[...] = dx1
        dy_ref[...] = (g1_ref[...] * dx1).astype(BF16)
        st_ref[0:1, :] += jnp.sum(dh, axis=0, keepdims=True)
        st_ref[1:2, :] += jnp.sum(dh * norm, axis=0, keepdims=True)
        st_ref[2:3, :] += dgain
        st_ref[3:4, :] += jnp.sum(dx1 * y_ref[...].astype(F32), axis=0, keepdims=True)

    half_tile = ROW_TILE // 2

    def rows(width):
        return pl.BlockSpec((half_tile, width), lambda i: (i, 0))

    whole = pl.BlockSpec((D_MODEL, D_FF), lambda i: (0, 0))
    dx1, dy, stats = pl.pallas_call(
        in_body,
        grid=(seq // half_tile,),
        in_specs=[rows(D_FF), rows(D_FF), whole, whole, rows(D_MODEL), rows(D_MODEL), rows(D_MODEL),
                  _row(D_MODEL), _row(D_MODEL), _row(D_MODEL)],
        out_specs=[rows(D_MODEL), rows(D_MODEL), pl.BlockSpec((8, D_MODEL), lambda i: (0, 0))],
        out_shape=[jax.ShapeDtypeStruct((seq, D_MODEL), F32), jax.ShapeDtypeStruct((seq, D_MODEL), BF16),
                   jax.ShapeDtypeStruct((8, D_MODEL), F32)],
        compiler_params=_params(("arbitrary",)),
        name="ffn_bwd_in",
    )(dgate, dup, w_gate, w_up, dx2, x1, y, gate1, gain, scale)
    return dgate, dup, act, dy2, dx1, dy, stats


def _outproj_bwd(dy, w_out):
    seq = dy.shape[0]

    def body(dy_ref, w_ref, out_ref):
        out_ref[...] = _nt(dy_ref[...], w_ref[...])

    rows = pl.BlockSpec((ROW_TILE, D_MODEL), lambda i: (i, 0))
    return pl.pallas_call(
        body,
        grid=(seq // ROW_TILE,),
        in_specs=[rows, pl.BlockSpec((D_MODEL, D_MODEL), lambda i: (0, 0))],
        out_specs=rows,
        out_shape=jax.ShapeDtypeStruct((seq, D_MODEL), F32),
        compiler_params=_params(("arbitrary",)),
        name="outproj_bwd",
    )(dy, w_out)


def _inproj_bwd(dq, dk, dv, dxd, dz, dba, w_a, w_d, w_ba, x, dx1, gain, scale):
    seq = x.shape[0]

    def body(dq_ref, dk_ref, dv_ref, dxd_ref, dz_ref, dba_ref, wa_ref, wd_ref, wb_ref, x_ref, dx1_ref, g_ref, sc_ref,
             gx_ref, st_ref):
        @pl.when(pl.program_id(0) == 0)
        def _():
            st_ref[...] = jnp.zeros_like(st_ref)

        dh = (_nt(dq_ref[...].astype(BF16), wa_ref[:, 0:HEAD_W])
              + _nt(dk_ref[...].astype(BF16), wa_ref[:, HEAD_W:2 * HEAD_W])
              + _nt(dv_ref[...].astype(BF16), wa_ref[:, 2 * HEAD_W:])
              + _nt(dxd_ref[...].astype(BF16), wd_ref[:, 0:3 * HEAD_W])
              + _nt(dz_ref[...].astype(BF16), wd_ref[:, 3 * HEAD_W:])
              + _nt(dba_ref[...].astype(BF16), wb_ref[...]))
        xhat, rstd, norm = _rms_fwd(x_ref[...], g_ref[...])
        dxn, dgain = _rms_bwd(dh * (1.0 + sc_ref[...]), xhat, rstd, g_ref[...])
        gx_ref[...] = dx1_ref[...] + dxn
        st_ref[0:1, :] += jnp.sum(dh, axis=0, keepdims=True)
        st_ref[1:2, :] += jnp.sum(dh * norm, axis=0, keepdims=True)
        st_ref[2:3, :] += dgain

    def rows(width):
        return pl.BlockSpec((ROW_TILE, width), lambda i: (i, 0))

    def whole(a):
        return pl.BlockSpec(a.shape, lambda i: (0, 0))

    return pl.pallas_call(
        body,
        grid=(seq // ROW_TILE,),
        in_specs=[rows(HEAD_W), rows(HEAD_W), rows(HEAD_W), rows(3 * HEAD_W), rows(HEAD_W), rows(LANES),
                  whole(w_a), whole(w_d), whole(w_ba), rows(D_MODEL), rows(D_MODEL), _row(D_MODEL), _row(D_MODEL)],
        out_specs=[rows(D_MODEL), pl.BlockSpec((8, D_MODEL), lambda i: (0, 0))],
        out_shape=[jax.ShapeDtypeStruct((seq, D_MODEL), F32), jax.ShapeDtypeStruct((8, D_MODEL), F32)],
        compiler_params=_params(("arbitrary",)),
        name="inproj_bwd",
    )(dq, dk, dv, dxd, dz, dba, w_a, w_d, w_ba, x, dx1, gain, scale)


def _weight_grad(a, b, name):
    seq, m = a.shape
    n = b.shape[1]
    tm = m if m <= 1536 else m // 2
    tn = n if n <= 1536 else n // 2
    n_k = seq // ROW_TILE

    def body(a_ref, b_ref, out_ref):
        part = _tn(a_ref[...].astype(BF16), b_ref[...].astype(BF16))

        @pl.when(pl.program_id(2) == 0)
        def _():
            out_ref[...] = part

        @pl.when(pl.program_id(2) > 0)
        def _():
            out_ref[...] += part

    return pl.pallas_call(
        body,
        grid=(m // tm, n // tn, n_k),
        in_specs=[pl.BlockSpec((ROW_TILE, tm), lambda i, j, k: (k, i)),
                  pl.BlockSpec((ROW_TILE, tn), lambda i, j, k: (k, j))],
        out_specs=pl.BlockSpec((tm, tn), lambda i, j, k: (i, j)),
        out_shape=jax.ShapeDtypeStruct((m, n), F32),
        compiler_params=_params(("arbitrary", "arbitrary", "arbitrary")),
        name=name,
    )(a, b)


def _weight_grad_stack(pieces, b, name):
    seq, n = b.shape
    widths = [a.shape[1] for a in pieces]
    starts = [sum(widths[:i]) for i in range(len(pieces))]

    def body(*refs):
        a_refs, b_ref, out_ref = refs[:len(pieces)], refs[len(pieces)], refs[len(pieces) + 1]

        @pl.when(pl.program_id(0) == 0)
        def _():
            out_ref[...] = jnp.zeros_like(out_ref)

        bb = b_ref[...].astype(BF16)
        for a_ref, start, width in zip(a_refs, starts, widths):
            out_ref[start:start + width, :] += _tn(a_ref[...].astype(BF16), bb)

    def rows(width):
        return pl.BlockSpec((ROW_TILE, width), lambda k: (k, 0))

    return pl.pallas_call(
        body,
        grid=(seq // ROW_TILE,),
        in_specs=[rows(w) for w in widths] + [rows(n)],
        out_specs=pl.BlockSpec((sum(widths), n), lambda k: (0, 0)),
        out_shape=jax.ShapeDtypeStruct((sum(widths), n), F32),
        compiler_params=_params(("arbitrary",)),
        name=name,
    )(*pieces, b)


def _adamw(w, g, m, v, name):
    n_rows, n_cols = w.shape
    if n_rows % 256 == 0:
        block, grid, index = (256, n_cols), (n_rows // 256,), lambda i: (i, 0)
    elif n_cols % 256 == 0:
        block, grid, index = (n_rows, 256), (n_cols // 256,), lambda i: (0, i)
    else:
        block, grid, index = (n_rows, n_cols), (1,), lambda i: (0, 0)

    def body(w_ref, g_ref, m_ref, v_ref, d_ref, nm_ref, nv_ref):
        gv = g_ref[...]
        nm = ADAM_B1 * m_ref[...] + (1.0 - ADAM_B1) * gv
        nv = ADAM_B2 * v_ref[...] + (1.0 - ADAM_B2) * (gv * gv)
        m_hat = nm / (1.0 - ADAM_B1 ** ADAM_STEP)
        v_hat = nv / (1.0 - ADAM_B2 ** ADAM_STEP)
        d_ref[...] = -ADAM_LR * (m_hat / (jnp.sqrt(v_hat) + ADAM_EPS) + ADAM_WD * w_ref[...])
        nm_ref[...] = nm
        nv_ref[...] = nv

    blk = pl.BlockSpec(block, index)
    shape = jax.ShapeDtypeStruct((n_rows, n_cols), F32)
    return pl.pallas_call(
        body,
        grid=grid,
        in_specs=[blk] * 4,
        out_specs=[blk] * 3,
        out_shape=[shape] * 3,
        compiler_params=_params(("arbitrary",)),
        name=name,
    )(w, g, m, v)


IN_WIDTH = 3600
BA_COL = 7 * HEAD_W


def _local_step(x, target, mod, norm_attn_g, w_in, rel_bias, conv_w, a_log, dt_bias, delta_norm_g,
                norm_ffn_g, final_norm_g, shards, assemble, reduce_pairs):
    sh1, sc1, g1, sh2, sc2, g2 = [mod[:, i * D_MODEL:(i + 1) * D_MODEL] for i in range(6)]
    w_a = w_in[:, :3 * HEAD_W]
    w_d = w_in[:, 3 * HEAD_W:BA_COL]
    w_ba = jnp.pad(w_in[:, BA_COL:], ((0, 0), (0, LANES - 2 * N_HEADS)))
    tables = jnp.asarray(_attn_tables())
    alog_row = jnp.pad(a_log, ((0, 0), (N_HEADS, LANES - 2 * N_HEADS)))
    dt_row = jnp.pad(dt_bias, ((0, 0), (N_HEADS, LANES - 2 * N_HEADS)))
    gain_row = jnp.tile(delta_norm_g, (1, N_HEADS))

    h1, qkv_a, qkvz, ba = _inproj_fwd(x, norm_attn_g, sc1, sh1, w_a, w_d, w_ba)
    bias = _attention_bias(rel_bias, tables)
    y_attn, lse, *gathered = _attention_fwd(qkv_a, bias, shards)
    w_out, w_gate, w_up, w_down = assemble(gathered)
    xs = _delta_prep_fwd(qkvz, ba, conv_w, alog_row, dt_row)
    inv_h, qk_h, u_h, w_h = _delta_chunk_fwd(xs)
    o, st_h = _delta_scan_fwd(xs, qk_h, u_h, w_h)
    y_delta = _delta_post_fwd(o, qkvz, gain_row)
    x1, h2, y = _outproj_fwd(y_attn, y_delta, w_out, x, g1, norm_ffn_g, sc2, sh2)
    gate, up, dx2, st_f = _ffn_fwd(h2, w_gate, w_up, w_down, x1, g2, final_norm_g, target)

    dgate, dup, act, dy2, dx1, dy, st_b = _ffn_bwd(dx2, gate, up, w_gate, w_up, w_down, x1, y, g2, g1, norm_ffn_g, sc2)
    partials = reduce_pairs([_weight_grad_stack([y_attn, y_delta], dy, "wgrad_out"),
                             _weight_grad(dgate, h2, "wgrad_gate"), _weight_grad(dup, h2, "wgrad_up"),
                             _weight_grad(act, dy2, "wgrad_down")])
    grads = {}
    dycat = _outproj_bwd(dy, w_out)
    do, dz, dgain = _delta_post_bwd(dycat, o, qkvz, gain_row)
    dsn_h, dvn_h = _delta_scan_bwd(xs, qk_h, w_h, do)
    dxs = _delta_chunk_bwd(xs, inv_h, u_h, w_h, st_h, dsn_h, dvn_h, do)
    dconv, dba, dvec = _delta_prep_bwd(qkvz, ba, conv_w, alog_row, dt_row, dxs)
    dxd, grads["conv_w"] = _conv_bwd(dconv, qkvz, conv_w)
    dq, dk, dv, dbias, *scattered = _attention_bwd(qkv_a, dycat, y_attn, lse, bias, partials)
    grad_x, st_i = _inproj_bwd(dq, dk, dv, dxd, dz, dba, w_a, w_d, w_ba, x, dx1, norm_attn_g, sc1)
    grads["w_in"] = jnp.concatenate(
        [_weight_grad_stack([dq, dk, dv], h1, "wgrad_in_attn"),
         _weight_grad_stack([dxd, dz, dba], h1, "wgrad_in_delta")[:IN_WIDTH - 3 * HEAD_W]], axis=0)
    grads["rel_bias"] = _rel_bias_grad(dbias, tables)[:, :N_BUCKETS].T
    grads["a_log"] = dvec[0:1, N_HEADS:2 * N_HEADS]
    grads["dt_bias"] = dvec[1:2, N_HEADS:2 * N_HEADS]
    grads["delta_norm_g"] = dgain[1:2, :HEAD_DIM]
    grads["norm_attn_g"] = st_i[2:3]
    grads["norm_ffn_g"] = st_b[2:3]
    grads["final_norm_g"] = st_f[0:1]
    dmod = jnp.concatenate([st_i[0:1], st_i[1:2], st_b[3:4], st_b[0:1], st_b[1:2], st_f[1:2]], axis=1)
    return st_f[3, 0], grad_x, grads, dmod, (partials, scattered)


MESH = pl.DeviceIdType.MESH
OTHER_CHIPS = ((1, 0), (0, 1), (1, 1))
ALL_PEERS = tuple((m >> 2 & 1, m >> 1 & 1, m & 1) for m in range(1, 8))
ANY = pl.BlockSpec(memory_space=pl.ANY)
VMEM_SPEC = pl.BlockSpec(memory_space=pltpu.VMEM)
N_BIG = 5


def _me():
    return lax.axis_index("x"), lax.axis_index("y"), lax.axis_index("c")


def _flip(pos, mask):
    return tuple(1 - p if m else p for p, m in zip(pos, mask))


def _remote(src, dst, send_sems, recv_sems, k, to):
    return pltpu.make_async_remote_copy(src_ref=src, dst_ref=dst, send_sem=send_sems.at[k], recv_sem=recv_sems.at[k],
                                        device_id=to, device_id_type=MESH)


def _ada_exchange(c8, w_ada, b_ada, conv8):
    def body(c_ref, w_ref, b_ref, cv_ref, mod_ref, cact_ref, conv_ref, c_all, part_all, send_sems, recv_sems):
        x, y, c = me = _me()
        dev = 4 * x + 2 * y + c
        chip = 2 * x + y
        c_all[dev] = c_ref[...]
        conv_ref[chip] = cv_ref[...]
        first = [_remote(c_ref, c_all.at[dev], send_sems, recv_sems, k, _flip(me, mask))
                 for k, mask in enumerate(ALL_PEERS)]
        first += [_remote(cv_ref, conv_ref.at[chip], send_sems, recv_sems, 7 + j, _flip(me, (*mask, 0)))
                  for j, mask in enumerate(OTHER_CHIPS)]
        for cp in first:
            cp.start()
        for cp in first:
            cp.wait()
        row = lax.broadcasted_iota(jnp.int32, (8, D_MODEL), 0)
        c_rows = jnp.zeros((8, D_MODEL), F32)
        for d in range(8):
            c_rows = jnp.where(row == d, c_all[d], c_rows)
        c_act = _silu(c_rows)
        cact_ref[...] = c_act
        part_all[chip] = _nn(c_act, w_ref[...], HIGHEST)
        second = [_remote(part_all.at[chip], part_all.at[chip], send_sems, recv_sems, 10 + j, _flip(me, (*mask, 0)))
                  for j, mask in enumerate(OTHER_CHIPS)]
        for cp in second:
            cp.start()
        for cp in second:
            cp.wait()
        cols = w_ref.shape[1]
        for k in range(4):
            mod_ref[:, k * cols:(k + 1) * cols] = part_all[k] + b_ref[:, k * cols:(k + 1) * cols]

    cols = w_ada.shape[1]
    return pl.pallas_call(
        body,
        in_specs=[VMEM_SPEC] * 4,
        out_specs=[VMEM_SPEC] * 3,
        out_shape=[jax.ShapeDtypeStruct((8, 4 * cols), F32), jax.ShapeDtypeStruct((8, D_MODEL), F32),
                   jax.ShapeDtypeStruct((4, 8, conv8.shape[1]), F32)],
        scratch_shapes=[pltpu.VMEM((8, 8, D_MODEL), F32), pltpu.VMEM((4, 8, cols), F32),
                        pltpu.SemaphoreType.DMA((13,)), pltpu.SemaphoreType.DMA((13,))],
        compiler_params=pltpu.CompilerParams(vmem_limit_bytes=VMEM_LIMIT),
        name="ada_exchange",
    )(c8, w_ada, b_ada, conv8)


def _gather_weights(shards):
    n = len(shards)

    def body(*refs):
        first, passed = _gather_copies(refs[:n], refs[n:2 * n], *refs[2 * n:])
        for cp in first:
            cp.start()
        for cp, fwd in zip(first, passed):
            cp.wait_recv()
            fwd.start()
        for cp in first:
            cp.wait_send()
        for fwd in passed:
            fwd.wait()

    return pl.pallas_call(
        body,
        in_specs=[ANY] * n,
        out_specs=[ANY] * n,
        out_shape=_gathered_shapes(shards),
        scratch_shapes=[pltpu.SemaphoreType.DMA((6 * n,)), pltpu.SemaphoreType.DMA((6 * n,))],
        name="gather_weights",
    )(*shards)


def _gathered_shapes(shards):
    return [jax.ShapeDtypeStruct((4, *s.shape), s.dtype) for s in shards]


def _gather_copies(srcs, dsts, send_sems, recv_sems):
    x, y, c = me = _me()
    chip = 2 * x + y
    sibling = _flip(me, (0, 0, 1))
    first, passed = [], []
    for a, (src, dst) in enumerate(zip(srcs, dsts)):
        for j, mask in enumerate(OTHER_CHIPS):
            to = _flip(me, (*mask, 0))
            first.append(_remote(src.at[c], dst.at[chip, c], send_sems, recv_sems, 6 * a + j, to))
            landed = dst.at[2 * to[0] + to[1], c]
            passed.append(_remote(landed, landed, send_sems, recv_sems, 6 * a + 3 + j, sibling))
    return first, passed


def _scatter_copies(srcs, dsts, send_sems, recv_sems):
    x, y, c = me = _me()
    chip = 2 * x + y
    copies = []
    for a, (src, dst) in enumerate(zip(srcs, dsts)):
        for j, mask in enumerate(OTHER_CHIPS):
            to = _flip(me, (*mask, 0))
            copies.append(_remote(src.at[2 * to[0] + to[1]], dst.at[chip], send_sems, recv_sems, 3 * a + j, to))
    return copies


def _start_and_wait(copies):
    for cp in copies:
        cp.start()
    for cp in copies:
        cp.wait()


def _swap_halves(grads):
    n = len(grads)

    def body(*refs):
        srcs, got = refs[:n], refs[n:2 * n]
        send_sems, recv_sems = refs[2 * n:]
        x, y, c = me = _me()
        _start_and_wait([_remote(srcs[a].at[:, 1 - c], got[a], send_sems, recv_sems, a, _flip(me, (0, 0, 1)))
                         for a in range(n)])

    return pl.pallas_call(
        body,
        in_specs=[ANY] * n,
        out_specs=[ANY] * n,
        out_shape=[jax.ShapeDtypeStruct((4, g.shape[2], g.shape[3]), g.dtype) for g in grads],
        scratch_shapes=[pltpu.SemaphoreType.DMA((n,)), pltpu.SemaphoreType.DMA((n,))],
        name=f"swap_halves_{n}",
    )(*grads)


def _scatter_partials(partials):
    n = len(partials)

    def body(*refs):
        _start_and_wait(_scatter_copies(refs[:n], refs[n:2 * n], *refs[2 * n:]))

    return pl.pallas_call(
        body,
        in_specs=[ANY] * n,
        out_specs=[ANY] * n,
        out_shape=[jax.ShapeDtypeStruct(p.shape, p.dtype) for p in partials],
        scratch_shapes=[pltpu.SemaphoreType.DMA((3 * n,)), pltpu.SemaphoreType.DMA((3 * n,))],
        name="scatter_partials",
    )(*partials)


def _join_halves(halves):
    n = len(halves)

    def body(*refs):
        srcs, dsts = refs[:n], refs[n:2 * n]
        send_sems, recv_sems = refs[2 * n:]
        x, y, c = me = _me()
        _start_and_wait([_remote(srcs[a], dsts[a].at[c], send_sems, recv_sems, a, _flip(me, (0, 0, 1)))
                         for a in range(n)])

    return pl.pallas_call(
        body,
        in_specs=[ANY] * n,
        out_specs=[ANY] * n,
        out_shape=[jax.ShapeDtypeStruct((2, *h.shape), h.dtype) for h in halves],
        scratch_shapes=[pltpu.SemaphoreType.DMA((n,)), pltpu.SemaphoreType.DMA((n,))],
        name=f"join_halves_{n}",
    )(*halves)


def _gather_small(packed):
    n_rows = packed.shape[0]

    def body(p_ref, all_ref, sum_ref, send_sems, recv_sems):
        x, y, c = me = _me()
        dev = 4 * x + 2 * y + c
        all_ref[dev] = p_ref[...]
        copies = [_remote(p_ref, all_ref.at[dev], send_sems, recv_sems, k, _flip(me, mask))
                  for k, mask in enumerate(ALL_PEERS)]
        for cp in copies:
            cp.start()
        for cp in copies:
            cp.wait()
        total = all_ref[0]
        for d in range(1, 8):
            total = total + all_ref[d]
        sum_ref[...] = total

    return pl.pallas_call(
        body,
        in_specs=[VMEM_SPEC],
        out_specs=[VMEM_SPEC, VMEM_SPEC],
        out_shape=[jax.ShapeDtypeStruct((8, n_rows, LANES), F32), jax.ShapeDtypeStruct((n_rows, LANES), F32)],
        scratch_shapes=[pltpu.SemaphoreType.DMA((7,)), pltpu.SemaphoreType.DMA((7,))],
        name="gather_small",
    )(packed)


def _add_pair(a, b, out_dtype, name):
    def body(a_ref, b_ref, o_ref):
        o_ref[...] = (a_ref[...] + b_ref[...]).astype(o_ref.dtype)

    blk = pl.BlockSpec((1, *a.shape[1:]), lambda i: (i, 0, 0))
    return pl.pallas_call(
        body, grid=(a.shape[0],), in_specs=[blk, blk], out_specs=blk,
        out_shape=jax.ShapeDtypeStruct(a.shape, out_dtype),
        compiler_params=_params(("arbitrary",)), name=name,
    )(a, b)


def _add_slots(a, name):
    def body(a_ref, o_ref):
        total = a_ref[0].astype(F32)
        for k in range(1, 4):
            total = total + a_ref[k].astype(F32)
        o_ref[...] = total

    return pl.pallas_call(
        body, in_specs=[VMEM_SPEC], out_specs=VMEM_SPEC,
        out_shape=jax.ShapeDtypeStruct(a.shape[1:], F32),
        compiler_params=pltpu.CompilerParams(vmem_limit_bytes=VMEM_LIMIT), name=name,
    )(a)


def _ada_weight_grad(c_act, dmod_cols):
    def body(c_ref, d_ref, o_ref):
        o_ref[...] = _tn(c_ref[...], d_ref[...], HIGHEST)

    return pl.pallas_call(
        body, in_specs=[VMEM_SPEC, VMEM_SPEC], out_specs=VMEM_SPEC,
        out_shape=jax.ShapeDtypeStruct((c_act.shape[1], dmod_cols.shape[1]), F32),
        compiler_params=pltpu.CompilerParams(vmem_limit_bytes=VMEM_LIMIT), name="ada_weight_grad",
    )(c_act, dmod_cols)


def kernel(x, c, w_ada, b_ada, norm_attn_g, w_in, rel_bias, conv_w, a_log, dt_bias, delta_norm_g, w_out, norm_ffn_g, w_gate, w_up, w_down, final_norm_g, loss_target, m_w_ada, m_b_ada, m_norm_attn_g, m_w_in, m_rel_bias, m_conv_w, m_a_log, m_dt_bias, m_delta_norm_g, m_w_out, m_norm_ffn_g, m_w_gate, m_w_up, m_w_down, m_final_norm_g, v_w_ada, v_b_ada, v_norm_attn_g, v_w_in, v_rel_bias, v_conv_w, v_a_log, v_dt_bias, v_delta_norm_g, v_w_out, v_norm_ffn_g, v_w_gate, v_w_up, v_w_down, v_final_norm_g):
    xi, yi, ci = _me()
    dev = 4 * xi + 2 * yi + ci
    chip = 2 * xi + yi

    conv_cols = conv_w.shape[2]
    mod_all, c_act, conv_all = _ada_exchange(jnp.broadcast_to(c, (8, D_MODEL)), w_ada[0], b_ada,
                                             jnp.pad(conv_w[0], ((0, 4), (0, 0))))
    mod = lax.dynamic_slice_in_dim(mod_all, dev, 1, axis=0)
    conv_full = jnp.swapaxes(conv_all[:, :4, :], 0, 1).reshape(4, 4 * conv_cols)

    big_names = ("w_in", "w_out", "w_gate", "w_up", "w_down")
    by_cols = (True, False, True, True, False)

    def rows_form(a, cols):
        return jnp.swapaxes(a[0], 0, 1) if cols else a[0]

    def halves_form(w):
        rows, lanes = w.shape
        if (rows // 2) % 16:
            rows, lanes = w.size // LANES, LANES
        return (2, rows // 2, lanes)

    big = [rows_form(w, cols) for w, cols in zip((w_in, w_out, w_gate, w_up, w_down), by_cols)]
    shards = [w.astype(BF16).reshape(halves_form(w)) for w in big]

    def assemble(gathered, first):
        ws, ss, cols = big[first:first + len(gathered)], shards[first:], by_cols[first:]
        full = [lax.dynamic_update_index_in_dim(g, s, chip, 0).reshape(4 * w.shape[0], w.shape[1])
                for g, s, w in zip(gathered, ss, ws)]
        return [g.T if c else g for g, c in zip(full, cols)]

    def reduce_pairs(grads, first, tag):
        slots = [g.reshape(4, *halves_form(w)) for g, w in zip(grads, big[first:])]
        return [_add_pair(lax.dynamic_index_in_dim(s, ci, 1, keepdims=False), got, BF16, f"add_pair_{tag}{a}")
                for a, (s, got) in enumerate(zip(slots, _swap_halves(slots)))]

    def finish(partials, scattered, first, tag):
        by_source = [lax.dynamic_update_index_in_dim(b, lax.dynamic_index_in_dim(p, chip, 0, keepdims=False), chip, 0)
                     for b, p in zip(scattered, partials)]
        halves = [_add_slots(p, f"add_slots_{tag}{a}") for a, p in enumerate(by_source)]
        joined = [lax.dynamic_update_index_in_dim(j, h, ci, 0) for j, h in zip(_join_halves(halves), halves)]
        return [j.reshape(w.shape) for j, w in zip(joined, big[first:])]

    whole_in, = assemble(_gather_weights(shards[:1]), 0)
    loss, grad_x, grads, dmod, (partials_rest, scattered_rest) = _local_step(
        x[0], loss_target[0], mod, norm_attn_g, whole_in, rel_bias, conv_full, a_log, dt_bias, delta_norm_g,
        norm_ffn_g, final_norm_g[None], shards[1:], functools.partial(assemble, first=1),
        functools.partial(reduce_pairs, first=1, tag="rest"))

    partials_in = reduce_pairs([grads["w_in"]], 0, "in")
    big_grads = (finish(partials_in, _scatter_partials(partials_in), 0, "in")
                 + finish(partials_rest, scattered_rest, 1, "rest"))

    pieces = [dmod, grads["conv_w"], grads["norm_attn_g"], grads["norm_ffn_g"], grads["final_norm_g"],
              grads["rel_bias"], grads["a_log"], grads["dt_bias"], grads["delta_norm_g"]]
    flat = [jnp.pad(p.reshape(-1), (0, -p.size % LANES)) for p in pieces]
    n_rows = [f.size // LANES for f in flat]
    packed = jnp.concatenate(flat).reshape(-1, LANES)
    packed = jnp.pad(packed, ((0, -packed.shape[0] % 8), (0, 0)))
    all_small, total = _gather_small(packed)
    sums, start = [], 0
    for p, n in zip(pieces, n_rows):
        sums.append(total[start:start + n].reshape(-1)[:p.size].reshape(p.shape))
        start += n
    g_b_ada, g_conv, g_norm_attn, g_norm_ffn, g_final, g_rel, g_alog, g_dt, g_dnorm = sums
    dmod_all = all_small[:, :n_rows[0], :].reshape(8, -1)
    ada_cols = w_ada.shape[2]
    g_w_ada = _ada_weight_grad(c_act, lax.dynamic_slice_in_dim(dmod_all, chip * ada_cols, ada_cols, axis=1))
    g_conv = lax.dynamic_slice_in_dim(g_conv, chip * conv_cols, conv_cols, axis=1)

    grad = {"w_ada": g_w_ada[None], "b_ada": g_b_ada, "norm_attn_g": g_norm_attn,
            "rel_bias": g_rel, "conv_w": g_conv[None], "a_log": g_alog, "dt_bias": g_dt, "delta_norm_g": g_dnorm,
            "norm_ffn_g": g_norm_ffn, "final_norm_g": g_final.reshape(-1)}
    weight = {"w_ada": w_ada, "b_ada": b_ada, "norm_attn_g": norm_attn_g, "w_in": w_in, "rel_bias": rel_bias,
              "conv_w": conv_w, "a_log": a_log, "dt_bias": dt_bias, "delta_norm_g": delta_norm_g, "w_out": w_out,
              "norm_ffn_g": norm_ffn_g, "w_gate": w_gate, "w_up": w_up, "w_down": w_down, "final_norm_g": final_norm_g}
    first = {"w_ada": m_w_ada, "b_ada": m_b_ada, "norm_attn_g": m_norm_attn_g, "w_in": m_w_in, "rel_bias": m_rel_bias,
             "conv_w": m_conv_w, "a_log": m_a_log, "dt_bias": m_dt_bias, "delta_norm_g": m_delta_norm_g,
             "w_out": m_w_out, "norm_ffn_g": m_norm_ffn_g, "w_gate": m_w_gate, "w_up": m_w_up, "w_down": m_w_down,
             "final_norm_g": m_final_norm_g}
    second = {"w_ada": v_w_ada, "b_ada": v_b_ada, "norm_attn_g": v_norm_attn_g, "w_in": v_w_in, "rel_bias": v_rel_bias,
              "conv_w": v_conv_w, "a_log": v_a_log, "dt_bias": v_dt_bias, "delta_norm_g": v_delta_norm_g,
              "w_out": v_w_out, "norm_ffn_g": v_norm_ffn_g, "w_gate": v_w_gate, "w_up": v_w_up, "w_down": v_w_down,
              "final_norm_g": v_final_norm_g}
    delta, new_m, new_v = {}, {}, {}
    for name, w in weight.items():
        if name in big_names:
            continue
        two_d = (-1, w.shape[-1])
        d, nm, nv = _adamw(w.reshape(two_d), grad[name].reshape(two_d), first[name].reshape(two_d),
                           second[name].reshape(two_d), f"adamw_{name}")
        delta[name], new_m[name], new_v[name] = d.reshape(w.shape), nm.reshape(w.shape), nv.reshape(w.shape)
    for name, w, g, cols in zip(big_names, big, big_grads, by_cols):
        outs = _adamw(w, g, rows_form(first[name], cols), rows_form(second[name], cols), f"adamw_{name}")
        grad[name], delta[name], new_m[name], new_v[name] = [
            (jnp.swapaxes(o, 0, 1) if cols else o)[None] for o in (g, *outs)]

    names = list(weight)
    return (lax.psum(loss, ("x", "y", "c")), grad_x[None], *[grad[n] for n in names], *[delta[n] for n in names],
            *[new_m[n] for n in names], *[new_v[n] for n in names])
```

```python
import functools
import math

import numpy as np
import jax
import jax.numpy as jnp
from jax import lax
from jax.experimental import pallas as pl
from jax.experimental.pallas import tpu as pltpu

F32 = jnp.float32
BF16 = jnp.bfloat16
HIGHEST = lax.Precision.HIGHEST

D_MODEL = 1024
HEAD_DIM = 64
N_HEADS = 8
HEAD_W = 512
BRANCHES = ((128, 1), (512, 4), (2048, 16))
BAND = 128
ATT_TILE = 2048
ATT_UNROLL = 4
N_BUCKETS = 32
MAX_DISTANCE = 2048
CHUNK = 64
D_FF = 2816
EPS = 1e-6
NEG_INF = -1e30
LANES = 128
VMEM_LIMIT = 56 * 1024 * 1024

ADAM_LR = 0.001
ADAM_B1 = 0.9
ADAM_B2 = 0.999
ADAM_EPS = 1e-08
ADAM_WD = 0.01
ADAM_STEP = 10


def _nn(a, b, precision=None):
    return jnp.dot(a, b, preferred_element_type=F32, precision=precision)


def _nt(a, b, precision=None):
    return lax.dot_general(a, b, (((1,), (1,)), ((), ())), preferred_element_type=F32, precision=precision)


def _tn(a, b, precision=None):
    return lax.dot_general(a, b, (((0,), (0,)), ((), ())), preferred_element_type=F32, precision=precision)


def _params(sem, vmem=VMEM_LIMIT):
    return pltpu.CompilerParams(dimension_semantics=sem, vmem_limit_bytes=vmem)


def _sigmoid(x):
    return 0.5 * jnp.tanh(0.5 * x) + 0.5


def _silu_and_slope(x):
    s = _sigmoid(x)
    return x * s, s * (1.0 + x * (1.0 - s))


def _silu(x):
    return x * _sigmoid(x)


def _attn_tables():
    qi = np.arange(BAND)[:, None]
    kj = np.arange(2 * BAND)[None, :]
    steps = qi + BAND - kj
    in_window = (steps >= 0) & (steps <= BAND)
    max_exact = N_BUCKETS // 2
    out = np.zeros((3, 2, BAND, 2 * BAND), np.int32)
    for b, (_, dil) in enumerate(BRANCHES):
        dist = np.maximum(steps, 0) * dil
        dist_f = np.maximum(dist, 1).astype(np.float32)
        large = max_exact + (np.log(dist_f / np.float32(max_exact)) / np.float32(math.log(MAX_DISTANCE / max_exact))
                             * np.float32(N_BUCKETS - max_exact)).astype(np.int32)
        bucket = np.where(dist < max_exact, dist, np.minimum(large, N_BUCKETS - 1)).astype(np.int32)
        out[b, 0] = np.where(in_window, bucket, -1)
        out[b, 1] = np.where(in_window & (kj >= BAND), bucket, -1)
    return out


def _attention_bias(rel_bias, tables):
    def body(rel_ref, tab_ref, out_ref):
        head = pl.program_id(0)
        for b in range(3):
            tab = tab_ref[b, 0]

            def pick(kk, acc, tab=tab):
                return jnp.where(tab == kk, rel_ref[kk, head], acc)

            acc = lax.fori_loop(0, N_BUCKETS, pick, jnp.zeros((BAND, 2 * BAND), F32))
            for first in range(2):
                out_ref[0, b, first] = jnp.where(tab_ref[b, first] < 0, NEG_INF, acc)

    return pl.pallas_call(
        body,
        grid=(N_HEADS,),
        in_specs=[pl.BlockSpec(memory_space=pltpu.SMEM),
                  pl.BlockSpec((3, 2, BAND, 2 * BAND), lambda h: (0, 0, 0, 0))],
        out_specs=pl.BlockSpec((1, 3, 2, BAND, 2 * BAND), lambda h: (h, 0, 0, 0, 0)),
        out_shape=jax.ShapeDtypeStruct((N_HEADS, 3, 2, BAND, 2 * BAND), F32),
        compiler_params=_params(("arbitrary",)),
        name="attn_bias",
    )(rel_bias, tables)


def _bias_spec():
    return pl.BlockSpec((2, 3, 2, BAND, 2 * BAND), lambda p, t: (p, 0, 0, 0, 0))


def _attn_block_index(idx, t, r):
    nb = ATT_TILE // (BAND * r)
    rho = idx // nb
    n = idx % nb
    qs = rho + r * BAND * n
    gs = t * ATT_TILE + qs
    first = (t * nb + n) == 0
    ps = jnp.where(first, gs, gs - r * BAND)
    return qs, gs, ps, first.astype(jnp.int32)


def _rows(start, r):
    return pl.ds(start, BAND) if r == 1 else pl.ds(start, BAND, stride=r)


def _attention_fwd(qkv, bias, shards):
    seq = qkv.shape[0]
    n_tiles = seq // ATT_TILE
    n = len(shards)

    def body(*refs):
        bias_ref, q_ref, k_ref, v_ref = refs[:4]
        y_ref, lse_ref = refs[4 + n:6 + n]
        o_s, l_s = refs[6 + 2 * n:8 + 2 * n]
        riding = (refs[4:4 + n], refs[6 + n:6 + 2 * n], *refs[8 + 2 * n:])
        pair = pl.program_id(0)
        t = pl.program_id(1)
        if n:
            @pl.when((pair == 0) & (t == 0))
            def _():
                for cp in _gather_copies(*riding)[0]:
                    cp.start()

            @pl.when((pair == 2) & (t == 0))
            def _():
                for cp, fwd in zip(*_gather_copies(*riding)):
                    cp.wait_recv()
                    fwd.start()

        lane = lax.broadcasted_iota(jnp.int32, (1, LANES), 1)
        head0 = lane < HEAD_DIM
        masks = (head0, jnp.logical_not(head0))
        ones = jnp.ones((2 * BAND, LANES), BF16)
        for b, (_, r) in enumerate(BRANCHES):
            def blocks(it, carry, b=b, r=r):
                idx = [_attn_block_index(it * ATT_UNROLL + j, t, r) for j in range(ATT_UNROLL)]
                qb = [q_ref[_rows(qs, r), :] * (HEAD_DIM ** -0.5) for qs, _, _, _ in idx]
                kcat = [jnp.concatenate([k_ref[_rows(ps, r), :], k_ref[_rows(gs, r), :]], axis=0).astype(BF16)
                        for _, gs, ps, _ in idx]
                vcat = [jnp.concatenate([v_ref[_rows(ps, r), :], v_ref[_rows(gs, r), :]], axis=0).astype(BF16)
                        for _, gs, ps, _ in idx]
                work = [(j, hh) for j in range(ATT_UNROLL) for hh in range(2)]
                s = [_nt(jnp.where(masks[hh], qb[j], 0.0).astype(BF16), kcat[j]) + bias_ref[hh, b, idx[j][3]]
                     for j, hh in work]
                m = [jnp.max(sv, axis=-1, keepdims=True) for sv in s]
                e = [jnp.exp(sv - mv) for sv, mv in zip(s, m)]
                eb = [ev.astype(BF16) for ev in e]
                den = [_nn(ev, ones) for ev in eb]
                out = [_nn(ev, vcat[j]) / dv for ev, dv, (j, _) in zip(eb, den, work)]
                lse = [mv + jnp.log(dv) for mv, dv in zip(m, den)]
                for j in range(ATT_UNROLL):
                    o_s[b, _rows(idx[j][0], r), :] = jnp.where(head0, out[2 * j], out[2 * j + 1])
                    l_s[b, _rows(idx[j][0], r), :] = jnp.where(head0, lse[2 * j], lse[2 * j + 1])
                return carry

            lax.fori_loop(0, ATT_TILE // BAND // ATT_UNROLL, blocks, 0)

        def merge(i, carry):
            rows = pl.ds(pl.multiple_of(i * BAND, BAND), BAND)
            l0, l1, l2 = l_s[0, rows, :], l_s[1, rows, :], l_s[2, rows, :]
            m = jnp.maximum(jnp.maximum(l0, l1), l2)
            w0, w1, w2 = jnp.exp(l0 - m), jnp.exp(l1 - m), jnp.exp(l2 - m)
            tot = w0 + w1 + w2
            y_ref[rows, :] = (w0 * o_s[0, rows, :] + w1 * o_s[1, rows, :] + w2 * o_s[2, rows, :]) / tot
            lse_ref[rows, :] = m + jnp.log(tot)
            return carry

        lax.fori_loop(0, ATT_TILE // BAND, merge, 0)

        if n:
            @pl.when((pair == N_HEADS // 2 - 1) & (t == n_tiles - 1))
            def _():
                first, passed = _gather_copies(*riding)
                for cp in first:
                    cp.wait_send()
                for fwd in passed:
                    fwd.wait()

    tile = pl.BlockSpec((ATT_TILE, LANES), lambda p, t: (t, p))
    sems = [pltpu.SemaphoreType.DMA((6 * n,)), pltpu.SemaphoreType.DMA((6 * n,))] if n else []
    return pl.pallas_call(
        body,
        grid=(N_HEADS // 2, n_tiles),
        in_specs=[
            _bias_spec(),
            pl.BlockSpec((ATT_TILE, LANES), lambda p, t: (t, p)),
            pl.BlockSpec((seq, LANES), lambda p, t: (0, 4 + p)),
            pl.BlockSpec((seq, LANES), lambda p, t: (0, 8 + p)),
        ] + [ANY] * n,
        out_specs=[tile, tile] + [ANY] * n,
        out_shape=[jax.ShapeDtypeStruct((seq, HEAD_W), F32), jax.ShapeDtypeStruct((seq, HEAD_W), F32)]
        + _gathered_shapes(shards),
        scratch_shapes=[
            pltpu.VMEM((3, ATT_TILE, LANES), F32),
            pltpu.VMEM((3, ATT_TILE, LANES), F32),
        ] + sems,
        compiler_params=_params(("arbitrary", "arbitrary")),
        name="attn_fwd",
    )(bias, qkv, qkv, qkv, *shards)


def _attention_bwd(qkv, dy, y, lse, bias, partials):
    seq = qkv.shape[0]
    n_tiles = seq // ATT_TILE
    n = len(partials)

    def body(*refs):
        bias_ref, q_ref, k_ref, v_ref, dy_ref, y_ref, lse_ref = refs[:7]
        dq_ref, dk_ref, dv_ref, dbias_ref = refs[7 + n:11 + n]
        riding = (refs[7:7 + n], refs[11 + n:11 + 2 * n], *refs[11 + 2 * n:])
        pair = pl.program_id(0)
        t = pl.program_id(1)
        if n:
            @pl.when((pair == 0) & (t == 0))
            def _():
                for cp in _scatter_copies(*riding):
                    cp.start()

        lane = lax.broadcasted_iota(jnp.int32, (1, LANES), 1)
        head0 = lane < HEAD_DIM

        @pl.when(t == 0)
        def _():
            dk_ref[...] = jnp.zeros_like(dk_ref)
            dv_ref[...] = jnp.zeros_like(dv_ref)
            dbias_ref[...] = jnp.zeros_like(dbias_ref)

        dq_ref[...] = jnp.zeros_like(dq_ref)

        masks = (head0, jnp.logical_not(head0))
        ones = jnp.ones((LANES, LANES), BF16)
        scale = HEAD_DIM ** -0.5
        for b, (_, r) in enumerate(BRANCHES):
            def blocks(it, carry, b=b, r=r):
                idx = [_attn_block_index(it * ATT_UNROLL + j, t, r) for j in range(ATT_UNROLL)]
                qb = [q_ref[_rows(qs, r), :] * scale for qs, _, _, _ in idx]
                kcat = [jnp.concatenate([k_ref[_rows(ps, r), :], k_ref[_rows(gs, r), :]], axis=0).astype(BF16)
                        for _, gs, ps, _ in idx]
                vcat = [jnp.concatenate([v_ref[_rows(ps, r), :], v_ref[_rows(gs, r), :]], axis=0).astype(BF16)
                        for _, gs, ps, _ in idx]
                dob = [dy_ref[_rows(qs, r), :] for qs, _, _, _ in idx]
                ob = [y_ref[_rows(qs, r), :] for qs, _, _, _ in idx]
                lb = [lse_ref[_rows(qs, r), :] for qs, _, _, _ in idx]
                work = [(j, hh) for j in range(ATT_UNROLL) for hh in range(2)]
                qh = [jnp.where(masks[hh], qb[j], 0.0).astype(BF16) for j, hh in work]
                doh = [jnp.where(masks[hh], dob[j], 0.0) for j, hh in work]
                dohb = [d.astype(BF16) for d in doh]
                s = [_nt(qh[w], kcat[j]) + bias_ref[hh, b, idx[j][3]] for w, (j, hh) in enumerate(work)]
                dp = [_nt(dohb[w], vcat[j]) for w, (j, _) in enumerate(work)]
                lrot = [pltpu.roll(lv, HEAD_DIM, 1) for lv in lb]
                lcol = [jnp.where(masks[hh], lb[j], lrot[j]) for j, hh in work]
                parts = [_split(doh[w] * ob[j]) for w, (j, _) in enumerate(work)]
                delta = [_nn(hi, ones) + _nn(lo, ones) for hi, lo in parts]
                prob = [jnp.exp(sv - jnp.concatenate([lv, lv], axis=1)) for sv, lv in zip(s, lcol)]
                ds = [pv * (dv - jnp.concatenate([de, de], axis=1)) for pv, dv, de in zip(prob, dp, delta)]
                dsb = [d.astype(BF16) for d in ds]
                dq = [_nn(dsb[w], kcat[j]) for w, (j, _) in enumerate(work)]
                dkc = [_tn(dsb[w], qh[w]) for w in range(len(work))]
                dvc = [_tn(prob[w].astype(BF16), dohb[w]) for w in range(len(work))]
                for w, (j, hh) in enumerate(work):
                    dbias_ref[0, b, hh] += ds[w]
                for j in range(ATT_UNROLL):
                    qs, gs, ps, _ = idx[j]
                    dkcat = dkc[2 * j] + dkc[2 * j + 1]
                    dvcat = dvc[2 * j] + dvc[2 * j + 1]
                    dq_ref[_rows(qs, r), :] += jnp.where(head0, dq[2 * j], dq[2 * j + 1]) * scale
                    dk_ref[_rows(ps, r), :] += dkcat[:BAND]
                    dk_ref[_rows(gs, r), :] += dkcat[BAND:]
                    dv_ref[_rows(ps, r), :] += dvcat[:BAND]
                    dv_ref[_rows(gs, r), :] += dvcat[BAND:]
                return carry

            lax.fori_loop(0, ATT_TILE // BAND // ATT_UNROLL, blocks, 0)

        if n:
            @pl.when((pair == N_HEADS // 2 - 1) & (t == n_tiles - 1))
            def _():
                for cp in _scatter_copies(*riding):
                    cp.wait()

    tile = pl.BlockSpec((ATT_TILE, LANES), lambda p, t: (t, p))
    full = pl.BlockSpec((seq, LANES), lambda p, t: (0, p))
    sems = [pltpu.SemaphoreType.DMA((3 * n,)), pltpu.SemaphoreType.DMA((3 * n,))] if n else []
    return pl.pallas_call(
        body,
        grid=(N_HEADS // 2, n_tiles),
        in_specs=[
            _bias_spec(),
            pl.BlockSpec((ATT_TILE, LANES), lambda p, t: (t, p)),
            pl.BlockSpec((seq, LANES), lambda p, t: (0, 4 + p)),
            pl.BlockSpec((seq, LANES), lambda p, t: (0, 8 + p)),
            tile, tile, tile,
        ] + [ANY] * n,
        out_specs=[tile, full, full,
                   pl.BlockSpec((1, 3, 2, BAND, 2 * BAND), lambda p, t: (p, 0, 0, 0, 0))] + [ANY] * n,
        out_shape=[jax.ShapeDtypeStruct((seq, HEAD_W), F32)] * 3
        + [jax.ShapeDtypeStruct((N_HEADS // 2, 3, 2, BAND, 2 * BAND), F32)]
        + [jax.ShapeDtypeStruct(p.shape, p.dtype) for p in partials],
        scratch_shapes=sems,
        compiler_params=_params(("arbitrary", "arbitrary")),
        name="attn_bwd",
    )(bias, qkv, qkv, qkv, dy, y, lse, *partials)


def _rel_bias_grad(dbias, tables):
    def body(tab_ref, db_ref, out_ref):
        lane = lax.broadcasted_iota(jnp.int32, (1, LANES), 1)
        out_ref[...] = jnp.zeros_like(out_ref)
        for b in range(3):
            tab = tab_ref[b, 0]

            def head(h, carry, b=b, tab=tab):
                d = db_ref[h // 2, b, h % 2]
                sums = [jnp.sum(jnp.where(tab == kk, d, 0.0), keepdims=True) for kk in range(N_BUCKETS)]
                row = jnp.zeros((1, LANES), F32)
                for kk, s in enumerate(sums):
                    row = row + jnp.where(lane == kk, s, 0.0)
                out_ref[pl.ds(h, 1), :] += row
                return carry

            lax.fori_loop(0, N_HEADS, head, 0)

    return pl.pallas_call(
        body,
        out_shape=jax.ShapeDtypeStruct((N_HEADS, LANES), F32),
        compiler_params=pltpu.CompilerParams(vmem_limit_bytes=VMEM_LIMIT),
        name="rel_bias_grad",
    )(tables, dbias)


ROW_TILE = 512


def _head_sum_matrix():
    return (lax.broadcasted_iota(jnp.int32, (HEAD_W, LANES), 0) // HEAD_DIM
            == lax.broadcasted_iota(jnp.int32, (HEAD_W, LANES), 1)).astype(F32)


def _head_spread_matrix(offset=0):
    return (lax.broadcasted_iota(jnp.int32, (LANES, HEAD_W), 0)
            == lax.broadcasted_iota(jnp.int32, (LANES, HEAD_W), 1) // HEAD_DIM + offset).astype(F32)


def _head_gather_matrix(offset=0):
    return (lax.broadcasted_iota(jnp.int32, (HEAD_W, LANES), 0) // HEAD_DIM + offset
            == lax.broadcasted_iota(jnp.int32, (HEAD_W, LANES), 1)).astype(F32)


def _split3(x):
    hi = x.astype(BF16)
    rest = x - hi.astype(F32)
    mid = rest.astype(BF16)
    return hi, mid, (rest - mid.astype(F32)).astype(BF16)


def _pick(x, onehot):
    m = onehot.astype(BF16)
    hi, mid, lo = _split3(x)
    return _nn(hi, m) + (_nn(mid, m) + _nn(lo, m))


def _pick_left(onehot, x):
    m = onehot.astype(BF16)
    hi, mid, lo = _split3(x)
    return _nn(m, hi) + (_nn(m, mid) + _nn(m, lo))


def _tri(lower, strict=False):
    r = lax.broadcasted_iota(jnp.int32, (CHUNK, CHUNK), 0)
    c = lax.broadcasted_iota(jnp.int32, (CHUNK, CHUNK), 1)
    if lower:
        return (c < r) if strict else (c <= r)
    return c >= r


def _softplus(z):
    return jnp.maximum(z, 0.0) + jnp.log(1.0 + jnp.exp(-jnp.abs(z)))


def _conv_taps(stage, w_ref, rows):
    return (w_ref[3:4, :] * stage[8:8 + rows, :] + w_ref[2:3, :] * stage[7:7 + rows, :]
            + w_ref[1:2, :] * stage[6:6 + rows, :] + w_ref[0:1, :] * stage[5:5 + rows, :])


def _l2_scale(xc, hsum, hspread):
    ssq = _pick(xc * xc, hsum)
    return _pick(lax.rsqrt(ssq + EPS), hspread)


def _stage_rows(stage, x_ref, xp_ref, i):
    stage[0:8, :] = jnp.where(i == 0, 0.0, xp_ref[...])
    stage[8:8 + ROW_TILE, :] = x_ref[...]


def _delta_prep_fwd(qkvz, ba, conv_w, alog_row, dt_row):
    seq = qkvz.shape[0]
    qkv_w = 3 * HEAD_W

    def body(x_ref, xp_ref, ba_ref, w_ref, al_ref, dt_ref, out_ref, stage):
        i = pl.program_id(0)
        _stage_rows(stage, x_ref, xp_ref, i)
        act = _silu(_conv_taps(stage, w_ref, ROW_TILE))
        hsum, hspread = _head_sum_matrix(), _head_spread_matrix()
        qc, kc = act[:, :HEAD_W], act[:, HEAD_W:2 * HEAD_W]
        out_ref[0] = qc * _l2_scale(qc, hsum, hspread) * (HEAD_DIM ** -0.5)
        out_ref[1] = kc * _l2_scale(kc, hsum, hspread)
        out_ref[2] = act[:, 2 * HEAD_W:]
        bav = ba_ref[...]
        out_ref[3] = _pick(_sigmoid(bav), hspread)
        g8 = -jnp.exp(al_ref[...]) * _softplus(bav + dt_ref[...])
        gb = _pick(g8, _head_spread_matrix(N_HEADS))
        cum = _tri(True).astype(F32)
        for ch in range(ROW_TILE // CHUNK):
            rows = slice(ch * CHUNK, (ch + 1) * CHUNK)
            out_ref[4, rows, :] = _pick_left(cum, gb[rows])

    return pl.pallas_call(
        body,
        grid=(seq // ROW_TILE,),
        in_specs=[
            pl.BlockSpec((ROW_TILE, qkv_w), lambda i: (i, 0)),
            pl.BlockSpec((8, qkv_w), lambda i: (jnp.maximum(i * (ROW_TILE // 8) - 1, 0), 0)),
            pl.BlockSpec((ROW_TILE, LANES), lambda i: (i, 0)),
            pl.BlockSpec((4, qkv_w), lambda i: (0, 0)),
            pl.BlockSpec((1, LANES), lambda i: (0, 0)),
            pl.BlockSpec((1, LANES), lambda i: (0, 0)),
        ],
        out_specs=pl.BlockSpec((5, ROW_TILE, HEAD_W), lambda i: (0, i, 0)),
        out_shape=jax.ShapeDtypeStruct((5, seq, HEAD_W), F32),
        scratch_shapes=[pltpu.VMEM((ROW_TILE + 8, qkv_w), F32)],
        compiler_params=_params(("arbitrary",)),
        name="delta_prep_fwd",
    )(qkvz, qkvz, ba, conv_w, alog_row, dt_row)


def _split(x):
    hi = x.astype(BF16)
    return hi, (x - hi.astype(F32)).astype(BF16)


def _dot3(a, b, dot=_nn):
    return dot(a[0], b[0]) + (dot(a[0], b[1]) + dot(a[1], b[0]))


def _unit_lower_inverses(mats):
    eye = (lax.broadcasted_iota(jnp.int32, (CHUNK, CHUNK), 0)
           == lax.broadcasted_iota(jnp.int32, (CHUNK, CHUNK), 1)).astype(F32)
    invs = [eye - a for a in mats]
    powers = [_split(a) for a in mats]
    for step in range(5):
        squares = [_dot3(p, p) for p in powers]
        powers = [_split(s) for s in squares]
        invs = [inv + _dot3(_split(inv), p) for inv, p in zip(invs, powers)]
    return invs


def _chunk_terms(q, k, v, beta, gc):
    causal, strict = _tri(True), _tri(True, strict=True)
    e = jnp.exp(gc)
    g_last = jnp.broadcast_to(gc[CHUNK - 1:CHUNK, :], (CHUNK, CHUNK))
    f = jnp.exp(g_last - gc)
    e_last = jnp.exp(g_last)
    decay = jnp.where(causal, jnp.exp(jnp.where(causal, gc - gc.T, 0.0)), 0.0)
    kb = k * beta
    a_mat = jnp.where(strict, _nt(kb.astype(BF16), k.astype(BF16)) * decay, 0.0)
    qk = jnp.where(causal, _nt(q.astype(BF16), k.astype(BF16)) * decay, 0.0)
    return e, f, e_last, decay, kb, a_mat, qk


GROUP = 8
UNROLL = 8


def _chunk_rows(ci):
    return pl.ds(pl.multiple_of(ci * CHUNK, CHUNK), CHUNK)


def _pair_specs(n_planes):
    return pl.BlockSpec((n_planes, GROUP * CHUNK, LANES), lambda p, g: (0, g, p))


def _delta_chunk_fwd(xs):
    seq = xs.shape[1]
    rows_per_step = GROUP * CHUNK

    def body(x_ref, inv_ref, qk_ref, u_ref, w_ref):
        for hh in range(2):
            lanes = slice(hh * HEAD_DIM, (hh + 1) * HEAD_DIM)
            rows = [slice(step * CHUNK, (step + 1) * CHUNK) for step in range(GROUP)]
            xh = [[x_ref[j, r, lanes] for j in range(5)] for r in rows]
            terms = [_chunk_terms(*x) for x in xh]
            invs = _unit_lower_inverses([t[5] for t in terms])
            for r, x, t, inv in zip(rows, xh, terms, invs):
                e, kb, qk = t[0], t[4], t[6]
                inv_parts = _split(inv)
                inv_ref[hh, r, :] = inv
                qk_ref[hh, r, :] = qk
                u_ref[hh, r, :] = _dot3(inv_parts, _split(x[2] * x[3]))
                w_ref[hh, r, :] = _dot3(inv_parts, _split(kb * e))

    out = pl.BlockSpec((2, rows_per_step, HEAD_DIM), lambda p, g: (p, g, 0))
    return pl.pallas_call(
        body,
        grid=(N_HEADS // 2, seq // rows_per_step),
        in_specs=[_pair_specs(5)],
        out_specs=[out] * 4,
        out_shape=[jax.ShapeDtypeStruct((N_HEADS, seq, HEAD_DIM), F32)] * 4,
        compiler_params=_params(("parallel", "parallel")),
        name="delta_chunk_fwd",
    )(xs)


def _decays(gc):
    g_last = jnp.broadcast_to(gc[CHUNK - 1:CHUNK, :], (CHUNK, CHUNK))
    return jnp.exp(gc), jnp.exp(g_last - gc), jnp.exp(g_last)


def _token_blocks(index, n_steps=None):
    rows_per_step = GROUP * CHUNK
    if n_steps is None:
        return pl.BlockSpec((1, rows_per_step, HEAD_W), lambda g: (index, g, 0))
    return pl.BlockSpec((1, rows_per_step, HEAD_W), lambda g: (index, n_steps - 1 - g, 0))


def _head_lanes(h):
    return pl.ds(h * HEAD_DIM, HEAD_DIM)


def _delta_scan_fwd(xs, qk_h, u_h, w_h):
    seq = xs.shape[1]
    rows_per_step = GROUP * CHUNK

    def body(q_ref, k_ref, gc_ref, qk_ref, u_ref, w_ref, o_ref, st_ref, state):
        @pl.when(pl.program_id(0) == 0)
        def _():
            state[...] = jnp.zeros_like(state)

        def chunk(ci, carry):
            rows = _chunk_rows(ci)
            heads = range(N_HEADS)
            dec = [_decays(gc_ref[0, rows, _head_lanes(h)]) for h in heads]
            s = [state[h] for h in heads]
            sb = [s[h].astype(BF16) for h in heads]
            vnb = [(u_ref[h, rows, :] - _nn(w_ref[h, rows, :].astype(BF16), sb[h])).astype(BF16) for h in heads]
            for h in heads:
                o_ref[rows, _head_lanes(h)] = (_nn((q_ref[0, rows, _head_lanes(h)] * dec[h][0]).astype(BF16), sb[h])
                                               + _nn(qk_ref[h, rows, :].astype(BF16), vnb[h]))
                st_ref[h, rows, :] = s[h]
            for h in heads:
                state[h] = s[h] * dec[h][2] + _tn((k_ref[0, rows, _head_lanes(h)] * dec[h][1]).astype(BF16), vnb[h])
            return carry

        lax.fori_loop(0, GROUP, chunk, 0)

    blk = pl.BlockSpec((N_HEADS, rows_per_step, HEAD_DIM), lambda g: (0, g, 0))
    return pl.pallas_call(
        body,
        grid=(seq // rows_per_step,),
        in_specs=[_token_blocks(0), _token_blocks(1), _token_blocks(4), blk, blk, blk],
        out_specs=[pl.BlockSpec((rows_per_step, HEAD_W), lambda g: (g, 0)), blk],
        out_shape=[jax.ShapeDtypeStruct((seq, HEAD_W), F32), jax.ShapeDtypeStruct((N_HEADS, seq, HEAD_DIM), F32)],
        scratch_shapes=[pltpu.VMEM((N_HEADS, CHUNK, CHUNK), F32)],
        compiler_params=_params(("arbitrary",)),
        name="delta_scan_fwd",
    )(xs, xs, xs, qk_h, u_h, w_h)


def _delta_scan_bwd(xs, qk_h, w_h, do):
    seq = xs.shape[1]
    rows_per_step = GROUP * CHUNK
    n_steps = seq // rows_per_step

    def body(q_ref, k_ref, gc_ref, qk_ref, w_ref, do_ref, dsn_ref, dvn_ref, dstate):
        @pl.when(pl.program_id(0) == 0)
        def _():
            dstate[...] = jnp.zeros_like(dstate)

        def chunk(step, carry):
            rows = _chunk_rows(GROUP - 1 - step)
            heads = range(N_HEADS)
            dec = [_decays(gc_ref[0, rows, _head_lanes(h)]) for h in heads]
            ds_next = [dstate[h] for h in heads]
            dob = [do_ref[rows, _head_lanes(h)].astype(BF16) for h in heads]
            dv_new = [_tn(qk_ref[h, rows, :].astype(BF16), dob[h])
                      + _nn((k_ref[0, rows, _head_lanes(h)] * dec[h][1]).astype(BF16), ds_next[h].astype(BF16))
                      for h in heads]
            for h in heads:
                dsn_ref[h, rows, :] = ds_next[h]
                dvn_ref[h, rows, :] = dv_new[h]
            for h in heads:
                dstate[h] = (_tn((q_ref[0, rows, _head_lanes(h)] * dec[h][0]).astype(BF16), dob[h])
                             + dec[h][2] * ds_next[h] - _tn(w_ref[h, rows, :].astype(BF16), dv_new[h].astype(BF16)))
            return carry

        lax.fori_loop(0, GROUP, chunk, 0)

    blk = pl.BlockSpec((N_HEADS, rows_per_step, HEAD_DIM), lambda g: (0, n_steps - 1 - g, 0))
    return pl.pallas_call(
        body,
        grid=(n_steps,),
        in_specs=[_token_blocks(0, n_steps), _token_blocks(1, n_steps), _token_blocks(4, n_steps), blk, blk,
                  pl.BlockSpec((rows_per_step, HEAD_W), lambda g: (n_steps - 1 - g, 0))],
        out_specs=[blk, blk],
        out_shape=[jax.ShapeDtypeStruct((N_HEADS, seq, HEAD_DIM), F32)] * 2,
        scratch_shapes=[pltpu.VMEM((N_HEADS, CHUNK, CHUNK), F32)],
        compiler_params=_params(("arbitrary",)),
        name="delta_scan_bwd",
    )(xs, xs, xs, qk_h, w_h, do)


def _delta_chunk_bwd(xs, inv_h, u_h, w_h, st_h, dsn_h, dvn_h, do):
    seq = xs.shape[1]
    rows_per_step = GROUP * CHUNK

    def body(x_ref, inv_ref, u_ref, w_ref, st_ref, dsn_ref, dvn_ref, do_ref, dx_ref):
        causal, strict = _tri(True), _tri(True, strict=True)
        last_row = lax.broadcasted_iota(jnp.int32, (CHUNK, CHUNK), 0) == CHUNK - 1

        def bf(vals):
            return [val.astype(BF16) for val in vals]

        def group(hh, first):
            lanes = slice(hh * HEAD_DIM, (hh + 1) * HEAD_DIM)
            rows = [slice(step * CHUNK, (step + 1) * CHUNK) for step in range(first, first + UNROLL)]
            n = range(UNROLL)
            q, k, v, beta, gc = [[x_ref[j, r, lanes] for r in rows] for j in range(5)]
            terms = [_chunk_terms(q[i], k[i], v[i], beta[i], gc[i]) for i in n]
            e, f, e_last, decay, kb, a_mat, qk = [[t[j] for t in terms] for j in range(7)]
            inv = [_split(inv_ref[hh, r, :]) for r in rows]
            u = [u_ref[hh, r, :] for r in rows]
            w = [w_ref[hh, r, :] for r in rows]
            s = [st_ref[hh, r, :] for r in rows]
            ds_next = [dsn_ref[hh, r, :] for r in rows]
            dv_new = [dvn_ref[hh, r, :] for r in rows]
            sb, dsb, dvb, wb = bf(s), bf(ds_next), bf(dv_new), bf(w)
            dob = bf([do_ref[r, lanes] for r in rows])
            qbf, kbf, kbb = bf(q), bf(k), bf(kb)
            vnb = bf([u[i] - _nn(wb[i], sb[i]) for i in n])
            dqe = [_nt(dob[i], sb[i]) for i in n]
            dw = [-_nt(dvb[i], sb[i]) for i in n]
            dkf = [_nt(vnb[i], dsb[i]) for i in n]
            dqk = [jnp.where(causal, _nt(dob[i], vnb[i]), 0.0) for i in n]
            drhs_u = [_dot3(inv[i], _split(dv_new[i]), _tn) for i in n]
            drhs_w = [_dot3(inv[i], _split(dw[i]), _tn) for i in n]
            da = [-jnp.where(strict, _nt(drhs_u[i].astype(BF16), u[i].astype(BF16))
                             + _nt(drhs_w[i].astype(BF16), wb[i]), 0.0) for i in n]
            dad = bf([da[i] * decay[i] for i in n])
            dqd = bf([dqk[i] * decay[i] for i in n])
            dkb = [e[i] * drhs_w[i] + _nn(dad[i], kbf[i]) for i in n]
            dk = [_tn(dad[i], kbb[i]) + _tn(dqd[i], qbf[i]) + f[i] * dkf[i] + beta[i] * dkb[i] for i in n]
            dq = [_nn(dqd[i], kbf[i]) + e[i] * dqe[i] for i in n]
            for i in n:
                de_full = kb[i] * drhs_w[i] + q[i] * dqe[i]
                df_full = k[i] * dkf[i]
                m = da[i] * a_mat[i] + dqk[i] * qk[i]
                dgc = de_full * e[i] - df_full * f[i] + m - m.T
                tail = jnp.sum(df_full * f[i] + s[i] * ds_next[i] * e_last[i], axis=0, keepdims=True)
                dgc = dgc + jnp.where(last_row, jnp.broadcast_to(tail, (CHUNK, CHUNK)), 0.0)
                dx_ref[0, rows[i], lanes] = dq[i]
                dx_ref[1, rows[i], lanes] = dk[i]
                dx_ref[2, rows[i], lanes] = beta[i] * drhs_u[i]
                dx_ref[3, rows[i], lanes] = v[i] * drhs_u[i] + k[i] * dkb[i]
                dx_ref[4, rows[i], lanes] = dgc

        for hh in range(2):
            for first in range(0, GROUP, UNROLL):
                group(hh, first)

    blk = pl.BlockSpec((2, rows_per_step, HEAD_DIM), lambda p, g: (p, g, 0))
    return pl.pallas_call(
        body,
        grid=(N_HEADS // 2, seq // rows_per_step),
        in_specs=[_pair_specs(5)] + [blk] * 6 + [pl.BlockSpec((rows_per_step, LANES), lambda p, g: (g, p))],
        out_specs=_pair_specs(5),
        out_shape=jax.ShapeDtypeStruct((5, seq, HEAD_W), F32),
        compiler_params=_params(("parallel", "parallel")),
        name="delta_chunk_bwd",
    )(xs, inv_h, u_h, w_h, st_h, dsn_h, dvn_h, do)


def _delta_post_fwd(o, qkvz, gain_row):
    seq = o.shape[0]

    def body(o_ref, z_ref, g_ref, y_ref):
        ov = o_ref[...]
        ms = _pick(ov * ov, _head_sum_matrix()) * (1.0 / HEAD_DIM)
        rb = _pick(lax.rsqrt(ms + EPS), _head_spread_matrix())
        y_ref[...] = (ov * rb * g_ref[...] * _silu(z_ref[...])).astype(y_ref.dtype)

    tile = pl.BlockSpec((ROW_TILE, HEAD_W), lambda i: (i, 0))
    return pl.pallas_call(
        body,
        grid=(seq // ROW_TILE,),
        in_specs=[tile, pl.BlockSpec((ROW_TILE, HEAD_W), lambda i: (i, 3)), pl.BlockSpec((1, HEAD_W), lambda i: (0, 0))],
        out_specs=tile,
        out_shape=jax.ShapeDtypeStruct((seq, HEAD_W), BF16),
        compiler_params=_params(("arbitrary",)),
        name="delta_post_fwd",
    )(o, qkvz, gain_row)


def _delta_post_bwd(dy, o, qkvz, gain_row):
    seq = o.shape[0]

    def body(dy_ref, o_ref, z_ref, g_ref, do_ref, dz_ref, dg_ref):
        @pl.when(pl.program_id(0) == 0)
        def _():
            dg_ref[...] = jnp.zeros_like(dg_ref)

        ov, zv, dyv, gain = o_ref[...], z_ref[...], dy_ref[...], g_ref[...]
        hsum, hspread = _head_sum_matrix(), _head_spread_matrix()
        ms = _pick(ov * ov, hsum) * (1.0 / HEAD_DIM)
        rb = _pick(lax.rsqrt(ms + EPS), hspread)
        ohat = ov * rb
        silu_z, slope_z = _silu_and_slope(zv)
        dz_ref[...] = dyv * ohat * gain * slope_z
        dn = dyv * silu_z
        dg_ref[0:1, :] += jnp.sum(dn * ohat, axis=0, keepdims=True)
        dohat = dn * gain

        @pl.when(pl.program_id(0) == pl.num_programs(0) - 1)
        def _():
            fold = (lax.broadcasted_iota(jnp.int32, (HEAD_W, HEAD_W), 0) % HEAD_DIM
                    == lax.broadcasted_iota(jnp.int32, (HEAD_W, HEAD_W), 1)).astype(F32)
            dg_ref[1:2, :] = _pick(dg_ref[0:1, :], fold)

        proj = _pick(_pick(dohat * ohat, hsum) * (1.0 / HEAD_DIM), hspread)
        do_ref[...] = rb * (dohat - ohat * proj)

    tile = pl.BlockSpec((ROW_TILE, HEAD_W), lambda i: (i, 0))
    return pl.pallas_call(
        body,
        grid=(seq // ROW_TILE,),
        in_specs=[pl.BlockSpec((ROW_TILE, HEAD_W), lambda i: (i, 1)), tile,
                  pl.BlockSpec((ROW_TILE, HEAD_W), lambda i: (i, 3)), pl.BlockSpec((1, HEAD_W), lambda i: (0, 0))],
        out_specs=[tile, tile, pl.BlockSpec((2, HEAD_W), lambda i: (0, 0))],
        out_shape=[jax.ShapeDtypeStruct((seq, HEAD_W), F32), jax.ShapeDtypeStruct((seq, HEAD_W), F32),
                   jax.ShapeDtypeStruct((2, HEAD_W), F32)],
        compiler_params=_params(("arbitrary",)),
        name="delta_post_bwd",
    )(dy, o, qkvz, gain_row)


def _delta_prep_bwd(qkvz, ba, conv_w, alog_row, dt_row, dxs):
    seq = qkvz.shape[0]
    qkv_w = 3 * HEAD_W

    def body(x_ref, xp_ref, ba_ref, w_ref, al_ref, dt_ref, dx_ref, dconv_ref, dba_ref, dvec_ref, stage):
        i = pl.program_id(0)

        @pl.when(i == 0)
        def _():
            dvec_ref[...] = jnp.zeros_like(dvec_ref)

        _stage_rows(stage, x_ref, xp_ref, i)
        pre = _conv_taps(stage, w_ref, ROW_TILE)
        act, slope = _silu_and_slope(pre)
        hsum, hspread = _head_sum_matrix(), _head_spread_matrix()
        for j, scale in ((0, HEAD_DIM ** -0.5), (1, 1.0)):
            cols = slice(j * HEAD_W, (j + 1) * HEAD_W)
            xc = act[:, cols]
            rb = _l2_scale(xc, hsum, hspread)
            xhat = xc * rb
            dhat = dx_ref[j] * scale
            proj = _pick(_pick(dhat * xhat, hsum), hspread)
            dconv_ref[:, cols] = rb * (dhat - xhat * proj) * slope[:, cols]
        dconv_ref[:, 2 * HEAD_W:] = dx_ref[2] * slope[:, 2 * HEAD_W:]

        bav = ba_ref[...]
        beta8 = _sigmoid(bav)
        dbeta8 = _pick(dx_ref[3], _head_gather_matrix())
        dgc8 = _pick(dx_ref[4], _head_gather_matrix(N_HEADS))
        rev = _tri(False).astype(F32)
        z = bav + dt_ref[...]
        ea = jnp.exp(al_ref[...])
        g8 = -ea * _softplus(z)
        sig = _sigmoid(z)
        d_alog = jnp.zeros((1, LANES), F32)
        d_dt = jnp.zeros((1, LANES), F32)
        for ch in range(ROW_TILE // CHUNK):
            rows = slice(ch * CHUNK, (ch + 1) * CHUNK)
            dg8 = _pick_left(rev, dgc8[rows])
            da = -dg8 * ea * sig[rows]
            dba_ref[rows, :] = dbeta8[rows] * beta8[rows] * (1.0 - beta8[rows]) + da
            d_alog = d_alog + jnp.sum(dg8 * g8[rows], axis=0, keepdims=True)
            d_dt = d_dt + jnp.sum(da, axis=0, keepdims=True)
        dvec_ref[0:1, :] += d_alog
        dvec_ref[1:2, :] += d_dt

    return pl.pallas_call(
        body,
        grid=(seq // ROW_TILE,),
        in_specs=[
            pl.BlockSpec((ROW_TILE, qkv_w), lambda i: (i, 0)),
            pl.BlockSpec((8, qkv_w), lambda i: (jnp.maximum(i * (ROW_TILE // 8) - 1, 0), 0)),
            pl.BlockSpec((ROW_TILE, LANES), lambda i: (i, 0)),
            pl.BlockSpec((4, qkv_w), lambda i: (0, 0)),
            pl.BlockSpec((1, LANES), lambda i: (0, 0)),
            pl.BlockSpec((1, LANES), lambda i: (0, 0)),
            pl.BlockSpec((5, ROW_TILE, HEAD_W), lambda i: (0, i, 0)),
        ],
        out_specs=[pl.BlockSpec((ROW_TILE, qkv_w), lambda i: (i, 0)),
                   pl.BlockSpec((ROW_TILE, LANES), lambda i: (i, 0)),
                   pl.BlockSpec((2, LANES), lambda i: (0, 0))],
        out_shape=[jax.ShapeDtypeStruct((seq, qkv_w), F32), jax.ShapeDtypeStruct((seq, LANES), F32),
                   jax.ShapeDtypeStruct((2, LANES), F32)],
        scratch_shapes=[pltpu.VMEM((ROW_TILE + 8, qkv_w), F32)],
        compiler_params=_params(("arbitrary",)),
        name="delta_prep_bwd",
    )(qkvz, qkvz, ba, conv_w, alog_row, dt_row, dxs)


def _conv_bwd(dconv, qkvz, conv_w):
    seq = dconv.shape[0]
    qkv_w = 3 * HEAD_W
    n_tiles = seq // ROW_TILE

    def body(dy_ref, dyn_ref, x_ref, xp_ref, w_ref, dx_ref, dw_ref, stage, dstage):
        i = pl.program_id(0)

        @pl.when(i == 0)
        def _():
            dw_ref[...] = jnp.zeros_like(dw_ref)

        _stage_rows(stage, x_ref, xp_ref, i)
        dstage[0:ROW_TILE, :] = dy_ref[...]
        dstage[ROW_TILE:ROW_TILE + 8, :] = jnp.where(i == n_tiles - 1, 0.0, dyn_ref[...])
        dy = dy_ref[...]
        dx_ref[...] = (w_ref[3:4, :] * dy + w_ref[2:3, :] * dstage[1:1 + ROW_TILE, :]
                       + w_ref[1:2, :] * dstage[2:2 + ROW_TILE, :] + w_ref[0:1, :] * dstage[3:3 + ROW_TILE, :])
        for j in range(4):
            dw_ref[j:j + 1, :] += jnp.sum(dy * stage[5 + j:5 + j + ROW_TILE, :], axis=0, keepdims=True)

    tile = pl.BlockSpec((ROW_TILE, qkv_w), lambda i: (i, 0))
    return pl.pallas_call(
        body,
        grid=(n_tiles,),
        in_specs=[
            tile,
            pl.BlockSpec((8, qkv_w), lambda i: (jnp.minimum((i + 1) * (ROW_TILE // 8), seq // 8 - 1), 0)),
            tile,
            pl.BlockSpec((8, qkv_w), lambda i: (jnp.maximum(i * (ROW_TILE // 8) - 1, 0), 0)),
            pl.BlockSpec((4, qkv_w), lambda i: (0, 0)),
        ],
        out_specs=[tile, pl.BlockSpec((4, qkv_w), lambda i: (0, 0))],
        out_shape=[jax.ShapeDtypeStruct((seq, qkv_w), F32), jax.ShapeDtypeStruct((4, qkv_w), F32)],
        scratch_shapes=[pltpu.VMEM((ROW_TILE + 8, qkv_w), F32), pltpu.VMEM((ROW_TILE + 8, qkv_w), F32)],
        compiler_params=_params(("arbitrary",)),
        name="conv_bwd",
    )(dconv, dconv, qkvz, qkvz, conv_w)


FF_TILE = 1408


def _row(a):
    return pl.BlockSpec((1, a), lambda *_: (0, 0))


def _rms_fwd(xv, gain):
    rstd = lax.rsqrt(jnp.mean(xv * xv, axis=-1, keepdims=True) + EPS)
    xhat = xv * rstd
    return xhat, rstd, xhat * gain


def _rms_bwd(dnorm, xhat, rstd, gain):
    dxhat = dnorm * gain
    dx = rstd * (dxhat - xhat * jnp.mean(dxhat * xhat, axis=-1, keepdims=True))
    return dx, jnp.sum(dnorm * xhat, axis=0, keepdims=True)


def _inproj_fwd(x, gain, scale, shift, w_a, w_d, w_ba):
    seq = x.shape[0]

    def body(x_ref, g_ref, sc_ref, sh_ref, wa_ref, wd_ref, wb_ref, h_ref, a_ref, d_ref, b_ref):
        _, _, norm = _rms_fwd(x_ref[...], g_ref[...])
        h = (norm * (1.0 + sc_ref[...]) + sh_ref[...]).astype(BF16)
        h_ref[...] = h
        a_ref[...] = _nn(h, wa_ref[...])
        d_ref[...] = _nn(h, wd_ref[...])
        b_ref[...] = _nn(h, wb_ref[...])

    def rows(width):
        return pl.BlockSpec((ROW_TILE, width), lambda i: (i, 0))

    def whole(a):
        return pl.BlockSpec(a.shape, lambda i: (0, 0))

    return pl.pallas_call(
        body,
        grid=(seq // ROW_TILE,),
        in_specs=[rows(D_MODEL), _row(D_MODEL), _row(D_MODEL), _row(D_MODEL), whole(w_a), whole(w_d), whole(w_ba)],
        out_specs=[rows(D_MODEL), rows(3 * HEAD_W), rows(4 * HEAD_W), rows(LANES)],
        out_shape=[jax.ShapeDtypeStruct((seq, D_MODEL), BF16), jax.ShapeDtypeStruct((seq, 3 * HEAD_W), F32),
                   jax.ShapeDtypeStruct((seq, 4 * HEAD_W), F32), jax.ShapeDtypeStruct((seq, LANES), F32)],
        compiler_params=_params(("arbitrary",)),
        name="inproj_fwd",
    )(x, gain, scale, shift, w_a, w_d, w_ba)


def _outproj_fwd(y_attn, y_delta, w_out, x, gate1, gain, scale, shift):
    seq = x.shape[0]

    def body(ya_ref, yd_ref, wa_ref, wd_ref, x_ref, g1_ref, g_ref, sc_ref, sh_ref, x1_ref, h_ref, y_ref):
        y = _nn(ya_ref[...].astype(BF16), wa_ref[...]) + _nn(yd_ref[...], wd_ref[...])
        x1 = x_ref[...] + g1_ref[...] * y
        _, _, norm = _rms_fwd(x1, g_ref[...])
        x1_ref[...] = x1
        h_ref[...] = (norm * (1.0 + sc_ref[...]) + sh_ref[...]).astype(BF16)
        y_ref[...] = y.astype(BF16)

    def rows(width):
        return pl.BlockSpec((ROW_TILE, width), lambda i: (i, 0))

    return pl.pallas_call(
        body,
        grid=(seq // ROW_TILE,),
        in_specs=[rows(HEAD_W), rows(HEAD_W),
                  pl.BlockSpec((HEAD_W, D_MODEL), lambda i: (0, 0)), pl.BlockSpec((HEAD_W, D_MODEL), lambda i: (1, 0)),
                  rows(D_MODEL), _row(D_MODEL), _row(D_MODEL), _row(D_MODEL), _row(D_MODEL)],
        out_specs=[rows(D_MODEL), rows(D_MODEL), rows(D_MODEL)],
        out_shape=[jax.ShapeDtypeStruct((seq, D_MODEL), F32), jax.ShapeDtypeStruct((seq, D_MODEL), BF16),
                   jax.ShapeDtypeStruct((seq, D_MODEL), BF16)],
        compiler_params=_params(("arbitrary",)),
        name="outproj_fwd",
    )(y_attn, y_delta, w_out, w_out, x, gate1, gain, scale, shift)


def _ffn_fwd(h2, w_gate, w_up, w_down, x1, gate2, final_gain, target):
    seq = h2.shape[0]
    n_rows, n_ff = seq // ROW_TILE, D_FF // FF_TILE

    def body(h_ref, wg_ref, wu_ref, wd_ref, x1_ref, g2_ref, gf_ref, t_ref, gate_ref, up_ref, dx2_ref, st_ref, acc):
        i, j = pl.program_id(0), pl.program_id(1)

        @pl.when((i == 0) & (j == 0))
        def _():
            st_ref[...] = jnp.zeros_like(st_ref)

        h = h_ref[...]
        gate = _nn(h, wg_ref[...])
        up = _nn(h, wu_ref[...])
        gate_ref[...] = gate.astype(BF16)
        up_ref[...] = up.astype(BF16)
        part = _nn((_silu(gate) * up).astype(BF16), wd_ref[...])

        @pl.when(j == 0)
        def _():
            acc[...] = part

        @pl.when(j > 0)
        def _():
            acc[...] += part

        @pl.when(j == n_ff - 1)
        def _():
            y2 = acc[...]
            x2 = x1_ref[...] + g2_ref[...] * y2
            xhat, rstd, out = _rms_fwd(x2, gf_ref[...])
            diff = out - t_ref[...]
            dx2, dgain = _rms_bwd(diff * (1.0 / D_MODEL), xhat, rstd, gf_ref[...])
            dx2_ref[...] = dx2
            st_ref[0:1, :] += dgain
            st_ref[1:2, :] += jnp.sum(dx2 * y2, axis=0, keepdims=True)
            st_ref[2:3, :] += jnp.sum(diff * diff, axis=0, keepdims=True) * (0.5 / D_MODEL)

        @pl.when((i == n_rows - 1) & (j == n_ff - 1))
        def _():
            st_ref[3:4, :] = jnp.broadcast_to(jnp.sum(st_ref[2:3, :], keepdims=True), (1, D_MODEL))

    def rows(width):
        return pl.BlockSpec((ROW_TILE, width), lambda i, j: (i, 0))

    ff = pl.BlockSpec((ROW_TILE, FF_TILE), lambda i, j: (i, j))
    return pl.pallas_call(
        body,
        grid=(n_rows, n_ff),
        in_specs=[rows(D_MODEL),
                  pl.BlockSpec((D_MODEL, FF_TILE), lambda i, j: (0, j)), pl.BlockSpec((D_MODEL, FF_TILE), lambda i, j: (0, j)),
                  pl.BlockSpec((FF_TILE, D_MODEL), lambda i, j: (j, 0)),
                  rows(D_MODEL), _row(D_MODEL), _row(D_MODEL), rows(D_MODEL)],
        out_specs=[ff, ff, rows(D_MODEL), pl.BlockSpec((8, D_MODEL), lambda i, j: (0, 0))],
        out_shape=[jax.ShapeDtypeStruct((seq, D_FF), BF16), jax.ShapeDtypeStruct((seq, D_FF), BF16),
                   jax.ShapeDtypeStruct((seq, D_MODEL), F32), jax.ShapeDtypeStruct((8, D_MODEL), F32)],
        scratch_shapes=[pltpu.VMEM((ROW_TILE, D_MODEL), F32)],
        compiler_params=_params(("arbitrary", "arbitrary")),
        name="ffn_fwd",
    )(h2, w_gate, w_up, w_down, x1, gate2, final_gain, target)


def _ffn_bwd(dx2, gate, up, w_gate, w_up, w_down, x1, y, gate2, gate1, gain, scale):
    seq = dx2.shape[0]

    def act_body(dx2_ref, g2_ref, gate_ref, up_ref, wd_ref, dgate_ref, dup_ref, act_ref, dy2_ref):
        dy2 = (g2_ref[...] * dx2_ref[...]).astype(BF16)
        dy2_ref[...] = dy2
        gate = gate_ref[...].astype(F32)
        up = up_ref[...].astype(F32)
        dact = _nt(dy2, wd_ref[...])
        silu, slope = _silu_and_slope(gate)
        act_ref[...] = (silu * up).astype(BF16)
        dgate_ref[...] = (dact * up * slope).astype(BF16)
        dup_ref[...] = (dact * silu).astype(BF16)

    def rows2(width):
        return pl.BlockSpec((ROW_TILE, width), lambda i, j: (i, 0))

    ff = pl.BlockSpec((ROW_TILE, FF_TILE), lambda i, j: (i, j))
    dgate, dup, act, dy2 = pl.pallas_call(
        act_body,
        grid=(seq // ROW_TILE, D_FF // FF_TILE),
        in_specs=[rows2(D_MODEL), _row(D_MODEL), ff, ff, pl.BlockSpec((FF_TILE, D_MODEL), lambda i, j: (j, 0))],
        out_specs=[ff, ff, ff, rows2(D_MODEL)],
        out_shape=[jax.ShapeDtypeStruct((seq, D_FF), BF16)] * 3 + [jax.ShapeDtypeStruct((seq, D_MODEL), BF16)],
        compiler_params=_params(("arbitrary", "arbitrary")),
        name="ffn_bwd_act",
    )(dx2, gate2, gate, up, w_down)

    def in_body(dgate_ref, dup_ref, wg_ref, wu_ref, dx2_ref, x1_ref, y_ref, g1_ref, g_ref, sc_ref,
                dx1_ref, dy_ref, st_ref):
        @pl.when(pl.program_id(0) == 0)
        def _():
            st_ref[...] = jnp.zeros_like(st_ref)

        dh = _nt(dgate_ref[...], wg_ref[...]) + _nt(dup_ref[...], wu_ref[...])
        xhat, rstd, norm = _rms_fwd(x1_ref[...], g_ref[...])
        dxn, dgain = _rms_bwd(dh * (1.0 + sc_ref[...]), xhat, rstd, g_ref[...])
        dx1 = dx2_ref[...] + dxn
        dx1_ref[...] = dx1
        dy_ref[...] = (g1_ref[...] * dx1).astype(BF16)
        st_ref[0:1, :] += jnp.sum(dh, axis=0, keepdims=True)
        st_ref[1:2, :] += jnp.sum(dh * norm, axis=0, keepdims=True)
        st_ref[2:3, :] += dgain
        st_ref[3:4, :] += jnp.sum(dx1 * y_ref[...].astype(F32), axis=0, keepdims=True)

    half_tile = ROW_TILE // 2

    def rows(width):
        return pl.BlockSpec((half_tile, width), lambda i: (i, 0))

    whole = pl.BlockSpec((D_MODEL, D_FF), lambda i: (0, 0))
    dx1, dy, stats = pl.pallas_call(
        in_body,
        grid=(seq // half_tile,),
        in_specs=[rows(D_FF), rows(D_FF), whole, whole, rows(D_MODEL), rows(D_MODEL), rows(D_MODEL),
                  _row(D_MODEL), _row(D_MODEL), _row(D_MODEL)],
        out_specs=[rows(D_MODEL), rows(D_MODEL), pl.BlockSpec((8, D_MODEL), lambda i: (0, 0))],
        out_shape=[jax.ShapeDtypeStruct((seq, D_MODEL), F32), jax.ShapeDtypeStruct((seq, D_MODEL), BF16),
                   jax.ShapeDtypeStruct((8, D_MODEL), F32)],
        compiler_params=_params(("arbitrary",)),
        name="ffn_bwd_in",
    )(dgate, dup, w_gate, w_up, dx2, x1, y, gate1, gain, scale)
    return dgate, dup, act, dy2, dx1, dy, stats


def _outproj_bwd(dy, w_out):
    seq = dy.shape[0]

    def body(dy_ref, w_ref, out_ref):
        out_ref[...] = _nt(dy_ref[...], w_ref[...])

    rows = pl.BlockSpec((ROW_TILE, D_MODEL), lambda i: (i, 0))
    return pl.pallas_call(
        body,
        grid=(seq // ROW_TILE,),
        in_specs=[rows, pl.BlockSpec((D_MODEL, D_MODEL), lambda i: (0, 0))],
        out_specs=rows,
        out_shape=jax.ShapeDtypeStruct((seq, D_MODEL), F32),
        compiler_params=_params(("arbitrary",)),
        name="outproj_bwd",
    )(dy, w_out)


def _inproj_bwd(dq, dk, dv, dxd, dz, dba, w_a, w_d, w_ba, x, dx1, gain, scale, partials):
    seq = x.shape[0]
    n = len(partials)
    n_steps = seq // ROW_TILE

    def body(*refs):
        (dq_ref, dk_ref, dv_ref, dxd_ref, dz_ref, dba_ref, wa_ref, wd_ref, wb_ref, x_ref, dx1_ref, g_ref,
         sc_ref) = refs[:13]
        gx_ref, st_ref = refs[13 + n:15 + n]
        riding = (refs[13:13 + n], refs[15 + n:15 + 2 * n], *refs[15 + 2 * n:])

        @pl.when(pl.program_id(0) == 0)
        def _():
            st_ref[...] = jnp.zeros_like(st_ref)
            for cp in (_scatter_copies(*riding) if n else []):
                cp.start()

        dh = (_nt(dq_ref[...].astype(BF16), wa_ref[:, 0:HEAD_W])
              + _nt(dk_ref[...].astype(BF16), wa_ref[:, HEAD_W:2 * HEAD_W])
              + _nt(dv_ref[...].astype(BF16), wa_ref[:, 2 * HEAD_W:])
              + _nt(dxd_ref[...].astype(BF16), wd_ref[:, 0:3 * HEAD_W])
              + _nt(dz_ref[...].astype(BF16), wd_ref[:, 3 * HEAD_W:])
              + _nt(dba_ref[...].astype(BF16), wb_ref[...]))
        xhat, rstd, norm = _rms_fwd(x_ref[...], g_ref[...])
        dxn, dgain = _rms_bwd(dh * (1.0 + sc_ref[...]), xhat, rstd, g_ref[...])
        gx_ref[...] = dx1_ref[...] + dxn
        st_ref[0:1, :] += jnp.sum(dh, axis=0, keepdims=True)
        st_ref[1:2, :] += jnp.sum(dh * norm, axis=0, keepdims=True)
        st_ref[2:3, :] += dgain

        if n:
            @pl.when(pl.program_id(0) == n_steps - 1)
            def _():
                for cp in _scatter_copies(*riding):
                    cp.wait()

    def rows(width):
        return pl.BlockSpec((ROW_TILE, width), lambda i: (i, 0))

    def whole(a):
        return pl.BlockSpec(a.shape, lambda i: (0, 0))

    sems = [pltpu.SemaphoreType.DMA((3 * n,)), pltpu.SemaphoreType.DMA((3 * n,))] if n else []
    return pl.pallas_call(
        body,
        grid=(n_steps,),
        in_specs=[rows(HEAD_W), rows(HEAD_W), rows(HEAD_W), rows(3 * HEAD_W), rows(HEAD_W), rows(LANES),
                  whole(w_a), whole(w_d), whole(w_ba), rows(D_MODEL), rows(D_MODEL), _row(D_MODEL), _row(D_MODEL)]
        + [ANY] * n,
        out_specs=[rows(D_MODEL), pl.BlockSpec((8, D_MODEL), lambda i: (0, 0))] + [ANY] * n,
        out_shape=[jax.ShapeDtypeStruct((seq, D_MODEL), F32), jax.ShapeDtypeStruct((8, D_MODEL), F32)]
        + [jax.ShapeDtypeStruct(p.shape, p.dtype) for p in partials],
        scratch_shapes=sems,
        compiler_params=_params(("arbitrary",)),
        name="inproj_bwd",
    )(dq, dk, dv, dxd, dz, dba, w_a, w_d, w_ba, x, dx1, gain, scale, *partials)


def _weight_grad(a, b, name):
    seq, m = a.shape
    n = b.shape[1]
    tm = m if m <= 1536 else m // 2
    tn = n if n <= 1536 else n // 2
    n_k = seq // ROW_TILE

    def body(a_ref, b_ref, out_ref):
        part = _tn(a_ref[...].astype(BF16), b_ref[...].astype(BF16))

        @pl.when(pl.program_id(2) == 0)
        def _():
            out_ref[...] = part

        @pl.when(pl.program_id(2) > 0)
        def _():
            out_ref[...] += part

    return pl.pallas_call(
        body,
        grid=(m // tm, n // tn, n_k),
        in_specs=[pl.BlockSpec((ROW_TILE, tm), lambda i, j, k: (k, i)),
                  pl.BlockSpec((ROW_TILE, tn), lambda i, j, k: (k, j))],
        out_specs=pl.BlockSpec((tm, tn), lambda i, j, k: (i, j)),
        out_shape=jax.ShapeDtypeStruct((m, n), F32),
        compiler_params=_params(("arbitrary", "arbitrary", "arbitrary")),
        name=name,
    )(a, b)


def _weight_grad_stack(pieces, b, name):
    seq, n = b.shape
    widths = [a.shape[1] for a in pieces]
    starts = [sum(widths[:i]) for i in range(len(pieces))]

    def body(*refs):
        a_refs, b_ref, out_ref = refs[:len(pieces)], refs[len(pieces)], refs[len(pieces) + 1]

        @pl.when(pl.program_id(0) == 0)
        def _():
            out_ref[...] = jnp.zeros_like(out_ref)

        bb = b_ref[...].astype(BF16)
        for a_ref, start, width in zip(a_refs, starts, widths):
            out_ref[start:start + width, :] += _tn(a_ref[...].astype(BF16), bb)

    def rows(width):
        return pl.BlockSpec((ROW_TILE, width), lambda k: (k, 0))

    return pl.pallas_call(
        body,
        grid=(seq // ROW_TILE,),
        in_specs=[rows(w) for w in widths] + [rows(n)],
        out_specs=pl.BlockSpec((sum(widths), n), lambda k: (0, 0)),
        out_shape=jax.ShapeDtypeStruct((sum(widths), n), F32),
        compiler_params=_params(("arbitrary",)),
        name=name,
    )(*pieces, b)


def _adamw(w, g, m, v, name):
    n_rows, n_cols = w.shape
    if n_rows % 256 == 0:
        block, grid, index = (256, n_cols), (n_rows // 256,), lambda i: (i, 0)
    elif n_cols % 256 == 0:
        block, grid, index = (n_rows, 256), (n_cols // 256,), lambda i: (0, i)
    else:
        block, grid, index = (n_rows, n_cols), (1,), lambda i: (0, 0)

    def body(w_ref, g_ref, m_ref, v_ref, d_ref, nm_ref, nv_ref):
        gv = g_ref[...]
        nm = ADAM_B1 * m_ref[...] + (1.0 - ADAM_B1) * gv
        nv = ADAM_B2 * v_ref[...] + (1.0 - ADAM_B2) * (gv * gv)
        m_hat = nm / (1.0 - ADAM_B1 ** ADAM_STEP)
        v_hat = nv / (1.0 - ADAM_B2 ** ADAM_STEP)
        d_ref[...] = -ADAM_LR * (m_hat / (jnp.sqrt(v_hat) + ADAM_EPS) + ADAM_WD * w_ref[...])
        nm_ref[...] = nm
        nv_ref[...] = nv

    blk = pl.BlockSpec(block, index)
    shape = jax.ShapeDtypeStruct((n_rows, n_cols), F32)
    return pl.pallas_call(
        body,
        grid=grid,
        in_specs=[blk] * 4,
        out_specs=[blk] * 3,
        out_shape=[shape] * 3,
        compiler_params=_params(("arbitrary",)),
        name=name,
    )(w, g, m, v)


IN_WIDTH = 3600
BA_COL = 7 * HEAD_W


def _local_step(x, target, mod, norm_attn_g, w_in, rel_bias, conv_w, a_log, dt_bias, delta_norm_g,
                norm_ffn_g, final_norm_g, shards, assemble, reduce_pairs):
    sh1, sc1, g1, sh2, sc2, g2 = [mod[:, i * D_MODEL:(i + 1) * D_MODEL] for i in range(6)]
    w_a = w_in[:, :3 * HEAD_W]
    w_d = w_in[:, 3 * HEAD_W:BA_COL]
    w_ba = jnp.pad(w_in[:, BA_COL:], ((0, 0), (0, LANES - 2 * N_HEADS)))
    tables = jnp.asarray(_attn_tables())
    alog_row = jnp.pad(a_log, ((0, 0), (N_HEADS, LANES - 2 * N_HEADS)))
    dt_row = jnp.pad(dt_bias, ((0, 0), (N_HEADS, LANES - 2 * N_HEADS)))
    gain_row = jnp.tile(delta_norm_g, (1, N_HEADS))

    h1, qkv_a, qkvz, ba = _inproj_fwd(x, norm_attn_g, sc1, sh1, w_a, w_d, w_ba)
    bias = _attention_bias(rel_bias, tables)
    y_attn, lse, *gathered = _attention_fwd(qkv_a, bias, shards)
    w_out, w_gate, w_up, w_down = assemble(gathered)
    xs = _delta_prep_fwd(qkvz, ba, conv_w, alog_row, dt_row)
    inv_h, qk_h, u_h, w_h = _delta_chunk_fwd(xs)
    o, st_h = _delta_scan_fwd(xs, qk_h, u_h, w_h)
    y_delta = _delta_post_fwd(o, qkvz, gain_row)
    x1, h2, y = _outproj_fwd(y_attn, y_delta, w_out, x, g1, norm_ffn_g, sc2, sh2)
    gate, up, dx2, st_f = _ffn_fwd(h2, w_gate, w_up, w_down, x1, g2, final_norm_g, target)

    dgate, dup, act, dy2, dx1, dy, st_b = _ffn_bwd(dx2, gate, up, w_gate, w_up, w_down, x1, y, g2, g1, norm_ffn_g, sc2)
    partials = reduce_pairs([_weight_grad_stack([y_attn, y_delta], dy, "wgrad_out"),
                             _weight_grad(dgate, h2, "wgrad_gate"), _weight_grad(dup, h2, "wgrad_up"),
                             _weight_grad(act, dy2, "wgrad_down")], 1, "rest")
    grads = {}
    dycat = _outproj_bwd(dy, w_out)
    do, dz, dgain = _delta_post_bwd(dycat, o, qkvz, gain_row)
    dsn_h, dvn_h = _delta_scan_bwd(xs, qk_h, w_h, do)
    dxs = _delta_chunk_bwd(xs, inv_h, u_h, w_h, st_h, dsn_h, dvn_h, do)
    dconv, dba, dvec = _delta_prep_bwd(qkvz, ba, conv_w, alog_row, dt_row, dxs)
    dxd, grads["conv_w"] = _conv_bwd(dconv, qkvz, conv_w)
    dq, dk, dv, dbias, *scattered = _attention_bwd(qkv_a, dycat, y_attn, lse, bias, partials)
    partials_in = reduce_pairs([jnp.concatenate(
        [_weight_grad_stack([dq, dk, dv], h1, "wgrad_in_attn"),
         _weight_grad_stack([dxd, dz, dba], h1, "wgrad_in_delta")[:IN_WIDTH - 3 * HEAD_W]], axis=0)], 0, "in")
    grad_x, st_i, *scattered_in = _inproj_bwd(dq, dk, dv, dxd, dz, dba, w_a, w_d, w_ba, x, dx1, norm_attn_g, sc1,
                                              partials_in)
    grads["rel_bias"] = _rel_bias_grad(dbias, tables)[:, :N_BUCKETS].T
    grads["a_log"] = dvec[0:1, N_HEADS:2 * N_HEADS]
    grads["dt_bias"] = dvec[1:2, N_HEADS:2 * N_HEADS]
    grads["delta_norm_g"] = dgain[1:2, :HEAD_DIM]
    grads["norm_attn_g"] = st_i[2:3]
    grads["norm_ffn_g"] = st_b[2:3]
    grads["final_norm_g"] = st_f[0:1]
    dmod = jnp.concatenate([st_i[0:1], st_i[1:2], st_b[3:4], st_b[0:1], st_b[1:2], st_f[1:2]], axis=1)
    return st_f[3, 0], grad_x, grads, dmod, (partials_in + partials, scattered_in + scattered)


MESH = pl.DeviceIdType.MESH
OTHER_CHIPS = ((1, 0), (0, 1), (1, 1))
ALL_PEERS = tuple((m >> 2 & 1, m >> 1 & 1, m & 1) for m in range(1, 8))
ANY = pl.BlockSpec(memory_space=pl.ANY)
VMEM_SPEC = pl.BlockSpec(memory_space=pltpu.VMEM)


def _me():
    return lax.axis_index("x"), lax.axis_index("y"), lax.axis_index("c")


def _flip(pos, mask):
    return tuple(1 - p if m else p for p, m in zip(pos, mask))


def _remote(src, dst, send_sems, recv_sems, k, to):
    return pltpu.make_async_remote_copy(src_ref=src, dst_ref=dst, send_sem=send_sems.at[k], recv_sem=recv_sems.at[k],
                                        device_id=to, device_id_type=MESH)


def _ada_exchange(c8, w_ada, b_ada, conv8):
    def body(c_ref, w_ref, b_ref, cv_ref, mod_ref, cact_ref, conv_ref, c_all, part_all, send_sems, recv_sems):
        x, y, c = me = _me()
        dev = 4 * x + 2 * y + c
        chip = 2 * x + y
        c_all[dev] = c_ref[...]
        conv_ref[chip] = cv_ref[...]
        first = [_remote(c_ref, c_all.at[dev], send_sems, recv_sems, k, _flip(me, mask))
                 for k, mask in enumerate(ALL_PEERS)]
        first += [_remote(cv_ref, conv_ref.at[chip], send_sems, recv_sems, 7 + j, _flip(me, (*mask, 0)))
                  for j, mask in enumerate(OTHER_CHIPS)]
        for cp in first:
            cp.start()
        for cp in first:
            cp.wait()
        row = lax.broadcasted_iota(jnp.int32, (8, D_MODEL), 0)
        c_rows = jnp.zeros((8, D_MODEL), F32)
        for d in range(8):
            c_rows = jnp.where(row == d, c_all[d], c_rows)
        c_act = _silu(c_rows)
        cact_ref[...] = c_act
        part_all[chip] = _nn(c_act, w_ref[...], HIGHEST)
        second = [_remote(part_all.at[chip], part_all.at[chip], send_sems, recv_sems, 10 + j, _flip(me, (*mask, 0)))
                  for j, mask in enumerate(OTHER_CHIPS)]
        for cp in second:
            cp.start()
        for cp in second:
            cp.wait()
        cols = w_ref.shape[1]
        for k in range(4):
            mod_ref[:, k * cols:(k + 1) * cols] = part_all[k] + b_ref[:, k * cols:(k + 1) * cols]

    cols = w_ada.shape[1]
    return pl.pallas_call(
        body,
        in_specs=[VMEM_SPEC] * 4,
        out_specs=[VMEM_SPEC] * 3,
        out_shape=[jax.ShapeDtypeStruct((8, 4 * cols), F32), jax.ShapeDtypeStruct((8, D_MODEL), F32),
                   jax.ShapeDtypeStruct((4, 8, conv8.shape[1]), F32)],
        scratch_shapes=[pltpu.VMEM((8, 8, D_MODEL), F32), pltpu.VMEM((4, 8, cols), F32),
                        pltpu.SemaphoreType.DMA((13,)), pltpu.SemaphoreType.DMA((13,))],
        compiler_params=pltpu.CompilerParams(vmem_limit_bytes=VMEM_LIMIT),
        name="ada_exchange",
    )(c8, w_ada, b_ada, conv8)


def _gather_weights(shards):
    n = len(shards)

    def body(*refs):
        first, passed = _gather_copies(refs[:n], refs[n:2 * n], *refs[2 * n:])
        for cp in first:
            cp.start()
        for cp, fwd in zip(first, passed):
            cp.wait_recv()
            fwd.start()
        for cp in first:
            cp.wait_send()
        for fwd in passed:
            fwd.wait()

    return pl.pallas_call(
        body,
        in_specs=[ANY] * n,
        out_specs=[ANY] * n,
        out_shape=_gathered_shapes(shards),
        scratch_shapes=[pltpu.SemaphoreType.DMA((6 * n,)), pltpu.SemaphoreType.DMA((6 * n,))],
        name="gather_weights",
    )(*shards)


def _gathered_shapes(shards):
    return [jax.ShapeDtypeStruct((4, *s.shape), s.dtype) for s in shards]


def _gather_copies(srcs, dsts, send_sems, recv_sems):
    x, y, c = me = _me()
    chip = 2 * x + y
    sibling = _flip(me, (0, 0, 1))
    first, passed = [], []
    for a, (src, dst) in enumerate(zip(srcs, dsts)):
        for j, mask in enumerate(OTHER_CHIPS):
            to = _flip(me, (*mask, 0))
            first.append(_remote(src.at[c], dst.at[chip, c], send_sems, recv_sems, 6 * a + j, to))
            landed = dst.at[2 * to[0] + to[1], c]
            passed.append(_remote(landed, landed, send_sems, recv_sems, 6 * a + 3 + j, sibling))
    return first, passed


def _scatter_copies(srcs, dsts, send_sems, recv_sems):
    x, y, c = me = _me()
    chip = 2 * x + y
    copies = []
    for a, (src, dst) in enumerate(zip(srcs, dsts)):
        for j, mask in enumerate(OTHER_CHIPS):
            to = _flip(me, (*mask, 0))
            copies.append(_remote(src.at[2 * to[0] + to[1]], dst.at[chip], send_sems, recv_sems, 3 * a + j, to))
    return copies


def _start_and_wait(copies):
    for cp in copies:
        cp.start()
    for cp in copies:
        cp.wait()


def _swap_halves(grads):
    n = len(grads)

    def body(*refs):
        srcs, got = refs[:n], refs[n:2 * n]
        send_sems, recv_sems = refs[2 * n:]
        x, y, c = me = _me()
        _start_and_wait([_remote(srcs[a].at[:, 1 - c], got[a], send_sems, recv_sems, a, _flip(me, (0, 0, 1)))
                         for a in range(n)])

    return pl.pallas_call(
        body,
        in_specs=[ANY] * n,
        out_specs=[ANY] * n,
        out_shape=[jax.ShapeDtypeStruct((4, g.shape[2], g.shape[3]), g.dtype) for g in grads],
        scratch_shapes=[pltpu.SemaphoreType.DMA((n,)), pltpu.SemaphoreType.DMA((n,))],
        name=f"swap_halves_{n}",
    )(*grads)


def _join_halves(halves):
    n = len(halves)

    def body(*refs):
        srcs, dsts = refs[:n], refs[n:2 * n]
        send_sems, recv_sems = refs[2 * n:]
        x, y, c = me = _me()
        _start_and_wait([_remote(srcs[a], dsts[a].at[c], send_sems, recv_sems, a, _flip(me, (0, 0, 1)))
                         for a in range(n)])

    return pl.pallas_call(
        body,
        in_specs=[ANY] * n,
        out_specs=[ANY] * n,
        out_shape=[jax.ShapeDtypeStruct((2, *h.shape), h.dtype) for h in halves],
        scratch_shapes=[pltpu.SemaphoreType.DMA((n,)), pltpu.SemaphoreType.DMA((n,))],
        name=f"join_halves_{n}",
    )(*halves)


def _gather_small(packed):
    n_rows = packed.shape[0]

    def body(p_ref, all_ref, sum_ref, send_sems, recv_sems):
        x, y, c = me = _me()
        dev = 4 * x + 2 * y + c
        all_ref[dev] = p_ref[...]
        copies = [_remote(p_ref, all_ref.at[dev], send_sems, recv_sems, k, _flip(me, mask))
                  for k, mask in enumerate(ALL_PEERS)]
        for cp in copies:
            cp.start()
        for cp in copies:
            cp.wait()
        total = all_ref[0]
        for d in range(1, 8):
            total = total + all_ref[d]
        sum_ref[...] = total

    return pl.pallas_call(
        body,
        in_specs=[VMEM_SPEC],
        out_specs=[VMEM_SPEC, VMEM_SPEC],
        out_shape=[jax.ShapeDtypeStruct((8, n_rows, LANES), F32), jax.ShapeDtypeStruct((n_rows, LANES), F32)],
        scratch_shapes=[pltpu.SemaphoreType.DMA((7,)), pltpu.SemaphoreType.DMA((7,))],
        name="gather_small",
    )(packed)


def _add_pair(a, b, out_dtype, name):
    def body(a_ref, b_ref, o_ref):
        o_ref[...] = (a_ref[...] + b_ref[...]).astype(o_ref.dtype)

    blk = pl.BlockSpec((1, *a.shape[1:]), lambda i: (i, 0, 0))
    return pl.pallas_call(
        body, grid=(a.shape[0],), in_specs=[blk, blk], out_specs=blk,
        out_shape=jax.ShapeDtypeStruct(a.shape, out_dtype),
        compiler_params=_params(("arbitrary",)), name=name,
    )(a, b)


def _add_slots(a, name):
    def body(a_ref, o_ref):
        total = a_ref[0].astype(F32)
        for k in range(1, 4):
            total = total + a_ref[k].astype(F32)
        o_ref[...] = total

    return pl.pallas_call(
        body, in_specs=[VMEM_SPEC], out_specs=VMEM_SPEC,
        out_shape=jax.ShapeDtypeStruct(a.shape[1:], F32),
        compiler_params=pltpu.CompilerParams(vmem_limit_bytes=VMEM_LIMIT), name=name,
    )(a)


def _ada_weight_grad(c_act, dmod_cols):
    def body(c_ref, d_ref, o_ref):
        o_ref[...] = _tn(c_ref[...], d_ref[...], HIGHEST)

    return pl.pallas_call(
        body, in_specs=[VMEM_SPEC, VMEM_SPEC], out_specs=VMEM_SPEC,
        out_shape=jax.ShapeDtypeStruct((c_act.shape[1], dmod_cols.shape[1]), F32),
        compiler_params=pltpu.CompilerParams(vmem_limit_bytes=VMEM_LIMIT), name="ada_weight_grad",
    )(c_act, dmod_cols)


def kernel(x, c, w_ada, b_ada, norm_attn_g, w_in, rel_bias, conv_w, a_log, dt_bias, delta_norm_g, w_out, norm_ffn_g, w_gate, w_up, w_down, final_norm_g, loss_target, m_w_ada, m_b_ada, m_norm_attn_g, m_w_in, m_rel_bias, m_conv_w, m_a_log, m_dt_bias, m_delta_norm_g, m_w_out, m_norm_ffn_g, m_w_gate, m_w_up, m_w_down, m_final_norm_g, v_w_ada, v_b_ada, v_norm_attn_g, v_w_in, v_rel_bias, v_conv_w, v_a_log, v_dt_bias, v_delta_norm_g, v_w_out, v_norm_ffn_g, v_w_gate, v_w_up, v_w_down, v_final_norm_g):
    xi, yi, ci = _me()
    dev = 4 * xi + 2 * yi + ci
    chip = 2 * xi + yi

    conv_cols = conv_w.shape[2]
    mod_all, c_act, conv_all = _ada_exchange(jnp.broadcast_to(c, (8, D_MODEL)), w_ada[0], b_ada,
                                             jnp.pad(conv_w[0], ((0, 4), (0, 0))))
    mod = lax.dynamic_slice_in_dim(mod_all, dev, 1, axis=0)
    conv_full = jnp.swapaxes(conv_all[:, :4, :], 0, 1).reshape(4, 4 * conv_cols)

    big_names = ("w_in", "w_out", "w_gate", "w_up", "w_down")
    by_cols = (True, False, True, True, False)

    def rows_form(a, cols):
        return jnp.swapaxes(a[0], 0, 1) if cols else a[0]

    def halves_form(w):
        rows, lanes = w.shape
        if (rows // 2) % 16:
            rows, lanes = w.size // LANES, LANES
        return (2, rows // 2, lanes)

    big = [rows_form(w, cols) for w, cols in zip((w_in, w_out, w_gate, w_up, w_down), by_cols)]
    shards = [w.astype(BF16).reshape(halves_form(w)) for w in big]

    def assemble(gathered, first):
        ws, ss, cols = big[first:first + len(gathered)], shards[first:], by_cols[first:]
        full = [lax.dynamic_update_index_in_dim(g, s, chip, 0).reshape(4 * w.shape[0], w.shape[1])
                for g, s, w in zip(gathered, ss, ws)]
        return [g.T if c else g for g, c in zip(full, cols)]

    def reduce_pairs(grads, first, tag):
        slots = [g.reshape(4, *halves_form(w)) for g, w in zip(grads, big[first:])]
        return [_add_pair(lax.dynamic_index_in_dim(s, ci, 1, keepdims=False), got, BF16, f"add_pair_{tag}{a}")
                for a, (s, got) in enumerate(zip(slots, _swap_halves(slots)))]

    def finish(partials, scattered, first, tag):
        by_source = [lax.dynamic_update_index_in_dim(b, lax.dynamic_index_in_dim(p, chip, 0, keepdims=False), chip, 0)
                     for b, p in zip(scattered, partials)]
        halves = [_add_slots(p, f"add_slots_{tag}{a}") for a, p in enumerate(by_source)]
        joined = [lax.dynamic_update_index_in_dim(j, h, ci, 0) for j, h in zip(_join_halves(halves), halves)]
        return [j.reshape(w.shape) for j, w in zip(joined, big[first:])]

    whole_in, = assemble(_gather_weights(shards[:1]), 0)
    loss, grad_x, grads, dmod, (partials, scattered) = _local_step(
        x[0], loss_target[0], mod, norm_attn_g, whole_in, rel_bias, conv_full, a_log, dt_bias, delta_norm_g,
        norm_ffn_g, final_norm_g[None], shards[1:], functools.partial(assemble, first=1), reduce_pairs)

    big_grads = finish(partials, scattered, 0, "all")

    pieces = [dmod, grads["conv_w"], grads["norm_attn_g"], grads["norm_ffn_g"], grads["final_norm_g"],
              grads["rel_bias"], grads["a_log"], grads["dt_bias"], grads["delta_norm_g"]]
    flat = [jnp.pad(p.reshape(-1), (0, -p.size % LANES)) for p in pieces]
    n_rows = [f.size // LANES for f in flat]
    packed = jnp.concatenate(flat).reshape(-1, LANES)
    packed = jnp.pad(packed, ((0, -packed.shape[0] % 8), (0, 0)))
    all_small, total = _gather_small(packed)
    sums, start = [], 0
    for p, n in zip(pieces, n_rows):
        sums.append(total[start:start + n].reshape(-1)[:p.size].reshape(p.shape))
        start += n
    g_b_ada, g_conv, g_norm_attn, g_norm_ffn, g_final, g_rel, g_alog, g_dt, g_dnorm = sums
    dmod_all = all_small[:, :n_rows[0], :].reshape(8, -1)
    ada_cols = w_ada.shape[2]
    g_w_ada = _ada_weight_grad(c_act, lax.dynamic_slice_in_dim(dmod_all, chip * ada_cols, ada_cols, axis=1))
    g_conv = lax.dynamic_slice_in_dim(g_conv, chip * conv_cols, conv_cols, axis=1)

    grad = {"w_ada": g_w_ada[None], "b_ada": g_b_ada, "norm_attn_g": g_norm_attn,
            "rel_bias": g_rel, "conv_w": g_conv[None], "a_log": g_alog, "dt_bias": g_dt, "delta_norm_g": g_dnorm,
            "norm_ffn_g": g_norm_ffn, "final_norm_g": g_final.reshape(-1)}
    weight = {"w_ada": w_ada, "b_ada": b_ada, "norm_attn_g": norm_attn_g, "w_in": w_in, "rel_bias": rel_bias,
              "conv_w": conv_w, "a_log": a_log, "dt_bias": dt_bias, "delta_norm_g": delta_norm_g, "w_out": w_out,
              "norm_ffn_g": norm_ffn_g, "w_gate": w_gate, "w_up": w_up, "w_down": w_down, "final_norm_g": final_norm_g}
    first = {"w_ada": m_w_ada, "b_ada": m_b_ada, "norm_attn_g": m_norm_attn_g, "w_in": m_w_in, "rel_bias": m_rel_bias,
             "conv_w": m_conv_w, "a_log": m_a_log, "dt_bias": m_dt_bias, "delta_norm_g": m_delta_norm_g,
             "w_out": m_w_out, "norm_ffn_g": m_norm_ffn_g, "w_gate": m_w_gate, "w_up": m_w_up, "w_down": m_w_down,
             "final_norm_g": m_final_norm_g}
    second = {"w_ada": v_w_ada, "b_ada": v_b_ada, "norm_attn_g": v_norm_attn_g, "w_in": v_w_in, "rel_bias": v_rel_bias,
              "conv_w": v_conv_w, "a_log": v_a_log, "dt_bias": v_dt_bias, "delta_norm_g": v_delta_norm_g,
              "w_out": v_w_out, "norm_ffn_g": v_norm_ffn_g, "w_gate": v_w_gate, "w_up": v_w_up, "w_down": v_w_down,
              "final_norm_g": v_final_norm_g}
    delta, new_m, new_v = {}, {}, {}
    for name, w in weight.items():
        if name in big_names:
            continue
        two_d = (-1, w.shape[-1])
        d, nm, nv = _adamw(w.reshape(two_d), grad[name].reshape(two_d), first[name].reshape(two_d),
                           second[name].reshape(two_d), f"adamw_{name}")
        delta[name], new_m[name], new_v[name] = d.reshape(w.shape), nm.reshape(w.shape), nv.reshape(w.shape)
    for name, w, g, cols in zip(big_names, big, big_grads, by_cols):
        outs = _adamw(w, g, rows_form(first[name], cols), rows_form(second[name], cols), f"adamw_{name}")
        grad[name], delta[name], new_m[name], new_v[name] = [
            (jnp.swapaxes(o, 0, 1) if cols else o)[None] for o in (g, *outs)]

    names = list(weight)
    return (lax.psum(loss, ("x", "y", "c")), grad_x[None], *[grad[n] for n in names], *[delta[n] for n in names],
            *[new_m[n] for n in names], *[new_v[n] for n in names])
```

```python
import functools
import math

import numpy as np
import jax
import jax.numpy as jnp
from jax import lax
from jax.experimental import pallas as pl
from jax.experimental.pallas import tpu as pltpu

F32 = jnp.float32
BF16 = jnp.bfloat16
HIGHEST = lax.Precision.HIGHEST

D_MODEL = 1024
HEAD_DIM = 64
N_HEADS = 8
HEAD_W = 512
BRANCHES = ((128, 1), (512, 4), (2048, 16))
BAND = 128
ATT_TILE = 2048
ATT_UNROLL = 4
N_BUCKETS = 32
MAX_DISTANCE = 2048
CHUNK = 64
D_FF = 2816
EPS = 1e-6
NEG_INF = -1e30
LANES = 128
VMEM_LIMIT = 56 * 1024 * 1024

ADAM_LR = 0.001
ADAM_B1 = 0.9
ADAM_B2 = 0.999
ADAM_EPS = 1e-08
ADAM_WD = 0.01
ADAM_STEP = 10


def _nn(a, b, precision=None):
    return jnp.dot(a, b, preferred_element_type=F32, precision=precision)


def _nt(a, b, precision=None):
    return lax.dot_general(a, b, (((1,), (1,)), ((), ())), preferred_element_type=F32, precision=precision)


def _tn(a, b, precision=None):
    return lax.dot_general(a, b, (((0,), (0,)), ((), ())), preferred_element_type=F32, precision=precision)


def _params(sem, vmem=VMEM_LIMIT):
    return pltpu.CompilerParams(dimension_semantics=sem, vmem_limit_bytes=vmem)


def _sigmoid(x):
    return 0.5 * jnp.tanh(0.5 * x) + 0.5


def _silu_and_slope(x):
    s = _sigmoid(x)
    return x * s, s * (1.0 + x * (1.0 - s))


def _silu(x):
    return x * _sigmoid(x)


def _attn_tables():
    qi = np.arange(BAND)[:, None]
    kj = np.arange(2 * BAND)[None, :]
    steps = qi + BAND - kj
    in_window = (steps >= 0) & (steps <= BAND)
    max_exact = N_BUCKETS // 2
    out = np.zeros((3, 2, BAND, 2 * BAND), np.int32)
    for b, (_, dil) in enumerate(BRANCHES):
        dist = np.maximum(steps, 0) * dil
        dist_f = np.maximum(dist, 1).astype(np.float32)
        large = max_exact + (np.log(dist_f / np.float32(max_exact)) / np.float32(math.log(MAX_DISTANCE / max_exact))
                             * np.float32(N_BUCKETS - max_exact)).astype(np.int32)
        bucket = np.where(dist < max_exact, dist, np.minimum(large, N_BUCKETS - 1)).astype(np.int32)
        out[b, 0] = np.where(in_window, bucket, -1)
        out[b, 1] = np.where(in_window & (kj >= BAND), bucket, -1)
    return out


def _attention_bias(rel_bias, tables):
    def body(rel_ref, tab_ref, out_ref):
        head = pl.program_id(0)
        for b in range(3):
            tab = tab_ref[b, 0]

            def pick(kk, acc, tab=tab):
                return jnp.where(tab == kk, rel_ref[kk, head], acc)

            acc = lax.fori_loop(0, N_BUCKETS, pick, jnp.zeros((BAND, 2 * BAND), F32))
            for first in range(2):
                out_ref[0, b, first] = jnp.where(tab_ref[b, first] < 0, NEG_INF, acc)

    return pl.pallas_call(
        body,
        grid=(N_HEADS,),
        in_specs=[pl.BlockSpec(memory_space=pltpu.SMEM),
                  pl.BlockSpec((3, 2, BAND, 2 * BAND), lambda h: (0, 0, 0, 0))],
        out_specs=pl.BlockSpec((1, 3, 2, BAND, 2 * BAND), lambda h: (h, 0, 0, 0, 0)),
        out_shape=jax.ShapeDtypeStruct((N_HEADS, 3, 2, BAND, 2 * BAND), F32),
        compiler_params=_params(("arbitrary",)),
        name="attn_bias",
    )(rel_bias, tables)


def _bias_spec():
    return pl.BlockSpec((2, 3, 2, BAND, 2 * BAND), lambda p, t: (p, 0, 0, 0, 0))


def _attn_block_index(idx, t, r):
    nb = ATT_TILE // (BAND * r)
    rho = idx // nb
    n = idx % nb
    qs = rho + r * BAND * n
    gs = t * ATT_TILE + qs
    first = (t * nb + n) == 0
    ps = jnp.where(first, gs, gs - r * BAND)
    return qs, gs, ps, first.astype(jnp.int32)


def _rows(start, r):
    return pl.ds(start, BAND) if r == 1 else pl.ds(start, BAND, stride=r)


def _attention_fwd(qkv, bias, shards):
    seq = qkv.shape[0]
    n_tiles = seq // ATT_TILE
    n = len(shards)

    def body(*refs):
        bias_ref, q_ref, k_ref, v_ref = refs[:4]
        y_ref, lse_ref = refs[4 + n:6 + n]
        o_s, l_s = refs[6 + 2 * n:8 + 2 * n]
        riding = (refs[4:4 + n], refs[6 + n:6 + 2 * n], *refs[8 + 2 * n:])
        pair = pl.program_id(0)
        t = pl.program_id(1)
        if n:
            @pl.when((pair == 0) & (t == 0))
            def _():
                for cp in _gather_copies(*riding)[0]:
                    cp.start()

            @pl.when((pair == 2) & (t == 0))
            def _():
                for cp, fwd in zip(*_gather_copies(*riding)):
                    cp.wait_recv()
                    fwd.start()

        lane = lax.broadcasted_iota(jnp.int32, (1, LANES), 1)
        head0 = lane < HEAD_DIM
        masks = (head0, jnp.logical_not(head0))
        ones = jnp.ones((2 * BAND, LANES), BF16)
        for b, (_, r) in enumerate(BRANCHES):
            def blocks(it, carry, b=b, r=r):
                idx = [_attn_block_index(it * ATT_UNROLL + j, t, r) for j in range(ATT_UNROLL)]
                qb = [q_ref[_rows(qs, r), :] * (HEAD_DIM ** -0.5) for qs, _, _, _ in idx]
                kcat = [jnp.concatenate([k_ref[_rows(ps, r), :], k_ref[_rows(gs, r), :]], axis=0).astype(BF16)
                        for _, gs, ps, _ in idx]
                vcat = [jnp.concatenate([v_ref[_rows(ps, r), :], v_ref[_rows(gs, r), :]], axis=0).astype(BF16)
                        for _, gs, ps, _ in idx]
                work = [(j, hh) for j in range(ATT_UNROLL) for hh in range(2)]
                s = [_nt(jnp.where(masks[hh], qb[j], 0.0).astype(BF16), kcat[j]) + bias_ref[hh, b, idx[j][3]]
                     for j, hh in work]
                m = [jnp.max(sv, axis=-1, keepdims=True) for sv in s]
                e = [jnp.exp(sv - mv) for sv, mv in zip(s, m)]
                eb = [ev.astype(BF16) for ev in e]
                den = [_nn(ev, ones) for ev in eb]
                out = [_nn(ev, vcat[j]) / dv for ev, dv, (j, _) in zip(eb, den, work)]
                lse = [mv + jnp.log(dv) for mv, dv in zip(m, den)]
                for j in range(ATT_UNROLL):
                    o_s[b, _rows(idx[j][0], r), :] = jnp.where(head0, out[2 * j], out[2 * j + 1])
                    l_s[b, _rows(idx[j][0], r), :] = jnp.where(head0, lse[2 * j], lse[2 * j + 1])
                return carry

            lax.fori_loop(0, ATT_TILE // BAND // ATT_UNROLL, blocks, 0)

        def merge(i, carry):
            rows = pl.ds(pl.multiple_of(i * BAND, BAND), BAND)
            l0, l1, l2 = l_s[0, rows, :], l_s[1, rows, :], l_s[2, rows, :]
            m = jnp.maximum(jnp.maximum(l0, l1), l2)
            w0, w1, w2 = jnp.exp(l0 - m), jnp.exp(l1 - m), jnp.exp(l2 - m)
            tot = w0 + w1 + w2
            y_ref[rows, :] = (w0 * o_s[0, rows, :] + w1 * o_s[1, rows, :] + w2 * o_s[2, rows, :]) / tot
            lse_ref[rows, :] = m + jnp.log(tot)
            return carry

        lax.fori_loop(0, ATT_TILE // BAND, merge, 0)

        if n:
            @pl.when((pair == N_HEADS // 2 - 1) & (t == n_tiles - 1))
            def _():
                first, passed = _gather_copies(*riding)
                for cp in first:
                    cp.wait_send()
                for fwd in passed:
                    fwd.wait()

    tile = pl.BlockSpec((ATT_TILE, LANES), lambda p, t: (t, p))
    sems = [pltpu.SemaphoreType.DMA((6 * n,)), pltpu.SemaphoreType.DMA((6 * n,))] if n else []
    return pl.pallas_call(
        body,
        grid=(N_HEADS // 2, n_tiles),
        in_specs=[
            _bias_spec(),
            pl.BlockSpec((ATT_TILE, LANES), lambda p, t: (t, p)),
            pl.BlockSpec((seq, LANES), lambda p, t: (0, 4 + p)),
            pl.BlockSpec((seq, LANES), lambda p, t: (0, 8 + p)),
        ] + [ANY] * n,
        out_specs=[tile, tile] + [ANY] * n,
        out_shape=[jax.ShapeDtypeStruct((seq, HEAD_W), F32), jax.ShapeDtypeStruct((seq, HEAD_W), F32)]
        + _gathered_shapes(shards),
        scratch_shapes=[
            pltpu.VMEM((3, ATT_TILE, LANES), F32),
            pltpu.VMEM((3, ATT_TILE, LANES), F32),
        ] + sems,
        compiler_params=_params(("arbitrary", "arbitrary")),
        name="attn_fwd",
    )(bias, qkv, qkv, qkv, *shards)


def _attention_bwd(qkv, dy, y, lse, bias, partials):
    seq = qkv.shape[0]
    n_tiles = seq // ATT_TILE
    n = len(partials)

    def body(*refs):
        bias_ref, q_ref, k_ref, v_ref, dy_ref, y_ref, lse_ref = refs[:7]
        dq_ref, dk_ref, dv_ref, dbias_ref = refs[7 + n:11 + n]
        riding = (refs[7:7 + n], refs[11 + n:11 + 2 * n], *refs[11 + 2 * n:])
        pair = pl.program_id(0)
        t = pl.program_id(1)
        if n:
            @pl.when((pair == 0) & (t == 0))
            def _():
                for cp in _scatter_copies(*riding):
                    cp.start()

        lane = lax.broadcasted_iota(jnp.int32, (1, LANES), 1)
        head0 = lane < HEAD_DIM

        @pl.when(t == 0)
        def _():
            dk_ref[...] = jnp.zeros_like(dk_ref)
            dv_ref[...] = jnp.zeros_like(dv_ref)
            dbias_ref[...] = jnp.zeros_like(dbias_ref)

        dq_ref[...] = jnp.zeros_like(dq_ref)

        masks = (head0, jnp.logical_not(head0))
        ones = jnp.ones((LANES, LANES), BF16)
        scale = HEAD_DIM ** -0.5
        for b, (_, r) in enumerate(BRANCHES):
            def blocks(it, carry, b=b, r=r):
                idx = [_attn_block_index(it * ATT_UNROLL + j, t, r) for j in range(ATT_UNROLL)]
                qb = [q_ref[_rows(qs, r), :] * scale for qs, _, _, _ in idx]
                kcat = [jnp.concatenate([k_ref[_rows(ps, r), :], k_ref[_rows(gs, r), :]], axis=0).astype(BF16)
                        for _, gs, ps, _ in idx]
                vcat = [jnp.concatenate([v_ref[_rows(ps, r), :], v_ref[_rows(gs, r), :]], axis=0).astype(BF16)
                        for _, gs, ps, _ in idx]
                dob = [dy_ref[_rows(qs, r), :] for qs, _, _, _ in idx]
                ob = [y_ref[_rows(qs, r), :] for qs, _, _, _ in idx]
                lb = [lse_ref[_rows(qs, r), :] for qs, _, _, _ in idx]
                work = [(j, hh) for j in range(ATT_UNROLL) for hh in range(2)]
                qh = [jnp.where(masks[hh], qb[j], 0.0).astype(BF16) for j, hh in work]
                doh = [jnp.where(masks[hh], dob[j], 0.0) for j, hh in work]
                dohb = [d.astype(BF16) for d in doh]
                s = [_nt(qh[w], kcat[j]) + bias_ref[hh, b, idx[j][3]] for w, (j, hh) in enumerate(work)]
                dp = [_nt(dohb[w], vcat[j]) for w, (j, _) in enumerate(work)]
                lrot = [pltpu.roll(lv, HEAD_DIM, 1) for lv in lb]
                lcol = [jnp.where(masks[hh], lb[j], lrot[j]) for j, hh in work]
                parts = [_split(doh[w] * ob[j]) for w, (j, _) in enumerate(work)]
                delta = [_nn(hi, ones) + _nn(lo, ones) for hi, lo in parts]
                prob = [jnp.exp(sv - jnp.concatenate([lv, lv], axis=1)) for sv, lv in zip(s, lcol)]
                ds = [pv * (dv - jnp.concatenate([de, de], axis=1)) for pv, dv, de in zip(prob, dp, delta)]
                dsb = [d.astype(BF16) for d in ds]
                dq = [_nn(dsb[w], kcat[j]) for w, (j, _) in enumerate(work)]
                dkc = [_tn(dsb[w], qh[w]) for w in range(len(work))]
                dvc = [_tn(prob[w].astype(BF16), dohb[w]) for w in range(len(work))]
                for hh in range(2):
                    dbias_ref[0, b, hh] += sum(ds[w] for w, (_, head) in enumerate(work) if head == hh)
                for j in range(ATT_UNROLL):
                    qs, gs, ps, _ = idx[j]
                    dkcat = dkc[2 * j] + dkc[2 * j + 1]
                    dvcat = dvc[2 * j] + dvc[2 * j + 1]
                    dq_ref[_rows(qs, r), :] += jnp.where(head0, dq[2 * j], dq[2 * j + 1]) * scale
                    dk_ref[_rows(ps, r), :] += dkcat[:BAND]
                    dk_ref[_rows(gs, r), :] += dkcat[BAND:]
                    dv_ref[_rows(ps, r), :] += dvcat[:BAND]
                    dv_ref[_rows(gs, r), :] += dvcat[BAND:]
                return carry

            lax.fori_loop(0, ATT_TILE // BAND // ATT_UNROLL, blocks, 0)

        if n:
            @pl.when((pair == N_HEADS // 2 - 1) & (t == n_tiles - 1))
            def _():
                for cp in _scatter_copies(*riding):
                    cp.wait()

    tile = pl.BlockSpec((ATT_TILE, LANES), lambda p, t: (t, p))
    full = pl.BlockSpec((seq, LANES), lambda p, t: (0, p))
    sems = [pltpu.SemaphoreType.DMA((3 * n,)), pltpu.SemaphoreType.DMA((3 * n,))] if n else []
    return pl.pallas_call(
        body,
        grid=(N_HEADS // 2, n_tiles),
        in_specs=[
            _bias_spec(),
            pl.BlockSpec((ATT_TILE, LANES), lambda p, t: (t, p)),
            pl.BlockSpec((seq, LANES), lambda p, t: (0, 4 + p)),
            pl.BlockSpec((seq, LANES), lambda p, t: (0, 8 + p)),
            tile, tile, tile,
        ] + [ANY] * n,
        out_specs=[tile, full, full,
                   pl.BlockSpec((1, 3, 2, BAND, 2 * BAND), lambda p, t: (p, 0, 0, 0, 0))] + [ANY] * n,
        out_shape=[jax.ShapeDtypeStruct((seq, HEAD_W), F32)] * 3
        + [jax.ShapeDtypeStruct((N_HEADS // 2, 3, 2, BAND, 2 * BAND), F32)]
        + [jax.ShapeDtypeStruct(p.shape, p.dtype) for p in partials],
        scratch_shapes=sems,
        compiler_params=_params(("arbitrary", "arbitrary")),
        name="attn_bwd",
    )(bias, qkv, qkv, qkv, dy, y, lse, *partials)


def _rel_bias_grad(dbias, tables):
    def body(tab_ref, db_ref, out_ref):
        lane = lax.broadcasted_iota(jnp.int32, (1, LANES), 1)
        out_ref[...] = jnp.zeros_like(out_ref)
        for b in range(3):
            tab = tab_ref[b, 0]

            def head(h, carry, b=b, tab=tab):
                d = db_ref[h // 2, b, h % 2]
                sums = [jnp.sum(jnp.where(tab == kk, d, 0.0), keepdims=True) for kk in range(N_BUCKETS)]
                row = jnp.zeros((1, LANES), F32)
                for kk, s in enumerate(sums):
                    row = row + jnp.where(lane == kk, s, 0.0)
                out_ref[pl.ds(h, 1), :] += row
                return carry

            lax.fori_loop(0, N_HEADS, head, 0)

    return pl.pallas_call(
        body,
        out_shape=jax.ShapeDtypeStruct((N_HEADS, LANES), F32),
        compiler_params=pltpu.CompilerParams(vmem_limit_bytes=VMEM_LIMIT),
        name="rel_bias_grad",
    )(tables, dbias)


ROW_TILE = 512


def _head_sum_matrix():
    return (lax.broadcasted_iota(jnp.int32, (HEAD_W, LANES), 0) // HEAD_DIM
            == lax.broadcasted_iota(jnp.int32, (HEAD_W, LANES), 1)).astype(F32)


def _head_spread_matrix(offset=0):
    return (lax.broadcasted_iota(jnp.int32, (LANES, HEAD_W), 0)
            == lax.broadcasted_iota(jnp.int32, (LANES, HEAD_W), 1) // HEAD_DIM + offset).astype(F32)


def _head_gather_matrix(offset=0):
    return (lax.broadcasted_iota(jnp.int32, (HEAD_W, LANES), 0) // HEAD_DIM + offset
            == lax.broadcasted_iota(jnp.int32, (HEAD_W, LANES), 1)).astype(F32)


def _split3(x):
    hi = x.astype(BF16)
    rest = x - hi.astype(F32)
    mid = rest.astype(BF16)
    return hi, mid, (rest - mid.astype(F32)).astype(BF16)


def _pick(x, onehot):
    m = onehot.astype(BF16)
    hi, mid, lo = _split3(x)
    return _nn(hi, m) + (_nn(mid, m) + _nn(lo, m))


def _pick_left(onehot, x):
    m = onehot.astype(BF16)
    hi, mid, lo = _split3(x)
    return _nn(m, hi) + (_nn(m, mid) + _nn(m, lo))


def _tri(lower, strict=False):
    r = lax.broadcasted_iota(jnp.int32, (CHUNK, CHUNK), 0)
    c = lax.broadcasted_iota(jnp.int32, (CHUNK, CHUNK), 1)
    if lower:
        return (c < r) if strict else (c <= r)
    return c >= r


def _softplus(z):
    return jnp.maximum(z, 0.0) + jnp.log(1.0 + jnp.exp(-jnp.abs(z)))


def _conv_taps(stage, w_ref, rows):
    return (w_ref[3:4, :] * stage[8:8 + rows, :] + w_ref[2:3, :] * stage[7:7 + rows, :]
            + w_ref[1:2, :] * stage[6:6 + rows, :] + w_ref[0:1, :] * stage[5:5 + rows, :])


def _l2_scale(xc, hsum, hspread):
    ssq = _pick(xc * xc, hsum)
    return _pick(lax.rsqrt(ssq + EPS), hspread)


def _stage_rows(stage, x_ref, xp_ref, i):
    stage[0:8, :] = jnp.where(i == 0, 0.0, xp_ref[...])
    stage[8:8 + ROW_TILE, :] = x_ref[...]


def _delta_prep_fwd(qkvz, ba, conv_w, alog_row, dt_row):
    seq = qkvz.shape[0]
    qkv_w = 3 * HEAD_W

    def body(x_ref, xp_ref, ba_ref, w_ref, al_ref, dt_ref, out_ref, stage):
        i = pl.program_id(0)
        _stage_rows(stage, x_ref, xp_ref, i)
        act = _silu(_conv_taps(stage, w_ref, ROW_TILE))
        hsum, hspread = _head_sum_matrix(), _head_spread_matrix()
        qc, kc = act[:, :HEAD_W], act[:, HEAD_W:2 * HEAD_W]
        out_ref[0] = qc * _l2_scale(qc, hsum, hspread) * (HEAD_DIM ** -0.5)
        out_ref[1] = kc * _l2_scale(kc, hsum, hspread)
        out_ref[2] = act[:, 2 * HEAD_W:]
        bav = ba_ref[...]
        out_ref[3] = _pick(_sigmoid(bav), hspread)
        g8 = -jnp.exp(al_ref[...]) * _softplus(bav + dt_ref[...])
        gb = _pick(g8, _head_spread_matrix(N_HEADS))
        cum = _tri(True).astype(F32)
        for ch in range(ROW_TILE // CHUNK):
            rows = slice(ch * CHUNK, (ch + 1) * CHUNK)
            out_ref[4, rows, :] = _pick_left(cum, gb[rows])

    return pl.pallas_call(
        body,
        grid=(seq // ROW_TILE,),
        in_specs=[
            pl.BlockSpec((ROW_TILE, qkv_w), lambda i: (i, 0)),
            pl.BlockSpec((8, qkv_w), lambda i: (jnp.maximum(i * (ROW_TILE // 8) - 1, 0), 0)),
            pl.BlockSpec((ROW_TILE, LANES), lambda i: (i, 0)),
            pl.BlockSpec((4, qkv_w), lambda i: (0, 0)),
            pl.BlockSpec((1, LANES), lambda i: (0, 0)),
            pl.BlockSpec((1, LANES), lambda i: (0, 0)),
        ],
        out_specs=pl.BlockSpec((5, ROW_TILE, HEAD_W), lambda i: (0, i, 0)),
        out_shape=jax.ShapeDtypeStruct((5, seq, HEAD_W), F32),
        scratch_shapes=[pltpu.VMEM((ROW_TILE + 8, qkv_w), F32)],
        compiler_params=_params(("arbitrary",)),
        name="delta_prep_fwd",
    )(qkvz, qkvz, ba, conv_w, alog_row, dt_row)


def _split(x):
    hi = x.astype(BF16)
    return hi, (x - hi.astype(F32)).astype(BF16)


def _dot3(a, b, dot=_nn):
    return dot(a[0], b[0]) + (dot(a[0], b[1]) + dot(a[1], b[0]))


def _unit_lower_inverses(mats):
    eye = (lax.broadcasted_iota(jnp.int32, (CHUNK, CHUNK), 0)
           == lax.broadcasted_iota(jnp.int32, (CHUNK, CHUNK), 1)).astype(F32)
    invs = [eye - a for a in mats]
    powers = [_split(a) for a in mats]
    for step in range(5):
        squares = [_dot3(p, p) for p in powers]
        powers = [_split(s) for s in squares]
        invs = [inv + _dot3(_split(inv), p) for inv, p in zip(invs, powers)]
    return invs


def _chunk_terms(q, k, v, beta, gc):
    causal, strict = _tri(True), _tri(True, strict=True)
    e = jnp.exp(gc)
    g_last = jnp.broadcast_to(gc[CHUNK - 1:CHUNK, :], (CHUNK, CHUNK))
    f = jnp.exp(g_last - gc)
    e_last = jnp.exp(g_last)
    decay = jnp.where(causal, jnp.exp(jnp.where(causal, gc - gc.T, 0.0)), 0.0)
    kb = k * beta
    a_mat = jnp.where(strict, _nt(kb.astype(BF16), k.astype(BF16)) * decay, 0.0)
    qk = jnp.where(causal, _nt(q.astype(BF16), k.astype(BF16)) * decay, 0.0)
    return e, f, e_last, decay, kb, a_mat, qk


GROUP = 8
UNROLL = 8


def _chunk_rows(ci):
    return pl.ds(pl.multiple_of(ci * CHUNK, CHUNK), CHUNK)


def _pair_specs(n_planes):
    return pl.BlockSpec((n_planes, GROUP * CHUNK, LANES), lambda p, g: (0, g, p))


def _delta_chunk_fwd(xs):
    seq = xs.shape[1]
    rows_per_step = GROUP * CHUNK

    def body(x_ref, inv_ref, qk_ref, u_ref, w_ref):
        for hh in range(2):
            lanes = slice(hh * HEAD_DIM, (hh + 1) * HEAD_DIM)
            rows = [slice(step * CHUNK, (step + 1) * CHUNK) for step in range(GROUP)]
            xh = [[x_ref[j, r, lanes] for j in range(5)] for r in rows]
            terms = [_chunk_terms(*x) for x in xh]
            invs = _unit_lower_inverses([t[5] for t in terms])
            for r, x, t, inv in zip(rows, xh, terms, invs):
                e, kb, qk = t[0], t[4], t[6]
                inv_parts = _split(inv)
                inv_ref[hh, r, :] = inv
                qk_ref[hh, r, :] = qk
                u_ref[hh, r, :] = _dot3(inv_parts, _split(x[2] * x[3]))
                w_ref[hh, r, :] = _dot3(inv_parts, _split(kb * e))

    out = pl.BlockSpec((2, rows_per_step, HEAD_DIM), lambda p, g: (p, g, 0))
    return pl.pallas_call(
        body,
        grid=(N_HEADS // 2, seq // rows_per_step),
        in_specs=[_pair_specs(5)],
        out_specs=[out] * 4,
        out_shape=[jax.ShapeDtypeStruct((N_HEADS, seq, HEAD_DIM), F32)] * 4,
        compiler_params=_params(("parallel", "parallel")),
        name="delta_chunk_fwd",
    )(xs)


def _decays(gc):
    g_last = jnp.broadcast_to(gc[CHUNK - 1:CHUNK, :], (CHUNK, CHUNK))
    return jnp.exp(gc), jnp.exp(g_last - gc), jnp.exp(g_last)


def _token_blocks(index, n_steps=None):
    rows_per_step = GROUP * CHUNK
    if n_steps is None:
        return pl.BlockSpec((1, rows_per_step, HEAD_W), lambda g: (index, g, 0))
    return pl.BlockSpec((1, rows_per_step, HEAD_W), lambda g: (index, n_steps - 1 - g, 0))


def _head_lanes(h):
    return pl.ds(h * HEAD_DIM, HEAD_DIM)


def _delta_scan_fwd(xs, qk_h, u_h, w_h):
    seq = xs.shape[1]
    rows_per_step = GROUP * CHUNK

    def body(q_ref, k_ref, gc_ref, qk_ref, u_ref, w_ref, o_ref, st_ref, state):
        @pl.when(pl.program_id(0) == 0)
        def _():
            state[...] = jnp.zeros_like(state)

        def chunk(ci, carry):
            rows = _chunk_rows(ci)
            heads = range(N_HEADS)
            dec = [_decays(gc_ref[0, rows, _head_lanes(h)]) for h in heads]
            s = [state[h] for h in heads]
            sb = [s[h].astype(BF16) for h in heads]
            vnb = [(u_ref[h, rows, :] - _nn(w_ref[h, rows, :].astype(BF16), sb[h])).astype(BF16) for h in heads]
            for h in heads:
                o_ref[rows, _head_lanes(h)] = (_nn((q_ref[0, rows, _head_lanes(h)] * dec[h][0]).astype(BF16), sb[h])
                                               + _nn(qk_ref[h, rows, :].astype(BF16), vnb[h]))
                st_ref[h, rows, :] = s[h]
            for h in heads:
                state[h] = s[h] * dec[h][2] + _tn((k_ref[0, rows, _head_lanes(h)] * dec[h][1]).astype(BF16), vnb[h])
            return carry

        lax.fori_loop(0, GROUP, chunk, 0)

    blk = pl.BlockSpec((N_HEADS, rows_per_step, HEAD_DIM), lambda g: (0, g, 0))
    return pl.pallas_call(
        body,
        grid=(seq // rows_per_step,),
        in_specs=[_token_blocks(0), _token_blocks(1), _token_blocks(4), blk, blk, blk],
        out_specs=[pl.BlockSpec((rows_per_step, HEAD_W), lambda g: (g, 0)), blk],
        out_shape=[jax.ShapeDtypeStruct((seq, HEAD_W), F32), jax.ShapeDtypeStruct((N_HEADS, seq, HEAD_DIM), F32)],
        scratch_shapes=[pltpu.VMEM((N_HEADS, CHUNK, CHUNK), F32)],
        compiler_params=_params(("arbitrary",)),
        name="delta_scan_fwd",
    )(xs, xs, xs, qk_h, u_h, w_h)


def _delta_scan_bwd(xs, qk_h, w_h, do):
    seq = xs.shape[1]
    rows_per_step = GROUP * CHUNK
    n_steps = seq // rows_per_step

    def body(q_ref, k_ref, gc_ref, qk_ref, w_ref, do_ref, dsn_ref, dvn_ref, dstate):
        @pl.when(pl.program_id(0) == 0)
        def _():
            dstate[...] = jnp.zeros_like(dstate)

        def chunk(step, carry):
            rows = _chunk_rows(GROUP - 1 - step)
            heads = range(N_HEADS)
            dec = [_decays(gc_ref[0, rows, _head_lanes(h)]) for h in heads]
            ds_next = [dstate[h] for h in heads]
            dob = [do_ref[rows, _head_lanes(h)].astype(BF16) for h in heads]
            dv_new = [_tn(qk_ref[h, rows, :].astype(BF16), dob[h])
                      + _nn((k_ref[0, rows, _head_lanes(h)] * dec[h][1]).astype(BF16), ds_next[h].astype(BF16))
                      for h in heads]
            for h in heads:
                dsn_ref[h, rows, :] = ds_next[h]
                dvn_ref[h, rows, :] = dv_new[h]
            for h in heads:
                dstate[h] = (_tn((q_ref[0, rows, _head_lanes(h)] * dec[h][0]).astype(BF16), dob[h])
                             + dec[h][2] * ds_next[h] - _tn(w_ref[h, rows, :].astype(BF16), dv_new[h].astype(BF16)))
            return carry

        lax.fori_loop(0, GROUP, chunk, 0)

    blk = pl.BlockSpec((N_HEADS, rows_per_step, HEAD_DIM), lambda g: (0, n_steps - 1 - g, 0))
    return pl.pallas_call(
        body,
        grid=(n_steps,),
        in_specs=[_token_blocks(0, n_steps), _token_blocks(1, n_steps), _token_blocks(4, n_steps), blk, blk,
                  pl.BlockSpec((rows_per_step, HEAD_W), lambda g: (n_steps - 1 - g, 0))],
        out_specs=[blk, blk],
        out_shape=[jax.ShapeDtypeStruct((N_HEADS, seq, HEAD_DIM), F32)] * 2,
        scratch_shapes=[pltpu.VMEM((N_HEADS, CHUNK, CHUNK), F32)],
        compiler_params=_params(("arbitrary",)),
        name="delta_scan_bwd",
    )(xs, xs, xs, qk_h, w_h, do)


def _delta_chunk_bwd(xs, inv_h, u_h, w_h, st_h, dsn_h, dvn_h, do):
    seq = xs.shape[1]
    rows_per_step = GROUP * CHUNK

    def body(x_ref, inv_ref, u_ref, w_ref, st_ref, dsn_ref, dvn_ref, do_ref, dx_ref):
        causal, strict = _tri(True), _tri(True, strict=True)
        last_row = lax.broadcasted_iota(jnp.int32, (CHUNK, CHUNK), 0) == CHUNK - 1

        def bf(vals):
            return [val.astype(BF16) for val in vals]

        def group(hh, first):
            lanes = slice(hh * HEAD_DIM, (hh + 1) * HEAD_DIM)
            rows = [slice(step * CHUNK, (step + 1) * CHUNK) for step in range(first, first + UNROLL)]
            n = range(UNROLL)
            q, k, v, beta, gc = [[x_ref[j, r, lanes] for r in rows] for j in range(5)]
            terms = [_chunk_terms(q[i], k[i], v[i], beta[i], gc[i]) for i in n]
            e, f, e_last, decay, kb, a_mat, qk = [[t[j] for t in terms] for j in range(7)]
            inv = [_split(inv_ref[hh, r, :]) for r in rows]
            u = [u_ref[hh, r, :] for r in rows]
            w = [w_ref[hh, r, :] for r in rows]
            s = [st_ref[hh, r, :] for r in rows]
            ds_next = [dsn_ref[hh, r, :] for r in rows]
            dv_new = [dvn_ref[hh, r, :] for r in rows]
            sb, dsb, dvb, wb = bf(s), bf(ds_next), bf(dv_new), bf(w)
            dob = bf([do_ref[r, lanes] for r in rows])
            qbf, kbf, kbb = bf(q), bf(k), bf(kb)
            vnb = bf([u[i] - _nn(wb[i], sb[i]) for i in n])
            dqe = [_nt(dob[i], sb[i]) for i in n]
            dw = [-_nt(dvb[i], sb[i]) for i in n]
            dkf = [_nt(vnb[i], dsb[i]) for i in n]
            dqk = [jnp.where(causal, _nt(dob[i], vnb[i]), 0.0) for i in n]
            drhs_u = [_dot3(inv[i], _split(dv_new[i]), _tn) for i in n]
            drhs_w = [_dot3(inv[i], _split(dw[i]), _tn) for i in n]
            da = [-jnp.where(strict, _nt(drhs_u[i].astype(BF16), u[i].astype(BF16))
                             + _nt(drhs_w[i].astype(BF16), wb[i]), 0.0) for i in n]
            dad = bf([da[i] * decay[i] for i in n])
            dqd = bf([dqk[i] * decay[i] for i in n])
            dkb = [e[i] * drhs_w[i] + _nn(dad[i], kbf[i]) for i in n]
            dk = [_tn(dad[i], kbb[i]) + _tn(dqd[i], qbf[i]) + f[i] * dkf[i] + beta[i] * dkb[i] for i in n]
            dq = [_nn(dqd[i], kbf[i]) + e[i] * dqe[i] for i in n]
            for i in n:
                de_full = kb[i] * drhs_w[i] + q[i] * dqe[i]
                df_full = k[i] * dkf[i]
                m = da[i] * a_mat[i] + dqk[i] * qk[i]
                dgc = de_full * e[i] - df_full * f[i] + m - m.T
                tail = jnp.sum(df_full * f[i] + s[i] * ds_next[i] * e_last[i], axis=0, keepdims=True)
                dgc = dgc + jnp.where(last_row, jnp.broadcast_to(tail, (CHUNK, CHUNK)), 0.0)
                dx_ref[0, rows[i], lanes] = dq[i]
                dx_ref[1, rows[i], lanes] = dk[i]
                dx_ref[2, rows[i], lanes] = beta[i] * drhs_u[i]
                dx_ref[3, rows[i], lanes] = v[i] * drhs_u[i] + k[i] * dkb[i]
                dx_ref[4, rows[i], lanes] = dgc

        for hh in range(2):
            for first in range(0, GROUP, UNROLL):
                group(hh, first)

    blk = pl.BlockSpec((2, rows_per_step, HEAD_DIM), lambda p, g: (p, g, 0))
    return pl.pallas_call(
        body,
        grid=(N_HEADS // 2, seq // rows_per_step),
        in_specs=[_pair_specs(5)] + [blk] * 6 + [pl.BlockSpec((rows_per_step, LANES), lambda p, g: (g, p))],
        out_specs=_pair_specs(5),
        out_shape=jax.ShapeDtypeStruct((5, seq, HEAD_W), F32),
        compiler_params=_params(("parallel", "parallel")),
        name="delta_chunk_bwd",
    )(xs, inv_h, u_h, w_h, st_h, dsn_h, dvn_h, do)


def _delta_post_fwd(o, qkvz, gain_row):
    seq = o.shape[0]

    def body(o_ref, z_ref, g_ref, y_ref):
        ov = o_ref[...]
        ms = _pick(ov * ov, _head_sum_matrix()) * (1.0 / HEAD_DIM)
        rb = _pick(lax.rsqrt(ms + EPS), _head_spread_matrix())
        y_ref[...] = (ov * rb * g_ref[...] * _silu(z_ref[...])).astype(y_ref.dtype)

    tile = pl.BlockSpec((ROW_TILE, HEAD_W), lambda i: (i, 0))
    return pl.pallas_call(
        body,
        grid=(seq // ROW_TILE,),
        in_specs=[tile, pl.BlockSpec((ROW_TILE, HEAD_W), lambda i: (i, 3)), pl.BlockSpec((1, HEAD_W), lambda i: (0, 0))],
        out_specs=tile,
        out_shape=jax.ShapeDtypeStruct((seq, HEAD_W), BF16),
        compiler_params=_params(("arbitrary",)),
        name="delta_post_fwd",
    )(o, qkvz, gain_row)


def _delta_post_bwd(dy, o, qkvz, gain_row):
    seq = o.shape[0]

    def body(dy_ref, o_ref, z_ref, g_ref, do_ref, dz_ref, dg_ref):
        @pl.when(pl.program_id(0) == 0)
        def _():
            dg_ref[...] = jnp.zeros_like(dg_ref)

        ov, zv, dyv, gain = o_ref[...], z_ref[...], dy_ref[...], g_ref[...]
        hsum, hspread = _head_sum_matrix(), _head_spread_matrix()
        ms = _pick(ov * ov, hsum) * (1.0 / HEAD_DIM)
        rb = _pick(lax.rsqrt(ms + EPS), hspread)
        ohat = ov * rb
        silu_z, slope_z = _silu_and_slope(zv)
        dz_ref[...] = dyv * ohat * gain * slope_z
        dn = dyv * silu_z
        dg_ref[0:1, :] += jnp.sum(dn * ohat, axis=0, keepdims=True)
        dohat = dn * gain

        @pl.when(pl.program_id(0) == pl.num_programs(0) - 1)
        def _():
            fold = (lax.broadcasted_iota(jnp.int32, (HEAD_W, HEAD_W), 0) % HEAD_DIM
                    == lax.broadcasted_iota(jnp.int32, (HEAD_W, HEAD_W), 1)).astype(F32)
            dg_ref[1:2, :] = _pick(dg_ref[0:1, :], fold)

        proj = _pick(_pick(dohat * ohat, hsum) * (1.0 / HEAD_DIM), hspread)
        do_ref[...] = rb * (dohat - ohat * proj)

    tile = pl.BlockSpec((ROW_TILE, HEAD_W), lambda i: (i, 0))
    return pl.pallas_call(
        body,
        grid=(seq // ROW_TILE,),
        in_specs=[pl.BlockSpec((ROW_TILE, HEAD_W), lambda i: (i, 1)), tile,
                  pl.BlockSpec((ROW_TILE, HEAD_W), lambda i: (i, 3)), pl.BlockSpec((1, HEAD_W), lambda i: (0, 0))],
        out_specs=[tile, tile, pl.BlockSpec((2, HEAD_W), lambda i: (0, 0))],
        out_shape=[jax.ShapeDtypeStruct((seq, HEAD_W), F32), jax.ShapeDtypeStruct((seq, HEAD_W), F32),
                   jax.ShapeDtypeStruct((2, HEAD_W), F32)],
        compiler_params=_params(("arbitrary",)),
        name="delta_post_bwd",
    )(dy, o, qkvz, gain_row)


def _delta_prep_bwd(qkvz, ba, conv_w, alog_row, dt_row, dxs):
    seq = qkvz.shape[0]
    qkv_w = 3 * HEAD_W

    def body(x_ref, xp_ref, ba_ref, w_ref, al_ref, dt_ref, dx_ref, dconv_ref, dba_ref, dvec_ref, stage):
        i = pl.program_id(0)

        @pl.when(i == 0)
        def _():
            dvec_ref[...] = jnp.zeros_like(dvec_ref)

        _stage_rows(stage, x_ref, xp_ref, i)
        pre = _conv_taps(stage, w_ref, ROW_TILE)
        act, slope = _silu_and_slope(pre)
        hsum, hspread = _head_sum_matrix(), _head_spread_matrix()
        for j, scale in ((0, HEAD_DIM ** -0.5), (1, 1.0)):
            cols = slice(j * HEAD_W, (j + 1) * HEAD_W)
            xc = act[:, cols]
            rb = _l2_scale(xc, hsum, hspread)
            xhat = xc * rb
            dhat = dx_ref[j] * scale
            proj = _pick(_pick(dhat * xhat, hsum), hspread)
            dconv_ref[:, cols] = rb * (dhat - xhat * proj) * slope[:, cols]
        dconv_ref[:, 2 * HEAD_W:] = dx_ref[2] * slope[:, 2 * HEAD_W:]

        bav = ba_ref[...]
        beta8 = _sigmoid(bav)
        dbeta8 = _pick(dx_ref[3], _head_gather_matrix())
        dgc8 = _pick(dx_ref[4], _head_gather_matrix(N_HEADS))
        rev = _tri(False).astype(F32)
        z = bav + dt_ref[...]
        ea = jnp.exp(al_ref[...])
        g8 = -ea * _softplus(z)
        sig = _sigmoid(z)
        d_alog = jnp.zeros((1, LANES), F32)
        d_dt = jnp.zeros((1, LANES), F32)
        for ch in range(ROW_TILE // CHUNK):
            rows = slice(ch * CHUNK, (ch + 1) * CHUNK)
            dg8 = _pick_left(rev, dgc8[rows])
            da = -dg8 * ea * sig[rows]
            dba_ref[rows, :] = dbeta8[rows] * beta8[rows] * (1.0 - beta8[rows]) + da
            d_alog = d_alog + jnp.sum(dg8 * g8[rows], axis=0, keepdims=True)
            d_dt = d_dt + jnp.sum(da, axis=0, keepdims=True)
        dvec_ref[0:1, :] += d_alog
        dvec_ref[1:2, :] += d_dt

    return pl.pallas_call(
        body,
        grid=(seq // ROW_TILE,),
        in_specs=[
            pl.BlockSpec((ROW_TILE, qkv_w), lambda i: (i, 0)),
            pl.BlockSpec((8, qkv_w), lambda i: (jnp.maximum(i * (ROW_TILE // 8) - 1, 0), 0)),
            pl.BlockSpec((ROW_TILE, LANES), lambda i: (i, 0)),
            pl.BlockSpec((4, qkv_w), lambda i: (0, 0)),
            pl.BlockSpec((1, LANES), lambda i: (0, 0)),
            pl.BlockSpec((1, LANES), lambda i: (0, 0)),
            pl.BlockSpec((5, ROW_TILE, HEAD_W), lambda i: (0, i, 0)),
        ],
        out_specs=[pl.BlockSpec((ROW_TILE, qkv_w), lambda i: (i, 0)),
                   pl.BlockSpec((ROW_TILE, LANES), lambda i: (i, 0)),
                   pl.BlockSpec((2, LANES), lambda i: (0, 0))],
        out_shape=[jax.ShapeDtypeStruct((seq, qkv_w), F32), jax.ShapeDtypeStruct((seq, LANES), F32),
                   jax.ShapeDtypeStruct((2, LANES), F32)],
        scratch_shapes=[pltpu.VMEM((ROW_TILE + 8, qkv_w), F32)],
        compiler_params=_params(("arbitrary",)),
        name="delta_prep_bwd",
    )(qkvz, qkvz, ba, conv_w, alog_row, dt_row, dxs)


def _conv_bwd(dconv, qkvz, conv_w):
    seq = dconv.shape[0]
    qkv_w = 3 * HEAD_W
    n_tiles = seq // ROW_TILE

    def body(dy_ref, dyn_ref, x_ref, xp_ref, w_ref, dx_ref, dw_ref, stage, dstage):
        i = pl.program_id(0)

        @pl.when(i == 0)
        def _():
            dw_ref[...] = jnp.zeros_like(dw_ref)

        _stage_rows(stage, x_ref, xp_ref, i)
        dstage[0:ROW_TILE, :] = dy_ref[...]
        dstage[ROW_TILE:ROW_TILE + 8, :] = jnp.where(i == n_tiles - 1, 0.0, dyn_ref[...])
        dy = dy_ref[...]
        dx_ref[...] = (w_ref[3:4, :] * dy + w_ref[2:3, :] * dstage[1:1 + ROW_TILE, :]
                       + w_ref[1:2, :] * dstage[2:2 + ROW_TILE, :] + w_ref[0:1, :] * dstage[3:3 + ROW_TILE, :])
        for j in range(4):
            dw_ref[j:j + 1, :] += jnp.sum(dy * stage[5 + j:5 + j + ROW_TILE, :], axis=0, keepdims=True)

    tile = pl.BlockSpec((ROW_TILE, qkv_w), lambda i: (i, 0))
    return pl.pallas_call(
        body,
        grid=(n_tiles,),
        in_specs=[
            tile,
            pl.BlockSpec((8, qkv_w), lambda i: (jnp.minimum((i + 1) * (ROW_TILE // 8), seq // 8 - 1), 0)),
            tile,
            pl.BlockSpec((8, qkv_w), lambda i: (jnp.maximum(i * (ROW_TILE // 8) - 1, 0), 0)),
            pl.BlockSpec((4, qkv_w), lambda i: (0, 0)),
        ],
        out_specs=[tile, pl.BlockSpec((4, qkv_w), lambda i: (0, 0))],
        out_shape=[jax.ShapeDtypeStruct((seq, qkv_w), F32), jax.ShapeDtypeStruct((4, qkv_w), F32)],
        scratch_shapes=[pltpu.VMEM((ROW_TILE + 8, qkv_w), F32), pltpu.VMEM((ROW_TILE + 8, qkv_w), F32)],
        compiler_params=_params(("arbitrary",)),
        name="conv_bwd",
    )(dconv, dconv, qkvz, qkvz, conv_w)


FF_TILE = 1408
WGRAD_ROWS = 1024


def _row(a):
    return pl.BlockSpec((1, a), lambda *_: (0, 0))


def _rms_fwd(xv, gain):
    rstd = lax.rsqrt(jnp.mean(xv * xv, axis=-1, keepdims=True) + EPS)
    xhat = xv * rstd
    return xhat, rstd, xhat * gain


def _rms_bwd(dnorm, xhat, rstd, gain):
    dxhat = dnorm * gain
    dx = rstd * (dxhat - xhat * jnp.mean(dxhat * xhat, axis=-1, keepdims=True))
    return dx, jnp.sum(dnorm * xhat, axis=0, keepdims=True)


def _inproj_fwd(x, gain, scale, shift, w_a, w_d, w_ba):
    seq = x.shape[0]

    def body(x_ref, g_ref, sc_ref, sh_ref, wa_ref, wd_ref, wb_ref, h_ref, a_ref, d_ref, b_ref):
        _, _, norm = _rms_fwd(x_ref[...], g_ref[...])
        h = (norm * (1.0 + sc_ref[...]) + sh_ref[...]).astype(BF16)
        h_ref[...] = h
        a_ref[...] = _nt(h, wa_ref[...])
        d_ref[...] = _nt(h, wd_ref[...])
        b_ref[...] = _nt(h, wb_ref[...])

    def rows(width):
        return pl.BlockSpec((ROW_TILE, width), lambda i: (i, 0))

    def whole(a):
        return pl.BlockSpec(a.shape, lambda i: (0, 0))

    return pl.pallas_call(
        body,
        grid=(seq // ROW_TILE,),
        in_specs=[rows(D_MODEL), _row(D_MODEL), _row(D_MODEL), _row(D_MODEL), whole(w_a), whole(w_d), whole(w_ba)],
        out_specs=[rows(D_MODEL), rows(3 * HEAD_W), rows(4 * HEAD_W), rows(LANES)],
        out_shape=[jax.ShapeDtypeStruct((seq, D_MODEL), BF16), jax.ShapeDtypeStruct((seq, 3 * HEAD_W), F32),
                   jax.ShapeDtypeStruct((seq, 4 * HEAD_W), F32), jax.ShapeDtypeStruct((seq, LANES), F32)],
        compiler_params=_params(("arbitrary",)),
        name="inproj_fwd",
    )(x, gain, scale, shift, w_a, w_d, w_ba)


def _outproj_fwd(y_attn, y_delta, w_out, x, gate1, gain, scale, shift):
    seq = x.shape[0]

    def body(ya_ref, yd_ref, wa_ref, wd_ref, x_ref, g1_ref, g_ref, sc_ref, sh_ref, x1_ref, h_ref, y_ref):
        y = _nn(ya_ref[...].astype(BF16), wa_ref[...]) + _nn(yd_ref[...], wd_ref[...])
        x1 = x_ref[...] + g1_ref[...] * y
        _, _, norm = _rms_fwd(x1, g_ref[...])
        x1_ref[...] = x1
        h_ref[...] = (norm * (1.0 + sc_ref[...]) + sh_ref[...]).astype(BF16)
        y_ref[...] = y.astype(BF16)

    def rows(width):
        return pl.BlockSpec((ROW_TILE, width), lambda i: (i, 0))

    return pl.pallas_call(
        body,
        grid=(seq // ROW_TILE,),
        in_specs=[rows(HEAD_W), rows(HEAD_W),
                  pl.BlockSpec((HEAD_W, D_MODEL), lambda i: (0, 0)), pl.BlockSpec((HEAD_W, D_MODEL), lambda i: (1, 0)),
                  rows(D_MODEL), _row(D_MODEL), _row(D_MODEL), _row(D_MODEL), _row(D_MODEL)],
        out_specs=[rows(D_MODEL), rows(D_MODEL), rows(D_MODEL)],
        out_shape=[jax.ShapeDtypeStruct((seq, D_MODEL), F32), jax.ShapeDtypeStruct((seq, D_MODEL), BF16),
                   jax.ShapeDtypeStruct((seq, D_MODEL), BF16)],
        compiler_params=_params(("arbitrary",)),
        name="outproj_fwd",
    )(y_attn, y_delta, w_out, w_out, x, gate1, gain, scale, shift)


def _ffn_fwd(h2, w_gate, w_up, w_down, x1, gate2, final_gain, target):
    seq = h2.shape[0]
    n_rows, n_ff = seq // ROW_TILE, D_FF // FF_TILE

    def body(h_ref, wg_ref, wu_ref, wd_ref, x1_ref, g2_ref, gf_ref, t_ref, gate_ref, up_ref, dx2_ref, st_ref, acc):
        i, j = pl.program_id(0), pl.program_id(1)

        @pl.when((i == 0) & (j == 0))
        def _():
            st_ref[...] = jnp.zeros_like(st_ref)

        h = h_ref[...]
        gate = _nt(h, wg_ref[...])
        up = _nt(h, wu_ref[...])
        gate_ref[...] = gate.astype(BF16)
        up_ref[...] = up.astype(BF16)
        part = _nn((_silu(gate) * up).astype(BF16), wd_ref[...])

        @pl.when(j == 0)
        def _():
            acc[...] = part

        @pl.when(j > 0)
        def _():
            acc[...] += part

        @pl.when(j == n_ff - 1)
        def _():
            y2 = acc[...]
            x2 = x1_ref[...] + g2_ref[...] * y2
            xhat, rstd, out = _rms_fwd(x2, gf_ref[...])
            diff = out - t_ref[...]
            dx2, dgain = _rms_bwd(diff * (1.0 / D_MODEL), xhat, rstd, gf_ref[...])
            dx2_ref[...] = dx2
            st_ref[0:1, :] += dgain
            st_ref[1:2, :] += jnp.sum(dx2 * y2, axis=0, keepdims=True)
            st_ref[2:3, :] += jnp.sum(diff * diff, axis=0, keepdims=True) * (0.5 / D_MODEL)

        @pl.when((i == n_rows - 1) & (j == n_ff - 1))
        def _():
            st_ref[3:4, :] = jnp.broadcast_to(jnp.sum(st_ref[2:3, :], keepdims=True), (1, D_MODEL))

    def rows(width):
        return pl.BlockSpec((ROW_TILE, width), lambda i, j: (i, 0))

    ff = pl.BlockSpec((ROW_TILE, FF_TILE), lambda i, j: (i, j))
    return pl.pallas_call(
        body,
        grid=(n_rows, n_ff),
        in_specs=[rows(D_MODEL),
                  pl.BlockSpec((FF_TILE, D_MODEL), lambda i, j: (j, 0)), pl.BlockSpec((FF_TILE, D_MODEL), lambda i, j: (j, 0)),
                  pl.BlockSpec((FF_TILE, D_MODEL), lambda i, j: (j, 0)),
                  rows(D_MODEL), _row(D_MODEL), _row(D_MODEL), rows(D_MODEL)],
        out_specs=[ff, ff, rows(D_MODEL), pl.BlockSpec((8, D_MODEL), lambda i, j: (0, 0))],
        out_shape=[jax.ShapeDtypeStruct((seq, D_FF), BF16), jax.ShapeDtypeStruct((seq, D_FF), BF16),
                   jax.ShapeDtypeStruct((seq, D_MODEL), F32), jax.ShapeDtypeStruct((8, D_MODEL), F32)],
        scratch_shapes=[pltpu.VMEM((ROW_TILE, D_MODEL), F32)],
        compiler_params=_params(("arbitrary", "arbitrary")),
        name="ffn_fwd",
    )(h2, w_gate, w_up, w_down, x1, gate2, final_gain, target)


def _ffn_bwd(dx2, gate, up, w_gate, w_up, w_down, x1, y, gate2, gate1, gain, scale):
    seq = dx2.shape[0]

    def act_body(dx2_ref, g2_ref, gate_ref, up_ref, wd_ref, dgate_ref, dup_ref, act_ref, dy2_ref):
        dy2 = (g2_ref[...] * dx2_ref[...]).astype(BF16)
        dy2_ref[...] = dy2
        gate = gate_ref[...].astype(F32)
        up = up_ref[...].astype(F32)
        dact = _nt(dy2, wd_ref[...])
        silu, slope = _silu_and_slope(gate)
        act_ref[...] = (silu * up).astype(BF16)
        dgate_ref[...] = (dact * up * slope).astype(BF16)
        dup_ref[...] = (dact * silu).astype(BF16)

    def rows2(width):
        return pl.BlockSpec((ROW_TILE, width), lambda i, j: (i, 0))

    ff = pl.BlockSpec((ROW_TILE, FF_TILE), lambda i, j: (i, j))
    dgate, dup, act, dy2 = pl.pallas_call(
        act_body,
        grid=(seq // ROW_TILE, D_FF // FF_TILE),
        in_specs=[rows2(D_MODEL), _row(D_MODEL), ff, ff, pl.BlockSpec((FF_TILE, D_MODEL), lambda i, j: (j, 0))],
        out_specs=[ff, ff, ff, rows2(D_MODEL)],
        out_shape=[jax.ShapeDtypeStruct((seq, D_FF), BF16)] * 3 + [jax.ShapeDtypeStruct((seq, D_MODEL), BF16)],
        compiler_params=_params(("arbitrary", "arbitrary")),
        name="ffn_bwd_act",
    )(dx2, gate2, gate, up, w_down)

    def in_body(dgate_ref, dup_ref, wg_ref, wu_ref, dx2_ref, x1_ref, y_ref, g1_ref, g_ref, sc_ref,
                dx1_ref, dy_ref, st_ref):
        @pl.when(pl.program_id(0) == 0)
        def _():
            st_ref[...] = jnp.zeros_like(st_ref)

        dh = _nn(dgate_ref[...], wg_ref[...]) + _nn(dup_ref[...], wu_ref[...])
        xhat, rstd, norm = _rms_fwd(x1_ref[...], g_ref[...])
        dxn, dgain = _rms_bwd(dh * (1.0 + sc_ref[...]), xhat, rstd, g_ref[...])
        dx1 = dx2_ref[...] + dxn
        dx1_ref[...] = dx1
        dy_ref[...] = (g1_ref[...] * dx1).astype(BF16)
        st_ref[0:1, :] += jnp.sum(dh, axis=0, keepdims=True)
        st_ref[1:2, :] += jnp.sum(dh * norm, axis=0, keepdims=True)
        st_ref[2:3, :] += dgain
        st_ref[3:4, :] += jnp.sum(dx1 * y_ref[...].astype(F32), axis=0, keepdims=True)

    half_tile = ROW_TILE // 2

    def rows(width):
        return pl.BlockSpec((half_tile, width), lambda i: (i, 0))

    whole = pl.BlockSpec((D_FF, D_MODEL), lambda i: (0, 0))
    dx1, dy, stats = pl.pallas_call(
        in_body,
        grid=(seq // half_tile,),
        in_specs=[rows(D_FF), rows(D_FF), whole, whole, rows(D_MODEL), rows(D_MODEL), rows(D_MODEL),
                  _row(D_MODEL), _row(D_MODEL), _row(D_MODEL)],
        out_specs=[rows(D_MODEL), rows(D_MODEL), pl.BlockSpec((8, D_MODEL), lambda i: (0, 0))],
        out_shape=[jax.ShapeDtypeStruct((seq, D_MODEL), F32), jax.ShapeDtypeStruct((seq, D_MODEL), BF16),
                   jax.ShapeDtypeStruct((8, D_MODEL), F32)],
        compiler_params=_params(("arbitrary",)),
        name="ffn_bwd_in",
    )(dgate, dup, w_gate, w_up, dx2, x1, y, gate1, gain, scale)
    return dgate, dup, act, dy2, dx1, dy, stats


def _outproj_bwd(dy, w_out):
    seq = dy.shape[0]

    def body(dy_ref, w_ref, out_ref):
        out_ref[...] = _nt(dy_ref[...], w_ref[...])

    rows = pl.BlockSpec((ROW_TILE, D_MODEL), lambda i: (i, 0))
    return pl.pallas_call(
        body,
        grid=(seq // ROW_TILE,),
        in_specs=[rows, pl.BlockSpec((D_MODEL, D_MODEL), lambda i: (0, 0))],
        out_specs=rows,
        out_shape=jax.ShapeDtypeStruct((seq, D_MODEL), F32),
        compiler_params=_params(("arbitrary",)),
        name="outproj_bwd",
    )(dy, w_out)


def _inproj_bwd(dq, dk, dv, dxd, dz, dba, w_a, w_d, w_ba, x, dx1, gain, scale, partials):
    seq = x.shape[0]
    n = len(partials)
    n_steps = seq // ROW_TILE

    def body(*refs):
        (dq_ref, dk_ref, dv_ref, dxd_ref, dz_ref, dba_ref, wa_ref, wd_ref, wb_ref, x_ref, dx1_ref, g_ref,
         sc_ref) = refs[:13]
        gx_ref, st_ref = refs[13 + n:15 + n]
        riding = (refs[13:13 + n], refs[15 + n:15 + 2 * n], *refs[15 + 2 * n:])

        @pl.when(pl.program_id(0) == 0)
        def _():
            st_ref[...] = jnp.zeros_like(st_ref)
            for cp in (_scatter_copies(*riding) if n else []):
                cp.start()

        dh = (_nn(dq_ref[...].astype(BF16), wa_ref[0:HEAD_W, :])
              + _nn(dk_ref[...].astype(BF16), wa_ref[HEAD_W:2 * HEAD_W, :])
              + _nn(dv_ref[...].astype(BF16), wa_ref[2 * HEAD_W:, :])
              + _nn(dxd_ref[...].astype(BF16), wd_ref[0:3 * HEAD_W, :])
              + _nn(dz_ref[...].astype(BF16), wd_ref[3 * HEAD_W:, :])
              + _nn(dba_ref[...].astype(BF16), wb_ref[...]))
        xhat, rstd, norm = _rms_fwd(x_ref[...], g_ref[...])
        dxn, dgain = _rms_bwd(dh * (1.0 + sc_ref[...]), xhat, rstd, g_ref[...])
        gx_ref[...] = dx1_ref[...] + dxn
        st_ref[0:1, :] += jnp.sum(dh, axis=0, keepdims=True)
        st_ref[1:2, :] += jnp.sum(dh * norm, axis=0, keepdims=True)
        st_ref[2:3, :] += dgain

        if n:
            @pl.when(pl.program_id(0) == n_steps - 1)
            def _():
                for cp in _scatter_copies(*riding):
                    cp.wait()

    def rows(width):
        return pl.BlockSpec((ROW_TILE, width), lambda i: (i, 0))

    def whole(a):
        return pl.BlockSpec(a.shape, lambda i: (0, 0))

    sems = [pltpu.SemaphoreType.DMA((3 * n,)), pltpu.SemaphoreType.DMA((3 * n,))] if n else []
    return pl.pallas_call(
        body,
        grid=(n_steps,),
        in_specs=[rows(HEAD_W), rows(HEAD_W), rows(HEAD_W), rows(3 * HEAD_W), rows(HEAD_W), rows(LANES),
                  whole(w_a), whole(w_d), whole(w_ba), rows(D_MODEL), rows(D_MODEL), _row(D_MODEL), _row(D_MODEL)]
        + [ANY] * n,
        out_specs=[rows(D_MODEL), pl.BlockSpec((8, D_MODEL), lambda i: (0, 0))] + [ANY] * n,
        out_shape=[jax.ShapeDtypeStruct((seq, D_MODEL), F32), jax.ShapeDtypeStruct((8, D_MODEL), F32)]
        + [jax.ShapeDtypeStruct(p.shape, p.dtype) for p in partials],
        scratch_shapes=sems,
        compiler_params=_params(("arbitrary",)),
        name="inproj_bwd",
    )(dq, dk, dv, dxd, dz, dba, w_a, w_d, w_ba, x, dx1, gain, scale, *partials)


def _weight_grad(a, b, name):
    seq, m = a.shape
    n = b.shape[1]
    tm = m if m <= 1536 else m // 2
    tn = n if n <= 1536 else n // 2
    n_k = seq // WGRAD_ROWS

    def body(a_ref, b_ref, out_ref):
        part = _tn(a_ref[...].astype(BF16), b_ref[...].astype(BF16))

        @pl.when(pl.program_id(2) == 0)
        def _():
            out_ref[...] = part

        @pl.when(pl.program_id(2) > 0)
        def _():
            out_ref[...] += part

    return pl.pallas_call(
        body,
        grid=(m // tm, n // tn, n_k),
        in_specs=[pl.BlockSpec((WGRAD_ROWS, tm), lambda i, j, k: (k, i)),
                  pl.BlockSpec((WGRAD_ROWS, tn), lambda i, j, k: (k, j))],
        out_specs=pl.BlockSpec((tm, tn), lambda i, j, k: (i, j)),
        out_shape=jax.ShapeDtypeStruct((m, n), F32),
        compiler_params=_params(("arbitrary", "arbitrary", "arbitrary")),
        name=name,
    )(a, b)


def _weight_grad_stack(pieces, b, name):
    seq, n = b.shape
    widths = [a.shape[1] for a in pieces]
    starts = [sum(widths[:i]) for i in range(len(pieces))]

    def body(*refs):
        a_refs, b_ref, out_ref = refs[:len(pieces)], refs[len(pieces)], refs[len(pieces) + 1]

        @pl.when(pl.program_id(0) == 0)
        def _():
            out_ref[...] = jnp.zeros_like(out_ref)

        bb = b_ref[...].astype(BF16)
        for a_ref, start, width in zip(a_refs, starts, widths):
            out_ref[start:start + width, :] += _tn(a_ref[...].astype(BF16), bb)

    def rows(width):
        return pl.BlockSpec((WGRAD_ROWS, width), lambda k: (k, 0))

    return pl.pallas_call(
        body,
        grid=(seq // WGRAD_ROWS,),
        in_specs=[rows(w) for w in widths] + [rows(n)],
        out_specs=pl.BlockSpec((sum(widths), n), lambda k: (0, 0)),
        out_shape=jax.ShapeDtypeStruct((sum(widths), n), F32),
        compiler_params=_params(("arbitrary",)),
        name=name,
    )(*pieces, b)


def _adamw(w, g, m, v, name):
    n_rows, n_cols = w.shape
    if n_rows % 256 == 0:
        block, grid, index = (256, n_cols), (n_rows // 256,), lambda i: (i, 0)
    elif n_cols % 256 == 0:
        block, grid, index = (n_rows, 256), (n_cols // 256,), lambda i: (0, i)
    else:
        block, grid, index = (n_rows, n_cols), (1,), lambda i: (0, 0)

    def body(w_ref, g_ref, m_ref, v_ref, d_ref, nm_ref, nv_ref):
        gv = g_ref[...]
        nm = ADAM_B1 * m_ref[...] + (1.0 - ADAM_B1) * gv
        nv = ADAM_B2 * v_ref[...] + (1.0 - ADAM_B2) * (gv * gv)
        m_hat = nm / (1.0 - ADAM_B1 ** ADAM_STEP)
        v_hat = nv / (1.0 - ADAM_B2 ** ADAM_STEP)
        d_ref[...] = -ADAM_LR * (m_hat / (jnp.sqrt(v_hat) + ADAM_EPS) + ADAM_WD * w_ref[...])
        nm_ref[...] = nm
        nv_ref[...] = nv

    blk = pl.BlockSpec(block, index)
    shape = jax.ShapeDtypeStruct((n_rows, n_cols), F32)
    return pl.pallas_call(
        body,
        grid=grid,
        in_specs=[blk] * 4,
        out_specs=[blk] * 3,
        out_shape=[shape] * 3,
        compiler_params=_params(("arbitrary",)),
        name=name,
    )(w, g, m, v)


IN_WIDTH = 3600
BA_COL = 7 * HEAD_W


def _local_step(x, target, mod, norm_attn_g, w_in, rel_bias, conv_w, a_log, dt_bias, delta_norm_g,
                norm_ffn_g, final_norm_g, shards, assemble, reduce_pairs):
    sh1, sc1, g1, sh2, sc2, g2 = [mod[:, i * D_MODEL:(i + 1) * D_MODEL] for i in range(6)]
    w_a = w_in[:3 * HEAD_W]
    w_d = w_in[3 * HEAD_W:BA_COL]
    w_ba = jnp.pad(w_in[BA_COL:], ((0, LANES - 2 * N_HEADS), (0, 0)))
    tables = jnp.asarray(_attn_tables())
    alog_row = jnp.pad(a_log, ((0, 0), (N_HEADS, LANES - 2 * N_HEADS)))
    dt_row = jnp.pad(dt_bias, ((0, 0), (N_HEADS, LANES - 2 * N_HEADS)))
    gain_row = jnp.tile(delta_norm_g, (1, N_HEADS))

    h1, qkv_a, qkvz, ba = _inproj_fwd(x, norm_attn_g, sc1, sh1, w_a, w_d, w_ba)
    bias = _attention_bias(rel_bias, tables)
    y_attn, lse, *gathered = _attention_fwd(qkv_a, bias, shards)
    w_out, w_gate, w_up, w_down = assemble(gathered)
    xs = _delta_prep_fwd(qkvz, ba, conv_w, alog_row, dt_row)
    inv_h, qk_h, u_h, w_h = _delta_chunk_fwd(xs)
    o, st_h = _delta_scan_fwd(xs, qk_h, u_h, w_h)
    y_delta = _delta_post_fwd(o, qkvz, gain_row)
    x1, h2, y = _outproj_fwd(y_attn, y_delta, w_out, x, g1, norm_ffn_g, sc2, sh2)
    gate, up, dx2, st_f = _ffn_fwd(h2, w_gate, w_up, w_down, x1, g2, final_norm_g, target)

    dgate, dup, act, dy2, dx1, dy, st_b = _ffn_bwd(dx2, gate, up, w_gate, w_up, w_down, x1, y, g2, g1, norm_ffn_g, sc2)
    partials = reduce_pairs([_weight_grad_stack([y_attn, y_delta], dy, "wgrad_out"),
                             _weight_grad(dgate, h2, "wgrad_gate"), _weight_grad(dup, h2, "wgrad_up"),
                             _weight_grad(act, dy2, "wgrad_down")], 1, "rest")
    grads = {}
    dycat = _outproj_bwd(dy, w_out)
    do, dz, dgain = _delta_post_bwd(dycat, o, qkvz, gain_row)
    dsn_h, dvn_h = _delta_scan_bwd(xs, qk_h, w_h, do)
    dxs = _delta_chunk_bwd(xs, inv_h, u_h, w_h, st_h, dsn_h, dvn_h, do)
    dconv, dba, dvec = _delta_prep_bwd(qkvz, ba, conv_w, alog_row, dt_row, dxs)
    dxd, grads["conv_w"] = _conv_bwd(dconv, qkvz, conv_w)
    dq, dk, dv, dbias, *scattered = _attention_bwd(qkv_a, dycat, y_attn, lse, bias, partials)
    partials_in = reduce_pairs([jnp.concatenate(
        [_weight_grad_stack([dq, dk, dv], h1, "wgrad_in_attn"),
         _weight_grad_stack([dxd, dz, dba], h1, "wgrad_in_delta")[:IN_WIDTH - 3 * HEAD_W]], axis=0)], 0, "in")
    grad_x, st_i, *scattered_in = _inproj_bwd(dq, dk, dv, dxd, dz, dba, w_a, w_d, w_ba, x, dx1, norm_attn_g, sc1,
                                              partials_in)
    grads["rel_bias"] = _rel_bias_grad(dbias, tables)[:, :N_BUCKETS].T
    grads["a_log"] = dvec[0:1, N_HEADS:2 * N_HEADS]
    grads["dt_bias"] = dvec[1:2, N_HEADS:2 * N_HEADS]
    grads["delta_norm_g"] = dgain[1:2, :HEAD_DIM]
    grads["norm_attn_g"] = st_i[2:3]
    grads["norm_ffn_g"] = st_b[2:3]
    grads["final_norm_g"] = st_f[0:1]
    dmod = jnp.concatenate([st_i[0:1], st_i[1:2], st_b[3:4], st_b[0:1], st_b[1:2], st_f[1:2]], axis=1)
    return st_f[3, 0], grad_x, grads, dmod, (partials_in + partials, scattered_in + scattered)


MESH = pl.DeviceIdType.MESH
OTHER_CHIPS = ((1, 0), (0, 1), (1, 1))
ALL_PEERS = tuple((m >> 2 & 1, m >> 1 & 1, m & 1) for m in range(1, 8))
ANY = pl.BlockSpec(memory_space=pl.ANY)
VMEM_SPEC = pl.BlockSpec(memory_space=pltpu.VMEM)


def _me():
    return lax.axis_index("x"), lax.axis_index("y"), lax.axis_index("c")


def _flip(pos, mask):
    return tuple(1 - p if m else p for p, m in zip(pos, mask))


def _remote(src, dst, send_sems, recv_sems, k, to):
    return pltpu.make_async_remote_copy(src_ref=src, dst_ref=dst, send_sem=send_sems.at[k], recv_sem=recv_sems.at[k],
                                        device_id=to, device_id_type=MESH)


def _ada_exchange(c8, w_ada, b_ada, conv8):
    def body(c_ref, w_ref, b_ref, cv_ref, mod_ref, cact_ref, conv_ref, c_all, part_all, send_sems, recv_sems):
        x, y, c = me = _me()
        dev = 4 * x + 2 * y + c
        chip = 2 * x + y
        c_all[dev] = c_ref[...]
        conv_ref[chip] = cv_ref[...]
        first = [_remote(c_ref, c_all.at[dev], send_sems, recv_sems, k, _flip(me, mask))
                 for k, mask in enumerate(ALL_PEERS)]
        first += [_remote(cv_ref, conv_ref.at[chip], send_sems, recv_sems, 7 + j, _flip(me, (*mask, 0)))
                  for j, mask in enumerate(OTHER_CHIPS)]
        for cp in first:
            cp.start()
        for cp in first:
            cp.wait()
        row = lax.broadcasted_iota(jnp.int32, (8, D_MODEL), 0)
        c_rows = jnp.zeros((8, D_MODEL), F32)
        for d in range(8):
            c_rows = jnp.where(row == d, c_all[d], c_rows)
        c_act = _silu(c_rows)
        cact_ref[...] = c_act
        part_all[chip] = _nn(c_act, w_ref[...], HIGHEST)
        second = [_remote(part_all.at[chip], part_all.at[chip], send_sems, recv_sems, 10 + j, _flip(me, (*mask, 0)))
                  for j, mask in enumerate(OTHER_CHIPS)]
        for cp in second:
            cp.start()
        for cp in second:
            cp.wait()
        cols = w_ref.shape[1]
        for k in range(4):
            mod_ref[:, k * cols:(k + 1) * cols] = part_all[k] + b_ref[:, k * cols:(k + 1) * cols]

    cols = w_ada.shape[1]
    return pl.pallas_call(
        body,
        in_specs=[VMEM_SPEC] * 4,
        out_specs=[VMEM_SPEC] * 3,
        out_shape=[jax.ShapeDtypeStruct((8, 4 * cols), F32), jax.ShapeDtypeStruct((8, D_MODEL), F32),
                   jax.ShapeDtypeStruct((4, 8, conv8.shape[1]), F32)],
        scratch_shapes=[pltpu.VMEM((8, 8, D_MODEL), F32), pltpu.VMEM((4, 8, cols), F32),
                        pltpu.SemaphoreType.DMA((13,)), pltpu.SemaphoreType.DMA((13,))],
        compiler_params=pltpu.CompilerParams(vmem_limit_bytes=VMEM_LIMIT),
        name="ada_exchange",
    )(c8, w_ada, b_ada, conv8)


def _gather_weights(shards):
    n = len(shards)

    def body(*refs):
        first, passed = _gather_copies(refs[:n], refs[n:2 * n], *refs[2 * n:])
        for cp in first:
            cp.start()
        for cp, fwd in zip(first, passed):
            cp.wait_recv()
            fwd.start()
        for cp in first:
            cp.wait_send()
        for fwd in passed:
            fwd.wait()

    return pl.pallas_call(
        body,
        in_specs=[ANY] * n,
        out_specs=[ANY] * n,
        out_shape=_gathered_shapes(shards),
        scratch_shapes=[pltpu.SemaphoreType.DMA((6 * n,)), pltpu.SemaphoreType.DMA((6 * n,))],
        name="gather_weights",
    )(*shards)


def _gathered_shapes(shards):
    return [jax.ShapeDtypeStruct((4, *s.shape), s.dtype) for s in shards]


def _gather_copies(srcs, dsts, send_sems, recv_sems):
    x, y, c = me = _me()
    chip = 2 * x + y
    sibling = _flip(me, (0, 0, 1))
    first, passed = [], []
    for a, (src, dst) in enumerate(zip(srcs, dsts)):
        for j, mask in enumerate(OTHER_CHIPS):
            to = _flip(me, (*mask, 0))
            first.append(_remote(src.at[c], dst.at[chip, c], send_sems, recv_sems, 6 * a + j, to))
            landed = dst.at[2 * to[0] + to[1], c]
            passed.append(_remote(landed, landed, send_sems, recv_sems, 6 * a + 3 + j, sibling))
    return first, passed


def _scatter_copies(srcs, dsts, send_sems, recv_sems):
    x, y, c = me = _me()
    chip = 2 * x + y
    copies = []
    for a, (src, dst) in enumerate(zip(srcs, dsts)):
        for j, mask in enumerate(OTHER_CHIPS):
            to = _flip(me, (*mask, 0))
            copies.append(_remote(src.at[2 * to[0] + to[1]], dst.at[chip], send_sems, recv_sems, 3 * a + j, to))
    return copies


def _start_and_wait(copies):
    for cp in copies:
        cp.start()
    for cp in copies:
        cp.wait()


def _swap_halves(grads):
    n = len(grads)

    def body(*refs):
        srcs, got = refs[:n], refs[n:2 * n]
        send_sems, recv_sems = refs[2 * n:]
        x, y, c = me = _me()
        _start_and_wait([_remote(srcs[a].at[:, 1 - c], got[a], send_sems, recv_sems, a, _flip(me, (0, 0, 1)))
                         for a in range(n)])

    return pl.pallas_call(
        body,
        in_specs=[ANY] * n,
        out_specs=[ANY] * n,
        out_shape=[jax.ShapeDtypeStruct((4, g.shape[2], g.shape[3]), g.dtype) for g in grads],
        scratch_shapes=[pltpu.SemaphoreType.DMA((n,)), pltpu.SemaphoreType.DMA((n,))],
        name=f"swap_halves_{n}",
    )(*grads)


def _join_halves(halves):
    n = len(halves)

    def body(*refs):
        srcs, dsts = refs[:n], refs[n:2 * n]
        send_sems, recv_sems = refs[2 * n:]
        x, y, c = me = _me()
        _start_and_wait([_remote(srcs[a], dsts[a].at[c], send_sems, recv_sems, a, _flip(me, (0, 0, 1)))
                         for a in range(n)])

    return pl.pallas_call(
        body,
        in_specs=[ANY] * n,
        out_specs=[ANY] * n,
        out_shape=[jax.ShapeDtypeStruct((2, *h.shape), h.dtype) for h in halves],
        scratch_shapes=[pltpu.SemaphoreType.DMA((n,)), pltpu.SemaphoreType.DMA((n,))],
        name=f"join_halves_{n}",
    )(*halves)


def _gather_small(packed):
    n_rows = packed.shape[0]

    def body(p_ref, all_ref, sum_ref, send_sems, recv_sems):
        x, y, c = me = _me()
        dev = 4 * x + 2 * y + c
        all_ref[dev] = p_ref[...]
        copies = [_remote(p_ref, all_ref.at[dev], send_sems, recv_sems, k, _flip(me, mask))
                  for k, mask in enumerate(ALL_PEERS)]
        for cp in copies:
            cp.start()
        for cp in copies:
            cp.wait()
        total = all_ref[0]
        for d in range(1, 8):
            total = total + all_ref[d]
        sum_ref[...] = total

    return pl.pallas_call(
        body,
        in_specs=[VMEM_SPEC],
        out_specs=[VMEM_SPEC, VMEM_SPEC],
        out_shape=[jax.ShapeDtypeStruct((8, n_rows, LANES), F32), jax.ShapeDtypeStruct((n_rows, LANES), F32)],
        scratch_shapes=[pltpu.SemaphoreType.DMA((7,)), pltpu.SemaphoreType.DMA((7,))],
        name="gather_small",
    )(packed)


def _add_pair(a, b, out_dtype, name):
    def body(a_ref, b_ref, o_ref):
        o_ref[...] = (a_ref[...] + b_ref[...]).astype(o_ref.dtype)

    blk = pl.BlockSpec((1, *a.shape[1:]), lambda i: (i, 0, 0))
    return pl.pallas_call(
        body, grid=(a.shape[0],), in_specs=[blk, blk], out_specs=blk,
        out_shape=jax.ShapeDtypeStruct(a.shape, out_dtype),
        compiler_params=_params(("arbitrary",)), name=name,
    )(a, b)


def _add_slots(a, name):
    def body(a_ref, o_ref):
        total = a_ref[0].astype(F32)
        for k in range(1, 4):
            total = total + a_ref[k].astype(F32)
        o_ref[...] = total

    return pl.pallas_call(
        body, in_specs=[VMEM_SPEC], out_specs=VMEM_SPEC,
        out_shape=jax.ShapeDtypeStruct(a.shape[1:], F32),
        compiler_params=pltpu.CompilerParams(vmem_limit_bytes=VMEM_LIMIT), name=name,
    )(a)


def _ada_weight_grad(c_act, dmod_cols):
    def body(c_ref, d_ref, o_ref):
        o_ref[...] = _tn(c_ref[...], d_ref[...], HIGHEST)

    return pl.pallas_call(
        body, in_specs=[VMEM_SPEC, VMEM_SPEC], out_specs=VMEM_SPEC,
        out_shape=jax.ShapeDtypeStruct((c_act.shape[1], dmod_cols.shape[1]), F32),
        compiler_params=pltpu.CompilerParams(vmem_limit_bytes=VMEM_LIMIT), name="ada_weight_grad",
    )(c_act, dmod_cols)


def kernel(x, c, w_ada, b_ada, norm_attn_g, w_in, rel_bias, conv_w, a_log, dt_bias, delta_norm_g, w_out, norm_ffn_g, w_gate, w_up, w_down, final_norm_g, loss_target, m_w_ada, m_b_ada, m_norm_attn_g, m_w_in, m_rel_bias, m_conv_w, m_a_log, m_dt_bias, m_delta_norm_g, m_w_out, m_norm_ffn_g, m_w_gate, m_w_up, m_w_down, m_final_norm_g, v_w_ada, v_b_ada, v_norm_attn_g, v_w_in, v_rel_bias, v_conv_w, v_a_log, v_dt_bias, v_delta_norm_g, v_w_out, v_norm_ffn_g, v_w_gate, v_w_up, v_w_down, v_final_norm_g):
    xi, yi, ci = _me()
    dev = 4 * xi + 2 * yi + ci
    chip = 2 * xi + yi

    conv_cols = conv_w.shape[2]
    mod_all, c_act, conv_all = _ada_exchange(jnp.broadcast_to(c, (8, D_MODEL)), w_ada[0], b_ada,
                                             jnp.pad(conv_w[0], ((0, 4), (0, 0))))
    mod = lax.dynamic_slice_in_dim(mod_all, dev, 1, axis=0)
    conv_full = jnp.swapaxes(conv_all[:, :4, :], 0, 1).reshape(4, 4 * conv_cols)

    big_names = ("w_in", "w_out", "w_gate", "w_up", "w_down")
    by_cols = (True, False, True, True, False)

    def rows_form(a, cols):
        return jnp.swapaxes(a[0], 0, 1) if cols else a[0]

    def halves_form(w):
        rows, lanes = w.shape
        if (rows // 2) % 16:
            rows, lanes = w.size // LANES, LANES
        return (2, rows // 2, lanes)

    big = [rows_form(w, cols) for w, cols in zip((w_in, w_out, w_gate, w_up, w_down), by_cols)]
    shards = [w.astype(BF16).reshape(halves_form(w)) for w in big]

    def assemble(gathered, first):
        return [lax.dynamic_update_index_in_dim(g, s, chip, 0).reshape(4 * w.shape[0], w.shape[1])
                for g, s, w in zip(gathered, shards[first:], big[first:])]

    def reduce_pairs(grads, first, tag):
        slots = [g.reshape(4, *halves_form(w)) for g, w in zip(grads, big[first:])]
        return [_add_pair(lax.dynamic_index_in_dim(s, ci, 1, keepdims=False), got, BF16, f"add_pair_{tag}{a}")
                for a, (s, got) in enumerate(zip(slots, _swap_halves(slots)))]

    def finish(partials, scattered, first, tag):
        by_source = [lax.dynamic_update_index_in_dim(b, lax.dynamic_index_in_dim(p, chip, 0, keepdims=False), chip, 0)
                     for b, p in zip(scattered, partials)]
        halves = [_add_slots(p, f"add_slots_{tag}{a}") for a, p in enumerate(by_source)]
        joined = [lax.dynamic_update_index_in_dim(j, h, ci, 0) for j, h in zip(_join_halves(halves), halves)]
        return [j.reshape(w.shape) for j, w in zip(joined, big[first:])]

    whole_in, = assemble(_gather_weights(shards[:1]), 0)
    loss, grad_x, grads, dmod, (partials, scattered) = _local_step(
        x[0], loss_target[0], mod, norm_attn_g, whole_in, rel_bias, conv_full, a_log, dt_bias, delta_norm_g,
        norm_ffn_g, final_norm_g[None], shards[1:], functools.partial(assemble, first=1), reduce_pairs)

    big_grads = finish(partials, scattered, 0, "all")

    pieces = [dmod, grads["conv_w"], grads["norm_attn_g"], grads["norm_ffn_g"], grads["final_norm_g"],
              grads["rel_bias"], grads["a_log"], grads["dt_bias"], grads["delta_norm_g"]]
    flat = [jnp.pad(p.reshape(-1), (0, -p.size % LANES)) for p in pieces]
    n_rows = [f.size // LANES for f in flat]
    packed = jnp.concatenate(flat).reshape(-1, LANES)
    packed = jnp.pad(packed, ((0, -packed.shape[0] % 8), (0, 0)))
    all_small, total = _gather_small(packed)
    sums, start = [], 0
    for p, n in zip(pieces, n_rows):
        sums.append(total[start:start + n].reshape(-1)[:p.size].reshape(p.shape))
        start += n
    g_b_ada, g_conv, g_norm_attn, g_norm_ffn, g_final, g_rel, g_alog, g_dt, g_dnorm = sums
    dmod_all = all_small[:, :n_rows[0], :].reshape(8, -1)
    ada_cols = w_ada.shape[2]
    g_w_ada = _ada_weight_grad(c_act, lax.dynamic_slice_in_dim(dmod_all, chip * ada_cols, ada_cols, axis=1))
    g_conv = lax.dynamic_slice_in_dim(g_conv, chip * conv_cols, conv_cols, axis=1)

    grad = {"w_ada": g_w_ada[None], "b_ada": g_b_ada, "norm_attn_g": g_norm_attn,
            "rel_bias": g_rel, "conv_w": g_conv[None], "a_log": g_alog, "dt_bias": g_dt, "delta_norm_g": g_dnorm,
            "norm_ffn_g": g_norm_ffn, "final_norm_g": g_final.reshape(-1)}
    weight = {"w_ada": w_ada, "b_ada": b_ada, "norm_attn_g": norm_attn_g, "w_in": w_in, "rel_bias": rel_bias,
              "conv_w": conv_w, "a_log": a_log, "dt_bias": dt_bias, "delta_norm_g": delta_norm_g, "w_out": w_out,
              "norm_ffn_g": norm_ffn_g, "w_gate": w_gate, "w_up": w_up, "w_down": w_down, "final_norm_g": final_norm_g}
    first = {"w_ada": m_w_ada, "b_ada": m_b_ada, "norm_attn_g": m_norm_attn_g, "w_in": m_w_in, "rel_bias": m_rel_bias,
             "conv_w": m_conv_w, "a_log": m_a_log, "dt_bias": m_dt_bias, "delta_norm_g": m_delta_norm_g,
             "w_out": m_w_out, "norm_ffn_g": m_norm_ffn_g, "w_gate": m_w_gate, "w_up": m_w_up, "w_down": m_w_down,
             "final_norm_g": m_final_norm_g}
    second = {"w_ada": v_w_ada, "b_ada": v_b_ada, "norm_attn_g": v_norm_attn_g, "w_in": v_w_in, "rel_bias": v_rel_bias,
              "conv_w": v_conv_w, "a_log": v_a_log, "dt_bias": v_dt_bias, "delta_norm_g": v_delta_norm_g,
              "w_out": v_w_out, "norm_ffn_g": v_norm_ffn_g, "w_gate": v_w_gate, "w_up": v_w_up, "w_down": v_w_down,
              "final_norm_g": v_final_norm_g}
    delta, new_m, new_v = {}, {}, {}
    for name, w in weight.items():
        if name in big_names:
            continue
        two_d = (-1, w.shape[-1])
        d, nm, nv = _adamw(w.reshape(two_d), grad[name].reshape(two_d), first[name].reshape(two_d),
                           second[name].reshape(two_d), f"adamw_{name}")
        delta[name], new_m[name], new_v[name] = d.reshape(w.shape), nm.reshape(w.shape), nv.reshape(w.shape)
    for name, w, g, cols in zip(big_names, big, big_grads, by_cols):
        outs = _adamw(w, g, rows_form(first[name], cols), rows_form(second[name], cols), f"adamw_{name}")
        grad[name], delta[name], new_m[name], new_v[name] = [
            (jnp.swapaxes(o, 0, 1) if cols else o)[None] for o in (g, *outs)]

    names = list(weight)
    return (lax.psum(loss, ("x", "y", "c")), grad_x[None], *[grad[n] for n in names], *[delta[n] for n in names],
            *[new_m[n] for n in names], *[new_v[n] for n in names])
```

```python
import functools
import math

import numpy as np
import jax
import jax.numpy as jnp
from jax import lax
from jax.experimental import pallas as pl
from jax.experimental.pallas import tpu as pltpu

F32 = jnp.float32
BF16 = jnp.bfloat16
HIGHEST = lax.Precision.HIGHEST

D_MODEL = 1024
HEAD_DIM = 64
N_HEADS = 8
HEAD_W = 512
BRANCHES = ((128, 1), (512, 4), (2048, 16))
BAND = 128
ATT_TILE = 2048
ATT_UNROLL = 8
ATT_UNROLL_BWD = 4
N_BUCKETS = 32
MAX_DISTANCE = 2048
CHUNK = 64
D_FF = 2816
EPS = 1e-6
NEG_INF = -1e30
LANES = 128
VMEM_LIMIT = 56 * 1024 * 1024

ADAM_LR = 0.001
ADAM_B1 = 0.9
ADAM_B2 = 0.999
ADAM_EPS = 1e-08
ADAM_WD = 0.01
ADAM_STEP = 10


def _nn(a, b, precision=None):
    return jnp.dot(a, b, preferred_element_type=F32, precision=precision)


def _nt(a, b, precision=None):
    return lax.dot_general(a, b, (((1,), (1,)), ((), ())), preferred_element_type=F32, precision=precision)


def _tn(a, b, precision=None):
    return lax.dot_general(a, b, (((0,), (0,)), ((), ())), preferred_element_type=F32, precision=precision)


def _params(sem, vmem=VMEM_LIMIT):
    return pltpu.CompilerParams(dimension_semantics=sem, vmem_limit_bytes=vmem)


def _sigmoid(x):
    return 0.5 * jnp.tanh(0.5 * x) + 0.5


def _silu_and_slope(x):
    s = _sigmoid(x)
    return x * s, s * (1.0 + x * (1.0 - s))


def _silu(x):
    return x * _sigmoid(x)


def _attn_tables():
    qi = np.arange(BAND)[:, None]
    kj = np.arange(2 * BAND)[None, :]
    steps = qi + BAND - kj
    in_window = (steps >= 0) & (steps <= BAND)
    max_exact = N_BUCKETS // 2
    out = np.zeros((3, 2, BAND, 2 * BAND), np.int32)
    for b, (_, dil) in enumerate(BRANCHES):
        dist = np.maximum(steps, 0) * dil
        dist_f = np.maximum(dist, 1).astype(np.float32)
        large = max_exact + (np.log(dist_f / np.float32(max_exact)) / np.float32(math.log(MAX_DISTANCE / max_exact))
                             * np.float32(N_BUCKETS - max_exact)).astype(np.int32)
        bucket = np.where(dist < max_exact, dist, np.minimum(large, N_BUCKETS - 1)).astype(np.int32)
        out[b, 0] = np.where(in_window, bucket, -1)
        out[b, 1] = np.where(in_window & (kj >= BAND), bucket, -1)
    return out


def _attention_bias(rel_bias, tables):
    def body(rel_ref, tab_ref, out_ref):
        head = pl.program_id(0)
        for b in range(3):
            tab = tab_ref[b, 0]

            def pick(kk, acc, tab=tab):
                return jnp.where(tab == kk, rel_ref[kk, head], acc)

            acc = lax.fori_loop(0, N_BUCKETS, pick, jnp.zeros((BAND, 2 * BAND), F32))
            for first in range(2):
                out_ref[0, b, first] = jnp.where(tab_ref[b, first] < 0, NEG_INF, acc)

    return pl.pallas_call(
        body,
        grid=(N_HEADS,),
        in_specs=[pl.BlockSpec(memory_space=pltpu.SMEM),
                  pl.BlockSpec((3, 2, BAND, 2 * BAND), lambda h: (0, 0, 0, 0))],
        out_specs=pl.BlockSpec((1, 3, 2, BAND, 2 * BAND), lambda h: (h, 0, 0, 0, 0)),
        out_shape=jax.ShapeDtypeStruct((N_HEADS, 3, 2, BAND, 2 * BAND), F32),
        compiler_params=_params(("arbitrary",)),
        name="attn_bias",
    )(rel_bias, tables)


def _bias_spec():
    return pl.BlockSpec((2, 3, 2, BAND, 2 * BAND), lambda p, t: (p, 0, 0, 0, 0))


def _attn_block_index(idx, t, r):
    nb = ATT_TILE // (BAND * r)
    rho = idx // nb
    n = idx % nb
    qs = rho + r * BAND * n
    gs = t * ATT_TILE + qs
    first = (t * nb + n) == 0
    ps = jnp.where(first, gs, gs - r * BAND)
    return qs, gs, ps, first.astype(jnp.int32)


def _rows(start, r):
    return pl.ds(start, BAND) if r == 1 else pl.ds(start, BAND, stride=r)


def _attention_fwd(qkv, bias, shards):
    seq = qkv.shape[0]
    n_tiles = seq // ATT_TILE
    n = len(shards)

    def body(*refs):
        bias_ref, q_ref, k_ref, v_ref = refs[:4]
        y_ref, lse_ref = refs[4 + n:6 + n]
        o_s, l_s = refs[6 + 2 * n:8 + 2 * n]
        riding = (refs[4:4 + n], refs[6 + n:6 + 2 * n], *refs[8 + 2 * n:])
        pair = pl.program_id(0)
        t = pl.program_id(1)
        if n:
            @pl.when((pair == 0) & (t == 0))
            def _():
                for cp in _gather_copies(*riding, hand_over=False)[0]:
                    cp.start()

            @pl.when((pair == 2) & (t == 0))
            def _():
                for cp, fwd in zip(*_gather_copies(*riding)):
                    cp.wait_recv()
                    fwd.start()

        lane = lax.broadcasted_iota(jnp.int32, (1, LANES), 1)
        head0 = lane < HEAD_DIM
        masks = (head0, jnp.logical_not(head0))
        ones = jnp.ones((2 * BAND, LANES), BF16)
        for b, (_, r) in enumerate(BRANCHES):
            def blocks(it, carry, b=b, r=r):
                idx = [_attn_block_index(it * ATT_UNROLL + j, t, r) for j in range(ATT_UNROLL)]
                qb = [q_ref[_rows(qs, r), :] * (HEAD_DIM ** -0.5) for qs, _, _, _ in idx]
                kcat = [jnp.concatenate([k_ref[_rows(ps, r), :], k_ref[_rows(gs, r), :]], axis=0).astype(BF16)
                        for _, gs, ps, _ in idx]
                vcat = [jnp.concatenate([v_ref[_rows(ps, r), :], v_ref[_rows(gs, r), :]], axis=0).astype(BF16)
                        for _, gs, ps, _ in idx]
                work = [(j, hh) for j in range(ATT_UNROLL) for hh in range(2)]
                s = [_nt(jnp.where(masks[hh], qb[j], 0.0).astype(BF16), kcat[j]) + bias_ref[hh, b, idx[j][3]]
                     for j, hh in work]
                m = [jnp.max(sv, axis=-1, keepdims=True) for sv in s]
                e = [jnp.exp(sv - mv) for sv, mv in zip(s, m)]
                eb = [ev.astype(BF16) for ev in e]
                den = [_nn(ev, ones) for ev in eb]
                out = [_nn(ev, vcat[j]) / dv for ev, dv, (j, _) in zip(eb, den, work)]
                lse = [mv + jnp.log(dv) for mv, dv in zip(m, den)]
                for j in range(ATT_UNROLL):
                    o_s[b, _rows(idx[j][0], r), :] = jnp.where(head0, out[2 * j], out[2 * j + 1])
                    l_s[b, _rows(idx[j][0], r), :] = jnp.where(head0, lse[2 * j], lse[2 * j + 1])
                return carry

            lax.fori_loop(0, ATT_TILE // BAND // ATT_UNROLL, blocks, 0)

        def merge(i, carry):
            rows = pl.ds(pl.multiple_of(i * BAND, BAND), BAND)
            l0, l1, l2 = l_s[0, rows, :], l_s[1, rows, :], l_s[2, rows, :]
            m = jnp.maximum(jnp.maximum(l0, l1), l2)
            w0, w1, w2 = jnp.exp(l0 - m), jnp.exp(l1 - m), jnp.exp(l2 - m)
            tot = w0 + w1 + w2
            y_ref[rows, :] = (w0 * o_s[0, rows, :] + w1 * o_s[1, rows, :] + w2 * o_s[2, rows, :]) / tot
            lse_ref[rows, :] = m + jnp.log(tot)
            return carry

        lax.fori_loop(0, ATT_TILE // BAND, merge, 0)

        if n:
            @pl.when((pair == N_HEADS // 2 - 1) & (t == n_tiles - 1))
            def _():
                first, passed = _gather_copies(*riding)
                for cp in first:
                    cp.wait_send()
                for fwd in passed:
                    fwd.wait()

    tile = pl.BlockSpec((ATT_TILE, LANES), lambda p, t: (t, p))
    sems = [pltpu.SemaphoreType.DMA((6 * n,)), pltpu.SemaphoreType.DMA((6 * n,))] if n else []
    return pl.pallas_call(
        body,
        grid=(N_HEADS // 2, n_tiles),
        in_specs=[
            _bias_spec(),
            pl.BlockSpec((ATT_TILE, LANES), lambda p, t: (t, p)),
            pl.BlockSpec((seq, LANES), lambda p, t: (0, 4 + p)),
            pl.BlockSpec((seq, LANES), lambda p, t: (0, 8 + p)),
        ] + [ANY] * n,
        out_specs=[tile, tile] + [ANY] * n,
        out_shape=[jax.ShapeDtypeStruct((seq, HEAD_W), F32), jax.ShapeDtypeStruct((seq, HEAD_W), F32)]
        + _gathered_shapes(shards),
        scratch_shapes=[
            pltpu.VMEM((3, ATT_TILE, LANES), F32),
            pltpu.VMEM((3, ATT_TILE, LANES), F32),
        ] + sems,
        compiler_params=_params(("arbitrary", "arbitrary")),
        name="attn_fwd",
    )(bias, qkv, qkv, qkv, *shards)


def _attention_bwd(qkv, dy, y, lse, bias, partials):
    seq = qkv.shape[0]
    n_tiles = seq // ATT_TILE
    n = len(partials)

    def body(*refs):
        bias_ref, q_ref, k_ref, v_ref, dy_ref, y_ref, lse_ref = refs[:7]
        dq_ref, dk_ref, dv_ref, dbias_ref = refs[7 + n:11 + n]
        riding = (refs[7:7 + n], refs[11 + n:11 + 2 * n], *refs[11 + 2 * n:])
        pair = pl.program_id(0)
        t = pl.program_id(1)
        if n:
            @pl.when((pair == 0) & (t == 0))
            def _():
                for cp in _scatter_copies(*riding):
                    cp.start()

        lane = lax.broadcasted_iota(jnp.int32, (1, LANES), 1)
        head0 = lane < HEAD_DIM

        @pl.when(t == 0)
        def _():
            dk_ref[...] = jnp.zeros_like(dk_ref)
            dv_ref[...] = jnp.zeros_like(dv_ref)
            dbias_ref[...] = jnp.zeros_like(dbias_ref)

        dq_ref[...] = jnp.zeros_like(dq_ref)

        masks = (head0, jnp.logical_not(head0))
        ones = jnp.ones((LANES, LANES), BF16)
        scale = HEAD_DIM ** -0.5
        for b, (_, r) in enumerate(BRANCHES):
            def blocks(it, carry, b=b, r=r):
                idx = [_attn_block_index(it * ATT_UNROLL_BWD + j, t, r) for j in range(ATT_UNROLL_BWD)]
                qb = [q_ref[_rows(qs, r), :] * scale for qs, _, _, _ in idx]
                kcat = [jnp.concatenate([k_ref[_rows(ps, r), :], k_ref[_rows(gs, r), :]], axis=0).astype(BF16)
                        for _, gs, ps, _ in idx]
                vcat = [jnp.concatenate([v_ref[_rows(ps, r), :], v_ref[_rows(gs, r), :]], axis=0).astype(BF16)
                        for _, gs, ps, _ in idx]
                dob = [dy_ref[_rows(qs, r), :] for qs, _, _, _ in idx]
                ob = [y_ref[_rows(qs, r), :] for qs, _, _, _ in idx]
                lb = [lse_ref[_rows(qs, r), :] for qs, _, _, _ in idx]
                work = [(j, hh) for j in range(ATT_UNROLL_BWD) for hh in range(2)]
                qh = [jnp.where(masks[hh], qb[j], 0.0).astype(BF16) for j, hh in work]
                doh = [jnp.where(masks[hh], dob[j], 0.0) for j, hh in work]
                dohb = [d.astype(BF16) for d in doh]
                s = [_nt(qh[w], kcat[j]) + bias_ref[hh, b, idx[j][3]] for w, (j, hh) in enumerate(work)]
                dp = [_nt(dohb[w], vcat[j]) for w, (j, _) in enumerate(work)]
                lrot = [pltpu.roll(lv, HEAD_DIM, 1) for lv in lb]
                lcol = [jnp.where(masks[hh], lb[j], lrot[j]) for j, hh in work]
                parts = [_split(doh[w] * ob[j]) for w, (j, _) in enumerate(work)]
                delta = [_nn(hi, ones) + _nn(lo, ones) for hi, lo in parts]
                prob = [jnp.exp(sv - jnp.concatenate([lv, lv], axis=1)) for sv, lv in zip(s, lcol)]
                ds = [pv * (dv - jnp.concatenate([de, de], axis=1)) for pv, dv, de in zip(prob, dp, delta)]
                dsb = [d.astype(BF16) for d in ds]
                dq = [_nn(dsb[w], kcat[j]) for w, (j, _) in enumerate(work)]
                dkc = [_tn(dsb[w], qh[w]) for w in range(len(work))]
                dvc = [_tn(prob[w].astype(BF16), dohb[w]) for w in range(len(work))]
                for hh in range(2):
                    dbias_ref[0, b, hh] += sum(ds[w] for w, (_, head) in enumerate(work) if head == hh)
                for j in range(ATT_UNROLL_BWD):
                    qs, gs, ps, _ = idx[j]
                    dkcat = dkc[2 * j] + dkc[2 * j + 1]
                    dvcat = dvc[2 * j] + dvc[2 * j + 1]
                    dq_ref[_rows(qs, r), :] += jnp.where(head0, dq[2 * j], dq[2 * j + 1]) * scale
                    dk_ref[_rows(ps, r), :] += dkcat[:BAND]
                    dk_ref[_rows(gs, r), :] += dkcat[BAND:]
                    dv_ref[_rows(ps, r), :] += dvcat[:BAND]
                    dv_ref[_rows(gs, r), :] += dvcat[BAND:]
                return carry

            lax.fori_loop(0, ATT_TILE // BAND // ATT_UNROLL_BWD, blocks, 0)

        if n:
            @pl.when((pair == N_HEADS // 2 - 1) & (t == n_tiles - 1))
            def _():
                for cp in _scatter_copies(*riding):
                    cp.wait()

    tile = pl.BlockSpec((ATT_TILE, LANES), lambda p, t: (t, p))
    full = pl.BlockSpec((seq, LANES), lambda p, t: (0, p))
    sems = [pltpu.SemaphoreType.DMA((3 * n,)), pltpu.SemaphoreType.DMA((3 * n,))] if n else []
    return pl.pallas_call(
        body,
        grid=(N_HEADS // 2, n_tiles),
        in_specs=[
            _bias_spec(),
            pl.BlockSpec((ATT_TILE, LANES), lambda p, t: (t, p)),
            pl.BlockSpec((seq, LANES), lambda p, t: (0, 4 + p)),
            pl.BlockSpec((seq, LANES), lambda p, t: (0, 8 + p)),
            tile, tile, tile,
        ] + [ANY] * n,
        out_specs=[tile, full, full,
                   pl.BlockSpec((1, 3, 2, BAND, 2 * BAND), lambda p, t: (p, 0, 0, 0, 0))] + [ANY] * n,
        out_shape=[jax.ShapeDtypeStruct((seq, HEAD_W), F32)] * 3
        + [jax.ShapeDtypeStruct((N_HEADS // 2, 3, 2, BAND, 2 * BAND), F32)]
        + [jax.ShapeDtypeStruct(p.shape, p.dtype) for p in partials],
        scratch_shapes=sems,
        compiler_params=_params(("arbitrary", "arbitrary")),
        name="attn_bwd",
    )(bias, qkv, qkv, qkv, dy, y, lse, *partials)


def _rel_bias_grad(dbias, tables):
    def body(tab_ref, db_ref, out_ref):
        lane = lax.broadcasted_iota(jnp.int32, (1, LANES), 1)
        out_ref[...] = jnp.zeros_like(out_ref)
        for b in range(3):
            tab = tab_ref[b, 0]

            def head(h, carry, b=b, tab=tab):
                d = db_ref[h // 2, b, h % 2]
                sums = [jnp.sum(jnp.where(tab == kk, d, 0.0), keepdims=True) for kk in range(N_BUCKETS)]
                row = jnp.zeros((1, LANES), F32)
                for kk, s in enumerate(sums):
                    row = row + jnp.where(lane == kk, s, 0.0)
                out_ref[pl.ds(h, 1), :] += row
                return carry

            lax.fori_loop(0, N_HEADS, head, 0)

    return pl.pallas_call(
        body,
        out_shape=jax.ShapeDtypeStruct((N_HEADS, LANES), F32),
        compiler_params=pltpu.CompilerParams(vmem_limit_bytes=VMEM_LIMIT),
        name="rel_bias_grad",
    )(tables, dbias)


ROW_TILE = 512


def _head_sum_matrix():
    return (lax.broadcasted_iota(jnp.int32, (HEAD_W, LANES), 0) // HEAD_DIM
            == lax.broadcasted_iota(jnp.int32, (HEAD_W, LANES), 1)).astype(F32)


def _head_spread_matrix(offset=0):
    return (lax.broadcasted_iota(jnp.int32, (LANES, HEAD_W), 0)
            == lax.broadcasted_iota(jnp.int32, (LANES, HEAD_W), 1) // HEAD_DIM + offset).astype(F32)


def _head_gather_matrix(offset=0):
    return (lax.broadcasted_iota(jnp.int32, (HEAD_W, LANES), 0) // HEAD_DIM + offset
            == lax.broadcasted_iota(jnp.int32, (HEAD_W, LANES), 1)).astype(F32)


def _split3(x):
    hi = x.astype(BF16)
    rest = x - hi.astype(F32)
    mid = rest.astype(BF16)
    return hi, mid, (rest - mid.astype(F32)).astype(BF16)


def _pick(x, onehot):
    m = onehot.astype(BF16)
    hi, mid, lo = _split3(x)
    return _nn(hi, m) + (_nn(mid, m) + _nn(lo, m))


def _pick_left(onehot, x):
    m = onehot.astype(BF16)
    hi, mid, lo = _split3(x)
    return _nn(m, hi) + (_nn(m, mid) + _nn(m, lo))


def _tri(lower, strict=False):
    r = lax.broadcasted_iota(jnp.int32, (CHUNK, CHUNK), 0)
    c = lax.broadcasted_iota(jnp.int32, (CHUNK, CHUNK), 1)
    if lower:
        return (c < r) if strict else (c <= r)
    return c >= r


def _softplus(z):
    return jnp.maximum(z, 0.0) + jnp.log(1.0 + jnp.exp(-jnp.abs(z)))


def _conv_taps(stage, w_ref, rows):
    return (w_ref[3:4, :] * stage[8:8 + rows, :] + w_ref[2:3, :] * stage[7:7 + rows, :]
            + w_ref[1:2, :] * stage[6:6 + rows, :] + w_ref[0:1, :] * stage[5:5 + rows, :])


def _l2_scale(xc, hsum, hspread):
    ssq = _pick(xc * xc, hsum)
    return _pick(lax.rsqrt(ssq + EPS), hspread)


def _stage_rows(stage, x_ref, xp_ref, i):
    stage[0:8, :] = jnp.where(i == 0, 0.0, xp_ref[...])
    stage[8:8 + ROW_TILE, :] = x_ref[...]


def _delta_prep_fwd(qkvz, ba, conv_w, alog_row, dt_row):
    seq = qkvz.shape[0]
    qkv_w = 3 * HEAD_W

    def body(x_ref, xp_ref, ba_ref, w_ref, al_ref, dt_ref, out_ref, stage):
        i = pl.program_id(0)
        _stage_rows(stage, x_ref, xp_ref, i)
        act = _silu(_conv_taps(stage, w_ref, ROW_TILE))
        hsum, hspread = _head_sum_matrix(), _head_spread_matrix()
        qc, kc = act[:, :HEAD_W], act[:, HEAD_W:2 * HEAD_W]
        out_ref[0] = qc * _l2_scale(qc, hsum, hspread) * (HEAD_DIM ** -0.5)
        out_ref[1] = kc * _l2_scale(kc, hsum, hspread)
        out_ref[2] = act[:, 2 * HEAD_W:]
        bav = ba_ref[...]
        out_ref[3] = _pick(_sigmoid(bav), hspread)
        g8 = -jnp.exp(al_ref[...]) * _softplus(bav + dt_ref[...])
        gb = _pick(g8, _head_spread_matrix(N_HEADS))
        cum = _tri(True).astype(F32)
        for ch in range(ROW_TILE // CHUNK):
            rows = slice(ch * CHUNK, (ch + 1) * CHUNK)
            out_ref[4, rows, :] = _pick_left(cum, gb[rows])

    return pl.pallas_call(
        body,
        grid=(seq // ROW_TILE,),
        in_specs=[
            pl.BlockSpec((ROW_TILE, qkv_w), lambda i: (i, 0)),
            pl.BlockSpec((8, qkv_w), lambda i: (jnp.maximum(i * (ROW_TILE // 8) - 1, 0), 0)),
            pl.BlockSpec((ROW_TILE, LANES), lambda i: (i, 0)),
            pl.BlockSpec((4, qkv_w), lambda i: (0, 0)),
            pl.BlockSpec((1, LANES), lambda i: (0, 0)),
            pl.BlockSpec((1, LANES), lambda i: (0, 0)),
        ],
        out_specs=pl.BlockSpec((5, ROW_TILE, HEAD_W), lambda i: (0, i, 0)),
        out_shape=jax.ShapeDtypeStruct((5, seq, HEAD_W), F32),
        scratch_shapes=[pltpu.VMEM((ROW_TILE + 8, qkv_w), F32)],
        compiler_params=_params(("arbitrary",)),
        name="delta_prep_fwd",
    )(qkvz, qkvz, ba, conv_w, alog_row, dt_row)


def _split(x):
    hi = x.astype(BF16)
    return hi, (x - hi.astype(F32)).astype(BF16)


def _dot3(a, b, dot=_nn):
    return dot(a[0], b[0]) + (dot(a[0], b[1]) + dot(a[1], b[0]))


def _unit_lower_inverses(mats):
    eye = (lax.broadcasted_iota(jnp.int32, (CHUNK, CHUNK), 0)
           == lax.broadcasted_iota(jnp.int32, (CHUNK, CHUNK), 1)).astype(F32)
    invs = [eye - a for a in mats]
    powers = [_split(a) for a in mats]
    for step in range(5):
        squares = [_dot3(p, p) for p in powers]
        powers = [_split(s) for s in squares]
        invs = [inv + _dot3(_split(inv), p) for inv, p in zip(invs, powers)]
    return invs


def _chunk_terms(q, k, v, beta, gc):
    causal, strict = _tri(True), _tri(True, strict=True)
    e = jnp.exp(gc)
    g_last = jnp.broadcast_to(gc[CHUNK - 1:CHUNK, :], (CHUNK, CHUNK))
    f = jnp.exp(g_last - gc)
    e_last = jnp.exp(g_last)
    decay = jnp.where(causal, jnp.exp(jnp.where(causal, gc - gc.T, 0.0)), 0.0)
    kb = k * beta
    a_mat = jnp.where(strict, _nt(kb.astype(BF16), k.astype(BF16)) * decay, 0.0)
    qk = jnp.where(causal, _nt(q.astype(BF16), k.astype(BF16)) * decay, 0.0)
    return e, f, e_last, decay, kb, a_mat, qk


GROUP = 8
UNROLL = 8


def _chunk_rows(ci):
    return pl.ds(pl.multiple_of(ci * CHUNK, CHUNK), CHUNK)


def _pair_specs(n_planes):
    return pl.BlockSpec((n_planes, GROUP * CHUNK, LANES), lambda p, g: (0, g, p))


def _delta_chunk_fwd(xs):
    seq = xs.shape[1]
    rows_per_step = GROUP * CHUNK

    def body(x_ref, inv_ref, qk_ref, u_ref, w_ref):
        for hh in range(2):
            lanes = slice(hh * HEAD_DIM, (hh + 1) * HEAD_DIM)
            rows = [slice(step * CHUNK, (step + 1) * CHUNK) for step in range(GROUP)]
            xh = [[x_ref[j, r, lanes] for j in range(5)] for r in rows]
            terms = [_chunk_terms(*x) for x in xh]
            invs = _unit_lower_inverses([t[5] for t in terms])
            for r, x, t, inv in zip(rows, xh, terms, invs):
                e, kb, qk = t[0], t[4], t[6]
                inv_parts = _split(inv)
                inv_ref[hh, r, :] = inv
                qk_ref[hh, r, :] = qk
                u_ref[hh, r, :] = _dot3(inv_parts, _split(x[2] * x[3]))
                w_ref[hh, r, :] = _dot3(inv_parts, _split(kb * e))

    out = pl.BlockSpec((2, rows_per_step, HEAD_DIM), lambda p, g: (p, g, 0))
    return pl.pallas_call(
        body,
        grid=(N_HEADS // 2, seq // rows_per_step),
        in_specs=[_pair_specs(5)],
        out_specs=[out] * 4,
        out_shape=[jax.ShapeDtypeStruct((N_HEADS, seq, HEAD_DIM), F32)] * 4,
        compiler_params=_params(("parallel", "parallel")),
        name="delta_chunk_fwd",
    )(xs)


def _decays(gc):
    g_last = jnp.broadcast_to(gc[CHUNK - 1:CHUNK, :], (CHUNK, CHUNK))
    return jnp.exp(gc), jnp.exp(g_last - gc), jnp.exp(g_last)


def _token_blocks(index, n_steps=None):
    rows_per_step = GROUP * CHUNK
    if n_steps is None:
        return pl.BlockSpec((1, rows_per_step, HEAD_W), lambda g: (index, g, 0))
    return pl.BlockSpec((1, rows_per_step, HEAD_W), lambda g: (index, n_steps - 1 - g, 0))


def _head_lanes(h):
    return pl.ds(h * HEAD_DIM, HEAD_DIM)


def _delta_scan_fwd(xs, qk_h, u_h, w_h):
    seq = xs.shape[1]
    rows_per_step = GROUP * CHUNK

    def body(q_ref, k_ref, gc_ref, qk_ref, u_ref, w_ref, o_ref, st_ref, state):
        @pl.when(pl.program_id(0) == 0)
        def _():
            state[...] = jnp.zeros_like(state)

        def chunk(ci, carry):
            rows = _chunk_rows(ci)
            heads = range(N_HEADS)
            dec = [_decays(gc_ref[0, rows, _head_lanes(h)]) for h in heads]
            s = [state[h] for h in heads]
            sb = [s[h].astype(BF16) for h in heads]
            vnb = [(u_ref[h, rows, :] - _nn(w_ref[h, rows, :].astype(BF16), sb[h])).astype(BF16) for h in heads]
            for h in heads:
                o_ref[rows, _head_lanes(h)] = (_nn((q_ref[0, rows, _head_lanes(h)] * dec[h][0]).astype(BF16), sb[h])
                                               + _nn(qk_ref[h, rows, :].astype(BF16), vnb[h]))
                st_ref[h, rows, :] = s[h]
            for h in heads:
                state[h] = s[h] * dec[h][2] + _tn((k_ref[0, rows, _head_lanes(h)] * dec[h][1]).astype(BF16), vnb[h])
            return carry

        lax.fori_loop(0, GROUP, chunk, 0)

    blk = pl.BlockSpec((N_HEADS, rows_per_step, HEAD_DIM), lambda g: (0, g, 0))
    return pl.pallas_call(
        body,
        grid=(seq // rows_per_step,),
        in_specs=[_token_blocks(0), _token_blocks(1), _token_blocks(4), blk, blk, blk],
        out_specs=[pl.BlockSpec((rows_per_step, HEAD_W), lambda g: (g, 0)), blk],
        out_shape=[jax.ShapeDtypeStruct((seq, HEAD_W), F32), jax.ShapeDtypeStruct((N_HEADS, seq, HEAD_DIM), F32)],
        scratch_shapes=[pltpu.VMEM((N_HEADS, CHUNK, CHUNK), F32)],
        compiler_params=_params(("arbitrary",)),
        name="delta_scan_fwd",
    )(xs, xs, xs, qk_h, u_h, w_h)


def _delta_scan_bwd(xs, qk_h, w_h, do):
    seq = xs.shape[1]
    rows_per_step = GROUP * CHUNK
    n_steps = seq // rows_per_step

    def body(q_ref, k_ref, gc_ref, qk_ref, w_ref, do_ref, dsn_ref, dvn_ref, dstate):
        @pl.when(pl.program_id(0) == 0)
        def _():
            dstate[...] = jnp.zeros_like(dstate)

        def chunk(step, carry):
            rows = _chunk_rows(GROUP - 1 - step)
            heads = range(N_HEADS)
            dec = [_decays(gc_ref[0, rows, _head_lanes(h)]) for h in heads]
            ds_next = [dstate[h] for h in heads]
            dob = [do_ref[rows, _head_lanes(h)].astype(BF16) for h in heads]
            dv_new = [_tn(qk_ref[h, rows, :].astype(BF16), dob[h])
                      + _nn((k_ref[0, rows, _head_lanes(h)] * dec[h][1]).astype(BF16), ds_next[h].astype(BF16))
                      for h in heads]
            for h in heads:
                dsn_ref[h, rows, :] = ds_next[h]
                dvn_ref[h, rows, :] = dv_new[h]
            for h in heads:
                dstate[h] = (_tn((q_ref[0, rows, _head_lanes(h)] * dec[h][0]).astype(BF16), dob[h])
                             + dec[h][2] * ds_next[h] - _tn(w_ref[h, rows, :].astype(BF16), dv_new[h].astype(BF16)))
            return carry

        lax.fori_loop(0, GROUP, chunk, 0)

    blk = pl.BlockSpec((N_HEADS, rows_per_step, HEAD_DIM), lambda g: (0, n_steps - 1 - g, 0))
    return pl.pallas_call(
        body,
        grid=(n_steps,),
        in_specs=[_token_blocks(0, n_steps), _token_blocks(1, n_steps), _token_blocks(4, n_steps), blk, blk,
                  pl.BlockSpec((rows_per_step, HEAD_W), lambda g: (n_steps - 1 - g, 0))],
        out_specs=[blk, blk],
        out_shape=[jax.ShapeDtypeStruct((N_HEADS, seq, HEAD_DIM), F32)] * 2,
        scratch_shapes=[pltpu.VMEM((N_HEADS, CHUNK, CHUNK), F32)],
        compiler_params=_params(("arbitrary",)),
        name="delta_scan_bwd",
    )(xs, xs, xs, qk_h, w_h, do)


def _delta_chunk_bwd(xs, inv_h, u_h, w_h, st_h, dsn_h, dvn_h, do):
    seq = xs.shape[1]
    rows_per_step = GROUP * CHUNK

    def body(x_ref, inv_ref, u_ref, w_ref, st_ref, dsn_ref, dvn_ref, do_ref, dx_ref):
        causal, strict = _tri(True), _tri(True, strict=True)
        last_row = lax.broadcasted_iota(jnp.int32, (CHUNK, CHUNK), 0) == CHUNK - 1

        def bf(vals):
            return [val.astype(BF16) for val in vals]

        def group(hh, first):
            lanes = slice(hh * HEAD_DIM, (hh + 1) * HEAD_DIM)
            rows = [slice(step * CHUNK, (step + 1) * CHUNK) for step in range(first, first + UNROLL)]
            n = range(UNROLL)
            q, k, v, beta, gc = [[x_ref[j, r, lanes] for r in rows] for j in range(5)]
            terms = [_chunk_terms(q[i], k[i], v[i], beta[i], gc[i]) for i in n]
            e, f, e_last, decay, kb, a_mat, qk = [[t[j] for t in terms] for j in range(7)]
            inv = [_split(inv_ref[hh, r, :]) for r in rows]
            u = [u_ref[hh, r, :] for r in rows]
            w = [w_ref[hh, r, :] for r in rows]
            s = [st_ref[hh, r, :] for r in rows]
            ds_next = [dsn_ref[hh, r, :] for r in rows]
            dv_new = [dvn_ref[hh, r, :] for r in rows]
            sb, dsb, dvb, wb = bf(s), bf(ds_next), bf(dv_new), bf(w)
            dob = bf([do_ref[r, lanes] for r in rows])
            qbf, kbf, kbb = bf(q), bf(k), bf(kb)
            vnb = bf([u[i] - _nn(wb[i], sb[i]) for i in n])
            dqe = [_nt(dob[i], sb[i]) for i in n]
            dw = [-_nt(dvb[i], sb[i]) for i in n]
            dkf = [_nt(vnb[i], dsb[i]) for i in n]
            dqk = [jnp.where(causal, _nt(dob[i], vnb[i]), 0.0) for i in n]
            drhs_u = [_dot3(inv[i], _split(dv_new[i]), _tn) for i in n]
            drhs_w = [_dot3(inv[i], _split(dw[i]), _tn) for i in n]
            da = [-jnp.where(strict, _nt(drhs_u[i].astype(BF16), u[i].astype(BF16))
                             + _nt(drhs_w[i].astype(BF16), wb[i]), 0.0) for i in n]
            dad = bf([da[i] * decay[i] for i in n])
            dqd = bf([dqk[i] * decay[i] for i in n])
            dkb = [e[i] * drhs_w[i] + _nn(dad[i], kbf[i]) for i in n]
            dk = [_tn(dad[i], kbb[i]) + _tn(dqd[i], qbf[i]) + f[i] * dkf[i] + beta[i] * dkb[i] for i in n]
            dq = [_nn(dqd[i], kbf[i]) + e[i] * dqe[i] for i in n]
            for i in n:
                de_full = kb[i] * drhs_w[i] + q[i] * dqe[i]
                df_full = k[i] * dkf[i]
                m = da[i] * a_mat[i] + dqk[i] * qk[i]
                dgc = de_full * e[i] - df_full * f[i] + m - m.T
                tail = jnp.sum(df_full * f[i] + s[i] * ds_next[i] * e_last[i], axis=0, keepdims=True)
                dgc = dgc + jnp.where(last_row, jnp.broadcast_to(tail, (CHUNK, CHUNK)), 0.0)
                dx_ref[0, rows[i], lanes] = dq[i]
                dx_ref[1, rows[i], lanes] = dk[i]
                dx_ref[2, rows[i], lanes] = beta[i] * drhs_u[i]
                dx_ref[3, rows[i], lanes] = v[i] * drhs_u[i] + k[i] * dkb[i]
                dx_ref[4, rows[i], lanes] = dgc

        for hh in range(2):
            for first in range(0, GROUP, UNROLL):
                group(hh, first)

    blk = pl.BlockSpec((2, rows_per_step, HEAD_DIM), lambda p, g: (p, g, 0))
    return pl.pallas_call(
        body,
        grid=(N_HEADS // 2, seq // rows_per_step),
        in_specs=[_pair_specs(5)] + [blk] * 6 + [pl.BlockSpec((rows_per_step, LANES), lambda p, g: (g, p))],
        out_specs=_pair_specs(5),
        out_shape=jax.ShapeDtypeStruct((5, seq, HEAD_W), F32),
        compiler_params=_params(("parallel", "parallel")),
        name="delta_chunk_bwd",
    )(xs, inv_h, u_h, w_h, st_h, dsn_h, dvn_h, do)


def _delta_post_fwd(o, qkvz, gain_row):
    seq = o.shape[0]

    def body(o_ref, z_ref, g_ref, y_ref):
        ov = o_ref[...]
        ms = _pick(ov * ov, _head_sum_matrix()) * (1.0 / HEAD_DIM)
        rb = _pick(lax.rsqrt(ms + EPS), _head_spread_matrix())
        y_ref[...] = (ov * rb * g_ref[...] * _silu(z_ref[...])).astype(y_ref.dtype)

    tile = pl.BlockSpec((ROW_TILE, HEAD_W), lambda i: (i, 0))
    return pl.pallas_call(
        body,
        grid=(seq // ROW_TILE,),
        in_specs=[tile, pl.BlockSpec((ROW_TILE, HEAD_W), lambda i: (i, 3)), pl.BlockSpec((1, HEAD_W), lambda i: (0, 0))],
        out_specs=tile,
        out_shape=jax.ShapeDtypeStruct((seq, HEAD_W), BF16),
        compiler_params=_params(("arbitrary",)),
        name="delta_post_fwd",
    )(o, qkvz, gain_row)


def _delta_post_bwd(dy, o, qkvz, gain_row):
    seq = o.shape[0]

    def body(dy_ref, o_ref, z_ref, g_ref, do_ref, dz_ref, dg_ref):
        @pl.when(pl.program_id(0) == 0)
        def _():
            dg_ref[...] = jnp.zeros_like(dg_ref)

        ov, zv, dyv, gain = o_ref[...], z_ref[...], dy_ref[...], g_ref[...]
        hsum, hspread = _head_sum_matrix(), _head_spread_matrix()
        ms = _pick(ov * ov, hsum) * (1.0 / HEAD_DIM)
        rb = _pick(lax.rsqrt(ms + EPS), hspread)
        ohat = ov * rb
        silu_z, slope_z = _silu_and_slope(zv)
        dz_ref[...] = dyv * ohat * gain * slope_z
        dn = dyv * silu_z
        dg_ref[0:1, :] += jnp.sum(dn * ohat, axis=0, keepdims=True)
        dohat = dn * gain

        @pl.when(pl.program_id(0) == pl.num_programs(0) - 1)
        def _():
            fold = (lax.broadcasted_iota(jnp.int32, (HEAD_W, HEAD_W), 0) % HEAD_DIM
                    == lax.broadcasted_iota(jnp.int32, (HEAD_W, HEAD_W), 1)).astype(F32)
            dg_ref[1:2, :] = _pick(dg_ref[0:1, :], fold)

        proj = _pick(_pick(dohat * ohat, hsum) * (1.0 / HEAD_DIM), hspread)
        do_ref[...] = rb * (dohat - ohat * proj)

    tile = pl.BlockSpec((ROW_TILE, HEAD_W), lambda i: (i, 0))
    return pl.pallas_call(
        body,
        grid=(seq // ROW_TILE,),
        in_specs=[pl.BlockSpec((ROW_TILE, HEAD_W), lambda i: (i, 1)), tile,
                  pl.BlockSpec((ROW_TILE, HEAD_W), lambda i: (i, 3)), pl.BlockSpec((1, HEAD_W), lambda i: (0, 0))],
        out_specs=[tile, tile, pl.BlockSpec((2, HEAD_W), lambda i: (0, 0))],
        out_shape=[jax.ShapeDtypeStruct((seq, HEAD_W), F32), jax.ShapeDtypeStruct((seq, HEAD_W), F32),
                   jax.ShapeDtypeStruct((2, HEAD_W), F32)],
        compiler_params=_params(("arbitrary",)),
        name="delta_post_bwd",
    )(dy, o, qkvz, gain_row)


def _delta_prep_bwd(qkvz, ba, conv_w, alog_row, dt_row, dxs):
    seq = qkvz.shape[0]
    qkv_w = 3 * HEAD_W

    def body(x_ref, xp_ref, ba_ref, w_ref, al_ref, dt_ref, dx_ref, dconv_ref, dba_ref, dvec_ref, stage):
        i = pl.program_id(0)

        @pl.when(i == 0)
        def _():
            dvec_ref[...] = jnp.zeros_like(dvec_ref)

        _stage_rows(stage, x_ref, xp_ref, i)
        pre = _conv_taps(stage, w_ref, ROW_TILE)
        act, slope = _silu_and_slope(pre)
        hsum, hspread = _head_sum_matrix(), _head_spread_matrix()
        for j, scale in ((0, HEAD_DIM ** -0.5), (1, 1.0)):
            cols = slice(j * HEAD_W, (j + 1) * HEAD_W)
            xc = act[:, cols]
            rb = _l2_scale(xc, hsum, hspread)
            xhat = xc * rb
            dhat = dx_ref[j] * scale
            proj = _pick(_pick(dhat * xhat, hsum), hspread)
            dconv_ref[:, cols] = rb * (dhat - xhat * proj) * slope[:, cols]
        dconv_ref[:, 2 * HEAD_W:] = dx_ref[2] * slope[:, 2 * HEAD_W:]

        bav = ba_ref[...]
        beta8 = _sigmoid(bav)
        dbeta8 = _pick(dx_ref[3], _head_gather_matrix())
        dgc8 = _pick(dx_ref[4], _head_gather_matrix(N_HEADS))
        rev = _tri(False).astype(F32)
        z = bav + dt_ref[...]
        ea = jnp.exp(al_ref[...])
        g8 = -ea * _softplus(z)
        sig = _sigmoid(z)
        d_alog = jnp.zeros((1, LANES), F32)
        d_dt = jnp.zeros((1, LANES), F32)
        for ch in range(ROW_TILE // CHUNK):
            rows = slice(ch * CHUNK, (ch + 1) * CHUNK)
            dg8 = _pick_left(rev, dgc8[rows])
            da = -dg8 * ea * sig[rows]
            dba_ref[rows, :] = dbeta8[rows] * beta8[rows] * (1.0 - beta8[rows]) + da
            d_alog = d_alog + jnp.sum(dg8 * g8[rows], axis=0, keepdims=True)
            d_dt = d_dt + jnp.sum(da, axis=0, keepdims=True)
        dvec_ref[0:1, :] += d_alog
        dvec_ref[1:2, :] += d_dt

    return pl.pallas_call(
        body,
        grid=(seq // ROW_TILE,),
        in_specs=[
            pl.BlockSpec((ROW_TILE, qkv_w), lambda i: (i, 0)),
            pl.BlockSpec((8, qkv_w), lambda i: (jnp.maximum(i * (ROW_TILE // 8) - 1, 0), 0)),
            pl.BlockSpec((ROW_TILE, LANES), lambda i: (i, 0)),
            pl.BlockSpec((4, qkv_w), lambda i: (0, 0)),
            pl.BlockSpec((1, LANES), lambda i: (0, 0)),
            pl.BlockSpec((1, LANES), lambda i: (0, 0)),
            pl.BlockSpec((5, ROW_TILE, HEAD_W), lambda i: (0, i, 0)),
        ],
        out_specs=[pl.BlockSpec((ROW_TILE, qkv_w), lambda i: (i, 0)),
                   pl.BlockSpec((ROW_TILE, LANES), lambda i: (i, 0)),
                   pl.BlockSpec((2, LANES), lambda i: (0, 0))],
        out_shape=[jax.ShapeDtypeStruct((seq, qkv_w), F32), jax.ShapeDtypeStruct((seq, LANES), F32),
                   jax.ShapeDtypeStruct((2, LANES), F32)],
        scratch_shapes=[pltpu.VMEM((ROW_TILE + 8, qkv_w), F32)],
        compiler_params=_params(("arbitrary",)),
        name="delta_prep_bwd",
    )(qkvz, qkvz, ba, conv_w, alog_row, dt_row, dxs)


def _conv_bwd(dconv, qkvz, conv_w):
    seq = dconv.shape[0]
    qkv_w = 3 * HEAD_W
    n_tiles = seq // ROW_TILE

    def body(dy_ref, dyn_ref, x_ref, xp_ref, w_ref, dx_ref, dw_ref, stage, dstage):
        i = pl.program_id(0)

        @pl.when(i == 0)
        def _():
            dw_ref[...] = jnp.zeros_like(dw_ref)

        _stage_rows(stage, x_ref, xp_ref, i)
        dstage[0:ROW_TILE, :] = dy_ref[...]
        dstage[ROW_TILE:ROW_TILE + 8, :] = jnp.where(i == n_tiles - 1, 0.0, dyn_ref[...])
        dy = dy_ref[...]
        dx_ref[...] = (w_ref[3:4, :] * dy + w_ref[2:3, :] * dstage[1:1 + ROW_TILE, :]
                       + w_ref[1:2, :] * dstage[2:2 + ROW_TILE, :] + w_ref[0:1, :] * dstage[3:3 + ROW_TILE, :])
        for j in range(4):
            dw_ref[j:j + 1, :] += jnp.sum(dy * stage[5 + j:5 + j + ROW_TILE, :], axis=0, keepdims=True)

    tile = pl.BlockSpec((ROW_TILE, qkv_w), lambda i: (i, 0))
    return pl.pallas_call(
        body,
        grid=(n_tiles,),
        in_specs=[
            tile,
            pl.BlockSpec((8, qkv_w), lambda i: (jnp.minimum((i + 1) * (ROW_TILE // 8), seq // 8 - 1), 0)),
            tile,
            pl.BlockSpec((8, qkv_w), lambda i: (jnp.maximum(i * (ROW_TILE // 8) - 1, 0), 0)),
            pl.BlockSpec((4, qkv_w), lambda i: (0, 0)),
        ],
        out_specs=[tile, pl.BlockSpec((4, qkv_w), lambda i: (0, 0))],
        out_shape=[jax.ShapeDtypeStruct((seq, qkv_w), F32), jax.ShapeDtypeStruct((4, qkv_w), F32)],
        scratch_shapes=[pltpu.VMEM((ROW_TILE + 8, qkv_w), F32), pltpu.VMEM((ROW_TILE + 8, qkv_w), F32)],
        compiler_params=_params(("arbitrary",)),
        name="conv_bwd",
    )(dconv, dconv, qkvz, qkvz, conv_w)


FF_TILE = 1408
WGRAD_ROWS = 1024


def _row(a):
    return pl.BlockSpec((1, a), lambda *_: (0, 0))


def _rms_fwd(xv, gain):
    rstd = lax.rsqrt(jnp.mean(xv * xv, axis=-1, keepdims=True) + EPS)
    xhat = xv * rstd
    return xhat, rstd, xhat * gain


def _rms_bwd(dnorm, xhat, rstd, gain):
    dxhat = dnorm * gain
    dx = rstd * (dxhat - xhat * jnp.mean(dxhat * xhat, axis=-1, keepdims=True))
    return dx, jnp.sum(dnorm * xhat, axis=0, keepdims=True)


def _inproj_fwd(x, gain, scale, shift, w_a, w_d, w_ba):
    seq = x.shape[0]

    def body(x_ref, g_ref, sc_ref, sh_ref, wa_ref, wd_ref, wb_ref, h_ref, a_ref, d_ref, b_ref):
        _, _, norm = _rms_fwd(x_ref[...], g_ref[...])
        h = (norm * (1.0 + sc_ref[...]) + sh_ref[...]).astype(BF16)
        h_ref[...] = h
        a_ref[...] = _nt(h, wa_ref[...])
        d_ref[...] = _nt(h, wd_ref[...])
        b_ref[...] = _nt(h, wb_ref[...])

    def rows(width):
        return pl.BlockSpec((ROW_TILE, width), lambda i: (i, 0))

    def whole(a):
        return pl.BlockSpec(a.shape, lambda i: (0, 0))

    return pl.pallas_call(
        body,
        grid=(seq // ROW_TILE,),
        in_specs=[rows(D_MODEL), _row(D_MODEL), _row(D_MODEL), _row(D_MODEL), whole(w_a), whole(w_d), whole(w_ba)],
        out_specs=[rows(D_MODEL), rows(3 * HEAD_W), rows(4 * HEAD_W), rows(LANES)],
        out_shape=[jax.ShapeDtypeStruct((seq, D_MODEL), BF16), jax.ShapeDtypeStruct((seq, 3 * HEAD_W), F32),
                   jax.ShapeDtypeStruct((seq, 4 * HEAD_W), F32), jax.ShapeDtypeStruct((seq, LANES), F32)],
        compiler_params=_params(("arbitrary",)),
        name="inproj_fwd",
    )(x, gain, scale, shift, w_a, w_d, w_ba)


def _outproj_fwd(y_attn, y_delta, w_out, x, gate1, gain, scale, shift):
    seq = x.shape[0]

    def body(ya_ref, yd_ref, wa_ref, wd_ref, x_ref, g1_ref, g_ref, sc_ref, sh_ref, x1_ref, h_ref, y_ref):
        y = _nn(ya_ref[...].astype(BF16), wa_ref[...]) + _nn(yd_ref[...], wd_ref[...])
        x1 = x_ref[...] + g1_ref[...] * y
        _, _, norm = _rms_fwd(x1, g_ref[...])
        x1_ref[...] = x1
        h_ref[...] = (norm * (1.0 + sc_ref[...]) + sh_ref[...]).astype(BF16)
        y_ref[...] = y.astype(BF16)

    def rows(width):
        return pl.BlockSpec((ROW_TILE, width), lambda i: (i, 0))

    return pl.pallas_call(
        body,
        grid=(seq // ROW_TILE,),
        in_specs=[rows(HEAD_W), rows(HEAD_W),
                  pl.BlockSpec((HEAD_W, D_MODEL), lambda i: (0, 0)), pl.BlockSpec((HEAD_W, D_MODEL), lambda i: (1, 0)),
                  rows(D_MODEL), _row(D_MODEL), _row(D_MODEL), _row(D_MODEL), _row(D_MODEL)],
        out_specs=[rows(D_MODEL), rows(D_MODEL), rows(D_MODEL)],
        out_shape=[jax.ShapeDtypeStruct((seq, D_MODEL), F32), jax.ShapeDtypeStruct((seq, D_MODEL), BF16),
                   jax.ShapeDtypeStruct((seq, D_MODEL), BF16)],
        compiler_params=_params(("arbitrary",)),
        name="outproj_fwd",
    )(y_attn, y_delta, w_out, w_out, x, gate1, gain, scale, shift)


def _ffn_fwd(h2, w_gate, w_up, w_down, x1, gate2, final_gain, target):
    seq = h2.shape[0]
    n_rows, n_ff = seq // ROW_TILE, D_FF // FF_TILE

    def body(h_ref, wg_ref, wu_ref, wd_ref, x1_ref, g2_ref, gf_ref, t_ref, gate_ref, up_ref, dx2_ref, st_ref, acc):
        i, j = pl.program_id(0), pl.program_id(1)

        @pl.when((i == 0) & (j == 0))
        def _():
            st_ref[...] = jnp.zeros_like(st_ref)

        h = h_ref[...]
        gate = _nt(h, wg_ref[...])
        up = _nt(h, wu_ref[...])
        gate_ref[...] = gate.astype(BF16)
        up_ref[...] = up.astype(BF16)
        part = _nn((_silu(gate) * up).astype(BF16), wd_ref[...])

        @pl.when(j == 0)
        def _():
            acc[...] = part

        @pl.when(j > 0)
        def _():
            acc[...] += part

        @pl.when(j == n_ff - 1)
        def _():
            y2 = acc[...]
            x2 = x1_ref[...] + g2_ref[...] * y2
            xhat, rstd, out = _rms_fwd(x2, gf_ref[...])
            diff = out - t_ref[...]
            dx2, dgain = _rms_bwd(diff * (1.0 / D_MODEL), xhat, rstd, gf_ref[...])
            dx2_ref[...] = dx2
            st_ref[0:1, :] += dgain
            st_ref[1:2, :] += jnp.sum(dx2 * y2, axis=0, keepdims=True)
            st_ref[2:3, :] += jnp.sum(diff * diff, axis=0, keepdims=True) * (0.5 / D_MODEL)

        @pl.when((i == n_rows - 1) & (j == n_ff - 1))
        def _():
            st_ref[3:4, :] = jnp.broadcast_to(jnp.sum(st_ref[2:3, :], keepdims=True), (1, D_MODEL))

    def rows(width):
        return pl.BlockSpec((ROW_TILE, width), lambda i, j: (i, 0))

    ff = pl.BlockSpec((ROW_TILE, FF_TILE), lambda i, j: (i, j))
    return pl.pallas_call(
        body,
        grid=(n_rows, n_ff),
        in_specs=[rows(D_MODEL),
                  pl.BlockSpec((FF_TILE, D_MODEL), lambda i, j: (j, 0)), pl.BlockSpec((FF_TILE, D_MODEL), lambda i, j: (j, 0)),
                  pl.BlockSpec((FF_TILE, D_MODEL), lambda i, j: (j, 0)),
                  rows(D_MODEL), _row(D_MODEL), _row(D_MODEL), rows(D_MODEL)],
        out_specs=[ff, ff, rows(D_MODEL), pl.BlockSpec((8, D_MODEL), lambda i, j: (0, 0))],
        out_shape=[jax.ShapeDtypeStruct((seq, D_FF), BF16), jax.ShapeDtypeStruct((seq, D_FF), BF16),
                   jax.ShapeDtypeStruct((seq, D_MODEL), F32), jax.ShapeDtypeStruct((8, D_MODEL), F32)],
        scratch_shapes=[pltpu.VMEM((ROW_TILE, D_MODEL), F32)],
        compiler_params=_params(("arbitrary", "arbitrary")),
        name="ffn_fwd",
    )(h2, w_gate, w_up, w_down, x1, gate2, final_gain, target)


def _ffn_bwd(dx2, gate, up, w_gate, w_up, w_down, x1, y, gate2, gate1, gain, scale):
    seq = dx2.shape[0]

    def act_body(dx2_ref, g2_ref, gate_ref, up_ref, wd_ref, dgate_ref, dup_ref, act_ref, dy2_ref):
        dy2 = (g2_ref[...] * dx2_ref[...]).astype(BF16)
        dy2_ref[...] = dy2
        gate = gate_ref[...].astype(F32)
        up = up_ref[...].astype(F32)
        dact = _nt(dy2, wd_ref[...])
        silu, slope = _silu_and_slope(gate)
        act_ref[...] = (silu * up).astype(BF16)
        dgate_ref[...] = (dact * up * slope).astype(BF16)
        dup_ref[...] = (dact * silu).astype(BF16)

    def rows2(width):
        return pl.BlockSpec((ROW_TILE, width), lambda i, j: (i, 0))

    ff = pl.BlockSpec((ROW_TILE, FF_TILE), lambda i, j: (i, j))
    dgate, dup, act, dy2 = pl.pallas_call(
        act_body,
        grid=(seq // ROW_TILE, D_FF // FF_TILE),
        in_specs=[rows2(D_MODEL), _row(D_MODEL), ff, ff, pl.BlockSpec((FF_TILE, D_MODEL), lambda i, j: (j, 0))],
        out_specs=[ff, ff, ff, rows2(D_MODEL)],
        out_shape=[jax.ShapeDtypeStruct((seq, D_FF), BF16)] * 3 + [jax.ShapeDtypeStruct((seq, D_MODEL), BF16)],
        compiler_params=_params(("arbitrary", "arbitrary")),
        name="ffn_bwd_act",
    )(dx2, gate2, gate, up, w_down)

    def in_body(dgate_ref, dup_ref, wg_ref, wu_ref, dx2_ref, x1_ref, y_ref, g1_ref, g_ref, sc_ref,
                dx1_ref, dy_ref, st_ref):
        @pl.when(pl.program_id(0) == 0)
        def _():
            st_ref[...] = jnp.zeros_like(st_ref)

        dh = _nn(dgate_ref[...], wg_ref[...]) + _nn(dup_ref[...], wu_ref[...])
        xhat, rstd, norm = _rms_fwd(x1_ref[...], g_ref[...])
        dxn, dgain = _rms_bwd(dh * (1.0 + sc_ref[...]), xhat, rstd, g_ref[...])
        dx1 = dx2_ref[...] + dxn
        dx1_ref[...] = dx1
        dy_ref[...] = (g1_ref[...] * dx1).astype(BF16)
        st_ref[0:1, :] += jnp.sum(dh, axis=0, keepdims=True)
        st_ref[1:2, :] += jnp.sum(dh * norm, axis=0, keepdims=True)
        st_ref[2:3, :] += dgain
        st_ref[3:4, :] += jnp.sum(dx1 * y_ref[...].astype(F32), axis=0, keepdims=True)

    half_tile = ROW_TILE // 2

    def rows(width):
        return pl.BlockSpec((half_tile, width), lambda i: (i, 0))

    whole = pl.BlockSpec((D_FF, D_MODEL), lambda i: (0, 0))
    dx1, dy, stats = pl.pallas_call(
        in_body,
        grid=(seq // half_tile,),
        in_specs=[rows(D_FF), rows(D_FF), whole, whole, rows(D_MODEL), rows(D_MODEL), rows(D_MODEL),
                  _row(D_MODEL), _row(D_MODEL), _row(D_MODEL)],
        out_specs=[rows(D_MODEL), rows(D_MODEL), pl.BlockSpec((8, D_MODEL), lambda i: (0, 0))],
        out_shape=[jax.ShapeDtypeStruct((seq, D_MODEL), F32), jax.ShapeDtypeStruct((seq, D_MODEL), BF16),
                   jax.ShapeDtypeStruct((8, D_MODEL), F32)],
        compiler_params=_params(("arbitrary",)),
        name="ffn_bwd_in",
    )(dgate, dup, w_gate, w_up, dx2, x1, y, gate1, gain, scale)
    return dgate, dup, act, dy2, dx1, dy, stats


def _outproj_bwd(dy, w_out):
    seq = dy.shape[0]

    def body(dy_ref, w_ref, out_ref):
        out_ref[...] = _nt(dy_ref[...], w_ref[...])

    rows = pl.BlockSpec((ROW_TILE, D_MODEL), lambda i: (i, 0))
    return pl.pallas_call(
        body,
        grid=(seq // ROW_TILE,),
        in_specs=[rows, pl.BlockSpec((D_MODEL, D_MODEL), lambda i: (0, 0))],
        out_specs=rows,
        out_shape=jax.ShapeDtypeStruct((seq, D_MODEL), F32),
        compiler_params=_params(("arbitrary",)),
        name="outproj_bwd",
    )(dy, w_out)


def _inproj_bwd(dq, dk, dv, dxd, dz, dba, w_a, w_d, w_ba, x, dx1, gain, scale, partials):
    seq = x.shape[0]
    n = len(partials)
    n_steps = seq // ROW_TILE

    def body(*refs):
        (dq_ref, dk_ref, dv_ref, dxd_ref, dz_ref, dba_ref, wa_ref, wd_ref, wb_ref, x_ref, dx1_ref, g_ref,
         sc_ref) = refs[:13]
        gx_ref, st_ref = refs[13 + n:15 + n]
        riding = (refs[13:13 + n], refs[15 + n:15 + 2 * n], *refs[15 + 2 * n:])

        @pl.when(pl.program_id(0) == 0)
        def _():
            st_ref[...] = jnp.zeros_like(st_ref)
            for cp in (_scatter_copies(*riding) if n else []):
                cp.start()

        dh = (_nn(dq_ref[...].astype(BF16), wa_ref[0:HEAD_W, :])
              + _nn(dk_ref[...].astype(BF16), wa_ref[HEAD_W:2 * HEAD_W, :])
              + _nn(dv_ref[...].astype(BF16), wa_ref[2 * HEAD_W:, :])
              + _nn(dxd_ref[...].astype(BF16), wd_ref[0:3 * HEAD_W, :])
              + _nn(dz_ref[...].astype(BF16), wd_ref[3 * HEAD_W:, :])
              + _nn(dba_ref[...].astype(BF16), wb_ref[...]))
        xhat, rstd, norm = _rms_fwd(x_ref[...], g_ref[...])
        dxn, dgain = _rms_bwd(dh * (1.0 + sc_ref[...]), xhat, rstd, g_ref[...])
        gx_ref[...] = dx1_ref[...] + dxn
        st_ref[0:1, :] += jnp.sum(dh, axis=0, keepdims=True)
        st_ref[1:2, :] += jnp.sum(dh * norm, axis=0, keepdims=True)
        st_ref[2:3, :] += dgain

        if n:
            @pl.when(pl.program_id(0) == n_steps - 1)
            def _():
                for cp in _scatter_copies(*riding):
                    cp.wait()

    def rows(width):
        return pl.BlockSpec((ROW_TILE, width), lambda i: (i, 0))

    def whole(a):
        return pl.BlockSpec(a.shape, lambda i: (0, 0))

    sems = [pltpu.SemaphoreType.DMA((3 * n,)), pltpu.SemaphoreType.DMA((3 * n,))] if n else []
    return pl.pallas_call(
        body,
        grid=(n_steps,),
        in_specs=[rows(HEAD_W), rows(HEAD_W), rows(HEAD_W), rows(3 * HEAD_W), rows(HEAD_W), rows(LANES),
                  whole(w_a), whole(w_d), whole(w_ba), rows(D_MODEL), rows(D_MODEL), _row(D_MODEL), _row(D_MODEL)]
        + [ANY] * n,
        out_specs=[rows(D_MODEL), pl.BlockSpec((8, D_MODEL), lambda i: (0, 0))] + [ANY] * n,
        out_shape=[jax.ShapeDtypeStruct((seq, D_MODEL), F32), jax.ShapeDtypeStruct((8, D_MODEL), F32)]
        + [jax.ShapeDtypeStruct(p.shape, p.dtype) for p in partials],
        scratch_shapes=sems,
        compiler_params=_params(("arbitrary",)),
        name="inproj_bwd",
    )(dq, dk, dv, dxd, dz, dba, w_a, w_d, w_ba, x, dx1, gain, scale, *partials)


def _weight_grad(a, b, name):
    seq, m = a.shape
    n = b.shape[1]
    tm = m if m <= 1536 else m // 2
    tn = n if n <= 1536 else n // 2
    rows = 2 * WGRAD_ROWS
    n_k = seq // rows

    def body(a_ref, b_ref, out_ref):
        part = _tn(a_ref[...].astype(BF16), b_ref[...].astype(BF16))

        @pl.when(pl.program_id(2) == 0)
        def _():
            out_ref[...] = part

        @pl.when(pl.program_id(2) > 0)
        def _():
            out_ref[...] += part

    return pl.pallas_call(
        body,
        grid=(m // tm, n // tn, n_k),
        in_specs=[pl.BlockSpec((rows, tm), lambda i, j, k: (k, i)),
                  pl.BlockSpec((rows, tn), lambda i, j, k: (k, j))],
        out_specs=pl.BlockSpec((tm, tn), lambda i, j, k: (i, j)),
        out_shape=jax.ShapeDtypeStruct((m, n), F32),
        compiler_params=_params(("arbitrary", "arbitrary", "arbitrary")),
        name=name,
    )(a, b)


def _weight_grad_stack(pieces, b, name):
    seq, n = b.shape
    widths = [a.shape[1] for a in pieces]
    starts = [sum(widths[:i]) for i in range(len(pieces))]

    def body(*refs):
        a_refs, b_ref, out_ref = refs[:len(pieces)], refs[len(pieces)], refs[len(pieces) + 1]

        @pl.when(pl.program_id(0) == 0)
        def _():
            out_ref[...] = jnp.zeros_like(out_ref)

        bb = b_ref[...].astype(BF16)
        for a_ref, start, width in zip(a_refs, starts, widths):
            out_ref[start:start + width, :] += _tn(a_ref[...].astype(BF16), bb)

    def rows(width):
        return pl.BlockSpec((WGRAD_ROWS, width), lambda k: (k, 0))

    return pl.pallas_call(
        body,
        grid=(seq // WGRAD_ROWS,),
        in_specs=[rows(w) for w in widths] + [rows(n)],
        out_specs=pl.BlockSpec((sum(widths), n), lambda k: (0, 0)),
        out_shape=jax.ShapeDtypeStruct((sum(widths), n), F32),
        compiler_params=_params(("arbitrary",)),
        name=name,
    )(*pieces, b)


def _adamw(w, g, m, v, name):
    n_rows, n_cols = w.shape
    if n_rows % 256 == 0:
        block, grid, index = (256, n_cols), (n_rows // 256,), lambda i: (i, 0)
    elif n_cols % 256 == 0:
        block, grid, index = (n_rows, 256), (n_cols // 256,), lambda i: (0, i)
    else:
        block, grid, index = (n_rows, n_cols), (1,), lambda i: (0, 0)

    def body(w_ref, g_ref, m_ref, v_ref, d_ref, nm_ref, nv_ref):
        gv = g_ref[...]
        nm = ADAM_B1 * m_ref[...] + (1.0 - ADAM_B1) * gv
        nv = ADAM_B2 * v_ref[...] + (1.0 - ADAM_B2) * (gv * gv)
        m_hat = nm / (1.0 - ADAM_B1 ** ADAM_STEP)
        v_hat = nv / (1.0 - ADAM_B2 ** ADAM_STEP)
        d_ref[...] = -ADAM_LR * (m_hat / (jnp.sqrt(v_hat) + ADAM_EPS) + ADAM_WD * w_ref[...])
        nm_ref[...] = nm
        nv_ref[...] = nv

    blk = pl.BlockSpec(block, index)
    shape = jax.ShapeDtypeStruct((n_rows, n_cols), F32)
    return pl.pallas_call(
        body,
        grid=grid,
        in_specs=[blk] * 4,
        out_specs=[blk] * 3,
        out_shape=[shape] * 3,
        compiler_params=_params(("arbitrary",)),
        name=name,
    )(w, g, m, v)


IN_WIDTH = 3600
BA_COL = 7 * HEAD_W


def _local_step(x, target, mod, norm_attn_g, w_in, rel_bias, conv_w, a_log, dt_bias, delta_norm_g,
                norm_ffn_g, final_norm_g, shards, assemble, reduce_pairs):
    sh1, sc1, g1, sh2, sc2, g2 = [mod[:, i * D_MODEL:(i + 1) * D_MODEL] for i in range(6)]
    w_a = w_in[:3 * HEAD_W]
    w_d = w_in[3 * HEAD_W:BA_COL]
    w_ba = jnp.pad(w_in[BA_COL:], ((0, LANES - 2 * N_HEADS), (0, 0)))
    tables = jnp.asarray(_attn_tables())
    alog_row = jnp.pad(a_log, ((0, 0), (N_HEADS, LANES - 2 * N_HEADS)))
    dt_row = jnp.pad(dt_bias, ((0, 0), (N_HEADS, LANES - 2 * N_HEADS)))
    gain_row = jnp.tile(delta_norm_g, (1, N_HEADS))

    h1, qkv_a, qkvz, ba = _inproj_fwd(x, norm_attn_g, sc1, sh1, w_a, w_d, w_ba)
    bias = _attention_bias(rel_bias, tables)
    y_attn, lse, *gathered = _attention_fwd(qkv_a, bias, shards)
    w_out, w_gate, w_up, w_down = assemble(gathered)
    xs = _delta_prep_fwd(qkvz, ba, conv_w, alog_row, dt_row)
    inv_h, qk_h, u_h, w_h = _delta_chunk_fwd(xs)
    o, st_h = _delta_scan_fwd(xs, qk_h, u_h, w_h)
    y_delta = _delta_post_fwd(o, qkvz, gain_row)
    x1, h2, y = _outproj_fwd(y_attn, y_delta, w_out, x, g1, norm_ffn_g, sc2, sh2)
    gate, up, dx2, st_f = _ffn_fwd(h2, w_gate, w_up, w_down, x1, g2, final_norm_g, target)

    dgate, dup, act, dy2, dx1, dy, st_b = _ffn_bwd(dx2, gate, up, w_gate, w_up, w_down, x1, y, g2, g1, norm_ffn_g, sc2)
    partials = reduce_pairs([_weight_grad_stack([y_attn, y_delta], dy, "wgrad_out"),
                             _weight_grad(dgate, h2, "wgrad_gate"), _weight_grad(dup, h2, "wgrad_up"),
                             _weight_grad(act, dy2, "wgrad_down")], 1, "rest")
    grads = {}
    dycat = _outproj_bwd(dy, w_out)
    do, dz, dgain = _delta_post_bwd(dycat, o, qkvz, gain_row)
    dsn_h, dvn_h = _delta_scan_bwd(xs, qk_h, w_h, do)
    dxs = _delta_chunk_bwd(xs, inv_h, u_h, w_h, st_h, dsn_h, dvn_h, do)
    dconv, dba, dvec = _delta_prep_bwd(qkvz, ba, conv_w, alog_row, dt_row, dxs)
    dxd, grads["conv_w"] = _conv_bwd(dconv, qkvz, conv_w)
    dq, dk, dv, dbias, *scattered = _attention_bwd(qkv_a, dycat, y_attn, lse, bias, partials)
    partials_in = reduce_pairs([jnp.concatenate(
        [_weight_grad_stack([dq, dk, dv], h1, "wgrad_in_attn"),
         _weight_grad_stack([dxd, dz, dba], h1, "wgrad_in_delta")[:IN_WIDTH - 3 * HEAD_W]], axis=0)], 0, "in")
    grad_x, st_i, *scattered_in = _inproj_bwd(dq, dk, dv, dxd, dz, dba, w_a, w_d, w_ba, x, dx1, norm_attn_g, sc1,
                                              partials_in)
    grads["rel_bias"] = _rel_bias_grad(dbias, tables)[:, :N_BUCKETS].T
    grads["a_log"] = dvec[0:1, N_HEADS:2 * N_HEADS]
    grads["dt_bias"] = dvec[1:2, N_HEADS:2 * N_HEADS]
    grads["delta_norm_g"] = dgain[1:2, :HEAD_DIM]
    grads["norm_attn_g"] = st_i[2:3]
    grads["norm_ffn_g"] = st_b[2:3]
    grads["final_norm_g"] = st_f[0:1]
    dmod = jnp.concatenate([st_i[0:1], st_i[1:2], st_b[3:4], st_b[0:1], st_b[1:2], st_f[1:2]], axis=1)
    return st_f[3, 0], grad_x, grads, dmod, (partials_in + partials, scattered_in + scattered)


MESH = pl.DeviceIdType.MESH
OTHER_CHIPS = ((1, 0), (0, 1), (1, 1))
ALL_PEERS = tuple((m >> 2 & 1, m >> 1 & 1, m & 1) for m in range(1, 8))
ANY = pl.BlockSpec(memory_space=pl.ANY)
VMEM_SPEC = pl.BlockSpec(memory_space=pltpu.VMEM)


def _me():
    return lax.axis_index("x"), lax.axis_index("y"), lax.axis_index("c")


def _flip(pos, mask):
    return tuple(1 - p if m else p for p, m in zip(pos, mask))


def _remote(src, dst, send_sems, recv_sems, k, to):
    return pltpu.make_async_remote_copy(src_ref=src, dst_ref=dst, send_sem=send_sems.at[k], recv_sem=recv_sems.at[k],
                                        device_id=to, device_id_type=MESH)


def _ada_exchange(c8, w_ada, b_ada, conv8):
    def body(c_ref, w_ref, b_ref, cv_ref, mod_ref, cact_ref, conv_ref, c_all, part_all, send_sems, recv_sems):
        x, y, c = me = _me()
        dev = 4 * x + 2 * y + c
        chip = 2 * x + y
        c_all[dev] = c_ref[...]
        conv_ref[chip] = cv_ref[...]
        first = [_remote(c_ref, c_all.at[dev], send_sems, recv_sems, k, _flip(me, mask))
                 for k, mask in enumerate(ALL_PEERS)]
        first += [_remote(cv_ref, conv_ref.at[chip], send_sems, recv_sems, 7 + j, _flip(me, (*mask, 0)))
                  for j, mask in enumerate(OTHER_CHIPS)]
        for cp in first:
            cp.start()
        for cp in first:
            cp.wait()
        row = lax.broadcasted_iota(jnp.int32, (8, D_MODEL), 0)
        c_rows = jnp.zeros((8, D_MODEL), F32)
        for d in range(8):
            c_rows = jnp.where(row == d, c_all[d], c_rows)
        c_act = _silu(c_rows)
        cact_ref[...] = c_act
        part_all[chip] = _nn(c_act, w_ref[...], HIGHEST)
        second = [_remote(part_all.at[chip], part_all.at[chip], send_sems, recv_sems, 10 + j, _flip(me, (*mask, 0)))
                  for j, mask in enumerate(OTHER_CHIPS)]
        for cp in second:
            cp.start()
        for cp in second:
            cp.wait()
        cols = w_ref.shape[1]
        for k in range(4):
            mod_ref[:, k * cols:(k + 1) * cols] = part_all[k] + b_ref[:, k * cols:(k + 1) * cols]

    cols = w_ada.shape[1]
    return pl.pallas_call(
        body,
        in_specs=[VMEM_SPEC] * 4,
        out_specs=[VMEM_SPEC] * 3,
        out_shape=[jax.ShapeDtypeStruct((8, 4 * cols), F32), jax.ShapeDtypeStruct((8, D_MODEL), F32),
                   jax.ShapeDtypeStruct((4, 8, conv8.shape[1]), F32)],
        scratch_shapes=[pltpu.VMEM((8, 8, D_MODEL), F32), pltpu.VMEM((4, 8, cols), F32),
                        pltpu.SemaphoreType.DMA((13,)), pltpu.SemaphoreType.DMA((13,))],
        compiler_params=pltpu.CompilerParams(vmem_limit_bytes=VMEM_LIMIT),
        name="ada_exchange",
    )(c8, w_ada, b_ada, conv8)


def _gather_weights(shards):
    n = len(shards)

    def body(*refs):
        first, passed = _gather_copies(refs[:n], refs[n:2 * n], *refs[2 * n:])
        for cp in first:
            cp.start()
        for cp, fwd in zip(first, passed):
            cp.wait_recv()
            fwd.start()
        for cp in first:
            cp.wait_send()
        for fwd in passed:
            fwd.wait()

    return pl.pallas_call(
        body,
        in_specs=[ANY] * n,
        out_specs=[ANY] * n,
        out_shape=_gathered_shapes(shards),
        scratch_shapes=[pltpu.SemaphoreType.DMA((6 * n,)), pltpu.SemaphoreType.DMA((6 * n,))],
        name="gather_weights",
    )(*shards)


def _gathered_shapes(shards):
    return [jax.ShapeDtypeStruct((4, *s.shape), s.dtype) for s in shards]


def _gather_copies(srcs, dsts, send_sems, recv_sems, hand_over=True):
    x, y, c = me = _me()
    chip = 2 * x + y
    sibling = _flip(me, (0, 0, 1))
    first, passed = [], []
    for a, (src, dst) in enumerate(zip(srcs, dsts)):
        for j, mask in enumerate(OTHER_CHIPS):
            to = _flip(me, (*mask, 0))
            first.append(_remote(src.at[c], dst.at[chip, c], send_sems, recv_sems, 6 * a + j, to))
            if hand_over:
                landed = dst.at[2 * to[0] + to[1], c]
                passed.append(_remote(landed, landed, send_sems, recv_sems, 6 * a + 3 + j, sibling))
    return first, passed


def _scatter_copies(srcs, dsts, send_sems, recv_sems):
    x, y, c = me = _me()
    chip = 2 * x + y
    copies = []
    for a, (src, dst) in enumerate(zip(srcs, dsts)):
        for j, mask in enumerate(OTHER_CHIPS):
            to = _flip(me, (*mask, 0))
            copies.append(_remote(src.at[2 * to[0] + to[1]], dst.at[chip], send_sems, recv_sems, 3 * a + j, to))
    return copies


def _start_and_wait(copies):
    for cp in copies:
        cp.start()
    for cp in copies:
        cp.wait()


def _swap_halves(grads):
    n = len(grads)

    def body(*refs):
        srcs, got = refs[:n], refs[n:2 * n]
        send_sems, recv_sems = refs[2 * n:]
        x, y, c = me = _me()
        _start_and_wait([_remote(srcs[a].at[:, 1 - c], got[a], send_sems, recv_sems, a, _flip(me, (0, 0, 1)))
                         for a in range(n)])

    return pl.pallas_call(
        body,
        in_specs=[ANY] * n,
        out_specs=[ANY] * n,
        out_shape=[jax.ShapeDtypeStruct((4, g.shape[2], g.shape[3]), g.dtype) for g in grads],
        scratch_shapes=[pltpu.SemaphoreType.DMA((n,)), pltpu.SemaphoreType.DMA((n,))],
        name=f"swap_halves_{n}",
    )(*grads)


def _join_halves(halves):
    n = len(halves)

    def body(*refs):
        srcs, dsts = refs[:n], refs[n:2 * n]
        send_sems, recv_sems = refs[2 * n:]
        x, y, c = me = _me()
        _start_and_wait([_remote(srcs[a], dsts[a].at[c], send_sems, recv_sems, a, _flip(me, (0, 0, 1)))
                         for a in range(n)])

    return pl.pallas_call(
        body,
        in_specs=[ANY] * n,
        out_specs=[ANY] * n,
        out_shape=[jax.ShapeDtypeStruct((2, *h.shape), h.dtype) for h in halves],
        scratch_shapes=[pltpu.SemaphoreType.DMA((n,)), pltpu.SemaphoreType.DMA((n,))],
        name=f"join_halves_{n}",
    )(*halves)


def _gather_small(packed):
    n_rows = packed.shape[0]

    def body(p_ref, all_ref, sum_ref, send_sems, recv_sems):
        x, y, c = me = _me()
        dev = 4 * x + 2 * y + c
        all_ref[dev] = p_ref[...]
        copies = [_remote(p_ref, all_ref.at[dev], send_sems, recv_sems, k, _flip(me, mask))
                  for k, mask in enumerate(ALL_PEERS)]
        for cp in copies:
            cp.start()
        for cp in copies:
            cp.wait()
        total = all_ref[0]
        for d in range(1, 8):
            total = total + all_ref[d]
        sum_ref[...] = total

    return pl.pallas_call(
        body,
        in_specs=[VMEM_SPEC],
        out_specs=[VMEM_SPEC, VMEM_SPEC],
        out_shape=[jax.ShapeDtypeStruct((8, n_rows, LANES), F32), jax.ShapeDtypeStruct((n_rows, LANES), F32)],
        scratch_shapes=[pltpu.SemaphoreType.DMA((7,)), pltpu.SemaphoreType.DMA((7,))],
        name="gather_small",
    )(packed)


def _add_pair(a, b, out_dtype, name):
    def body(a_ref, b_ref, o_ref):
        o_ref[...] = (a_ref[...] + b_ref[...]).astype(o_ref.dtype)

    blk = pl.BlockSpec((1, *a.shape[1:]), lambda i: (i, 0, 0))
    return pl.pallas_call(
        body, grid=(a.shape[0],), in_specs=[blk, blk], out_specs=blk,
        out_shape=jax.ShapeDtypeStruct(a.shape, out_dtype),
        compiler_params=_params(("arbitrary",)), name=name,
    )(a, b)


def _add_slots(a, name):
    def body(a_ref, o_ref):
        total = a_ref[0].astype(F32)
        for k in range(1, 4):
            total = total + a_ref[k].astype(F32)
        o_ref[...] = total

    return pl.pallas_call(
        body, in_specs=[VMEM_SPEC], out_specs=VMEM_SPEC,
        out_shape=jax.ShapeDtypeStruct(a.shape[1:], F32),
        compiler_params=pltpu.CompilerParams(vmem_limit_bytes=VMEM_LIMIT), name=name,
    )(a)


def _ada_weight_grad(c_act, dmod_cols):
    def body(c_ref, d_ref, o_ref):
        o_ref[...] = _tn(c_ref[...], d_ref[...], HIGHEST)

    return pl.pallas_call(
        body, in_specs=[VMEM_SPEC, VMEM_SPEC], out_specs=VMEM_SPEC,
        out_shape=jax.ShapeDtypeStruct((c_act.shape[1], dmod_cols.shape[1]), F32),
        compiler_params=pltpu.CompilerParams(vmem_limit_bytes=VMEM_LIMIT), name="ada_weight_grad",
    )(c_act, dmod_cols)


def kernel(x, c, w_ada, b_ada, norm_attn_g, w_in, rel_bias, conv_w, a_log, dt_bias, delta_norm_g, w_out, norm_ffn_g, w_gate, w_up, w_down, final_norm_g, loss_target, m_w_ada, m_b_ada, m_norm_attn_g, m_w_in, m_rel_bias, m_conv_w, m_a_log, m_dt_bias, m_delta_norm_g, m_w_out, m_norm_ffn_g, m_w_gate, m_w_up, m_w_down, m_final_norm_g, v_w_ada, v_b_ada, v_norm_attn_g, v_w_in, v_rel_bias, v_conv_w, v_a_log, v_dt_bias, v_delta_norm_g, v_w_out, v_norm_ffn_g, v_w_gate, v_w_up, v_w_down, v_final_norm_g):
    xi, yi, ci = _me()
    dev = 4 * xi + 2 * yi + ci
    chip = 2 * xi + yi

    conv_cols = conv_w.shape[2]
    mod_all, c_act, conv_all = _ada_exchange(jnp.broadcast_to(c, (8, D_MODEL)), w_ada[0], b_ada,
                                             jnp.pad(conv_w[0], ((0, 4), (0, 0))))
    mod = lax.dynamic_slice_in_dim(mod_all, dev, 1, axis=0)
    conv_full = jnp.swapaxes(conv_all[:, :4, :], 0, 1).reshape(4, 4 * conv_cols)

    big_names = ("w_in", "w_out", "w_gate", "w_up", "w_down")
    by_cols = (True, False, True, True, False)

    def rows_form(a, cols):
        return jnp.swapaxes(a[0], 0, 1) if cols else a[0]

    def halves_form(w):
        rows, lanes = w.shape
        if (rows // 2) % 16:
            rows, lanes = w.size // LANES, LANES
        return (2, rows // 2, lanes)

    big = [rows_form(w, cols) for w, cols in zip((w_in, w_out, w_gate, w_up, w_down), by_cols)]
    shards = [w.astype(BF16).reshape(halves_form(w)) for w in big]

    def assemble(gathered, first):
        return [lax.dynamic_update_index_in_dim(g, s, chip, 0).reshape(4 * w.shape[0], w.shape[1])
                for g, s, w in zip(gathered, shards[first:], big[first:])]

    def reduce_pairs(grads, first, tag):
        slots = [g.reshape(4, *halves_form(w)) for g, w in zip(grads, big[first:])]
        return [_add_pair(lax.dynamic_index_in_dim(s, ci, 1, keepdims=False), got, BF16, f"add_pair_{tag}{a}")
                for a, (s, got) in enumerate(zip(slots, _swap_halves(slots)))]

    def finish(partials, scattered, first, tag):
        by_source = [lax.dynamic_update_index_in_dim(b, lax.dynamic_index_in_dim(p, chip, 0, keepdims=False), chip, 0)
                     for b, p in zip(scattered, partials)]
        halves = [_add_slots(p, f"add_slots_{tag}{a}") for a, p in enumerate(by_source)]
        joined = [lax.dynamic_update_index_in_dim(j, h, ci, 0) for j, h in zip(_join_halves(halves), halves)]
        return [j.reshape(w.shape) for j, w in zip(joined, big[first:])]

    whole_in, = assemble(_gather_weights(shards[:1]), 0)
    loss, grad_x, grads, dmod, (partials, scattered) = _local_step(
        x[0], loss_target[0], mod, norm_attn_g, whole_in, rel_bias, conv_full, a_log, dt_bias, delta_norm_g,
        norm_ffn_g, final_norm_g[None], shards[1:], functools.partial(assemble, first=1), reduce_pairs)

    big_grads = finish(partials, scattered, 0, "all")

    pieces = [dmod, grads["conv_w"], grads["norm_attn_g"], grads["norm_ffn_g"], grads["final_norm_g"],
              grads["rel_bias"], grads["a_log"], grads["dt_bias"], grads["delta_norm_g"]]
    flat = [jnp.pad(p.reshape(-1), (0, -p.size % LANES)) for p in pieces]
    n_rows = [f.size // LANES for f in flat]
    packed = jnp.concatenate(flat).reshape(-1, LANES)
    packed = jnp.pad(packed, ((0, -packed.shape[0] % 8), (0, 0)))
    all_small, total = _gather_small(packed)
    sums, start = [], 0
    for p, n in zip(pieces, n_rows):
        sums.append(total[start:start + n].reshape(-1)[:p.size].reshape(p.shape))
        start += n
    g_b_ada, g_conv, g_norm_attn, g_norm_ffn, g_final, g_rel, g_alog, g_dt, g_dnorm = sums
    dmod_all = all_small[:, :n_rows[0], :].reshape(8, -1)
    ada_cols = w_ada.shape[2]
    g_w_ada = _ada_weight_grad(c_act, lax.dynamic_slice_in_dim(dmod_all, chip * ada_cols, ada_cols, axis=1))
    g_conv = lax.dynamic_slice_in_dim(g_conv, chip * conv_cols, conv_cols, axis=1)

    grad = {"w_ada": g_w_ada[None], "b_ada": g_b_ada, "norm_attn_g": g_norm_attn,
            "rel_bias": g_rel, "conv_w": g_conv[None], "a_log": g_alog, "dt_bias": g_dt, "delta_norm_g": g_dnorm,
            "norm_ffn_g": g_norm_ffn, "final_norm_g": g_final.reshape(-1)}
    weight = {"w_ada": w_ada, "b_ada": b_ada, "norm_attn_g": norm_attn_g, "w_in": w_in, "rel_bias": rel_bias,
              "conv_w": conv_w, "a_log": a_log, "dt_bias": dt_bias, "delta_norm_g": delta_norm_g, "w_out": w_out,
              "norm_ffn_g": norm_ffn_g, "w_gate": w_gate, "w_up": w_up, "w_down": w_down, "final_norm_g": final_norm_g}
    first = {"w_ada": m_w_ada, "b_ada": m_b_ada, "norm_attn_g": m_norm_attn_g, "w_in": m_w_in, "rel_bias": m_rel_bias,
             "conv_w": m_conv_w, "a_log": m_a_log, "dt_bias": m_dt_bias, "delta_norm_g": m_delta_norm_g,
             "w_out": m_w_out, "norm_ffn_g": m_norm_ffn_g, "w_gate": m_w_gate, "w_up": m_w_up, "w_down": m_w_down,
             "final_norm_g": m_final_norm_g}
    second = {"w_ada": v_w_ada, "b_ada": v_b_ada, "norm_attn_g": v_norm_attn_g, "w_in": v_w_in, "rel_bias": v_rel_bias,
              "conv_w": v_conv_w, "a_log": v_a_log, "dt_bias": v_dt_bias, "delta_norm_g": v_delta_norm_g,
              "w_out": v_w_out, "norm_ffn_g": v_norm_ffn_g, "w_gate": v_w_gate, "w_up": v_w_up, "w_down": v_w_down,
              "final_norm_g": v_final_norm_g}
    delta, new_m, new_v = {}, {}, {}
    for name, w in weight.items():
        if name in big_names:
            continue
        two_d = (-1, w.shape[-1])
        d, nm, nv = _adamw(w.reshape(two_d), grad[name].reshape(two_d), first[name].reshape(two_d),
                           second[name].reshape(two_d), f"adamw_{name}")
        delta[name], new_m[name], new_v[name] = d.reshape(w.shape), nm.reshape(w.shape), nv.reshape(w.shape)
    for name, w, g, cols in zip(big_names, big, big_grads, by_cols):
        outs = _adamw(w, g, rows_form(first[name], cols), rows_form(second[name], cols), f"adamw_{name}")
        grad[name], delta[name], new_m[name], new_v[name] = [
            (jnp.swapaxes(o, 0, 1) if cols else o)[None] for o in (g, *outs)]

    names = list(weight)
    return (lax.psum(loss, ("x", "y", "c")), grad_x[None], *[grad[n] for n in names], *[delta[n] for n in names],
            *[new_m[n] for n in names], *[new_v[n] for n in names])
```

```python
import functools
import math

import numpy as np
import jax
import jax.numpy as jnp
from jax import lax
from jax.experimental import pallas as pl
from jax.experimental.pallas import tpu as pltpu

F32 = jnp.float32
BF16 = jnp.bfloat16
HIGHEST = lax.Precision.HIGHEST

D_MODEL = 1024
HEAD_DIM = 64
N_HEADS = 8
HEAD_W = 512
BRANCHES = ((128, 1), (512, 4), (2048, 16))
BAND = 128
ATT_TILE = 2048
ATT_UNROLL = 8
ATT_UNROLL_BWD = 4
N_BUCKETS = 32
MAX_DISTANCE = 2048
CHUNK = 64
D_FF = 2816
EPS = 1e-6
NEG_INF = -1e30
LANES = 128
VMEM_LIMIT = 56 * 1024 * 1024

ADAM_LR = 0.001
ADAM_B1 = 0.9
ADAM_B2 = 0.999
ADAM_EPS = 1e-08
ADAM_WD = 0.01
ADAM_STEP = 10


def _nn(a, b, precision=None):
    return jnp.dot(a, b, preferred_element_type=F32, precision=precision)


def _nt(a, b, precision=None):
    return lax.dot_general(a, b, (((1,), (1,)), ((), ())), preferred_element_type=F32, precision=precision)


def _tn(a, b, precision=None):
    return lax.dot_general(a, b, (((0,), (0,)), ((), ())), preferred_element_type=F32, precision=precision)


def _params(sem, vmem=VMEM_LIMIT):
    return pltpu.CompilerParams(dimension_semantics=sem, vmem_limit_bytes=vmem)


def _sigmoid(x):
    return 0.5 * jnp.tanh(0.5 * x) + 0.5


def _silu_and_slope(x):
    s = _sigmoid(x)
    return x * s, s * (1.0 + x * (1.0 - s))


def _silu(x):
    return x * _sigmoid(x)


def _attn_tables():
    qi = np.arange(BAND)[:, None]
    kj = np.arange(2 * BAND)[None, :]
    steps = qi + BAND - kj
    in_window = (steps >= 0) & (steps <= BAND)
    max_exact = N_BUCKETS // 2
    out = np.zeros((3, 2, BAND, 2 * BAND), np.int32)
    for b, (_, dil) in enumerate(BRANCHES):
        dist = np.maximum(steps, 0) * dil
        dist_f = np.maximum(dist, 1).astype(np.float32)
        large = max_exact + (np.log(dist_f / np.float32(max_exact)) / np.float32(math.log(MAX_DISTANCE / max_exact))
                             * np.float32(N_BUCKETS - max_exact)).astype(np.int32)
        bucket = np.where(dist < max_exact, dist, np.minimum(large, N_BUCKETS - 1)).astype(np.int32)
        out[b, 0] = np.where(in_window, bucket, -1)
        out[b, 1] = np.where(in_window & (kj >= BAND), bucket, -1)
    return out


def _attention_bias(rel_bias, tables):
    def body(rel_ref, tab_ref, out_ref):
        head = pl.program_id(0)
        for b in range(3):
            tab = tab_ref[b, 0]

            def pick(kk, acc, tab=tab):
                return jnp.where(tab == kk, rel_ref[kk, head], acc)

            acc = lax.fori_loop(0, N_BUCKETS, pick, jnp.zeros((BAND, 2 * BAND), F32))
            for first in range(2):
                out_ref[0, b, first] = jnp.where(tab_ref[b, first] < 0, NEG_INF, acc)

    return pl.pallas_call(
        body,
        grid=(N_HEADS,),
        in_specs=[pl.BlockSpec(memory_space=pltpu.SMEM),
                  pl.BlockSpec((3, 2, BAND, 2 * BAND), lambda h: (0, 0, 0, 0))],
        out_specs=pl.BlockSpec((1, 3, 2, BAND, 2 * BAND), lambda h: (h, 0, 0, 0, 0)),
        out_shape=jax.ShapeDtypeStruct((N_HEADS, 3, 2, BAND, 2 * BAND), F32),
        compiler_params=_params(("arbitrary",)),
        name="attn_bias",
    )(rel_bias, tables)


def _bias_spec():
    return pl.BlockSpec((2, 3, 2, BAND, 2 * BAND), lambda p, t: (p, 0, 0, 0, 0))


def _attn_block_index(idx, t, r):
    nb = ATT_TILE // (BAND * r)
    rho = idx // nb
    n = idx % nb
    qs = rho + r * BAND * n
    gs = t * ATT_TILE + qs
    first = (t * nb + n) == 0
    ps = jnp.where(first, gs, gs - r * BAND)
    return qs, gs, ps, first.astype(jnp.int32)


def _rows(start, r):
    return pl.ds(start, BAND) if r == 1 else pl.ds(start, BAND, stride=r)


def _attention_fwd(qkv, bias, shards):
    seq = qkv.shape[0]
    n_tiles = seq // ATT_TILE
    n = len(shards)

    def body(*refs):
        bias_ref, q_ref, k_ref, v_ref = refs[:4]
        y_ref, lse_ref = refs[4 + n:6 + n]
        o_s, l_s = refs[6 + 2 * n:8 + 2 * n]
        riding = (refs[4:4 + n], refs[6 + n:6 + 2 * n], *refs[8 + 2 * n:])
        pair = pl.program_id(0)
        t = pl.program_id(1)
        if n:
            @pl.when((pair == 0) & (t == 0))
            def _():
                for cp in _gather_copies(*riding, hand_over=False)[0]:
                    cp.start()

            @pl.when((pair == 2) & (t == 0))
            def _():
                for cp, fwd in zip(*_gather_copies(*riding)):
                    cp.wait_recv()
                    fwd.start()

        lane = lax.broadcasted_iota(jnp.int32, (1, LANES), 1)
        head0 = lane < HEAD_DIM
        masks = (head0, jnp.logical_not(head0))
        ones = jnp.ones((2 * BAND, LANES), BF16)
        for b, (_, r) in enumerate(BRANCHES):
            def blocks(it, carry, b=b, r=r):
                idx = [_attn_block_index(it * ATT_UNROLL + j, t, r) for j in range(ATT_UNROLL)]
                qb = [q_ref[_rows(qs, r), :] * (HEAD_DIM ** -0.5) for qs, _, _, _ in idx]
                kcat = [jnp.concatenate([k_ref[_rows(ps, r), :], k_ref[_rows(gs, r), :]], axis=0).astype(BF16)
                        for _, gs, ps, _ in idx]
                vcat = [jnp.concatenate([v_ref[_rows(ps, r), :], v_ref[_rows(gs, r), :]], axis=0).astype(BF16)
                        for _, gs, ps, _ in idx]
                work = [(j, hh) for j in range(ATT_UNROLL) for hh in range(2)]
                s = [_nt(jnp.where(masks[hh], qb[j], 0.0).astype(BF16), kcat[j]) + bias_ref[hh, b, idx[j][3]]
                     for j, hh in work]
                m = [jnp.max(sv, axis=-1, keepdims=True) for sv in s]
                e = [jnp.exp(sv - mv) for sv, mv in zip(s, m)]
                eb = [ev.astype(BF16) for ev in e]
                den = [_nn(ev, ones) for ev in eb]
                out = [_nn(ev, vcat[j]) / dv for ev, dv, (j, _) in zip(eb, den, work)]
                lse = [mv + jnp.log(dv) for mv, dv in zip(m, den)]
                for j in range(ATT_UNROLL):
                    o_s[b, _rows(idx[j][0], r), :] = jnp.where(head0, out[2 * j], out[2 * j + 1])
                    l_s[b, _rows(idx[j][0], r), :] = jnp.where(head0, lse[2 * j], lse[2 * j + 1])
                return carry

            lax.fori_loop(0, ATT_TILE // BAND // ATT_UNROLL, blocks, 0)

        def merge(i, carry):
            rows = pl.ds(pl.multiple_of(i * BAND, BAND), BAND)
            l0, l1, l2 = l_s[0, rows, :], l_s[1, rows, :], l_s[2, rows, :]
            m = jnp.maximum(jnp.maximum(l0, l1), l2)
            w0, w1, w2 = jnp.exp(l0 - m), jnp.exp(l1 - m), jnp.exp(l2 - m)
            tot = w0 + w1 + w2
            y_ref[rows, :] = (w0 * o_s[0, rows, :] + w1 * o_s[1, rows, :] + w2 * o_s[2, rows, :]) / tot
            lse_ref[rows, :] = m + jnp.log(tot)
            return carry

        lax.fori_loop(0, ATT_TILE // BAND, merge, 0)

        if n:
            @pl.when((pair == N_HEADS // 2 - 1) & (t == n_tiles - 1))
            def _():
                first, passed = _gather_copies(*riding)
                for cp in first:
                    cp.wait_send()
                for fwd in passed:
                    fwd.wait()

    tile = pl.BlockSpec((ATT_TILE, LANES), lambda p, t: (t, p))
    sems = [pltpu.SemaphoreType.DMA((6 * n,)), pltpu.SemaphoreType.DMA((6 * n,))] if n else []
    return pl.pallas_call(
        body,
        grid=(N_HEADS // 2, n_tiles),
        in_specs=[
            _bias_spec(),
            pl.BlockSpec((ATT_TILE, LANES), lambda p, t: (t, p)),
            pl.BlockSpec((seq, LANES), lambda p, t: (0, 4 + p)),
            pl.BlockSpec((seq, LANES), lambda p, t: (0, 8 + p)),
        ] + [ANY] * n,
        out_specs=[tile, tile] + [ANY] * n,
        out_shape=[jax.ShapeDtypeStruct((seq, HEAD_W), F32), jax.ShapeDtypeStruct((seq, HEAD_W), F32)]
        + _gathered_shapes(shards),
        scratch_shapes=[
            pltpu.VMEM((3, ATT_TILE, LANES), F32),
            pltpu.VMEM((3, ATT_TILE, LANES), F32),
        ] + sems,
        compiler_params=_params(("arbitrary", "arbitrary")),
        name="attn_fwd",
    )(bias, qkv, qkv, qkv, *shards)


def _attention_bwd(qkv, dy, y, lse, bias, partials):
    seq = qkv.shape[0]
    n_tiles = seq // ATT_TILE
    n = len(partials)

    def body(*refs):
        bias_ref, q_ref, k_ref, v_ref, dy_ref, y_ref, lse_ref = refs[:7]
        dq_ref, dk_ref, dv_ref, dbias_ref = refs[7 + n:11 + n]
        riding = (refs[7:7 + n], refs[11 + n:11 + 2 * n], *refs[11 + 2 * n:])
        pair = pl.program_id(0)
        t = pl.program_id(1)
        if n:
            @pl.when((pair == 0) & (t == 0))
            def _():
                for cp in _scatter_copies(*riding):
                    cp.start()

        lane = lax.broadcasted_iota(jnp.int32, (1, LANES), 1)
        head0 = lane < HEAD_DIM

        @pl.when(t == 0)
        def _():
            dk_ref[...] = jnp.zeros_like(dk_ref)
            dv_ref[...] = jnp.zeros_like(dv_ref)
            dbias_ref[...] = jnp.zeros_like(dbias_ref)

        dq_ref[...] = jnp.zeros_like(dq_ref)

        masks = (head0, jnp.logical_not(head0))
        ones = jnp.ones((LANES, LANES), BF16)
        scale = HEAD_DIM ** -0.5
        for b, (_, r) in enumerate(BRANCHES):
            def blocks(it, carry, b=b, r=r):
                idx = [_attn_block_index(it * ATT_UNROLL_BWD + j, t, r) for j in range(ATT_UNROLL_BWD)]
                qb = [q_ref[_rows(qs, r), :] * scale for qs, _, _, _ in idx]
                kcat = [jnp.concatenate([k_ref[_rows(ps, r), :], k_ref[_rows(gs, r), :]], axis=0).astype(BF16)
                        for _, gs, ps, _ in idx]
                vcat = [jnp.concatenate([v_ref[_rows(ps, r), :], v_ref[_rows(gs, r), :]], axis=0).astype(BF16)
                        for _, gs, ps, _ in idx]
                dob = [dy_ref[_rows(qs, r), :] for qs, _, _, _ in idx]
                ob = [y_ref[_rows(qs, r), :] for qs, _, _, _ in idx]
                lb = [lse_ref[_rows(qs, r), :] for qs, _, _, _ in idx]
                work = [(j, hh) for j in range(ATT_UNROLL_BWD) for hh in range(2)]
                qh = [jnp.where(masks[hh], qb[j], 0.0).astype(BF16) for j, hh in work]
                doh = [jnp.where(masks[hh], dob[j], 0.0) for j, hh in work]
                dohb = [d.astype(BF16) for d in doh]
                s = [_nt(qh[w], kcat[j]) + bias_ref[hh, b, idx[j][3]] for w, (j, hh) in enumerate(work)]
                dp = [_nt(dohb[w], vcat[j]) for w, (j, _) in enumerate(work)]
                lrot = [pltpu.roll(lv, HEAD_DIM, 1) for lv in lb]
                lcol = [jnp.where(masks[hh], lb[j], lrot[j]) for j, hh in work]
                parts = [_split(doh[w] * ob[j]) for w, (j, _) in enumerate(work)]
                delta = [_nn(hi, ones) + _nn(lo, ones) for hi, lo in parts]
                prob = [jnp.exp(sv - jnp.concatenate([lv, lv], axis=1)) for sv, lv in zip(s, lcol)]
                ds = [pv * (dv - jnp.concatenate([de, de], axis=1)) for pv, dv, de in zip(prob, dp, delta)]
                dsb = [d.astype(BF16) for d in ds]
                dq = [_nn(dsb[w], kcat[j]) for w, (j, _) in enumerate(work)]
                dkc = [_tn(dsb[w], qh[w]) for w in range(len(work))]
                dvc = [_tn(prob[w].astype(BF16), dohb[w]) for w in range(len(work))]
                for hh in range(2):
                    dbias_ref[0, b, hh] += sum(ds[w] for w, (_, head) in enumerate(work) if head == hh)
                for j in range(ATT_UNROLL_BWD):
                    qs, gs, ps, _ = idx[j]
                    dkcat = dkc[2 * j] + dkc[2 * j + 1]
                    dvcat = dvc[2 * j] + dvc[2 * j + 1]
                    dq_ref[_rows(qs, r), :] += jnp.where(head0, dq[2 * j], dq[2 * j + 1]) * scale
                    dk_ref[_rows(ps, r), :] += dkcat[:BAND]
                    dk_ref[_rows(gs, r), :] += dkcat[BAND:]
                    dv_ref[_rows(ps, r), :] += dvcat[:BAND]
                    dv_ref[_rows(gs, r), :] += dvcat[BAND:]
                return carry

            lax.fori_loop(0, ATT_TILE // BAND // ATT_UNROLL_BWD, blocks, 0)

        if n:
            @pl.when((pair == N_HEADS // 2 - 1) & (t == n_tiles - 1))
            def _():
                for cp in _scatter_copies(*riding):
                    cp.wait()

    tile = pl.BlockSpec((ATT_TILE, LANES), lambda p, t: (t, p))
    full = pl.BlockSpec((seq, LANES), lambda p, t: (0, p))
    sems = [pltpu.SemaphoreType.DMA((3 * n,)), pltpu.SemaphoreType.DMA((3 * n,))] if n else []
    return pl.pallas_call(
        body,
        grid=(N_HEADS // 2, n_tiles),
        in_specs=[
            _bias_spec(),
            pl.BlockSpec((ATT_TILE, LANES), lambda p, t: (t, p)),
            pl.BlockSpec((seq, LANES), lambda p, t: (0, 4 + p)),
            pl.BlockSpec((seq, LANES), lambda p, t: (0, 8 + p)),
            tile, tile, tile,
        ] + [ANY] * n,
        out_specs=[tile, full, full,
                   pl.BlockSpec((1, 3, 2, BAND, 2 * BAND), lambda p, t: (p, 0, 0, 0, 0))] + [ANY] * n,
        out_shape=[jax.ShapeDtypeStruct((seq, HEAD_W), F32)] * 3
        + [jax.ShapeDtypeStruct((N_HEADS // 2, 3, 2, BAND, 2 * BAND), F32)]
        + [jax.ShapeDtypeStruct(p.shape, p.dtype) for p in partials],
        scratch_shapes=sems,
        compiler_params=_params(("arbitrary", "arbitrary")),
        name="attn_bwd",
    )(bias, qkv, qkv, qkv, dy, y, lse, *partials)


def _rel_bias_grad(dbias, tables):
    def body(tab_ref, db_ref, out_ref):
        lane = lax.broadcasted_iota(jnp.int32, (1, LANES), 1)
        out_ref[...] = jnp.zeros_like(out_ref)
        for b in range(3):
            tab = tab_ref[b, 0]

            def head(h, carry, b=b, tab=tab):
                d = db_ref[h // 2, b, h % 2]
                sums = [jnp.sum(jnp.where(tab == kk, d, 0.0), keepdims=True) for kk in range(N_BUCKETS)]
                row = jnp.zeros((1, LANES), F32)
                for kk, s in enumerate(sums):
                    row = row + jnp.where(lane == kk, s, 0.0)
                out_ref[pl.ds(h, 1), :] += row
                return carry

            lax.fori_loop(0, N_HEADS, head, 0)

    return pl.pallas_call(
        body,
        out_shape=jax.ShapeDtypeStruct((N_HEADS, LANES), F32),
        compiler_params=pltpu.CompilerParams(vmem_limit_bytes=VMEM_LIMIT),
        name="rel_bias_grad",
    )(tables, dbias)


ROW_TILE = 512


def _head_sum_matrix():
    return (lax.broadcasted_iota(jnp.int32, (HEAD_W, LANES), 0) // HEAD_DIM
            == lax.broadcasted_iota(jnp.int32, (HEAD_W, LANES), 1)).astype(F32)


def _head_spread_matrix(offset=0):
    return (lax.broadcasted_iota(jnp.int32, (LANES, HEAD_W), 0)
            == lax.broadcasted_iota(jnp.int32, (LANES, HEAD_W), 1) // HEAD_DIM + offset).astype(F32)


def _head_gather_matrix(offset=0):
    return (lax.broadcasted_iota(jnp.int32, (HEAD_W, LANES), 0) // HEAD_DIM + offset
            == lax.broadcasted_iota(jnp.int32, (HEAD_W, LANES), 1)).astype(F32)


def _split3(x):
    hi = x.astype(BF16)
    rest = x - hi.astype(F32)
    mid = rest.astype(BF16)
    return hi, mid, (rest - mid.astype(F32)).astype(BF16)


def _pick(x, onehot):
    m = onehot.astype(BF16)
    hi, mid, lo = _split3(x)
    return _nn(hi, m) + (_nn(mid, m) + _nn(lo, m))


def _pick_left(onehot, x):
    m = onehot.astype(BF16)
    hi, mid, lo = _split3(x)
    return _nn(m, hi) + (_nn(m, mid) + _nn(m, lo))


def _tri(lower, strict=False):
    r = lax.broadcasted_iota(jnp.int32, (CHUNK, CHUNK), 0)
    c = lax.broadcasted_iota(jnp.int32, (CHUNK, CHUNK), 1)
    if lower:
        return (c < r) if strict else (c <= r)
    return c >= r


def _softplus(z):
    return jnp.maximum(z, 0.0) + jnp.log(1.0 + jnp.exp(-jnp.abs(z)))


def _conv_taps(stage, w_ref, rows):
    return (w_ref[3:4, :] * stage[8:8 + rows, :] + w_ref[2:3, :] * stage[7:7 + rows, :]
            + w_ref[1:2, :] * stage[6:6 + rows, :] + w_ref[0:1, :] * stage[5:5 + rows, :])


def _l2_scale(xc, hsum, hspread):
    ssq = _pick(xc * xc, hsum)
    return _pick(lax.rsqrt(ssq + EPS), hspread)


def _stage_rows(stage, x_ref, xp_ref, i):
    stage[0:8, :] = jnp.where(i == 0, 0.0, xp_ref[...])
    stage[8:8 + ROW_TILE, :] = x_ref[...]


def _delta_prep_fwd(qkvz, ba, conv_w, alog_row, dt_row):
    seq = qkvz.shape[0]
    qkv_w = 3 * HEAD_W

    def body(x_ref, xp_ref, ba_ref, w_ref, al_ref, dt_ref, out_ref, stage):
        i = pl.program_id(0)
        _stage_rows(stage, x_ref, xp_ref, i)
        act = _silu(_conv_taps(stage, w_ref, ROW_TILE))
        hsum, hspread = _head_sum_matrix(), _head_spread_matrix()
        qc, kc = act[:, :HEAD_W], act[:, HEAD_W:2 * HEAD_W]
        out_ref[0] = qc * _l2_scale(qc, hsum, hspread) * (HEAD_DIM ** -0.5)
        out_ref[1] = kc * _l2_scale(kc, hsum, hspread)
        out_ref[2] = act[:, 2 * HEAD_W:]
        bav = ba_ref[...]
        out_ref[3] = _pick(_sigmoid(bav), hspread)
        g8 = -jnp.exp(al_ref[...]) * _softplus(bav + dt_ref[...])
        gb = _pick(g8, _head_spread_matrix(N_HEADS))
        cum = _tri(True).astype(F32)
        for ch in range(ROW_TILE // CHUNK):
            rows = slice(ch * CHUNK, (ch + 1) * CHUNK)
            out_ref[4, rows, :] = _pick_left(cum, gb[rows])

    return pl.pallas_call(
        body,
        grid=(seq // ROW_TILE,),
        in_specs=[
            pl.BlockSpec((ROW_TILE, qkv_w), lambda i: (i, 0)),
            pl.BlockSpec((8, qkv_w), lambda i: (jnp.maximum(i * (ROW_TILE // 8) - 1, 0), 0)),
            pl.BlockSpec((ROW_TILE, LANES), lambda i: (i, 0)),
            pl.BlockSpec((4, qkv_w), lambda i: (0, 0)),
            pl.BlockSpec((1, LANES), lambda i: (0, 0)),
            pl.BlockSpec((1, LANES), lambda i: (0, 0)),
        ],
        out_specs=pl.BlockSpec((5, ROW_TILE, HEAD_W), lambda i: (0, i, 0)),
        out_shape=jax.ShapeDtypeStruct((5, seq, HEAD_W), F32),
        scratch_shapes=[pltpu.VMEM((ROW_TILE + 8, qkv_w), F32)],
        compiler_params=_params(("arbitrary",)),
        name="delta_prep_fwd",
    )(qkvz, qkvz, ba, conv_w, alog_row, dt_row)


def _split(x):
    hi = x.astype(BF16)
    return hi, (x - hi.astype(F32)).astype(BF16)


def _dot3(a, b, dot=_nn):
    return dot(a[0], b[0]) + (dot(a[0], b[1]) + dot(a[1], b[0]))


def _unit_lower_inverses(mats):
    eye = (lax.broadcasted_iota(jnp.int32, (CHUNK, CHUNK), 0)
           == lax.broadcasted_iota(jnp.int32, (CHUNK, CHUNK), 1)).astype(F32)
    invs = [eye - a for a in mats]
    powers = [_split(a) for a in mats]
    for step in range(5):
        squares = [_dot3(p, p) for p in powers]
        powers = [_split(s) for s in squares]
        invs = [inv + _dot3(_split(inv), p) for inv, p in zip(invs, powers)]
    return invs


def _chunk_terms(q, k, v, beta, gc):
    causal, strict = _tri(True), _tri(True, strict=True)
    e = jnp.exp(gc)
    g_last = jnp.broadcast_to(gc[CHUNK - 1:CHUNK, :], (CHUNK, CHUNK))
    f = jnp.exp(g_last - gc)
    e_last = jnp.exp(g_last)
    decay = jnp.where(causal, jnp.exp(jnp.where(causal, gc - gc.T, 0.0)), 0.0)
    kb = k * beta
    a_mat = jnp.where(strict, _nt(kb.astype(BF16), k.astype(BF16)) * decay, 0.0)
    qk = jnp.where(causal, _nt(q.astype(BF16), k.astype(BF16)) * decay, 0.0)
    return e, f, e_last, decay, kb, a_mat, qk


GROUP = 8
UNROLL = 8


def _chunk_rows(ci):
    return pl.ds(pl.multiple_of(ci * CHUNK, CHUNK), CHUNK)


def _pair_specs(n_planes):
    return pl.BlockSpec((n_planes, GROUP * CHUNK, LANES), lambda p, g: (0, g, p))


def _delta_chunk_fwd(xs):
    seq = xs.shape[1]
    rows_per_step = GROUP * CHUNK

    def body(x_ref, inv_ref, qk_ref, u_ref, w_ref):
        for hh in range(2):
            lanes = slice(hh * HEAD_DIM, (hh + 1) * HEAD_DIM)
            rows = [slice(step * CHUNK, (step + 1) * CHUNK) for step in range(GROUP)]
            xh = [[x_ref[j, r, lanes] for j in range(5)] for r in rows]
            terms = [_chunk_terms(*x) for x in xh]
            invs = _unit_lower_inverses([t[5] for t in terms])
            for r, x, t, inv in zip(rows, xh, terms, invs):
                e, kb, qk = t[0], t[4], t[6]
                inv_parts = _split(inv)
                inv_ref[hh, r, :] = inv
                qk_ref[hh, r, :] = qk
                u_ref[hh, r, :] = _dot3(inv_parts, _split(x[2] * x[3]))
                w_ref[hh, r, :] = _dot3(inv_parts, _split(kb * e))

    out = pl.BlockSpec((2, rows_per_step, HEAD_DIM), lambda p, g: (p, g, 0))
    return pl.pallas_call(
        body,
        grid=(N_HEADS // 2, seq // rows_per_step),
        in_specs=[_pair_specs(5)],
        out_specs=[out] * 4,
        out_shape=[jax.ShapeDtypeStruct((N_HEADS, seq, HEAD_DIM), F32)] * 4,
        compiler_params=_params(("parallel", "parallel")),
        name="delta_chunk_fwd",
    )(xs)


def _decays(gc):
    g_last = jnp.broadcast_to(gc[CHUNK - 1:CHUNK, :], (CHUNK, CHUNK))
    return jnp.exp(gc), jnp.exp(g_last - gc), jnp.exp(g_last)


def _token_blocks(index, n_steps=None):
    rows_per_step = GROUP * CHUNK
    if n_steps is None:
        return pl.BlockSpec((1, rows_per_step, HEAD_W), lambda g: (index, g, 0))
    return pl.BlockSpec((1, rows_per_step, HEAD_W), lambda g: (index, n_steps - 1 - g, 0))


def _head_lanes(h):
    return pl.ds(h * HEAD_DIM, HEAD_DIM)


def _delta_scan_fwd(xs, qk_h, u_h, w_h):
    seq = xs.shape[1]
    rows_per_step = GROUP * CHUNK

    def body(q_ref, k_ref, gc_ref, qk_ref, u_ref, w_ref, o_ref, st_ref, state):
        @pl.when(pl.program_id(0) == 0)
        def _():
            state[...] = jnp.zeros_like(state)

        def chunk(ci, carry):
            rows = _chunk_rows(ci)
            heads = range(N_HEADS)
            dec = [_decays(gc_ref[0, rows, _head_lanes(h)]) for h in heads]
            s = [state[h] for h in heads]
            sb = [s[h].astype(BF16) for h in heads]
            vnb = [(u_ref[h, rows, :] - _nn(w_ref[h, rows, :].astype(BF16), sb[h])).astype(BF16) for h in heads]
            for h in heads:
                o_ref[rows, _head_lanes(h)] = (_nn((q_ref[0, rows, _head_lanes(h)] * dec[h][0]).astype(BF16), sb[h])
                                               + _nn(qk_ref[h, rows, :].astype(BF16), vnb[h]))
                st_ref[h, rows, :] = s[h]
            for h in heads:
                state[h] = s[h] * dec[h][2] + _tn((k_ref[0, rows, _head_lanes(h)] * dec[h][1]).astype(BF16), vnb[h])
            return carry

        lax.fori_loop(0, GROUP, chunk, 0)

    blk = pl.BlockSpec((N_HEADS, rows_per_step, HEAD_DIM), lambda g: (0, g, 0))
    return pl.pallas_call(
        body,
        grid=(seq // rows_per_step,),
        in_specs=[_token_blocks(0), _token_blocks(1), _token_blocks(4), blk, blk, blk],
        out_specs=[pl.BlockSpec((rows_per_step, HEAD_W), lambda g: (g, 0)), blk],
        out_shape=[jax.ShapeDtypeStruct((seq, HEAD_W), F32), jax.ShapeDtypeStruct((N_HEADS, seq, HEAD_DIM), F32)],
        scratch_shapes=[pltpu.VMEM((N_HEADS, CHUNK, CHUNK), F32)],
        compiler_params=_params(("arbitrary",)),
        name="delta_scan_fwd",
    )(xs, xs, xs, qk_h, u_h, w_h)


def _delta_scan_bwd(xs, qk_h, w_h, do):
    seq = xs.shape[1]
    rows_per_step = GROUP * CHUNK
    n_steps = seq // rows_per_step

    def body(q_ref, k_ref, gc_ref, qk_ref, w_ref, do_ref, dsn_ref, dvn_ref, dstate):
        @pl.when(pl.program_id(0) == 0)
        def _():
            dstate[...] = jnp.zeros_like(dstate)

        def chunk(step, carry):
            rows = _chunk_rows(GROUP - 1 - step)
            heads = range(N_HEADS)
            dec = [_decays(gc_ref[0, rows, _head_lanes(h)]) for h in heads]
            ds_next = [dstate[h] for h in heads]
            dob = [do_ref[rows, _head_lanes(h)].astype(BF16) for h in heads]
            dv_new = [_tn(qk_ref[h, rows, :].astype(BF16), dob[h])
                      + _nn((k_ref[0, rows, _head_lanes(h)] * dec[h][1]).astype(BF16), ds_next[h].astype(BF16))
                      for h in heads]
            for h in heads:
                dsn_ref[h, rows, :] = ds_next[h]
                dvn_ref[h, rows, :] = dv_new[h]
            for h in heads:
                dstate[h] = (_tn((q_ref[0, rows, _head_lanes(h)] * dec[h][0]).astype(BF16), dob[h])
                             + dec[h][2] * ds_next[h] - _tn(w_ref[h, rows, :].astype(BF16), dv_new[h].astype(BF16)))
            return carry

        lax.fori_loop(0, GROUP, chunk, 0)

    blk = pl.BlockSpec((N_HEADS, rows_per_step, HEAD_DIM), lambda g: (0, n_steps - 1 - g, 0))
    return pl.pallas_call(
        body,
        grid=(n_steps,),
        in_specs=[_token_blocks(0, n_steps), _token_blocks(1, n_steps), _token_blocks(4, n_steps), blk, blk,
                  pl.BlockSpec((rows_per_step, HEAD_W), lambda g: (n_steps - 1 - g, 0))],
        out_specs=[blk, blk],
        out_shape=[jax.ShapeDtypeStruct((N_HEADS, seq, HEAD_DIM), F32)] * 2,
        scratch_shapes=[pltpu.VMEM((N_HEADS, CHUNK, CHUNK), F32)],
        compiler_params=_params(("arbitrary",)),
        name="delta_scan_bwd",
    )(xs, xs, xs, qk_h, w_h, do)


def _delta_chunk_bwd(xs, inv_h, u_h, w_h, st_h, dsn_h, dvn_h, do):
    seq = xs.shape[1]
    rows_per_step = GROUP * CHUNK

    def body(x_ref, inv_ref, u_ref, w_ref, st_ref, dsn_ref, dvn_ref, do_ref, dx_ref):
        causal, strict = _tri(True), _tri(True, strict=True)
        last_row = lax.broadcasted_iota(jnp.int32, (CHUNK, CHUNK), 0) == CHUNK - 1

        def bf(vals):
            return [val.astype(BF16) for val in vals]

        def group(hh, first):
            lanes = slice(hh * HEAD_DIM, (hh + 1) * HEAD_DIM)
            rows = [slice(step * CHUNK, (step + 1) * CHUNK) for step in range(first, first + UNROLL)]
            n = range(UNROLL)
            q, k, v, beta, gc = [[x_ref[j, r, lanes] for r in rows] for j in range(5)]
            terms = [_chunk_terms(q[i], k[i], v[i], beta[i], gc[i]) for i in n]
            e, f, e_last, decay, kb, a_mat, qk = [[t[j] for t in terms] for j in range(7)]
            inv = [_split(inv_ref[hh, r, :]) for r in rows]
            u = [u_ref[hh, r, :] for r in rows]
            w = [w_ref[hh, r, :] for r in rows]
            s = [st_ref[hh, r, :] for r in rows]
            ds_next = [dsn_ref[hh, r, :] for r in rows]
            dv_new = [dvn_ref[hh, r, :] for r in rows]
            sb, dsb, dvb, wb = bf(s), bf(ds_next), bf(dv_new), bf(w)
            dob = bf([do_ref[r, lanes] for r in rows])
            qbf, kbf, kbb = bf(q), bf(k), bf(kb)
            vnb = bf([u[i] - _nn(wb[i], sb[i]) for i in n])
            dqe = [_nt(dob[i], sb[i]) for i in n]
            dw = [-_nt(dvb[i], sb[i]) for i in n]
            dkf = [_nt(vnb[i], dsb[i]) for i in n]
            dqk = [jnp.where(causal, _nt(dob[i], vnb[i]), 0.0) for i in n]
            drhs_u = [_dot3(inv[i], _split(dv_new[i]), _tn) for i in n]
            drhs_w = [_dot3(inv[i], _split(dw[i]), _tn) for i in n]
            da = [-jnp.where(strict, _nt(drhs_u[i].astype(BF16), u[i].astype(BF16))
                             + _nt(drhs_w[i].astype(BF16), wb[i]), 0.0) for i in n]
            dad = bf([da[i] * decay[i] for i in n])
            dqd = bf([dqk[i] * decay[i] for i in n])
            dkb = [e[i] * drhs_w[i] + _nn(dad[i], kbf[i]) for i in n]
            dk = [_tn(dad[i], kbb[i]) + _tn(dqd[i], qbf[i]) + f[i] * dkf[i] + beta[i] * dkb[i] for i in n]
            dq = [_nn(dqd[i], kbf[i]) + e[i] * dqe[i] for i in n]
            for i in n:
                de_full = kb[i] * drhs_w[i] + q[i] * dqe[i]
                df_full = k[i] * dkf[i]
                m = da[i] * a_mat[i] + dqk[i] * qk[i]
                dgc = de_full * e[i] - df_full * f[i] + m - m.T
                tail = jnp.sum(df_full * f[i] + s[i] * ds_next[i] * e_last[i], axis=0, keepdims=True)
                dgc = dgc + jnp.where(last_row, jnp.broadcast_to(tail, (CHUNK, CHUNK)), 0.0)
                dx_ref[0, rows[i], lanes] = dq[i]
                dx_ref[1, rows[i], lanes] = dk[i]
                dx_ref[2, rows[i], lanes] = beta[i] * drhs_u[i]
                dx_ref[3, rows[i], lanes] = v[i] * drhs_u[i] + k[i] * dkb[i]
                dx_ref[4, rows[i], lanes] = dgc

        for hh in range(2):
            for first in range(0, GROUP, UNROLL):
                group(hh, first)

    blk = pl.BlockSpec((2, rows_per_step, HEAD_DIM), lambda p, g: (p, g, 0))
    return pl.pallas_call(
        body,
        grid=(N_HEADS // 2, seq // rows_per_step),
        in_specs=[_pair_specs(5)] + [blk] * 6 + [pl.BlockSpec((rows_per_step, LANES), lambda p, g: (g, p))],
        out_specs=_pair_specs(5),
        out_shape=jax.ShapeDtypeStruct((5, seq, HEAD_W), F32),
        compiler_params=_params(("parallel", "parallel")),
        name="delta_chunk_bwd",
    )(xs, inv_h, u_h, w_h, st_h, dsn_h, dvn_h, do)


def _delta_post_fwd(o, qkvz, gain_row):
    seq = o.shape[0]

    def body(o_ref, z_ref, g_ref, y_ref):
        ov = o_ref[...]
        ms = _pick(ov * ov, _head_sum_matrix()) * (1.0 / HEAD_DIM)
        rb = _pick(lax.rsqrt(ms + EPS), _head_spread_matrix())
        y_ref[...] = (ov * rb * g_ref[...] * _silu(z_ref[...])).astype(y_ref.dtype)

    tile = pl.BlockSpec((ROW_TILE, HEAD_W), lambda i: (i, 0))
    return pl.pallas_call(
        body,
        grid=(seq // ROW_TILE,),
        in_specs=[tile, pl.BlockSpec((ROW_TILE, HEAD_W), lambda i: (i, 3)), pl.BlockSpec((1, HEAD_W), lambda i: (0, 0))],
        out_specs=tile,
        out_shape=jax.ShapeDtypeStruct((seq, HEAD_W), BF16),
        compiler_params=_params(("arbitrary",)),
        name="delta_post_fwd",
    )(o, qkvz, gain_row)


def _delta_post_bwd(dy, o, qkvz, gain_row):
    seq = o.shape[0]

    def body(dy_ref, o_ref, z_ref, g_ref, do_ref, dz_ref, dg_ref):
        @pl.when(pl.program_id(0) == 0)
        def _():
            dg_ref[...] = jnp.zeros_like(dg_ref)

        ov, zv, dyv, gain = o_ref[...], z_ref[...], dy_ref[...], g_ref[...]
        hsum, hspread = _head_sum_matrix(), _head_spread_matrix()
        ms = _pick(ov * ov, hsum) * (1.0 / HEAD_DIM)
        rb = _pick(lax.rsqrt(ms + EPS), hspread)
        ohat = ov * rb
        silu_z, slope_z = _silu_and_slope(zv)
        dz_ref[...] = dyv * ohat * gain * slope_z
        dn = dyv * silu_z
        dg_ref[0:1, :] += jnp.sum(dn * ohat, axis=0, keepdims=True)
        dohat = dn * gain

        @pl.when(pl.program_id(0) == pl.num_programs(0) - 1)
        def _():
            fold = (lax.broadcasted_iota(jnp.int32, (HEAD_W, HEAD_W), 0) % HEAD_DIM
                    == lax.broadcasted_iota(jnp.int32, (HEAD_W, HEAD_W), 1)).astype(F32)
            dg_ref[1:2, :] = _pick(dg_ref[0:1, :], fold)

        proj = _pick(_pick(dohat * ohat, hsum) * (1.0 / HEAD_DIM), hspread)
        do_ref[...] = rb * (dohat - ohat * proj)

    tile = pl.BlockSpec((ROW_TILE, HEAD_W), lambda i: (i, 0))
    return pl.pallas_call(
        body,
        grid=(seq // ROW_TILE,),
        in_specs=[pl.BlockSpec((ROW_TILE, HEAD_W), lambda i: (i, 1)), tile,
                  pl.BlockSpec((ROW_TILE, HEAD_W), lambda i: (i, 3)), pl.BlockSpec((1, HEAD_W), lambda i: (0, 0))],
        out_specs=[tile, tile, pl.BlockSpec((2, HEAD_W), lambda i: (0, 0))],
        out_shape=[jax.ShapeDtypeStruct((seq, HEAD_W), F32), jax.ShapeDtypeStruct((seq, HEAD_W), F32),
                   jax.ShapeDtypeStruct((2, HEAD_W), F32)],
        compiler_params=_params(("arbitrary",)),
        name="delta_post_bwd",
    )(dy, o, qkvz, gain_row)


def _delta_prep_bwd(qkvz, ba, conv_w, alog_row, dt_row, dxs):
    seq = qkvz.shape[0]
    qkv_w = 3 * HEAD_W

    def body(x_ref, xp_ref, ba_ref, w_ref, al_ref, dt_ref, dx_ref, dconv_ref, dba_ref, dvec_ref, stage):
        i = pl.program_id(0)

        @pl.when(i == 0)
        def _():
            dvec_ref[...] = jnp.zeros_like(dvec_ref)

        _stage_rows(stage, x_ref, xp_ref, i)
        pre = _conv_taps(stage, w_ref, ROW_TILE)
        act, slope = _silu_and_slope(pre)
        hsum, hspread = _head_sum_matrix(), _head_spread_matrix()
        for j, scale in ((0, HEAD_DIM ** -0.5), (1, 1.0)):
            cols = slice(j * HEAD_W, (j + 1) * HEAD_W)
            xc = act[:, cols]
            rb = _l2_scale(xc, hsum, hspread)
            xhat = xc * rb
            dhat = dx_ref[j] * scale
            proj = _pick(_pick(dhat * xhat, hsum), hspread)
            dconv_ref[:, cols] = rb * (dhat - xhat * proj) * slope[:, cols]
        dconv_ref[:, 2 * HEAD_W:] = dx_ref[2] * slope[:, 2 * HEAD_W:]

        bav = ba_ref[...]
        beta8 = _sigmoid(bav)
        dbeta8 = _pick(dx_ref[3], _head_gather_matrix())
        dgc8 = _pick(dx_ref[4], _head_gather_matrix(N_HEADS))
        rev = _tri(False).astype(F32)
        z = bav + dt_ref[...]
        ea = jnp.exp(al_ref[...])
        g8 = -ea * _softplus(z)
        sig = _sigmoid(z)
        d_alog = jnp.zeros((1, LANES), F32)
        d_dt = jnp.zeros((1, LANES), F32)
        for ch in range(ROW_TILE // CHUNK):
            rows = slice(ch * CHUNK, (ch + 1) * CHUNK)
            dg8 = _pick_left(rev, dgc8[rows])
            da = -dg8 * ea * sig[rows]
            dba_ref[rows, :] = dbeta8[rows] * beta8[rows] * (1.0 - beta8[rows]) + da
            d_alog = d_alog + jnp.sum(dg8 * g8[rows], axis=0, keepdims=True)
            d_dt = d_dt + jnp.sum(da, axis=0, keepdims=True)
        dvec_ref[0:1, :] += d_alog
        dvec_ref[1:2, :] += d_dt

    return pl.pallas_call(
        body,
        grid=(seq // ROW_TILE,),
        in_specs=[
            pl.BlockSpec((ROW_TILE, qkv_w), lambda i: (i, 0)),
            pl.BlockSpec((8, qkv_w), lambda i: (jnp.maximum(i * (ROW_TILE // 8) - 1, 0), 0)),
            pl.BlockSpec((ROW_TILE, LANES), lambda i: (i, 0)),
            pl.BlockSpec((4, qkv_w), lambda i: (0, 0)),
            pl.BlockSpec((1, LANES), lambda i: (0, 0)),
            pl.BlockSpec((1, LANES), lambda i: (0, 0)),
            pl.BlockSpec((5, ROW_TILE, HEAD_W), lambda i: (0, i, 0)),
        ],
        out_specs=[pl.BlockSpec((ROW_TILE, qkv_w), lambda i: (i, 0)),
                   pl.BlockSpec((ROW_TILE, LANES), lambda i: (i, 0)),
                   pl.BlockSpec((2, LANES), lambda i: (0, 0))],
        out_shape=[jax.ShapeDtypeStruct((seq, qkv_w), F32), jax.ShapeDtypeStruct((seq, LANES), F32),
                   jax.ShapeDtypeStruct((2, LANES), F32)],
        scratch_shapes=[pltpu.VMEM((ROW_TILE + 8, qkv_w), F32)],
        compiler_params=_params(("arbitrary",)),
        name="delta_prep_bwd",
    )(qkvz, qkvz, ba, conv_w, alog_row, dt_row, dxs)


def _conv_bwd(dconv, qkvz, conv_w):
    seq = dconv.shape[0]
    qkv_w = 3 * HEAD_W
    n_tiles = seq // ROW_TILE

    def body(dy_ref, dyn_ref, x_ref, xp_ref, w_ref, dx_ref, dw_ref, stage, dstage):
        i = pl.program_id(0)

        @pl.when(i == 0)
        def _():
            dw_ref[...] = jnp.zeros_like(dw_ref)

        _stage_rows(stage, x_ref, xp_ref, i)
        dstage[0:ROW_TILE, :] = dy_ref[...]
        dstage[ROW_TILE:ROW_TILE + 8, :] = jnp.where(i == n_tiles - 1, 0.0, dyn_ref[...])
        dy = dy_ref[...]
        dx_ref[...] = (w_ref[3:4, :] * dy + w_ref[2:3, :] * dstage[1:1 + ROW_TILE, :]
                       + w_ref[1:2, :] * dstage[2:2 + ROW_TILE, :] + w_ref[0:1, :] * dstage[3:3 + ROW_TILE, :])
        for j in range(4):
            dw_ref[j:j + 1, :] += jnp.sum(dy * stage[5 + j:5 + j + ROW_TILE, :], axis=0, keepdims=True)

    tile = pl.BlockSpec((ROW_TILE, qkv_w), lambda i: (i, 0))
    return pl.pallas_call(
        body,
        grid=(n_tiles,),
        in_specs=[
            tile,
            pl.BlockSpec((8, qkv_w), lambda i: (jnp.minimum((i + 1) * (ROW_TILE // 8), seq // 8 - 1), 0)),
            tile,
            pl.BlockSpec((8, qkv_w), lambda i: (jnp.maximum(i * (ROW_TILE // 8) - 1, 0), 0)),
            pl.BlockSpec((4, qkv_w), lambda i: (0, 0)),
        ],
        out_specs=[tile, pl.BlockSpec((4, qkv_w), lambda i: (0, 0))],
        out_shape=[jax.ShapeDtypeStruct((seq, qkv_w), F32), jax.ShapeDtypeStruct((4, qkv_w), F32)],
        scratch_shapes=[pltpu.VMEM((ROW_TILE + 8, qkv_w), F32), pltpu.VMEM((ROW_TILE + 8, qkv_w), F32)],
        compiler_params=_params(("arbitrary",)),
        name="conv_bwd",
    )(dconv, dconv, qkvz, qkvz, conv_w)


FF_TILE = 1408
WGRAD_ROWS = 1024


def _row(a):
    return pl.BlockSpec((1, a), lambda *_: (0, 0))


def _rms_fwd(xv, gain):
    rstd = lax.rsqrt(jnp.mean(xv * xv, axis=-1, keepdims=True) + EPS)
    xhat = xv * rstd
    return xhat, rstd, xhat * gain


def _rms_bwd(dnorm, xhat, rstd, gain):
    dxhat = dnorm * gain
    dx = rstd * (dxhat - xhat * jnp.mean(dxhat * xhat, axis=-1, keepdims=True))
    return dx, jnp.sum(dnorm * xhat, axis=0, keepdims=True)


def _inproj_fwd(x, gain, scale, shift, w_a, w_d, w_ba):
    seq = x.shape[0]

    def body(x_ref, g_ref, sc_ref, sh_ref, wa_ref, wd_ref, wb_ref, h_ref, a_ref, d_ref, b_ref):
        _, _, norm = _rms_fwd(x_ref[...], g_ref[...])
        h = (norm * (1.0 + sc_ref[...]) + sh_ref[...]).astype(BF16)
        h_ref[...] = h
        a_ref[...] = _nt(h, wa_ref[...])
        d_ref[...] = _nt(h, wd_ref[...])
        b_ref[...] = _nt(h, wb_ref[...])

    def rows(width):
        return pl.BlockSpec((ROW_TILE, width), lambda i: (i, 0))

    def whole(a):
        return pl.BlockSpec(a.shape, lambda i: (0, 0))

    return pl.pallas_call(
        body,
        grid=(seq // ROW_TILE,),
        in_specs=[rows(D_MODEL), _row(D_MODEL), _row(D_MODEL), _row(D_MODEL), whole(w_a), whole(w_d), whole(w_ba)],
        out_specs=[rows(D_MODEL), rows(3 * HEAD_W), rows(4 * HEAD_W), rows(LANES)],
        out_shape=[jax.ShapeDtypeStruct((seq, D_MODEL), BF16), jax.ShapeDtypeStruct((seq, 3 * HEAD_W), F32),
                   jax.ShapeDtypeStruct((seq, 4 * HEAD_W), F32), jax.ShapeDtypeStruct((seq, LANES), F32)],
        compiler_params=_params(("arbitrary",)),
        name="inproj_fwd",
    )(x, gain, scale, shift, w_a, w_d, w_ba)


def _outproj_fwd(y_attn, y_delta, w_out, x, gate1, gain, scale, shift):
    seq = x.shape[0]

    def body(ya_ref, yd_ref, wa_ref, wd_ref, x_ref, g1_ref, g_ref, sc_ref, sh_ref, x1_ref, h_ref, y_ref):
        y = _nn(ya_ref[...].astype(BF16), wa_ref[...]) + _nn(yd_ref[...], wd_ref[...])
        x1 = x_ref[...] + g1_ref[...] * y
        _, _, norm = _rms_fwd(x1, g_ref[...])
        x1_ref[...] = x1
        h_ref[...] = (norm * (1.0 + sc_ref[...]) + sh_ref[...]).astype(BF16)
        y_ref[...] = y.astype(BF16)

    def rows(width):
        return pl.BlockSpec((ROW_TILE, width), lambda i: (i, 0))

    return pl.pallas_call(
        body,
        grid=(seq // ROW_TILE,),
        in_specs=[rows(HEAD_W), rows(HEAD_W),
                  pl.BlockSpec((HEAD_W, D_MODEL), lambda i: (0, 0)), pl.BlockSpec((HEAD_W, D_MODEL), lambda i: (1, 0)),
                  rows(D_MODEL), _row(D_MODEL), _row(D_MODEL), _row(D_MODEL), _row(D_MODEL)],
        out_specs=[rows(D_MODEL), rows(D_MODEL), rows(D_MODEL)],
        out_shape=[jax.ShapeDtypeStruct((seq, D_MODEL), F32), jax.ShapeDtypeStruct((seq, D_MODEL), BF16),
                   jax.ShapeDtypeStruct((seq, D_MODEL), BF16)],
        compiler_params=_params(("arbitrary",)),
        name="outproj_fwd",
    )(y_attn, y_delta, w_out, w_out, x, gate1, gain, scale, shift)


def _ffn_fwd(h2, w_gate, w_up, w_down, x1, gate2, final_gain, target):
    seq = h2.shape[0]
    n_rows, n_ff = seq // ROW_TILE, D_FF // FF_TILE

    def body(h_ref, wg_ref, wu_ref, wd_ref, x1_ref, g2_ref, gf_ref, t_ref, gate_ref, up_ref, dx2_ref, st_ref, acc):
        i, j = pl.program_id(0), pl.program_id(1)

        @pl.when((i == 0) & (j == 0))
        def _():
            st_ref[...] = jnp.zeros_like(st_ref)

        h = h_ref[...]
        gate = _nt(h, wg_ref[...])
        up = _nt(h, wu_ref[...])
        gate_ref[...] = gate.astype(BF16)
        up_ref[...] = up.astype(BF16)
        part = _nn((_silu(gate) * up).astype(BF16), wd_ref[...])

        @pl.when(j == 0)
        def _():
            acc[...] = part

        @pl.when(j > 0)
        def _():
            acc[...] += part

        @pl.when(j == n_ff - 1)
        def _():
            y2 = acc[...]
            x2 = x1_ref[...] + g2_ref[...] * y2
            xhat, rstd, out = _rms_fwd(x2, gf_ref[...])
            diff = out - t_ref[...]
            dx2, dgain = _rms_bwd(diff * (1.0 / D_MODEL), xhat, rstd, gf_ref[...])
            dx2_ref[...] = dx2
            st_ref[0:1, :] += dgain
            st_ref[1:2, :] += jnp.sum(dx2 * y2, axis=0, keepdims=True)
            st_ref[2:3, :] += jnp.sum(diff * diff, axis=0, keepdims=True) * (0.5 / D_MODEL)

        @pl.when((i == n_rows - 1) & (j == n_ff - 1))
        def _():
            st_ref[3:4, :] = jnp.broadcast_to(jnp.sum(st_ref[2:3, :], keepdims=True), (1, D_MODEL))

    def rows(width):
        return pl.BlockSpec((ROW_TILE, width), lambda i, j: (i, 0))

    ff = pl.BlockSpec((ROW_TILE, FF_TILE), lambda i, j: (i, j))
    return pl.pallas_call(
        body,
        grid=(n_rows, n_ff),
        in_specs=[rows(D_MODEL),
                  pl.BlockSpec((FF_TILE, D_MODEL), lambda i, j: (j, 0)), pl.BlockSpec((FF_TILE, D_MODEL), lambda i, j: (j, 0)),
                  pl.BlockSpec((FF_TILE, D_MODEL), lambda i, j: (j, 0)),
                  rows(D_MODEL), _row(D_MODEL), _row(D_MODEL), rows(D_MODEL)],
        out_specs=[ff, ff, rows(D_MODEL), pl.BlockSpec((8, D_MODEL), lambda i, j: (0, 0))],
        out_shape=[jax.ShapeDtypeStruct((seq, D_FF), BF16), jax.ShapeDtypeStruct((seq, D_FF), BF16),
                   jax.ShapeDtypeStruct((seq, D_MODEL), F32), jax.ShapeDtypeStruct((8, D_MODEL), F32)],
        scratch_shapes=[pltpu.VMEM((ROW_TILE, D_MODEL), F32)],
        compiler_params=_params(("arbitrary", "arbitrary")),
        name="ffn_fwd",
    )(h2, w_gate, w_up, w_down, x1, gate2, final_gain, target)


def _ffn_bwd(dx2, gate, up, w_gate, w_up, w_down, x1, y, gate2, gate1, gain, scale):
    seq = dx2.shape[0]

    def act_body(dx2_ref, g2_ref, gate_ref, up_ref, wd_ref, dgate_ref, dup_ref, act_ref, dy2_ref):
        dy2 = (g2_ref[...] * dx2_ref[...]).astype(BF16)
        dy2_ref[...] = dy2
        gate = gate_ref[...].astype(F32)
        up = up_ref[...].astype(F32)
        dact = _nt(dy2, wd_ref[...])
        silu, slope = _silu_and_slope(gate)
        act_ref[...] = (silu * up).astype(BF16)
        dgate_ref[...] = (dact * up * slope).astype(BF16)
        dup_ref[...] = (dact * silu).astype(BF16)

    def rows2(width):
        return pl.BlockSpec((ROW_TILE, width), lambda i, j: (i, 0))

    ff = pl.BlockSpec((ROW_TILE, FF_TILE), lambda i, j: (i, j))
    dgate, dup, act, dy2 = pl.pallas_call(
        act_body,
        grid=(seq // ROW_TILE, D_FF // FF_TILE),
        in_specs=[rows2(D_MODEL), _row(D_MODEL), ff, ff, pl.BlockSpec((FF_TILE, D_MODEL), lambda i, j: (j, 0))],
        out_specs=[ff, ff, ff, rows2(D_MODEL)],
        out_shape=[jax.ShapeDtypeStruct((seq, D_FF), BF16)] * 3 + [jax.ShapeDtypeStruct((seq, D_MODEL), BF16)],
        compiler_params=_params(("arbitrary", "arbitrary")),
        name="ffn_bwd_act",
    )(dx2, gate2, gate, up, w_down)

    def in_body(dgate_ref, dup_ref, wg_ref, wu_ref, dx2_ref, x1_ref, y_ref, g1_ref, g_ref, sc_ref,
                dx1_ref, dy_ref, st_ref):
        @pl.when(pl.program_id(0) == 0)
        def _():
            st_ref[...] = jnp.zeros_like(st_ref)

        dh = _nn(dgate_ref[...], wg_ref[...]) + _nn(dup_ref[...], wu_ref[...])
        xhat, rstd, norm = _rms_fwd(x1_ref[...], g_ref[...])
        dxn, dgain = _rms_bwd(dh * (1.0 + sc_ref[...]), xhat, rstd, g_ref[...])
        dx1 = dx2_ref[...] + dxn
        dx1_ref[...] = dx1
        dy_ref[...] = (g1_ref[...] * dx1).astype(BF16)
        st_ref[0:1, :] += jnp.sum(dh, axis=0, keepdims=True)
        st_ref[1:2, :] += jnp.sum(dh * norm, axis=0, keepdims=True)
        st_ref[2:3, :] += dgain
        st_ref[3:4, :] += jnp.sum(dx1 * y_ref[...].astype(F32), axis=0, keepdims=True)

    half_tile = ROW_TILE // 2

    def rows(width):
        return pl.BlockSpec((half_tile, width), lambda i: (i, 0))

    whole = pl.BlockSpec((D_FF, D_MODEL), lambda i: (0, 0))
    dx1, dy, stats = pl.pallas_call(
        in_body,
        grid=(seq // half_tile,),
        in_specs=[rows(D_FF), rows(D_FF), whole, whole, rows(D_MODEL), rows(D_MODEL), rows(D_MODEL),
                  _row(D_MODEL), _row(D_MODEL), _row(D_MODEL)],
        out_specs=[rows(D_MODEL), rows(D_MODEL), pl.BlockSpec((8, D_MODEL), lambda i: (0, 0))],
        out_shape=[jax.ShapeDtypeStruct((seq, D_MODEL), F32), jax.ShapeDtypeStruct((seq, D_MODEL), BF16),
                   jax.ShapeDtypeStruct((8, D_MODEL), F32)],
        compiler_params=_params(("arbitrary",)),
        name="ffn_bwd_in",
    )(dgate, dup, w_gate, w_up, dx2, x1, y, gate1, gain, scale)
    return dgate, dup, act, dy2, dx1, dy, stats


def _outproj_bwd(dy, w_out):
    seq = dy.shape[0]

    def body(dy_ref, w_ref, out_ref):
        out_ref[...] = _nt(dy_ref[...], w_ref[...])

    rows = pl.BlockSpec((ROW_TILE, D_MODEL), lambda i: (i, 0))
    return pl.pallas_call(
        body,
        grid=(seq // ROW_TILE,),
        in_specs=[rows, pl.BlockSpec((D_MODEL, D_MODEL), lambda i: (0, 0))],
        out_specs=rows,
        out_shape=jax.ShapeDtypeStruct((seq, D_MODEL), F32),
        compiler_params=_params(("arbitrary",)),
        name="outproj_bwd",
    )(dy, w_out)


def _inproj_bwd(dq, dk, dv, dxd, dz, dba, w_a, w_d, w_ba, x, dx1, gain, scale, partials):
    seq = x.shape[0]
    n = len(partials)
    n_steps = seq // ROW_TILE

    def body(*refs):
        (dq_ref, dk_ref, dv_ref, dxd_ref, dz_ref, dba_ref, wa_ref, wd_ref, wb_ref, x_ref, dx1_ref, g_ref,
         sc_ref) = refs[:13]
        gx_ref, st_ref = refs[13 + n:15 + n]
        riding = (refs[13:13 + n], refs[15 + n:15 + 2 * n], *refs[15 + 2 * n:])

        @pl.when(pl.program_id(0) == 0)
        def _():
            st_ref[...] = jnp.zeros_like(st_ref)
            for cp in (_scatter_copies(*riding) if n else []):
                cp.start()

        dh = (_nn(dq_ref[...].astype(BF16), wa_ref[0:HEAD_W, :])
              + _nn(dk_ref[...].astype(BF16), wa_ref[HEAD_W:2 * HEAD_W, :])
              + _nn(dv_ref[...].astype(BF16), wa_ref[2 * HEAD_W:, :])
              + _nn(dxd_ref[...].astype(BF16), wd_ref[0:3 * HEAD_W, :])
              + _nn(dz_ref[...].astype(BF16), wd_ref[3 * HEAD_W:, :])
              + _nn(dba_ref[...].astype(BF16), wb_ref[...]))
        xhat, rstd, norm = _rms_fwd(x_ref[...], g_ref[...])
        dxn, dgain = _rms_bwd(dh * (1.0 + sc_ref[...]), xhat, rstd, g_ref[...])
        gx_ref[...] = dx1_ref[...] + dxn
        st_ref[0:1, :] += jnp.sum(dh, axis=0, keepdims=True)
        st_ref[1:2, :] += jnp.sum(dh * norm, axis=0, keepdims=True)
        st_ref[2:3, :] += dgain

        if n:
            @pl.when(pl.program_id(0) == n_steps - 1)
            def _():
                for cp in _scatter_copies(*riding):
                    cp.wait()

    def rows(width):
        return pl.BlockSpec((ROW_TILE, width), lambda i: (i, 0))

    def whole(a):
        return pl.BlockSpec(a.shape, lambda i: (0, 0))

    sems = [pltpu.SemaphoreType.DMA((3 * n,)), pltpu.SemaphoreType.DMA((3 * n,))] if n else []
    return pl.pallas_call(
        body,
        grid=(n_steps,),
        in_specs=[rows(HEAD_W), rows(HEAD_W), rows(HEAD_W), rows(3 * HEAD_W), rows(HEAD_W), rows(LANES),
                  whole(w_a), whole(w_d), whole(w_ba), rows(D_MODEL), rows(D_MODEL), _row(D_MODEL), _row(D_MODEL)]
        + [ANY] * n,
        out_specs=[rows(D_MODEL), pl.BlockSpec((8, D_MODEL), lambda i: (0, 0))] + [ANY] * n,
        out_shape=[jax.ShapeDtypeStruct((seq, D_MODEL), F32), jax.ShapeDtypeStruct((8, D_MODEL), F32)]
        + [jax.ShapeDtypeStruct(p.shape, p.dtype) for p in partials],
        scratch_shapes=sems,
        compiler_params=_params(("arbitrary",)),
        name="inproj_bwd",
    )(dq, dk, dv, dxd, dz, dba, w_a, w_d, w_ba, x, dx1, gain, scale, *partials)


def _weight_grad(a, b, name):
    seq, m = a.shape
    n = b.shape[1]
    tm = m if m <= 1536 else m // 2
    tn = n if n <= 1536 else n // 2
    rows = 2 * WGRAD_ROWS
    n_k = seq // rows

    def body(a_ref, b_ref, out_ref):
        part = _tn(a_ref[...].astype(BF16), b_ref[...].astype(BF16))

        @pl.when(pl.program_id(2) == 0)
        def _():
            out_ref[...] = part

        @pl.when(pl.program_id(2) > 0)
        def _():
            out_ref[...] += part

    return pl.pallas_call(
        body,
        grid=(m // tm, n // tn, n_k),
        in_specs=[pl.BlockSpec((rows, tm), lambda i, j, k: (k, i)),
                  pl.BlockSpec((rows, tn), lambda i, j, k: (k, j))],
        out_specs=pl.BlockSpec((tm, tn), lambda i, j, k: (i, j)),
        out_shape=jax.ShapeDtypeStruct((m, n), F32),
        compiler_params=_params(("arbitrary", "arbitrary", "arbitrary")),
        name=name,
    )(a, b)


def _weight_grad_stack(pieces, b, name):
    seq, n = b.shape
    widths = [a.shape[1] for a in pieces]
    starts = [sum(widths[:i]) for i in range(len(pieces))]

    def body(*refs):
        a_refs, b_ref, out_ref = refs[:len(pieces)], refs[len(pieces)], refs[len(pieces) + 1]

        @pl.when(pl.program_id(0) == 0)
        def _():
            out_ref[...] = jnp.zeros_like(out_ref)

        bb = b_ref[...].astype(BF16)
        for a_ref, start, width in zip(a_refs, starts, widths):
            out_ref[start:start + width, :] += _tn(a_ref[...].astype(BF16), bb)

    def rows(width):
        return pl.BlockSpec((WGRAD_ROWS, width), lambda k: (k, 0))

    return pl.pallas_call(
        body,
        grid=(seq // WGRAD_ROWS,),
        in_specs=[rows(w) for w in widths] + [rows(n)],
        out_specs=pl.BlockSpec((sum(widths), n), lambda k: (0, 0)),
        out_shape=jax.ShapeDtypeStruct((sum(widths), n), F32),
        compiler_params=_params(("arbitrary",)),
        name=name,
    )(*pieces, b)


def _adamw(w, g, m, v, name):
    n_rows, n_cols = w.shape
    if w.size <= 64 * 1024:
        block, grid, index = (n_rows, n_cols), (1,), lambda i: (0, 0)
    elif n_rows % 256 == 0:
        block, grid, index = (256, n_cols), (n_rows // 256,), lambda i: (i, 0)
    elif n_cols % 256 == 0:
        block, grid, index = (n_rows, 256), (n_cols // 256,), lambda i: (0, i)
    else:
        block, grid, index = (n_rows, n_cols), (1,), lambda i: (0, 0)

    def body(w_ref, g_ref, m_ref, v_ref, d_ref, nm_ref, nv_ref):
        gv = g_ref[...]
        nm = ADAM_B1 * m_ref[...] + (1.0 - ADAM_B1) * gv
        nv = ADAM_B2 * v_ref[...] + (1.0 - ADAM_B2) * (gv * gv)
        m_hat = nm / (1.0 - ADAM_B1 ** ADAM_STEP)
        v_hat = nv / (1.0 - ADAM_B2 ** ADAM_STEP)
        d_ref[...] = -ADAM_LR * (m_hat / (jnp.sqrt(v_hat) + ADAM_EPS) + ADAM_WD * w_ref[...])
        nm_ref[...] = nm
        nv_ref[...] = nv

    blk = pl.BlockSpec(block, index)
    shape = jax.ShapeDtypeStruct((n_rows, n_cols), F32)
    return pl.pallas_call(
        body,
        grid=grid,
        in_specs=[blk] * 4,
        out_specs=[blk] * 3,
        out_shape=[shape] * 3,
        compiler_params=_params(("arbitrary",)),
        name=name,
    )(w, g, m, v)


IN_WIDTH = 3600
BA_COL = 7 * HEAD_W


def _local_step(x, target, mod, norm_attn_g, w_in, rel_bias, conv_w, a_log, dt_bias, delta_norm_g,
                norm_ffn_g, final_norm_g, shards, assemble, reduce_pairs):
    sh1, sc1, g1, sh2, sc2, g2 = [mod[:, i * D_MODEL:(i + 1) * D_MODEL] for i in range(6)]
    w_a = w_in[:3 * HEAD_W]
    w_d = w_in[3 * HEAD_W:BA_COL]
    w_ba = jnp.pad(w_in[BA_COL:], ((0, LANES - 2 * N_HEADS), (0, 0)))
    tables = jnp.asarray(_attn_tables())
    alog_row = jnp.pad(a_log, ((0, 0), (N_HEADS, LANES - 2 * N_HEADS)))
    dt_row = jnp.pad(dt_bias, ((0, 0), (N_HEADS, LANES - 2 * N_HEADS)))
    gain_row = jnp.tile(delta_norm_g, (1, N_HEADS))

    h1, qkv_a, qkvz, ba = _inproj_fwd(x, norm_attn_g, sc1, sh1, w_a, w_d, w_ba)
    bias = _attention_bias(rel_bias, tables)
    y_attn, lse, *gathered = _attention_fwd(qkv_a, bias, shards)
    w_out, w_gate, w_up, w_down = assemble(gathered)
    xs = _delta_prep_fwd(qkvz, ba, conv_w, alog_row, dt_row)
    inv_h, qk_h, u_h, w_h = _delta_chunk_fwd(xs)
    o, st_h = _delta_scan_fwd(xs, qk_h, u_h, w_h)
    y_delta = _delta_post_fwd(o, qkvz, gain_row)
    x1, h2, y = _outproj_fwd(y_attn, y_delta, w_out, x, g1, norm_ffn_g, sc2, sh2)
    gate, up, dx2, st_f = _ffn_fwd(h2, w_gate, w_up, w_down, x1, g2, final_norm_g, target)

    dgate, dup, act, dy2, dx1, dy, st_b = _ffn_bwd(dx2, gate, up, w_gate, w_up, w_down, x1, y, g2, g1, norm_ffn_g, sc2)
    partials = reduce_pairs([_weight_grad_stack([y_attn, y_delta], dy, "wgrad_out"),
                             _weight_grad(dgate, h2, "wgrad_gate"), _weight_grad(dup, h2, "wgrad_up"),
                             _weight_grad(act, dy2, "wgrad_down")], 1, "rest")
    grads = {}
    dycat = _outproj_bwd(dy, w_out)
    do, dz, dgain = _delta_post_bwd(dycat, o, qkvz, gain_row)
    dsn_h, dvn_h = _delta_scan_bwd(xs, qk_h, w_h, do)
    dxs = _delta_chunk_bwd(xs, inv_h, u_h, w_h, st_h, dsn_h, dvn_h, do)
    dconv, dba, dvec = _delta_prep_bwd(qkvz, ba, conv_w, alog_row, dt_row, dxs)
    dxd, grads["conv_w"] = _conv_bwd(dconv, qkvz, conv_w)
    dq, dk, dv, dbias, *scattered = _attention_bwd(qkv_a, dycat, y_attn, lse, bias, partials)
    partials_in = reduce_pairs([jnp.concatenate(
        [_weight_grad_stack([dq, dk, dv], h1, "wgrad_in_attn"),
         _weight_grad_stack([dxd, dz, dba], h1, "wgrad_in_delta")[:IN_WIDTH - 3 * HEAD_W]], axis=0)], 0, "in")
    grad_x, st_i, *scattered_in = _inproj_bwd(dq, dk, dv, dxd, dz, dba, w_a, w_d, w_ba, x, dx1, norm_attn_g, sc1,
                                              partials_in)
    grads["rel_bias"] = _rel_bias_grad(dbias, tables)[:, :N_BUCKETS].T
    grads["a_log"] = dvec[0:1, N_HEADS:2 * N_HEADS]
    grads["dt_bias"] = dvec[1:2, N_HEADS:2 * N_HEADS]
    grads["delta_norm_g"] = dgain[1:2, :HEAD_DIM]
    grads["norm_attn_g"] = st_i[2:3]
    grads["norm_ffn_g"] = st_b[2:3]
    grads["final_norm_g"] = st_f[0:1]
    dmod = jnp.concatenate([st_i[0:1], st_i[1:2], st_b[3:4], st_b[0:1], st_b[1:2], st_f[1:2]], axis=1)
    return st_f[3, 0], grad_x, grads, dmod, (partials_in + partials, scattered_in + scattered)


MESH = pl.DeviceIdType.MESH
OTHER_CHIPS = ((1, 0), (0, 1), (1, 1))
ALL_PEERS = tuple((m >> 2 & 1, m >> 1 & 1, m & 1) for m in range(1, 8))
ANY = pl.BlockSpec(memory_space=pl.ANY)
VMEM_SPEC = pl.BlockSpec(memory_space=pltpu.VMEM)


def _me():
    return lax.axis_index("x"), lax.axis_index("y"), lax.axis_index("c")


def _flip(pos, mask):
    return tuple(1 - p if m else p for p, m in zip(pos, mask))


def _remote(src, dst, send_sems, recv_sems, k, to):
    return pltpu.make_async_remote_copy(src_ref=src, dst_ref=dst, send_sem=send_sems.at[k], recv_sem=recv_sems.at[k],
                                        device_id=to, device_id_type=MESH)


def _ada_exchange(c8, w_ada, b_ada, conv8, shard):
    def body(c_ref, w_ref, b_ref, cv_ref, shard_ref, mod_ref, cact_ref, conv_ref, whole_ref,
             c_all, part_all, send_sems, recv_sems, ride_send, ride_recv):
        x, y, c = me = _me()
        dev = 4 * x + 2 * y + c
        chip = 2 * x + y
        riding = ([shard_ref], [whole_ref], ride_send, ride_recv)
        for cp in _gather_copies(*riding, hand_over=False)[0]:
            cp.start()
        c_all[dev] = c_ref[...]
        conv_ref[chip] = cv_ref[...]
        first = [_remote(c_ref, c_all.at[dev], send_sems, recv_sems, k, _flip(me, mask))
                 for k, mask in enumerate(ALL_PEERS)]
        first += [_remote(cv_ref, conv_ref.at[chip], send_sems, recv_sems, 7 + j, _flip(me, (*mask, 0)))
                  for j, mask in enumerate(OTHER_CHIPS)]
        for cp in first:
            cp.start()
        for cp in first:
            cp.wait()
        row = lax.broadcasted_iota(jnp.int32, (8, D_MODEL), 0)
        c_rows = jnp.zeros((8, D_MODEL), F32)
        for d in range(8):
            c_rows = jnp.where(row == d, c_all[d], c_rows)
        c_act = _silu(c_rows)
        cact_ref[...] = c_act
        part_all[chip] = _nn(c_act, w_ref[...], HIGHEST)
        second = [_remote(part_all.at[chip], part_all.at[chip], send_sems, recv_sems, 10 + j, _flip(me, (*mask, 0)))
                  for j, mask in enumerate(OTHER_CHIPS)]
        for cp in second:
            cp.start()
        for cp in second:
            cp.wait()
        cols = w_ref.shape[1]
        for k in range(4):
            mod_ref[:, k * cols:(k + 1) * cols] = part_all[k] + b_ref[:, k * cols:(k + 1) * cols]
        first, passed = _gather_copies(*riding)
        for cp, fwd in zip(first, passed):
            cp.wait_recv()
            fwd.start()
        for cp in first:
            cp.wait_send()
        for fwd in passed:
            fwd.wait()

    cols = w_ada.shape[1]
    return pl.pallas_call(
        body,
        in_specs=[VMEM_SPEC] * 4 + [ANY],
        out_specs=[VMEM_SPEC] * 3 + [ANY],
        out_shape=[jax.ShapeDtypeStruct((8, 4 * cols), F32), jax.ShapeDtypeStruct((8, D_MODEL), F32),
                   jax.ShapeDtypeStruct((4, 8, conv8.shape[1]), F32)] + _gathered_shapes([shard]),
        scratch_shapes=[pltpu.VMEM((8, 8, D_MODEL), F32), pltpu.VMEM((4, 8, cols), F32),
                        pltpu.SemaphoreType.DMA((13,)), pltpu.SemaphoreType.DMA((13,)),
                        pltpu.SemaphoreType.DMA((6,)), pltpu.SemaphoreType.DMA((6,))],
        compiler_params=pltpu.CompilerParams(vmem_limit_bytes=VMEM_LIMIT),
        name="ada_exchange",
    )(c8, w_ada, b_ada, conv8, shard)


def _gathered_shapes(shards):
    return [jax.ShapeDtypeStruct((4, *s.shape), s.dtype) for s in shards]


def _gather_copies(srcs, dsts, send_sems, recv_sems, hand_over=True):
    x, y, c = me = _me()
    chip = 2 * x + y
    sibling = _flip(me, (0, 0, 1))
    first, passed = [], []
    for a, (src, dst) in enumerate(zip(srcs, dsts)):
        for j, mask in enumerate(OTHER_CHIPS):
            to = _flip(me, (*mask, 0))
            first.append(_remote(src.at[c], dst.at[chip, c], send_sems, recv_sems, 6 * a + j, to))
            if hand_over:
                landed = dst.at[2 * to[0] + to[1], c]
                passed.append(_remote(landed, landed, send_sems, recv_sems, 6 * a + 3 + j, sibling))
    return first, passed


def _scatter_copies(srcs, dsts, send_sems, recv_sems):
    x, y, c = me = _me()
    chip = 2 * x + y
    copies = []
    for a, (src, dst) in enumerate(zip(srcs, dsts)):
        for j, mask in enumerate(OTHER_CHIPS):
            to = _flip(me, (*mask, 0))
            copies.append(_remote(src.at[2 * to[0] + to[1]], dst.at[chip], send_sems, recv_sems, 3 * a + j, to))
    return copies


def _start_and_wait(copies):
    for cp in copies:
        cp.start()
    for cp in copies:
        cp.wait()


def _swap_halves(grads):
    n = len(grads)

    def body(*refs):
        srcs, got = refs[:n], refs[n:2 * n]
        send_sems, recv_sems = refs[2 * n:]
        x, y, c = me = _me()
        _start_and_wait([_remote(srcs[a].at[:, 1 - c], got[a], send_sems, recv_sems, a, _flip(me, (0, 0, 1)))
                         for a in range(n)])

    return pl.pallas_call(
        body,
        in_specs=[ANY] * n,
        out_specs=[ANY] * n,
        out_shape=[jax.ShapeDtypeStruct((4, g.shape[2], g.shape[3]), g.dtype) for g in grads],
        scratch_shapes=[pltpu.SemaphoreType.DMA((n,)), pltpu.SemaphoreType.DMA((n,))],
        name=f"swap_halves_{n}",
    )(*grads)


def _join_halves(halves):
    n = len(halves)

    def body(*refs):
        srcs, dsts = refs[:n], refs[n:2 * n]
        send_sems, recv_sems = refs[2 * n:]
        x, y, c = me = _me()
        _start_and_wait([_remote(srcs[a], dsts[a].at[c], send_sems, recv_sems, a, _flip(me, (0, 0, 1)))
                         for a in range(n)])

    return pl.pallas_call(
        body,
        in_specs=[ANY] * n,
        out_specs=[ANY] * n,
        out_shape=[jax.ShapeDtypeStruct((2, *h.shape), h.dtype) for h in halves],
        scratch_shapes=[pltpu.SemaphoreType.DMA((n,)), pltpu.SemaphoreType.DMA((n,))],
        name=f"join_halves_{n}",
    )(*halves)


def _gather_small(packed):
    n_rows = packed.shape[0]

    def body(p_ref, all_ref, sum_ref, send_sems, recv_sems):
        x, y, c = me = _me()
        dev = 4 * x + 2 * y + c
        all_ref[dev] = p_ref[...]
        copies = [_remote(p_ref, all_ref.at[dev], send_sems, recv_sems, k, _flip(me, mask))
                  for k, mask in enumerate(ALL_PEERS)]
        for cp in copies:
            cp.start()
        for cp in copies:
            cp.wait()
        total = all_ref[0]
        for d in range(1, 8):
            total = total + all_ref[d]
        sum_ref[...] = total

    return pl.pallas_call(
        body,
        in_specs=[VMEM_SPEC],
        out_specs=[VMEM_SPEC, VMEM_SPEC],
        out_shape=[jax.ShapeDtypeStruct((8, n_rows, LANES), F32), jax.ShapeDtypeStruct((n_rows, LANES), F32)],
        scratch_shapes=[pltpu.SemaphoreType.DMA((7,)), pltpu.SemaphoreType.DMA((7,))],
        name="gather_small",
    )(packed)


def _add_pair(a, b, out_dtype, name):
    def body(a_ref, b_ref, o_ref):
        o_ref[...] = (a_ref[...] + b_ref[...]).astype(o_ref.dtype)

    blk = pl.BlockSpec((1, *a.shape[1:]), lambda i: (i, 0, 0))
    return pl.pallas_call(
        body, grid=(a.shape[0],), in_specs=[blk, blk], out_specs=blk,
        out_shape=jax.ShapeDtypeStruct(a.shape, out_dtype),
        compiler_params=_params(("arbitrary",)), name=name,
    )(a, b)


def _add_slots(a, name):
    def body(a_ref, o_ref):
        total = a_ref[0].astype(F32)
        for k in range(1, 4):
            total = total + a_ref[k].astype(F32)
        o_ref[...] = total

    return pl.pallas_call(
        body, in_specs=[VMEM_SPEC], out_specs=VMEM_SPEC,
        out_shape=jax.ShapeDtypeStruct(a.shape[1:], F32),
        compiler_params=pltpu.CompilerParams(vmem_limit_bytes=VMEM_LIMIT), name=name,
    )(a)


def _ada_weight_grad(c_act, dmod_cols):
    def body(c_ref, d_ref, o_ref):
        o_ref[...] = _tn(c_ref[...], d_ref[...], HIGHEST)

    return pl.pallas_call(
        body, in_specs=[VMEM_SPEC, VMEM_SPEC], out_specs=VMEM_SPEC,
        out_shape=jax.ShapeDtypeStruct((c_act.shape[1], dmod_cols.shape[1]), F32),
        compiler_params=pltpu.CompilerParams(vmem_limit_bytes=VMEM_LIMIT), name="ada_weight_grad",
    )(c_act, dmod_cols)


def kernel(x, c, w_ada, b_ada, norm_attn_g, w_in, rel_bias, conv_w, a_log, dt_bias, delta_norm_g, w_out, norm_ffn_g, w_gate, w_up, w_down, final_norm_g, loss_target, m_w_ada, m_b_ada, m_norm_attn_g, m_w_in, m_rel_bias, m_conv_w, m_a_log, m_dt_bias, m_delta_norm_g, m_w_out, m_norm_ffn_g, m_w_gate, m_w_up, m_w_down, m_final_norm_g, v_w_ada, v_b_ada, v_norm_attn_g, v_w_in, v_rel_bias, v_conv_w, v_a_log, v_dt_bias, v_delta_norm_g, v_w_out, v_norm_ffn_g, v_w_gate, v_w_up, v_w_down, v_final_norm_g):
    xi, yi, ci = _me()
    dev = 4 * xi + 2 * yi + ci
    chip = 2 * xi + yi

    big_names = ("w_in", "w_out", "w_gate", "w_up", "w_down")
    by_cols = (True, False, True, True, False)

    def rows_form(a, cols):
        return jnp.swapaxes(a[0], 0, 1) if cols else a[0]

    def halves_form(w):
        rows, lanes = w.shape
        if (rows // 2) % 16:
            rows, lanes = w.size // LANES, LANES
        return (2, rows // 2, lanes)

    big = [rows_form(w, cols) for w, cols in zip((w_in, w_out, w_gate, w_up, w_down), by_cols)]
    shards = [w.astype(BF16).reshape(halves_form(w)) for w in big]

    def assemble(gathered, first):
        return [lax.dynamic_update_index_in_dim(g, s, chip, 0).reshape(4 * w.shape[0], w.shape[1])
                for g, s, w in zip(gathered, shards[first:], big[first:])]

    def reduce_pairs(grads, first, tag):
        slots = [g.reshape(4, *halves_form(w)) for g, w in zip(grads, big[first:])]
        return [_add_pair(lax.dynamic_index_in_dim(s, ci, 1, keepdims=False), got, BF16, f"add_pair_{tag}{a}")
                for a, (s, got) in enumerate(zip(slots, _swap_halves(slots)))]

    def finish(partials, scattered, first, tag):
        by_source = [lax.dynamic_update_index_in_dim(b, lax.dynamic_index_in_dim(p, chip, 0, keepdims=False), chip, 0)
                     for b, p in zip(scattered, partials)]
        halves = [_add_slots(p, f"add_slots_{tag}{a}") for a, p in enumerate(by_source)]
        joined = [lax.dynamic_update_index_in_dim(j, h, ci, 0) for j, h in zip(_join_halves(halves), halves)]
        return [j.reshape(w.shape) for j, w in zip(joined, big[first:])]

    conv_cols = conv_w.shape[2]
    mod_all, c_act, conv_all, gathered_in = _ada_exchange(
        jnp.broadcast_to(c, (8, D_MODEL)), w_ada[0], b_ada, jnp.pad(conv_w[0], ((0, 4), (0, 0))), shards[0])
    mod = lax.dynamic_slice_in_dim(mod_all, dev, 1, axis=0)
    conv_full = jnp.swapaxes(conv_all[:, :4, :], 0, 1).reshape(4, 4 * conv_cols)
    whole_in, = assemble([gathered_in], 0)
    loss, grad_x, grads, dmod, (partials, scattered) = _local_step(
        x[0], loss_target[0], mod, norm_attn_g, whole_in, rel_bias, conv_full, a_log, dt_bias, delta_norm_g,
        norm_ffn_g, final_norm_g[None], shards[1:], functools.partial(assemble, first=1), reduce_pairs)

    big_grads = finish(partials, scattered, 0, "all")

    pieces = [dmod, grads["conv_w"], grads["norm_attn_g"], grads["norm_ffn_g"], grads["final_norm_g"],
              grads["rel_bias"], grads["a_log"], grads["dt_bias"], grads["delta_norm_g"]]
    flat = [jnp.pad(p.reshape(-1), (0, -p.size % LANES)) for p in pieces]
    n_rows = [f.size // LANES for f in flat]
    packed = jnp.concatenate(flat).reshape(-1, LANES)
    packed = jnp.pad(packed, ((0, -packed.shape[0] % 8), (0, 0)))
    all_small, total = _gather_small(packed)
    sums, start = [], 0
    for p, n in zip(pieces, n_rows):
        sums.append(total[start:start + n].reshape(-1)[:p.size].reshape(p.shape))
        start += n
    g_b_ada, g_conv, g_norm_attn, g_norm_ffn, g_final, g_rel, g_alog, g_dt, g_dnorm = sums
    dmod_all = all_small[:, :n_rows[0], :].reshape(8, -1)
    ada_cols = w_ada.shape[2]
    g_w_ada = _ada_weight_grad(c_act, lax.dynamic_slice_in_dim(dmod_all, chip * ada_cols, ada_cols, axis=1))
    g_conv = lax.dynamic_slice_in_dim(g_conv, chip * conv_cols, conv_cols, axis=1)

    grad = {"w_ada": g_w_ada[None], "b_ada": g_b_ada, "norm_attn_g": g_norm_attn,
            "rel_bias": g_rel, "conv_w": g_conv[None], "a_log": g_alog, "dt_bias": g_dt, "delta_norm_g": g_dnorm,
            "norm_ffn_g": g_norm_ffn, "final_norm_g": g_final.reshape(-1)}
    weight = {"w_ada": w_ada, "b_ada": b_ada, "norm_attn_g": norm_attn_g, "w_in": w_in, "rel_bias": rel_bias,
              "conv_w": conv_w, "a_log": a_log, "dt_bias": dt_bias, "delta_norm_g": delta_norm_g, "w_out": w_out,
              "norm_ffn_g": norm_ffn_g, "w_gate": w_gate, "w_up": w_up, "w_down": w_down, "final_norm_g": final_norm_g}
    first = {"w_ada": m_w_ada, "b_ada": m_b_ada, "norm_attn_g": m_norm_attn_g, "w_in": m_w_in, "rel_bias": m_rel_bias,
             "conv_w": m_conv_w, "a_log": m_a_log, "dt_bias": m_dt_bias, "delta_norm_g": m_delta_norm_g,
             "w_out": m_w_out, "norm_ffn_g": m_norm_ffn_g, "w_gate": m_w_gate, "w_up": m_w_up, "w_down": m_w_down,
             "final_norm_g": m_final_norm_g}
    second = {"w_ada": v_w_ada, "b_ada": v_b_ada, "norm_attn_g": v_norm_attn_g, "w_in": v_w_in, "rel_bias": v_rel_bias,
              "conv_w": v_conv_w, "a_log": v_a_log, "dt_bias": v_dt_bias, "delta_norm_g": v_delta_norm_g,
              "w_out": v_w_out, "norm_ffn_g": v_norm_ffn_g, "w_gate": v_w_gate, "w_up": v_w_up, "w_down": v_w_down,
              "final_norm_g": v_final_norm_g}
    delta, new_m, new_v = {}, {}, {}
    for name, w in weight.items():
        if name in big_names:
            continue
        two_d = (-1, w.shape[-1])
        d, nm, nv = _adamw(w.reshape(two_d), grad[name].reshape(two_d), first[name].reshape(two_d),
                           second[name].reshape(two_d), f"adamw_{name}")
        delta[name], new_m[name], new_v[name] = d.reshape(w.shape), nm.reshape(w.shape), nv.reshape(w.shape)
    for name, w, g, cols in zip(big_names, big, big_grads, by_cols):
        outs = _adamw(w, g, rows_form(first[name], cols), rows_form(second[name], cols), f"adamw_{name}")
        grad[name], delta[name], new_m[name], new_v[name] = [
            (jnp.swapaxes(o, 0, 1) if cols else o)[None] for o in (g, *outs)]

    names = list(weight)
    return (lax.psum(loss, ("x", "y", "c")), grad_x[None], *[grad[n] for n in names], *[delta[n] for n in names],
            *[new_m[n] for n in names], *[new_v[n] for n in names])
```

```python
import functools
import math

import numpy as np
import jax
import jax.numpy as jnp
from jax import lax
from jax.experimental import pallas as pl
from jax.experimental.pallas import tpu as pltpu

F32 = jnp.float32
BF16 = jnp.bfloat16
HIGHEST = lax.Precision.HIGHEST

D_MODEL = 1024
HEAD_DIM = 64
N_HEADS = 8
HEAD_W = 512
BRANCHES = ((128, 1), (512, 4), (2048, 16))
BAND = 128
ATT_TILE = 2048
ATT_UNROLL = 8
ATT_UNROLL_BWD = 4
N_BUCKETS = 32
MAX_DISTANCE = 2048
CHUNK = 64
D_FF = 2816
EPS = 1e-6
NEG_INF = -1e30
LANES = 128
VMEM_LIMIT = 56 * 1024 * 1024

ADAM_LR = 0.001
ADAM_B1 = 0.9
ADAM_B2 = 0.999
ADAM_EPS = 1e-08
ADAM_WD = 0.01
ADAM_STEP = 10


def _nn(a, b, precision=None):
    return jnp.dot(a, b, preferred_element_type=F32, precision=precision)


def _nt(a, b, precision=None):
    return lax.dot_general(a, b, (((1,), (1,)), ((), ())), preferred_element_type=F32, precision=precision)


def _tn(a, b, precision=None):
    return lax.dot_general(a, b, (((0,), (0,)), ((), ())), preferred_element_type=F32, precision=precision)


def _params(sem, vmem=VMEM_LIMIT):
    return pltpu.CompilerParams(dimension_semantics=sem, vmem_limit_bytes=vmem)


def _sigmoid(x):
    return 0.5 * jnp.tanh(0.5 * x) + 0.5


def _silu_and_slope(x):
    s = _sigmoid(x)
    return x * s, s * (1.0 + x * (1.0 - s))


def _silu(x):
    return x * _sigmoid(x)


def _attn_tables():
    qi = np.arange(BAND)[:, None]
    kj = np.arange(2 * BAND)[None, :]
    steps = qi + BAND - kj
    in_window = (steps >= 0) & (steps <= BAND)
    max_exact = N_BUCKETS // 2
    out = np.zeros((3, 2, BAND, 2 * BAND), np.int32)
    for b, (_, dil) in enumerate(BRANCHES):
        dist = np.maximum(steps, 0) * dil
        dist_f = np.maximum(dist, 1).astype(np.float32)
        large = max_exact + (np.log(dist_f / np.float32(max_exact)) / np.float32(math.log(MAX_DISTANCE / max_exact))
                             * np.float32(N_BUCKETS - max_exact)).astype(np.int32)
        bucket = np.where(dist < max_exact, dist, np.minimum(large, N_BUCKETS - 1)).astype(np.int32)
        out[b, 0] = np.where(in_window, bucket, -1)
        out[b, 1] = np.where(in_window & (kj >= BAND), bucket, -1)
    return out


def _attention_bias(rel_bias, tables):
    def body(rel_ref, tab_ref, out_ref):
        head = pl.program_id(0)
        for b in range(3):
            tab = tab_ref[b, 0]

            def pick(kk, acc, tab=tab):
                return jnp.where(tab == kk, rel_ref[kk, head], acc)

            acc = lax.fori_loop(0, N_BUCKETS, pick, jnp.zeros((BAND, 2 * BAND), F32))
            for first in range(2):
                out_ref[0, b, first] = jnp.where(tab_ref[b, first] < 0, NEG_INF, acc)

    return pl.pallas_call(
        body,
        grid=(N_HEADS,),
        in_specs=[pl.BlockSpec(memory_space=pltpu.SMEM),
                  pl.BlockSpec((3, 2, BAND, 2 * BAND), lambda h: (0, 0, 0, 0))],
        out_specs=pl.BlockSpec((1, 3, 2, BAND, 2 * BAND), lambda h: (h, 0, 0, 0, 0)),
        out_shape=jax.ShapeDtypeStruct((N_HEADS, 3, 2, BAND, 2 * BAND), F32),
        compiler_params=_params(("arbitrary",)),
        name="attn_bias",
    )(rel_bias, tables)


def _bias_spec():
    return pl.BlockSpec((2, 3, 2, BAND, 2 * BAND), lambda p, t: (p, 0, 0, 0, 0))


def _attn_block_index(idx, t, r):
    nb = ATT_TILE // (BAND * r)
    rho = idx // nb
    n = idx % nb
    qs = rho + r * BAND * n
    gs = t * ATT_TILE + qs
    first = (t * nb + n) == 0
    ps = jnp.where(first, gs, gs - r * BAND)
    return qs, gs, ps, first.astype(jnp.int32)


def _rows(start, r):
    return pl.ds(start, BAND) if r == 1 else pl.ds(start, BAND, stride=r)


def _attention_fwd(qkv, bias, shards):
    seq = qkv.shape[0]
    n_tiles = seq // ATT_TILE
    n = len(shards)

    def body(*refs):
        bias_ref, q_ref, k_ref, v_ref = refs[:4]
        y_ref, lse_ref = refs[4 + n:6 + n]
        o_s, l_s = refs[6 + 2 * n:8 + 2 * n]
        riding = (refs[4:4 + n], refs[6 + n:6 + 2 * n], *refs[8 + 2 * n:])
        pair = pl.program_id(0)
        t = pl.program_id(1)
        if n:
            @pl.when((pair == 0) & (t == 0))
            def _():
                for cp in _gather_copies(*riding, hand_over=False)[0]:
                    cp.start()

            @pl.when((pair == 2) & (t == 0))
            def _():
                for cp, fwd in zip(*_gather_copies(*riding)):
                    cp.wait_recv()
                    fwd.start()

        lane = lax.broadcasted_iota(jnp.int32, (1, LANES), 1)
        head0 = lane < HEAD_DIM
        masks = (head0, jnp.logical_not(head0))
        ones = jnp.ones((2 * BAND, LANES), BF16)
        for b, (_, r) in enumerate(BRANCHES):
            def blocks(it, carry, b=b, r=r):
                idx = [_attn_block_index(it * ATT_UNROLL + j, t, r) for j in range(ATT_UNROLL)]
                qb = [q_ref[_rows(qs, r), :] * (HEAD_DIM ** -0.5) for qs, _, _, _ in idx]
                kcat = [jnp.concatenate([k_ref[_rows(ps, r), :], k_ref[_rows(gs, r), :]], axis=0).astype(BF16)
                        for _, gs, ps, _ in idx]
                vcat = [jnp.concatenate([v_ref[_rows(ps, r), :], v_ref[_rows(gs, r), :]], axis=0).astype(BF16)
                        for _, gs, ps, _ in idx]
                work = [(j, hh) for j in range(ATT_UNROLL) for hh in range(2)]
                s = [_nt(jnp.where(masks[hh], qb[j], 0.0).astype(BF16), kcat[j]) + bias_ref[hh, b, idx[j][3]]
                     for j, hh in work]
                m = [jnp.max(sv, axis=-1, keepdims=True) for sv in s]
                e = [jnp.exp(sv - mv) for sv, mv in zip(s, m)]
                eb = [ev.astype(BF16) for ev in e]
                den = [_nn(ev, ones) for ev in eb]
                out = [_nn(ev, vcat[j]) / dv for ev, dv, (j, _) in zip(eb, den, work)]
                lse = [mv + jnp.log(dv) for mv, dv in zip(m, den)]
                for j in range(ATT_UNROLL):
                    o_s[b, _rows(idx[j][0], r), :] = jnp.where(head0, out[2 * j], out[2 * j + 1])
                    l_s[b, _rows(idx[j][0], r), :] = jnp.where(head0, lse[2 * j], lse[2 * j + 1])
                return carry

            lax.fori_loop(0, ATT_TILE // BAND // ATT_UNROLL, blocks, 0)

        def merge(i, carry):
            rows = pl.ds(pl.multiple_of(i * BAND, BAND), BAND)
            l0, l1, l2 = l_s[0, rows, :], l_s[1, rows, :], l_s[2, rows, :]
            m = jnp.maximum(jnp.maximum(l0, l1), l2)
            w0, w1, w2 = jnp.exp(l0 - m), jnp.exp(l1 - m), jnp.exp(l2 - m)
            tot = w0 + w1 + w2
            y_ref[rows, :] = (w0 * o_s[0, rows, :] + w1 * o_s[1, rows, :] + w2 * o_s[2, rows, :]) / tot
            lse_ref[rows, :] = m + jnp.log(tot)
            return carry

        lax.fori_loop(0, ATT_TILE // BAND, merge, 0)

        if n:
            @pl.when((pair == N_HEADS // 2 - 1) & (t == n_tiles - 1))
            def _():
                first, passed = _gather_copies(*riding)
                for cp in first:
                    cp.wait_send()
                for fwd in passed:
                    fwd.wait()

    tile = pl.BlockSpec((ATT_TILE, LANES), lambda p, t: (t, p))
    sems = [pltpu.SemaphoreType.DMA((6 * n,)), pltpu.SemaphoreType.DMA((6 * n,))] if n else []
    return pl.pallas_call(
        body,
        grid=(N_HEADS // 2, n_tiles),
        in_specs=[
            _bias_spec(),
            pl.BlockSpec((ATT_TILE, LANES), lambda p, t: (t, p)),
            pl.BlockSpec((seq, LANES), lambda p, t: (0, 4 + p)),
            pl.BlockSpec((seq, LANES), lambda p, t: (0, 8 + p)),
        ] + [ANY] * n,
        out_specs=[tile, tile] + [ANY] * n,
        out_shape=[jax.ShapeDtypeStruct((seq, HEAD_W), F32), jax.ShapeDtypeStruct((seq, HEAD_W), F32)]
        + _gathered_shapes(shards),
        scratch_shapes=[
            pltpu.VMEM((3, ATT_TILE, LANES), F32),
            pltpu.VMEM((3, ATT_TILE, LANES), F32),
        ] + sems,
        compiler_params=_params(("arbitrary", "arbitrary")),
        name="attn_fwd",
    )(bias, qkv, qkv, qkv, *shards)


def _attention_bwd(qkv, dy, y, lse, bias, partials):
    seq = qkv.shape[0]
    n_tiles = seq // ATT_TILE
    n = len(partials)

    def body(*refs):
        bias_ref, q_ref, k_ref, v_ref, dy_ref, y_ref, lse_ref = refs[:7]
        dq_ref, dk_ref, dv_ref, dbias_ref = refs[7 + n:11 + n]
        riding = (refs[7:7 + n], refs[11 + n:11 + 2 * n], *refs[11 + 2 * n:])
        pair = pl.program_id(0)
        t = pl.program_id(1)
        if n:
            @pl.when((pair == 0) & (t == 0))
            def _():
                for cp in _scatter_copies(*riding):
                    cp.start()

        lane = lax.broadcasted_iota(jnp.int32, (1, LANES), 1)
        head0 = lane < HEAD_DIM

        @pl.when(t == 0)
        def _():
            dk_ref[...] = jnp.zeros_like(dk_ref)
            dv_ref[...] = jnp.zeros_like(dv_ref)
            dbias_ref[...] = jnp.zeros_like(dbias_ref)

        dq_ref[...] = jnp.zeros_like(dq_ref)

        masks = (head0, jnp.logical_not(head0))
        ones = jnp.ones((LANES, LANES), BF16)
        scale = HEAD_DIM ** -0.5
        for b, (_, r) in enumerate(BRANCHES):
            def blocks(it, carry, b=b, r=r):
                idx = [_attn_block_index(it * ATT_UNROLL_BWD + j, t, r) for j in range(ATT_UNROLL_BWD)]
                qb = [q_ref[_rows(qs, r), :] * scale for qs, _, _, _ in idx]
                kcat = [jnp.concatenate([k_ref[_rows(ps, r), :], k_ref[_rows(gs, r), :]], axis=0).astype(BF16)
                        for _, gs, ps, _ in idx]
                vcat = [jnp.concatenate([v_ref[_rows(ps, r), :], v_ref[_rows(gs, r), :]], axis=0).astype(BF16)
                        for _, gs, ps, _ in idx]
                dob = [dy_ref[_rows(qs, r), :] for qs, _, _, _ in idx]
                ob = [y_ref[_rows(qs, r), :] for qs, _, _, _ in idx]
                lb = [lse_ref[_rows(qs, r), :] for qs, _, _, _ in idx]
                work = [(j, hh) for j in range(ATT_UNROLL_BWD) for hh in range(2)]
                qh = [jnp.where(masks[hh], qb[j], 0.0).astype(BF16) for j, hh in work]
                doh = [jnp.where(masks[hh], dob[j], 0.0) for j, hh in work]
                dohb = [d.astype(BF16) for d in doh]
                s = [_nt(qh[w], kcat[j]) + bias_ref[hh, b, idx[j][3]] for w, (j, hh) in enumerate(work)]
                dp = [_nt(dohb[w], vcat[j]) for w, (j, _) in enumerate(work)]
                lrot = [pltpu.roll(lv, HEAD_DIM, 1) for lv in lb]
                lcol = [jnp.where(masks[hh], lb[j], lrot[j]) for j, hh in work]
                parts = [_split(doh[w] * ob[j]) for w, (j, _) in enumerate(work)]
                delta = [_nn(hi, ones) + _nn(lo, ones) for hi, lo in parts]
                prob = [jnp.exp(sv - jnp.concatenate([lv, lv], axis=1)) for sv, lv in zip(s, lcol)]
                ds = [pv * (dv - jnp.concatenate([de, de], axis=1)) for pv, dv, de in zip(prob, dp, delta)]
                dsb = [d.astype(BF16) for d in ds]
                dq = [_nn(dsb[w], kcat[j]) for w, (j, _) in enumerate(work)]
                dkc = [_tn(dsb[w], qh[w]) for w in range(len(work))]
                dvc = [_tn(prob[w].astype(BF16), dohb[w]) for w in range(len(work))]
                for hh in range(2):
                    dbias_ref[0, b, hh] += sum(ds[w] for w, (_, head) in enumerate(work) if head == hh)
                for j in range(ATT_UNROLL_BWD):
                    qs, gs, ps, _ = idx[j]
                    dkcat = dkc[2 * j] + dkc[2 * j + 1]
                    dvcat = dvc[2 * j] + dvc[2 * j + 1]
                    dq_ref[_rows(qs, r), :] += jnp.where(head0, dq[2 * j], dq[2 * j + 1]) * scale
                    dk_ref[_rows(ps, r), :] += dkcat[:BAND]
                    dk_ref[_rows(gs, r), :] += dkcat[BAND:]
                    dv_ref[_rows(ps, r), :] += dvcat[:BAND]
                    dv_ref[_rows(gs, r), :] += dvcat[BAND:]
                return carry

            lax.fori_loop(0, ATT_TILE // BAND // ATT_UNROLL_BWD, blocks, 0)

        if n:
            @pl.when((pair == N_HEADS // 2 - 1) & (t == n_tiles - 1))
            def _():
                for cp in _scatter_copies(*riding):
                    cp.wait()

    tile = pl.BlockSpec((ATT_TILE, LANES), lambda p, t: (t, p))
    full = pl.BlockSpec((seq, LANES), lambda p, t: (0, p))
    sems = [pltpu.SemaphoreType.DMA((3 * n,)), pltpu.SemaphoreType.DMA((3 * n,))] if n else []
    return pl.pallas_call(
        body,
        grid=(N_HEADS // 2, n_tiles),
        in_specs=[
            _bias_spec(),
            pl.BlockSpec((ATT_TILE, LANES), lambda p, t: (t, p)),
            pl.BlockSpec((seq, LANES), lambda p, t: (0, 4 + p)),
            pl.BlockSpec((seq, LANES), lambda p, t: (0, 8 + p)),
            tile, tile, tile,
        ] + [ANY] * n,
        out_specs=[tile, full, full,
                   pl.BlockSpec((1, 3, 2, BAND, 2 * BAND), lambda p, t: (p, 0, 0, 0, 0))] + [ANY] * n,
        out_shape=[jax.ShapeDtypeStruct((seq, HEAD_W), F32)] * 3
        + [jax.ShapeDtypeStruct((N_HEADS // 2, 3, 2, BAND, 2 * BAND), F32)]
        + [jax.ShapeDtypeStruct(p.shape, p.dtype) for p in partials],
        scratch_shapes=sems,
        compiler_params=_params(("arbitrary", "arbitrary")),
        name="attn_bwd",
    )(bias, qkv, qkv, qkv, dy, y, lse, *partials)


def _rel_bias_grad(dbias, tables):
    def body(tab_ref, db_ref, out_ref):
        lane = lax.broadcasted_iota(jnp.int32, (1, LANES), 1)
        out_ref[...] = jnp.zeros_like(out_ref)
        for b in range(3):
            tab = tab_ref[b, 0]

            def head(h, carry, b=b, tab=tab):
                d = db_ref[h // 2, b, h % 2]
                sums = [jnp.sum(jnp.where(tab == kk, d, 0.0), keepdims=True) for kk in range(N_BUCKETS)]
                row = jnp.zeros((1, LANES), F32)
                for kk, s in enumerate(sums):
                    row = row + jnp.where(lane == kk, s, 0.0)
                out_ref[pl.ds(h, 1), :] += row
                return carry

            lax.fori_loop(0, N_HEADS, head, 0)

    return pl.pallas_call(
        body,
        out_shape=jax.ShapeDtypeStruct((N_HEADS, LANES), F32),
        compiler_params=pltpu.CompilerParams(vmem_limit_bytes=VMEM_LIMIT),
        name="rel_bias_grad",
    )(tables, dbias)


ROW_TILE = 512


def _head_sum_matrix():
    return (lax.broadcasted_iota(jnp.int32, (HEAD_W, HEAD_W), 0) // HEAD_DIM
            == lax.broadcasted_iota(jnp.int32, (HEAD_W, HEAD_W), 1) // HEAD_DIM).astype(F32)


def _head_spread_matrix(offset=0):
    return (lax.broadcasted_iota(jnp.int32, (LANES, HEAD_W), 0)
            == lax.broadcasted_iota(jnp.int32, (LANES, HEAD_W), 1) // HEAD_DIM + offset).astype(F32)


def _head_gather_matrix(offset=0):
    return (lax.broadcasted_iota(jnp.int32, (HEAD_W, LANES), 0) // HEAD_DIM + offset
            == lax.broadcasted_iota(jnp.int32, (HEAD_W, LANES), 1)).astype(F32)


def _split3(x):
    hi = x.astype(BF16)
    rest = x - hi.astype(F32)
    mid = rest.astype(BF16)
    return hi, mid, (rest - mid.astype(F32)).astype(BF16)


def _pick(x, onehot):
    m = onehot.astype(BF16)
    hi, mid, lo = _split3(x)
    return _nn(hi, m) + (_nn(mid, m) + _nn(lo, m))


def _pick_left(onehot, x):
    m = onehot.astype(BF16)
    hi, mid, lo = _split3(x)
    return _nn(m, hi) + (_nn(m, mid) + _nn(m, lo))


def _tri(lower, strict=False):
    r = lax.broadcasted_iota(jnp.int32, (CHUNK, CHUNK), 0)
    c = lax.broadcasted_iota(jnp.int32, (CHUNK, CHUNK), 1)
    if lower:
        return (c < r) if strict else (c <= r)
    return c >= r


def _softplus(z):
    return jnp.maximum(z, 0.0) + jnp.log(1.0 + jnp.exp(-jnp.abs(z)))


def _conv_taps(stage, w_ref, rows):
    return (w_ref[3:4, :] * stage[8:8 + rows, :] + w_ref[2:3, :] * stage[7:7 + rows, :]
            + w_ref[1:2, :] * stage[6:6 + rows, :] + w_ref[0:1, :] * stage[5:5 + rows, :])


def _l2_scale(xc, hsum):
    return lax.rsqrt(_pick(xc * xc, hsum) + EPS)


def _stage_rows(stage, x_ref, xp_ref, i):
    stage[0:8, :] = jnp.where(i == 0, 0.0, xp_ref[...])
    stage[8:8 + ROW_TILE, :] = x_ref[...]


def _delta_prep_fwd(qkvz, ba, conv_w, alog_row, dt_row):
    seq = qkvz.shape[0]
    qkv_w = 3 * HEAD_W

    def body(x_ref, xp_ref, ba_ref, w_ref, al_ref, dt_ref, out_ref, stage):
        i = pl.program_id(0)
        _stage_rows(stage, x_ref, xp_ref, i)
        act = _silu(_conv_taps(stage, w_ref, ROW_TILE))
        hsum, hspread = _head_sum_matrix(), _head_spread_matrix()
        qc, kc = act[:, :HEAD_W], act[:, HEAD_W:2 * HEAD_W]
        out_ref[0] = qc * _l2_scale(qc, hsum) * (HEAD_DIM ** -0.5)
        out_ref[1] = kc * _l2_scale(kc, hsum)
        out_ref[2] = act[:, 2 * HEAD_W:]
        bav = ba_ref[...]
        out_ref[3] = _pick(_sigmoid(bav), hspread)
        g8 = -jnp.exp(al_ref[...]) * _softplus(bav + dt_ref[...])
        gb = _pick(g8, _head_spread_matrix(N_HEADS))
        cum = _tri(True).astype(F32)
        for ch in range(ROW_TILE // CHUNK):
            rows = slice(ch * CHUNK, (ch + 1) * CHUNK)
            out_ref[4, rows, :] = _pick_left(cum, gb[rows])

    return pl.pallas_call(
        body,
        grid=(seq // ROW_TILE,),
        in_specs=[
            pl.BlockSpec((ROW_TILE, qkv_w), lambda i: (i, 0)),
            pl.BlockSpec((8, qkv_w), lambda i: (jnp.maximum(i * (ROW_TILE // 8) - 1, 0), 0)),
            pl.BlockSpec((ROW_TILE, LANES), lambda i: (i, 0)),
            pl.BlockSpec((4, qkv_w), lambda i: (0, 0)),
            pl.BlockSpec((1, LANES), lambda i: (0, 0)),
            pl.BlockSpec((1, LANES), lambda i: (0, 0)),
        ],
        out_specs=pl.BlockSpec((5, ROW_TILE, HEAD_W), lambda i: (0, i, 0)),
        out_shape=jax.ShapeDtypeStruct((5, seq, HEAD_W), F32),
        scratch_shapes=[pltpu.VMEM((ROW_TILE + 8, qkv_w), F32)],
        compiler_params=_params(("arbitrary",)),
        name="delta_prep_fwd",
    )(qkvz, qkvz, ba, conv_w, alog_row, dt_row)


def _split(x):
    hi = x.astype(BF16)
    return hi, (x - hi.astype(F32)).astype(BF16)


def _dot3(a, b, dot=_nn):
    return dot(a[0], b[0]) + (dot(a[0], b[1]) + dot(a[1], b[0]))


def _unit_lower_inverses(mats):
    eye = (lax.broadcasted_iota(jnp.int32, (CHUNK, CHUNK), 0)
           == lax.broadcasted_iota(jnp.int32, (CHUNK, CHUNK), 1)).astype(F32)
    invs = [eye - a for a in mats]
    powers = [_split(a) for a in mats]
    for step in range(5):
        squares = [_dot3(p, p) for p in powers]
        powers = [_split(s) for s in squares]
        invs = [inv + _dot3(_split(inv), p) for inv, p in zip(invs, powers)]
    return invs


def _chunk_terms(q, k, v, beta, gc):
    causal, strict = _tri(True), _tri(True, strict=True)
    e = jnp.exp(gc)
    g_last = jnp.broadcast_to(gc[CHUNK - 1:CHUNK, :], (CHUNK, CHUNK))
    f = jnp.exp(g_last - gc)
    e_last = jnp.exp(g_last)
    decay = jnp.where(causal, jnp.exp(jnp.where(causal, gc - gc.T, 0.0)), 0.0)
    kb = k * beta
    a_mat = jnp.where(strict, _nt(kb.astype(BF16), k.astype(BF16)) * decay, 0.0)
    qk = jnp.where(causal, _nt(q.astype(BF16), k.astype(BF16)) * decay, 0.0)
    return e, f, e_last, decay, kb, a_mat, qk


GROUP = 8
UNROLL = 8


def _chunk_rows(ci):
    return pl.ds(pl.multiple_of(ci * CHUNK, CHUNK), CHUNK)


def _pair_specs(n_planes):
    return pl.BlockSpec((n_planes, GROUP * CHUNK, LANES), lambda p, g: (0, g, p))


def _delta_chunk_fwd(xs):
    seq = xs.shape[1]
    rows_per_step = GROUP * CHUNK

    def body(x_ref, inv_ref, qk_ref, u_ref, w_ref):
        for hh in range(2):
            lanes = slice(hh * HEAD_DIM, (hh + 1) * HEAD_DIM)
            rows = [slice(step * CHUNK, (step + 1) * CHUNK) for step in range(GROUP)]
            xh = [[x_ref[j, r, lanes] for j in range(5)] for r in rows]
            terms = [_chunk_terms(*x) for x in xh]
            invs = _unit_lower_inverses([t[5] for t in terms])
            for r, x, t, inv in zip(rows, xh, terms, invs):
                e, kb, qk = t[0], t[4], t[6]
                inv_parts = _split(inv)
                inv_ref[hh, r, :] = inv
                qk_ref[hh, r, :] = qk
                u_ref[hh, r, :] = _dot3(inv_parts, _split(x[2] * x[3]))
                w_ref[hh, r, :] = _dot3(inv_parts, _split(kb * e))

    out = pl.BlockSpec((2, rows_per_step, HEAD_DIM), lambda p, g: (p, g, 0))
    return pl.pallas_call(
        body,
        grid=(N_HEADS // 2, seq // rows_per_step),
        in_specs=[_pair_specs(5)],
        out_specs=[out] * 4,
        out_shape=[jax.ShapeDtypeStruct((N_HEADS, seq, HEAD_DIM), F32)] * 4,
        compiler_params=_params(("parallel", "parallel")),
        name="delta_chunk_fwd",
    )(xs)


def _decays(gc):
    g_last = jnp.broadcast_to(gc[CHUNK - 1:CHUNK, :], (CHUNK, CHUNK))
    return jnp.exp(gc), jnp.exp(g_last - gc), jnp.exp(g_last)


def _token_blocks(index, n_steps=None):
    rows_per_step = GROUP * CHUNK
    if n_steps is None:
        return pl.BlockSpec((1, rows_per_step, HEAD_W), lambda g: (index, g, 0))
    return pl.BlockSpec((1, rows_per_step, HEAD_W), lambda g: (index, n_steps - 1 - g, 0))


def _head_lanes(h):
    return pl.ds(h * HEAD_DIM, HEAD_DIM)


def _delta_scan_fwd(xs, qk_h, u_h, w_h):
    seq = xs.shape[1]
    rows_per_step = GROUP * CHUNK

    def body(q_ref, k_ref, gc_ref, qk_ref, u_ref, w_ref, o_ref, st_ref, state):
        @pl.when(pl.program_id(0) == 0)
        def _():
            state[...] = jnp.zeros_like(state)

        def chunk(ci, carry):
            rows = _chunk_rows(ci)
            heads = range(N_HEADS)
            dec = [_decays(gc_ref[0, rows, _head_lanes(h)]) for h in heads]
            s = [state[h] for h in heads]
            sb = [s[h].astype(BF16) for h in heads]
            vnb = [(u_ref[h, rows, :] - _nn(w_ref[h, rows, :].astype(BF16), sb[h])).astype(BF16) for h in heads]
            for h in heads:
                o_ref[rows, _head_lanes(h)] = (_nn((q_ref[0, rows, _head_lanes(h)] * dec[h][0]).astype(BF16), sb[h])
                                               + _nn(qk_ref[h, rows, :].astype(BF16), vnb[h]))
                st_ref[h, rows, :] = s[h]
            for h in heads:
                state[h] = s[h] * dec[h][2] + _tn((k_ref[0, rows, _head_lanes(h)] * dec[h][1]).astype(BF16), vnb[h])
            return carry

        lax.fori_loop(0, GROUP, chunk, 0)

    blk = pl.BlockSpec((N_HEADS, rows_per_step, HEAD_DIM), lambda g: (0, g, 0))
    return pl.pallas_call(
        body,
        grid=(seq // rows_per_step,),
        in_specs=[_token_blocks(0), _token_blocks(1), _token_blocks(4), blk, blk, blk],
        out_specs=[pl.BlockSpec((rows_per_step, HEAD_W), lambda g: (g, 0)), blk],
        out_shape=[jax.ShapeDtypeStruct((seq, HEAD_W), F32), jax.ShapeDtypeStruct((N_HEADS, seq, HEAD_DIM), F32)],
        scratch_shapes=[pltpu.VMEM((N_HEADS, CHUNK, CHUNK), F32)],
        compiler_params=_params(("arbitrary",)),
        name="delta_scan_fwd",
    )(xs, xs, xs, qk_h, u_h, w_h)


def _delta_scan_bwd(xs, qk_h, w_h, do):
    seq = xs.shape[1]
    rows_per_step = GROUP * CHUNK
    n_steps = seq // rows_per_step

    def body(q_ref, k_ref, gc_ref, qk_ref, w_ref, do_ref, dsn_ref, dvn_ref, dstate):
        @pl.when(pl.program_id(0) == 0)
        def _():
            dstate[...] = jnp.zeros_like(dstate)

        def chunk(step, carry):
            rows = _chunk_rows(GROUP - 1 - step)
            heads = range(N_HEADS)
            dec = [_decays(gc_ref[0, rows, _head_lanes(h)]) for h in heads]
            ds_next = [dstate[h] for h in heads]
            dob = [do_ref[rows, _head_lanes(h)].astype(BF16) for h in heads]
            dv_new = [_tn(qk_ref[h, rows, :].astype(BF16), dob[h])
                      + _nn((k_ref[0, rows, _head_lanes(h)] * dec[h][1]).astype(BF16), ds_next[h].astype(BF16))
                      for h in heads]
            for h in heads:
                dsn_ref[h, rows, :] = ds_next[h]
                dvn_ref[h, rows, :] = dv_new[h]
            for h in heads:
                dstate[h] = (_tn((q_ref[0, rows, _head_lanes(h)] * dec[h][0]).astype(BF16), dob[h])
                             + dec[h][2] * ds_next[h] - _tn(w_ref[h, rows, :].astype(BF16), dv_new[h].astype(BF16)))
            return carry

        lax.fori_loop(0, GROUP, chunk, 0)

    blk = pl.BlockSpec((N_HEADS, rows_per_step, HEAD_DIM), lambda g: (0, n_steps - 1 - g, 0))
    return pl.pallas_call(
        body,
        grid=(n_steps,),
        in_specs=[_token_blocks(0, n_steps), _token_blocks(1, n_steps), _token_blocks(4, n_steps), blk, blk,
                  pl.BlockSpec((rows_per_step, HEAD_W), lambda g: (n_steps - 1 - g, 0))],
        out_specs=[blk, blk],
        out_shape=[jax.ShapeDtypeStruct((N_HEADS, seq, HEAD_DIM), F32)] * 2,
        scratch_shapes=[pltpu.VMEM((N_HEADS, CHUNK, CHUNK), F32)],
        compiler_params=_params(("arbitrary",)),
        name="delta_scan_bwd",
    )(xs, xs, xs, qk_h, w_h, do)


def _delta_chunk_bwd(xs, inv_h, u_h, w_h, st_h, dsn_h, dvn_h, do):
    seq = xs.shape[1]
    rows_per_step = GROUP * CHUNK

    def body(x_ref, inv_ref, u_ref, w_ref, st_ref, dsn_ref, dvn_ref, do_ref, dx_ref):
        causal, strict = _tri(True), _tri(True, strict=True)
        last_row = lax.broadcasted_iota(jnp.int32, (CHUNK, CHUNK), 0) == CHUNK - 1

        def bf(vals):
            return [val.astype(BF16) for val in vals]

        def group(hh, first):
            lanes = slice(hh * HEAD_DIM, (hh + 1) * HEAD_DIM)
            rows = [slice(step * CHUNK, (step + 1) * CHUNK) for step in range(first, first + UNROLL)]
            n = range(UNROLL)
            q, k, v, beta, gc = [[x_ref[j, r, lanes] for r in rows] for j in range(5)]
            terms = [_chunk_terms(q[i], k[i], v[i], beta[i], gc[i]) for i in n]
            e, f, e_last, decay, kb, a_mat, qk = [[t[j] for t in terms] for j in range(7)]
            inv = [_split(inv_ref[hh, r, :]) for r in rows]
            u = [u_ref[hh, r, :] for r in rows]
            w = [w_ref[hh, r, :] for r in rows]
            s = [st_ref[hh, r, :] for r in rows]
            ds_next = [dsn_ref[hh, r, :] for r in rows]
            dv_new = [dvn_ref[hh, r, :] for r in rows]
            sb, dsb, dvb, wb = bf(s), bf(ds_next), bf(dv_new), bf(w)
            dob = bf([do_ref[r, lanes] for r in rows])
            qbf, kbf, kbb = bf(q), bf(k), bf(kb)
            vnb = bf([u[i] - _nn(wb[i], sb[i]) for i in n])
            dqe = [_nt(dob[i], sb[i]) for i in n]
            dw = [-_nt(dvb[i], sb[i]) for i in n]
            dkf = [_nt(vnb[i], dsb[i]) for i in n]
            dqk = [jnp.where(causal, _nt(dob[i], vnb[i]), 0.0) for i in n]
            drhs_u = [_dot3(inv[i], _split(dv_new[i]), _tn) for i in n]
            drhs_w = [_dot3(inv[i], _split(dw[i]), _tn) for i in n]
            da = [-jnp.where(strict, _nt(drhs_u[i].astype(BF16), u[i].astype(BF16))
                             + _nt(drhs_w[i].astype(BF16), wb[i]), 0.0) for i in n]
            dad = bf([da[i] * decay[i] for i in n])
            dqd = bf([dqk[i] * decay[i] for i in n])
            dkb = [e[i] * drhs_w[i] + _nn(dad[i], kbf[i]) for i in n]
            dk = [_tn(dad[i], kbb[i]) + _tn(dqd[i], qbf[i]) + f[i] * dkf[i] + beta[i] * dkb[i] for i in n]
            dq = [_nn(dqd[i], kbf[i]) + e[i] * dqe[i] for i in n]
            for i in n:
                de_full = kb[i] * drhs_w[i] + q[i] * dqe[i]
                df_full = k[i] * dkf[i]
                m = da[i] * a_mat[i] + dqk[i] * qk[i]
                dgc = de_full * e[i] - df_full * f[i] + m - m.T
                tail = jnp.sum(df_full * f[i] + s[i] * ds_next[i] * e_last[i], axis=0, keepdims=True)
                dgc = dgc + jnp.where(last_row, jnp.broadcast_to(tail, (CHUNK, CHUNK)), 0.0)
                dx_ref[0, rows[i], lanes] = dq[i]
                dx_ref[1, rows[i], lanes] = dk[i]
                dx_ref[2, rows[i], lanes] = beta[i] * drhs_u[i]
                dx_ref[3, rows[i], lanes] = v[i] * drhs_u[i] + k[i] * dkb[i]
                dx_ref[4, rows[i], lanes] = dgc

        for hh in range(2):
            for first in range(0, GROUP, UNROLL):
                group(hh, first)

    blk = pl.BlockSpec((2, rows_per_step, HEAD_DIM), lambda p, g: (p, g, 0))
    return pl.pallas_call(
        body,
        grid=(N_HEADS // 2, seq // rows_per_step),
        in_specs=[_pair_specs(5)] + [blk] * 6 + [pl.BlockSpec((rows_per_step, LANES), lambda p, g: (g, p))],
        out_specs=_pair_specs(5),
        out_shape=jax.ShapeDtypeStruct((5, seq, HEAD_W), F32),
        compiler_params=_params(("parallel", "parallel")),
        name="delta_chunk_bwd",
    )(xs, inv_h, u_h, w_h, st_h, dsn_h, dvn_h, do)


def _delta_post_fwd(o, qkvz, gain_row):
    seq = o.shape[0]

    def body(o_ref, z_ref, g_ref, y_ref):
        ov = o_ref[...]
        rb = lax.rsqrt(_pick(ov * ov, _head_sum_matrix()) * (1.0 / HEAD_DIM) + EPS)
        y_ref[...] = (ov * rb * g_ref[...] * _silu(z_ref[...])).astype(y_ref.dtype)

    tile = pl.BlockSpec((ROW_TILE, HEAD_W), lambda i: (i, 0))
    return pl.pallas_call(
        body,
        grid=(seq // ROW_TILE,),
        in_specs=[tile, pl.BlockSpec((ROW_TILE, HEAD_W), lambda i: (i, 3)), pl.BlockSpec((1, HEAD_W), lambda i: (0, 0))],
        out_specs=tile,
        out_shape=jax.ShapeDtypeStruct((seq, HEAD_W), BF16),
        compiler_params=_params(("arbitrary",)),
        name="delta_post_fwd",
    )(o, qkvz, gain_row)


def _delta_post_bwd(dy, o, qkvz, gain_row):
    seq = o.shape[0]

    def body(dy_ref, o_ref, z_ref, g_ref, do_ref, dz_ref, dg_ref):
        @pl.when(pl.program_id(0) == 0)
        def _():
            dg_ref[...] = jnp.zeros_like(dg_ref)

        ov, zv, dyv, gain = o_ref[...], z_ref[...], dy_ref[...], g_ref[...]
        hsum = _head_sum_matrix()
        rb = lax.rsqrt(_pick(ov * ov, hsum) * (1.0 / HEAD_DIM) + EPS)
        ohat = ov * rb
        silu_z, slope_z = _silu_and_slope(zv)
        dz_ref[...] = dyv * ohat * gain * slope_z
        dn = dyv * silu_z
        dg_ref[0:1, :] += jnp.sum(dn * ohat, axis=0, keepdims=True)
        dohat = dn * gain

        @pl.when(pl.program_id(0) == pl.num_programs(0) - 1)
        def _():
            fold = (lax.broadcasted_iota(jnp.int32, (HEAD_W, HEAD_W), 0) % HEAD_DIM
                    == lax.broadcasted_iota(jnp.int32, (HEAD_W, HEAD_W), 1)).astype(F32)
            dg_ref[1:2, :] = _pick(dg_ref[0:1, :], fold)

        proj = _pick(dohat * ohat, hsum) * (1.0 / HEAD_DIM)
        do_ref[...] = rb * (dohat - ohat * proj)

    tile = pl.BlockSpec((ROW_TILE, HEAD_W), lambda i: (i, 0))
    return pl.pallas_call(
        body,
        grid=(seq // ROW_TILE,),
        in_specs=[pl.BlockSpec((ROW_TILE, HEAD_W), lambda i: (i, 1)), tile,
                  pl.BlockSpec((ROW_TILE, HEAD_W), lambda i: (i, 3)), pl.BlockSpec((1, HEAD_W), lambda i: (0, 0))],
        out_specs=[tile, tile, pl.BlockSpec((2, HEAD_W), lambda i: (0, 0))],
        out_shape=[jax.ShapeDtypeStruct((seq, HEAD_W), F32), jax.ShapeDtypeStruct((seq, HEAD_W), F32),
                   jax.ShapeDtypeStruct((2, HEAD_W), F32)],
        compiler_params=_params(("arbitrary",)),
        name="delta_post_bwd",
    )(dy, o, qkvz, gain_row)


def _delta_prep_bwd(qkvz, ba, conv_w, alog_row, dt_row, dxs):
    seq = qkvz.shape[0]
    qkv_w = 3 * HEAD_W

    def body(x_ref, xp_ref, ba_ref, w_ref, al_ref, dt_ref, dx_ref, dconv_ref, dba_ref, dvec_ref, stage):
        i = pl.program_id(0)

        @pl.when(i == 0)
        def _():
            dvec_ref[...] = jnp.zeros_like(dvec_ref)

        _stage_rows(stage, x_ref, xp_ref, i)
        pre = _conv_taps(stage, w_ref, ROW_TILE)
        act, slope = _silu_and_slope(pre)
        hsum = _head_sum_matrix()
        for j, scale in ((0, HEAD_DIM ** -0.5), (1, 1.0)):
            cols = slice(j * HEAD_W, (j + 1) * HEAD_W)
            xc = act[:, cols]
            rb = _l2_scale(xc, hsum)
            xhat = xc * rb
            dhat = dx_ref[j] * scale
            proj = _pick(dhat * xhat, hsum)
            dconv_ref[:, cols] = rb * (dhat - xhat * proj) * slope[:, cols]
        dconv_ref[:, 2 * HEAD_W:] = dx_ref[2] * slope[:, 2 * HEAD_W:]

        bav = ba_ref[...]
        beta8 = _sigmoid(bav)
        dbeta8 = _pick(dx_ref[3], _head_gather_matrix())
        dgc8 = _pick(dx_ref[4], _head_gather_matrix(N_HEADS))
        rev = _tri(False).astype(F32)
        z = bav + dt_ref[...]
        ea = jnp.exp(al_ref[...])
        g8 = -ea * _softplus(z)
        sig = _sigmoid(z)
        d_alog = jnp.zeros((1, LANES), F32)
        d_dt = jnp.zeros((1, LANES), F32)
        for ch in range(ROW_TILE // CHUNK):
            rows = slice(ch * CHUNK, (ch + 1) * CHUNK)
            dg8 = _pick_left(rev, dgc8[rows])
            da = -dg8 * ea * sig[rows]
            dba_ref[rows, :] = dbeta8[rows] * beta8[rows] * (1.0 - beta8[rows]) + da
            d_alog = d_alog + jnp.sum(dg8 * g8[rows], axis=0, keepdims=True)
            d_dt = d_dt + jnp.sum(da, axis=0, keepdims=True)
        dvec_ref[0:1, :] += d_alog
        dvec_ref[1:2, :] += d_dt

    return pl.pallas_call(
        body,
        grid=(seq // ROW_TILE,),
        in_specs=[
            pl.BlockSpec((ROW_TILE, qkv_w), lambda i: (i, 0)),
            pl.BlockSpec((8, qkv_w), lambda i: (jnp.maximum(i * (ROW_TILE // 8) - 1, 0), 0)),
            pl.BlockSpec((ROW_TILE, LANES), lambda i: (i, 0)),
            pl.BlockSpec((4, qkv_w), lambda i: (0, 0)),
            pl.BlockSpec((1, LANES), lambda i: (0, 0)),
            pl.BlockSpec((1, LANES), lambda i: (0, 0)),
            pl.BlockSpec((5, ROW_TILE, HEAD_W), lambda i: (0, i, 0)),
        ],
        out_specs=[pl.BlockSpec((ROW_TILE, qkv_w), lambda i: (i, 0)),
                   pl.BlockSpec((ROW_TILE, LANES), lambda i: (i, 0)),
                   pl.BlockSpec((2, LANES), lambda i: (0, 0))],
        out_shape=[jax.ShapeDtypeStruct((seq, qkv_w), F32), jax.ShapeDtypeStruct((seq, LANES), F32),
                   jax.ShapeDtypeStruct((2, LANES), F32)],
        scratch_shapes=[pltpu.VMEM((ROW_TILE + 8, qkv_w), F32)],
        compiler_params=_params(("arbitrary",)),
        name="delta_prep_bwd",
    )(qkvz, qkvz, ba, conv_w, alog_row, dt_row, dxs)


def _conv_bwd(dconv, qkvz, conv_w):
    seq = dconv.shape[0]
    qkv_w = 3 * HEAD_W
    n_tiles = seq // ROW_TILE

    def body(dy_ref, dyn_ref, x_ref, xp_ref, w_ref, dx_ref, dw_ref, stage, dstage):
        i = pl.program_id(0)

        @pl.when(i == 0)
        def _():
            dw_ref[...] = jnp.zeros_like(dw_ref)

        _stage_rows(stage, x_ref, xp_ref, i)
        dstage[0:ROW_TILE, :] = dy_ref[...]
        dstage[ROW_TILE:ROW_TILE + 8, :] = jnp.where(i == n_tiles - 1, 0.0, dyn_ref[...])
        dy = dy_ref[...]
        dx_ref[...] = (w_ref[3:4, :] * dy + w_ref[2:3, :] * dstage[1:1 + ROW_TILE, :]
                       + w_ref[1:2, :] * dstage[2:2 + ROW_TILE, :] + w_ref[0:1, :] * dstage[3:3 + ROW_TILE, :])
        for j in range(4):
            dw_ref[j:j + 1, :] += jnp.sum(dy * stage[5 + j:5 + j + ROW_TILE, :], axis=0, keepdims=True)

    tile = pl.BlockSpec((ROW_TILE, qkv_w), lambda i: (i, 0))
    return pl.pallas_call(
        body,
        grid=(n_tiles,),
        in_specs=[
            tile,
            pl.BlockSpec((8, qkv_w), lambda i: (jnp.minimum((i + 1) * (ROW_TILE // 8), seq // 8 - 1), 0)),
            tile,
            pl.BlockSpec((8, qkv_w), lambda i: (jnp.maximum(i * (ROW_TILE // 8) - 1, 0), 0)),
            pl.BlockSpec((4, qkv_w), lambda i: (0, 0)),
        ],
        out_specs=[tile, pl.BlockSpec((4, qkv_w), lambda i: (0, 0))],
        out_shape=[jax.ShapeDtypeStruct((seq, qkv_w), F32), jax.ShapeDtypeStruct((4, qkv_w), F32)],
        scratch_shapes=[pltpu.VMEM((ROW_TILE + 8, qkv_w), F32), pltpu.VMEM((ROW_TILE + 8, qkv_w), F32)],
        compiler_params=_params(("arbitrary",)),
        name="conv_bwd",
    )(dconv, dconv, qkvz, qkvz, conv_w)


FF_TILE = 1408
WGRAD_ROWS = 1024


def _row(a):
    return pl.BlockSpec((1, a), lambda *_: (0, 0))


def _rms_fwd(xv, gain):
    rstd = lax.rsqrt(jnp.mean(xv * xv, axis=-1, keepdims=True) + EPS)
    xhat = xv * rstd
    return xhat, rstd, xhat * gain


def _rms_bwd(dnorm, xhat, rstd, gain):
    dxhat = dnorm * gain
    dx = rstd * (dxhat - xhat * jnp.mean(dxhat * xhat, axis=-1, keepdims=True))
    return dx, jnp.sum(dnorm * xhat, axis=0, keepdims=True)


def _inproj_fwd(x, gain, scale, shift, w_a, w_d, w_ba):
    seq = x.shape[0]

    def body(x_ref, g_ref, sc_ref, sh_ref, wa_ref, wd_ref, wb_ref, h_ref, a_ref, d_ref, b_ref):
        _, _, norm = _rms_fwd(x_ref[...], g_ref[...])
        h = (norm * (1.0 + sc_ref[...]) + sh_ref[...]).astype(BF16)
        h_ref[...] = h
        a_ref[...] = _nt(h, wa_ref[...])
        d_ref[...] = _nt(h, wd_ref[...])
        b_ref[...] = _nt(h, wb_ref[...])

    def rows(width):
        return pl.BlockSpec((ROW_TILE, width), lambda i: (i, 0))

    def whole(a):
        return pl.BlockSpec(a.shape, lambda i: (0, 0))

    return pl.pallas_call(
        body,
        grid=(seq // ROW_TILE,),
        in_specs=[rows(D_MODEL), _row(D_MODEL), _row(D_MODEL), _row(D_MODEL), whole(w_a), whole(w_d), whole(w_ba)],
        out_specs=[rows(D_MODEL), rows(3 * HEAD_W), rows(4 * HEAD_W), rows(LANES)],
        out_shape=[jax.ShapeDtypeStruct((seq, D_MODEL), BF16), jax.ShapeDtypeStruct((seq, 3 * HEAD_W), F32),
                   jax.ShapeDtypeStruct((seq, 4 * HEAD_W), F32), jax.ShapeDtypeStruct((seq, LANES), F32)],
        compiler_params=_params(("arbitrary",)),
        name="inproj_fwd",
    )(x, gain, scale, shift, w_a, w_d, w_ba)


def _outproj_fwd(y_attn, y_delta, w_out, x, gate1, gain, scale, shift):
    seq = x.shape[0]

    def body(ya_ref, yd_ref, wa_ref, wd_ref, x_ref, g1_ref, g_ref, sc_ref, sh_ref, x1_ref, h_ref, y_ref):
        y = _nn(ya_ref[...].astype(BF16), wa_ref[...]) + _nn(yd_ref[...], wd_ref[...])
        x1 = x_ref[...] + g1_ref[...] * y
        _, _, norm = _rms_fwd(x1, g_ref[...])
        x1_ref[...] = x1
        h_ref[...] = (norm * (1.0 + sc_ref[...]) + sh_ref[...]).astype(BF16)
        y_ref[...] = y.astype(BF16)

    def rows(width):
        return pl.BlockSpec((ROW_TILE, width), lambda i: (i, 0))

    return pl.pallas_call(
        body,
        grid=(seq // ROW_TILE,),
        in_specs=[rows(HEAD_W), rows(HEAD_W),
                  pl.BlockSpec((HEAD_W, D_MODEL), lambda i: (0, 0)), pl.BlockSpec((HEAD_W, D_MODEL), lambda i: (1, 0)),
                  rows(D_MODEL), _row(D_MODEL), _row(D_MODEL), _row(D_MODEL), _row(D_MODEL)],
        out_specs=[rows(D_MODEL), rows(D_MODEL), rows(D_MODEL)],
        out_shape=[jax.ShapeDtypeStruct((seq, D_MODEL), F32), jax.ShapeDtypeStruct((seq, D_MODEL), BF16),
                   jax.ShapeDtypeStruct((seq, D_MODEL), BF16)],
        compiler_params=_params(("arbitrary",)),
        name="outproj_fwd",
    )(y_attn, y_delta, w_out, w_out, x, gate1, gain, scale, shift)


def _ffn_fwd(h2, w_gate, w_up, w_down, x1, gate2, final_gain, target):
    seq = h2.shape[0]
    n_rows, n_ff = seq // ROW_TILE, D_FF // FF_TILE

    def body(h_ref, wg_ref, wu_ref, wd_ref, x1_ref, g2_ref, gf_ref, t_ref, gate_ref, up_ref, dx2_ref, st_ref, acc):
        i, j = pl.program_id(0), pl.program_id(1)

        @pl.when((i == 0) & (j == 0))
        def _():
            st_ref[...] = jnp.zeros_like(st_ref)

        h = h_ref[...]
        gate = _nt(h, wg_ref[...])
        up = _nt(h, wu_ref[...])
        gate_ref[...] = gate.astype(BF16)
        up_ref[...] = up.astype(BF16)
        part = _nn((_silu(gate) * up).astype(BF16), wd_ref[...])

        @pl.when(j == 0)
        def _():
            acc[...] = part

        @pl.when(j > 0)
        def _():
            acc[...] += part

        @pl.when(j == n_ff - 1)
        def _():
            y2 = acc[...]
            x2 = x1_ref[...] + g2_ref[...] * y2
            xhat, rstd, out = _rms_fwd(x2, gf_ref[...])
            diff = out - t_ref[...]
            dx2, dgain = _rms_bwd(diff * (1.0 / D_MODEL), xhat, rstd, gf_ref[...])
            dx2_ref[...] = dx2
            st_ref[0:1, :] += dgain
            st_ref[1:2, :] += jnp.sum(dx2 * y2, axis=0, keepdims=True)
            st_ref[2:3, :] += jnp.sum(diff * diff, axis=0, keepdims=True) * (0.5 / D_MODEL)

        @pl.when((i == n_rows - 1) & (j == n_ff - 1))
        def _():
            st_ref[3:4, :] = jnp.broadcast_to(jnp.sum(st_ref[2:3, :], keepdims=True), (1, D_MODEL))

    def rows(width):
        return pl.BlockSpec((ROW_TILE, width), lambda i, j: (i, 0))

    ff = pl.BlockSpec((ROW_TILE, FF_TILE), lambda i, j: (i, j))
    return pl.pallas_call(
        body,
        grid=(n_rows, n_ff),
        in_specs=[rows(D_MODEL),
                  pl.BlockSpec((FF_TILE, D_MODEL), lambda i, j: (j, 0)), pl.BlockSpec((FF_TILE, D_MODEL), lambda i, j: (j, 0)),
                  pl.BlockSpec((FF_TILE, D_MODEL), lambda i, j: (j, 0)),
                  rows(D_MODEL), _row(D_MODEL), _row(D_MODEL), rows(D_MODEL)],
        out_specs=[ff, ff, rows(D_MODEL), pl.BlockSpec((8, D_MODEL), lambda i, j: (0, 0))],
        out_shape=[jax.ShapeDtypeStruct((seq, D_FF), BF16), jax.ShapeDtypeStruct((seq, D_FF), BF16),
                   jax.ShapeDtypeStruct((seq, D_MODEL), F32), jax.ShapeDtypeStruct((8, D_MODEL), F32)],
        scratch_shapes=[pltpu.VMEM((ROW_TILE, D_MODEL), F32)],
        compiler_params=_params(("arbitrary", "arbitrary")),
        name="ffn_fwd",
    )(h2, w_gate, w_up, w_down, x1, gate2, final_gain, target)


def _ffn_bwd(dx2, gate, up, w_gate, w_up, w_down, x1, y, gate2, gate1, gain, scale):
    seq = dx2.shape[0]

    def act_body(dx2_ref, g2_ref, gate_ref, up_ref, wd_ref, dgate_ref, dup_ref, act_ref, dy2_ref):
        dy2 = (g2_ref[...] * dx2_ref[...]).astype(BF16)
        dy2_ref[...] = dy2
        gate = gate_ref[...].astype(F32)
        up = up_ref[...].astype(F32)
        dact = _nt(dy2, wd_ref[...])
        silu, slope = _silu_and_slope(gate)
        act_ref[...] = (silu * up).astype(BF16)
        dgate_ref[...] = (dact * up * slope).astype(BF16)
        dup_ref[...] = (dact * silu).astype(BF16)

    def rows2(width):
        return pl.BlockSpec((ROW_TILE, width), lambda i, j: (i, 0))

    ff = pl.BlockSpec((ROW_TILE, FF_TILE), lambda i, j: (i, j))
    dgate, dup, act, dy2 = pl.pallas_call(
        act_body,
        grid=(seq // ROW_TILE, D_FF // FF_TILE),
        in_specs=[rows2(D_MODEL), _row(D_MODEL), ff, ff, pl.BlockSpec((FF_TILE, D_MODEL), lambda i, j: (j, 0))],
        out_specs=[ff, ff, ff, rows2(D_MODEL)],
        out_shape=[jax.ShapeDtypeStruct((seq, D_FF), BF16)] * 3 + [jax.ShapeDtypeStruct((seq, D_MODEL), BF16)],
        compiler_params=_params(("arbitrary", "arbitrary")),
        name="ffn_bwd_act",
    )(dx2, gate2, gate, up, w_down)

    def in_body(dgate_ref, dup_ref, wg_ref, wu_ref, dx2_ref, x1_ref, y_ref, g1_ref, g_ref, sc_ref,
                dx1_ref, dy_ref, st_ref):
        @pl.when(pl.program_id(0) == 0)
        def _():
            st_ref[...] = jnp.zeros_like(st_ref)

        dh = _nn(dgate_ref[...], wg_ref[...]) + _nn(dup_ref[...], wu_ref[...])
        xhat, rstd, norm = _rms_fwd(x1_ref[...], g_ref[...])
        dxn, dgain = _rms_bwd(dh * (1.0 + sc_ref[...]), xhat, rstd, g_ref[...])
        dx1 = dx2_ref[...] + dxn
        dx1_ref[...] = dx1
        dy_ref[...] = (g1_ref[...] * dx1).astype(BF16)
        st_ref[0:1, :] += jnp.sum(dh, axis=0, keepdims=True)
        st_ref[1:2, :] += jnp.sum(dh * norm, axis=0, keepdims=True)
        st_ref[2:3, :] += dgain
        st_ref[3:4, :] += jnp.sum(dx1 * y_ref[...].astype(F32), axis=0, keepdims=True)

    half_tile = ROW_TILE // 2

    def rows(width):
        return pl.BlockSpec((half_tile, width), lambda i: (i, 0))

    whole = pl.BlockSpec((D_FF, D_MODEL), lambda i: (0, 0))
    dx1, dy, stats = pl.pallas_call(
        in_body,
        grid=(seq // half_tile,),
        in_specs=[rows(D_FF), rows(D_FF), whole, whole, rows(D_MODEL), rows(D_MODEL), rows(D_MODEL),
                  _row(D_MODEL), _row(D_MODEL), _row(D_MODEL)],
        out_specs=[rows(D_MODEL), rows(D_MODEL), pl.BlockSpec((8, D_MODEL), lambda i: (0, 0))],
        out_shape=[jax.ShapeDtypeStruct((seq, D_MODEL), F32), jax.ShapeDtypeStruct((seq, D_MODEL), BF16),
                   jax.ShapeDtypeStruct((8, D_MODEL), F32)],
        compiler_params=_params(("arbitrary",)),
        name="ffn_bwd_in",
    )(dgate, dup, w_gate, w_up, dx2, x1, y, gate1, gain, scale)
    return dgate, dup, act, dy2, dx1, dy, stats


def _outproj_bwd(dy, w_out):
    seq = dy.shape[0]

    def body(dy_ref, w_ref, out_ref):
        out_ref[...] = _nt(dy_ref[...], w_ref[...])

    rows = pl.BlockSpec((ROW_TILE, D_MODEL), lambda i: (i, 0))
    return pl.pallas_call(
        body,
        grid=(seq // ROW_TILE,),
        in_specs=[rows, pl.BlockSpec((D_MODEL, D_MODEL), lambda i: (0, 0))],
        out_specs=rows,
        out_shape=jax.ShapeDtypeStruct((seq, D_MODEL), F32),
        compiler_params=_params(("arbitrary",)),
        name="outproj_bwd",
    )(dy, w_out)


def _inproj_bwd(dq, dk, dv, dxd, dz, dba, w_a, w_d, w_ba, x, dx1, gain, scale, partials):
    seq = x.shape[0]
    n = len(partials)
    n_steps = seq // ROW_TILE

    def body(*refs):
        (dq_ref, dk_ref, dv_ref, dxd_ref, dz_ref, dba_ref, wa_ref, wd_ref, wb_ref, x_ref, dx1_ref, g_ref,
         sc_ref) = refs[:13]
        gx_ref, st_ref = refs[13 + n:15 + n]
        riding = (refs[13:13 + n], refs[15 + n:15 + 2 * n], *refs[15 + 2 * n:])

        @pl.when(pl.program_id(0) == 0)
        def _():
            st_ref[...] = jnp.zeros_like(st_ref)
            for cp in (_scatter_copies(*riding) if n else []):
                cp.start()

        dh = (_nn(dq_ref[...].astype(BF16), wa_ref[0:HEAD_W, :])
              + _nn(dk_ref[...].astype(BF16), wa_ref[HEAD_W:2 * HEAD_W, :])
              + _nn(dv_ref[...].astype(BF16), wa_ref[2 * HEAD_W:, :])
              + _nn(dxd_ref[...].astype(BF16), wd_ref[0:3 * HEAD_W, :])
              + _nn(dz_ref[...].astype(BF16), wd_ref[3 * HEAD_W:, :])
              + _nn(dba_ref[...].astype(BF16), wb_ref[...]))
        xhat, rstd, norm = _rms_fwd(x_ref[...], g_ref[...])
        dxn, dgain = _rms_bwd(dh * (1.0 + sc_ref[...]), xhat, rstd, g_ref[...])
        gx_ref[...] = dx1_ref[...] + dxn
        st_ref[0:1, :] += jnp.sum(dh, axis=0, keepdims=True)
        st_ref[1:2, :] += jnp.sum(dh * norm, axis=0, keepdims=True)
        st_ref[2:3, :] += dgain

        if n:
            @pl.when(pl.program_id(0) == n_steps - 1)
            def _():
                for cp in _scatter_copies(*riding):
                    cp.wait()

    def rows(width):
        return pl.BlockSpec((ROW_TILE, width), lambda i: (i, 0))

    def whole(a):
        return pl.BlockSpec(a.shape, lambda i: (0, 0))

    sems = [pltpu.SemaphoreType.DMA((3 * n,)), pltpu.SemaphoreType.DMA((3 * n,))] if n else []
    return pl.pallas_call(
        body,
        grid=(n_steps,),
        in_specs=[rows(HEAD_W), rows(HEAD_W), rows(HEAD_W), rows(3 * HEAD_W), rows(HEAD_W), rows(LANES),
                  whole(w_a), whole(w_d), whole(w_ba), rows(D_MODEL), rows(D_MODEL), _row(D_MODEL), _row(D_MODEL)]
        + [ANY] * n,
        out_specs=[rows(D_MODEL), pl.BlockSpec((8, D_MODEL), lambda i: (0, 0))] + [ANY] * n,
        out_shape=[jax.ShapeDtypeStruct((seq, D_MODEL), F32), jax.ShapeDtypeStruct((8, D_MODEL), F32)]
        + [jax.ShapeDtypeStruct(p.shape, p.dtype) for p in partials],
        scratch_shapes=sems,
        compiler_params=_params(("arbitrary",)),
        name="inproj_bwd",
    )(dq, dk, dv, dxd, dz, dba, w_a, w_d, w_ba, x, dx1, gain, scale, *partials)


def _weight_grad(a, b, name):
    seq, m = a.shape
    n = b.shape[1]
    tm = m if m <= 1536 else m // 2
    tn = n if n <= 1536 else n // 2
    rows = 2 * WGRAD_ROWS
    n_k = seq // rows

    def body(a_ref, b_ref, out_ref):
        part = _tn(a_ref[...].astype(BF16), b_ref[...].astype(BF16))

        @pl.when(pl.program_id(2) == 0)
        def _():
            out_ref[...] = part

        @pl.when(pl.program_id(2) > 0)
        def _():
            out_ref[...] += part

    return pl.pallas_call(
        body,
        grid=(m // tm, n // tn, n_k),
        in_specs=[pl.BlockSpec((rows, tm), lambda i, j, k: (k, i)),
                  pl.BlockSpec((rows, tn), lambda i, j, k: (k, j))],
        out_specs=pl.BlockSpec((tm, tn), lambda i, j, k: (i, j)),
        out_shape=jax.ShapeDtypeStruct((m, n), F32),
        compiler_params=_params(("arbitrary", "arbitrary", "arbitrary")),
        name=name,
    )(a, b)


def _weight_grad_stack(pieces, b, name):
    seq, n = b.shape
    widths = [a.shape[1] for a in pieces]
    starts = [sum(widths[:i]) for i in range(len(pieces))]

    def body(*refs):
        a_refs, b_ref, out_ref = refs[:len(pieces)], refs[len(pieces)], refs[len(pieces) + 1]

        @pl.when(pl.program_id(0) == 0)
        def _():
            out_ref[...] = jnp.zeros_like(out_ref)

        bb = b_ref[...].astype(BF16)
        for a_ref, start, width in zip(a_refs, starts, widths):
            out_ref[start:start + width, :] += _tn(a_ref[...].astype(BF16), bb)

    def rows(width):
        return pl.BlockSpec((WGRAD_ROWS, width), lambda k: (k, 0))

    return pl.pallas_call(
        body,
        grid=(seq // WGRAD_ROWS,),
        in_specs=[rows(w) for w in widths] + [rows(n)],
        out_specs=pl.BlockSpec((sum(widths), n), lambda k: (0, 0)),
        out_shape=jax.ShapeDtypeStruct((sum(widths), n), F32),
        compiler_params=_params(("arbitrary",)),
        name=name,
    )(*pieces, b)


def _adamw(w, g, m, v, name):
    n_rows, n_cols = w.shape
    if w.size <= 64 * 1024:
        block, grid, index = (n_rows, n_cols), (1,), lambda i: (0, 0)
    elif n_rows % 256 == 0:
        block, grid, index = (256, n_cols), (n_rows // 256,), lambda i: (i, 0)
    elif n_cols % 256 == 0:
        block, grid, index = (n_rows, 256), (n_cols // 256,), lambda i: (0, i)
    else:
        block, grid, index = (n_rows, n_cols), (1,), lambda i: (0, 0)

    def body(w_ref, g_ref, m_ref, v_ref, d_ref, nm_ref, nv_ref):
        gv = g_ref[...]
        nm = ADAM_B1 * m_ref[...] + (1.0 - ADAM_B1) * gv
        nv = ADAM_B2 * v_ref[...] + (1.0 - ADAM_B2) * (gv * gv)
        m_hat = nm / (1.0 - ADAM_B1 ** ADAM_STEP)
        v_hat = nv / (1.0 - ADAM_B2 ** ADAM_STEP)
        d_ref[...] = -ADAM_LR * (m_hat / (jnp.sqrt(v_hat) + ADAM_EPS) + ADAM_WD * w_ref[...])
        nm_ref[...] = nm
        nv_ref[...] = nv

    blk = pl.BlockSpec(block, index)
    shape = jax.ShapeDtypeStruct((n_rows, n_cols), F32)
    return pl.pallas_call(
        body,
        grid=grid,
        in_specs=[blk] * 4,
        out_specs=[blk] * 3,
        out_shape=[shape] * 3,
        compiler_params=_params(("arbitrary",)),
        name=name,
    )(w, g, m, v)


IN_WIDTH = 3600
BA_COL = 7 * HEAD_W


def _local_step(x, target, mod, norm_attn_g, w_in, rel_bias, conv_w, a_log, dt_bias, delta_norm_g,
                norm_ffn_g, final_norm_g, shards, assemble, reduce_pairs):
    sh1, sc1, g1, sh2, sc2, g2 = [mod[:, i * D_MODEL:(i + 1) * D_MODEL] for i in range(6)]
    w_a = w_in[:3 * HEAD_W]
    w_d = w_in[3 * HEAD_W:BA_COL]
    w_ba = jnp.pad(w_in[BA_COL:], ((0, LANES - 2 * N_HEADS), (0, 0)))
    tables = jnp.asarray(_attn_tables())
    alog_row = jnp.pad(a_log, ((0, 0), (N_HEADS, LANES - 2 * N_HEADS)))
    dt_row = jnp.pad(dt_bias, ((0, 0), (N_HEADS, LANES - 2 * N_HEADS)))
    gain_row = jnp.tile(delta_norm_g, (1, N_HEADS))

    h1, qkv_a, qkvz, ba = _inproj_fwd(x, norm_attn_g, sc1, sh1, w_a, w_d, w_ba)
    bias = _attention_bias(rel_bias, tables)
    y_attn, lse, *gathered = _attention_fwd(qkv_a, bias, shards)
    w_out, w_gate, w_up, w_down = assemble(gathered)
    xs = _delta_prep_fwd(qkvz, ba, conv_w, alog_row, dt_row)
    inv_h, qk_h, u_h, w_h = _delta_chunk_fwd(xs)
    o, st_h = _delta_scan_fwd(xs, qk_h, u_h, w_h)
    y_delta = _delta_post_fwd(o, qkvz, gain_row)
    x1, h2, y = _outproj_fwd(y_attn, y_delta, w_out, x, g1, norm_ffn_g, sc2, sh2)
    gate, up, dx2, st_f = _ffn_fwd(h2, w_gate, w_up, w_down, x1, g2, final_norm_g, target)

    dgate, dup, act, dy2, dx1, dy, st_b = _ffn_bwd(dx2, gate, up, w_gate, w_up, w_down, x1, y, g2, g1, norm_ffn_g, sc2)
    partials = reduce_pairs([_weight_grad_stack([y_attn, y_delta], dy, "wgrad_out"),
                             _weight_grad(dgate, h2, "wgrad_gate"), _weight_grad(dup, h2, "wgrad_up"),
                             _weight_grad(act, dy2, "wgrad_down")], 1, "rest")
    grads = {}
    dycat = _outproj_bwd(dy, w_out)
    do, dz, dgain = _delta_post_bwd(dycat, o, qkvz, gain_row)
    dsn_h, dvn_h = _delta_scan_bwd(xs, qk_h, w_h, do)
    dxs = _delta_chunk_bwd(xs, inv_h, u_h, w_h, st_h, dsn_h, dvn_h, do)
    dconv, dba, dvec = _delta_prep_bwd(qkvz, ba, conv_w, alog_row, dt_row, dxs)
    dxd, grads["conv_w"] = _conv_bwd(dconv, qkvz, conv_w)
    dq, dk, dv, dbias, *scattered = _attention_bwd(qkv_a, dycat, y_attn, lse, bias, partials)
    partials_in = reduce_pairs([jnp.concatenate(
        [_weight_grad_stack([dq, dk, dv], h1, "wgrad_in_attn"),
         _weight_grad_stack([dxd, dz, dba], h1, "wgrad_in_delta")[:IN_WIDTH - 3 * HEAD_W]], axis=0)], 0, "in")
    grad_x, st_i, *scattered_in = _inproj_bwd(dq, dk, dv, dxd, dz, dba, w_a, w_d, w_ba, x, dx1, norm_attn_g, sc1,
                                              partials_in)
    grads["rel_bias"] = _rel_bias_grad(dbias, tables)[:, :N_BUCKETS].T
    grads["a_log"] = dvec[0:1, N_HEADS:2 * N_HEADS]
    grads["dt_bias"] = dvec[1:2, N_HEADS:2 * N_HEADS]
    grads["delta_norm_g"] = dgain[1:2, :HEAD_DIM]
    grads["norm_attn_g"] = st_i[2:3]
    grads["norm_ffn_g"] = st_b[2:3]
    grads["final_norm_g"] = st_f[0:1]
    dmod = jnp.concatenate([st_i[0:1], st_i[1:2], st_b[3:4], st_b[0:1], st_b[1:2], st_f[1:2]], axis=1)
    return st_f[3, 0], grad_x, grads, dmod, (partials_in + partials, scattered_in + scattered)


MESH = pl.DeviceIdType.MESH
OTHER_CHIPS = ((1, 0), (0, 1), (1, 1))
ALL_PEERS = tuple((m >> 2 & 1, m >> 1 & 1, m & 1) for m in range(1, 8))
ANY = pl.BlockSpec(memory_space=pl.ANY)
VMEM_SPEC = pl.BlockSpec(memory_space=pltpu.VMEM)


def _me():
    return lax.axis_index("x"), lax.axis_index("y"), lax.axis_index("c")


def _flip(pos, mask):
    return tuple(1 - p if m else p for p, m in zip(pos, mask))


def _remote(src, dst, send_sems, recv_sems, k, to):
    return pltpu.make_async_remote_copy(src_ref=src, dst_ref=dst, send_sem=send_sems.at[k], recv_sem=recv_sems.at[k],
                                        device_id=to, device_id_type=MESH)


def _ada_exchange(c8, w_ada, b_ada, conv8, shard):
    def body(c_ref, w_ref, b_ref, cv_ref, shard_ref, mod_ref, cact_ref, conv_ref, whole_ref,
             c_all, part_all, send_sems, recv_sems, ride_send, ride_recv):
        x, y, c = me = _me()
        dev = 4 * x + 2 * y + c
        chip = 2 * x + y
        riding = ([shard_ref], [whole_ref], ride_send, ride_recv)
        for cp in _gather_copies(*riding, hand_over=False)[0]:
            cp.start()
        c_all[dev] = c_ref[...]
        conv_ref[chip] = cv_ref[...]
        first = [_remote(c_ref, c_all.at[dev], send_sems, recv_sems, k, _flip(me, mask))
                 for k, mask in enumerate(ALL_PEERS)]
        first += [_remote(cv_ref, conv_ref.at[chip], send_sems, recv_sems, 7 + j, _flip(me, (*mask, 0)))
                  for j, mask in enumerate(OTHER_CHIPS)]
        for cp in first:
            cp.start()
        for cp in first:
            cp.wait()
        row = lax.broadcasted_iota(jnp.int32, (8, D_MODEL), 0)
        c_rows = jnp.zeros((8, D_MODEL), F32)
        for d in range(8):
            c_rows = jnp.where(row == d, c_all[d], c_rows)
        c_act = _silu(c_rows)
        cact_ref[...] = c_act
        part_all[chip] = _nn(c_act, w_ref[...], HIGHEST)
        second = [_remote(part_all.at[chip], part_all.at[chip], send_sems, recv_sems, 10 + j, _flip(me, (*mask, 0)))
                  for j, mask in enumerate(OTHER_CHIPS)]
        for cp in second:
            cp.start()
        for cp in second:
            cp.wait()
        cols = w_ref.shape[1]
        for k in range(4):
            mod_ref[:, k * cols:(k + 1) * cols] = part_all[k] + b_ref[:, k * cols:(k + 1) * cols]
        first, passed = _gather_copies(*riding)
        for cp, fwd in zip(first, passed):
            cp.wait_recv()
            fwd.start()
        for cp in first:
            cp.wait_send()
        for fwd in passed:
            fwd.wait()

    cols = w_ada.shape[1]
    return pl.pallas_call(
        body,
        in_specs=[VMEM_SPEC] * 4 + [ANY],
        out_specs=[VMEM_SPEC] * 3 + [ANY],
        out_shape=[jax.ShapeDtypeStruct((8, 4 * cols), F32), jax.ShapeDtypeStruct((8, D_MODEL), F32),
                   jax.ShapeDtypeStruct((4, 8, conv8.shape[1]), F32)] + _gathered_shapes([shard]),
        scratch_shapes=[pltpu.VMEM((8, 8, D_MODEL), F32), pltpu.VMEM((4, 8, cols), F32),
                        pltpu.SemaphoreType.DMA((13,)), pltpu.SemaphoreType.DMA((13,)),
                        pltpu.SemaphoreType.DMA((6,)), pltpu.SemaphoreType.DMA((6,))],
        compiler_params=pltpu.CompilerParams(vmem_limit_bytes=VMEM_LIMIT),
        name="ada_exchange",
    )(c8, w_ada, b_ada, conv8, shard)


def _gathered_shapes(shards):
    return [jax.ShapeDtypeStruct((4, *s.shape), s.dtype) for s in shards]


def _gather_copies(srcs, dsts, send_sems, recv_sems, hand_over=True):
    x, y, c = me = _me()
    chip = 2 * x + y
    sibling = _flip(me, (0, 0, 1))
    first, passed = [], []
    for a, (src, dst) in enumerate(zip(srcs, dsts)):
        for j, mask in enumerate(OTHER_CHIPS):
            to = _flip(me, (*mask, 0))
            first.append(_remote(src.at[c], dst.at[chip, c], send_sems, recv_sems, 6 * a + j, to))
            if hand_over:
                landed = dst.at[2 * to[0] + to[1], c]
                passed.append(_remote(landed, landed, send_sems, recv_sems, 6 * a + 3 + j, sibling))
    return first, passed


def _scatter_copies(srcs, dsts, send_sems, recv_sems):
    x, y, c = me = _me()
    chip = 2 * x + y
    copies = []
    for a, (src, dst) in enumerate(zip(srcs, dsts)):
        for j, mask in enumerate(OTHER_CHIPS):
            to = _flip(me, (*mask, 0))
            copies.append(_remote(src.at[2 * to[0] + to[1]], dst.at[chip], send_sems, recv_sems, 3 * a + j, to))
    return copies


def _start_and_wait(copies):
    for cp in copies:
        cp.start()
    for cp in copies:
        cp.wait()


def _swap_halves(grads):
    n = len(grads)

    def body(*refs):
        srcs, got = refs[:n], refs[n:2 * n]
        send_sems, recv_sems = refs[2 * n:]
        x, y, c = me = _me()
        _start_and_wait([_remote(srcs[a].at[:, 1 - c], got[a], send_sems, recv_sems, a, _flip(me, (0, 0, 1)))
                         for a in range(n)])

    return pl.pallas_call(
        body,
        in_specs=[ANY] * n,
        out_specs=[ANY] * n,
        out_shape=[jax.ShapeDtypeStruct((4, g.shape[2], g.shape[3]), g.dtype) for g in grads],
        scratch_shapes=[pltpu.SemaphoreType.DMA((n,)), pltpu.SemaphoreType.DMA((n,))],
        name=f"swap_halves_{n}",
    )(*grads)


def _join_halves(halves):
    n = len(halves)

    def body(*refs):
        srcs, dsts = refs[:n], refs[n:2 * n]
        send_sems, recv_sems = refs[2 * n:]
        x, y, c = me = _me()
        _start_and_wait([_remote(srcs[a], dsts[a].at[c], send_sems, recv_sems, a, _flip(me, (0, 0, 1)))
                         for a in range(n)])

    return pl.pallas_call(
        body,
        in_specs=[ANY] * n,
        out_specs=[ANY] * n,
        out_shape=[jax.ShapeDtypeStruct((2, *h.shape), h.dtype) for h in halves],
        scratch_shapes=[pltpu.SemaphoreType.DMA((n,)), pltpu.SemaphoreType.DMA((n,))],
        name=f"join_halves_{n}",
    )(*halves)


def _gather_small(packed):
    n_rows = packed.shape[0]

    def body(p_ref, all_ref, sum_ref, send_sems, recv_sems):
        x, y, c = me = _me()
        dev = 4 * x + 2 * y + c
        all_ref[dev] = p_ref[...]
        copies = [_remote(p_ref, all_ref.at[dev], send_sems, recv_sems, k, _flip(me, mask))
                  for k, mask in enumerate(ALL_PEERS)]
        for cp in copies:
            cp.start()
        for cp in copies:
            cp.wait()
        total = all_ref[0]
        for d in range(1, 8):
            total = total + all_ref[d]
        sum_ref[...] = total

    return pl.pallas_call(
        body,
        in_specs=[VMEM_SPEC],
        out_specs=[VMEM_SPEC, VMEM_SPEC],
        out_shape=[jax.ShapeDtypeStruct((8, n_rows, LANES), F32), jax.ShapeDtypeStruct((n_rows, LANES), F32)],
        scratch_shapes=[pltpu.SemaphoreType.DMA((7,)), pltpu.SemaphoreType.DMA((7,))],
        name="gather_small",
    )(packed)


def _add_pair(a, b, out_dtype, name):
    def body(a_ref, b_ref, o_ref):
        o_ref[...] = (a_ref[...] + b_ref[...]).astype(o_ref.dtype)

    blk = pl.BlockSpec((1, *a.shape[1:]), lambda i: (i, 0, 0))
    return pl.pallas_call(
        body, grid=(a.shape[0],), in_specs=[blk, blk], out_specs=blk,
        out_shape=jax.ShapeDtypeStruct(a.shape, out_dtype),
        compiler_params=_params(("arbitrary",)), name=name,
    )(a, b)


def _add_slots(a, name):
    def body(a_ref, o_ref):
        total = a_ref[0].astype(F32)
        for k in range(1, 4):
            total = total + a_ref[k].astype(F32)
        o_ref[...] = total

    return pl.pallas_call(
        body, in_specs=[VMEM_SPEC], out_specs=VMEM_SPEC,
        out_shape=jax.ShapeDtypeStruct(a.shape[1:], F32),
        compiler_params=pltpu.CompilerParams(vmem_limit_bytes=VMEM_LIMIT), name=name,
    )(a)


def _ada_weight_grad(c_act, dmod_cols):
    def body(c_ref, d_ref, o_ref):
        o_ref[...] = _tn(c_ref[...], d_ref[...], HIGHEST)

    return pl.pallas_call(
        body, in_specs=[VMEM_SPEC, VMEM_SPEC], out_specs=VMEM_SPEC,
        out_shape=jax.ShapeDtypeStruct((c_act.shape[1], dmod_cols.shape[1]), F32),
        compiler_params=pltpu.CompilerParams(vmem_limit_bytes=VMEM_LIMIT), name="ada_weight_grad",
    )(c_act, dmod_cols)


def kernel(x, c, w_ada, b_ada, norm_attn_g, w_in, rel_bias, conv_w, a_log, dt_bias, delta_norm_g, w_out, norm_ffn_g, w_gate, w_up, w_down, final_norm_g, loss_target, m_w_ada, m_b_ada, m_norm_attn_g, m_w_in, m_rel_bias, m_conv_w, m_a_log, m_dt_bias, m_delta_norm_g, m_w_out, m_norm_ffn_g, m_w_gate, m_w_up, m_w_down, m_final_norm_g, v_w_ada, v_b_ada, v_norm_attn_g, v_w_in, v_rel_bias, v_conv_w, v_a_log, v_dt_bias, v_delta_norm_g, v_w_out, v_norm_ffn_g, v_w_gate, v_w_up, v_w_down, v_final_norm_g):
    xi, yi, ci = _me()
    dev = 4 * xi + 2 * yi + ci
    chip = 2 * xi + yi

    big_names = ("w_in", "w_out", "w_gate", "w_up", "w_down")
    by_cols = (True, False, True, True, False)

    def rows_form(a, cols):
        return jnp.swapaxes(a[0], 0, 1) if cols else a[0]

    def halves_form(w):
        rows, lanes = w.shape
        if (rows // 2) % 16:
            rows, lanes = w.size // LANES, LANES
        return (2, rows // 2, lanes)

    big = [rows_form(w, cols) for w, cols in zip((w_in, w_out, w_gate, w_up, w_down), by_cols)]
    shards = [w.astype(BF16).reshape(halves_form(w)) for w in big]

    def assemble(gathered, first):
        return [lax.dynamic_update_index_in_dim(g, s, chip, 0).reshape(4 * w.shape[0], w.shape[1])
                for g, s, w in zip(gathered, shards[first:], big[first:])]

    def reduce_pairs(grads, first, tag):
        slots = [g.reshape(4, *halves_form(w)) for g, w in zip(grads, big[first:])]
        return [_add_pair(lax.dynamic_index_in_dim(s, ci, 1, keepdims=False), got, BF16, f"add_pair_{tag}{a}")
                for a, (s, got) in enumerate(zip(slots, _swap_halves(slots)))]

    def finish(partials, scattered, first, tag):
        by_source = [lax.dynamic_update_index_in_dim(b, lax.dynamic_index_in_dim(p, chip, 0, keepdims=False), chip, 0)
                     for b, p in zip(scattered, partials)]
        halves = [_add_slots(p, f"add_slots_{tag}{a}") for a, p in enumerate(by_source)]
        joined = [lax.dynamic_update_index_in_dim(j, h, ci, 0) for j, h in zip(_join_halves(halves), halves)]
        return [j.reshape(w.shape) for j, w in zip(joined, big[first:])]

    conv_cols = conv_w.shape[2]
    mod_all, c_act, conv_all, gathered_in = _ada_exchange(
        jnp.broadcast_to(c, (8, D_MODEL)), w_ada[0], b_ada, jnp.pad(conv_w[0], ((0, 4), (0, 0))), shards[0])
    mod = lax.dynamic_slice_in_dim(mod_all, dev, 1, axis=0)
    conv_full = jnp.swapaxes(conv_all[:, :4, :], 0, 1).reshape(4, 4 * conv_cols)
    whole_in, = assemble([gathered_in], 0)
    loss, grad_x, grads, dmod, (partials, scattered) = _local_step(
        x[0], loss_target[0], mod, norm_attn_g, whole_in, rel_bias, conv_full, a_log, dt_bias, delta_norm_g,
        norm_ffn_g, final_norm_g[None], shards[1:], functools.partial(assemble, first=1), reduce_pairs)

    big_grads = finish(partials, scattered, 0, "all")

    pieces = [dmod, grads["conv_w"], grads["norm_attn_g"], grads["norm_ffn_g"], grads["final_norm_g"],
              grads["rel_bias"], grads["a_log"], grads["dt_bias"], grads["delta_norm_g"]]
    flat = [jnp.pad(p.reshape(-1), (0, -p.size % LANES)) for p in pieces]
    n_rows = [f.size // LANES for f in flat]
    packed = jnp.concatenate(flat).reshape(-1, LANES)
    packed = jnp.pad(packed, ((0, -packed.shape[0] % 8), (0, 0)))
    all_small, total = _gather_small(packed)
    sums, start = [], 0
    for p, n in zip(pieces, n_rows):
        sums.append(total[start:start + n].reshape(-1)[:p.size].reshape(p.shape))
        start += n
    g_b_ada, g_conv, g_norm_attn, g_norm_ffn, g_final, g_rel, g_alog, g_dt, g_dnorm = sums
    dmod_all = all_small[:, :n_rows[0], :].reshape(8, -1)
    ada_cols = w_ada.shape[2]
    g_w_ada = _ada_weight_grad(c_act, lax.dynamic_slice_in_dim(dmod_all, chip * ada_cols, ada_cols, axis=1))
    g_conv = lax.dynamic_slice_in_dim(g_conv, chip * conv_cols, conv_cols, axis=1)

    grad = {"w_ada": g_w_ada[None], "b_ada": g_b_ada, "norm_attn_g": g_norm_attn,
            "rel_bias": g_rel, "conv_w": g_conv[None], "a_log": g_alog, "dt_bias": g_dt, "delta_norm_g": g_dnorm,
            "norm_ffn_g": g_norm_ffn, "final_norm_g": g_final.reshape(-1)}
    weight = {"w_ada": w_ada, "b_ada": b_ada, "norm_attn_g": norm_attn_g, "w_in": w_in, "rel_bias": rel_bias,
              "conv_w": conv_w, "a_log": a_log, "dt_bias": dt_bias, "delta_norm_g": delta_norm_g, "w_out": w_out,
              "norm_ffn_g": norm_ffn_g, "w_gate": w_gate, "w_up": w_up, "w_down": w_down, "final_norm_g": final_norm_g}
    first = {"w_ada": m_w_ada, "b_ada": m_b_ada, "norm_attn_g": m_norm_attn_g, "w_in": m_w_in, "rel_bias": m_rel_bias,
             "conv_w": m_conv_w, "a_log": m_a_log, "dt_bias": m_dt_bias, "delta_norm_g": m_delta_norm_g,
             "w_out": m_w_out, "norm_ffn_g": m_norm_ffn_g, "w_gate": m_w_gate, "w_up": m_w_up, "w_down": m_w_down,
             "final_norm_g": m_final_norm_g}
    second = {"w_ada": v_w_ada, "b_ada": v_b_ada, "norm_attn_g": v_norm_attn_g, "w_in": v_w_in, "rel_bias": v_rel_bias,
              "conv_w": v_conv_w, "a_log": v_a_log, "dt_bias": v_dt_bias, "delta_norm_g": v_delta_norm_g,
              "w_out": v_w_out, "norm_ffn_g": v_norm_ffn_g, "w_gate": v_w_gate, "w_up": v_w_up, "w_down": v_w_down,
              "final_norm_g": v_final_norm_g}
    delta, new_m, new_v = {}, {}, {}
    for name, w in weight.items():
        if name in big_names:
            continue
        two_d = (-1, w.shape[-1])
        d, nm, nv = _adamw(w.reshape(two_d), grad[name].reshape(two_d), first[name].reshape(two_d),
                           second[name].reshape(two_d), f"adamw_{name}")
        delta[name], new_m[name], new_v[name] = d.reshape(w.shape), nm.reshape(w.shape), nv.reshape(w.shape)
    for name, w, g, cols in zip(big_names, big, big_grads, by_cols):
        outs = _adamw(w, g, rows_form(first[name], cols), rows_form(second[name], cols), f"adamw_{name}")
        grad[name], delta[name], new_m[name], new_v[name] = [
            (jnp.swapaxes(o, 0, 1) if cols else o)[None] for o in (g, *outs)]

    names = list(weight)
    return (lax.psum(loss, ("x", "y", "c")), grad_x[None], *[grad[n] for n in names], *[delta[n] for n in names],
            *[new_m[n] for n in names], *[new_v[n] for n in names])
```

```python
import functools
import math

import numpy as np
import jax
import jax.numpy as jnp
from jax import lax
from jax.experimental import pallas as pl
from jax.experimental.pallas import tpu as pltpu

F32 = jnp.float32
BF16 = jnp.bfloat16
HIGHEST = lax.Precision.HIGHEST

D_MODEL = 1024
HEAD_DIM = 64
N_HEADS = 8
HEAD_W = 512
BRANCHES = ((128, 1), (512, 4), (2048, 16))
BAND = 128
ATT_TILE = 2048
ATT_UNROLL = 8
ATT_UNROLL_BWD = 4
N_BUCKETS = 32
MAX_DISTANCE = 2048
CHUNK = 64
D_FF = 2816
EPS = 1e-6
NEG_INF = -1e30
LANES = 128
VMEM_LIMIT = 56 * 1024 * 1024

ADAM_LR = 0.001
ADAM_B1 = 0.9
ADAM_B2 = 0.999
ADAM_EPS = 1e-08
ADAM_WD = 0.01
ADAM_STEP = 10


def _nn(a, b, precision=None):
    return jnp.dot(a, b, preferred_element_type=F32, precision=precision)


def _nt(a, b, precision=None):
    return lax.dot_general(a, b, (((1,), (1,)), ((), ())), preferred_element_type=F32, precision=precision)


def _tn(a, b, precision=None):
    return lax.dot_general(a, b, (((0,), (0,)), ((), ())), preferred_element_type=F32, precision=precision)


def _params(sem, vmem=VMEM_LIMIT):
    return pltpu.CompilerParams(dimension_semantics=sem, vmem_limit_bytes=vmem)


def _sigmoid(x):
    return 0.5 * jnp.tanh(0.5 * x) + 0.5


def _silu_and_slope(x):
    s = _sigmoid(x)
    return x * s, s * (1.0 + x * (1.0 - s))


def _silu(x):
    return x * _sigmoid(x)


def _attn_tables():
    qi = np.arange(BAND)[:, None]
    kj = np.arange(2 * BAND)[None, :]
    steps = qi + BAND - kj
    in_window = (steps >= 0) & (steps <= BAND)
    max_exact = N_BUCKETS // 2
    out = np.zeros((3, 2, BAND, 2 * BAND), np.int32)
    for b, (_, dil) in enumerate(BRANCHES):
        dist = np.maximum(steps, 0) * dil
        dist_f = np.maximum(dist, 1).astype(np.float32)
        large = max_exact + (np.log(dist_f / np.float32(max_exact)) / np.float32(math.log(MAX_DISTANCE / max_exact))
                             * np.float32(N_BUCKETS - max_exact)).astype(np.int32)
        bucket = np.where(dist < max_exact, dist, np.minimum(large, N_BUCKETS - 1)).astype(np.int32)
        out[b, 0] = np.where(in_window, bucket, -1)
        out[b, 1] = np.where(in_window & (kj >= BAND), bucket, -1)
    return out


def _attention_bias(rel_bias, tables):
    def body(rel_ref, tab_ref, out_ref):
        head = pl.program_id(0)
        for b in range(3):
            tab = tab_ref[b, 0]

            def pick(kk, acc, tab=tab):
                return jnp.where(tab == kk, rel_ref[kk, head], acc)

            acc = lax.fori_loop(0, N_BUCKETS, pick, jnp.zeros((BAND, 2 * BAND), F32))
            for first in range(2):
                out_ref[0, b, first] = jnp.where(tab_ref[b, first] < 0, NEG_INF, acc)

    return pl.pallas_call(
        body,
        grid=(N_HEADS,),
        in_specs=[pl.BlockSpec(memory_space=pltpu.SMEM),
                  pl.BlockSpec((3, 2, BAND, 2 * BAND), lambda h: (0, 0, 0, 0))],
        out_specs=pl.BlockSpec((1, 3, 2, BAND, 2 * BAND), lambda h: (h, 0, 0, 0, 0)),
        out_shape=jax.ShapeDtypeStruct((N_HEADS, 3, 2, BAND, 2 * BAND), F32),
        compiler_params=_params(("arbitrary",)),
        name="attn_bias",
    )(rel_bias, tables)


def _bias_spec():
    return pl.BlockSpec((2, 3, 2, BAND, 2 * BAND), lambda p, t: (p, 0, 0, 0, 0))


def _attn_block_index(idx, t, r):
    nb = ATT_TILE // (BAND * r)
    rho = idx // nb
    n = idx % nb
    qs = rho + r * BAND * n
    gs = t * ATT_TILE + qs
    first = (t * nb + n) == 0
    ps = jnp.where(first, gs, gs - r * BAND)
    return qs, gs, ps, first.astype(jnp.int32)


def _rows(start, r):
    return pl.ds(start, BAND) if r == 1 else pl.ds(start, BAND, stride=r)


def _attention_fwd(qkv, bias, shards):
    seq = qkv.shape[0]
    n_tiles = seq // ATT_TILE
    n = len(shards)

    def body(*refs):
        bias_ref, q_ref, k_ref, v_ref = refs[:4]
        y_ref, lse_ref = refs[4 + n:6 + n]
        o_s, l_s = refs[6 + 2 * n:8 + 2 * n]
        riding = (refs[4:4 + n], refs[6 + n:6 + 2 * n], *refs[8 + 2 * n:])
        pair = pl.program_id(0)
        t = pl.program_id(1)
        if n:
            @pl.when((pair == 0) & (t == 0))
            def _():
                for cp in _gather_copies(*riding, hand_over=False)[0]:
                    cp.start()

            @pl.when((pair == 2) & (t == 0))
            def _():
                for cp, fwd in zip(*_gather_copies(*riding)):
                    cp.wait_recv()
                    fwd.start()

        lane = lax.broadcasted_iota(jnp.int32, (1, LANES), 1)
        head0 = lane < HEAD_DIM
        masks = (head0, jnp.logical_not(head0))
        ones = jnp.ones((2 * BAND, LANES), BF16)
        for b, (_, r) in enumerate(BRANCHES):
            def blocks(it, carry, b=b, r=r):
                idx = [_attn_block_index(it * ATT_UNROLL + j, t, r) for j in range(ATT_UNROLL)]
                qb = [q_ref[_rows(qs, r), :] * (HEAD_DIM ** -0.5) for qs, _, _, _ in idx]
                kcat = [jnp.concatenate([k_ref[_rows(ps, r), :], k_ref[_rows(gs, r), :]], axis=0).astype(BF16)
                        for _, gs, ps, _ in idx]
                vcat = [jnp.concatenate([v_ref[_rows(ps, r), :], v_ref[_rows(gs, r), :]], axis=0).astype(BF16)
                        for _, gs, ps, _ in idx]
                work = [(j, hh) for j in range(ATT_UNROLL) for hh in range(2)]
                s = [_nt(jnp.where(masks[hh], qb[j], 0.0).astype(BF16), kcat[j]) + bias_ref[hh, b, idx[j][3]]
                     for j, hh in work]
                m = [jnp.max(sv, axis=-1, keepdims=True) for sv in s]
                e = [jnp.exp(sv - mv) for sv, mv in zip(s, m)]
                eb = [ev.astype(BF16) for ev in e]
                den = [_nn(ev, ones) for ev in eb]
                out = [_nn(ev, vcat[j]) / dv for ev, dv, (j, _) in zip(eb, den, work)]
                lse = [mv + jnp.log(dv) for mv, dv in zip(m, den)]
                for j in range(ATT_UNROLL):
                    o_s[b, _rows(idx[j][0], r), :] = jnp.where(head0, out[2 * j], out[2 * j + 1])
                    l_s[b, _rows(idx[j][0], r), :] = jnp.where(head0, lse[2 * j], lse[2 * j + 1])
                return carry

            lax.fori_loop(0, ATT_TILE // BAND // ATT_UNROLL, blocks, 0)

        def merge(i, carry):
            rows = pl.ds(pl.multiple_of(i * BAND, BAND), BAND)
            l0, l1, l2 = l_s[0, rows, :], l_s[1, rows, :], l_s[2, rows, :]
            m = jnp.maximum(jnp.maximum(l0, l1), l2)
            w0, w1, w2 = jnp.exp(l0 - m), jnp.exp(l1 - m), jnp.exp(l2 - m)
            tot = w0 + w1 + w2
            y_ref[rows, :] = (w0 * o_s[0, rows, :] + w1 * o_s[1, rows, :] + w2 * o_s[2, rows, :]) / tot
            lse_ref[rows, :] = m + jnp.log(tot)
            return carry

        lax.fori_loop(0, ATT_TILE // BAND, merge, 0)

        if n:
            @pl.when((pair == N_HEADS // 2 - 1) & (t == n_tiles - 1))
            def _():
                first, passed = _gather_copies(*riding)
                for cp in first:
                    cp.wait_send()
                for fwd in passed:
                    fwd.wait()

    tile = pl.BlockSpec((ATT_TILE, LANES), lambda p, t: (t, p))
    sems = [pltpu.SemaphoreType.DMA((6 * n,)), pltpu.SemaphoreType.DMA((6 * n,))] if n else []
    return pl.pallas_call(
        body,
        grid=(N_HEADS // 2, n_tiles),
        in_specs=[
            _bias_spec(),
            pl.BlockSpec((ATT_TILE, LANES), lambda p, t: (t, p)),
            pl.BlockSpec((seq, LANES), lambda p, t: (0, 4 + p)),
            pl.BlockSpec((seq, LANES), lambda p, t: (0, 8 + p)),
        ] + [ANY] * n,
        out_specs=[tile, tile] + [ANY] * n,
        out_shape=[jax.ShapeDtypeStruct((seq, HEAD_W), F32), jax.ShapeDtypeStruct((seq, HEAD_W), F32)]
        + _gathered_shapes(shards),
        scratch_shapes=[
            pltpu.VMEM((3, ATT_TILE, LANES), F32),
            pltpu.VMEM((3, ATT_TILE, LANES), F32),
        ] + sems,
        compiler_params=_params(("arbitrary", "arbitrary")),
        name="attn_fwd",
    )(bias, qkv, qkv, qkv, *shards)


def _attention_bwd(qkv, dy, y, lse, bias, partials):
    seq = qkv.shape[0]
    n_tiles = seq // ATT_TILE
    n = len(partials)

    def body(*refs):
        bias_ref, q_ref, k_ref, v_ref, dy_ref, y_ref, lse_ref = refs[:7]
        dq_ref, dk_ref, dv_ref, dbias_ref = refs[7 + n:11 + n]
        riding = (refs[7:7 + n], refs[11 + n:11 + 2 * n], *refs[11 + 2 * n:])
        pair = pl.program_id(0)
        t = pl.program_id(1)
        if n:
            @pl.when((pair == 0) & (t == 0))
            def _():
                for cp in _scatter_copies(*riding):
                    cp.start()

        lane = lax.broadcasted_iota(jnp.int32, (1, LANES), 1)
        head0 = lane < HEAD_DIM

        @pl.when(t == 0)
        def _():
            dk_ref[...] = jnp.zeros_like(dk_ref)
            dv_ref[...] = jnp.zeros_like(dv_ref)
            dbias_ref[...] = jnp.zeros_like(dbias_ref)

        dq_ref[...] = jnp.zeros_like(dq_ref)

        masks = (head0, jnp.logical_not(head0))
        ones = jnp.ones((LANES, LANES), BF16)
        scale = HEAD_DIM ** -0.5
        for b, (_, r) in enumerate(BRANCHES):
            def blocks(it, carry, b=b, r=r):
                idx = [_attn_block_index(it * ATT_UNROLL_BWD + j, t, r) for j in range(ATT_UNROLL_BWD)]
                qb = [q_ref[_rows(qs, r), :] * scale for qs, _, _, _ in idx]
                kcat = [jnp.concatenate([k_ref[_rows(ps, r), :], k_ref[_rows(gs, r), :]], axis=0).astype(BF16)
                        for _, gs, ps, _ in idx]
                vcat = [jnp.concatenate([v_ref[_rows(ps, r), :], v_ref[_rows(gs, r), :]], axis=0).astype(BF16)
                        for _, gs, ps, _ in idx]
                dob = [dy_ref[_rows(qs, r), :] for qs, _, _, _ in idx]
                ob = [y_ref[_rows(qs, r), :] for qs, _, _, _ in idx]
                lb = [lse_ref[_rows(qs, r), :] for qs, _, _, _ in idx]
                work = [(j, hh) for j in range(ATT_UNROLL_BWD) for hh in range(2)]
                qh = [jnp.where(masks[hh], qb[j], 0.0).astype(BF16) for j, hh in work]
                doh = [jnp.where(masks[hh], dob[j], 0.0) for j, hh in work]
                dohb = [d.astype(BF16) for d in doh]
                s = [_nt(qh[w], kcat[j]) + bias_ref[hh, b, idx[j][3]] for w, (j, hh) in enumerate(work)]
                dp = [_nt(dohb[w], vcat[j]) for w, (j, _) in enumerate(work)]
                lrot = [pltpu.roll(lv, HEAD_DIM, 1) for lv in lb]
                lcol = [jnp.where(masks[hh], lb[j], lrot[j]) for j, hh in work]
                parts = [_split(doh[w] * ob[j]) for w, (j, _) in enumerate(work)]
                delta = [_nn(hi, ones) + _nn(lo, ones) for hi, lo in parts]
                prob = [jnp.exp(sv - jnp.concatenate([lv, lv], axis=1)) for sv, lv in zip(s, lcol)]
                ds = [pv * (dv - jnp.concatenate([de, de], axis=1)) for pv, dv, de in zip(prob, dp, delta)]
                dsb = [d.astype(BF16) for d in ds]
                dq = [_nn(dsb[w], kcat[j]) for w, (j, _) in enumerate(work)]
                dkc = [_tn(dsb[w], qh[w]) for w in range(len(work))]
                dvc = [_tn(prob[w].astype(BF16), dohb[w]) for w in range(len(work))]
                for hh in range(2):
                    dbias_ref[0, b, hh] += sum(ds[w] for w, (_, head) in enumerate(work) if head == hh)
                for j in range(ATT_UNROLL_BWD):
                    qs, gs, ps, _ = idx[j]
                    dkcat = dkc[2 * j] + dkc[2 * j + 1]
                    dvcat = dvc[2 * j] + dvc[2 * j + 1]
                    dq_ref[_rows(qs, r), :] += jnp.where(head0, dq[2 * j], dq[2 * j + 1]) * scale
                    dk_ref[_rows(ps, r), :] += dkcat[:BAND]
                    dk_ref[_rows(gs, r), :] += dkcat[BAND:]
                    dv_ref[_rows(ps, r), :] += dvcat[:BAND]
                    dv_ref[_rows(gs, r), :] += dvcat[BAND:]
                return carry

            lax.fori_loop(0, ATT_TILE // BAND // ATT_UNROLL_BWD, blocks, 0)

        if n:
            @pl.when((pair == N_HEADS // 2 - 1) & (t == n_tiles - 1))
            def _():
                for cp in _scatter_copies(*riding):
                    cp.wait()

    tile = pl.BlockSpec((ATT_TILE, LANES), lambda p, t: (t, p))
    full = pl.BlockSpec((seq, LANES), lambda p, t: (0, p))
    sems = [pltpu.SemaphoreType.DMA((3 * n,)), pltpu.SemaphoreType.DMA((3 * n,))] if n else []
    return pl.pallas_call(
        body,
        grid=(N_HEADS // 2, n_tiles),
        in_specs=[
            _bias_spec(),
            pl.BlockSpec((ATT_TILE, LANES), lambda p, t: (t, p)),
            pl.BlockSpec((seq, LANES), lambda p, t: (0, 4 + p)),
            pl.BlockSpec((seq, LANES), lambda p, t: (0, 8 + p)),
            tile, tile, tile,
        ] + [ANY] * n,
        out_specs=[tile, full, full,
                   pl.BlockSpec((1, 3, 2, BAND, 2 * BAND), lambda p, t: (p, 0, 0, 0, 0))] + [ANY] * n,
        out_shape=[jax.ShapeDtypeStruct((seq, HEAD_W), F32)] * 3
        + [jax.ShapeDtypeStruct((N_HEADS // 2, 3, 2, BAND, 2 * BAND), F32)]
        + [jax.ShapeDtypeStruct(p.shape, p.dtype) for p in partials],
        scratch_shapes=sems,
        compiler_params=_params(("arbitrary", "arbitrary")),
        name="attn_bwd",
    )(bias, qkv, qkv, qkv, dy, y, lse, *partials)


def _rel_bias_grad(dbias, tables):
    def body(tab_ref, db_ref, out_ref):
        lane = lax.broadcasted_iota(jnp.int32, (1, LANES), 1)
        out_ref[...] = jnp.zeros_like(out_ref)
        for b in range(3):
            tab = tab_ref[b, 0]

            def head(h, carry, b=b, tab=tab):
                d = db_ref[h // 2, b, h % 2]
                sums = [jnp.sum(jnp.where(tab == kk, d, 0.0), keepdims=True) for kk in range(N_BUCKETS)]
                row = jnp.zeros((1, LANES), F32)
                for kk, s in enumerate(sums):
                    row = row + jnp.where(lane == kk, s, 0.0)
                out_ref[pl.ds(h, 1), :] += row
                return carry

            lax.fori_loop(0, N_HEADS, head, 0)

    return pl.pallas_call(
        body,
        out_shape=jax.ShapeDtypeStruct((N_HEADS, LANES), F32),
        compiler_params=pltpu.CompilerParams(vmem_limit_bytes=VMEM_LIMIT),
        name="rel_bias_grad",
    )(tables, dbias)


ROW_TILE = 512


def _head_sum_matrix():
    return (lax.broadcasted_iota(jnp.int32, (HEAD_W, HEAD_W), 0) // HEAD_DIM
            == lax.broadcasted_iota(jnp.int32, (HEAD_W, HEAD_W), 1) // HEAD_DIM).astype(F32)


def _head_spread_matrix(offset=0):
    return (lax.broadcasted_iota(jnp.int32, (LANES, HEAD_W), 0)
            == lax.broadcasted_iota(jnp.int32, (LANES, HEAD_W), 1) // HEAD_DIM + offset).astype(F32)


def _head_gather_matrix(offset=0):
    return (lax.broadcasted_iota(jnp.int32, (HEAD_W, LANES), 0) // HEAD_DIM + offset
            == lax.broadcasted_iota(jnp.int32, (HEAD_W, LANES), 1)).astype(F32)


def _split3(x):
    hi = x.astype(BF16)
    rest = x - hi.astype(F32)
    mid = rest.astype(BF16)
    return hi, mid, (rest - mid.astype(F32)).astype(BF16)


def _pick(x, onehot):
    m = onehot.astype(BF16)
    hi, mid, lo = _split3(x)
    return _nn(hi, m) + (_nn(mid, m) + _nn(lo, m))


def _pick_left(onehot, x):
    m = onehot.astype(BF16)
    hi, mid, lo = _split3(x)
    return _nn(m, hi) + (_nn(m, mid) + _nn(m, lo))


def _tri(lower, strict=False):
    r = lax.broadcasted_iota(jnp.int32, (CHUNK, CHUNK), 0)
    c = lax.broadcasted_iota(jnp.int32, (CHUNK, CHUNK), 1)
    if lower:
        return (c < r) if strict else (c <= r)
    return c >= r


def _softplus(z):
    return jnp.maximum(z, 0.0) + jnp.log(1.0 + jnp.exp(-jnp.abs(z)))


def _conv_taps(stage, w_ref, rows):
    return (w_ref[3:4, :] * stage[8:8 + rows, :] + w_ref[2:3, :] * stage[7:7 + rows, :]
            + w_ref[1:2, :] * stage[6:6 + rows, :] + w_ref[0:1, :] * stage[5:5 + rows, :])


def _l2_scale(xc, hsum):
    return lax.rsqrt(_pick(xc * xc, hsum) + EPS)


def _stage_rows(stage, x_ref, xp_ref, i):
    stage[0:8, :] = jnp.where(i == 0, 0.0, xp_ref[...])
    stage[8:8 + ROW_TILE, :] = x_ref[...]


def _delta_prep_fwd(qkvz, ba, conv_w, alog_row, dt_row):
    seq = qkvz.shape[0]
    qkv_w = 3 * HEAD_W

    def body(x_ref, xp_ref, ba_ref, w_ref, al_ref, dt_ref, out_ref, stage):
        i = pl.program_id(0)
        _stage_rows(stage, x_ref, xp_ref, i)
        act = _silu(_conv_taps(stage, w_ref, ROW_TILE))
        hsum, hspread = _head_sum_matrix(), _head_spread_matrix()
        qc, kc = act[:, :HEAD_W], act[:, HEAD_W:2 * HEAD_W]
        out_ref[0] = qc * _l2_scale(qc, hsum) * (HEAD_DIM ** -0.5)
        out_ref[1] = kc * _l2_scale(kc, hsum)
        out_ref[2] = act[:, 2 * HEAD_W:]
        bav = ba_ref[...]
        out_ref[3] = _pick(_sigmoid(bav), hspread)
        g8 = -jnp.exp(al_ref[...]) * _softplus(bav + dt_ref[...])
        gb = _pick(g8, _head_spread_matrix(N_HEADS))
        cum = _tri(True).astype(F32)
        for ch in range(ROW_TILE // CHUNK):
            rows = slice(ch * CHUNK, (ch + 1) * CHUNK)
            out_ref[4, rows, :] = _pick_left(cum, gb[rows])

    return pl.pallas_call(
        body,
        grid=(seq // ROW_TILE,),
        in_specs=[
            pl.BlockSpec((ROW_TILE, qkv_w), lambda i: (i, 0)),
            pl.BlockSpec((8, qkv_w), lambda i: (jnp.maximum(i * (ROW_TILE // 8) - 1, 0), 0)),
            pl.BlockSpec((ROW_TILE, LANES), lambda i: (i, 0)),
            pl.BlockSpec((4, qkv_w), lambda i: (0, 0)),
            pl.BlockSpec((1, LANES), lambda i: (0, 0)),
            pl.BlockSpec((1, LANES), lambda i: (0, 0)),
        ],
        out_specs=pl.BlockSpec((5, ROW_TILE, HEAD_W), lambda i: (0, i, 0)),
        out_shape=jax.ShapeDtypeStruct((5, seq, HEAD_W), F32),
        scratch_shapes=[pltpu.VMEM((ROW_TILE + 8, qkv_w), F32)],
        compiler_params=_params(("arbitrary",)),
        name="delta_prep_fwd",
    )(qkvz, qkvz, ba, conv_w, alog_row, dt_row)


def _split(x):
    hi = x.astype(BF16)
    return hi, (x - hi.astype(F32)).astype(BF16)


def _dot3(a, b, dot=_nn):
    return dot(a[0], b[0]) + (dot(a[0], b[1]) + dot(a[1], b[0]))


def _unit_lower_inverses(mats):
    eye = (lax.broadcasted_iota(jnp.int32, (CHUNK, CHUNK), 0)
           == lax.broadcasted_iota(jnp.int32, (CHUNK, CHUNK), 1)).astype(F32)
    invs = [eye - a for a in mats]
    powers = [_split(a) for a in mats]
    for step in range(5):
        squares = [_dot3(p, p) for p in powers]
        powers = [_split(s) for s in squares]
        invs = [inv + _dot3(_split(inv), p) for inv, p in zip(invs, powers)]
    return invs


def _chunk_terms(q, k, v, beta, gc):
    causal, strict = _tri(True), _tri(True, strict=True)
    e = jnp.exp(gc)
    g_last = jnp.broadcast_to(gc[CHUNK - 1:CHUNK, :], (CHUNK, CHUNK))
    f = jnp.exp(g_last - gc)
    e_last = jnp.exp(g_last)
    decay = jnp.where(causal, jnp.exp(jnp.where(causal, gc - gc.T, 0.0)), 0.0)
    kb = k * beta
    a_mat = jnp.where(strict, _nt(kb.astype(BF16), k.astype(BF16)) * decay, 0.0)
    qk = jnp.where(causal, _nt(q.astype(BF16), k.astype(BF16)) * decay, 0.0)
    return e, f, e_last, decay, kb, a_mat, qk


GROUP = 8
UNROLL = 8


def _chunk_rows(ci):
    return pl.ds(pl.multiple_of(ci * CHUNK, CHUNK), CHUNK)


def _pair_specs(n_planes):
    return pl.BlockSpec((n_planes, GROUP * CHUNK, LANES), lambda p, g: (0, g, p))


def _delta_chunk_fwd(xs):
    seq = xs.shape[1]
    rows_per_step = GROUP * CHUNK

    def body(x_ref, inv_ref, qk_ref, u_ref, w_ref):
        work = [(hh, slice(step * CHUNK, (step + 1) * CHUNK)) for hh in range(2) for step in range(GROUP)]
        xh = [[x_ref[j, r, hh * HEAD_DIM:(hh + 1) * HEAD_DIM] for j in range(5)] for hh, r in work]
        terms = [_chunk_terms(*x) for x in xh]
        invs = _unit_lower_inverses([t[5] for t in terms])
        for (hh, r), x, t, inv in zip(work, xh, terms, invs):
            e, kb, qk = t[0], t[4], t[6]
            inv_parts = _split(inv)
            inv_ref[hh, r, :] = inv
            qk_ref[hh, r, :] = qk
            u_ref[hh, r, :] = _dot3(inv_parts, _split(x[2] * x[3]))
            w_ref[hh, r, :] = _dot3(inv_parts, _split(kb * e))

    out = pl.BlockSpec((2, rows_per_step, HEAD_DIM), lambda p, g: (p, g, 0))
    return pl.pallas_call(
        body,
        grid=(N_HEADS // 2, seq // rows_per_step),
        in_specs=[_pair_specs(5)],
        out_specs=[out] * 4,
        out_shape=[jax.ShapeDtypeStruct((N_HEADS, seq, HEAD_DIM), F32)] * 4,
        compiler_params=_params(("parallel", "parallel")),
        name="delta_chunk_fwd",
    )(xs)


def _decays(gc):
    g_last = jnp.broadcast_to(gc[CHUNK - 1:CHUNK, :], (CHUNK, CHUNK))
    return jnp.exp(gc), jnp.exp(g_last - gc), jnp.exp(g_last)


def _token_blocks(index, n_steps=None):
    rows_per_step = GROUP * CHUNK
    if n_steps is None:
        return pl.BlockSpec((1, rows_per_step, HEAD_W), lambda g: (index, g, 0))
    return pl.BlockSpec((1, rows_per_step, HEAD_W), lambda g: (index, n_steps - 1 - g, 0))


def _head_lanes(h):
    return pl.ds(h * HEAD_DIM, HEAD_DIM)


def _delta_scan_fwd(xs, qk_h, u_h, w_h):
    seq = xs.shape[1]
    rows_per_step = GROUP * CHUNK

    def body(q_ref, k_ref, gc_ref, qk_ref, u_ref, w_ref, o_ref, st_ref, state):
        @pl.when(pl.program_id(0) == 0)
        def _():
            state[...] = jnp.zeros_like(state)

        def chunk(ci, carry):
            rows = _chunk_rows(ci)
            heads = range(N_HEADS)
            dec = [_decays(gc_ref[0, rows, _head_lanes(h)]) for h in heads]
            s = [state[h] for h in heads]
            sb = [s[h].astype(BF16) for h in heads]
            vnb = [(u_ref[h, rows, :] - _nn(w_ref[h, rows, :].astype(BF16), sb[h])).astype(BF16) for h in heads]
            for h in heads:
                o_ref[rows, _head_lanes(h)] = (_nn((q_ref[0, rows, _head_lanes(h)] * dec[h][0]).astype(BF16), sb[h])
                                               + _nn(qk_ref[h, rows, :].astype(BF16), vnb[h]))
                st_ref[h, rows, :] = s[h]
            for h in heads:
                state[h] = s[h] * dec[h][2] + _tn((k_ref[0, rows, _head_lanes(h)] * dec[h][1]).astype(BF16), vnb[h])
            return carry

        lax.fori_loop(0, GROUP, chunk, 0)

    blk = pl.BlockSpec((N_HEADS, rows_per_step, HEAD_DIM), lambda g: (0, g, 0))
    return pl.pallas_call(
        body,
        grid=(seq // rows_per_step,),
        in_specs=[_token_blocks(0), _token_blocks(1), _token_blocks(4), blk, blk, blk],
        out_specs=[pl.BlockSpec((rows_per_step, HEAD_W), lambda g: (g, 0)), blk],
        out_shape=[jax.ShapeDtypeStruct((seq, HEAD_W), F32), jax.ShapeDtypeStruct((N_HEADS, seq, HEAD_DIM), F32)],
        scratch_shapes=[pltpu.VMEM((N_HEADS, CHUNK, CHUNK), F32)],
        compiler_params=_params(("arbitrary",)),
        name="delta_scan_fwd",
    )(xs, xs, xs, qk_h, u_h, w_h)


def _delta_scan_bwd(xs, qk_h, w_h, do):
    seq = xs.shape[1]
    rows_per_step = GROUP * CHUNK
    n_steps = seq // rows_per_step

    def body(q_ref, k_ref, gc_ref, qk_ref, w_ref, do_ref, dsn_ref, dvn_ref, dstate):
        @pl.when(pl.program_id(0) == 0)
        def _():
            dstate[...] = jnp.zeros_like(dstate)

        def chunk(step, carry):
            rows = _chunk_rows(GROUP - 1 - step)
            heads = range(N_HEADS)
            dec = [_decays(gc_ref[0, rows, _head_lanes(h)]) for h in heads]
            ds_next = [dstate[h] for h in heads]
            dob = [do_ref[rows, _head_lanes(h)].astype(BF16) for h in heads]
            dv_new = [_tn(qk_ref[h, rows, :].astype(BF16), dob[h])
                      + _nn((k_ref[0, rows, _head_lanes(h)] * dec[h][1]).astype(BF16), ds_next[h].astype(BF16))
                      for h in heads]
            for h in heads:
                dsn_ref[h, rows, :] = ds_next[h]
                dvn_ref[h, rows, :] = dv_new[h]
            for h in heads:
                dstate[h] = (_tn((q_ref[0, rows, _head_lanes(h)] * dec[h][0]).astype(BF16), dob[h])
                             + dec[h][2] * ds_next[h] - _tn(w_ref[h, rows, :].astype(BF16), dv_new[h].astype(BF16)))
            return carry

        lax.fori_loop(0, GROUP, chunk, 0)

    blk = pl.BlockSpec((N_HEADS, rows_per_step, HEAD_DIM), lambda g: (0, n_steps - 1 - g, 0))
    return pl.pallas_call(
        body,
        grid=(n_steps,),
        in_specs=[_token_blocks(0, n_steps), _token_blocks(1, n_steps), _token_blocks(4, n_steps), blk, blk,
                  pl.BlockSpec((rows_per_step, HEAD_W), lambda g: (n_steps - 1 - g, 0))],
        out_specs=[blk, blk],
        out_shape=[jax.ShapeDtypeStruct((N_HEADS, seq, HEAD_DIM), F32)] * 2,
        scratch_shapes=[pltpu.VMEM((N_HEADS, CHUNK, CHUNK), F32)],
        compiler_params=_params(("arbitrary",)),
        name="delta_scan_bwd",
    )(xs, xs, xs, qk_h, w_h, do)


def _delta_chunk_bwd(xs, inv_h, u_h, w_h, st_h, dsn_h, dvn_h, do):
    seq = xs.shape[1]
    rows_per_step = GROUP * CHUNK

    def body(x_ref, inv_ref, u_ref, w_ref, st_ref, dsn_ref, dvn_ref, do_ref, dx_ref):
        causal, strict = _tri(True), _tri(True, strict=True)
        last_row = lax.broadcasted_iota(jnp.int32, (CHUNK, CHUNK), 0) == CHUNK - 1

        def bf(vals):
            return [val.astype(BF16) for val in vals]

        def group(items):
            heads = [hh for hh, _ in items]
            lanes = [slice(hh * HEAD_DIM, (hh + 1) * HEAD_DIM) for hh in heads]
            rows = [slice(step * CHUNK, (step + 1) * CHUNK) for _, step in items]
            n = range(len(items))
            q, k, v, beta, gc = [[x_ref[j, rows[i], lanes[i]] for i in n] for j in range(5)]
            terms = [_chunk_terms(q[i], k[i], v[i], beta[i], gc[i]) for i in n]
            e, f, e_last, decay, kb, a_mat, qk = [[t[j] for t in terms] for j in range(7)]
            inv = [_split(inv_ref[heads[i], rows[i], :]) for i in n]
            u = [u_ref[heads[i], rows[i], :] for i in n]
            w = [w_ref[heads[i], rows[i], :] for i in n]
            s = [st_ref[heads[i], rows[i], :] for i in n]
            ds_next = [dsn_ref[heads[i], rows[i], :] for i in n]
            dv_new = [dvn_ref[heads[i], rows[i], :] for i in n]
            sb, dsb, dvb, wb = bf(s), bf(ds_next), bf(dv_new), bf(w)
            dob = bf([do_ref[rows[i], lanes[i]] for i in n])
            qbf, kbf, kbb = bf(q), bf(k), bf(kb)
            vnb = bf([u[i] - _nn(wb[i], sb[i]) for i in n])
            dqe = [_nt(dob[i], sb[i]) for i in n]
            dw = [-_nt(dvb[i], sb[i]) for i in n]
            dkf = [_nt(vnb[i], dsb[i]) for i in n]
            dqk = [jnp.where(causal, _nt(dob[i], vnb[i]), 0.0) for i in n]
            drhs_u = [_dot3(inv[i], _split(dv_new[i]), _tn) for i in n]
            drhs_w = [_dot3(inv[i], _split(dw[i]), _tn) for i in n]
            da = [-jnp.where(strict, _nt(drhs_u[i].astype(BF16), u[i].astype(BF16))
                             + _nt(drhs_w[i].astype(BF16), wb[i]), 0.0) for i in n]
            dad = bf([da[i] * decay[i] for i in n])
            dqd = bf([dqk[i] * decay[i] for i in n])
            dkb = [e[i] * drhs_w[i] + _nn(dad[i], kbf[i]) for i in n]
            dk = [_tn(dad[i], kbb[i]) + _tn(dqd[i], qbf[i]) + f[i] * dkf[i] + beta[i] * dkb[i] for i in n]
            dq = [_nn(dqd[i], kbf[i]) + e[i] * dqe[i] for i in n]
            for i in n:
                de_full = kb[i] * drhs_w[i] + q[i] * dqe[i]
                df_full = k[i] * dkf[i]
                m = da[i] * a_mat[i] + dqk[i] * qk[i]
                dgc = de_full * e[i] - df_full * f[i] + m - m.T
                tail = jnp.sum(df_full * f[i] + s[i] * ds_next[i] * e_last[i], axis=0, keepdims=True)
                dgc = dgc + jnp.where(last_row, jnp.broadcast_to(tail, (CHUNK, CHUNK)), 0.0)
                dx_ref[0, rows[i], lanes[i]] = dq[i]
                dx_ref[1, rows[i], lanes[i]] = dk[i]
                dx_ref[2, rows[i], lanes[i]] = beta[i] * drhs_u[i]
                dx_ref[3, rows[i], lanes[i]] = v[i] * drhs_u[i] + k[i] * dkb[i]
                dx_ref[4, rows[i], lanes[i]] = dgc

        work = [(hh, step) for hh in range(2) for step in range(GROUP)]
        for first in range(0, len(work), UNROLL):
            group(work[first:first + UNROLL])

    blk = pl.BlockSpec((2, rows_per_step, HEAD_DIM), lambda p, g: (p, g, 0))
    return pl.pallas_call(
        body,
        grid=(N_HEADS // 2, seq // rows_per_step),
        in_specs=[_pair_specs(5)] + [blk] * 6 + [pl.BlockSpec((rows_per_step, LANES), lambda p, g: (g, p))],
        out_specs=_pair_specs(5),
        out_shape=jax.ShapeDtypeStruct((5, seq, HEAD_W), F32),
        compiler_params=_params(("parallel", "parallel")),
        name="delta_chunk_bwd",
    )(xs, inv_h, u_h, w_h, st_h, dsn_h, dvn_h, do)


def _delta_post_fwd(o, qkvz, gain_row):
    seq = o.shape[0]

    def body(o_ref, z_ref, g_ref, y_ref):
        ov = o_ref[...]
        rb = lax.rsqrt(_pick(ov * ov, _head_sum_matrix()) * (1.0 / HEAD_DIM) + EPS)
        y_ref[...] = (ov * rb * g_ref[...] * _silu(z_ref[...])).astype(y_ref.dtype)

    tile = pl.BlockSpec((ROW_TILE, HEAD_W), lambda i: (i, 0))
    return pl.pallas_call(
        body,
        grid=(seq // ROW_TILE,),
        in_specs=[tile, pl.BlockSpec((ROW_TILE, HEAD_W), lambda i: (i, 3)), pl.BlockSpec((1, HEAD_W), lambda i: (0, 0))],
        out_specs=tile,
        out_shape=jax.ShapeDtypeStruct((seq, HEAD_W), BF16),
        compiler_params=_params(("arbitrary",)),
        name="delta_post_fwd",
    )(o, qkvz, gain_row)


def _delta_post_bwd(dy, o, qkvz, gain_row):
    seq = o.shape[0]

    def body(dy_ref, o_ref, z_ref, g_ref, do_ref, dz_ref, dg_ref):
        @pl.when(pl.program_id(0) == 0)
        def _():
            dg_ref[...] = jnp.zeros_like(dg_ref)

        ov, zv, dyv, gain = o_ref[...], z_ref[...], dy_ref[...], g_ref[...]
        hsum = _head_sum_matrix()
        rb = lax.rsqrt(_pick(ov * ov, hsum) * (1.0 / HEAD_DIM) + EPS)
        ohat = ov * rb
        silu_z, slope_z = _silu_and_slope(zv)
        dz_ref[...] = dyv * ohat * gain * slope_z
        dn = dyv * silu_z
        dg_ref[0:1, :] += jnp.sum(dn * ohat, axis=0, keepdims=True)
        dohat = dn * gain

        @pl.when(pl.program_id(0) == pl.num_programs(0) - 1)
        def _():
            fold = (lax.broadcasted_iota(jnp.int32, (HEAD_W, HEAD_W), 0) % HEAD_DIM
                    == lax.broadcasted_iota(jnp.int32, (HEAD_W, HEAD_W), 1)).astype(F32)
            dg_ref[1:2, :] = _pick(dg_ref[0:1, :], fold)

        proj = _pick(dohat * ohat, hsum) * (1.0 / HEAD_DIM)
        do_ref[...] = rb * (dohat - ohat * proj)

    tile = pl.BlockSpec((ROW_TILE, HEAD_W), lambda i: (i, 0))
    return pl.pallas_call(
        body,
        grid=(seq // ROW_TILE,),
        in_specs=[pl.BlockSpec((ROW_TILE, HEAD_W), lambda i: (i, 1)), tile,
                  pl.BlockSpec((ROW_TILE, HEAD_W), lambda i: (i, 3)), pl.BlockSpec((1, HEAD_W), lambda i: (0, 0))],
        out_specs=[tile, tile, pl.BlockSpec((2, HEAD_W), lambda i: (0, 0))],
        out_shape=[jax.ShapeDtypeStruct((seq, HEAD_W), F32), jax.ShapeDtypeStruct((seq, HEAD_W), F32),
                   jax.ShapeDtypeStruct((2, HEAD_W), F32)],
        compiler_params=_params(("arbitrary",)),
        name="delta_post_bwd",
    )(dy, o, qkvz, gain_row)


def _delta_prep_bwd(qkvz, ba, conv_w, alog_row, dt_row, dxs):
    seq = qkvz.shape[0]
    qkv_w = 3 * HEAD_W

    def body(x_ref, xp_ref, ba_ref, w_ref, al_ref, dt_ref, dx_ref, dconv_ref, dba_ref, dvec_ref, stage):
        i = pl.program_id(0)

        @pl.when(i == 0)
        def _():
            dvec_ref[...] = jnp.zeros_like(dvec_ref)

        _stage_rows(stage, x_ref, xp_ref, i)
        pre = _conv_taps(stage, w_ref, ROW_TILE)
        act, slope = _silu_and_slope(pre)
        hsum = _head_sum_matrix()
        for j, scale in ((0, HEAD_DIM ** -0.5), (1, 1.0)):
            cols = slice(j * HEAD_W, (j + 1) * HEAD_W)
            xc = act[:, cols]
            rb = _l2_scale(xc, hsum)
            xhat = xc * rb
            dhat = dx_ref[j] * scale
            proj = _pick(dhat * xhat, hsum)
            dconv_ref[:, cols] = rb * (dhat - xhat * proj) * slope[:, cols]
        dconv_ref[:, 2 * HEAD_W:] = dx_ref[2] * slope[:, 2 * HEAD_W:]

        bav = ba_ref[...]
        beta8 = _sigmoid(bav)
        dbeta8 = _pick(dx_ref[3], _head_gather_matrix())
        dgc8 = _pick(dx_ref[4], _head_gather_matrix(N_HEADS))
        rev = _tri(False).astype(F32)
        z = bav + dt_ref[...]
        ea = jnp.exp(al_ref[...])
        g8 = -ea * _softplus(z)
        sig = _sigmoid(z)
        d_alog = jnp.zeros((1, LANES), F32)
        d_dt = jnp.zeros((1, LANES), F32)
        for ch in range(ROW_TILE // CHUNK):
            rows = slice(ch * CHUNK, (ch + 1) * CHUNK)
            dg8 = _pick_left(rev, dgc8[rows])
            da = -dg8 * ea * sig[rows]
            dba_ref[rows, :] = dbeta8[rows] * beta8[rows] * (1.0 - beta8[rows]) + da
            d_alog = d_alog + jnp.sum(dg8 * g8[rows], axis=0, keepdims=True)
            d_dt = d_dt + jnp.sum(da, axis=0, keepdims=True)
        dvec_ref[0:1, :] += d_alog
        dvec_ref[1:2, :] += d_dt

    return pl.pallas_call(
        body,
        grid=(seq // ROW_TILE,),
        in_specs=[
            pl.BlockSpec((ROW_TILE, qkv_w), lambda i: (i, 0)),
            pl.BlockSpec((8, qkv_w), lambda i: (jnp.maximum(i * (ROW_TILE // 8) - 1, 0), 0)),
            pl.BlockSpec((ROW_TILE, LANES), lambda i: (i, 0)),
            pl.BlockSpec((4, qkv_w), lambda i: (0, 0)),
            pl.BlockSpec((1, LANES), lambda i: (0, 0)),
            pl.BlockSpec((1, LANES), lambda i: (0, 0)),
            pl.BlockSpec((5, ROW_TILE, HEAD_W), lambda i: (0, i, 0)),
        ],
        out_specs=[pl.BlockSpec((ROW_TILE, qkv_w), lambda i: (i, 0)),
                   pl.BlockSpec((ROW_TILE, LANES), lambda i: (i, 0)),
                   pl.BlockSpec((2, LANES), lambda i: (0, 0))],
        out_shape=[jax.ShapeDtypeStruct((seq, qkv_w), F32), jax.ShapeDtypeStruct((seq, LANES), F32),
                   jax.ShapeDtypeStruct((2, LANES), F32)],
        scratch_shapes=[pltpu.VMEM((ROW_TILE + 8, qkv_w), F32)],
        compiler_params=_params(("arbitrary",)),
        name="delta_prep_bwd",
    )(qkvz, qkvz, ba, conv_w, alog_row, dt_row, dxs)


def _conv_bwd(dconv, qkvz, conv_w):
    seq = dconv.shape[0]
    qkv_w = 3 * HEAD_W
    n_tiles = seq // ROW_TILE

    def body(dy_ref, dyn_ref, x_ref, xp_ref, w_ref, dx_ref, dw_ref, stage, dstage):
        i = pl.program_id(0)

        @pl.when(i == 0)
        def _():
            dw_ref[...] = jnp.zeros_like(dw_ref)

        _stage_rows(stage, x_ref, xp_ref, i)
        dstage[0:ROW_TILE, :] = dy_ref[...]
        dstage[ROW_TILE:ROW_TILE + 8, :] = jnp.where(i == n_tiles - 1, 0.0, dyn_ref[...])
        dy = dy_ref[...]
        dx_ref[...] = (w_ref[3:4, :] * dy + w_ref[2:3, :] * dstage[1:1 + ROW_TILE, :]
                       + w_ref[1:2, :] * dstage[2:2 + ROW_TILE, :] + w_ref[0:1, :] * dstage[3:3 + ROW_TILE, :])
        for j in range(4):
            dw_ref[j:j + 1, :] += jnp.sum(dy * stage[5 + j:5 + j + ROW_TILE, :], axis=0, keepdims=True)

    tile = pl.BlockSpec((ROW_TILE, qkv_w), lambda i: (i, 0))
    return pl.pallas_call(
        body,
        grid=(n_tiles,),
        in_specs=[
            tile,
            pl.BlockSpec((8, qkv_w), lambda i: (jnp.minimum((i + 1) * (ROW_TILE // 8), seq // 8 - 1), 0)),
            tile,
            pl.BlockSpec((8, qkv_w), lambda i: (jnp.maximum(i * (ROW_TILE // 8) - 1, 0), 0)),
            pl.BlockSpec((4, qkv_w), lambda i: (0, 0)),
        ],
        out_specs=[tile, pl.BlockSpec((4, qkv_w), lambda i: (0, 0))],
        out_shape=[jax.ShapeDtypeStruct((seq, qkv_w), F32), jax.ShapeDtypeStruct((4, qkv_w), F32)],
        scratch_shapes=[pltpu.VMEM((ROW_TILE + 8, qkv_w), F32), pltpu.VMEM((ROW_TILE + 8, qkv_w), F32)],
        compiler_params=_params(("arbitrary",)),
        name="conv_bwd",
    )(dconv, dconv, qkvz, qkvz, conv_w)


FF_TILE = 1408
WGRAD_ROWS = 1024


def _row(a):
    return pl.BlockSpec((1, a), lambda *_: (0, 0))


def _rms_fwd(xv, gain):
    rstd = lax.rsqrt(jnp.mean(xv * xv, axis=-1, keepdims=True) + EPS)
    xhat = xv * rstd
    return xhat, rstd, xhat * gain


def _rms_bwd(dnorm, xhat, rstd, gain):
    dxhat = dnorm * gain
    dx = rstd * (dxhat - xhat * jnp.mean(dxhat * xhat, axis=-1, keepdims=True))
    return dx, jnp.sum(dnorm * xhat, axis=0, keepdims=True)


IN_SPLITS = (0, 3 * HEAD_W, 7 * HEAD_W, 7 * HEAD_W + LANES)


def _inproj_fwd(x, gain, scale, shift, w_rows):
    seq = x.shape[0]

    def body(x_ref, g_ref, sc_ref, sh_ref, w_ref, h_ref, a_ref, d_ref, b_ref):
        _, _, norm = _rms_fwd(x_ref[...], g_ref[...])
        h = (norm * (1.0 + sc_ref[...]) + sh_ref[...]).astype(BF16)
        h_ref[...] = h
        for out_ref, lo, hi in zip((a_ref, d_ref, b_ref), IN_SPLITS[:-1], IN_SPLITS[1:]):
            out_ref[...] = _nt(h, w_ref[lo:hi, :])

    def rows(width):
        return pl.BlockSpec((ROW_TILE, width), lambda i: (i, 0))

    return pl.pallas_call(
        body,
        grid=(seq // ROW_TILE,),
        in_specs=[rows(D_MODEL), _row(D_MODEL), _row(D_MODEL), _row(D_MODEL),
                  pl.BlockSpec(w_rows.shape, lambda i: (0, 0))],
        out_specs=[rows(D_MODEL), rows(3 * HEAD_W), rows(4 * HEAD_W), rows(LANES)],
        out_shape=[jax.ShapeDtypeStruct((seq, D_MODEL), BF16), jax.ShapeDtypeStruct((seq, 3 * HEAD_W), F32),
                   jax.ShapeDtypeStruct((seq, 4 * HEAD_W), F32), jax.ShapeDtypeStruct((seq, LANES), F32)],
        compiler_params=_params(("arbitrary",)),
        name="inproj_fwd",
    )(x, gain, scale, shift, w_rows)


def _outproj_fwd(y_attn, y_delta, w_out, x, gate1, gain, scale, shift):
    seq = x.shape[0]

    def body(ya_ref, yd_ref, wa_ref, wd_ref, x_ref, g1_ref, g_ref, sc_ref, sh_ref, x1_ref, h_ref, y_ref):
        y = _nn(ya_ref[...].astype(BF16), wa_ref[...]) + _nn(yd_ref[...], wd_ref[...])
        x1 = x_ref[...] + g1_ref[...] * y
        _, _, norm = _rms_fwd(x1, g_ref[...])
        x1_ref[...] = x1
        h_ref[...] = (norm * (1.0 + sc_ref[...]) + sh_ref[...]).astype(BF16)
        y_ref[...] = y.astype(BF16)

    def rows(width):
        return pl.BlockSpec((ROW_TILE, width), lambda i: (i, 0))

    return pl.pallas_call(
        body,
        grid=(seq // ROW_TILE,),
        in_specs=[rows(HEAD_W), rows(HEAD_W),
                  pl.BlockSpec((HEAD_W, D_MODEL), lambda i: (0, 0)), pl.BlockSpec((HEAD_W, D_MODEL), lambda i: (1, 0)),
                  rows(D_MODEL), _row(D_MODEL), _row(D_MODEL), _row(D_MODEL), _row(D_MODEL)],
        out_specs=[rows(D_MODEL), rows(D_MODEL), rows(D_MODEL)],
        out_shape=[jax.ShapeDtypeStruct((seq, D_MODEL), F32), jax.ShapeDtypeStruct((seq, D_MODEL), BF16),
                   jax.ShapeDtypeStruct((seq, D_MODEL), BF16)],
        compiler_params=_params(("arbitrary",)),
        name="outproj_fwd",
    )(y_attn, y_delta, w_out, w_out, x, gate1, gain, scale, shift)


def _ffn_fwd(h2, w_gate, w_up, w_down, x1, gate2, final_gain, target):
    seq = h2.shape[0]
    n_rows, n_ff = seq // ROW_TILE, D_FF // FF_TILE

    def body(h_ref, wg_ref, wu_ref, wd_ref, x1_ref, g2_ref, gf_ref, t_ref, gate_ref, up_ref, dx2_ref, st_ref, acc):
        i, j = pl.program_id(0), pl.program_id(1)

        @pl.when((i == 0) & (j == 0))
        def _():
            st_ref[...] = jnp.zeros_like(st_ref)

        h = h_ref[...]
        gate = _nt(h, wg_ref[...])
        up = _nt(h, wu_ref[...])
        gate_ref[...] = gate.astype(BF16)
        up_ref[...] = up.astype(BF16)
        part = _nn((_silu(gate) * up).astype(BF16), wd_ref[...])

        @pl.when(j == 0)
        def _():
            acc[...] = part

        @pl.when(j > 0)
        def _():
            acc[...] += part

        @pl.when(j == n_ff - 1)
        def _():
            y2 = acc[...]
            x2 = x1_ref[...] + g2_ref[...] * y2
            xhat, rstd, out = _rms_fwd(x2, gf_ref[...])
            diff = out - t_ref[...]
            dx2, dgain = _rms_bwd(diff * (1.0 / D_MODEL), xhat, rstd, gf_ref[...])
            dx2_ref[...] = dx2
            st_ref[0:1, :] += dgain
            st_ref[1:2, :] += jnp.sum(dx2 * y2, axis=0, keepdims=True)
            st_ref[2:3, :] += jnp.sum(diff * diff, axis=0, keepdims=True) * (0.5 / D_MODEL)

        @pl.when((i == n_rows - 1) & (j == n_ff - 1))
        def _():
            st_ref[3:4, :] = jnp.broadcast_to(jnp.sum(st_ref[2:3, :], keepdims=True), (1, D_MODEL))

    def rows(width):
        return pl.BlockSpec((ROW_TILE, width), lambda i, j: (i, 0))

    ff = pl.BlockSpec((ROW_TILE, FF_TILE), lambda i, j: (i, j))
    return pl.pallas_call(
        body,
        grid=(n_rows, n_ff),
        in_specs=[rows(D_MODEL),
                  pl.BlockSpec((FF_TILE, D_MODEL), lambda i, j: (j, 0)), pl.BlockSpec((FF_TILE, D_MODEL), lambda i, j: (j, 0)),
                  pl.BlockSpec((FF_TILE, D_MODEL), lambda i, j: (j, 0)),
                  rows(D_MODEL), _row(D_MODEL), _row(D_MODEL), rows(D_MODEL)],
        out_specs=[ff, ff, rows(D_MODEL), pl.BlockSpec((8, D_MODEL), lambda i, j: (0, 0))],
        out_shape=[jax.ShapeDtypeStruct((seq, D_FF), BF16), jax.ShapeDtypeStruct((seq, D_FF), BF16),
                   jax.ShapeDtypeStruct((seq, D_MODEL), F32), jax.ShapeDtypeStruct((8, D_MODEL), F32)],
        scratch_shapes=[pltpu.VMEM((ROW_TILE, D_MODEL), F32)],
        compiler_params=_params(("arbitrary", "arbitrary")),
        name="ffn_fwd",
    )(h2, w_gate, w_up, w_down, x1, gate2, final_gain, target)


def _ffn_bwd(dx2, gate, up, w_gate, w_up, w_down, x1, y, gate2, gate1, gain, scale):
    seq = dx2.shape[0]

    def act_body(dx2_ref, g2_ref, gate_ref, up_ref, wd_ref, dgate_ref, dup_ref, act_ref, dy2_ref):
        dy2 = (g2_ref[...] * dx2_ref[...]).astype(BF16)
        dy2_ref[...] = dy2
        gate = gate_ref[...].astype(F32)
        up = up_ref[...].astype(F32)
        dact = _nt(dy2, wd_ref[...])
        silu, slope = _silu_and_slope(gate)
        act_ref[...] = (silu * up).astype(BF16)
        dgate_ref[...] = (dact * up * slope).astype(BF16)
        dup_ref[...] = (dact * silu).astype(BF16)

    def rows2(width):
        return pl.BlockSpec((ROW_TILE, width), lambda i, j: (i, 0))

    ff = pl.BlockSpec((ROW_TILE, FF_TILE), lambda i, j: (i, j))
    dgate, dup, act, dy2 = pl.pallas_call(
        act_body,
        grid=(seq // ROW_TILE, D_FF // FF_TILE),
        in_specs=[rows2(D_MODEL), _row(D_MODEL), ff, ff, pl.BlockSpec((FF_TILE, D_MODEL), lambda i, j: (j, 0))],
        out_specs=[ff, ff, ff, rows2(D_MODEL)],
        out_shape=[jax.ShapeDtypeStruct((seq, D_FF), BF16)] * 3 + [jax.ShapeDtypeStruct((seq, D_MODEL), BF16)],
        compiler_params=_params(("arbitrary", "arbitrary")),
        name="ffn_bwd_act",
    )(dx2, gate2, gate, up, w_down)

    def in_body(dgate_ref, dup_ref, wg_ref, wu_ref, dx2_ref, x1_ref, y_ref, g1_ref, g_ref, sc_ref,
                dx1_ref, dy_ref, st_ref):
        @pl.when(pl.program_id(0) == 0)
        def _():
            st_ref[...] = jnp.zeros_like(st_ref)

        dh = _nn(dgate_ref[...], wg_ref[...]) + _nn(dup_ref[...], wu_ref[...])
        xhat, rstd, norm = _rms_fwd(x1_ref[...], g_ref[...])
        dxn, dgain = _rms_bwd(dh * (1.0 + sc_ref[...]), xhat, rstd, g_ref[...])
        dx1 = dx2_ref[...] + dxn
        dx1_ref[...] = dx1
        dy_ref[...] = (g1_ref[...] * dx1).astype(BF16)
        st_ref[0:1, :] += jnp.sum(dh, axis=0, keepdims=True)
        st_ref[1:2, :] += jnp.sum(dh * norm, axis=0, keepdims=True)
        st_ref[2:3, :] += dgain
        st_ref[3:4, :] += jnp.sum(dx1 * y_ref[...].astype(F32), axis=0, keepdims=True)

    half_tile = ROW_TILE // 2

    def rows(width):
        return pl.BlockSpec((half_tile, width), lambda i: (i, 0))

    whole = pl.BlockSpec((D_FF, D_MODEL), lambda i: (0, 0))
    dx1, dy, stats = pl.pallas_call(
        in_body,
        grid=(seq // half_tile,),
        in_specs=[rows(D_FF), rows(D_FF), whole, whole, rows(D_MODEL), rows(D_MODEL), rows(D_MODEL),
                  _row(D_MODEL), _row(D_MODEL), _row(D_MODEL)],
        out_specs=[rows(D_MODEL), rows(D_MODEL), pl.BlockSpec((8, D_MODEL), lambda i: (0, 0))],
        out_shape=[jax.ShapeDtypeStruct((seq, D_MODEL), F32), jax.ShapeDtypeStruct((seq, D_MODEL), BF16),
                   jax.ShapeDtypeStruct((8, D_MODEL), F32)],
        compiler_params=_params(("arbitrary",)),
        name="ffn_bwd_in",
    )(dgate, dup, w_gate, w_up, dx2, x1, y, gate1, gain, scale)
    return dgate, dup, act, dy2, dx1, dy, stats


def _outproj_bwd(dy, w_out):
    seq = dy.shape[0]

    def body(dy_ref, w_ref, out_ref):
        out_ref[...] = _nt(dy_ref[...], w_ref[...])

    rows = pl.BlockSpec((ROW_TILE, D_MODEL), lambda i: (i, 0))
    return pl.pallas_call(
        body,
        grid=(seq // ROW_TILE,),
        in_specs=[rows, pl.BlockSpec((D_MODEL, D_MODEL), lambda i: (0, 0))],
        out_specs=rows,
        out_shape=jax.ShapeDtypeStruct((seq, D_MODEL), F32),
        compiler_params=_params(("arbitrary",)),
        name="outproj_bwd",
    )(dy, w_out)


def _inproj_bwd(dq, dk, dv, dxd, dz, dba, w_rows, x, dx1, gain, scale, partials):
    seq = x.shape[0]
    n = len(partials)
    n_steps = seq // ROW_TILE

    def body(*refs):
        pieces, (w_ref, x_ref, dx1_ref, g_ref, sc_ref) = refs[:6], refs[6:11]
        gx_ref, st_ref = refs[11 + n:13 + n]
        riding = (refs[11:11 + n], refs[13 + n:13 + 2 * n], *refs[13 + 2 * n:])

        @pl.when(pl.program_id(0) == 0)
        def _():
            st_ref[...] = jnp.zeros_like(st_ref)
            for cp in (_scatter_copies(*riding) if n else []):
                cp.start()

        dh = _nn(jnp.concatenate([p[...].astype(BF16) for p in pieces], axis=1), w_ref[...])
        xhat, rstd, norm = _rms_fwd(x_ref[...], g_ref[...])
        dxn, dgain = _rms_bwd(dh * (1.0 + sc_ref[...]), xhat, rstd, g_ref[...])
        gx_ref[...] = dx1_ref[...] + dxn
        st_ref[0:1, :] += jnp.sum(dh, axis=0, keepdims=True)
        st_ref[1:2, :] += jnp.sum(dh * norm, axis=0, keepdims=True)
        st_ref[2:3, :] += dgain

        if n:
            @pl.when(pl.program_id(0) == n_steps - 1)
            def _():
                for cp in _scatter_copies(*riding):
                    cp.wait()

    def rows(width):
        return pl.BlockSpec((ROW_TILE, width), lambda i: (i, 0))

    sems = [pltpu.SemaphoreType.DMA((3 * n,)), pltpu.SemaphoreType.DMA((3 * n,))] if n else []
    return pl.pallas_call(
        body,
        grid=(n_steps,),
        in_specs=[rows(HEAD_W), rows(HEAD_W), rows(HEAD_W), rows(3 * HEAD_W), rows(HEAD_W), rows(LANES),
                  pl.BlockSpec(w_rows.shape, lambda i: (0, 0)), rows(D_MODEL), rows(D_MODEL), _row(D_MODEL),
                  _row(D_MODEL)]
        + [ANY] * n,
        out_specs=[rows(D_MODEL), pl.BlockSpec((8, D_MODEL), lambda i: (0, 0))] + [ANY] * n,
        out_shape=[jax.ShapeDtypeStruct((seq, D_MODEL), F32), jax.ShapeDtypeStruct((8, D_MODEL), F32)]
        + [jax.ShapeDtypeStruct(p.shape, p.dtype) for p in partials],
        scratch_shapes=sems,
        compiler_params=_params(("arbitrary",)),
        name="inproj_bwd",
    )(dq, dk, dv, dxd, dz, dba, w_rows, x, dx1, gain, scale, *partials)


def _weight_grad(a, b, name):
    seq, m = a.shape
    n = b.shape[1]
    tm = m if m <= 1536 else m // 2
    tn = n if n <= 1536 else n // 2
    rows = 2 * WGRAD_ROWS
    n_k = seq // rows

    def body(a_ref, b_ref, out_ref):
        part = _tn(a_ref[...].astype(BF16), b_ref[...].astype(BF16))

        @pl.when(pl.program_id(2) == 0)
        def _():
            out_ref[...] = part

        @pl.when(pl.program_id(2) > 0)
        def _():
            out_ref[...] += part

    return pl.pallas_call(
        body,
        grid=(m // tm, n // tn, n_k),
        in_specs=[pl.BlockSpec((rows, tm), lambda i, j, k: (k, i)),
                  pl.BlockSpec((rows, tn), lambda i, j, k: (k, j))],
        out_specs=pl.BlockSpec((tm, tn), lambda i, j, k: (i, j)),
        out_shape=jax.ShapeDtypeStruct((m, n), F32),
        compiler_params=_params(("arbitrary", "arbitrary", "arbitrary")),
        name=name,
    )(a, b)


def _weight_grad_stack(pieces, b, name):
    seq, n = b.shape
    widths = [a.shape[1] for a in pieces]
    starts = [sum(widths[:i]) for i in range(len(pieces))]

    def body(*refs):
        a_refs, b_ref, out_ref = refs[:len(pieces)], refs[len(pieces)], refs[len(pieces) + 1]

        @pl.when(pl.program_id(0) == 0)
        def _():
            out_ref[...] = jnp.zeros_like(out_ref)

        bb = b_ref[...].astype(BF16)
        for a_ref, start, width in zip(a_refs, starts, widths):
            out_ref[start:start + width, :] += _tn(a_ref[...].astype(BF16), bb)

    def rows(width):
        return pl.BlockSpec((WGRAD_ROWS, width), lambda k: (k, 0))

    return pl.pallas_call(
        body,
        grid=(seq // WGRAD_ROWS,),
        in_specs=[rows(w) for w in widths] + [rows(n)],
        out_specs=pl.BlockSpec((sum(widths), n), lambda k: (0, 0)),
        out_shape=jax.ShapeDtypeStruct((sum(widths), n), F32),
        compiler_params=_params(("arbitrary",)),
        name=name,
    )(*pieces, b)


def _adamw(w, g, m, v, name):
    n_rows, n_cols = w.shape
    if w.size <= 64 * 1024:
        block, grid, index = (n_rows, n_cols), (1,), lambda i: (0, 0)
    elif n_rows % 256 == 0:
        block, grid, index = (256, n_cols), (n_rows // 256,), lambda i: (i, 0)
    elif n_cols % 256 == 0:
        block, grid, index = (n_rows, 256), (n_cols // 256,), lambda i: (0, i)
    else:
        block, grid, index = (n_rows, n_cols), (1,), lambda i: (0, 0)

    def body(w_ref, g_ref, m_ref, v_ref, d_ref, nm_ref, nv_ref):
        gv = g_ref[...]
        nm = ADAM_B1 * m_ref[...] + (1.0 - ADAM_B1) * gv
        nv = ADAM_B2 * v_ref[...] + (1.0 - ADAM_B2) * (gv * gv)
        m_hat = nm / (1.0 - ADAM_B1 ** ADAM_STEP)
        v_hat = nv / (1.0 - ADAM_B2 ** ADAM_STEP)
        d_ref[...] = -ADAM_LR * (m_hat / (jnp.sqrt(v_hat) + ADAM_EPS) + ADAM_WD * w_ref[...])
        nm_ref[...] = nm
        nv_ref[...] = nv

    blk = pl.BlockSpec(block, index)
    shape = jax.ShapeDtypeStruct((n_rows, n_cols), F32)
    return pl.pallas_call(
        body,
        grid=grid,
        in_specs=[blk] * 4,
        out_specs=[blk] * 3,
        out_shape=[shape] * 3,
        compiler_params=_params(("arbitrary",)),
        name=name,
    )(w, g, m, v)


IN_WIDTH = 3600


def _local_step(x, target, mod, norm_attn_g, w_in, rel_bias, conv_w, a_log, dt_bias, delta_norm_g,
                norm_ffn_g, final_norm_g, shards, assemble, reduce_pairs):
    sh1, sc1, g1, sh2, sc2, g2 = [mod[:, i * D_MODEL:(i + 1) * D_MODEL] for i in range(6)]
    w_rows = jnp.pad(w_in, ((0, IN_SPLITS[-1] - IN_WIDTH), (0, 0)))
    tables = jnp.asarray(_attn_tables())
    alog_row = jnp.pad(a_log, ((0, 0), (N_HEADS, LANES - 2 * N_HEADS)))
    dt_row = jnp.pad(dt_bias, ((0, 0), (N_HEADS, LANES - 2 * N_HEADS)))
    gain_row = jnp.tile(delta_norm_g, (1, N_HEADS))

    h1, qkv_a, qkvz, ba = _inproj_fwd(x, norm_attn_g, sc1, sh1, w_rows)
    bias = _attention_bias(rel_bias, tables)
    y_attn, lse, *gathered = _attention_fwd(qkv_a, bias, shards)
    w_out, w_gate, w_up, w_down = assemble(gathered)
    xs = _delta_prep_fwd(qkvz, ba, conv_w, alog_row, dt_row)
    inv_h, qk_h, u_h, w_h = _delta_chunk_fwd(xs)
    o, st_h = _delta_scan_fwd(xs, qk_h, u_h, w_h)
    y_delta = _delta_post_fwd(o, qkvz, gain_row)
    x1, h2, y = _outproj_fwd(y_attn, y_delta, w_out, x, g1, norm_ffn_g, sc2, sh2)
    gate, up, dx2, st_f = _ffn_fwd(h2, w_gate, w_up, w_down, x1, g2, final_norm_g, target)

    dgate, dup, act, dy2, dx1, dy, st_b = _ffn_bwd(dx2, gate, up, w_gate, w_up, w_down, x1, y, g2, g1, norm_ffn_g, sc2)
    partials = reduce_pairs([_weight_grad_stack([y_attn, y_delta], dy, "wgrad_out"),
                             _weight_grad(dgate, h2, "wgrad_gate"), _weight_grad(dup, h2, "wgrad_up"),
                             _weight_grad(act, dy2, "wgrad_down")], 1, "rest")
    grads = {}
    dycat = _outproj_bwd(dy, w_out)
    do, dz, dgain = _delta_post_bwd(dycat, o, qkvz, gain_row)
    dsn_h, dvn_h = _delta_scan_bwd(xs, qk_h, w_h, do)
    dxs = _delta_chunk_bwd(xs, inv_h, u_h, w_h, st_h, dsn_h, dvn_h, do)
    dconv, dba, dvec = _delta_prep_bwd(qkvz, ba, conv_w, alog_row, dt_row, dxs)
    dxd, grads["conv_w"] = _conv_bwd(dconv, qkvz, conv_w)
    dq, dk, dv, dbias, *scattered = _attention_bwd(qkv_a, dycat, y_attn, lse, bias, partials)
    partials_in = reduce_pairs([jnp.concatenate(
        [_weight_grad_stack([dq, dk, dv], h1, "wgrad_in_attn"),
         _weight_grad_stack([dxd, dz, dba], h1, "wgrad_in_delta")[:IN_WIDTH - 3 * HEAD_W]], axis=0)], 0, "in")
    grad_x, st_i, *scattered_in = _inproj_bwd(dq, dk, dv, dxd, dz, dba, w_rows, x, dx1, norm_attn_g, sc1,
                                              partials_in)
    grads["rel_bias"] = _rel_bias_grad(dbias, tables)[:, :N_BUCKETS].T
    grads["a_log"] = dvec[0:1, N_HEADS:2 * N_HEADS]
    grads["dt_bias"] = dvec[1:2, N_HEADS:2 * N_HEADS]
    grads["delta_norm_g"] = dgain[1:2, :HEAD_DIM]
    grads["norm_attn_g"] = st_i[2:3]
    grads["norm_ffn_g"] = st_b[2:3]
    grads["final_norm_g"] = st_f[0:1]
    dmod = jnp.concatenate([st_i[0:1], st_i[1:2], st_b[3:4], st_b[0:1], st_b[1:2], st_f[1:2]], axis=1)
    return st_f[3, 0], grad_x, grads, dmod, (partials_in + partials, scattered_in + scattered)


MESH = pl.DeviceIdType.MESH
OTHER_CHIPS = ((1, 0), (0, 1), (1, 1))
ALL_PEERS = tuple((m >> 2 & 1, m >> 1 & 1, m & 1) for m in range(1, 8))
ANY = pl.BlockSpec(memory_space=pl.ANY)
VMEM_SPEC = pl.BlockSpec(memory_space=pltpu.VMEM)


def _me():
    return lax.axis_index("x"), lax.axis_index("y"), lax.axis_index("c")


def _flip(pos, mask):
    return tuple(1 - p if m else p for p, m in zip(pos, mask))


def _remote(src, dst, send_sems, recv_sems, k, to):
    return pltpu.make_async_remote_copy(src_ref=src, dst_ref=dst, send_sem=send_sems.at[k], recv_sem=recv_sems.at[k],
                                        device_id=to, device_id_type=MESH)


def _ada_exchange(c8, w_ada, b_ada, conv8, shard):
    def body(c_ref, w_ref, b_ref, cv_ref, shard_ref, mod_ref, cact_ref, conv_ref, whole_ref,
             c_all, part_all, send_sems, recv_sems, ride_send, ride_recv):
        x, y, c = me = _me()
        dev = 4 * x + 2 * y + c
        chip = 2 * x + y
        riding = ([shard_ref], [whole_ref], ride_send, ride_recv)
        for cp in _gather_copies(*riding, hand_over=False)[0]:
            cp.start()
        c_all[dev] = c_ref[...]
        conv_ref[chip] = cv_ref[...]
        first = [_remote(c_ref, c_all.at[dev], send_sems, recv_sems, k, _flip(me, mask))
                 for k, mask in enumerate(ALL_PEERS)]
        first += [_remote(cv_ref, conv_ref.at[chip], send_sems, recv_sems, 7 + j, _flip(me, (*mask, 0)))
                  for j, mask in enumerate(OTHER_CHIPS)]
        for cp in first:
            cp.start()
        for cp in first:
            cp.wait()
        row = lax.broadcasted_iota(jnp.int32, (8, D_MODEL), 0)
        c_rows = jnp.zeros((8, D_MODEL), F32)
        for d in range(8):
            c_rows = jnp.where(row == d, c_all[d], c_rows)
        c_act = _silu(c_rows)
        cact_ref[...] = c_act
        part_all[chip] = _nn(c_act, w_ref[...], HIGHEST)
        second = [_remote(part_all.at[chip], part_all.at[chip], send_sems, recv_sems, 10 + j, _flip(me, (*mask, 0)))
                  for j, mask in enumerate(OTHER_CHIPS)]
        for cp in second:
            cp.start()
        for cp in second:
            cp.wait()
        cols = w_ref.shape[1]
        for k in range(4):
            mod_ref[:, k * cols:(k + 1) * cols] = part_all[k] + b_ref[:, k * cols:(k + 1) * cols]
        first, passed = _gather_copies(*riding)
        for cp, fwd in zip(first, passed):
            cp.wait_recv()
            fwd.start()
        for cp in first:
            cp.wait_send()
        for fwd in passed:
            fwd.wait()

    cols = w_ada.shape[1]
    return pl.pallas_call(
        body,
        in_specs=[VMEM_SPEC] * 4 + [ANY],
        out_specs=[VMEM_SPEC] * 3 + [ANY],
        out_shape=[jax.ShapeDtypeStruct((8, 4 * cols), F32), jax.ShapeDtypeStruct((8, D_MODEL), F32),
                   jax.ShapeDtypeStruct((4, 8, conv8.shape[1]), F32)] + _gathered_shapes([shard]),
        scratch_shapes=[pltpu.VMEM((8, 8, D_MODEL), F32), pltpu.VMEM((4, 8, cols), F32),
                        pltpu.SemaphoreType.DMA((13,)), pltpu.SemaphoreType.DMA((13,)),
                        pltpu.SemaphoreType.DMA((6,)), pltpu.SemaphoreType.DMA((6,))],
        compiler_params=pltpu.CompilerParams(vmem_limit_bytes=VMEM_LIMIT),
        name="ada_exchange",
    )(c8, w_ada, b_ada, conv8, shard)


def _gathered_shapes(shards):
    return [jax.ShapeDtypeStruct((4, *s.shape), s.dtype) for s in shards]


def _gather_copies(srcs, dsts, send_sems, recv_sems, hand_over=True):
    x, y, c = me = _me()
    chip = 2 * x + y
    sibling = _flip(me, (0, 0, 1))
    first, passed = [], []
    for a, (src, dst) in enumerate(zip(srcs, dsts)):
        for j, mask in enumerate(OTHER_CHIPS):
            to = _flip(me, (*mask, 0))
            first.append(_remote(src.at[c], dst.at[chip, c], send_sems, recv_sems, 6 * a + j, to))
            if hand_over:
                landed = dst.at[2 * to[0] + to[1], c]
                passed.append(_remote(landed, landed, send_sems, recv_sems, 6 * a + 3 + j, sibling))
    return first, passed


def _scatter_copies(srcs, dsts, send_sems, recv_sems):
    x, y, c = me = _me()
    chip = 2 * x + y
    copies = []
    for a, (src, dst) in enumerate(zip(srcs, dsts)):
        for j, mask in enumerate(OTHER_CHIPS):
            to = _flip(me, (*mask, 0))
            copies.append(_remote(src.at[2 * to[0] + to[1]], dst.at[chip], send_sems, recv_sems, 3 * a + j, to))
    return copies


def _start_and_wait(copies):
    for cp in copies:
        cp.start()
    for cp in copies:
        cp.wait()


def _swap_halves(grads):
    n = len(grads)

    def body(*refs):
        srcs, got = refs[:n], refs[n:2 * n]
        send_sems, recv_sems = refs[2 * n:]
        x, y, c = me = _me()
        _start_and_wait([_remote(srcs[a].at[:, 1 - c], got[a], send_sems, recv_sems, a, _flip(me, (0, 0, 1)))
                         for a in range(n)])

    return pl.pallas_call(
        body,
        in_specs=[ANY] * n,
        out_specs=[ANY] * n,
        out_shape=[jax.ShapeDtypeStruct((4, g.shape[2], g.shape[3]), g.dtype) for g in grads],
        scratch_shapes=[pltpu.SemaphoreType.DMA((n,)), pltpu.SemaphoreType.DMA((n,))],
        name=f"swap_halves_{n}",
    )(*grads)


def _join_halves(halves):
    n = len(halves)

    def body(*refs):
        srcs, dsts = refs[:n], refs[n:2 * n]
        send_sems, recv_sems = refs[2 * n:]
        x, y, c = me = _me()
        _start_and_wait([_remote(srcs[a], dsts[a].at[c], send_sems, recv_sems, a, _flip(me, (0, 0, 1)))
                         for a in range(n)])

    return pl.pallas_call(
        body,
        in_specs=[ANY] * n,
        out_specs=[ANY] * n,
        out_shape=[jax.ShapeDtypeStruct((2, *h.shape), h.dtype) for h in halves],
        scratch_shapes=[pltpu.SemaphoreType.DMA((n,)), pltpu.SemaphoreType.DMA((n,))],
        name=f"join_halves_{n}",
    )(*halves)


def _gather_small(packed):
    n_rows = packed.shape[0]

    def body(p_ref, all_ref, sum_ref, send_sems, recv_sems):
        x, y, c = me = _me()
        dev = 4 * x + 2 * y + c
        all_ref[dev] = p_ref[...]
        copies = [_remote(p_ref, all_ref.at[dev], send_sems, recv_sems, k, _flip(me, mask))
                  for k, mask in enumerate(ALL_PEERS)]
        for cp in copies:
            cp.start()
        for cp in copies:
            cp.wait()
        total = all_ref[0]
        for d in range(1, 8):
            total = total + all_ref[d]
        sum_ref[...] = total

    return pl.pallas_call(
        body,
        in_specs=[VMEM_SPEC],
        out_specs=[VMEM_SPEC, VMEM_SPEC],
        out_shape=[jax.ShapeDtypeStruct((8, n_rows, LANES), F32), jax.ShapeDtypeStruct((n_rows, LANES), F32)],
        scratch_shapes=[pltpu.SemaphoreType.DMA((7,)), pltpu.SemaphoreType.DMA((7,))],
        name="gather_small",
    )(packed)


def _add_pair(a, b, out_dtype, name):
    def body(a_ref, b_ref, o_ref):
        o_ref[...] = (a_ref[...] + b_ref[...]).astype(o_ref.dtype)

    blk = pl.BlockSpec((1, *a.shape[1:]), lambda i: (i, 0, 0))
    return pl.pallas_call(
        body, grid=(a.shape[0],), in_specs=[blk, blk], out_specs=blk,
        out_shape=jax.ShapeDtypeStruct(a.shape, out_dtype),
        compiler_params=_params(("arbitrary",)), name=name,
    )(a, b)


def _add_slots(a, name):
    def body(a_ref, o_ref):
        total = a_ref[0].astype(F32)
        for k in range(1, 4):
            total = total + a_ref[k].astype(F32)
        o_ref[...] = total

    return pl.pallas_call(
        body, in_specs=[VMEM_SPEC], out_specs=VMEM_SPEC,
        out_shape=jax.ShapeDtypeStruct(a.shape[1:], F32),
        compiler_params=pltpu.CompilerParams(vmem_limit_bytes=VMEM_LIMIT), name=name,
    )(a)


def _ada_weight_grad(c_act, dmod_cols):
    def body(c_ref, d_ref, o_ref):
        o_ref[...] = _tn(c_ref[...], d_ref[...], HIGHEST)

    return pl.pallas_call(
        body, in_specs=[VMEM_SPEC, VMEM_SPEC], out_specs=VMEM_SPEC,
        out_shape=jax.ShapeDtypeStruct((c_act.shape[1], dmod_cols.shape[1]), F32),
        compiler_params=pltpu.CompilerParams(vmem_limit_bytes=VMEM_LIMIT), name="ada_weight_grad",
    )(c_act, dmod_cols)


def kernel(x, c, w_ada, b_ada, norm_attn_g, w_in, rel_bias, conv_w, a_log, dt_bias, delta_norm_g, w_out, norm_ffn_g, w_gate, w_up, w_down, final_norm_g, loss_target, m_w_ada, m_b_ada, m_norm_attn_g, m_w_in, m_rel_bias, m_conv_w, m_a_log, m_dt_bias, m_delta_norm_g, m_w_out, m_norm_ffn_g, m_w_gate, m_w_up, m_w_down, m_final_norm_g, v_w_ada, v_b_ada, v_norm_attn_g, v_w_in, v_rel_bias, v_conv_w, v_a_log, v_dt_bias, v_delta_norm_g, v_w_out, v_norm_ffn_g, v_w_gate, v_w_up, v_w_down, v_final_norm_g):
    xi, yi, ci = _me()
    dev = 4 * xi + 2 * yi + ci
    chip = 2 * xi + yi

    big_names = ("w_in", "w_out", "w_gate", "w_up", "w_down")
    by_cols = (True, False, True, True, False)

    def rows_form(a, cols):
        return jnp.swapaxes(a[0], 0, 1) if cols else a[0]

    def halves_form(w):
        rows, lanes = w.shape
        if (rows // 2) % 16:
            rows, lanes = w.size // LANES, LANES
        return (2, rows // 2, lanes)

    big = [rows_form(w, cols) for w, cols in zip((w_in, w_out, w_gate, w_up, w_down), by_cols)]
    shards = [w.astype(BF16).reshape(halves_form(w)) for w in big]

    def assemble(gathered, first):
        return [lax.dynamic_update_index_in_dim(g, s, chip, 0).reshape(4 * w.shape[0], w.shape[1])
                for g, s, w in zip(gathered, shards[first:], big[first:])]

    def reduce_pairs(grads, first, tag):
        slots = [g.reshape(4, *halves_form(w)) for g, w in zip(grads, big[first:])]
        return [_add_pair(lax.dynamic_index_in_dim(s, ci, 1, keepdims=False), got, BF16, f"add_pair_{tag}{a}")
                for a, (s, got) in enumerate(zip(slots, _swap_halves(slots)))]

    def finish(partials, scattered, first, tag):
        by_source = [lax.dynamic_update_index_in_dim(b, lax.dynamic_index_in_dim(p, chip, 0, keepdims=False), chip, 0)
                     for b, p in zip(scattered, partials)]
        halves = [_add_slots(p, f"add_slots_{tag}{a}") for a, p in enumerate(by_source)]
        joined = [lax.dynamic_update_index_in_dim(j, h, ci, 0) for j, h in zip(_join_halves(halves), halves)]
        return [j.reshape(w.shape) for j, w in zip(joined, big[first:])]

    conv_cols = conv_w.shape[2]
    mod_all, c_act, conv_all, gathered_in = _ada_exchange(
        jnp.broadcast_to(c, (8, D_MODEL)), w_ada[0], b_ada, jnp.pad(conv_w[0], ((0, 4), (0, 0))), shards[0])
    mod = lax.dynamic_slice_in_dim(mod_all, dev, 1, axis=0)
    conv_full = jnp.swapaxes(conv_all[:, :4, :], 0, 1).reshape(4, 4 * conv_cols)
    whole_in, = assemble([gathered_in], 0)
    loss, grad_x, grads, dmod, (partials, scattered) = _local_step(
        x[0], loss_target[0], mod, norm_attn_g, whole_in, rel_bias, conv_full, a_log, dt_bias, delta_norm_g,
        norm_ffn_g, final_norm_g[None], shards[1:], functools.partial(assemble, first=1), reduce_pairs)

    big_grads = finish(partials, scattered, 0, "all")

    pieces = [dmod, grads["conv_w"], grads["norm_attn_g"], grads["norm_ffn_g"], grads["final_norm_g"],
              grads["rel_bias"], grads["a_log"], grads["dt_bias"], grads["delta_norm_g"]]
    flat = [jnp.pad(p.reshape(-1), (0, -p.size % LANES)) for p in pieces]
    n_rows = [f.size // LANES for f in flat]
    packed = jnp.concatenate(flat).reshape(-1, LANES)
    packed = jnp.pad(packed, ((0, -packed.shape[0] % 8), (0, 0)))
    all_small, total = _gather_small(packed)
    sums, start = [], 0
    for p, n in zip(pieces, n_rows):
        sums.append(total[start:start + n].reshape(-1)[:p.size].reshape(p.shape))
        start += n
    g_b_ada, g_conv, g_norm_attn, g_norm_ffn, g_final, g_rel, g_alog, g_dt, g_dnorm = sums
    dmod_all = all_small[:, :n_rows[0], :].reshape(8, -1)
    ada_cols = w_ada.shape[2]
    g_w_ada = _ada_weight_grad(c_act, lax.dynamic_slice_in_dim(dmod_all, chip * ada_cols, ada_cols, axis=1))
    g_conv = lax.dynamic_slice_in_dim(g_conv, chip * conv_cols, conv_cols, axis=1)

    grad = {"w_ada": g_w_ada[None], "b_ada": g_b_ada, "norm_attn_g": g_norm_attn,
            "rel_bias": g_rel, "conv_w": g_conv[None], "a_log": g_alog, "dt_bias": g_dt, "delta_norm_g": g_dnorm,
            "norm_ffn_g": g_norm_ffn, "final_norm_g": g_final.reshape(-1)}
    weight = {"w_ada": w_ada, "b_ada": b_ada, "norm_attn_g": norm_attn_g, "w_in": w_in, "rel_bias": rel_bias,
              "conv_w": conv_w, "a_log": a_log, "dt_bias": dt_bias, "delta_norm_g": delta_norm_g, "w_out": w_out,
              "norm_ffn_g": norm_ffn_g, "w_gate": w_gate, "w_up": w_up, "w_down": w_down, "final_norm_g": final_norm_g}
    first = {"w_ada": m_w_ada, "b_ada": m_b_ada, "norm_attn_g": m_norm_attn_g, "w_in": m_w_in, "rel_bias": m_rel_bias,
             "conv_w": m_conv_w, "a_log": m_a_log, "dt_bias": m_dt_bias, "delta_norm_g": m_delta_norm_g,
             "w_out": m_w_out, "norm_ffn_g": m_norm_ffn_g, "w_gate": m_w_gate, "w_up": m_w_up, "w_down": m_w_down,
             "final_norm_g": m_final_norm_g}
    second = {"w_ada": v_w_ada, "b_ada": v_b_ada, "norm_attn_g": v_norm_attn_g, "w_in": v_w_in, "rel_bias": v_rel_bias,
              "conv_w": v_conv_w, "a_log": v_a_log, "dt_bias": v_dt_bias, "delta_norm_g": v_delta_norm_g,
              "w_out": v_w_out, "norm_ffn_g": v_norm_ffn_g, "w_gate": v_w_gate, "w_up": v_w_up, "w_down": v_w_down,
              "final_norm_g": v_final_norm_g}
    delta, new_m, new_v = {}, {}, {}
    for name, w in weight.items():
        if name in big_names:
            continue
        two_d = (-1, w.shape[-1])
        d, nm, nv = _adamw(w.reshape(two_d), grad[name].reshape(two_d), first[name].reshape(two_d),
                           second[name].reshape(two_d), f"adamw_{name}")
        delta[name], new_m[name], new_v[name] = d.reshape(w.shape), nm.reshape(w.shape), nv.reshape(w.shape)
    for name, w, g, cols in zip(big_names, big, big_grads, by_cols):
        outs = _adamw(w, g, rows_form(first[name], cols), rows_form(second[name], cols), f"adamw_{name}")
        grad[name], delta[name], new_m[name], new_v[name] = [
            (jnp.swapaxes(o, 0, 1) if cols else o)[None] for o in (g, *outs)]

    names = list(weight)
    return (lax.psum(loss, ("x", "y", "c")), grad_x[None], *[grad[n] for n in names], *[delta[n] for n in names],
            *[new_m[n] for n in names], *[new_v[n] for n in names])
```

```python
import functools
import math

import numpy as np
import jax
import jax.numpy as jnp
from jax import lax
from jax.experimental import pallas as pl
from jax.experimental.pallas import tpu as pltpu

F32 = jnp.float32
BF16 = jnp.bfloat16
HIGHEST = lax.Precision.HIGHEST

D_MODEL = 1024
HEAD_DIM = 64
N_HEADS = 8
HEAD_W = 512
BRANCHES = ((128, 1), (512, 4), (2048, 16))
BAND = 128
ATT_TILE = 2048
ATT_UNROLL = 8
ATT_UNROLL_BWD = 4
N_BUCKETS = 32
MAX_DISTANCE = 2048
CHUNK = 64
D_FF = 2816
EPS = 1e-6
NEG_INF = -1e30
LANES = 128
VMEM_LIMIT = 56 * 1024 * 1024

ADAM_LR = 0.001
ADAM_B1 = 0.9
ADAM_B2 = 0.999
ADAM_EPS = 1e-08
ADAM_WD = 0.01
ADAM_STEP = 10


def _nn(a, b, precision=None):
    return jnp.dot(a, b, preferred_element_type=F32, precision=precision)


def _nt(a, b, precision=None):
    return lax.dot_general(a, b, (((1,), (1,)), ((), ())), preferred_element_type=F32, precision=precision)


def _tn(a, b, precision=None):
    return lax.dot_general(a, b, (((0,), (0,)), ((), ())), preferred_element_type=F32, precision=precision)


def _params(sem, vmem=VMEM_LIMIT):
    return pltpu.CompilerParams(dimension_semantics=sem, vmem_limit_bytes=vmem)


def _sigmoid(x):
    return 0.5 * jnp.tanh(0.5 * x) + 0.5


def _silu_and_slope(x):
    s = _sigmoid(x)
    return x * s, s * (1.0 + x * (1.0 - s))


def _silu(x):
    return x * _sigmoid(x)


def _attn_tables():
    qi = np.arange(BAND)[:, None]
    kj = np.arange(2 * BAND)[None, :]
    steps = qi + BAND - kj
    in_window = (steps >= 0) & (steps <= BAND)
    max_exact = N_BUCKETS // 2
    out = np.zeros((3, 2, BAND, 2 * BAND), np.int32)
    for b, (_, dil) in enumerate(BRANCHES):
        dist = np.maximum(steps, 0) * dil
        dist_f = np.maximum(dist, 1).astype(np.float32)
        large = max_exact + (np.log(dist_f / np.float32(max_exact)) / np.float32(math.log(MAX_DISTANCE / max_exact))
                             * np.float32(N_BUCKETS - max_exact)).astype(np.int32)
        bucket = np.where(dist < max_exact, dist, np.minimum(large, N_BUCKETS - 1)).astype(np.int32)
        out[b, 0] = np.where(in_window, bucket, -1)
        out[b, 1] = np.where(in_window & (kj >= BAND), bucket, -1)
    return out


def _attention_bias(rel_bias, tables):
    def body(rel_ref, tab_ref, out_ref):
        head = pl.program_id(0)
        for b in range(3):
            tab = tab_ref[b, 0]

            def pick(kk, acc, tab=tab):
                return jnp.where(tab == kk, rel_ref[kk, head], acc)

            acc = lax.fori_loop(0, N_BUCKETS, pick, jnp.zeros((BAND, 2 * BAND), F32))
            for first in range(2):
                out_ref[0, b, first] = jnp.where(tab_ref[b, first] < 0, NEG_INF, acc)

    return pl.pallas_call(
        body,
        grid=(N_HEADS,),
        in_specs=[pl.BlockSpec(memory_space=pltpu.SMEM),
                  pl.BlockSpec((3, 2, BAND, 2 * BAND), lambda h: (0, 0, 0, 0))],
        out_specs=pl.BlockSpec((1, 3, 2, BAND, 2 * BAND), lambda h: (h, 0, 0, 0, 0)),
        out_shape=jax.ShapeDtypeStruct((N_HEADS, 3, 2, BAND, 2 * BAND), F32),
        compiler_params=_params(("arbitrary",)),
        name="attn_bias",
    )(rel_bias, tables)


def _bias_spec():
    return pl.BlockSpec((2, 3, 2, BAND, 2 * BAND), lambda p, t: (p, 0, 0, 0, 0))


def _attn_block_index(idx, t, r):
    nb = ATT_TILE // (BAND * r)
    rho = idx // nb
    n = idx % nb
    qs = rho + r * BAND * n
    gs = t * ATT_TILE + qs
    first = (t * nb + n) == 0
    ps = jnp.where(first, gs, gs - r * BAND)
    return qs, gs, ps, first.astype(jnp.int32)


def _rows(start, r):
    return pl.ds(start, BAND) if r == 1 else pl.ds(start, BAND, stride=r)


def _attention_fwd(qkv, bias, shards):
    seq = qkv.shape[0]
    n_tiles = seq // ATT_TILE
    n = len(shards)

    def body(*refs):
        bias_ref, q_ref, k_ref, v_ref = refs[:4]
        y_ref, lse_ref = refs[4 + n:6 + n]
        o_s, l_s = refs[6 + 2 * n:8 + 2 * n]
        riding = (refs[4:4 + n], refs[6 + n:6 + 2 * n], *refs[8 + 2 * n:])
        pair = pl.program_id(0)
        t = pl.program_id(1)
        if n:
            @pl.when((pair == 0) & (t == 0))
            def _():
                for cp in _gather_copies(*riding, hand_over=False)[0]:
                    cp.start()

            @pl.when((pair == 2) & (t == 0))
            def _():
                for cp, fwd in zip(*_gather_copies(*riding)):
                    cp.wait_recv()
                    fwd.start()

        lane = lax.broadcasted_iota(jnp.int32, (1, LANES), 1)
        head0 = lane < HEAD_DIM
        masks = (head0, jnp.logical_not(head0))
        ones = jnp.ones((2 * BAND, LANES), BF16)
        for b, (_, r) in enumerate(BRANCHES):
            def blocks(it, carry, b=b, r=r):
                idx = [_attn_block_index(it * ATT_UNROLL + j, t, r) for j in range(ATT_UNROLL)]
                qb = [q_ref[_rows(qs, r), :] * (HEAD_DIM ** -0.5) for qs, _, _, _ in idx]
                kcat = [jnp.concatenate([k_ref[_rows(ps, r), :], k_ref[_rows(gs, r), :]], axis=0).astype(BF16)
                        for _, gs, ps, _ in idx]
                vcat = [jnp.concatenate([v_ref[_rows(ps, r), :], v_ref[_rows(gs, r), :]], axis=0).astype(BF16)
                        for _, gs, ps, _ in idx]
                work = [(j, hh) for j in range(ATT_UNROLL) for hh in range(2)]
                s = [_nt(jnp.where(masks[hh], qb[j], 0.0).astype(BF16), kcat[j]) + bias_ref[hh, b, idx[j][3]]
                     for j, hh in work]
                m = [jnp.max(sv, axis=-1, keepdims=True) for sv in s]
                e = [jnp.exp(sv - mv) for sv, mv in zip(s, m)]
                eb = [ev.astype(BF16) for ev in e]
                den = [_nn(ev, ones) for ev in eb]
                out = [_nn(ev, vcat[j]) / dv for ev, dv, (j, _) in zip(eb, den, work)]
                lse = [mv + jnp.log(dv) for mv, dv in zip(m, den)]
                for j in range(ATT_UNROLL):
                    o_s[b, _rows(idx[j][0], r), :] = jnp.where(head0, out[2 * j], out[2 * j + 1])
                    l_s[b, _rows(idx[j][0], r), :] = jnp.where(head0, lse[2 * j], lse[2 * j + 1])
                return carry

            lax.fori_loop(0, ATT_TILE // BAND // ATT_UNROLL, blocks, 0)

        def merge(i, carry):
            rows = pl.ds(pl.multiple_of(i * BAND, BAND), BAND)
            l0, l1, l2 = l_s[0, rows, :], l_s[1, rows, :], l_s[2, rows, :]
            m = jnp.maximum(jnp.maximum(l0, l1), l2)
            w0, w1, w2 = jnp.exp(l0 - m), jnp.exp(l1 - m), jnp.exp(l2 - m)
            tot = w0 + w1 + w2
            y_ref[rows, :] = (w0 * o_s[0, rows, :] + w1 * o_s[1, rows, :] + w2 * o_s[2, rows, :]) / tot
            lse_ref[rows, :] = m + jnp.log(tot)
            return carry

        lax.fori_loop(0, ATT_TILE // BAND, merge, 0)

        if n:
            @pl.when((pair == N_HEADS // 2 - 1) & (t == n_tiles - 1))
            def _():
                first, passed = _gather_copies(*riding)
                for cp in first:
                    cp.wait_send()
                for fwd in passed:
                    fwd.wait()

    tile = pl.BlockSpec((ATT_TILE, LANES), lambda p, t: (t, p))
    sems = [pltpu.SemaphoreType.DMA((6 * n,)), pltpu.SemaphoreType.DMA((6 * n,))] if n else []
    return pl.pallas_call(
        body,
        grid=(N_HEADS // 2, n_tiles),
        in_specs=[
            _bias_spec(),
            pl.BlockSpec((ATT_TILE, LANES), lambda p, t: (t, p)),
            pl.BlockSpec((seq, LANES), lambda p, t: (0, 4 + p)),
            pl.BlockSpec((seq, LANES), lambda p, t: (0, 8 + p)),
        ] + [ANY] * n,
        out_specs=[tile, tile] + [ANY] * n,
        out_shape=[jax.ShapeDtypeStruct((seq, HEAD_W), F32), jax.ShapeDtypeStruct((seq, HEAD_W), F32)]
        + _gathered_shapes(shards),
        scratch_shapes=[
            pltpu.VMEM((3, ATT_TILE, LANES), F32),
            pltpu.VMEM((3, ATT_TILE, LANES), F32),
        ] + sems,
        compiler_params=_params(("arbitrary", "arbitrary")),
        name="attn_fwd",
    )(bias, qkv, qkv, qkv, *shards)


def _attention_bwd(qkv, dy, y, lse, bias, partials):
    seq = qkv.shape[0]
    n_tiles = seq // ATT_TILE
    n = len(partials)

    def body(*refs):
        bias_ref, q_ref, k_ref, v_ref, dy_ref, y_ref, lse_ref = refs[:7]
        dq_ref, dk_ref, dv_ref, dbias_ref = refs[7 + n:11 + n]
        riding = (refs[7:7 + n], refs[11 + n:11 + 2 * n], *refs[11 + 2 * n:])
        pair = pl.program_id(0)
        t = pl.program_id(1)
        if n:
            @pl.when((pair == 0) & (t == 0))
            def _():
                for cp in _scatter_copies(*riding):
                    cp.start()

        lane = lax.broadcasted_iota(jnp.int32, (1, LANES), 1)
        head0 = lane < HEAD_DIM

        @pl.when(t == 0)
        def _():
            dk_ref[...] = jnp.zeros_like(dk_ref)
            dv_ref[...] = jnp.zeros_like(dv_ref)
            dbias_ref[...] = jnp.zeros_like(dbias_ref)

        dq_ref[...] = jnp.zeros_like(dq_ref)

        masks = (head0, jnp.logical_not(head0))
        ones = jnp.ones((LANES, LANES), BF16)
        scale = HEAD_DIM ** -0.5
        for b, (_, r) in enumerate(BRANCHES):
            def blocks(it, carry, b=b, r=r):
                idx = [_attn_block_index(it * ATT_UNROLL_BWD + j, t, r) for j in range(ATT_UNROLL_BWD)]
                qb = [q_ref[_rows(qs, r), :] * scale for qs, _, _, _ in idx]
                kcat = [jnp.concatenate([k_ref[_rows(ps, r), :], k_ref[_rows(gs, r), :]], axis=0).astype(BF16)
                        for _, gs, ps, _ in idx]
                vcat = [jnp.concatenate([v_ref[_rows(ps, r), :], v_ref[_rows(gs, r), :]], axis=0).astype(BF16)
                        for _, gs, ps, _ in idx]
                dob = [dy_ref[_rows(qs, r), :] for qs, _, _, _ in idx]
                ob = [y_ref[_rows(qs, r), :] for qs, _, _, _ in idx]
                lb = [lse_ref[_rows(qs, r), :] for qs, _, _, _ in idx]
                work = [(j, hh) for j in range(ATT_UNROLL_BWD) for hh in range(2)]
                qh = [jnp.where(masks[hh], qb[j], 0.0).astype(BF16) for j, hh in work]
                doh = [jnp.where(masks[hh], dob[j], 0.0) for j, hh in work]
                dohb = [d.astype(BF16) for d in doh]
                s = [_nt(qh[w], kcat[j]) + bias_ref[hh, b, idx[j][3]] for w, (j, hh) in enumerate(work)]
                dp = [_nt(dohb[w], vcat[j]) for w, (j, _) in enumerate(work)]
                lrot = [pltpu.roll(lv, HEAD_DIM, 1) for lv in lb]
                lcol = [jnp.where(masks[hh], lb[j], lrot[j]) for j, hh in work]
                parts = [_split(doh[w] * ob[j]) for w, (j, _) in enumerate(work)]
                delta = [_nn(hi, ones) + _nn(lo, ones) for hi, lo in parts]
                prob = [jnp.exp(sv - jnp.concatenate([lv, lv], axis=1)) for sv, lv in zip(s, lcol)]
                ds = [pv * (dv - jnp.concatenate([de, de], axis=1)) for pv, dv, de in zip(prob, dp, delta)]
                dsb = [d.astype(BF16) for d in ds]
                dq = [_nn(dsb[w], kcat[j]) for w, (j, _) in enumerate(work)]
                dkc = [_tn(dsb[w], qh[w]) for w in range(len(work))]
                dvc = [_tn(prob[w].astype(BF16), dohb[w]) for w in range(len(work))]
                for hh in range(2):
                    dbias_ref[0, b, hh] += sum(ds[w] for w, (_, head) in enumerate(work) if head == hh)
                for j in range(ATT_UNROLL_BWD):
                    qs, gs, ps, _ = idx[j]
                    dkcat = dkc[2 * j] + dkc[2 * j + 1]
                    dvcat = dvc[2 * j] + dvc[2 * j + 1]
                    dq_ref[_rows(qs, r), :] += jnp.where(head0, dq[2 * j], dq[2 * j + 1]) * scale
                    dk_ref[_rows(ps, r), :] += dkcat[:BAND]
                    dk_ref[_rows(gs, r), :] += dkcat[BAND:]
                    dv_ref[_rows(ps, r), :] += dvcat[:BAND]
                    dv_ref[_rows(gs, r), :] += dvcat[BAND:]
                return carry

            lax.fori_loop(0, ATT_TILE // BAND // ATT_UNROLL_BWD, blocks, 0)

        if n:
            @pl.when((pair == N_HEADS // 2 - 1) & (t == n_tiles - 1))
            def _():
                for cp in _scatter_copies(*riding):
                    cp.wait()

    tile = pl.BlockSpec((ATT_TILE, LANES), lambda p, t: (t, p))
    full = pl.BlockSpec((seq, LANES), lambda p, t: (0, p))
    sems = [pltpu.SemaphoreType.DMA((3 * n,)), pltpu.SemaphoreType.DMA((3 * n,))] if n else []
    return pl.pallas_call(
        body,
        grid=(N_HEADS // 2, n_tiles),
        in_specs=[
            _bias_spec(),
            pl.BlockSpec((ATT_TILE, LANES), lambda p, t: (t, p)),
            pl.BlockSpec((seq, LANES), lambda p, t: (0, 4 + p)),
            pl.BlockSpec((seq, LANES), lambda p, t: (0, 8 + p)),
            tile, tile, tile,
        ] + [ANY] * n,
        out_specs=[tile, full, full,
                   pl.BlockSpec((1, 3, 2, BAND, 2 * BAND), lambda p, t: (p, 0, 0, 0, 0))] + [ANY] * n,
        out_shape=[jax.ShapeDtypeStruct((seq, HEAD_W), F32)] * 3
        + [jax.ShapeDtypeStruct((N_HEADS // 2, 3, 2, BAND, 2 * BAND), F32)]
        + [jax.ShapeDtypeStruct(p.shape, p.dtype) for p in partials],
        scratch_shapes=sems,
        compiler_params=_params(("arbitrary", "arbitrary")),
        name="attn_bwd",
    )(bias, qkv, qkv, qkv, dy, y, lse, *partials)


def _rel_bias_grad(dbias, tables):
    def body(tab_ref, db_ref, out_ref):
        lane = lax.broadcasted_iota(jnp.int32, (1, LANES), 1)
        out_ref[...] = jnp.zeros_like(out_ref)
        for b in range(3):
            tab = tab_ref[b, 0]

            def head(h, carry, b=b, tab=tab):
                d = db_ref[h // 2, b, h % 2]
                sums = [jnp.sum(jnp.where(tab == kk, d, 0.0), keepdims=True) for kk in range(N_BUCKETS)]
                row = jnp.zeros((1, LANES), F32)
                for kk, s in enumerate(sums):
                    row = row + jnp.where(lane == kk, s, 0.0)
                out_ref[pl.ds(h, 1), :] += row
                return carry

            lax.fori_loop(0, N_HEADS, head, 0)

    return pl.pallas_call(
        body,
        out_shape=jax.ShapeDtypeStruct((N_HEADS, LANES), F32),
        compiler_params=pltpu.CompilerParams(vmem_limit_bytes=VMEM_LIMIT),
        name="rel_bias_grad",
    )(tables, dbias)


ROW_TILE = 512


def _head_sum_matrix():
    return (lax.broadcasted_iota(jnp.int32, (HEAD_W, HEAD_W), 0) // HEAD_DIM
            == lax.broadcasted_iota(jnp.int32, (HEAD_W, HEAD_W), 1) // HEAD_DIM).astype(F32)


def _head_spread_matrix(offset=0):
    return (lax.broadcasted_iota(jnp.int32, (LANES, HEAD_W), 0)
            == lax.broadcasted_iota(jnp.int32, (LANES, HEAD_W), 1) // HEAD_DIM + offset).astype(F32)


def _head_gather_matrix(offset=0):
    return (lax.broadcasted_iota(jnp.int32, (HEAD_W, LANES), 0) // HEAD_DIM + offset
            == lax.broadcasted_iota(jnp.int32, (HEAD_W, LANES), 1)).astype(F32)


def _split3(x):
    hi = x.astype(BF16)
    rest = x - hi.astype(F32)
    mid = rest.astype(BF16)
    return hi, mid, (rest - mid.astype(F32)).astype(BF16)


def _pick(x, onehot):
    m = onehot.astype(BF16)
    hi, mid, lo = _split3(x)
    return _nn(hi, m) + (_nn(mid, m) + _nn(lo, m))


def _pick_left(onehot, x):
    m = onehot.astype(BF16)
    hi, mid, lo = _split3(x)
    return _nn(m, hi) + (_nn(m, mid) + _nn(m, lo))


def _tri(lower, strict=False):
    r = lax.broadcasted_iota(jnp.int32, (CHUNK, CHUNK), 0)
    c = lax.broadcasted_iota(jnp.int32, (CHUNK, CHUNK), 1)
    if lower:
        return (c < r) if strict else (c <= r)
    return c >= r


def _softplus(z):
    return jnp.maximum(z, 0.0) + jnp.log(1.0 + jnp.exp(-jnp.abs(z)))


def _conv_taps(stage, w_ref, rows):
    return (w_ref[3:4, :] * stage[8:8 + rows, :] + w_ref[2:3, :] * stage[7:7 + rows, :]
            + w_ref[1:2, :] * stage[6:6 + rows, :] + w_ref[0:1, :] * stage[5:5 + rows, :])


def _l2_scale(xc, hsum):
    return lax.rsqrt(_pick(xc * xc, hsum) + EPS)


def _stage_rows(stage, x_ref, xp_ref, i):
    stage[0:8, :] = jnp.where(i == 0, 0.0, xp_ref[...])
    stage[8:8 + ROW_TILE, :] = x_ref[...]


def _delta_prep_fwd(qkvz, ba, conv_w, alog_row, dt_row):
    seq = qkvz.shape[0]
    qkv_w = 3 * HEAD_W

    def body(x_ref, xp_ref, ba_ref, w_ref, al_ref, dt_ref, out_ref, stage):
        i = pl.program_id(0)
        _stage_rows(stage, x_ref, xp_ref, i)
        act = _silu(_conv_taps(stage, w_ref, ROW_TILE))
        hsum, hspread = _head_sum_matrix(), _head_spread_matrix()
        qc, kc = act[:, :HEAD_W], act[:, HEAD_W:2 * HEAD_W]
        out_ref[0] = qc * _l2_scale(qc, hsum) * (HEAD_DIM ** -0.5)
        out_ref[1] = kc * _l2_scale(kc, hsum)
        out_ref[2] = act[:, 2 * HEAD_W:]
        bav = ba_ref[...]
        out_ref[3] = _pick(_sigmoid(bav), hspread)
        g8 = -jnp.exp(al_ref[...]) * _softplus(bav + dt_ref[...])
        gb = _pick(g8, _head_spread_matrix(N_HEADS))
        cum = _tri(True).astype(F32)
        for ch in range(ROW_TILE // CHUNK):
            rows = slice(ch * CHUNK, (ch + 1) * CHUNK)
            out_ref[4, rows, :] = _pick_left(cum, gb[rows])

    return pl.pallas_call(
        body,
        grid=(seq // ROW_TILE,),
        in_specs=[
            pl.BlockSpec((ROW_TILE, qkv_w), lambda i: (i, 0)),
            pl.BlockSpec((8, qkv_w), lambda i: (jnp.maximum(i * (ROW_TILE // 8) - 1, 0), 0)),
            pl.BlockSpec((ROW_TILE, LANES), lambda i: (i, 0)),
            pl.BlockSpec((4, qkv_w), lambda i: (0, 0)),
            pl.BlockSpec((1, LANES), lambda i: (0, 0)),
            pl.BlockSpec((1, LANES), lambda i: (0, 0)),
        ],
        out_specs=pl.BlockSpec((5, ROW_TILE, HEAD_W), lambda i: (0, i, 0)),
        out_shape=jax.ShapeDtypeStruct((5, seq, HEAD_W), F32),
        scratch_shapes=[pltpu.VMEM((ROW_TILE + 8, qkv_w), F32)],
        compiler_params=_params(("arbitrary",)),
        name="delta_prep_fwd",
    )(qkvz, qkvz, ba, conv_w, alog_row, dt_row)


def _split(x):
    hi = x.astype(BF16)
    return hi, (x - hi.astype(F32)).astype(BF16)


def _dot3(a, b, dot=_nn):
    return dot(a[0], b[0]) + (dot(a[0], b[1]) + dot(a[1], b[0]))


def _unit_lower_inverses(mats):
    eye = (lax.broadcasted_iota(jnp.int32, (CHUNK, CHUNK), 0)
           == lax.broadcasted_iota(jnp.int32, (CHUNK, CHUNK), 1)).astype(F32)
    invs = [eye - a for a in mats]
    powers = [_split(a) for a in mats]
    for step in range(5):
        squares = [_dot3(p, p) for p in powers]
        powers = [_split(s) for s in squares]
        invs = [inv + _dot3(_split(inv), p) for inv, p in zip(invs, powers)]
    return invs


def _chunk_terms(q, k, v, beta, gc):
    causal, strict = _tri(True), _tri(True, strict=True)
    e = jnp.exp(gc)
    g_last = jnp.broadcast_to(gc[CHUNK - 1:CHUNK, :], (CHUNK, CHUNK))
    f = jnp.exp(g_last - gc)
    e_last = jnp.exp(g_last)
    decay = jnp.where(causal, jnp.exp(jnp.where(causal, gc - gc.T, 0.0)), 0.0)
    kb = k * beta
    a_mat = jnp.where(strict, _nt(kb.astype(BF16), k.astype(BF16)) * decay, 0.0)
    qk = jnp.where(causal, _nt(q.astype(BF16), k.astype(BF16)) * decay, 0.0)
    return e, f, e_last, decay, kb, a_mat, qk


GROUP = 8
UNROLL = 8


def _chunk_rows(ci):
    return pl.ds(pl.multiple_of(ci * CHUNK, CHUNK), CHUNK)


def _pair_specs(n_planes):
    return pl.BlockSpec((n_planes, GROUP * CHUNK, LANES), lambda p, g: (0, g, p))


def _delta_chunk_fwd(xs):
    seq = xs.shape[1]
    rows_per_step = GROUP * CHUNK

    def body(x_ref, inv_ref, qk_ref, u_ref, w_ref):
        work = [(hh, slice(step * CHUNK, (step + 1) * CHUNK)) for hh in range(2) for step in range(GROUP)]
        xh = [[x_ref[j, r, hh * HEAD_DIM:(hh + 1) * HEAD_DIM] for j in range(5)] for hh, r in work]
        terms = [_chunk_terms(*x) for x in xh]
        invs = _unit_lower_inverses([t[5] for t in terms])
        for (hh, r), x, t, inv in zip(work, xh, terms, invs):
            e, kb, qk = t[0], t[4], t[6]
            inv_parts = _split(inv)
            inv_ref[hh, r, :] = inv
            qk_ref[hh, r, :] = qk
            u_ref[hh, r, :] = _dot3(inv_parts, _split(x[2] * x[3]))
            w_ref[hh, r, :] = _dot3(inv_parts, _split(kb * e))

    out = pl.BlockSpec((2, rows_per_step, HEAD_DIM), lambda p, g: (p, g, 0))
    return pl.pallas_call(
        body,
        grid=(N_HEADS // 2, seq // rows_per_step),
        in_specs=[_pair_specs(5)],
        out_specs=[out] * 4,
        out_shape=[jax.ShapeDtypeStruct((N_HEADS, seq, HEAD_DIM), F32)] * 4,
        compiler_params=_params(("parallel", "parallel")),
        name="delta_chunk_fwd",
    )(xs)


def _decays(gc):
    g_last = jnp.broadcast_to(gc[CHUNK - 1:CHUNK, :], (CHUNK, CHUNK))
    return jnp.exp(gc), jnp.exp(g_last - gc), jnp.exp(g_last)


def _token_blocks(index, n_steps=None):
    rows_per_step = GROUP * CHUNK
    if n_steps is None:
        return pl.BlockSpec((1, rows_per_step, HEAD_W), lambda g: (index, g, 0))
    return pl.BlockSpec((1, rows_per_step, HEAD_W), lambda g: (index, n_steps - 1 - g, 0))


def _head_lanes(h):
    return pl.ds(h * HEAD_DIM, HEAD_DIM)


def _delta_scan_fwd(xs, qk_h, u_h, w_h):
    seq = xs.shape[1]
    rows_per_step = GROUP * CHUNK

    def body(q_ref, k_ref, gc_ref, qk_ref, u_ref, w_ref, o_ref, st_ref, state):
        @pl.when(pl.program_id(0) == 0)
        def _():
            state[...] = jnp.zeros_like(state)

        def chunk(ci, carry):
            rows = _chunk_rows(ci)
            heads = range(N_HEADS)
            dec = [_decays(gc_ref[0, rows, _head_lanes(h)]) for h in heads]
            s = [state[h] for h in heads]
            sb = [s[h].astype(BF16) for h in heads]
            vnb = [(u_ref[h, rows, :] - _nn(w_ref[h, rows, :].astype(BF16), sb[h])).astype(BF16) for h in heads]
            for h in heads:
                o_ref[rows, _head_lanes(h)] = (_nn((q_ref[0, rows, _head_lanes(h)] * dec[h][0]).astype(BF16), sb[h])
                                               + _nn(qk_ref[h, rows, :].astype(BF16), vnb[h]))
                st_ref[h, rows, :] = s[h]
            for h in heads:
                state[h] = s[h] * dec[h][2] + _tn((k_ref[0, rows, _head_lanes(h)] * dec[h][1]).astype(BF16), vnb[h])
            return carry

        lax.fori_loop(0, GROUP, chunk, 0)

    blk = pl.BlockSpec((N_HEADS, rows_per_step, HEAD_DIM), lambda g: (0, g, 0))
    return pl.pallas_call(
        body,
        grid=(seq // rows_per_step,),
        in_specs=[_token_blocks(0), _token_blocks(1), _token_blocks(4), blk, blk, blk],
        out_specs=[pl.BlockSpec((rows_per_step, HEAD_W), lambda g: (g, 0)), blk],
        out_shape=[jax.ShapeDtypeStruct((seq, HEAD_W), F32), jax.ShapeDtypeStruct((N_HEADS, seq, HEAD_DIM), F32)],
        scratch_shapes=[pltpu.VMEM((N_HEADS, CHUNK, CHUNK), F32)],
        compiler_params=_params(("arbitrary",)),
        name="delta_scan_fwd",
    )(xs, xs, xs, qk_h, u_h, w_h)


def _delta_scan_bwd(xs, qk_h, w_h, do):
    seq = xs.shape[1]
    rows_per_step = GROUP * CHUNK
    n_steps = seq // rows_per_step

    def body(q_ref, k_ref, gc_ref, qk_ref, w_ref, do_ref, dsn_ref, dvn_ref, dstate):
        @pl.when(pl.program_id(0) == 0)
        def _():
            dstate[...] = jnp.zeros_like(dstate)

        def chunk(step, carry):
            rows = _chunk_rows(GROUP - 1 - step)
            heads = range(N_HEADS)
            dec = [_decays(gc_ref[0, rows, _head_lanes(h)]) for h in heads]
            ds_next = [dstate[h] for h in heads]
            dob = [do_ref[rows, _head_lanes(h)].astype(BF16) for h in heads]
            dv_new = [_tn(qk_ref[h, rows, :].astype(BF16), dob[h])
                      + _nn((k_ref[0, rows, _head_lanes(h)] * dec[h][1]).astype(BF16), ds_next[h].astype(BF16))
                      for h in heads]
            for h in heads:
                dsn_ref[h, rows, :] = ds_next[h]
                dvn_ref[h, rows, :] = dv_new[h]
            for h in heads:
                dstate[h] = (_tn((q_ref[0, rows, _head_lanes(h)] * dec[h][0]).astype(BF16), dob[h])
                             + dec[h][2] * ds_next[h] - _tn(w_ref[h, rows, :].astype(BF16), dv_new[h].astype(BF16)))
            return carry

        lax.fori_loop(0, GROUP, chunk, 0)

    blk = pl.BlockSpec((N_HEADS, rows_per_step, HEAD_DIM), lambda g: (0, n_steps - 1 - g, 0))
    return pl.pallas_call(
        body,
        grid=(n_steps,),
        in_specs=[_token_blocks(0, n_steps), _token_blocks(1, n_steps), _token_blocks(4, n_steps), blk, blk,
                  pl.BlockSpec((rows_per_step, HEAD_W), lambda g: (n_steps - 1 - g, 0))],
        out_specs=[blk, blk],
        out_shape=[jax.ShapeDtypeStruct((N_HEADS, seq, HEAD_DIM), F32)] * 2,
        scratch_shapes=[pltpu.VMEM((N_HEADS, CHUNK, CHUNK), F32)],
        compiler_params=_params(("arbitrary",)),
        name="delta_scan_bwd",
    )(xs, xs, xs, qk_h, w_h, do)


def _delta_chunk_bwd(xs, inv_h, u_h, w_h, st_h, dsn_h, dvn_h, do):
    seq = xs.shape[1]
    rows_per_step = GROUP * CHUNK

    def body(x_ref, inv_ref, u_ref, w_ref, st_ref, dsn_ref, dvn_ref, do_ref, dx_ref):
        causal, strict = _tri(True), _tri(True, strict=True)
        last_row = lax.broadcasted_iota(jnp.int32, (CHUNK, CHUNK), 0) == CHUNK - 1

        def bf(vals):
            return [val.astype(BF16) for val in vals]

        def group(items):
            heads = [hh for hh, _ in items]
            lanes = [slice(hh * HEAD_DIM, (hh + 1) * HEAD_DIM) for hh in heads]
            rows = [slice(step * CHUNK, (step + 1) * CHUNK) for _, step in items]
            n = range(len(items))
            q, k, v, beta, gc = [[x_ref[j, rows[i], lanes[i]] for i in n] for j in range(5)]
            terms = [_chunk_terms(q[i], k[i], v[i], beta[i], gc[i]) for i in n]
            e, f, e_last, decay, kb, a_mat, qk = [[t[j] for t in terms] for j in range(7)]
            inv = [_split(inv_ref[heads[i], rows[i], :]) for i in n]
            u = [u_ref[heads[i], rows[i], :] for i in n]
            w = [w_ref[heads[i], rows[i], :] for i in n]
            s = [st_ref[heads[i], rows[i], :] for i in n]
            ds_next = [dsn_ref[heads[i], rows[i], :] for i in n]
            dv_new = [dvn_ref[heads[i], rows[i], :] for i in n]
            sb, dsb, dvb, wb = bf(s), bf(ds_next), bf(dv_new), bf(w)
            dob = bf([do_ref[rows[i], lanes[i]] for i in n])
            qbf, kbf, kbb = bf(q), bf(k), bf(kb)
            vnb = bf([u[i] - _nn(wb[i], sb[i]) for i in n])
            dqe = [_nt(dob[i], sb[i]) for i in n]
            dw = [-_nt(dvb[i], sb[i]) for i in n]
            dkf = [_nt(vnb[i], dsb[i]) for i in n]
            dqk = [jnp.where(causal, _nt(dob[i], vnb[i]), 0.0) for i in n]
            drhs_u = [_dot3(inv[i], _split(dv_new[i]), _tn) for i in n]
            drhs_w = [_dot3(inv[i], _split(dw[i]), _tn) for i in n]
            da = [-jnp.where(strict, _nt(drhs_u[i].astype(BF16), u[i].astype(BF16))
                             + _nt(drhs_w[i].astype(BF16), wb[i]), 0.0) for i in n]
            dad = bf([da[i] * decay[i] for i in n])
            dqd = bf([dqk[i] * decay[i] for i in n])
            dkb = [e[i] * drhs_w[i] + _nn(dad[i], kbf[i]) for i in n]
            dk = [_tn(dad[i], kbb[i]) + _tn(dqd[i], qbf[i]) + f[i] * dkf[i] + beta[i] * dkb[i] for i in n]
            dq = [_nn(dqd[i], kbf[i]) + e[i] * dqe[i] for i in n]
            for i in n:
                de_full = kb[i] * drhs_w[i] + q[i] * dqe[i]
                df_full = k[i] * dkf[i]
                m = da[i] * a_mat[i] + dqk[i] * qk[i]
                dgc = de_full * e[i] - df_full * f[i] + m - m.T
                tail = jnp.sum(df_full * f[i] + s[i] * ds_next[i] * e_last[i], axis=0, keepdims=True)
                dgc = dgc + jnp.where(last_row, jnp.broadcast_to(tail, (CHUNK, CHUNK)), 0.0)
                dx_ref[0, rows[i], lanes[i]] = dq[i]
                dx_ref[1, rows[i], lanes[i]] = dk[i]
                dx_ref[2, rows[i], lanes[i]] = beta[i] * drhs_u[i]
                dx_ref[3, rows[i], lanes[i]] = v[i] * drhs_u[i] + k[i] * dkb[i]
                dx_ref[4, rows[i], lanes[i]] = dgc

        work = [(hh, step) for hh in range(2) for step in range(GROUP)]
        for first in range(0, len(work), UNROLL):
            group(work[first:first + UNROLL])

    blk = pl.BlockSpec((2, rows_per_step, HEAD_DIM), lambda p, g: (p, g, 0))
    return pl.pallas_call(
        body,
        grid=(N_HEADS // 2, seq // rows_per_step),
        in_specs=[_pair_specs(5)] + [blk] * 6 + [pl.BlockSpec((rows_per_step, LANES), lambda p, g: (g, p))],
        out_specs=_pair_specs(5),
        out_shape=jax.ShapeDtypeStruct((5, seq, HEAD_W), F32),
        compiler_params=_params(("parallel", "parallel")),
        name="delta_chunk_bwd",
    )(xs, inv_h, u_h, w_h, st_h, dsn_h, dvn_h, do)


def _delta_post_fwd(o, qkvz, gain_row):
    seq = o.shape[0]

    def body(o_ref, z_ref, g_ref, y_ref):
        ov = o_ref[...]
        rb = lax.rsqrt(_pick(ov * ov, _head_sum_matrix()) * (1.0 / HEAD_DIM) + EPS)
        y_ref[...] = (ov * rb * g_ref[...] * _silu(z_ref[...])).astype(y_ref.dtype)

    tile = pl.BlockSpec((ROW_TILE, HEAD_W), lambda i: (i, 0))
    return pl.pallas_call(
        body,
        grid=(seq // ROW_TILE,),
        in_specs=[tile, pl.BlockSpec((ROW_TILE, HEAD_W), lambda i: (i, 3)), pl.BlockSpec((1, HEAD_W), lambda i: (0, 0))],
        out_specs=tile,
        out_shape=jax.ShapeDtypeStruct((seq, HEAD_W), BF16),
        compiler_params=_params(("arbitrary",)),
        name="delta_post_fwd",
    )(o, qkvz, gain_row)


def _delta_post_bwd(dy, o, qkvz, gain_row):
    seq = o.shape[0]

    def body(dy_ref, o_ref, z_ref, g_ref, do_ref, dz_ref, dg_ref):
        @pl.when(pl.program_id(0) == 0)
        def _():
            dg_ref[...] = jnp.zeros_like(dg_ref)

        ov, zv, dyv, gain = o_ref[...], z_ref[...], dy_ref[...], g_ref[...]
        hsum = _head_sum_matrix()
        rb = lax.rsqrt(_pick(ov * ov, hsum) * (1.0 / HEAD_DIM) + EPS)
        ohat = ov * rb
        silu_z, slope_z = _silu_and_slope(zv)
        dz_ref[...] = dyv * ohat * gain * slope_z
        dn = dyv * silu_z
        dg_ref[0:1, :] += jnp.sum(dn * ohat, axis=0, keepdims=True)
        dohat = dn * gain

        @pl.when(pl.program_id(0) == pl.num_programs(0) - 1)
        def _():
            fold = (lax.broadcasted_iota(jnp.int32, (HEAD_W, HEAD_W), 0) % HEAD_DIM
                    == lax.broadcasted_iota(jnp.int32, (HEAD_W, HEAD_W), 1)).astype(F32)
            dg_ref[1:2, :] = _pick(dg_ref[0:1, :], fold)

        proj = _pick(dohat * ohat, hsum) * (1.0 / HEAD_DIM)
        do_ref[...] = rb * (dohat - ohat * proj)

    tile = pl.BlockSpec((ROW_TILE, HEAD_W), lambda i: (i, 0))
    return pl.pallas_call(
        body,
        grid=(seq // ROW_TILE,),
        in_specs=[pl.BlockSpec((ROW_TILE, HEAD_W), lambda i: (i, 1)), tile,
                  pl.BlockSpec((ROW_TILE, HEAD_W), lambda i: (i, 3)), pl.BlockSpec((1, HEAD_W), lambda i: (0, 0))],
        out_specs=[tile, tile, pl.BlockSpec((2, HEAD_W), lambda i: (0, 0))],
        out_shape=[jax.ShapeDtypeStruct((seq, HEAD_W), F32), jax.ShapeDtypeStruct((seq, HEAD_W), F32),
                   jax.ShapeDtypeStruct((2, HEAD_W), F32)],
        compiler_params=_params(("arbitrary",)),
        name="delta_post_bwd",
    )(dy, o, qkvz, gain_row)


def _delta_prep_bwd(qkvz, ba, conv_w, alog_row, dt_row, dxs):
    seq = qkvz.shape[0]
    qkv_w = 3 * HEAD_W

    def body(x_ref, xp_ref, ba_ref, w_ref, al_ref, dt_ref, dx_ref, dconv_ref, dba_ref, dvec_ref, stage):
        i = pl.program_id(0)

        @pl.when(i == 0)
        def _():
            dvec_ref[...] = jnp.zeros_like(dvec_ref)

        _stage_rows(stage, x_ref, xp_ref, i)
        pre = _conv_taps(stage, w_ref, ROW_TILE)
        act, slope = _silu_and_slope(pre)
        hsum = _head_sum_matrix()
        for j, scale in ((0, HEAD_DIM ** -0.5), (1, 1.0)):
            cols = slice(j * HEAD_W, (j + 1) * HEAD_W)
            xc = act[:, cols]
            rb = _l2_scale(xc, hsum)
            xhat = xc * rb
            dhat = dx_ref[j] * scale
            proj = _pick(dhat * xhat, hsum)
            dconv_ref[:, cols] = rb * (dhat - xhat * proj) * slope[:, cols]
        dconv_ref[:, 2 * HEAD_W:] = dx_ref[2] * slope[:, 2 * HEAD_W:]

        bav = ba_ref[...]
        beta8 = _sigmoid(bav)
        dbeta8 = _pick(dx_ref[3], _head_gather_matrix())
        dgc8 = _pick(dx_ref[4], _head_gather_matrix(N_HEADS))
        rev = _tri(False).astype(F32)
        z = bav + dt_ref[...]
        ea = jnp.exp(al_ref[...])
        g8 = -ea * _softplus(z)
        sig = _sigmoid(z)
        d_alog = jnp.zeros((1, LANES), F32)
        d_dt = jnp.zeros((1, LANES), F32)
        for ch in range(ROW_TILE // CHUNK):
            rows = slice(ch * CHUNK, (ch + 1) * CHUNK)
            dg8 = _pick_left(rev, dgc8[rows])
            da = -dg8 * ea * sig[rows]
            dba_ref[rows, :] = dbeta8[rows] * beta8[rows] * (1.0 - beta8[rows]) + da
            d_alog = d_alog + jnp.sum(dg8 * g8[rows], axis=0, keepdims=True)
            d_dt = d_dt + jnp.sum(da, axis=0, keepdims=True)
        dvec_ref[0:1, :] += d_alog
        dvec_ref[1:2, :] += d_dt

    return pl.pallas_call(
        body,
        grid=(seq // ROW_TILE,),
        in_specs=[
            pl.BlockSpec((ROW_TILE, qkv_w), lambda i: (i, 0)),
            pl.BlockSpec((8, qkv_w), lambda i: (jnp.maximum(i * (ROW_TILE // 8) - 1, 0), 0)),
            pl.BlockSpec((ROW_TILE, LANES), lambda i: (i, 0)),
            pl.BlockSpec((4, qkv_w), lambda i: (0, 0)),
            pl.BlockSpec((1, LANES), lambda i: (0, 0)),
            pl.BlockSpec((1, LANES), lambda i: (0, 0)),
            pl.BlockSpec((5, ROW_TILE, HEAD_W), lambda i: (0, i, 0)),
        ],
        out_specs=[pl.BlockSpec((ROW_TILE, qkv_w), lambda i: (i, 0)),
                   pl.BlockSpec((ROW_TILE, LANES), lambda i: (i, 0)),
                   pl.BlockSpec((2, LANES), lambda i: (0, 0))],
        out_shape=[jax.ShapeDtypeStruct((seq, qkv_w), F32), jax.ShapeDtypeStruct((seq, LANES), F32),
                   jax.ShapeDtypeStruct((2, LANES), F32)],
        scratch_shapes=[pltpu.VMEM((ROW_TILE + 8, qkv_w), F32)],
        compiler_params=_params(("arbitrary",)),
        name="delta_prep_bwd",
    )(qkvz, qkvz, ba, conv_w, alog_row, dt_row, dxs)


def _conv_bwd(dconv, qkvz, conv_w):
    seq = dconv.shape[0]
    qkv_w = 3 * HEAD_W
    n_tiles = seq // ROW_TILE

    def body(dy_ref, dyn_ref, x_ref, xp_ref, w_ref, dx_ref, dw_ref, stage, dstage):
        i = pl.program_id(0)

        @pl.when(i == 0)
        def _():
            dw_ref[...] = jnp.zeros_like(dw_ref)

        _stage_rows(stage, x_ref, xp_ref, i)
        dstage[0:ROW_TILE, :] = dy_ref[...]
        dstage[ROW_TILE:ROW_TILE + 8, :] = jnp.where(i == n_tiles - 1, 0.0, dyn_ref[...])
        dy = dy_ref[...]
        dx_ref[...] = (w_ref[3:4, :] * dy + w_ref[2:3, :] * dstage[1:1 + ROW_TILE, :]
                       + w_ref[1:2, :] * dstage[2:2 + ROW_TILE, :] + w_ref[0:1, :] * dstage[3:3 + ROW_TILE, :])
        for j in range(4):
            dw_ref[j:j + 1, :] += jnp.sum(dy * stage[5 + j:5 + j + ROW_TILE, :], axis=0, keepdims=True)

    tile = pl.BlockSpec((ROW_TILE, qkv_w), lambda i: (i, 0))
    return pl.pallas_call(
        body,
        grid=(n_tiles,),
        in_specs=[
            tile,
            pl.BlockSpec((8, qkv_w), lambda i: (jnp.minimum((i + 1) * (ROW_TILE // 8), seq // 8 - 1), 0)),
            tile,
            pl.BlockSpec((8, qkv_w), lambda i: (jnp.maximum(i * (ROW_TILE // 8) - 1, 0), 0)),
            pl.BlockSpec((4, qkv_w), lambda i: (0, 0)),
        ],
        out_specs=[tile, pl.BlockSpec((4, qkv_w), lambda i: (0, 0))],
        out_shape=[jax.ShapeDtypeStruct((seq, qkv_w), F32), jax.ShapeDtypeStruct((4, qkv_w), F32)],
        scratch_shapes=[pltpu.VMEM((ROW_TILE + 8, qkv_w), F32), pltpu.VMEM((ROW_TILE + 8, qkv_w), F32)],
        compiler_params=_params(("arbitrary",)),
        name="conv_bwd",
    )(dconv, dconv, qkvz, qkvz, conv_w)


FF_TILE = 1408
WGRAD_ROWS = 1024


def _row(a):
    return pl.BlockSpec((1, a), lambda *_: (0, 0))


def _rms_fwd(xv, gain):
    rstd = lax.rsqrt(jnp.mean(xv * xv, axis=-1, keepdims=True) + EPS)
    xhat = xv * rstd
    return xhat, rstd, xhat * gain


def _rms_bwd(dnorm, xhat, rstd, gain):
    dxhat = dnorm * gain
    dx = rstd * (dxhat - xhat * jnp.mean(dxhat * xhat, axis=-1, keepdims=True))
    return dx, jnp.sum(dnorm * xhat, axis=0, keepdims=True)


IN_SPLITS = (0, 3 * HEAD_W, 7 * HEAD_W, 7 * HEAD_W + LANES)


def _inproj_fwd(x, gain, scale, shift, w_rows):
    seq = x.shape[0]

    def body(x_ref, g_ref, sc_ref, sh_ref, w_ref, h_ref, a_ref, d_ref, b_ref):
        _, _, norm = _rms_fwd(x_ref[...], g_ref[...])
        h = (norm * (1.0 + sc_ref[...]) + sh_ref[...]).astype(BF16)
        h_ref[...] = h
        for out_ref, lo, hi in zip((a_ref, d_ref, b_ref), IN_SPLITS[:-1], IN_SPLITS[1:]):
            out_ref[...] = _nt(h, w_ref[lo:hi, :])

    def rows(width):
        return pl.BlockSpec((ROW_TILE, width), lambda i: (i, 0))

    return pl.pallas_call(
        body,
        grid=(seq // ROW_TILE,),
        in_specs=[rows(D_MODEL), _row(D_MODEL), _row(D_MODEL), _row(D_MODEL),
                  pl.BlockSpec(w_rows.shape, lambda i: (0, 0))],
        out_specs=[rows(D_MODEL), rows(3 * HEAD_W), rows(4 * HEAD_W), rows(LANES)],
        out_shape=[jax.ShapeDtypeStruct((seq, D_MODEL), BF16), jax.ShapeDtypeStruct((seq, 3 * HEAD_W), F32),
                   jax.ShapeDtypeStruct((seq, 4 * HEAD_W), F32), jax.ShapeDtypeStruct((seq, LANES), F32)],
        compiler_params=_params(("arbitrary",)),
        name="inproj_fwd",
    )(x, gain, scale, shift, w_rows)


def _outproj_fwd(y_attn, y_delta, w_out, x, gate1, gain, scale, shift):
    seq = x.shape[0]

    def body(ya_ref, yd_ref, wa_ref, wd_ref, x_ref, g1_ref, g_ref, sc_ref, sh_ref, x1_ref, h_ref, y_ref):
        y = _nn(ya_ref[...].astype(BF16), wa_ref[...]) + _nn(yd_ref[...], wd_ref[...])
        x1 = x_ref[...] + g1_ref[...] * y
        _, _, norm = _rms_fwd(x1, g_ref[...])
        x1_ref[...] = x1
        h_ref[...] = (norm * (1.0 + sc_ref[...]) + sh_ref[...]).astype(BF16)
        y_ref[...] = y.astype(BF16)

    def rows(width):
        return pl.BlockSpec((ROW_TILE, width), lambda i: (i, 0))

    return pl.pallas_call(
        body,
        grid=(seq // ROW_TILE,),
        in_specs=[rows(HEAD_W), rows(HEAD_W),
                  pl.BlockSpec((HEAD_W, D_MODEL), lambda i: (0, 0)), pl.BlockSpec((HEAD_W, D_MODEL), lambda i: (1, 0)),
                  rows(D_MODEL), _row(D_MODEL), _row(D_MODEL), _row(D_MODEL), _row(D_MODEL)],
        out_specs=[rows(D_MODEL), rows(D_MODEL), rows(D_MODEL)],
        out_shape=[jax.ShapeDtypeStruct((seq, D_MODEL), F32), jax.ShapeDtypeStruct((seq, D_MODEL), BF16),
                   jax.ShapeDtypeStruct((seq, D_MODEL), BF16)],
        compiler_params=_params(("arbitrary",)),
        name="outproj_fwd",
    )(y_attn, y_delta, w_out, w_out, x, gate1, gain, scale, shift)


def _ffn_fwd(h2, w_gate, w_up, w_down, x1, gate2, final_gain, target):
    seq = h2.shape[0]
    n_rows, n_ff = seq // ROW_TILE, D_FF // FF_TILE

    def body(h_ref, wg_ref, wu_ref, wd_ref, x1_ref, g2_ref, gf_ref, t_ref, gate_ref, up_ref, dx2_ref, st_ref, acc):
        i, j = pl.program_id(0), pl.program_id(1)

        @pl.when((i == 0) & (j == 0))
        def _():
            st_ref[...] = jnp.zeros_like(st_ref)

        h = h_ref[...]
        gate = _nt(h, wg_ref[...])
        up = _nt(h, wu_ref[...])
        gate_ref[...] = gate.astype(BF16)
        up_ref[...] = up.astype(BF16)
        part = _nn((_silu(gate) * up).astype(BF16), wd_ref[...])

        @pl.when(j == 0)
        def _():
            acc[...] = part

        @pl.when(j > 0)
        def _():
            acc[...] += part

        @pl.when(j == n_ff - 1)
        def _():
            y2 = acc[...]
            x2 = x1_ref[...] + g2_ref[...] * y2
            xhat, rstd, out = _rms_fwd(x2, gf_ref[...])
            diff = out - t_ref[...]
            dx2, dgain = _rms_bwd(diff * (1.0 / D_MODEL), xhat, rstd, gf_ref[...])
            dx2_ref[...] = dx2
            st_ref[0:1, :] += dgain
            st_ref[1:2, :] += jnp.sum(dx2 * y2, axis=0, keepdims=True)
            st_ref[2:3, :] += jnp.sum(diff * diff, axis=0, keepdims=True) * (0.5 / D_MODEL)

        @pl.when((i == n_rows - 1) & (j == n_ff - 1))
        def _():
            st_ref[3:4, :] = jnp.broadcast_to(jnp.sum(st_ref[2:3, :], keepdims=True), (1, D_MODEL))

    def rows(width):
        return pl.BlockSpec((ROW_TILE, width), lambda i, j: (i, 0))

    ff = pl.BlockSpec((ROW_TILE, FF_TILE), lambda i, j: (i, j))
    return pl.pallas_call(
        body,
        grid=(n_rows, n_ff),
        in_specs=[rows(D_MODEL),
                  pl.BlockSpec((FF_TILE, D_MODEL), lambda i, j: (j, 0)), pl.BlockSpec((FF_TILE, D_MODEL), lambda i, j: (j, 0)),
                  pl.BlockSpec((FF_TILE, D_MODEL), lambda i, j: (j, 0)),
                  rows(D_MODEL), _row(D_MODEL), _row(D_MODEL), rows(D_MODEL)],
        out_specs=[ff, ff, rows(D_MODEL), pl.BlockSpec((8, D_MODEL), lambda i, j: (0, 0))],
        out_shape=[jax.ShapeDtypeStruct((seq, D_FF), BF16), jax.ShapeDtypeStruct((seq, D_FF), BF16),
                   jax.ShapeDtypeStruct((seq, D_MODEL), F32), jax.ShapeDtypeStruct((8, D_MODEL), F32)],
        scratch_shapes=[pltpu.VMEM((ROW_TILE, D_MODEL), F32)],
        compiler_params=_params(("arbitrary", "arbitrary")),
        name="ffn_fwd",
    )(h2, w_gate, w_up, w_down, x1, gate2, final_gain, target)


def _ffn_bwd(dx2, gate, up, w_gate, w_up, w_down, x1, y, gate2, gate1, gain, scale):
    seq = dx2.shape[0]

    def act_body(dx2_ref, g2_ref, gate_ref, up_ref, wd_ref, dgate_ref, dup_ref, dwd_ref):
        dy2 = (g2_ref[...] * dx2_ref[...]).astype(BF16)
        gate = gate_ref[...].astype(F32)
        up = up_ref[...].astype(F32)
        dact = _nt(dy2, wd_ref[...])
        silu, slope = _silu_and_slope(gate)
        dgate_ref[...] = (dact * up * slope).astype(BF16)
        dup_ref[...] = (dact * silu).astype(BF16)
        part = _tn((silu * up).astype(BF16), dy2)

        @pl.when(pl.program_id(1) == 0)
        def _():
            dwd_ref[...] = part

        @pl.when(pl.program_id(1) > 0)
        def _():
            dwd_ref[...] += part

    ff = pl.BlockSpec((ROW_TILE, FF_TILE), lambda j, i: (i, j))
    w_tile = pl.BlockSpec((FF_TILE, D_MODEL), lambda j, i: (j, 0))
    dgate, dup, dw_down = pl.pallas_call(
        act_body,
        grid=(D_FF // FF_TILE, seq // ROW_TILE),
        in_specs=[pl.BlockSpec((ROW_TILE, D_MODEL), lambda j, i: (i, 0)), _row(D_MODEL), ff, ff, w_tile],
        out_specs=[ff, ff, w_tile],
        out_shape=[jax.ShapeDtypeStruct((seq, D_FF), BF16)] * 2 + [jax.ShapeDtypeStruct((D_FF, D_MODEL), F32)],
        compiler_params=_params(("arbitrary", "arbitrary")),
        name="ffn_bwd_act",
    )(dx2, gate2, gate, up, w_down)

    def in_body(dgate_ref, dup_ref, wg_ref, wu_ref, dx2_ref, x1_ref, y_ref, g1_ref, g_ref, sc_ref,
                dx1_ref, dy_ref, st_ref):
        @pl.when(pl.program_id(0) == 0)
        def _():
            st_ref[...] = jnp.zeros_like(st_ref)

        dh = _nn(dgate_ref[...], wg_ref[...]) + _nn(dup_ref[...], wu_ref[...])
        xhat, rstd, norm = _rms_fwd(x1_ref[...], g_ref[...])
        dxn, dgain = _rms_bwd(dh * (1.0 + sc_ref[...]), xhat, rstd, g_ref[...])
        dx1 = dx2_ref[...] + dxn
        dx1_ref[...] = dx1
        dy_ref[...] = (g1_ref[...] * dx1).astype(BF16)
        st_ref[0:1, :] += jnp.sum(dh, axis=0, keepdims=True)
        st_ref[1:2, :] += jnp.sum(dh * norm, axis=0, keepdims=True)
        st_ref[2:3, :] += dgain
        st_ref[3:4, :] += jnp.sum(dx1 * y_ref[...].astype(F32), axis=0, keepdims=True)

    half_tile = ROW_TILE // 2

    def rows(width):
        return pl.BlockSpec((half_tile, width), lambda i: (i, 0))

    whole = pl.BlockSpec((D_FF, D_MODEL), lambda i: (0, 0))
    dx1, dy, stats = pl.pallas_call(
        in_body,
        grid=(seq // half_tile,),
        in_specs=[rows(D_FF), rows(D_FF), whole, whole, rows(D_MODEL), rows(D_MODEL), rows(D_MODEL),
                  _row(D_MODEL), _row(D_MODEL), _row(D_MODEL)],
        out_specs=[rows(D_MODEL), rows(D_MODEL), pl.BlockSpec((8, D_MODEL), lambda i: (0, 0))],
        out_shape=[jax.ShapeDtypeStruct((seq, D_MODEL), F32), jax.ShapeDtypeStruct((seq, D_MODEL), BF16),
                   jax.ShapeDtypeStruct((8, D_MODEL), F32)],
        compiler_params=_params(("arbitrary",)),
        name="ffn_bwd_in",
    )(dgate, dup, w_gate, w_up, dx2, x1, y, gate1, gain, scale)
    return dgate, dup, dw_down, dx1, dy, stats


def _outproj_bwd(dy, w_out):
    seq = dy.shape[0]

    def body(dy_ref, w_ref, out_ref):
        out_ref[...] = _nt(dy_ref[...], w_ref[...])

    rows = pl.BlockSpec((ROW_TILE, D_MODEL), lambda i: (i, 0))
    return pl.pallas_call(
        body,
        grid=(seq // ROW_TILE,),
        in_specs=[rows, pl.BlockSpec((D_MODEL, D_MODEL), lambda i: (0, 0))],
        out_specs=rows,
        out_shape=jax.ShapeDtypeStruct((seq, D_MODEL), F32),
        compiler_params=_params(("arbitrary",)),
        name="outproj_bwd",
    )(dy, w_out)


def _inproj_bwd(dq, dk, dv, dxd, dz, dba, w_rows, x, dx1, gain, scale, partials):
    seq = x.shape[0]
    n = len(partials)
    n_steps = seq // ROW_TILE

    def body(*refs):
        pieces, (w_ref, x_ref, dx1_ref, g_ref, sc_ref) = refs[:6], refs[6:11]
        gx_ref, st_ref = refs[11 + n:13 + n]
        riding = (refs[11:11 + n], refs[13 + n:13 + 2 * n], *refs[13 + 2 * n:])

        @pl.when(pl.program_id(0) == 0)
        def _():
            st_ref[...] = jnp.zeros_like(st_ref)
            for cp in (_scatter_copies(*riding) if n else []):
                cp.start()

        dh = _nn(jnp.concatenate([p[...].astype(BF16) for p in pieces], axis=1), w_ref[...])
        xhat, rstd, norm = _rms_fwd(x_ref[...], g_ref[...])
        dxn, dgain = _rms_bwd(dh * (1.0 + sc_ref[...]), xhat, rstd, g_ref[...])
        gx_ref[...] = dx1_ref[...] + dxn
        st_ref[0:1, :] += jnp.sum(dh, axis=0, keepdims=True)
        st_ref[1:2, :] += jnp.sum(dh * norm, axis=0, keepdims=True)
        st_ref[2:3, :] += dgain

        if n:
            @pl.when(pl.program_id(0) == n_steps - 1)
            def _():
                for cp in _scatter_copies(*riding):
                    cp.wait()

    def rows(width):
        return pl.BlockSpec((ROW_TILE, width), lambda i: (i, 0))

    sems = [pltpu.SemaphoreType.DMA((3 * n,)), pltpu.SemaphoreType.DMA((3 * n,))] if n else []
    return pl.pallas_call(
        body,
        grid=(n_steps,),
        in_specs=[rows(HEAD_W), rows(HEAD_W), rows(HEAD_W), rows(3 * HEAD_W), rows(HEAD_W), rows(LANES),
                  pl.BlockSpec(w_rows.shape, lambda i: (0, 0)), rows(D_MODEL), rows(D_MODEL), _row(D_MODEL),
                  _row(D_MODEL)]
        + [ANY] * n,
        out_specs=[rows(D_MODEL), pl.BlockSpec((8, D_MODEL), lambda i: (0, 0))] + [ANY] * n,
        out_shape=[jax.ShapeDtypeStruct((seq, D_MODEL), F32), jax.ShapeDtypeStruct((8, D_MODEL), F32)]
        + [jax.ShapeDtypeStruct(p.shape, p.dtype) for p in partials],
        scratch_shapes=sems,
        compiler_params=_params(("arbitrary",)),
        name="inproj_bwd",
    )(dq, dk, dv, dxd, dz, dba, w_rows, x, dx1, gain, scale, *partials)


def _weight_grad(a, b, name):
    seq, m = a.shape
    n = b.shape[1]
    tm = m if m <= 1536 else m // 2
    tn = n if n <= 1536 else n // 2
    rows = 2 * WGRAD_ROWS
    n_k = seq // rows

    def body(a_ref, b_ref, out_ref):
        part = _tn(a_ref[...].astype(BF16), b_ref[...].astype(BF16))

        @pl.when(pl.program_id(2) == 0)
        def _():
            out_ref[...] = part

        @pl.when(pl.program_id(2) > 0)
        def _():
            out_ref[...] += part

    return pl.pallas_call(
        body,
        grid=(m // tm, n // tn, n_k),
        in_specs=[pl.BlockSpec((rows, tm), lambda i, j, k: (k, i)),
                  pl.BlockSpec((rows, tn), lambda i, j, k: (k, j))],
        out_specs=pl.BlockSpec((tm, tn), lambda i, j, k: (i, j)),
        out_shape=jax.ShapeDtypeStruct((m, n), F32),
        compiler_params=_params(("arbitrary", "arbitrary", "arbitrary")),
        name=name,
    )(a, b)


def _weight_grad_stack(pieces, b, name):
    seq, n = b.shape
    widths = [a.shape[1] for a in pieces]
    starts = [sum(widths[:i]) for i in range(len(pieces))]

    def body(*refs):
        a_refs, b_ref, out_ref = refs[:len(pieces)], refs[len(pieces)], refs[len(pieces) + 1]

        @pl.when(pl.program_id(0) == 0)
        def _():
            out_ref[...] = jnp.zeros_like(out_ref)

        bb = b_ref[...].astype(BF16)
        for a_ref, start, width in zip(a_refs, starts, widths):
            out_ref[start:start + width, :] += _tn(a_ref[...].astype(BF16), bb)

    def rows(width):
        return pl.BlockSpec((WGRAD_ROWS, width), lambda k: (k, 0))

    return pl.pallas_call(
        body,
        grid=(seq // WGRAD_ROWS,),
        in_specs=[rows(w) for w in widths] + [rows(n)],
        out_specs=pl.BlockSpec((sum(widths), n), lambda k: (0, 0)),
        out_shape=jax.ShapeDtypeStruct((sum(widths), n), F32),
        compiler_params=_params(("arbitrary",)),
        name=name,
    )(*pieces, b)


def _adamw(w, g, m, v, name):
    n_rows, n_cols = w.shape
    if w.size <= 64 * 1024:
        block, grid, index = (n_rows, n_cols), (1,), lambda i: (0, 0)
    elif n_rows % 256 == 0:
        block, grid, index = (256, n_cols), (n_rows // 256,), lambda i: (i, 0)
    elif n_cols % 256 == 0:
        block, grid, index = (n_rows, 256), (n_cols // 256,), lambda i: (0, i)
    else:
        block, grid, index = (n_rows, n_cols), (1,), lambda i: (0, 0)

    def body(w_ref, g_ref, m_ref, v_ref, d_ref, nm_ref, nv_ref):
        gv = g_ref[...]
        nm = ADAM_B1 * m_ref[...] + (1.0 - ADAM_B1) * gv
        nv = ADAM_B2 * v_ref[...] + (1.0 - ADAM_B2) * (gv * gv)
        m_hat = nm / (1.0 - ADAM_B1 ** ADAM_STEP)
        v_hat = nv / (1.0 - ADAM_B2 ** ADAM_STEP)
        d_ref[...] = -ADAM_LR * (m_hat / (jnp.sqrt(v_hat) + ADAM_EPS) + ADAM_WD * w_ref[...])
        nm_ref[...] = nm
        nv_ref[...] = nv

    blk = pl.BlockSpec(block, index)
    shape = jax.ShapeDtypeStruct((n_rows, n_cols), F32)
    return pl.pallas_call(
        body,
        grid=grid,
        in_specs=[blk] * 4,
        out_specs=[blk] * 3,
        out_shape=[shape] * 3,
        compiler_params=_params(("arbitrary",)),
        name=name,
    )(w, g, m, v)


IN_WIDTH = 3600


def _local_step(x, target, mod, norm_attn_g, w_in, rel_bias, conv_w, a_log, dt_bias, delta_norm_g,
                norm_ffn_g, final_norm_g, shards, assemble, reduce_pairs):
    sh1, sc1, g1, sh2, sc2, g2 = [mod[:, i * D_MODEL:(i + 1) * D_MODEL] for i in range(6)]
    w_rows = jnp.pad(w_in, ((0, IN_SPLITS[-1] - IN_WIDTH), (0, 0)))
    tables = jnp.asarray(_attn_tables())
    alog_row = jnp.pad(a_log, ((0, 0), (N_HEADS, LANES - 2 * N_HEADS)))
    dt_row = jnp.pad(dt_bias, ((0, 0), (N_HEADS, LANES - 2 * N_HEADS)))
    gain_row = jnp.tile(delta_norm_g, (1, N_HEADS))

    h1, qkv_a, qkvz, ba = _inproj_fwd(x, norm_attn_g, sc1, sh1, w_rows)
    bias = _attention_bias(rel_bias, tables)
    y_attn, lse, *gathered = _attention_fwd(qkv_a, bias, shards)
    w_out, w_gate, w_up, w_down = assemble(gathered)
    xs = _delta_prep_fwd(qkvz, ba, conv_w, alog_row, dt_row)
    inv_h, qk_h, u_h, w_h = _delta_chunk_fwd(xs)
    o, st_h = _delta_scan_fwd(xs, qk_h, u_h, w_h)
    y_delta = _delta_post_fwd(o, qkvz, gain_row)
    x1, h2, y = _outproj_fwd(y_attn, y_delta, w_out, x, g1, norm_ffn_g, sc2, sh2)
    gate, up, dx2, st_f = _ffn_fwd(h2, w_gate, w_up, w_down, x1, g2, final_norm_g, target)

    dgate, dup, dw_down, dx1, dy, st_b = _ffn_bwd(dx2, gate, up, w_gate, w_up, w_down, x1, y, g2, g1, norm_ffn_g, sc2)
    partials = reduce_pairs([_weight_grad_stack([y_attn, y_delta], dy, "wgrad_out"),
                             _weight_grad(dgate, h2, "wgrad_gate"), _weight_grad(dup, h2, "wgrad_up"),
                             dw_down], 1, "rest")
    grads = {}
    dycat = _outproj_bwd(dy, w_out)
    do, dz, dgain = _delta_post_bwd(dycat, o, qkvz, gain_row)
    dsn_h, dvn_h = _delta_scan_bwd(xs, qk_h, w_h, do)
    dxs = _delta_chunk_bwd(xs, inv_h, u_h, w_h, st_h, dsn_h, dvn_h, do)
    dconv, dba, dvec = _delta_prep_bwd(qkvz, ba, conv_w, alog_row, dt_row, dxs)
    dxd, grads["conv_w"] = _conv_bwd(dconv, qkvz, conv_w)
    dq, dk, dv, dbias, *scattered = _attention_bwd(qkv_a, dycat, y_attn, lse, bias, partials)
    partials_in = reduce_pairs([jnp.concatenate(
        [_weight_grad_stack([dq, dk, dv], h1, "wgrad_in_attn"),
         _weight_grad_stack([dxd, dz, dba], h1, "wgrad_in_delta")[:IN_WIDTH - 3 * HEAD_W]], axis=0)], 0, "in")
    grad_x, st_i, *scattered_in = _inproj_bwd(dq, dk, dv, dxd, dz, dba, w_rows, x, dx1, norm_attn_g, sc1,
                                              partials_in)
    grads["rel_bias"] = _rel_bias_grad(dbias, tables)[:, :N_BUCKETS].T
    grads["a_log"] = dvec[0:1, N_HEADS:2 * N_HEADS]
    grads["dt_bias"] = dvec[1:2, N_HEADS:2 * N_HEADS]
    grads["delta_norm_g"] = dgain[1:2, :HEAD_DIM]
    grads["norm_attn_g"] = st_i[2:3]
    grads["norm_ffn_g"] = st_b[2:3]
    grads["final_norm_g"] = st_f[0:1]
    dmod = jnp.concatenate([st_i[0:1], st_i[1:2], st_b[3:4], st_b[0:1], st_b[1:2], st_f[1:2]], axis=1)
    return st_f[3, 0], grad_x, grads, dmod, (partials_in + partials, scattered_in + scattered)


MESH = pl.DeviceIdType.MESH
OTHER_CHIPS = ((1, 0), (0, 1), (1, 1))
ALL_PEERS = tuple((m >> 2 & 1, m >> 1 & 1, m & 1) for m in range(1, 8))
ANY = pl.BlockSpec(memory_space=pl.ANY)
VMEM_SPEC = pl.BlockSpec(memory_space=pltpu.VMEM)


def _me():
    return lax.axis_index("x"), lax.axis_index("y"), lax.axis_index("c")


def _flip(pos, mask):
    return tuple(1 - p if m else p for p, m in zip(pos, mask))


def _remote(src, dst, send_sems, recv_sems, k, to):
    return pltpu.make_async_remote_copy(src_ref=src, dst_ref=dst, send_sem=send_sems.at[k], recv_sem=recv_sems.at[k],
                                        device_id=to, device_id_type=MESH)


def _ada_exchange(c8, w_ada, b_ada, conv8, shard):
    def body(c_ref, w_ref, b_ref, cv_ref, shard_ref, mod_ref, cact_ref, conv_ref, whole_ref,
             c_all, part_all, send_sems, recv_sems, ride_send, ride_recv):
        x, y, c = me = _me()
        dev = 4 * x + 2 * y + c
        chip = 2 * x + y
        riding = ([shard_ref], [whole_ref], ride_send, ride_recv)
        for cp in _gather_copies(*riding, hand_over=False)[0]:
            cp.start()
        c_all[dev] = c_ref[...]
        conv_ref[chip] = cv_ref[...]
        first = [_remote(c_ref, c_all.at[dev], send_sems, recv_sems, k, _flip(me, mask))
                 for k, mask in enumerate(ALL_PEERS)]
        first += [_remote(cv_ref, conv_ref.at[chip], send_sems, recv_sems, 7 + j, _flip(me, (*mask, 0)))
                  for j, mask in enumerate(OTHER_CHIPS)]
        for cp in first:
            cp.start()
        for cp in first:
            cp.wait()
        row = lax.broadcasted_iota(jnp.int32, (8, D_MODEL), 0)
        c_rows = jnp.zeros((8, D_MODEL), F32)
        for d in range(8):
            c_rows = jnp.where(row == d, c_all[d], c_rows)
        c_act = _silu(c_rows)
        cact_ref[...] = c_act
        part_all[chip] = _nn(c_act, w_ref[...], HIGHEST)
        second = [_remote(part_all.at[chip], part_all.at[chip], send_sems, recv_sems, 10 + j, _flip(me, (*mask, 0)))
                  for j, mask in enumerate(OTHER_CHIPS)]
        for cp in second:
            cp.start()
        for cp in second:
            cp.wait()
        cols = w_ref.shape[1]
        for k in range(4):
            mod_ref[:, k * cols:(k + 1) * cols] = part_all[k] + b_ref[:, k * cols:(k + 1) * cols]
        first, passed = _gather_copies(*riding)
        for cp, fwd in zip(first, passed):
            cp.wait_recv()
            fwd.start()
        for cp in first:
            cp.wait_send()
        for fwd in passed:
            fwd.wait()

    cols = w_ada.shape[1]
    return pl.pallas_call(
        body,
        in_specs=[VMEM_SPEC] * 4 + [ANY],
        out_specs=[VMEM_SPEC] * 3 + [ANY],
        out_shape=[jax.ShapeDtypeStruct((8, 4 * cols), F32), jax.ShapeDtypeStruct((8, D_MODEL), F32),
                   jax.ShapeDtypeStruct((4, 8, conv8.shape[1]), F32)] + _gathered_shapes([shard]),
        scratch_shapes=[pltpu.VMEM((8, 8, D_MODEL), F32), pltpu.VMEM((4, 8, cols), F32),
                        pltpu.SemaphoreType.DMA((13,)), pltpu.SemaphoreType.DMA((13,)),
                        pltpu.SemaphoreType.DMA((6,)), pltpu.SemaphoreType.DMA((6,))],
        compiler_params=pltpu.CompilerParams(vmem_limit_bytes=VMEM_LIMIT),
        name="ada_exchange",
    )(c8, w_ada, b_ada, conv8, shard)


def _gathered_shapes(shards):
    return [jax.ShapeDtypeStruct((4, *s.shape), s.dtype) for s in shards]


def _gather_copies(srcs, dsts, send_sems, recv_sems, hand_over=True):
    x, y, c = me = _me()
    chip = 2 * x + y
    sibling = _flip(me, (0, 0, 1))
    first, passed = [], []
    for a, (src, dst) in enumerate(zip(srcs, dsts)):
        for j, mask in enumerate(OTHER_CHIPS):
            to = _flip(me, (*mask, 0))
            first.append(_remote(src.at[c], dst.at[chip, c], send_sems, recv_sems, 6 * a + j, to))
            if hand_over:
                landed = dst.at[2 * to[0] + to[1], c]
                passed.append(_remote(landed, landed, send_sems, recv_sems, 6 * a + 3 + j, sibling))
    return first, passed


def _scatter_copies(srcs, dsts, send_sems, recv_sems):
    x, y, c = me = _me()
    chip = 2 * x + y
    copies = []
    for a, (src, dst) in enumerate(zip(srcs, dsts)):
        for j, mask in enumerate(OTHER_CHIPS):
            to = _flip(me, (*mask, 0))
            copies.append(_remote(src.at[2 * to[0] + to[1]], dst.at[chip], send_sems, recv_sems, 3 * a + j, to))
    return copies


def _start_and_wait(copies):
    for cp in copies:
        cp.start()
    for cp in copies:
        cp.wait()


def _swap_halves(grads):
    n = len(grads)

    def body(*refs):
        srcs, got = refs[:n], refs[n:2 * n]
        send_sems, recv_sems = refs[2 * n:]
        x, y, c = me = _me()
        _start_and_wait([_remote(srcs[a].at[:, 1 - c], got[a], send_sems, recv_sems, a, _flip(me, (0, 0, 1)))
                         for a in range(n)])

    return pl.pallas_call(
        body,
        in_specs=[ANY] * n,
        out_specs=[ANY] * n,
        out_shape=[jax.ShapeDtypeStruct((4, g.shape[2], g.shape[3]), g.dtype) for g in grads],
        scratch_shapes=[pltpu.SemaphoreType.DMA((n,)), pltpu.SemaphoreType.DMA((n,))],
        name=f"swap_halves_{n}",
    )(*grads)


def _join_halves(halves):
    n = len(halves)

    def body(*refs):
        srcs, dsts = refs[:n], refs[n:2 * n]
        send_sems, recv_sems = refs[2 * n:]
        x, y, c = me = _me()
        _start_and_wait([_remote(srcs[a], dsts[a].at[c], send_sems, recv_sems, a, _flip(me, (0, 0, 1)))
                         for a in range(n)])

    return pl.pallas_call(
        body,
        in_specs=[ANY] * n,
        out_specs=[ANY] * n,
        out_shape=[jax.ShapeDtypeStruct((2, *h.shape), h.dtype) for h in halves],
        scratch_shapes=[pltpu.SemaphoreType.DMA((n,)), pltpu.SemaphoreType.DMA((n,))],
        name=f"join_halves_{n}",
    )(*halves)


def _gather_small(packed):
    n_rows = packed.shape[0]

    def body(p_ref, all_ref, sum_ref, send_sems, recv_sems):
        x, y, c = me = _me()
        dev = 4 * x + 2 * y + c
        all_ref[dev] = p_ref[...]
        copies = [_remote(p_ref, all_ref.at[dev], send_sems, recv_sems, k, _flip(me, mask))
                  for k, mask in enumerate(ALL_PEERS)]
        for cp in copies:
            cp.start()
        for cp in copies:
            cp.wait()
        total = all_ref[0]
        for d in range(1, 8):
            total = total + all_ref[d]
        sum_ref[...] = total

    return pl.pallas_call(
        body,
        in_specs=[VMEM_SPEC],
        out_specs=[VMEM_SPEC, VMEM_SPEC],
        out_shape=[jax.ShapeDtypeStruct((8, n_rows, LANES), F32), jax.ShapeDtypeStruct((n_rows, LANES), F32)],
        scratch_shapes=[pltpu.SemaphoreType.DMA((7,)), pltpu.SemaphoreType.DMA((7,))],
        name="gather_small",
    )(packed)


def _add_pair(a, b, out_dtype, name):
    def body(a_ref, b_ref, o_ref):
        o_ref[...] = (a_ref[...] + b_ref[...]).astype(o_ref.dtype)

    blk = pl.BlockSpec((1, *a.shape[1:]), lambda i: (i, 0, 0))
    return pl.pallas_call(
        body, grid=(a.shape[0],), in_specs=[blk, blk], out_specs=blk,
        out_shape=jax.ShapeDtypeStruct(a.shape, out_dtype),
        compiler_params=_params(("arbitrary",)), name=name,
    )(a, b)


def _add_slots(a, name):
    def body(a_ref, o_ref):
        total = a_ref[0].astype(F32)
        for k in range(1, 4):
            total = total + a_ref[k].astype(F32)
        o_ref[...] = total

    return pl.pallas_call(
        body, in_specs=[VMEM_SPEC], out_specs=VMEM_SPEC,
        out_shape=jax.ShapeDtypeStruct(a.shape[1:], F32),
        compiler_params=pltpu.CompilerParams(vmem_limit_bytes=VMEM_LIMIT), name=name,
    )(a)


def _ada_weight_grad(c_act, dmod_cols):
    def body(c_ref, d_ref, o_ref):
        o_ref[...] = _tn(c_ref[...], d_ref[...], HIGHEST)

    return pl.pallas_call(
        body, in_specs=[VMEM_SPEC, VMEM_SPEC], out_specs=VMEM_SPEC,
        out_shape=jax.ShapeDtypeStruct((c_act.shape[1], dmod_cols.shape[1]), F32),
        compiler_params=pltpu.CompilerParams(vmem_limit_bytes=VMEM_LIMIT), name="ada_weight_grad",
    )(c_act, dmod_cols)


def kernel(x, c, w_ada, b_ada, norm_attn_g, w_in, rel_bias, conv_w, a_log, dt_bias, delta_norm_g, w_out, norm_ffn_g, w_gate, w_up, w_down, final_norm_g, loss_target, m_w_ada, m_b_ada, m_norm_attn_g, m_w_in, m_rel_bias, m_conv_w, m_a_log, m_dt_bias, m_delta_norm_g, m_w_out, m_norm_ffn_g, m_w_gate, m_w_up, m_w_down, m_final_norm_g, v_w_ada, v_b_ada, v_norm_attn_g, v_w_in, v_rel_bias, v_conv_w, v_a_log, v_dt_bias, v_delta_norm_g, v_w_out, v_norm_ffn_g, v_w_gate, v_w_up, v_w_down, v_final_norm_g):
    xi, yi, ci = _me()
    dev = 4 * xi + 2 * yi + ci
    chip = 2 * xi + yi

    big_names = ("w_in", "w_out", "w_gate", "w_up", "w_down")
    by_cols = (True, False, True, True, False)

    def rows_form(a, cols):
        return jnp.swapaxes(a[0], 0, 1) if cols else a[0]

    def halves_form(w):
        rows, lanes = w.shape
        if (rows // 2) % 16:
            rows, lanes = w.size // LANES, LANES
        return (2, rows // 2, lanes)

    big = [rows_form(w, cols) for w, cols in zip((w_in, w_out, w_gate, w_up, w_down), by_cols)]
    shards = [w.astype(BF16).reshape(halves_form(w)) for w in big]

    def assemble(gathered, first):
        return [lax.dynamic_update_index_in_dim(g, s, chip, 0).reshape(4 * w.shape[0], w.shape[1])
                for g, s, w in zip(gathered, shards[first:], big[first:])]

    def reduce_pairs(grads, first, tag):
        slots = [g.reshape(4, *halves_form(w)) for g, w in zip(grads, big[first:])]
        return [_add_pair(lax.dynamic_index_in_dim(s, ci, 1, keepdims=False), got, BF16, f"add_pair_{tag}{a}")
                for a, (s, got) in enumerate(zip(slots, _swap_halves(slots)))]

    def finish(partials, scattered, first, tag):
        by_source = [lax.dynamic_update_index_in_dim(b, lax.dynamic_index_in_dim(p, chip, 0, keepdims=False), chip, 0)
                     for b, p in zip(scattered, partials)]
        halves = [_add_slots(p, f"add_slots_{tag}{a}") for a, p in enumerate(by_source)]
        joined = [lax.dynamic_update_index_in_dim(j, h, ci, 0) for j, h in zip(_join_halves(halves), halves)]
        return [j.reshape(w.shape) for j, w in zip(joined, big[first:])]

    conv_cols = conv_w.shape[2]
    mod_all, c_act, conv_all, gathered_in = _ada_exchange(
        jnp.broadcast_to(c, (8, D_MODEL)), w_ada[0], b_ada, jnp.pad(conv_w[0], ((0, 4), (0, 0))), shards[0])
    mod = lax.dynamic_slice_in_dim(mod_all, dev, 1, axis=0)
    conv_full = jnp.swapaxes(conv_all[:, :4, :], 0, 1).reshape(4, 4 * conv_cols)
    whole_in, = assemble([gathered_in], 0)
    loss, grad_x, grads, dmod, (partials, scattered) = _local_step(
        x[0], loss_target[0], mod, norm_attn_g, whole_in, rel_bias, conv_full, a_log, dt_bias, delta_norm_g,
        norm_ffn_g, final_norm_g[None], shards[1:], functools.partial(assemble, first=1), reduce_pairs)

    big_grads = finish(partials, scattered, 0, "all")

    pieces = [dmod, grads["conv_w"], grads["norm_attn_g"], grads["norm_ffn_g"], grads["final_norm_g"],
              grads["rel_bias"], grads["a_log"], grads["dt_bias"], grads["delta_norm_g"]]
    flat = [jnp.pad(p.reshape(-1), (0, -p.size % LANES)) for p in pieces]
    n_rows = [f.size // LANES for f in flat]
    packed = jnp.concatenate(flat).reshape(-1, LANES)
    packed = jnp.pad(packed, ((0, -packed.shape[0] % 8), (0, 0)))
    all_small, total = _gather_small(packed)
    sums, start = [], 0
    for p, n in zip(pieces, n_rows):
        sums.append(total[start:start + n].reshape(-1)[:p.size].reshape(p.shape))
        start += n
    g_b_ada, g_conv, g_norm_attn, g_norm_ffn, g_final, g_rel, g_alog, g_dt, g_dnorm = sums
    dmod_all = all_small[:, :n_rows[0], :].reshape(8, -1)
    ada_cols = w_ada.shape[2]
    g_w_ada = _ada_weight_grad(c_act, lax.dynamic_slice_in_dim(dmod_all, chip * ada_cols, ada_cols, axis=1))
    g_conv = lax.dynamic_slice_in_dim(g_conv, chip * conv_cols, conv_cols, axis=1)

    grad = {"w_ada": g_w_ada[None], "b_ada": g_b_ada, "norm_attn_g": g_norm_attn,
            "rel_bias": g_rel, "conv_w": g_conv[None], "a_log": g_alog, "dt_bias": g_dt, "delta_norm_g": g_dnorm,
            "norm_ffn_g": g_norm_ffn, "final_norm_g": g_final.reshape(-1)}
    weight = {"w_ada": w_ada, "b_ada": b_ada, "norm_attn_g": norm_attn_g, "w_in": w_in, "rel_bias": rel_bias,
              "conv_w": conv_w, "a_log": a_log, "dt_bias": dt_bias, "delta_norm_g": delta_norm_g, "w_out": w_out,
              "norm_ffn_g": norm_ffn_g, "w_gate": w_gate, "w_up": w_up, "w_down": w_down, "final_norm_g": final_norm_g}
    first = {"w_ada": m_w_ada, "b_ada": m_b_ada, "norm_attn_g": m_norm_attn_g, "w_in": m_w_in, "rel_bias": m_rel_bias,
             "conv_w": m_conv_w, "a_log": m_a_log, "dt_bias": m_dt_bias, "delta_norm_g": m_delta_norm_g,
             "w_out": m_w_out, "norm_ffn_g": m_norm_ffn_g, "w_gate": m_w_gate, "w_up": m_w_up, "w_down": m_w_down,
             "final_norm_g": m_final_norm_g}
    second = {"w_ada": v_w_ada, "b_ada": v_b_ada, "norm_attn_g": v_norm_attn_g, "w_in": v_w_in, "rel_bias": v_rel_bias,
              "conv_w": v_conv_w, "a_log": v_a_log, "dt_bias": v_dt_bias, "delta_norm_g": v_delta_norm_g,
              "w_out": v_w_out, "norm_ffn_g": v_norm_ffn_g, "w_gate": v_w_gate, "w_up": v_w_up, "w_down": v_w_down,
              "final_norm_g": v_final_norm_g}
    delta, new_m, new_v = {}, {}, {}
    for name, w in weight.items():
        if name in big_names:
            continue
        two_d = (-1, w.shape[-1])
        d, nm, nv = _adamw(w.reshape(two_d), grad[name].reshape(two_d), first[name].reshape(two_d),
                           second[name].reshape(two_d), f"adamw_{name}")
        delta[name], new_m[name], new_v[name] = d.reshape(w.shape), nm.reshape(w.shape), nv.reshape(w.shape)
    for name, w, g, cols in zip(big_names, big, big_grads, by_cols):
        outs = _adamw(w, g, rows_form(first[name], cols), rows_form(second[name], cols), f"adamw_{name}")
        grad[name], delta[name], new_m[name], new_v[name] = [
            (jnp.swapaxes(o, 0, 1) if cols else o)[None] for o in (g, *outs)]

    names = list(weight)
    return (lax.psum(loss, ("x", "y", "c")), grad_x[None], *[grad[n] for n in names], *[delta[n] for n in names],
            *[new_m[n] for n in names], *[new_v[n] for n in names])
```

```python
import functools
import math

import numpy as np
import jax
import jax.numpy as jnp
from jax import lax
from jax.experimental import pallas as pl
from jax.experimental.pallas import tpu as pltpu

F32 = jnp.float32
BF16 = jnp.bfloat16
HIGHEST = lax.Precision.HIGHEST

D_MODEL = 1024
HEAD_DIM = 64
N_HEADS = 8
HEAD_W = 512
BRANCHES = ((128, 1), (512, 4), (2048, 16))
BAND = 128
ATT_TILE = 2048
ATT_UNROLL = 8
ATT_UNROLL_BWD = 4
N_BUCKETS = 32
MAX_DISTANCE = 2048
CHUNK = 64
D_FF = 2816
EPS = 1e-6
NEG_INF = -1e30
LANES = 128
VMEM_LIMIT = 56 * 1024 * 1024

ADAM_LR = 0.001
ADAM_B1 = 0.9
ADAM_B2 = 0.999
ADAM_EPS = 1e-08
ADAM_WD = 0.01
ADAM_STEP = 10


def _nn(a, b, precision=None):
    return jnp.dot(a, b, preferred_element_type=F32, precision=precision)


def _nt(a, b, precision=None):
    return lax.dot_general(a, b, (((1,), (1,)), ((), ())), preferred_element_type=F32, precision=precision)


def _tn(a, b, precision=None):
    return lax.dot_general(a, b, (((0,), (0,)), ((), ())), preferred_element_type=F32, precision=precision)


def _params(sem, vmem=VMEM_LIMIT):
    return pltpu.CompilerParams(dimension_semantics=sem, vmem_limit_bytes=vmem)


def _sigmoid(x):
    return 0.5 * jnp.tanh(0.5 * x) + 0.5


def _silu_and_slope(x):
    s = _sigmoid(x)
    return x * s, s * (1.0 + x * (1.0 - s))


def _silu(x):
    return x * _sigmoid(x)


def _attn_tables():
    qi = np.arange(BAND)[:, None]
    kj = np.arange(2 * BAND)[None, :]
    steps = qi + BAND - kj
    in_window = (steps >= 0) & (steps <= BAND)
    max_exact = N_BUCKETS // 2
    out = np.zeros((3, 2, BAND, 2 * BAND), np.int32)
    for b, (_, dil) in enumerate(BRANCHES):
        dist = np.maximum(steps, 0) * dil
        dist_f = np.maximum(dist, 1).astype(np.float32)
        large = max_exact + (np.log(dist_f / np.float32(max_exact)) / np.float32(math.log(MAX_DISTANCE / max_exact))
                             * np.float32(N_BUCKETS - max_exact)).astype(np.int32)
        bucket = np.where(dist < max_exact, dist, np.minimum(large, N_BUCKETS - 1)).astype(np.int32)
        out[b, 0] = np.where(in_window, bucket, -1)
        out[b, 1] = np.where(in_window & (kj >= BAND), bucket, -1)
    return out


def _attention_bias(rel_bias, tables):
    def body(rel_ref, tab_ref, out_ref):
        head = pl.program_id(0)
        for b in range(3):
            tab = tab_ref[b, 0]

            def pick(kk, acc, tab=tab):
                return jnp.where(tab == kk, rel_ref[kk, head], acc)

            acc = lax.fori_loop(0, N_BUCKETS, pick, jnp.zeros((BAND, 2 * BAND), F32))
            for first in range(2):
                out_ref[0, b, first] = jnp.where(tab_ref[b, first] < 0, NEG_INF, acc)

    return pl.pallas_call(
        body,
        grid=(N_HEADS,),
        in_specs=[pl.BlockSpec(memory_space=pltpu.SMEM),
                  pl.BlockSpec((3, 2, BAND, 2 * BAND), lambda h: (0, 0, 0, 0))],
        out_specs=pl.BlockSpec((1, 3, 2, BAND, 2 * BAND), lambda h: (h, 0, 0, 0, 0)),
        out_shape=jax.ShapeDtypeStruct((N_HEADS, 3, 2, BAND, 2 * BAND), F32),
        compiler_params=_params(("arbitrary",)),
        name="attn_bias",
    )(rel_bias, tables)


def _bias_spec():
    return pl.BlockSpec((2, 3, 2, BAND, 2 * BAND), lambda p, t: (p, 0, 0, 0, 0))


def _attn_block_index(idx, t, r):
    nb = ATT_TILE // (BAND * r)
    rho = idx // nb
    n = idx % nb
    qs = rho + r * BAND * n
    gs = t * ATT_TILE + qs
    first = (t * nb + n) == 0
    ps = jnp.where(first, gs, gs - r * BAND)
    return qs, gs, ps, first.astype(jnp.int32)


def _rows(start, r):
    return pl.ds(start, BAND) if r == 1 else pl.ds(start, BAND, stride=r)


def _attention_fwd(qkv, bias, shards):
    seq = qkv.shape[0]
    n_tiles = seq // ATT_TILE
    n = len(shards)

    def body(*refs):
        bias_ref, q_ref, k_ref, v_ref = refs[:4]
        y_ref, lse_ref = refs[4 + n:6 + n]
        o_s, l_s = refs[6 + 2 * n:8 + 2 * n]
        riding = (refs[4:4 + n], refs[6 + n:6 + 2 * n], *refs[8 + 2 * n:])
        pair = pl.program_id(0)
        t = pl.program_id(1)
        if n:
            @pl.when((pair == 0) & (t == 0))
            def _():
                for cp in _gather_copies(*riding, hand_over=False)[0]:
                    cp.start()

            @pl.when((pair == 2) & (t == 0))
            def _():
                for cp, fwd in zip(*_gather_copies(*riding)):
                    cp.wait_recv()
                    fwd.start()

        lane = lax.broadcasted_iota(jnp.int32, (1, LANES), 1)
        head0 = lane < HEAD_DIM
        masks = (head0, jnp.logical_not(head0))
        ones = jnp.ones((2 * BAND, LANES), BF16)
        for b, (_, r) in enumerate(BRANCHES):
            def blocks(it, carry, b=b, r=r):
                idx = [_attn_block_index(it * ATT_UNROLL + j, t, r) for j in range(ATT_UNROLL)]
                qb = [q_ref[_rows(qs, r), :] * (HEAD_DIM ** -0.5) for qs, _, _, _ in idx]
                kcat = [jnp.concatenate([k_ref[_rows(ps, r), :], k_ref[_rows(gs, r), :]], axis=0).astype(BF16)
                        for _, gs, ps, _ in idx]
                vcat = [jnp.concatenate([v_ref[_rows(ps, r), :], v_ref[_rows(gs, r), :]], axis=0).astype(BF16)
                        for _, gs, ps, _ in idx]
                work = [(j, hh) for j in range(ATT_UNROLL) for hh in range(2)]
                s = [_nt(jnp.where(masks[hh], qb[j], 0.0).astype(BF16), kcat[j]) + bias_ref[hh, b, idx[j][3]]
                     for j, hh in work]
                m = [jnp.max(sv, axis=-1, keepdims=True) for sv in s]
                e = [jnp.exp(sv - mv) for sv, mv in zip(s, m)]
                eb = [ev.astype(BF16) for ev in e]
                den = [_nn(ev, ones) for ev in eb]
                out = [_nn(ev, vcat[j]) / dv for ev, dv, (j, _) in zip(eb, den, work)]
                lse = [mv + jnp.log(dv) for mv, dv in zip(m, den)]
                for j in range(ATT_UNROLL):
                    o_s[b, _rows(idx[j][0], r), :] = jnp.where(head0, out[2 * j], out[2 * j + 1])
                    l_s[b, _rows(idx[j][0], r), :] = jnp.where(head0, lse[2 * j], lse[2 * j + 1])
                return carry

            lax.fori_loop(0, ATT_TILE // BAND // ATT_UNROLL, blocks, 0)

        def merge(i, carry):
            rows = pl.ds(pl.multiple_of(i * BAND, BAND), BAND)
            l0, l1, l2 = l_s[0, rows, :], l_s[1, rows, :], l_s[2, rows, :]
            m = jnp.maximum(jnp.maximum(l0, l1), l2)
            w0, w1, w2 = jnp.exp(l0 - m), jnp.exp(l1 - m), jnp.exp(l2 - m)
            tot = w0 + w1 + w2
            y_ref[rows, :] = (w0 * o_s[0, rows, :] + w1 * o_s[1, rows, :] + w2 * o_s[2, rows, :]) / tot
            lse_ref[rows, :] = m + jnp.log(tot)
            return carry

        lax.fori_loop(0, ATT_TILE // BAND, merge, 0)

        if n:
            @pl.when((pair == N_HEADS // 2 - 1) & (t == n_tiles - 1))
            def _():
                first, passed = _gather_copies(*riding)
                for cp in first:
                    cp.wait_send()
                for fwd in passed:
                    fwd.wait()

    tile = pl.BlockSpec((ATT_TILE, LANES), lambda p, t: (t, p))
    sems = [pltpu.SemaphoreType.DMA((6 * n,)), pltpu.SemaphoreType.DMA((6 * n,))] if n else []
    return pl.pallas_call(
        body,
        grid=(N_HEADS // 2, n_tiles),
        in_specs=[
            _bias_spec(),
            pl.BlockSpec((ATT_TILE, LANES), lambda p, t: (t, p)),
            pl.BlockSpec((seq, LANES), lambda p, t: (0, 4 + p)),
            pl.BlockSpec((seq, LANES), lambda p, t: (0, 8 + p)),
        ] + [ANY] * n,
        out_specs=[tile, tile] + [ANY] * n,
        out_shape=[jax.ShapeDtypeStruct((seq, HEAD_W), F32), jax.ShapeDtypeStruct((seq, HEAD_W), F32)]
        + _gathered_shapes(shards),
        scratch_shapes=[
            pltpu.VMEM((3, ATT_TILE, LANES), F32),
            pltpu.VMEM((3, ATT_TILE, LANES), F32),
        ] + sems,
        compiler_params=_params(("arbitrary", "arbitrary")),
        name="attn_fwd",
    )(bias, qkv, qkv, qkv, *shards)


def _attention_bwd(qkv, dy, y, lse, bias, partials):
    seq = qkv.shape[0]
    n_tiles = seq // ATT_TILE
    n = len(partials)

    def body(*refs):
        bias_ref, q_ref, k_ref, v_ref, dy_ref, y_ref, lse_ref = refs[:7]
        dq_ref, dk_ref, dv_ref, dbias_ref = refs[7 + n:11 + n]
        riding = (refs[7:7 + n], refs[11 + n:11 + 2 * n], *refs[11 + 2 * n:])
        pair = pl.program_id(0)
        t = pl.program_id(1)
        if n:
            @pl.when((pair == 0) & (t == 0))
            def _():
                for cp in _scatter_copies(*riding):
                    cp.start()

        lane = lax.broadcasted_iota(jnp.int32, (1, LANES), 1)
        head0 = lane < HEAD_DIM

        @pl.when(t == 0)
        def _():
            dk_ref[...] = jnp.zeros_like(dk_ref)
            dv_ref[...] = jnp.zeros_like(dv_ref)
            dbias_ref[...] = jnp.zeros_like(dbias_ref)

        dq_ref[...] = jnp.zeros_like(dq_ref)

        masks = (head0, jnp.logical_not(head0))
        ones = jnp.ones((LANES, LANES), BF16)
        scale = HEAD_DIM ** -0.5
        for b, (_, r) in enumerate(BRANCHES):
            def blocks(it, carry, b=b, r=r):
                idx = [_attn_block_index(it * ATT_UNROLL_BWD + j, t, r) for j in range(ATT_UNROLL_BWD)]
                qb = [q_ref[_rows(qs, r), :] * scale for qs, _, _, _ in idx]
                kcat = [jnp.concatenate([k_ref[_rows(ps, r), :], k_ref[_rows(gs, r), :]], axis=0).astype(BF16)
                        for _, gs, ps, _ in idx]
                vcat = [jnp.concatenate([v_ref[_rows(ps, r), :], v_ref[_rows(gs, r), :]], axis=0).astype(BF16)
                        for _, gs, ps, _ in idx]
                dob = [dy_ref[_rows(qs, r), :] for qs, _, _, _ in idx]
                ob = [y_ref[_rows(qs, r), :] for qs, _, _, _ in idx]
                lb = [lse_ref[_rows(qs, r), :] for qs, _, _, _ in idx]
                work = [(j, hh) for j in range(ATT_UNROLL_BWD) for hh in range(2)]
                qh = [jnp.where(masks[hh], qb[j], 0.0).astype(BF16) for j, hh in work]
                doh = [jnp.where(masks[hh], dob[j], 0.0) for j, hh in work]
                dohb = [d.astype(BF16) for d in doh]
                s = [_nt(qh[w], kcat[j]) + bias_ref[hh, b, idx[j][3]] for w, (j, hh) in enumerate(work)]
                dp = [_nt(dohb[w], vcat[j]) for w, (j, _) in enumerate(work)]
                lrot = [pltpu.roll(lv, HEAD_DIM, 1) for lv in lb]
                lcol = [jnp.where(masks[hh], lb[j], lrot[j]) for j, hh in work]
                parts = [_split(doh[w] * ob[j]) for w, (j, _) in enumerate(work)]
                delta = [_nn(hi, ones) + _nn(lo, ones) for hi, lo in parts]
                prob = [jnp.exp(sv - jnp.concatenate([lv, lv], axis=1)) for sv, lv in zip(s, lcol)]
                ds = [pv * (dv - jnp.concatenate([de, de], axis=1)) for pv, dv, de in zip(prob, dp, delta)]
                dsb = [d.astype(BF16) for d in ds]
                dq = [_nn(dsb[w], kcat[j]) for w, (j, _) in enumerate(work)]
                dkc = [_tn(dsb[w], qh[w]) for w in range(len(work))]
                dvc = [_tn(prob[w].astype(BF16), dohb[w]) for w in range(len(work))]
                for hh in range(2):
                    dbias_ref[0, b, hh] += sum(ds[w] for w, (_, head) in enumerate(work) if head == hh)
                for j in range(ATT_UNROLL_BWD):
                    qs, gs, ps, _ = idx[j]
                    dkcat = dkc[2 * j] + dkc[2 * j + 1]
                    dvcat = dvc[2 * j] + dvc[2 * j + 1]
                    dq_ref[_rows(qs, r), :] += jnp.where(head0, dq[2 * j], dq[2 * j + 1]) * scale
                    dk_ref[_rows(ps, r), :] += dkcat[:BAND]
                    dk_ref[_rows(gs, r), :] += dkcat[BAND:]
                    dv_ref[_rows(ps, r), :] += dvcat[:BAND]
                    dv_ref[_rows(gs, r), :] += dvcat[BAND:]
                return carry

            lax.fori_loop(0, ATT_TILE // BAND // ATT_UNROLL_BWD, blocks, 0)

        if n:
            @pl.when((pair == N_HEADS // 2 - 1) & (t == n_tiles - 1))
            def _():
                for cp in _scatter_copies(*riding):
                    cp.wait()

    tile = pl.BlockSpec((ATT_TILE, LANES), lambda p, t: (t, p))
    full = pl.BlockSpec((seq, LANES), lambda p, t: (0, p))
    sems = [pltpu.SemaphoreType.DMA((3 * n,)), pltpu.SemaphoreType.DMA((3 * n,))] if n else []
    return pl.pallas_call(
        body,
        grid=(N_HEADS // 2, n_tiles),
        in_specs=[
            _bias_spec(),
            pl.BlockSpec((ATT_TILE, LANES), lambda p, t: (t, p)),
            pl.BlockSpec((seq, LANES), lambda p, t: (0, 4 + p)),
            pl.BlockSpec((seq, LANES), lambda p, t: (0, 8 + p)),
            tile, tile, tile,
        ] + [ANY] * n,
        out_specs=[tile, full, full,
                   pl.BlockSpec((1, 3, 2, BAND, 2 * BAND), lambda p, t: (p, 0, 0, 0, 0))] + [ANY] * n,
        out_shape=[jax.ShapeDtypeStruct((seq, HEAD_W), F32)] * 3
        + [jax.ShapeDtypeStruct((N_HEADS // 2, 3, 2, BAND, 2 * BAND), F32)]
        + [jax.ShapeDtypeStruct(p.shape, p.dtype) for p in partials],
        scratch_shapes=sems,
        compiler_params=_params(("arbitrary", "arbitrary")),
        name="attn_bwd",
    )(bias, qkv, qkv, qkv, dy, y, lse, *partials)


def _rel_bias_grad(dbias, tables):
    def body(tab_ref, db_ref, out_ref):
        lane = lax.broadcasted_iota(jnp.int32, (1, LANES), 1)
        out_ref[...] = jnp.zeros_like(out_ref)
        for b in range(3):
            tab = tab_ref[b, 0]

            def head(h, carry, b=b, tab=tab):
                d = db_ref[h // 2, b, h % 2]
                sums = [jnp.sum(jnp.where(tab == kk, d, 0.0), keepdims=True) for kk in range(N_BUCKETS)]
                row = jnp.zeros((1, LANES), F32)
                for kk, s in enumerate(sums):
                    row = row + jnp.where(lane == kk, s, 0.0)
                out_ref[pl.ds(h, 1), :] += row
                return carry

            lax.fori_loop(0, N_HEADS, head, 0)

    return pl.pallas_call(
        body,
        out_shape=jax.ShapeDtypeStruct((N_HEADS, LANES), F32),
        compiler_params=pltpu.CompilerParams(vmem_limit_bytes=VMEM_LIMIT),
        name="rel_bias_grad",
    )(tables, dbias)


ROW_TILE = 512


def _head_sum_matrix():
    return (lax.broadcasted_iota(jnp.int32, (HEAD_W, HEAD_W), 0) // HEAD_DIM
            == lax.broadcasted_iota(jnp.int32, (HEAD_W, HEAD_W), 1) // HEAD_DIM).astype(F32)


def _head_spread_matrix(offset=0):
    return (lax.broadcasted_iota(jnp.int32, (LANES, HEAD_W), 0)
            == lax.broadcasted_iota(jnp.int32, (LANES, HEAD_W), 1) // HEAD_DIM + offset).astype(F32)


def _head_gather_matrix(offset=0):
    return (lax.broadcasted_iota(jnp.int32, (HEAD_W, LANES), 0) // HEAD_DIM + offset
            == lax.broadcasted_iota(jnp.int32, (HEAD_W, LANES), 1)).astype(F32)


def _split3(x):
    hi = x.astype(BF16)
    rest = x - hi.astype(F32)
    mid = rest.astype(BF16)
    return hi, mid, (rest - mid.astype(F32)).astype(BF16)


def _pick(x, onehot):
    m = onehot.astype(BF16)
    hi, mid, lo = _split3(x)
    return _nn(hi, m) + (_nn(mid, m) + _nn(lo, m))


def _pick_left(onehot, x):
    m = onehot.astype(BF16)
    hi, mid, lo = _split3(x)
    return _nn(m, hi) + (_nn(m, mid) + _nn(m, lo))


def _tri(lower, strict=False):
    r = lax.broadcasted_iota(jnp.int32, (CHUNK, CHUNK), 0)
    c = lax.broadcasted_iota(jnp.int32, (CHUNK, CHUNK), 1)
    if lower:
        return (c < r) if strict else (c <= r)
    return c >= r


def _softplus(z):
    return jnp.maximum(z, 0.0) + jnp.log(1.0 + jnp.exp(-jnp.abs(z)))


def _conv_taps(stage, w_ref, rows):
    return (w_ref[3:4, :] * stage[8:8 + rows, :] + w_ref[2:3, :] * stage[7:7 + rows, :]
            + w_ref[1:2, :] * stage[6:6 + rows, :] + w_ref[0:1, :] * stage[5:5 + rows, :])


def _l2_scale(xc, hsum):
    return lax.rsqrt(_pick(xc * xc, hsum) + EPS)


def _stage_rows(stage, x_ref, xp_ref, i):
    stage[0:8, :] = jnp.where(i == 0, 0.0, xp_ref[...])
    stage[8:8 + ROW_TILE, :] = x_ref[...]


def _delta_prep_fwd(qkvz, ba, conv_w, alog_row, dt_row):
    seq = qkvz.shape[0]
    qkv_w = 3 * HEAD_W

    def body(x_ref, xp_ref, ba_ref, w_ref, al_ref, dt_ref, out_ref, stage):
        i = pl.program_id(0)
        _stage_rows(stage, x_ref, xp_ref, i)
        act = _silu(_conv_taps(stage, w_ref, ROW_TILE))
        hsum, hspread = _head_sum_matrix(), _head_spread_matrix()
        qc, kc = act[:, :HEAD_W], act[:, HEAD_W:2 * HEAD_W]
        out_ref[0] = qc * _l2_scale(qc, hsum) * (HEAD_DIM ** -0.5)
        out_ref[1] = kc * _l2_scale(kc, hsum)
        out_ref[2] = act[:, 2 * HEAD_W:]
        bav = ba_ref[...]
        out_ref[3] = _pick(_sigmoid(bav), hspread)
        g8 = -jnp.exp(al_ref[...]) * _softplus(bav + dt_ref[...])
        gb = _pick(g8, _head_spread_matrix(N_HEADS))
        cum = _tri(True).astype(F32)
        for ch in range(ROW_TILE // CHUNK):
            rows = slice(ch * CHUNK, (ch + 1) * CHUNK)
            out_ref[4, rows, :] = _pick_left(cum, gb[rows])

    return pl.pallas_call(
        body,
        grid=(seq // ROW_TILE,),
        in_specs=[
            pl.BlockSpec((ROW_TILE, qkv_w), lambda i: (i, 0)),
            pl.BlockSpec((8, qkv_w), lambda i: (jnp.maximum(i * (ROW_TILE // 8) - 1, 0), 0)),
            pl.BlockSpec((ROW_TILE, LANES), lambda i: (i, 0)),
            pl.BlockSpec((4, qkv_w), lambda i: (0, 0)),
            pl.BlockSpec((1, LANES), lambda i: (0, 0)),
            pl.BlockSpec((1, LANES), lambda i: (0, 0)),
        ],
        out_specs=pl.BlockSpec((5, ROW_TILE, HEAD_W), lambda i: (0, i, 0)),
        out_shape=jax.ShapeDtypeStruct((5, seq, HEAD_W), F32),
        scratch_shapes=[pltpu.VMEM((ROW_TILE + 8, qkv_w), F32)],
        compiler_params=_params(("arbitrary",)),
        name="delta_prep_fwd",
    )(qkvz, qkvz, ba, conv_w, alog_row, dt_row)


def _split(x):
    hi = x.astype(BF16)
    return hi, (x - hi.astype(F32)).astype(BF16)


def _dot3(a, b, dot=_nn):
    return dot(a[0], b[0]) + (dot(a[0], b[1]) + dot(a[1], b[0]))


def _unit_lower_inverses(mats):
    eye = (lax.broadcasted_iota(jnp.int32, (CHUNK, CHUNK), 0)
           == lax.broadcasted_iota(jnp.int32, (CHUNK, CHUNK), 1)).astype(F32)
    invs = [eye - a for a in mats]
    powers = [_split(a) for a in mats]
    for step in range(5):
        squares = [_dot3(p, p) for p in powers]
        powers = [_split(s) for s in squares]
        invs = [inv + _dot3(_split(inv), p) for inv, p in zip(invs, powers)]
    return invs


def _chunk_terms(q, k, v, beta, gc):
    causal, strict = _tri(True), _tri(True, strict=True)
    e = jnp.exp(gc)
    g_last = jnp.broadcast_to(gc[CHUNK - 1:CHUNK, :], (CHUNK, CHUNK))
    f = jnp.exp(g_last - gc)
    e_last = jnp.exp(g_last)
    decay = jnp.where(causal, jnp.exp(jnp.where(causal, gc - gc.T, 0.0)), 0.0)
    kb = k * beta
    a_mat = jnp.where(strict, _nt(kb.astype(BF16), k.astype(BF16)) * decay, 0.0)
    qk = jnp.where(causal, _nt(q.astype(BF16), k.astype(BF16)) * decay, 0.0)
    return e, f, e_last, decay, kb, a_mat, qk


GROUP = 8
UNROLL = 8


def _chunk_rows(ci):
    return pl.ds(pl.multiple_of(ci * CHUNK, CHUNK), CHUNK)


def _pair_specs(n_planes):
    return pl.BlockSpec((n_planes, GROUP * CHUNK, LANES), lambda p, g: (0, g, p))


def _delta_chunk_fwd(xs):
    seq = xs.shape[1]
    rows_per_step = GROUP * CHUNK

    def body(x_ref, inv_ref, qk_ref, u_ref, w_ref):
        work = [(hh, slice(step * CHUNK, (step + 1) * CHUNK)) for hh in range(2) for step in range(GROUP)]
        xh = [[x_ref[j, r, hh * HEAD_DIM:(hh + 1) * HEAD_DIM] for j in range(5)] for hh, r in work]
        terms = [_chunk_terms(*x) for x in xh]
        invs = _unit_lower_inverses([t[5] for t in terms])
        for (hh, r), x, t, inv in zip(work, xh, terms, invs):
            e, kb, qk = t[0], t[4], t[6]
            inv_parts = _split(inv)
            inv_ref[hh, r, :] = inv
            qk_ref[hh, r, :] = qk
            u_ref[hh, r, :] = _dot3(inv_parts, _split(x[2] * x[3]))
            w_ref[hh, r, :] = _dot3(inv_parts, _split(kb * e))

    out = pl.BlockSpec((2, rows_per_step, HEAD_DIM), lambda p, g: (p, g, 0))
    return pl.pallas_call(
        body,
        grid=(N_HEADS // 2, seq // rows_per_step),
        in_specs=[_pair_specs(5)],
        out_specs=[out] * 4,
        out_shape=[jax.ShapeDtypeStruct((N_HEADS, seq, HEAD_DIM), F32)] * 4,
        compiler_params=_params(("parallel", "parallel")),
        name="delta_chunk_fwd",
    )(xs)


def _decays(gc):
    g_last = jnp.broadcast_to(gc[CHUNK - 1:CHUNK, :], (CHUNK, CHUNK))
    return jnp.exp(gc), jnp.exp(g_last - gc), jnp.exp(g_last)


def _token_blocks(index, n_steps=None):
    rows_per_step = GROUP * CHUNK
    if n_steps is None:
        return pl.BlockSpec((1, rows_per_step, HEAD_W), lambda g: (index, g, 0))
    return pl.BlockSpec((1, rows_per_step, HEAD_W), lambda g: (index, n_steps - 1 - g, 0))


def _head_lanes(h):
    return pl.ds(h * HEAD_DIM, HEAD_DIM)


def _delta_scan_fwd(xs, qk_h, u_h, w_h):
    seq = xs.shape[1]
    rows_per_step = GROUP * CHUNK

    def body(q_ref, k_ref, gc_ref, qk_ref, u_ref, w_ref, o_ref, st_ref, state):
        @pl.when(pl.program_id(0) == 0)
        def _():
            state[...] = jnp.zeros_like(state)

        def chunk(ci, carry):
            rows = _chunk_rows(ci)
            heads = range(N_HEADS)
            dec = [_decays(gc_ref[0, rows, _head_lanes(h)]) for h in heads]
            s = [state[h] for h in heads]
            sb = [s[h].astype(BF16) for h in heads]
            vnb = [(u_ref[h, rows, :] - _nn(w_ref[h, rows, :].astype(BF16), sb[h])).astype(BF16) for h in heads]
            for h in heads:
                o_ref[rows, _head_lanes(h)] = (_nn((q_ref[0, rows, _head_lanes(h)] * dec[h][0]).astype(BF16), sb[h])
                                               + _nn(qk_ref[h, rows, :].astype(BF16), vnb[h]))
                st_ref[h, rows, :] = s[h]
            for h in heads:
                state[h] = s[h] * dec[h][2] + _tn((k_ref[0, rows, _head_lanes(h)] * dec[h][1]).astype(BF16), vnb[h])
            return carry

        lax.fori_loop(0, GROUP, chunk, 0)

    blk = pl.BlockSpec((N_HEADS, rows_per_step, HEAD_DIM), lambda g: (0, g, 0))
    return pl.pallas_call(
        body,
        grid=(seq // rows_per_step,),
        in_specs=[_token_blocks(0), _token_blocks(1), _token_blocks(4), blk, blk, blk],
        out_specs=[pl.BlockSpec((rows_per_step, HEAD_W), lambda g: (g, 0)), blk],
        out_shape=[jax.ShapeDtypeStruct((seq, HEAD_W), F32), jax.ShapeDtypeStruct((N_HEADS, seq, HEAD_DIM), F32)],
        scratch_shapes=[pltpu.VMEM((N_HEADS, CHUNK, CHUNK), F32)],
        compiler_params=_params(("arbitrary",)),
        name="delta_scan_fwd",
    )(xs, xs, xs, qk_h, u_h, w_h)


def _delta_scan_bwd(xs, qk_h, w_h, do):
    seq = xs.shape[1]
    rows_per_step = GROUP * CHUNK
    n_steps = seq // rows_per_step

    def body(q_ref, k_ref, gc_ref, qk_ref, w_ref, do_ref, dsn_ref, dvn_ref, dstate):
        @pl.when(pl.program_id(0) == 0)
        def _():
            dstate[...] = jnp.zeros_like(dstate)

        def chunk(step, carry):
            rows = _chunk_rows(GROUP - 1 - step)
            heads = range(N_HEADS)
            dec = [_decays(gc_ref[0, rows, _head_lanes(h)]) for h in heads]
            ds_next = [dstate[h] for h in heads]
            dob = [do_ref[rows, _head_lanes(h)].astype(BF16) for h in heads]
            dv_new = [_tn(qk_ref[h, rows, :].astype(BF16), dob[h])
                      + _nn((k_ref[0, rows, _head_lanes(h)] * dec[h][1]).astype(BF16), ds_next[h].astype(BF16))
                      for h in heads]
            for h in heads:
                dsn_ref[h, rows, :] = ds_next[h]
                dvn_ref[h, rows, :] = dv_new[h]
            for h in heads:
                dstate[h] = (_tn((q_ref[0, rows, _head_lanes(h)] * dec[h][0]).astype(BF16), dob[h])
                             + dec[h][2] * ds_next[h] - _tn(w_ref[h, rows, :].astype(BF16), dv_new[h].astype(BF16)))
            return carry

        lax.fori_loop(0, GROUP, chunk, 0)

    blk = pl.BlockSpec((N_HEADS, rows_per_step, HEAD_DIM), lambda g: (0, n_steps - 1 - g, 0))
    return pl.pallas_call(
        body,
        grid=(n_steps,),
        in_specs=[_token_blocks(0, n_steps), _token_blocks(1, n_steps), _token_blocks(4, n_steps), blk, blk,
                  pl.BlockSpec((rows_per_step, HEAD_W), lambda g: (n_steps - 1 - g, 0))],
        out_specs=[blk, blk],
        out_shape=[jax.ShapeDtypeStruct((N_HEADS, seq, HEAD_DIM), F32)] * 2,
        scratch_shapes=[pltpu.VMEM((N_HEADS, CHUNK, CHUNK), F32)],
        compiler_params=_params(("arbitrary",)),
        name="delta_scan_bwd",
    )(xs, xs, xs, qk_h, w_h, do)


def _delta_chunk_bwd(xs, inv_h, u_h, w_h, st_h, dsn_h, dvn_h, do):
    seq = xs.shape[1]
    rows_per_step = GROUP * CHUNK

    def body(x_ref, inv_ref, u_ref, w_ref, st_ref, dsn_ref, dvn_ref, do_ref, dx_ref):
        causal, strict = _tri(True), _tri(True, strict=True)
        last_row = lax.broadcasted_iota(jnp.int32, (CHUNK, CHUNK), 0) == CHUNK - 1

        def bf(vals):
            return [val.astype(BF16) for val in vals]

        def group(items):
            heads = [hh for hh, _ in items]
            lanes = [slice(hh * HEAD_DIM, (hh + 1) * HEAD_DIM) for hh in heads]
            rows = [slice(step * CHUNK, (step + 1) * CHUNK) for _, step in items]
            n = range(len(items))
            q, k, v, beta, gc = [[x_ref[j, rows[i], lanes[i]] for i in n] for j in range(5)]
            terms = [_chunk_terms(q[i], k[i], v[i], beta[i], gc[i]) for i in n]
            e, f, e_last, decay, kb, a_mat, qk = [[t[j] for t in terms] for j in range(7)]
            inv = [_split(inv_ref[heads[i], rows[i], :]) for i in n]
            u = [u_ref[heads[i], rows[i], :] for i in n]
            w = [w_ref[heads[i], rows[i], :] for i in n]
            s = [st_ref[heads[i], rows[i], :] for i in n]
            ds_next = [dsn_ref[heads[i], rows[i], :] for i in n]
            dv_new = [dvn_ref[heads[i], rows[i], :] for i in n]
            sb, dsb, dvb, wb = bf(s), bf(ds_next), bf(dv_new), bf(w)
            dob = bf([do_ref[rows[i], lanes[i]] for i in n])
            qbf, kbf, kbb = bf(q), bf(k), bf(kb)
            vnb = bf([u[i] - _nn(wb[i], sb[i]) for i in n])
            dqe = [_nt(dob[i], sb[i]) for i in n]
            dw = [-_nt(dvb[i], sb[i]) for i in n]
            dkf = [_nt(vnb[i], dsb[i]) for i in n]
            dqk = [jnp.where(causal, _nt(dob[i], vnb[i]), 0.0) for i in n]
            drhs_u = [_dot3(inv[i], _split(dv_new[i]), _tn) for i in n]
            drhs_w = [_dot3(inv[i], _split(dw[i]), _tn) for i in n]
            da = [-jnp.where(strict, _nt(drhs_u[i].astype(BF16), u[i].astype(BF16))
                             + _nt(drhs_w[i].astype(BF16), wb[i]), 0.0) for i in n]
            dad = bf([da[i] * decay[i] for i in n])
            dqd = bf([dqk[i] * decay[i] for i in n])
            dkb = [e[i] * drhs_w[i] + _nn(dad[i], kbf[i]) for i in n]
            dk = [_tn(dad[i], kbb[i]) + _tn(dqd[i], qbf[i]) + f[i] * dkf[i] + beta[i] * dkb[i] for i in n]
            dq = [_nn(dqd[i], kbf[i]) + e[i] * dqe[i] for i in n]
            for i in n:
                de_full = kb[i] * drhs_w[i] + q[i] * dqe[i]
                df_full = k[i] * dkf[i]
                m = da[i] * a_mat[i] + dqk[i] * qk[i]
                dgc = de_full * e[i] - df_full * f[i] + m - m.T
                tail = jnp.sum(df_full * f[i] + s[i] * ds_next[i] * e_last[i], axis=0, keepdims=True)
                dgc = dgc + jnp.where(last_row, jnp.broadcast_to(tail, (CHUNK, CHUNK)), 0.0)
                dx_ref[0, rows[i], lanes[i]] = dq[i]
                dx_ref[1, rows[i], lanes[i]] = dk[i]
                dx_ref[2, rows[i], lanes[i]] = beta[i] * drhs_u[i]
                dx_ref[3, rows[i], lanes[i]] = v[i] * drhs_u[i] + k[i] * dkb[i]
                dx_ref[4, rows[i], lanes[i]] = dgc

        work = [(hh, step) for hh in range(2) for step in range(GROUP)]
        for first in range(0, len(work), UNROLL):
            group(work[first:first + UNROLL])

    blk = pl.BlockSpec((2, rows_per_step, HEAD_DIM), lambda p, g: (p, g, 0))
    return pl.pallas_call(
        body,
        grid=(N_HEADS // 2, seq // rows_per_step),
        in_specs=[_pair_specs(5)] + [blk] * 6 + [pl.BlockSpec((rows_per_step, LANES), lambda p, g: (g, p))],
        out_specs=_pair_specs(5),
        out_shape=jax.ShapeDtypeStruct((5, seq, HEAD_W), F32),
        compiler_params=_params(("parallel", "parallel")),
        name="delta_chunk_bwd",
    )(xs, inv_h, u_h, w_h, st_h, dsn_h, dvn_h, do)


def _delta_post_fwd(o, qkvz, gain_row):
    seq = o.shape[0]

    def body(o_ref, z_ref, g_ref, y_ref):
        ov = o_ref[...]
        rb = lax.rsqrt(_pick(ov * ov, _head_sum_matrix()) * (1.0 / HEAD_DIM) + EPS)
        y_ref[...] = (ov * rb * g_ref[...] * _silu(z_ref[...])).astype(y_ref.dtype)

    tile = pl.BlockSpec((ROW_TILE, HEAD_W), lambda i: (i, 0))
    return pl.pallas_call(
        body,
        grid=(seq // ROW_TILE,),
        in_specs=[tile, pl.BlockSpec((ROW_TILE, HEAD_W), lambda i: (i, 3)), pl.BlockSpec((1, HEAD_W), lambda i: (0, 0))],
        out_specs=tile,
        out_shape=jax.ShapeDtypeStruct((seq, HEAD_W), BF16),
        compiler_params=_params(("arbitrary",)),
        name="delta_post_fwd",
    )(o, qkvz, gain_row)


def _delta_post_bwd(dy, o, qkvz, gain_row):
    seq = o.shape[0]

    def body(dy_ref, o_ref, z_ref, g_ref, do_ref, dz_ref, dg_ref):
        @pl.when(pl.program_id(0) == 0)
        def _():
            dg_ref[...] = jnp.zeros_like(dg_ref)

        ov, zv, dyv, gain = o_ref[...], z_ref[...], dy_ref[...], g_ref[...]
        hsum = _head_sum_matrix()
        rb = lax.rsqrt(_pick(ov * ov, hsum) * (1.0 / HEAD_DIM) + EPS)
        ohat = ov * rb
        silu_z, slope_z = _silu_and_slope(zv)
        dz_ref[...] = dyv * ohat * gain * slope_z
        dn = dyv * silu_z
        dg_ref[0:1, :] += jnp.sum(dn * ohat, axis=0, keepdims=True)
        dohat = dn * gain

        @pl.when(pl.program_id(0) == pl.num_programs(0) - 1)
        def _():
            fold = (lax.broadcasted_iota(jnp.int32, (HEAD_W, HEAD_W), 0) % HEAD_DIM
                    == lax.broadcasted_iota(jnp.int32, (HEAD_W, HEAD_W), 1)).astype(F32)
            dg_ref[1:2, :] = _pick(dg_ref[0:1, :], fold)

        proj = _pick(dohat * ohat, hsum) * (1.0 / HEAD_DIM)
        do_ref[...] = rb * (dohat - ohat * proj)

    tile = pl.BlockSpec((ROW_TILE, HEAD_W), lambda i: (i, 0))
    return pl.pallas_call(
        body,
        grid=(seq // ROW_TILE,),
        in_specs=[pl.BlockSpec((ROW_TILE, HEAD_W), lambda i: (i, 1)), tile,
                  pl.BlockSpec((ROW_TILE, HEAD_W), lambda i: (i, 3)), pl.BlockSpec((1, HEAD_W), lambda i: (0, 0))],
        out_specs=[tile, tile, pl.BlockSpec((2, HEAD_W), lambda i: (0, 0))],
        out_shape=[jax.ShapeDtypeStruct((seq, HEAD_W), F32), jax.ShapeDtypeStruct((seq, HEAD_W), F32),
                   jax.ShapeDtypeStruct((2, HEAD_W), F32)],
        compiler_params=_params(("arbitrary",)),
        name="delta_post_bwd",
    )(dy, o, qkvz, gain_row)


def _delta_prep_bwd(qkvz, ba, conv_w, alog_row, dt_row, dxs):
    seq = qkvz.shape[0]
    qkv_w = 3 * HEAD_W

    def body(x_ref, xp_ref, ba_ref, w_ref, al_ref, dt_ref, dx_ref, dconv_ref, dba_ref, dvec_ref, stage):
        i = pl.program_id(0)

        @pl.when(i == 0)
        def _():
            dvec_ref[...] = jnp.zeros_like(dvec_ref)

        _stage_rows(stage, x_ref, xp_ref, i)
        pre = _conv_taps(stage, w_ref, ROW_TILE)
        act, slope = _silu_and_slope(pre)
        hsum = _head_sum_matrix()
        for j, scale in ((0, HEAD_DIM ** -0.5), (1, 1.0)):
            cols = slice(j * HEAD_W, (j + 1) * HEAD_W)
            xc = act[:, cols]
            rb = _l2_scale(xc, hsum)
            xhat = xc * rb
            dhat = dx_ref[j] * scale
            proj = _pick(dhat * xhat, hsum)
            dconv_ref[:, cols] = rb * (dhat - xhat * proj) * slope[:, cols]
        dconv_ref[:, 2 * HEAD_W:] = dx_ref[2] * slope[:, 2 * HEAD_W:]

        bav = ba_ref[...]
        beta8 = _sigmoid(bav)
        dbeta8 = _pick(dx_ref[3], _head_gather_matrix())
        dgc8 = _pick(dx_ref[4], _head_gather_matrix(N_HEADS))
        rev = _tri(False).astype(F32)
        z = bav + dt_ref[...]
        ea = jnp.exp(al_ref[...])
        g8 = -ea * _softplus(z)
        sig = _sigmoid(z)
        d_alog = jnp.zeros((1, LANES), F32)
        d_dt = jnp.zeros((1, LANES), F32)
        for ch in range(ROW_TILE // CHUNK):
            rows = slice(ch * CHUNK, (ch + 1) * CHUNK)
            dg8 = _pick_left(rev, dgc8[rows])
            da = -dg8 * ea * sig[rows]
            dba_ref[rows, :] = dbeta8[rows] * beta8[rows] * (1.0 - beta8[rows]) + da
            d_alog = d_alog + jnp.sum(dg8 * g8[rows], axis=0, keepdims=True)
            d_dt = d_dt + jnp.sum(da, axis=0, keepdims=True)
        dvec_ref[0:1, :] += d_alog
        dvec_ref[1:2, :] += d_dt

    return pl.pallas_call(
        body,
        grid=(seq // ROW_TILE,),
        in_specs=[
            pl.BlockSpec((ROW_TILE, qkv_w), lambda i: (i, 0)),
            pl.BlockSpec((8, qkv_w), lambda i: (jnp.maximum(i * (ROW_TILE // 8) - 1, 0), 0)),
            pl.BlockSpec((ROW_TILE, LANES), lambda i: (i, 0)),
            pl.BlockSpec((4, qkv_w), lambda i: (0, 0)),
            pl.BlockSpec((1, LANES), lambda i: (0, 0)),
            pl.BlockSpec((1, LANES), lambda i: (0, 0)),
            pl.BlockSpec((5, ROW_TILE, HEAD_W), lambda i: (0, i, 0)),
        ],
        out_specs=[pl.BlockSpec((ROW_TILE, qkv_w), lambda i: (i, 0)),
                   pl.BlockSpec((ROW_TILE, LANES), lambda i: (i, 0)),
                   pl.BlockSpec((2, LANES), lambda i: (0, 0))],
        out_shape=[jax.ShapeDtypeStruct((seq, qkv_w), F32), jax.ShapeDtypeStruct((seq, LANES), F32),
                   jax.ShapeDtypeStruct((2, LANES), F32)],
        scratch_shapes=[pltpu.VMEM((ROW_TILE + 8, qkv_w), F32)],
        compiler_params=_params(("arbitrary",)),
        name="delta_prep_bwd",
    )(qkvz, qkvz, ba, conv_w, alog_row, dt_row, dxs)


def _conv_bwd(dconv, qkvz, conv_w):
    seq = dconv.shape[0]
    qkv_w = 3 * HEAD_W
    n_tiles = seq // ROW_TILE

    def body(dy_ref, dyn_ref, x_ref, xp_ref, w_ref, dx_ref, dw_ref, stage, dstage):
        i = pl.program_id(0)

        @pl.when(i == 0)
        def _():
            dw_ref[...] = jnp.zeros_like(dw_ref)

        _stage_rows(stage, x_ref, xp_ref, i)
        dstage[0:ROW_TILE, :] = dy_ref[...]
        dstage[ROW_TILE:ROW_TILE + 8, :] = jnp.where(i == n_tiles - 1, 0.0, dyn_ref[...])
        dy = dy_ref[...]
        dx_ref[...] = (w_ref[3:4, :] * dy + w_ref[2:3, :] * dstage[1:1 + ROW_TILE, :]
                       + w_ref[1:2, :] * dstage[2:2 + ROW_TILE, :] + w_ref[0:1, :] * dstage[3:3 + ROW_TILE, :])
        for j in range(4):
            dw_ref[j:j + 1, :] += jnp.sum(dy * stage[5 + j:5 + j + ROW_TILE, :], axis=0, keepdims=True)

    tile = pl.BlockSpec((ROW_TILE, qkv_w), lambda i: (i, 0))
    return pl.pallas_call(
        body,
        grid=(n_tiles,),
        in_specs=[
            tile,
            pl.BlockSpec((8, qkv_w), lambda i: (jnp.minimum((i + 1) * (ROW_TILE // 8), seq // 8 - 1), 0)),
            tile,
            pl.BlockSpec((8, qkv_w), lambda i: (jnp.maximum(i * (ROW_TILE // 8) - 1, 0), 0)),
            pl.BlockSpec((4, qkv_w), lambda i: (0, 0)),
        ],
        out_specs=[tile, pl.BlockSpec((4, qkv_w), lambda i: (0, 0))],
        out_shape=[jax.ShapeDtypeStruct((seq, qkv_w), F32), jax.ShapeDtypeStruct((4, qkv_w), F32)],
        scratch_shapes=[pltpu.VMEM((ROW_TILE + 8, qkv_w), F32), pltpu.VMEM((ROW_TILE + 8, qkv_w), F32)],
        compiler_params=_params(("arbitrary",)),
        name="conv_bwd",
    )(dconv, dconv, qkvz, qkvz, conv_w)


FF_TILE = 1408
WGRAD_ROWS = 1024


def _row(a):
    return pl.BlockSpec((1, a), lambda *_: (0, 0))


def _rms_fwd(xv, gain):
    rstd = lax.rsqrt(jnp.mean(xv * xv, axis=-1, keepdims=True) + EPS)
    xhat = xv * rstd
    return xhat, rstd, xhat * gain


def _rms_bwd(dnorm, xhat, rstd, gain):
    dxhat = dnorm * gain
    dx = rstd * (dxhat - xhat * jnp.mean(dxhat * xhat, axis=-1, keepdims=True))
    return dx, jnp.sum(dnorm * xhat, axis=0, keepdims=True)


IN_SPLITS = (0, 3 * HEAD_W, 7 * HEAD_W, 7 * HEAD_W + LANES)


def _inproj_fwd(x, gain, scale, shift, w_rows):
    seq = x.shape[0]

    def body(x_ref, g_ref, sc_ref, sh_ref, w_ref, h_ref, a_ref, d_ref, b_ref):
        _, _, norm = _rms_fwd(x_ref[...], g_ref[...])
        h = (norm * (1.0 + sc_ref[...]) + sh_ref[...]).astype(BF16)
        h_ref[...] = h
        for out_ref, lo, hi in zip((a_ref, d_ref, b_ref), IN_SPLITS[:-1], IN_SPLITS[1:]):
            out_ref[...] = _nt(h, w_ref[lo:hi, :])

    def rows(width):
        return pl.BlockSpec((ROW_TILE, width), lambda i: (i, 0))

    return pl.pallas_call(
        body,
        grid=(seq // ROW_TILE,),
        in_specs=[rows(D_MODEL), _row(D_MODEL), _row(D_MODEL), _row(D_MODEL),
                  pl.BlockSpec(w_rows.shape, lambda i: (0, 0))],
        out_specs=[rows(D_MODEL), rows(3 * HEAD_W), rows(4 * HEAD_W), rows(LANES)],
        out_shape=[jax.ShapeDtypeStruct((seq, D_MODEL), BF16), jax.ShapeDtypeStruct((seq, 3 * HEAD_W), F32),
                   jax.ShapeDtypeStruct((seq, 4 * HEAD_W), F32), jax.ShapeDtypeStruct((seq, LANES), F32)],
        compiler_params=_params(("arbitrary",)),
        name="inproj_fwd",
    )(x, gain, scale, shift, w_rows)


def _outproj_fwd(y_attn, y_delta, w_out, x, gate1, gain, scale, shift):
    seq = x.shape[0]

    def body(ya_ref, yd_ref, wa_ref, wd_ref, x_ref, g1_ref, g_ref, sc_ref, sh_ref, x1_ref, h_ref, y_ref):
        y = _nn(ya_ref[...].astype(BF16), wa_ref[...]) + _nn(yd_ref[...], wd_ref[...])
        x1 = x_ref[...] + g1_ref[...] * y
        _, _, norm = _rms_fwd(x1, g_ref[...])
        x1_ref[...] = x1
        h_ref[...] = (norm * (1.0 + sc_ref[...]) + sh_ref[...]).astype(BF16)
        y_ref[...] = y.astype(BF16)

    def rows(width):
        return pl.BlockSpec((ROW_TILE, width), lambda i: (i, 0))

    return pl.pallas_call(
        body,
        grid=(seq // ROW_TILE,),
        in_specs=[rows(HEAD_W), rows(HEAD_W),
                  pl.BlockSpec((HEAD_W, D_MODEL), lambda i: (0, 0)), pl.BlockSpec((HEAD_W, D_MODEL), lambda i: (1, 0)),
                  rows(D_MODEL), _row(D_MODEL), _row(D_MODEL), _row(D_MODEL), _row(D_MODEL)],
        out_specs=[rows(D_MODEL), rows(D_MODEL), rows(D_MODEL)],
        out_shape=[jax.ShapeDtypeStruct((seq, D_MODEL), F32), jax.ShapeDtypeStruct((seq, D_MODEL), BF16),
                   jax.ShapeDtypeStruct((seq, D_MODEL), BF16)],
        compiler_params=_params(("arbitrary",)),
        name="outproj_fwd",
    )(y_attn, y_delta, w_out, w_out, x, gate1, gain, scale, shift)


def _ffn_fwd(h2, w_gate, w_up, w_down, x1, gate2, final_gain, target):
    seq = h2.shape[0]
    n_rows, n_ff = seq // ROW_TILE, D_FF // FF_TILE

    def body(h_ref, wg_ref, wu_ref, wd_ref, x1_ref, g2_ref, gf_ref, t_ref, gate_ref, up_ref, dx2_ref, st_ref, acc):
        i, j = pl.program_id(0), pl.program_id(1)

        @pl.when((i == 0) & (j == 0))
        def _():
            st_ref[...] = jnp.zeros_like(st_ref)

        h = h_ref[...]
        gate = _nt(h, wg_ref[...])
        up = _nt(h, wu_ref[...])
        gate_ref[...] = gate.astype(BF16)
        up_ref[...] = up.astype(BF16)
        part = _nn((_silu(gate) * up).astype(BF16), wd_ref[...])

        @pl.when(j == 0)
        def _():
            acc[...] = part

        @pl.when(j > 0)
        def _():
            acc[...] += part

        @pl.when(j == n_ff - 1)
        def _():
            y2 = acc[...]
            x2 = x1_ref[...] + g2_ref[...] * y2
            xhat, rstd, out = _rms_fwd(x2, gf_ref[...])
            diff = out - t_ref[...]
            dx2, dgain = _rms_bwd(diff * (1.0 / D_MODEL), xhat, rstd, gf_ref[...])
            dx2_ref[...] = dx2
            st_ref[0:1, :] += dgain
            st_ref[1:2, :] += jnp.sum(dx2 * y2, axis=0, keepdims=True)
            st_ref[2:3, :] += jnp.sum(diff * diff, axis=0, keepdims=True) * (0.5 / D_MODEL)

        @pl.when((i == n_rows - 1) & (j == n_ff - 1))
        def _():
            st_ref[3:4, :] = jnp.broadcast_to(jnp.sum(st_ref[2:3, :], keepdims=True), (1, D_MODEL))

    def rows(width):
        return pl.BlockSpec((ROW_TILE, width), lambda i, j: (i, 0))

    ff = pl.BlockSpec((ROW_TILE, FF_TILE), lambda i, j: (i, j))
    return pl.pallas_call(
        body,
        grid=(n_rows, n_ff),
        in_specs=[rows(D_MODEL),
                  pl.BlockSpec((FF_TILE, D_MODEL), lambda i, j: (j, 0)), pl.BlockSpec((FF_TILE, D_MODEL), lambda i, j: (j, 0)),
                  pl.BlockSpec((FF_TILE, D_MODEL), lambda i, j: (j, 0)),
                  rows(D_MODEL), _row(D_MODEL), _row(D_MODEL), rows(D_MODEL)],
        out_specs=[ff, ff, rows(D_MODEL), pl.BlockSpec((8, D_MODEL), lambda i, j: (0, 0))],
        out_shape=[jax.ShapeDtypeStruct((seq, D_FF), BF16), jax.ShapeDtypeStruct((seq, D_FF), BF16),
                   jax.ShapeDtypeStruct((seq, D_MODEL), F32), jax.ShapeDtypeStruct((8, D_MODEL), F32)],
        scratch_shapes=[pltpu.VMEM((ROW_TILE, D_MODEL), F32)],
        compiler_params=_params(("arbitrary", "arbitrary")),
        name="ffn_fwd",
    )(h2, w_gate, w_up, w_down, x1, gate2, final_gain, target)


def _ffn_bwd(dx2, gate, up, w_gate, w_up, w_down, x1, y, gate2, gate1, gain, scale):
    seq = dx2.shape[0]

    def act_body(dx2_ref, g2_ref, gate_ref, up_ref, wd_ref, dgate_ref, dup_ref, dwd_ref):
        dy2 = (g2_ref[...] * dx2_ref[...]).astype(BF16)
        gate = gate_ref[...].astype(F32)
        up = up_ref[...].astype(F32)
        dact = _nt(dy2, wd_ref[...])
        silu, slope = _silu_and_slope(gate)
        dgate_ref[...] = (dact * up * slope).astype(BF16)
        dup_ref[...] = (dact * silu).astype(BF16)
        part = _tn((silu * up).astype(BF16), dy2)

        @pl.when(pl.program_id(1) == 0)
        def _():
            dwd_ref[...] = part

        @pl.when(pl.program_id(1) > 0)
        def _():
            dwd_ref[...] += part

    ff = pl.BlockSpec((ROW_TILE, FF_TILE), lambda j, i: (i, j))
    w_tile = pl.BlockSpec((FF_TILE, D_MODEL), lambda j, i: (j, 0))
    dgate, dup, dw_down = pl.pallas_call(
        act_body,
        grid=(D_FF // FF_TILE, seq // ROW_TILE),
        in_specs=[pl.BlockSpec((ROW_TILE, D_MODEL), lambda j, i: (i, 0)), _row(D_MODEL), ff, ff, w_tile],
        out_specs=[ff, ff, w_tile],
        out_shape=[jax.ShapeDtypeStruct((seq, D_FF), BF16)] * 2 + [jax.ShapeDtypeStruct((D_FF, D_MODEL), F32)],
        compiler_params=_params(("arbitrary", "arbitrary")),
        name="ffn_bwd_act",
    )(dx2, gate2, gate, up, w_down)

    def in_body(dgate_ref, dup_ref, wg_ref, wu_ref, dx2_ref, x1_ref, y_ref, g1_ref, g_ref, sc_ref,
                dx1_ref, dy_ref, st_ref):
        @pl.when(pl.program_id(0) == 0)
        def _():
            st_ref[...] = jnp.zeros_like(st_ref)

        dh = _nn(dgate_ref[...], wg_ref[...]) + _nn(dup_ref[...], wu_ref[...])
        xhat, rstd, norm = _rms_fwd(x1_ref[...], g_ref[...])
        dxn, dgain = _rms_bwd(dh * (1.0 + sc_ref[...]), xhat, rstd, g_ref[...])
        dx1 = dx2_ref[...] + dxn
        dx1_ref[...] = dx1
        dy_ref[...] = (g1_ref[...] * dx1).astype(BF16)
        st_ref[0:1, :] += jnp.sum(dh, axis=0, keepdims=True)
        st_ref[1:2, :] += jnp.sum(dh * norm, axis=0, keepdims=True)
        st_ref[2:3, :] += dgain
        st_ref[3:4, :] += jnp.sum(dx1 * y_ref[...].astype(F32), axis=0, keepdims=True)

    half_tile = ROW_TILE // 2

    def rows(width):
        return pl.BlockSpec((half_tile, width), lambda i: (i, 0))

    whole = pl.BlockSpec((D_FF, D_MODEL), lambda i: (0, 0))
    dx1, dy, stats = pl.pallas_call(
        in_body,
        grid=(seq // half_tile,),
        in_specs=[rows(D_FF), rows(D_FF), whole, whole, rows(D_MODEL), rows(D_MODEL), rows(D_MODEL),
                  _row(D_MODEL), _row(D_MODEL), _row(D_MODEL)],
        out_specs=[rows(D_MODEL), rows(D_MODEL), pl.BlockSpec((8, D_MODEL), lambda i: (0, 0))],
        out_shape=[jax.ShapeDtypeStruct((seq, D_MODEL), F32), jax.ShapeDtypeStruct((seq, D_MODEL), BF16),
                   jax.ShapeDtypeStruct((8, D_MODEL), F32)],
        compiler_params=_params(("arbitrary",)),
        name="ffn_bwd_in",
    )(dgate, dup, w_gate, w_up, dx2, x1, y, gate1, gain, scale)
    return dgate, dup, dw_down, dx1, dy, stats


def _outproj_bwd(dy, w_out, y_attn, y_delta):
    seq = dy.shape[0]

    def body(dy_ref, w_ref, ya_ref, yd_ref, out_ref, dw_ref):
        @pl.when(pl.program_id(0) == 0)
        def _():
            dw_ref[...] = jnp.zeros_like(dw_ref)

        dyv = dy_ref[...]
        out_ref[...] = _nt(dyv, w_ref[...])
        dw_ref[0:HEAD_W, :] += _tn(ya_ref[...].astype(BF16), dyv)
        dw_ref[HEAD_W:, :] += _tn(yd_ref[...], dyv)

    rows = pl.BlockSpec((ROW_TILE, D_MODEL), lambda i: (i, 0))
    half = pl.BlockSpec((ROW_TILE, HEAD_W), lambda i: (i, 0))
    whole = pl.BlockSpec((D_MODEL, D_MODEL), lambda i: (0, 0))
    return pl.pallas_call(
        body,
        grid=(seq // ROW_TILE,),
        in_specs=[rows, whole, half, half],
        out_specs=[rows, whole],
        out_shape=[jax.ShapeDtypeStruct((seq, D_MODEL), F32), jax.ShapeDtypeStruct((D_MODEL, D_MODEL), F32)],
        compiler_params=_params(("arbitrary",)),
        name="outproj_bwd",
    )(dy, w_out, y_attn, y_delta)


def _inproj_bwd(dq, dk, dv, dxd, dz, dba, w_rows, x, dx1, gain, scale, partials):
    seq = x.shape[0]
    n = len(partials)
    n_steps = seq // ROW_TILE

    def body(*refs):
        pieces, (w_ref, x_ref, dx1_ref, g_ref, sc_ref) = refs[:6], refs[6:11]
        gx_ref, st_ref = refs[11 + n:13 + n]
        riding = (refs[11:11 + n], refs[13 + n:13 + 2 * n], *refs[13 + 2 * n:])

        @pl.when(pl.program_id(0) == 0)
        def _():
            st_ref[...] = jnp.zeros_like(st_ref)
            for cp in (_scatter_copies(*riding) if n else []):
                cp.start()

        dh = _nn(jnp.concatenate([p[...].astype(BF16) for p in pieces], axis=1), w_ref[...])
        xhat, rstd, norm = _rms_fwd(x_ref[...], g_ref[...])
        dxn, dgain = _rms_bwd(dh * (1.0 + sc_ref[...]), xhat, rstd, g_ref[...])
        gx_ref[...] = dx1_ref[...] + dxn
        st_ref[0:1, :] += jnp.sum(dh, axis=0, keepdims=True)
        st_ref[1:2, :] += jnp.sum(dh * norm, axis=0, keepdims=True)
        st_ref[2:3, :] += dgain

        if n:
            @pl.when(pl.program_id(0) == n_steps - 1)
            def _():
                for cp in _scatter_copies(*riding):
                    cp.wait()

    def rows(width):
        return pl.BlockSpec((ROW_TILE, width), lambda i: (i, 0))

    sems = [pltpu.SemaphoreType.DMA((3 * n,)), pltpu.SemaphoreType.DMA((3 * n,))] if n else []
    return pl.pallas_call(
        body,
        grid=(n_steps,),
        in_specs=[rows(HEAD_W), rows(HEAD_W), rows(HEAD_W), rows(3 * HEAD_W), rows(HEAD_W), rows(LANES),
                  pl.BlockSpec(w_rows.shape, lambda i: (0, 0)), rows(D_MODEL), rows(D_MODEL), _row(D_MODEL),
                  _row(D_MODEL)]
        + [ANY] * n,
        out_specs=[rows(D_MODEL), pl.BlockSpec((8, D_MODEL), lambda i: (0, 0))] + [ANY] * n,
        out_shape=[jax.ShapeDtypeStruct((seq, D_MODEL), F32), jax.ShapeDtypeStruct((8, D_MODEL), F32)]
        + [jax.ShapeDtypeStruct(p.shape, p.dtype) for p in partials],
        scratch_shapes=sems,
        compiler_params=_params(("arbitrary",)),
        name="inproj_bwd",
    )(dq, dk, dv, dxd, dz, dba, w_rows, x, dx1, gain, scale, *partials)


def _weight_grad(a, b, name):
    seq, m = a.shape
    n = b.shape[1]
    tm = m if m <= 1536 else m // 2
    tn = n if n <= 1536 else n // 2
    rows = 2 * WGRAD_ROWS
    n_k = seq // rows

    def body(a_ref, b_ref, out_ref):
        part = _tn(a_ref[...].astype(BF16), b_ref[...].astype(BF16))

        @pl.when(pl.program_id(2) == 0)
        def _():
            out_ref[...] = part

        @pl.when(pl.program_id(2) > 0)
        def _():
            out_ref[...] += part

    return pl.pallas_call(
        body,
        grid=(m // tm, n // tn, n_k),
        in_specs=[pl.BlockSpec((rows, tm), lambda i, j, k: (k, i)),
                  pl.BlockSpec((rows, tn), lambda i, j, k: (k, j))],
        out_specs=pl.BlockSpec((tm, tn), lambda i, j, k: (i, j)),
        out_shape=jax.ShapeDtypeStruct((m, n), F32),
        compiler_params=_params(("arbitrary", "arbitrary", "arbitrary")),
        name=name,
    )(a, b)


def _weight_grad_stack(pieces, b, name):
    seq, n = b.shape
    widths = [a.shape[1] for a in pieces]
    starts = [sum(widths[:i]) for i in range(len(pieces))]

    def body(*refs):
        a_refs, b_ref, out_ref = refs[:len(pieces)], refs[len(pieces)], refs[len(pieces) + 1]

        @pl.when(pl.program_id(0) == 0)
        def _():
            out_ref[...] = jnp.zeros_like(out_ref)

        bb = b_ref[...].astype(BF16)
        for a_ref, start, width in zip(a_refs, starts, widths):
            out_ref[start:start + width, :] += _tn(a_ref[...].astype(BF16), bb)

    def rows(width):
        return pl.BlockSpec((WGRAD_ROWS, width), lambda k: (k, 0))

    return pl.pallas_call(
        body,
        grid=(seq // WGRAD_ROWS,),
        in_specs=[rows(w) for w in widths] + [rows(n)],
        out_specs=pl.BlockSpec((sum(widths), n), lambda k: (0, 0)),
        out_shape=jax.ShapeDtypeStruct((sum(widths), n), F32),
        compiler_params=_params(("arbitrary",)),
        name=name,
    )(*pieces, b)


def _adamw(w, g, m, v, name):
    n_rows, n_cols = w.shape
    if w.size <= 64 * 1024:
        block, grid, index = (n_rows, n_cols), (1,), lambda i: (0, 0)
    elif n_rows % 256 == 0:
        block, grid, index = (256, n_cols), (n_rows // 256,), lambda i: (i, 0)
    elif n_cols % 256 == 0:
        block, grid, index = (n_rows, 256), (n_cols // 256,), lambda i: (0, i)
    else:
        block, grid, index = (n_rows, n_cols), (1,), lambda i: (0, 0)

    def body(w_ref, g_ref, m_ref, v_ref, d_ref, nm_ref, nv_ref):
        gv = g_ref[...]
        nm = ADAM_B1 * m_ref[...] + (1.0 - ADAM_B1) * gv
        nv = ADAM_B2 * v_ref[...] + (1.0 - ADAM_B2) * (gv * gv)
        m_hat = nm / (1.0 - ADAM_B1 ** ADAM_STEP)
        v_hat = nv / (1.0 - ADAM_B2 ** ADAM_STEP)
        d_ref[...] = -ADAM_LR * (m_hat / (jnp.sqrt(v_hat) + ADAM_EPS) + ADAM_WD * w_ref[...])
        nm_ref[...] = nm
        nv_ref[...] = nv

    blk = pl.BlockSpec(block, index)
    shape = jax.ShapeDtypeStruct((n_rows, n_cols), F32)
    return pl.pallas_call(
        body,
        grid=grid,
        in_specs=[blk] * 4,
        out_specs=[blk] * 3,
        out_shape=[shape] * 3,
        compiler_params=_params(("arbitrary",)),
        name=name,
    )(w, g, m, v)


IN_WIDTH = 3600


def _local_step(x, target, mod, norm_attn_g, w_in, rel_bias, conv_w, a_log, dt_bias, delta_norm_g,
                norm_ffn_g, final_norm_g, shards, assemble, reduce_pairs):
    sh1, sc1, g1, sh2, sc2, g2 = [mod[:, i * D_MODEL:(i + 1) * D_MODEL] for i in range(6)]
    w_rows = jnp.pad(w_in, ((0, IN_SPLITS[-1] - IN_WIDTH), (0, 0)))
    tables = jnp.asarray(_attn_tables())
    alog_row = jnp.pad(a_log, ((0, 0), (N_HEADS, LANES - 2 * N_HEADS)))
    dt_row = jnp.pad(dt_bias, ((0, 0), (N_HEADS, LANES - 2 * N_HEADS)))
    gain_row = jnp.tile(delta_norm_g, (1, N_HEADS))

    h1, qkv_a, qkvz, ba = _inproj_fwd(x, norm_attn_g, sc1, sh1, w_rows)
    bias = _attention_bias(rel_bias, tables)
    y_attn, lse, *gathered = _attention_fwd(qkv_a, bias, shards)
    w_out, w_gate, w_up, w_down = assemble(gathered)
    xs = _delta_prep_fwd(qkvz, ba, conv_w, alog_row, dt_row)
    inv_h, qk_h, u_h, w_h = _delta_chunk_fwd(xs)
    o, st_h = _delta_scan_fwd(xs, qk_h, u_h, w_h)
    y_delta = _delta_post_fwd(o, qkvz, gain_row)
    x1, h2, y = _outproj_fwd(y_attn, y_delta, w_out, x, g1, norm_ffn_g, sc2, sh2)
    gate, up, dx2, st_f = _ffn_fwd(h2, w_gate, w_up, w_down, x1, g2, final_norm_g, target)

    dgate, dup, dw_down, dx1, dy, st_b = _ffn_bwd(dx2, gate, up, w_gate, w_up, w_down, x1, y, g2, g1, norm_ffn_g, sc2)
    dycat, dw_out = _outproj_bwd(dy, w_out, y_attn, y_delta)
    partials = reduce_pairs([dw_out, _weight_grad(dgate, h2, "wgrad_gate"), _weight_grad(dup, h2, "wgrad_up"),
                             dw_down], 1, "rest")
    grads = {}
    do, dz, dgain = _delta_post_bwd(dycat, o, qkvz, gain_row)
    dsn_h, dvn_h = _delta_scan_bwd(xs, qk_h, w_h, do)
    dxs = _delta_chunk_bwd(xs, inv_h, u_h, w_h, st_h, dsn_h, dvn_h, do)
    dconv, dba, dvec = _delta_prep_bwd(qkvz, ba, conv_w, alog_row, dt_row, dxs)
    dxd, grads["conv_w"] = _conv_bwd(dconv, qkvz, conv_w)
    dq, dk, dv, dbias, *scattered = _attention_bwd(qkv_a, dycat, y_attn, lse, bias, partials)
    partials_in = reduce_pairs([jnp.concatenate(
        [_weight_grad_stack([dq, dk, dv], h1, "wgrad_in_attn"),
         _weight_grad_stack([dxd, dz, dba], h1, "wgrad_in_delta")[:IN_WIDTH - 3 * HEAD_W]], axis=0)], 0, "in")
    grad_x, st_i, *scattered_in = _inproj_bwd(dq, dk, dv, dxd, dz, dba, w_rows, x, dx1, norm_attn_g, sc1,
                                              partials_in)
    grads["rel_bias"] = _rel_bias_grad(dbias, tables)[:, :N_BUCKETS].T
    grads["a_log"] = dvec[0:1, N_HEADS:2 * N_HEADS]
    grads["dt_bias"] = dvec[1:2, N_HEADS:2 * N_HEADS]
    grads["delta_norm_g"] = dgain[1:2, :HEAD_DIM]
    grads["norm_attn_g"] = st_i[2:3]
    grads["norm_ffn_g"] = st_b[2:3]
    grads["final_norm_g"] = st_f[0:1]
    dmod = jnp.concatenate([st_i[0:1], st_i[1:2], st_b[3:4], st_b[0:1], st_b[1:2], st_f[1:2]], axis=1)
    return st_f[3, 0], grad_x, grads, dmod, (partials_in + partials, scattered_in + scattered)


MESH = pl.DeviceIdType.MESH
OTHER_CHIPS = ((1, 0), (0, 1), (1, 1))
ALL_PEERS = tuple((m >> 2 & 1, m >> 1 & 1, m & 1) for m in range(1, 8))
ANY = pl.BlockSpec(memory_space=pl.ANY)
VMEM_SPEC = pl.BlockSpec(memory_space=pltpu.VMEM)


def _me():
    return lax.axis_index("x"), lax.axis_index("y"), lax.axis_index("c")


def _flip(pos, mask):
    return tuple(1 - p if m else p for p, m in zip(pos, mask))


def _remote(src, dst, send_sems, recv_sems, k, to):
    return pltpu.make_async_remote_copy(src_ref=src, dst_ref=dst, send_sem=send_sems.at[k], recv_sem=recv_sems.at[k],
                                        device_id=to, device_id_type=MESH)


def _ada_exchange(c8, w_ada, b_ada, conv8, shard):
    def body(c_ref, w_ref, b_ref, cv_ref, shard_ref, mod_ref, cact_ref, conv_ref, whole_ref,
             c_all, part_all, send_sems, recv_sems, ride_send, ride_recv):
        x, y, c = me = _me()
        dev = 4 * x + 2 * y + c
        chip = 2 * x + y
        riding = ([shard_ref], [whole_ref], ride_send, ride_recv)
        for cp in _gather_copies(*riding, hand_over=False)[0]:
            cp.start()
        c_all[dev] = c_ref[...]
        conv_ref[chip] = cv_ref[...]
        first = [_remote(c_ref, c_all.at[dev], send_sems, recv_sems, k, _flip(me, mask))
                 for k, mask in enumerate(ALL_PEERS)]
        first += [_remote(cv_ref, conv_ref.at[chip], send_sems, recv_sems, 7 + j, _flip(me, (*mask, 0)))
                  for j, mask in enumerate(OTHER_CHIPS)]
        for cp in first:
            cp.start()
        for cp in first:
            cp.wait()
        row = lax.broadcasted_iota(jnp.int32, (8, D_MODEL), 0)
        c_rows = jnp.zeros((8, D_MODEL), F32)
        for d in range(8):
            c_rows = jnp.where(row == d, c_all[d], c_rows)
        c_act = _silu(c_rows)
        cact_ref[...] = c_act
        part_all[chip] = _nn(c_act, w_ref[...], HIGHEST)
        second = [_remote(part_all.at[chip], part_all.at[chip], send_sems, recv_sems, 10 + j, _flip(me, (*mask, 0)))
                  for j, mask in enumerate(OTHER_CHIPS)]
        for cp in second:
            cp.start()
        for cp in second:
            cp.wait()
        cols = w_ref.shape[1]
        for k in range(4):
            mod_ref[:, k * cols:(k + 1) * cols] = part_all[k] + b_ref[:, k * cols:(k + 1) * cols]
        first, passed = _gather_copies(*riding)
        for cp, fwd in zip(first, passed):
            cp.wait_recv()
            fwd.start()
        for cp in first:
            cp.wait_send()
        for fwd in passed:
            fwd.wait()

    cols = w_ada.shape[1]
    return pl.pallas_call(
        body,
        in_specs=[VMEM_SPEC] * 4 + [ANY],
        out_specs=[VMEM_SPEC] * 3 + [ANY],
        out_shape=[jax.ShapeDtypeStruct((8, 4 * cols), F32), jax.ShapeDtypeStruct((8, D_MODEL), F32),
                   jax.ShapeDtypeStruct((4, 8, conv8.shape[1]), F32)] + _gathered_shapes([shard]),
        scratch_shapes=[pltpu.VMEM((8, 8, D_MODEL), F32), pltpu.VMEM((4, 8, cols), F32),
                        pltpu.SemaphoreType.DMA((13,)), pltpu.SemaphoreType.DMA((13,)),
                        pltpu.SemaphoreType.DMA((6,)), pltpu.SemaphoreType.DMA((6,))],
        compiler_params=pltpu.CompilerParams(vmem_limit_bytes=VMEM_LIMIT),
        name="ada_exchange",
    )(c8, w_ada, b_ada, conv8, shard)


def _gathered_shapes(shards):
    return [jax.ShapeDtypeStruct((4, *s.shape), s.dtype) for s in shards]


def _gather_copies(srcs, dsts, send_sems, recv_sems, hand_over=True):
    x, y, c = me = _me()
    chip = 2 * x + y
    sibling = _flip(me, (0, 0, 1))
    first, passed = [], []
    for a, (src, dst) in enumerate(zip(srcs, dsts)):
        for j, mask in enumerate(OTHER_CHIPS):
            to = _flip(me, (*mask, 0))
            first.append(_remote(src.at[c], dst.at[chip, c], send_sems, recv_sems, 6 * a + j, to))
            if hand_over:
                landed = dst.at[2 * to[0] + to[1], c]
                passed.append(_remote(landed, landed, send_sems, recv_sems, 6 * a + 3 + j, sibling))
    return first, passed


def _scatter_copies(srcs, dsts, send_sems, recv_sems):
    x, y, c = me = _me()
    chip = 2 * x + y
    copies = []
    for a, (src, dst) in enumerate(zip(srcs, dsts)):
        for j, mask in enumerate(OTHER_CHIPS):
            to = _flip(me, (*mask, 0))
            copies.append(_remote(src.at[2 * to[0] + to[1]], dst.at[chip], send_sems, recv_sems, 3 * a + j, to))
    return copies


def _start_and_wait(copies):
    for cp in copies:
        cp.start()
    for cp in copies:
        cp.wait()


def _swap_halves(grads):
    n = len(grads)

    def body(*refs):
        srcs, got = refs[:n], refs[n:2 * n]
        send_sems, recv_sems = refs[2 * n:]
        x, y, c = me = _me()
        _start_and_wait([_remote(srcs[a].at[:, 1 - c], got[a], send_sems, recv_sems, a, _flip(me, (0, 0, 1)))
                         for a in range(n)])

    return pl.pallas_call(
        body,
        in_specs=[ANY] * n,
        out_specs=[ANY] * n,
        out_shape=[jax.ShapeDtypeStruct((4, g.shape[2], g.shape[3]), g.dtype) for g in grads],
        scratch_shapes=[pltpu.SemaphoreType.DMA((n,)), pltpu.SemaphoreType.DMA((n,))],
        name=f"swap_halves_{n}",
    )(*grads)


def _join_halves(halves):
    n = len(halves)

    def body(*refs):
        srcs, dsts = refs[:n], refs[n:2 * n]
        send_sems, recv_sems = refs[2 * n:]
        x, y, c = me = _me()
        _start_and_wait([_remote(srcs[a], dsts[a].at[c], send_sems, recv_sems, a, _flip(me, (0, 0, 1)))
                         for a in range(n)])

    return pl.pallas_call(
        body,
        in_specs=[ANY] * n,
        out_specs=[ANY] * n,
        out_shape=[jax.ShapeDtypeStruct((2, *h.shape), h.dtype) for h in halves],
        scratch_shapes=[pltpu.SemaphoreType.DMA((n,)), pltpu.SemaphoreType.DMA((n,))],
        name=f"join_halves_{n}",
    )(*halves)


def _gather_small(packed):
    n_rows = packed.shape[0]

    def body(p_ref, all_ref, sum_ref, send_sems, recv_sems):
        x, y, c = me = _me()
        dev = 4 * x + 2 * y + c
        all_ref[dev] = p_ref[...]
        copies = [_remote(p_ref, all_ref.at[dev], send_sems, recv_sems, k, _flip(me, mask))
                  for k, mask in enumerate(ALL_PEERS)]
        for cp in copies:
            cp.start()
        for cp in copies:
            cp.wait()
        total = all_ref[0]
        for d in range(1, 8):
            total = total + all_ref[d]
        sum_ref[...] = total

    return pl.pallas_call(
        body,
        in_specs=[VMEM_SPEC],
        out_specs=[VMEM_SPEC, VMEM_SPEC],
        out_shape=[jax.ShapeDtypeStruct((8, n_rows, LANES), F32), jax.ShapeDtypeStruct((n_rows, LANES), F32)],
        scratch_shapes=[pltpu.SemaphoreType.DMA((7,)), pltpu.SemaphoreType.DMA((7,))],
        name="gather_small",
    )(packed)


def _add_pair(a, b, out_dtype, name):
    def body(a_ref, b_ref, o_ref):
        o_ref[...] = (a_ref[...] + b_ref[...]).astype(o_ref.dtype)

    blk = pl.BlockSpec((1, *a.shape[1:]), lambda i: (i, 0, 0))
    return pl.pallas_call(
        body, grid=(a.shape[0],), in_specs=[blk, blk], out_specs=blk,
        out_shape=jax.ShapeDtypeStruct(a.shape, out_dtype),
        compiler_params=_params(("arbitrary",)), name=name,
    )(a, b)


def _add_slots(a, name):
    def body(a_ref, o_ref):
        total = a_ref[0].astype(F32)
        for k in range(1, 4):
            total = total + a_ref[k].astype(F32)
        o_ref[...] = total

    return pl.pallas_call(
        body, in_specs=[VMEM_SPEC], out_specs=VMEM_SPEC,
        out_shape=jax.ShapeDtypeStruct(a.shape[1:], F32),
        compiler_params=pltpu.CompilerParams(vmem_limit_bytes=VMEM_LIMIT), name=name,
    )(a)


def _ada_weight_grad(c_act, dmod_cols):
    def body(c_ref, d_ref, o_ref):
        o_ref[...] = _tn(c_ref[...], d_ref[...], HIGHEST)

    return pl.pallas_call(
        body, in_specs=[VMEM_SPEC, VMEM_SPEC], out_specs=VMEM_SPEC,
        out_shape=jax.ShapeDtypeStruct((c_act.shape[1], dmod_cols.shape[1]), F32),
        compiler_params=pltpu.CompilerParams(vmem_limit_bytes=VMEM_LIMIT), name="ada_weight_grad",
    )(c_act, dmod_cols)


def kernel(x, c, w_ada, b_ada, norm_attn_g, w_in, rel_bias, conv_w, a_log, dt_bias, delta_norm_g, w_out, norm_ffn_g, w_gate, w_up, w_down, final_norm_g, loss_target, m_w_ada, m_b_ada, m_norm_attn_g, m_w_in, m_rel_bias, m_conv_w, m_a_log, m_dt_bias, m_delta_norm_g, m_w_out, m_norm_ffn_g, m_w_gate, m_w_up, m_w_down, m_final_norm_g, v_w_ada, v_b_ada, v_norm_attn_g, v_w_in, v_rel_bias, v_conv_w, v_a_log, v_dt_bias, v_delta_norm_g, v_w_out, v_norm_ffn_g, v_w_gate, v_w_up, v_w_down, v_final_norm_g):
    xi, yi, ci = _me()
    dev = 4 * xi + 2 * yi + ci
    chip = 2 * xi + yi

    big_names = ("w_in", "w_out", "w_gate", "w_up", "w_down")
    by_cols = (True, False, True, True, False)

    def rows_form(a, cols):
        return jnp.swapaxes(a[0], 0, 1) if cols else a[0]

    def halves_form(w):
        rows, lanes = w.shape
        if (rows // 2) % 16:
            rows, lanes = w.size // LANES, LANES
        return (2, rows // 2, lanes)

    big = [rows_form(w, cols) for w, cols in zip((w_in, w_out, w_gate, w_up, w_down), by_cols)]
    shards = [w.astype(BF16).reshape(halves_form(w)) for w in big]

    def assemble(gathered, first):
        return [lax.dynamic_update_index_in_dim(g, s, chip, 0).reshape(4 * w.shape[0], w.shape[1])
                for g, s, w in zip(gathered, shards[first:], big[first:])]

    def reduce_pairs(grads, first, tag):
        slots = [g.reshape(4, *halves_form(w)) for g, w in zip(grads, big[first:])]
        return [_add_pair(lax.dynamic_index_in_dim(s, ci, 1, keepdims=False), got, BF16, f"add_pair_{tag}{a}")
                for a, (s, got) in enumerate(zip(slots, _swap_halves(slots)))]

    def finish(partials, scattered, first, tag):
        by_source = [lax.dynamic_update_index_in_dim(b, lax.dynamic_index_in_dim(p, chip, 0, keepdims=False), chip, 0)
                     for b, p in zip(scattered, partials)]
        halves = [_add_slots(p, f"add_slots_{tag}{a}") for a, p in enumerate(by_source)]
        joined = [lax.dynamic_update_index_in_dim(j, h, ci, 0) for j, h in zip(_join_halves(halves), halves)]
        return [j.reshape(w.shape) for j, w in zip(joined, big[first:])]

    conv_cols = conv_w.shape[2]
    mod_all, c_act, conv_all, gathered_in = _ada_exchange(
        jnp.broadcast_to(c, (8, D_MODEL)), w_ada[0], b_ada, jnp.pad(conv_w[0], ((0, 4), (0, 0))), shards[0])
    mod = lax.dynamic_slice_in_dim(mod_all, dev, 1, axis=0)
    conv_full = jnp.swapaxes(conv_all[:, :4, :], 0, 1).reshape(4, 4 * conv_cols)
    whole_in, = assemble([gathered_in], 0)
    loss, grad_x, grads, dmod, (partials, scattered) = _local_step(
        x[0], loss_target[0], mod, norm_attn_g, whole_in, rel_bias, conv_full, a_log, dt_bias, delta_norm_g,
        norm_ffn_g, final_norm_g[None], shards[1:], functools.partial(assemble, first=1), reduce_pairs)

    big_grads = finish(partials, scattered, 0, "all")

    pieces = [dmod, grads["conv_w"], grads["norm_attn_g"], grads["norm_ffn_g"], grads["final_norm_g"],
              grads["rel_bias"], grads["a_log"], grads["dt_bias"], grads["delta_norm_g"]]
    flat = [jnp.pad(p.reshape(-1), (0, -p.size % LANES)) for p in pieces]
    n_rows = [f.size // LANES for f in flat]
    packed = jnp.concatenate(flat).reshape(-1, LANES)
    packed = jnp.pad(packed, ((0, -packed.shape[0] % 8), (0, 0)))
    all_small, total = _gather_small(packed)
    sums, start = [], 0
    for p, n in zip(pieces, n_rows):
        sums.append(total[start:start + n].reshape(-1)[:p.size].reshape(p.shape))
        start += n
    g_b_ada, g_conv, g_norm_attn, g_norm_ffn, g_final, g_rel, g_alog, g_dt, g_dnorm = sums
    dmod_all = all_small[:, :n_rows[0], :].reshape(8, -1)
    ada_cols = w_ada.shape[2]
    g_w_ada = _ada_weight_grad(c_act, lax.dynamic_slice_in_dim(dmod_all, chip * ada_cols, ada_cols, axis=1))
    g_conv = lax.dynamic_slice_in_dim(g_conv, chip * conv_cols, conv_cols, axis=1)

    grad = {"w_ada": g_w_ada[None], "b_ada": g_b_ada, "norm_attn_g": g_norm_attn,
            "rel_bias": g_rel, "conv_w": g_conv[None], "a_log": g_alog, "dt_bias": g_dt, "delta_norm_g": g_dnorm,
            "norm_ffn_g": g_norm_ffn, "final_norm_g": g_final.reshape(-1)}
    weight = {"w_ada": w_ada, "b_ada": b_ada, "norm_attn_g": norm_attn_g, "w_in": w_in, "rel_bias": rel_bias,
              "conv_w": conv_w, "a_log": a_log, "dt_bias": dt_bias, "delta_norm_g": delta_norm_g, "w_out": w_out,
              "norm_ffn_g": norm_ffn_g, "w_gate": w_gate, "w_up": w_up, "w_down": w_down, "final_norm_g": final_norm_g}
    first = {"w_ada": m_w_ada, "b_ada": m_b_ada, "norm_attn_g": m_norm_attn_g, "w_in": m_w_in, "rel_bias": m_rel_bias,
             "conv_w": m_conv_w, "a_log": m_a_log, "dt_bias": m_dt_bias, "delta_norm_g": m_delta_norm_g,
             "w_out": m_w_out, "norm_ffn_g": m_norm_ffn_g, "w_gate": m_w_gate, "w_up": m_w_up, "w_down": m_w_down,
             "final_norm_g": m_final_norm_g}
    second = {"w_ada": v_w_ada, "b_ada": v_b_ada, "norm_attn_g": v_norm_attn_g, "w_in": v_w_in, "rel_bias": v_rel_bias,
              "conv_w": v_conv_w, "a_log": v_a_log, "dt_bias": v_dt_bias, "delta_norm_g": v_delta_norm_g,
              "w_out": v_w_out, "norm_ffn_g": v_norm_ffn_g, "w_gate": v_w_gate, "w_up": v_w_up, "w_down": v_w_down,
              "final_norm_g": v_final_norm_g}
    delta, new_m, new_v = {}, {}, {}
    for name, w in weight.items():
        if name in big_names:
            continue
        two_d = (-1, w.shape[-1])
        d, nm, nv = _adamw(w.reshape(two_d), grad[name].reshape(two_d), first[name].reshape(two_d),
                           second[name].reshape(two_d), f"adamw_{name}")
        delta[name], new_m[name], new_v[name] = d.reshape(w.shape), nm.reshape(w.shape), nv.reshape(w.shape)
    for name, w, g, cols in zip(big_names, big, big_grads, by_cols):
        outs = _adamw(w, g, rows_form(first[name], cols), rows_form(second[name], cols), f"adamw_{name}")
        grad[name], delta[name], new_m[name], new_v[name] = [
            (jnp.swapaxes(o, 0, 1) if cols else o)[None] for o in (g, *outs)]

    names = list(weight)
    return (lax.psum(loss, ("x", "y", "c")), grad_x[None], *[grad[n] for n in names], *[delta[n] for n in names],
            *[new_m[n] for n in names], *[new_v[n] for n in names])
```

```python
import functools
import math

import numpy as np
import jax
import jax.numpy as jnp
from jax import lax
from jax.experimental import pallas as pl
from jax.experimental.pallas import tpu as pltpu

F32 = jnp.float32
BF16 = jnp.bfloat16
HIGHEST = lax.Precision.HIGHEST

D_MODEL = 1024
HEAD_DIM = 64
N_HEADS = 8
HEAD_W = 512
BRANCHES = ((128, 1), (512, 4), (2048, 16))
BAND = 128
ATT_TILE = 2048
ATT_UNROLL = 8
ATT_UNROLL_BWD = 4
N_BUCKETS = 32
MAX_DISTANCE = 2048
CHUNK = 64
D_FF = 2816
EPS = 1e-6
NEG_INF = -1e30
LANES = 128
VMEM_LIMIT = 56 * 1024 * 1024

ADAM_LR = 0.001
ADAM_B1 = 0.9
ADAM_B2 = 0.999
ADAM_EPS = 1e-08
ADAM_WD = 0.01
ADAM_STEP = 10


def _nn(a, b, precision=None):
    return jnp.dot(a, b, preferred_element_type=F32, precision=precision)


def _nt(a, b, precision=None):
    return lax.dot_general(a, b, (((1,), (1,)), ((), ())), preferred_element_type=F32, precision=precision)


def _tn(a, b, precision=None):
    return lax.dot_general(a, b, (((0,), (0,)), ((), ())), preferred_element_type=F32, precision=precision)


def _params(sem, vmem=VMEM_LIMIT):
    return pltpu.CompilerParams(dimension_semantics=sem, vmem_limit_bytes=vmem)


def _sigmoid(x):
    return 0.5 * jnp.tanh(0.5 * x) + 0.5


def _silu_and_slope(x):
    s = _sigmoid(x)
    return x * s, s * (1.0 + x * (1.0 - s))


def _silu(x):
    return x * _sigmoid(x)


def _attn_tables():
    qi = np.arange(BAND)[:, None]
    kj = np.arange(2 * BAND)[None, :]
    steps = qi + BAND - kj
    in_window = (steps >= 0) & (steps <= BAND)
    max_exact = N_BUCKETS // 2
    out = np.zeros((3, 2, BAND, 2 * BAND), np.int32)
    for b, (_, dil) in enumerate(BRANCHES):
        dist = np.maximum(steps, 0) * dil
        dist_f = np.maximum(dist, 1).astype(np.float32)
        large = max_exact + (np.log(dist_f / np.float32(max_exact)) / np.float32(math.log(MAX_DISTANCE / max_exact))
                             * np.float32(N_BUCKETS - max_exact)).astype(np.int32)
        bucket = np.where(dist < max_exact, dist, np.minimum(large, N_BUCKETS - 1)).astype(np.int32)
        out[b, 0] = np.where(in_window, bucket, -1)
        out[b, 1] = np.where(in_window & (kj >= BAND), bucket, -1)
    return out


def _attention_bias(rel_bias, tables):
    def body(rel_ref, tab_ref, out_ref):
        head = pl.program_id(0)
        for b in range(3):
            tab = tab_ref[b, 0]

            def pick(kk, acc, tab=tab):
                return jnp.where(tab == kk, rel_ref[kk, head], acc)

            acc = lax.fori_loop(0, N_BUCKETS, pick, jnp.zeros((BAND, 2 * BAND), F32))
            for first in range(2):
                out_ref[0, b, first] = jnp.where(tab_ref[b, first] < 0, NEG_INF, acc)

    return pl.pallas_call(
        body,
        grid=(N_HEADS,),
        in_specs=[pl.BlockSpec(memory_space=pltpu.SMEM),
                  pl.BlockSpec((3, 2, BAND, 2 * BAND), lambda h: (0, 0, 0, 0))],
        out_specs=pl.BlockSpec((1, 3, 2, BAND, 2 * BAND), lambda h: (h, 0, 0, 0, 0)),
        out_shape=jax.ShapeDtypeStruct((N_HEADS, 3, 2, BAND, 2 * BAND), F32),
        compiler_params=_params(("arbitrary",)),
        name="attn_bias",
    )(rel_bias, tables)


def _bias_spec():
    return pl.BlockSpec((2, 3, 2, BAND, 2 * BAND), lambda p, t: (p, 0, 0, 0, 0))


def _attn_block_index(idx, t, r):
    nb = ATT_TILE // (BAND * r)
    rho = idx // nb
    n = idx % nb
    qs = rho + r * BAND * n
    gs = t * ATT_TILE + qs
    first = (t * nb + n) == 0
    ps = jnp.where(first, gs, gs - r * BAND)
    return qs, gs, ps, first.astype(jnp.int32)


def _rows(start, r):
    return pl.ds(start, BAND) if r == 1 else pl.ds(start, BAND, stride=r)


def _attention_fwd(qkv, bias, shards):
    seq = qkv.shape[0]
    n_tiles = seq // ATT_TILE
    n = len(shards)

    def body(*refs):
        bias_ref, q_ref, k_ref, v_ref = refs[:4]
        y_ref, lse_ref = refs[4 + n:6 + n]
        o_s, l_s = refs[6 + 2 * n:8 + 2 * n]
        riding = (refs[4:4 + n], refs[6 + n:6 + 2 * n], *refs[8 + 2 * n:])
        pair = pl.program_id(0)
        t = pl.program_id(1)
        if n:
            @pl.when((pair == 0) & (t == 0))
            def _():
                for cp in _gather_copies(*riding, hand_over=False)[0]:
                    cp.start()

            @pl.when((pair == 2) & (t == 0))
            def _():
                for cp, fwd in zip(*_gather_copies(*riding)):
                    cp.wait_recv()
                    fwd.start()

        lane = lax.broadcasted_iota(jnp.int32, (1, LANES), 1)
        head0 = lane < HEAD_DIM
        masks = (head0, jnp.logical_not(head0))
        ones = jnp.ones((2 * BAND, LANES), BF16)
        for b, (_, r) in enumerate(BRANCHES):
            def blocks(it, carry, b=b, r=r):
                idx = [_attn_block_index(it * ATT_UNROLL + j, t, r) for j in range(ATT_UNROLL)]
                qb = [q_ref[_rows(qs, r), :] * (HEAD_DIM ** -0.5) for qs, _, _, _ in idx]
                kcat = [jnp.concatenate([k_ref[_rows(ps, r), :], k_ref[_rows(gs, r), :]], axis=0).astype(BF16)
                        for _, gs, ps, _ in idx]
                vcat = [jnp.concatenate([v_ref[_rows(ps, r), :], v_ref[_rows(gs, r), :]], axis=0).astype(BF16)
                        for _, gs, ps, _ in idx]
                work = [(j, hh) for j in range(ATT_UNROLL) for hh in range(2)]
                s = [_nt(jnp.where(masks[hh], qb[j], 0.0).astype(BF16), kcat[j]) + bias_ref[hh, b, idx[j][3]]
                     for j, hh in work]
                m = [jnp.max(sv, axis=-1, keepdims=True) for sv in s]
                e = [jnp.exp(sv - mv) for sv, mv in zip(s, m)]
                eb = [ev.astype(BF16) for ev in e]
                den = [_nn(ev, ones) for ev in eb]
                out = [_nn(ev, vcat[j]) / dv for ev, dv, (j, _) in zip(eb, den, work)]
                lse = [mv + jnp.log(dv) for mv, dv in zip(m, den)]
                for j in range(ATT_UNROLL):
                    o_s[b, _rows(idx[j][0], r), :] = jnp.where(head0, out[2 * j], out[2 * j + 1])
                    l_s[b, _rows(idx[j][0], r), :] = jnp.where(head0, lse[2 * j], lse[2 * j + 1])
                return carry

            lax.fori_loop(0, ATT_TILE // BAND // ATT_UNROLL, blocks, 0)

        def merge(i, carry):
            rows = pl.ds(pl.multiple_of(i * BAND, BAND), BAND)
            l0, l1, l2 = l_s[0, rows, :], l_s[1, rows, :], l_s[2, rows, :]
            m = jnp.maximum(jnp.maximum(l0, l1), l2)
            w0, w1, w2 = jnp.exp(l0 - m), jnp.exp(l1 - m), jnp.exp(l2 - m)
            tot = w0 + w1 + w2
            y_ref[rows, :] = (w0 * o_s[0, rows, :] + w1 * o_s[1, rows, :] + w2 * o_s[2, rows, :]) / tot
            lse_ref[rows, :] = m + jnp.log(tot)
            return carry

        lax.fori_loop(0, ATT_TILE // BAND, merge, 0)

        if n:
            @pl.when((pair == N_HEADS // 2 - 1) & (t == n_tiles - 1))
            def _():
                first, passed = _gather_copies(*riding)
                for cp in first:
                    cp.wait_send()
                for fwd in passed:
                    fwd.wait()

    tile = pl.BlockSpec((ATT_TILE, LANES), lambda p, t: (t, p))
    sems = [pltpu.SemaphoreType.DMA((6 * n,)), pltpu.SemaphoreType.DMA((6 * n,))] if n else []
    return pl.pallas_call(
        body,
        grid=(N_HEADS // 2, n_tiles),
        in_specs=[
            _bias_spec(),
            pl.BlockSpec((ATT_TILE, LANES), lambda p, t: (t, p)),
            pl.BlockSpec((seq, LANES), lambda p, t: (0, 4 + p)),
            pl.BlockSpec((seq, LANES), lambda p, t: (0, 8 + p)),
        ] + [ANY] * n,
        out_specs=[tile, tile] + [ANY] * n,
        out_shape=[jax.ShapeDtypeStruct((seq, HEAD_W), F32), jax.ShapeDtypeStruct((seq, HEAD_W), F32)]
        + _gathered_shapes(shards),
        scratch_shapes=[
            pltpu.VMEM((3, ATT_TILE, LANES), F32),
            pltpu.VMEM((3, ATT_TILE, LANES), F32),
        ] + sems,
        compiler_params=_params(("arbitrary", "arbitrary")),
        name="attn_fwd",
    )(bias, qkv, qkv, qkv, *shards)


def _attention_bwd(qkv, dy, y, lse, bias, partials):
    seq = qkv.shape[0]
    n_tiles = seq // ATT_TILE
    n = len(partials)

    def body(*refs):
        bias_ref, q_ref, k_ref, v_ref, dy_ref, y_ref, lse_ref = refs[:7]
        dq_ref, dk_ref, dv_ref, dbias_ref = refs[7 + n:11 + n]
        riding = (refs[7:7 + n], refs[11 + n:11 + 2 * n], *refs[11 + 2 * n:])
        pair = pl.program_id(0)
        t = pl.program_id(1)
        if n:
            @pl.when((pair == 0) & (t == 0))
            def _():
                for cp in _scatter_copies(*riding):
                    cp.start()

        lane = lax.broadcasted_iota(jnp.int32, (1, LANES), 1)
        head0 = lane < HEAD_DIM

        @pl.when(t == 0)
        def _():
            dk_ref[...] = jnp.zeros_like(dk_ref)
            dv_ref[...] = jnp.zeros_like(dv_ref)
            dbias_ref[...] = jnp.zeros_like(dbias_ref)

        dq_ref[...] = jnp.zeros_like(dq_ref)

        masks = (head0, jnp.logical_not(head0))
        ones = jnp.ones((LANES, LANES), BF16)
        scale = HEAD_DIM ** -0.5
        for b, (_, r) in enumerate(BRANCHES):
            def blocks(it, carry, b=b, r=r):
                idx = [_attn_block_index(it * ATT_UNROLL_BWD + j, t, r) for j in range(ATT_UNROLL_BWD)]
                qb = [q_ref[_rows(qs, r), :] * scale for qs, _, _, _ in idx]
                kcat = [jnp.concatenate([k_ref[_rows(ps, r), :], k_ref[_rows(gs, r), :]], axis=0).astype(BF16)
                        for _, gs, ps, _ in idx]
                vcat = [jnp.concatenate([v_ref[_rows(ps, r), :], v_ref[_rows(gs, r), :]], axis=0).astype(BF16)
                        for _, gs, ps, _ in idx]
                dob = [dy_ref[_rows(qs, r), :] for qs, _, _, _ in idx]
                ob = [y_ref[_rows(qs, r), :] for qs, _, _, _ in idx]
                lb = [lse_ref[_rows(qs, r), :] for qs, _, _, _ in idx]
                work = [(j, hh) for j in range(ATT_UNROLL_BWD) for hh in range(2)]
                qh = [jnp.where(masks[hh], qb[j], 0.0).astype(BF16) for j, hh in work]
                doh = [jnp.where(masks[hh], dob[j], 0.0) for j, hh in work]
                dohb = [d.astype(BF16) for d in doh]
                s = [_nt(qh[w], kcat[j]) + bias_ref[hh, b, idx[j][3]] for w, (j, hh) in enumerate(work)]
                dp = [_nt(dohb[w], vcat[j]) for w, (j, _) in enumerate(work)]
                lrot = [pltpu.roll(lv, HEAD_DIM, 1) for lv in lb]
                lcol = [jnp.where(masks[hh], lb[j], lrot[j]) for j, hh in work]
                parts = [_split(doh[w] * ob[j]) for w, (j, _) in enumerate(work)]
                delta = [_nn(hi, ones) + _nn(lo, ones) for hi, lo in parts]
                prob = [jnp.exp(sv - jnp.concatenate([lv, lv], axis=1)) for sv, lv in zip(s, lcol)]
                ds = [pv * (dv - jnp.concatenate([de, de], axis=1)) for pv, dv, de in zip(prob, dp, delta)]
                dsb = [d.astype(BF16) for d in ds]
                dq = [_nn(dsb[w], kcat[j]) for w, (j, _) in enumerate(work)]
                dkc = [_tn(dsb[w], qh[w]) for w in range(len(work))]
                dvc = [_tn(prob[w].astype(BF16), dohb[w]) for w in range(len(work))]
                for hh in range(2):
                    dbias_ref[0, b, hh] += sum(ds[w] for w, (_, head) in enumerate(work) if head == hh)
                for j in range(ATT_UNROLL_BWD):
                    qs, gs, ps, _ = idx[j]
                    dkcat = dkc[2 * j] + dkc[2 * j + 1]
                    dvcat = dvc[2 * j] + dvc[2 * j + 1]
                    dq_ref[_rows(qs, r), :] += jnp.where(head0, dq[2 * j], dq[2 * j + 1]) * scale
                    dk_ref[_rows(ps, r), :] += dkcat[:BAND]
                    dk_ref[_rows(gs, r), :] += dkcat[BAND:]
                    dv_ref[_rows(ps, r), :] += dvcat[:BAND]
                    dv_ref[_rows(gs, r), :] += dvcat[BAND:]
                return carry

            lax.fori_loop(0, ATT_TILE // BAND // ATT_UNROLL_BWD, blocks, 0)

        if n:
            @pl.when((pair == N_HEADS // 2 - 1) & (t == n_tiles - 1))
            def _():
                for cp in _scatter_copies(*riding):
                    cp.wait()

    tile = pl.BlockSpec((ATT_TILE, LANES), lambda p, t: (t, p))
    full = pl.BlockSpec((seq, LANES), lambda p, t: (0, p))
    sems = [pltpu.SemaphoreType.DMA((3 * n,)), pltpu.SemaphoreType.DMA((3 * n,))] if n else []
    return pl.pallas_call(
        body,
        grid=(N_HEADS // 2, n_tiles),
        in_specs=[
            _bias_spec(),
            pl.BlockSpec((ATT_TILE, LANES), lambda p, t: (t, p)),
            pl.BlockSpec((seq, LANES), lambda p, t: (0, 4 + p)),
            pl.BlockSpec((seq, LANES), lambda p, t: (0, 8 + p)),
            tile, tile, tile,
        ] + [ANY] * n,
        out_specs=[tile, full, full,
                   pl.BlockSpec((1, 3, 2, BAND, 2 * BAND), lambda p, t: (p, 0, 0, 0, 0))] + [ANY] * n,
        out_shape=[jax.ShapeDtypeStruct((seq, HEAD_W), F32)] * 3
        + [jax.ShapeDtypeStruct((N_HEADS // 2, 3, 2, BAND, 2 * BAND), F32)]
        + [jax.ShapeDtypeStruct(p.shape, p.dtype) for p in partials],
        scratch_shapes=sems,
        compiler_params=_params(("arbitrary", "arbitrary")),
        name="attn_bwd",
    )(bias, qkv, qkv, qkv, dy, y, lse, *partials)


def _rel_bias_grad(dbias, tables):
    def body(tab_ref, db_ref, out_ref):
        lane = lax.broadcasted_iota(jnp.int32, (1, LANES), 1)
        out_ref[...] = jnp.zeros_like(out_ref)
        for b in range(3):
            tab = tab_ref[b, 0]

            def head(h, carry, b=b, tab=tab):
                d = db_ref[h // 2, b, h % 2]
                sums = [jnp.sum(jnp.where(tab == kk, d, 0.0), keepdims=True) for kk in range(N_BUCKETS)]
                row = jnp.zeros((1, LANES), F32)
                for kk, s in enumerate(sums):
                    row = row + jnp.where(lane == kk, s, 0.0)
                out_ref[pl.ds(h, 1), :] += row
                return carry

            lax.fori_loop(0, N_HEADS, head, 0)

    return pl.pallas_call(
        body,
        out_shape=jax.ShapeDtypeStruct((N_HEADS, LANES), F32),
        compiler_params=pltpu.CompilerParams(vmem_limit_bytes=VMEM_LIMIT),
        name="rel_bias_grad",
    )(tables, dbias)


ROW_TILE = 512


def _head_sum_matrix():
    return (lax.broadcasted_iota(jnp.int32, (HEAD_W, HEAD_W), 0) // HEAD_DIM
            == lax.broadcasted_iota(jnp.int32, (HEAD_W, HEAD_W), 1) // HEAD_DIM).astype(F32)


def _head_spread_matrix(offset=0):
    return (lax.broadcasted_iota(jnp.int32, (LANES, HEAD_W), 0)
            == lax.broadcasted_iota(jnp.int32, (LANES, HEAD_W), 1) // HEAD_DIM + offset).astype(F32)


def _head_gather_matrix(offset=0):
    return (lax.broadcasted_iota(jnp.int32, (HEAD_W, LANES), 0) // HEAD_DIM + offset
            == lax.broadcasted_iota(jnp.int32, (HEAD_W, LANES), 1)).astype(F32)


def _split3(x):
    hi = x.astype(BF16)
    rest = x - hi.astype(F32)
    mid = rest.astype(BF16)
    return hi, mid, (rest - mid.astype(F32)).astype(BF16)


def _pick(x, onehot):
    m = onehot.astype(BF16)
    hi, mid, lo = _split3(x)
    return _nn(hi, m) + (_nn(mid, m) + _nn(lo, m))


def _pick_left(onehot, x):
    m = onehot.astype(BF16)
    hi, mid, lo = _split3(x)
    return _nn(m, hi) + (_nn(m, mid) + _nn(m, lo))


def _tri(lower, strict=False):
    r = lax.broadcasted_iota(jnp.int32, (CHUNK, CHUNK), 0)
    c = lax.broadcasted_iota(jnp.int32, (CHUNK, CHUNK), 1)
    if lower:
        return (c < r) if strict else (c <= r)
    return c >= r


def _softplus(z):
    return jnp.maximum(z, 0.0) + jnp.log(1.0 + jnp.exp(-jnp.abs(z)))


def _conv_taps(stage, w_ref, rows):
    return (w_ref[3:4, :] * stage[8:8 + rows, :] + w_ref[2:3, :] * stage[7:7 + rows, :]
            + w_ref[1:2, :] * stage[6:6 + rows, :] + w_ref[0:1, :] * stage[5:5 + rows, :])


def _l2_scale(xc, hsum):
    return lax.rsqrt(_pick(xc * xc, hsum) + EPS)


def _stage_rows(stage, x_ref, xp_ref, i):
    stage[0:8, :] = jnp.where(i == 0, 0.0, xp_ref[...])
    stage[8:8 + ROW_TILE, :] = x_ref[...]


def _delta_prep_fwd(qkvz, ba, conv_w, alog_row, dt_row):
    seq = qkvz.shape[0]
    qkv_w = 3 * HEAD_W

    def body(x_ref, xp_ref, ba_ref, w_ref, al_ref, dt_ref, out_ref, stage):
        i = pl.program_id(0)
        _stage_rows(stage, x_ref, xp_ref, i)
        act = _silu(_conv_taps(stage, w_ref, ROW_TILE))
        hsum, hspread = _head_sum_matrix(), _head_spread_matrix()
        qc, kc = act[:, :HEAD_W], act[:, HEAD_W:2 * HEAD_W]
        out_ref[0] = qc * _l2_scale(qc, hsum) * (HEAD_DIM ** -0.5)
        out_ref[1] = kc * _l2_scale(kc, hsum)
        out_ref[2] = act[:, 2 * HEAD_W:]
        bav = ba_ref[...]
        out_ref[3] = _pick(_sigmoid(bav), hspread)
        g8 = -jnp.exp(al_ref[...]) * _softplus(bav + dt_ref[...])
        gb = _pick(g8, _head_spread_matrix(N_HEADS))
        cum = _tri(True).astype(F32)
        for ch in range(ROW_TILE // CHUNK):
            rows = slice(ch * CHUNK, (ch + 1) * CHUNK)
            out_ref[4, rows, :] = _pick_left(cum, gb[rows])

    return pl.pallas_call(
        body,
        grid=(seq // ROW_TILE,),
        in_specs=[
            pl.BlockSpec((ROW_TILE, qkv_w), lambda i: (i, 0)),
            pl.BlockSpec((8, qkv_w), lambda i: (jnp.maximum(i * (ROW_TILE // 8) - 1, 0), 0)),
            pl.BlockSpec((ROW_TILE, LANES), lambda i: (i, 0)),
            pl.BlockSpec((4, qkv_w), lambda i: (0, 0)),
            pl.BlockSpec((1, LANES), lambda i: (0, 0)),
            pl.BlockSpec((1, LANES), lambda i: (0, 0)),
        ],
        out_specs=pl.BlockSpec((5, ROW_TILE, HEAD_W), lambda i: (0, i, 0)),
        out_shape=jax.ShapeDtypeStruct((5, seq, HEAD_W), F32),
        scratch_shapes=[pltpu.VMEM((ROW_TILE + 8, qkv_w), F32)],
        compiler_params=_params(("arbitrary",)),
        name="delta_prep_fwd",
    )(qkvz, qkvz, ba, conv_w, alog_row, dt_row)


def _split(x):
    hi = x.astype(BF16)
    return hi, (x - hi.astype(F32)).astype(BF16)


def _dot3(a, b, dot=_nn):
    return dot(a[0], b[0]) + (dot(a[0], b[1]) + dot(a[1], b[0]))


def _unit_lower_inverses(mats):
    eye = (lax.broadcasted_iota(jnp.int32, (CHUNK, CHUNK), 0)
           == lax.broadcasted_iota(jnp.int32, (CHUNK, CHUNK), 1)).astype(F32)
    invs = [eye - a for a in mats]
    powers = [_split(a) for a in mats]
    for step in range(5):
        squares = [_dot3(p, p) for p in powers]
        powers = [_split(s) for s in squares]
        invs = [inv + _dot3(_split(inv), p) for inv, p in zip(invs, powers)]
    return invs


def _chunk_terms(q, k, v, beta, gc):
    causal, strict = _tri(True), _tri(True, strict=True)
    e = jnp.exp(gc)
    g_last = jnp.broadcast_to(gc[CHUNK - 1:CHUNK, :], (CHUNK, CHUNK))
    f = jnp.exp(g_last - gc)
    e_last = jnp.exp(g_last)
    decay = jnp.where(causal, jnp.exp(jnp.where(causal, gc - gc.T, 0.0)), 0.0)
    kb = k * beta
    a_mat = jnp.where(strict, _nt(kb.astype(BF16), k.astype(BF16)) * decay, 0.0)
    qk = jnp.where(causal, _nt(q.astype(BF16), k.astype(BF16)) * decay, 0.0)
    return e, f, e_last, decay, kb, a_mat, qk


GROUP = 8
UNROLL = 8


def _chunk_rows(ci):
    return pl.ds(pl.multiple_of(ci * CHUNK, CHUNK), CHUNK)


def _pair_specs(n_planes):
    return pl.BlockSpec((n_planes, GROUP * CHUNK, LANES), lambda p, g: (0, g, p))


def _delta_chunk_fwd(xs):
    seq = xs.shape[1]
    rows_per_step = GROUP * CHUNK

    def body(x_ref, inv_ref, qk_ref, u_ref, w_ref):
        work = [(hh, slice(step * CHUNK, (step + 1) * CHUNK)) for hh in range(2) for step in range(GROUP)]
        xh = [[x_ref[j, r, hh * HEAD_DIM:(hh + 1) * HEAD_DIM] for j in range(5)] for hh, r in work]
        terms = [_chunk_terms(*x) for x in xh]
        invs = _unit_lower_inverses([t[5] for t in terms])
        for (hh, r), x, t, inv in zip(work, xh, terms, invs):
            e, kb, qk = t[0], t[4], t[6]
            inv_parts = _split(inv)
            inv_ref[hh, r, :] = inv
            qk_ref[hh, r, :] = qk
            u_ref[hh, r, :] = _dot3(inv_parts, _split(x[2] * x[3]))
            w_ref[hh, r, :] = _dot3(inv_parts, _split(kb * e))

    out = pl.BlockSpec((2, rows_per_step, HEAD_DIM), lambda p, g: (p, g, 0))
    return pl.pallas_call(
        body,
        grid=(N_HEADS // 2, seq // rows_per_step),
        in_specs=[_pair_specs(5)],
        out_specs=[out] * 4,
        out_shape=[jax.ShapeDtypeStruct((N_HEADS, seq, HEAD_DIM), F32)] * 4,
        compiler_params=_params(("parallel", "parallel")),
        name="delta_chunk_fwd",
    )(xs)


def _decays(gc):
    g_last = jnp.broadcast_to(gc[CHUNK - 1:CHUNK, :], (CHUNK, CHUNK))
    return jnp.exp(gc), jnp.exp(g_last - gc), jnp.exp(g_last)


def _token_blocks(index, n_steps=None):
    rows_per_step = GROUP * CHUNK
    if n_steps is None:
        return pl.BlockSpec((1, rows_per_step, HEAD_W), lambda g: (index, g, 0))
    return pl.BlockSpec((1, rows_per_step, HEAD_W), lambda g: (index, n_steps - 1 - g, 0))


def _head_lanes(h):
    return pl.ds(h * HEAD_DIM, HEAD_DIM)


def _delta_scan_fwd(xs, qk_h, u_h, w_h):
    seq = xs.shape[1]
    rows_per_step = GROUP * CHUNK

    def body(q_ref, k_ref, gc_ref, qk_ref, u_ref, w_ref, o_ref, st_ref, state):
        @pl.when(pl.program_id(0) == 0)
        def _():
            state[...] = jnp.zeros_like(state)

        def chunk(ci, carry):
            rows = _chunk_rows(ci)
            heads = range(N_HEADS)
            dec = [_decays(gc_ref[0, rows, _head_lanes(h)]) for h in heads]
            s = [state[h] for h in heads]
            sb = [s[h].astype(BF16) for h in heads]
            vnb = [(u_ref[h, rows, :] - _nn(w_ref[h, rows, :].astype(BF16), sb[h])).astype(BF16) for h in heads]
            for h in heads:
                o_ref[rows, _head_lanes(h)] = (_nn((q_ref[0, rows, _head_lanes(h)] * dec[h][0]).astype(BF16), sb[h])
                                               + _nn(qk_ref[h, rows, :].astype(BF16), vnb[h]))
                st_ref[h, rows, :] = s[h]
            for h in heads:
                state[h] = s[h] * dec[h][2] + _tn((k_ref[0, rows, _head_lanes(h)] * dec[h][1]).astype(BF16), vnb[h])
            return carry

        lax.fori_loop(0, GROUP, chunk, 0)

    blk = pl.BlockSpec((N_HEADS, rows_per_step, HEAD_DIM), lambda g: (0, g, 0))
    return pl.pallas_call(
        body,
        grid=(seq // rows_per_step,),
        in_specs=[_token_blocks(0), _token_blocks(1), _token_blocks(4), blk, blk, blk],
        out_specs=[pl.BlockSpec((rows_per_step, HEAD_W), lambda g: (g, 0)), blk],
        out_shape=[jax.ShapeDtypeStruct((seq, HEAD_W), F32), jax.ShapeDtypeStruct((N_HEADS, seq, HEAD_DIM), F32)],
        scratch_shapes=[pltpu.VMEM((N_HEADS, CHUNK, CHUNK), F32)],
        compiler_params=_params(("arbitrary",)),
        name="delta_scan_fwd",
    )(xs, xs, xs, qk_h, u_h, w_h)


def _delta_scan_bwd(xs, qk_h, w_h, do):
    seq = xs.shape[1]
    rows_per_step = GROUP * CHUNK
    n_steps = seq // rows_per_step

    def body(q_ref, k_ref, gc_ref, qk_ref, w_ref, do_ref, dsn_ref, dvn_ref, dstate):
        @pl.when(pl.program_id(0) == 0)
        def _():
            dstate[...] = jnp.zeros_like(dstate)

        def chunk(step, carry):
            rows = _chunk_rows(GROUP - 1 - step)
            heads = range(N_HEADS)
            dec = [_decays(gc_ref[0, rows, _head_lanes(h)]) for h in heads]
            ds_next = [dstate[h] for h in heads]
            dob = [do_ref[rows, _head_lanes(h)].astype(BF16) for h in heads]
            dv_new = [_tn(qk_ref[h, rows, :].astype(BF16), dob[h])
                      + _nn((k_ref[0, rows, _head_lanes(h)] * dec[h][1]).astype(BF16), ds_next[h].astype(BF16))
                      for h in heads]
            for h in heads:
                dsn_ref[h, rows, :] = ds_next[h]
                dvn_ref[h, rows, :] = dv_new[h]
            for h in heads:
                dstate[h] = (_tn((q_ref[0, rows, _head_lanes(h)] * dec[h][0]).astype(BF16), dob[h])
                             + dec[h][2] * ds_next[h] - _tn(w_ref[h, rows, :].astype(BF16), dv_new[h].astype(BF16)))
            return carry

        lax.fori_loop(0, GROUP, chunk, 0)

    blk = pl.BlockSpec((N_HEADS, rows_per_step, HEAD_DIM), lambda g: (0, n_steps - 1 - g, 0))
    return pl.pallas_call(
        body,
        grid=(n_steps,),
        in_specs=[_token_blocks(0, n_steps), _token_blocks(1, n_steps), _token_blocks(4, n_steps), blk, blk,
                  pl.BlockSpec((rows_per_step, HEAD_W), lambda g: (n_steps - 1 - g, 0))],
        out_specs=[blk, blk],
        out_shape=[jax.ShapeDtypeStruct((N_HEADS, seq, HEAD_DIM), F32)] * 2,
        scratch_shapes=[pltpu.VMEM((N_HEADS, CHUNK, CHUNK), F32)],
        compiler_params=_params(("arbitrary",)),
        name="delta_scan_bwd",
    )(xs, xs, xs, qk_h, w_h, do)


def _delta_chunk_bwd(xs, inv_h, u_h, w_h, st_h, dsn_h, dvn_h, do):
    seq = xs.shape[1]
    rows_per_step = GROUP * CHUNK

    def body(x_ref, inv_ref, u_ref, w_ref, st_ref, dsn_ref, dvn_ref, do_ref, dx_ref):
        causal, strict = _tri(True), _tri(True, strict=True)
        last_row = lax.broadcasted_iota(jnp.int32, (CHUNK, CHUNK), 0) == CHUNK - 1

        def bf(vals):
            return [val.astype(BF16) for val in vals]

        def group(items):
            heads = [hh for hh, _ in items]
            lanes = [slice(hh * HEAD_DIM, (hh + 1) * HEAD_DIM) for hh in heads]
            rows = [slice(step * CHUNK, (step + 1) * CHUNK) for _, step in items]
            n = range(len(items))
            q, k, v, beta, gc = [[x_ref[j, rows[i], lanes[i]] for i in n] for j in range(5)]
            terms = [_chunk_terms(q[i], k[i], v[i], beta[i], gc[i]) for i in n]
            e, f, e_last, decay, kb, a_mat, qk = [[t[j] for t in terms] for j in range(7)]
            inv = [_split(inv_ref[heads[i], rows[i], :]) for i in n]
            u = [u_ref[heads[i], rows[i], :] for i in n]
            w = [w_ref[heads[i], rows[i], :] for i in n]
            s = [st_ref[heads[i], rows[i], :] for i in n]
            ds_next = [dsn_ref[heads[i], rows[i], :] for i in n]
            dv_new = [dvn_ref[heads[i], rows[i], :] for i in n]
            sb, dsb, dvb, wb = bf(s), bf(ds_next), bf(dv_new), bf(w)
            dob = bf([do_ref[rows[i], lanes[i]] for i in n])
            qbf, kbf, kbb = bf(q), bf(k), bf(kb)
            vnb = bf([u[i] - _nn(wb[i], sb[i]) for i in n])
            dqe = [_nt(dob[i], sb[i]) for i in n]
            dw = [-_nt(dvb[i], sb[i]) for i in n]
            dkf = [_nt(vnb[i], dsb[i]) for i in n]
            dqk = [jnp.where(causal, _nt(dob[i], vnb[i]), 0.0) for i in n]
            drhs_u = [_dot3(inv[i], _split(dv_new[i]), _tn) for i in n]
            drhs_w = [_dot3(inv[i], _split(dw[i]), _tn) for i in n]
            da = [-jnp.where(strict, _nt(drhs_u[i].astype(BF16), u[i].astype(BF16))
                             + _nt(drhs_w[i].astype(BF16), wb[i]), 0.0) for i in n]
            dad = bf([da[i] * decay[i] for i in n])
            dqd = bf([dqk[i] * decay[i] for i in n])
            dkb = [e[i] * drhs_w[i] + _nn(dad[i], kbf[i]) for i in n]
            dk = [_tn(dad[i], kbb[i]) + _tn(dqd[i], qbf[i]) + f[i] * dkf[i] + beta[i] * dkb[i] for i in n]
            dq = [_nn(dqd[i], kbf[i]) + e[i] * dqe[i] for i in n]
            for i in n:
                de_full = kb[i] * drhs_w[i] + q[i] * dqe[i]
                df_full = k[i] * dkf[i]
                m = da[i] * a_mat[i] + dqk[i] * qk[i]
                dgc = de_full * e[i] - df_full * f[i] + m - m.T
                tail = jnp.sum(df_full * f[i] + s[i] * ds_next[i] * e_last[i], axis=0, keepdims=True)
                dgc = dgc + jnp.where(last_row, jnp.broadcast_to(tail, (CHUNK, CHUNK)), 0.0)
                dx_ref[0, rows[i], lanes[i]] = dq[i]
                dx_ref[1, rows[i], lanes[i]] = dk[i]
                dx_ref[2, rows[i], lanes[i]] = beta[i] * drhs_u[i]
                dx_ref[3, rows[i], lanes[i]] = v[i] * drhs_u[i] + k[i] * dkb[i]
                dx_ref[4, rows[i], lanes[i]] = dgc

        work = [(hh, step) for hh in range(2) for step in range(GROUP)]
        for first in range(0, len(work), UNROLL):
            group(work[first:first + UNROLL])

    blk = pl.BlockSpec((2, rows_per_step, HEAD_DIM), lambda p, g: (p, g, 0))
    return pl.pallas_call(
        body,
        grid=(N_HEADS // 2, seq // rows_per_step),
        in_specs=[_pair_specs(5)] + [blk] * 6 + [pl.BlockSpec((rows_per_step, LANES), lambda p, g: (g, p))],
        out_specs=_pair_specs(5),
        out_shape=jax.ShapeDtypeStruct((5, seq, HEAD_W), F32),
        compiler_params=_params(("parallel", "parallel")),
        name="delta_chunk_bwd",
    )(xs, inv_h, u_h, w_h, st_h, dsn_h, dvn_h, do)


def _delta_post_fwd(o, qkvz, gain_row):
    seq = o.shape[0]

    def body(o_ref, z_ref, g_ref, y_ref):
        ov = o_ref[...]
        rb = lax.rsqrt(_pick(ov * ov, _head_sum_matrix()) * (1.0 / HEAD_DIM) + EPS)
        y_ref[...] = (ov * rb * g_ref[...] * _silu(z_ref[...])).astype(y_ref.dtype)

    tile = pl.BlockSpec((ROW_TILE, HEAD_W), lambda i: (i, 0))
    return pl.pallas_call(
        body,
        grid=(seq // ROW_TILE,),
        in_specs=[tile, pl.BlockSpec((ROW_TILE, HEAD_W), lambda i: (i, 3)), pl.BlockSpec((1, HEAD_W), lambda i: (0, 0))],
        out_specs=tile,
        out_shape=jax.ShapeDtypeStruct((seq, HEAD_W), BF16),
        compiler_params=_params(("arbitrary",)),
        name="delta_post_fwd",
    )(o, qkvz, gain_row)


def _delta_post_bwd(dy, o, qkvz, gain_row):
    seq = o.shape[0]

    def body(dy_ref, o_ref, z_ref, g_ref, do_ref, dz_ref, dg_ref):
        @pl.when(pl.program_id(0) == 0)
        def _():
            dg_ref[...] = jnp.zeros_like(dg_ref)

        ov, zv, dyv, gain = o_ref[...], z_ref[...], dy_ref[...], g_ref[...]
        hsum = _head_sum_matrix()
        rb = lax.rsqrt(_pick(ov * ov, hsum) * (1.0 / HEAD_DIM) + EPS)
        ohat = ov * rb
        silu_z, slope_z = _silu_and_slope(zv)
        dz_ref[...] = dyv * ohat * gain * slope_z
        dn = dyv * silu_z
        dg_ref[0:1, :] += jnp.sum(dn * ohat, axis=0, keepdims=True)
        dohat = dn * gain

        @pl.when(pl.program_id(0) == pl.num_programs(0) - 1)
        def _():
            fold = (lax.broadcasted_iota(jnp.int32, (HEAD_W, HEAD_W), 0) % HEAD_DIM
                    == lax.broadcasted_iota(jnp.int32, (HEAD_W, HEAD_W), 1)).astype(F32)
            dg_ref[1:2, :] = _pick(dg_ref[0:1, :], fold)

        proj = _pick(dohat * ohat, hsum) * (1.0 / HEAD_DIM)
        do_ref[...] = rb * (dohat - ohat * proj)

    tile = pl.BlockSpec((ROW_TILE, HEAD_W), lambda i: (i, 0))
    return pl.pallas_call(
        body,
        grid=(seq // ROW_TILE,),
        in_specs=[pl.BlockSpec((ROW_TILE, HEAD_W), lambda i: (i, 1)), tile,
                  pl.BlockSpec((ROW_TILE, HEAD_W), lambda i: (i, 3)), pl.BlockSpec((1, HEAD_W), lambda i: (0, 0))],
        out_specs=[tile, tile, pl.BlockSpec((2, HEAD_W), lambda i: (0, 0))],
        out_shape=[jax.ShapeDtypeStruct((seq, HEAD_W), F32), jax.ShapeDtypeStruct((seq, HEAD_W), F32),
                   jax.ShapeDtypeStruct((2, HEAD_W), F32)],
        compiler_params=_params(("arbitrary",)),
        name="delta_post_bwd",
    )(dy, o, qkvz, gain_row)


def _delta_prep_bwd(qkvz, ba, conv_w, alog_row, dt_row, dxs):
    seq = qkvz.shape[0]
    qkv_w = 3 * HEAD_W

    def body(x_ref, xp_ref, ba_ref, w_ref, al_ref, dt_ref, dx_ref, dconv_ref, dba_ref, dvec_ref, stage):
        i = pl.program_id(0)

        @pl.when(i == 0)
        def _():
            dvec_ref[...] = jnp.zeros_like(dvec_ref)

        _stage_rows(stage, x_ref, xp_ref, i)
        pre = _conv_taps(stage, w_ref, ROW_TILE)
        act, slope = _silu_and_slope(pre)
        hsum = _head_sum_matrix()
        for j, scale in ((0, HEAD_DIM ** -0.5), (1, 1.0)):
            cols = slice(j * HEAD_W, (j + 1) * HEAD_W)
            xc = act[:, cols]
            rb = _l2_scale(xc, hsum)
            xhat = xc * rb
            dhat = dx_ref[j] * scale
            proj = _pick(dhat * xhat, hsum)
            dconv_ref[:, cols] = rb * (dhat - xhat * proj) * slope[:, cols]
        dconv_ref[:, 2 * HEAD_W:] = dx_ref[2] * slope[:, 2 * HEAD_W:]

        bav = ba_ref[...]
        beta8 = _sigmoid(bav)
        dbeta8 = _pick(dx_ref[3], _head_gather_matrix())
        dgc8 = _pick(dx_ref[4], _head_gather_matrix(N_HEADS))
        rev = _tri(False).astype(F32)
        z = bav + dt_ref[...]
        ea = jnp.exp(al_ref[...])
        g8 = -ea * _softplus(z)
        sig = _sigmoid(z)
        d_alog = jnp.zeros((1, LANES), F32)
        d_dt = jnp.zeros((1, LANES), F32)
        for ch in range(ROW_TILE // CHUNK):
            rows = slice(ch * CHUNK, (ch + 1) * CHUNK)
            dg8 = _pick_left(rev, dgc8[rows])
            da = -dg8 * ea * sig[rows]
            dba_ref[rows, :] = dbeta8[rows] * beta8[rows] * (1.0 - beta8[rows]) + da
            d_alog = d_alog + jnp.sum(dg8 * g8[rows], axis=0, keepdims=True)
            d_dt = d_dt + jnp.sum(da, axis=0, keepdims=True)
        dvec_ref[0:1, :] += d_alog
        dvec_ref[1:2, :] += d_dt

    return pl.pallas_call(
        body,
        grid=(seq // ROW_TILE,),
        in_specs=[
            pl.BlockSpec((ROW_TILE, qkv_w), lambda i: (i, 0)),
            pl.BlockSpec((8, qkv_w), lambda i: (jnp.maximum(i * (ROW_TILE // 8) - 1, 0), 0)),
            pl.BlockSpec((ROW_TILE, LANES), lambda i: (i, 0)),
            pl.BlockSpec((4, qkv_w), lambda i: (0, 0)),
            pl.BlockSpec((1, LANES), lambda i: (0, 0)),
            pl.BlockSpec((1, LANES), lambda i: (0, 0)),
            pl.BlockSpec((5, ROW_TILE, HEAD_W), lambda i: (0, i, 0)),
        ],
        out_specs=[pl.BlockSpec((ROW_TILE, qkv_w), lambda i: (i, 0)),
                   pl.BlockSpec((ROW_TILE, LANES), lambda i: (i, 0)),
                   pl.BlockSpec((2, LANES), lambda i: (0, 0))],
        out_shape=[jax.ShapeDtypeStruct((seq, qkv_w), F32), jax.ShapeDtypeStruct((seq, LANES), F32),
                   jax.ShapeDtypeStruct((2, LANES), F32)],
        scratch_shapes=[pltpu.VMEM((ROW_TILE + 8, qkv_w), F32)],
        compiler_params=_params(("arbitrary",)),
        name="delta_prep_bwd",
    )(qkvz, qkvz, ba, conv_w, alog_row, dt_row, dxs)


def _conv_bwd(dconv, qkvz, conv_w):
    seq = dconv.shape[0]
    qkv_w = 3 * HEAD_W
    n_tiles = seq // ROW_TILE

    def body(dy_ref, dyn_ref, x_ref, xp_ref, w_ref, dx_ref, dw_ref, stage, dstage):
        i = pl.program_id(0)

        @pl.when(i == 0)
        def _():
            dw_ref[...] = jnp.zeros_like(dw_ref)

        _stage_rows(stage, x_ref, xp_ref, i)
        dstage[0:ROW_TILE, :] = dy_ref[...]
        dstage[ROW_TILE:ROW_TILE + 8, :] = jnp.where(i == n_tiles - 1, 0.0, dyn_ref[...])
        dy = dy_ref[...]
        dx_ref[...] = (w_ref[3:4, :] * dy + w_ref[2:3, :] * dstage[1:1 + ROW_TILE, :]
                       + w_ref[1:2, :] * dstage[2:2 + ROW_TILE, :] + w_ref[0:1, :] * dstage[3:3 + ROW_TILE, :])
        for j in range(4):
            dw_ref[j:j + 1, :] += jnp.sum(dy * stage[5 + j:5 + j + ROW_TILE, :], axis=0, keepdims=True)

    tile = pl.BlockSpec((ROW_TILE, qkv_w), lambda i: (i, 0))
    return pl.pallas_call(
        body,
        grid=(n_tiles,),
        in_specs=[
            tile,
            pl.BlockSpec((8, qkv_w), lambda i: (jnp.minimum((i + 1) * (ROW_TILE // 8), seq // 8 - 1), 0)),
            tile,
            pl.BlockSpec((8, qkv_w), lambda i: (jnp.maximum(i * (ROW_TILE // 8) - 1, 0), 0)),
            pl.BlockSpec((4, qkv_w), lambda i: (0, 0)),
        ],
        out_specs=[tile, pl.BlockSpec((4, qkv_w), lambda i: (0, 0))],
        out_shape=[jax.ShapeDtypeStruct((seq, qkv_w), F32), jax.ShapeDtypeStruct((4, qkv_w), F32)],
        scratch_shapes=[pltpu.VMEM((ROW_TILE + 8, qkv_w), F32), pltpu.VMEM((ROW_TILE + 8, qkv_w), F32)],
        compiler_params=_params(("arbitrary",)),
        name="conv_bwd",
    )(dconv, dconv, qkvz, qkvz, conv_w)


FF_TILE = 1408
WGRAD_ROWS = 1024


def _row(a):
    return pl.BlockSpec((1, a), lambda *_: (0, 0))


def _rms_fwd(xv, gain):
    rstd = lax.rsqrt(jnp.mean(xv * xv, axis=-1, keepdims=True) + EPS)
    xhat = xv * rstd
    return xhat, rstd, xhat * gain


def _rms_bwd(dnorm, xhat, rstd, gain):
    dxhat = dnorm * gain
    dx = rstd * (dxhat - xhat * jnp.mean(dxhat * xhat, axis=-1, keepdims=True))
    return dx, jnp.sum(dnorm * xhat, axis=0, keepdims=True)


IN_SPLITS = (0, 3 * HEAD_W, 7 * HEAD_W, 7 * HEAD_W + LANES)


def _inproj_fwd(x, gain, scale, shift, w_rows):
    seq = x.shape[0]

    def body(x_ref, g_ref, sc_ref, sh_ref, w_ref, h_ref, a_ref, d_ref, b_ref):
        _, _, norm = _rms_fwd(x_ref[...], g_ref[...])
        h = (norm * (1.0 + sc_ref[...]) + sh_ref[...]).astype(BF16)
        h_ref[...] = h
        for out_ref, lo, hi in zip((a_ref, d_ref, b_ref), IN_SPLITS[:-1], IN_SPLITS[1:]):
            out_ref[...] = _nt(h, w_ref[lo:hi, :])

    def rows(width):
        return pl.BlockSpec((ROW_TILE, width), lambda i: (i, 0))

    return pl.pallas_call(
        body,
        grid=(seq // ROW_TILE,),
        in_specs=[rows(D_MODEL), _row(D_MODEL), _row(D_MODEL), _row(D_MODEL),
                  pl.BlockSpec(w_rows.shape, lambda i: (0, 0))],
        out_specs=[rows(D_MODEL), rows(3 * HEAD_W), rows(4 * HEAD_W), rows(LANES)],
        out_shape=[jax.ShapeDtypeStruct((seq, D_MODEL), BF16), jax.ShapeDtypeStruct((seq, 3 * HEAD_W), F32),
                   jax.ShapeDtypeStruct((seq, 4 * HEAD_W), F32), jax.ShapeDtypeStruct((seq, LANES), F32)],
        compiler_params=_params(("arbitrary",)),
        name="inproj_fwd",
    )(x, gain, scale, shift, w_rows)


def _outproj_fwd(y_attn, y_delta, w_out, x, gate1, gain, scale, shift):
    seq = x.shape[0]

    def body(ya_ref, yd_ref, wa_ref, wd_ref, x_ref, g1_ref, g_ref, sc_ref, sh_ref, x1_ref, h_ref, y_ref):
        y = _nn(ya_ref[...].astype(BF16), wa_ref[...]) + _nn(yd_ref[...], wd_ref[...])
        x1 = x_ref[...] + g1_ref[...] * y
        _, _, norm = _rms_fwd(x1, g_ref[...])
        x1_ref[...] = x1
        h_ref[...] = (norm * (1.0 + sc_ref[...]) + sh_ref[...]).astype(BF16)
        y_ref[...] = y.astype(BF16)

    def rows(width):
        return pl.BlockSpec((ROW_TILE, width), lambda i: (i, 0))

    return pl.pallas_call(
        body,
        grid=(seq // ROW_TILE,),
        in_specs=[rows(HEAD_W), rows(HEAD_W),
                  pl.BlockSpec((HEAD_W, D_MODEL), lambda i: (0, 0)), pl.BlockSpec((HEAD_W, D_MODEL), lambda i: (1, 0)),
                  rows(D_MODEL), _row(D_MODEL), _row(D_MODEL), _row(D_MODEL), _row(D_MODEL)],
        out_specs=[rows(D_MODEL), rows(D_MODEL), rows(D_MODEL)],
        out_shape=[jax.ShapeDtypeStruct((seq, D_MODEL), F32), jax.ShapeDtypeStruct((seq, D_MODEL), BF16),
                   jax.ShapeDtypeStruct((seq, D_MODEL), BF16)],
        compiler_params=_params(("arbitrary",)),
        name="outproj_fwd",
    )(y_attn, y_delta, w_out, w_out, x, gate1, gain, scale, shift)


def _ffn_fwd(h2, w_gate, w_up, w_down, x1, gate2, final_gain, target):
    seq = h2.shape[0]
    n_rows, n_ff = seq // ROW_TILE, D_FF // FF_TILE

    def body(h_ref, wg_ref, wu_ref, wd_ref, x1_ref, g2_ref, gf_ref, t_ref, gate_ref, up_ref, dx2_ref, st_ref, acc):
        i, j = pl.program_id(0), pl.program_id(1)

        @pl.when((i == 0) & (j == 0))
        def _():
            st_ref[...] = jnp.zeros_like(st_ref)

        h = h_ref[...]
        gate = _nt(h, wg_ref[...])
        up = _nt(h, wu_ref[...])
        gate_ref[...] = gate.astype(BF16)
        up_ref[...] = up.astype(BF16)
        part = _nn((_silu(gate) * up).astype(BF16), wd_ref[...])

        @pl.when(j == 0)
        def _():
            acc[...] = part

        @pl.when(j > 0)
        def _():
            acc[...] += part

        @pl.when(j == n_ff - 1)
        def _():
            y2 = acc[...]
            x2 = x1_ref[...] + g2_ref[...] * y2
            xhat, rstd, out = _rms_fwd(x2, gf_ref[...])
            diff = out - t_ref[...]
            dx2, dgain = _rms_bwd(diff * (1.0 / D_MODEL), xhat, rstd, gf_ref[...])
            dx2_ref[...] = dx2
            st_ref[0:1, :] += dgain
            st_ref[1:2, :] += jnp.sum(dx2 * y2, axis=0, keepdims=True)
            st_ref[2:3, :] += jnp.sum(diff * diff, axis=0, keepdims=True) * (0.5 / D_MODEL)

        @pl.when((i == n_rows - 1) & (j == n_ff - 1))
        def _():
            st_ref[3:4, :] = jnp.broadcast_to(jnp.sum(st_ref[2:3, :], keepdims=True), (1, D_MODEL))

    def rows(width):
        return pl.BlockSpec((ROW_TILE, width), lambda i, j: (i, 0))

    ff = pl.BlockSpec((ROW_TILE, FF_TILE), lambda i, j: (i, j))
    return pl.pallas_call(
        body,
        grid=(n_rows, n_ff),
        in_specs=[rows(D_MODEL),
                  pl.BlockSpec((FF_TILE, D_MODEL), lambda i, j: (j, 0)), pl.BlockSpec((FF_TILE, D_MODEL), lambda i, j: (j, 0)),
                  pl.BlockSpec((FF_TILE, D_MODEL), lambda i, j: (j, 0)),
                  rows(D_MODEL), _row(D_MODEL), _row(D_MODEL), rows(D_MODEL)],
        out_specs=[ff, ff, rows(D_MODEL), pl.BlockSpec((8, D_MODEL), lambda i, j: (0, 0))],
        out_shape=[jax.ShapeDtypeStruct((seq, D_FF), BF16), jax.ShapeDtypeStruct((seq, D_FF), BF16),
                   jax.ShapeDtypeStruct((seq, D_MODEL), F32), jax.ShapeDtypeStruct((8, D_MODEL), F32)],
        scratch_shapes=[pltpu.VMEM((ROW_TILE, D_MODEL), F32)],
        compiler_params=_params(("arbitrary", "arbitrary")),
        name="ffn_fwd",
    )(h2, w_gate, w_up, w_down, x1, gate2, final_gain, target)


def _ffn_bwd(dx2, gate, up, h2, w_gate, w_up, w_down, x1, y, gate2, gate1, gain, scale):
    seq = dx2.shape[0]

    half_tile = ROW_TILE // 2

    def act_body(dx2_ref, g2_ref, gate_ref, up_ref, h_ref, wd_ref, dgate_ref, dup_ref, dwd_ref, dwg_ref):
        dy2 = (g2_ref[...] * dx2_ref[...]).astype(BF16)
        gate = gate_ref[...].astype(F32)
        up = up_ref[...].astype(F32)
        dact = _nt(dy2, wd_ref[...])
        silu, slope = _silu_and_slope(gate)
        dgate = (dact * up * slope).astype(BF16)
        dup = (dact * silu).astype(BF16)
        dgate_ref[...] = dgate
        dup_ref[...] = dup
        parts = (_tn((silu * up).astype(BF16), dy2), _tn(dgate, h_ref[...]))

        @pl.when(pl.program_id(1) == 0)
        def _():
            for ref, part in zip((dwd_ref, dwg_ref), parts):
                ref[...] = part

        @pl.when(pl.program_id(1) > 0)
        def _():
            for ref, part in zip((dwd_ref, dwg_ref), parts):
                ref[...] += part

    ff = pl.BlockSpec((half_tile, FF_TILE), lambda j, i: (i, j))
    w_tile = pl.BlockSpec((FF_TILE, D_MODEL), lambda j, i: (j, 0))
    tokens = pl.BlockSpec((half_tile, D_MODEL), lambda j, i: (i, 0))
    dgate, dup, dw_down, dw_gate = pl.pallas_call(
        act_body,
        grid=(D_FF // FF_TILE, seq // half_tile),
        in_specs=[tokens, _row(D_MODEL), ff, ff, tokens, w_tile],
        out_specs=[ff, ff, w_tile, w_tile],
        out_shape=[jax.ShapeDtypeStruct((seq, D_FF), BF16)] * 2 + [jax.ShapeDtypeStruct((D_FF, D_MODEL), F32)] * 2,
        compiler_params=_params(("arbitrary", "arbitrary")),
        name="ffn_bwd_act",
    )(dx2, gate2, gate, up, h2, w_down)
    dw_up = _weight_grad(dup, h2, "wgrad_up")

    def in_body(dgate_ref, dup_ref, wg_ref, wu_ref, dx2_ref, x1_ref, y_ref, g1_ref, g_ref, sc_ref,
                dx1_ref, dy_ref, st_ref):
        @pl.when(pl.program_id(0) == 0)
        def _():
            st_ref[...] = jnp.zeros_like(st_ref)

        dh = _nn(dgate_ref[...], wg_ref[...]) + _nn(dup_ref[...], wu_ref[...])
        xhat, rstd, norm = _rms_fwd(x1_ref[...], g_ref[...])
        dxn, dgain = _rms_bwd(dh * (1.0 + sc_ref[...]), xhat, rstd, g_ref[...])
        dx1 = dx2_ref[...] + dxn
        dx1_ref[...] = dx1
        dy_ref[...] = (g1_ref[...] * dx1).astype(BF16)
        st_ref[0:1, :] += jnp.sum(dh, axis=0, keepdims=True)
        st_ref[1:2, :] += jnp.sum(dh * norm, axis=0, keepdims=True)
        st_ref[2:3, :] += dgain
        st_ref[3:4, :] += jnp.sum(dx1 * y_ref[...].astype(F32), axis=0, keepdims=True)


    def rows(width):
        return pl.BlockSpec((half_tile, width), lambda i: (i, 0))

    whole = pl.BlockSpec((D_FF, D_MODEL), lambda i: (0, 0))
    dx1, dy, stats = pl.pallas_call(
        in_body,
        grid=(seq // half_tile,),
        in_specs=[rows(D_FF), rows(D_FF), whole, whole, rows(D_MODEL), rows(D_MODEL), rows(D_MODEL),
                  _row(D_MODEL), _row(D_MODEL), _row(D_MODEL)],
        out_specs=[rows(D_MODEL), rows(D_MODEL), pl.BlockSpec((8, D_MODEL), lambda i: (0, 0))],
        out_shape=[jax.ShapeDtypeStruct((seq, D_MODEL), F32), jax.ShapeDtypeStruct((seq, D_MODEL), BF16),
                   jax.ShapeDtypeStruct((8, D_MODEL), F32)],
        compiler_params=_params(("arbitrary",)),
        name="ffn_bwd_in",
    )(dgate, dup, w_gate, w_up, dx2, x1, y, gate1, gain, scale)
    return (dw_gate, dw_up, dw_down), dx1, dy, stats


def _outproj_bwd(dy, w_out, y_attn, y_delta):
    seq = dy.shape[0]

    def body(dy_ref, w_ref, ya_ref, yd_ref, out_ref, dw_ref):
        @pl.when(pl.program_id(0) == 0)
        def _():
            dw_ref[...] = jnp.zeros_like(dw_ref)

        dyv = dy_ref[...]
        out_ref[...] = _nt(dyv, w_ref[...])
        dw_ref[0:HEAD_W, :] += _tn(ya_ref[...].astype(BF16), dyv)
        dw_ref[HEAD_W:, :] += _tn(yd_ref[...], dyv)

    rows = pl.BlockSpec((ROW_TILE, D_MODEL), lambda i: (i, 0))
    half = pl.BlockSpec((ROW_TILE, HEAD_W), lambda i: (i, 0))
    whole = pl.BlockSpec((D_MODEL, D_MODEL), lambda i: (0, 0))
    return pl.pallas_call(
        body,
        grid=(seq // ROW_TILE,),
        in_specs=[rows, whole, half, half],
        out_specs=[rows, whole],
        out_shape=[jax.ShapeDtypeStruct((seq, D_MODEL), F32), jax.ShapeDtypeStruct((D_MODEL, D_MODEL), F32)],
        compiler_params=_params(("arbitrary",)),
        name="outproj_bwd",
    )(dy, w_out, y_attn, y_delta)


def _inproj_bwd(dq, dk, dv, dxd, dz, dba, w_rows, x, dx1, gain, scale, partials):
    seq = x.shape[0]
    n = len(partials)
    n_steps = seq // ROW_TILE

    def body(*refs):
        pieces, (w_ref, x_ref, dx1_ref, g_ref, sc_ref) = refs[:6], refs[6:11]
        gx_ref, st_ref = refs[11 + n:13 + n]
        riding = (refs[11:11 + n], refs[13 + n:13 + 2 * n], *refs[13 + 2 * n:])

        @pl.when(pl.program_id(0) == 0)
        def _():
            st_ref[...] = jnp.zeros_like(st_ref)
            for cp in (_scatter_copies(*riding) if n else []):
                cp.start()

        dh = _nn(jnp.concatenate([p[...].astype(BF16) for p in pieces], axis=1), w_ref[...])
        xhat, rstd, norm = _rms_fwd(x_ref[...], g_ref[...])
        dxn, dgain = _rms_bwd(dh * (1.0 + sc_ref[...]), xhat, rstd, g_ref[...])
        gx_ref[...] = dx1_ref[...] + dxn
        st_ref[0:1, :] += jnp.sum(dh, axis=0, keepdims=True)
        st_ref[1:2, :] += jnp.sum(dh * norm, axis=0, keepdims=True)
        st_ref[2:3, :] += dgain

        if n:
            @pl.when(pl.program_id(0) == n_steps - 1)
            def _():
                for cp in _scatter_copies(*riding):
                    cp.wait()

    def rows(width):
        return pl.BlockSpec((ROW_TILE, width), lambda i: (i, 0))

    sems = [pltpu.SemaphoreType.DMA((3 * n,)), pltpu.SemaphoreType.DMA((3 * n,))] if n else []
    return pl.pallas_call(
        body,
        grid=(n_steps,),
        in_specs=[rows(HEAD_W), rows(HEAD_W), rows(HEAD_W), rows(3 * HEAD_W), rows(HEAD_W), rows(LANES),
                  pl.BlockSpec(w_rows.shape, lambda i: (0, 0)), rows(D_MODEL), rows(D_MODEL), _row(D_MODEL),
                  _row(D_MODEL)]
        + [ANY] * n,
        out_specs=[rows(D_MODEL), pl.BlockSpec((8, D_MODEL), lambda i: (0, 0))] + [ANY] * n,
        out_shape=[jax.ShapeDtypeStruct((seq, D_MODEL), F32), jax.ShapeDtypeStruct((8, D_MODEL), F32)]
        + [jax.ShapeDtypeStruct(p.shape, p.dtype) for p in partials],
        scratch_shapes=sems,
        compiler_params=_params(("arbitrary",)),
        name="inproj_bwd",
    )(dq, dk, dv, dxd, dz, dba, w_rows, x, dx1, gain, scale, *partials)


def _weight_grad(a, b, name):
    seq, m = a.shape
    n = b.shape[1]
    tm = m if m <= 1536 else m // 2
    tn = n if n <= 1536 else n // 2
    rows = 2 * WGRAD_ROWS
    n_k = seq // rows

    def body(a_ref, b_ref, out_ref):
        part = _tn(a_ref[...].astype(BF16), b_ref[...].astype(BF16))

        @pl.when(pl.program_id(2) == 0)
        def _():
            out_ref[...] = part

        @pl.when(pl.program_id(2) > 0)
        def _():
            out_ref[...] += part

    return pl.pallas_call(
        body,
        grid=(m // tm, n // tn, n_k),
        in_specs=[pl.BlockSpec((rows, tm), lambda i, j, k: (k, i)),
                  pl.BlockSpec((rows, tn), lambda i, j, k: (k, j))],
        out_specs=pl.BlockSpec((tm, tn), lambda i, j, k: (i, j)),
        out_shape=jax.ShapeDtypeStruct((m, n), F32),
        compiler_params=_params(("arbitrary", "arbitrary", "arbitrary")),
        name=name,
    )(a, b)


def _weight_grad_stack(pieces, b, name):
    seq, n = b.shape
    widths = [a.shape[1] for a in pieces]
    starts = [sum(widths[:i]) for i in range(len(pieces))]

    def body(*refs):
        a_refs, b_ref, out_ref = refs[:len(pieces)], refs[len(pieces)], refs[len(pieces) + 1]

        @pl.when(pl.program_id(0) == 0)
        def _():
            out_ref[...] = jnp.zeros_like(out_ref)

        bb = b_ref[...].astype(BF16)
        for a_ref, start, width in zip(a_refs, starts, widths):
            out_ref[start:start + width, :] += _tn(a_ref[...].astype(BF16), bb)

    def rows(width):
        return pl.BlockSpec((WGRAD_ROWS, width), lambda k: (k, 0))

    return pl.pallas_call(
        body,
        grid=(seq // WGRAD_ROWS,),
        in_specs=[rows(w) for w in widths] + [rows(n)],
        out_specs=pl.BlockSpec((sum(widths), n), lambda k: (0, 0)),
        out_shape=jax.ShapeDtypeStruct((sum(widths), n), F32),
        compiler_params=_params(("arbitrary",)),
        name=name,
    )(*pieces, b)


def _adamw(w, g, m, v, name):
    n_rows, n_cols = w.shape
    if w.size <= 64 * 1024:
        block, grid, index = (n_rows, n_cols), (1,), lambda i: (0, 0)
    elif n_rows % 256 == 0:
        block, grid, index = (256, n_cols), (n_rows // 256,), lambda i: (i, 0)
    elif n_cols % 256 == 0:
        block, grid, index = (n_rows, 256), (n_cols // 256,), lambda i: (0, i)
    else:
        block, grid, index = (n_rows, n_cols), (1,), lambda i: (0, 0)

    def body(w_ref, g_ref, m_ref, v_ref, d_ref, nm_ref, nv_ref):
        gv = g_ref[...]
        nm = ADAM_B1 * m_ref[...] + (1.0 - ADAM_B1) * gv
        nv = ADAM_B2 * v_ref[...] + (1.0 - ADAM_B2) * (gv * gv)
        m_hat = nm / (1.0 - ADAM_B1 ** ADAM_STEP)
        v_hat = nv / (1.0 - ADAM_B2 ** ADAM_STEP)
        d_ref[...] = -ADAM_LR * (m_hat / (jnp.sqrt(v_hat) + ADAM_EPS) + ADAM_WD * w_ref[...])
        nm_ref[...] = nm
        nv_ref[...] = nv

    blk = pl.BlockSpec(block, index)
    shape = jax.ShapeDtypeStruct((n_rows, n_cols), F32)
    return pl.pallas_call(
        body,
        grid=grid,
        in_specs=[blk] * 4,
        out_specs=[blk] * 3,
        out_shape=[shape] * 3,
        compiler_params=_params(("arbitrary",)),
        name=name,
    )(w, g, m, v)


IN_WIDTH = 3600


def _local_step(x, target, mod, norm_attn_g, w_in, rel_bias, conv_w, a_log, dt_bias, delta_norm_g,
                norm_ffn_g, final_norm_g, shards, assemble, reduce_pairs):
    sh1, sc1, g1, sh2, sc2, g2 = [mod[:, i * D_MODEL:(i + 1) * D_MODEL] for i in range(6)]
    w_rows = jnp.pad(w_in, ((0, IN_SPLITS[-1] - IN_WIDTH), (0, 0)))
    tables = jnp.asarray(_attn_tables())
    alog_row = jnp.pad(a_log, ((0, 0), (N_HEADS, LANES - 2 * N_HEADS)))
    dt_row = jnp.pad(dt_bias, ((0, 0), (N_HEADS, LANES - 2 * N_HEADS)))
    gain_row = jnp.tile(delta_norm_g, (1, N_HEADS))

    h1, qkv_a, qkvz, ba = _inproj_fwd(x, norm_attn_g, sc1, sh1, w_rows)
    bias = _attention_bias(rel_bias, tables)
    y_attn, lse, *gathered = _attention_fwd(qkv_a, bias, shards)
    w_out, w_gate, w_up, w_down = assemble(gathered)
    xs = _delta_prep_fwd(qkvz, ba, conv_w, alog_row, dt_row)
    inv_h, qk_h, u_h, w_h = _delta_chunk_fwd(xs)
    o, st_h = _delta_scan_fwd(xs, qk_h, u_h, w_h)
    y_delta = _delta_post_fwd(o, qkvz, gain_row)
    x1, h2, y = _outproj_fwd(y_attn, y_delta, w_out, x, g1, norm_ffn_g, sc2, sh2)
    gate, up, dx2, st_f = _ffn_fwd(h2, w_gate, w_up, w_down, x1, g2, final_norm_g, target)

    dw_ffn, dx1, dy, st_b = _ffn_bwd(dx2, gate, up, h2, w_gate, w_up, w_down, x1, y, g2, g1, norm_ffn_g, sc2)
    dycat, dw_out = _outproj_bwd(dy, w_out, y_attn, y_delta)
    partials = reduce_pairs([dw_out, *dw_ffn], 1, "rest")
    grads = {}
    do, dz, dgain = _delta_post_bwd(dycat, o, qkvz, gain_row)
    dsn_h, dvn_h = _delta_scan_bwd(xs, qk_h, w_h, do)
    dxs = _delta_chunk_bwd(xs, inv_h, u_h, w_h, st_h, dsn_h, dvn_h, do)
    dconv, dba, dvec = _delta_prep_bwd(qkvz, ba, conv_w, alog_row, dt_row, dxs)
    dxd, grads["conv_w"] = _conv_bwd(dconv, qkvz, conv_w)
    dq, dk, dv, dbias, *scattered = _attention_bwd(qkv_a, dycat, y_attn, lse, bias, partials)
    partials_in = reduce_pairs([jnp.concatenate(
        [_weight_grad_stack([dq, dk, dv], h1, "wgrad_in_attn"),
         _weight_grad_stack([dxd, dz, dba], h1, "wgrad_in_delta")[:IN_WIDTH - 3 * HEAD_W]], axis=0)], 0, "in")
    grad_x, st_i, *scattered_in = _inproj_bwd(dq, dk, dv, dxd, dz, dba, w_rows, x, dx1, norm_attn_g, sc1,
                                              partials_in)
    grads["rel_bias"] = _rel_bias_grad(dbias, tables)[:, :N_BUCKETS].T
    grads["a_log"] = dvec[0:1, N_HEADS:2 * N_HEADS]
    grads["dt_bias"] = dvec[1:2, N_HEADS:2 * N_HEADS]
    grads["delta_norm_g"] = dgain[1:2, :HEAD_DIM]
    grads["norm_attn_g"] = st_i[2:3]
    grads["norm_ffn_g"] = st_b[2:3]
    grads["final_norm_g"] = st_f[0:1]
    dmod = jnp.concatenate([st_i[0:1], st_i[1:2], st_b[3:4], st_b[0:1], st_b[1:2], st_f[1:2]], axis=1)
    return st_f[3, 0], grad_x, grads, dmod, (partials_in + partials, scattered_in + scattered)


MESH = pl.DeviceIdType.MESH
OTHER_CHIPS = ((1, 0), (0, 1), (1, 1))
ALL_PEERS = tuple((m >> 2 & 1, m >> 1 & 1, m & 1) for m in range(1, 8))
ANY = pl.BlockSpec(memory_space=pl.ANY)
VMEM_SPEC = pl.BlockSpec(memory_space=pltpu.VMEM)


def _me():
    return lax.axis_index("x"), lax.axis_index("y"), lax.axis_index("c")


def _flip(pos, mask):
    return tuple(1 - p if m else p for p, m in zip(pos, mask))


def _remote(src, dst, send_sems, recv_sems, k, to):
    return pltpu.make_async_remote_copy(src_ref=src, dst_ref=dst, send_sem=send_sems.at[k], recv_sem=recv_sems.at[k],
                                        device_id=to, device_id_type=MESH)


def _ada_exchange(c8, w_ada, b_ada, conv8, shard):
    def body(c_ref, w_ref, b_ref, cv_ref, shard_ref, mod_ref, cact_ref, conv_ref, whole_ref,
             c_all, part_all, send_sems, recv_sems, ride_send, ride_recv):
        x, y, c = me = _me()
        dev = 4 * x + 2 * y + c
        chip = 2 * x + y
        riding = ([shard_ref], [whole_ref], ride_send, ride_recv)
        for cp in _gather_copies(*riding, hand_over=False)[0]:
            cp.start()
        c_all[dev] = c_ref[...]
        conv_ref[chip] = cv_ref[...]
        first = [_remote(c_ref, c_all.at[dev], send_sems, recv_sems, k, _flip(me, mask))
                 for k, mask in enumerate(ALL_PEERS)]
        first += [_remote(cv_ref, conv_ref.at[chip], send_sems, recv_sems, 7 + j, _flip(me, (*mask, 0)))
                  for j, mask in enumerate(OTHER_CHIPS)]
        for cp in first:
            cp.start()
        for cp in first:
            cp.wait()
        row = lax.broadcasted_iota(jnp.int32, (8, D_MODEL), 0)
        c_rows = jnp.zeros((8, D_MODEL), F32)
        for d in range(8):
            c_rows = jnp.where(row == d, c_all[d], c_rows)
        c_act = _silu(c_rows)
        cact_ref[...] = c_act
        part_all[chip] = _nn(c_act, w_ref[...], HIGHEST)
        second = [_remote(part_all.at[chip], part_all.at[chip], send_sems, recv_sems, 10 + j, _flip(me, (*mask, 0)))
                  for j, mask in enumerate(OTHER_CHIPS)]
        for cp in second:
            cp.start()
        for cp in second:
            cp.wait()
        cols = w_ref.shape[1]
        for k in range(4):
            mod_ref[:, k * cols:(k + 1) * cols] = part_all[k] + b_ref[:, k * cols:(k + 1) * cols]
        first, passed = _gather_copies(*riding)
        for cp, fwd in zip(first, passed):
            cp.wait_recv()
            fwd.start()
        for cp in first:
            cp.wait_send()
        for fwd in passed:
            fwd.wait()

    cols = w_ada.shape[1]
    return pl.pallas_call(
        body,
        in_specs=[VMEM_SPEC] * 4 + [ANY],
        out_specs=[VMEM_SPEC] * 3 + [ANY],
        out_shape=[jax.ShapeDtypeStruct((8, 4 * cols), F32), jax.ShapeDtypeStruct((8, D_MODEL), F32),
                   jax.ShapeDtypeStruct((4, 8, conv8.shape[1]), F32)] + _gathered_shapes([shard]),
        scratch_shapes=[pltpu.VMEM((8, 8, D_MODEL), F32), pltpu.VMEM((4, 8, cols), F32),
                        pltpu.SemaphoreType.DMA((13,)), pltpu.SemaphoreType.DMA((13,)),
                        pltpu.SemaphoreType.DMA((6,)), pltpu.SemaphoreType.DMA((6,))],
        compiler_params=pltpu.CompilerParams(vmem_limit_bytes=VMEM_LIMIT),
        name="ada_exchange",
    )(c8, w_ada, b_ada, conv8, shard)


def _gathered_shapes(shards):
    return [jax.ShapeDtypeStruct((4, *s.shape), s.dtype) for s in shards]


def _gather_copies(srcs, dsts, send_sems, recv_sems, hand_over=True):
    x, y, c = me = _me()
    chip = 2 * x + y
    sibling = _flip(me, (0, 0, 1))
    first, passed = [], []
    for a, (src, dst) in enumerate(zip(srcs, dsts)):
        for j, mask in enumerate(OTHER_CHIPS):
            to = _flip(me, (*mask, 0))
            first.append(_remote(src.at[c], dst.at[chip, c], send_sems, recv_sems, 6 * a + j, to))
            if hand_over:
                landed = dst.at[2 * to[0] + to[1], c]
                passed.append(_remote(landed, landed, send_sems, recv_sems, 6 * a + 3 + j, sibling))
    return first, passed


def _scatter_copies(srcs, dsts, send_sems, recv_sems):
    x, y, c = me = _me()
    chip = 2 * x + y
    copies = []
    for a, (src, dst) in enumerate(zip(srcs, dsts)):
        for j, mask in enumerate(OTHER_CHIPS):
            to = _flip(me, (*mask, 0))
            copies.append(_remote(src.at[2 * to[0] + to[1]], dst.at[chip], send_sems, recv_sems, 3 * a + j, to))
    return copies


def _start_and_wait(copies):
    for cp in copies:
        cp.start()
    for cp in copies:
        cp.wait()


def _swap_halves(grads):
    n = len(grads)

    def body(*refs):
        srcs, got = refs[:n], refs[n:2 * n]
        send_sems, recv_sems = refs[2 * n:]
        x, y, c = me = _me()
        _start_and_wait([_remote(srcs[a].at[:, 1 - c], got[a], send_sems, recv_sems, a, _flip(me, (0, 0, 1)))
                         for a in range(n)])

    return pl.pallas_call(
        body,
        in_specs=[ANY] * n,
        out_specs=[ANY] * n,
        out_shape=[jax.ShapeDtypeStruct((4, g.shape[2], g.shape[3]), g.dtype) for g in grads],
        scratch_shapes=[pltpu.SemaphoreType.DMA((n,)), pltpu.SemaphoreType.DMA((n,))],
        name=f"swap_halves_{n}",
    )(*grads)


def _join_halves(halves):
    n = len(halves)

    def body(*refs):
        srcs, dsts = refs[:n], refs[n:2 * n]
        send_sems, recv_sems = refs[2 * n:]
        x, y, c = me = _me()
        _start_and_wait([_remote(srcs[a], dsts[a].at[c], send_sems, recv_sems, a, _flip(me, (0, 0, 1)))
                         for a in range(n)])

    return pl.pallas_call(
        body,
        in_specs=[ANY] * n,
        out_specs=[ANY] * n,
        out_shape=[jax.ShapeDtypeStruct((2, *h.shape), h.dtype) for h in halves],
        scratch_shapes=[pltpu.SemaphoreType.DMA((n,)), pltpu.SemaphoreType.DMA((n,))],
        name=f"join_halves_{n}",
    )(*halves)


def _gather_small(packed):
    n_rows = packed.shape[0]

    def body(p_ref, all_ref, sum_ref, send_sems, recv_sems):
        x, y, c = me = _me()
        dev = 4 * x + 2 * y + c
        all_ref[dev] = p_ref[...]
        copies = [_remote(p_ref, all_ref.at[dev], send_sems, recv_sems, k, _flip(me, mask))
                  for k, mask in enumerate(ALL_PEERS)]
        for cp in copies:
            cp.start()
        for cp in copies:
            cp.wait()
        total = all_ref[0]
        for d in range(1, 8):
            total = total + all_ref[d]
        sum_ref[...] = total

    return pl.pallas_call(
        body,
        in_specs=[VMEM_SPEC],
        out_specs=[VMEM_SPEC, VMEM_SPEC],
        out_shape=[jax.ShapeDtypeStruct((8, n_rows, LANES), F32), jax.ShapeDtypeStruct((n_rows, LANES), F32)],
        scratch_shapes=[pltpu.SemaphoreType.DMA((7,)), pltpu.SemaphoreType.DMA((7,))],
        name="gather_small",
    )(packed)


def _add_pair(a, b, out_dtype, name):
    def body(a_ref, b_ref, o_ref):
        o_ref[...] = (a_ref[...] + b_ref[...]).astype(o_ref.dtype)

    blk = pl.BlockSpec((1, *a.shape[1:]), lambda i: (i, 0, 0))
    return pl.pallas_call(
        body, grid=(a.shape[0],), in_specs=[blk, blk], out_specs=blk,
        out_shape=jax.ShapeDtypeStruct(a.shape, out_dtype),
        compiler_params=_params(("arbitrary",)), name=name,
    )(a, b)


def _add_slots(a, name):
    def body(a_ref, o_ref):
        total = a_ref[0].astype(F32)
        for k in range(1, 4):
            total = total + a_ref[k].astype(F32)
        o_ref[...] = total

    return pl.pallas_call(
        body, in_specs=[VMEM_SPEC], out_specs=VMEM_SPEC,
        out_shape=jax.ShapeDtypeStruct(a.shape[1:], F32),
        compiler_params=pltpu.CompilerParams(vmem_limit_bytes=VMEM_LIMIT), name=name,
    )(a)


def _ada_weight_grad(c_act, dmod_cols):
    def body(c_ref, d_ref, o_ref):
        o_ref[...] = _tn(c_ref[...], d_ref[...], HIGHEST)

    return pl.pallas_call(
        body, in_specs=[VMEM_SPEC, VMEM_SPEC], out_specs=VMEM_SPEC,
        out_shape=jax.ShapeDtypeStruct((c_act.shape[1], dmod_cols.shape[1]), F32),
        compiler_params=pltpu.CompilerParams(vmem_limit_bytes=VMEM_LIMIT), name="ada_weight_grad",
    )(c_act, dmod_cols)


def kernel(x, c, w_ada, b_ada, norm_attn_g, w_in, rel_bias, conv_w, a_log, dt_bias, delta_norm_g, w_out, norm_ffn_g, w_gate, w_up, w_down, final_norm_g, loss_target, m_w_ada, m_b_ada, m_norm_attn_g, m_w_in, m_rel_bias, m_conv_w, m_a_log, m_dt_bias, m_delta_norm_g, m_w_out, m_norm_ffn_g, m_w_gate, m_w_up, m_w_down, m_final_norm_g, v_w_ada, v_b_ada, v_norm_attn_g, v_w_in, v_rel_bias, v_conv_w, v_a_log, v_dt_bias, v_delta_norm_g, v_w_out, v_norm_ffn_g, v_w_gate, v_w_up, v_w_down, v_final_norm_g):
    xi, yi, ci = _me()
    dev = 4 * xi + 2 * yi + ci
    chip = 2 * xi + yi

    big_names = ("w_in", "w_out", "w_gate", "w_up", "w_down")
    by_cols = (True, False, True, True, False)

    def rows_form(a, cols):
        return jnp.swapaxes(a[0], 0, 1) if cols else a[0]

    def halves_form(w):
        rows, lanes = w.shape
        if (rows // 2) % 16:
            rows, lanes = w.size // LANES, LANES
        return (2, rows // 2, lanes)

    big = [rows_form(w, cols) for w, cols in zip((w_in, w_out, w_gate, w_up, w_down), by_cols)]
    shards = [w.astype(BF16).reshape(halves_form(w)) for w in big]

    def assemble(gathered, first):
        return [lax.dynamic_update_index_in_dim(g, s, chip, 0).reshape(4 * w.shape[0], w.shape[1])
                for g, s, w in zip(gathered, shards[first:], big[first:])]

    def reduce_pairs(grads, first, tag):
        slots = [g.reshape(4, *halves_form(w)) for g, w in zip(grads, big[first:])]
        return [_add_pair(lax.dynamic_index_in_dim(s, ci, 1, keepdims=False), got, BF16, f"add_pair_{tag}{a}")
                for a, (s, got) in enumerate(zip(slots, _swap_halves(slots)))]

    def finish(partials, scattered, first, tag):
        by_source = [lax.dynamic_update_index_in_dim(b, lax.dynamic_index_in_dim(p, chip, 0, keepdims=False), chip, 0)
                     for b, p in zip(scattered, partials)]
        halves = [_add_slots(p, f"add_slots_{tag}{a}") for a, p in enumerate(by_source)]
        joined = [lax.dynamic_update_index_in_dim(j, h, ci, 0) for j, h in zip(_join_halves(halves), halves)]
        return [j.reshape(w.shape) for j, w in zip(joined, big[first:])]

    conv_cols = conv_w.shape[2]
    mod_all, c_act, conv_all, gathered_in = _ada_exchange(
        jnp.broadcast_to(c, (8, D_MODEL)), w_ada[0], b_ada, jnp.pad(conv_w[0], ((0, 4), (0, 0))), shards[0])
    mod = lax.dynamic_slice_in_dim(mod_all, dev, 1, axis=0)
    conv_full = jnp.swapaxes(conv_all[:, :4, :], 0, 1).reshape(4, 4 * conv_cols)
    whole_in, = assemble([gathered_in], 0)
    loss, grad_x, grads, dmod, (partials, scattered) = _local_step(
        x[0], loss_target[0], mod, norm_attn_g, whole_in, rel_bias, conv_full, a_log, dt_bias, delta_norm_g,
        norm_ffn_g, final_norm_g[None], shards[1:], functools.partial(assemble, first=1), reduce_pairs)

    big_grads = finish(partials, scattered, 0, "all")

    pieces = [dmod, grads["conv_w"], grads["norm_attn_g"], grads["norm_ffn_g"], grads["final_norm_g"],
              grads["rel_bias"], grads["a_log"], grads["dt_bias"], grads["delta_norm_g"]]
    flat = [jnp.pad(p.reshape(-1), (0, -p.size % LANES)) for p in pieces]
    n_rows = [f.size // LANES for f in flat]
    packed = jnp.concatenate(flat).reshape(-1, LANES)
    packed = jnp.pad(packed, ((0, -packed.shape[0] % 8), (0, 0)))
    all_small, total = _gather_small(packed)
    sums, start = [], 0
    for p, n in zip(pieces, n_rows):
        sums.append(total[start:start + n].reshape(-1)[:p.size].reshape(p.shape))
        start += n
    g_b_ada, g_conv, g_norm_attn, g_norm_ffn, g_final, g_rel, g_alog, g_dt, g_dnorm = sums
    dmod_all = all_small[:, :n_rows[0], :].reshape(8, -1)
    ada_cols = w_ada.shape[2]
    g_w_ada = _ada_weight_grad(c_act, lax.dynamic_slice_in_dim(dmod_all, chip * ada_cols, ada_cols, axis=1))
    g_conv = lax.dynamic_slice_in_dim(g_conv, chip * conv_cols, conv_cols, axis=1)

    grad = {"w_ada": g_w_ada[None], "b_ada": g_b_ada, "norm_attn_g": g_norm_attn,
            "rel_bias": g_rel, "conv_w": g_conv[None], "a_log": g_alog, "dt_bias": g_dt, "delta_norm_g": g_dnorm,
            "norm_ffn_g": g_norm_ffn, "final_norm_g": g_final.reshape(-1)}
    weight = {"w_ada": w_ada, "b_ada": b_ada, "norm_attn_g": norm_attn_g, "w_in": w_in, "rel_bias": rel_bias,
              "conv_w": conv_w, "a_log": a_log, "dt_bias": dt_bias, "delta_norm_g": delta_norm_g, "w_out": w_out,
              "norm_ffn_g": norm_ffn_g, "w_gate": w_gate, "w_up": w_up, "w_down": w_down, "final_norm_g": final_norm_g}
    first = {"w_ada": m_w_ada, "b_ada": m_b_ada, "norm_attn_g": m_norm_attn_g, "w_in": m_w_in, "rel_bias": m_rel_bias,
             "conv_w": m_conv_w, "a_log": m_a_log, "dt_bias": m_dt_bias, "delta_norm_g": m_delta_norm_g,
             "w_out": m_w_out, "norm_ffn_g": m_norm_ffn_g, "w_gate": m_w_gate, "w_up": m_w_up, "w_down": m_w_down,
             "final_norm_g": m_final_norm_g}
    second = {"w_ada": v_w_ada, "b_ada": v_b_ada, "norm_attn_g": v_norm_attn_g, "w_in": v_w_in, "rel_bias": v_rel_bias,
              "conv_w": v_conv_w, "a_log": v_a_log, "dt_bias": v_dt_bias, "delta_norm_g": v_delta_norm_g,
              "w_out": v_w_out, "norm_ffn_g": v_norm_ffn_g, "w_gate": v_w_gate, "w_up": v_w_up, "w_down": v_w_down,
              "final_norm_g": v_final_norm_g}
    delta, new_m, new_v = {}, {}, {}
    for name, w in weight.items():
        if name in big_names:
            continue
        two_d = (-1, w.shape[-1])
        d, nm, nv = _adamw(w.reshape(two_d), grad[name].reshape(two_d), first[name].reshape(two_d),
                           second[name].reshape(two_d), f"adamw_{name}")
        delta[name], new_m[name], new_v[name] = d.reshape(w.shape), nm.reshape(w.shape), nv.reshape(w.shape)
    for name, w, g, cols in zip(big_names, big, big_grads, by_cols):
        outs = _adamw(w, g, rows_form(first[name], cols), rows_form(second[name], cols), f"adamw_{name}")
        grad[name], delta[name], new_m[name], new_v[name] = [
            (jnp.swapaxes(o, 0, 1) if cols else o)[None] for o in (g, *outs)]

    names = list(weight)
    return (lax.psum(loss, ("x", "y", "c")), grad_x[None], *[grad[n] for n in names], *[delta[n] for n in names],
            *[new_m[n] for n in names], *[new_v[n] for n in names])
```

```python
import functools
import math

import numpy as np
import jax
import jax.numpy as jnp
from jax import lax
from jax.experimental import pallas as pl
from jax.experimental.pallas import tpu as pltpu

F32 = jnp.float32
BF16 = jnp.bfloat16
HIGHEST = lax.Precision.HIGHEST

D_MODEL = 1024
HEAD_DIM = 64
N_HEADS = 8
HEAD_W = 512
BRANCHES = ((128, 1), (512, 4), (2048, 16))
BAND = 128
ATT_TILE = 2048
ATT_UNROLL = 8
ATT_UNROLL_BWD = 4
N_BUCKETS = 32
MAX_DISTANCE = 2048
CHUNK = 64
D_FF = 2816
EPS = 1e-6
NEG_INF = -1e30
LANES = 128
VMEM_LIMIT = 56 * 1024 * 1024

ADAM_LR = 0.001
ADAM_B1 = 0.9
ADAM_B2 = 0.999
ADAM_EPS = 1e-08
ADAM_WD = 0.01
ADAM_STEP = 10


def _nn(a, b, precision=None):
    return jnp.dot(a, b, preferred_element_type=F32, precision=precision)


def _nt(a, b, precision=None):
    return lax.dot_general(a, b, (((1,), (1,)), ((), ())), preferred_element_type=F32, precision=precision)


def _tn(a, b, precision=None):
    return lax.dot_general(a, b, (((0,), (0,)), ((), ())), preferred_element_type=F32, precision=precision)


def _params(sem, vmem=VMEM_LIMIT):
    return pltpu.CompilerParams(dimension_semantics=sem, vmem_limit_bytes=vmem)


def _sigmoid(x):
    return 0.5 * jnp.tanh(0.5 * x) + 0.5


def _silu_and_slope(x):
    s = _sigmoid(x)
    return x * s, s * (1.0 + x * (1.0 - s))


def _silu(x):
    return x * _sigmoid(x)


def _attn_tables():
    qi = np.arange(BAND)[:, None]
    kj = np.arange(2 * BAND)[None, :]
    steps = qi + BAND - kj
    in_window = (steps >= 0) & (steps <= BAND)
    max_exact = N_BUCKETS // 2
    out = np.zeros((3, 2, BAND, 2 * BAND), np.int32)
    for b, (_, dil) in enumerate(BRANCHES):
        dist = np.maximum(steps, 0) * dil
        dist_f = np.maximum(dist, 1).astype(np.float32)
        large = max_exact + (np.log(dist_f / np.float32(max_exact)) / np.float32(math.log(MAX_DISTANCE / max_exact))
                             * np.float32(N_BUCKETS - max_exact)).astype(np.int32)
        bucket = np.where(dist < max_exact, dist, np.minimum(large, N_BUCKETS - 1)).astype(np.int32)
        out[b, 0] = np.where(in_window, bucket, -1)
        out[b, 1] = np.where(in_window & (kj >= BAND), bucket, -1)
    return out


def _attention_bias(rel_bias, tables):
    def body(rel_ref, tab_ref, out_ref):
        head = pl.program_id(0)
        for b in range(3):
            tab = tab_ref[b, 0]

            def pick(kk, acc, tab=tab):
                return jnp.where(tab == kk, rel_ref[kk, head], acc)

            acc = lax.fori_loop(0, N_BUCKETS, pick, jnp.zeros((BAND, 2 * BAND), F32))
            for first in range(2):
                out_ref[0, b, first] = jnp.where(tab_ref[b, first] < 0, NEG_INF, acc)

    return pl.pallas_call(
        body,
        grid=(N_HEADS,),
        in_specs=[pl.BlockSpec(memory_space=pltpu.SMEM),
                  pl.BlockSpec((3, 2, BAND, 2 * BAND), lambda h: (0, 0, 0, 0))],
        out_specs=pl.BlockSpec((1, 3, 2, BAND, 2 * BAND), lambda h: (h, 0, 0, 0, 0)),
        out_shape=jax.ShapeDtypeStruct((N_HEADS, 3, 2, BAND, 2 * BAND), F32),
        compiler_params=_params(("arbitrary",)),
        name="attn_bias",
    )(rel_bias, tables)


def _bias_spec():
    return pl.BlockSpec((2, 3, 2, BAND, 2 * BAND), lambda p, t: (p, 0, 0, 0, 0))


def _attn_block_index(idx, t, r):
    nb = ATT_TILE // (BAND * r)
    rho = idx // nb
    n = idx % nb
    qs = rho + r * BAND * n
    gs = t * ATT_TILE + qs
    first = (t * nb + n) == 0
    ps = jnp.where(first, gs, gs - r * BAND)
    return qs, gs, ps, first.astype(jnp.int32)


def _rows(start, r):
    return pl.ds(start, BAND) if r == 1 else pl.ds(start, BAND, stride=r)


def _attention_fwd(qkv, bias, shards):
    seq = qkv.shape[0]
    n_tiles = seq // ATT_TILE
    n = len(shards)

    def body(*refs):
        bias_ref, q_ref, k_ref, v_ref = refs[:4]
        y_ref, lse_ref = refs[4 + n:6 + n]
        o_s, l_s = refs[6 + 2 * n:8 + 2 * n]
        riding = (refs[4:4 + n], refs[6 + n:6 + 2 * n], *refs[8 + 2 * n:])
        pair = pl.program_id(0)
        t = pl.program_id(1)
        if n:
            @pl.when((pair == 0) & (t == 0))
            def _():
                for cp in _gather_copies(*riding, hand_over=False)[0]:
                    cp.start()

            @pl.when((pair == 2) & (t == 0))
            def _():
                for cp, fwd in zip(*_gather_copies(*riding)):
                    cp.wait_recv()
                    fwd.start()

        lane = lax.broadcasted_iota(jnp.int32, (1, LANES), 1)
        head0 = lane < HEAD_DIM
        masks = (head0, jnp.logical_not(head0))
        ones = jnp.ones((2 * BAND, LANES), BF16)
        for b, (_, r) in enumerate(BRANCHES):
            def blocks(it, carry, b=b, r=r):
                idx = [_attn_block_index(it * ATT_UNROLL + j, t, r) for j in range(ATT_UNROLL)]
                qb = [q_ref[_rows(qs, r), :] * (HEAD_DIM ** -0.5) for qs, _, _, _ in idx]
                kcat = [jnp.concatenate([k_ref[_rows(ps, r), :], k_ref[_rows(gs, r), :]], axis=0).astype(BF16)
                        for _, gs, ps, _ in idx]
                vcat = [jnp.concatenate([v_ref[_rows(ps, r), :], v_ref[_rows(gs, r), :]], axis=0).astype(BF16)
                        for _, gs, ps, _ in idx]
                work = [(j, hh) for j in range(ATT_UNROLL) for hh in range(2)]
                s = [_nt(jnp.where(masks[hh], qb[j], 0.0).astype(BF16), kcat[j]) + bias_ref[hh, b, idx[j][3]]
                     for j, hh in work]
                m = [jnp.max(sv, axis=-1, keepdims=True) for sv in s]
                e = [jnp.exp(sv - mv) for sv, mv in zip(s, m)]
                eb = [ev.astype(BF16) for ev in e]
                den = [_nn(ev, ones) for ev in eb]
                out = [_nn(ev, vcat[j]) / dv for ev, dv, (j, _) in zip(eb, den, work)]
                lse = [mv + jnp.log(dv) for mv, dv in zip(m, den)]
                for j in range(ATT_UNROLL):
                    o_s[b, _rows(idx[j][0], r), :] = jnp.where(head0, out[2 * j], out[2 * j + 1])
                    l_s[b, _rows(idx[j][0], r), :] = jnp.where(head0, lse[2 * j], lse[2 * j + 1])
                return carry

            lax.fori_loop(0, ATT_TILE // BAND // ATT_UNROLL, blocks, 0)

        def merge(i, carry):
            rows = pl.ds(pl.multiple_of(i * BAND, BAND), BAND)
            l0, l1, l2 = l_s[0, rows, :], l_s[1, rows, :], l_s[2, rows, :]
            m = jnp.maximum(jnp.maximum(l0, l1), l2)
            w0, w1, w2 = jnp.exp(l0 - m), jnp.exp(l1 - m), jnp.exp(l2 - m)
            tot = w0 + w1 + w2
            y_ref[rows, :] = (w0 * o_s[0, rows, :] + w1 * o_s[1, rows, :] + w2 * o_s[2, rows, :]) / tot
            lse_ref[rows, :] = m + jnp.log(tot)
            return carry

        lax.fori_loop(0, ATT_TILE // BAND, merge, 0)

        if n:
            @pl.when((pair == N_HEADS // 2 - 1) & (t == n_tiles - 1))
            def _():
                first, passed = _gather_copies(*riding)
                for cp in first:
                    cp.wait_send()
                for fwd in passed:
                    fwd.wait()

    tile = pl.BlockSpec((ATT_TILE, LANES), lambda p, t: (t, p))
    sems = [pltpu.SemaphoreType.DMA((6 * n,)), pltpu.SemaphoreType.DMA((6 * n,))] if n else []
    return pl.pallas_call(
        body,
        grid=(N_HEADS // 2, n_tiles),
        in_specs=[
            _bias_spec(),
            pl.BlockSpec((ATT_TILE, LANES), lambda p, t: (t, p)),
            pl.BlockSpec((seq, LANES), lambda p, t: (0, 4 + p)),
            pl.BlockSpec((seq, LANES), lambda p, t: (0, 8 + p)),
        ] + [ANY] * n,
        out_specs=[tile, tile] + [ANY] * n,
        out_shape=[jax.ShapeDtypeStruct((seq, HEAD_W), F32), jax.ShapeDtypeStruct((seq, HEAD_W), F32)]
        + _gathered_shapes(shards),
        scratch_shapes=[
            pltpu.VMEM((3, ATT_TILE, LANES), F32),
            pltpu.VMEM((3, ATT_TILE, LANES), F32),
        ] + sems,
        compiler_params=_params(("arbitrary", "arbitrary")),
        name="attn_fwd",
    )(bias, qkv, qkv, qkv, *shards)


def _attention_bwd(qkv, dy, y, lse, bias, partials):
    seq = qkv.shape[0]
    n_tiles = seq // ATT_TILE
    n = len(partials)

    def body(*refs):
        bias_ref, q_ref, k_ref, v_ref, dy_ref, y_ref, lse_ref = refs[:7]
        dq_ref, dk_ref, dv_ref, dbias_ref = refs[7 + n:11 + n]
        riding = (refs[7:7 + n], refs[11 + n:11 + 2 * n], *refs[11 + 2 * n:])
        pair = pl.program_id(0)
        t = pl.program_id(1)
        if n:
            @pl.when((pair == 0) & (t == 0))
            def _():
                for cp in _scatter_copies(*riding):
                    cp.start()

        lane = lax.broadcasted_iota(jnp.int32, (1, LANES), 1)
        head0 = lane < HEAD_DIM

        @pl.when(t == 0)
        def _():
            dk_ref[...] = jnp.zeros_like(dk_ref)
            dv_ref[...] = jnp.zeros_like(dv_ref)
            dbias_ref[...] = jnp.zeros_like(dbias_ref)

        dq_ref[...] = jnp.zeros_like(dq_ref)

        masks = (head0, jnp.logical_not(head0))
        ones = jnp.ones((LANES, LANES), BF16)
        scale = HEAD_DIM ** -0.5
        for b, (_, r) in enumerate(BRANCHES):
            def blocks(it, carry, b=b, r=r):
                idx = [_attn_block_index(it * ATT_UNROLL_BWD + j, t, r) for j in range(ATT_UNROLL_BWD)]
                qb = [q_ref[_rows(qs, r), :] * scale for qs, _, _, _ in idx]
                kcat = [jnp.concatenate([k_ref[_rows(ps, r), :], k_ref[_rows(gs, r), :]], axis=0).astype(BF16)
                        for _, gs, ps, _ in idx]
                vcat = [jnp.concatenate([v_ref[_rows(ps, r), :], v_ref[_rows(gs, r), :]], axis=0).astype(BF16)
                        for _, gs, ps, _ in idx]
                dob = [dy_ref[_rows(qs, r), :] for qs, _, _, _ in idx]
                ob = [y_ref[_rows(qs, r), :] for qs, _, _, _ in idx]
                lb = [lse_ref[_rows(qs, r), :] for qs, _, _, _ in idx]
                work = [(j, hh) for j in range(ATT_UNROLL_BWD) for hh in range(2)]
                qh = [jnp.where(masks[hh], qb[j], 0.0).astype(BF16) for j, hh in work]
                doh = [jnp.where(masks[hh], dob[j], 0.0) for j, hh in work]
                dohb = [d.astype(BF16) for d in doh]
                s = [_nt(qh[w], kcat[j]) + bias_ref[hh, b, idx[j][3]] for w, (j, hh) in enumerate(work)]
                dp = [_nt(dohb[w], vcat[j]) for w, (j, _) in enumerate(work)]
                lrot = [pltpu.roll(lv, HEAD_DIM, 1) for lv in lb]
                lcol = [jnp.where(masks[hh], lb[j], lrot[j]) for j, hh in work]
                parts = [_split(doh[w] * ob[j]) for w, (j, _) in enumerate(work)]
                delta = [_nn(hi, ones) + _nn(lo, ones) for hi, lo in parts]
                prob = [jnp.exp(sv - jnp.concatenate([lv, lv], axis=1)) for sv, lv in zip(s, lcol)]
                ds = [pv * (dv - jnp.concatenate([de, de], axis=1)) for pv, dv, de in zip(prob, dp, delta)]
                dsb = [d.astype(BF16) for d in ds]
                dq = [_nn(dsb[w], kcat[j]) for w, (j, _) in enumerate(work)]
                dkc = [_tn(dsb[w], qh[w]) for w in range(len(work))]
                dvc = [_tn(prob[w].astype(BF16), dohb[w]) for w in range(len(work))]
                for hh in range(2):
                    dbias_ref[0, b, hh] += sum(ds[w] for w, (_, head) in enumerate(work) if head == hh)
                for j in range(ATT_UNROLL_BWD):
                    qs, gs, ps, _ = idx[j]
                    dkcat = dkc[2 * j] + dkc[2 * j + 1]
                    dvcat = dvc[2 * j] + dvc[2 * j + 1]
                    dq_ref[_rows(qs, r), :] += jnp.where(head0, dq[2 * j], dq[2 * j + 1]) * scale
                    dk_ref[_rows(ps, r), :] += dkcat[:BAND]
                    dk_ref[_rows(gs, r), :] += dkcat[BAND:]
                    dv_ref[_rows(ps, r), :] += dvcat[:BAND]
                    dv_ref[_rows(gs, r), :] += dvcat[BAND:]
                return carry

            lax.fori_loop(0, ATT_TILE // BAND // ATT_UNROLL_BWD, blocks, 0)

        if n:
            @pl.when((pair == N_HEADS // 2 - 1) & (t == n_tiles - 1))
            def _():
                for cp in _scatter_copies(*riding):
                    cp.wait()

    tile = pl.BlockSpec((ATT_TILE, LANES), lambda p, t: (t, p))
    full = pl.BlockSpec((seq, LANES), lambda p, t: (0, p))
    sems = [pltpu.SemaphoreType.DMA((3 * n,)), pltpu.SemaphoreType.DMA((3 * n,))] if n else []
    return pl.pallas_call(
        body,
        grid=(N_HEADS // 2, n_tiles),
        in_specs=[
            _bias_spec(),
            pl.BlockSpec((ATT_TILE, LANES), lambda p, t: (t, p)),
            pl.BlockSpec((seq, LANES), lambda p, t: (0, 4 + p)),
            pl.BlockSpec((seq, LANES), lambda p, t: (0, 8 + p)),
            tile, tile, tile,
        ] + [ANY] * n,
        out_specs=[tile, full, full,
                   pl.BlockSpec((1, 3, 2, BAND, 2 * BAND), lambda p, t: (p, 0, 0, 0, 0))] + [ANY] * n,
        out_shape=[jax.ShapeDtypeStruct((seq, HEAD_W), F32)] * 3
        + [jax.ShapeDtypeStruct((N_HEADS // 2, 3, 2, BAND, 2 * BAND), F32)]
        + [jax.ShapeDtypeStruct(p.shape, p.dtype) for p in partials],
        scratch_shapes=sems,
        compiler_params=_params(("arbitrary", "arbitrary")),
        name="attn_bwd",
    )(bias, qkv, qkv, qkv, dy, y, lse, *partials)


def _rel_bias_grad(dbias, tables):
    def body(tab_ref, db_ref, out_ref):
        lane = lax.broadcasted_iota(jnp.int32, (1, LANES), 1)
        out_ref[...] = jnp.zeros_like(out_ref)
        for b in range(3):
            tab = tab_ref[b, 0]

            def head(h, carry, b=b, tab=tab):
                d = db_ref[h // 2, b, h % 2]
                sums = [jnp.sum(jnp.where(tab == kk, d, 0.0), keepdims=True) for kk in range(N_BUCKETS)]
                row = jnp.zeros((1, LANES), F32)
                for kk, s in enumerate(sums):
                    row = row + jnp.where(lane == kk, s, 0.0)
                out_ref[pl.ds(h, 1), :] += row
                return carry

            lax.fori_loop(0, N_HEADS, head, 0)

    return pl.pallas_call(
        body,
        out_shape=jax.ShapeDtypeStruct((N_HEADS, LANES), F32),
        compiler_params=pltpu.CompilerParams(vmem_limit_bytes=VMEM_LIMIT),
        name="rel_bias_grad",
    )(tables, dbias)


ROW_TILE = 512


def _head_sum_matrix():
    return (lax.broadcasted_iota(jnp.int32, (HEAD_W, HEAD_W), 0) // HEAD_DIM
            == lax.broadcasted_iota(jnp.int32, (HEAD_W, HEAD_W), 1) // HEAD_DIM).astype(F32)


def _head_spread_matrix(offset=0):
    return (lax.broadcasted_iota(jnp.int32, (LANES, HEAD_W), 0)
            == lax.broadcasted_iota(jnp.int32, (LANES, HEAD_W), 1) // HEAD_DIM + offset).astype(F32)


def _head_gather_matrix(offset=0):
    return (lax.broadcasted_iota(jnp.int32, (HEAD_W, LANES), 0) // HEAD_DIM + offset
            == lax.broadcasted_iota(jnp.int32, (HEAD_W, LANES), 1)).astype(F32)


def _split3(x):
    hi = x.astype(BF16)
    rest = x - hi.astype(F32)
    mid = rest.astype(BF16)
    return hi, mid, (rest - mid.astype(F32)).astype(BF16)


def _pick(x, onehot):
    m = onehot.astype(BF16)
    hi, mid, lo = _split3(x)
    return _nn(hi, m) + (_nn(mid, m) + _nn(lo, m))


def _pick_left(onehot, x):
    m = onehot.astype(BF16)
    hi, mid, lo = _split3(x)
    return _nn(m, hi) + (_nn(m, mid) + _nn(m, lo))


def _tri(lower, strict=False):
    r = lax.broadcasted_iota(jnp.int32, (CHUNK, CHUNK), 0)
    c = lax.broadcasted_iota(jnp.int32, (CHUNK, CHUNK), 1)
    if lower:
        return (c < r) if strict else (c <= r)
    return c >= r


def _softplus(z):
    return jnp.maximum(z, 0.0) + jnp.log(1.0 + jnp.exp(-jnp.abs(z)))


def _conv_taps(stage, w_ref, rows):
    return (w_ref[3:4, :] * stage[8:8 + rows, :] + w_ref[2:3, :] * stage[7:7 + rows, :]
            + w_ref[1:2, :] * stage[6:6 + rows, :] + w_ref[0:1, :] * stage[5:5 + rows, :])


def _l2_scale(xc, hsum):
    return lax.rsqrt(_pick(xc * xc, hsum) + EPS)


def _stage_rows(stage, x_ref, xp_ref, i):
    stage[0:8, :] = jnp.where(i == 0, 0.0, xp_ref[...])
    stage[8:8 + ROW_TILE, :] = x_ref[...]


def _delta_prep_fwd(qkvz, ba, conv_w, alog_row, dt_row):
    seq = qkvz.shape[0]
    qkv_w = 3 * HEAD_W

    def body(x_ref, xp_ref, ba_ref, w_ref, al_ref, dt_ref, out_ref, stage):
        i = pl.program_id(0)
        _stage_rows(stage, x_ref, xp_ref, i)
        act = _silu(_conv_taps(stage, w_ref, ROW_TILE))
        hsum, hspread = _head_sum_matrix(), _head_spread_matrix()
        qc, kc = act[:, :HEAD_W], act[:, HEAD_W:2 * HEAD_W]
        out_ref[0] = qc * _l2_scale(qc, hsum) * (HEAD_DIM ** -0.5)
        out_ref[1] = kc * _l2_scale(kc, hsum)
        out_ref[2] = act[:, 2 * HEAD_W:]
        bav = ba_ref[...]
        out_ref[3] = _pick(_sigmoid(bav), hspread)
        g8 = -jnp.exp(al_ref[...]) * _softplus(bav + dt_ref[...])
        gb = _pick(g8, _head_spread_matrix(N_HEADS))
        cum = _tri(True).astype(F32)
        for ch in range(ROW_TILE // CHUNK):
            rows = slice(ch * CHUNK, (ch + 1) * CHUNK)
            out_ref[4, rows, :] = _pick_left(cum, gb[rows])

    return pl.pallas_call(
        body,
        grid=(seq // ROW_TILE,),
        in_specs=[
            pl.BlockSpec((ROW_TILE, qkv_w), lambda i: (i, 0)),
            pl.BlockSpec((8, qkv_w), lambda i: (jnp.maximum(i * (ROW_TILE // 8) - 1, 0), 0)),
            pl.BlockSpec((ROW_TILE, LANES), lambda i: (i, 0)),
            pl.BlockSpec((4, qkv_w), lambda i: (0, 0)),
            pl.BlockSpec((1, LANES), lambda i: (0, 0)),
            pl.BlockSpec((1, LANES), lambda i: (0, 0)),
        ],
        out_specs=pl.BlockSpec((5, ROW_TILE, HEAD_W), lambda i: (0, i, 0)),
        out_shape=jax.ShapeDtypeStruct((5, seq, HEAD_W), F32),
        scratch_shapes=[pltpu.VMEM((ROW_TILE + 8, qkv_w), F32)],
        compiler_params=_params(("arbitrary",)),
        name="delta_prep_fwd",
    )(qkvz, qkvz, ba, conv_w, alog_row, dt_row)


def _split(x):
    hi = x.astype(BF16)
    return hi, (x - hi.astype(F32)).astype(BF16)


def _dot3(a, b, dot=_nn):
    return dot(a[0], b[0]) + (dot(a[0], b[1]) + dot(a[1], b[0]))


def _unit_lower_inverses(mats):
    eye = (lax.broadcasted_iota(jnp.int32, (CHUNK, CHUNK), 0)
           == lax.broadcasted_iota(jnp.int32, (CHUNK, CHUNK), 1)).astype(F32)
    invs = [eye - a for a in mats]
    powers = [_split(a) for a in mats]
    for step in range(5):
        squares = [_dot3(p, p) for p in powers]
        powers = [_split(s) for s in squares]
        invs = [inv + _dot3(_split(inv), p) for inv, p in zip(invs, powers)]
    return invs


def _chunk_terms(q, k, v, beta, gc):
    causal, strict = _tri(True), _tri(True, strict=True)
    e = jnp.exp(gc)
    g_last = jnp.broadcast_to(gc[CHUNK - 1:CHUNK, :], (CHUNK, CHUNK))
    f = jnp.exp(g_last - gc)
    e_last = jnp.exp(g_last)
    decay = jnp.where(causal, jnp.exp(jnp.where(causal, gc - gc.T, 0.0)), 0.0)
    kb = k * beta
    a_mat = jnp.where(strict, _nt(kb.astype(BF16), k.astype(BF16)) * decay, 0.0)
    qk = jnp.where(causal, _nt(q.astype(BF16), k.astype(BF16)) * decay, 0.0)
    return e, f, e_last, decay, kb, a_mat, qk


GROUP = 8
UNROLL = 8


def _chunk_rows(ci):
    return pl.ds(pl.multiple_of(ci * CHUNK, CHUNK), CHUNK)


def _pair_specs(n_planes):
    return pl.BlockSpec((n_planes, GROUP * CHUNK, LANES), lambda p, g: (0, g, p))


def _delta_chunk_fwd(xs):
    seq = xs.shape[1]
    rows_per_step = GROUP * CHUNK

    def body(x_ref, inv_ref, qk_ref, u_ref, w_ref):
        work = [(hh, slice(step * CHUNK, (step + 1) * CHUNK)) for hh in range(2) for step in range(GROUP)]
        xh = [[x_ref[j, r, hh * HEAD_DIM:(hh + 1) * HEAD_DIM] for j in range(5)] for hh, r in work]
        terms = [_chunk_terms(*x) for x in xh]
        invs = _unit_lower_inverses([t[5] for t in terms])
        for (hh, r), x, t, inv in zip(work, xh, terms, invs):
            e, kb, qk = t[0], t[4], t[6]
            inv_parts = _split(inv)
            inv_ref[hh, r, :] = inv
            qk_ref[hh, r, :] = qk
            u_ref[hh, r, :] = _dot3(inv_parts, _split(x[2] * x[3]))
            w_ref[hh, r, :] = _dot3(inv_parts, _split(kb * e))

    out = pl.BlockSpec((2, rows_per_step, HEAD_DIM), lambda p, g: (p, g, 0))
    return pl.pallas_call(
        body,
        grid=(N_HEADS // 2, seq // rows_per_step),
        in_specs=[_pair_specs(5)],
        out_specs=[out] * 4,
        out_shape=[jax.ShapeDtypeStruct((N_HEADS, seq, HEAD_DIM), F32)] * 4,
        compiler_params=_params(("parallel", "parallel")),
        name="delta_chunk_fwd",
    )(xs)


def _decays(gc):
    g_last = jnp.broadcast_to(gc[CHUNK - 1:CHUNK, :], (CHUNK, CHUNK))
    return jnp.exp(gc), jnp.exp(g_last - gc), jnp.exp(g_last)


def _token_blocks(index, n_steps=None):
    rows_per_step = GROUP * CHUNK
    if n_steps is None:
        return pl.BlockSpec((1, rows_per_step, HEAD_W), lambda g: (index, g, 0))
    return pl.BlockSpec((1, rows_per_step, HEAD_W), lambda g: (index, n_steps - 1 - g, 0))


def _head_lanes(h):
    return pl.ds(h * HEAD_DIM, HEAD_DIM)


def _delta_scan_fwd(xs, qk_h, u_h, w_h):
    seq = xs.shape[1]
    rows_per_step = GROUP * CHUNK

    def body(q_ref, k_ref, gc_ref, qk_ref, u_ref, w_ref, o_ref, st_ref, state):
        @pl.when(pl.program_id(0) == 0)
        def _():
            state[...] = jnp.zeros_like(state)

        def chunk(ci, carry):
            rows = _chunk_rows(ci)
            heads = range(N_HEADS)
            dec = [_decays(gc_ref[0, rows, _head_lanes(h)]) for h in heads]
            s = [state[h] for h in heads]
            sb = [s[h].astype(BF16) for h in heads]
            vnb = [(u_ref[h, rows, :] - _nn(w_ref[h, rows, :].astype(BF16), sb[h])).astype(BF16) for h in heads]
            for h in heads:
                o_ref[rows, _head_lanes(h)] = (_nn((q_ref[0, rows, _head_lanes(h)] * dec[h][0]).astype(BF16), sb[h])
                                               + _nn(qk_ref[h, rows, :].astype(BF16), vnb[h]))
                st_ref[h, rows, :] = s[h]
            for h in heads:
                state[h] = s[h] * dec[h][2] + _tn((k_ref[0, rows, _head_lanes(h)] * dec[h][1]).astype(BF16), vnb[h])
            return carry

        lax.fori_loop(0, GROUP, chunk, 0)

    blk = pl.BlockSpec((N_HEADS, rows_per_step, HEAD_DIM), lambda g: (0, g, 0))
    return pl.pallas_call(
        body,
        grid=(seq // rows_per_step,),
        in_specs=[_token_blocks(0), _token_blocks(1), _token_blocks(4), blk, blk, blk],
        out_specs=[pl.BlockSpec((rows_per_step, HEAD_W), lambda g: (g, 0)), blk],
        out_shape=[jax.ShapeDtypeStruct((seq, HEAD_W), F32), jax.ShapeDtypeStruct((N_HEADS, seq, HEAD_DIM), F32)],
        scratch_shapes=[pltpu.VMEM((N_HEADS, CHUNK, CHUNK), F32)],
        compiler_params=_params(("arbitrary",)),
        name="delta_scan_fwd",
    )(xs, xs, xs, qk_h, u_h, w_h)


def _delta_scan_bwd(xs, qk_h, w_h, do):
    seq = xs.shape[1]
    rows_per_step = GROUP * CHUNK
    n_steps = seq // rows_per_step

    def body(q_ref, k_ref, gc_ref, qk_ref, w_ref, do_ref, dsn_ref, dvn_ref, dstate):
        @pl.when(pl.program_id(0) == 0)
        def _():
            dstate[...] = jnp.zeros_like(dstate)

        def chunk(step, carry):
            rows = _chunk_rows(GROUP - 1 - step)
            heads = range(N_HEADS)
            dec = [_decays(gc_ref[0, rows, _head_lanes(h)]) for h in heads]
            ds_next = [dstate[h] for h in heads]
            dob = [do_ref[rows, _head_lanes(h)].astype(BF16) for h in heads]
            dv_new = [_tn(qk_ref[h, rows, :].astype(BF16), dob[h])
                      + _nn((k_ref[0, rows, _head_lanes(h)] * dec[h][1]).astype(BF16), ds_next[h].astype(BF16))
                      for h in heads]
            for h in heads:
                dsn_ref[h, rows, :] = ds_next[h]
                dvn_ref[h, rows, :] = dv_new[h]
            for h in heads:
                dstate[h] = (_tn((q_ref[0, rows, _head_lanes(h)] * dec[h][0]).astype(BF16), dob[h])
                             + dec[h][2] * ds_next[h] - _tn(w_ref[h, rows, :].astype(BF16), dv_new[h].astype(BF16)))
            return carry

        lax.fori_loop(0, GROUP, chunk, 0)

    blk = pl.BlockSpec((N_HEADS, rows_per_step, HEAD_DIM), lambda g: (0, n_steps - 1 - g, 0))
    return pl.pallas_call(
        body,
        grid=(n_steps,),
        in_specs=[_token_blocks(0, n_steps), _token_blocks(1, n_steps), _token_blocks(4, n_steps), blk, blk,
                  pl.BlockSpec((rows_per_step, HEAD_W), lambda g: (n_steps - 1 - g, 0))],
        out_specs=[blk, blk],
        out_shape=[jax.ShapeDtypeStruct((N_HEADS, seq, HEAD_DIM), F32)] * 2,
        scratch_shapes=[pltpu.VMEM((N_HEADS, CHUNK, CHUNK), F32)],
        compiler_params=_params(("arbitrary",)),
        name="delta_scan_bwd",
    )(xs, xs, xs, qk_h, w_h, do)


def _delta_chunk_bwd(xs, inv_h, u_h, w_h, st_h, dsn_h, dvn_h, do):
    seq = xs.shape[1]
    rows_per_step = GROUP * CHUNK

    def body(x_ref, inv_ref, u_ref, w_ref, st_ref, dsn_ref, dvn_ref, do_ref, dx_ref):
        causal, strict = _tri(True), _tri(True, strict=True)
        last_row = lax.broadcasted_iota(jnp.int32, (CHUNK, CHUNK), 0) == CHUNK - 1

        def bf(vals):
            return [val.astype(BF16) for val in vals]

        def group(items):
            heads = [hh for hh, _ in items]
            lanes = [slice(hh * HEAD_DIM, (hh + 1) * HEAD_DIM) for hh in heads]
            rows = [slice(step * CHUNK, (step + 1) * CHUNK) for _, step in items]
            n = range(len(items))
            q, k, v, beta, gc = [[x_ref[j, rows[i], lanes[i]] for i in n] for j in range(5)]
            terms = [_chunk_terms(q[i], k[i], v[i], beta[i], gc[i]) for i in n]
            e, f, e_last, decay, kb, a_mat, qk = [[t[j] for t in terms] for j in range(7)]
            inv = [_split(inv_ref[heads[i], rows[i], :]) for i in n]
            u = [u_ref[heads[i], rows[i], :] for i in n]
            w = [w_ref[heads[i], rows[i], :] for i in n]
            s = [st_ref[heads[i], rows[i], :] for i in n]
            ds_next = [dsn_ref[heads[i], rows[i], :] for i in n]
            dv_new = [dvn_ref[heads[i], rows[i], :] for i in n]
            sb, dsb, dvb, wb = bf(s), bf(ds_next), bf(dv_new), bf(w)
            dob = bf([do_ref[rows[i], lanes[i]] for i in n])
            qbf, kbf, kbb = bf(q), bf(k), bf(kb)
            vnb = bf([u[i] - _nn(wb[i], sb[i]) for i in n])
            dqe = [_nt(dob[i], sb[i]) for i in n]
            dw = [-_nt(dvb[i], sb[i]) for i in n]
            dkf = [_nt(vnb[i], dsb[i]) for i in n]
            dqk = [jnp.where(causal, _nt(dob[i], vnb[i]), 0.0) for i in n]
            drhs_u = [_dot3(inv[i], _split(dv_new[i]), _tn) for i in n]
            drhs_w = [_dot3(inv[i], _split(dw[i]), _tn) for i in n]
            da = [-jnp.where(strict, _nt(drhs_u[i].astype(BF16), u[i].astype(BF16))
                             + _nt(drhs_w[i].astype(BF16), wb[i]), 0.0) for i in n]
            dad = bf([da[i] * decay[i] for i in n])
            dqd = bf([dqk[i] * decay[i] for i in n])
            dkb = [e[i] * drhs_w[i] + _nn(dad[i], kbf[i]) for i in n]
            dk = [_tn(dad[i], kbb[i]) + _tn(dqd[i], qbf[i]) + f[i] * dkf[i] + beta[i] * dkb[i] for i in n]
            dq = [_nn(dqd[i], kbf[i]) + e[i] * dqe[i] for i in n]
            for i in n:
                de_full = kb[i] * drhs_w[i] + q[i] * dqe[i]
                df_full = k[i] * dkf[i]
                m = da[i] * a_mat[i] + dqk[i] * qk[i]
                dgc = de_full * e[i] - df_full * f[i] + m - m.T
                tail = jnp.sum(df_full * f[i] + s[i] * ds_next[i] * e_last[i], axis=0, keepdims=True)
                dgc = dgc + jnp.where(last_row, jnp.broadcast_to(tail, (CHUNK, CHUNK)), 0.0)
                dx_ref[0, rows[i], lanes[i]] = dq[i]
                dx_ref[1, rows[i], lanes[i]] = dk[i]
                dx_ref[2, rows[i], lanes[i]] = beta[i] * drhs_u[i]
                dx_ref[3, rows[i], lanes[i]] = v[i] * drhs_u[i] + k[i] * dkb[i]
                dx_ref[4, rows[i], lanes[i]] = dgc

        work = [(hh, step) for hh in range(2) for step in range(GROUP)]
        for first in range(0, len(work), UNROLL):
            group(work[first:first + UNROLL])

    blk = pl.BlockSpec((2, rows_per_step, HEAD_DIM), lambda p, g: (p, g, 0))
    return pl.pallas_call(
        body,
        grid=(N_HEADS // 2, seq // rows_per_step),
        in_specs=[_pair_specs(5)] + [blk] * 6 + [pl.BlockSpec((rows_per_step, LANES), lambda p, g: (g, p))],
        out_specs=_pair_specs(5),
        out_shape=jax.ShapeDtypeStruct((5, seq, HEAD_W), F32),
        compiler_params=_params(("parallel", "parallel")),
        name="delta_chunk_bwd",
    )(xs, inv_h, u_h, w_h, st_h, dsn_h, dvn_h, do)


def _delta_post_fwd(o, qkvz, gain_row):
    seq = o.shape[0]

    def body(o_ref, z_ref, g_ref, y_ref):
        ov = o_ref[...]
        rb = lax.rsqrt(_pick(ov * ov, _head_sum_matrix()) * (1.0 / HEAD_DIM) + EPS)
        y_ref[...] = (ov * rb * g_ref[...] * _silu(z_ref[...])).astype(y_ref.dtype)

    tile = pl.BlockSpec((ROW_TILE, HEAD_W), lambda i: (i, 0))
    return pl.pallas_call(
        body,
        grid=(seq // ROW_TILE,),
        in_specs=[tile, pl.BlockSpec((ROW_TILE, HEAD_W), lambda i: (i, 3)), pl.BlockSpec((1, HEAD_W), lambda i: (0, 0))],
        out_specs=tile,
        out_shape=jax.ShapeDtypeStruct((seq, HEAD_W), BF16),
        compiler_params=_params(("arbitrary",)),
        name="delta_post_fwd",
    )(o, qkvz, gain_row)


def _delta_post_bwd(dy, o, qkvz, gain_row):
    seq = o.shape[0]

    def body(dy_ref, o_ref, z_ref, g_ref, do_ref, dz_ref, dg_ref):
        @pl.when(pl.program_id(0) == 0)
        def _():
            dg_ref[...] = jnp.zeros_like(dg_ref)

        ov, zv, dyv, gain = o_ref[...], z_ref[...], dy_ref[...], g_ref[...]
        hsum = _head_sum_matrix()
        rb = lax.rsqrt(_pick(ov * ov, hsum) * (1.0 / HEAD_DIM) + EPS)
        ohat = ov * rb
        silu_z, slope_z = _silu_and_slope(zv)
        dz_ref[...] = dyv * ohat * gain * slope_z
        dn = dyv * silu_z
        dg_ref[0:1, :] += jnp.sum(dn * ohat, axis=0, keepdims=True)
        dohat = dn * gain

        @pl.when(pl.program_id(0) == pl.num_programs(0) - 1)
        def _():
            fold = (lax.broadcasted_iota(jnp.int32, (HEAD_W, HEAD_W), 0) % HEAD_DIM
                    == lax.broadcasted_iota(jnp.int32, (HEAD_W, HEAD_W), 1)).astype(F32)
            dg_ref[1:2, :] = _pick(dg_ref[0:1, :], fold)

        proj = _pick(dohat * ohat, hsum) * (1.0 / HEAD_DIM)
        do_ref[...] = rb * (dohat - ohat * proj)

    tile = pl.BlockSpec((ROW_TILE, HEAD_W), lambda i: (i, 0))
    return pl.pallas_call(
        body,
        grid=(seq // ROW_TILE,),
        in_specs=[pl.BlockSpec((ROW_TILE, HEAD_W), lambda i: (i, 1)), tile,
                  pl.BlockSpec((ROW_TILE, HEAD_W), lambda i: (i, 3)), pl.BlockSpec((1, HEAD_W), lambda i: (0, 0))],
        out_specs=[tile, tile, pl.BlockSpec((2, HEAD_W), lambda i: (0, 0))],
        out_shape=[jax.ShapeDtypeStruct((seq, HEAD_W), F32), jax.ShapeDtypeStruct((seq, HEAD_W), F32),
                   jax.ShapeDtypeStruct((2, HEAD_W), F32)],
        compiler_params=_params(("arbitrary",)),
        name="delta_post_bwd",
    )(dy, o, qkvz, gain_row)


def _delta_prep_bwd(qkvz, ba, conv_w, alog_row, dt_row, dxs):
    seq = qkvz.shape[0]
    qkv_w = 3 * HEAD_W

    def body(x_ref, xp_ref, ba_ref, w_ref, al_ref, dt_ref, dx_ref, dconv_ref, dba_ref, dvec_ref, stage):
        i = pl.program_id(0)

        @pl.when(i == 0)
        def _():
            dvec_ref[...] = jnp.zeros_like(dvec_ref)

        _stage_rows(stage, x_ref, xp_ref, i)
        pre = _conv_taps(stage, w_ref, ROW_TILE)
        act, slope = _silu_and_slope(pre)
        hsum = _head_sum_matrix()
        for j, scale in ((0, HEAD_DIM ** -0.5), (1, 1.0)):
            cols = slice(j * HEAD_W, (j + 1) * HEAD_W)
            xc = act[:, cols]
            rb = _l2_scale(xc, hsum)
            xhat = xc * rb
            dhat = dx_ref[j] * scale
            proj = _pick(dhat * xhat, hsum)
            dconv_ref[:, cols] = rb * (dhat - xhat * proj) * slope[:, cols]
        dconv_ref[:, 2 * HEAD_W:] = dx_ref[2] * slope[:, 2 * HEAD_W:]

        bav = ba_ref[...]
        beta8 = _sigmoid(bav)
        dbeta8 = _pick(dx_ref[3], _head_gather_matrix())
        dgc8 = _pick(dx_ref[4], _head_gather_matrix(N_HEADS))
        rev = _tri(False).astype(F32)
        z = bav + dt_ref[...]
        ea = jnp.exp(al_ref[...])
        g8 = -ea * _softplus(z)
        sig = _sigmoid(z)
        d_alog = jnp.zeros((1, LANES), F32)
        d_dt = jnp.zeros((1, LANES), F32)
        for ch in range(ROW_TILE // CHUNK):
            rows = slice(ch * CHUNK, (ch + 1) * CHUNK)
            dg8 = _pick_left(rev, dgc8[rows])
            da = -dg8 * ea * sig[rows]
            dba_ref[rows, :] = dbeta8[rows] * beta8[rows] * (1.0 - beta8[rows]) + da
            d_alog = d_alog + jnp.sum(dg8 * g8[rows], axis=0, keepdims=True)
            d_dt = d_dt + jnp.sum(da, axis=0, keepdims=True)
        dvec_ref[0:1, :] += d_alog
        dvec_ref[1:2, :] += d_dt

    return pl.pallas_call(
        body,
        grid=(seq // ROW_TILE,),
        in_specs=[
            pl.BlockSpec((ROW_TILE, qkv_w), lambda i: (i, 0)),
            pl.BlockSpec((8, qkv_w), lambda i: (jnp.maximum(i * (ROW_TILE // 8) - 1, 0), 0)),
            pl.BlockSpec((ROW_TILE, LANES), lambda i: (i, 0)),
            pl.BlockSpec((4, qkv_w), lambda i: (0, 0)),
            pl.BlockSpec((1, LANES), lambda i: (0, 0)),
            pl.BlockSpec((1, LANES), lambda i: (0, 0)),
            pl.BlockSpec((5, ROW_TILE, HEAD_W), lambda i: (0, i, 0)),
        ],
        out_specs=[pl.BlockSpec((ROW_TILE, qkv_w), lambda i: (i, 0)),
                   pl.BlockSpec((ROW_TILE, LANES), lambda i: (i, 0)),
                   pl.BlockSpec((2, LANES), lambda i: (0, 0))],
        out_shape=[jax.ShapeDtypeStruct((seq, qkv_w), F32), jax.ShapeDtypeStruct((seq, LANES), F32),
                   jax.ShapeDtypeStruct((2, LANES), F32)],
        scratch_shapes=[pltpu.VMEM((ROW_TILE + 8, qkv_w), F32)],
        compiler_params=_params(("arbitrary",)),
        name="delta_prep_bwd",
    )(qkvz, qkvz, ba, conv_w, alog_row, dt_row, dxs)


def _conv_bwd(dconv, qkvz, conv_w, h):
    seq = dconv.shape[0]
    qkv_w = 3 * HEAD_W
    n_tiles = seq // ROW_TILE

    def body(dy_ref, dyn_ref, x_ref, xp_ref, w_ref, h_ref, dx_ref, dw_ref, dwin_ref, stage, dstage):
        i = pl.program_id(0)

        @pl.when(i == 0)
        def _():
            dw_ref[...] = jnp.zeros_like(dw_ref)
            dwin_ref[...] = jnp.zeros_like(dwin_ref)

        _stage_rows(stage, x_ref, xp_ref, i)
        dstage[0:ROW_TILE, :] = dy_ref[...]
        dstage[ROW_TILE:ROW_TILE + 8, :] = jnp.where(i == n_tiles - 1, 0.0, dyn_ref[...])
        dy = dy_ref[...]
        dx = (w_ref[3:4, :] * dy + w_ref[2:3, :] * dstage[1:1 + ROW_TILE, :]
              + w_ref[1:2, :] * dstage[2:2 + ROW_TILE, :] + w_ref[0:1, :] * dstage[3:3 + ROW_TILE, :])
        dx_ref[...] = dx
        dwin_ref[...] += _tn(dx.astype(BF16), h_ref[...])
        for j in range(4):
            dw_ref[j:j + 1, :] += jnp.sum(dy * stage[5 + j:5 + j + ROW_TILE, :], axis=0, keepdims=True)

    tile = pl.BlockSpec((ROW_TILE, qkv_w), lambda i: (i, 0))
    return pl.pallas_call(
        body,
        grid=(n_tiles,),
        in_specs=[
            tile,
            pl.BlockSpec((8, qkv_w), lambda i: (jnp.minimum((i + 1) * (ROW_TILE // 8), seq // 8 - 1), 0)),
            tile,
            pl.BlockSpec((8, qkv_w), lambda i: (jnp.maximum(i * (ROW_TILE // 8) - 1, 0), 0)),
            pl.BlockSpec((4, qkv_w), lambda i: (0, 0)),
            pl.BlockSpec((ROW_TILE, D_MODEL), lambda i: (i, 0)),
        ],
        out_specs=[tile, pl.BlockSpec((4, qkv_w), lambda i: (0, 0)), pl.BlockSpec((qkv_w, D_MODEL), lambda i: (0, 0))],
        out_shape=[jax.ShapeDtypeStruct((seq, qkv_w), F32), jax.ShapeDtypeStruct((4, qkv_w), F32),
                   jax.ShapeDtypeStruct((qkv_w, D_MODEL), F32)],
        scratch_shapes=[pltpu.VMEM((ROW_TILE + 8, qkv_w), F32), pltpu.VMEM((ROW_TILE + 8, qkv_w), F32)],
        compiler_params=_params(("arbitrary",)),
        name="conv_bwd",
    )(dconv, dconv, qkvz, qkvz, conv_w, h)


FF_TILE = 1408
WGRAD_ROWS = 1024


def _row(a):
    return pl.BlockSpec((1, a), lambda *_: (0, 0))


def _rms_fwd(xv, gain):
    rstd = lax.rsqrt(jnp.mean(xv * xv, axis=-1, keepdims=True) + EPS)
    xhat = xv * rstd
    return xhat, rstd, xhat * gain


def _rms_bwd(dnorm, xhat, rstd, gain):
    dxhat = dnorm * gain
    dx = rstd * (dxhat - xhat * jnp.mean(dxhat * xhat, axis=-1, keepdims=True))
    return dx, jnp.sum(dnorm * xhat, axis=0, keepdims=True)


IN_SPLITS = (0, 3 * HEAD_W, 7 * HEAD_W, 7 * HEAD_W + LANES)


def _inproj_fwd(x, gain, scale, shift, w_rows):
    seq = x.shape[0]

    def body(x_ref, g_ref, sc_ref, sh_ref, w_ref, h_ref, a_ref, d_ref, b_ref):
        _, _, norm = _rms_fwd(x_ref[...], g_ref[...])
        h = (norm * (1.0 + sc_ref[...]) + sh_ref[...]).astype(BF16)
        h_ref[...] = h
        for out_ref, lo, hi in zip((a_ref, d_ref, b_ref), IN_SPLITS[:-1], IN_SPLITS[1:]):
            out_ref[...] = _nt(h, w_ref[lo:hi, :])

    def rows(width):
        return pl.BlockSpec((ROW_TILE, width), lambda i: (i, 0))

    return pl.pallas_call(
        body,
        grid=(seq // ROW_TILE,),
        in_specs=[rows(D_MODEL), _row(D_MODEL), _row(D_MODEL), _row(D_MODEL),
                  pl.BlockSpec(w_rows.shape, lambda i: (0, 0))],
        out_specs=[rows(D_MODEL), rows(3 * HEAD_W), rows(4 * HEAD_W), rows(LANES)],
        out_shape=[jax.ShapeDtypeStruct((seq, D_MODEL), BF16), jax.ShapeDtypeStruct((seq, 3 * HEAD_W), F32),
                   jax.ShapeDtypeStruct((seq, 4 * HEAD_W), F32), jax.ShapeDtypeStruct((seq, LANES), F32)],
        compiler_params=_params(("arbitrary",)),
        name="inproj_fwd",
    )(x, gain, scale, shift, w_rows)


def _outproj_fwd(y_attn, y_delta, w_out, x, gate1, gain, scale, shift):
    seq = x.shape[0]

    def body(ya_ref, yd_ref, wa_ref, wd_ref, x_ref, g1_ref, g_ref, sc_ref, sh_ref, x1_ref, h_ref, y_ref):
        y = _nn(ya_ref[...].astype(BF16), wa_ref[...]) + _nn(yd_ref[...], wd_ref[...])
        x1 = x_ref[...] + g1_ref[...] * y
        _, _, norm = _rms_fwd(x1, g_ref[...])
        x1_ref[...] = x1
        h_ref[...] = (norm * (1.0 + sc_ref[...]) + sh_ref[...]).astype(BF16)
        y_ref[...] = y.astype(BF16)

    def rows(width):
        return pl.BlockSpec((ROW_TILE, width), lambda i: (i, 0))

    return pl.pallas_call(
        body,
        grid=(seq // ROW_TILE,),
        in_specs=[rows(HEAD_W), rows(HEAD_W),
                  pl.BlockSpec((HEAD_W, D_MODEL), lambda i: (0, 0)), pl.BlockSpec((HEAD_W, D_MODEL), lambda i: (1, 0)),
                  rows(D_MODEL), _row(D_MODEL), _row(D_MODEL), _row(D_MODEL), _row(D_MODEL)],
        out_specs=[rows(D_MODEL), rows(D_MODEL), rows(D_MODEL)],
        out_shape=[jax.ShapeDtypeStruct((seq, D_MODEL), F32), jax.ShapeDtypeStruct((seq, D_MODEL), BF16),
                   jax.ShapeDtypeStruct((seq, D_MODEL), BF16)],
        compiler_params=_params(("arbitrary",)),
        name="outproj_fwd",
    )(y_attn, y_delta, w_out, w_out, x, gate1, gain, scale, shift)


def _ffn_fwd(h2, w_gate, w_up, w_down, x1, gate2, final_gain, target):
    seq = h2.shape[0]
    n_rows, n_ff = seq // ROW_TILE, D_FF // FF_TILE

    def body(h_ref, wg_ref, wu_ref, wd_ref, x1_ref, g2_ref, gf_ref, t_ref, gate_ref, up_ref, dx2_ref, st_ref, acc):
        i, j = pl.program_id(0), pl.program_id(1)

        @pl.when((i == 0) & (j == 0))
        def _():
            st_ref[...] = jnp.zeros_like(st_ref)

        h = h_ref[...]
        gate = _nt(h, wg_ref[...])
        up = _nt(h, wu_ref[...])
        gate_ref[...] = gate.astype(BF16)
        up_ref[...] = up.astype(BF16)
        part = _nn((_silu(gate) * up).astype(BF16), wd_ref[...])

        @pl.when(j == 0)
        def _():
            acc[...] = part

        @pl.when(j > 0)
        def _():
            acc[...] += part

        @pl.when(j == n_ff - 1)
        def _():
            y2 = acc[...]
            x2 = x1_ref[...] + g2_ref[...] * y2
            xhat, rstd, out = _rms_fwd(x2, gf_ref[...])
            diff = out - t_ref[...]
            dx2, dgain = _rms_bwd(diff * (1.0 / D_MODEL), xhat, rstd, gf_ref[...])
            dx2_ref[...] = dx2
            st_ref[0:1, :] += dgain
            st_ref[1:2, :] += jnp.sum(dx2 * y2, axis=0, keepdims=True)
            st_ref[2:3, :] += jnp.sum(diff * diff, axis=0, keepdims=True) * (0.5 / D_MODEL)

        @pl.when((i == n_rows - 1) & (j == n_ff - 1))
        def _():
            st_ref[3:4, :] = jnp.broadcast_to(jnp.sum(st_ref[2:3, :], keepdims=True), (1, D_MODEL))

    def rows(width):
        return pl.BlockSpec((ROW_TILE, width), lambda i, j: (i, 0))

    ff = pl.BlockSpec((ROW_TILE, FF_TILE), lambda i, j: (i, j))
    return pl.pallas_call(
        body,
        grid=(n_rows, n_ff),
        in_specs=[rows(D_MODEL),
                  pl.BlockSpec((FF_TILE, D_MODEL), lambda i, j: (j, 0)), pl.BlockSpec((FF_TILE, D_MODEL), lambda i, j: (j, 0)),
                  pl.BlockSpec((FF_TILE, D_MODEL), lambda i, j: (j, 0)),
                  rows(D_MODEL), _row(D_MODEL), _row(D_MODEL), rows(D_MODEL)],
        out_specs=[ff, ff, rows(D_MODEL), pl.BlockSpec((8, D_MODEL), lambda i, j: (0, 0))],
        out_shape=[jax.ShapeDtypeStruct((seq, D_FF), BF16), jax.ShapeDtypeStruct((seq, D_FF), BF16),
                   jax.ShapeDtypeStruct((seq, D_MODEL), F32), jax.ShapeDtypeStruct((8, D_MODEL), F32)],
        scratch_shapes=[pltpu.VMEM((ROW_TILE, D_MODEL), F32)],
        compiler_params=_params(("arbitrary", "arbitrary")),
        name="ffn_fwd",
    )(h2, w_gate, w_up, w_down, x1, gate2, final_gain, target)


def _ffn_bwd(dx2, gate, up, w_gate, w_up, w_down, x1, y, gate2, gate1, gain, scale):
    seq = dx2.shape[0]

    def act_body(dx2_ref, g2_ref, gate_ref, up_ref, wd_ref, dgate_ref, dup_ref, dwd_ref):
        dy2 = (g2_ref[...] * dx2_ref[...]).astype(BF16)
        gate = gate_ref[...].astype(F32)
        up = up_ref[...].astype(F32)
        dact = _nt(dy2, wd_ref[...])
        silu, slope = _silu_and_slope(gate)
        dgate_ref[...] = (dact * up * slope).astype(BF16)
        dup_ref[...] = (dact * silu).astype(BF16)
        part = _tn((silu * up).astype(BF16), dy2)

        @pl.when(pl.program_id(1) == 0)
        def _():
            dwd_ref[...] = part

        @pl.when(pl.program_id(1) > 0)
        def _():
            dwd_ref[...] += part

    ff = pl.BlockSpec((ROW_TILE, FF_TILE), lambda j, i: (i, j))
    w_tile = pl.BlockSpec((FF_TILE, D_MODEL), lambda j, i: (j, 0))
    dgate, dup, dw_down = pl.pallas_call(
        act_body,
        grid=(D_FF // FF_TILE, seq // ROW_TILE),
        in_specs=[pl.BlockSpec((ROW_TILE, D_MODEL), lambda j, i: (i, 0)), _row(D_MODEL), ff, ff, w_tile],
        out_specs=[ff, ff, w_tile],
        out_shape=[jax.ShapeDtypeStruct((seq, D_FF), BF16)] * 2 + [jax.ShapeDtypeStruct((D_FF, D_MODEL), F32)],
        compiler_params=_params(("arbitrary", "arbitrary")),
        name="ffn_bwd_act",
    )(dx2, gate2, gate, up, w_down)

    def in_body(dgate_ref, dup_ref, wg_ref, wu_ref, dx2_ref, x1_ref, y_ref, g1_ref, g_ref, sc_ref,
                dx1_ref, dy_ref, st_ref):
        @pl.when(pl.program_id(0) == 0)
        def _():
            st_ref[...] = jnp.zeros_like(st_ref)

        dh = _nn(dgate_ref[...], wg_ref[...]) + _nn(dup_ref[...], wu_ref[...])
        xhat, rstd, norm = _rms_fwd(x1_ref[...], g_ref[...])
        dxn, dgain = _rms_bwd(dh * (1.0 + sc_ref[...]), xhat, rstd, g_ref[...])
        dx1 = dx2_ref[...] + dxn
        dx1_ref[...] = dx1
        dy_ref[...] = (g1_ref[...] * dx1).astype(BF16)
        st_ref[0:1, :] += jnp.sum(dh, axis=0, keepdims=True)
        st_ref[1:2, :] += jnp.sum(dh * norm, axis=0, keepdims=True)
        st_ref[2:3, :] += dgain
        st_ref[3:4, :] += jnp.sum(dx1 * y_ref[...].astype(F32), axis=0, keepdims=True)

    half_tile = ROW_TILE // 2

    def rows(width):
        return pl.BlockSpec((half_tile, width), lambda i: (i, 0))

    whole = pl.BlockSpec((D_FF, D_MODEL), lambda i: (0, 0))
    dx1, dy, stats = pl.pallas_call(
        in_body,
        grid=(seq // half_tile,),
        in_specs=[rows(D_FF), rows(D_FF), whole, whole, rows(D_MODEL), rows(D_MODEL), rows(D_MODEL),
                  _row(D_MODEL), _row(D_MODEL), _row(D_MODEL)],
        out_specs=[rows(D_MODEL), rows(D_MODEL), pl.BlockSpec((8, D_MODEL), lambda i: (0, 0))],
        out_shape=[jax.ShapeDtypeStruct((seq, D_MODEL), F32), jax.ShapeDtypeStruct((seq, D_MODEL), BF16),
                   jax.ShapeDtypeStruct((8, D_MODEL), F32)],
        compiler_params=_params(("arbitrary",)),
        name="ffn_bwd_in",
    )(dgate, dup, w_gate, w_up, dx2, x1, y, gate1, gain, scale)
    return dgate, dup, dw_down, dx1, dy, stats


def _outproj_bwd(dy, w_out, y_attn, y_delta):
    seq = dy.shape[0]

    def body(dy_ref, w_ref, ya_ref, yd_ref, out_ref, dw_ref):
        @pl.when(pl.program_id(0) == 0)
        def _():
            dw_ref[...] = jnp.zeros_like(dw_ref)

        dyv = dy_ref[...]
        out_ref[...] = _nt(dyv, w_ref[...])
        dw_ref[0:HEAD_W, :] += _tn(ya_ref[...].astype(BF16), dyv)
        dw_ref[HEAD_W:, :] += _tn(yd_ref[...], dyv)

    rows = pl.BlockSpec((ROW_TILE, D_MODEL), lambda i: (i, 0))
    half = pl.BlockSpec((ROW_TILE, HEAD_W), lambda i: (i, 0))
    whole = pl.BlockSpec((D_MODEL, D_MODEL), lambda i: (0, 0))
    return pl.pallas_call(
        body,
        grid=(seq // ROW_TILE,),
        in_specs=[rows, whole, half, half],
        out_specs=[rows, whole],
        out_shape=[jax.ShapeDtypeStruct((seq, D_MODEL), F32), jax.ShapeDtypeStruct((D_MODEL, D_MODEL), F32)],
        compiler_params=_params(("arbitrary",)),
        name="outproj_bwd",
    )(dy, w_out, y_attn, y_delta)


def _inproj_bwd(dq, dk, dv, dxd, dz, dba, w_rows, x, dx1, gain, scale, partials):
    seq = x.shape[0]
    n = len(partials)
    n_steps = seq // ROW_TILE

    def body(*refs):
        pieces, (w_ref, x_ref, dx1_ref, g_ref, sc_ref) = refs[:6], refs[6:11]
        gx_ref, st_ref = refs[11 + n:13 + n]
        riding = (refs[11:11 + n], refs[13 + n:13 + 2 * n], *refs[13 + 2 * n:])

        @pl.when(pl.program_id(0) == 0)
        def _():
            st_ref[...] = jnp.zeros_like(st_ref)
            for cp in (_scatter_copies(*riding) if n else []):
                cp.start()

        dh = _nn(jnp.concatenate([p[...].astype(BF16) for p in pieces], axis=1), w_ref[...])
        xhat, rstd, norm = _rms_fwd(x_ref[...], g_ref[...])
        dxn, dgain = _rms_bwd(dh * (1.0 + sc_ref[...]), xhat, rstd, g_ref[...])
        gx_ref[...] = dx1_ref[...] + dxn
        st_ref[0:1, :] += jnp.sum(dh, axis=0, keepdims=True)
        st_ref[1:2, :] += jnp.sum(dh * norm, axis=0, keepdims=True)
        st_ref[2:3, :] += dgain

        if n:
            @pl.when(pl.program_id(0) == n_steps - 1)
            def _():
                for cp in _scatter_copies(*riding):
                    cp.wait()

    def rows(width):
        return pl.BlockSpec((ROW_TILE, width), lambda i: (i, 0))

    sems = [pltpu.SemaphoreType.DMA((3 * n,)), pltpu.SemaphoreType.DMA((3 * n,))] if n else []
    return pl.pallas_call(
        body,
        grid=(n_steps,),
        in_specs=[rows(HEAD_W), rows(HEAD_W), rows(HEAD_W), rows(3 * HEAD_W), rows(HEAD_W), rows(LANES),
                  pl.BlockSpec(w_rows.shape, lambda i: (0, 0)), rows(D_MODEL), rows(D_MODEL), _row(D_MODEL),
                  _row(D_MODEL)]
        + [ANY] * n,
        out_specs=[rows(D_MODEL), pl.BlockSpec((8, D_MODEL), lambda i: (0, 0))] + [ANY] * n,
        out_shape=[jax.ShapeDtypeStruct((seq, D_MODEL), F32), jax.ShapeDtypeStruct((8, D_MODEL), F32)]
        + [jax.ShapeDtypeStruct(p.shape, p.dtype) for p in partials],
        scratch_shapes=sems,
        compiler_params=_params(("arbitrary",)),
        name="inproj_bwd",
    )(dq, dk, dv, dxd, dz, dba, w_rows, x, dx1, gain, scale, *partials)


def _weight_grad(a, b, name):
    seq, m = a.shape
    n = b.shape[1]
    tm = m if m <= 1536 else m // 2
    tn = n if n <= 1536 else n // 2
    rows = 2 * WGRAD_ROWS
    n_k = seq // rows

    def body(a_ref, b_ref, out_ref):
        part = _tn(a_ref[...].astype(BF16), b_ref[...].astype(BF16))

        @pl.when(pl.program_id(2) == 0)
        def _():
            out_ref[...] = part

        @pl.when(pl.program_id(2) > 0)
        def _():
            out_ref[...] += part

    return pl.pallas_call(
        body,
        grid=(m // tm, n // tn, n_k),
        in_specs=[pl.BlockSpec((rows, tm), lambda i, j, k: (k, i)),
                  pl.BlockSpec((rows, tn), lambda i, j, k: (k, j))],
        out_specs=pl.BlockSpec((tm, tn), lambda i, j, k: (i, j)),
        out_shape=jax.ShapeDtypeStruct((m, n), F32),
        compiler_params=_params(("arbitrary", "arbitrary", "arbitrary")),
        name=name,
    )(a, b)


def _weight_grad_stack(pieces, b, name):
    seq, n = b.shape
    widths = [a.shape[1] for a in pieces]
    starts = [sum(widths[:i]) for i in range(len(pieces))]

    def body(*refs):
        a_refs, b_ref, out_ref = refs[:len(pieces)], refs[len(pieces)], refs[len(pieces) + 1]

        @pl.when(pl.program_id(0) == 0)
        def _():
            out_ref[...] = jnp.zeros_like(out_ref)

        bb = b_ref[...].astype(BF16)
        for a_ref, start, width in zip(a_refs, starts, widths):
            out_ref[start:start + width, :] += _tn(a_ref[...].astype(BF16), bb)

    def rows(width):
        return pl.BlockSpec((WGRAD_ROWS, width), lambda k: (k, 0))

    return pl.pallas_call(
        body,
        grid=(seq // WGRAD_ROWS,),
        in_specs=[rows(w) for w in widths] + [rows(n)],
        out_specs=pl.BlockSpec((sum(widths), n), lambda k: (0, 0)),
        out_shape=jax.ShapeDtypeStruct((sum(widths), n), F32),
        compiler_params=_params(("arbitrary",)),
        name=name,
    )(*pieces, b)


def _adamw(w, g, m, v, name):
    n_rows, n_cols = w.shape
    if w.size <= 64 * 1024:
        block, grid, index = (n_rows, n_cols), (1,), lambda i: (0, 0)
    elif n_rows % 256 == 0:
        block, grid, index = (256, n_cols), (n_rows // 256,), lambda i: (i, 0)
    elif n_cols % 256 == 0:
        block, grid, index = (n_rows, 256), (n_cols // 256,), lambda i: (0, i)
    else:
        block, grid, index = (n_rows, n_cols), (1,), lambda i: (0, 0)

    def body(w_ref, g_ref, m_ref, v_ref, d_ref, nm_ref, nv_ref):
        gv = g_ref[...]
        nm = ADAM_B1 * m_ref[...] + (1.0 - ADAM_B1) * gv
        nv = ADAM_B2 * v_ref[...] + (1.0 - ADAM_B2) * (gv * gv)
        m_hat = nm / (1.0 - ADAM_B1 ** ADAM_STEP)
        v_hat = nv / (1.0 - ADAM_B2 ** ADAM_STEP)
        d_ref[...] = -ADAM_LR * (m_hat / (jnp.sqrt(v_hat) + ADAM_EPS) + ADAM_WD * w_ref[...])
        nm_ref[...] = nm
        nv_ref[...] = nv

    blk = pl.BlockSpec(block, index)
    shape = jax.ShapeDtypeStruct((n_rows, n_cols), F32)
    return pl.pallas_call(
        body,
        grid=grid,
        in_specs=[blk] * 4,
        out_specs=[blk] * 3,
        out_shape=[shape] * 3,
        compiler_params=_params(("arbitrary",)),
        name=name,
    )(w, g, m, v)


IN_WIDTH = 3600


def _local_step(x, target, mod, norm_attn_g, w_in, rel_bias, conv_w, a_log, dt_bias, delta_norm_g,
                norm_ffn_g, final_norm_g, shards, assemble, reduce_pairs):
    sh1, sc1, g1, sh2, sc2, g2 = [mod[:, i * D_MODEL:(i + 1) * D_MODEL] for i in range(6)]
    w_rows = jnp.pad(w_in, ((0, IN_SPLITS[-1] - IN_WIDTH), (0, 0)))
    tables = jnp.asarray(_attn_tables())
    alog_row = jnp.pad(a_log, ((0, 0), (N_HEADS, LANES - 2 * N_HEADS)))
    dt_row = jnp.pad(dt_bias, ((0, 0), (N_HEADS, LANES - 2 * N_HEADS)))
    gain_row = jnp.tile(delta_norm_g, (1, N_HEADS))

    h1, qkv_a, qkvz, ba = _inproj_fwd(x, norm_attn_g, sc1, sh1, w_rows)
    bias = _attention_bias(rel_bias, tables)
    y_attn, lse, *gathered = _attention_fwd(qkv_a, bias, shards)
    w_out, w_gate, w_up, w_down = assemble(gathered)
    xs = _delta_prep_fwd(qkvz, ba, conv_w, alog_row, dt_row)
    inv_h, qk_h, u_h, w_h = _delta_chunk_fwd(xs)
    o, st_h = _delta_scan_fwd(xs, qk_h, u_h, w_h)
    y_delta = _delta_post_fwd(o, qkvz, gain_row)
    x1, h2, y = _outproj_fwd(y_attn, y_delta, w_out, x, g1, norm_ffn_g, sc2, sh2)
    gate, up, dx2, st_f = _ffn_fwd(h2, w_gate, w_up, w_down, x1, g2, final_norm_g, target)

    dgate, dup, dw_down, dx1, dy, st_b = _ffn_bwd(dx2, gate, up, w_gate, w_up, w_down, x1, y, g2, g1, norm_ffn_g, sc2)
    dycat, dw_out = _outproj_bwd(dy, w_out, y_attn, y_delta)
    partials = reduce_pairs([dw_out, _weight_grad(dgate, h2, "wgrad_gate"), _weight_grad(dup, h2, "wgrad_up"),
                             dw_down], 1, "rest")
    grads = {}
    do, dz, dgain = _delta_post_bwd(dycat, o, qkvz, gain_row)
    dsn_h, dvn_h = _delta_scan_bwd(xs, qk_h, w_h, do)
    dxs = _delta_chunk_bwd(xs, inv_h, u_h, w_h, st_h, dsn_h, dvn_h, do)
    dconv, dba, dvec = _delta_prep_bwd(qkvz, ba, conv_w, alog_row, dt_row, dxs)
    dxd, grads["conv_w"], dw_in_delta = _conv_bwd(dconv, qkvz, conv_w, h1)
    dq, dk, dv, dbias, *scattered = _attention_bwd(qkv_a, dycat, y_attn, lse, bias, partials)
    partials_in = reduce_pairs([jnp.concatenate(
        [_weight_grad_stack([dq, dk, dv], h1, "wgrad_in_attn"), dw_in_delta,
         _weight_grad_stack([dz, dba], h1, "wgrad_in_gates")[:IN_WIDTH - 6 * HEAD_W]], axis=0)], 0, "in")
    grad_x, st_i, *scattered_in = _inproj_bwd(dq, dk, dv, dxd, dz, dba, w_rows, x, dx1, norm_attn_g, sc1,
                                              partials_in)
    grads["rel_bias"] = _rel_bias_grad(dbias, tables)[:, :N_BUCKETS].T
    grads["a_log"] = dvec[0:1, N_HEADS:2 * N_HEADS]
    grads["dt_bias"] = dvec[1:2, N_HEADS:2 * N_HEADS]
    grads["delta_norm_g"] = dgain[1:2, :HEAD_DIM]
    grads["norm_attn_g"] = st_i[2:3]
    grads["norm_ffn_g"] = st_b[2:3]
    grads["final_norm_g"] = st_f[0:1]
    dmod = jnp.concatenate([st_i[0:1], st_i[1:2], st_b[3:4], st_b[0:1], st_b[1:2], st_f[1:2]], axis=1)
    return st_f[3, 0], grad_x, grads, dmod, (partials_in + partials, scattered_in + scattered)


MESH = pl.DeviceIdType.MESH
OTHER_CHIPS = ((1, 0), (0, 1), (1, 1))
ALL_PEERS = tuple((m >> 2 & 1, m >> 1 & 1, m & 1) for m in range(1, 8))
ANY = pl.BlockSpec(memory_space=pl.ANY)
VMEM_SPEC = pl.BlockSpec(memory_space=pltpu.VMEM)


def _me():
    return lax.axis_index("x"), lax.axis_index("y"), lax.axis_index("c")


def _flip(pos, mask):
    return tuple(1 - p if m else p for p, m in zip(pos, mask))


def _remote(src, dst, send_sems, recv_sems, k, to):
    return pltpu.make_async_remote_copy(src_ref=src, dst_ref=dst, send_sem=send_sems.at[k], recv_sem=recv_sems.at[k],
                                        device_id=to, device_id_type=MESH)


def _ada_exchange(c8, w_ada, b_ada, conv8, shard):
    def body(c_ref, w_ref, b_ref, cv_ref, shard_ref, mod_ref, cact_ref, conv_ref, whole_ref,
             c_all, part_all, send_sems, recv_sems, ride_send, ride_recv):
        x, y, c = me = _me()
        dev = 4 * x + 2 * y + c
        chip = 2 * x + y
        riding = ([shard_ref], [whole_ref], ride_send, ride_recv)
        for cp in _gather_copies(*riding, hand_over=False)[0]:
            cp.start()
        c_all[dev] = c_ref[...]
        conv_ref[chip] = cv_ref[...]
        first = [_remote(c_ref, c_all.at[dev], send_sems, recv_sems, k, _flip(me, mask))
                 for k, mask in enumerate(ALL_PEERS)]
        first += [_remote(cv_ref, conv_ref.at[chip], send_sems, recv_sems, 7 + j, _flip(me, (*mask, 0)))
                  for j, mask in enumerate(OTHER_CHIPS)]
        for cp in first:
            cp.start()
        for cp in first:
            cp.wait()
        row = lax.broadcasted_iota(jnp.int32, (8, D_MODEL), 0)
        c_rows = jnp.zeros((8, D_MODEL), F32)
        for d in range(8):
            c_rows = jnp.where(row == d, c_all[d], c_rows)
        c_act = _silu(c_rows)
        cact_ref[...] = c_act
        part_all[chip] = _nn(c_act, w_ref[...], HIGHEST)
        second = [_remote(part_all.at[chip], part_all.at[chip], send_sems, recv_sems, 10 + j, _flip(me, (*mask, 0)))
                  for j, mask in enumerate(OTHER_CHIPS)]
        for cp in second:
            cp.start()
        for cp in second:
            cp.wait()
        cols = w_ref.shape[1]
        for k in range(4):
            mod_ref[:, k * cols:(k + 1) * cols] = part_all[k] + b_ref[:, k * cols:(k + 1) * cols]
        first, passed = _gather_copies(*riding)
        for cp, fwd in zip(first, passed):
            cp.wait_recv()
            fwd.start()
        for cp in first:
            cp.wait_send()
        for fwd in passed:
            fwd.wait()

    cols = w_ada.shape[1]
    return pl.pallas_call(
        body,
        in_specs=[VMEM_SPEC] * 4 + [ANY],
        out_specs=[VMEM_SPEC] * 3 + [ANY],
        out_shape=[jax.ShapeDtypeStruct((8, 4 * cols), F32), jax.ShapeDtypeStruct((8, D_MODEL), F32),
                   jax.ShapeDtypeStruct((4, 8, conv8.shape[1]), F32)] + _gathered_shapes([shard]),
        scratch_shapes=[pltpu.VMEM((8, 8, D_MODEL), F32), pltpu.VMEM((4, 8, cols), F32),
                        pltpu.SemaphoreType.DMA((13,)), pltpu.SemaphoreType.DMA((13,)),
                        pltpu.SemaphoreType.DMA((6,)), pltpu.SemaphoreType.DMA((6,))],
        compiler_params=pltpu.CompilerParams(vmem_limit_bytes=VMEM_LIMIT),
        name="ada_exchange",
    )(c8, w_ada, b_ada, conv8, shard)


def _gathered_shapes(shards):
    return [jax.ShapeDtypeStruct((4, *s.shape), s.dtype) for s in shards]


def _gather_copies(srcs, dsts, send_sems, recv_sems, hand_over=True):
    x, y, c = me = _me()
    chip = 2 * x + y
    sibling = _flip(me, (0, 0, 1))
    first, passed = [], []
    for a, (src, dst) in enumerate(zip(srcs, dsts)):
        for j, mask in enumerate(OTHER_CHIPS):
            to = _flip(me, (*mask, 0))
            first.append(_remote(src.at[c], dst.at[chip, c], send_sems, recv_sems, 6 * a + j, to))
            if hand_over:
                landed = dst.at[2 * to[0] + to[1], c]
                passed.append(_remote(landed, landed, send_sems, recv_sems, 6 * a + 3 + j, sibling))
    return first, passed


def _scatter_copies(srcs, dsts, send_sems, recv_sems):
    x, y, c = me = _me()
    chip = 2 * x + y
    copies = []
    for a, (src, dst) in enumerate(zip(srcs, dsts)):
        for j, mask in enumerate(OTHER_CHIPS):
            to = _flip(me, (*mask, 0))
            copies.append(_remote(src.at[2 * to[0] + to[1]], dst.at[chip], send_sems, recv_sems, 3 * a + j, to))
    return copies


def _start_and_wait(copies):
    for cp in copies:
        cp.start()
    for cp in copies:
        cp.wait()


def _swap_halves(grads):
    n = len(grads)

    def body(*refs):
        srcs, got = refs[:n], refs[n:2 * n]
        send_sems, recv_sems = refs[2 * n:]
        x, y, c = me = _me()
        _start_and_wait([_remote(srcs[a].at[:, 1 - c], got[a], send_sems, recv_sems, a, _flip(me, (0, 0, 1)))
                         for a in range(n)])

    return pl.pallas_call(
        body,
        in_specs=[ANY] * n,
        out_specs=[ANY] * n,
        out_shape=[jax.ShapeDtypeStruct((4, g.shape[2], g.shape[3]), g.dtype) for g in grads],
        scratch_shapes=[pltpu.SemaphoreType.DMA((n,)), pltpu.SemaphoreType.DMA((n,))],
        name=f"swap_halves_{n}",
    )(*grads)


def _join_halves(halves):
    n = len(halves)

    def body(*refs):
        srcs, dsts = refs[:n], refs[n:2 * n]
        send_sems, recv_sems = refs[2 * n:]
        x, y, c = me = _me()
        _start_and_wait([_remote(srcs[a], dsts[a].at[c], send_sems, recv_sems, a, _flip(me, (0, 0, 1)))
                         for a in range(n)])

    return pl.pallas_call(
        body,
        in_specs=[ANY] * n,
        out_specs=[ANY] * n,
        out_shape=[jax.ShapeDtypeStruct((2, *h.shape), h.dtype) for h in halves],
        scratch_shapes=[pltpu.SemaphoreType.DMA((n,)), pltpu.SemaphoreType.DMA((n,))],
        name=f"join_halves_{n}",
    )(*halves)


def _gather_small(packed):
    n_rows = packed.shape[0]

    def body(p_ref, all_ref, sum_ref, send_sems, recv_sems):
        x, y, c = me = _me()
        dev = 4 * x + 2 * y + c
        all_ref[dev] = p_ref[...]
        copies = [_remote(p_ref, all_ref.at[dev], send_sems, recv_sems, k, _flip(me, mask))
                  for k, mask in enumerate(ALL_PEERS)]
        for cp in copies:
            cp.start()
        for cp in copies:
            cp.wait()
        total = all_ref[0]
        for d in range(1, 8):
            total = total + all_ref[d]
        sum_ref[...] = total

    return pl.pallas_call(
        body,
        in_specs=[VMEM_SPEC],
        out_specs=[VMEM_SPEC, VMEM_SPEC],
        out_shape=[jax.ShapeDtypeStruct((8, n_rows, LANES), F32), jax.ShapeDtypeStruct((n_rows, LANES), F32)],
        scratch_shapes=[pltpu.SemaphoreType.DMA((7,)), pltpu.SemaphoreType.DMA((7,))],
        name="gather_small",
    )(packed)


def _add_pair(a, b, out_dtype, name):
    def body(a_ref, b_ref, o_ref):
        o_ref[...] = (a_ref[...] + b_ref[...]).astype(o_ref.dtype)

    blk = pl.BlockSpec((1, *a.shape[1:]), lambda i: (i, 0, 0))
    return pl.pallas_call(
        body, grid=(a.shape[0],), in_specs=[blk, blk], out_specs=blk,
        out_shape=jax.ShapeDtypeStruct(a.shape, out_dtype),
        compiler_params=_params(("arbitrary",)), name=name,
    )(a, b)


def _add_slots(a, name):
    def body(a_ref, o_ref):
        total = a_ref[0].astype(F32)
        for k in range(1, 4):
            total = total + a_ref[k].astype(F32)
        o_ref[...] = total

    return pl.pallas_call(
        body, in_specs=[VMEM_SPEC], out_specs=VMEM_SPEC,
        out_shape=jax.ShapeDtypeStruct(a.shape[1:], F32),
        compiler_params=pltpu.CompilerParams(vmem_limit_bytes=VMEM_LIMIT), name=name,
    )(a)


def _ada_weight_grad(c_act, dmod_cols):
    def body(c_ref, d_ref, o_ref):
        o_ref[...] = _tn(c_ref[...], d_ref[...], HIGHEST)

    return pl.pallas_call(
        body, in_specs=[VMEM_SPEC, VMEM_SPEC], out_specs=VMEM_SPEC,
        out_shape=jax.ShapeDtypeStruct((c_act.shape[1], dmod_cols.shape[1]), F32),
        compiler_params=pltpu.CompilerParams(vmem_limit_bytes=VMEM_LIMIT), name="ada_weight_grad",
    )(c_act, dmod_cols)


def kernel(x, c, w_ada, b_ada, norm_attn_g, w_in, rel_bias, conv_w, a_log, dt_bias, delta_norm_g, w_out, norm_ffn_g, w_gate, w_up, w_down, final_norm_g, loss_target, m_w_ada, m_b_ada, m_norm_attn_g, m_w_in, m_rel_bias, m_conv_w, m_a_log, m_dt_bias, m_delta_norm_g, m_w_out, m_norm_ffn_g, m_w_gate, m_w_up, m_w_down, m_final_norm_g, v_w_ada, v_b_ada, v_norm_attn_g, v_w_in, v_rel_bias, v_conv_w, v_a_log, v_dt_bias, v_delta_norm_g, v_w_out, v_norm_ffn_g, v_w_gate, v_w_up, v_w_down, v_final_norm_g):
    xi, yi, ci = _me()
    dev = 4 * xi + 2 * yi + ci
    chip = 2 * xi + yi

    big_names = ("w_in", "w_out", "w_gate", "w_up", "w_down")
    by_cols = (True, False, True, True, False)

    def rows_form(a, cols):
        return jnp.swapaxes(a[0], 0, 1) if cols else a[0]

    def halves_form(w):
        rows, lanes = w.shape
        if (rows // 2) % 16:
            rows, lanes = w.size // LANES, LANES
        return (2, rows // 2, lanes)

    big = [rows_form(w, cols) for w, cols in zip((w_in, w_out, w_gate, w_up, w_down), by_cols)]
    shards = [w.astype(BF16).reshape(halves_form(w)) for w in big]

    def assemble(gathered, first):
        return [lax.dynamic_update_index_in_dim(g, s, chip, 0).reshape(4 * w.shape[0], w.shape[1])
                for g, s, w in zip(gathered, shards[first:], big[first:])]

    def reduce_pairs(grads, first, tag):
        slots = [g.reshape(4, *halves_form(w)) for g, w in zip(grads, big[first:])]
        return [_add_pair(lax.dynamic_index_in_dim(s, ci, 1, keepdims=False), got, BF16, f"add_pair_{tag}{a}")
                for a, (s, got) in enumerate(zip(slots, _swap_halves(slots)))]

    def finish(partials, scattered, first, tag):
        by_source = [lax.dynamic_update_index_in_dim(b, lax.dynamic_index_in_dim(p, chip, 0, keepdims=False), chip, 0)
                     for b, p in zip(scattered, partials)]
        halves = [_add_slots(p, f"add_slots_{tag}{a}") for a, p in enumerate(by_source)]
        joined = [lax.dynamic_update_index_in_dim(j, h, ci, 0) for j, h in zip(_join_halves(halves), halves)]
        return [j.reshape(w.shape) for j, w in zip(joined, big[first:])]

    conv_cols = conv_w.shape[2]
    mod_all, c_act, conv_all, gathered_in = _ada_exchange(
        jnp.broadcast_to(c, (8, D_MODEL)), w_ada[0], b_ada, jnp.pad(conv_w[0], ((0, 4), (0, 0))), shards[0])
    mod = lax.dynamic_slice_in_dim(mod_all, dev, 1, axis=0)
    conv_full = jnp.swapaxes(conv_all[:, :4, :], 0, 1).reshape(4, 4 * conv_cols)
    whole_in, = assemble([gathered_in], 0)
    loss, grad_x, grads, dmod, (partials, scattered) = _local_step(
        x[0], loss_target[0], mod, norm_attn_g, whole_in, rel_bias, conv_full, a_log, dt_bias, delta_norm_g,
        norm_ffn_g, final_norm_g[None], shards[1:], functools.partial(assemble, first=1), reduce_pairs)

    big_grads = finish(partials, scattered, 0, "all")

    pieces = [dmod, grads["conv_w"], grads["norm_attn_g"], grads["norm_ffn_g"], grads["final_norm_g"],
              grads["rel_bias"], grads["a_log"], grads["dt_bias"], grads["delta_norm_g"]]
    flat = [jnp.pad(p.reshape(-1), (0, -p.size % LANES)) for p in pieces]
    n_rows = [f.size // LANES for f in flat]
    packed = jnp.concatenate(flat).reshape(-1, LANES)
    packed = jnp.pad(packed, ((0, -packed.shape[0] % 8), (0, 0)))
    all_small, total = _gather_small(packed)
    sums, start = [], 0
    for p, n in zip(pieces, n_rows):
        sums.append(total[start:start + n].reshape(-1)[:p.size].reshape(p.shape))
        start += n
    g_b_ada, g_conv, g_norm_attn, g_norm_ffn, g_final, g_rel, g_alog, g_dt, g_dnorm = sums
    dmod_all = all_small[:, :n_rows[0], :].reshape(8, -1)
    ada_cols = w_ada.shape[2]
    g_w_ada = _ada_weight_grad(c_act, lax.dynamic_slice_in_dim(dmod_all, chip * ada_cols, ada_cols, axis=1))
    g_conv = lax.dynamic_slice_in_dim(g_conv, chip * conv_cols, conv_cols, axis=1)

    grad = {"w_ada": g_w_ada[None], "b_ada": g_b_ada, "norm_attn_g": g_norm_attn,
            "rel_bias": g_rel, "conv_w": g_conv[None], "a_log": g_alog, "dt_bias": g_dt, "delta_norm_g": g_dnorm,
            "norm_ffn_g": g_norm_ffn, "final_norm_g": g_final.reshape(-1)}
    weight = {"w_ada": w_ada, "b_ada": b_ada, "norm_attn_g": norm_attn_g, "w_in": w_in, "rel_bias": rel_bias,
              "conv_w": conv_w, "a_log": a_log, "dt_bias": dt_bias, "delta_norm_g": delta_norm_g, "w_out": w_out,
              "norm_ffn_g": norm_ffn_g, "w_gate": w_gate, "w_up": w_up, "w_down": w_down, "final_norm_g": final_norm_g}
    first = {"w_ada": m_w_ada, "b_ada": m_b_ada, "norm_attn_g": m_norm_attn_g, "w_in": m_w_in, "rel_bias": m_rel_bias,
             "conv_w": m_conv_w, "a_log": m_a_log, "dt_bias": m_dt_bias, "delta_norm_g": m_delta_norm_g,
             "w_out": m_w_out, "norm_ffn_g": m_norm_ffn_g, "w_gate": m_w_gate, "w_up": m_w_up, "w_down": m_w_down,
             "final_norm_g": m_final_norm_g}
    second = {"w_ada": v_w_ada, "b_ada": v_b_ada, "norm_attn_g": v_norm_attn_g, "w_in": v_w_in, "rel_bias": v_rel_bias,
              "conv_w": v_conv_w, "a_log": v_a_log, "dt_bias": v_dt_bias, "delta_norm_g": v_delta_norm_g,
              "w_out": v_w_out, "norm_ffn_g": v_norm_ffn_g, "w_gate": v_w_gate, "w_up": v_w_up, "w_down": v_w_down,
              "final_norm_g": v_final_norm_g}
    delta, new_m, new_v = {}, {}, {}
    for name, w in weight.items():
        if name in big_names:
            continue
        two_d = (-1, w.shape[-1])
        d, nm, nv = _adamw(w.reshape(two_d), grad[name].reshape(two_d), first[name].reshape(two_d),
                           second[name].reshape(two_d), f"adamw_{name}")
        delta[name], new_m[name], new_v[name] = d.reshape(w.shape), nm.reshape(w.shape), nv.reshape(w.shape)
    for name, w, g, cols in zip(big_names, big, big_grads, by_cols):
        outs = _adamw(w, g, rows_form(first[name], cols), rows_form(second[name], cols), f"adamw_{name}")
        grad[name], delta[name], new_m[name], new_v[name] = [
            (jnp.swapaxes(o, 0, 1) if cols else o)[None] for o in (g, *outs)]

    names = list(weight)
    return (lax.psum(loss, ("x", "y", "c")), grad_x[None], *[grad[n] for n in names], *[delta[n] for n in names],
            *[new_m[n] for n in names], *[new_v[n] for n in names])
```

```python
import functools
import math

import numpy as np
import jax
import jax.numpy as jnp
from jax import lax
from jax.experimental import pallas as pl
from jax.experimental.pallas import tpu as pltpu

F32 = jnp.float32
BF16 = jnp.bfloat16
HIGHEST = lax.Precision.HIGHEST

D_MODEL = 1024
HEAD_DIM = 64
N_HEADS = 8
HEAD_W = 512
BRANCHES = ((128, 1), (512, 4), (2048, 16))
BAND = 128
ATT_TILE = 2048
ATT_UNROLL = 8
ATT_UNROLL_BWD = 4
N_BUCKETS = 32
MAX_DISTANCE = 2048
CHUNK = 64
D_FF = 2816
EPS = 1e-6
NEG_INF = -1e30
LANES = 128
VMEM_LIMIT = 56 * 1024 * 1024

ADAM_LR = 0.001
ADAM_B1 = 0.9
ADAM_B2 = 0.999
ADAM_EPS = 1e-08
ADAM_WD = 0.01
ADAM_STEP = 10


def _nn(a, b, precision=None):
    return jnp.dot(a, b, preferred_element_type=F32, precision=precision)


def _nt(a, b, precision=None):
    return lax.dot_general(a, b, (((1,), (1,)), ((), ())), preferred_element_type=F32, precision=precision)


def _tn(a, b, precision=None):
    return lax.dot_general(a, b, (((0,), (0,)), ((), ())), preferred_element_type=F32, precision=precision)


def _params(sem, vmem=VMEM_LIMIT):
    return pltpu.CompilerParams(dimension_semantics=sem, vmem_limit_bytes=vmem)


def _sigmoid(x):
    return 0.5 * jnp.tanh(0.5 * x) + 0.5


def _silu_and_slope(x):
    s = _sigmoid(x)
    return x * s, s * (1.0 + x * (1.0 - s))


def _silu(x):
    return x * _sigmoid(x)


def _attn_tables():
    qi = np.arange(BAND)[:, None]
    kj = np.arange(2 * BAND)[None, :]
    steps = qi + BAND - kj
    in_window = (steps >= 0) & (steps <= BAND)
    max_exact = N_BUCKETS // 2
    out = np.zeros((3, 2, BAND, 2 * BAND), np.int32)
    for b, (_, dil) in enumerate(BRANCHES):
        dist = np.maximum(steps, 0) * dil
        dist_f = np.maximum(dist, 1).astype(np.float32)
        large = max_exact + (np.log(dist_f / np.float32(max_exact)) / np.float32(math.log(MAX_DISTANCE / max_exact))
                             * np.float32(N_BUCKETS - max_exact)).astype(np.int32)
        bucket = np.where(dist < max_exact, dist, np.minimum(large, N_BUCKETS - 1)).astype(np.int32)
        out[b, 0] = np.where(in_window, bucket, -1)
        out[b, 1] = np.where(in_window & (kj >= BAND), bucket, -1)
    return out


def _attention_bias(rel_bias, tables):
    def body(rel_ref, tab_ref, out_ref):
        head = pl.program_id(0)
        for b in range(3):
            tab = tab_ref[b, 0]

            def pick(kk, acc, tab=tab):
                return jnp.where(tab == kk, rel_ref[kk, head], acc)

            acc = lax.fori_loop(0, N_BUCKETS, pick, jnp.zeros((BAND, 2 * BAND), F32))
            for first in range(2):
                out_ref[0, b, first] = jnp.where(tab_ref[b, first] < 0, NEG_INF, acc)

    return pl.pallas_call(
        body,
        grid=(N_HEADS,),
        in_specs=[pl.BlockSpec(memory_space=pltpu.SMEM),
                  pl.BlockSpec((3, 2, BAND, 2 * BAND), lambda h: (0, 0, 0, 0))],
        out_specs=pl.BlockSpec((1, 3, 2, BAND, 2 * BAND), lambda h: (h, 0, 0, 0, 0)),
        out_shape=jax.ShapeDtypeStruct((N_HEADS, 3, 2, BAND, 2 * BAND), F32),
        compiler_params=_params(("arbitrary",)),
        name="attn_bias",
    )(rel_bias, tables)


def _bias_spec():
    return pl.BlockSpec((2, 3, 2, BAND, 2 * BAND), lambda p, t: (p, 0, 0, 0, 0))


def _attn_block_index(idx, t, r):
    nb = ATT_TILE // (BAND * r)
    rho = idx // nb
    n = idx % nb
    qs = rho + r * BAND * n
    gs = t * ATT_TILE + qs
    first = (t * nb + n) == 0
    ps = jnp.where(first, gs, gs - r * BAND)
    return qs, gs, ps, first.astype(jnp.int32)


def _rows(start, r):
    return pl.ds(start, BAND) if r == 1 else pl.ds(start, BAND, stride=r)


def _attention_fwd(qkv, bias, shards):
    seq = qkv.shape[0]
    n_tiles = seq // ATT_TILE
    n = len(shards)

    def body(*refs):
        bias_ref, q_ref, k_ref, v_ref = refs[:4]
        y_ref, lse_ref = refs[4 + n:6 + n]
        o_s, l_s = refs[6 + 2 * n:8 + 2 * n]
        riding = (refs[4:4 + n], refs[6 + n:6 + 2 * n], *refs[8 + 2 * n:])
        pair = pl.program_id(0)
        t = pl.program_id(1)
        if n:
            @pl.when((pair == 0) & (t == 0))
            def _():
                for cp in _gather_copies(*riding, hand_over=False)[0]:
                    cp.start()

            @pl.when((pair == 2) & (t == 0))
            def _():
                for cp, fwd in zip(*_gather_copies(*riding)):
                    cp.wait_recv()
                    fwd.start()

        lane = lax.broadcasted_iota(jnp.int32, (1, LANES), 1)
        head0 = lane < HEAD_DIM
        masks = (head0, jnp.logical_not(head0))
        ones = jnp.ones((2 * BAND, LANES), BF16)
        for b, (_, r) in enumerate(BRANCHES):
            def blocks(it, carry, b=b, r=r):
                idx = [_attn_block_index(it * ATT_UNROLL + j, t, r) for j in range(ATT_UNROLL)]
                qb = [q_ref[_rows(qs, r), :] * (HEAD_DIM ** -0.5) for qs, _, _, _ in idx]
                kcat = [jnp.concatenate([k_ref[_rows(ps, r), :], k_ref[_rows(gs, r), :]], axis=0).astype(BF16)
                        for _, gs, ps, _ in idx]
                vcat = [jnp.concatenate([v_ref[_rows(ps, r), :], v_ref[_rows(gs, r), :]], axis=0).astype(BF16)
                        for _, gs, ps, _ in idx]
                work = [(j, hh) for j in range(ATT_UNROLL) for hh in range(2)]
                s = [_nt(jnp.where(masks[hh], qb[j], 0.0).astype(BF16), kcat[j]) + bias_ref[hh, b, idx[j][3]]
                     for j, hh in work]
                m = [jnp.max(sv, axis=-1, keepdims=True) for sv in s]
                e = [jnp.exp(sv - mv) for sv, mv in zip(s, m)]
                eb = [ev.astype(BF16) for ev in e]
                den = [_nn(ev, ones) for ev in eb]
                out = [_nn(ev, vcat[j]) / dv for ev, dv, (j, _) in zip(eb, den, work)]
                lse = [mv + jnp.log(dv) for mv, dv in zip(m, den)]
                for j in range(ATT_UNROLL):
                    o_s[b, _rows(idx[j][0], r), :] = jnp.where(head0, out[2 * j], out[2 * j + 1])
                    l_s[b, _rows(idx[j][0], r), :] = jnp.where(head0, lse[2 * j], lse[2 * j + 1])
                return carry

            lax.fori_loop(0, ATT_TILE // BAND // ATT_UNROLL, blocks, 0)

        def merge(i, carry):
            rows = pl.ds(pl.multiple_of(i * BAND, BAND), BAND)
            l0, l1, l2 = l_s[0, rows, :], l_s[1, rows, :], l_s[2, rows, :]
            m = jnp.maximum(jnp.maximum(l0, l1), l2)
            w0, w1, w2 = jnp.exp(l0 - m), jnp.exp(l1 - m), jnp.exp(l2 - m)
            tot = w0 + w1 + w2
            y_ref[rows, :] = (w0 * o_s[0, rows, :] + w1 * o_s[1, rows, :] + w2 * o_s[2, rows, :]) / tot
            lse_ref[rows, :] = m + jnp.log(tot)
            return carry

        lax.fori_loop(0, ATT_TILE // BAND, merge, 0)

        if n:
            @pl.when((pair == N_HEADS // 2 - 1) & (t == n_tiles - 1))
            def _():
                first, passed = _gather_copies(*riding)
                for cp in first:
                    cp.wait_send()
                for fwd in passed:
                    fwd.wait()

    tile = pl.BlockSpec((ATT_TILE, LANES), lambda p, t: (t, p))
    sems = [pltpu.SemaphoreType.DMA((6 * n,)), pltpu.SemaphoreType.DMA((6 * n,))] if n else []
    return pl.pallas_call(
        body,
        grid=(N_HEADS // 2, n_tiles),
        in_specs=[
            _bias_spec(),
            pl.BlockSpec((ATT_TILE, LANES), lambda p, t: (t, p)),
            pl.BlockSpec((seq, LANES), lambda p, t: (0, 4 + p)),
            pl.BlockSpec((seq, LANES), lambda p, t: (0, 8 + p)),
        ] + [ANY] * n,
        out_specs=[tile, tile] + [ANY] * n,
        out_shape=[jax.ShapeDtypeStruct((seq, HEAD_W), F32), jax.ShapeDtypeStruct((seq, HEAD_W), F32)]
        + _gathered_shapes(shards),
        scratch_shapes=[
            pltpu.VMEM((3, ATT_TILE, LANES), F32),
            pltpu.VMEM((3, ATT_TILE, LANES), F32),
        ] + sems,
        compiler_params=_params(("arbitrary", "arbitrary")),
        name="attn_fwd",
    )(bias, qkv, qkv, qkv, *shards)


def _attention_bwd(qkv, dy, y, lse, bias, partials):
    seq = qkv.shape[0]
    n_tiles = seq // ATT_TILE
    n = len(partials)

    def body(*refs):
        bias_ref, q_ref, k_ref, v_ref, dy_ref, y_ref, lse_ref = refs[:7]
        dq_ref, dk_ref, dv_ref, dbias_ref = refs[7 + n:11 + n]
        riding = (refs[7:7 + n], refs[11 + n:11 + 2 * n], *refs[11 + 2 * n:])
        pair = pl.program_id(0)
        t = pl.program_id(1)
        if n:
            @pl.when((pair == 0) & (t == 0))
            def _():
                for cp in _scatter_copies(*riding):
                    cp.start()

        lane = lax.broadcasted_iota(jnp.int32, (1, LANES), 1)
        head0 = lane < HEAD_DIM

        @pl.when(t == 0)
        def _():
            dk_ref[...] = jnp.zeros_like(dk_ref)
            dv_ref[...] = jnp.zeros_like(dv_ref)
            dbias_ref[...] = jnp.zeros_like(dbias_ref)

        dq_ref[...] = jnp.zeros_like(dq_ref)

        masks = (head0, jnp.logical_not(head0))
        ones = jnp.ones((LANES, LANES), BF16)
        scale = HEAD_DIM ** -0.5
        for b, (_, r) in enumerate(BRANCHES):
            def blocks(it, carry, b=b, r=r):
                idx = [_attn_block_index(it * ATT_UNROLL_BWD + j, t, r) for j in range(ATT_UNROLL_BWD)]
                qb = [q_ref[_rows(qs, r), :] * scale for qs, _, _, _ in idx]
                kcat = [jnp.concatenate([k_ref[_rows(ps, r), :], k_ref[_rows(gs, r), :]], axis=0).astype(BF16)
                        for _, gs, ps, _ in idx]
                vcat = [jnp.concatenate([v_ref[_rows(ps, r), :], v_ref[_rows(gs, r), :]], axis=0).astype(BF16)
                        for _, gs, ps, _ in idx]
                dob = [dy_ref[_rows(qs, r), :] for qs, _, _, _ in idx]
                ob = [y_ref[_rows(qs, r), :] for qs, _, _, _ in idx]
                lb = [lse_ref[_rows(qs, r), :] for qs, _, _, _ in idx]
                work = [(j, hh) for j in range(ATT_UNROLL_BWD) for hh in range(2)]
                qh = [jnp.where(masks[hh], qb[j], 0.0).astype(BF16) for j, hh in work]
                doh = [jnp.where(masks[hh], dob[j], 0.0) for j, hh in work]
                dohb = [d.astype(BF16) for d in doh]
                s = [_nt(qh[w], kcat[j]) + bias_ref[hh, b, idx[j][3]] for w, (j, hh) in enumerate(work)]
                dp = [_nt(dohb[w], vcat[j]) for w, (j, _) in enumerate(work)]
                lrot = [pltpu.roll(lv, HEAD_DIM, 1) for lv in lb]
                lcol = [jnp.where(masks[hh], lb[j], lrot[j]) for j, hh in work]
                parts = [_split(doh[w] * ob[j]) for w, (j, _) in enumerate(work)]
                delta = [_nn(hi, ones) + _nn(lo, ones) for hi, lo in parts]
                prob = [jnp.exp(sv - jnp.concatenate([lv, lv], axis=1)) for sv, lv in zip(s, lcol)]
                ds = [pv * (dv - jnp.concatenate([de, de], axis=1)) for pv, dv, de in zip(prob, dp, delta)]
                dsb = [d.astype(BF16) for d in ds]
                dq = [_nn(dsb[w], kcat[j]) for w, (j, _) in enumerate(work)]
                dkc = [_tn(dsb[w], qh[w]) for w in range(len(work))]
                dvc = [_tn(prob[w].astype(BF16), dohb[w]) for w in range(len(work))]
                for hh in range(2):
                    dbias_ref[0, b, hh] += sum(ds[w] for w, (_, head) in enumerate(work) if head == hh)
                for j in range(ATT_UNROLL_BWD):
                    qs, gs, ps, _ = idx[j]
                    dkcat = dkc[2 * j] + dkc[2 * j + 1]
                    dvcat = dvc[2 * j] + dvc[2 * j + 1]
                    dq_ref[_rows(qs, r), :] += jnp.where(head0, dq[2 * j], dq[2 * j + 1]) * scale
                    dk_ref[_rows(ps, r), :] += dkcat[:BAND]
                    dk_ref[_rows(gs, r), :] += dkcat[BAND:]
                    dv_ref[_rows(ps, r), :] += dvcat[:BAND]
                    dv_ref[_rows(gs, r), :] += dvcat[BAND:]
                return carry

            lax.fori_loop(0, ATT_TILE // BAND // ATT_UNROLL_BWD, blocks, 0)

        if n:
            @pl.when((pair == N_HEADS // 2 - 1) & (t == n_tiles - 1))
            def _():
                for cp in _scatter_copies(*riding):
                    cp.wait()

    tile = pl.BlockSpec((ATT_TILE, LANES), lambda p, t: (t, p))
    full = pl.BlockSpec((seq, LANES), lambda p, t: (0, p))
    sems = [pltpu.SemaphoreType.DMA((3 * n,)), pltpu.SemaphoreType.DMA((3 * n,))] if n else []
    return pl.pallas_call(
        body,
        grid=(N_HEADS // 2, n_tiles),
        in_specs=[
            _bias_spec(),
            pl.BlockSpec((ATT_TILE, LANES), lambda p, t: (t, p)),
            pl.BlockSpec((seq, LANES), lambda p, t: (0, 4 + p)),
            pl.BlockSpec((seq, LANES), lambda p, t: (0, 8 + p)),
            tile, tile, tile,
        ] + [ANY] * n,
        out_specs=[tile, full, full,
                   pl.BlockSpec((1, 3, 2, BAND, 2 * BAND), lambda p, t: (p, 0, 0, 0, 0))] + [ANY] * n,
        out_shape=[jax.ShapeDtypeStruct((seq, HEAD_W), F32)] * 3
        + [jax.ShapeDtypeStruct((N_HEADS // 2, 3, 2, BAND, 2 * BAND), F32)]
        + [jax.ShapeDtypeStruct(p.shape, p.dtype) for p in partials],
        scratch_shapes=sems,
        compiler_params=_params(("arbitrary", "arbitrary")),
        name="attn_bwd",
    )(bias, qkv, qkv, qkv, dy, y, lse, *partials)


def _rel_bias_grad(dbias, tables):
    def body(tab_ref, db_ref, out_ref):
        lane = lax.broadcasted_iota(jnp.int32, (1, LANES), 1)
        out_ref[...] = jnp.zeros_like(out_ref)
        for b in range(3):
            tab = tab_ref[b, 0]

            def head(h, carry, b=b, tab=tab):
                d = db_ref[h // 2, b, h % 2]
                sums = [jnp.sum(jnp.where(tab == kk, d, 0.0), keepdims=True) for kk in range(N_BUCKETS)]
                row = jnp.zeros((1, LANES), F32)
                for kk, s in enumerate(sums):
                    row = row + jnp.where(lane == kk, s, 0.0)
                out_ref[pl.ds(h, 1), :] += row
                return carry

            lax.fori_loop(0, N_HEADS, head, 0)

    return pl.pallas_call(
        body,
        out_shape=jax.ShapeDtypeStruct((N_HEADS, LANES), F32),
        compiler_params=pltpu.CompilerParams(vmem_limit_bytes=VMEM_LIMIT),
        name="rel_bias_grad",
    )(tables, dbias)


ROW_TILE = 512


def _head_sum_matrix():
    return (lax.broadcasted_iota(jnp.int32, (HEAD_W, HEAD_W), 0) // HEAD_DIM
            == lax.broadcasted_iota(jnp.int32, (HEAD_W, HEAD_W), 1) // HEAD_DIM).astype(F32)


def _head_spread_matrix(offset=0):
    return (lax.broadcasted_iota(jnp.int32, (LANES, HEAD_W), 0)
            == lax.broadcasted_iota(jnp.int32, (LANES, HEAD_W), 1) // HEAD_DIM + offset).astype(F32)


def _head_gather_matrix(offset=0):
    return (lax.broadcasted_iota(jnp.int32, (HEAD_W, LANES), 0) // HEAD_DIM + offset
            == lax.broadcasted_iota(jnp.int32, (HEAD_W, LANES), 1)).astype(F32)


def _split3(x):
    hi = x.astype(BF16)
    rest = x - hi.astype(F32)
    mid = rest.astype(BF16)
    return hi, mid, (rest - mid.astype(F32)).astype(BF16)


def _pick(x, onehot):
    m = onehot.astype(BF16)
    hi, mid, lo = _split3(x)
    return _nn(hi, m) + (_nn(mid, m) + _nn(lo, m))


def _pick_left(onehot, x):
    m = onehot.astype(BF16)
    hi, mid, lo = _split3(x)
    return _nn(m, hi) + (_nn(m, mid) + _nn(m, lo))


def _tri(lower, strict=False):
    r = lax.broadcasted_iota(jnp.int32, (CHUNK, CHUNK), 0)
    c = lax.broadcasted_iota(jnp.int32, (CHUNK, CHUNK), 1)
    if lower:
        return (c < r) if strict else (c <= r)
    return c >= r


def _softplus(z):
    return jnp.maximum(z, 0.0) + jnp.log(1.0 + jnp.exp(-jnp.abs(z)))


def _conv_taps(stage, w_ref, rows):
    return (w_ref[3:4, :] * stage[8:8 + rows, :] + w_ref[2:3, :] * stage[7:7 + rows, :]
            + w_ref[1:2, :] * stage[6:6 + rows, :] + w_ref[0:1, :] * stage[5:5 + rows, :])


def _l2_scale(xc, hsum):
    return lax.rsqrt(_pick(xc * xc, hsum) + EPS)


def _stage_rows(stage, x_ref, xp_ref, i):
    stage[0:8, :] = jnp.where(i == 0, 0.0, xp_ref[...])
    stage[8:8 + ROW_TILE, :] = x_ref[...]


def _delta_prep_fwd(qkvz, ba, conv_w, alog_row, dt_row):
    seq = qkvz.shape[0]
    qkv_w = 3 * HEAD_W

    def body(x_ref, xp_ref, ba_ref, w_ref, al_ref, dt_ref, out_ref, stage):
        i = pl.program_id(0)
        _stage_rows(stage, x_ref, xp_ref, i)
        act = _silu(_conv_taps(stage, w_ref, ROW_TILE))
        hsum, hspread = _head_sum_matrix(), _head_spread_matrix()
        qc, kc = act[:, :HEAD_W], act[:, HEAD_W:2 * HEAD_W]
        out_ref[0] = qc * _l2_scale(qc, hsum) * (HEAD_DIM ** -0.5)
        out_ref[1] = kc * _l2_scale(kc, hsum)
        out_ref[2] = act[:, 2 * HEAD_W:]
        bav = ba_ref[...]
        out_ref[3] = _pick(_sigmoid(bav), hspread)
        g8 = -jnp.exp(al_ref[...]) * _softplus(bav + dt_ref[...])
        gb = _pick(g8, _head_spread_matrix(N_HEADS))
        cum = _tri(True).astype(F32)
        for ch in range(ROW_TILE // CHUNK):
            rows = slice(ch * CHUNK, (ch + 1) * CHUNK)
            out_ref[4, rows, :] = _pick_left(cum, gb[rows])

    return pl.pallas_call(
        body,
        grid=(seq // ROW_TILE,),
        in_specs=[
            pl.BlockSpec((ROW_TILE, qkv_w), lambda i: (i, 0)),
            pl.BlockSpec((8, qkv_w), lambda i: (jnp.maximum(i * (ROW_TILE // 8) - 1, 0), 0)),
            pl.BlockSpec((ROW_TILE, LANES), lambda i: (i, 0)),
            pl.BlockSpec((4, qkv_w), lambda i: (0, 0)),
            pl.BlockSpec((1, LANES), lambda i: (0, 0)),
            pl.BlockSpec((1, LANES), lambda i: (0, 0)),
        ],
        out_specs=pl.BlockSpec((5, ROW_TILE, HEAD_W), lambda i: (0, i, 0)),
        out_shape=jax.ShapeDtypeStruct((5, seq, HEAD_W), F32),
        scratch_shapes=[pltpu.VMEM((ROW_TILE + 8, qkv_w), F32)],
        compiler_params=_params(("arbitrary",)),
        name="delta_prep_fwd",
    )(qkvz, qkvz, ba, conv_w, alog_row, dt_row)


def _split(x):
    hi = x.astype(BF16)
    return hi, (x - hi.astype(F32)).astype(BF16)


def _dot3(a, b, dot=_nn):
    return dot(a[0], b[0]) + (dot(a[0], b[1]) + dot(a[1], b[0]))


def _unit_lower_inverses(mats):
    eye = (lax.broadcasted_iota(jnp.int32, (CHUNK, CHUNK), 0)
           == lax.broadcasted_iota(jnp.int32, (CHUNK, CHUNK), 1)).astype(F32)
    invs = [eye - a for a in mats]
    powers = [_split(a) for a in mats]
    for step in range(5):
        squares = [_dot3(p, p) for p in powers]
        powers = [_split(s) for s in squares]
        invs = [inv + _dot3(_split(inv), p) for inv, p in zip(invs, powers)]
    return invs


def _chunk_terms(q, k, v, beta, gc):
    causal, strict = _tri(True), _tri(True, strict=True)
    e = jnp.exp(gc)
    g_last = jnp.broadcast_to(gc[CHUNK - 1:CHUNK, :], (CHUNK, CHUNK))
    f = jnp.exp(g_last - gc)
    e_last = jnp.exp(g_last)
    decay = jnp.where(causal, jnp.exp(jnp.where(causal, gc - gc.T, 0.0)), 0.0)
    kb = k * beta
    a_mat = jnp.where(strict, _nt(kb.astype(BF16), k.astype(BF16)) * decay, 0.0)
    qk = jnp.where(causal, _nt(q.astype(BF16), k.astype(BF16)) * decay, 0.0)
    return e, f, e_last, decay, kb, a_mat, qk


GROUP = 8
UNROLL = 8


def _chunk_rows(ci):
    return pl.ds(pl.multiple_of(ci * CHUNK, CHUNK), CHUNK)


def _pair_specs(n_planes):
    return pl.BlockSpec((n_planes, GROUP * CHUNK, LANES), lambda p, g: (0, g, p))


def _delta_chunk_fwd(xs):
    seq = xs.shape[1]
    rows_per_step = GROUP * CHUNK

    def body(x_ref, inv_ref, qk_ref, u_ref, w_ref):
        work = [(hh, slice(step * CHUNK, (step + 1) * CHUNK)) for hh in range(2) for step in range(GROUP)]
        xh = [[x_ref[j, r, hh * HEAD_DIM:(hh + 1) * HEAD_DIM] for j in range(5)] for hh, r in work]
        terms = [_chunk_terms(*x) for x in xh]
        invs = _unit_lower_inverses([t[5] for t in terms])
        for (hh, r), x, t, inv in zip(work, xh, terms, invs):
            e, kb, qk = t[0], t[4], t[6]
            inv_parts = _split(inv)
            inv_ref[hh, r, :] = inv
            qk_ref[hh, r, :] = qk
            u_ref[hh, r, :] = _dot3(inv_parts, _split(x[2] * x[3]))
            w_ref[hh, r, :] = _dot3(inv_parts, _split(kb * e))

    out = pl.BlockSpec((2, rows_per_step, HEAD_DIM), lambda p, g: (p, g, 0))
    return pl.pallas_call(
        body,
        grid=(N_HEADS // 2, seq // rows_per_step),
        in_specs=[_pair_specs(5)],
        out_specs=[out] * 4,
        out_shape=[jax.ShapeDtypeStruct((N_HEADS, seq, HEAD_DIM), F32)] * 4,
        compiler_params=_params(("parallel", "parallel")),
        name="delta_chunk_fwd",
    )(xs)


def _decays(gc):
    g_last = jnp.broadcast_to(gc[CHUNK - 1:CHUNK, :], (CHUNK, CHUNK))
    return jnp.exp(gc), jnp.exp(g_last - gc), jnp.exp(g_last)


def _token_blocks(index, n_steps=None):
    rows_per_step = GROUP * CHUNK
    if n_steps is None:
        return pl.BlockSpec((1, rows_per_step, HEAD_W), lambda g: (index, g, 0))
    return pl.BlockSpec((1, rows_per_step, HEAD_W), lambda g: (index, n_steps - 1 - g, 0))


def _head_lanes(h):
    return pl.ds(h * HEAD_DIM, HEAD_DIM)


def _delta_scan_fwd(xs, qk_h, u_h, w_h):
    seq = xs.shape[1]
    rows_per_step = GROUP * CHUNK

    def body(q_ref, k_ref, gc_ref, qk_ref, u_ref, w_ref, o_ref, st_ref, state):
        @pl.when(pl.program_id(0) == 0)
        def _():
            state[...] = jnp.zeros_like(state)

        def chunk(ci, carry):
            rows = _chunk_rows(ci)
            heads = range(N_HEADS)
            dec = [_decays(gc_ref[0, rows, _head_lanes(h)]) for h in heads]
            s = [state[h] for h in heads]
            sb = [s[h].astype(BF16) for h in heads]
            vnb = [(u_ref[h, rows, :] - _nn(w_ref[h, rows, :].astype(BF16), sb[h])).astype(BF16) for h in heads]
            for h in heads:
                o_ref[rows, _head_lanes(h)] = (_nn((q_ref[0, rows, _head_lanes(h)] * dec[h][0]).astype(BF16), sb[h])
                                               + _nn(qk_ref[h, rows, :].astype(BF16), vnb[h]))
                st_ref[h, rows, :] = s[h]
            for h in heads:
                state[h] = s[h] * dec[h][2] + _tn((k_ref[0, rows, _head_lanes(h)] * dec[h][1]).astype(BF16), vnb[h])
            return carry

        lax.fori_loop(0, GROUP, chunk, 0)

    blk = pl.BlockSpec((N_HEADS, rows_per_step, HEAD_DIM), lambda g: (0, g, 0))
    return pl.pallas_call(
        body,
        grid=(seq // rows_per_step,),
        in_specs=[_token_blocks(0), _token_blocks(1), _token_blocks(4), blk, blk, blk],
        out_specs=[pl.BlockSpec((rows_per_step, HEAD_W), lambda g: (g, 0)), blk],
        out_shape=[jax.ShapeDtypeStruct((seq, HEAD_W), F32), jax.ShapeDtypeStruct((N_HEADS, seq, HEAD_DIM), F32)],
        scratch_shapes=[pltpu.VMEM((N_HEADS, CHUNK, CHUNK), F32)],
        compiler_params=_params(("arbitrary",)),
        name="delta_scan_fwd",
    )(xs, xs, xs, qk_h, u_h, w_h)


def _delta_scan_bwd(xs, qk_h, w_h, do):
    seq = xs.shape[1]
    rows_per_step = GROUP * CHUNK
    n_steps = seq // rows_per_step

    def body(q_ref, k_ref, gc_ref, qk_ref, w_ref, do_ref, dsn_ref, dvn_ref, dstate):
        @pl.when(pl.program_id(0) == 0)
        def _():
            dstate[...] = jnp.zeros_like(dstate)

        def chunk(step, carry):
            rows = _chunk_rows(GROUP - 1 - step)
            heads = range(N_HEADS)
            dec = [_decays(gc_ref[0, rows, _head_lanes(h)]) for h in heads]
            ds_next = [dstate[h] for h in heads]
            dob = [do_ref[rows, _head_lanes(h)].astype(BF16) for h in heads]
            dv_new = [_tn(qk_ref[h, rows, :].astype(BF16), dob[h])
                      + _nn((k_ref[0, rows, _head_lanes(h)] * dec[h][1]).astype(BF16), ds_next[h].astype(BF16))
                      for h in heads]
            for h in heads:
                dsn_ref[h, rows, :] = ds_next[h]
                dvn_ref[h, rows, :] = dv_new[h]
            for h in heads:
                dstate[h] = (_tn((q_ref[0, rows, _head_lanes(h)] * dec[h][0]).astype(BF16), dob[h])
                             + dec[h][2] * ds_next[h] - _tn(w_ref[h, rows, :].astype(BF16), dv_new[h].astype(BF16)))
            return carry

        lax.fori_loop(0, GROUP, chunk, 0)

    blk = pl.BlockSpec((N_HEADS, rows_per_step, HEAD_DIM), lambda g: (0, n_steps - 1 - g, 0))
    return pl.pallas_call(
        body,
        grid=(n_steps,),
        in_specs=[_token_blocks(0, n_steps), _token_blocks(1, n_steps), _token_blocks(4, n_steps), blk, blk,
                  pl.BlockSpec((rows_per_step, HEAD_W), lambda g: (n_steps - 1 - g, 0))],
        out_specs=[blk, blk],
        out_shape=[jax.ShapeDtypeStruct((N_HEADS, seq, HEAD_DIM), F32)] * 2,
        scratch_shapes=[pltpu.VMEM((N_HEADS, CHUNK, CHUNK), F32)],
        compiler_params=_params(("arbitrary",)),
        name="delta_scan_bwd",
    )(xs, xs, xs, qk_h, w_h, do)


def _delta_chunk_bwd(xs, inv_h, u_h, w_h, st_h, dsn_h, dvn_h, do):
    seq = xs.shape[1]
    rows_per_step = GROUP * CHUNK

    def body(x_ref, inv_ref, u_ref, w_ref, st_ref, dsn_ref, dvn_ref, do_ref, dx_ref):
        causal, strict = _tri(True), _tri(True, strict=True)
        last_row = lax.broadcasted_iota(jnp.int32, (CHUNK, CHUNK), 0) == CHUNK - 1

        def bf(vals):
            return [val.astype(BF16) for val in vals]

        def group(items):
            heads = [hh for hh, _ in items]
            lanes = [slice(hh * HEAD_DIM, (hh + 1) * HEAD_DIM) for hh in heads]
            rows = [slice(step * CHUNK, (step + 1) * CHUNK) for _, step in items]
            n = range(len(items))
            q, k, v, beta, gc = [[x_ref[j, rows[i], lanes[i]] for i in n] for j in range(5)]
            terms = [_chunk_terms(q[i], k[i], v[i], beta[i], gc[i]) for i in n]
            e, f, e_last, decay, kb, a_mat, qk = [[t[j] for t in terms] for j in range(7)]
            inv = [_split(inv_ref[heads[i], rows[i], :]) for i in n]
            u = [u_ref[heads[i], rows[i], :] for i in n]
            w = [w_ref[heads[i], rows[i], :] for i in n]
            s = [st_ref[heads[i], rows[i], :] for i in n]
            ds_next = [dsn_ref[heads[i], rows[i], :] for i in n]
            dv_new = [dvn_ref[heads[i], rows[i], :] for i in n]
            sb, dsb, dvb, wb = bf(s), bf(ds_next), bf(dv_new), bf(w)
            dob = bf([do_ref[rows[i], lanes[i]] for i in n])
            qbf, kbf, kbb = bf(q), bf(k), bf(kb)
            vnb = bf([u[i] - _nn(wb[i], sb[i]) for i in n])
            dqe = [_nt(dob[i], sb[i]) for i in n]
            dw = [-_nt(dvb[i], sb[i]) for i in n]
            dkf = [_nt(vnb[i], dsb[i]) for i in n]
            dqk = [jnp.where(causal, _nt(dob[i], vnb[i]), 0.0) for i in n]
            drhs_u = [_dot3(inv[i], _split(dv_new[i]), _tn) for i in n]
            drhs_w = [_dot3(inv[i], _split(dw[i]), _tn) for i in n]
            da = [-jnp.where(strict, _nt(drhs_u[i].astype(BF16), u[i].astype(BF16))
                             + _nt(drhs_w[i].astype(BF16), wb[i]), 0.0) for i in n]
            dad = bf([da[i] * decay[i] for i in n])
            dqd = bf([dqk[i] * decay[i] for i in n])
            dkb = [e[i] * drhs_w[i] + _nn(dad[i], kbf[i]) for i in n]
            dk = [_tn(dad[i], kbb[i]) + _tn(dqd[i], qbf[i]) + f[i] * dkf[i] + beta[i] * dkb[i] for i in n]
            dq = [_nn(dqd[i], kbf[i]) + e[i] * dqe[i] for i in n]
            for i in n:
                de_full = kb[i] * drhs_w[i] + q[i] * dqe[i]
                df_full = k[i] * dkf[i]
                m = da[i] * a_mat[i] + dqk[i] * qk[i]
                dgc = de_full * e[i] - df_full * f[i] + m - m.T
                tail = jnp.sum(df_full * f[i] + s[i] * ds_next[i] * e_last[i], axis=0, keepdims=True)
                dgc = dgc + jnp.where(last_row, jnp.broadcast_to(tail, (CHUNK, CHUNK)), 0.0)
                dx_ref[0, rows[i], lanes[i]] = dq[i]
                dx_ref[1, rows[i], lanes[i]] = dk[i]
                dx_ref[2, rows[i], lanes[i]] = beta[i] * drhs_u[i]
                dx_ref[3, rows[i], lanes[i]] = v[i] * drhs_u[i] + k[i] * dkb[i]
                dx_ref[4, rows[i], lanes[i]] = dgc

        work = [(hh, step) for hh in range(2) for step in range(GROUP)]
        for first in range(0, len(work), UNROLL):
            group(work[first:first + UNROLL])

    blk = pl.BlockSpec((2, rows_per_step, HEAD_DIM), lambda p, g: (p, g, 0))
    return pl.pallas_call(
        body,
        grid=(N_HEADS // 2, seq // rows_per_step),
        in_specs=[_pair_specs(5)] + [blk] * 6 + [pl.BlockSpec((rows_per_step, LANES), lambda p, g: (g, p))],
        out_specs=_pair_specs(5),
        out_shape=jax.ShapeDtypeStruct((5, seq, HEAD_W), F32),
        compiler_params=_params(("parallel", "parallel")),
        name="delta_chunk_bwd",
    )(xs, inv_h, u_h, w_h, st_h, dsn_h, dvn_h, do)


def _delta_post_fwd(o, qkvz, gain_row):
    seq = o.shape[0]

    def body(o_ref, z_ref, g_ref, y_ref):
        ov = o_ref[...]
        rb = lax.rsqrt(_pick(ov * ov, _head_sum_matrix()) * (1.0 / HEAD_DIM) + EPS)
        y_ref[...] = (ov * rb * g_ref[...] * _silu(z_ref[...])).astype(y_ref.dtype)

    tile = pl.BlockSpec((ROW_TILE, HEAD_W), lambda i: (i, 0))
    return pl.pallas_call(
        body,
        grid=(seq // ROW_TILE,),
        in_specs=[tile, pl.BlockSpec((ROW_TILE, HEAD_W), lambda i: (i, 3)), pl.BlockSpec((1, HEAD_W), lambda i: (0, 0))],
        out_specs=tile,
        out_shape=jax.ShapeDtypeStruct((seq, HEAD_W), BF16),
        compiler_params=_params(("arbitrary",)),
        name="delta_post_fwd",
    )(o, qkvz, gain_row)


def _delta_post_bwd(dy, o, qkvz, gain_row, h):
    seq = o.shape[0]

    def body(dy_ref, o_ref, z_ref, g_ref, h_ref, do_ref, dz_ref, dg_ref, dwin_ref):
        @pl.when(pl.program_id(0) == 0)
        def _():
            dg_ref[...] = jnp.zeros_like(dg_ref)
            dwin_ref[...] = jnp.zeros_like(dwin_ref)

        ov, zv, dyv, gain = o_ref[...], z_ref[...], dy_ref[...], g_ref[...]
        hsum = _head_sum_matrix()
        rb = lax.rsqrt(_pick(ov * ov, hsum) * (1.0 / HEAD_DIM) + EPS)
        ohat = ov * rb
        silu_z, slope_z = _silu_and_slope(zv)
        dz = dyv * ohat * gain * slope_z
        dz_ref[...] = dz
        dwin_ref[...] += _tn(dz.astype(BF16), h_ref[...])
        dn = dyv * silu_z
        dg_ref[0:1, :] += jnp.sum(dn * ohat, axis=0, keepdims=True)
        dohat = dn * gain

        @pl.when(pl.program_id(0) == pl.num_programs(0) - 1)
        def _():
            fold = (lax.broadcasted_iota(jnp.int32, (HEAD_W, HEAD_W), 0) % HEAD_DIM
                    == lax.broadcasted_iota(jnp.int32, (HEAD_W, HEAD_W), 1)).astype(F32)
            dg_ref[1:2, :] = _pick(dg_ref[0:1, :], fold)

        proj = _pick(dohat * ohat, hsum) * (1.0 / HEAD_DIM)
        do_ref[...] = rb * (dohat - ohat * proj)

    tile = pl.BlockSpec((ROW_TILE, HEAD_W), lambda i: (i, 0))
    return pl.pallas_call(
        body,
        grid=(seq // ROW_TILE,),
        in_specs=[pl.BlockSpec((ROW_TILE, HEAD_W), lambda i: (i, 1)), tile,
                  pl.BlockSpec((ROW_TILE, HEAD_W), lambda i: (i, 3)), pl.BlockSpec((1, HEAD_W), lambda i: (0, 0)),
                  pl.BlockSpec((ROW_TILE, D_MODEL), lambda i: (i, 0))],
        out_specs=[tile, tile, pl.BlockSpec((2, HEAD_W), lambda i: (0, 0)),
                   pl.BlockSpec((HEAD_W, D_MODEL), lambda i: (0, 0))],
        out_shape=[jax.ShapeDtypeStruct((seq, HEAD_W), F32), jax.ShapeDtypeStruct((seq, HEAD_W), F32),
                   jax.ShapeDtypeStruct((2, HEAD_W), F32), jax.ShapeDtypeStruct((HEAD_W, D_MODEL), F32)],
        compiler_params=_params(("arbitrary",)),
        name="delta_post_bwd",
    )(dy, o, qkvz, gain_row, h)


def _delta_prep_bwd(qkvz, ba, conv_w, alog_row, dt_row, dxs):
    seq = qkvz.shape[0]
    qkv_w = 3 * HEAD_W

    def body(x_ref, xp_ref, ba_ref, w_ref, al_ref, dt_ref, dx_ref, dconv_ref, dba_ref, dvec_ref, stage):
        i = pl.program_id(0)

        @pl.when(i == 0)
        def _():
            dvec_ref[...] = jnp.zeros_like(dvec_ref)

        _stage_rows(stage, x_ref, xp_ref, i)
        pre = _conv_taps(stage, w_ref, ROW_TILE)
        act, slope = _silu_and_slope(pre)
        hsum = _head_sum_matrix()
        for j, scale in ((0, HEAD_DIM ** -0.5), (1, 1.0)):
            cols = slice(j * HEAD_W, (j + 1) * HEAD_W)
            xc = act[:, cols]
            rb = _l2_scale(xc, hsum)
            xhat = xc * rb
            dhat = dx_ref[j] * scale
            proj = _pick(dhat * xhat, hsum)
            dconv_ref[:, cols] = rb * (dhat - xhat * proj) * slope[:, cols]
        dconv_ref[:, 2 * HEAD_W:] = dx_ref[2] * slope[:, 2 * HEAD_W:]

        bav = ba_ref[...]
        beta8 = _sigmoid(bav)
        dbeta8 = _pick(dx_ref[3], _head_gather_matrix())
        dgc8 = _pick(dx_ref[4], _head_gather_matrix(N_HEADS))
        rev = _tri(False).astype(F32)
        z = bav + dt_ref[...]
        ea = jnp.exp(al_ref[...])
        g8 = -ea * _softplus(z)
        sig = _sigmoid(z)
        d_alog = jnp.zeros((1, LANES), F32)
        d_dt = jnp.zeros((1, LANES), F32)
        for ch in range(ROW_TILE // CHUNK):
            rows = slice(ch * CHUNK, (ch + 1) * CHUNK)
            dg8 = _pick_left(rev, dgc8[rows])
            da = -dg8 * ea * sig[rows]
            dba_ref[rows, :] = dbeta8[rows] * beta8[rows] * (1.0 - beta8[rows]) + da
            d_alog = d_alog + jnp.sum(dg8 * g8[rows], axis=0, keepdims=True)
            d_dt = d_dt + jnp.sum(da, axis=0, keepdims=True)
        dvec_ref[0:1, :] += d_alog
        dvec_ref[1:2, :] += d_dt

    return pl.pallas_call(
        body,
        grid=(seq // ROW_TILE,),
        in_specs=[
            pl.BlockSpec((ROW_TILE, qkv_w), lambda i: (i, 0)),
            pl.BlockSpec((8, qkv_w), lambda i: (jnp.maximum(i * (ROW_TILE // 8) - 1, 0), 0)),
            pl.BlockSpec((ROW_TILE, LANES), lambda i: (i, 0)),
            pl.BlockSpec((4, qkv_w), lambda i: (0, 0)),
            pl.BlockSpec((1, LANES), lambda i: (0, 0)),
            pl.BlockSpec((1, LANES), lambda i: (0, 0)),
            pl.BlockSpec((5, ROW_TILE, HEAD_W), lambda i: (0, i, 0)),
        ],
        out_specs=[pl.BlockSpec((ROW_TILE, qkv_w), lambda i: (i, 0)),
                   pl.BlockSpec((ROW_TILE, LANES), lambda i: (i, 0)),
                   pl.BlockSpec((2, LANES), lambda i: (0, 0))],
        out_shape=[jax.ShapeDtypeStruct((seq, qkv_w), F32), jax.ShapeDtypeStruct((seq, LANES), F32),
                   jax.ShapeDtypeStruct((2, LANES), F32)],
        scratch_shapes=[pltpu.VMEM((ROW_TILE + 8, qkv_w), F32)],
        compiler_params=_params(("arbitrary",)),
        name="delta_prep_bwd",
    )(qkvz, qkvz, ba, conv_w, alog_row, dt_row, dxs)


def _conv_bwd(dconv, qkvz, conv_w, h):
    seq = dconv.shape[0]
    qkv_w = 3 * HEAD_W
    n_tiles = seq // ROW_TILE

    def body(dy_ref, dyn_ref, x_ref, xp_ref, w_ref, h_ref, dx_ref, dw_ref, dwin_ref, stage, dstage):
        i = pl.program_id(0)

        @pl.when(i == 0)
        def _():
            dw_ref[...] = jnp.zeros_like(dw_ref)
            dwin_ref[...] = jnp.zeros_like(dwin_ref)

        _stage_rows(stage, x_ref, xp_ref, i)
        dstage[0:ROW_TILE, :] = dy_ref[...]
        dstage[ROW_TILE:ROW_TILE + 8, :] = jnp.where(i == n_tiles - 1, 0.0, dyn_ref[...])
        dy = dy_ref[...]
        dx = (w_ref[3:4, :] * dy + w_ref[2:3, :] * dstage[1:1 + ROW_TILE, :]
              + w_ref[1:2, :] * dstage[2:2 + ROW_TILE, :] + w_ref[0:1, :] * dstage[3:3 + ROW_TILE, :])
        dx_ref[...] = dx
        dwin_ref[...] += _tn(dx.astype(BF16), h_ref[...])
        for j in range(4):
            dw_ref[j:j + 1, :] += jnp.sum(dy * stage[5 + j:5 + j + ROW_TILE, :], axis=0, keepdims=True)

    tile = pl.BlockSpec((ROW_TILE, qkv_w), lambda i: (i, 0))
    return pl.pallas_call(
        body,
        grid=(n_tiles,),
        in_specs=[
            tile,
            pl.BlockSpec((8, qkv_w), lambda i: (jnp.minimum((i + 1) * (ROW_TILE // 8), seq // 8 - 1), 0)),
            tile,
            pl.BlockSpec((8, qkv_w), lambda i: (jnp.maximum(i * (ROW_TILE // 8) - 1, 0), 0)),
            pl.BlockSpec((4, qkv_w), lambda i: (0, 0)),
            pl.BlockSpec((ROW_TILE, D_MODEL), lambda i: (i, 0)),
        ],
        out_specs=[tile, pl.BlockSpec((4, qkv_w), lambda i: (0, 0)), pl.BlockSpec((qkv_w, D_MODEL), lambda i: (0, 0))],
        out_shape=[jax.ShapeDtypeStruct((seq, qkv_w), F32), jax.ShapeDtypeStruct((4, qkv_w), F32),
                   jax.ShapeDtypeStruct((qkv_w, D_MODEL), F32)],
        scratch_shapes=[pltpu.VMEM((ROW_TILE + 8, qkv_w), F32), pltpu.VMEM((ROW_TILE + 8, qkv_w), F32)],
        compiler_params=_params(("arbitrary",)),
        name="conv_bwd",
    )(dconv, dconv, qkvz, qkvz, conv_w, h)


FF_TILE = 1408
WGRAD_ROWS = 1024


def _row(a):
    return pl.BlockSpec((1, a), lambda *_: (0, 0))


def _rms_fwd(xv, gain):
    rstd = lax.rsqrt(jnp.mean(xv * xv, axis=-1, keepdims=True) + EPS)
    xhat = xv * rstd
    return xhat, rstd, xhat * gain


def _rms_bwd(dnorm, xhat, rstd, gain):
    dxhat = dnorm * gain
    dx = rstd * (dxhat - xhat * jnp.mean(dxhat * xhat, axis=-1, keepdims=True))
    return dx, jnp.sum(dnorm * xhat, axis=0, keepdims=True)


IN_SPLITS = (0, 3 * HEAD_W, 7 * HEAD_W, 7 * HEAD_W + LANES)


def _inproj_fwd(x, gain, scale, shift, w_rows):
    seq = x.shape[0]

    def body(x_ref, g_ref, sc_ref, sh_ref, w_ref, h_ref, a_ref, d_ref, b_ref):
        _, _, norm = _rms_fwd(x_ref[...], g_ref[...])
        h = (norm * (1.0 + sc_ref[...]) + sh_ref[...]).astype(BF16)
        h_ref[...] = h
        for out_ref, lo, hi in zip((a_ref, d_ref, b_ref), IN_SPLITS[:-1], IN_SPLITS[1:]):
            out_ref[...] = _nt(h, w_ref[lo:hi, :])

    def rows(width):
        return pl.BlockSpec((ROW_TILE, width), lambda i: (i, 0))

    return pl.pallas_call(
        body,
        grid=(seq // ROW_TILE,),
        in_specs=[rows(D_MODEL), _row(D_MODEL), _row(D_MODEL), _row(D_MODEL),
                  pl.BlockSpec(w_rows.shape, lambda i: (0, 0))],
        out_specs=[rows(D_MODEL), rows(3 * HEAD_W), rows(4 * HEAD_W), rows(LANES)],
        out_shape=[jax.ShapeDtypeStruct((seq, D_MODEL), BF16), jax.ShapeDtypeStruct((seq, 3 * HEAD_W), F32),
                   jax.ShapeDtypeStruct((seq, 4 * HEAD_W), F32), jax.ShapeDtypeStruct((seq, LANES), F32)],
        compiler_params=_params(("arbitrary",)),
        name="inproj_fwd",
    )(x, gain, scale, shift, w_rows)


def _outproj_fwd(y_attn, y_delta, w_out, x, gate1, gain, scale, shift):
    seq = x.shape[0]

    def body(ya_ref, yd_ref, wa_ref, wd_ref, x_ref, g1_ref, g_ref, sc_ref, sh_ref, x1_ref, h_ref, y_ref):
        y = _nn(ya_ref[...].astype(BF16), wa_ref[...]) + _nn(yd_ref[...], wd_ref[...])
        x1 = x_ref[...] + g1_ref[...] * y
        _, _, norm = _rms_fwd(x1, g_ref[...])
        x1_ref[...] = x1
        h_ref[...] = (norm * (1.0 + sc_ref[...]) + sh_ref[...]).astype(BF16)
        y_ref[...] = y.astype(BF16)

    def rows(width):
        return pl.BlockSpec((ROW_TILE, width), lambda i: (i, 0))

    return pl.pallas_call(
        body,
        grid=(seq // ROW_TILE,),
        in_specs=[rows(HEAD_W), rows(HEAD_W),
                  pl.BlockSpec((HEAD_W, D_MODEL), lambda i: (0, 0)), pl.BlockSpec((HEAD_W, D_MODEL), lambda i: (1, 0)),
                  rows(D_MODEL), _row(D_MODEL), _row(D_MODEL), _row(D_MODEL), _row(D_MODEL)],
        out_specs=[rows(D_MODEL), rows(D_MODEL), rows(D_MODEL)],
        out_shape=[jax.ShapeDtypeStruct((seq, D_MODEL), F32), jax.ShapeDtypeStruct((seq, D_MODEL), BF16),
                   jax.ShapeDtypeStruct((seq, D_MODEL), BF16)],
        compiler_params=_params(("arbitrary",)),
        name="outproj_fwd",
    )(y_attn, y_delta, w_out, w_out, x, gate1, gain, scale, shift)


def _ffn_fwd(h2, w_gate, w_up, w_down, x1, gate2, final_gain, target):
    seq = h2.shape[0]
    n_rows, n_ff = seq // ROW_TILE, D_FF // FF_TILE

    def body(h_ref, wg_ref, wu_ref, wd_ref, x1_ref, g2_ref, gf_ref, t_ref, gate_ref, up_ref, dx2_ref, st_ref, acc):
        i, j = pl.program_id(0), pl.program_id(1)

        @pl.when((i == 0) & (j == 0))
        def _():
            st_ref[...] = jnp.zeros_like(st_ref)

        h = h_ref[...]
        gate = _nt(h, wg_ref[...])
        up = _nt(h, wu_ref[...])
        gate_ref[...] = gate.astype(BF16)
        up_ref[...] = up.astype(BF16)
        part = _nn((_silu(gate) * up).astype(BF16), wd_ref[...])

        @pl.when(j == 0)
        def _():
            acc[...] = part

        @pl.when(j > 0)
        def _():
            acc[...] += part

        @pl.when(j == n_ff - 1)
        def _():
            y2 = acc[...]
            x2 = x1_ref[...] + g2_ref[...] * y2
            xhat, rstd, out = _rms_fwd(x2, gf_ref[...])
            diff = out - t_ref[...]
            dx2, dgain = _rms_bwd(diff * (1.0 / D_MODEL), xhat, rstd, gf_ref[...])
            dx2_ref[...] = dx2
            st_ref[0:1, :] += dgain
            st_ref[1:2, :] += jnp.sum(dx2 * y2, axis=0, keepdims=True)
            st_ref[2:3, :] += jnp.sum(diff * diff, axis=0, keepdims=True) * (0.5 / D_MODEL)

        @pl.when((i == n_rows - 1) & (j == n_ff - 1))
        def _():
            st_ref[3:4, :] = jnp.broadcast_to(jnp.sum(st_ref[2:3, :], keepdims=True), (1, D_MODEL))

    def rows(width):
        return pl.BlockSpec((ROW_TILE, width), lambda i, j: (i, 0))

    ff = pl.BlockSpec((ROW_TILE, FF_TILE), lambda i, j: (i, j))
    return pl.pallas_call(
        body,
        grid=(n_rows, n_ff),
        in_specs=[rows(D_MODEL),
                  pl.BlockSpec((FF_TILE, D_MODEL), lambda i, j: (j, 0)), pl.BlockSpec((FF_TILE, D_MODEL), lambda i, j: (j, 0)),
                  pl.BlockSpec((FF_TILE, D_MODEL), lambda i, j: (j, 0)),
                  rows(D_MODEL), _row(D_MODEL), _row(D_MODEL), rows(D_MODEL)],
        out_specs=[ff, ff, rows(D_MODEL), pl.BlockSpec((8, D_MODEL), lambda i, j: (0, 0))],
        out_shape=[jax.ShapeDtypeStruct((seq, D_FF), BF16), jax.ShapeDtypeStruct((seq, D_FF), BF16),
                   jax.ShapeDtypeStruct((seq, D_MODEL), F32), jax.ShapeDtypeStruct((8, D_MODEL), F32)],
        scratch_shapes=[pltpu.VMEM((ROW_TILE, D_MODEL), F32)],
        compiler_params=_params(("arbitrary", "arbitrary")),
        name="ffn_fwd",
    )(h2, w_gate, w_up, w_down, x1, gate2, final_gain, target)


def _ffn_bwd(dx2, gate, up, w_gate, w_up, w_down, x1, y, gate2, gate1, gain, scale):
    seq = dx2.shape[0]

    def act_body(dx2_ref, g2_ref, gate_ref, up_ref, wd_ref, dgate_ref, dup_ref, dwd_ref):
        dy2 = (g2_ref[...] * dx2_ref[...]).astype(BF16)
        gate = gate_ref[...].astype(F32)
        up = up_ref[...].astype(F32)
        dact = _nt(dy2, wd_ref[...])
        silu, slope = _silu_and_slope(gate)
        dgate_ref[...] = (dact * up * slope).astype(BF16)
        dup_ref[...] = (dact * silu).astype(BF16)
        part = _tn((silu * up).astype(BF16), dy2)

        @pl.when(pl.program_id(1) == 0)
        def _():
            dwd_ref[...] = part

        @pl.when(pl.program_id(1) > 0)
        def _():
            dwd_ref[...] += part

    ff = pl.BlockSpec((ROW_TILE, FF_TILE), lambda j, i: (i, j))
    w_tile = pl.BlockSpec((FF_TILE, D_MODEL), lambda j, i: (j, 0))
    dgate, dup, dw_down = pl.pallas_call(
        act_body,
        grid=(D_FF // FF_TILE, seq // ROW_TILE),
        in_specs=[pl.BlockSpec((ROW_TILE, D_MODEL), lambda j, i: (i, 0)), _row(D_MODEL), ff, ff, w_tile],
        out_specs=[ff, ff, w_tile],
        out_shape=[jax.ShapeDtypeStruct((seq, D_FF), BF16)] * 2 + [jax.ShapeDtypeStruct((D_FF, D_MODEL), F32)],
        compiler_params=_params(("arbitrary", "arbitrary")),
        name="ffn_bwd_act",
    )(dx2, gate2, gate, up, w_down)

    def in_body(dgate_ref, dup_ref, wg_ref, wu_ref, dx2_ref, x1_ref, y_ref, g1_ref, g_ref, sc_ref,
                dx1_ref, dy_ref, st_ref):
        @pl.when(pl.program_id(0) == 0)
        def _():
            st_ref[...] = jnp.zeros_like(st_ref)

        dh = _nn(dgate_ref[...], wg_ref[...]) + _nn(dup_ref[...], wu_ref[...])
        xhat, rstd, norm = _rms_fwd(x1_ref[...], g_ref[...])
        dxn, dgain = _rms_bwd(dh * (1.0 + sc_ref[...]), xhat, rstd, g_ref[...])
        dx1 = dx2_ref[...] + dxn
        dx1_ref[...] = dx1
        dy_ref[...] = (g1_ref[...] * dx1).astype(BF16)
        st_ref[0:1, :] += jnp.sum(dh, axis=0, keepdims=True)
        st_ref[1:2, :] += jnp.sum(dh * norm, axis=0, keepdims=True)
        st_ref[2:3, :] += dgain
        st_ref[3:4, :] += jnp.sum(dx1 * y_ref[...].astype(F32), axis=0, keepdims=True)

    half_tile = ROW_TILE // 2

    def rows(width):
        return pl.BlockSpec((half_tile, width), lambda i: (i, 0))

    whole = pl.BlockSpec((D_FF, D_MODEL), lambda i: (0, 0))
    dx1, dy, stats = pl.pallas_call(
        in_body,
        grid=(seq // half_tile,),
        in_specs=[rows(D_FF), rows(D_FF), whole, whole, rows(D_MODEL), rows(D_MODEL), rows(D_MODEL),
                  _row(D_MODEL), _row(D_MODEL), _row(D_MODEL)],
        out_specs=[rows(D_MODEL), rows(D_MODEL), pl.BlockSpec((8, D_MODEL), lambda i: (0, 0))],
        out_shape=[jax.ShapeDtypeStruct((seq, D_MODEL), F32), jax.ShapeDtypeStruct((seq, D_MODEL), BF16),
                   jax.ShapeDtypeStruct((8, D_MODEL), F32)],
        compiler_params=_params(("arbitrary",)),
        name="ffn_bwd_in",
    )(dgate, dup, w_gate, w_up, dx2, x1, y, gate1, gain, scale)
    return dgate, dup, dw_down, dx1, dy, stats


def _outproj_bwd(dy, w_out, y_attn, y_delta):
    seq = dy.shape[0]

    def body(dy_ref, w_ref, ya_ref, yd_ref, out_ref, dw_ref):
        @pl.when(pl.program_id(0) == 0)
        def _():
            dw_ref[...] = jnp.zeros_like(dw_ref)

        dyv = dy_ref[...]
        out_ref[...] = _nt(dyv, w_ref[...])
        dw_ref[0:HEAD_W, :] += _tn(ya_ref[...].astype(BF16), dyv)
        dw_ref[HEAD_W:, :] += _tn(yd_ref[...], dyv)

    rows = pl.BlockSpec((ROW_TILE, D_MODEL), lambda i: (i, 0))
    half = pl.BlockSpec((ROW_TILE, HEAD_W), lambda i: (i, 0))
    whole = pl.BlockSpec((D_MODEL, D_MODEL), lambda i: (0, 0))
    return pl.pallas_call(
        body,
        grid=(seq // ROW_TILE,),
        in_specs=[rows, whole, half, half],
        out_specs=[rows, whole],
        out_shape=[jax.ShapeDtypeStruct((seq, D_MODEL), F32), jax.ShapeDtypeStruct((D_MODEL, D_MODEL), F32)],
        compiler_params=_params(("arbitrary",)),
        name="outproj_bwd",
    )(dy, w_out, y_attn, y_delta)


def _inproj_bwd(dq, dk, dv, dxd, dz, dba, w_rows, x, dx1, gain, scale, partials):
    seq = x.shape[0]
    n = len(partials)
    n_steps = seq // ROW_TILE

    def body(*refs):
        pieces, (w_ref, x_ref, dx1_ref, g_ref, sc_ref) = refs[:6], refs[6:11]
        gx_ref, st_ref = refs[11 + n:13 + n]
        riding = (refs[11:11 + n], refs[13 + n:13 + 2 * n], *refs[13 + 2 * n:])

        @pl.when(pl.program_id(0) == 0)
        def _():
            st_ref[...] = jnp.zeros_like(st_ref)
            for cp in (_scatter_copies(*riding) if n else []):
                cp.start()

        dh = _nn(jnp.concatenate([p[...].astype(BF16) for p in pieces], axis=1), w_ref[...])
        xhat, rstd, norm = _rms_fwd(x_ref[...], g_ref[...])
        dxn, dgain = _rms_bwd(dh * (1.0 + sc_ref[...]), xhat, rstd, g_ref[...])
        gx_ref[...] = dx1_ref[...] + dxn
        st_ref[0:1, :] += jnp.sum(dh, axis=0, keepdims=True)
        st_ref[1:2, :] += jnp.sum(dh * norm, axis=0, keepdims=True)
        st_ref[2:3, :] += dgain

        if n:
            @pl.when(pl.program_id(0) == n_steps - 1)
            def _():
                for cp in _scatter_copies(*riding):
                    cp.wait()

    def rows(width):
        return pl.BlockSpec((ROW_TILE, width), lambda i: (i, 0))

    sems = [pltpu.SemaphoreType.DMA((3 * n,)), pltpu.SemaphoreType.DMA((3 * n,))] if n else []
    return pl.pallas_call(
        body,
        grid=(n_steps,),
        in_specs=[rows(HEAD_W), rows(HEAD_W), rows(HEAD_W), rows(3 * HEAD_W), rows(HEAD_W), rows(LANES),
                  pl.BlockSpec(w_rows.shape, lambda i: (0, 0)), rows(D_MODEL), rows(D_MODEL), _row(D_MODEL),
                  _row(D_MODEL)]
        + [ANY] * n,
        out_specs=[rows(D_MODEL), pl.BlockSpec((8, D_MODEL), lambda i: (0, 0))] + [ANY] * n,
        out_shape=[jax.ShapeDtypeStruct((seq, D_MODEL), F32), jax.ShapeDtypeStruct((8, D_MODEL), F32)]
        + [jax.ShapeDtypeStruct(p.shape, p.dtype) for p in partials],
        scratch_shapes=sems,
        compiler_params=_params(("arbitrary",)),
        name="inproj_bwd",
    )(dq, dk, dv, dxd, dz, dba, w_rows, x, dx1, gain, scale, *partials)


def _weight_grad(a, b, name):
    seq, m = a.shape
    n = b.shape[1]
    tm = m if m <= 1536 else m // 2
    tn = n if n <= 1536 else n // 2
    rows = 2 * WGRAD_ROWS
    n_k = seq // rows

    def body(a_ref, b_ref, out_ref):
        part = _tn(a_ref[...].astype(BF16), b_ref[...].astype(BF16))

        @pl.when(pl.program_id(2) == 0)
        def _():
            out_ref[...] = part

        @pl.when(pl.program_id(2) > 0)
        def _():
            out_ref[...] += part

    return pl.pallas_call(
        body,
        grid=(m // tm, n // tn, n_k),
        in_specs=[pl.BlockSpec((rows, tm), lambda i, j, k: (k, i)),
                  pl.BlockSpec((rows, tn), lambda i, j, k: (k, j))],
        out_specs=pl.BlockSpec((tm, tn), lambda i, j, k: (i, j)),
        out_shape=jax.ShapeDtypeStruct((m, n), F32),
        compiler_params=_params(("arbitrary", "arbitrary", "arbitrary")),
        name=name,
    )(a, b)


def _weight_grad_stack(pieces, b, name):
    seq, n = b.shape
    widths = [a.shape[1] for a in pieces]
    starts = [sum(widths[:i]) for i in range(len(pieces))]

    def body(*refs):
        a_refs, b_ref, out_ref = refs[:len(pieces)], refs[len(pieces)], refs[len(pieces) + 1]

        @pl.when(pl.program_id(0) == 0)
        def _():
            out_ref[...] = jnp.zeros_like(out_ref)

        bb = b_ref[...].astype(BF16)
        for a_ref, start, width in zip(a_refs, starts, widths):
            out_ref[start:start + width, :] += _tn(a_ref[...].astype(BF16), bb)

    def rows(width):
        return pl.BlockSpec((WGRAD_ROWS, width), lambda k: (k, 0))

    return pl.pallas_call(
        body,
        grid=(seq // WGRAD_ROWS,),
        in_specs=[rows(w) for w in widths] + [rows(n)],
        out_specs=pl.BlockSpec((sum(widths), n), lambda k: (0, 0)),
        out_shape=jax.ShapeDtypeStruct((sum(widths), n), F32),
        compiler_params=_params(("arbitrary",)),
        name=name,
    )(*pieces, b)


def _adamw(w, g, m, v, name):
    n_rows, n_cols = w.shape
    if w.size <= 64 * 1024:
        block, grid, index = (n_rows, n_cols), (1,), lambda i: (0, 0)
    elif n_rows % 256 == 0:
        block, grid, index = (256, n_cols), (n_rows // 256,), lambda i: (i, 0)
    elif n_cols % 256 == 0:
        block, grid, index = (n_rows, 256), (n_cols // 256,), lambda i: (0, i)
    else:
        block, grid, index = (n_rows, n_cols), (1,), lambda i: (0, 0)

    def body(w_ref, g_ref, m_ref, v_ref, d_ref, nm_ref, nv_ref):
        gv = g_ref[...]
        nm = ADAM_B1 * m_ref[...] + (1.0 - ADAM_B1) * gv
        nv = ADAM_B2 * v_ref[...] + (1.0 - ADAM_B2) * (gv * gv)
        m_hat = nm / (1.0 - ADAM_B1 ** ADAM_STEP)
        v_hat = nv / (1.0 - ADAM_B2 ** ADAM_STEP)
        d_ref[...] = -ADAM_LR * (m_hat / (jnp.sqrt(v_hat) + ADAM_EPS) + ADAM_WD * w_ref[...])
        nm_ref[...] = nm
        nv_ref[...] = nv

    blk = pl.BlockSpec(block, index)
    shape = jax.ShapeDtypeStruct((n_rows, n_cols), F32)
    return pl.pallas_call(
        body,
        grid=grid,
        in_specs=[blk] * 4,
        out_specs=[blk] * 3,
        out_shape=[shape] * 3,
        compiler_params=_params(("arbitrary",)),
        name=name,
    )(w, g, m, v)


IN_WIDTH = 3600


def _local_step(x, target, mod, norm_attn_g, w_in, rel_bias, conv_w, a_log, dt_bias, delta_norm_g,
                norm_ffn_g, final_norm_g, shards, assemble, reduce_pairs):
    sh1, sc1, g1, sh2, sc2, g2 = [mod[:, i * D_MODEL:(i + 1) * D_MODEL] for i in range(6)]
    w_rows = jnp.pad(w_in, ((0, IN_SPLITS[-1] - IN_WIDTH), (0, 0)))
    tables = jnp.asarray(_attn_tables())
    alog_row = jnp.pad(a_log, ((0, 0), (N_HEADS, LANES - 2 * N_HEADS)))
    dt_row = jnp.pad(dt_bias, ((0, 0), (N_HEADS, LANES - 2 * N_HEADS)))
    gain_row = jnp.tile(delta_norm_g, (1, N_HEADS))

    h1, qkv_a, qkvz, ba = _inproj_fwd(x, norm_attn_g, sc1, sh1, w_rows)
    bias = _attention_bias(rel_bias, tables)
    y_attn, lse, *gathered = _attention_fwd(qkv_a, bias, shards)
    w_out, w_gate, w_up, w_down = assemble(gathered)
    xs = _delta_prep_fwd(qkvz, ba, conv_w, alog_row, dt_row)
    inv_h, qk_h, u_h, w_h = _delta_chunk_fwd(xs)
    o, st_h = _delta_scan_fwd(xs, qk_h, u_h, w_h)
    y_delta = _delta_post_fwd(o, qkvz, gain_row)
    x1, h2, y = _outproj_fwd(y_attn, y_delta, w_out, x, g1, norm_ffn_g, sc2, sh2)
    gate, up, dx2, st_f = _ffn_fwd(h2, w_gate, w_up, w_down, x1, g2, final_norm_g, target)

    dgate, dup, dw_down, dx1, dy, st_b = _ffn_bwd(dx2, gate, up, w_gate, w_up, w_down, x1, y, g2, g1, norm_ffn_g, sc2)
    dycat, dw_out = _outproj_bwd(dy, w_out, y_attn, y_delta)
    partials = reduce_pairs([dw_out, _weight_grad(dgate, h2, "wgrad_gate"), _weight_grad(dup, h2, "wgrad_up"),
                             dw_down], 1, "rest")
    grads = {}
    do, dz, dgain, dw_in_z = _delta_post_bwd(dycat, o, qkvz, gain_row, h1)
    dsn_h, dvn_h = _delta_scan_bwd(xs, qk_h, w_h, do)
    dxs = _delta_chunk_bwd(xs, inv_h, u_h, w_h, st_h, dsn_h, dvn_h, do)
    dconv, dba, dvec = _delta_prep_bwd(qkvz, ba, conv_w, alog_row, dt_row, dxs)
    dxd, grads["conv_w"], dw_in_delta = _conv_bwd(dconv, qkvz, conv_w, h1)
    dq, dk, dv, dbias, *scattered = _attention_bwd(qkv_a, dycat, y_attn, lse, bias, partials)
    partials_in = reduce_pairs([jnp.concatenate(
        [_weight_grad_stack([dq, dk, dv], h1, "wgrad_in_attn"), dw_in_delta, dw_in_z,
         _weight_grad_stack([dba], h1, "wgrad_in_gates")[:2 * N_HEADS]], axis=0)], 0, "in")
    grad_x, st_i, *scattered_in = _inproj_bwd(dq, dk, dv, dxd, dz, dba, w_rows, x, dx1, norm_attn_g, sc1,
                                              partials_in)
    grads["rel_bias"] = _rel_bias_grad(dbias, tables)[:, :N_BUCKETS].T
    grads["a_log"] = dvec[0:1, N_HEADS:2 * N_HEADS]
    grads["dt_bias"] = dvec[1:2, N_HEADS:2 * N_HEADS]
    grads["delta_norm_g"] = dgain[1:2, :HEAD_DIM]
    grads["norm_attn_g"] = st_i[2:3]
    grads["norm_ffn_g"] = st_b[2:3]
    grads["final_norm_g"] = st_f[0:1]
    dmod = jnp.concatenate([st_i[0:1], st_i[1:2], st_b[3:4], st_b[0:1], st_b[1:2], st_f[1:2]], axis=1)
    return st_f[3, 0], grad_x, grads, dmod, (partials_in + partials, scattered_in + scattered)


MESH = pl.DeviceIdType.MESH
OTHER_CHIPS = ((1, 0), (0, 1), (1, 1))
ALL_PEERS = tuple((m >> 2 & 1, m >> 1 & 1, m & 1) for m in range(1, 8))
ANY = pl.BlockSpec(memory_space=pl.ANY)
VMEM_SPEC = pl.BlockSpec(memory_space=pltpu.VMEM)


def _me():
    return lax.axis_index("x"), lax.axis_index("y"), lax.axis_index("c")


def _flip(pos, mask):
    return tuple(1 - p if m else p for p, m in zip(pos, mask))


def _remote(src, dst, send_sems, recv_sems, k, to):
    return pltpu.make_async_remote_copy(src_ref=src, dst_ref=dst, send_sem=send_sems.at[k], recv_sem=recv_sems.at[k],
                                        device_id=to, device_id_type=MESH)


def _ada_exchange(c8, w_ada, b_ada, conv8, shard):
    def body(c_ref, w_ref, b_ref, cv_ref, shard_ref, mod_ref, cact_ref, conv_ref, whole_ref,
             c_all, part_all, send_sems, recv_sems, ride_send, ride_recv):
        x, y, c = me = _me()
        dev = 4 * x + 2 * y + c
        chip = 2 * x + y
        riding = ([shard_ref], [whole_ref], ride_send, ride_recv)
        for cp in _gather_copies(*riding, hand_over=False)[0]:
            cp.start()
        c_all[dev] = c_ref[...]
        conv_ref[chip] = cv_ref[...]
        first = [_remote(c_ref, c_all.at[dev], send_sems, recv_sems, k, _flip(me, mask))
                 for k, mask in enumerate(ALL_PEERS)]
        first += [_remote(cv_ref, conv_ref.at[chip], send_sems, recv_sems, 7 + j, _flip(me, (*mask, 0)))
                  for j, mask in enumerate(OTHER_CHIPS)]
        for cp in first:
            cp.start()
        for cp in first:
            cp.wait()
        row = lax.broadcasted_iota(jnp.int32, (8, D_MODEL), 0)
        c_rows = jnp.zeros((8, D_MODEL), F32)
        for d in range(8):
            c_rows = jnp.where(row == d, c_all[d], c_rows)
        c_act = _silu(c_rows)
        cact_ref[...] = c_act
        part_all[chip] = _nn(c_act, w_ref[...], HIGHEST)
        second = [_remote(part_all.at[chip], part_all.at[chip], send_sems, recv_sems, 10 + j, _flip(me, (*mask, 0)))
                  for j, mask in enumerate(OTHER_CHIPS)]
        for cp in second:
            cp.start()
        for cp in second:
            cp.wait()
        cols = w_ref.shape[1]
        for k in range(4):
            mod_ref[:, k * cols:(k + 1) * cols] = part_all[k] + b_ref[:, k * cols:(k + 1) * cols]
        first, passed = _gather_copies(*riding)
        for cp, fwd in zip(first, passed):
            cp.wait_recv()
            fwd.start()
        for cp in first:
            cp.wait_send()
        for fwd in passed:
            fwd.wait()

    cols = w_ada.shape[1]
    return pl.pallas_call(
        body,
        in_specs=[VMEM_SPEC] * 4 + [ANY],
        out_specs=[VMEM_SPEC] * 3 + [ANY],
        out_shape=[jax.ShapeDtypeStruct((8, 4 * cols), F32), jax.ShapeDtypeStruct((8, D_MODEL), F32),
                   jax.ShapeDtypeStruct((4, 8, conv8.shape[1]), F32)] + _gathered_shapes([shard]),
        scratch_shapes=[pltpu.VMEM((8, 8, D_MODEL), F32), pltpu.VMEM((4, 8, cols), F32),
                        pltpu.SemaphoreType.DMA((13,)), pltpu.SemaphoreType.DMA((13,)),
                        pltpu.SemaphoreType.DMA((6,)), pltpu.SemaphoreType.DMA((6,))],
        compiler_params=pltpu.CompilerParams(vmem_limit_bytes=VMEM_LIMIT),
        name="ada_exchange",
    )(c8, w_ada, b_ada, conv8, shard)


def _gathered_shapes(shards):
    return [jax.ShapeDtypeStruct((4, *s.shape), s.dtype) for s in shards]


def _gather_copies(srcs, dsts, send_sems, recv_sems, hand_over=True):
    x, y, c = me = _me()
    chip = 2 * x + y
    sibling = _flip(me, (0, 0, 1))
    first, passed = [], []
    for a, (src, dst) in enumerate(zip(srcs, dsts)):
        for j, mask in enumerate(OTHER_CHIPS):
            to = _flip(me, (*mask, 0))
            first.append(_remote(src.at[c], dst.at[chip, c], send_sems, recv_sems, 6 * a + j, to))
            if hand_over:
                landed = dst.at[2 * to[0] + to[1], c]
                passed.append(_remote(landed, landed, send_sems, recv_sems, 6 * a + 3 + j, sibling))
    return first, passed


def _scatter_copies(srcs, dsts, send_sems, recv_sems):
    x, y, c = me = _me()
    chip = 2 * x + y
    copies = []
    for a, (src, dst) in enumerate(zip(srcs, dsts)):
        for j, mask in enumerate(OTHER_CHIPS):
            to = _flip(me, (*mask, 0))
            copies.append(_remote(src.at[2 * to[0] + to[1]], dst.at[chip], send_sems, recv_sems, 3 * a + j, to))
    return copies


def _start_and_wait(copies):
    for cp in copies:
        cp.start()
    for cp in copies:
        cp.wait()


def _swap_halves(grads):
    n = len(grads)

    def body(*refs):
        srcs, got = refs[:n], refs[n:2 * n]
        send_sems, recv_sems = refs[2 * n:]
        x, y, c = me = _me()
        _start_and_wait([_remote(srcs[a].at[:, 1 - c], got[a], send_sems, recv_sems, a, _flip(me, (0, 0, 1)))
                         for a in range(n)])

    return pl.pallas_call(
        body,
        in_specs=[ANY] * n,
        out_specs=[ANY] * n,
        out_shape=[jax.ShapeDtypeStruct((4, g.shape[2], g.shape[3]), g.dtype) for g in grads],
        scratch_shapes=[pltpu.SemaphoreType.DMA((n,)), pltpu.SemaphoreType.DMA((n,))],
        name=f"swap_halves_{n}",
    )(*grads)


def _join_halves(halves):
    n = len(halves)

    def body(*refs):
        srcs, dsts = refs[:n], refs[n:2 * n]
        send_sems, recv_sems = refs[2 * n:]
        x, y, c = me = _me()
        _start_and_wait([_remote(srcs[a], dsts[a].at[c], send_sems, recv_sems, a, _flip(me, (0, 0, 1)))
                         for a in range(n)])

    return pl.pallas_call(
        body,
        in_specs=[ANY] * n,
        out_specs=[ANY] * n,
        out_shape=[jax.ShapeDtypeStruct((2, *h.shape), h.dtype) for h in halves],
        scratch_shapes=[pltpu.SemaphoreType.DMA((n,)), pltpu.SemaphoreType.DMA((n,))],
        name=f"join_halves_{n}",
    )(*halves)


def _gather_small(packed):
    n_rows = packed.shape[0]

    def body(p_ref, all_ref, sum_ref, send_sems, recv_sems):
        x, y, c = me = _me()
        dev = 4 * x + 2 * y + c
        all_ref[dev] = p_ref[...]
        copies = [_remote(p_ref, all_ref.at[dev], send_sems, recv_sems, k, _flip(me, mask))
                  for k, mask in enumerate(ALL_PEERS)]
        for cp in copies:
            cp.start()
        for cp in copies:
            cp.wait()
        total = all_ref[0]
        for d in range(1, 8):
            total = total + all_ref[d]
        sum_ref[...] = total

    return pl.pallas_call(
        body,
        in_specs=[VMEM_SPEC],
        out_specs=[VMEM_SPEC, VMEM_SPEC],
        out_shape=[jax.ShapeDtypeStruct((8, n_rows, LANES), F32), jax.ShapeDtypeStruct((n_rows, LANES), F32)],
        scratch_shapes=[pltpu.SemaphoreType.DMA((7,)), pltpu.SemaphoreType.DMA((7,))],
        name="gather_small",
    )(packed)


def _add_pair(a, b, out_dtype, name):
    def body(a_ref, b_ref, o_ref):
        o_ref[...] = (a_ref[...] + b_ref[...]).astype(o_ref.dtype)

    blk = pl.BlockSpec((1, *a.shape[1:]), lambda i: (i, 0, 0))
    return pl.pallas_call(
        body, grid=(a.shape[0],), in_specs=[blk, blk], out_specs=blk,
        out_shape=jax.ShapeDtypeStruct(a.shape, out_dtype),
        compiler_params=_params(("arbitrary",)), name=name,
    )(a, b)


def _add_slots(a, name):
    def body(a_ref, o_ref):
        total = a_ref[0].astype(F32)
        for k in range(1, 4):
            total = total + a_ref[k].astype(F32)
        o_ref[...] = total

    return pl.pallas_call(
        body, in_specs=[VMEM_SPEC], out_specs=VMEM_SPEC,
        out_shape=jax.ShapeDtypeStruct(a.shape[1:], F32),
        compiler_params=pltpu.CompilerParams(vmem_limit_bytes=VMEM_LIMIT), name=name,
    )(a)


def _ada_weight_grad(c_act, dmod_cols):
    def body(c_ref, d_ref, o_ref):
        o_ref[...] = _tn(c_ref[...], d_ref[...], HIGHEST)

    return pl.pallas_call(
        body, in_specs=[VMEM_SPEC, VMEM_SPEC], out_specs=VMEM_SPEC,
        out_shape=jax.ShapeDtypeStruct((c_act.shape[1], dmod_cols.shape[1]), F32),
        compiler_params=pltpu.CompilerParams(vmem_limit_bytes=VMEM_LIMIT), name="ada_weight_grad",
    )(c_act, dmod_cols)


def kernel(x, c, w_ada, b_ada, norm_attn_g, w_in, rel_bias, conv_w, a_log, dt_bias, delta_norm_g, w_out, norm_ffn_g, w_gate, w_up, w_down, final_norm_g, loss_target, m_w_ada, m_b_ada, m_norm_attn_g, m_w_in, m_rel_bias, m_conv_w, m_a_log, m_dt_bias, m_delta_norm_g, m_w_out, m_norm_ffn_g, m_w_gate, m_w_up, m_w_down, m_final_norm_g, v_w_ada, v_b_ada, v_norm_attn_g, v_w_in, v_rel_bias, v_conv_w, v_a_log, v_dt_bias, v_delta_norm_g, v_w_out, v_norm_ffn_g, v_w_gate, v_w_up, v_w_down, v_final_norm_g):
    xi, yi, ci = _me()
    dev = 4 * xi + 2 * yi + ci
    chip = 2 * xi + yi

    big_names = ("w_in", "w_out", "w_gate", "w_up", "w_down")
    by_cols = (True, False, True, True, False)

    def rows_form(a, cols):
        return jnp.swapaxes(a[0], 0, 1) if cols else a[0]

    def halves_form(w):
        rows, lanes = w.shape
        if (rows // 2) % 16:
            rows, lanes = w.size // LANES, LANES
        return (2, rows // 2, lanes)

    big = [rows_form(w, cols) for w, cols in zip((w_in, w_out, w_gate, w_up, w_down), by_cols)]
    shards = [w.astype(BF16).reshape(halves_form(w)) for w in big]

    def assemble(gathered, first):
        return [lax.dynamic_update_index_in_dim(g, s, chip, 0).reshape(4 * w.shape[0], w.shape[1])
                for g, s, w in zip(gathered, shards[first:], big[first:])]

    def reduce_pairs(grads, first, tag):
        slots = [g.reshape(4, *halves_form(w)) for g, w in zip(grads, big[first:])]
        return [_add_pair(lax.dynamic_index_in_dim(s, ci, 1, keepdims=False), got, BF16, f"add_pair_{tag}{a}")
                for a, (s, got) in enumerate(zip(slots, _swap_halves(slots)))]

    def finish(partials, scattered, first, tag):
        by_source = [lax.dynamic_update_index_in_dim(b, lax.dynamic_index_in_dim(p, chip, 0, keepdims=False), chip, 0)
                     for b, p in zip(scattered, partials)]
        halves = [_add_slots(p, f"add_slots_{tag}{a}") for a, p in enumerate(by_source)]
        joined = [lax.dynamic_update_index_in_dim(j, h, ci, 0) for j, h in zip(_join_halves(halves), halves)]
        return [j.reshape(w.shape) for j, w in zip(joined, big[first:])]

    conv_cols = conv_w.shape[2]
    mod_all, c_act, conv_all, gathered_in = _ada_exchange(
        jnp.broadcast_to(c, (8, D_MODEL)), w_ada[0], b_ada, jnp.pad(conv_w[0], ((0, 4), (0, 0))), shards[0])
    mod = lax.dynamic_slice_in_dim(mod_all, dev, 1, axis=0)
    conv_full = jnp.swapaxes(conv_all[:, :4, :], 0, 1).reshape(4, 4 * conv_cols)
    whole_in, = assemble([gathered_in], 0)
    loss, grad_x, grads, dmod, (partials, scattered) = _local_step(
        x[0], loss_target[0], mod, norm_attn_g, whole_in, rel_bias, conv_full, a_log, dt_bias, delta_norm_g,
        norm_ffn_g, final_norm_g[None], shards[1:], functools.partial(assemble, first=1), reduce_pairs)

    big_grads = finish(partials, scattered, 0, "all")

    pieces = [dmod, grads["conv_w"], grads["norm_attn_g"], grads["norm_ffn_g"], grads["final_norm_g"],
              grads["rel_bias"], grads["a_log"], grads["dt_bias"], grads["delta_norm_g"]]
    flat = [jnp.pad(p.reshape(-1), (0, -p.size % LANES)) for p in pieces]
    n_rows = [f.size // LANES for f in flat]
    packed = jnp.concatenate(flat).reshape(-1, LANES)
    packed = jnp.pad(packed, ((0, -packed.shape[0] % 8), (0, 0)))
    all_small, total = _gather_small(packed)
    sums, start = [], 0
    for p, n in zip(pieces, n_rows):
        sums.append(total[start:start + n].reshape(-1)[:p.size].reshape(p.shape))
        start += n
    g_b_ada, g_conv, g_norm_attn, g_norm_ffn, g_final, g_rel, g_alog, g_dt, g_dnorm = sums
    dmod_all = all_small[:, :n_rows[0], :].reshape(8, -1)
    ada_cols = w_ada.shape[2]
    g_w_ada = _ada_weight_grad(c_act, lax.dynamic_slice_in_dim(dmod_all, chip * ada_cols, ada_cols, axis=1))
    g_conv = lax.dynamic_slice_in_dim(g_conv, chip * conv_cols, conv_cols, axis=1)

    grad = {"w_ada": g_w_ada[None], "b_ada": g_b_ada, "norm_attn_g": g_norm_attn,
            "rel_bias": g_rel, "conv_w": g_conv[None], "a_log": g_alog, "dt_bias": g_dt, "delta_norm_g": g_dnorm,
            "norm_ffn_g": g_norm_ffn, "final_norm_g": g_final.reshape(-1)}
    weight = {"w_ada": w_ada, "b_ada": b_ada, "norm_attn_g": norm_attn_g, "w_in": w_in, "rel_bias": rel_bias,
              "conv_w": conv_w, "a_log": a_log, "dt_bias": dt_bias, "delta_norm_g": delta_norm_g, "w_out": w_out,
              "norm_ffn_g": norm_ffn_g, "w_gate": w_gate, "w_up": w_up, "w_down": w_down, "final_norm_g": final_norm_g}
    first = {"w_ada": m_w_ada, "b_ada": m_b_ada, "norm_attn_g": m_norm_attn_g, "w_in": m_w_in, "rel_bias": m_rel_bias,
             "conv_w": m_conv_w, "a_log": m_a_log, "dt_bias": m_dt_bias, "delta_norm_g": m_delta_norm_g,
             "w_out": m_w_out, "norm_ffn_g": m_norm_ffn_g, "w_gate": m_w_gate, "w_up": m_w_up, "w_down": m_w_down,
             "final_norm_g": m_final_norm_g}
    second = {"w_ada": v_w_ada, "b_ada": v_b_ada, "norm_attn_g": v_norm_attn_g, "w_in": v_w_in, "rel_bias": v_rel_bias,
              "conv_w": v_conv_w, "a_log": v_a_log, "dt_bias": v_dt_bias, "delta_norm_g": v_delta_norm_g,
              "w_out": v_w_out, "norm_ffn_g": v_norm_ffn_g, "w_gate": v_w_gate, "w_up": v_w_up, "w_down": v_w_down,
              "final_norm_g": v_final_norm_g}
    delta, new_m, new_v = {}, {}, {}
    for name, w in weight.items():
        if name in big_names:
            continue
        two_d = (-1, w.shape[-1])
        d, nm, nv = _adamw(w.reshape(two_d), grad[name].reshape(two_d), first[name].reshape(two_d),
                           second[name].reshape(two_d), f"adamw_{name}")
        delta[name], new_m[name], new_v[name] = d.reshape(w.shape), nm.reshape(w.shape), nv.reshape(w.shape)
    for name, w, g, cols in zip(big_names, big, big_grads, by_cols):
        outs = _adamw(w, g, rows_form(first[name], cols), rows_form(second[name], cols), f"adamw_{name}")
        grad[name], delta[name], new_m[name], new_v[name] = [
            (jnp.swapaxes(o, 0, 1) if cols else o)[None] for o in (g, *outs)]

    names = list(weight)
    return (lax.psum(loss, ("x", "y", "c")), grad_x[None], *[grad[n] for n in names], *[delta[n] for n in names],
            *[new_m[n] for n in names], *[new_v[n] for n in names])
```

```python
import functools
import math

import numpy as np
import jax
import jax.numpy as jnp
from jax import lax
from jax.experimental import pallas as pl
from jax.experimental.pallas import tpu as pltpu

F32 = jnp.float32
BF16 = jnp.bfloat16
HIGHEST = lax.Precision.HIGHEST

D_MODEL = 1024
HEAD_DIM = 64
N_HEADS = 8
HEAD_W = 512
BRANCHES = ((128, 1), (512, 4), (2048, 16))
BAND = 128
ATT_TILE = 2048
ATT_UNROLL = 8
ATT_UNROLL_BWD = 4
N_BUCKETS = 32
MAX_DISTANCE = 2048
CHUNK = 64
D_FF = 2816
EPS = 1e-6
NEG_INF = -1e30
LANES = 128
VMEM_LIMIT = 56 * 1024 * 1024

ADAM_LR = 0.001
ADAM_B1 = 0.9
ADAM_B2 = 0.999
ADAM_EPS = 1e-08
ADAM_WD = 0.01
ADAM_STEP = 10


def _nn(a, b, precision=None):
    return jnp.dot(a, b, preferred_element_type=F32, precision=precision)


def _nt(a, b, precision=None):
    return lax.dot_general(a, b, (((1,), (1,)), ((), ())), preferred_element_type=F32, precision=precision)


def _tn(a, b, precision=None):
    return lax.dot_general(a, b, (((0,), (0,)), ((), ())), preferred_element_type=F32, precision=precision)


def _params(sem, vmem=VMEM_LIMIT):
    return pltpu.CompilerParams(dimension_semantics=sem, vmem_limit_bytes=vmem)


def _sigmoid(x):
    return 0.5 * jnp.tanh(0.5 * x) + 0.5


def _silu_and_slope(x):
    s = _sigmoid(x)
    return x * s, s * (1.0 + x * (1.0 - s))


def _silu(x):
    return x * _sigmoid(x)


def _attn_tables():
    qi = np.arange(BAND)[:, None]
    kj = np.arange(2 * BAND)[None, :]
    steps = qi + BAND - kj
    in_window = (steps >= 0) & (steps <= BAND)
    max_exact = N_BUCKETS // 2
    out = np.zeros((3, 2, BAND, 2 * BAND), np.int32)
    for b, (_, dil) in enumerate(BRANCHES):
        dist = np.maximum(steps, 0) * dil
        dist_f = np.maximum(dist, 1).astype(np.float32)
        large = max_exact + (np.log(dist_f / np.float32(max_exact)) / np.float32(math.log(MAX_DISTANCE / max_exact))
                             * np.float32(N_BUCKETS - max_exact)).astype(np.int32)
        bucket = np.where(dist < max_exact, dist, np.minimum(large, N_BUCKETS - 1)).astype(np.int32)
        out[b, 0] = np.where(in_window, bucket, -1)
        out[b, 1] = np.where(in_window & (kj >= BAND), bucket, -1)
    return out


def _attention_bias(rel_bias, tables):
    def body(rel_ref, tab_ref, out_ref):
        head = pl.program_id(0)
        for b in range(3):
            tab = tab_ref[b, 0]

            def pick(kk, acc, tab=tab):
                return jnp.where(tab == kk, rel_ref[kk, head], acc)

            acc = lax.fori_loop(0, N_BUCKETS, pick, jnp.zeros((BAND, 2 * BAND), F32))
            for first in range(2):
                out_ref[0, b, first] = jnp.where(tab_ref[b, first] < 0, NEG_INF, acc)

    return pl.pallas_call(
        body,
        grid=(N_HEADS,),
        in_specs=[pl.BlockSpec(memory_space=pltpu.SMEM),
                  pl.BlockSpec((3, 2, BAND, 2 * BAND), lambda h: (0, 0, 0, 0))],
        out_specs=pl.BlockSpec((1, 3, 2, BAND, 2 * BAND), lambda h: (h, 0, 0, 0, 0)),
        out_shape=jax.ShapeDtypeStruct((N_HEADS, 3, 2, BAND, 2 * BAND), F32),
        compiler_params=_params(("arbitrary",)),
        name="attn_bias",
    )(rel_bias, tables)


def _bias_spec():
    return pl.BlockSpec((2, 3, 2, BAND, 2 * BAND), lambda p, t: (p, 0, 0, 0, 0))


def _attn_block_index(idx, t, r):
    nb = ATT_TILE // (BAND * r)
    rho = idx // nb
    n = idx % nb
    qs = rho + r * BAND * n
    gs = t * ATT_TILE + qs
    first = (t * nb + n) == 0
    ps = jnp.where(first, gs, gs - r * BAND)
    return qs, gs, ps, first.astype(jnp.int32)


def _rows(start, r):
    return pl.ds(start, BAND) if r == 1 else pl.ds(start, BAND, stride=r)


def _attention_fwd(qkv, bias, shards):
    seq = qkv.shape[0]
    n_tiles = seq // ATT_TILE
    n = len(shards)

    def body(*refs):
        bias_ref, q_ref, k_ref, v_ref = refs[:4]
        y_ref, lse_ref = refs[4 + n:6 + n]
        o_s, l_s = refs[6 + 2 * n:8 + 2 * n]
        riding = (refs[4:4 + n], refs[6 + n:6 + 2 * n], *refs[8 + 2 * n:])
        pair = pl.program_id(0)
        t = pl.program_id(1)
        if n:
            @pl.when((pair == 0) & (t == 0))
            def _():
                for cp in _gather_copies(*riding, hand_over=False)[0]:
                    cp.start()

            @pl.when((pair == 2) & (t == 0))
            def _():
                for cp, fwd in zip(*_gather_copies(*riding)):
                    cp.wait_recv()
                    fwd.start()

        lane = lax.broadcasted_iota(jnp.int32, (1, LANES), 1)
        head0 = lane < HEAD_DIM
        masks = (head0, jnp.logical_not(head0))
        ones = jnp.ones((2 * BAND, LANES), BF16)
        for b, (_, r) in enumerate(BRANCHES):
            def blocks(it, carry, b=b, r=r):
                idx = [_attn_block_index(it * ATT_UNROLL + j, t, r) for j in range(ATT_UNROLL)]
                qb = [q_ref[_rows(qs, r), :] * (HEAD_DIM ** -0.5) for qs, _, _, _ in idx]
                kcat = [jnp.concatenate([k_ref[_rows(ps, r), :], k_ref[_rows(gs, r), :]], axis=0).astype(BF16)
                        for _, gs, ps, _ in idx]
                vcat = [jnp.concatenate([v_ref[_rows(ps, r), :], v_ref[_rows(gs, r), :]], axis=0).astype(BF16)
                        for _, gs, ps, _ in idx]
                work = [(j, hh) for j in range(ATT_UNROLL) for hh in range(2)]
                s = [_nt(jnp.where(masks[hh], qb[j], 0.0).astype(BF16), kcat[j]) + bias_ref[hh, b, idx[j][3]]
                     for j, hh in work]
                m = [jnp.max(sv, axis=-1, keepdims=True) for sv in s]
                e = [jnp.exp(sv - mv) for sv, mv in zip(s, m)]
                eb = [ev.astype(BF16) for ev in e]
                den = [_nn(ev, ones) for ev in eb]
                out = [_nn(ev, vcat[j]) / dv for ev, dv, (j, _) in zip(eb, den, work)]
                lse = [mv + jnp.log(dv) for mv, dv in zip(m, den)]
                for j in range(ATT_UNROLL):
                    o_s[b, _rows(idx[j][0], r), :] = jnp.where(head0, out[2 * j], out[2 * j + 1])
                    l_s[b, _rows(idx[j][0], r), :] = jnp.where(head0, lse[2 * j], lse[2 * j + 1])
                return carry

            lax.fori_loop(0, ATT_TILE // BAND // ATT_UNROLL, blocks, 0)

        def merge(i, carry):
            rows = pl.ds(pl.multiple_of(i * BAND, BAND), BAND)
            l0, l1, l2 = l_s[0, rows, :], l_s[1, rows, :], l_s[2, rows, :]
            m = jnp.maximum(jnp.maximum(l0, l1), l2)
            w0, w1, w2 = jnp.exp(l0 - m), jnp.exp(l1 - m), jnp.exp(l2 - m)
            tot = w0 + w1 + w2
            y_ref[rows, :] = (w0 * o_s[0, rows, :] + w1 * o_s[1, rows, :] + w2 * o_s[2, rows, :]) / tot
            lse_ref[rows, :] = m + jnp.log(tot)
            return carry

        lax.fori_loop(0, ATT_TILE // BAND, merge, 0)

        if n:
            @pl.when((pair == N_HEADS // 2 - 1) & (t == n_tiles - 1))
            def _():
                first, passed = _gather_copies(*riding)
                for cp in first:
                    cp.wait_send()
                for fwd in passed:
                    fwd.wait()

    tile = pl.BlockSpec((ATT_TILE, LANES), lambda p, t: (t, p))
    sems = [pltpu.SemaphoreType.DMA((6 * n,)), pltpu.SemaphoreType.DMA((6 * n,))] if n else []
    return pl.pallas_call(
        body,
        grid=(N_HEADS // 2, n_tiles),
        in_specs=[
            _bias_spec(),
            pl.BlockSpec((ATT_TILE, LANES), lambda p, t: (t, p)),
            pl.BlockSpec((seq, LANES), lambda p, t: (0, 4 + p)),
            pl.BlockSpec((seq, LANES), lambda p, t: (0, 8 + p)),
        ] + [ANY] * n,
        out_specs=[tile, tile] + [ANY] * n,
        out_shape=[jax.ShapeDtypeStruct((seq, HEAD_W), F32), jax.ShapeDtypeStruct((seq, HEAD_W), F32)]
        + _gathered_shapes(shards),
        scratch_shapes=[
            pltpu.VMEM((3, ATT_TILE, LANES), F32),
            pltpu.VMEM((3, ATT_TILE, LANES), F32),
        ] + sems,
        compiler_params=_params(("arbitrary", "arbitrary")),
        name="attn_fwd",
    )(bias, qkv, qkv, qkv, *shards)


def _attention_bwd(qkv, dy, y, lse, bias, partials):
    seq = qkv.shape[0]
    n_tiles = seq // ATT_TILE
    n = len(partials)

    def body(*refs):
        bias_ref, q_ref, k_ref, v_ref, dy_ref, y_ref, lse_ref = refs[:7]
        dq_ref, dk_ref, dv_ref, dbias_ref = refs[7 + n:11 + n]
        riding = (refs[7:7 + n], refs[11 + n:11 + 2 * n], *refs[11 + 2 * n:])
        pair = pl.program_id(0)
        t = pl.program_id(1)
        if n:
            @pl.when((pair == 0) & (t == 0))
            def _():
                for cp in _scatter_copies(*riding):
                    cp.start()

        lane = lax.broadcasted_iota(jnp.int32, (1, LANES), 1)
        head0 = lane < HEAD_DIM

        @pl.when(t == 0)
        def _():
            dk_ref[...] = jnp.zeros_like(dk_ref)
            dv_ref[...] = jnp.zeros_like(dv_ref)
            dbias_ref[...] = jnp.zeros_like(dbias_ref)

        dq_ref[...] = jnp.zeros_like(dq_ref)

        masks = (head0, jnp.logical_not(head0))
        ones = jnp.ones((LANES, LANES), BF16)
        scale = HEAD_DIM ** -0.5
        for b, (_, r) in enumerate(BRANCHES):
            def blocks(it, carry, b=b, r=r):
                idx = [_attn_block_index(it * ATT_UNROLL_BWD + j, t, r) for j in range(ATT_UNROLL_BWD)]
                qb = [q_ref[_rows(qs, r), :] * scale for qs, _, _, _ in idx]
                kcat = [jnp.concatenate([k_ref[_rows(ps, r), :], k_ref[_rows(gs, r), :]], axis=0).astype(BF16)
                        for _, gs, ps, _ in idx]
                vcat = [jnp.concatenate([v_ref[_rows(ps, r), :], v_ref[_rows(gs, r), :]], axis=0).astype(BF16)
                        for _, gs, ps, _ in idx]
                dob = [dy_ref[_rows(qs, r), :] for qs, _, _, _ in idx]
                ob = [y_ref[_rows(qs, r), :] for qs, _, _, _ in idx]
                lb = [lse_ref[_rows(qs, r), :] for qs, _, _, _ in idx]
                work = [(j, hh) for j in range(ATT_UNROLL_BWD) for hh in range(2)]
                qh = [jnp.where(masks[hh], qb[j], 0.0).astype(BF16) for j, hh in work]
                doh = [jnp.where(masks[hh], dob[j], 0.0) for j, hh in work]
                dohb = [d.astype(BF16) for d in doh]
                s = [_nt(qh[w], kcat[j]) + bias_ref[hh, b, idx[j][3]] for w, (j, hh) in enumerate(work)]
                dp = [_nt(dohb[w], vcat[j]) for w, (j, _) in enumerate(work)]
                lrot = [pltpu.roll(lv, HEAD_DIM, 1) for lv in lb]
                lcol = [jnp.where(masks[hh], lb[j], lrot[j]) for j, hh in work]
                parts = [_split(doh[w] * ob[j]) for w, (j, _) in enumerate(work)]
                delta = [_nn(hi, ones) + _nn(lo, ones) for hi, lo in parts]
                prob = [jnp.exp(sv - jnp.concatenate([lv, lv], axis=1)) for sv, lv in zip(s, lcol)]
                ds = [pv * (dv - jnp.concatenate([de, de], axis=1)) for pv, dv, de in zip(prob, dp, delta)]
                dsb = [d.astype(BF16) for d in ds]
                dq = [_nn(dsb[w], kcat[j]) for w, (j, _) in enumerate(work)]
                dkc = [_tn(dsb[w], qh[w]) for w in range(len(work))]
                dvc = [_tn(prob[w].astype(BF16), dohb[w]) for w in range(len(work))]
                for hh in range(2):
                    dbias_ref[0, b, hh] += sum(ds[w] for w, (_, head) in enumerate(work) if head == hh)
                for j in range(ATT_UNROLL_BWD):
                    qs, gs, ps, _ = idx[j]
                    dkcat = dkc[2 * j] + dkc[2 * j + 1]
                    dvcat = dvc[2 * j] + dvc[2 * j + 1]
                    dq_ref[_rows(qs, r), :] += jnp.where(head0, dq[2 * j], dq[2 * j + 1]) * scale
                    dk_ref[_rows(ps, r), :] += dkcat[:BAND]
                    dk_ref[_rows(gs, r), :] += dkcat[BAND:]
                    dv_ref[_rows(ps, r), :] += dvcat[:BAND]
                    dv_ref[_rows(gs, r), :] += dvcat[BAND:]
                return carry

            lax.fori_loop(0, ATT_TILE // BAND // ATT_UNROLL_BWD, blocks, 0)

        if n:
            @pl.when((pair == N_HEADS // 2 - 1) & (t == n_tiles - 1))
            def _():
                for cp in _scatter_copies(*riding):
                    cp.wait()

    tile = pl.BlockSpec((ATT_TILE, LANES), lambda p, t: (t, p))
    full = pl.BlockSpec((seq, LANES), lambda p, t: (0, p))
    sems = [pltpu.SemaphoreType.DMA((3 * n,)), pltpu.SemaphoreType.DMA((3 * n,))] if n else []
    return pl.pallas_call(
        body,
        grid=(N_HEADS // 2, n_tiles),
        in_specs=[
            _bias_spec(),
            pl.BlockSpec((ATT_TILE, LANES), lambda p, t: (t, p)),
            pl.BlockSpec((seq, LANES), lambda p, t: (0, 4 + p)),
            pl.BlockSpec((seq, LANES), lambda p, t: (0, 8 + p)),
            tile, tile, tile,
        ] + [ANY] * n,
        out_specs=[tile, full, full,
                   pl.BlockSpec((1, 3, 2, BAND, 2 * BAND), lambda p, t: (p, 0, 0, 0, 0))] + [ANY] * n,
        out_shape=[jax.ShapeDtypeStruct((seq, HEAD_W), F32)] * 3
        + [jax.ShapeDtypeStruct((N_HEADS // 2, 3, 2, BAND, 2 * BAND), F32)]
        + [jax.ShapeDtypeStruct(p.shape, p.dtype) for p in partials],
        scratch_shapes=sems,
        compiler_params=_params(("arbitrary", "arbitrary")),
        name="attn_bwd",
    )(bias, qkv, qkv, qkv, dy, y, lse, *partials)


def _rel_bias_grad(dbias, tables):
    def body(tab_ref, db_ref, out_ref):
        lane = lax.broadcasted_iota(jnp.int32, (1, LANES), 1)
        out_ref[...] = jnp.zeros_like(out_ref)
        for b in range(3):
            tab = tab_ref[b, 0]

            def head(h, carry, b=b, tab=tab):
                d = db_ref[h // 2, b, h % 2]
                sums = [jnp.sum(jnp.where(tab == kk, d, 0.0), keepdims=True) for kk in range(N_BUCKETS)]
                row = jnp.zeros((1, LANES), F32)
                for kk, s in enumerate(sums):
                    row = row + jnp.where(lane == kk, s, 0.0)
                out_ref[pl.ds(h, 1), :] += row
                return carry

            lax.fori_loop(0, N_HEADS, head, 0)

    return pl.pallas_call(
        body,
        out_shape=jax.ShapeDtypeStruct((N_HEADS, LANES), F32),
        compiler_params=pltpu.CompilerParams(vmem_limit_bytes=VMEM_LIMIT),
        name="rel_bias_grad",
    )(tables, dbias)


ROW_TILE = 512


def _head_sum_matrix():
    return (lax.broadcasted_iota(jnp.int32, (HEAD_W, HEAD_W), 0) // HEAD_DIM
            == lax.broadcasted_iota(jnp.int32, (HEAD_W, HEAD_W), 1) // HEAD_DIM).astype(F32)


def _head_spread_matrix(offset=0):
    return (lax.broadcasted_iota(jnp.int32, (LANES, HEAD_W), 0)
            == lax.broadcasted_iota(jnp.int32, (LANES, HEAD_W), 1) // HEAD_DIM + offset).astype(F32)


def _head_gather_matrix(offset=0):
    return (lax.broadcasted_iota(jnp.int32, (HEAD_W, LANES), 0) // HEAD_DIM + offset
            == lax.broadcasted_iota(jnp.int32, (HEAD_W, LANES), 1)).astype(F32)


def _split3(x):
    hi = x.astype(BF16)
    rest = x - hi.astype(F32)
    mid = rest.astype(BF16)
    return hi, mid, (rest - mid.astype(F32)).astype(BF16)


def _pick(x, onehot):
    m = onehot.astype(BF16)
    hi, mid, lo = _split3(x)
    return _nn(hi, m) + (_nn(mid, m) + _nn(lo, m))


def _pick_left(onehot, x):
    m = onehot.astype(BF16)
    hi, mid, lo = _split3(x)
    return _nn(m, hi) + (_nn(m, mid) + _nn(m, lo))


def _tri(lower, strict=False):
    r = lax.broadcasted_iota(jnp.int32, (CHUNK, CHUNK), 0)
    c = lax.broadcasted_iota(jnp.int32, (CHUNK, CHUNK), 1)
    if lower:
        return (c < r) if strict else (c <= r)
    return c >= r


def _softplus(z):
    return jnp.maximum(z, 0.0) + jnp.log(1.0 + jnp.exp(-jnp.abs(z)))


def _conv_taps(stage, w_ref, rows):
    return (w_ref[3:4, :] * stage[8:8 + rows, :] + w_ref[2:3, :] * stage[7:7 + rows, :]
            + w_ref[1:2, :] * stage[6:6 + rows, :] + w_ref[0:1, :] * stage[5:5 + rows, :])


def _l2_scale(xc, hsum):
    return lax.rsqrt(_pick(xc * xc, hsum) + EPS)


def _stage_rows(stage, x_ref, xp_ref, i):
    stage[0:8, :] = jnp.where(i == 0, 0.0, xp_ref[...])
    stage[8:8 + ROW_TILE, :] = x_ref[...]


def _delta_prep_fwd(qkvz, ba, conv_w, alog_row, dt_row):
    seq = qkvz.shape[0]
    qkv_w = 3 * HEAD_W

    def body(x_ref, xp_ref, ba_ref, w_ref, al_ref, dt_ref, out_ref, stage):
        i = pl.program_id(0)
        _stage_rows(stage, x_ref, xp_ref, i)
        act = _silu(_conv_taps(stage, w_ref, ROW_TILE))
        hsum, hspread = _head_sum_matrix(), _head_spread_matrix()
        qc, kc = act[:, :HEAD_W], act[:, HEAD_W:2 * HEAD_W]
        out_ref[0] = qc * _l2_scale(qc, hsum) * (HEAD_DIM ** -0.5)
        out_ref[1] = kc * _l2_scale(kc, hsum)
        out_ref[2] = act[:, 2 * HEAD_W:]
        bav = ba_ref[...]
        out_ref[3] = _pick(_sigmoid(bav), hspread)
        g8 = -jnp.exp(al_ref[...]) * _softplus(bav + dt_ref[...])
        gb = _pick(g8, _head_spread_matrix(N_HEADS))
        cum = _tri(True).astype(F32)
        for ch in range(ROW_TILE // CHUNK):
            rows = slice(ch * CHUNK, (ch + 1) * CHUNK)
            out_ref[4, rows, :] = _pick_left(cum, gb[rows])

    return pl.pallas_call(
        body,
        grid=(seq // ROW_TILE,),
        in_specs=[
            pl.BlockSpec((ROW_TILE, qkv_w), lambda i: (i, 0)),
            pl.BlockSpec((8, qkv_w), lambda i: (jnp.maximum(i * (ROW_TILE // 8) - 1, 0), 0)),
            pl.BlockSpec((ROW_TILE, LANES), lambda i: (i, 0)),
            pl.BlockSpec((4, qkv_w), lambda i: (0, 0)),
            pl.BlockSpec((1, LANES), lambda i: (0, 0)),
            pl.BlockSpec((1, LANES), lambda i: (0, 0)),
        ],
        out_specs=pl.BlockSpec((5, ROW_TILE, HEAD_W), lambda i: (0, i, 0)),
        out_shape=jax.ShapeDtypeStruct((5, seq, HEAD_W), F32),
        scratch_shapes=[pltpu.VMEM((ROW_TILE + 8, qkv_w), F32)],
        compiler_params=_params(("arbitrary",)),
        name="delta_prep_fwd",
    )(qkvz, qkvz, ba, conv_w, alog_row, dt_row)


def _split(x):
    hi = x.astype(BF16)
    return hi, (x - hi.astype(F32)).astype(BF16)


def _dot3(a, b, dot=_nn):
    return dot(a[0], b[0]) + (dot(a[0], b[1]) + dot(a[1], b[0]))


def _unit_lower_inverses(mats):
    eye = (lax.broadcasted_iota(jnp.int32, (CHUNK, CHUNK), 0)
           == lax.broadcasted_iota(jnp.int32, (CHUNK, CHUNK), 1)).astype(F32)
    invs = [eye - a for a in mats]
    powers = [_split(a) for a in mats]
    for step in range(5):
        squares = [_dot3(p, p) for p in powers]
        powers = [_split(s) for s in squares]
        invs = [inv + _dot3(_split(inv), p) for inv, p in zip(invs, powers)]
    return invs


def _chunk_terms(q, k, v, beta, gc):
    causal, strict = _tri(True), _tri(True, strict=True)
    e = jnp.exp(gc)
    g_last = jnp.broadcast_to(gc[CHUNK - 1:CHUNK, :], (CHUNK, CHUNK))
    f = jnp.exp(g_last - gc)
    e_last = jnp.exp(g_last)
    decay = jnp.where(causal, jnp.exp(jnp.where(causal, gc - gc.T, 0.0)), 0.0)
    kb = k * beta
    a_mat = jnp.where(strict, _nt(kb.astype(BF16), k.astype(BF16)) * decay, 0.0)
    qk = jnp.where(causal, _nt(q.astype(BF16), k.astype(BF16)) * decay, 0.0)
    return e, f, e_last, decay, kb, a_mat, qk


GROUP = 8
UNROLL = 8


def _chunk_rows(ci):
    return pl.ds(pl.multiple_of(ci * CHUNK, CHUNK), CHUNK)


def _pair_specs(n_planes):
    return pl.BlockSpec((n_planes, GROUP * CHUNK, LANES), lambda p, g: (0, g, p))


def _delta_chunk_fwd(xs):
    seq = xs.shape[1]
    rows_per_step = GROUP * CHUNK

    def body(x_ref, inv_ref, qk_ref, u_ref, w_ref):
        work = [(hh, slice(step * CHUNK, (step + 1) * CHUNK)) for hh in range(2) for step in range(GROUP)]
        xh = [[x_ref[j, r, hh * HEAD_DIM:(hh + 1) * HEAD_DIM] for j in range(5)] for hh, r in work]
        terms = [_chunk_terms(*x) for x in xh]
        invs = _unit_lower_inverses([t[5] for t in terms])
        for (hh, r), x, t, inv in zip(work, xh, terms, invs):
            e, kb, qk = t[0], t[4], t[6]
            inv_parts = _split(inv)
            inv_ref[hh, r, :] = inv
            qk_ref[hh, r, :] = qk
            u_ref[hh, r, :] = _dot3(inv_parts, _split(x[2] * x[3]))
            w_ref[hh, r, :] = _dot3(inv_parts, _split(kb * e))

    out = pl.BlockSpec((2, rows_per_step, HEAD_DIM), lambda p, g: (p, g, 0))
    return pl.pallas_call(
        body,
        grid=(N_HEADS // 2, seq // rows_per_step),
        in_specs=[_pair_specs(5)],
        out_specs=[out] * 4,
        out_shape=[jax.ShapeDtypeStruct((N_HEADS, seq, HEAD_DIM), F32)] * 4,
        compiler_params=_params(("parallel", "parallel")),
        name="delta_chunk_fwd",
    )(xs)


def _decays(gc):
    g_last = jnp.broadcast_to(gc[CHUNK - 1:CHUNK, :], (CHUNK, CHUNK))
    return jnp.exp(gc), jnp.exp(g_last - gc), jnp.exp(g_last)


def _token_blocks(index, n_steps=None):
    rows_per_step = GROUP * CHUNK
    if n_steps is None:
        return pl.BlockSpec((1, rows_per_step, HEAD_W), lambda g: (index, g, 0))
    return pl.BlockSpec((1, rows_per_step, HEAD_W), lambda g: (index, n_steps - 1 - g, 0))


def _head_lanes(h):
    return pl.ds(h * HEAD_DIM, HEAD_DIM)


def _delta_scan_fwd(xs, qk_h, u_h, w_h):
    seq = xs.shape[1]
    rows_per_step = GROUP * CHUNK

    def body(q_ref, k_ref, gc_ref, qk_ref, u_ref, w_ref, o_ref, st_ref, state):
        @pl.when(pl.program_id(0) == 0)
        def _():
            state[...] = jnp.zeros_like(state)

        def chunk(ci, carry):
            rows = _chunk_rows(ci)
            heads = range(N_HEADS)
            dec = [_decays(gc_ref[0, rows, _head_lanes(h)]) for h in heads]
            s = [state[h] for h in heads]
            sb = [s[h].astype(BF16) for h in heads]
            vnb = [(u_ref[h, rows, :] - _nn(w_ref[h, rows, :].astype(BF16), sb[h])).astype(BF16) for h in heads]
            for h in heads:
                o_ref[rows, _head_lanes(h)] = (_nn((q_ref[0, rows, _head_lanes(h)] * dec[h][0]).astype(BF16), sb[h])
                                               + _nn(qk_ref[h, rows, :].astype(BF16), vnb[h]))
                st_ref[h, rows, :] = s[h]
            for h in heads:
                state[h] = s[h] * dec[h][2] + _tn((k_ref[0, rows, _head_lanes(h)] * dec[h][1]).astype(BF16), vnb[h])
            return carry

        lax.fori_loop(0, GROUP, chunk, 0)

    blk = pl.BlockSpec((N_HEADS, rows_per_step, HEAD_DIM), lambda g: (0, g, 0))
    return pl.pallas_call(
        body,
        grid=(seq // rows_per_step,),
        in_specs=[_token_blocks(0), _token_blocks(1), _token_blocks(4), blk, blk, blk],
        out_specs=[pl.BlockSpec((rows_per_step, HEAD_W), lambda g: (g, 0)), blk],
        out_shape=[jax.ShapeDtypeStruct((seq, HEAD_W), F32), jax.ShapeDtypeStruct((N_HEADS, seq, HEAD_DIM), F32)],
        scratch_shapes=[pltpu.VMEM((N_HEADS, CHUNK, CHUNK), F32)],
        compiler_params=_params(("arbitrary",)),
        name="delta_scan_fwd",
    )(xs, xs, xs, qk_h, u_h, w_h)


def _delta_scan_bwd(xs, qk_h, w_h, do):
    seq = xs.shape[1]
    rows_per_step = GROUP * CHUNK
    n_steps = seq // rows_per_step

    def body(q_ref, k_ref, gc_ref, qk_ref, w_ref, do_ref, dsn_ref, dvn_ref, dstate):
        @pl.when(pl.program_id(0) == 0)
        def _():
            dstate[...] = jnp.zeros_like(dstate)

        def chunk(step, carry):
            rows = _chunk_rows(GROUP - 1 - step)
            heads = range(N_HEADS)
            dec = [_decays(gc_ref[0, rows, _head_lanes(h)]) for h in heads]
            ds_next = [dstate[h] for h in heads]
            dob = [do_ref[rows, _head_lanes(h)].astype(BF16) for h in heads]
            dv_new = [_tn(qk_ref[h, rows, :].astype(BF16), dob[h])
                      + _nn((k_ref[0, rows, _head_lanes(h)] * dec[h][1]).astype(BF16), ds_next[h].astype(BF16))
                      for h in heads]
            for h in heads:
                dsn_ref[h, rows, :] = ds_next[h]
                dvn_ref[h, rows, :] = dv_new[h]
            for h in heads:
                dstate[h] = (_tn((q_ref[0, rows, _head_lanes(h)] * dec[h][0]).astype(BF16), dob[h])
                             + dec[h][2] * ds_next[h] - _tn(w_ref[h, rows, :].astype(BF16), dv_new[h].astype(BF16)))
            return carry

        lax.fori_loop(0, GROUP, chunk, 0)

    blk = pl.BlockSpec((N_HEADS, rows_per_step, HEAD_DIM), lambda g: (0, n_steps - 1 - g, 0))
    return pl.pallas_call(
        body,
        grid=(n_steps,),
        in_specs=[_token_blocks(0, n_steps), _token_blocks(1, n_steps), _token_blocks(4, n_steps), blk, blk,
                  pl.BlockSpec((rows_per_step, HEAD_W), lambda g: (n_steps - 1 - g, 0))],
        out_specs=[blk, blk],
        out_shape=[jax.ShapeDtypeStruct((N_HEADS, seq, HEAD_DIM), F32)] * 2,
        scratch_shapes=[pltpu.VMEM((N_HEADS, CHUNK, CHUNK), F32)],
        compiler_params=_params(("arbitrary",)),
        name="delta_scan_bwd",
    )(xs, xs, xs, qk_h, w_h, do)


def _delta_chunk_bwd(xs, inv_h, u_h, w_h, st_h, dsn_h, dvn_h, do):
    seq = xs.shape[1]
    rows_per_step = GROUP * CHUNK

    def body(x_ref, inv_ref, u_ref, w_ref, st_ref, dsn_ref, dvn_ref, do_ref, dx_ref):
        causal, strict = _tri(True), _tri(True, strict=True)
        last_row = lax.broadcasted_iota(jnp.int32, (CHUNK, CHUNK), 0) == CHUNK - 1

        def bf(vals):
            return [val.astype(BF16) for val in vals]

        def group(items):
            heads = [hh for hh, _ in items]
            lanes = [slice(hh * HEAD_DIM, (hh + 1) * HEAD_DIM) for hh in heads]
            rows = [slice(step * CHUNK, (step + 1) * CHUNK) for _, step in items]
            n = range(len(items))
            q, k, v, beta, gc = [[x_ref[j, rows[i], lanes[i]] for i in n] for j in range(5)]
            terms = [_chunk_terms(q[i], k[i], v[i], beta[i], gc[i]) for i in n]
            e, f, e_last, decay, kb, a_mat, qk = [[t[j] for t in terms] for j in range(7)]
            inv = [_split(inv_ref[heads[i], rows[i], :]) for i in n]
            u = [u_ref[heads[i], rows[i], :] for i in n]
            w = [w_ref[heads[i], rows[i], :] for i in n]
            s = [st_ref[heads[i], rows[i], :] for i in n]
            ds_next = [dsn_ref[heads[i], rows[i], :] for i in n]
            dv_new = [dvn_ref[heads[i], rows[i], :] for i in n]
            sb, dsb, dvb, wb = bf(s), bf(ds_next), bf(dv_new), bf(w)
            dob = bf([do_ref[rows[i], lanes[i]] for i in n])
            qbf, kbf, kbb = bf(q), bf(k), bf(kb)
            vnb = bf([u[i] - _nn(wb[i], sb[i]) for i in n])
            dqe = [_nt(dob[i], sb[i]) for i in n]
            dw = [-_nt(dvb[i], sb[i]) for i in n]
            dkf = [_nt(vnb[i], dsb[i]) for i in n]
            dqk = [jnp.where(causal, _nt(dob[i], vnb[i]), 0.0) for i in n]
            drhs_u = [_dot3(inv[i], _split(dv_new[i]), _tn) for i in n]
            drhs_w = [_dot3(inv[i], _split(dw[i]), _tn) for i in n]
            da = [-jnp.where(strict, _nt(drhs_u[i].astype(BF16), u[i].astype(BF16))
                             + _nt(drhs_w[i].astype(BF16), wb[i]), 0.0) for i in n]
            dad = bf([da[i] * decay[i] for i in n])
            dqd = bf([dqk[i] * decay[i] for i in n])
            dkb = [e[i] * drhs_w[i] + _nn(dad[i], kbf[i]) for i in n]
            dk = [_tn(dad[i], kbb[i]) + _tn(dqd[i], qbf[i]) + f[i] * dkf[i] + beta[i] * dkb[i] for i in n]
            dq = [_nn(dqd[i], kbf[i]) + e[i] * dqe[i] for i in n]
            for i in n:
                de_full = kb[i] * drhs_w[i] + q[i] * dqe[i]
                df_full = k[i] * dkf[i]
                m = da[i] * a_mat[i] + dqk[i] * qk[i]
                dgc = de_full * e[i] - df_full * f[i] + m - m.T
                tail = jnp.sum(df_full * f[i] + s[i] * ds_next[i] * e_last[i], axis=0, keepdims=True)
                dgc = dgc + jnp.where(last_row, jnp.broadcast_to(tail, (CHUNK, CHUNK)), 0.0)
                dx_ref[0, rows[i], lanes[i]] = dq[i]
                dx_ref[1, rows[i], lanes[i]] = dk[i]
                dx_ref[2, rows[i], lanes[i]] = beta[i] * drhs_u[i]
                dx_ref[3, rows[i], lanes[i]] = v[i] * drhs_u[i] + k[i] * dkb[i]
                dx_ref[4, rows[i], lanes[i]] = dgc

        work = [(hh, step) for hh in range(2) for step in range(GROUP)]
        for first in range(0, len(work), UNROLL):
            group(work[first:first + UNROLL])

    blk = pl.BlockSpec((2, rows_per_step, HEAD_DIM), lambda p, g: (p, g, 0))
    return pl.pallas_call(
        body,
        grid=(N_HEADS // 2, seq // rows_per_step),
        in_specs=[_pair_specs(5)] + [blk] * 6 + [pl.BlockSpec((rows_per_step, LANES), lambda p, g: (g, p))],
        out_specs=_pair_specs(5),
        out_shape=jax.ShapeDtypeStruct((5, seq, HEAD_W), F32),
        compiler_params=_params(("parallel", "parallel")),
        name="delta_chunk_bwd",
    )(xs, inv_h, u_h, w_h, st_h, dsn_h, dvn_h, do)


def _delta_post_fwd(o, qkvz, gain_row):
    seq = o.shape[0]

    def body(o_ref, z_ref, g_ref, y_ref):
        ov = o_ref[...]
        rb = lax.rsqrt(_pick(ov * ov, _head_sum_matrix()) * (1.0 / HEAD_DIM) + EPS)
        y_ref[...] = (ov * rb * g_ref[...] * _silu(z_ref[...])).astype(y_ref.dtype)

    tile = pl.BlockSpec((ROW_TILE, HEAD_W), lambda i: (i, 0))
    return pl.pallas_call(
        body,
        grid=(seq // ROW_TILE,),
        in_specs=[tile, pl.BlockSpec((ROW_TILE, HEAD_W), lambda i: (i, 3)), pl.BlockSpec((1, HEAD_W), lambda i: (0, 0))],
        out_specs=tile,
        out_shape=jax.ShapeDtypeStruct((seq, HEAD_W), BF16),
        compiler_params=_params(("arbitrary",)),
        name="delta_post_fwd",
    )(o, qkvz, gain_row)


def _delta_post_bwd(dy, o, qkvz, gain_row, h):
    seq = o.shape[0]

    def body(dy_ref, o_ref, z_ref, g_ref, h_ref, do_ref, dz_ref, dg_ref, dwin_ref):
        @pl.when(pl.program_id(0) == 0)
        def _():
            dg_ref[...] = jnp.zeros_like(dg_ref)
            dwin_ref[...] = jnp.zeros_like(dwin_ref)

        ov, zv, dyv, gain = o_ref[...], z_ref[...], dy_ref[...], g_ref[...]
        hsum = _head_sum_matrix()
        rb = lax.rsqrt(_pick(ov * ov, hsum) * (1.0 / HEAD_DIM) + EPS)
        ohat = ov * rb
        silu_z, slope_z = _silu_and_slope(zv)
        dz = dyv * ohat * gain * slope_z
        dz_ref[...] = dz
        dwin_ref[...] += _tn(dz.astype(BF16), h_ref[...])
        dn = dyv * silu_z
        dg_ref[0:1, :] += jnp.sum(dn * ohat, axis=0, keepdims=True)
        dohat = dn * gain

        @pl.when(pl.program_id(0) == pl.num_programs(0) - 1)
        def _():
            fold = (lax.broadcasted_iota(jnp.int32, (HEAD_W, HEAD_W), 0) % HEAD_DIM
                    == lax.broadcasted_iota(jnp.int32, (HEAD_W, HEAD_W), 1)).astype(F32)
            dg_ref[1:2, :] = _pick(dg_ref[0:1, :], fold)

        proj = _pick(dohat * ohat, hsum) * (1.0 / HEAD_DIM)
        do_ref[...] = rb * (dohat - ohat * proj)

    tile = pl.BlockSpec((ROW_TILE, HEAD_W), lambda i: (i, 0))
    return pl.pallas_call(
        body,
        grid=(seq // ROW_TILE,),
        in_specs=[pl.BlockSpec((ROW_TILE, HEAD_W), lambda i: (i, 1)), tile,
                  pl.BlockSpec((ROW_TILE, HEAD_W), lambda i: (i, 3)), pl.BlockSpec((1, HEAD_W), lambda i: (0, 0)),
                  pl.BlockSpec((ROW_TILE, D_MODEL), lambda i: (i, 0))],
        out_specs=[tile, tile, pl.BlockSpec((2, HEAD_W), lambda i: (0, 0)),
                   pl.BlockSpec((HEAD_W, D_MODEL), lambda i: (0, 0))],
        out_shape=[jax.ShapeDtypeStruct((seq, HEAD_W), F32), jax.ShapeDtypeStruct((seq, HEAD_W), F32),
                   jax.ShapeDtypeStruct((2, HEAD_W), F32), jax.ShapeDtypeStruct((HEAD_W, D_MODEL), F32)],
        compiler_params=_params(("arbitrary",)),
        name="delta_post_bwd",
    )(dy, o, qkvz, gain_row, h)


def _delta_prep_bwd(qkvz, ba, conv_w, alog_row, dt_row, dxs):
    seq = qkvz.shape[0]
    qkv_w = 3 * HEAD_W

    def body(x_ref, xp_ref, ba_ref, w_ref, al_ref, dt_ref, dx_ref, dconv_ref, dba_ref, dvec_ref, stage):
        i = pl.program_id(0)

        @pl.when(i == 0)
        def _():
            dvec_ref[...] = jnp.zeros_like(dvec_ref)

        _stage_rows(stage, x_ref, xp_ref, i)
        pre = _conv_taps(stage, w_ref, ROW_TILE)
        act, slope = _silu_and_slope(pre)
        hsum = _head_sum_matrix()
        for j, scale in ((0, HEAD_DIM ** -0.5), (1, 1.0)):
            cols = slice(j * HEAD_W, (j + 1) * HEAD_W)
            xc = act[:, cols]
            rb = _l2_scale(xc, hsum)
            xhat = xc * rb
            dhat = dx_ref[j] * scale
            proj = _pick(dhat * xhat, hsum)
            dconv_ref[:, cols] = rb * (dhat - xhat * proj) * slope[:, cols]
        dconv_ref[:, 2 * HEAD_W:] = dx_ref[2] * slope[:, 2 * HEAD_W:]

        bav = ba_ref[...]
        beta8 = _sigmoid(bav)
        dbeta8 = _pick(dx_ref[3], _head_gather_matrix())
        dgc8 = _pick(dx_ref[4], _head_gather_matrix(N_HEADS))
        rev = _tri(False).astype(F32)
        z = bav + dt_ref[...]
        ea = jnp.exp(al_ref[...])
        g8 = -ea * _softplus(z)
        sig = _sigmoid(z)
        d_alog = jnp.zeros((1, LANES), F32)
        d_dt = jnp.zeros((1, LANES), F32)
        for ch in range(ROW_TILE // CHUNK):
            rows = slice(ch * CHUNK, (ch + 1) * CHUNK)
            dg8 = _pick_left(rev, dgc8[rows])
            da = -dg8 * ea * sig[rows]
            dba_ref[rows, :] = dbeta8[rows] * beta8[rows] * (1.0 - beta8[rows]) + da
            d_alog = d_alog + jnp.sum(dg8 * g8[rows], axis=0, keepdims=True)
            d_dt = d_dt + jnp.sum(da, axis=0, keepdims=True)
        dvec_ref[0:1, :] += d_alog
        dvec_ref[1:2, :] += d_dt

    return pl.pallas_call(
        body,
        grid=(seq // ROW_TILE,),
        in_specs=[
            pl.BlockSpec((ROW_TILE, qkv_w), lambda i: (i, 0)),
            pl.BlockSpec((8, qkv_w), lambda i: (jnp.maximum(i * (ROW_TILE // 8) - 1, 0), 0)),
            pl.BlockSpec((ROW_TILE, LANES), lambda i: (i, 0)),
            pl.BlockSpec((4, qkv_w), lambda i: (0, 0)),
            pl.BlockSpec((1, LANES), lambda i: (0, 0)),
            pl.BlockSpec((1, LANES), lambda i: (0, 0)),
            pl.BlockSpec((5, ROW_TILE, HEAD_W), lambda i: (0, i, 0)),
        ],
        out_specs=[pl.BlockSpec((ROW_TILE, qkv_w), lambda i: (i, 0)),
                   pl.BlockSpec((ROW_TILE, LANES), lambda i: (i, 0)),
                   pl.BlockSpec((2, LANES), lambda i: (0, 0))],
        out_shape=[jax.ShapeDtypeStruct((seq, qkv_w), F32), jax.ShapeDtypeStruct((seq, LANES), F32),
                   jax.ShapeDtypeStruct((2, LANES), F32)],
        scratch_shapes=[pltpu.VMEM((ROW_TILE + 8, qkv_w), F32)],
        compiler_params=_params(("arbitrary",)),
        name="delta_prep_bwd",
    )(qkvz, qkvz, ba, conv_w, alog_row, dt_row, dxs)


def _conv_bwd(dconv, qkvz, conv_w, h):
    seq = dconv.shape[0]
    qkv_w = 3 * HEAD_W
    n_tiles = seq // ROW_TILE

    def body(dy_ref, dyn_ref, x_ref, xp_ref, w_ref, h_ref, dx_ref, dw_ref, dwin_ref, stage, dstage):
        i = pl.program_id(0)

        @pl.when(i == 0)
        def _():
            dw_ref[...] = jnp.zeros_like(dw_ref)
            dwin_ref[...] = jnp.zeros_like(dwin_ref)

        _stage_rows(stage, x_ref, xp_ref, i)
        dstage[0:ROW_TILE, :] = dy_ref[...]
        dstage[ROW_TILE:ROW_TILE + 8, :] = jnp.where(i == n_tiles - 1, 0.0, dyn_ref[...])
        dy = dy_ref[...]
        dx = (w_ref[3:4, :] * dy + w_ref[2:3, :] * dstage[1:1 + ROW_TILE, :]
              + w_ref[1:2, :] * dstage[2:2 + ROW_TILE, :] + w_ref[0:1, :] * dstage[3:3 + ROW_TILE, :])
        dx_ref[...] = dx
        dwin_ref[...] += _tn(dx.astype(BF16), h_ref[...])
        for j in range(4):
            dw_ref[j:j + 1, :] += jnp.sum(dy * stage[5 + j:5 + j + ROW_TILE, :], axis=0, keepdims=True)

    tile = pl.BlockSpec((ROW_TILE, qkv_w), lambda i: (i, 0))
    return pl.pallas_call(
        body,
        grid=(n_tiles,),
        in_specs=[
            tile,
            pl.BlockSpec((8, qkv_w), lambda i: (jnp.minimum((i + 1) * (ROW_TILE // 8), seq // 8 - 1), 0)),
            tile,
            pl.BlockSpec((8, qkv_w), lambda i: (jnp.maximum(i * (ROW_TILE // 8) - 1, 0), 0)),
            pl.BlockSpec((4, qkv_w), lambda i: (0, 0)),
            pl.BlockSpec((ROW_TILE, D_MODEL), lambda i: (i, 0)),
        ],
        out_specs=[tile, pl.BlockSpec((4, qkv_w), lambda i: (0, 0)), pl.BlockSpec((qkv_w, D_MODEL), lambda i: (0, 0))],
        out_shape=[jax.ShapeDtypeStruct((seq, qkv_w), F32), jax.ShapeDtypeStruct((4, qkv_w), F32),
                   jax.ShapeDtypeStruct((qkv_w, D_MODEL), F32)],
        scratch_shapes=[pltpu.VMEM((ROW_TILE + 8, qkv_w), F32), pltpu.VMEM((ROW_TILE + 8, qkv_w), F32)],
        compiler_params=_params(("arbitrary",)),
        name="conv_bwd",
    )(dconv, dconv, qkvz, qkvz, conv_w, h)


FF_TILE = 1408
WGRAD_ROWS = 1024


def _row(a):
    return pl.BlockSpec((1, a), lambda *_: (0, 0))


def _rms_fwd(xv, gain):
    rstd = lax.rsqrt(jnp.mean(xv * xv, axis=-1, keepdims=True) + EPS)
    xhat = xv * rstd
    return xhat, rstd, xhat * gain


def _rms_bwd(dnorm, xhat, rstd, gain):
    dxhat = dnorm * gain
    dx = rstd * (dxhat - xhat * jnp.mean(dxhat * xhat, axis=-1, keepdims=True))
    return dx, jnp.sum(dnorm * xhat, axis=0, keepdims=True)


IN_SPLITS = (0, 3 * HEAD_W, 7 * HEAD_W, 7 * HEAD_W + LANES)


def _inproj_fwd(x, gain, scale, shift, w_rows):
    seq = x.shape[0]

    def body(x_ref, g_ref, sc_ref, sh_ref, w_ref, h_ref, a_ref, d_ref, b_ref):
        _, _, norm = _rms_fwd(x_ref[...], g_ref[...])
        h = (norm * (1.0 + sc_ref[...]) + sh_ref[...]).astype(BF16)
        h_ref[...] = h
        for out_ref, lo, hi in zip((a_ref, d_ref, b_ref), IN_SPLITS[:-1], IN_SPLITS[1:]):
            out_ref[...] = _nt(h, w_ref[lo:hi, :])

    def rows(width):
        return pl.BlockSpec((ROW_TILE, width), lambda i: (i, 0))

    return pl.pallas_call(
        body,
        grid=(seq // ROW_TILE,),
        in_specs=[rows(D_MODEL), _row(D_MODEL), _row(D_MODEL), _row(D_MODEL),
                  pl.BlockSpec(w_rows.shape, lambda i: (0, 0))],
        out_specs=[rows(D_MODEL), rows(3 * HEAD_W), rows(4 * HEAD_W), rows(LANES)],
        out_shape=[jax.ShapeDtypeStruct((seq, D_MODEL), BF16), jax.ShapeDtypeStruct((seq, 3 * HEAD_W), F32),
                   jax.ShapeDtypeStruct((seq, 4 * HEAD_W), F32), jax.ShapeDtypeStruct((seq, LANES), F32)],
        compiler_params=_params(("arbitrary",)),
        name="inproj_fwd",
    )(x, gain, scale, shift, w_rows)


def _outproj_fwd(y_attn, y_delta, w_out, x, gate1, gain, scale, shift):
    seq = x.shape[0]

    def body(ya_ref, yd_ref, wa_ref, wd_ref, x_ref, g1_ref, g_ref, sc_ref, sh_ref, x1_ref, h_ref, y_ref):
        y = _nn(ya_ref[...].astype(BF16), wa_ref[...]) + _nn(yd_ref[...], wd_ref[...])
        x1 = x_ref[...] + g1_ref[...] * y
        _, _, norm = _rms_fwd(x1, g_ref[...])
        x1_ref[...] = x1
        h_ref[...] = (norm * (1.0 + sc_ref[...]) + sh_ref[...]).astype(BF16)
        y_ref[...] = y.astype(BF16)

    def rows(width):
        return pl.BlockSpec((ROW_TILE, width), lambda i: (i, 0))

    return pl.pallas_call(
        body,
        grid=(seq // ROW_TILE,),
        in_specs=[rows(HEAD_W), rows(HEAD_W),
                  pl.BlockSpec((HEAD_W, D_MODEL), lambda i: (0, 0)), pl.BlockSpec((HEAD_W, D_MODEL), lambda i: (1, 0)),
                  rows(D_MODEL), _row(D_MODEL), _row(D_MODEL), _row(D_MODEL), _row(D_MODEL)],
        out_specs=[rows(D_MODEL), rows(D_MODEL), rows(D_MODEL)],
        out_shape=[jax.ShapeDtypeStruct((seq, D_MODEL), F32), jax.ShapeDtypeStruct((seq, D_MODEL), BF16),
                   jax.ShapeDtypeStruct((seq, D_MODEL), BF16)],
        compiler_params=_params(("arbitrary",)),
        name="outproj_fwd",
    )(y_attn, y_delta, w_out, w_out, x, gate1, gain, scale, shift)


def _ffn_fwd(h2, w_gate, w_up, w_down, x1, gate2, final_gain, target):
    seq = h2.shape[0]
    n_rows, n_ff = seq // ROW_TILE, D_FF // FF_TILE
    assert n_ff >= 2

    def body(h_ref, wg_ref, wu_ref, wd_ref, x1_ref, g2_ref, gf_ref, t_ref, gate_ref, up_ref, dx2_ref, st_ref, acc):
        i, j = pl.program_id(0), pl.program_id(1)

        @pl.when((i == 0) & (j == 0))
        def _():
            st_ref[...] = jnp.zeros_like(st_ref)

        h = h_ref[...]
        gate = _nt(h, wg_ref[...])
        up = _nt(h, wu_ref[...])
        gate_ref[...] = gate.astype(BF16)
        up_ref[...] = up.astype(BF16)
        part = _nn((_silu(gate) * up).astype(BF16), wd_ref[...])

        @pl.when(j == 0)
        def _():
            acc[...] = part

        @pl.when((j > 0) & (j < n_ff - 1))
        def _():
            acc[...] += part

        @pl.when(j == n_ff - 1)
        def _():
            y2 = acc[...] + part
            x2 = x1_ref[...] + g2_ref[...] * y2
            xhat, rstd, out = _rms_fwd(x2, gf_ref[...])
            diff = out - t_ref[...]
            dx2, dgain = _rms_bwd(diff * (1.0 / D_MODEL), xhat, rstd, gf_ref[...])
            dx2_ref[...] = dx2
            st_ref[0:1, :] += dgain
            st_ref[1:2, :] += jnp.sum(dx2 * y2, axis=0, keepdims=True)
            st_ref[2:3, :] += jnp.sum(diff * diff, axis=0, keepdims=True) * (0.5 / D_MODEL)

        @pl.when((i == n_rows - 1) & (j == n_ff - 1))
        def _():
            st_ref[3:4, :] = jnp.broadcast_to(jnp.sum(st_ref[2:3, :], keepdims=True), (1, D_MODEL))

    def rows(width):
        return pl.BlockSpec((ROW_TILE, width), lambda i, j: (i, 0))

    ff = pl.BlockSpec((ROW_TILE, FF_TILE), lambda i, j: (i, j))
    return pl.pallas_call(
        body,
        grid=(n_rows, n_ff),
        in_specs=[rows(D_MODEL),
                  pl.BlockSpec((FF_TILE, D_MODEL), lambda i, j: (j, 0)), pl.BlockSpec((FF_TILE, D_MODEL), lambda i, j: (j, 0)),
                  pl.BlockSpec((FF_TILE, D_MODEL), lambda i, j: (j, 0)),
                  rows(D_MODEL), _row(D_MODEL), _row(D_MODEL), rows(D_MODEL)],
        out_specs=[ff, ff, rows(D_MODEL), pl.BlockSpec((8, D_MODEL), lambda i, j: (0, 0))],
        out_shape=[jax.ShapeDtypeStruct((seq, D_FF), BF16), jax.ShapeDtypeStruct((seq, D_FF), BF16),
                   jax.ShapeDtypeStruct((seq, D_MODEL), F32), jax.ShapeDtypeStruct((8, D_MODEL), F32)],
        scratch_shapes=[pltpu.VMEM((ROW_TILE, D_MODEL), F32)],
        compiler_params=_params(("arbitrary", "arbitrary")),
        name="ffn_fwd",
    )(h2, w_gate, w_up, w_down, x1, gate2, final_gain, target)


def _ffn_bwd(dx2, gate, up, w_gate, w_up, w_down, x1, y, gate2, gate1, gain, scale):
    seq = dx2.shape[0]

    def act_body(dx2_ref, g2_ref, gate_ref, up_ref, wd_ref, dgate_ref, dup_ref, dwd_ref):
        dy2 = (g2_ref[...] * dx2_ref[...]).astype(BF16)
        gate = gate_ref[...].astype(F32)
        up = up_ref[...].astype(F32)
        dact = _nt(dy2, wd_ref[...])
        silu, slope = _silu_and_slope(gate)
        dgate_ref[...] = (dact * up * slope).astype(BF16)
        dup_ref[...] = (dact * silu).astype(BF16)
        part = _tn((silu * up).astype(BF16), dy2)

        @pl.when(pl.program_id(1) == 0)
        def _():
            dwd_ref[...] = part

        @pl.when(pl.program_id(1) > 0)
        def _():
            dwd_ref[...] += part

    ff = pl.BlockSpec((ROW_TILE, FF_TILE), lambda j, i: (i, j))
    w_tile = pl.BlockSpec((FF_TILE, D_MODEL), lambda j, i: (j, 0))
    dgate, dup, dw_down = pl.pallas_call(
        act_body,
        grid=(D_FF // FF_TILE, seq // ROW_TILE),
        in_specs=[pl.BlockSpec((ROW_TILE, D_MODEL), lambda j, i: (i, 0)), _row(D_MODEL), ff, ff, w_tile],
        out_specs=[ff, ff, w_tile],
        out_shape=[jax.ShapeDtypeStruct((seq, D_FF), BF16)] * 2 + [jax.ShapeDtypeStruct((D_FF, D_MODEL), F32)],
        compiler_params=_params(("arbitrary", "arbitrary")),
        name="ffn_bwd_act",
    )(dx2, gate2, gate, up, w_down)

    def in_body(dgate_ref, dup_ref, wg_ref, wu_ref, dx2_ref, x1_ref, y_ref, g1_ref, g_ref, sc_ref,
                dx1_ref, dy_ref, st_ref):
        @pl.when(pl.program_id(0) == 0)
        def _():
            st_ref[...] = jnp.zeros_like(st_ref)

        dh = _nn(dgate_ref[...], wg_ref[...]) + _nn(dup_ref[...], wu_ref[...])
        xhat, rstd, norm = _rms_fwd(x1_ref[...], g_ref[...])
        dxn, dgain = _rms_bwd(dh * (1.0 + sc_ref[...]), xhat, rstd, g_ref[...])
        dx1 = dx2_ref[...] + dxn
        dx1_ref[...] = dx1
        dy_ref[...] = (g1_ref[...] * dx1).astype(BF16)
        st_ref[0:1, :] += jnp.sum(dh, axis=0, keepdims=True)
        st_ref[1:2, :] += jnp.sum(dh * norm, axis=0, keepdims=True)
        st_ref[2:3, :] += dgain
        st_ref[3:4, :] += jnp.sum(dx1 * y_ref[...].astype(F32), axis=0, keepdims=True)

    half_tile = ROW_TILE // 2

    def rows(width):
        return pl.BlockSpec((half_tile, width), lambda i: (i, 0))

    whole = pl.BlockSpec((D_FF, D_MODEL), lambda i: (0, 0))
    dx1, dy, stats = pl.pallas_call(
        in_body,
        grid=(seq // half_tile,),
        in_specs=[rows(D_FF), rows(D_FF), whole, whole, rows(D_MODEL), rows(D_MODEL), rows(D_MODEL),
                  _row(D_MODEL), _row(D_MODEL), _row(D_MODEL)],
        out_specs=[rows(D_MODEL), rows(D_MODEL), pl.BlockSpec((8, D_MODEL), lambda i: (0, 0))],
        out_shape=[jax.ShapeDtypeStruct((seq, D_MODEL), F32), jax.ShapeDtypeStruct((seq, D_MODEL), BF16),
                   jax.ShapeDtypeStruct((8, D_MODEL), F32)],
        compiler_params=_params(("arbitrary",)),
        name="ffn_bwd_in",
    )(dgate, dup, w_gate, w_up, dx2, x1, y, gate1, gain, scale)
    return dgate, dup, dw_down, dx1, dy, stats


def _outproj_bwd(dy, w_out, y_attn, y_delta):
    seq = dy.shape[0]

    def body(dy_ref, w_ref, ya_ref, yd_ref, out_ref, dw_ref):
        @pl.when(pl.program_id(0) == 0)
        def _():
            dw_ref[...] = jnp.zeros_like(dw_ref)

        dyv = dy_ref[...]
        out_ref[...] = _nt(dyv, w_ref[...])
        dw_ref[0:HEAD_W, :] += _tn(ya_ref[...].astype(BF16), dyv)
        dw_ref[HEAD_W:, :] += _tn(yd_ref[...], dyv)

    rows = pl.BlockSpec((ROW_TILE, D_MODEL), lambda i: (i, 0))
    half = pl.BlockSpec((ROW_TILE, HEAD_W), lambda i: (i, 0))
    whole = pl.BlockSpec((D_MODEL, D_MODEL), lambda i: (0, 0))
    return pl.pallas_call(
        body,
        grid=(seq // ROW_TILE,),
        in_specs=[rows, whole, half, half],
        out_specs=[rows, whole],
        out_shape=[jax.ShapeDtypeStruct((seq, D_MODEL), F32), jax.ShapeDtypeStruct((D_MODEL, D_MODEL), F32)],
        compiler_params=_params(("arbitrary",)),
        name="outproj_bwd",
    )(dy, w_out, y_attn, y_delta)


def _inproj_bwd(dq, dk, dv, dxd, dz, dba, w_rows, x, dx1, gain, scale, partials):
    seq = x.shape[0]
    n = len(partials)
    n_steps = seq // ROW_TILE

    def body(*refs):
        pieces, (w_ref, x_ref, dx1_ref, g_ref, sc_ref) = refs[:6], refs[6:11]
        gx_ref, st_ref = refs[11 + n:13 + n]
        riding = (refs[11:11 + n], refs[13 + n:13 + 2 * n], *refs[13 + 2 * n:])

        @pl.when(pl.program_id(0) == 0)
        def _():
            st_ref[...] = jnp.zeros_like(st_ref)
            for cp in (_scatter_copies(*riding) if n else []):
                cp.start()

        dh = _nn(jnp.concatenate([p[...].astype(BF16) for p in pieces], axis=1), w_ref[...])
        xhat, rstd, norm = _rms_fwd(x_ref[...], g_ref[...])
        dxn, dgain = _rms_bwd(dh * (1.0 + sc_ref[...]), xhat, rstd, g_ref[...])
        gx_ref[...] = dx1_ref[...] + dxn
        st_ref[0:1, :] += jnp.sum(dh, axis=0, keepdims=True)
        st_ref[1:2, :] += jnp.sum(dh * norm, axis=0, keepdims=True)
        st_ref[2:3, :] += dgain

        if n:
            @pl.when(pl.program_id(0) == n_steps - 1)
            def _():
                for cp in _scatter_copies(*riding):
                    cp.wait()

    def rows(width):
        return pl.BlockSpec((ROW_TILE, width), lambda i: (i, 0))

    sems = [pltpu.SemaphoreType.DMA((3 * n,)), pltpu.SemaphoreType.DMA((3 * n,))] if n else []
    return pl.pallas_call(
        body,
        grid=(n_steps,),
        in_specs=[rows(HEAD_W), rows(HEAD_W), rows(HEAD_W), rows(3 * HEAD_W), rows(HEAD_W), rows(LANES),
                  pl.BlockSpec(w_rows.shape, lambda i: (0, 0)), rows(D_MODEL), rows(D_MODEL), _row(D_MODEL),
                  _row(D_MODEL)]
        + [ANY] * n,
        out_specs=[rows(D_MODEL), pl.BlockSpec((8, D_MODEL), lambda i: (0, 0))] + [ANY] * n,
        out_shape=[jax.ShapeDtypeStruct((seq, D_MODEL), F32), jax.ShapeDtypeStruct((8, D_MODEL), F32)]
        + [jax.ShapeDtypeStruct(p.shape, p.dtype) for p in partials],
        scratch_shapes=sems,
        compiler_params=_params(("arbitrary",)),
        name="inproj_bwd",
    )(dq, dk, dv, dxd, dz, dba, w_rows, x, dx1, gain, scale, *partials)


def _weight_grad(a, b, name):
    seq, m = a.shape
    n = b.shape[1]
    tm = m if m <= 1536 else m // 2
    tn = n if n <= 1536 else n // 2
    rows = 2 * WGRAD_ROWS
    n_k = seq // rows

    def body(a_ref, b_ref, out_ref):
        part = _tn(a_ref[...].astype(BF16), b_ref[...].astype(BF16))

        @pl.when(pl.program_id(2) == 0)
        def _():
            out_ref[...] = part

        @pl.when(pl.program_id(2) > 0)
        def _():
            out_ref[...] += part

    return pl.pallas_call(
        body,
        grid=(m // tm, n // tn, n_k),
        in_specs=[pl.BlockSpec((rows, tm), lambda i, j, k: (k, i)),
                  pl.BlockSpec((rows, tn), lambda i, j, k: (k, j))],
        out_specs=pl.BlockSpec((tm, tn), lambda i, j, k: (i, j)),
        out_shape=jax.ShapeDtypeStruct((m, n), F32),
        compiler_params=_params(("arbitrary", "arbitrary", "arbitrary")),
        name=name,
    )(a, b)


def _weight_grad_stack(pieces, b, name):
    seq, n = b.shape
    widths = [a.shape[1] for a in pieces]
    starts = [sum(widths[:i]) for i in range(len(pieces))]

    def body(*refs):
        a_refs, b_ref, out_ref = refs[:len(pieces)], refs[len(pieces)], refs[len(pieces) + 1]

        @pl.when(pl.program_id(0) == 0)
        def _():
            out_ref[...] = jnp.zeros_like(out_ref)

        bb = b_ref[...].astype(BF16)
        for a_ref, start, width in zip(a_refs, starts, widths):
            out_ref[start:start + width, :] += _tn(a_ref[...].astype(BF16), bb)

    def rows(width):
        return pl.BlockSpec((WGRAD_ROWS, width), lambda k: (k, 0))

    return pl.pallas_call(
        body,
        grid=(seq // WGRAD_ROWS,),
        in_specs=[rows(w) for w in widths] + [rows(n)],
        out_specs=pl.BlockSpec((sum(widths), n), lambda k: (0, 0)),
        out_shape=jax.ShapeDtypeStruct((sum(widths), n), F32),
        compiler_params=_params(("arbitrary",)),
        name=name,
    )(*pieces, b)


def _adamw(w, g, m, v, name):
    n_rows, n_cols = w.shape
    if w.size <= 64 * 1024:
        block, grid, index = (n_rows, n_cols), (1,), lambda i: (0, 0)
    elif n_rows % 256 == 0:
        block, grid, index = (256, n_cols), (n_rows // 256,), lambda i: (i, 0)
    elif n_cols % 256 == 0:
        block, grid, index = (n_rows, 256), (n_cols // 256,), lambda i: (0, i)
    else:
        block, grid, index = (n_rows, n_cols), (1,), lambda i: (0, 0)

    def body(w_ref, g_ref, m_ref, v_ref, d_ref, nm_ref, nv_ref):
        gv = g_ref[...]
        nm = ADAM_B1 * m_ref[...] + (1.0 - ADAM_B1) * gv
        nv = ADAM_B2 * v_ref[...] + (1.0 - ADAM_B2) * (gv * gv)
        m_hat = nm / (1.0 - ADAM_B1 ** ADAM_STEP)
        v_hat = nv / (1.0 - ADAM_B2 ** ADAM_STEP)
        d_ref[...] = -ADAM_LR * (m_hat / (jnp.sqrt(v_hat) + ADAM_EPS) + ADAM_WD * w_ref[...])
        nm_ref[...] = nm
        nv_ref[...] = nv

    blk = pl.BlockSpec(block, index)
    shape = jax.ShapeDtypeStruct((n_rows, n_cols), F32)
    return pl.pallas_call(
        body,
        grid=grid,
        in_specs=[blk] * 4,
        out_specs=[blk] * 3,
        out_shape=[shape] * 3,
        compiler_params=_params(("arbitrary",)),
        name=name,
    )(w, g, m, v)


IN_WIDTH = 3600


def _local_step(x, target, mod, norm_attn_g, w_in, rel_bias, conv_w, a_log, dt_bias, delta_norm_g,
                norm_ffn_g, final_norm_g, shards, assemble, reduce_pairs):
    sh1, sc1, g1, sh2, sc2, g2 = [mod[:, i * D_MODEL:(i + 1) * D_MODEL] for i in range(6)]
    w_rows = jnp.pad(w_in, ((0, IN_SPLITS[-1] - IN_WIDTH), (0, 0)))
    tables = jnp.asarray(_attn_tables())
    alog_row = jnp.pad(a_log, ((0, 0), (N_HEADS, LANES - 2 * N_HEADS)))
    dt_row = jnp.pad(dt_bias, ((0, 0), (N_HEADS, LANES - 2 * N_HEADS)))
    gain_row = jnp.tile(delta_norm_g, (1, N_HEADS))

    h1, qkv_a, qkvz, ba = _inproj_fwd(x, norm_attn_g, sc1, sh1, w_rows)
    bias = _attention_bias(rel_bias, tables)
    y_attn, lse, *gathered = _attention_fwd(qkv_a, bias, shards)
    w_out, w_gate, w_up, w_down = assemble(gathered)
    xs = _delta_prep_fwd(qkvz, ba, conv_w, alog_row, dt_row)
    inv_h, qk_h, u_h, w_h = _delta_chunk_fwd(xs)
    o, st_h = _delta_scan_fwd(xs, qk_h, u_h, w_h)
    y_delta = _delta_post_fwd(o, qkvz, gain_row)
    x1, h2, y = _outproj_fwd(y_attn, y_delta, w_out, x, g1, norm_ffn_g, sc2, sh2)
    gate, up, dx2, st_f = _ffn_fwd(h2, w_gate, w_up, w_down, x1, g2, final_norm_g, target)

    dgate, dup, dw_down, dx1, dy, st_b = _ffn_bwd(dx2, gate, up, w_gate, w_up, w_down, x1, y, g2, g1, norm_ffn_g, sc2)
    dycat, dw_out = _outproj_bwd(dy, w_out, y_attn, y_delta)
    partials = reduce_pairs([dw_out, _weight_grad(dgate, h2, "wgrad_gate"), _weight_grad(dup, h2, "wgrad_up"),
                             dw_down], 1, "rest")
    grads = {}
    do, dz, dgain, dw_in_z = _delta_post_bwd(dycat, o, qkvz, gain_row, h1)
    dsn_h, dvn_h = _delta_scan_bwd(xs, qk_h, w_h, do)
    dxs = _delta_chunk_bwd(xs, inv_h, u_h, w_h, st_h, dsn_h, dvn_h, do)
    dconv, dba, dvec = _delta_prep_bwd(qkvz, ba, conv_w, alog_row, dt_row, dxs)
    dxd, grads["conv_w"], dw_in_delta = _conv_bwd(dconv, qkvz, conv_w, h1)
    dq, dk, dv, dbias, *scattered = _attention_bwd(qkv_a, dycat, y_attn, lse, bias, partials)
    partials_in = reduce_pairs([jnp.concatenate(
        [_weight_grad_stack([dq, dk, dv], h1, "wgrad_in_attn"), dw_in_delta, dw_in_z,
         _weight_grad_stack([dba], h1, "wgrad_in_gates")[:2 * N_HEADS]], axis=0)], 0, "in")
    grad_x, st_i, *scattered_in = _inproj_bwd(dq, dk, dv, dxd, dz, dba, w_rows, x, dx1, norm_attn_g, sc1,
                                              partials_in)
    grads["rel_bias"] = _rel_bias_grad(dbias, tables)[:, :N_BUCKETS].T
    grads["a_log"] = dvec[0:1, N_HEADS:2 * N_HEADS]
    grads["dt_bias"] = dvec[1:2, N_HEADS:2 * N_HEADS]
    grads["delta_norm_g"] = dgain[1:2, :HEAD_DIM]
    grads["norm_attn_g"] = st_i[2:3]
    grads["norm_ffn_g"] = st_b[2:3]
    grads["final_norm_g"] = st_f[0:1]
    dmod = jnp.concatenate([st_i[0:1], st_i[1:2], st_b[3:4], st_b[0:1], st_b[1:2], st_f[1:2]], axis=1)
    return st_f[3, 0], grad_x, grads, dmod, (partials_in + partials, scattered_in + scattered)


MESH = pl.DeviceIdType.MESH
OTHER_CHIPS = ((1, 0), (0, 1), (1, 1))
ALL_PEERS = tuple((m >> 2 & 1, m >> 1 & 1, m & 1) for m in range(1, 8))
ANY = pl.BlockSpec(memory_space=pl.ANY)
VMEM_SPEC = pl.BlockSpec(memory_space=pltpu.VMEM)


def _me():
    return lax.axis_index("x"), lax.axis_index("y"), lax.axis_index("c")


def _flip(pos, mask):
    return tuple(1 - p if m else p for p, m in zip(pos, mask))


def _remote(src, dst, send_sems, recv_sems, k, to):
    return pltpu.make_async_remote_copy(src_ref=src, dst_ref=dst, send_sem=send_sems.at[k], recv_sem=recv_sems.at[k],
                                        device_id=to, device_id_type=MESH)


def _ada_exchange(c8, w_ada, b_ada, conv8, shard):
    def body(c_ref, w_ref, b_ref, cv_ref, shard_ref, mod_ref, cact_ref, conv_ref, whole_ref,
             c_all, part_all, send_sems, recv_sems, ride_send, ride_recv):
        x, y, c = me = _me()
        dev = 4 * x + 2 * y + c
        chip = 2 * x + y
        riding = ([shard_ref], [whole_ref], ride_send, ride_recv)
        for cp in _gather_copies(*riding, hand_over=False)[0]:
            cp.start()
        c_all[dev] = c_ref[...]
        conv_ref[chip] = cv_ref[...]
        first = [_remote(c_ref, c_all.at[dev], send_sems, recv_sems, k, _flip(me, mask))
                 for k, mask in enumerate(ALL_PEERS)]
        first += [_remote(cv_ref, conv_ref.at[chip], send_sems, recv_sems, 7 + j, _flip(me, (*mask, 0)))
                  for j, mask in enumerate(OTHER_CHIPS)]
        for cp in first:
            cp.start()
        for cp in first:
            cp.wait()
        row = lax.broadcasted_iota(jnp.int32, (8, D_MODEL), 0)
        c_rows = jnp.zeros((8, D_MODEL), F32)
        for d in range(8):
            c_rows = jnp.where(row == d, c_all[d], c_rows)
        c_act = _silu(c_rows)
        cact_ref[...] = c_act
        part_all[chip] = _nn(c_act, w_ref[...], HIGHEST)
        second = [_remote(part_all.at[chip], part_all.at[chip], send_sems, recv_sems, 10 + j, _flip(me, (*mask, 0)))
                  for j, mask in enumerate(OTHER_CHIPS)]
        for cp in second:
            cp.start()
        for cp in second:
            cp.wait()
        cols = w_ref.shape[1]
        for k in range(4):
            mod_ref[:, k * cols:(k + 1) * cols] = part_all[k] + b_ref[:, k * cols:(k + 1) * cols]
        first, passed = _gather_copies(*riding)
        for cp, fwd in zip(first, passed):
            cp.wait_recv()
            fwd.start()
        for cp in first:
            cp.wait_send()
        for fwd in passed:
            fwd.wait()

    cols = w_ada.shape[1]
    return pl.pallas_call(
        body,
        in_specs=[VMEM_SPEC] * 4 + [ANY],
        out_specs=[VMEM_SPEC] * 3 + [ANY],
        out_shape=[jax.ShapeDtypeStruct((8, 4 * cols), F32), jax.ShapeDtypeStruct((8, D_MODEL), F32),
                   jax.ShapeDtypeStruct((4, 8, conv8.shape[1]), F32)] + _gathered_shapes([shard]),
        scratch_shapes=[pltpu.VMEM((8, 8, D_MODEL), F32), pltpu.VMEM((4, 8, cols), F32),
                        pltpu.SemaphoreType.DMA((13,)), pltpu.SemaphoreType.DMA((13,)),
                        pltpu.SemaphoreType.DMA((6,)), pltpu.SemaphoreType.DMA((6,))],
        compiler_params=pltpu.CompilerParams(vmem_limit_bytes=VMEM_LIMIT),
        name="ada_exchange",
    )(c8, w_ada, b_ada, conv8, shard)


def _gathered_shapes(shards):
    return [jax.ShapeDtypeStruct((4, *s.shape), s.dtype) for s in shards]


def _gather_copies(srcs, dsts, send_sems, recv_sems, hand_over=True):
    x, y, c = me = _me()
    chip = 2 * x + y
    sibling = _flip(me, (0, 0, 1))
    first, passed = [], []
    for a, (src, dst) in enumerate(zip(srcs, dsts)):
        for j, mask in enumerate(OTHER_CHIPS):
            to = _flip(me, (*mask, 0))
            first.append(_remote(src.at[c], dst.at[chip, c], send_sems, recv_sems, 6 * a + j, to))
            if hand_over:
                landed = dst.at[2 * to[0] + to[1], c]
                passed.append(_remote(landed, landed, send_sems, recv_sems, 6 * a + 3 + j, sibling))
    return first, passed


def _scatter_copies(srcs, dsts, send_sems, recv_sems):
    x, y, c = me = _me()
    chip = 2 * x + y
    copies = []
    for a, (src, dst) in enumerate(zip(srcs, dsts)):
        for j, mask in enumerate(OTHER_CHIPS):
            to = _flip(me, (*mask, 0))
            copies.append(_remote(src.at[2 * to[0] + to[1]], dst.at[chip], send_sems, recv_sems, 3 * a + j, to))
    return copies


def _start_and_wait(copies):
    for cp in copies:
        cp.start()
    for cp in copies:
        cp.wait()


def _swap_halves(grads):
    n = len(grads)

    def body(*refs):
        srcs, got = refs[:n], refs[n:2 * n]
        send_sems, recv_sems = refs[2 * n:]
        x, y, c = me = _me()
        _start_and_wait([_remote(srcs[a].at[:, 1 - c], got[a], send_sems, recv_sems, a, _flip(me, (0, 0, 1)))
                         for a in range(n)])

    return pl.pallas_call(
        body,
        in_specs=[ANY] * n,
        out_specs=[ANY] * n,
        out_shape=[jax.ShapeDtypeStruct((4, g.shape[2], g.shape[3]), g.dtype) for g in grads],
        scratch_shapes=[pltpu.SemaphoreType.DMA((n,)), pltpu.SemaphoreType.DMA((n,))],
        name=f"swap_halves_{n}",
    )(*grads)


def _join_halves(halves):
    n = len(halves)

    def body(*refs):
        srcs, dsts = refs[:n], refs[n:2 * n]
        send_sems, recv_sems = refs[2 * n:]
        x, y, c = me = _me()
        _start_and_wait([_remote(srcs[a], dsts[a].at[c], send_sems, recv_sems, a, _flip(me, (0, 0, 1)))
                         for a in range(n)])

    return pl.pallas_call(
        body,
        in_specs=[ANY] * n,
        out_specs=[ANY] * n,
        out_shape=[jax.ShapeDtypeStruct((2, *h.shape), h.dtype) for h in halves],
        scratch_shapes=[pltpu.SemaphoreType.DMA((n,)), pltpu.SemaphoreType.DMA((n,))],
        name=f"join_halves_{n}",
    )(*halves)


def _gather_small(packed):
    n_rows = packed.shape[0]

    def body(p_ref, all_ref, sum_ref, send_sems, recv_sems):
        x, y, c = me = _me()
        dev = 4 * x + 2 * y + c
        all_ref[dev] = p_ref[...]
        copies = [_remote(p_ref, all_ref.at[dev], send_sems, recv_sems, k, _flip(me, mask))
                  for k, mask in enumerate(ALL_PEERS)]
        for cp in copies:
            cp.start()
        for cp in copies:
            cp.wait()
        total = all_ref[0]
        for d in range(1, 8):
            total = total + all_ref[d]
        sum_ref[...] = total

    return pl.pallas_call(
        body,
        in_specs=[VMEM_SPEC],
        out_specs=[VMEM_SPEC, VMEM_SPEC],
        out_shape=[jax.ShapeDtypeStruct((8, n_rows, LANES), F32), jax.ShapeDtypeStruct((n_rows, LANES), F32)],
        scratch_shapes=[pltpu.SemaphoreType.DMA((7,)), pltpu.SemaphoreType.DMA((7,))],
        name="gather_small",
    )(packed)


def _add_pair(a, b, out_dtype, name):
    def body(a_ref, b_ref, o_ref):
        o_ref[...] = (a_ref[...] + b_ref[...]).astype(o_ref.dtype)

    blk = pl.BlockSpec((1, *a.shape[1:]), lambda i: (i, 0, 0))
    return pl.pallas_call(
        body, grid=(a.shape[0],), in_specs=[blk, blk], out_specs=blk,
        out_shape=jax.ShapeDtypeStruct(a.shape, out_dtype),
        compiler_params=_params(("arbitrary",)), name=name,
    )(a, b)


def _add_slots(a, name):
    def body(a_ref, o_ref):
        total = a_ref[0].astype(F32)
        for k in range(1, 4):
            total = total + a_ref[k].astype(F32)
        o_ref[...] = total

    return pl.pallas_call(
        body, in_specs=[VMEM_SPEC], out_specs=VMEM_SPEC,
        out_shape=jax.ShapeDtypeStruct(a.shape[1:], F32),
        compiler_params=pltpu.CompilerParams(vmem_limit_bytes=VMEM_LIMIT), name=name,
    )(a)


def _ada_weight_grad(c_act, dmod_cols):
    def body(c_ref, d_ref, o_ref):
        o_ref[...] = _tn(c_ref[...], d_ref[...], HIGHEST)

    return pl.pallas_call(
        body, in_specs=[VMEM_SPEC, VMEM_SPEC], out_specs=VMEM_SPEC,
        out_shape=jax.ShapeDtypeStruct((c_act.shape[1], dmod_cols.shape[1]), F32),
        compiler_params=pltpu.CompilerParams(vmem_limit_bytes=VMEM_LIMIT), name="ada_weight_grad",
    )(c_act, dmod_cols)


def kernel(x, c, w_ada, b_ada, norm_attn_g, w_in, rel_bias, conv_w, a_log, dt_bias, delta_norm_g, w_out, norm_ffn_g, w_gate, w_up, w_down, final_norm_g, loss_target, m_w_ada, m_b_ada, m_norm_attn_g, m_w_in, m_rel_bias, m_conv_w, m_a_log, m_dt_bias, m_delta_norm_g, m_w_out, m_norm_ffn_g, m_w_gate, m_w_up, m_w_down, m_final_norm_g, v_w_ada, v_b_ada, v_norm_attn_g, v_w_in, v_rel_bias, v_conv_w, v_a_log, v_dt_bias, v_delta_norm_g, v_w_out, v_norm_ffn_g, v_w_gate, v_w_up, v_w_down, v_final_norm_g):
    xi, yi, ci = _me()
    dev = 4 * xi + 2 * yi + ci
    chip = 2 * xi + yi

    big_names = ("w_in", "w_out", "w_gate", "w_up", "w_down")
    by_cols = (True, False, True, True, False)

    def rows_form(a, cols):
        return jnp.swapaxes(a[0], 0, 1) if cols else a[0]

    def halves_form(w):
        rows, lanes = w.shape
        if (rows // 2) % 16:
            rows, lanes = w.size // LANES, LANES
        return (2, rows // 2, lanes)

    big = [rows_form(w, cols) for w, cols in zip((w_in, w_out, w_gate, w_up, w_down), by_cols)]
    shards = [w.astype(BF16).reshape(halves_form(w)) for w in big]

    def assemble(gathered, first):
        return [lax.dynamic_update_index_in_dim(g, s, chip, 0).reshape(4 * w.shape[0], w.shape[1])
                for g, s, w in zip(gathered, shards[first:], big[first:])]

    def reduce_pairs(grads, first, tag):
        slots = [g.reshape(4, *halves_form(w)) for g, w in zip(grads, big[first:])]
        return [_add_pair(lax.dynamic_index_in_dim(s, ci, 1, keepdims=False), got, BF16, f"add_pair_{tag}{a}")
                for a, (s, got) in enumerate(zip(slots, _swap_halves(slots)))]

    def finish(partials, scattered, first, tag):
        by_source = [lax.dynamic_update_index_in_dim(b, lax.dynamic_index_in_dim(p, chip, 0, keepdims=False), chip, 0)
                     for b, p in zip(scattered, partials)]
        halves = [_add_slots(p, f"add_slots_{tag}{a}") for a, p in enumerate(by_source)]
        joined = [lax.dynamic_update_index_in_dim(j, h, ci, 0) for j, h in zip(_join_halves(halves), halves)]
        return [j.reshape(w.shape) for j, w in zip(joined, big[first:])]

    conv_cols = conv_w.shape[2]
    mod_all, c_act, conv_all, gathered_in = _ada_exchange(
        jnp.broadcast_to(c, (8, D_MODEL)), w_ada[0], b_ada, jnp.pad(conv_w[0], ((0, 4), (0, 0))), shards[0])
    mod = lax.dynamic_slice_in_dim(mod_all, dev, 1, axis=0)
    conv_full = jnp.swapaxes(conv_all[:, :4, :], 0, 1).reshape(4, 4 * conv_cols)
    whole_in, = assemble([gathered_in], 0)
    loss, grad_x, grads, dmod, (partials, scattered) = _local_step(
        x[0], loss_target[0], mod, norm_attn_g, whole_in, rel_bias, conv_full, a_log, dt_bias, delta_norm_g,
        norm_ffn_g, final_norm_g[None], shards[1:], functools.partial(assemble, first=1), reduce_pairs)

    big_grads = finish(partials, scattered, 0, "all")

    pieces = [dmod, grads["conv_w"], grads["norm_attn_g"], grads["norm_ffn_g"], grads["final_norm_g"],
              grads["rel_bias"], grads["a_log"], grads["dt_bias"], grads["delta_norm_g"]]
    flat = [jnp.pad(p.reshape(-1), (0, -p.size % LANES)) for p in pieces]
    n_rows = [f.size // LANES for f in flat]
    packed = jnp.concatenate(flat).reshape(-1, LANES)
    packed = jnp.pad(packed, ((0, -packed.shape[0] % 8), (0, 0)))
    all_small, total = _gather_small(packed)
    sums, start = [], 0
    for p, n in zip(pieces, n_rows):
        sums.append(total[start:start + n].reshape(-1)[:p.size].reshape(p.shape))
        start += n
    g_b_ada, g_conv, g_norm_attn, g_norm_ffn, g_final, g_rel, g_alog, g_dt, g_dnorm = sums
    dmod_all = all_small[:, :n_rows[0], :].reshape(8, -1)
    ada_cols = w_ada.shape[2]
    g_w_ada = _ada_weight_grad(c_act, lax.dynamic_slice_in_dim(dmod_all, chip * ada_cols, ada_cols, axis=1))
    g_conv = lax.dynamic_slice_in_dim(g_conv, chip * conv_cols, conv_cols, axis=1)

    grad = {"w_ada": g_w_ada[None], "b_ada": g_b_ada, "norm_attn_g": g_norm_attn,
            "rel_bias": g_rel, "conv_w": g_conv[None], "a_log": g_alog, "dt_bias": g_dt, "delta_norm_g": g_dnorm,
            "norm_ffn_g": g_norm_ffn, "final_norm_g": g_final.reshape(-1)}
    weight = {"w_ada": w_ada, "b_ada": b_ada, "norm_attn_g": norm_attn_g, "w_in": w_in, "rel_bias": rel_bias,
              "conv_w": conv_w, "a_log": a_log, "dt_bias": dt_bias, "delta_norm_g": delta_norm_g, "w_out": w_out,
              "norm_ffn_g": norm_ffn_g, "w_gate": w_gate, "w_up": w_up, "w_down": w_down, "final_norm_g": final_norm_g}
    first = {"w_ada": m_w_ada, "b_ada": m_b_ada, "norm_attn_g": m_norm_attn_g, "w_in": m_w_in, "rel_bias": m_rel_bias,
             "conv_w": m_conv_w, "a_log": m_a_log, "dt_bias": m_dt_bias, "delta_norm_g": m_delta_norm_g,
             "w_out": m_w_out, "norm_ffn_g": m_norm_ffn_g, "w_gate": m_w_gate, "w_up": m_w_up, "w_down": m_w_down,
             "final_norm_g": m_final_norm_g}
    second = {"w_ada": v_w_ada, "b_ada": v_b_ada, "norm_attn_g": v_norm_attn_g, "w_in": v_w_in, "rel_bias": v_rel_bias,
              "conv_w": v_conv_w, "a_log": v_a_log, "dt_bias": v_dt_bias, "delta_norm_g": v_delta_norm_g,
              "w_out": v_w_out, "norm_ffn_g": v_norm_ffn_g, "w_gate": v_w_gate, "w_up": v_w_up, "w_down": v_w_down,
              "final_norm_g": v_final_norm_g}
    delta, new_m, new_v = {}, {}, {}
    for name, w in weight.items():
        if name in big_names:
            continue
        two_d = (-1, w.shape[-1])
        d, nm, nv = _adamw(w.reshape(two_d), grad[name].reshape(two_d), first[name].reshape(two_d),
                           second[name].reshape(two_d), f"adamw_{name}")
        delta[name], new_m[name], new_v[name] = d.reshape(w.shape), nm.reshape(w.shape), nv.reshape(w.shape)
    for name, w, g, cols in zip(big_names, big, big_grads, by_cols):
        outs = _adamw(w, g, rows_form(first[name], cols), rows_form(second[name], cols), f"adamw_{name}")
        grad[name], delta[name], new_m[name], new_v[name] = [
            (jnp.swapaxes(o, 0, 1) if cols else o)[None] for o in (g, *outs)]

    names = list(weight)
    return (lax.psum(loss, ("x", "y", "c")), grad_x[None], *[grad[n] for n in names], *[delta[n] for n in names],
            *[new_m[n] for n in names], *[new_v[n] for n in names])
```

```python
import functools
import math

import numpy as np
import jax
import jax.numpy as jnp
from jax import lax
from jax.experimental import pallas as pl
from jax.experimental.pallas import tpu as pltpu

F32 = jnp.float32
BF16 = jnp.bfloat16
HIGHEST = lax.Precision.HIGHEST

D_MODEL = 1024
HEAD_DIM = 64
N_HEADS = 8
HEAD_W = 512
BRANCHES = ((128, 1), (512, 4), (2048, 16))
BAND = 128
ATT_TILE = 2048
ATT_UNROLL = 8
ATT_UNROLL_BWD = 4
N_BUCKETS = 32
MAX_DISTANCE = 2048
CHUNK = 64
D_FF = 2816
EPS = 1e-6
NEG_INF = -1e30
LANES = 128
VMEM_LIMIT = 56 * 1024 * 1024

ADAM_LR = 0.001
ADAM_B1 = 0.9
ADAM_B2 = 0.999
ADAM_EPS = 1e-08
ADAM_WD = 0.01
ADAM_STEP = 10


def _nn(a, b, precision=None):
    return jnp.dot(a, b, preferred_element_type=F32, precision=precision)


def _nt(a, b, precision=None):
    return lax.dot_general(a, b, (((1,), (1,)), ((), ())), preferred_element_type=F32, precision=precision)


def _tn(a, b, precision=None):
    return lax.dot_general(a, b, (((0,), (0,)), ((), ())), preferred_element_type=F32, precision=precision)


def _params(sem, vmem=VMEM_LIMIT):
    return pltpu.CompilerParams(dimension_semantics=sem, vmem_limit_bytes=vmem)


def _sigmoid(x):
    return 0.5 * jnp.tanh(0.5 * x) + 0.5


def _silu_and_slope(x):
    s = _sigmoid(x)
    return x * s, s * (1.0 + x * (1.0 - s))


def _silu(x):
    return x * _sigmoid(x)


def _attn_tables():
    qi = np.arange(BAND)[:, None]
    kj = np.arange(2 * BAND)[None, :]
    steps = qi + BAND - kj
    in_window = (steps >= 0) & (steps <= BAND)
    max_exact = N_BUCKETS // 2
    out = np.zeros((3, 2, BAND, 2 * BAND), np.int32)
    for b, (_, dil) in enumerate(BRANCHES):
        dist = np.maximum(steps, 0) * dil
        dist_f = np.maximum(dist, 1).astype(np.float32)
        large = max_exact + (np.log(dist_f / np.float32(max_exact)) / np.float32(math.log(MAX_DISTANCE / max_exact))
                             * np.float32(N_BUCKETS - max_exact)).astype(np.int32)
        bucket = np.where(dist < max_exact, dist, np.minimum(large, N_BUCKETS - 1)).astype(np.int32)
        out[b, 0] = np.where(in_window, bucket, -1)
        out[b, 1] = np.where(in_window & (kj >= BAND), bucket, -1)
    return out


def _attention_bias(rel_bias, tables):
    def body(rel_ref, tab_ref, out_ref):
        head = pl.program_id(0)
        for b in range(3):
            tab = tab_ref[b, 0]

            def pick(kk, acc, tab=tab):
                return jnp.where(tab == kk, rel_ref[kk, head], acc)

            acc = lax.fori_loop(0, N_BUCKETS, pick, jnp.zeros((BAND, 2 * BAND), F32))
            for first in range(2):
                out_ref[0, b, first] = jnp.where(tab_ref[b, first] < 0, NEG_INF, acc)

    return pl.pallas_call(
        body,
        grid=(N_HEADS,),
        in_specs=[pl.BlockSpec(memory_space=pltpu.SMEM),
                  pl.BlockSpec((3, 2, BAND, 2 * BAND), lambda h: (0, 0, 0, 0))],
        out_specs=pl.BlockSpec((1, 3, 2, BAND, 2 * BAND), lambda h: (h, 0, 0, 0, 0)),
        out_shape=jax.ShapeDtypeStruct((N_HEADS, 3, 2, BAND, 2 * BAND), F32),
        compiler_params=_params(("arbitrary",)),
        name="attn_bias",
    )(rel_bias, tables)


def _bias_spec():
    return pl.BlockSpec((2, 3, 2, BAND, 2 * BAND), lambda p, t: (p, 0, 0, 0, 0))


def _attn_block_index(idx, t, r):
    nb = ATT_TILE // (BAND * r)
    rho = idx // nb
    n = idx % nb
    qs = rho + r * BAND * n
    gs = t * ATT_TILE + qs
    first = (t * nb + n) == 0
    ps = jnp.where(first, gs, gs - r * BAND)
    return qs, gs, ps, first.astype(jnp.int32)


def _rows(start, r):
    return pl.ds(start, BAND) if r == 1 else pl.ds(start, BAND, stride=r)


def _attention_fwd(qkv, bias, shards):
    seq = qkv.shape[0]
    n_tiles = seq // ATT_TILE
    n = len(shards)

    def body(*refs):
        bias_ref, q_ref, k_ref, v_ref = refs[:4]
        y_ref, lse_ref = refs[4 + n:6 + n]
        o_s, l_s = refs[6 + 2 * n:8 + 2 * n]
        riding = (refs[4:4 + n], refs[6 + n:6 + 2 * n], *refs[8 + 2 * n:])
        pair = pl.program_id(0)
        t = pl.program_id(1)
        if n:
            @pl.when((pair == 0) & (t == 0))
            def _():
                for cp in _gather_copies(*riding, hand_over=False)[0]:
                    cp.start()

            @pl.when((pair == 2) & (t == 0))
            def _():
                for cp, fwd in zip(*_gather_copies(*riding)):
                    cp.wait_recv()
                    fwd.start()

        lane = lax.broadcasted_iota(jnp.int32, (1, LANES), 1)
        head0 = lane < HEAD_DIM
        masks = (head0, jnp.logical_not(head0))
        ones = jnp.ones((2 * BAND, LANES), BF16)
        for b, (_, r) in enumerate(BRANCHES):
            def blocks(it, carry, b=b, r=r):
                idx = [_attn_block_index(it * ATT_UNROLL + j, t, r) for j in range(ATT_UNROLL)]
                qb = [q_ref[_rows(qs, r), :] * (HEAD_DIM ** -0.5) for qs, _, _, _ in idx]
                kcat = [jnp.concatenate([k_ref[_rows(ps, r), :], k_ref[_rows(gs, r), :]], axis=0).astype(BF16)
                        for _, gs, ps, _ in idx]
                vcat = [jnp.concatenate([v_ref[_rows(ps, r), :], v_ref[_rows(gs, r), :]], axis=0).astype(BF16)
                        for _, gs, ps, _ in idx]
                work = [(j, hh) for j in range(ATT_UNROLL) for hh in range(2)]
                s = [_nt(jnp.where(masks[hh], qb[j], 0.0).astype(BF16), kcat[j]) + bias_ref[hh, b, idx[j][3]]
                     for j, hh in work]
                m = [jnp.max(sv, axis=-1, keepdims=True) for sv in s]
                e = [jnp.exp(sv - mv) for sv, mv in zip(s, m)]
                eb = [ev.astype(BF16) for ev in e]
                den = [_nn(ev, ones) for ev in eb]
                out = [_nn(ev, vcat[j]) / dv for ev, dv, (j, _) in zip(eb, den, work)]
                lse = [mv + jnp.log(dv) for mv, dv in zip(m, den)]
                for j in range(ATT_UNROLL):
                    o_s[b, _rows(idx[j][0], r), :] = jnp.where(head0, out[2 * j], out[2 * j + 1])
                    l_s[b, _rows(idx[j][0], r), :] = jnp.where(head0, lse[2 * j], lse[2 * j + 1])
                return carry

            lax.fori_loop(0, ATT_TILE // BAND // ATT_UNROLL, blocks, 0)

        def merge(i, carry):
            rows = pl.ds(pl.multiple_of(i * BAND, BAND), BAND)
            l0, l1, l2 = l_s[0, rows, :], l_s[1, rows, :], l_s[2, rows, :]
            m = jnp.maximum(jnp.maximum(l0, l1), l2)
            w0, w1, w2 = jnp.exp(l0 - m), jnp.exp(l1 - m), jnp.exp(l2 - m)
            tot = w0 + w1 + w2
            y_ref[rows, :] = (w0 * o_s[0, rows, :] + w1 * o_s[1, rows, :] + w2 * o_s[2, rows, :]) / tot
            lse_ref[rows, :] = m + jnp.log(tot)
            return carry

        lax.fori_loop(0, ATT_TILE // BAND, merge, 0)

        if n:
            @pl.when((pair == N_HEADS // 2 - 1) & (t == n_tiles - 1))
            def _():
                first, passed = _gather_copies(*riding)
                for cp in first:
                    cp.wait_send()
                for fwd in passed:
                    fwd.wait()

    tile = pl.BlockSpec((ATT_TILE, LANES), lambda p, t: (t, p))
    sems = [pltpu.SemaphoreType.DMA((6 * n,)), pltpu.SemaphoreType.DMA((6 * n,))] if n else []
    return pl.pallas_call(
        body,
        grid=(N_HEADS // 2, n_tiles),
        in_specs=[
            _bias_spec(),
            pl.BlockSpec((ATT_TILE, LANES), lambda p, t: (t, p)),
            pl.BlockSpec((seq, LANES), lambda p, t: (0, 4 + p)),
            pl.BlockSpec((seq, LANES), lambda p, t: (0, 8 + p)),
        ] + [ANY] * n,
        out_specs=[tile, tile] + [ANY] * n,
        out_shape=[jax.ShapeDtypeStruct((seq, HEAD_W), F32), jax.ShapeDtypeStruct((seq, HEAD_W), F32)]
        + _gathered_shapes(shards),
        scratch_shapes=[
            pltpu.VMEM((3, ATT_TILE, LANES), F32),
            pltpu.VMEM((3, ATT_TILE, LANES), F32),
        ] + sems,
        compiler_params=_params(("arbitrary", "arbitrary")),
        name="attn_fwd",
    )(bias, qkv, qkv, qkv, *shards)


def _attention_bwd(qkv, dy, y, lse, bias, partials):
    seq = qkv.shape[0]
    n_tiles = seq // ATT_TILE
    n = len(partials)

    def body(*refs):
        bias_ref, q_ref, k_ref, v_ref, dy_ref, y_ref, lse_ref = refs[:7]
        dq_ref, dk_ref, dv_ref, dbias_ref = refs[7 + n:11 + n]
        riding = (refs[7:7 + n], refs[11 + n:11 + 2 * n], *refs[11 + 2 * n:])
        pair = pl.program_id(0)
        t = pl.program_id(1)
        if n:
            @pl.when((pair == 0) & (t == 0))
            def _():
                for cp in _scatter_copies(*riding):
                    cp.start()

        lane = lax.broadcasted_iota(jnp.int32, (1, LANES), 1)
        head0 = lane < HEAD_DIM

        @pl.when(t == 0)
        def _():
            dk_ref[...] = jnp.zeros_like(dk_ref)
            dv_ref[...] = jnp.zeros_like(dv_ref)
            dbias_ref[...] = jnp.zeros_like(dbias_ref)

        dq_ref[...] = jnp.zeros_like(dq_ref)

        masks = (head0, jnp.logical_not(head0))
        ones = jnp.ones((LANES, LANES), BF16)
        scale = HEAD_DIM ** -0.5
        for b, (_, r) in enumerate(BRANCHES):
            def blocks(it, carry, b=b, r=r):
                idx = [_attn_block_index(it * ATT_UNROLL_BWD + j, t, r) for j in range(ATT_UNROLL_BWD)]
                qb = [q_ref[_rows(qs, r), :] * scale for qs, _, _, _ in idx]
                kcat = [jnp.concatenate([k_ref[_rows(ps, r), :], k_ref[_rows(gs, r), :]], axis=0).astype(BF16)
                        for _, gs, ps, _ in idx]
                vcat = [jnp.concatenate([v_ref[_rows(ps, r), :], v_ref[_rows(gs, r), :]], axis=0).astype(BF16)
                        for _, gs, ps, _ in idx]
                dob = [dy_ref[_rows(qs, r), :] for qs, _, _, _ in idx]
                ob = [y_ref[_rows(qs, r), :] for qs, _, _, _ in idx]
                lb = [lse_ref[_rows(qs, r), :] for qs, _, _, _ in idx]
                work = [(j, hh) for j in range(ATT_UNROLL_BWD) for hh in range(2)]
                qh = [jnp.where(masks[hh], qb[j], 0.0).astype(BF16) for j, hh in work]
                doh = [jnp.where(masks[hh], dob[j], 0.0) for j, hh in work]
                dohb = [d.astype(BF16) for d in doh]
                s = [_nt(qh[w], kcat[j]) + bias_ref[hh, b, idx[j][3]] for w, (j, hh) in enumerate(work)]
                dp = [_nt(dohb[w], vcat[j]) for w, (j, _) in enumerate(work)]
                lrot = [pltpu.roll(lv, HEAD_DIM, 1) for lv in lb]
                lcol = [jnp.where(masks[hh], lb[j], lrot[j]) for j, hh in work]
                parts = [_split(doh[w] * ob[j]) for w, (j, _) in enumerate(work)]
                delta = [_nn(hi, ones) + _nn(lo, ones) for hi, lo in parts]
                prob = [jnp.exp(sv - jnp.concatenate([lv, lv], axis=1)) for sv, lv in zip(s, lcol)]
                ds = [pv * (dv - jnp.concatenate([de, de], axis=1)) for pv, dv, de in zip(prob, dp, delta)]
                dsb = [d.astype(BF16) for d in ds]
                dq = [_nn(dsb[w], kcat[j]) for w, (j, _) in enumerate(work)]
                dkc = [_tn(dsb[w], qh[w]) for w in range(len(work))]
                dvc = [_tn(prob[w].astype(BF16), dohb[w]) for w in range(len(work))]
                for hh in range(2):
                    dbias_ref[0, b, hh] += sum(ds[w] for w, (_, head) in enumerate(work) if head == hh)
                for j in range(ATT_UNROLL_BWD):
                    qs, gs, ps, _ = idx[j]
                    dkcat = dkc[2 * j] + dkc[2 * j + 1]
                    dvcat = dvc[2 * j] + dvc[2 * j + 1]
                    dq_ref[_rows(qs, r), :] += jnp.where(head0, dq[2 * j], dq[2 * j + 1]) * scale
                    dk_ref[_rows(ps, r), :] += dkcat[:BAND]
                    dk_ref[_rows(gs, r), :] += dkcat[BAND:]
                    dv_ref[_rows(ps, r), :] += dvcat[:BAND]
                    dv_ref[_rows(gs, r), :] += dvcat[BAND:]
                return carry

            lax.fori_loop(0, ATT_TILE // BAND // ATT_UNROLL_BWD, blocks, 0)

        if n:
            @pl.when((pair == N_HEADS // 2 - 1) & (t == n_tiles - 1))
            def _():
                for cp in _scatter_copies(*riding):
                    cp.wait()

    tile = pl.BlockSpec((ATT_TILE, LANES), lambda p, t: (t, p))
    full = pl.BlockSpec((seq, LANES), lambda p, t: (0, p))
    sems = [pltpu.SemaphoreType.DMA((3 * n,)), pltpu.SemaphoreType.DMA((3 * n,))] if n else []
    return pl.pallas_call(
        body,
        grid=(N_HEADS // 2, n_tiles),
        in_specs=[
            _bias_spec(),
            pl.BlockSpec((ATT_TILE, LANES), lambda p, t: (t, p)),
            pl.BlockSpec((seq, LANES), lambda p, t: (0, 4 + p)),
            pl.BlockSpec((seq, LANES), lambda p, t: (0, 8 + p)),
            tile, tile, tile,
        ] + [ANY] * n,
        out_specs=[tile, full, full,
                   pl.BlockSpec((1, 3, 2, BAND, 2 * BAND), lambda p, t: (p, 0, 0, 0, 0))] + [ANY] * n,
        out_shape=[jax.ShapeDtypeStruct((seq, HEAD_W), F32)] * 3
        + [jax.ShapeDtypeStruct((N_HEADS // 2, 3, 2, BAND, 2 * BAND), F32)]
        + [jax.ShapeDtypeStruct(p.shape, p.dtype) for p in partials],
        scratch_shapes=sems,
        compiler_params=_params(("arbitrary", "arbitrary")),
        name="attn_bwd",
    )(bias, qkv, qkv, qkv, dy, y, lse, *partials)


def _rel_bias_grad(dbias, tables):
    def body(tab_ref, db_ref, out_ref):
        lane = lax.broadcasted_iota(jnp.int32, (1, LANES), 1)
        out_ref[...] = jnp.zeros_like(out_ref)
        for b in range(3):
            tab = tab_ref[b, 0]

            def head(h, carry, b=b, tab=tab):
                d = db_ref[h // 2, b, h % 2]
                sums = [jnp.sum(jnp.where(tab == kk, d, 0.0), keepdims=True) for kk in range(N_BUCKETS)]
                row = jnp.zeros((1, LANES), F32)
                for kk, s in enumerate(sums):
                    row = row + jnp.where(lane == kk, s, 0.0)
                out_ref[pl.ds(h, 1), :] += row
                return carry

            lax.fori_loop(0, N_HEADS, head, 0)

    return pl.pallas_call(
        body,
        out_shape=jax.ShapeDtypeStruct((N_HEADS, LANES), F32),
        compiler_params=pltpu.CompilerParams(vmem_limit_bytes=VMEM_LIMIT),
        name="rel_bias_grad",
    )(tables, dbias)


ROW_TILE = 512


def _head_sum_matrix():
    return (lax.broadcasted_iota(jnp.int32, (HEAD_W, HEAD_W), 0) // HEAD_DIM
            == lax.broadcasted_iota(jnp.int32, (HEAD_W, HEAD_W), 1) // HEAD_DIM).astype(F32)


def _head_spread_matrix(offset=0):
    return (lax.broadcasted_iota(jnp.int32, (LANES, HEAD_W), 0)
            == lax.broadcasted_iota(jnp.int32, (LANES, HEAD_W), 1) // HEAD_DIM + offset).astype(F32)


def _head_gather_matrix(offset=0):
    return (lax.broadcasted_iota(jnp.int32, (HEAD_W, LANES), 0) // HEAD_DIM + offset
            == lax.broadcasted_iota(jnp.int32, (HEAD_W, LANES), 1)).astype(F32)


def _split3(x):
    hi = x.astype(BF16)
    rest = x - hi.astype(F32)
    mid = rest.astype(BF16)
    return hi, mid, (rest - mid.astype(F32)).astype(BF16)


def _pick(x, onehot):
    m = onehot.astype(BF16)
    hi, mid, lo = _split3(x)
    return _nn(hi, m) + (_nn(mid, m) + _nn(lo, m))


def _pick_left(onehot, x):
    m = onehot.astype(BF16)
    hi, mid, lo = _split3(x)
    return _nn(m, hi) + (_nn(m, mid) + _nn(m, lo))


def _tri(lower, strict=False):
    r = lax.broadcasted_iota(jnp.int32, (CHUNK, CHUNK), 0)
    c = lax.broadcasted_iota(jnp.int32, (CHUNK, CHUNK), 1)
    if lower:
        return (c < r) if strict else (c <= r)
    return c >= r


def _softplus(z):
    return jnp.maximum(z, 0.0) + jnp.log(1.0 + jnp.exp(-jnp.abs(z)))


def _conv_taps(stage, w_ref, rows):
    return (w_ref[3:4, :] * stage[8:8 + rows, :] + w_ref[2:3, :] * stage[7:7 + rows, :]
            + w_ref[1:2, :] * stage[6:6 + rows, :] + w_ref[0:1, :] * stage[5:5 + rows, :])


def _l2_scale(xc, hsum):
    return lax.rsqrt(_pick(xc * xc, hsum) + EPS)


def _stage_rows(stage, x_ref, xp_ref, i):
    stage[0:8, :] = jnp.where(i == 0, 0.0, xp_ref[...])
    stage[8:8 + ROW_TILE, :] = x_ref[...]


def _delta_prep_fwd(qkvz, ba, conv_w, alog_row, dt_row):
    seq = qkvz.shape[0]
    qkv_w = 3 * HEAD_W

    def body(x_ref, xp_ref, ba_ref, w_ref, al_ref, dt_ref, out_ref, stage):
        i = pl.program_id(0)
        _stage_rows(stage, x_ref, xp_ref, i)
        act = _silu(_conv_taps(stage, w_ref, ROW_TILE))
        hsum, hspread = _head_sum_matrix(), _head_spread_matrix()
        qc, kc = act[:, :HEAD_W], act[:, HEAD_W:2 * HEAD_W]
        out_ref[0] = qc * _l2_scale(qc, hsum) * (HEAD_DIM ** -0.5)
        out_ref[1] = kc * _l2_scale(kc, hsum)
        out_ref[2] = act[:, 2 * HEAD_W:]
        bav = ba_ref[...]
        out_ref[3] = _pick(_sigmoid(bav), hspread)
        g8 = -jnp.exp(al_ref[...]) * _softplus(bav + dt_ref[...])
        gb = _pick(g8, _head_spread_matrix(N_HEADS))
        cum = _tri(True).astype(F32)
        for ch in range(ROW_TILE // CHUNK):
            rows = slice(ch * CHUNK, (ch + 1) * CHUNK)
            out_ref[4, rows, :] = _pick_left(cum, gb[rows])

    return pl.pallas_call(
        body,
        grid=(seq // ROW_TILE,),
        in_specs=[
            pl.BlockSpec((ROW_TILE, qkv_w), lambda i: (i, 0)),
            pl.BlockSpec((8, qkv_w), lambda i: (jnp.maximum(i * (ROW_TILE // 8) - 1, 0), 0)),
            pl.BlockSpec((ROW_TILE, LANES), lambda i: (i, 0)),
            pl.BlockSpec((4, qkv_w), lambda i: (0, 0)),
            pl.BlockSpec((1, LANES), lambda i: (0, 0)),
            pl.BlockSpec((1, LANES), lambda i: (0, 0)),
        ],
        out_specs=pl.BlockSpec((5, ROW_TILE, HEAD_W), lambda i: (0, i, 0)),
        out_shape=jax.ShapeDtypeStruct((5, seq, HEAD_W), F32),
        scratch_shapes=[pltpu.VMEM((ROW_TILE + 8, qkv_w), F32)],
        compiler_params=_params(("arbitrary",)),
        name="delta_prep_fwd",
    )(qkvz, qkvz, ba, conv_w, alog_row, dt_row)


def _split(x):
    hi = x.astype(BF16)
    return hi, (x - hi.astype(F32)).astype(BF16)


def _dot3(a, b, dot=_nn):
    return dot(a[0], b[0]) + (dot(a[0], b[1]) + dot(a[1], b[0]))


def _unit_lower_inverses(mats):
    eye = (lax.broadcasted_iota(jnp.int32, (CHUNK, CHUNK), 0)
           == lax.broadcasted_iota(jnp.int32, (CHUNK, CHUNK), 1)).astype(F32)
    invs = [eye - a for a in mats]
    powers = [_split(a) for a in mats]
    for step in range(5):
        squares = [_dot3(p, p) for p in powers]
        powers = [_split(s) for s in squares]
        invs = [inv + _dot3(_split(inv), p) for inv, p in zip(invs, powers)]
    return invs


def _chunk_terms(q, k, v, beta, gc):
    causal, strict = _tri(True), _tri(True, strict=True)
    e = jnp.exp(gc)
    g_last = jnp.broadcast_to(gc[CHUNK - 1:CHUNK, :], (CHUNK, CHUNK))
    f = jnp.exp(g_last - gc)
    e_last = jnp.exp(g_last)
    decay = jnp.where(causal, jnp.exp(jnp.where(causal, gc - gc.T, 0.0)), 0.0)
    kb = k * beta
    a_mat = jnp.where(strict, _nt(kb.astype(BF16), k.astype(BF16)) * decay, 0.0)
    qk = jnp.where(causal, _nt(q.astype(BF16), k.astype(BF16)) * decay, 0.0)
    return e, f, e_last, decay, kb, a_mat, qk


GROUP = 8
UNROLL = 8


def _chunk_rows(ci):
    return pl.ds(pl.multiple_of(ci * CHUNK, CHUNK), CHUNK)


def _pair_specs(n_planes):
    return pl.BlockSpec((n_planes, GROUP * CHUNK, LANES), lambda p, g: (0, g, p))


def _delta_chunk_fwd(xs):
    seq = xs.shape[1]
    rows_per_step = GROUP * CHUNK

    def body(x_ref, inv_ref, qk_ref, u_ref, w_ref):
        work = [(hh, slice(step * CHUNK, (step + 1) * CHUNK)) for hh in range(2) for step in range(GROUP)]
        xh = [[x_ref[j, r, hh * HEAD_DIM:(hh + 1) * HEAD_DIM] for j in range(5)] for hh, r in work]
        terms = [_chunk_terms(*x) for x in xh]
        invs = _unit_lower_inverses([t[5] for t in terms])
        for (hh, r), x, t, inv in zip(work, xh, terms, invs):
            e, kb, qk = t[0], t[4], t[6]
            inv_parts = _split(inv)
            inv_ref[hh, r, :] = inv
            qk_ref[hh, r, :] = qk
            u_ref[hh, r, :] = _dot3(inv_parts, _split(x[2] * x[3]))
            w_ref[hh, r, :] = _dot3(inv_parts, _split(kb * e))

    out = pl.BlockSpec((2, rows_per_step, HEAD_DIM), lambda p, g: (p, g, 0))
    return pl.pallas_call(
        body,
        grid=(N_HEADS // 2, seq // rows_per_step),
        in_specs=[_pair_specs(5)],
        out_specs=[out] * 4,
        out_shape=[jax.ShapeDtypeStruct((N_HEADS, seq, HEAD_DIM), F32)] * 4,
        compiler_params=_params(("parallel", "parallel")),
        name="delta_chunk_fwd",
    )(xs)


def _decays(gc):
    g_last = jnp.broadcast_to(gc[CHUNK - 1:CHUNK, :], (CHUNK, CHUNK))
    return jnp.exp(gc), jnp.exp(g_last - gc), jnp.exp(g_last)


def _token_blocks(index, n_steps=None):
    rows_per_step = GROUP * CHUNK
    if n_steps is None:
        return pl.BlockSpec((1, rows_per_step, HEAD_W), lambda g: (index, g, 0))
    return pl.BlockSpec((1, rows_per_step, HEAD_W), lambda g: (index, n_steps - 1 - g, 0))


def _head_lanes(h):
    return pl.ds(h * HEAD_DIM, HEAD_DIM)


def _delta_scan_fwd(xs, qk_h, u_h, w_h):
    seq = xs.shape[1]
    rows_per_step = GROUP * CHUNK

    def body(q_ref, k_ref, gc_ref, qk_ref, u_ref, w_ref, o_ref, st_ref, state):
        @pl.when(pl.program_id(0) == 0)
        def _():
            state[...] = jnp.zeros_like(state)

        def chunk(ci, carry):
            rows = _chunk_rows(ci)
            heads = range(N_HEADS)
            dec = [_decays(gc_ref[0, rows, _head_lanes(h)]) for h in heads]
            s = [state[h] for h in heads]
            sb = [s[h].astype(BF16) for h in heads]
            vnb = [(u_ref[h, rows, :] - _nn(w_ref[h, rows, :].astype(BF16), sb[h])).astype(BF16) for h in heads]
            for h in heads:
                o_ref[rows, _head_lanes(h)] = (_nn((q_ref[0, rows, _head_lanes(h)] * dec[h][0]).astype(BF16), sb[h])
                                               + _nn(qk_ref[h, rows, :].astype(BF16), vnb[h]))
                st_ref[h, rows, :] = s[h]
            for h in heads:
                state[h] = s[h] * dec[h][2] + _tn((k_ref[0, rows, _head_lanes(h)] * dec[h][1]).astype(BF16), vnb[h])
            return carry

        lax.fori_loop(0, GROUP, chunk, 0)

    blk = pl.BlockSpec((N_HEADS, rows_per_step, HEAD_DIM), lambda g: (0, g, 0))
    return pl.pallas_call(
        body,
        grid=(seq // rows_per_step,),
        in_specs=[_token_blocks(0), _token_blocks(1), _token_blocks(4), blk, blk, blk],
        out_specs=[pl.BlockSpec((rows_per_step, HEAD_W), lambda g: (g, 0)), blk],
        out_shape=[jax.ShapeDtypeStruct((seq, HEAD_W), F32), jax.ShapeDtypeStruct((N_HEADS, seq, HEAD_DIM), F32)],
        scratch_shapes=[pltpu.VMEM((N_HEADS, CHUNK, CHUNK), F32)],
        compiler_params=_params(("arbitrary",)),
        name="delta_scan_fwd",
    )(xs, xs, xs, qk_h, u_h, w_h)


def _delta_scan_bwd(xs, qk_h, w_h, do):
    seq = xs.shape[1]
    rows_per_step = GROUP * CHUNK
    n_steps = seq // rows_per_step

    def body(q_ref, k_ref, gc_ref, qk_ref, w_ref, do_ref, dsn_ref, dvn_ref, dstate):
        @pl.when(pl.program_id(0) == 0)
        def _():
            dstate[...] = jnp.zeros_like(dstate)

        def chunk(step, carry):
            rows = _chunk_rows(GROUP - 1 - step)
            heads = range(N_HEADS)
            dec = [_decays(gc_ref[0, rows, _head_lanes(h)]) for h in heads]
            ds_next = [dstate[h] for h in heads]
            dob = [do_ref[rows, _head_lanes(h)].astype(BF16) for h in heads]
            dv_new = [_tn(qk_ref[h, rows, :].astype(BF16), dob[h])
                      + _nn((k_ref[0, rows, _head_lanes(h)] * dec[h][1]).astype(BF16), ds_next[h].astype(BF16))
                      for h in heads]
            for h in heads:
                dsn_ref[h, rows, :] = ds_next[h]
                dvn_ref[h, rows, :] = dv_new[h]
            for h in heads:
                dstate[h] = (_tn((q_ref[0, rows, _head_lanes(h)] * dec[h][0]).astype(BF16), dob[h])
                             + dec[h][2] * ds_next[h] - _tn(w_ref[h, rows, :].astype(BF16), dv_new[h].astype(BF16)))
            return carry

        lax.fori_loop(0, GROUP, chunk, 0)

    blk = pl.BlockSpec((N_HEADS, rows_per_step, HEAD_DIM), lambda g: (0, n_steps - 1 - g, 0))
    return pl.pallas_call(
        body,
        grid=(n_steps,),
        in_specs=[_token_blocks(0, n_steps), _token_blocks(1, n_steps), _token_blocks(4, n_steps), blk, blk,
                  pl.BlockSpec((rows_per_step, HEAD_W), lambda g: (n_steps - 1 - g, 0))],
        out_specs=[blk, blk],
        out_shape=[jax.ShapeDtypeStruct((N_HEADS, seq, HEAD_DIM), F32)] * 2,
        scratch_shapes=[pltpu.VMEM((N_HEADS, CHUNK, CHUNK), F32)],
        compiler_params=_params(("arbitrary",)),
        name="delta_scan_bwd",
    )(xs, xs, xs, qk_h, w_h, do)


def _delta_chunk_bwd(xs, inv_h, u_h, w_h, st_h, dsn_h, dvn_h, do):
    seq = xs.shape[1]
    rows_per_step = GROUP * CHUNK

    def body(x_ref, inv_ref, u_ref, w_ref, st_ref, dsn_ref, dvn_ref, do_ref, dx_ref):
        causal, strict = _tri(True), _tri(True, strict=True)
        last_row = lax.broadcasted_iota(jnp.int32, (CHUNK, CHUNK), 0) == CHUNK - 1

        def bf(vals):
            return [val.astype(BF16) for val in vals]

        def group(items):
            heads = [hh for hh, _ in items]
            lanes = [slice(hh * HEAD_DIM, (hh + 1) * HEAD_DIM) for hh in heads]
            rows = [slice(step * CHUNK, (step + 1) * CHUNK) for _, step in items]
            n = range(len(items))
            q, k, v, beta, gc = [[x_ref[j, rows[i], lanes[i]] for i in n] for j in range(5)]
            terms = [_chunk_terms(q[i], k[i], v[i], beta[i], gc[i]) for i in n]
            e, f, e_last, decay, kb, a_mat, qk = [[t[j] for t in terms] for j in range(7)]
            inv = [_split(inv_ref[heads[i], rows[i], :]) for i in n]
            u = [u_ref[heads[i], rows[i], :] for i in n]
            w = [w_ref[heads[i], rows[i], :] for i in n]
            s = [st_ref[heads[i], rows[i], :] for i in n]
            ds_next = [dsn_ref[heads[i], rows[i], :] for i in n]
            dv_new = [dvn_ref[heads[i], rows[i], :] for i in n]
            sb, dsb, dvb, wb = bf(s), bf(ds_next), bf(dv_new), bf(w)
            dob = bf([do_ref[rows[i], lanes[i]] for i in n])
            qbf, kbf, kbb = bf(q), bf(k), bf(kb)
            vnb = bf([u[i] - _nn(wb[i], sb[i]) for i in n])
            dqe = [_nt(dob[i], sb[i]) for i in n]
            dw = [-_nt(dvb[i], sb[i]) for i in n]
            dkf = [_nt(vnb[i], dsb[i]) for i in n]
            dqk = [jnp.where(causal, _nt(dob[i], vnb[i]), 0.0) for i in n]
            drhs_u = [_dot3(inv[i], _split(dv_new[i]), _tn) for i in n]
            drhs_w = [_dot3(inv[i], _split(dw[i]), _tn) for i in n]
            da = [-jnp.where(strict, _nt(drhs_u[i].astype(BF16), u[i].astype(BF16))
                             + _nt(drhs_w[i].astype(BF16), wb[i]), 0.0) for i in n]
            dad = bf([da[i] * decay[i] for i in n])
            dqd = bf([dqk[i] * decay[i] for i in n])
            dkb = [e[i] * drhs_w[i] + _nn(dad[i], kbf[i]) for i in n]
            dk = [_tn(dad[i], kbb[i]) + _tn(dqd[i], qbf[i]) + f[i] * dkf[i] + beta[i] * dkb[i] for i in n]
            dq = [_nn(dqd[i], kbf[i]) + e[i] * dqe[i] for i in n]
            for i in n:
                de_full = kb[i] * drhs_w[i] + q[i] * dqe[i]
                df_full = k[i] * dkf[i]
                m = da[i] * a_mat[i] + dqk[i] * qk[i]
                dgc = de_full * e[i] - df_full * f[i] + m - m.T
                tail = jnp.sum(df_full * f[i] + s[i] * ds_next[i] * e_last[i], axis=0, keepdims=True)
                dgc = dgc + jnp.where(last_row, jnp.broadcast_to(tail, (CHUNK, CHUNK)), 0.0)
                dx_ref[0, rows[i], lanes[i]] = dq[i]
                dx_ref[1, rows[i], lanes[i]] = dk[i]
                dx_ref[2, rows[i], lanes[i]] = beta[i] * drhs_u[i]
                dx_ref[3, rows[i], lanes[i]] = v[i] * drhs_u[i] + k[i] * dkb[i]
                dx_ref[4, rows[i], lanes[i]] = dgc

        work = [(hh, step) for hh in range(2) for step in range(GROUP)]
        for first in range(0, len(work), UNROLL):
            group(work[first:first + UNROLL])

    blk = pl.BlockSpec((2, rows_per_step, HEAD_DIM), lambda p, g: (p, g, 0))
    return pl.pallas_call(
        body,
        grid=(N_HEADS // 2, seq // rows_per_step),
        in_specs=[_pair_specs(5)] + [blk] * 6 + [pl.BlockSpec((rows_per_step, LANES), lambda p, g: (g, p))],
        out_specs=_pair_specs(5),
        out_shape=jax.ShapeDtypeStruct((5, seq, HEAD_W), F32),
        compiler_params=_params(("parallel", "parallel")),
        name="delta_chunk_bwd",
    )(xs, inv_h, u_h, w_h, st_h, dsn_h, dvn_h, do)


def _delta_post_fwd(o, qkvz, gain_row):
    seq = o.shape[0]

    def body(o_ref, z_ref, g_ref, y_ref):
        ov = o_ref[...]
        rb = lax.rsqrt(_pick(ov * ov, _head_sum_matrix()) * (1.0 / HEAD_DIM) + EPS)
        y_ref[...] = (ov * rb * g_ref[...] * _silu(z_ref[...])).astype(y_ref.dtype)

    tile = pl.BlockSpec((ROW_TILE, HEAD_W), lambda i: (i, 0))
    return pl.pallas_call(
        body,
        grid=(seq // ROW_TILE,),
        in_specs=[tile, pl.BlockSpec((ROW_TILE, HEAD_W), lambda i: (i, 3)), pl.BlockSpec((1, HEAD_W), lambda i: (0, 0))],
        out_specs=tile,
        out_shape=jax.ShapeDtypeStruct((seq, HEAD_W), BF16),
        compiler_params=_params(("arbitrary",)),
        name="delta_post_fwd",
    )(o, qkvz, gain_row)


def _delta_post_bwd(dy, o, qkvz, gain_row, h):
    seq = o.shape[0]

    def body(dy_ref, o_ref, z_ref, g_ref, h_ref, do_ref, dz_ref, dg_ref, dwin_ref):
        @pl.when(pl.program_id(0) == 0)
        def _():
            dg_ref[...] = jnp.zeros_like(dg_ref)
            dwin_ref[...] = jnp.zeros_like(dwin_ref)

        ov, zv, dyv, gain = o_ref[...], z_ref[...], dy_ref[...], g_ref[...]
        hsum = _head_sum_matrix()
        rb = lax.rsqrt(_pick(ov * ov, hsum) * (1.0 / HEAD_DIM) + EPS)
        ohat = ov * rb
        silu_z, slope_z = _silu_and_slope(zv)
        dz = dyv * ohat * gain * slope_z
        dz_ref[...] = dz
        dwin_ref[...] += _tn(dz.astype(BF16), h_ref[...])
        dn = dyv * silu_z
        dg_ref[0:1, :] += jnp.sum(dn * ohat, axis=0, keepdims=True)
        dohat = dn * gain

        @pl.when(pl.program_id(0) == pl.num_programs(0) - 1)
        def _():
            fold = (lax.broadcasted_iota(jnp.int32, (HEAD_W, HEAD_W), 0) % HEAD_DIM
                    == lax.broadcasted_iota(jnp.int32, (HEAD_W, HEAD_W), 1)).astype(F32)
            dg_ref[1:2, :] = _pick(dg_ref[0:1, :], fold)

        proj = _pick(dohat * ohat, hsum) * (1.0 / HEAD_DIM)
        do_ref[...] = rb * (dohat - ohat * proj)

    tile = pl.BlockSpec((ROW_TILE, HEAD_W), lambda i: (i, 0))
    return pl.pallas_call(
        body,
        grid=(seq // ROW_TILE,),
        in_specs=[pl.BlockSpec((ROW_TILE, HEAD_W), lambda i: (i, 1)), tile,
                  pl.BlockSpec((ROW_TILE, HEAD_W), lambda i: (i, 3)), pl.BlockSpec((1, HEAD_W), lambda i: (0, 0)),
                  pl.BlockSpec((ROW_TILE, D_MODEL), lambda i: (i, 0))],
        out_specs=[tile, tile, pl.BlockSpec((2, HEAD_W), lambda i: (0, 0)),
                   pl.BlockSpec((HEAD_W, D_MODEL), lambda i: (0, 0))],
        out_shape=[jax.ShapeDtypeStruct((seq, HEAD_W), F32), jax.ShapeDtypeStruct((seq, HEAD_W), F32),
                   jax.ShapeDtypeStruct((2, HEAD_W), F32), jax.ShapeDtypeStruct((HEAD_W, D_MODEL), F32)],
        compiler_params=_params(("arbitrary",)),
        name="delta_post_bwd",
    )(dy, o, qkvz, gain_row, h)


def _delta_prep_bwd(qkvz, ba, conv_w, alog_row, dt_row, dxs, h):
    seq = qkvz.shape[0]
    qkv_w = 3 * HEAD_W

    def body(x_ref, xp_ref, ba_ref, w_ref, al_ref, dt_ref, dx_ref, h_ref, dconv_ref, dba_ref, dvec_ref, dwin_ref, stage):
        i = pl.program_id(0)

        @pl.when(i == 0)
        def _():
            dvec_ref[...] = jnp.zeros_like(dvec_ref)
            dwin_ref[...] = jnp.zeros_like(dwin_ref)

        _stage_rows(stage, x_ref, xp_ref, i)
        pre = _conv_taps(stage, w_ref, ROW_TILE)
        act, slope = _silu_and_slope(pre)
        hsum = _head_sum_matrix()
        for j, scale in ((0, HEAD_DIM ** -0.5), (1, 1.0)):
            cols = slice(j * HEAD_W, (j + 1) * HEAD_W)
            xc = act[:, cols]
            rb = _l2_scale(xc, hsum)
            xhat = xc * rb
            dhat = dx_ref[j] * scale
            proj = _pick(dhat * xhat, hsum)
            dconv_ref[:, cols] = rb * (dhat - xhat * proj) * slope[:, cols]
        dconv_ref[:, 2 * HEAD_W:] = dx_ref[2] * slope[:, 2 * HEAD_W:]

        bav = ba_ref[...]
        beta8 = _sigmoid(bav)
        dbeta8 = _pick(dx_ref[3], _head_gather_matrix())
        dgc8 = _pick(dx_ref[4], _head_gather_matrix(N_HEADS))
        rev = _tri(False).astype(F32)
        z = bav + dt_ref[...]
        ea = jnp.exp(al_ref[...])
        g8 = -ea * _softplus(z)
        sig = _sigmoid(z)
        d_alog = jnp.zeros((1, LANES), F32)
        d_dt = jnp.zeros((1, LANES), F32)
        for ch in range(ROW_TILE // CHUNK):
            rows = slice(ch * CHUNK, (ch + 1) * CHUNK)
            dg8 = _pick_left(rev, dgc8[rows])
            da = -dg8 * ea * sig[rows]
            dba_ref[rows, :] = dbeta8[rows] * beta8[rows] * (1.0 - beta8[rows]) + da
            d_alog = d_alog + jnp.sum(dg8 * g8[rows], axis=0, keepdims=True)
            d_dt = d_dt + jnp.sum(da, axis=0, keepdims=True)
        dvec_ref[0:1, :] += d_alog
        dvec_ref[1:2, :] += d_dt
        dwin_ref[...] += _tn(dba_ref[...].astype(BF16), h_ref[...])

    return pl.pallas_call(
        body,
        grid=(seq // ROW_TILE,),
        in_specs=[
            pl.BlockSpec((ROW_TILE, qkv_w), lambda i: (i, 0)),
            pl.BlockSpec((8, qkv_w), lambda i: (jnp.maximum(i * (ROW_TILE // 8) - 1, 0), 0)),
            pl.BlockSpec((ROW_TILE, LANES), lambda i: (i, 0)),
            pl.BlockSpec((4, qkv_w), lambda i: (0, 0)),
            pl.BlockSpec((1, LANES), lambda i: (0, 0)),
            pl.BlockSpec((1, LANES), lambda i: (0, 0)),
            pl.BlockSpec((5, ROW_TILE, HEAD_W), lambda i: (0, i, 0)),
            pl.BlockSpec((ROW_TILE, D_MODEL), lambda i: (i, 0)),
        ],
        out_specs=[pl.BlockSpec((ROW_TILE, qkv_w), lambda i: (i, 0)),
                   pl.BlockSpec((ROW_TILE, LANES), lambda i: (i, 0)),
                   pl.BlockSpec((2, LANES), lambda i: (0, 0)),
                   pl.BlockSpec((LANES, D_MODEL), lambda i: (0, 0))],
        out_shape=[jax.ShapeDtypeStruct((seq, qkv_w), F32), jax.ShapeDtypeStruct((seq, LANES), F32),
                   jax.ShapeDtypeStruct((2, LANES), F32), jax.ShapeDtypeStruct((LANES, D_MODEL), F32)],
        scratch_shapes=[pltpu.VMEM((ROW_TILE + 8, qkv_w), F32)],
        compiler_params=_params(("arbitrary",)),
        name="delta_prep_bwd",
    )(qkvz, qkvz, ba, conv_w, alog_row, dt_row, dxs, h)


def _conv_bwd(dconv, qkvz, conv_w, h):
    seq = dconv.shape[0]
    qkv_w = 3 * HEAD_W
    n_tiles = seq // ROW_TILE

    def body(dy_ref, dyn_ref, x_ref, xp_ref, w_ref, h_ref, dx_ref, dw_ref, dwin_ref, stage, dstage):
        i = pl.program_id(0)

        @pl.when(i == 0)
        def _():
            dw_ref[...] = jnp.zeros_like(dw_ref)
            dwin_ref[...] = jnp.zeros_like(dwin_ref)

        _stage_rows(stage, x_ref, xp_ref, i)
        dstage[0:ROW_TILE, :] = dy_ref[...]
        dstage[ROW_TILE:ROW_TILE + 8, :] = jnp.where(i == n_tiles - 1, 0.0, dyn_ref[...])
        dy = dy_ref[...]
        dx = (w_ref[3:4, :] * dy + w_ref[2:3, :] * dstage[1:1 + ROW_TILE, :]
              + w_ref[1:2, :] * dstage[2:2 + ROW_TILE, :] + w_ref[0:1, :] * dstage[3:3 + ROW_TILE, :])
        dx_ref[...] = dx
        dwin_ref[...] += _tn(dx.astype(BF16), h_ref[...])
        for j in range(4):
            dw_ref[j:j + 1, :] += jnp.sum(dy * stage[5 + j:5 + j + ROW_TILE, :], axis=0, keepdims=True)

    tile = pl.BlockSpec((ROW_TILE, qkv_w), lambda i: (i, 0))
    return pl.pallas_call(
        body,
        grid=(n_tiles,),
        in_specs=[
            tile,
            pl.BlockSpec((8, qkv_w), lambda i: (jnp.minimum((i + 1) * (ROW_TILE // 8), seq // 8 - 1), 0)),
            tile,
            pl.BlockSpec((8, qkv_w), lambda i: (jnp.maximum(i * (ROW_TILE // 8) - 1, 0), 0)),
            pl.BlockSpec((4, qkv_w), lambda i: (0, 0)),
            pl.BlockSpec((ROW_TILE, D_MODEL), lambda i: (i, 0)),
        ],
        out_specs=[tile, pl.BlockSpec((4, qkv_w), lambda i: (0, 0)), pl.BlockSpec((qkv_w, D_MODEL), lambda i: (0, 0))],
        out_shape=[jax.ShapeDtypeStruct((seq, qkv_w), F32), jax.ShapeDtypeStruct((4, qkv_w), F32),
                   jax.ShapeDtypeStruct((qkv_w, D_MODEL), F32)],
        scratch_shapes=[pltpu.VMEM((ROW_TILE + 8, qkv_w), F32), pltpu.VMEM((ROW_TILE + 8, qkv_w), F32)],
        compiler_params=_params(("arbitrary",)),
        name="conv_bwd",
    )(dconv, dconv, qkvz, qkvz, conv_w, h)


FF_TILE = 1408
WGRAD_ROWS = 1024


def _row(a):
    return pl.BlockSpec((1, a), lambda *_: (0, 0))


def _rms_fwd(xv, gain):
    rstd = lax.rsqrt(jnp.mean(xv * xv, axis=-1, keepdims=True) + EPS)
    xhat = xv * rstd
    return xhat, rstd, xhat * gain


def _rms_bwd(dnorm, xhat, rstd, gain):
    dxhat = dnorm * gain
    dx = rstd * (dxhat - xhat * jnp.mean(dxhat * xhat, axis=-1, keepdims=True))
    return dx, jnp.sum(dnorm * xhat, axis=0, keepdims=True)


IN_SPLITS = (0, 3 * HEAD_W, 7 * HEAD_W, 7 * HEAD_W + LANES)


def _inproj_fwd(x, gain, scale, shift, w_rows):
    seq = x.shape[0]

    def body(x_ref, g_ref, sc_ref, sh_ref, w_ref, h_ref, a_ref, d_ref, b_ref):
        _, _, norm = _rms_fwd(x_ref[...], g_ref[...])
        h = (norm * (1.0 + sc_ref[...]) + sh_ref[...]).astype(BF16)
        h_ref[...] = h
        for out_ref, lo, hi in zip((a_ref, d_ref, b_ref), IN_SPLITS[:-1], IN_SPLITS[1:]):
            out_ref[...] = _nt(h, w_ref[lo:hi, :])

    def rows(width):
        return pl.BlockSpec((ROW_TILE, width), lambda i: (i, 0))

    return pl.pallas_call(
        body,
        grid=(seq // ROW_TILE,),
        in_specs=[rows(D_MODEL), _row(D_MODEL), _row(D_MODEL), _row(D_MODEL),
                  pl.BlockSpec(w_rows.shape, lambda i: (0, 0))],
        out_specs=[rows(D_MODEL), rows(3 * HEAD_W), rows(4 * HEAD_W), rows(LANES)],
        out_shape=[jax.ShapeDtypeStruct((seq, D_MODEL), BF16), jax.ShapeDtypeStruct((seq, 3 * HEAD_W), F32),
                   jax.ShapeDtypeStruct((seq, 4 * HEAD_W), F32), jax.ShapeDtypeStruct((seq, LANES), F32)],
        compiler_params=_params(("arbitrary",)),
        name="inproj_fwd",
    )(x, gain, scale, shift, w_rows)


def _outproj_fwd(y_attn, y_delta, w_out, x, gate1, gain, scale, shift):
    seq = x.shape[0]

    def body(ya_ref, yd_ref, wa_ref, wd_ref, x_ref, g1_ref, g_ref, sc_ref, sh_ref, x1_ref, h_ref, y_ref):
        y = _nn(ya_ref[...].astype(BF16), wa_ref[...]) + _nn(yd_ref[...], wd_ref[...])
        x1 = x_ref[...] + g1_ref[...] * y
        _, _, norm = _rms_fwd(x1, g_ref[...])
        x1_ref[...] = x1
        h_ref[...] = (norm * (1.0 + sc_ref[...]) + sh_ref[...]).astype(BF16)
        y_ref[...] = y.astype(BF16)

    def rows(width):
        return pl.BlockSpec((ROW_TILE, width), lambda i: (i, 0))

    return pl.pallas_call(
        body,
        grid=(seq // ROW_TILE,),
        in_specs=[rows(HEAD_W), rows(HEAD_W),
                  pl.BlockSpec((HEAD_W, D_MODEL), lambda i: (0, 0)), pl.BlockSpec((HEAD_W, D_MODEL), lambda i: (1, 0)),
                  rows(D_MODEL), _row(D_MODEL), _row(D_MODEL), _row(D_MODEL), _row(D_MODEL)],
        out_specs=[rows(D_MODEL), rows(D_MODEL), rows(D_MODEL)],
        out_shape=[jax.ShapeDtypeStruct((seq, D_MODEL), F32), jax.ShapeDtypeStruct((seq, D_MODEL), BF16),
                   jax.ShapeDtypeStruct((seq, D_MODEL), BF16)],
        compiler_params=_params(("arbitrary",)),
        name="outproj_fwd",
    )(y_attn, y_delta, w_out, w_out, x, gate1, gain, scale, shift)


def _ffn_fwd(h2, w_gate, w_up, w_down, x1, gate2, final_gain, target):
    seq = h2.shape[0]
    n_rows, n_ff = seq // ROW_TILE, D_FF // FF_TILE

    def body(h_ref, wg_ref, wu_ref, wd_ref, x1_ref, g2_ref, gf_ref, t_ref, gate_ref, up_ref, dx2_ref, st_ref, acc):
        i, j = pl.program_id(0), pl.program_id(1)

        @pl.when((i == 0) & (j == 0))
        def _():
            st_ref[...] = jnp.zeros_like(st_ref)

        h = h_ref[...]
        gate = _nt(h, wg_ref[...])
        up = _nt(h, wu_ref[...])
        gate_ref[...] = gate.astype(BF16)
        up_ref[...] = up.astype(BF16)
        part = _nn((_silu(gate) * up).astype(BF16), wd_ref[...])

        @pl.when(j == 0)
        def _():
            acc[...] = part

        @pl.when(j > 0)
        def _():
            acc[...] += part

        @pl.when(j == n_ff - 1)
        def _():
            y2 = acc[...]
            x2 = x1_ref[...] + g2_ref[...] * y2
            xhat, rstd, out = _rms_fwd(x2, gf_ref[...])
            diff = out - t_ref[...]
            dx2, dgain = _rms_bwd(diff * (1.0 / D_MODEL), xhat, rstd, gf_ref[...])
            dx2_ref[...] = dx2
            st_ref[0:1, :] += dgain
            st_ref[1:2, :] += jnp.sum(dx2 * y2, axis=0, keepdims=True)
            st_ref[2:3, :] += jnp.sum(diff * diff, axis=0, keepdims=True) * (0.5 / D_MODEL)

        @pl.when((i == n_rows - 1) & (j == n_ff - 1))
        def _():
            st_ref[3:4, :] = jnp.broadcast_to(jnp.sum(st_ref[2:3, :], keepdims=True), (1, D_MODEL))

    def rows(width):
        return pl.BlockSpec((ROW_TILE, width), lambda i, j: (i, 0))

    ff = pl.BlockSpec((ROW_TILE, FF_TILE), lambda i, j: (i, j))
    return pl.pallas_call(
        body,
        grid=(n_rows, n_ff),
        in_specs=[rows(D_MODEL),
                  pl.BlockSpec((FF_TILE, D_MODEL), lambda i, j: (j, 0)), pl.BlockSpec((FF_TILE, D_MODEL), lambda i, j: (j, 0)),
                  pl.BlockSpec((FF_TILE, D_MODEL), lambda i, j: (j, 0)),
                  rows(D_MODEL), _row(D_MODEL), _row(D_MODEL), rows(D_MODEL)],
        out_specs=[ff, ff, rows(D_MODEL), pl.BlockSpec((8, D_MODEL), lambda i, j: (0, 0))],
        out_shape=[jax.ShapeDtypeStruct((seq, D_FF), BF16), jax.ShapeDtypeStruct((seq, D_FF), BF16),
                   jax.ShapeDtypeStruct((seq, D_MODEL), F32), jax.ShapeDtypeStruct((8, D_MODEL), F32)],
        scratch_shapes=[pltpu.VMEM((ROW_TILE, D_MODEL), F32)],
        compiler_params=_params(("arbitrary", "arbitrary")),
        name="ffn_fwd",
    )(h2, w_gate, w_up, w_down, x1, gate2, final_gain, target)


def _ffn_bwd(dx2, gate, up, w_gate, w_up, w_down, x1, y, gate2, gate1, gain, scale):
    seq = dx2.shape[0]

    def act_body(dx2_ref, g2_ref, gate_ref, up_ref, wd_ref, dgate_ref, dup_ref, dwd_ref):
        dy2 = (g2_ref[...] * dx2_ref[...]).astype(BF16)
        gate = gate_ref[...].astype(F32)
        up = up_ref[...].astype(F32)
        dact = _nt(dy2, wd_ref[...])
        silu, slope = _silu_and_slope(gate)
        dgate_ref[...] = (dact * up * slope).astype(BF16)
        dup_ref[...] = (dact * silu).astype(BF16)
        part = _tn((silu * up).astype(BF16), dy2)

        @pl.when(pl.program_id(1) == 0)
        def _():
            dwd_ref[...] = part

        @pl.when(pl.program_id(1) > 0)
        def _():
            dwd_ref[...] += part

    ff = pl.BlockSpec((ROW_TILE, FF_TILE), lambda j, i: (i, j))
    w_tile = pl.BlockSpec((FF_TILE, D_MODEL), lambda j, i: (j, 0))
    dgate, dup, dw_down = pl.pallas_call(
        act_body,
        grid=(D_FF // FF_TILE, seq // ROW_TILE),
        in_specs=[pl.BlockSpec((ROW_TILE, D_MODEL), lambda j, i: (i, 0)), _row(D_MODEL), ff, ff, w_tile],
        out_specs=[ff, ff, w_tile],
        out_shape=[jax.ShapeDtypeStruct((seq, D_FF), BF16)] * 2 + [jax.ShapeDtypeStruct((D_FF, D_MODEL), F32)],
        compiler_params=_params(("arbitrary", "arbitrary")),
        name="ffn_bwd_act",
    )(dx2, gate2, gate, up, w_down)

    def in_body(dgate_ref, dup_ref, wg_ref, wu_ref, dx2_ref, x1_ref, y_ref, g1_ref, g_ref, sc_ref,
                dx1_ref, dy_ref, st_ref):
        @pl.when(pl.program_id(0) == 0)
        def _():
            st_ref[...] = jnp.zeros_like(st_ref)

        dh = _nn(dgate_ref[...], wg_ref[...]) + _nn(dup_ref[...], wu_ref[...])
        xhat, rstd, norm = _rms_fwd(x1_ref[...], g_ref[...])
        dxn, dgain = _rms_bwd(dh * (1.0 + sc_ref[...]), xhat, rstd, g_ref[...])
        dx1 = dx2_ref[...] + dxn
        dx1_ref[...] = dx1
        dy_ref[...] = (g1_ref[...] * dx1).astype(BF16)
        st_ref[0:1, :] += jnp.sum(dh, axis=0, keepdims=True)
        st_ref[1:2, :] += jnp.sum(dh * norm, axis=0, keepdims=True)
        st_ref[2:3, :] += dgain
        st_ref[3:4, :] += jnp.sum(dx1 * y_ref[...].astype(F32), axis=0, keepdims=True)

    half_tile = ROW_TILE // 2

    def rows(width):
        return pl.BlockSpec((half_tile, width), lambda i: (i, 0))

    whole = pl.BlockSpec((D_FF, D_MODEL), lambda i: (0, 0))
    dx1, dy, stats = pl.pallas_call(
        in_body,
        grid=(seq // half_tile,),
        in_specs=[rows(D_FF), rows(D_FF), whole, whole, rows(D_MODEL), rows(D_MODEL), rows(D_MODEL),
                  _row(D_MODEL), _row(D_MODEL), _row(D_MODEL)],
        out_specs=[rows(D_MODEL), rows(D_MODEL), pl.BlockSpec((8, D_MODEL), lambda i: (0, 0))],
        out_shape=[jax.ShapeDtypeStruct((seq, D_MODEL), F32), jax.ShapeDtypeStruct((seq, D_MODEL), BF16),
                   jax.ShapeDtypeStruct((8, D_MODEL), F32)],
        compiler_params=_params(("arbitrary",)),
        name="ffn_bwd_in",
    )(dgate, dup, w_gate, w_up, dx2, x1, y, gate1, gain, scale)
    return dgate, dup, dw_down, dx1, dy, stats


def _outproj_bwd(dy, w_out, y_attn, y_delta):
    seq = dy.shape[0]

    def body(dy_ref, w_ref, ya_ref, yd_ref, out_ref, dw_ref):
        @pl.when(pl.program_id(0) == 0)
        def _():
            dw_ref[...] = jnp.zeros_like(dw_ref)

        dyv = dy_ref[...]
        out_ref[...] = _nt(dyv, w_ref[...])
        dw_ref[0:HEAD_W, :] += _tn(ya_ref[...].astype(BF16), dyv)
        dw_ref[HEAD_W:, :] += _tn(yd_ref[...], dyv)

    rows = pl.BlockSpec((ROW_TILE, D_MODEL), lambda i: (i, 0))
    half = pl.BlockSpec((ROW_TILE, HEAD_W), lambda i: (i, 0))
    whole = pl.BlockSpec((D_MODEL, D_MODEL), lambda i: (0, 0))
    return pl.pallas_call(
        body,
        grid=(seq // ROW_TILE,),
        in_specs=[rows, whole, half, half],
        out_specs=[rows, whole],
        out_shape=[jax.ShapeDtypeStruct((seq, D_MODEL), F32), jax.ShapeDtypeStruct((D_MODEL, D_MODEL), F32)],
        compiler_params=_params(("arbitrary",)),
        name="outproj_bwd",
    )(dy, w_out, y_attn, y_delta)


def _inproj_bwd(dq, dk, dv, dxd, dz, dba, w_rows, x, dx1, gain, scale, partials):
    seq = x.shape[0]
    n = len(partials)
    n_steps = seq // ROW_TILE

    def body(*refs):
        pieces, (w_ref, x_ref, dx1_ref, g_ref, sc_ref) = refs[:6], refs[6:11]
        gx_ref, st_ref = refs[11 + n:13 + n]
        riding = (refs[11:11 + n], refs[13 + n:13 + 2 * n], *refs[13 + 2 * n:])

        @pl.when(pl.program_id(0) == 0)
        def _():
            st_ref[...] = jnp.zeros_like(st_ref)
            for cp in (_scatter_copies(*riding) if n else []):
                cp.start()

        dh = _nn(jnp.concatenate([p[...].astype(BF16) for p in pieces], axis=1), w_ref[...])
        xhat, rstd, norm = _rms_fwd(x_ref[...], g_ref[...])
        dxn, dgain = _rms_bwd(dh * (1.0 + sc_ref[...]), xhat, rstd, g_ref[...])
        gx_ref[...] = dx1_ref[...] + dxn
        st_ref[0:1, :] += jnp.sum(dh, axis=0, keepdims=True)
        st_ref[1:2, :] += jnp.sum(dh * norm, axis=0, keepdims=True)
        st_ref[2:3, :] += dgain

        if n:
            @pl.when(pl.program_id(0) == n_steps - 1)
            def _():
                for cp in _scatter_copies(*riding):
                    cp.wait()

    def rows(width):
        return pl.BlockSpec((ROW_TILE, width), lambda i: (i, 0))

    sems = [pltpu.SemaphoreType.DMA((3 * n,)), pltpu.SemaphoreType.DMA((3 * n,))] if n else []
    return pl.pallas_call(
        body,
        grid=(n_steps,),
        in_specs=[rows(HEAD_W), rows(HEAD_W), rows(HEAD_W), rows(3 * HEAD_W), rows(HEAD_W), rows(LANES),
                  pl.BlockSpec(w_rows.shape, lambda i: (0, 0)), rows(D_MODEL), rows(D_MODEL), _row(D_MODEL),
                  _row(D_MODEL)]
        + [ANY] * n,
        out_specs=[rows(D_MODEL), pl.BlockSpec((8, D_MODEL), lambda i: (0, 0))] + [ANY] * n,
        out_shape=[jax.ShapeDtypeStruct((seq, D_MODEL), F32), jax.ShapeDtypeStruct((8, D_MODEL), F32)]
        + [jax.ShapeDtypeStruct(p.shape, p.dtype) for p in partials],
        scratch_shapes=sems,
        compiler_params=_params(("arbitrary",)),
        name="inproj_bwd",
    )(dq, dk, dv, dxd, dz, dba, w_rows, x, dx1, gain, scale, *partials)


def _weight_grad(a, b, name):
    seq, m = a.shape
    n = b.shape[1]
    tm = m if m <= 1536 else m // 2
    tn = n if n <= 1536 else n // 2
    rows = 2 * WGRAD_ROWS
    n_k = seq // rows

    def body(a_ref, b_ref, out_ref):
        part = _tn(a_ref[...].astype(BF16), b_ref[...].astype(BF16))

        @pl.when(pl.program_id(2) == 0)
        def _():
            out_ref[...] = part

        @pl.when(pl.program_id(2) > 0)
        def _():
            out_ref[...] += part

    return pl.pallas_call(
        body,
        grid=(m // tm, n // tn, n_k),
        in_specs=[pl.BlockSpec((rows, tm), lambda i, j, k: (k, i)),
                  pl.BlockSpec((rows, tn), lambda i, j, k: (k, j))],
        out_specs=pl.BlockSpec((tm, tn), lambda i, j, k: (i, j)),
        out_shape=jax.ShapeDtypeStruct((m, n), F32),
        compiler_params=_params(("arbitrary", "arbitrary", "arbitrary")),
        name=name,
    )(a, b)


def _weight_grad_stack(pieces, b, name):
    seq, n = b.shape
    widths = [a.shape[1] for a in pieces]
    starts = [sum(widths[:i]) for i in range(len(pieces))]

    def body(*refs):
        a_refs, b_ref, out_ref = refs[:len(pieces)], refs[len(pieces)], refs[len(pieces) + 1]

        @pl.when(pl.program_id(0) == 0)
        def _():
            out_ref[...] = jnp.zeros_like(out_ref)

        bb = b_ref[...].astype(BF16)
        for a_ref, start, width in zip(a_refs, starts, widths):
            out_ref[start:start + width, :] += _tn(a_ref[...].astype(BF16), bb)

    def rows(width):
        return pl.BlockSpec((WGRAD_ROWS, width), lambda k: (k, 0))

    return pl.pallas_call(
        body,
        grid=(seq // WGRAD_ROWS,),
        in_specs=[rows(w) for w in widths] + [rows(n)],
        out_specs=pl.BlockSpec((sum(widths), n), lambda k: (0, 0)),
        out_shape=jax.ShapeDtypeStruct((sum(widths), n), F32),
        compiler_params=_params(("arbitrary",)),
        name=name,
    )(*pieces, b)


def _adamw(w, g, m, v, name):
    n_rows, n_cols = w.shape
    if w.size <= 64 * 1024:
        block, grid, index = (n_rows, n_cols), (1,), lambda i: (0, 0)
    elif n_rows % 256 == 0:
        block, grid, index = (256, n_cols), (n_rows // 256,), lambda i: (i, 0)
    elif n_cols % 256 == 0:
        block, grid, index = (n_rows, 256), (n_cols // 256,), lambda i: (0, i)
    else:
        block, grid, index = (n_rows, n_cols), (1,), lambda i: (0, 0)

    def body(w_ref, g_ref, m_ref, v_ref, d_ref, nm_ref, nv_ref):
        gv = g_ref[...]
        nm = ADAM_B1 * m_ref[...] + (1.0 - ADAM_B1) * gv
        nv = ADAM_B2 * v_ref[...] + (1.0 - ADAM_B2) * (gv * gv)
        m_hat = nm / (1.0 - ADAM_B1 ** ADAM_STEP)
        v_hat = nv / (1.0 - ADAM_B2 ** ADAM_STEP)
        d_ref[...] = -ADAM_LR * (m_hat / (jnp.sqrt(v_hat) + ADAM_EPS) + ADAM_WD * w_ref[...])
        nm_ref[...] = nm
        nv_ref[...] = nv

    blk = pl.BlockSpec(block, index)
    shape = jax.ShapeDtypeStruct((n_rows, n_cols), F32)
    return pl.pallas_call(
        body,
        grid=grid,
        in_specs=[blk] * 4,
        out_specs=[blk] * 3,
        out_shape=[shape] * 3,
        compiler_params=_params(("arbitrary",)),
        name=name,
    )(w, g, m, v)


IN_WIDTH = 3600


def _local_step(x, target, mod, norm_attn_g, w_in, rel_bias, conv_w, a_log, dt_bias, delta_norm_g,
                norm_ffn_g, final_norm_g, shards, assemble, reduce_pairs):
    sh1, sc1, g1, sh2, sc2, g2 = [mod[:, i * D_MODEL:(i + 1) * D_MODEL] for i in range(6)]
    w_rows = jnp.pad(w_in, ((0, IN_SPLITS[-1] - IN_WIDTH), (0, 0)))
    tables = jnp.asarray(_attn_tables())
    alog_row = jnp.pad(a_log, ((0, 0), (N_HEADS, LANES - 2 * N_HEADS)))
    dt_row = jnp.pad(dt_bias, ((0, 0), (N_HEADS, LANES - 2 * N_HEADS)))
    gain_row = jnp.tile(delta_norm_g, (1, N_HEADS))

    h1, qkv_a, qkvz, ba = _inproj_fwd(x, norm_attn_g, sc1, sh1, w_rows)
    bias = _attention_bias(rel_bias, tables)
    y_attn, lse, *gathered = _attention_fwd(qkv_a, bias, shards)
    w_out, w_gate, w_up, w_down = assemble(gathered)
    xs = _delta_prep_fwd(qkvz, ba, conv_w, alog_row, dt_row)
    inv_h, qk_h, u_h, w_h = _delta_chunk_fwd(xs)
    o, st_h = _delta_scan_fwd(xs, qk_h, u_h, w_h)
    y_delta = _delta_post_fwd(o, qkvz, gain_row)
    x1, h2, y = _outproj_fwd(y_attn, y_delta, w_out, x, g1, norm_ffn_g, sc2, sh2)
    gate, up, dx2, st_f = _ffn_fwd(h2, w_gate, w_up, w_down, x1, g2, final_norm_g, target)

    dgate, dup, dw_down, dx1, dy, st_b = _ffn_bwd(dx2, gate, up, w_gate, w_up, w_down, x1, y, g2, g1, norm_ffn_g, sc2)
    dycat, dw_out = _outproj_bwd(dy, w_out, y_attn, y_delta)
    partials = reduce_pairs([dw_out, _weight_grad(dgate, h2, "wgrad_gate"), _weight_grad(dup, h2, "wgrad_up"),
                             dw_down], 1, "rest")
    grads = {}
    do, dz, dgain, dw_in_z = _delta_post_bwd(dycat, o, qkvz, gain_row, h1)
    dsn_h, dvn_h = _delta_scan_bwd(xs, qk_h, w_h, do)
    dxs = _delta_chunk_bwd(xs, inv_h, u_h, w_h, st_h, dsn_h, dvn_h, do)
    dconv, dba, dvec, dw_in_gates = _delta_prep_bwd(qkvz, ba, conv_w, alog_row, dt_row, dxs, h1)
    dxd, grads["conv_w"], dw_in_delta = _conv_bwd(dconv, qkvz, conv_w, h1)
    dq, dk, dv, dbias, *scattered = _attention_bwd(qkv_a, dycat, y_attn, lse, bias, partials)
    partials_in = reduce_pairs([jnp.concatenate(
        [_weight_grad_stack([dq, dk, dv], h1, "wgrad_in_attn"), dw_in_delta, dw_in_z,
         dw_in_gates[:2 * N_HEADS]], axis=0)], 0, "in")
    grad_x, st_i, *scattered_in = _inproj_bwd(dq, dk, dv, dxd, dz, dba, w_rows, x, dx1, norm_attn_g, sc1,
                                              partials_in)
    grads["rel_bias"] = _rel_bias_grad(dbias, tables)[:, :N_BUCKETS].T
    grads["a_log"] = dvec[0:1, N_HEADS:2 * N_HEADS]
    grads["dt_bias"] = dvec[1:2, N_HEADS:2 * N_HEADS]
    grads["delta_norm_g"] = dgain[1:2, :HEAD_DIM]
    grads["norm_attn_g"] = st_i[2:3]
    grads["norm_ffn_g"] = st_b[2:3]
    grads["final_norm_g"] = st_f[0:1]
    dmod = jnp.concatenate([st_i[0:1], st_i[1:2], st_b[3:4], st_b[0:1], st_b[1:2], st_f[1:2]], axis=1)
    return st_f[3, 0], grad_x, grads, dmod, (partials_in + partials, scattered_in + scattered)


MESH = pl.DeviceIdType.MESH
OTHER_CHIPS = ((1, 0), (0, 1), (1, 1))
ALL_PEERS = tuple((m >> 2 & 1, m >> 1 & 1, m & 1) for m in range(1, 8))
ANY = pl.BlockSpec(memory_space=pl.ANY)
VMEM_SPEC = pl.BlockSpec(memory_space=pltpu.VMEM)


def _me():
    return lax.axis_index("x"), lax.axis_index("y"), lax.axis_index("c")


def _flip(pos, mask):
    return tuple(1 - p if m else p for p, m in zip(pos, mask))


def _remote(src, dst, send_sems, recv_sems, k, to):
    return pltpu.make_async_remote_copy(src_ref=src, dst_ref=dst, send_sem=send_sems.at[k], recv_sem=recv_sems.at[k],
                                        device_id=to, device_id_type=MESH)


def _ada_exchange(c8, w_ada, b_ada, conv8, shard):
    def body(c_ref, w_ref, b_ref, cv_ref, shard_ref, mod_ref, cact_ref, conv_ref, whole_ref,
             c_all, part_all, send_sems, recv_sems, ride_send, ride_recv):
        x, y, c = me = _me()
        dev = 4 * x + 2 * y + c
        chip = 2 * x + y
        riding = ([shard_ref], [whole_ref], ride_send, ride_recv)
        for cp in _gather_copies(*riding, hand_over=False)[0]:
            cp.start()
        c_all[dev] = c_ref[...]
        conv_ref[chip] = cv_ref[...]
        first = [_remote(c_ref, c_all.at[dev], send_sems, recv_sems, k, _flip(me, mask))
                 for k, mask in enumerate(ALL_PEERS)]
        first += [_remote(cv_ref, conv_ref.at[chip], send_sems, recv_sems, 7 + j, _flip(me, (*mask, 0)))
                  for j, mask in enumerate(OTHER_CHIPS)]
        for cp in first:
            cp.start()
        for cp in first:
            cp.wait()
        row = lax.broadcasted_iota(jnp.int32, (8, D_MODEL), 0)
        c_rows = jnp.zeros((8, D_MODEL), F32)
        for d in range(8):
            c_rows = jnp.where(row == d, c_all[d], c_rows)
        c_act = _silu(c_rows)
        cact_ref[...] = c_act
        part_all[chip] = _nn(c_act, w_ref[...], HIGHEST)
        second = [_remote(part_all.at[chip], part_all.at[chip], send_sems, recv_sems, 10 + j, _flip(me, (*mask, 0)))
                  for j, mask in enumerate(OTHER_CHIPS)]
        for cp in second:
            cp.start()
        for cp in second:
            cp.wait()
        cols = w_ref.shape[1]
        for k in range(4):
            mod_ref[:, k * cols:(k + 1) * cols] = part_all[k] + b_ref[:, k * cols:(k + 1) * cols]
        first, passed = _gather_copies(*riding)
        for cp, fwd in zip(first, passed):
            cp.wait_recv()
            fwd.start()
        for cp in first:
            cp.wait_send()
        for fwd in passed:
            fwd.wait()

    cols = w_ada.shape[1]
    return pl.pallas_call(
        body,
        in_specs=[VMEM_SPEC] * 4 + [ANY],
        out_specs=[VMEM_SPEC] * 3 + [ANY],
        out_shape=[jax.ShapeDtypeStruct((8, 4 * cols), F32), jax.ShapeDtypeStruct((8, D_MODEL), F32),
                   jax.ShapeDtypeStruct((4, 8, conv8.shape[1]), F32)] + _gathered_shapes([shard]),
        scratch_shapes=[pltpu.VMEM((8, 8, D_MODEL), F32), pltpu.VMEM((4, 8, cols), F32),
                        pltpu.SemaphoreType.DMA((13,)), pltpu.SemaphoreType.DMA((13,)),
                        pltpu.SemaphoreType.DMA((6,)), pltpu.SemaphoreType.DMA((6,))],
        compiler_params=pltpu.CompilerParams(vmem_limit_bytes=VMEM_LIMIT),
        name="ada_exchange",
    )(c8, w_ada, b_ada, conv8, shard)


def _gathered_shapes(shards):
    return [jax.ShapeDtypeStruct((4, *s.shape), s.dtype) for s in shards]


def _gather_copies(srcs, dsts, send_sems, recv_sems, hand_over=True):
    x, y, c = me = _me()
    chip = 2 * x + y
    sibling = _flip(me, (0, 0, 1))
    first, passed = [], []
    for a, (src, dst) in enumerate(zip(srcs, dsts)):
        for j, mask in enumerate(OTHER_CHIPS):
            to = _flip(me, (*mask, 0))
            first.append(_remote(src.at[c], dst.at[chip, c], send_sems, recv_sems, 6 * a + j, to))
            if hand_over:
                landed = dst.at[2 * to[0] + to[1], c]
                passed.append(_remote(landed, landed, send_sems, recv_sems, 6 * a + 3 + j, sibling))
    return first, passed


def _scatter_copies(srcs, dsts, send_sems, recv_sems):
    x, y, c = me = _me()
    chip = 2 * x + y
    copies = []
    for a, (src, dst) in enumerate(zip(srcs, dsts)):
        for j, mask in enumerate(OTHER_CHIPS):
            to = _flip(me, (*mask, 0))
            copies.append(_remote(src.at[2 * to[0] + to[1]], dst.at[chip], send_sems, recv_sems, 3 * a + j, to))
    return copies


def _start_and_wait(copies):
    for cp in copies:
        cp.start()
    for cp in copies:
        cp.wait()


def _swap_halves(grads):
    n = len(grads)

    def body(*refs):
        srcs, got = refs[:n], refs[n:2 * n]
        send_sems, recv_sems = refs[2 * n:]
        x, y, c = me = _me()
        _start_and_wait([_remote(srcs[a].at[:, 1 - c], got[a], send_sems, recv_sems, a, _flip(me, (0, 0, 1)))
                         for a in range(n)])

    return pl.pallas_call(
        body,
        in_specs=[ANY] * n,
        out_specs=[ANY] * n,
        out_shape=[jax.ShapeDtypeStruct((4, g.shape[2], g.shape[3]), g.dtype) for g in grads],
        scratch_shapes=[pltpu.SemaphoreType.DMA((n,)), pltpu.SemaphoreType.DMA((n,))],
        name=f"swap_halves_{n}",
    )(*grads)


def _join_halves(halves):
    n = len(halves)

    def body(*refs):
        srcs, dsts = refs[:n], refs[n:2 * n]
        send_sems, recv_sems = refs[2 * n:]
        x, y, c = me = _me()
        _start_and_wait([_remote(srcs[a], dsts[a].at[c], send_sems, recv_sems, a, _flip(me, (0, 0, 1)))
                         for a in range(n)])

    return pl.pallas_call(
        body,
        in_specs=[ANY] * n,
        out_specs=[ANY] * n,
        out_shape=[jax.ShapeDtypeStruct((2, *h.shape), h.dtype) for h in halves],
        scratch_shapes=[pltpu.SemaphoreType.DMA((n,)), pltpu.SemaphoreType.DMA((n,))],
        name=f"join_halves_{n}",
    )(*halves)


def _gather_small(packed):
    n_rows = packed.shape[0]

    def body(p_ref, all_ref, sum_ref, send_sems, recv_sems):
        x, y, c = me = _me()
        dev = 4 * x + 2 * y + c
        all_ref[dev] = p_ref[...]
        copies = [_remote(p_ref, all_ref.at[dev], send_sems, recv_sems, k, _flip(me, mask))
                  for k, mask in enumerate(ALL_PEERS)]
        for cp in copies:
            cp.start()
        for cp in copies:
            cp.wait()
        total = all_ref[0]
        for d in range(1, 8):
            total = total + all_ref[d]
        sum_ref[...] = total

    return pl.pallas_call(
        body,
        in_specs=[VMEM_SPEC],
        out_specs=[VMEM_SPEC, VMEM_SPEC],
        out_shape=[jax.ShapeDtypeStruct((8, n_rows, LANES), F32), jax.ShapeDtypeStruct((n_rows, LANES), F32)],
        scratch_shapes=[pltpu.SemaphoreType.DMA((7,)), pltpu.SemaphoreType.DMA((7,))],
        name="gather_small",
    )(packed)


def _add_pair(a, b, out_dtype, name):
    def body(a_ref, b_ref, o_ref):
        o_ref[...] = (a_ref[...] + b_ref[...]).astype(o_ref.dtype)

    blk = pl.BlockSpec((1, *a.shape[1:]), lambda i: (i, 0, 0))
    return pl.pallas_call(
        body, grid=(a.shape[0],), in_specs=[blk, blk], out_specs=blk,
        out_shape=jax.ShapeDtypeStruct(a.shape, out_dtype),
        compiler_params=_params(("arbitrary",)), name=name,
    )(a, b)


def _add_slots(a, name):
    def body(a_ref, o_ref):
        total = a_ref[0].astype(F32)
        for k in range(1, 4):
            total = total + a_ref[k].astype(F32)
        o_ref[...] = total

    return pl.pallas_call(
        body, in_specs=[VMEM_SPEC], out_specs=VMEM_SPEC,
        out_shape=jax.ShapeDtypeStruct(a.shape[1:], F32),
        compiler_params=pltpu.CompilerParams(vmem_limit_bytes=VMEM_LIMIT), name=name,
    )(a)


def _ada_weight_grad(c_act, dmod_cols):
    def body(c_ref, d_ref, o_ref):
        o_ref[...] = _tn(c_ref[...], d_ref[...], HIGHEST)

    return pl.pallas_call(
        body, in_specs=[VMEM_SPEC, VMEM_SPEC], out_specs=VMEM_SPEC,
        out_shape=jax.ShapeDtypeStruct((c_act.shape[1], dmod_cols.shape[1]), F32),
        compiler_params=pltpu.CompilerParams(vmem_limit_bytes=VMEM_LIMIT), name="ada_weight_grad",
    )(c_act, dmod_cols)


def kernel(x, c, w_ada, b_ada, norm_attn_g, w_in, rel_bias, conv_w, a_log, dt_bias, delta_norm_g, w_out, norm_ffn_g, w_gate, w_up, w_down, final_norm_g, loss_target, m_w_ada, m_b_ada, m_norm_attn_g, m_w_in, m_rel_bias, m_conv_w, m_a_log, m_dt_bias, m_delta_norm_g, m_w_out, m_norm_ffn_g, m_w_gate, m_w_up, m_w_down, m_final_norm_g, v_w_ada, v_b_ada, v_norm_attn_g, v_w_in, v_rel_bias, v_conv_w, v_a_log, v_dt_bias, v_delta_norm_g, v_w_out, v_norm_ffn_g, v_w_gate, v_w_up, v_w_down, v_final_norm_g):
    xi, yi, ci = _me()
    dev = 4 * xi + 2 * yi + ci
    chip = 2 * xi + yi

    big_names = ("w_in", "w_out", "w_gate", "w_up", "w_down")
    by_cols = (True, False, True, True, False)

    def rows_form(a, cols):
        return jnp.swapaxes(a[0], 0, 1) if cols else a[0]

    def halves_form(w):
        rows, lanes = w.shape
        if (rows // 2) % 16:
            rows, lanes = w.size // LANES, LANES
        return (2, rows // 2, lanes)

    big = [rows_form(w, cols) for w, cols in zip((w_in, w_out, w_gate, w_up, w_down), by_cols)]
    shards = [w.astype(BF16).reshape(halves_form(w)) for w in big]

    def assemble(gathered, first):
        return [lax.dynamic_update_index_in_dim(g, s, chip, 0).reshape(4 * w.shape[0], w.shape[1])
                for g, s, w in zip(gathered, shards[first:], big[first:])]

    def reduce_pairs(grads, first, tag):
        slots = [g.reshape(4, *halves_form(w)) for g, w in zip(grads, big[first:])]
        return [_add_pair(lax.dynamic_index_in_dim(s, ci, 1, keepdims=False), got, BF16, f"add_pair_{tag}{a}")
                for a, (s, got) in enumerate(zip(slots, _swap_halves(slots)))]

    def finish(partials, scattered, first, tag):
        by_source = [lax.dynamic_update_index_in_dim(b, lax.dynamic_index_in_dim(p, chip, 0, keepdims=False), chip, 0)
                     for b, p in zip(scattered, partials)]
        halves = [_add_slots(p, f"add_slots_{tag}{a}") for a, p in enumerate(by_source)]
        joined = [lax.dynamic_update_index_in_dim(j, h, ci, 0) for j, h in zip(_join_halves(halves), halves)]
        return [j.reshape(w.shape) for j, w in zip(joined, big[first:])]

    conv_cols = conv_w.shape[2]
    mod_all, c_act, conv_all, gathered_in = _ada_exchange(
        jnp.broadcast_to(c, (8, D_MODEL)), w_ada[0], b_ada, jnp.pad(conv_w[0], ((0, 4), (0, 0))), shards[0])
    mod = lax.dynamic_slice_in_dim(mod_all, dev, 1, axis=0)
    conv_full = jnp.swapaxes(conv_all[:, :4, :], 0, 1).reshape(4, 4 * conv_cols)
    whole_in, = assemble([gathered_in], 0)
    loss, grad_x, grads, dmod, (partials, scattered) = _local_step(
        x[0], loss_target[0], mod, norm_attn_g, whole_in, rel_bias, conv_full, a_log, dt_bias, delta_norm_g,
        norm_ffn_g, final_norm_g[None], shards[1:], functools.partial(assemble, first=1), reduce_pairs)

    big_grads = finish(partials, scattered, 0, "all")

    pieces = [dmod, grads["conv_w"], grads["norm_attn_g"], grads["norm_ffn_g"], grads["final_norm_g"],
              grads["rel_bias"], grads["a_log"], grads["dt_bias"], grads["delta_norm_g"]]
    flat = [jnp.pad(p.reshape(-1), (0, -p.size % LANES)) for p in pieces]
    n_rows = [f.size // LANES for f in flat]
    packed = jnp.concatenate(flat).reshape(-1, LANES)
    packed = jnp.pad(packed, ((0, -packed.shape[0] % 8), (0, 0)))
    all_small, total = _gather_small(packed)
    sums, start = [], 0
    for p, n in zip(pieces, n_rows):
        sums.append(total[start:start + n].reshape(-1)[:p.size].reshape(p.shape))
        start += n
    g_b_ada, g_conv, g_norm_attn, g_norm_ffn, g_final, g_rel, g_alog, g_dt, g_dnorm = sums
    dmod_all = all_small[:, :n_rows[0], :].reshape(8, -1)
    ada_cols = w_ada.shape[2]
    g_w_ada = _ada_weight_grad(c_act, lax.dynamic_slice_in_dim(dmod_all, chip * ada_cols, ada_cols, axis=1))
    g_conv = lax.dynamic_slice_in_dim(g_conv, chip * conv_cols, conv_cols, axis=1)

    grad = {"w_ada": g_w_ada[None], "b_ada": g_b_ada, "norm_attn_g": g_norm_attn,
            "rel_bias": g_rel, "conv_w": g_conv[None], "a_log": g_alog, "dt_bias": g_dt, "delta_norm_g": g_dnorm,
            "norm_ffn_g": g_norm_ffn, "final_norm_g": g_final.reshape(-1)}
    weight = {"w_ada": w_ada, "b_ada": b_ada, "norm_attn_g": norm_attn_g, "w_in": w_in, "rel_bias": rel_bias,
              "conv_w": conv_w, "a_log": a_log, "dt_bias": dt_bias, "delta_norm_g": delta_norm_g, "w_out": w_out,
              "norm_ffn_g": norm_ffn_g, "w_gate": w_gate, "w_up": w_up, "w_down": w_down, "final_norm_g": final_norm_g}
    first = {"w_ada": m_w_ada, "b_ada": m_b_ada, "norm_attn_g": m_norm_attn_g, "w_in": m_w_in, "rel_bias": m_rel_bias,
             "conv_w": m_conv_w, "a_log": m_a_log, "dt_bias": m_dt_bias, "delta_norm_g": m_delta_norm_g,
             "w_out": m_w_out, "norm_ffn_g": m_norm_ffn_g, "w_gate": m_w_gate, "w_up": m_w_up, "w_down": m_w_down,
             "final_norm_g": m_final_norm_g}
    second = {"w_ada": v_w_ada, "b_ada": v_b_ada, "norm_attn_g": v_norm_attn_g, "w_in": v_w_in, "rel_bias": v_rel_bias,
              "conv_w": v_conv_w, "a_log": v_a_log, "dt_bias": v_dt_bias, "delta_norm_g": v_delta_norm_g,
              "w_out": v_w_out, "norm_ffn_g": v_norm_ffn_g, "w_gate": v_w_gate, "w_up": v_w_up, "w_down": v_w_down,
              "final_norm_g": v_final_norm_g}
    delta, new_m, new_v = {}, {}, {}
    for name, w in weight.items():
        if name in big_names:
            continue
        two_d = (-1, w.shape[-1])
        d, nm, nv = _adamw(w.reshape(two_d), grad[name].reshape(two_d), first[name].reshape(two_d),
                           second[name].reshape(two_d), f"adamw_{name}")
        delta[name], new_m[name], new_v[name] = d.reshape(w.shape), nm.reshape(w.shape), nv.reshape(w.shape)
    for name, w, g, cols in zip(big_names, big, big_grads, by_cols):
        outs = _adamw(w, g, rows_form(first[name], cols), rows_form(second[name], cols), f"adamw_{name}")
        grad[name], delta[name], new_m[name], new_v[name] = [
            (jnp.swapaxes(o, 0, 1) if cols else o)[None] for o in (g, *outs)]

    names = list(weight)
    return (lax.psum(loss, ("x", "y", "c")), grad_x[None], *[grad[n] for n in names], *[delta[n] for n in names],
            *[new_m[n] for n in names], *[new_v[n] for n in names])
```
